```python
import jax, jax.numpy as jnp
from jax import lax
import numpy as np

D_MODEL = 1024
BATCH = 16
SEQ = 2048
DEPTH = 1

D_FF = 2816
N_HEADS = 8
QK_NOPE_DIM = 64
QK_ROPE_DIM = 32
QK_HEAD_DIM = QK_NOPE_DIM + QK_ROPE_DIM
V_HEAD_DIM = 64
Q_LORA_RANK = 384
KV_LORA_RANK = 256
CONV_DIM = 1024
CONV_WIDTH = 3
ROPE_THETA = 10000.0
Q_BLOCK = 128
NORM_EPS = 1e-6
ATTN_OUT_DIM = N_HEADS * V_HEAD_DIM
N_BRANCHES = 2
IN_SIZES = (Q_LORA_RANK, KV_LORA_RANK, QK_ROPE_DIM, CONV_DIM, CONV_DIM, CONV_DIM, D_MODEL, D_MODEL)
IN_DIM = sum(IN_SIZES)
IN_SPLITS = tuple(int(s) for s in np.cumsum(IN_SIZES)[:-1])

kernel_name = "hybrid_mla_shortconv_macaron_block"


def rmsnorm(x, gain):
    x32 = x.astype(jnp.float32)
    y = x32 * lax.rsqrt(jnp.mean(x32 * x32, axis=-1, keepdims=True) + NORM_EPS)
    return (y * gain.astype(jnp.float32)).astype(x.dtype)


def swiglu(h, w_gate, w_up, w_down):
    return (jax.nn.silu(h @ w_gate) * (h @ w_up)) @ w_down


def rope(t, positions):
    half = QK_ROPE_DIM // 2
    inv_freq = 1.0 / (ROPE_THETA ** (jnp.arange(half, dtype=jnp.float32) / half))
    ang = positions.astype(jnp.float32)[..., None] * inv_freq
    ang = ang.reshape(ang.shape[:2] + (1,) * (t.ndim - 3) + (half,))
    cos, sin = jnp.cos(ang).astype(t.dtype), jnp.sin(ang).astype(t.dtype)
    t1, t2 = t[..., :half], t[..., half:]
    return jnp.concatenate([t1 * cos - t2 * sin, t1 * sin + t2 * cos], axis=-1)


def causal_block_attention(q, k, v):
    b, s, h, dq = q.shape
    nb = s // Q_BLOCK
    scale = QK_HEAD_DIM ** -0.5
    q_blocks = q.reshape(b, nb, Q_BLOCK, h, dq).transpose(1, 0, 2, 3, 4)
    key_pos = jnp.arange(s)

    def one_block(args):
        qb, blk = args
        scores = jnp.einsum('bqhd,bkhd->bhqk', qb, k).astype(jnp.float32) * scale
        q_pos = blk * Q_BLOCK + jnp.arange(Q_BLOCK)
        mask = key_pos[None, :] <= q_pos[:, None]
        scores = jnp.where(mask[None, None], scores, -1e30)
        p = jax.nn.softmax(scores, axis=-1).astype(v.dtype)
        return jnp.einsum('bhqk,bkhd->bqhd', p, v)

    out = lax.map(one_block, (q_blocks, jnp.arange(nb)))
    return out.transpose(1, 0, 2, 3, 4).reshape(b, s, h * V_HEAD_DIM)


def _fwd_setup_inputs(seed: int = 0) -> dict:
    key = jax.random.key(seed)
    ks = jax.random.split(key, 32)

    def w(k, shape, fan_in):
        return jax.random.normal(k, shape, jnp.float32) * fan_in ** -0.5

    def gain(k, n):
        return 1.0 + 0.02 * jax.random.normal(k, (n,), jnp.float32)

    positions = jnp.broadcast_to(jnp.arange(SEQ, dtype=jnp.int32)[None, :], (BATCH, SEQ))
    return {
        "x": jax.random.normal(ks[0], (BATCH, SEQ, D_MODEL), jnp.float32),
        "positions": positions,
        "ffn1_norm": gain(ks[1], D_MODEL),
        "ffn1_w_gate": w(ks[2], (D_MODEL, D_FF), D_MODEL),
        "ffn1_w_up": w(ks[3], (D_MODEL, D_FF), D_MODEL),
        "ffn1_w_down": w(ks[4], (D_FF, D_MODEL), D_FF),
        "mix_norm": gain(ks[5], D_MODEL),
        "w_in": w(ks[6], (D_MODEL, IN_DIM), D_MODEL),
        "gate_bias": 0.01 * jax.random.normal(ks[7], (N_BRANCHES * D_MODEL,), jnp.float32),
        "q_a_norm": gain(ks[8], Q_LORA_RANK),
        "w_uq": w(ks[9], (Q_LORA_RANK, N_HEADS * QK_HEAD_DIM), Q_LORA_RANK),
        "kv_a_norm": gain(ks[10], KV_LORA_RANK),
        "w_uk": w(ks[11], (KV_LORA_RANK, N_HEADS * QK_NOPE_DIM), KV_LORA_RANK),
        "w_uv": w(ks[12], (KV_LORA_RANK, N_HEADS * V_HEAD_DIM), KV_LORA_RANK),
        "q_head_norm": gain(ks[13], QK_HEAD_DIM),
        "k_head_norm": gain(ks[14], QK_HEAD_DIM),
        "w_proj_attn": w(ks[15], (ATTN_OUT_DIM, D_MODEL), ATTN_OUT_DIM),
        "conv_w": w(ks[16], (CONV_WIDTH, CONV_DIM), CONV_WIDTH),
        "w_proj_conv": w(ks[17], (CONV_DIM, D_MODEL), CONV_DIM),
        "w_out": w(ks[18], (D_MODEL, D_MODEL), D_MODEL),
        "ffn2_norm": gain(ks[19], D_MODEL),
        "ffn2_w_gate": w(ks[20], (D_MODEL, D_FF), D_MODEL),
        "ffn2_w_up": w(ks[21], (D_MODEL, D_FF), D_MODEL),
        "ffn2_w_down": w(ks[22], (D_FF, D_MODEL), D_FF),
    }


def _fwd_reference(x, positions, ffn1_norm, ffn1_w_gate, ffn1_w_up, ffn1_w_down,
              mix_norm, w_in, gate_bias, q_a_norm, w_uq, kv_a_norm, w_uk, w_uv,
              q_head_norm, k_head_norm, w_proj_attn, conv_w, w_proj_conv, w_out,
              ffn2_norm, ffn2_w_gate, ffn2_w_up, ffn2_w_down):
    b, s, _ = x.shape
    for _layer in range(DEPTH):
        x = x + 0.5 * swiglu(rmsnorm(x, ffn1_norm), ffn1_w_gate, ffn1_w_up, ffn1_w_down)

        h = rmsnorm(x, mix_norm)
        proj = h @ w_in
        q_lat, kv_lat, k_rope_raw, xc, gB, gC, gate_logits = jnp.split(proj, IN_SPLITS, axis=-1)[:7] + []  if False else jnp.split(proj, IN_SPLITS, axis=-1)[:7]
        gate_logits = proj[..., IN_SPLITS[-1]:] if False else jnp.concatenate([gate_logits, proj[..., IN_SPLITS[-1]:]], axis=-1)

        q = (rmsnorm(q_lat, q_a_norm) @ w_uq).reshape(b, s, N_HEADS, QK_HEAD_DIM)
        c_kv = rmsnorm(kv_lat, kv_a_norm)
        k_nope = (c_kv @ w_uk).reshape(b, s, N_HEADS, QK_NOPE_DIM)
        v = (c_kv @ w_uv).reshape(b, s, N_HEADS, V_HEAD_DIM)
        k_rope = jnp.broadcast_to(k_rope_raw[:, :, None, :], (b, s, N_HEADS, QK_ROPE_DIM))
        k = jnp.concatenate([k_nope, k_rope], axis=-1)
        q = rmsnorm(q, q_head_norm)
        k = rmsnorm(k, k_head_norm)
        q = jnp.concatenate([q[..., :QK_NOPE_DIM], rope(q[..., QK_NOPE_DIM:], positions)], axis=-1)
        k = jnp.concatenate([k[..., :QK_NOPE_DIM], rope(k[..., QK_NOPE_DIM:], positions)], axis=-1)
        y_a = causal_block_attention(q, k, v) @ w_proj_attn

        u = gC * xc
        up = jnp.pad(u, ((0, 0), (CONV_WIDTH - 1, 0), (0, 0)))
        z = conv_w[0] * up[:, :s] + conv_w[1] * up[:, 1:s + 1] + conv_w[2] * up[:, 2:s + 2]
        y_b = (gB * z) @ w_proj_conv

        gates = jax.nn.sigmoid(gate_logits + gate_bias)
        merged = gates[..., :D_MODEL] * y_a + gates[..., D_MODEL:] * y_b
        x = x + merged @ w_out

        x = x + 0.5 * swiglu(rmsnorm(x, ffn2_norm), ffn2_w_gate, ffn2_w_up, ffn2_w_down)
    return x


import jax as _jax
import jax.numpy as _jnp

TWIN_FORMAT = 'train_step'
FWD_PARAMS = ['x', 'positions', 'ffn1_norm', 'ffn1_w_gate', 'ffn1_w_up', 'ffn1_w_down', 'mix_norm', 'w_in', 'gate_bias', 'q_a_norm', 'w_uq', 'kv_a_norm', 'w_uk', 'w_uv', 'q_head_norm', 'k_head_norm', 'w_proj_attn', 'conv_w', 'w_proj_conv', 'w_out', 'ffn2_norm', 'ffn2_w_gate', 'ffn2_w_up', 'ffn2_w_down']
TWIN_WEIGHTS = ['ffn1_norm', 'ffn1_w_gate', 'ffn1_w_up', 'ffn1_w_down', 'mix_norm', 'w_in', 'gate_bias', 'q_a_norm', 'w_uq', 'kv_a_norm', 'w_uk', 'w_uv', 'q_head_norm', 'k_head_norm', 'w_proj_attn', 'conv_w', 'w_proj_conv', 'w_out', 'ffn2_norm', 'ffn2_w_gate', 'ffn2_w_up', 'ffn2_w_down']
TWIN_DIFF_INPUT = 'x'
TWIN_INPUTS = ['x', 'positions', 'ffn1_norm', 'ffn1_w_gate', 'ffn1_w_up', 'ffn1_w_down', 'mix_norm', 'w_in', 'gate_bias', 'q_a_norm', 'w_uq', 'kv_a_norm', 'w_uk', 'w_uv', 'q_head_norm', 'k_head_norm', 'w_proj_attn', 'conv_w', 'w_proj_conv', 'w_out', 'ffn2_norm', 'ffn2_w_gate', 'ffn2_w_up', 'ffn2_w_down', 'loss_target', 'm_ffn1_norm', 'm_ffn1_w_gate', 'm_ffn1_w_up', 'm_ffn1_w_down', 'm_mix_norm', 'm_w_in', 'm_gate_bias', 'm_q_a_norm', 'm_w_uq', 'm_kv_a_norm', 'm_w_uk', 'm_w_uv', 'm_q_head_norm', 'm_k_head_norm', 'm_w_proj_attn', 'm_conv_w', 'm_w_proj_conv', 'm_w_out', 'm_ffn2_norm', 'm_ffn2_w_gate', 'm_ffn2_w_up', 'm_ffn2_w_down', 'v_ffn1_norm', 'v_ffn1_w_gate', 'v_ffn1_w_up', 'v_ffn1_w_down', 'v_mix_norm', 'v_w_in', 'v_gate_bias', 'v_q_a_norm', 'v_w_uq', 'v_kv_a_norm', 'v_w_uk', 'v_w_uv', 'v_q_head_norm', 'v_k_head_norm', 'v_w_proj_attn', 'v_conv_w', 'v_w_proj_conv', 'v_w_out', 'v_ffn2_norm', 'v_ffn2_w_gate', 'v_ffn2_w_up', 'v_ffn2_w_down']
TWIN_OUTPUTS = ['loss', 'grad_x', 'grad_ffn1_norm', 'grad_ffn1_w_gate', 'grad_ffn1_w_up', 'grad_ffn1_w_down', 'grad_mix_norm', 'grad_w_in', 'grad_gate_bias', 'grad_q_a_norm', 'grad_w_uq', 'grad_kv_a_norm', 'grad_w_uk', 'grad_w_uv', 'grad_q_head_norm', 'grad_k_head_norm', 'grad_w_proj_attn', 'grad_conv_w', 'grad_w_proj_conv', 'grad_w_out', 'grad_ffn2_norm', 'grad_ffn2_w_gate', 'grad_ffn2_w_up', 'grad_ffn2_w_down', 'delta_ffn1_norm', 'delta_ffn1_w_gate', 'delta_ffn1_w_up', 'delta_ffn1_w_down', 'delta_mix_norm', 'delta_w_in', 'delta_gate_bias', 'delta_q_a_norm', 'delta_w_uq', 'delta_kv_a_norm', 'delta_w_uk', 'delta_w_uv', 'delta_q_head_norm', 'delta_k_head_norm', 'delta_w_proj_attn', 'delta_conv_w', 'delta_w_proj_conv', 'delta_w_out', 'delta_ffn2_norm', 'delta_ffn2_w_gate', 'delta_ffn2_w_up', 'delta_ffn2_w_down', 'new_m_ffn1_norm', 'new_m_ffn1_w_gate', 'new_m_ffn1_w_up', 'new_m_ffn1_w_down', 'new_m_mix_norm', 'new_m_w_in', 'new_m_gate_bias', 'new_m_q_a_norm', 'new_m_w_uq', 'new_m_kv_a_norm', 'new_m_w_uk', 'new_m_w_uv', 'new_m_q_head_norm', 'new_m_k_head_norm', 'new_m_w_proj_attn', 'new_m_conv_w', 'new_m_w_proj_conv', 'new_m_w_out', 'new_m_ffn2_norm', 'new_m_ffn2_w_gate', 'new_m_ffn2_w_up', 'new_m_ffn2_w_down', 'new_v_ffn1_norm', 'new_v_ffn1_w_gate', 'new_v_ffn1_w_up', 'new_v_ffn1_w_down', 'new_v_mix_norm', 'new_v_w_in', 'new_v_gate_bias', 'new_v_q_a_norm', 'new_v_w_uq', 'new_v_kv_a_norm', 'new_v_w_uk', 'new_v_w_uv', 'new_v_q_head_norm', 'new_v_k_head_norm', 'new_v_w_proj_attn', 'new_v_conv_w', 'new_v_w_proj_conv', 'new_v_w_out', 'new_v_ffn2_norm', 'new_v_ffn2_w_gate', 'new_v_ffn2_w_up', 'new_v_ffn2_w_down']
TWIN_LEAF_KINDS = {'loss': 'loss', 'grad_x': 'grad_x', 'grad_ffn1_norm': 'grad_w', 'grad_ffn1_w_gate': 'grad_w', 'grad_ffn1_w_up': 'grad_w', 'grad_ffn1_w_down': 'grad_w', 'grad_mix_norm': 'grad_w', 'grad_w_in': 'grad_w', 'grad_gate_bias': 'grad_w', 'grad_q_a_norm': 'grad_w', 'grad_w_uq': 'grad_w', 'grad_kv_a_norm': 'grad_w', 'grad_w_uk': 'grad_w', 'grad_w_uv': 'grad_w', 'grad_q_head_norm': 'grad_w', 'grad_k_head_norm': 'grad_w', 'grad_w_proj_attn': 'grad_w', 'grad_conv_w': 'grad_w', 'grad_w_proj_conv': 'grad_w', 'grad_w_out': 'grad_w', 'grad_ffn2_norm': 'grad_w', 'grad_ffn2_w_gate': 'grad_w', 'grad_ffn2_w_up': 'grad_w', 'grad_ffn2_w_down': 'grad_w', 'delta_ffn1_norm': 'delta_w', 'delta_ffn1_w_gate': 'delta_w', 'delta_ffn1_w_up': 'delta_w', 'delta_ffn1_w_down': 'delta_w', 'delta_mix_norm': 'delta_w', 'delta_w_in': 'delta_w', 'delta_gate_bias': 'delta_w', 'delta_q_a_norm': 'delta_w', 'delta_w_uq': 'delta_w', 'delta_kv_a_norm': 'delta_w', 'delta_w_uk': 'delta_w', 'delta_w_uv': 'delta_w', 'delta_q_head_norm': 'delta_w', 'delta_k_head_norm': 'delta_w', 'delta_w_proj_attn': 'delta_w', 'delta_conv_w': 'delta_w', 'delta_w_proj_conv': 'delta_w', 'delta_w_out': 'delta_w', 'delta_ffn2_norm': 'delta_w', 'delta_ffn2_w_gate': 'delta_w', 'delta_ffn2_w_up': 'delta_w', 'delta_ffn2_w_down': 'delta_w', 'new_m_ffn1_norm': 'new_m', 'new_m_ffn1_w_gate': 'new_m', 'new_m_ffn1_w_up': 'new_m', 'new_m_ffn1_w_down': 'new_m', 'new_m_mix_norm': 'new_m', 'new_m_w_in': 'new_m', 'new_m_gate_bias': 'new_m', 'new_m_q_a_norm': 'new_m', 'new_m_w_uq': 'new_m', 'new_m_kv_a_norm': 'new_m', 'new_m_w_uk': 'new_m', 'new_m_w_uv': 'new_m', 'new_m_q_head_norm': 'new_m', 'new_m_k_head_norm': 'new_m', 'new_m_w_proj_attn': 'new_m', 'new_m_conv_w': 'new_m', 'new_m_w_proj_conv': 'new_m', 'new_m_w_out': 'new_m', 'new_m_ffn2_norm': 'new_m', 'new_m_ffn2_w_gate': 'new_m', 'new_m_ffn2_w_up': 'new_m', 'new_m_ffn2_w_down': 'new_m', 'new_v_ffn1_norm': 'new_v', 'new_v_ffn1_w_gate': 'new_v', 'new_v_ffn1_w_up': 'new_v', 'new_v_ffn1_w_down': 'new_v', 'new_v_mix_norm': 'new_v', 'new_v_w_in': 'new_v', 'new_v_gate_bias': 'new_v', 'new_v_q_a_norm': 'new_v', 'new_v_w_uq': 'new_v', 'new_v_kv_a_norm': 'new_v', 'new_v_w_uk': 'new_v', 'new_v_w_uv': 'new_v', 'new_v_q_head_norm': 'new_v', 'new_v_k_head_norm': 'new_v', 'new_v_w_proj_attn': 'new_v', 'new_v_conv_w': 'new_v', 'new_v_w_proj_conv': 'new_v', 'new_v_w_out': 'new_v', 'new_v_ffn2_norm': 'new_v', 'new_v_ffn2_w_gate': 'new_v', 'new_v_ffn2_w_up': 'new_v', 'new_v_ffn2_w_down': 'new_v'}


def _forward(args):
    return _fwd_reference(*[args[k] for k in FWD_PARAMS])


def _output_shape():
    out = _jax.eval_shape(lambda: _forward(_fwd_setup_inputs(0)))
    return out.shape, out.dtype

N_MICROBATCH = 1
ADAM_LR = 0.001
ADAM_B1 = 0.9
ADAM_B2 = 0.999
ADAM_EPS = 1e-08
ADAM_WD = 0.01
ADAM_STEP = 10
PER_EXAMPLE_BATCH_AXIS = {'x': 0, 'positions': 0, 'loss_target': 0}
SHARED_INPUTS = []
_WEIGHT_DTYPES = {'ffn1_norm': _jnp.float32, 'ffn1_w_gate': _jnp.float32, 'ffn1_w_up': _jnp.float32, 'ffn1_w_down': _jnp.float32, 'mix_norm': _jnp.float32, 'w_in': _jnp.float32, 'gate_bias': _jnp.float32, 'q_a_norm': _jnp.float32, 'w_uq': _jnp.float32, 'kv_a_norm': _jnp.float32, 'w_uk': _jnp.float32, 'w_uv': _jnp.float32, 'q_head_norm': _jnp.float32, 'k_head_norm': _jnp.float32, 'w_proj_attn': _jnp.float32, 'conv_w': _jnp.float32, 'w_proj_conv': _jnp.float32, 'w_out': _jnp.float32, 'ffn2_norm': _jnp.float32, 'ffn2_w_gate': _jnp.float32, 'ffn2_w_up': _jnp.float32, 'ffn2_w_down': _jnp.float32}
MOMENT_SCALE = {'ffn1_norm': 6.049037e+00, 'ffn1_w_gate': 9.717092e-02, 'ffn1_w_up': 1.009899e-01, 'ffn1_w_down': 1.667782e-01, 'mix_norm': 2.802040e+01, 'w_in': 2.310169e-01, 'gate_bias': 2.250124e+00, 'q_a_norm': 5.473481e-02, 'w_uq': 3.949289e-02, 'kv_a_norm': 3.356603e-01, 'w_uk': 4.007073e-02, 'w_uv': 6.284345e-02, 'q_head_norm': 4.947185e-01, 'k_head_norm': 4.942565e-01, 'w_proj_attn': 4.087066e-02, 'conv_w': 5.210036e+00, 'w_proj_conv': 3.403244e-01, 'w_out': 2.988438e-01, 'ffn2_norm': 6.177766e+00, 'ffn2_w_gate': 5.852691e-02, 'ffn2_w_up': 7.007204e-02, 'ffn2_w_down': 1.135141e-01}


def _to_microbatches(a, axis):
    t = _jnp.moveaxis(a, axis, 0)
    t = t.reshape((N_MICROBATCH, t.shape[0] // N_MICROBATCH) + t.shape[1:])
    return _jnp.moveaxis(t, 1, axis + 1)


def setup_inputs(seed: int = 0) -> dict:
    inp = _fwd_setup_inputs(seed)
    key = _jax.random.fold_in(_jax.random.key(seed), 7919)
    shape, _ = _output_shape()
    out = dict(inp)
    out["loss_target"] = _jax.random.normal(_jax.random.fold_in(key, 0), shape, _jnp.float32)
    for i, name in enumerate(TWIN_WEIGHTS):
        w = inp[name].astype(_jnp.float32)
        if MOMENT_SCALE is None:
            s = _jnp.sqrt(_jnp.mean(_jnp.square(w)) + 1e-30)
        else:
            s = MOMENT_SCALE[name]
        km, kv = _jax.random.split(_jax.random.fold_in(key, i + 1))
        out[name] = w
        out["m_" + name] = s * _jax.random.normal(km, w.shape, _jnp.float32)
        out["v_" + name] = (s * s) * _jax.random.uniform(kv, w.shape, _jnp.float32, 0.5, 1.5)
    if N_MICROBATCH > 1:
        for name, axis in PER_EXAMPLE_BATCH_AXIS.items():
            out[name] = _to_microbatches(out[name], axis)
    return {'x': out['x'], 'positions': out['positions'], 'ffn1_norm': out['ffn1_norm'], 'ffn1_w_gate': out['ffn1_w_gate'], 'ffn1_w_up': out['ffn1_w_up'], 'ffn1_w_down': out['ffn1_w_down'], 'mix_norm': out['mix_norm'], 'w_in': out['w_in'], 'gate_bias': out['gate_bias'], 'q_a_norm': out['q_a_norm'], 'w_uq': out['w_uq'], 'kv_a_norm': out['kv_a_norm'], 'w_uk': out['w_uk'], 'w_uv': out['w_uv'], 'q_head_norm': out['q_head_norm'], 'k_head_norm': out['k_head_norm'], 'w_proj_attn': out['w_proj_attn'], 'conv_w': out['conv_w'], 'w_proj_conv': out['w_proj_conv'], 'w_out': out['w_out'], 'ffn2_norm': out['ffn2_norm'], 'ffn2_w_gate': out['ffn2_w_gate'], 'ffn2_w_up': out['ffn2_w_up'], 'ffn2_w_down': out['ffn2_w_down'], 'loss_target': out['loss_target'], 'm_ffn1_norm': out['m_ffn1_norm'], 'm_ffn1_w_gate': out['m_ffn1_w_gate'], 'm_ffn1_w_up': out['m_ffn1_w_up'], 'm_ffn1_w_down': out['m_ffn1_w_down'], 'm_mix_norm': out['m_mix_norm'], 'm_w_in': out['m_w_in'], 'm_gate_bias': out['m_gate_bias'], 'm_q_a_norm': out['m_q_a_norm'], 'm_w_uq': out['m_w_uq'], 'm_kv_a_norm': out['m_kv_a_norm'], 'm_w_uk': out['m_w_uk'], 'm_w_uv': out['m_w_uv'], 'm_q_head_norm': out['m_q_head_norm'], 'm_k_head_norm': out['m_k_head_norm'], 'm_w_proj_attn': out['m_w_proj_attn'], 'm_conv_w': out['m_conv_w'], 'm_w_proj_conv': out['m_w_proj_conv'], 'm_w_out': out['m_w_out'], 'm_ffn2_norm': out['m_ffn2_norm'], 'm_ffn2_w_gate': out['m_ffn2_w_gate'], 'm_ffn2_w_up': out['m_ffn2_w_up'], 'm_ffn2_w_down': out['m_ffn2_w_down'], 'v_ffn1_norm': out['v_ffn1_norm'], 'v_ffn1_w_gate': out['v_ffn1_w_gate'], 'v_ffn1_w_up': out['v_ffn1_w_up'], 'v_ffn1_w_down': out['v_ffn1_w_down'], 'v_mix_norm': out['v_mix_norm'], 'v_w_in': out['v_w_in'], 'v_gate_bias': out['v_gate_bias'], 'v_q_a_norm': out['v_q_a_norm'], 'v_w_uq': out['v_w_uq'], 'v_kv_a_norm': out['v_kv_a_norm'], 'v_w_uk': out['v_w_uk'], 'v_w_uv': out['v_w_uv'], 'v_q_head_norm': out['v_q_head_norm'], 'v_k_head_norm': out['v_k_head_norm'], 'v_w_proj_attn': out['v_w_proj_attn'], 'v_conv_w': out['v_conv_w'], 'v_w_proj_conv': out['v_w_proj_conv'], 'v_w_out': out['v_w_out'], 'v_ffn2_norm': out['v_ffn2_norm'], 'v_ffn2_w_gate': out['v_ffn2_w_gate'], 'v_ffn2_w_up': out['v_ffn2_w_up'], 'v_ffn2_w_down': out['v_ffn2_w_down']}


def _loss(weights, diff, rest, loss_target):
    with _jax.named_scope("forward"):
        args = {**rest, TWIN_DIFF_INPUT: diff, **{k: w.astype(_WEIGHT_DTYPES[k]) for k, w in weights.items()}}
        y = _forward(args)
    with _jax.named_scope("loss_head"):
        err = _jnp.square(y.astype(_jnp.float32) - loss_target)
        return 0.5 * _jnp.sum(_jnp.mean(err, axis=-1)) if err.ndim else 0.5 * err


def _adamw(w, g, m, v):
    m = ADAM_B1 * m + (1.0 - ADAM_B1) * g
    v = ADAM_B2 * v + (1.0 - ADAM_B2) * _jnp.square(g)
    m_hat = m / (1.0 - ADAM_B1 ** ADAM_STEP)
    v_hat = v / (1.0 - ADAM_B2 ** ADAM_STEP)
    delta = -ADAM_LR * (m_hat / (_jnp.sqrt(v_hat) + ADAM_EPS) + ADAM_WD * w)
    return delta, m, v


def reference(x, positions, ffn1_norm, ffn1_w_gate, ffn1_w_up, ffn1_w_down, mix_norm, w_in, gate_bias, q_a_norm, w_uq, kv_a_norm, w_uk, w_uv, q_head_norm, k_head_norm, w_proj_attn, conv_w, w_proj_conv, w_out, ffn2_norm, ffn2_w_gate, ffn2_w_up, ffn2_w_down, loss_target, m_ffn1_norm, m_ffn1_w_gate, m_ffn1_w_up, m_ffn1_w_down, m_mix_norm, m_w_in, m_gate_bias, m_q_a_norm, m_w_uq, m_kv_a_norm, m_w_uk, m_w_uv, m_q_head_norm, m_k_head_norm, m_w_proj_attn, m_conv_w, m_w_proj_conv, m_w_out, m_ffn2_norm, m_ffn2_w_gate, m_ffn2_w_up, m_ffn2_w_down, v_ffn1_norm, v_ffn1_w_gate, v_ffn1_w_up, v_ffn1_w_down, v_mix_norm, v_w_in, v_gate_bias, v_q_a_norm, v_w_uq, v_kv_a_norm, v_w_uk, v_w_uv, v_q_head_norm, v_k_head_norm, v_w_proj_attn, v_conv_w, v_w_proj_conv, v_w_out, v_ffn2_norm, v_ffn2_w_gate, v_ffn2_w_up, v_ffn2_w_down):
    given = dict(x=x, positions=positions, ffn1_norm=ffn1_norm, ffn1_w_gate=ffn1_w_gate, ffn1_w_up=ffn1_w_up, ffn1_w_down=ffn1_w_down, mix_norm=mix_norm, w_in=w_in, gate_bias=gate_bias, q_a_norm=q_a_norm, w_uq=w_uq, kv_a_norm=kv_a_norm, w_uk=w_uk, w_uv=w_uv, q_head_norm=q_head_norm, k_head_norm=k_head_norm, w_proj_attn=w_proj_attn, conv_w=conv_w, w_proj_conv=w_proj_conv, w_out=w_out, ffn2_norm=ffn2_norm, ffn2_w_gate=ffn2_w_gate, ffn2_w_up=ffn2_w_up, ffn2_w_down=ffn2_w_down, loss_target=loss_target, m_ffn1_norm=m_ffn1_norm, m_ffn1_w_gate=m_ffn1_w_gate, m_ffn1_w_up=m_ffn1_w_up, m_ffn1_w_down=m_ffn1_w_down, m_mix_norm=m_mix_norm, m_w_in=m_w_in, m_gate_bias=m_gate_bias, m_q_a_norm=m_q_a_norm, m_w_uq=m_w_uq, m_kv_a_norm=m_kv_a_norm, m_w_uk=m_w_uk, m_w_uv=m_w_uv, m_q_head_norm=m_q_head_norm, m_k_head_norm=m_k_head_norm, m_w_proj_attn=m_w_proj_attn, m_conv_w=m_conv_w, m_w_proj_conv=m_w_proj_conv, m_w_out=m_w_out, m_ffn2_norm=m_ffn2_norm, m_ffn2_w_gate=m_ffn2_w_gate, m_ffn2_w_up=m_ffn2_w_up, m_ffn2_w_down=m_ffn2_w_down, v_ffn1_norm=v_ffn1_norm, v_ffn1_w_gate=v_ffn1_w_gate, v_ffn1_w_up=v_ffn1_w_up, v_ffn1_w_down=v_ffn1_w_down, v_mix_norm=v_mix_norm, v_w_in=v_w_in, v_gate_bias=v_gate_bias, v_q_a_norm=v_q_a_norm, v_w_uq=v_w_uq, v_kv_a_norm=v_kv_a_norm, v_w_uk=v_w_uk, v_w_uv=v_w_uv, v_q_head_norm=v_q_head_norm, v_k_head_norm=v_k_head_norm, v_w_proj_attn=v_w_proj_attn, v_conv_w=v_conv_w, v_w_proj_conv=v_w_proj_conv, v_w_out=v_w_out, v_ffn2_norm=v_ffn2_norm, v_ffn2_w_gate=v_ffn2_w_gate, v_ffn2_w_up=v_ffn2_w_up, v_ffn2_w_down=v_ffn2_w_down)
    weights = {n: given[n] for n in TWIN_WEIGHTS}
    shared = {n: given[n] for n in SHARED_INPUTS}
    per_example = {n: given[n] for n in ['x', 'positions']}
    grad_fn = _jax.value_and_grad(_loss, argnums=(0, 1))

    def one_microbatch(ex, loss_target):
        ex = dict(ex)
        diff = ex.pop(TWIN_DIFF_INPUT)
        return grad_fn(weights, diff, {**shared, **ex}, loss_target)

    if N_MICROBATCH == 1:
        loss, (grad_w, grad_x) = one_microbatch(per_example, given["loss_target"])
    else:
        def body(carry, xs):
            loss_sum, grad_sum = carry
            l_k, (gw_k, gx_k) = one_microbatch(xs[0], xs[1])
            with _jax.named_scope("update"):
                return (loss_sum + l_k, _jax.tree.map(_jnp.add, grad_sum, gw_k)), gx_k

        init = (_jnp.zeros((), _jnp.float32), _jax.tree.map(_jnp.zeros_like, weights))
        (loss, grad_w), grad_x = _jax.lax.scan(body, init, (per_example, given["loss_target"]))
    with _jax.named_scope("update"):
        delta_w, new_m, new_v = {}, {}, {}
        for n in TWIN_WEIGHTS:
            delta_w[n], new_m[n], new_v[n] = _adamw(weights[n], grad_w[n], given["m_" + n], given["v_" + n])
    return (loss, grad_x, *[grad_w[n] for n in TWIN_WEIGHTS], *[delta_w[n] for n in TWIN_WEIGHTS],
            *[new_m[n] for n in TWIN_WEIGHTS], *[new_v[n] for n in TWIN_WEIGHTS])
```

```python
import functools

import jax
import jax.numpy as jnp
from jax import lax
from jax.experimental import pallas as pl
from jax.experimental.pallas import tpu as pltpu

F32 = jnp.float32
BF16 = jnp.bfloat16

D_MODEL = 1024
N_HEADS = 8
QK_NOPE = 64
QK_ROPE = 32
QK_DIM = QK_NOPE + QK_ROPE
V_DIM = 64
HEAD_PAD = 128
Q_LORA = 384
KV_LORA = 256
ROPE_THETA = 10000.0
NORM_EPS = 1e-6
ATTN_SCALE = QK_DIM ** -0.5
MASK_VALUE = -1e30
N_CHIPS = 4
N_DEV = 8

ADAM_LR = 0.001
ADAM_B1 = 0.9
ADAM_B2 = 0.999
ADAM_EPS = 1e-08
ADAM_WD = 0.01
ADAM_STEP = 10

TOKEN_TILE = 256
ATTN_TILE = 256
TN_TILE = 512
VMEM_LIMIT = 56 * 1024 * 1024

M_COLS = 3 * D_MODEL
P_COLS = 2 * D_MODEL + Q_LORA + KV_LORA + HEAD_PAD
BIG_COLS = 5 * D_MODEL
LAT_COLS = Q_LORA + KV_LORA + HEAD_PAD

MESH_ID = pl.DeviceIdType.MESH
ANY = pl.BlockSpec(memory_space=pl.ANY)


def _params(semantics=None):
    return pltpu.CompilerParams(dimension_semantics=semantics, vmem_limit_bytes=VMEM_LIMIT)


def _resident(shape):
    nd = len(shape)
    return pl.BlockSpec(shape, lambda *_: (0,) * nd, pipeline_mode=pl.Buffered(1))


def _const(shape):
    nd = len(shape)
    return pl.BlockSpec(shape, lambda *_: (0,) * nd)


def _mm(a, b):
    return jnp.dot(a, b, preferred_element_type=F32)


def _mm_nt(a, b):
    return lax.dot_general(a, b, (((1,), (1,)), ((), ())), preferred_element_type=F32)


def _mm_tn(a, b):
    return lax.dot_general(a, b, (((0,), (0,)), ((), ())), preferred_element_type=F32)


def _bf(a):
    return a.astype(BF16)


def _sigmoid(a):
    return 1.0 / (1.0 + jnp.exp(-a))


def _rms(x, gain, n=None):
    n = x.shape[-1] if n is None else n
    r = lax.rsqrt(jnp.sum(x * x, axis=-1, keepdims=True) * (1.0 / n) + NORM_EPS)
    return (x * r) * gain, r


def _rms_bwd(x, r, gain, dh, n=None):
    n = x.shape[-1] if n is None else n
    u = dh * gain
    dx = r * u - x * ((r * r * r) * (jnp.sum(u * x, axis=-1, keepdims=True) * (1.0 / n)))
    dgain = jnp.sum(dh * (x * r), axis=0, keepdims=True)
    return dx, dgain


def _rope_swap(t):
    lane = lax.broadcasted_iota(jnp.int32, t.shape, 1)
    lo = (lane >= QK_NOPE) & (lane < QK_NOPE + QK_ROPE // 2)
    hi = (lane >= QK_NOPE + QK_ROPE // 2) & (lane < QK_DIM)
    up = pltpu.roll(t, HEAD_PAD - QK_ROPE // 2, 1)
    down = pltpu.roll(t, QK_ROPE // 2, 1)
    return jnp.where(lo, up, jnp.where(hi, down, 0.0))


def _rope(t, cos, sin):
    return t * cos + _rope_swap(t) * sin


def _rope_bwd(dt, cos, sin):
    return dt * cos + _rope_swap(dt * sin)


def _shift_down(u, prev8, k):
    s = pltpu.roll(u, k, 0)
    p = pltpu.roll(prev8, k, 0)
    row = lax.broadcasted_iota(jnp.int32, prev8.shape, 0)
    top = jnp.where(row < k, p, s[:8])
    return jnp.concatenate([top, s[8:]], axis=0)


def _shift_up(d, next8, k):
    tm = d.shape[0]
    s = pltpu.roll(d, tm - k, 0)
    n = pltpu.roll(next8, 8 - k, 0)
    row = lax.broadcasted_iota(jnp.int32, next8.shape, 0)
    bot = jnp.where(row >= 8 - k, n, s[tm - 8:])
    return jnp.concatenate([s[:tm - 8], bot], axis=0)


def _ffn_fwd(x, gain, wg, wu, wd, target, name):
    t, d = x.shape
    nb, _, f = wg.shape
    tm = TOKEN_TILE
    with_loss = target is not None

    def body(*refs):
        if with_loss:
            x_ref, g_ref, wg_ref, wu_ref, wd_ref, t_ref, out_ref, gate_ref, up_ref, act_ref, loss_ref = refs
        else:
            x_ref, g_ref, wg_ref, wu_ref, wd_ref, out_ref, gate_ref, up_ref, act_ref = refs
        xv = x_ref[...]
        h, _ = _rms(xv, g_ref[...])
        hb = _bf(h)
        y = jnp.zeros((tm, d), F32)
        for j in range(nb):
            gate = _mm(hb, wg_ref[j])
            up = _mm(hb, wu_ref[j])
            act = _bf((gate * _sigmoid(gate)) * up)
            y = y + _mm(act, wd_ref[j])
            gate_ref[j] = _bf(gate)
            up_ref[j] = _bf(up)
            act_ref[j] = act
        out = xv + 0.5 * y
        if with_loss:
            err = out - t_ref[...]
            out_ref[...] = err * (1.0 / d)

            @pl.when(pl.program_id(0) == 0)
            def _():
                loss_ref[...] = jnp.zeros_like(loss_ref)

            part = jnp.sum(jnp.sum(err * err, axis=1, keepdims=True), axis=0, keepdims=True)
            loss_ref[...] += jnp.broadcast_to(part * (0.5 / d), loss_ref.shape)
        else:
            out_ref[...] = out

    tok = pl.BlockSpec((tm, d), lambda i: (i, 0))
    blk = pl.BlockSpec((nb, tm, f), lambda i: (0, i, 0))
    in_specs = [tok, _const((1, d)), _resident(wg.shape), _resident(wu.shape), _resident(wd.shape)]
    args = [x, gain, wg, wu, wd]
    out_shape = [jax.ShapeDtypeStruct((t, d), F32)] + [jax.ShapeDtypeStruct((nb, t, f), BF16)] * 3
    out_specs = [tok, blk, blk, blk]
    if with_loss:
        in_specs.append(tok)
        args.append(target)
        out_shape.append(jax.ShapeDtypeStruct((1, 128), F32))
        out_specs.append(_const((1, 128)))
    return pl.pallas_call(body, name=name, grid=(t // tm,), in_specs=in_specs, out_specs=out_specs, out_shape=out_shape,
                          compiler_params=_params(("arbitrary",)))(*args)


def _ffn_bwd_x(x, gain, dout, gate, up, wg, wu, wd, name):
    t, d = x.shape
    nb, _, f = wg.shape
    tm = TOKEN_TILE

    def body(x_ref, g_ref, dout_ref, gate_ref, up_ref, wg_ref, wu_ref, wd_ref,
             dx_ref, dgain_ref, hb_ref, dgate_ref, dup_ref, dyb_ref):
        xv = x_ref[...]
        gain_v = g_ref[...]
        h, r = _rms(xv, gain_v)
        hb_ref[...] = _bf(h)
        dout_v = dout_ref[...]
        dyb = _bf(0.5 * dout_v)
        dyb_ref[...] = dyb
        dh = jnp.zeros((tm, d), F32)
        for j in range(nb):
            gt = gate_ref[j].astype(F32)
            uv = up_ref[j].astype(F32)
            s = _sigmoid(gt)
            dact = _mm_nt(dyb, wd_ref[j])
            dup = _bf(dact * (gt * s))
            dgate = _bf((dact * uv) * (s * (1.0 + gt * (1.0 - s))))
            dh = dh + _mm_nt(dgate, wg_ref[j]) + _mm_nt(dup, wu_ref[j])
            dgate_ref[j] = dgate
            dup_ref[j] = dup
        dxn, dgain = _rms_bwd(xv, r, gain_v, dh)
        dx_ref[...] = dout_v + dxn

        @pl.when(pl.program_id(0) == 0)
        def _():
            dgain_ref[...] = jnp.zeros_like(dgain_ref)

        dgain_ref[...] += dgain

    tok = pl.BlockSpec((tm, d), lambda i: (i, 0))
    blk = pl.BlockSpec((nb, tm, f), lambda i: (0, i, 0))
    return pl.pallas_call(
        body, name=name, grid=(t // tm,),
        in_specs=[tok, _const((1, d)), tok, blk, blk, _resident(wg.shape), _resident(wu.shape), _resident(wd.shape)],
        out_specs=[tok, _const((1, d)), tok, blk, blk, tok],
        out_shape=[jax.ShapeDtypeStruct((t, d), F32), jax.ShapeDtypeStruct((1, d), F32), jax.ShapeDtypeStruct((t, d), BF16),
                   jax.ShapeDtypeStruct((nb, t, f), BF16), jax.ShapeDtypeStruct((nb, t, f), BF16),
                   jax.ShapeDtypeStruct((t, d), BF16)],
        compiler_params=_params(("arbitrary",)))(x, gain, dout, gate, up, wg, wu, wd)


def _tn_matmul(a, b, name, split_n=1):
    t = a.shape[-2]
    k = a.shape[-1]
    n = b.shape[-1]
    tt = min(TN_TILE, t)
    nt = t // tt

    def body(a_ref, b_ref, o_ref):
        @pl.when(pl.program_id(1) == 0)
        def _():
            o_ref[...] = jnp.zeros_like(o_ref)

        o_ref[...] += _mm_tn(a_ref[...], b_ref[...])

    if split_n > 1:
        assert a.ndim == 2 and b.ndim == 2 and n % (split_n * 128) == 0
        tn = n // split_n
        g = split_n
        a_spec = pl.BlockSpec((tt, k), lambda gi, ti: (ti, 0))
        b_spec = pl.BlockSpec((tt, tn), lambda gi, ti: (ti, gi))
        o_spec = pl.BlockSpec((k, tn), lambda gi, ti: (0, gi))
        out_shape = jax.ShapeDtypeStruct((k, n), F32)
    else:
        g = a.shape[0] if a.ndim == 3 else b.shape[0]
        a_spec = (pl.BlockSpec((None, tt, k), lambda gi, ti: (gi, ti, 0)) if a.ndim == 3
                  else pl.BlockSpec((tt, k), lambda gi, ti: (ti, 0)))
        b_spec = (pl.BlockSpec((None, tt, n), lambda gi, ti: (gi, ti, 0)) if b.ndim == 3
                  else pl.BlockSpec((tt, n), lambda gi, ti: (ti, 0)))
        o_spec = pl.BlockSpec((None, k, n), lambda gi, ti: (gi, 0, 0))
        out_shape = jax.ShapeDtypeStruct((g, k, n), F32)
    return pl.pallas_call(body, name=name, grid=(g, nt), in_specs=[a_spec, b_spec], out_specs=o_spec, out_shape=out_shape,
                          compiler_params=_params(("arbitrary", "arbitrary")))(a, b)


def _inproj_fwd(x1, gain, w_m, w_p, qa_gain, kva_gain, qh_gain, kh_gain, w_uq, w_uk, w_uv, cos, sin):
    t, d = x1.shape
    tm = TOKEN_TILE

    def body(x_ref, g_ref, wm_ref, wp_ref, qa_ref, kva_ref, qh_ref, kh_ref, wuq_ref, wuk_ref, wuv_ref, cos_ref, sin_ref,
             hb_ref, big_ref, lat_ref, q_ref, k_ref, v_ref):
        h, _ = _rms(x_ref[...], g_ref[...])
        hb = _bf(h)
        hb_ref[...] = hb
        big_ref[:, :M_COLS] = _mm(hb, wm_ref[...])
        pp = _mm(hb, wp_ref[...])
        big_ref[:, M_COLS:] = pp[:, :2 * D_MODEL]
        lat = pp[:, 2 * D_MODEL:]
        lat_ref[...] = lat
        cq, _ = _rms(lat[:, :Q_LORA], qa_ref[...])
        ckv, _ = _rms(lat[:, Q_LORA:Q_LORA + KV_LORA], kva_ref[...])
        k_rope = lat[:, Q_LORA + KV_LORA:]
        cqb = _bf(cq)
        ckvb = _bf(ckv)
        cos_v = cos_ref[...]
        sin_v = sin_ref[...]
        for hd in range(N_HEADS):
            qn, _ = _rms(_mm(cqb, wuq_ref[hd]), qh_ref[...], QK_DIM)
            q_ref[hd] = _bf(_rope(qn, cos_v, sin_v))
            kn, _ = _rms(_mm(ckvb, wuk_ref[hd]) + k_rope, kh_ref[...], QK_DIM)
            k_ref[hd] = _bf(_rope(kn, cos_v, sin_v))
            v_ref[hd] = _bf(_mm(ckvb, wuv_ref[hd]))

    tok = lambda c: pl.BlockSpec((tm, c), lambda i: (i, 0))
    head = lambda c: pl.BlockSpec((N_HEADS, tm, c), lambda i: (0, i, 0))
    return pl.pallas_call(
        body, name="inproj_fwd", grid=(t // tm,),
        in_specs=[tok(d), _const((1, d)), _resident(w_m.shape), _resident(w_p.shape), _const((1, Q_LORA)), _const((1, KV_LORA)),
                  _const((1, HEAD_PAD)), _const((1, HEAD_PAD)), _resident(w_uq.shape), _resident(w_uk.shape),
                  _resident(w_uv.shape), tok(HEAD_PAD), tok(HEAD_PAD)],
        out_specs=[tok(d), tok(BIG_COLS), tok(LAT_COLS), head(HEAD_PAD), head(HEAD_PAD), head(V_DIM)],
        out_shape=[jax.ShapeDtypeStruct((t, d), BF16), jax.ShapeDtypeStruct((t, BIG_COLS), F32),
                   jax.ShapeDtypeStruct((t, LAT_COLS), F32), jax.ShapeDtypeStruct((N_HEADS, t, HEAD_PAD), BF16),
                   jax.ShapeDtypeStruct((N_HEADS, t, HEAD_PAD), BF16), jax.ShapeDtypeStruct((N_HEADS, t, V_DIM), BF16)],
        compiler_params=_params(("arbitrary",)))(x1, gain, w_m, w_p, qa_gain, kva_gain, qh_gain, kh_gain, w_uq, w_uk, w_uv,
                                                  cos, sin)


def _causal_mask(s, q0, k0):
    rows = q0 + lax.broadcasted_iota(jnp.int32, s.shape, 0)
    cols = k0 + lax.broadcasted_iota(jnp.int32, s.shape, 1)
    return jnp.where(cols <= rows, s, MASK_VALUE)


def _attn_fwd(q, k, v, seq):
    _, t, _ = q.shape
    nseq = t // seq
    tq = tk = ATTN_TILE
    nq = seq // tq

    def body(q_ref, k_ref, v_ref, o_ref, lse_ref):
        i = pl.program_id(2)
        outs = []
        for hh in range(2):
            qv = q_ref[hh]

            def step(j, carry, hh=hh, qv=qv):
                m, l, acc = carry
                k0 = pl.multiple_of(j * tk, tk)
                kj = k_ref[hh, pl.ds(k0, tk), :]
                vj = v_ref[hh, pl.ds(k0, tk), :]
                s = _causal_mask(_mm_nt(qv, kj) * ATTN_SCALE, i * tq, k0)
                m_new = jnp.maximum(m, jnp.max(s, axis=-1, keepdims=True))
                p = jnp.exp(s - m_new)
                alpha = jnp.exp(m - m_new)
                return m_new, alpha * l + jnp.sum(p, axis=-1, keepdims=True), alpha * acc + _mm(_bf(p), vj)

            init = (jnp.full((tq, 1), MASK_VALUE, F32), jnp.zeros((tq, 1), F32), jnp.zeros((tq, V_DIM), F32))
            m, l, acc = lax.fori_loop(0, i + 1, step, init)
            outs.append(acc / l)
            lse_ref[hh] = m + jnp.log(l)
        o_ref[...] = _bf(jnp.concatenate(outs, axis=-1))

    return pl.pallas_call(
        body, name="attn_fwd", grid=(nseq, N_HEADS // 2, nq),
        in_specs=[pl.BlockSpec((2, tq, HEAD_PAD), lambda b, hp, i: (hp, b * nq + i, 0)),
                  pl.BlockSpec((2, seq, HEAD_PAD), lambda b, hp, i: (hp, b, 0)),
                  pl.BlockSpec((2, seq, V_DIM), lambda b, hp, i: (hp, b, 0))],
        out_specs=[pl.BlockSpec((tq, 2 * V_DIM), lambda b, hp, i: (b * nq + i, hp)),
                   pl.BlockSpec((2, tq, 1), lambda b, hp, i: (hp, b * nq + i, 0))],
        out_shape=[jax.ShapeDtypeStruct((t, N_HEADS * V_DIM), BF16), jax.ShapeDtypeStruct((N_HEADS, t, 1), F32)],
        compiler_params=_params(("arbitrary", "arbitrary", "arbitrary")))(q, k, v)


def _attn_bwd(q, k, v, o, do, lse, seq):
    _, t, _ = q.shape
    nseq = t // seq
    tq = tk = ATTN_TILE
    n = seq // tq

    def body(q_ref, k_ref, v_ref, o_ref, do_ref, lse_ref, dq_ref, dk_ref, dv_ref):
        dq_ref[...] = jnp.zeros_like(dq_ref)
        dk_ref[...] = jnp.zeros_like(dk_ref)
        dv_ref[...] = jnp.zeros_like(dv_ref)
        for hh in range(2):
            cols = slice(hh * V_DIM, (hh + 1) * V_DIM)

            def kv_step(j, _, hh=hh, cols=cols):
                k0 = pl.multiple_of(j * tk, tk)
                kj = k_ref[hh, pl.ds(k0, tk), :]
                vj = v_ref[hh, pl.ds(k0, tk), :]

                def q_step(i, _):
                    q0 = pl.multiple_of(i * tq, tq)
                    qi = q_ref[hh, pl.ds(q0, tq), :]
                    doi = do_ref[pl.ds(q0, tq), cols]
                    oi = o_ref[pl.ds(q0, tq), cols].astype(F32)
                    delta = jnp.sum(doi * oi, axis=-1, keepdims=True)
                    doib = _bf(doi)
                    s = _causal_mask(_mm_nt(qi, kj) * ATTN_SCALE, q0, k0)
                    p = jnp.exp(s - lse_ref[hh, pl.ds(q0, tq), :])
                    dv_ref[hh, pl.ds(k0, tk), :] += _mm_tn(_bf(p), doib)
                    dp = _mm_nt(doib, vj)
                    ds = _bf((p * (dp - delta)) * ATTN_SCALE)
                    dk_ref[hh, pl.ds(k0, tk), :] += _mm_tn(ds, qi)
                    dq_ref[hh, pl.ds(q0, tq), :] += _mm(ds, kj)
                    return 0

                lax.fori_loop(j, n, q_step, 0)
                return 0

            lax.fori_loop(0, n, kv_step, 0)

    hspec = lambda c: pl.BlockSpec((2, seq, c), lambda b, hp: (hp, b, 0))
    pair = pl.BlockSpec((seq, 2 * V_DIM), lambda b, hp: (b, hp))
    return pl.pallas_call(
        body, name="attn_bwd", grid=(nseq, N_HEADS // 2),
        in_specs=[hspec(HEAD_PAD), hspec(HEAD_PAD), hspec(V_DIM), pair, pair, hspec(1)],
        out_specs=[hspec(HEAD_PAD), hspec(HEAD_PAD), hspec(V_DIM)],
        out_shape=[jax.ShapeDtypeStruct((N_HEADS, t, HEAD_PAD), F32), jax.ShapeDtypeStruct((N_HEADS, t, HEAD_PAD), F32),
                   jax.ShapeDtypeStruct((N_HEADS, t, V_DIM), F32)],
        compiler_params=_params(("arbitrary", "arbitrary")))(q, k, v, o, do, lse)


def _mixer_values(o_ref, gb_ref, gla_ref, glb_ref, xc_ref, gc_ref, xcp_ref, gcp_ref, bias_ref, cw_ref, wpa_ref, wpc_ref,
                  first_of_seq):
    gb = gb_ref[...]
    u = gc_ref[...] * xc_ref[...]
    u_prev = jnp.where(first_of_seq, 0.0, gcp_ref[...] * xcp_ref[...])
    cw = cw_ref[...]
    z = cw[2:3] * u + cw[1:2] * _shift_down(u, u_prev, 1) + cw[0:1] * _shift_down(u, u_prev, 2)
    gbz = _bf(gb * z)
    y_b = _mm(gbz, wpc_ref[...])
    y_a = _mm(o_ref[...], wpa_ref[...])
    bias = bias_ref[...]
    gate_a = _sigmoid(gla_ref[...] + bias[:, :D_MODEL])
    gate_b = _sigmoid(glb_ref[...] + bias[:, D_MODEL:])
    merged = _bf(gate_a * y_a + gate_b * y_b)
    return gb, u, z, gbz, y_a, y_b, gate_a, gate_b, merged


def _mixer_specs(tm, seq):
    d = D_MODEL
    tok = pl.BlockSpec((tm, d), lambda i: (i, 0))
    col = lambda c: pl.BlockSpec((tm, d), lambda i: (i, c))
    prev = lambda c: pl.BlockSpec((8, d), lambda i: (jnp.maximum(i * (tm // 8) - 1, 0), c))
    o_spec = pl.BlockSpec((tm, N_HEADS * V_DIM), lambda i: (i, 0))
    fwd_specs = [o_spec, col(0), col(1), col(2), col(3), col(4), prev(3), prev(4), _const((1, 2 * d)), _const((3, d)),
                 _resident((N_HEADS * V_DIM, d)), _resident((d, d)), _resident((d, d))]
    return tok, fwd_specs


def _mix_fwd(x1, o, big, gate_bias, conv_w, w_pa, w_pc, w_out, seq):
    t, d = x1.shape
    tm = TOKEN_TILE
    tiles_per_seq = seq // tm

    def body(x_ref, o_ref, gb_ref, gla_ref, glb_ref, xc_ref, gc_ref, xcp_ref, gcp_ref, bias_ref, cw_ref, wpa_ref, wpc_ref,
             wout_ref, x2_ref):
        first = pl.program_id(0) % tiles_per_seq == 0
        merged = _mixer_values(o_ref, gb_ref, gla_ref, glb_ref, xc_ref, gc_ref, xcp_ref, gcp_ref, bias_ref, cw_ref, wpa_ref,
                               wpc_ref, first)[-1]
        x2_ref[...] = x_ref[...] + _mm(merged, wout_ref[...])

    tok, fwd_specs = _mixer_specs(tm, seq)
    return pl.pallas_call(
        body, name="mix_fwd", grid=(t // tm,), in_specs=[tok] + fwd_specs, out_specs=tok,
        out_shape=jax.ShapeDtypeStruct((t, d), F32),
        compiler_params=_params(("arbitrary",)))(x1, o, big, big, big, big, big, big, big, gate_bias, conv_w, w_pa, w_pc, w_out)


def _mix_bwd(dx2, o, big, gate_bias, conv_w, w_pa, w_pc, w_out, seq):
    t, d = dx2.shape
    tm = TOKEN_TILE
    tiles_per_seq = seq // tm

    def body(dx_ref, o_ref, gb_ref, gla_ref, glb_ref, xc_ref, gc_ref, xcp_ref, gcp_ref, bias_ref, cw_ref, wpa_ref, wpc_ref,
             wout_ref, do_ref, dz_ref, dm_ref, dbias_ref, dwpa_ref, dwpc_ref, dwout_ref):
        first = pl.program_id(0) % tiles_per_seq == 0
        gb, _, z, gbz, y_a, y_b, gate_a, gate_b, merged = _mixer_values(
            o_ref, gb_ref, gla_ref, glb_ref, xc_ref, gc_ref, xcp_ref, gcp_ref, bias_ref, cw_ref, wpa_ref, wpc_ref, first)

        @pl.when(pl.program_id(0) == 0)
        def _():
            dbias_ref[...] = jnp.zeros_like(dbias_ref)
            dwpa_ref[...] = jnp.zeros_like(dwpa_ref)
            dwpc_ref[...] = jnp.zeros_like(dwpc_ref)
            dwout_ref[...] = jnp.zeros_like(dwout_ref)

        dxb = _bf(dx_ref[...])
        dmerged = _mm_nt(dxb, wout_ref[...])
        dwout_ref[...] += _mm_tn(merged, dxb)
        dla = (dmerged * y_a) * (gate_a * (1.0 - gate_a))
        dlb = (dmerged * y_b) * (gate_b * (1.0 - gate_b))
        dbias_ref[:, :d] += jnp.sum(dla, axis=0, keepdims=True)
        dbias_ref[:, d:] += jnp.sum(dlb, axis=0, keepdims=True)
        dya = _bf(dmerged * gate_a)
        dyb = _bf(dmerged * gate_b)
        do_ref[...] = _mm_nt(dya, wpa_ref[...])
        dwpa_ref[...] += _mm_tn(o_ref[...], dya)
        dgz = _mm_nt(dyb, wpc_ref[...])
        dwpc_ref[...] += _mm_tn(gbz, dyb)
        dz_ref[...] = dgz * gb
        dm_ref[:, :d] = _bf(dgz * z)
        dm_ref[:, d:2 * d] = _bf(dla)
        dm_ref[:, 2 * d:] = _bf(dlb)

    tok, fwd_specs = _mixer_specs(tm, seq)
    hv = N_HEADS * V_DIM
    return pl.pallas_call(
        body, name="mix_bwd", grid=(t // tm,), in_specs=[tok] + fwd_specs,
        out_specs=[pl.BlockSpec((tm, hv), lambda i: (i, 0)), tok, pl.BlockSpec((tm, M_COLS), lambda i: (i, 0)),
                   _const((1, 2 * d)), _const((hv, d)), _const((d, d)), _const((d, d))],
        out_shape=[jax.ShapeDtypeStruct((t, hv), F32), jax.ShapeDtypeStruct((t, d), F32), jax.ShapeDtypeStruct((t, M_COLS), BF16),
                   jax.ShapeDtypeStruct((1, 2 * d), F32), jax.ShapeDtypeStruct((hv, d), F32), jax.ShapeDtypeStruct((d, d), F32),
                   jax.ShapeDtypeStruct((d, d), F32)],
        compiler_params=_params(("arbitrary",)))(dx2, o, big, big, big, big, big, big, big, gate_bias, conv_w, w_pa, w_pc, w_out)


def _prep_bwd(lat, big, dz, dq, dk, dv, qa_gain, kva_gain, qh_gain, kh_gain, w_uq, w_uk, w_uv, cos, sin, conv_w, seq):
    t = lat.shape[0]
    d = D_MODEL
    tm = TOKEN_TILE
    tiles_per_seq = seq // tm
    last_blk = t // 8 - 1

    def body(lat_ref, xc_ref, gc_ref, dz_ref, dzn_ref, dq_ref, dk_ref, dv_ref, qa_ref, kva_ref, qh_ref, kh_ref, wuq_ref, wuk_ref,
             wuv_ref, cos_ref, sin_ref, cw_ref,
             dp_ref, dwuq_ref, dwuk_ref, dwuv_ref, dqa_ref, dkva_ref, dqh_ref, dkh_ref, dcw_ref):
        pid = pl.program_id(0)

        @pl.when(pid == 0)
        def _():
            for r in (dwuq_ref, dwuk_ref, dwuv_ref, dqa_ref, dkva_ref, dqh_ref, dkh_ref, dcw_ref):
                r[...] = jnp.zeros_like(r)

        last = pid % tiles_per_seq == tiles_per_seq - 1
        dzv = dz_ref[...]
        dz_next = jnp.where(last, 0.0, dzn_ref[...])
        dz1 = _shift_up(dzv, dz_next, 1)
        dz2 = _shift_up(dzv, dz_next, 2)
        cw = cw_ref[...]
        xc = xc_ref[...]
        gc = gc_ref[...]
        u = gc * xc
        du = cw[2:3] * dzv + cw[1:2] * dz1 + cw[0:1] * dz2
        dp_ref[:, :d] = _bf(du * gc)
        dp_ref[:, d:2 * d] = _bf(du * xc)
        dcw_ref[0:1, :] += jnp.sum(dz2 * u, axis=0, keepdims=True)
        dcw_ref[1:2, :] += jnp.sum(dz1 * u, axis=0, keepdims=True)
        dcw_ref[2:3, :] += jnp.sum(dzv * u, axis=0, keepdims=True)

        lat_v = lat_ref[...]
        q_lat = lat_v[:, :Q_LORA]
        kv_lat = lat_v[:, Q_LORA:Q_LORA + KV_LORA]
        k_rope = lat_v[:, Q_LORA + KV_LORA:]
        qa_gain_v = qa_ref[...]
        kva_gain_v = kva_ref[...]
        qh_gain_v = qh_ref[...]
        kh_gain_v = kh_ref[...]
        cq, rq = _rms(q_lat, qa_gain_v)
        ckv, rkv = _rms(kv_lat, kva_gain_v)
        cqb = _bf(cq)
        ckvb = _bf(ckv)
        cos_v = cos_ref[...]
        sin_v = sin_ref[...]
        lane = lax.broadcasted_iota(jnp.int32, (tm, HEAD_PAD), 1)
        rope_lanes = (lane >= QK_NOPE) & (lane < QK_DIM)
        dcq = jnp.zeros((tm, Q_LORA), F32)
        dckv = jnp.zeros((tm, KV_LORA), F32)
        dk_rope = jnp.zeros((tm, HEAD_PAD), F32)
        dqh_gain = jnp.zeros((1, HEAD_PAD), F32)
        dkh_gain = jnp.zeros((1, HEAD_PAD), F32)
        for hd in range(N_HEADS):
            q_pre = _mm(cqb, wuq_ref[hd])
            _, rr = _rms(q_pre, qh_gain_v, QK_DIM)
            dq_pre, dg = _rms_bwd(q_pre, rr, qh_gain_v, _rope_bwd(dq_ref[hd], cos_v, sin_v), QK_DIM)
            dqh_gain = dqh_gain + dg
            dq_pre_b = _bf(dq_pre)
            dcq = dcq + _mm_nt(dq_pre_b, wuq_ref[hd])
            dwuq_ref[hd] += _mm_tn(cqb, dq_pre_b)

            k_pre = _mm(ckvb, wuk_ref[hd]) + k_rope
            _, rr = _rms(k_pre, kh_gain_v, QK_DIM)
            dk_pre, dg = _rms_bwd(k_pre, rr, kh_gain_v, _rope_bwd(dk_ref[hd], cos_v, sin_v), QK_DIM)
            dkh_gain = dkh_gain + dg
            dk_rope = dk_rope + jnp.where(rope_lanes, dk_pre, 0.0)
            dk_pre_b = _bf(dk_pre)
            dvb = _bf(dv_ref[hd])
            dckv = dckv + _mm_nt(dk_pre_b, wuk_ref[hd]) + _mm_nt(dvb, wuv_ref[hd])
            dwuk_ref[hd] += _mm_tn(ckvb, dk_pre_b)
            dwuv_ref[hd] += _mm_tn(ckvb, dvb)
        dqh_ref[...] += dqh_gain
        dkh_ref[...] += dkh_gain
        dq_lat, dg = _rms_bwd(q_lat, rq, qa_gain_v, dcq)
        dqa_ref[...] += dg
        dkv_lat, dg = _rms_bwd(kv_lat, rkv, kva_gain_v, dckv)
        dkva_ref[...] += dg
        dp_ref[:, 2 * d:2 * d + Q_LORA] = _bf(dq_lat)
        dp_ref[:, 2 * d + Q_LORA:2 * d + Q_LORA + KV_LORA] = _bf(dkv_lat)
        dp_ref[:, 2 * d + Q_LORA + KV_LORA:] = _bf(dk_rope)

    tok = lambda c: pl.BlockSpec((tm, c), lambda i: (i, 0))
    col = lambda c: pl.BlockSpec((tm, d), lambda i: (i, c))
    head = lambda c: pl.BlockSpec((N_HEADS, tm, c), lambda i: (0, i, 0))
    nxt = pl.BlockSpec((8, d), lambda i: (jnp.minimum((i + 1) * (tm // 8), last_blk), 0))
    return pl.pallas_call(
        body, name="prep_bwd", grid=(t // tm,),
        in_specs=[tok(LAT_COLS), col(3), col(4), tok(d), nxt, head(HEAD_PAD), head(HEAD_PAD), head(V_DIM),
                  _const((1, Q_LORA)), _const((1, KV_LORA)), _const((1, HEAD_PAD)), _const((1, HEAD_PAD)),
                  _resident(w_uq.shape), _resident(w_uk.shape), _resident(w_uv.shape), tok(HEAD_PAD), tok(HEAD_PAD),
                  _const((3, d))],
        out_specs=[tok(P_COLS), _const(w_uq.shape), _const(w_uk.shape), _const(w_uv.shape), _const((1, Q_LORA)),
                   _const((1, KV_LORA)), _const((1, HEAD_PAD)), _const((1, HEAD_PAD)), _const((3, d))],
        out_shape=[jax.ShapeDtypeStruct((t, P_COLS), BF16), jax.ShapeDtypeStruct(w_uq.shape, F32),
                   jax.ShapeDtypeStruct(w_uk.shape, F32), jax.ShapeDtypeStruct(w_uv.shape, F32),
                   jax.ShapeDtypeStruct((1, Q_LORA), F32), jax.ShapeDtypeStruct((1, KV_LORA), F32),
                   jax.ShapeDtypeStruct((1, HEAD_PAD), F32), jax.ShapeDtypeStruct((1, HEAD_PAD), F32),
                   jax.ShapeDtypeStruct((3, d), F32)],
        compiler_params=_params(("arbitrary",)))(lat, big, big, dz, dz, dq, dk, dv, qa_gain, kva_gain, qh_gain, kh_gain,
                                                  w_uq, w_uk, w_uv, cos, sin, conv_w)


def _inproj_bwd(x1, gain, dx2, dm, dp, w_m, w_p):
    t, d = x1.shape
    tm = TOKEN_TILE

    def body(x_ref, g_ref, dx2_ref, dm_ref, dp_ref, wm_ref, wp_ref, dx1_ref, dgain_ref):
        xv = x_ref[...]
        gain_v = g_ref[...]
        _, r = _rms(xv, gain_v)
        dh = _mm_nt(dm_ref[...], wm_ref[...]) + _mm_nt(dp_ref[...], wp_ref[...])
        dxn, dgain = _rms_bwd(xv, r, gain_v, dh)
        dx1_ref[...] = dx2_ref[...] + dxn

        @pl.when(pl.program_id(0) == 0)
        def _():
            dgain_ref[...] = jnp.zeros_like(dgain_ref)

        dgain_ref[...] += dgain

    tok = lambda c: pl.BlockSpec((tm, c), lambda i: (i, 0))
    return pl.pallas_call(
        body, name="inproj_bwd", grid=(t // tm,),
        in_specs=[tok(d), _const((1, d)), tok(d), tok(M_COLS), tok(P_COLS), _resident(w_m.shape), _resident(w_p.shape)],
        out_specs=[tok(d), _const((1, d))],
        out_shape=[jax.ShapeDtypeStruct((t, d), F32), jax.ShapeDtypeStruct((1, d), F32)],
        compiler_params=_params(("arbitrary",)))(x1, gain, dx2, dm, dp, w_m, w_p)


def _adamw(w, g, m, v, name):
    rows, cols = w.shape
    tr = rows
    for cand in (512, 256, 128, 64, 32, 16, 8):
        if rows % cand == 0 and rows > cand:
            tr = cand
            break

    def body(w_ref, g_ref, m_ref, v_ref, delta_ref, nm_ref, nv_ref):
        gv = g_ref[...]
        nm = ADAM_B1 * m_ref[...] + (1.0 - ADAM_B1) * gv
        nv = ADAM_B2 * v_ref[...] + (1.0 - ADAM_B2) * (gv * gv)
        m_hat = nm * (1.0 / (1.0 - ADAM_B1 ** ADAM_STEP))
        v_hat = nv * (1.0 / (1.0 - ADAM_B2 ** ADAM_STEP))
        delta_ref[...] = -ADAM_LR * (m_hat / (jnp.sqrt(v_hat) + ADAM_EPS) + ADAM_WD * w_ref[...])
        nm_ref[...] = nm
        nv_ref[...] = nv

    spec = pl.BlockSpec((tr, cols), lambda i: (i, 0))
    shape = jax.ShapeDtypeStruct((rows, cols), F32)
    return pl.pallas_call(body, name=name, grid=(rows // tr,), in_specs=[spec] * 4, out_specs=[spec] * 3, out_shape=[shape] * 3,
                          compiler_params=_params(("arbitrary",)))(w, g, m, v)


def _place():
    x, y, c = lax.axis_index("x"), lax.axis_index("y"), lax.axis_index("c")
    other_chips = [(1 - x, y), (x, 1 - y), (1 - x, 1 - y)]
    return x, y, c, other_chips


def _gather_chips(shards):
    n = len(shards)

    def body(*refs):
        ins, outs = refs[:n], refs[n:2 * n]
        send_sems, recv_sems, local_sems = refs[2 * n:]
        x, y, c, other_chips = _place()
        me = 2 * x + y
        local = [pltpu.make_async_copy(ins[w], outs[w].at[me], local_sems.at[w]) for w in range(n)]
        for cp in local:
            cp.start()
        sends = []
        for w in range(n):
            for p, (px, py) in enumerate(other_chips):
                cp = pltpu.make_async_remote_copy(src_ref=ins[w], dst_ref=outs[w].at[me], send_sem=send_sems.at[3 * w + p],
                                                  recv_sem=recv_sems.at[3 * w + p], device_id=(px, py, c), device_id_type=MESH_ID)
                cp.start()
                sends.append(cp)
        for w in range(n):
            for p, (px, py) in enumerate(other_chips):
                pltpu.make_async_remote_copy(src_ref=ins[w], dst_ref=outs[w].at[2 * px + py], send_sem=send_sems.at[3 * w + p],
                                             recv_sem=recv_sems.at[3 * w + p], device_id=(px, py, c),
                                             device_id_type=MESH_ID).wait_recv()
        for cp in sends:
            cp.wait_send()
        for cp in local:
            cp.wait()

    return pl.pallas_call(
        body, name="gather_chips", in_specs=[ANY] * n, out_specs=[ANY] * n,
        out_shape=[jax.ShapeDtypeStruct((N_CHIPS,) + s.shape, s.dtype) for s in shards],
        scratch_shapes=[pltpu.SemaphoreType.DMA((3 * n,)), pltpu.SemaphoreType.DMA((3 * n,)), pltpu.SemaphoreType.DMA((n,))],
    )(*shards)


def _swap_halves(grads):
    n = len(grads)

    def body(*refs):
        ins, outs = refs[:n], refs[n:2 * n]
        send_sems, recv_sems = refs[2 * n:]
        x, y, c, _ = _place()
        copies = []
        for w in range(n):
            half = grads[w].shape[1] // 2
            start = pl.multiple_of((1 - c) * half, 8)
            cp = pltpu.make_async_remote_copy(src_ref=ins[w].at[:, pl.ds(start, half), :], dst_ref=outs[w],
                                              send_sem=send_sems.at[w], recv_sem=recv_sems.at[w], device_id=(x, y, 1 - c),
                                              device_id_type=MESH_ID)
            cp.start()
            copies.append(cp)
        for cp in copies:
            cp.wait()

    return pl.pallas_call(
        body, name="swap_halves", in_specs=[ANY] * n, out_specs=[ANY] * n,
        out_shape=[jax.ShapeDtypeStruct((g.shape[0], g.shape[1] // 2, g.shape[2]), F32) for g in grads],
        scratch_shapes=[pltpu.SemaphoreType.DMA((n,)), pltpu.SemaphoreType.DMA((n,))],
    )(*grads)


def _row_tile(rows):
    for cand in (256, 176, 128, 96, 64, 32, 16):
        if rows % cand == 0:
            return cand
    return rows


def _chip_partial(grad, other, c_arr, name):
    nblk, half, cols = other.shape
    tr = _row_tile(half)
    per_half = half // tr

    def body(c_ref, g_ref, o_ref, sum_ref, sum_bf_ref):
        s = g_ref[...] + o_ref[...]
        sum_ref[...] = s
        sum_bf_ref[...] = _bf(s)

    grid_spec = pltpu.PrefetchScalarGridSpec(
        num_scalar_prefetch=1, grid=(nblk, per_half),
        in_specs=[pl.BlockSpec((None, tr, cols), lambda b, i, c_ref: (b, c_ref[0] * per_half + i, 0)),
                  pl.BlockSpec((None, tr, cols), lambda b, i, c_ref: (b, i, 0))],
        out_specs=[pl.BlockSpec((None, tr, cols), lambda b, i, c_ref: (b, i, 0))] * 2)
    return pl.pallas_call(body, name=name, grid_spec=grid_spec,
                          out_shape=[jax.ShapeDtypeStruct(other.shape, F32), jax.ShapeDtypeStruct(other.shape, BF16)],
                          compiler_params=_params(("arbitrary", "arbitrary")))(c_arr, grad, other)


def _send_partials(partials):
    n = len(partials)

    def body(*refs):
        ins, outs = refs[:n], refs[n:2 * n]
        send_sems, recv_sems = refs[2 * n:]
        x, y, c, other_chips = _place()
        me = 2 * x + y
        sends = []
        for w in range(n):
            for p, (px, py) in enumerate(other_chips):
                cp = pltpu.make_async_remote_copy(src_ref=ins[w].at[2 * px + py], dst_ref=outs[w].at[me],
                                                  send_sem=send_sems.at[3 * w + p], recv_sem=recv_sems.at[3 * w + p],
                                                  device_id=(px, py, c), device_id_type=MESH_ID)
                cp.start()
                sends.append(cp)
        for w in range(n):
            for p, (px, py) in enumerate(other_chips):
                pltpu.make_async_remote_copy(src_ref=ins[w].at[me], dst_ref=outs[w].at[2 * px + py],
                                             send_sem=send_sems.at[3 * w + p], recv_sem=recv_sems.at[3 * w + p],
                                             device_id=(px, py, c), device_id_type=MESH_ID).wait_recv()
        for cp in sends:
            cp.wait_send()

    return pl.pallas_call(
        body, name="send_partials", in_specs=[ANY] * n, out_specs=[ANY] * n,
        out_shape=[jax.ShapeDtypeStruct(p.shape, BF16) for p in partials],
        scratch_shapes=[pltpu.SemaphoreType.DMA((3 * n,)), pltpu.SemaphoreType.DMA((3 * n,))],
    )(*partials)


def _chip_total(own, received, me_arr, name):
    nblk, half, cols = own.shape
    tr = _row_tile(half)

    def body(me_ref, own_ref, r1_ref, r2_ref, r3_ref, out_ref):
        out_ref[...] = own_ref[...] + ((r1_ref[...].astype(F32) + r2_ref[...].astype(F32)) + r3_ref[...].astype(F32))

    def slot(k):
        return pl.BlockSpec((None, tr, cols), lambda i, me_ref: ((me_ref[0] + k) % N_CHIPS, i, 0))

    grid_spec = pltpu.PrefetchScalarGridSpec(
        num_scalar_prefetch=1, grid=(half // tr,), in_specs=[slot(0), slot(1), slot(2), slot(3)],
        out_specs=pl.BlockSpec((tr, cols), lambda i, me_ref: (i, 0)))
    return pl.pallas_call(body, name=name, grid_spec=grid_spec, out_shape=jax.ShapeDtypeStruct((half, cols), F32),
                          compiler_params=_params(("arbitrary",)))(me_arr, own, received, received, received)


def _join_halves(halves):
    n = len(halves)

    def body(*refs):
        ins, outs = refs[:n], refs[n:2 * n]
        send_sems, recv_sems, local_sems = refs[2 * n:]
        x, y, c, _ = _place()
        copies = []
        for w in range(n):
            half = halves[w].shape[0]
            mine = outs[w].at[pl.ds(pl.multiple_of(c * half, 8), half), :]
            lc = pltpu.make_async_copy(ins[w], mine, local_sems.at[w])
            lc.start()
            cp = pltpu.make_async_remote_copy(src_ref=ins[w], dst_ref=mine, send_sem=send_sems.at[w], recv_sem=recv_sems.at[w],
                                              device_id=(x, y, 1 - c), device_id_type=MESH_ID)
            cp.start()
            copies += [lc, cp]
        for cp in copies:
            cp.wait()

    return pl.pallas_call(
        body, name="join_halves", in_specs=[ANY] * n, out_specs=[ANY] * n,
        out_shape=[jax.ShapeDtypeStruct((2 * h.shape[0], h.shape[1]), F32) for h in halves],
        scratch_shapes=[pltpu.SemaphoreType.DMA((n,)), pltpu.SemaphoreType.DMA((n,)), pltpu.SemaphoreType.DMA((n,))],
    )(*halves)


def _sum_devices(vec):
    rows, n = vec.shape

    def body(v_ref, out_ref, buf, send_sems, recv_sems):
        x, y, c, _ = _place()
        me = 4 * x + 2 * y + c
        buf[me] = v_ref[...]
        sends = []
        for k in range(1, N_DEV):
            peer = (1 - x if k & 4 else x, 1 - y if k & 2 else y, 1 - c if k & 1 else c)
            cp = pltpu.make_async_remote_copy(src_ref=v_ref, dst_ref=buf.at[me], send_sem=send_sems.at[k], recv_sem=recv_sems.at[k],
                                              device_id=peer, device_id_type=MESH_ID)
            cp.start()
            sends.append(cp)
        for cp in sends:
            cp.wait()
        total = buf[0]
        for dev in range(1, N_DEV):
            total = total + buf[dev]
        out_ref[...] = total

    vm = pl.BlockSpec(memory_space=pltpu.VMEM)
    return pl.pallas_call(
        body, name="sum_devices", in_specs=[vm], out_specs=vm, out_shape=jax.ShapeDtypeStruct((rows, n), F32),
        scratch_shapes=[pltpu.VMEM((N_DEV, rows, n), F32), pltpu.SemaphoreType.DMA((N_DEV,)), pltpu.SemaphoreType.DMA((N_DEV,))],
    )(vec)


def _rope_tables(positions):
    half = QK_ROPE // 2
    inv_freq = 1.0 / (ROPE_THETA ** (jnp.arange(half, dtype=F32) / half))
    ang = positions.astype(F32).reshape(-1, 1) * inv_freq
    cos, sin = jnp.cos(ang), jnp.sin(ang)
    t = ang.shape[0]
    ones, zeros = jnp.ones((t, QK_NOPE), F32), jnp.zeros((t, QK_NOPE), F32)
    pad = HEAD_PAD - QK_DIM
    cos_full = jnp.concatenate([ones, cos, cos, ones[:, :pad]], axis=1)
    sin_signed = jnp.concatenate([zeros, -sin, sin, zeros[:, :pad]], axis=1)
    return cos_full, sin_signed


def _local_step(x, cos, sin, target, p, w, seq):
    x1, gate1, up1, act1 = _ffn_fwd(x, p["ffn1_norm"], w["wg1"], w["wu1"], w["wd1"], None, "ffn1_fwd")
    h2b, big, lat, q, k, v = _inproj_fwd(x1, p["mix_norm"], w["w_m"], w["w_p"], p["q_a_norm"], p["kv_a_norm"], p["q_head_norm"],
                                         p["k_head_norm"], w["w_uq"], w["w_uk"], w["w_uv"], cos, sin)
    o, lse = _attn_fwd(q, k, v, seq)
    x2 = _mix_fwd(x1, o, big, p["gate_bias"], p["conv_w"], w["w_pa"], w["w_pc"], w["w_out"], seq)
    dx3, gate2, up2, act2, loss = _ffn_fwd(x2, p["ffn2_norm"], w["wg2"], w["wu2"], w["wd2"], target, "ffn2_fwd")

    dx2, dg_ffn2, hb2, dgate2, dup2, dyb2 = _ffn_bwd_x(x2, p["ffn2_norm"], dx3, gate2, up2, w["wg2"], w["wu2"], w["wd2"], "ffn2_bwd")
    do, dz, dm, dbias, dw_pa, dw_pc, dw_out = _mix_bwd(dx2, o, big, p["gate_bias"], p["conv_w"], w["w_pa"], w["w_pc"], w["w_out"], seq)
    dq, dk, dv = _attn_bwd(q, k, v, o, do, lse, seq)
    dp, dw_uq, dw_uk, dw_uv, dqa, dkva, dqh, dkh, dcw = _prep_bwd(
        lat, big, dz, dq, dk, dv, p["q_a_norm"], p["kv_a_norm"], p["q_head_norm"], p["k_head_norm"], w["w_uq"], w["w_uk"],
        w["w_uv"], cos, sin, p["conv_w"], seq)
    dx1, dg_mix = _inproj_bwd(x1, p["mix_norm"], dx2, dm, dp, w["w_m"], w["w_p"])
    grad_x, dg_ffn1, hb1, dgate1, dup1, dyb1 = _ffn_bwd_x(x, p["ffn1_norm"], dx1, gate1, up1, w["wg1"], w["wu1"], w["wd1"], "ffn1_bwd")

    big_grads = {
        "wg1": _tn_matmul(hb1, dgate1, "ffn1_dw_gate"), "wu1": _tn_matmul(hb1, dup1, "ffn1_dw_up"),
        "wd1": _tn_matmul(act1, dyb1, "ffn1_dw_down"),
        "wg2": _tn_matmul(hb2, dgate2, "ffn2_dw_gate"), "wu2": _tn_matmul(hb2, dup2, "ffn2_dw_up"),
        "wd2": _tn_matmul(act2, dyb2, "ffn2_dw_down"),
        "w_m": _tn_matmul(h2b, dm, "dw_in_m", split_n=2), "w_p": _tn_matmul(h2b, dp, "dw_in_p", split_n=2),
        "w_uq": dw_uq, "w_uk": dw_uk, "w_uv": dw_uv, "w_pa": dw_pa, "w_pc": dw_pc, "w_out": dw_out,
    }
    small_grads = {"ffn1_norm": dg_ffn1, "mix_norm": dg_mix, "gate_bias": dbias, "q_a_norm": dqa, "kv_a_norm": dkva,
                   "q_head_norm": dqh, "k_head_norm": dkh, "ffn2_norm": dg_ffn2, "conv_w": dcw}
    return loss, grad_x, small_grads, big_grads


def _kernel_layouts(full):
    d = D_MODEL
    w_in = full["w_in"]
    o_q, o_kv, o_kr, o_xc, o_gb, o_gc, o_gl = 0, Q_LORA, Q_LORA + KV_LORA, Q_LORA + KV_LORA + QK_ROPE, 0, 0, 0
    o_gb = o_xc + d
    o_gc = o_gb + d
    o_gl = o_gc + d
    k_rope_pad = jnp.pad(w_in[:, o_kr:o_xc], ((0, 0), (QK_NOPE, HEAD_PAD - QK_DIM)))
    w_m = jnp.concatenate([w_in[:, o_gb:o_gc], w_in[:, o_gl:]], axis=1)
    w_p = jnp.concatenate([w_in[:, o_xc:o_gb], w_in[:, o_gc:o_gl], w_in[:, o_q:o_kr], k_rope_pad], axis=1)
    w_uq = jnp.pad(full["w_uq"].reshape(Q_LORA, N_HEADS, QK_DIM), ((0, 0), (0, 0), (0, HEAD_PAD - QK_DIM))).transpose(1, 0, 2)
    w_uk = jnp.pad(full["w_uk"].reshape(KV_LORA, N_HEADS, QK_NOPE), ((0, 0), (0, 0), (0, HEAD_PAD - QK_NOPE))).transpose(1, 0, 2)
    w_uv = full["w_uv"].reshape(KV_LORA, N_HEADS, V_DIM).transpose(1, 0, 2)
    return {"w_m": w_m, "w_p": w_p, "w_uq": w_uq, "w_uk": w_uk, "w_uv": w_uv, "w_pa": full["w_proj_attn"],
            "w_pc": full["w_proj_conv"], "w_out": full["w_out"]}


def _global_layouts(g):
    d = D_MODEL
    dw = jnp.concatenate([g["w_m"], g["w_p"]], axis=1)
    o_xc, o_gc, o_lat = M_COLS, M_COLS + d, M_COLS + 2 * d
    o_kr = o_lat + Q_LORA + KV_LORA + QK_NOPE
    w_in = jnp.concatenate([dw[:, o_lat:o_lat + Q_LORA + KV_LORA], dw[:, o_kr:o_kr + QK_ROPE], dw[:, o_xc:o_gc], dw[:, :d],
                            dw[:, o_gc:o_lat], dw[:, d:M_COLS]], axis=1)
    w_uq = g["w_uq"][:, :, :QK_DIM].transpose(1, 0, 2).reshape(Q_LORA, N_HEADS * QK_DIM)
    w_uk = g["w_uk"][:, :, :QK_NOPE].transpose(1, 0, 2).reshape(KV_LORA, N_HEADS * QK_NOPE)
    w_uv = g["w_uv"].transpose(1, 0, 2).reshape(KV_LORA, N_HEADS * V_DIM)
    return {"w_in": w_in, "w_uq": w_uq, "w_uk": w_uk, "w_uv": w_uv, "w_proj_attn": g["w_pa"], "w_proj_conv": g["w_pc"],
            "w_out": g["w_out"]}


def _col_blocks(a):
    r, c = a.shape
    return a.reshape(r, N_CHIPS, c // N_CHIPS).transpose(1, 0, 2)


def _from_col_blocks(a):
    n, r, c = a.shape
    return a.transpose(1, 0, 2).reshape(r, n * c)


COL_SHARDED = ("w_in", "w_uq", "w_uk", "w_uv", "w_proj_attn")
ROW_SHARDED = ("w_proj_conv", "w_out")
SMALL = (("ffn1_norm", 1024), ("mix_norm", 1024), ("gate_bias", 2048), ("q_a_norm", 384), ("kv_a_norm", 256),
         ("q_head_norm", 128), ("k_head_norm", 128), ("ffn2_norm", 1024))
WEIGHT_ORDER = ("ffn1_norm", "ffn1_w_gate", "ffn1_w_up", "ffn1_w_down", "mix_norm", "w_in", "gate_bias", "q_a_norm", "w_uq",
                "kv_a_norm", "w_uk", "w_uv", "q_head_norm", "k_head_norm", "w_proj_attn", "conv_w", "w_proj_conv", "w_out",
                "ffn2_norm", "ffn2_w_gate", "ffn2_w_up", "ffn2_w_down")
MATRICES = ("ffn1_w_gate", "ffn1_w_up", "ffn1_w_down", "w_in", "w_uq", "w_uk", "w_uv", "w_proj_attn", "w_proj_conv", "w_out",
            "ffn2_w_gate", "ffn2_w_up", "ffn2_w_down")


def _pad_lanes(a, n):
    return jnp.pad(a.reshape(1, -1), ((0, 0), (0, n - a.size)))


def kernel(x, positions, ffn1_norm, ffn1_w_gate, ffn1_w_up, ffn1_w_down, mix_norm, w_in, gate_bias, q_a_norm, w_uq, kv_a_norm, w_uk, w_uv, q_head_norm, k_head_norm, w_proj_attn, conv_w, w_proj_conv, w_out, ffn2_norm, ffn2_w_gate, ffn2_w_up, ffn2_w_down, loss_target, m_ffn1_norm, m_ffn1_w_gate, m_ffn1_w_up, m_ffn1_w_down, m_mix_norm, m_w_in, m_gate_bias, m_q_a_norm, m_w_uq, m_kv_a_norm, m_w_uk, m_w_uv, m_q_head_norm, m_k_head_norm, m_w_proj_attn, m_conv_w, m_w_proj_conv, m_w_out, m_ffn2_norm, m_ffn2_w_gate, m_ffn2_w_up, m_ffn2_w_down, v_ffn1_norm, v_ffn1_w_gate, v_ffn1_w_up, v_ffn1_w_down, v_mix_norm, v_w_in, v_gate_bias, v_q_a_norm, v_w_uq, v_kv_a_norm, v_w_uk, v_w_uv, v_q_head_norm, v_k_head_norm, v_w_proj_attn, v_conv_w, v_w_proj_conv, v_w_out, v_ffn2_norm, v_ffn2_w_gate, v_ffn2_w_up, v_ffn2_w_down):
    args = dict(locals())
    weights = {n: args[n] for n in WEIGHT_ORDER}
    moments_m = {n: args["m_" + n] for n in WEIGHT_ORDER}
    moments_v = {n: args["v_" + n] for n in WEIGHT_ORDER}
    nb, seq, d = x.shape
    t = nb * seq
    chip = (2 * lax.axis_index("x") + lax.axis_index("y")).astype(jnp.int32)
    core = lax.axis_index("c").astype(jnp.int32)

    conv_shard = jnp.pad(conv_w, ((0, 8 - conv_w.shape[0]), (0, 0)))
    gathered = _gather_chips([_bf(weights[n]) for n in MATRICES] + [conv_shard])
    blocks = dict(zip(MATRICES, gathered[:-1]))
    conv_full = _from_col_blocks(gathered[-1])[:conv_w.shape[0]]
    full = {n: _from_col_blocks(blocks[n]) for n in COL_SHARDED}
    full.update({n: blocks[n].reshape(-1, blocks[n].shape[-1]) for n in ROW_SHARDED})
    w = _kernel_layouts(full)
    w.update({"wg1": blocks["ffn1_w_gate"], "wu1": blocks["ffn1_w_up"], "wd1": blocks["ffn1_w_down"],
              "wg2": blocks["ffn2_w_gate"], "wu2": blocks["ffn2_w_up"], "wd2": blocks["ffn2_w_down"]})
    p = {n: _pad_lanes(weights[n], size) for n, size in SMALL}
    p["conv_w"] = conv_full

    cos, sin = _rope_tables(positions)
    loss, grad_x, small_grads, g = _local_step(x.reshape(t, d), cos, sin, loss_target.reshape(t, d), p, w, seq)

    packed = jnp.concatenate([small_grads[n] for n, _ in SMALL] + [small_grads["conv_w"].reshape(1, -1), loss], axis=1)
    total = _sum_devices(jnp.pad(packed, ((0, 7), (0, 0))))[0:1]
    n_small = sum(size for _, size in SMALL)
    conv_cols = conv_w.shape[1]
    conv_total = total[:, n_small:n_small + 3 * d].reshape(3, d)
    grads = {"conv_w": lax.dynamic_slice_in_dim(conv_total, chip * conv_cols, conv_cols, axis=1)}
    loss_total = total[0, n_small + 3 * d]

    gg = _global_layouts(g)
    block_grads = {"ffn1_w_gate": g["wg1"], "ffn1_w_up": g["wu1"], "ffn1_w_down": g["wd1"],
                   "ffn2_w_gate": g["wg2"], "ffn2_w_up": g["wu2"], "ffn2_w_down": g["wd2"]}
    block_grads.update({n: _col_blocks(gg[n]) for n in COL_SHARDED})
    block_grads.update({n: gg[n].reshape(N_CHIPS, -1, gg[n].shape[-1]) for n in ROW_SHARDED})
    local = [block_grads[n] for n in MATRICES]
    from_sibling = _swap_halves(local)
    core_arr, chip_arr = core.reshape(1), chip.reshape(1)
    partial = [_chip_partial(a, b, core_arr, "chip_partial_" + n) for n, a, b in zip(MATRICES, local, from_sibling)]
    received = _send_partials([pb for _, pb in partial])
    halves = [_chip_total(pf, r, chip_arr, "chip_total_" + n) for n, (pf, _), r in zip(MATRICES, partial, received)]
    grads.update(dict(zip(MATRICES, _join_halves(halves))))

    delta, new_m, new_v = {}, {}, {}
    for n in MATRICES + ("conv_w",):
        delta[n], new_m[n], new_v[n] = _adamw(weights[n], grads[n], moments_m[n], moments_v[n], "adamw_" + n)
    pack = lambda src: jnp.concatenate([_pad_lanes(src[n], size) for n, size in SMALL], axis=1)
    sd, sm, sv = _adamw(pack(weights), total[:, :n_small], pack(moments_m), pack(moments_v), "adamw_small")
    off = 0
    for n, size in SMALL:
        real = weights[n].size
        grads[n] = total[0, off:off + real]
        delta[n], new_m[n], new_v[n] = sd[0, off:off + real], sm[0, off:off + real], sv[0, off:off + real]
        off += size

    return (loss_total, grad_x.reshape(nb, seq, d), *[grads[n] for n in WEIGHT_ORDER], *[delta[n] for n in WEIGHT_ORDER],
            *[new_m[n] for n in WEIGHT_ORDER], *[new_v[n] for n in WEIGHT_ORDER])
```

```python
import functools

import jax
import jax.numpy as jnp
from jax import lax
from jax.experimental import pallas as pl
from jax.experimental.pallas import tpu as pltpu

F32 = jnp.float32
BF16 = jnp.bfloat16

D_MODEL = 1024
N_HEADS = 8
QK_NOPE = 64
QK_ROPE = 32
QK_DIM = QK_NOPE + QK_ROPE
V_DIM = 64
HEAD_PAD = 128
Q_LORA = 384
KV_LORA = 256
ROPE_THETA = 10000.0
NORM_EPS = 1e-6
ATTN_SCALE = QK_DIM ** -0.5
MASK_VALUE = -1e30
N_CHIPS = 4
N_DEV = 8

ADAM_LR = 0.001
ADAM_B1 = 0.9
ADAM_B2 = 0.999
ADAM_EPS = 1e-08
ADAM_WD = 0.01
ADAM_STEP = 10

TOKEN_TILE = 256
ATTN_TILE = 512
TN_TILE = 512
VMEM_LIMIT = 56 * 1024 * 1024

M_COLS = 3 * D_MODEL
P_COLS = 2 * D_MODEL + Q_LORA + KV_LORA + HEAD_PAD
BIG_COLS = 5 * D_MODEL
LAT_COLS = Q_LORA + KV_LORA + HEAD_PAD

MESH_ID = pl.DeviceIdType.MESH
ANY = pl.BlockSpec(memory_space=pl.ANY)


def _params(semantics=None):
    return pltpu.CompilerParams(dimension_semantics=semantics, vmem_limit_bytes=VMEM_LIMIT)


def _resident(shape):
    nd = len(shape)
    return pl.BlockSpec(shape, lambda *_: (0,) * nd, pipeline_mode=pl.Buffered(1))


def _const(shape):
    nd = len(shape)
    return pl.BlockSpec(shape, lambda *_: (0,) * nd)


def _mm(a, b):
    return jnp.dot(a, b, preferred_element_type=F32)


def _mm_nt(a, b):
    return lax.dot_general(a, b, (((1,), (1,)), ((), ())), preferred_element_type=F32)


def _mm_tn(a, b):
    return lax.dot_general(a, b, (((0,), (0,)), ((), ())), preferred_element_type=F32)


def _bf(a):
    return a.astype(BF16)


def _sigmoid(a):
    return 1.0 / (1.0 + jnp.exp(-a))


def _rms(x, gain, n=None):
    n = x.shape[-1] if n is None else n
    r = lax.rsqrt(jnp.sum(x * x, axis=-1, keepdims=True) * (1.0 / n) + NORM_EPS)
    return (x * r) * gain, r


def _rms_bwd(x, r, gain, dh, n=None):
    n = x.shape[-1] if n is None else n
    u = dh * gain
    dx = r * u - x * ((r * r * r) * (jnp.sum(u * x, axis=-1, keepdims=True) * (1.0 / n)))
    dgain = jnp.sum(dh * (x * r), axis=0, keepdims=True)
    return dx, dgain


def _rope_swap(t):
    lane = lax.broadcasted_iota(jnp.int32, t.shape, 1)
    lo = (lane >= QK_NOPE) & (lane < QK_NOPE + QK_ROPE // 2)
    hi = (lane >= QK_NOPE + QK_ROPE // 2) & (lane < QK_DIM)
    up = pltpu.roll(t, HEAD_PAD - QK_ROPE // 2, 1)
    down = pltpu.roll(t, QK_ROPE // 2, 1)
    return jnp.where(lo, up, jnp.where(hi, down, 0.0))


def _rope(t, cos, sin):
    return t * cos + _rope_swap(t) * sin


def _rope_bwd(dt, cos, sin):
    return dt * cos + _rope_swap(dt * sin)


def _shift_down(u, prev8, k):
    s = pltpu.roll(u, k, 0)
    p = pltpu.roll(prev8, k, 0)
    row = lax.broadcasted_iota(jnp.int32, prev8.shape, 0)
    top = jnp.where(row < k, p, s[:8])
    return jnp.concatenate([top, s[8:]], axis=0)


def _shift_up(d, next8, k):
    tm = d.shape[0]
    s = pltpu.roll(d, tm - k, 0)
    n = pltpu.roll(next8, 8 - k, 0)
    row = lax.broadcasted_iota(jnp.int32, next8.shape, 0)
    bot = jnp.where(row >= 8 - k, n, s[tm - 8:])
    return jnp.concatenate([s[:tm - 8], bot], axis=0)


def _ffn_fwd(x, gain, wg, wu, wd, target, name):
    t, d = x.shape
    nb, _, f = wg.shape
    tm = TOKEN_TILE
    with_loss = target is not None

    def body(*refs):
        if with_loss:
            x_ref, g_ref, wg_ref, wu_ref, wd_ref, t_ref, out_ref, gate_ref, up_ref, act_ref, loss_ref = refs
        else:
            x_ref, g_ref, wg_ref, wu_ref, wd_ref, out_ref, gate_ref, up_ref, act_ref = refs
        xv = x_ref[...]
        h, _ = _rms(xv, g_ref[...])
        hb = _bf(h)
        y = jnp.zeros((tm, d), F32)
        for j in range(nb):
            gate = _mm(hb, wg_ref[j])
            up = _mm(hb, wu_ref[j])
            act = _bf((gate * _sigmoid(gate)) * up)
            y = y + _mm(act, wd_ref[j])
            gate_ref[j] = _bf(gate)
            up_ref[j] = _bf(up)
            act_ref[j] = act
        out = xv + 0.5 * y
        if with_loss:
            err = out - t_ref[...]
            out_ref[...] = err * (1.0 / d)

            @pl.when(pl.program_id(0) == 0)
            def _():
                loss_ref[...] = jnp.zeros_like(loss_ref)

            part = jnp.sum(jnp.sum(err * err, axis=1, keepdims=True), axis=0, keepdims=True)
            loss_ref[...] += jnp.broadcast_to(part * (0.5 / d), loss_ref.shape)
        else:
            out_ref[...] = out

    tok = pl.BlockSpec((tm, d), lambda i: (i, 0))
    blk = pl.BlockSpec((nb, tm, f), lambda i: (0, i, 0))
    in_specs = [tok, _const((1, d)), _resident(wg.shape), _resident(wu.shape), _resident(wd.shape)]
    args = [x, gain, wg, wu, wd]
    out_shape = [jax.ShapeDtypeStruct((t, d), F32)] + [jax.ShapeDtypeStruct((nb, t, f), BF16)] * 3
    out_specs = [tok, blk, blk, blk]
    if with_loss:
        in_specs.append(tok)
        args.append(target)
        out_shape.append(jax.ShapeDtypeStruct((1, 128), F32))
        out_specs.append(_const((1, 128)))
    return pl.pallas_call(body, name=name, grid=(t // tm,), in_specs=in_specs, out_specs=out_specs, out_shape=out_shape,
                          compiler_params=_params(("arbitrary",)))(*args)


def _ffn_bwd_x(x, gain, dout, gate, up, wg, wu, wd, name):
    t, d = x.shape
    nb, _, f = wg.shape
    tm = TOKEN_TILE

    def body(x_ref, g_ref, dout_ref, gate_ref, up_ref, wg_ref, wu_ref, wd_ref,
             dx_ref, dgain_ref, hb_ref, dgate_ref, dup_ref, dyb_ref):
        xv = x_ref[...]
        gain_v = g_ref[...]
        h, r = _rms(xv, gain_v)
        hb_ref[...] = _bf(h)
        dout_v = dout_ref[...]
        dyb = _bf(0.5 * dout_v)
        dyb_ref[...] = dyb
        dh = jnp.zeros((tm, d), F32)
        for j in range(nb):
            gt = gate_ref[j].astype(F32)
            uv = up_ref[j].astype(F32)
            s = _sigmoid(gt)
            dact = _mm_nt(dyb, wd_ref[j])
            dup = _bf(dact * (gt * s))
            dgate = _bf((dact * uv) * (s * (1.0 + gt * (1.0 - s))))
            dh = dh + _mm_nt(dgate, wg_ref[j]) + _mm_nt(dup, wu_ref[j])
            dgate_ref[j] = dgate
            dup_ref[j] = dup
        dxn, dgain = _rms_bwd(xv, r, gain_v, dh)
        dx_ref[...] = dout_v + dxn

        @pl.when(pl.program_id(0) == 0)
        def _():
            dgain_ref[...] = jnp.zeros_like(dgain_ref)

        dgain_ref[...] += dgain

    tok = pl.BlockSpec((tm, d), lambda i: (i, 0))
    blk = pl.BlockSpec((nb, tm, f), lambda i: (0, i, 0))
    return pl.pallas_call(
        body, name=name, grid=(t // tm,),
        in_specs=[tok, _const((1, d)), tok, blk, blk, _resident(wg.shape), _resident(wu.shape), _resident(wd.shape)],
        out_specs=[tok, _const((1, d)), tok, blk, blk, tok],
        out_shape=[jax.ShapeDtypeStruct((t, d), F32), jax.ShapeDtypeStruct((1, d), F32), jax.ShapeDtypeStruct((t, d), BF16),
                   jax.ShapeDtypeStruct((nb, t, f), BF16), jax.ShapeDtypeStruct((nb, t, f), BF16),
                   jax.ShapeDtypeStruct((t, d), BF16)],
        compiler_params=_params(("arbitrary",)))(x, gain, dout, gate, up, wg, wu, wd)


def _tn_matmul(a, b, name, split_n=1):
    t = a.shape[-2]
    k = a.shape[-1]
    n = b.shape[-1]
    tt = min(TN_TILE, t)
    nt = t // tt

    def body(a_ref, b_ref, o_ref):
        @pl.when(pl.program_id(1) == 0)
        def _():
            o_ref[...] = jnp.zeros_like(o_ref)

        o_ref[...] += _mm_tn(a_ref[...], b_ref[...])

    if split_n > 1:
        assert a.ndim == 2 and b.ndim == 2 and n % (split_n * 128) == 0
        tn = n // split_n
        g = split_n
        a_spec = pl.BlockSpec((tt, k), lambda gi, ti: (ti, 0))
        b_spec = pl.BlockSpec((tt, tn), lambda gi, ti: (ti, gi))
        o_spec = pl.BlockSpec((k, tn), lambda gi, ti: (0, gi))
        out_shape = jax.ShapeDtypeStruct((k, n), F32)
    else:
        g = a.shape[0] if a.ndim == 3 else b.shape[0]
        a_spec = (pl.BlockSpec((None, tt, k), lambda gi, ti: (gi, ti, 0)) if a.ndim == 3
                  else pl.BlockSpec((tt, k), lambda gi, ti: (ti, 0)))
        b_spec = (pl.BlockSpec((None, tt, n), lambda gi, ti: (gi, ti, 0)) if b.ndim == 3
                  else pl.BlockSpec((tt, n), lambda gi, ti: (ti, 0)))
        o_spec = pl.BlockSpec((None, k, n), lambda gi, ti: (gi, 0, 0))
        out_shape = jax.ShapeDtypeStruct((g, k, n), F32)
    return pl.pallas_call(body, name=name, grid=(g, nt), in_specs=[a_spec, b_spec], out_specs=o_spec, out_shape=out_shape,
                          compiler_params=_params(("arbitrary", "arbitrary")))(a, b)


def _inproj_fwd(x1, gain, w_m, w_p, qa_gain, kva_gain, qh_gain, kh_gain, w_uq, w_uk, w_uv, w_uvt, cos, sin):
    t, d = x1.shape
    tm = TOKEN_TILE

    def body(x_ref, g_ref, wm_ref, wp_ref, qa_ref, kva_ref, qh_ref, kh_ref, wuq_ref, wuk_ref, wuv_ref, wuvt_ref, cos_ref, sin_ref,
             hb_ref, big_ref, lat_ref, q_ref, k_ref, v_ref, vt_ref):
        h, _ = _rms(x_ref[...], g_ref[...])
        hb = _bf(h)
        hb_ref[...] = hb
        big_ref[:, :M_COLS] = _mm(hb, wm_ref[...])
        pp = _mm(hb, wp_ref[...])
        big_ref[:, M_COLS:] = pp[:, :2 * D_MODEL]
        lat = pp[:, 2 * D_MODEL:]
        lat_ref[...] = lat
        cq, _ = _rms(lat[:, :Q_LORA], qa_ref[...])
        ckv, _ = _rms(lat[:, Q_LORA:Q_LORA + KV_LORA], kva_ref[...])
        k_rope = lat[:, Q_LORA + KV_LORA:]
        cqb = _bf(cq)
        ckvb = _bf(ckv)
        cos_v = cos_ref[...]
        sin_v = sin_ref[...]
        for hd in range(N_HEADS):
            qn, _ = _rms(_mm(cqb, wuq_ref[hd]), qh_ref[...], QK_DIM)
            q_ref[hd] = _bf(_rope(qn, cos_v, sin_v))
            kn, _ = _rms(_mm(ckvb, wuk_ref[hd]) + k_rope, kh_ref[...], QK_DIM)
            k_ref[hd] = _bf(_rope(kn, cos_v, sin_v))
            v_ref[hd] = _bf(_mm(ckvb, wuv_ref[hd]))
            vt_ref[hd] = _bf(_mm_nt(wuvt_ref[hd], ckvb))

    tok = lambda c: pl.BlockSpec((tm, c), lambda i: (i, 0))
    head = lambda c: pl.BlockSpec((N_HEADS, tm, c), lambda i: (0, i, 0))
    return pl.pallas_call(
        body, name="inproj_fwd", grid=(t // tm,),
        in_specs=[tok(d), _const((1, d)), _resident(w_m.shape), _resident(w_p.shape), _const((1, Q_LORA)), _const((1, KV_LORA)),
                  _const((1, HEAD_PAD)), _const((1, HEAD_PAD)), _resident(w_uq.shape), _resident(w_uk.shape),
                  _resident(w_uv.shape), _resident(w_uvt.shape), tok(HEAD_PAD), tok(HEAD_PAD)],
        out_specs=[tok(d), tok(BIG_COLS), tok(LAT_COLS), head(HEAD_PAD), head(HEAD_PAD), head(V_DIM),
                   pl.BlockSpec((N_HEADS, V_DIM, tm), lambda i: (0, 0, i))],
        out_shape=[jax.ShapeDtypeStruct((t, d), BF16), jax.ShapeDtypeStruct((t, BIG_COLS), F32),
                   jax.ShapeDtypeStruct((t, LAT_COLS), F32), jax.ShapeDtypeStruct((N_HEADS, t, HEAD_PAD), BF16),
                   jax.ShapeDtypeStruct((N_HEADS, t, HEAD_PAD), BF16), jax.ShapeDtypeStruct((N_HEADS, t, V_DIM), BF16),
                   jax.ShapeDtypeStruct((N_HEADS, V_DIM, t), BF16)],
        compiler_params=_params(("arbitrary",)))(x1, gain, w_m, w_p, qa_gain, kva_gain, qh_gain, kh_gain, w_uq, w_uk, w_uv, w_uvt,
                                                  cos, sin)


EXP2_SCALE = ATTN_SCALE * 1.4426950408889634


def _diagonal_keep(tk, tq):
    return lax.broadcasted_iota(jnp.int32, (tk, tq), 0) <= lax.broadcasted_iota(jnp.int32, (tk, tq), 1)


def _attn_fwd(q, k, vt, seq):
    _, t, _ = q.shape
    nseq = t // seq
    tq = tk = ATTN_TILE
    nq = seq // tq

    def body(q_ref, k_ref, vt_ref, o_ref, lse_ref):
        i = pl.program_id(1)
        qs = [q_ref[h] for h in range(N_HEADS)]
        keep = _diagonal_keep(tk, tq)

        def tile(h, state, k0, diagonal):
            m, l, acc = state
            st = _mm_nt(k_ref[h, pl.ds(k0, tk), :], qs[h])
            if diagonal:
                st = jnp.where(keep, st, MASK_VALUE)
            m_new = jnp.maximum(m, jnp.max(st, axis=0, keepdims=True))
            pt = jnp.exp2((st - m_new) * EXP2_SCALE)
            alpha = jnp.exp2((m - m_new) * EXP2_SCALE)
            l_new = alpha * l + jnp.sum(pt, axis=0, keepdims=True)
            return m_new, l_new, alpha * acc + _mm(vt_ref[h, :, pl.ds(k0, tk)], _bf(pt))

        def step(j, states):
            k0 = pl.multiple_of(j * tk, tk)
            return tuple(tile(h, states[h], k0, False) for h in range(N_HEADS))

        init = tuple((jnp.full((1, tq), MASK_VALUE, F32), jnp.zeros((1, tq), F32), jnp.zeros((V_DIM, tq), F32))
                     for _ in range(N_HEADS))
        states = lax.fori_loop(0, i, step, init)
        k0 = pl.multiple_of(i * tk, tk)
        outs = []
        for h in range(N_HEADS):
            m, l, acc = tile(h, states[h], k0, True)
            outs.append((acc / l).T)
            lse_ref[h] = m * EXP2_SCALE + jnp.log2(l)
        o_ref[...] = _bf(jnp.concatenate(outs, axis=-1))

    return pl.pallas_call(
        body, name="attn_fwd", grid=(nseq, nq),
        in_specs=[pl.BlockSpec((N_HEADS, tq, HEAD_PAD), lambda b, i: (0, b * nq + i, 0)),
                  pl.BlockSpec((N_HEADS, seq, HEAD_PAD), lambda b, i: (0, b, 0)),
                  pl.BlockSpec((N_HEADS, V_DIM, seq), lambda b, i: (0, 0, b))],
        out_specs=[pl.BlockSpec((tq, N_HEADS * V_DIM), lambda b, i: (b * nq + i, 0)),
                   pl.BlockSpec((N_HEADS, 1, tq), lambda b, i: (0, 0, b * nq + i))],
        out_shape=[jax.ShapeDtypeStruct((t, N_HEADS * V_DIM), BF16), jax.ShapeDtypeStruct((N_HEADS, 1, t), F32)],
        compiler_params=_params(("arbitrary", "arbitrary")))(q, k, vt)


ATTN_BWD_HEADS = 4


def _attn_bwd(q, k, v, do, lse, delta, seq):
    _, t, _ = q.shape
    nseq = t // seq
    tq = tk = ATTN_TILE
    n = seq // tq
    hb = ATTN_BWD_HEADS

    def body(q_ref, k_ref, v_ref, do_ref, lse_ref, delta_ref, dq_ref, dk_ref, dv_ref):
        dq_ref[...] = jnp.zeros_like(dq_ref)
        dk_ref[...] = jnp.zeros_like(dk_ref)
        dv_ref[...] = jnp.zeros_like(dv_ref)
        keep = _diagonal_keep(tk, tq)

        def tile(h, k0, q0, diagonal):
            kj = k_ref[h, pl.ds(k0, tk), :]
            qi = q_ref[h, pl.ds(q0, tq), :]
            doi = _bf(do_ref[pl.ds(q0, tq), h * V_DIM:(h + 1) * V_DIM])
            st = _mm_nt(kj, qi)
            if diagonal:
                st = jnp.where(keep, st, MASK_VALUE)
            pt = jnp.exp2(st * EXP2_SCALE - lse_ref[h, :, pl.ds(q0, tq)])
            dv_ref[h, pl.ds(k0, tk), :] += _mm(_bf(pt), doi)
            dpt = _mm_nt(v_ref[h, pl.ds(k0, tk), :], doi)
            dst = _bf((pt * (dpt - delta_ref[pl.ds(h, 1), pl.ds(q0, tq)])) * ATTN_SCALE)
            dk_ref[h, pl.ds(k0, tk), :] += _mm(dst, qi)
            dq_ref[h, pl.ds(q0, tq), :] += _mm_tn(dst, kj)

        def kv_step(j, _):
            k0 = pl.multiple_of(j * tk, tk)
            for h in range(hb):
                tile(h, k0, k0, True)

            def q_step(i, _):
                q0 = pl.multiple_of(i * tq, tq)
                for h in range(hb):
                    tile(h, k0, q0, False)
                return 0

            lax.fori_loop(j + 1, n, q_step, 0)
            return 0

        lax.fori_loop(0, n, kv_step, 0)

    hspec = lambda c: pl.BlockSpec((hb, seq, c), lambda b, g: (g, b, 0))
    return pl.pallas_call(
        body, name="attn_bwd", grid=(nseq, N_HEADS // hb),
        in_specs=[hspec(HEAD_PAD), hspec(HEAD_PAD), hspec(V_DIM), pl.BlockSpec((seq, hb * V_DIM), lambda b, g: (b, g)),
                  pl.BlockSpec((hb, 1, seq), lambda b, g: (g, 0, b)), pl.BlockSpec((None, hb, seq), lambda b, g: (g, 0, b))],
        out_specs=[hspec(HEAD_PAD), hspec(HEAD_PAD), hspec(V_DIM)],
        out_shape=[jax.ShapeDtypeStruct((N_HEADS, t, HEAD_PAD), F32), jax.ShapeDtypeStruct((N_HEADS, t, HEAD_PAD), F32),
                   jax.ShapeDtypeStruct((N_HEADS, t, V_DIM), F32)],
        compiler_params=_params(("arbitrary", "arbitrary")))(q, k, v, do, lse, delta)


def _mixer_values(o_ref, gb_ref, gla_ref, glb_ref, xc_ref, gc_ref, xcp_ref, gcp_ref, bias_ref, cw_ref, wpa_ref, wpc_ref,
                  first_of_seq):
    gb = gb_ref[...]
    u = gc_ref[...] * xc_ref[...]
    u_prev = jnp.where(first_of_seq, 0.0, gcp_ref[...] * xcp_ref[...])
    cw = cw_ref[...]
    z = cw[2:3] * u + cw[1:2] * _shift_down(u, u_prev, 1) + cw[0:1] * _shift_down(u, u_prev, 2)
    gbz = _bf(gb * z)
    y_b = _mm(gbz, wpc_ref[...])
    y_a = _mm(o_ref[...], wpa_ref[...])
    bias = bias_ref[...]
    gate_a = _sigmoid(gla_ref[...] + bias[:, :D_MODEL])
    gate_b = _sigmoid(glb_ref[...] + bias[:, D_MODEL:])
    merged = _bf(gate_a * y_a + gate_b * y_b)
    return gb, u, z, gbz, y_a, y_b, gate_a, gate_b, merged


def _mixer_specs(tm, seq):
    d = D_MODEL
    tok = pl.BlockSpec((tm, d), lambda i: (i, 0))
    col = lambda c: pl.BlockSpec((tm, d), lambda i: (i, c))
    prev = lambda c: pl.BlockSpec((8, d), lambda i: (jnp.maximum(i * (tm // 8) - 1, 0), c))
    o_spec = pl.BlockSpec((tm, N_HEADS * V_DIM), lambda i: (i, 0))
    fwd_specs = [o_spec, col(0), col(1), col(2), col(3), col(4), prev(3), prev(4), _const((1, 2 * d)), _const((3, d)),
                 _resident((N_HEADS * V_DIM, d)), _resident((d, d)), _resident((d, d))]
    return tok, fwd_specs


def _mix_fwd(x1, o, big, gate_bias, conv_w, w_pa, w_pc, w_out, seq):
    t, d = x1.shape
    tm = TOKEN_TILE
    tiles_per_seq = seq // tm

    def body(x_ref, o_ref, gb_ref, gla_ref, glb_ref, xc_ref, gc_ref, xcp_ref, gcp_ref, bias_ref, cw_ref, wpa_ref, wpc_ref,
             wout_ref, x2_ref):
        first = pl.program_id(0) % tiles_per_seq == 0
        merged = _mixer_values(o_ref, gb_ref, gla_ref, glb_ref, xc_ref, gc_ref, xcp_ref, gcp_ref, bias_ref, cw_ref, wpa_ref,
                               wpc_ref, first)[-1]
        x2_ref[...] = x_ref[...] + _mm(merged, wout_ref[...])

    tok, fwd_specs = _mixer_specs(tm, seq)
    return pl.pallas_call(
        body, name="mix_fwd", grid=(t // tm,), in_specs=[tok] + fwd_specs, out_specs=tok,
        out_shape=jax.ShapeDtypeStruct((t, d), F32),
        compiler_params=_params(("arbitrary",)))(x1, o, big, big, big, big, big, big, big, gate_bias, conv_w, w_pa, w_pc, w_out)


def _mix_bwd(dx2, o, big, gate_bias, conv_w, w_pa, w_pc, w_out, seq):
    t, d = dx2.shape
    tm = TOKEN_TILE
    tiles_per_seq = seq // tm
    hv = N_HEADS * V_DIM

    def body(dx_ref, o_ref, gb_ref, gla_ref, glb_ref, xc_ref, gc_ref, xcp_ref, gcp_ref, bias_ref, cw_ref, wpa_ref, wpc_ref,
             wout_ref, do_ref, delta_ref, dz_ref, dm_ref, dbias_ref, dwpa_ref, dwpc_ref, dwout_ref):
        first = pl.program_id(0) % tiles_per_seq == 0
        gb, _, z, gbz, y_a, y_b, gate_a, gate_b, merged = _mixer_values(
            o_ref, gb_ref, gla_ref, glb_ref, xc_ref, gc_ref, xcp_ref, gcp_ref, bias_ref, cw_ref, wpa_ref, wpc_ref, first)

        @pl.when(pl.program_id(0) == 0)
        def _():
            dbias_ref[...] = jnp.zeros_like(dbias_ref)
            dwpa_ref[...] = jnp.zeros_like(dwpa_ref)
            dwpc_ref[...] = jnp.zeros_like(dwpc_ref)
            dwout_ref[...] = jnp.zeros_like(dwout_ref)

        dxb = _bf(dx_ref[...])
        dmerged = _mm_nt(dxb, wout_ref[...])
        dwout_ref[...] += _mm_tn(merged, dxb)
        dla = (dmerged * y_a) * (gate_a * (1.0 - gate_a))
        dlb = (dmerged * y_b) * (gate_b * (1.0 - gate_b))
        dbias_ref[:, :d] += jnp.sum(dla, axis=0, keepdims=True)
        dbias_ref[:, d:] += jnp.sum(dlb, axis=0, keepdims=True)
        dya = _bf(dmerged * gate_a)
        dyb = _bf(dmerged * gate_b)
        do_v = _mm_nt(dya, wpa_ref[...])
        do_ref[...] = do_v
        head = lax.broadcasted_iota(jnp.int32, (N_HEADS, hv), 0) * V_DIM
        col = lax.broadcasted_iota(jnp.int32, (N_HEADS, hv), 1)
        in_head = ((col >= head) & (col < head + V_DIM)).astype(F32)
        delta_ref[...] = lax.dot_general(in_head, do_v * o_ref[...].astype(F32), (((1,), (1,)), ((), ())),
                                         precision=lax.Precision.HIGHEST, preferred_element_type=F32)
        dwpa_ref[...] += _mm_tn(o_ref[...], dya)
        dgz = _mm_nt(dyb, wpc_ref[...])
        dwpc_ref[...] += _mm_tn(gbz, dyb)
        dz_ref[...] = dgz * gb
        dm_ref[:, :d] = _bf(dgz * z)
        dm_ref[:, d:2 * d] = _bf(dla)
        dm_ref[:, 2 * d:] = _bf(dlb)

    tok, fwd_specs = _mixer_specs(tm, seq)
    return pl.pallas_call(
        body, name="mix_bwd", grid=(t // tm,), in_specs=[tok] + fwd_specs,
        out_specs=[pl.BlockSpec((tm, hv), lambda i: (i, 0)), pl.BlockSpec((N_HEADS, tm), lambda i: (0, i)), tok,
                   pl.BlockSpec((tm, M_COLS), lambda i: (i, 0)), _const((1, 2 * d)), _const((hv, d)), _const((d, d)), _const((d, d))],
        out_shape=[jax.ShapeDtypeStruct((t, hv), F32), jax.ShapeDtypeStruct((N_HEADS, t), F32), jax.ShapeDtypeStruct((t, d), F32),
                   jax.ShapeDtypeStruct((t, M_COLS), BF16), jax.ShapeDtypeStruct((1, 2 * d), F32), jax.ShapeDtypeStruct((hv, d), F32),
                   jax.ShapeDtypeStruct((d, d), F32), jax.ShapeDtypeStruct((d, d), F32)],
        compiler_params=_params(("arbitrary",)))(dx2, o, big, big, big, big, big, big, big, gate_bias, conv_w, w_pa, w_pc, w_out)


def _prep_bwd(lat, big, dz, dq, dk, dv, qa_gain, kva_gain, qh_gain, kh_gain, w_uq, w_uk, w_uv, cos, sin, conv_w, seq):
    t = lat.shape[0]
    d = D_MODEL
    tm = TOKEN_TILE
    tiles_per_seq = seq // tm
    last_blk = t // 8 - 1

    def body(lat_ref, xc_ref, gc_ref, dz_ref, dzn_ref, dq_ref, dk_ref, dv_ref, qa_ref, kva_ref, qh_ref, kh_ref, wuq_ref, wuk_ref,
             wuv_ref, cos_ref, sin_ref, cw_ref,
             dp_ref, dwuq_ref, dwuk_ref, dwuv_ref, dqa_ref, dkva_ref, dqh_ref, dkh_ref, dcw_ref):
        pid = pl.program_id(0)

        @pl.when(pid == 0)
        def _():
            for r in (dwuq_ref, dwuk_ref, dwuv_ref, dqa_ref, dkva_ref, dqh_ref, dkh_ref, dcw_ref):
                r[...] = jnp.zeros_like(r)

        last = pid % tiles_per_seq == tiles_per_seq - 1
        dzv = dz_ref[...]
        dz_next = jnp.where(last, 0.0, dzn_ref[...])
        dz1 = _shift_up(dzv, dz_next, 1)
        dz2 = _shift_up(dzv, dz_next, 2)
        cw = cw_ref[...]
        xc = xc_ref[...]
        gc = gc_ref[...]
        u = gc * xc
        du = cw[2:3] * dzv + cw[1:2] * dz1 + cw[0:1] * dz2
        dp_ref[:, :d] = _bf(du * gc)
        dp_ref[:, d:2 * d] = _bf(du * xc)
        dcw_ref[0:1, :] += jnp.sum(dz2 * u, axis=0, keepdims=True)
        dcw_ref[1:2, :] += jnp.sum(dz1 * u, axis=0, keepdims=True)
        dcw_ref[2:3, :] += jnp.sum(dzv * u, axis=0, keepdims=True)

        lat_v = lat_ref[...]
        q_lat = lat_v[:, :Q_LORA]
        kv_lat = lat_v[:, Q_LORA:Q_LORA + KV_LORA]
        k_rope = lat_v[:, Q_LORA + KV_LORA:]
        qa_gain_v = qa_ref[...]
        kva_gain_v = kva_ref[...]
        qh_gain_v = qh_ref[...]
        kh_gain_v = kh_ref[...]
        cq, rq = _rms(q_lat, qa_gain_v)
        ckv, rkv = _rms(kv_lat, kva_gain_v)
        cqb = _bf(cq)
        ckvb = _bf(ckv)
        cos_v = cos_ref[...]
        sin_v = sin_ref[...]
        lane = lax.broadcasted_iota(jnp.int32, (tm, HEAD_PAD), 1)
        rope_lanes = (lane >= QK_NOPE) & (lane < QK_DIM)
        dcq = jnp.zeros((tm, Q_LORA), F32)
        dckv = jnp.zeros((tm, KV_LORA), F32)
        dk_rope = jnp.zeros((tm, HEAD_PAD), F32)
        dqh_gain = jnp.zeros((1, HEAD_PAD), F32)
        dkh_gain = jnp.zeros((1, HEAD_PAD), F32)
        for hd in range(N_HEADS):
            q_pre = _mm(cqb, wuq_ref[hd])
            _, rr = _rms(q_pre, qh_gain_v, QK_DIM)
            dq_pre, dg = _rms_bwd(q_pre, rr, qh_gain_v, _rope_bwd(dq_ref[hd], cos_v, sin_v), QK_DIM)
            dqh_gain = dqh_gain + dg
            dq_pre_b = _bf(dq_pre)
            dcq = dcq + _mm_nt(dq_pre_b, wuq_ref[hd])
            dwuq_ref[hd] += _mm_tn(cqb, dq_pre_b)

            k_pre = _mm(ckvb, wuk_ref[hd]) + k_rope
            _, rr = _rms(k_pre, kh_gain_v, QK_DIM)
            dk_pre, dg = _rms_bwd(k_pre, rr, kh_gain_v, _rope_bwd(dk_ref[hd], cos_v, sin_v), QK_DIM)
            dkh_gain = dkh_gain + dg
            dk_rope = dk_rope + jnp.where(rope_lanes, dk_pre, 0.0)
            dk_pre_b = _bf(dk_pre)
            dvb = _bf(dv_ref[hd])
            dckv = dckv + _mm_nt(dk_pre_b, wuk_ref[hd]) + _mm_nt(dvb, wuv_ref[hd])
            dwuk_ref[hd] += _mm_tn(ckvb, dk_pre_b)
            dwuv_ref[hd] += _mm_tn(ckvb, dvb)
        dqh_ref[...] += dqh_gain
        dkh_ref[...] += dkh_gain
        dq_lat, dg = _rms_bwd(q_lat, rq, qa_gain_v, dcq)
        dqa_ref[...] += dg
        dkv_lat, dg = _rms_bwd(kv_lat, rkv, kva_gain_v, dckv)
        dkva_ref[...] += dg
        dp_ref[:, 2 * d:2 * d + Q_LORA] = _bf(dq_lat)
        dp_ref[:, 2 * d + Q_LORA:2 * d + Q_LORA + KV_LORA] = _bf(dkv_lat)
        dp_ref[:, 2 * d + Q_LORA + KV_LORA:] = _bf(dk_rope)

    tok = lambda c: pl.BlockSpec((tm, c), lambda i: (i, 0))
    col = lambda c: pl.BlockSpec((tm, d), lambda i: (i, c))
    head = lambda c: pl.BlockSpec((N_HEADS, tm, c), lambda i: (0, i, 0))
    nxt = pl.BlockSpec((8, d), lambda i: (jnp.minimum((i + 1) * (tm // 8), last_blk), 0))
    return pl.pallas_call(
        body, name="prep_bwd", grid=(t // tm,),
        in_specs=[tok(LAT_COLS), col(3), col(4), tok(d), nxt, head(HEAD_PAD), head(HEAD_PAD), head(V_DIM),
                  _const((1, Q_LORA)), _const((1, KV_LORA)), _const((1, HEAD_PAD)), _const((1, HEAD_PAD)),
                  _resident(w_uq.shape), _resident(w_uk.shape), _resident(w_uv.shape), tok(HEAD_PAD), tok(HEAD_PAD),
                  _const((3, d))],
        out_specs=[tok(P_COLS), _const(w_uq.shape), _const(w_uk.shape), _const(w_uv.shape), _const((1, Q_LORA)),
                   _const((1, KV_LORA)), _const((1, HEAD_PAD)), _const((1, HEAD_PAD)), _const((3, d))],
        out_shape=[jax.ShapeDtypeStruct((t, P_COLS), BF16), jax.ShapeDtypeStruct(w_uq.shape, F32),
                   jax.ShapeDtypeStruct(w_uk.shape, F32), jax.ShapeDtypeStruct(w_uv.shape, F32),
                   jax.ShapeDtypeStruct((1, Q_LORA), F32), jax.ShapeDtypeStruct((1, KV_LORA), F32),
                   jax.ShapeDtypeStruct((1, HEAD_PAD), F32), jax.ShapeDtypeStruct((1, HEAD_PAD), F32),
                   jax.ShapeDtypeStruct((3, d), F32)],
        compiler_params=_params(("arbitrary",)))(lat, big, big, dz, dz, dq, dk, dv, qa_gain, kva_gain, qh_gain, kh_gain,
                                                  w_uq, w_uk, w_uv, cos, sin, conv_w)


def _inproj_bwd(x1, gain, dx2, dm, dp, w_m, w_p):
    t, d = x1.shape
    tm = TOKEN_TILE

    def body(x_ref, g_ref, dx2_ref, dm_ref, dp_ref, wm_ref, wp_ref, dx1_ref, dgain_ref):
        xv = x_ref[...]
        gain_v = g_ref[...]
        _, r = _rms(xv, gain_v)
        dh = _mm_nt(dm_ref[...], wm_ref[...]) + _mm_nt(dp_ref[...], wp_ref[...])
        dxn, dgain = _rms_bwd(xv, r, gain_v, dh)
        dx1_ref[...] = dx2_ref[...] + dxn

        @pl.when(pl.program_id(0) == 0)
        def _():
            dgain_ref[...] = jnp.zeros_like(dgain_ref)

        dgain_ref[...] += dgain

    tok = lambda c: pl.BlockSpec((tm, c), lambda i: (i, 0))
    return pl.pallas_call(
        body, name="inproj_bwd", grid=(t // tm,),
        in_specs=[tok(d), _const((1, d)), tok(d), tok(M_COLS), tok(P_COLS), _resident(w_m.shape), _resident(w_p.shape)],
        out_specs=[tok(d), _const((1, d))],
        out_shape=[jax.ShapeDtypeStruct((t, d), F32), jax.ShapeDtypeStruct((1, d), F32)],
        compiler_params=_params(("arbitrary",)))(x1, gain, dx2, dm, dp, w_m, w_p)


def _adamw(w, g, m, v, name):
    rows, cols = w.shape
    tr = rows
    for cand in (512, 256, 128, 64, 32, 16, 8):
        if rows % cand == 0 and rows > cand:
            tr = cand
            break

    def body(w_ref, g_ref, m_ref, v_ref, delta_ref, nm_ref, nv_ref):
        gv = g_ref[...]
        nm = ADAM_B1 * m_ref[...] + (1.0 - ADAM_B1) * gv
        nv = ADAM_B2 * v_ref[...] + (1.0 - ADAM_B2) * (gv * gv)
        m_hat = nm * (1.0 / (1.0 - ADAM_B1 ** ADAM_STEP))
        v_hat = nv * (1.0 / (1.0 - ADAM_B2 ** ADAM_STEP))
        delta_ref[...] = -ADAM_LR * (m_hat / (jnp.sqrt(v_hat) + ADAM_EPS) + ADAM_WD * w_ref[...])
        nm_ref[...] = nm
        nv_ref[...] = nv

    spec = pl.BlockSpec((tr, cols), lambda i: (i, 0))
    shape = jax.ShapeDtypeStruct((rows, cols), F32)
    return pl.pallas_call(body, name=name, grid=(rows // tr,), in_specs=[spec] * 4, out_specs=[spec] * 3, out_shape=[shape] * 3,
                          compiler_params=_params(("arbitrary",)))(w, g, m, v)


def _place():
    x, y, c = lax.axis_index("x"), lax.axis_index("y"), lax.axis_index("c")
    other_chips = [(1 - x, y), (x, 1 - y), (1 - x, 1 - y)]
    return x, y, c, other_chips


def _gather_chips(shards):
    n = len(shards)

    def body(*refs):
        ins, outs = refs[:n], refs[n:2 * n]
        send_sems, recv_sems, local_sems = refs[2 * n:]
        x, y, c, other_chips = _place()
        me = 2 * x + y
        local = [pltpu.make_async_copy(ins[w], outs[w].at[me], local_sems.at[w]) for w in range(n)]
        for cp in local:
            cp.start()
        sends = []
        for w in range(n):
            for p, (px, py) in enumerate(other_chips):
                cp = pltpu.make_async_remote_copy(src_ref=ins[w], dst_ref=outs[w].at[me], send_sem=send_sems.at[3 * w + p],
                                                  recv_sem=recv_sems.at[3 * w + p], device_id=(px, py, c), device_id_type=MESH_ID)
                cp.start()
                sends.append(cp)
        for w in range(n):
            for p, (px, py) in enumerate(other_chips):
                pltpu.make_async_remote_copy(src_ref=ins[w], dst_ref=outs[w].at[2 * px + py], send_sem=send_sems.at[3 * w + p],
                                             recv_sem=recv_sems.at[3 * w + p], device_id=(px, py, c),
                                             device_id_type=MESH_ID).wait_recv()
        for cp in sends:
            cp.wait_send()
        for cp in local:
            cp.wait()

    return pl.pallas_call(
        body, name="gather_chips", in_specs=[ANY] * n, out_specs=[ANY] * n,
        out_shape=[jax.ShapeDtypeStruct((N_CHIPS,) + s.shape, s.dtype) for s in shards],
        scratch_shapes=[pltpu.SemaphoreType.DMA((3 * n,)), pltpu.SemaphoreType.DMA((3 * n,)), pltpu.SemaphoreType.DMA((n,))],
    )(*shards)


def _swap_halves(grads):
    n = len(grads)

    def body(*refs):
        ins, outs = refs[:n], refs[n:2 * n]
        send_sems, recv_sems = refs[2 * n:]
        x, y, c, _ = _place()
        copies = []
        for w in range(n):
            half = grads[w].shape[1] // 2
            start = pl.multiple_of((1 - c) * half, 8)
            cp = pltpu.make_async_remote_copy(src_ref=ins[w].at[:, pl.ds(start, half), :], dst_ref=outs[w],
                                              send_sem=send_sems.at[w], recv_sem=recv_sems.at[w], device_id=(x, y, 1 - c),
                                              device_id_type=MESH_ID)
            cp.start()
            copies.append(cp)
        for cp in copies:
            cp.wait()

    return pl.pallas_call(
        body, name="swap_halves", in_specs=[ANY] * n, out_specs=[ANY] * n,
        out_shape=[jax.ShapeDtypeStruct((g.shape[0], g.shape[1] // 2, g.shape[2]), F32) for g in grads],
        scratch_shapes=[pltpu.SemaphoreType.DMA((n,)), pltpu.SemaphoreType.DMA((n,))],
    )(*grads)


def _row_tile(rows):
    for cand in (256, 176, 128, 96, 64, 32, 16):
        if rows % cand == 0:
            return cand
    return rows


def _chip_partial(grad, other, c_arr, name):
    nblk, half, cols = other.shape
    tr = _row_tile(half)
    per_half = half // tr

    def body(c_ref, g_ref, o_ref, sum_ref, sum_bf_ref):
        s = g_ref[...] + o_ref[...]
        sum_ref[...] = s
        sum_bf_ref[...] = _bf(s)

    grid_spec = pltpu.PrefetchScalarGridSpec(
        num_scalar_prefetch=1, grid=(nblk, per_half),
        in_specs=[pl.BlockSpec((None, tr, cols), lambda b, i, c_ref: (b, c_ref[0] * per_half + i, 0)),
                  pl.BlockSpec((None, tr, cols), lambda b, i, c_ref: (b, i, 0))],
        out_specs=[pl.BlockSpec((None, tr, cols), lambda b, i, c_ref: (b, i, 0))] * 2)
    return pl.pallas_call(body, name=name, grid_spec=grid_spec,
                          out_shape=[jax.ShapeDtypeStruct(other.shape, F32), jax.ShapeDtypeStruct(other.shape, BF16)],
                          compiler_params=_params(("arbitrary", "arbitrary")))(c_arr, grad, other)


def _send_partials(partials):
    n = len(partials)

    def body(*refs):
        ins, outs = refs[:n], refs[n:2 * n]
        send_sems, recv_sems = refs[2 * n:]
        x, y, c, other_chips = _place()
        me = 2 * x + y
        sends = []
        for w in range(n):
            for p, (px, py) in enumerate(other_chips):
                cp = pltpu.make_async_remote_copy(src_ref=ins[w].at[2 * px + py], dst_ref=outs[w].at[me],
                                                  send_sem=send_sems.at[3 * w + p], recv_sem=recv_sems.at[3 * w + p],
                                                  device_id=(px, py, c), device_id_type=MESH_ID)
                cp.start()
                sends.append(cp)
        for w in range(n):
            for p, (px, py) in enumerate(other_chips):
                pltpu.make_async_remote_copy(src_ref=ins[w].at[me], dst_ref=outs[w].at[2 * px + py],
                                             send_sem=send_sems.at[3 * w + p], recv_sem=recv_sems.at[3 * w + p],
                                             device_id=(px, py, c), device_id_type=MESH_ID).wait_recv()
        for cp in sends:
            cp.wait_send()

    return pl.pallas_call(
        body, name="send_partials", in_specs=[ANY] * n, out_specs=[ANY] * n,
        out_shape=[jax.ShapeDtypeStruct(p.shape, BF16) for p in partials],
        scratch_shapes=[pltpu.SemaphoreType.DMA((3 * n,)), pltpu.SemaphoreType.DMA((3 * n,))],
    )(*partials)


def _chip_total(own, received, me_arr, name):
    nblk, half, cols = own.shape
    tr = _row_tile(half)

    def body(me_ref, own_ref, r1_ref, r2_ref, r3_ref, out_ref):
        out_ref[...] = own_ref[...] + ((r1_ref[...].astype(F32) + r2_ref[...].astype(F32)) + r3_ref[...].astype(F32))

    def slot(k):
        return pl.BlockSpec((None, tr, cols), lambda i, me_ref: ((me_ref[0] + k) % N_CHIPS, i, 0))

    grid_spec = pltpu.PrefetchScalarGridSpec(
        num_scalar_prefetch=1, grid=(half // tr,), in_specs=[slot(0), slot(1), slot(2), slot(3)],
        out_specs=pl.BlockSpec((tr, cols), lambda i, me_ref: (i, 0)))
    return pl.pallas_call(body, name=name, grid_spec=grid_spec, out_shape=jax.ShapeDtypeStruct((half, cols), F32),
                          compiler_params=_params(("arbitrary",)))(me_arr, own, received, received, received)


def _join_halves(halves):
    n = len(halves)

    def body(*refs):
        ins, outs = refs[:n], refs[n:2 * n]
        send_sems, recv_sems, local_sems = refs[2 * n:]
        x, y, c, _ = _place()
        copies = []
        for w in range(n):
            half = halves[w].shape[0]
            mine = outs[w].at[pl.ds(pl.multiple_of(c * half, 8), half), :]
            lc = pltpu.make_async_copy(ins[w], mine, local_sems.at[w])
            lc.start()
            cp = pltpu.make_async_remote_copy(src_ref=ins[w], dst_ref=mine, send_sem=send_sems.at[w], recv_sem=recv_sems.at[w],
                                              device_id=(x, y, 1 - c), device_id_type=MESH_ID)
            cp.start()
            copies += [lc, cp]
        for cp in copies:
            cp.wait()

    return pl.pallas_call(
        body, name="join_halves", in_specs=[ANY] * n, out_specs=[ANY] * n,
        out_shape=[jax.ShapeDtypeStruct((2 * h.shape[0], h.shape[1]), F32) for h in halves],
        scratch_shapes=[pltpu.SemaphoreType.DMA((n,)), pltpu.SemaphoreType.DMA((n,)), pltpu.SemaphoreType.DMA((n,))],
    )(*halves)


def _sum_devices(vec):
    rows, n = vec.shape

    def body(v_ref, out_ref, buf, send_sems, recv_sems):
        x, y, c, _ = _place()
        me = 4 * x + 2 * y + c
        buf[me] = v_ref[...]
        sends = []
        for k in range(1, N_DEV):
            peer = (1 - x if k & 4 else x, 1 - y if k & 2 else y, 1 - c if k & 1 else c)
            cp = pltpu.make_async_remote_copy(src_ref=v_ref, dst_ref=buf.at[me], send_sem=send_sems.at[k], recv_sem=recv_sems.at[k],
                                              device_id=peer, device_id_type=MESH_ID)
            cp.start()
            sends.append(cp)
        for cp in sends:
            cp.wait()
        total = buf[0]
        for dev in range(1, N_DEV):
            total = total + buf[dev]
        out_ref[...] = total

    vm = pl.BlockSpec(memory_space=pltpu.VMEM)
    return pl.pallas_call(
        body, name="sum_devices", in_specs=[vm], out_specs=vm, out_shape=jax.ShapeDtypeStruct((rows, n), F32),
        scratch_shapes=[pltpu.VMEM((N_DEV, rows, n), F32), pltpu.SemaphoreType.DMA((N_DEV,)), pltpu.SemaphoreType.DMA((N_DEV,))],
    )(vec)


def _rope_tables(positions):
    half = QK_ROPE // 2
    inv_freq = 1.0 / (ROPE_THETA ** (jnp.arange(half, dtype=F32) / half))
    ang = positions.astype(F32).reshape(-1, 1) * inv_freq
    cos, sin = jnp.cos(ang), jnp.sin(ang)
    t = ang.shape[0]
    ones, zeros = jnp.ones((t, QK_NOPE), F32), jnp.zeros((t, QK_NOPE), F32)
    pad = HEAD_PAD - QK_DIM
    cos_full = jnp.concatenate([ones, cos, cos, ones[:, :pad]], axis=1)
    sin_signed = jnp.concatenate([zeros, -sin, sin, zeros[:, :pad]], axis=1)
    return cos_full, sin_signed


def _local_step(x, cos, sin, target, p, w, seq):
    x1, gate1, up1, act1 = _ffn_fwd(x, p["ffn1_norm"], w["wg1"], w["wu1"], w["wd1"], None, "ffn1_fwd")
    h2b, big, lat, q, k, v, vt = _inproj_fwd(x1, p["mix_norm"], w["w_m"], w["w_p"], p["q_a_norm"], p["kv_a_norm"],
                                             p["q_head_norm"], p["k_head_norm"], w["w_uq"], w["w_uk"], w["w_uv"], w["w_uvt"], cos, sin)
    o, lse = _attn_fwd(q, k, vt, seq)
    x2 = _mix_fwd(x1, o, big, p["gate_bias"], p["conv_w"], w["w_pa"], w["w_pc"], w["w_out"], seq)
    dx3, gate2, up2, act2, loss = _ffn_fwd(x2, p["ffn2_norm"], w["wg2"], w["wu2"], w["wd2"], target, "ffn2_fwd")

    dx2, dg_ffn2, hb2, dgate2, dup2, dyb2 = _ffn_bwd_x(x2, p["ffn2_norm"], dx3, gate2, up2, w["wg2"], w["wu2"], w["wd2"], "ffn2_bwd")
    do, delta, dz, dm, dbias, dw_pa, dw_pc, dw_out = _mix_bwd(dx2, o, big, p["gate_bias"], p["conv_w"], w["w_pa"], w["w_pc"],
                                                              w["w_out"], seq)
    dq, dk, dv = _attn_bwd(q, k, v, do, lse, delta.reshape(N_HEADS // ATTN_BWD_HEADS, ATTN_BWD_HEADS, -1), seq)
    dp, dw_uq, dw_uk, dw_uv, dqa, dkva, dqh, dkh, dcw = _prep_bwd(
        lat, big, dz, dq, dk, dv, p["q_a_norm"], p["kv_a_norm"], p["q_head_norm"], p["k_head_norm"], w["w_uq"], w["w_uk"],
        w["w_uv"], cos, sin, p["conv_w"], seq)
    dx1, dg_mix = _inproj_bwd(x1, p["mix_norm"], dx2, dm, dp, w["w_m"], w["w_p"])
    grad_x, dg_ffn1, hb1, dgate1, dup1, dyb1 = _ffn_bwd_x(x, p["ffn1_norm"], dx1, gate1, up1, w["wg1"], w["wu1"], w["wd1"], "ffn1_bwd")

    big_grads = {
        "wg1": _tn_matmul(hb1, dgate1, "ffn1_dw_gate"), "wu1": _tn_matmul(hb1, dup1, "ffn1_dw_up"),
        "wd1": _tn_matmul(act1, dyb1, "ffn1_dw_down"),
        "wg2": _tn_matmul(hb2, dgate2, "ffn2_dw_gate"), "wu2": _tn_matmul(hb2, dup2, "ffn2_dw_up"),
        "wd2": _tn_matmul(act2, dyb2, "ffn2_dw_down"),
        "w_m": _tn_matmul(h2b, dm, "dw_in_m", split_n=2), "w_p": _tn_matmul(h2b, dp, "dw_in_p", split_n=2),
        "w_uq": dw_uq, "w_uk": dw_uk, "w_uv": dw_uv, "w_pa": dw_pa, "w_pc": dw_pc, "w_out": dw_out,
    }
    small_grads = {"ffn1_norm": dg_ffn1, "mix_norm": dg_mix, "gate_bias": dbias, "q_a_norm": dqa, "kv_a_norm": dkva,
                   "q_head_norm": dqh, "k_head_norm": dkh, "ffn2_norm": dg_ffn2, "conv_w": dcw}
    return loss, grad_x, small_grads, big_grads


def _kernel_layouts(full):
    d = D_MODEL
    w_in = full["w_in"]
    o_q, o_kv, o_kr, o_xc, o_gb, o_gc, o_gl = 0, Q_LORA, Q_LORA + KV_LORA, Q_LORA + KV_LORA + QK_ROPE, 0, 0, 0
    o_gb = o_xc + d
    o_gc = o_gb + d
    o_gl = o_gc + d
    k_rope_pad = jnp.pad(w_in[:, o_kr:o_xc], ((0, 0), (QK_NOPE, HEAD_PAD - QK_DIM)))
    w_m = jnp.concatenate([w_in[:, o_gb:o_gc], w_in[:, o_gl:]], axis=1)
    w_p = jnp.concatenate([w_in[:, o_xc:o_gb], w_in[:, o_gc:o_gl], w_in[:, o_q:o_kr], k_rope_pad], axis=1)
    w_uq = jnp.pad(full["w_uq"].reshape(Q_LORA, N_HEADS, QK_DIM), ((0, 0), (0, 0), (0, HEAD_PAD - QK_DIM))).transpose(1, 0, 2)
    w_uk = jnp.pad(full["w_uk"].reshape(KV_LORA, N_HEADS, QK_NOPE), ((0, 0), (0, 0), (0, HEAD_PAD - QK_NOPE))).transpose(1, 0, 2)
    w_uv = full["w_uv"].reshape(KV_LORA, N_HEADS, V_DIM).transpose(1, 0, 2)
    return {"w_m": w_m, "w_p": w_p, "w_uq": w_uq, "w_uk": w_uk, "w_uv": w_uv, "w_uvt": w_uv.transpose(0, 2, 1),
            "w_pa": full["w_proj_attn"], "w_pc": full["w_proj_conv"], "w_out": full["w_out"]}


def _global_layouts(g):
    d = D_MODEL
    dw = jnp.concatenate([g["w_m"], g["w_p"]], axis=1)
    o_xc, o_gc, o_lat = M_COLS, M_COLS + d, M_COLS + 2 * d
    o_kr = o_lat + Q_LORA + KV_LORA + QK_NOPE
    w_in = jnp.concatenate([dw[:, o_lat:o_lat + Q_LORA + KV_LORA], dw[:, o_kr:o_kr + QK_ROPE], dw[:, o_xc:o_gc], dw[:, :d],
                            dw[:, o_gc:o_lat], dw[:, d:M_COLS]], axis=1)
    w_uq = g["w_uq"][:, :, :QK_DIM].transpose(1, 0, 2).reshape(Q_LORA, N_HEADS * QK_DIM)
    w_uk = g["w_uk"][:, :, :QK_NOPE].transpose(1, 0, 2).reshape(KV_LORA, N_HEADS * QK_NOPE)
    w_uv = g["w_uv"].transpose(1, 0, 2).reshape(KV_LORA, N_HEADS * V_DIM)
    return {"w_in": w_in, "w_uq": w_uq, "w_uk": w_uk, "w_uv": w_uv, "w_proj_attn": g["w_pa"], "w_proj_conv": g["w_pc"],
            "w_out": g["w_out"]}


def _col_blocks(a):
    r, c = a.shape
    return a.reshape(r, N_CHIPS, c // N_CHIPS).transpose(1, 0, 2)


def _from_col_blocks(a):
    n, r, c = a.shape
    return a.transpose(1, 0, 2).reshape(r, n * c)


COL_SHARDED = ("w_in", "w_uq", "w_uk", "w_uv", "w_proj_attn")
ROW_SHARDED = ("w_proj_conv", "w_out")
SMALL = (("ffn1_norm", 1024), ("mix_norm", 1024), ("gate_bias", 2048), ("q_a_norm", 384), ("kv_a_norm", 256),
         ("q_head_norm", 128), ("k_head_norm", 128), ("ffn2_norm", 1024))
WEIGHT_ORDER = ("ffn1_norm", "ffn1_w_gate", "ffn1_w_up", "ffn1_w_down", "mix_norm", "w_in", "gate_bias", "q_a_norm", "w_uq",
                "kv_a_norm", "w_uk", "w_uv", "q_head_norm", "k_head_norm", "w_proj_attn", "conv_w", "w_proj_conv", "w_out",
                "ffn2_norm", "ffn2_w_gate", "ffn2_w_up", "ffn2_w_down")
MATRICES = ("ffn1_w_gate", "ffn1_w_up", "ffn1_w_down", "w_in", "w_uq", "w_uk", "w_uv", "w_proj_attn", "w_proj_conv", "w_out",
            "ffn2_w_gate", "ffn2_w_up", "ffn2_w_down")


def _pad_lanes(a, n):
    return jnp.pad(a.reshape(1, -1), ((0, 0), (0, n - a.size)))


def kernel(x, positions, ffn1_norm, ffn1_w_gate, ffn1_w_up, ffn1_w_down, mix_norm, w_in, gate_bias, q_a_norm, w_uq, kv_a_norm, w_uk, w_uv, q_head_norm, k_head_norm, w_proj_attn, conv_w, w_proj_conv, w_out, ffn2_norm, ffn2_w_gate, ffn2_w_up, ffn2_w_down, loss_target, m_ffn1_norm, m_ffn1_w_gate, m_ffn1_w_up, m_ffn1_w_down, m_mix_norm, m_w_in, m_gate_bias, m_q_a_norm, m_w_uq, m_kv_a_norm, m_w_uk, m_w_uv, m_q_head_norm, m_k_head_norm, m_w_proj_attn, m_conv_w, m_w_proj_conv, m_w_out, m_ffn2_norm, m_ffn2_w_gate, m_ffn2_w_up, m_ffn2_w_down, v_ffn1_norm, v_ffn1_w_gate, v_ffn1_w_up, v_ffn1_w_down, v_mix_norm, v_w_in, v_gate_bias, v_q_a_norm, v_w_uq, v_kv_a_norm, v_w_uk, v_w_uv, v_q_head_norm, v_k_head_norm, v_w_proj_attn, v_conv_w, v_w_proj_conv, v_w_out, v_ffn2_norm, v_ffn2_w_gate, v_ffn2_w_up, v_ffn2_w_down):
    args = dict(locals())
    weights = {n: args[n] for n in WEIGHT_ORDER}
    moments_m = {n: args["m_" + n] for n in WEIGHT_ORDER}
    moments_v = {n: args["v_" + n] for n in WEIGHT_ORDER}
    nb, seq, d = x.shape
    t = nb * seq
    chip = (2 * lax.axis_index("x") + lax.axis_index("y")).astype(jnp.int32)
    core = lax.axis_index("c").astype(jnp.int32)

    conv_shard = jnp.pad(conv_w, ((0, 8 - conv_w.shape[0]), (0, 0)))
    gathered = _gather_chips([_bf(weights[n]) for n in MATRICES] + [conv_shard])
    blocks = dict(zip(MATRICES, gathered[:-1]))
    conv_full = _from_col_blocks(gathered[-1])[:conv_w.shape[0]]
    full = {n: _from_col_blocks(blocks[n]) for n in COL_SHARDED}
    full.update({n: blocks[n].reshape(-1, blocks[n].shape[-1]) for n in ROW_SHARDED})
    w = _kernel_layouts(full)
    w.update({"wg1": blocks["ffn1_w_gate"], "wu1": blocks["ffn1_w_up"], "wd1": blocks["ffn1_w_down"],
              "wg2": blocks["ffn2_w_gate"], "wu2": blocks["ffn2_w_up"], "wd2": blocks["ffn2_w_down"]})
    p = {n: _pad_lanes(weights[n], size) for n, size in SMALL}
    p["conv_w"] = conv_full

    cos, sin = _rope_tables(positions)
    loss, grad_x, small_grads, g = _local_step(x.reshape(t, d), cos, sin, loss_target.reshape(t, d), p, w, seq)

    packed = jnp.concatenate([small_grads[n] for n, _ in SMALL] + [small_grads["conv_w"].reshape(1, -1), loss], axis=1)
    total = _sum_devices(jnp.pad(packed, ((0, 7), (0, 0))))[0:1]
    n_small = sum(size for _, size in SMALL)
    conv_cols = conv_w.shape[1]
    conv_total = total[:, n_small:n_small + 3 * d].reshape(3, d)
    grads = {"conv_w": lax.dynamic_slice_in_dim(conv_total, chip * conv_cols, conv_cols, axis=1)}
    loss_total = total[0, n_small + 3 * d]

    gg = _global_layouts(g)
    block_grads = {"ffn1_w_gate": g["wg1"], "ffn1_w_up": g["wu1"], "ffn1_w_down": g["wd1"],
                   "ffn2_w_gate": g["wg2"], "ffn2_w_up": g["wu2"], "ffn2_w_down": g["wd2"]}
    block_grads.update({n: _col_blocks(gg[n]) for n in COL_SHARDED})
    block_grads.update({n: gg[n].reshape(N_CHIPS, -1, gg[n].shape[-1]) for n in ROW_SHARDED})
    local = [block_grads[n] for n in MATRICES]
    from_sibling = _swap_halves(local)
    core_arr, chip_arr = core.reshape(1), chip.reshape(1)
    partial = [_chip_partial(a, b, core_arr, "chip_partial_" + n) for n, a, b in zip(MATRICES, local, from_sibling)]
    received = _send_partials([pb for _, pb in partial])
    halves = [_chip_total(pf, r, chip_arr, "chip_total_" + n) for n, (pf, _), r in zip(MATRICES, partial, received)]
    grads.update(dict(zip(MATRICES, _join_halves(halves))))

    delta, new_m, new_v = {}, {}, {}
    for n in MATRICES + ("conv_w",):
        delta[n], new_m[n], new_v[n] = _adamw(weights[n], grads[n], moments_m[n], moments_v[n], "adamw_" + n)
    pack = lambda src: jnp.concatenate([_pad_lanes(src[n], size) for n, size in SMALL], axis=1)
    sd, sm, sv = _adamw(pack(weights), total[:, :n_small], pack(moments_m), pack(moments_v), "adamw_small")
    off = 0
    for n, size in SMALL:
        real = weights[n].size
        grads[n] = total[0, off:off + real]
        delta[n], new_m[n], new_v[n] = sd[0, off:off + real], sm[0, off:off + real], sv[0, off:off + real]
        off += size

    return (loss_total, grad_x.reshape(nb, seq, d), *[grads[n] for n in WEIGHT_ORDER], *[delta[n] for n in WEIGHT_ORDER],
            *[new_m[n] for n in WEIGHT_ORDER], *[new_v[n] for n in WEIGHT_ORDER])
```

```python
import functools

import jax
import jax.numpy as jnp
from jax import lax
from jax.experimental import pallas as pl
from jax.experimental.pallas import tpu as pltpu

F32 = jnp.float32
BF16 = jnp.bfloat16

D_MODEL = 1024
N_HEADS = 8
QK_NOPE = 64
QK_ROPE = 32
QK_DIM = QK_NOPE + QK_ROPE
V_DIM = 64
HEAD_PAD = 128
Q_LORA = 384
KV_LORA = 256
ROPE_THETA = 10000.0
NORM_EPS = 1e-6
ATTN_SCALE = QK_DIM ** -0.5
MASK_VALUE = -1e30
N_CHIPS = 4
N_DEV = 8

ADAM_LR = 0.001
ADAM_B1 = 0.9
ADAM_B2 = 0.999
ADAM_EPS = 1e-08
ADAM_WD = 0.01
ADAM_STEP = 10

TOKEN_TILE = 256
ATTN_TILE = 512
TN_TILE = 512
VMEM_LIMIT = 56 * 1024 * 1024

M_COLS = 3 * D_MODEL
P_COLS = 2 * D_MODEL + Q_LORA + KV_LORA + HEAD_PAD
BIG_COLS = 5 * D_MODEL
LAT_COLS = Q_LORA + KV_LORA + HEAD_PAD

MESH_ID = pl.DeviceIdType.MESH
ANY = pl.BlockSpec(memory_space=pl.ANY)


def _params(semantics=None):
    return pltpu.CompilerParams(dimension_semantics=semantics, vmem_limit_bytes=VMEM_LIMIT)


class _Carried:
    def __init__(self, operands, out_shapes, aliases, n_sems, start, finish):
        self.operands, self.out_shapes, self.aliases, self.n_sems = list(operands), list(out_shapes), dict(aliases), n_sems
        self.start, self.finish = start, finish
        self.results = None


def _both(a, b):
    na, nao = len(a.operands), len(a.out_shapes)

    def start(ins, outs, sems, base):
        a.start(ins[:na], outs[:nao], sems, base)
        b.start(ins[na:], outs[nao:], sems, base + a.n_sems)

    def finish(ins, outs, sems, base):
        a.finish(ins[:na], outs[:nao], sems, base)
        b.finish(ins[na:], outs[nao:], sems, base + a.n_sems)

    aliases = dict(a.aliases)
    aliases.update({na + i: nao + o for i, o in b.aliases.items()})
    both = _Carried(a.operands + b.operands, a.out_shapes + b.out_shapes, aliases, a.n_sems + b.n_sems, start, finish)
    both.parts = (a, b)
    return both


def _set_results(carried, results):
    carried.results = list(results)
    if hasattr(carried, "parts"):
        a, b = carried.parts
        _set_results(a, results[:len(a.out_shapes)])
        _set_results(b, results[len(a.out_shapes):])


def _pallas(body, name, grid, in_specs, out_specs, out_shape, args, semantics, carried=None):
    if carried is None:
        return pl.pallas_call(body, name=name, grid=grid, in_specs=in_specs, out_specs=out_specs, out_shape=out_shape,
                              compiler_params=_params(semantics))(*args)
    n_in, n_out, n_ci, n_co = len(in_specs), len(out_specs), len(carried.operands), len(carried.out_shapes)

    def wrapped(*refs):
        ins, c_ins = refs[:n_in], refs[n_in:n_in + n_ci]
        outs, c_outs = refs[n_in + n_ci:n_in + n_ci + n_out], refs[n_in + n_ci + n_out:n_in + n_ci + n_out + n_co]
        sems = refs[-1]
        first = pl.program_id(0) == 0
        last = pl.program_id(0) == grid[0] - 1
        for axis in range(1, len(grid)):
            first = jnp.logical_and(first, pl.program_id(axis) == 0)
            last = jnp.logical_and(last, pl.program_id(axis) == grid[axis] - 1)

        @pl.when(first)
        def _():
            carried.start(c_ins, c_outs, sems, 0)

        body(*ins, *outs)

        @pl.when(last)
        def _():
            carried.finish(c_ins, c_outs, sems, 0)

    results = pl.pallas_call(
        wrapped, name=name, grid=grid, in_specs=list(in_specs) + [ANY] * n_ci, out_specs=list(out_specs) + [ANY] * n_co,
        out_shape=list(out_shape) + carried.out_shapes,
        input_output_aliases={n_in + i: n_out + o for i, o in carried.aliases.items()},
        scratch_shapes=[pltpu.SemaphoreType.DMA((carried.n_sems,))], compiler_params=_params(semantics))(*args, *carried.operands)
    _set_results(carried, results[n_out:])
    return results[:n_out]


def _run(carried, name):
    n_ci, n_co = len(carried.operands), len(carried.out_shapes)

    def body(*refs):
        carried.start(refs[:n_ci], refs[n_ci:n_ci + n_co], refs[-1], 0)
        carried.finish(refs[:n_ci], refs[n_ci:n_ci + n_co], refs[-1], 0)

    results = pl.pallas_call(body, name=name, in_specs=[ANY] * n_ci, out_specs=[ANY] * n_co, out_shape=carried.out_shapes,
                             input_output_aliases=carried.aliases,
                             scratch_shapes=[pltpu.SemaphoreType.DMA((carried.n_sems,))])(*carried.operands)
    _set_results(carried, results)
    return carried.results


def _resident(shape):
    nd = len(shape)
    return pl.BlockSpec(shape, lambda *_: (0,) * nd, pipeline_mode=pl.Buffered(1))


def _const(shape):
    nd = len(shape)
    return pl.BlockSpec(shape, lambda *_: (0,) * nd)


def _mm(a, b):
    return jnp.dot(a, b, preferred_element_type=F32)


def _mm_nt(a, b):
    return lax.dot_general(a, b, (((1,), (1,)), ((), ())), preferred_element_type=F32)


def _mm_tn(a, b):
    return lax.dot_general(a, b, (((0,), (0,)), ((), ())), preferred_element_type=F32)


def _bf(a):
    return a.astype(BF16)


def _sigmoid(a):
    return 1.0 / (1.0 + jnp.exp(-a))


def _rms(x, gain, n=None):
    n = x.shape[-1] if n is None else n
    r = lax.rsqrt(jnp.sum(x * x, axis=-1, keepdims=True) * (1.0 / n) + NORM_EPS)
    return (x * r) * gain, r


def _rms_bwd(x, r, gain, dh, n=None):
    n = x.shape[-1] if n is None else n
    u = dh * gain
    dx = r * u - x * ((r * r * r) * (jnp.sum(u * x, axis=-1, keepdims=True) * (1.0 / n)))
    dgain = jnp.sum(dh * (x * r), axis=0, keepdims=True)
    return dx, dgain


def _rope_swap(t):
    lane = lax.broadcasted_iota(jnp.int32, t.shape, 1)
    lo = (lane >= QK_NOPE) & (lane < QK_NOPE + QK_ROPE // 2)
    hi = (lane >= QK_NOPE + QK_ROPE // 2) & (lane < QK_DIM)
    up = pltpu.roll(t, HEAD_PAD - QK_ROPE // 2, 1)
    down = pltpu.roll(t, QK_ROPE // 2, 1)
    return jnp.where(lo, up, jnp.where(hi, down, 0.0))


def _rope(t, cos, sin):
    return t * cos + _rope_swap(t) * sin


def _rope_bwd(dt, cos, sin):
    return dt * cos + _rope_swap(dt * sin)


def _shift_down(u, prev8, k):
    s = pltpu.roll(u, k, 0)
    p = pltpu.roll(prev8, k, 0)
    row = lax.broadcasted_iota(jnp.int32, prev8.shape, 0)
    top = jnp.where(row < k, p, s[:8])
    return jnp.concatenate([top, s[8:]], axis=0)


def _shift_up(d, next8, k):
    tm = d.shape[0]
    s = pltpu.roll(d, tm - k, 0)
    n = pltpu.roll(next8, 8 - k, 0)
    row = lax.broadcasted_iota(jnp.int32, next8.shape, 0)
    bot = jnp.where(row >= 8 - k, n, s[tm - 8:])
    return jnp.concatenate([s[:tm - 8], bot], axis=0)


def _ffn_fwd(x, gain, wg, wu, wd, target, name, carried=None):
    t, d = x.shape
    nb, _, f = wg.shape
    tm = TOKEN_TILE
    with_loss = target is not None

    def body(*refs):
        if with_loss:
            x_ref, g_ref, wg_ref, wu_ref, wd_ref, t_ref, out_ref, gate_ref, up_ref, act_ref, loss_ref = refs
        else:
            x_ref, g_ref, wg_ref, wu_ref, wd_ref, out_ref, gate_ref, up_ref, act_ref = refs
        xv = x_ref[...]
        h, _ = _rms(xv, g_ref[...])
        hb = _bf(h)
        y = jnp.zeros((tm, d), F32)
        for j in range(nb):
            gate = _mm(hb, wg_ref[j])
            up = _mm(hb, wu_ref[j])
            act = _bf((gate * _sigmoid(gate)) * up)
            y = y + _mm(act, wd_ref[j])
            gate_ref[j] = _bf(gate)
            up_ref[j] = _bf(up)
            act_ref[j] = act
        out = xv + 0.5 * y
        if with_loss:
            err = out - t_ref[...]
            out_ref[...] = err * (1.0 / d)

            @pl.when(pl.program_id(0) == 0)
            def _():
                loss_ref[...] = jnp.zeros_like(loss_ref)

            part = jnp.sum(jnp.sum(err * err, axis=1, keepdims=True), axis=0, keepdims=True)
            loss_ref[...] += jnp.broadcast_to(part * (0.5 / d), loss_ref.shape)
        else:
            out_ref[...] = out

    tok = pl.BlockSpec((tm, d), lambda i: (i, 0))
    blk = pl.BlockSpec((nb, tm, f), lambda i: (0, i, 0))
    in_specs = [tok, _const((1, d)), _resident(wg.shape), _resident(wu.shape), _resident(wd.shape)]
    args = [x, gain, wg, wu, wd]
    out_shape = [jax.ShapeDtypeStruct((t, d), F32)] + [jax.ShapeDtypeStruct((nb, t, f), BF16)] * 3
    out_specs = [tok, blk, blk, blk]
    if with_loss:
        in_specs.append(tok)
        args.append(target)
        out_shape.append(jax.ShapeDtypeStruct((1, 128), F32))
        out_specs.append(_const((1, 128)))
    return _pallas(body, name, (t // tm,), in_specs, out_specs, out_shape, args, ("arbitrary",), carried)


def _ffn_bwd_x(x, gain, dout, gate, up, wg, wu, wd, name, carried=None):
    t, d = x.shape
    nb, _, f = wg.shape
    tm = TOKEN_TILE

    def body(x_ref, g_ref, dout_ref, gate_ref, up_ref, wg_ref, wu_ref, wd_ref,
             dx_ref, dgain_ref, hb_ref, dgate_ref, dup_ref, dyb_ref):
        xv = x_ref[...]
        gain_v = g_ref[...]
        h, r = _rms(xv, gain_v)
        hb_ref[...] = _bf(h)
        dout_v = dout_ref[...]
        dyb = _bf(0.5 * dout_v)
        dyb_ref[...] = dyb
        dh = jnp.zeros((tm, d), F32)
        for j in range(nb):
            gt = gate_ref[j].astype(F32)
            uv = up_ref[j].astype(F32)
            s = _sigmoid(gt)
            dact = _mm_nt(dyb, wd_ref[j])
            dup = _bf(dact * (gt * s))
            dgate = _bf((dact * uv) * (s * (1.0 + gt * (1.0 - s))))
            dh = dh + _mm_nt(dgate, wg_ref[j]) + _mm_nt(dup, wu_ref[j])
            dgate_ref[j] = dgate
            dup_ref[j] = dup
        dxn, dgain = _rms_bwd(xv, r, gain_v, dh)
        dx_ref[...] = dout_v + dxn

        @pl.when(pl.program_id(0) == 0)
        def _():
            dgain_ref[...] = jnp.zeros_like(dgain_ref)

        dgain_ref[...] += dgain

    tok = pl.BlockSpec((tm, d), lambda i: (i, 0))
    blk = pl.BlockSpec((nb, tm, f), lambda i: (0, i, 0))
    return _pallas(
        body, name, (t // tm,),
        [tok, _const((1, d)), tok, blk, blk, _resident(wg.shape), _resident(wu.shape), _resident(wd.shape)],
        [tok, _const((1, d)), tok, blk, blk, tok],
        [jax.ShapeDtypeStruct((t, d), F32), jax.ShapeDtypeStruct((1, d), F32), jax.ShapeDtypeStruct((t, d), BF16),
         jax.ShapeDtypeStruct((nb, t, f), BF16), jax.ShapeDtypeStruct((nb, t, f), BF16), jax.ShapeDtypeStruct((t, d), BF16)],
        (x, gain, dout, gate, up, wg, wu, wd), ("arbitrary",), carried)


def _tn_matmul(a, b, name, split_n=1, carried=None):
    t = a.shape[-2]
    k = a.shape[-1]
    n = b.shape[-1]
    tt = min(TN_TILE, t)
    nt = t // tt

    def body(a_ref, b_ref, o_ref):
        @pl.when(pl.program_id(1) == 0)
        def _():
            o_ref[...] = jnp.zeros_like(o_ref)

        o_ref[...] += _mm_tn(a_ref[...], b_ref[...])

    if split_n > 1:
        assert a.ndim == 2 and b.ndim == 2 and n % (split_n * 128) == 0
        tn = n // split_n
        g = split_n
        a_spec = pl.BlockSpec((tt, k), lambda gi, ti: (ti, 0))
        b_spec = pl.BlockSpec((tt, tn), lambda gi, ti: (ti, gi))
        o_spec = pl.BlockSpec((k, tn), lambda gi, ti: (0, gi))
        out_shape = jax.ShapeDtypeStruct((k, n), F32)
    else:
        g = a.shape[0] if a.ndim == 3 else b.shape[0]
        a_spec = (pl.BlockSpec((None, tt, k), lambda gi, ti: (gi, ti, 0)) if a.ndim == 3
                  else pl.BlockSpec((tt, k), lambda gi, ti: (ti, 0)))
        b_spec = (pl.BlockSpec((None, tt, n), lambda gi, ti: (gi, ti, 0)) if b.ndim == 3
                  else pl.BlockSpec((tt, n), lambda gi, ti: (ti, 0)))
        o_spec = pl.BlockSpec((None, k, n), lambda gi, ti: (gi, 0, 0))
        out_shape = jax.ShapeDtypeStruct((g, k, n), F32)
    return _pallas(body, name, (g, nt), [a_spec, b_spec], [o_spec], [out_shape], (a, b), ("arbitrary", "arbitrary"), carried)[0]


def _inproj_fwd(x1, gain, w_m, w_p, qa_gain, kva_gain, qh_gain, kh_gain, w_uq, w_uk, w_uv, w_uvt, cos, sin, carried=None):
    t, d = x1.shape
    tm = TOKEN_TILE

    def body(x_ref, g_ref, wm_ref, wp_ref, qa_ref, kva_ref, qh_ref, kh_ref, wuq_ref, wuk_ref, wuv_ref, wuvt_ref, cos_ref, sin_ref,
             hb_ref, big_ref, lat_ref, q_ref, k_ref, v_ref, vt_ref):
        h, _ = _rms(x_ref[...], g_ref[...])
        hb = _bf(h)
        hb_ref[...] = hb
        big_ref[:, :M_COLS] = _mm(hb, wm_ref[...])
        pp = _mm(hb, wp_ref[...])
        big_ref[:, M_COLS:] = pp[:, :2 * D_MODEL]
        lat = pp[:, 2 * D_MODEL:]
        lat_ref[...] = lat
        cq, _ = _rms(lat[:, :Q_LORA], qa_ref[...])
        ckv, _ = _rms(lat[:, Q_LORA:Q_LORA + KV_LORA], kva_ref[...])
        k_rope = lat[:, Q_LORA + KV_LORA:]
        cqb = _bf(cq)
        ckvb = _bf(ckv)
        cos_v = cos_ref[...]
        sin_v = sin_ref[...]
        for hd in range(N_HEADS):
            qn, _ = _rms(_mm(cqb, wuq_ref[hd]), qh_ref[...], QK_DIM)
            q_ref[hd] = _bf(_rope(qn, cos_v, sin_v))
            kn, _ = _rms(_mm(ckvb, wuk_ref[hd]) + k_rope, kh_ref[...], QK_DIM)
            k_ref[hd] = _bf(_rope(kn, cos_v, sin_v))
            v_ref[hd] = _bf(_mm(ckvb, wuv_ref[hd]))
            vt_ref[hd] = _bf(_mm_nt(wuvt_ref[hd], ckvb))

    tok = lambda c: pl.BlockSpec((tm, c), lambda i: (i, 0))
    head = lambda c: pl.BlockSpec((N_HEADS, tm, c), lambda i: (0, i, 0))
    return _pallas(
        body, "inproj_fwd", (t // tm,),
        [tok(d), _const((1, d)), _resident(w_m.shape), _resident(w_p.shape), _const((1, Q_LORA)), _const((1, KV_LORA)),
         _const((1, HEAD_PAD)), _const((1, HEAD_PAD)), _resident(w_uq.shape), _resident(w_uk.shape),
         _resident(w_uv.shape), _resident(w_uvt.shape), tok(HEAD_PAD), tok(HEAD_PAD)],
        [tok(d), tok(BIG_COLS), tok(LAT_COLS), head(HEAD_PAD), head(HEAD_PAD), head(V_DIM),
         pl.BlockSpec((N_HEADS, V_DIM, tm), lambda i: (0, 0, i))],
        [jax.ShapeDtypeStruct((t, d), BF16), jax.ShapeDtypeStruct((t, BIG_COLS), F32),
         jax.ShapeDtypeStruct((t, LAT_COLS), F32), jax.ShapeDtypeStruct((N_HEADS, t, HEAD_PAD), BF16),
         jax.ShapeDtypeStruct((N_HEADS, t, HEAD_PAD), BF16), jax.ShapeDtypeStruct((N_HEADS, t, V_DIM), BF16),
         jax.ShapeDtypeStruct((N_HEADS, V_DIM, t), BF16)],
        (x1, gain, w_m, w_p, qa_gain, kva_gain, qh_gain, kh_gain, w_uq, w_uk, w_uv, w_uvt, cos, sin), ("arbitrary",), carried)


EXP2_SCALE = ATTN_SCALE * 1.4426950408889634


def _diagonal_keep(tk, tq):
    return lax.broadcasted_iota(jnp.int32, (tk, tq), 0) <= lax.broadcasted_iota(jnp.int32, (tk, tq), 1)


def _attn_fwd(q, k, vt, seq):
    _, t, _ = q.shape
    nseq = t // seq
    tq = tk = ATTN_TILE
    nq = seq // tq

    def body(q_ref, k_ref, vt_ref, o_ref, lse_ref):
        i = pl.program_id(1)
        qs = [q_ref[h] for h in range(N_HEADS)]
        keep = _diagonal_keep(tk, tq)

        def tile(h, state, k0, diagonal):
            m, l, acc = state
            st = _mm_nt(k_ref[h, pl.ds(k0, tk), :], qs[h])
            if diagonal:
                st = jnp.where(keep, st, MASK_VALUE)
            m_new = jnp.maximum(m, jnp.max(st, axis=0, keepdims=True))
            pt = jnp.exp2((st - m_new) * EXP2_SCALE)
            alpha = jnp.exp2((m - m_new) * EXP2_SCALE)
            l_new = alpha * l + jnp.sum(pt, axis=0, keepdims=True)
            return m_new, l_new, alpha * acc + _mm(vt_ref[h, :, pl.ds(k0, tk)], _bf(pt))

        def step(j, states):
            k0 = pl.multiple_of(j * tk, tk)
            return tuple(tile(h, states[h], k0, False) for h in range(N_HEADS))

        init = tuple((jnp.full((1, tq), MASK_VALUE, F32), jnp.zeros((1, tq), F32), jnp.zeros((V_DIM, tq), F32))
                     for _ in range(N_HEADS))
        states = lax.fori_loop(0, i, step, init)
        k0 = pl.multiple_of(i * tk, tk)
        outs = []
        for h in range(N_HEADS):
            m, l, acc = tile(h, states[h], k0, True)
            outs.append((acc / l).T)
            lse_ref[h] = m * EXP2_SCALE + jnp.log2(l)
        o_ref[...] = _bf(jnp.concatenate(outs, axis=-1))

    return pl.pallas_call(
        body, name="attn_fwd", grid=(nseq, nq),
        in_specs=[pl.BlockSpec((N_HEADS, tq, HEAD_PAD), lambda b, i: (0, b * nq + i, 0)),
                  pl.BlockSpec((N_HEADS, seq, HEAD_PAD), lambda b, i: (0, b, 0)),
                  pl.BlockSpec((N_HEADS, V_DIM, seq), lambda b, i: (0, 0, b))],
        out_specs=[pl.BlockSpec((tq, N_HEADS * V_DIM), lambda b, i: (b * nq + i, 0)),
                   pl.BlockSpec((N_HEADS, 1, tq), lambda b, i: (0, 0, b * nq + i))],
        out_shape=[jax.ShapeDtypeStruct((t, N_HEADS * V_DIM), BF16), jax.ShapeDtypeStruct((N_HEADS, 1, t), F32)],
        compiler_params=_params(("arbitrary", "arbitrary")))(q, k, vt)


ATTN_BWD_HEADS = 4


def _attn_bwd(q, k, v, do, lse, delta, seq, carried=None):
    _, t, _ = q.shape
    nseq = t // seq
    tq = tk = ATTN_TILE
    n = seq // tq
    hb = ATTN_BWD_HEADS

    def body(q_ref, k_ref, v_ref, do_ref, lse_ref, delta_ref, dq_ref, dk_ref, dv_ref):
        dq_ref[...] = jnp.zeros_like(dq_ref)
        dk_ref[...] = jnp.zeros_like(dk_ref)
        dv_ref[...] = jnp.zeros_like(dv_ref)
        keep = _diagonal_keep(tk, tq)

        def tile(h, k0, q0, diagonal):
            kj = k_ref[h, pl.ds(k0, tk), :]
            qi = q_ref[h, pl.ds(q0, tq), :]
            doi = _bf(do_ref[pl.ds(q0, tq), h * V_DIM:(h + 1) * V_DIM])
            st = _mm_nt(kj, qi)
            if diagonal:
                st = jnp.where(keep, st, MASK_VALUE)
            pt = jnp.exp2(st * EXP2_SCALE - lse_ref[h, :, pl.ds(q0, tq)])
            dv_ref[h, pl.ds(k0, tk), :] += _mm(_bf(pt), doi)
            dpt = _mm_nt(v_ref[h, pl.ds(k0, tk), :], doi)
            dst = _bf((pt * (dpt - delta_ref[pl.ds(h, 1), pl.ds(q0, tq)])) * ATTN_SCALE)
            dk_ref[h, pl.ds(k0, tk), :] += _mm(dst, qi)
            dq_ref[h, pl.ds(q0, tq), :] += _mm_tn(dst, kj)

        def kv_step(j, _):
            k0 = pl.multiple_of(j * tk, tk)
            for h in range(hb):
                tile(h, k0, k0, True)

            def q_step(i, _):
                q0 = pl.multiple_of(i * tq, tq)
                for h in range(hb):
                    tile(h, k0, q0, False)
                return 0

            lax.fori_loop(j + 1, n, q_step, 0)
            return 0

        lax.fori_loop(0, n, kv_step, 0)

    hspec = lambda c: pl.BlockSpec((hb, seq, c), lambda b, g: (g, b, 0))
    return _pallas(
        body, "attn_bwd", (nseq, N_HEADS // hb),
        [hspec(HEAD_PAD), hspec(HEAD_PAD), hspec(V_DIM), pl.BlockSpec((seq, hb * V_DIM), lambda b, g: (b, g)),
         pl.BlockSpec((hb, 1, seq), lambda b, g: (g, 0, b)), pl.BlockSpec((None, hb, seq), lambda b, g: (g, 0, b))],
        [hspec(HEAD_PAD), hspec(HEAD_PAD), hspec(V_DIM)],
        [jax.ShapeDtypeStruct((N_HEADS, t, HEAD_PAD), F32), jax.ShapeDtypeStruct((N_HEADS, t, HEAD_PAD), F32),
         jax.ShapeDtypeStruct((N_HEADS, t, V_DIM), F32)],
        (q, k, v, do, lse, delta), ("arbitrary", "arbitrary"), carried)


def _mixer_values(o_ref, gb_ref, gla_ref, glb_ref, xc_ref, gc_ref, xcp_ref, gcp_ref, bias_ref, cw_ref, wpa_ref, wpc_ref,
                  first_of_seq):
    gb = gb_ref[...]
    u = gc_ref[...] * xc_ref[...]
    u_prev = jnp.where(first_of_seq, 0.0, gcp_ref[...] * xcp_ref[...])
    cw = cw_ref[...]
    z = cw[2:3] * u + cw[1:2] * _shift_down(u, u_prev, 1) + cw[0:1] * _shift_down(u, u_prev, 2)
    gbz = _bf(gb * z)
    y_b = _mm(gbz, wpc_ref[...])
    y_a = _mm(o_ref[...], wpa_ref[...])
    bias = bias_ref[...]
    gate_a = _sigmoid(gla_ref[...] + bias[:, :D_MODEL])
    gate_b = _sigmoid(glb_ref[...] + bias[:, D_MODEL:])
    merged = _bf(gate_a * y_a + gate_b * y_b)
    return gb, u, z, gbz, y_a, y_b, gate_a, gate_b, merged


def _mixer_specs(tm, seq):
    d = D_MODEL
    tok = pl.BlockSpec((tm, d), lambda i: (i, 0))
    col = lambda c: pl.BlockSpec((tm, d), lambda i: (i, c))
    prev = lambda c: pl.BlockSpec((8, d), lambda i: (jnp.maximum(i * (tm // 8) - 1, 0), c))
    o_spec = pl.BlockSpec((tm, N_HEADS * V_DIM), lambda i: (i, 0))
    fwd_specs = [o_spec, col(0), col(1), col(2), col(3), col(4), prev(3), prev(4), _const((1, 2 * d)), _const((3, d)),
                 _resident((N_HEADS * V_DIM, d)), _resident((d, d)), _resident((d, d))]
    return tok, fwd_specs


def _mix_fwd(x1, o, big, gate_bias, conv_w, w_pa, w_pc, w_out, seq):
    t, d = x1.shape
    tm = TOKEN_TILE
    tiles_per_seq = seq // tm

    def body(x_ref, o_ref, gb_ref, gla_ref, glb_ref, xc_ref, gc_ref, xcp_ref, gcp_ref, bias_ref, cw_ref, wpa_ref, wpc_ref,
             wout_ref, x2_ref):
        first = pl.program_id(0) % tiles_per_seq == 0
        merged = _mixer_values(o_ref, gb_ref, gla_ref, glb_ref, xc_ref, gc_ref, xcp_ref, gcp_ref, bias_ref, cw_ref, wpa_ref,
                               wpc_ref, first)[-1]
        x2_ref[...] = x_ref[...] + _mm(merged, wout_ref[...])

    tok, fwd_specs = _mixer_specs(tm, seq)
    return pl.pallas_call(
        body, name="mix_fwd", grid=(t // tm,), in_specs=[tok] + fwd_specs, out_specs=tok,
        out_shape=jax.ShapeDtypeStruct((t, d), F32),
        compiler_params=_params(("arbitrary",)))(x1, o, big, big, big, big, big, big, big, gate_bias, conv_w, w_pa, w_pc, w_out)


def _mix_bwd(dx2, o, big, gate_bias, conv_w, w_pa, w_pc, w_out, seq, carried=None):
    t, d = dx2.shape
    tm = TOKEN_TILE
    tiles_per_seq = seq // tm
    hv = N_HEADS * V_DIM

    def body(dx_ref, o_ref, gb_ref, gla_ref, glb_ref, xc_ref, gc_ref, xcp_ref, gcp_ref, bias_ref, cw_ref, wpa_ref, wpc_ref,
             wout_ref, do_ref, delta_ref, dz_ref, dm_ref, dbias_ref, dwpa_ref, dwpc_ref, dwout_ref):
        first = pl.program_id(0) % tiles_per_seq == 0
        gb, _, z, gbz, y_a, y_b, gate_a, gate_b, merged = _mixer_values(
            o_ref, gb_ref, gla_ref, glb_ref, xc_ref, gc_ref, xcp_ref, gcp_ref, bias_ref, cw_ref, wpa_ref, wpc_ref, first)

        @pl.when(pl.program_id(0) == 0)
        def _():
            dbias_ref[...] = jnp.zeros_like(dbias_ref)
            dwpa_ref[...] = jnp.zeros_like(dwpa_ref)
            dwpc_ref[...] = jnp.zeros_like(dwpc_ref)
            dwout_ref[...] = jnp.zeros_like(dwout_ref)

        dxb = _bf(dx_ref[...])
        dmerged = _mm_nt(dxb, wout_ref[...])
        dwout_ref[...] += _mm_tn(merged, dxb)
        dla = (dmerged * y_a) * (gate_a * (1.0 - gate_a))
        dlb = (dmerged * y_b) * (gate_b * (1.0 - gate_b))
        dbias_ref[:, :d] += jnp.sum(dla, axis=0, keepdims=True)
        dbias_ref[:, d:] += jnp.sum(dlb, axis=0, keepdims=True)
        dya = _bf(dmerged * gate_a)
        dyb = _bf(dmerged * gate_b)
        do_v = _mm_nt(dya, wpa_ref[...])
        do_ref[...] = do_v
        head = lax.broadcasted_iota(jnp.int32, (N_HEADS, hv), 0) * V_DIM
        col = lax.broadcasted_iota(jnp.int32, (N_HEADS, hv), 1)
        in_head = ((col >= head) & (col < head + V_DIM)).astype(F32)
        delta_ref[...] = lax.dot_general(in_head, do_v * o_ref[...].astype(F32), (((1,), (1,)), ((), ())),
                                         precision=lax.Precision.HIGHEST, preferred_element_type=F32)
        dwpa_ref[...] += _mm_tn(o_ref[...], dya)
        dgz = _mm_nt(dyb, wpc_ref[...])
        dwpc_ref[...] += _mm_tn(gbz, dyb)
        dz_ref[...] = dgz * gb
        dm_ref[:, :d] = _bf(dgz * z)
        dm_ref[:, d:2 * d] = _bf(dla)
        dm_ref[:, 2 * d:] = _bf(dlb)

    tok, fwd_specs = _mixer_specs(tm, seq)
    return _pallas(
        body, "mix_bwd", (t // tm,), [tok] + fwd_specs,
        [pl.BlockSpec((tm, hv), lambda i: (i, 0)), pl.BlockSpec((N_HEADS, tm), lambda i: (0, i)), tok,
         pl.BlockSpec((tm, M_COLS), lambda i: (i, 0)), _const((1, 2 * d)), _const((hv, d)), _const((d, d)), _const((d, d))],
        [jax.ShapeDtypeStruct((t, hv), F32), jax.ShapeDtypeStruct((N_HEADS, t), F32), jax.ShapeDtypeStruct((t, d), F32),
         jax.ShapeDtypeStruct((t, M_COLS), BF16), jax.ShapeDtypeStruct((1, 2 * d), F32), jax.ShapeDtypeStruct((hv, d), F32),
         jax.ShapeDtypeStruct((d, d), F32), jax.ShapeDtypeStruct((d, d), F32)],
        (dx2, o, big, big, big, big, big, big, big, gate_bias, conv_w, w_pa, w_pc, w_out), ("arbitrary",), carried)


def _prep_bwd(lat, big, dz, dq, dk, dv, qa_gain, kva_gain, qh_gain, kh_gain, w_uq, w_uk, w_uv, cos, sin, conv_w, seq, carried=None):
    t = lat.shape[0]
    d = D_MODEL
    tm = TOKEN_TILE
    tiles_per_seq = seq // tm
    last_blk = t // 8 - 1

    def body(lat_ref, xc_ref, gc_ref, dz_ref, dzn_ref, dq_ref, dk_ref, dv_ref, qa_ref, kva_ref, qh_ref, kh_ref, wuq_ref, wuk_ref,
             wuv_ref, cos_ref, sin_ref, cw_ref,
             dp_ref, dwuq_ref, dwuk_ref, dwuv_ref, dqa_ref, dkva_ref, dqh_ref, dkh_ref, dcw_ref):
        pid = pl.program_id(0)

        @pl.when(pid == 0)
        def _():
            for r in (dwuq_ref, dwuk_ref, dwuv_ref, dqa_ref, dkva_ref, dqh_ref, dkh_ref, dcw_ref):
                r[...] = jnp.zeros_like(r)

        last = pid % tiles_per_seq == tiles_per_seq - 1
        dzv = dz_ref[...]
        dz_next = jnp.where(last, 0.0, dzn_ref[...])
        dz1 = _shift_up(dzv, dz_next, 1)
        dz2 = _shift_up(dzv, dz_next, 2)
        cw = cw_ref[...]
        xc = xc_ref[...]
        gc = gc_ref[...]
        u = gc * xc
        du = cw[2:3] * dzv + cw[1:2] * dz1 + cw[0:1] * dz2
        dp_ref[:, :d] = _bf(du * gc)
        dp_ref[:, d:2 * d] = _bf(du * xc)
        dcw_ref[0:1, :] += jnp.sum(dz2 * u, axis=0, keepdims=True)
        dcw_ref[1:2, :] += jnp.sum(dz1 * u, axis=0, keepdims=True)
        dcw_ref[2:3, :] += jnp.sum(dzv * u, axis=0, keepdims=True)

        lat_v = lat_ref[...]
        q_lat = lat_v[:, :Q_LORA]
        kv_lat = lat_v[:, Q_LORA:Q_LORA + KV_LORA]
        k_rope = lat_v[:, Q_LORA + KV_LORA:]
        qa_gain_v = qa_ref[...]
        kva_gain_v = kva_ref[...]
        qh_gain_v = qh_ref[...]
        kh_gain_v = kh_ref[...]
        cq, rq = _rms(q_lat, qa_gain_v)
        ckv, rkv = _rms(kv_lat, kva_gain_v)
        cqb = _bf(cq)
        ckvb = _bf(ckv)
        cos_v = cos_ref[...]
        sin_v = sin_ref[...]
        lane = lax.broadcasted_iota(jnp.int32, (tm, HEAD_PAD), 1)
        rope_lanes = (lane >= QK_NOPE) & (lane < QK_DIM)
        dcq = jnp.zeros((tm, Q_LORA), F32)
        dckv = jnp.zeros((tm, KV_LORA), F32)
        dk_rope = jnp.zeros((tm, HEAD_PAD), F32)
        dqh_gain = jnp.zeros((1, HEAD_PAD), F32)
        dkh_gain = jnp.zeros((1, HEAD_PAD), F32)
        for hd in range(N_HEADS):
            q_pre = _mm(cqb, wuq_ref[hd])
            _, rr = _rms(q_pre, qh_gain_v, QK_DIM)
            dq_pre, dg = _rms_bwd(q_pre, rr, qh_gain_v, _rope_bwd(dq_ref[hd], cos_v, sin_v), QK_DIM)
            dqh_gain = dqh_gain + dg
            dq_pre_b = _bf(dq_pre)
            dcq = dcq + _mm_nt(dq_pre_b, wuq_ref[hd])
            dwuq_ref[hd] += _mm_tn(cqb, dq_pre_b)

            k_pre = _mm(ckvb, wuk_ref[hd]) + k_rope
            _, rr = _rms(k_pre, kh_gain_v, QK_DIM)
            dk_pre, dg = _rms_bwd(k_pre, rr, kh_gain_v, _rope_bwd(dk_ref[hd], cos_v, sin_v), QK_DIM)
            dkh_gain = dkh_gain + dg
            dk_rope = dk_rope + jnp.where(rope_lanes, dk_pre, 0.0)
            dk_pre_b = _bf(dk_pre)
            dvb = _bf(dv_ref[hd])
            dckv = dckv + _mm_nt(dk_pre_b, wuk_ref[hd]) + _mm_nt(dvb, wuv_ref[hd])
            dwuk_ref[hd] += _mm_tn(ckvb, dk_pre_b)
            dwuv_ref[hd] += _mm_tn(ckvb, dvb)
        dqh_ref[...] += dqh_gain
        dkh_ref[...] += dkh_gain
        dq_lat, dg = _rms_bwd(q_lat, rq, qa_gain_v, dcq)
        dqa_ref[...] += dg
        dkv_lat, dg = _rms_bwd(kv_lat, rkv, kva_gain_v, dckv)
        dkva_ref[...] += dg
        dp_ref[:, 2 * d:2 * d + Q_LORA] = _bf(dq_lat)
        dp_ref[:, 2 * d + Q_LORA:2 * d + Q_LORA + KV_LORA] = _bf(dkv_lat)
        dp_ref[:, 2 * d + Q_LORA + KV_LORA:] = _bf(dk_rope)

    tok = lambda c: pl.BlockSpec((tm, c), lambda i: (i, 0))
    col = lambda c: pl.BlockSpec((tm, d), lambda i: (i, c))
    head = lambda c: pl.BlockSpec((N_HEADS, tm, c), lambda i: (0, i, 0))
    nxt = pl.BlockSpec((8, d), lambda i: (jnp.minimum((i + 1) * (tm // 8), last_blk), 0))
    return _pallas(
        body, "prep_bwd", (t // tm,),
        [tok(LAT_COLS), col(3), col(4), tok(d), nxt, head(HEAD_PAD), head(HEAD_PAD), head(V_DIM),
         _const((1, Q_LORA)), _const((1, KV_LORA)), _const((1, HEAD_PAD)), _const((1, HEAD_PAD)),
         _resident(w_uq.shape), _resident(w_uk.shape), _resident(w_uv.shape), tok(HEAD_PAD), tok(HEAD_PAD), _const((3, d))],
        [tok(P_COLS), _const(w_uq.shape), _const(w_uk.shape), _const(w_uv.shape), _const((1, Q_LORA)),
         _const((1, KV_LORA)), _const((1, HEAD_PAD)), _const((1, HEAD_PAD)), _const((3, d))],
        [jax.ShapeDtypeStruct((t, P_COLS), BF16), jax.ShapeDtypeStruct(w_uq.shape, F32),
         jax.ShapeDtypeStruct(w_uk.shape, F32), jax.ShapeDtypeStruct(w_uv.shape, F32),
         jax.ShapeDtypeStruct((1, Q_LORA), F32), jax.ShapeDtypeStruct((1, KV_LORA), F32),
         jax.ShapeDtypeStruct((1, HEAD_PAD), F32), jax.ShapeDtypeStruct((1, HEAD_PAD), F32), jax.ShapeDtypeStruct((3, d), F32)],
        (lat, big, big, dz, dz, dq, dk, dv, qa_gain, kva_gain, qh_gain, kh_gain, w_uq, w_uk, w_uv, cos, sin, conv_w),
        ("arbitrary",), carried)


def _inproj_bwd(x1, gain, dx2, dm, dp, w_m, w_p):
    t, d = x1.shape
    tm = TOKEN_TILE

    def body(x_ref, g_ref, dx2_ref, dm_ref, dp_ref, wm_ref, wp_ref, dx1_ref, dgain_ref):
        xv = x_ref[...]
        gain_v = g_ref[...]
        _, r = _rms(xv, gain_v)
        dh = _mm_nt(dm_ref[...], wm_ref[...]) + _mm_nt(dp_ref[...], wp_ref[...])
        dxn, dgain = _rms_bwd(xv, r, gain_v, dh)
        dx1_ref[...] = dx2_ref[...] + dxn

        @pl.when(pl.program_id(0) == 0)
        def _():
            dgain_ref[...] = jnp.zeros_like(dgain_ref)

        dgain_ref[...] += dgain

    tok = lambda c: pl.BlockSpec((tm, c), lambda i: (i, 0))
    return pl.pallas_call(
        body, name="inproj_bwd", grid=(t // tm,),
        in_specs=[tok(d), _const((1, d)), tok(d), tok(M_COLS), tok(P_COLS), _resident(w_m.shape), _resident(w_p.shape)],
        out_specs=[tok(d), _const((1, d))],
        out_shape=[jax.ShapeDtypeStruct((t, d), F32), jax.ShapeDtypeStruct((1, d), F32)],
        compiler_params=_params(("arbitrary",)))(x1, gain, dx2, dm, dp, w_m, w_p)


def _adamw(w, g, m, v, name):
    rows, cols = w.shape
    tr = rows
    for cand in (512, 256, 128, 64, 32, 16, 8):
        if rows % cand == 0 and rows > cand:
            tr = cand
            break

    def body(w_ref, g_ref, m_ref, v_ref, delta_ref, nm_ref, nv_ref):
        gv = g_ref[...]
        nm = ADAM_B1 * m_ref[...] + (1.0 - ADAM_B1) * gv
        nv = ADAM_B2 * v_ref[...] + (1.0 - ADAM_B2) * (gv * gv)
        m_hat = nm * (1.0 / (1.0 - ADAM_B1 ** ADAM_STEP))
        v_hat = nv * (1.0 / (1.0 - ADAM_B2 ** ADAM_STEP))
        delta_ref[...] = -ADAM_LR * (m_hat / (jnp.sqrt(v_hat) + ADAM_EPS) + ADAM_WD * w_ref[...])
        nm_ref[...] = nm
        nv_ref[...] = nv

    spec = pl.BlockSpec((tr, cols), lambda i: (i, 0))
    shape = jax.ShapeDtypeStruct((rows, cols), F32)
    return pl.pallas_call(body, name=name, grid=(rows // tr,), in_specs=[spec] * 4, out_specs=[spec] * 3, out_shape=[shape] * 3,
                          compiler_params=_params(("arbitrary",)))(w, g, m, v)


def _place():
    x, y, c = lax.axis_index("x"), lax.axis_index("y"), lax.axis_index("c")
    other_chips = [(1 - x, y), (x, 1 - y), (1 - x, 1 - y)]
    return x, y, c, other_chips


def _remote(src, dst, sems, send, recv, device):
    return pltpu.make_async_remote_copy(src_ref=src, dst_ref=dst, send_sem=sems.at[send], recv_sem=sems.at[recv],
                                        device_id=device, device_id_type=MESH_ID)


def _cast_shards(shards, out_dtypes):
    n = len(shards)

    def body(*refs):
        ins, outs, stage, sems = refs[:n], refs[n:2 * n], refs[2 * n:3 * n], refs[3 * n]
        x, y, _, _ = _place()
        me = 2 * x + y
        copies = []
        for w in range(n):
            stage[w][...] = ins[w][...].astype(out_dtypes[w])
            copies.append(pltpu.make_async_copy(stage[w], outs[w].at[me], sems.at[w]))
            copies[-1].start()
        for cp in copies:
            cp.wait()

    vm = pl.BlockSpec(memory_space=pltpu.VMEM)
    return pl.pallas_call(
        body, name="cast_shards", in_specs=[vm] * n, out_specs=[ANY] * n,
        out_shape=[jax.ShapeDtypeStruct((N_CHIPS,) + s.shape, dt) for s, dt in zip(shards, out_dtypes)],
        scratch_shapes=[pltpu.VMEM(s.shape, dt) for s, dt in zip(shards, out_dtypes)] + [pltpu.SemaphoreType.DMA((n,))],
        compiler_params=_params())(*shards)


def _gather_carried(bufs):
    n = len(bufs)

    def half_rows(w, c):
        half = bufs[w].shape[1] // 2
        return pl.ds(pl.multiple_of(c * half, 8), half)

    def start(ins, outs, sems, base):
        x, y, c, other_chips = _place()
        me = 2 * x + y
        for w in range(n):
            mine = outs[w].at[me, half_rows(w, c)]
            for p, (px, py) in enumerate(other_chips):
                _remote(mine, mine, sems, base + 12 * w + p, base + 12 * w + 3 + p, (px, py, c)).start()

    def finish(ins, outs, sems, base):
        x, y, c, other_chips = _place()
        me = 2 * x + y
        for w in range(n):
            for p, (px, py) in enumerate(other_chips):
                got = outs[w].at[2 * px + py, half_rows(w, c)]
                _remote(got, got, sems, base + 12 * w + p, base + 12 * w + 3 + p, (px, py, c)).wait_recv()
                _remote(got, got, sems, base + 12 * w + 6 + p, base + 12 * w + 9 + p, (x, y, 1 - c)).start()
        for w in range(n):
            mine = outs[w].at[me, half_rows(w, c)]
            for p, (px, py) in enumerate(other_chips):
                got = outs[w].at[2 * px + py, half_rows(w, c)]
                theirs = outs[w].at[2 * px + py, half_rows(w, 1 - c)]
                _remote(got, theirs, sems, base + 12 * w + 6 + p, base + 12 * w + 9 + p, (x, y, 1 - c)).wait()
                _remote(mine, mine, sems, base + 12 * w + p, base + 12 * w + 3 + p, (px, py, c)).wait_send()

    shapes = [jax.ShapeDtypeStruct(b.shape, b.dtype) for b in bufs]
    return _Carried(bufs, shapes, {w: w for w in range(n)}, 12 * n, start, finish)


def _swap_carried(grads):
    n = len(grads)

    def copy(w, ins, outs, sems, base):
        x, y, c, _ = _place()
        half = grads[w].shape[1] // 2
        rows = pl.ds(pl.multiple_of((1 - c) * half, 8), half)
        return _remote(ins[w].at[:, rows, :], outs[w], sems, base + 2 * w, base + 2 * w + 1, (x, y, 1 - c))

    def start(ins, outs, sems, base):
        for w in range(n):
            copy(w, ins, outs, sems, base).start()

    def finish(ins, outs, sems, base):
        for w in range(n):
            copy(w, ins, outs, sems, base).wait()

    shapes = [jax.ShapeDtypeStruct((g.shape[0], g.shape[1] // 2, g.shape[2]), F32) for g in grads]
    return _Carried(grads, shapes, {}, 2 * n, start, finish)


def _row_tile(rows):
    for cand in (256, 176, 128, 96, 64, 32, 16):
        if rows % cand == 0:
            return cand
    return rows


def _chip_partial(grad, other, place, name):
    nblk, half, cols = other.shape
    tr = _row_tile(half)
    per_half = half // tr

    def body(place_ref, g_ref, o_ref, sum_ref, sum_bf_ref):
        s = g_ref[...] + o_ref[...]
        sum_ref[...] = s
        sum_bf_ref[...] = _bf(s)

    grid_spec = pltpu.PrefetchScalarGridSpec(
        num_scalar_prefetch=1, grid=(nblk, per_half),
        in_specs=[pl.BlockSpec((None, tr, cols), lambda b, i, place_ref: (b, place_ref[1] * per_half + i, 0)),
                  pl.BlockSpec((None, tr, cols), lambda b, i, place_ref: (b, i, 0))],
        out_specs=[pl.BlockSpec((None, tr, cols), lambda b, i, place_ref: (b, i, 0))] * 2)
    return pl.pallas_call(body, name=name, grid_spec=grid_spec,
                          out_shape=[jax.ShapeDtypeStruct(other.shape, F32), jax.ShapeDtypeStruct(other.shape, BF16)],
                          compiler_params=_params(("arbitrary", "arbitrary")))(place, grad, other)


def _send_carried(partials):
    n = len(partials)

    def start(ins, outs, sems, base):
        x, y, c, other_chips = _place()
        me = 2 * x + y
        for w in range(n):
            for p, (px, py) in enumerate(other_chips):
                _remote(ins[w].at[2 * px + py], outs[w].at[me], sems, base + 6 * w + p, base + 6 * w + 3 + p, (px, py, c)).start()

    def finish(ins, outs, sems, base):
        x, y, c, other_chips = _place()
        for w in range(n):
            for p, (px, py) in enumerate(other_chips):
                _remote(ins[w].at[2 * px + py], outs[w].at[2 * px + py], sems, base + 6 * w + p, base + 6 * w + 3 + p,
                        (px, py, c)).wait()

    return _Carried(partials, [jax.ShapeDtypeStruct(p.shape, BF16) for p in partials], {}, 6 * n, start, finish)


def _chip_total(own, received, place, name):
    nblk, half, cols = own.shape
    tr = _row_tile(half)
    per_half = half // tr

    def body(place_ref, own_ref, r1_ref, r2_ref, r3_ref, out_ref):
        out_ref[...] = own_ref[...] + ((r1_ref[...].astype(F32) + r2_ref[...].astype(F32)) + r3_ref[...].astype(F32))

    def slot(k):
        return pl.BlockSpec((None, tr, cols), lambda i, place_ref: ((place_ref[0] + k) % N_CHIPS, i, 0))

    grid_spec = pltpu.PrefetchScalarGridSpec(
        num_scalar_prefetch=1, grid=(per_half,), in_specs=[slot(0), slot(1), slot(2), slot(3)],
        out_specs=pl.BlockSpec((tr, cols), lambda i, place_ref: (place_ref[1] * per_half + i, 0)))
    return pl.pallas_call(body, name=name, grid_spec=grid_spec, out_shape=jax.ShapeDtypeStruct((2 * half, cols), F32),
                          compiler_params=_params(("arbitrary",)))(place, own, received, received, received)


def _join_carried(totals):
    n = len(totals)

    def copy(w, outs, sems, base):
        x, y, c, _ = _place()
        half = totals[w].shape[0] // 2
        mine = outs[w].at[pl.ds(pl.multiple_of(c * half, 8), half), :]
        return _remote(mine, mine, sems, base + 2 * w, base + 2 * w + 1, (x, y, 1 - c))

    def start(ins, outs, sems, base):
        for w in range(n):
            copy(w, outs, sems, base).start()

    def finish(ins, outs, sems, base):
        for w in range(n):
            copy(w, outs, sems, base).wait()

    shapes = [jax.ShapeDtypeStruct(a.shape, F32) for a in totals]
    return _Carried(totals, shapes, {w: w for w in range(n)}, 2 * n, start, finish)


def _sum_devices(vec):
    rows, n = vec.shape

    def body(v_ref, out_ref, buf, send_sems, recv_sems):
        x, y, c, _ = _place()
        me = 4 * x + 2 * y + c
        buf[me] = v_ref[...]
        sends = []
        for k in range(1, N_DEV):
            peer = (1 - x if k & 4 else x, 1 - y if k & 2 else y, 1 - c if k & 1 else c)
            cp = pltpu.make_async_remote_copy(src_ref=v_ref, dst_ref=buf.at[me], send_sem=send_sems.at[k], recv_sem=recv_sems.at[k],
                                              device_id=peer, device_id_type=MESH_ID)
            cp.start()
            sends.append(cp)
        for cp in sends:
            cp.wait()
        total = buf[0]
        for dev in range(1, N_DEV):
            total = total + buf[dev]
        out_ref[...] = total

    vm = pl.BlockSpec(memory_space=pltpu.VMEM)
    return pl.pallas_call(
        body, name="sum_devices", in_specs=[vm], out_specs=vm, out_shape=jax.ShapeDtypeStruct((rows, n), F32),
        scratch_shapes=[pltpu.VMEM((N_DEV, rows, n), F32), pltpu.SemaphoreType.DMA((N_DEV,)), pltpu.SemaphoreType.DMA((N_DEV,))],
    )(vec)


def _rope_tables(positions):
    half = QK_ROPE // 2
    inv_freq = 1.0 / (ROPE_THETA ** (jnp.arange(half, dtype=F32) / half))
    ang = positions.astype(F32).reshape(-1, 1) * inv_freq
    cos, sin = jnp.cos(ang), jnp.sin(ang)
    t = ang.shape[0]
    ones, zeros = jnp.ones((t, QK_NOPE), F32), jnp.zeros((t, QK_NOPE), F32)
    pad = HEAD_PAD - QK_DIM
    cos_full = jnp.concatenate([ones, cos, cos, ones[:, :pad]], axis=1)
    sin_signed = jnp.concatenate([zeros, -sin, sin, zeros[:, :pad]], axis=1)
    return cos_full, sin_signed


def _partials(names, grads, from_sibling, place):
    return [_chip_partial(g, o, place, "chip_partial_" + n) for n, g, o in zip(names, grads, from_sibling)]


def _totals(names, partials, received, place):
    return [_chip_total(pf, r, place, "chip_total_" + n) for n, (pf, _), r in zip(names, partials, received)]


def _kernel_layouts(full):
    d = D_MODEL
    w_in = full["w_in"]
    o_q, o_kv, o_kr, o_xc, o_gb, o_gc, o_gl = 0, Q_LORA, Q_LORA + KV_LORA, Q_LORA + KV_LORA + QK_ROPE, 0, 0, 0
    o_gb = o_xc + d
    o_gc = o_gb + d
    o_gl = o_gc + d
    k_rope_pad = jnp.pad(w_in[:, o_kr:o_xc], ((0, 0), (QK_NOPE, HEAD_PAD - QK_DIM)))
    w_m = jnp.concatenate([w_in[:, o_gb:o_gc], w_in[:, o_gl:]], axis=1)
    w_p = jnp.concatenate([w_in[:, o_xc:o_gb], w_in[:, o_gc:o_gl], w_in[:, o_q:o_kr], k_rope_pad], axis=1)
    w_uq = jnp.pad(full["w_uq"].reshape(Q_LORA, N_HEADS, QK_DIM), ((0, 0), (0, 0), (0, HEAD_PAD - QK_DIM))).transpose(1, 0, 2)
    w_uk = jnp.pad(full["w_uk"].reshape(KV_LORA, N_HEADS, QK_NOPE), ((0, 0), (0, 0), (0, HEAD_PAD - QK_NOPE))).transpose(1, 0, 2)
    w_uv = full["w_uv"].reshape(KV_LORA, N_HEADS, V_DIM).transpose(1, 0, 2)
    return {"w_m": w_m, "w_p": w_p, "w_uq": w_uq, "w_uk": w_uk, "w_uv": w_uv, "w_uvt": w_uv.transpose(0, 2, 1)}


def _global_layouts(g):
    d = D_MODEL
    dw = jnp.concatenate([g["w_m"], g["w_p"]], axis=1)
    o_xc, o_gc, o_lat = M_COLS, M_COLS + d, M_COLS + 2 * d
    o_kr = o_lat + Q_LORA + KV_LORA + QK_NOPE
    w_in = jnp.concatenate([dw[:, o_lat:o_lat + Q_LORA + KV_LORA], dw[:, o_kr:o_kr + QK_ROPE], dw[:, o_xc:o_gc], dw[:, :d],
                            dw[:, o_gc:o_lat], dw[:, d:M_COLS]], axis=1)
    w_uq = g["w_uq"][:, :, :QK_DIM].transpose(1, 0, 2).reshape(Q_LORA, N_HEADS * QK_DIM)
    w_uk = g["w_uk"][:, :, :QK_NOPE].transpose(1, 0, 2).reshape(KV_LORA, N_HEADS * QK_NOPE)
    w_uv = g["w_uv"].transpose(1, 0, 2).reshape(KV_LORA, N_HEADS * V_DIM)
    return {"w_in": w_in, "w_uq": w_uq, "w_uk": w_uk, "w_uv": w_uv, "w_proj_attn": g["w_pa"], "w_proj_conv": g["w_pc"],
            "w_out": g["w_out"]}


def _col_blocks(a):
    r, c = a.shape
    return a.reshape(r, N_CHIPS, c // N_CHIPS).transpose(1, 0, 2)


def _from_col_blocks(a):
    n, r, c = a.shape
    return a.transpose(1, 0, 2).reshape(r, n * c)


COL_SHARDED = ("w_in", "w_uq", "w_uk", "w_uv", "w_proj_attn")
ROW_SHARDED = ("w_proj_conv", "w_out")
SMALL = (("ffn1_norm", 1024), ("mix_norm", 1024), ("gate_bias", 2048), ("q_a_norm", 384), ("kv_a_norm", 256),
         ("q_head_norm", 128), ("k_head_norm", 128), ("ffn2_norm", 1024))
WEIGHT_ORDER = ("ffn1_norm", "ffn1_w_gate", "ffn1_w_up", "ffn1_w_down", "mix_norm", "w_in", "gate_bias", "q_a_norm", "w_uq",
                "kv_a_norm", "w_uk", "w_uv", "q_head_norm", "k_head_norm", "w_proj_attn", "conv_w", "w_proj_conv", "w_out",
                "ffn2_norm", "ffn2_w_gate", "ffn2_w_up", "ffn2_w_down")
MATRICES = ("ffn1_w_gate", "ffn1_w_up", "ffn1_w_down", "w_in", "w_uq", "w_uk", "w_uv", "w_proj_attn", "w_proj_conv", "w_out",
            "ffn2_w_gate", "ffn2_w_up", "ffn2_w_down")
GROUP_FFN1 = ("ffn1_w_gate", "ffn1_w_up", "ffn1_w_down")
GROUP_IN = ("w_in", "w_uq", "w_uk", "w_uv", "conv_w")
GROUP_MIX = ("w_proj_attn", "w_proj_conv", "w_out")
GROUP_FFN2 = ("ffn2_w_gate", "ffn2_w_up", "ffn2_w_down")
GROUP_MID = ("w_in", "w_uq", "w_uk", "w_uv", "w_proj_attn", "w_proj_conv", "w_out")


def _pad_lanes(a, n):
    return jnp.pad(a.reshape(1, -1), ((0, 0), (0, n - a.size)))


def kernel(x, positions, ffn1_norm, ffn1_w_gate, ffn1_w_up, ffn1_w_down, mix_norm, w_in, gate_bias, q_a_norm, w_uq, kv_a_norm, w_uk, w_uv, q_head_norm, k_head_norm, w_proj_attn, conv_w, w_proj_conv, w_out, ffn2_norm, ffn2_w_gate, ffn2_w_up, ffn2_w_down, loss_target, m_ffn1_norm, m_ffn1_w_gate, m_ffn1_w_up, m_ffn1_w_down, m_mix_norm, m_w_in, m_gate_bias, m_q_a_norm, m_w_uq, m_kv_a_norm, m_w_uk, m_w_uv, m_q_head_norm, m_k_head_norm, m_w_proj_attn, m_conv_w, m_w_proj_conv, m_w_out, m_ffn2_norm, m_ffn2_w_gate, m_ffn2_w_up, m_ffn2_w_down, v_ffn1_norm, v_ffn1_w_gate, v_ffn1_w_up, v_ffn1_w_down, v_mix_norm, v_w_in, v_gate_bias, v_q_a_norm, v_w_uq, v_kv_a_norm, v_w_uk, v_w_uv, v_q_head_norm, v_k_head_norm, v_w_proj_attn, v_conv_w, v_w_proj_conv, v_w_out, v_ffn2_norm, v_ffn2_w_gate, v_ffn2_w_up, v_ffn2_w_down):
    args = dict(locals())
    weights = {n: args[n] for n in WEIGHT_ORDER}
    moments_m = {n: args["m_" + n] for n in WEIGHT_ORDER}
    moments_v = {n: args["v_" + n] for n in WEIGHT_ORDER}
    nb, seq, d = x.shape
    t = nb * seq
    chip = (2 * lax.axis_index("x") + lax.axis_index("y")).astype(jnp.int32)
    place = jnp.stack([chip, lax.axis_index("c").astype(jnp.int32)])
    grads, delta, new_m, new_v = {}, {}, {}, {}

    def adamw(names):
        for n in names:
            delta[n], new_m[n], new_v[n] = _adamw(weights[n], grads[n], moments_m[n], moments_v[n], "adamw_" + n)

    conv_rows = conv_w.shape[0]
    conv_shard = jnp.pad(conv_w, ((0, 16 - conv_rows), (0, 0)))
    bufs = dict(zip(MATRICES + ("conv_w",), _cast_shards([weights[n] for n in MATRICES] + [conv_shard],
                                                         [BF16] * len(MATRICES) + [F32])))
    blocks = dict(zip(GROUP_FFN1, _run(_gather_carried([bufs[n] for n in GROUP_FFN1]), "gather_ffn1")))
    p = {n: _pad_lanes(weights[n], size) for n, size in SMALL}
    cos, sin = _rope_tables(positions)
    x_tok = x.reshape(t, d)

    gather_in = _gather_carried([bufs[n] for n in GROUP_IN])
    x1, gate1, up1, act1 = _ffn_fwd(x_tok, p["ffn1_norm"], blocks["ffn1_w_gate"], blocks["ffn1_w_up"], blocks["ffn1_w_down"], None,
                                    "ffn1_fwd", gather_in)
    blocks.update(zip(GROUP_IN, gather_in.results))
    w = _kernel_layouts({n: _from_col_blocks(blocks[n]) for n in ("w_in", "w_uq", "w_uk", "w_uv")})
    p["conv_w"] = _from_col_blocks(blocks["conv_w"])[:conv_rows]

    gather_rest = _gather_carried([bufs[n] for n in GROUP_MIX + GROUP_FFN2])
    h2b, big, lat, q, k, v, vt = _inproj_fwd(x1, p["mix_norm"], w["w_m"], w["w_p"], p["q_a_norm"], p["kv_a_norm"], p["q_head_norm"],
                                             p["k_head_norm"], w["w_uq"], w["w_uk"], w["w_uv"], w["w_uvt"], cos, sin, gather_rest)
    blocks.update(zip(GROUP_MIX + GROUP_FFN2, gather_rest.results))
    w_pa = _from_col_blocks(blocks["w_proj_attn"])
    w_pc, w_out_full = blocks["w_proj_conv"].reshape(-1, d), blocks["w_out"].reshape(-1, d)
    wg2, wu2, wd2 = blocks["ffn2_w_gate"], blocks["ffn2_w_up"], blocks["ffn2_w_down"]

    o, lse = _attn_fwd(q, k, vt, seq)
    x2 = _mix_fwd(x1, o, big, p["gate_bias"], p["conv_w"], w_pa, w_pc, w_out_full, seq)
    dx3, gate2, up2, act2, loss = _ffn_fwd(x2, p["ffn2_norm"], wg2, wu2, wd2, loss_target.reshape(t, d), "ffn2_fwd")

    dx2, dg_ffn2, hb2, dgate2, dup2, dyb2 = _ffn_bwd_x(x2, p["ffn2_norm"], dx3, gate2, up2, wg2, wu2, wd2, "ffn2_bwd")
    g_ffn2 = [_tn_matmul(hb2, dgate2, "ffn2_dw_gate"), _tn_matmul(hb2, dup2, "ffn2_dw_up"), _tn_matmul(act2, dyb2, "ffn2_dw_down")]
    swap = _swap_carried(g_ffn2)
    do, delta_o, dz, dm, dbias, dw_pa, dw_pc, dw_out = _mix_bwd(dx2, o, big, p["gate_bias"], p["conv_w"], w_pa, w_pc, w_out_full, seq,
                                                                swap)
    part = _partials(GROUP_FFN2, g_ffn2, swap.results, place)
    send = _send_carried([pb for _, pb in part])
    dq, dk, dv = _attn_bwd(q, k, v, do, lse, delta_o.reshape(N_HEADS // ATTN_BWD_HEADS, ATTN_BWD_HEADS, -1), seq, send)
    join = _join_carried(_totals(GROUP_FFN2, part, send.results, place))
    dp, dw_uq, dw_uk, dw_uv, dqa, dkva, dqh, dkh, dcw = _prep_bwd(
        lat, big, dz, dq, dk, dv, p["q_a_norm"], p["kv_a_norm"], p["q_head_norm"], p["k_head_norm"], w["w_uq"], w["w_uk"],
        w["w_uv"], cos, sin, p["conv_w"], seq, join)
    grads.update(zip(GROUP_FFN2, join.results))
    adamw(GROUP_FFN2)

    dx1, dg_mix = _inproj_bwd(x1, p["mix_norm"], dx2, dm, dp, w["w_m"], w["w_p"])
    gg = _global_layouts({"w_m": _tn_matmul(h2b, dm, "dw_in_m", split_n=2), "w_p": _tn_matmul(h2b, dp, "dw_in_p", split_n=2),
                          "w_uq": dw_uq, "w_uk": dw_uk, "w_uv": dw_uv, "w_pa": dw_pa, "w_pc": dw_pc, "w_out": dw_out})
    g_mid = [_col_blocks(gg[n]) if n in COL_SHARDED else gg[n].reshape(N_CHIPS, -1, gg[n].shape[-1]) for n in GROUP_MID]
    swap = _swap_carried(g_mid)
    grad_x, dg_ffn1, hb1, dgate1, dup1, dyb1 = _ffn_bwd_x(x_tok, p["ffn1_norm"], dx1, gate1, up1, blocks["ffn1_w_gate"],
                                                         blocks["ffn1_w_up"], blocks["ffn1_w_down"], "ffn1_bwd", swap)

    small_grads = {"ffn1_norm": dg_ffn1, "mix_norm": dg_mix, "gate_bias": dbias, "q_a_norm": dqa, "kv_a_norm": dkva,
                   "q_head_norm": dqh, "k_head_norm": dkh, "ffn2_norm": dg_ffn2}
    packed = jnp.concatenate([small_grads[n] for n, _ in SMALL] + [dcw.reshape(1, -1), loss], axis=1)
    total = _sum_devices(jnp.pad(packed, ((0, 7), (0, 0))))[0:1]
    n_small = sum(size for _, size in SMALL)
    conv_cols = conv_w.shape[1]
    conv_total = total[:, n_small:n_small + conv_rows * d].reshape(conv_rows, d)
    grads["conv_w"] = lax.dynamic_slice_in_dim(conv_total, chip * conv_cols, conv_cols, axis=1)
    loss_total = total[0, n_small + conv_rows * d]

    part = _partials(GROUP_MID, g_mid, swap.results, place)
    send = _send_carried([pb for _, pb in part])
    g_gate = _tn_matmul(hb1, dgate1, "ffn1_dw_gate", carried=send)
    join = _join_carried(_totals(GROUP_MID, part, send.results, place))
    swap_gate = _swap_carried([g_gate])
    g_up = _tn_matmul(hb1, dup1, "ffn1_dw_up", carried=_both(join, swap_gate))
    grads.update(zip(GROUP_MID, join.results))
    adamw(GROUP_MID + ("conv_w",))
    part_gate = _partials(GROUP_FFN1[:1], [g_gate], swap_gate.results, place)
    send_gate, swap_up = _send_carried([part_gate[0][1]]), _swap_carried([g_up])
    g_down = _tn_matmul(act1, dyb1, "ffn1_dw_down", carried=_both(send_gate, swap_up))
    join_gate = _join_carried(_totals(GROUP_FFN1[:1], part_gate, send_gate.results, place))
    part_up = _partials(GROUP_FFN1[1:2], [g_up], swap_up.results, place)
    send_up, swap_down = _send_carried([part_up[0][1]]), _swap_carried([g_down])
    _run(_both(_both(send_up, swap_down), join_gate), "reduce_tail_1")
    grads["ffn1_w_gate"] = join_gate.results[0]
    adamw(GROUP_FFN1[:1])
    join_up = _join_carried(_totals(GROUP_FFN1[1:2], part_up, send_up.results, place))
    part_down = _partials(GROUP_FFN1[2:], [g_down], swap_down.results, place)
    send_down = _send_carried([part_down[0][1]])
    _run(_both(send_down, join_up), "reduce_tail_2")
    grads["ffn1_w_up"] = join_up.results[0]
    adamw(GROUP_FFN1[1:2])
    grads["ffn1_w_down"] = _run(_join_carried(_totals(GROUP_FFN1[2:], part_down, send_down.results, place)), "reduce_tail_3")[0]
    adamw(GROUP_FFN1[2:])

    pack = lambda src: jnp.concatenate([_pad_lanes(src[n], size) for n, size in SMALL], axis=1)
    sd, sm, sv = _adamw(pack(weights), total[:, :n_small], pack(moments_m), pack(moments_v), "adamw_small")
    off = 0
    for n, size in SMALL:
        real = weights[n].size
        grads[n] = total[0, off:off + real]
        delta[n], new_m[n], new_v[n] = sd[0, off:off + real], sm[0, off:off + real], sv[0, off:off + real]
        off += size

    return (loss_total, grad_x.reshape(nb, seq, d), *[grads[n] for n in WEIGHT_ORDER], *[delta[n] for n in WEIGHT_ORDER],
            *[new_m[n] for n in WEIGHT_ORDER], *[new_v[n] for n in WEIGHT_ORDER])
```

```python
import functools

import jax
import jax.numpy as jnp
from jax import lax
from jax.experimental import pallas as pl
from jax.experimental.pallas import tpu as pltpu

F32 = jnp.float32
BF16 = jnp.bfloat16

D_MODEL = 1024
N_HEADS = 8
QK_NOPE = 64
QK_ROPE = 32
QK_DIM = QK_NOPE + QK_ROPE
V_DIM = 64
HEAD_PAD = 128
Q_LORA = 384
KV_LORA = 256
ROPE_THETA = 10000.0
NORM_EPS = 1e-6
ATTN_SCALE = QK_DIM ** -0.5
MASK_VALUE = -1e30
N_CHIPS = 4
N_DEV = 8

ADAM_LR = 0.001
ADAM_B1 = 0.9
ADAM_B2 = 0.999
ADAM_EPS = 1e-08
ADAM_WD = 0.01
ADAM_STEP = 10

TOKEN_TILE = 256
ATTN_TILE = 512
TN_TILE = 512
VMEM_LIMIT = 56 * 1024 * 1024

M_COLS = 3 * D_MODEL
P_COLS = 2 * D_MODEL + Q_LORA + KV_LORA + HEAD_PAD
BIG_COLS = 5 * D_MODEL
LAT_COLS = Q_LORA + KV_LORA + HEAD_PAD

MESH_ID = pl.DeviceIdType.MESH
ANY = pl.BlockSpec(memory_space=pl.ANY)


def _params(semantics=None):
    return pltpu.CompilerParams(dimension_semantics=semantics, vmem_limit_bytes=VMEM_LIMIT)


class _Carried:
    def __init__(self, operands, out_shapes, aliases, n_sems, start, finish):
        self.operands, self.out_shapes, self.aliases, self.n_sems = list(operands), list(out_shapes), dict(aliases), n_sems
        self.start, self.finish = start, finish
        self.results = None


def _both(a, b):
    na, nao = len(a.operands), len(a.out_shapes)

    def start(ins, outs, sems, base):
        a.start(ins[:na], outs[:nao], sems, base)
        b.start(ins[na:], outs[nao:], sems, base + a.n_sems)

    def finish(ins, outs, sems, base):
        a.finish(ins[:na], outs[:nao], sems, base)
        b.finish(ins[na:], outs[nao:], sems, base + a.n_sems)

    aliases = dict(a.aliases)
    aliases.update({na + i: nao + o for i, o in b.aliases.items()})
    both = _Carried(a.operands + b.operands, a.out_shapes + b.out_shapes, aliases, a.n_sems + b.n_sems, start, finish)
    both.parts = (a, b)
    return both


def _set_results(carried, results):
    carried.results = list(results)
    if hasattr(carried, "parts"):
        a, b = carried.parts
        _set_results(a, results[:len(a.out_shapes)])
        _set_results(b, results[len(a.out_shapes):])


def _pallas(body, name, grid, in_specs, out_specs, out_shape, args, semantics, carried=None):
    if carried is None:
        return pl.pallas_call(body, name=name, grid=grid, in_specs=in_specs, out_specs=out_specs, out_shape=out_shape,
                              compiler_params=_params(semantics))(*args)
    n_in, n_out, n_ci, n_co = len(in_specs), len(out_specs), len(carried.operands), len(carried.out_shapes)

    def wrapped(*refs):
        ins, c_ins = refs[:n_in], refs[n_in:n_in + n_ci]
        outs, c_outs = refs[n_in + n_ci:n_in + n_ci + n_out], refs[n_in + n_ci + n_out:n_in + n_ci + n_out + n_co]
        sems = refs[-1]
        first = pl.program_id(0) == 0
        last = pl.program_id(0) == grid[0] - 1
        for axis in range(1, len(grid)):
            first = jnp.logical_and(first, pl.program_id(axis) == 0)
            last = jnp.logical_and(last, pl.program_id(axis) == grid[axis] - 1)

        @pl.when(first)
        def _():
            carried.start(c_ins, c_outs, sems, 0)

        body(*ins, *outs)

        @pl.when(last)
        def _():
            carried.finish(c_ins, c_outs, sems, 0)

    results = pl.pallas_call(
        wrapped, name=name, grid=grid, in_specs=list(in_specs) + [ANY] * n_ci, out_specs=list(out_specs) + [ANY] * n_co,
        out_shape=list(out_shape) + carried.out_shapes,
        input_output_aliases={n_in + i: n_out + o for i, o in carried.aliases.items()},
        scratch_shapes=[pltpu.SemaphoreType.DMA((carried.n_sems,))], compiler_params=_params(semantics))(*args, *carried.operands)
    _set_results(carried, results[n_out:])
    return results[:n_out]


def _run(carried, name):
    n_ci, n_co = len(carried.operands), len(carried.out_shapes)

    def body(*refs):
        carried.start(refs[:n_ci], refs[n_ci:n_ci + n_co], refs[-1], 0)
        carried.finish(refs[:n_ci], refs[n_ci:n_ci + n_co], refs[-1], 0)

    results = pl.pallas_call(body, name=name, in_specs=[ANY] * n_ci, out_specs=[ANY] * n_co, out_shape=carried.out_shapes,
                             input_output_aliases=carried.aliases,
                             scratch_shapes=[pltpu.SemaphoreType.DMA((carried.n_sems,))])(*carried.operands)
    _set_results(carried, results)
    return carried.results


def _resident(shape):
    nd = len(shape)
    return pl.BlockSpec(shape, lambda *_: (0,) * nd, pipeline_mode=pl.Buffered(1))


def _const(shape):
    nd = len(shape)
    return pl.BlockSpec(shape, lambda *_: (0,) * nd)


def _mm(a, b):
    return jnp.dot(a, b, preferred_element_type=F32)


def _mm_nt(a, b):
    return lax.dot_general(a, b, (((1,), (1,)), ((), ())), preferred_element_type=F32)


def _mm_tn(a, b):
    return lax.dot_general(a, b, (((0,), (0,)), ((), ())), preferred_element_type=F32)


def _bf(a):
    return a.astype(BF16)


def _sigmoid(a):
    return 1.0 / (1.0 + jnp.exp(-a))


def _rms(x, gain, n=None):
    n = x.shape[-1] if n is None else n
    r = lax.rsqrt(jnp.sum(x * x, axis=-1, keepdims=True) * (1.0 / n) + NORM_EPS)
    return (x * r) * gain, r


def _rms_bwd(x, r, gain, dh, n=None):
    n = x.shape[-1] if n is None else n
    u = dh * gain
    dx = r * u - x * ((r * r * r) * (jnp.sum(u * x, axis=-1, keepdims=True) * (1.0 / n)))
    dgain = jnp.sum(dh * (x * r), axis=0, keepdims=True)
    return dx, dgain


def _rope_swap(t):
    lane = lax.broadcasted_iota(jnp.int32, t.shape, 1)
    lo = (lane >= QK_NOPE) & (lane < QK_NOPE + QK_ROPE // 2)
    hi = (lane >= QK_NOPE + QK_ROPE // 2) & (lane < QK_DIM)
    up = pltpu.roll(t, HEAD_PAD - QK_ROPE // 2, 1)
    down = pltpu.roll(t, QK_ROPE // 2, 1)
    return jnp.where(lo, up, jnp.where(hi, down, 0.0))


def _rope(t, cos, sin):
    return t * cos + _rope_swap(t) * sin


def _rope_bwd(dt, cos, sin):
    return dt * cos + _rope_swap(dt * sin)


def _shift_down(u, prev8, k):
    s = pltpu.roll(u, k, 0)
    p = pltpu.roll(prev8, k, 0)
    row = lax.broadcasted_iota(jnp.int32, prev8.shape, 0)
    top = jnp.where(row < k, p, s[:8])
    return jnp.concatenate([top, s[8:]], axis=0)


def _shift_up(d, next8, k):
    tm = d.shape[0]
    s = pltpu.roll(d, tm - k, 0)
    n = pltpu.roll(next8, 8 - k, 0)
    row = lax.broadcasted_iota(jnp.int32, next8.shape, 0)
    bot = jnp.where(row >= 8 - k, n, s[tm - 8:])
    return jnp.concatenate([s[:tm - 8], bot], axis=0)


def _ffn_fwd(x, gain, wg, wu, wd, target, name, carried=None):
    t, d = x.shape
    nb, f, _ = wg.shape
    tm = TOKEN_TILE
    with_loss = target is not None

    def body(*refs):
        if with_loss:
            x_ref, g_ref, wg_ref, wu_ref, wd_ref, t_ref, out_ref, gate_ref, up_ref, act_ref, loss_ref = refs
        else:
            x_ref, g_ref, wg_ref, wu_ref, wd_ref, out_ref, gate_ref, up_ref, act_ref = refs
        xv = x_ref[...]
        h, _ = _rms(xv, g_ref[...])
        hb = _bf(h)
        y = jnp.zeros((tm, d), F32)
        for j in range(nb):
            gate = _mm_nt(hb, wg_ref[j])
            up = _mm_nt(hb, wu_ref[j])
            act = _bf((gate * _sigmoid(gate)) * up)
            y = y + _mm(act, wd_ref[j])
            gate_ref[j] = _bf(gate)
            up_ref[j] = _bf(up)
            act_ref[j] = act
        out = xv + 0.5 * y
        if with_loss:
            err = out - t_ref[...]
            out_ref[...] = err * (1.0 / d)

            @pl.when(pl.program_id(0) == 0)
            def _():
                loss_ref[...] = jnp.zeros_like(loss_ref)

            part = jnp.sum(jnp.sum(err * err, axis=1, keepdims=True), axis=0, keepdims=True)
            loss_ref[...] += jnp.broadcast_to(part * (0.5 / d), loss_ref.shape)
        else:
            out_ref[...] = out

    tok = pl.BlockSpec((tm, d), lambda i: (i, 0))
    blk = pl.BlockSpec((nb, tm, f), lambda i: (0, i, 0))
    in_specs = [tok, _const((1, d)), _resident(wg.shape), _resident(wu.shape), _resident(wd.shape)]
    args = [x, gain, wg, wu, wd]
    out_shape = [jax.ShapeDtypeStruct((t, d), F32)] + [jax.ShapeDtypeStruct((nb, t, f), BF16)] * 3
    out_specs = [tok, blk, blk, blk]
    if with_loss:
        in_specs.append(tok)
        args.append(target)
        out_shape.append(jax.ShapeDtypeStruct((1, 128), F32))
        out_specs.append(_const((1, 128)))
    return _pallas(body, name, (t // tm,), in_specs, out_specs, out_shape, args, ("arbitrary",), carried)


def _ffn_bwd_x(x, gain, dout, gate, up, wg, wu, wd, name, carried=None):
    t, d = x.shape
    nb, f, _ = wg.shape
    tm = TOKEN_TILE

    def body(x_ref, g_ref, dout_ref, gate_ref, up_ref, wg_ref, wu_ref, wd_ref,
             dx_ref, dgain_ref, hb_ref, dgate_ref, dup_ref, dyb_ref):
        xv = x_ref[...]
        gain_v = g_ref[...]
        h, r = _rms(xv, gain_v)
        hb_ref[...] = _bf(h)
        dout_v = dout_ref[...]
        dyb = _bf(0.5 * dout_v)
        dyb_ref[...] = dyb
        dh = jnp.zeros((tm, d), F32)
        for j in range(nb):
            gt = gate_ref[j].astype(F32)
            uv = up_ref[j].astype(F32)
            s = _sigmoid(gt)
            dact = _mm_nt(dyb, wd_ref[j])
            dup = _bf(dact * (gt * s))
            dgate = _bf((dact * uv) * (s * (1.0 + gt * (1.0 - s))))
            dh = dh + _mm(dgate, wg_ref[j]) + _mm(dup, wu_ref[j])
            dgate_ref[j] = dgate
            dup_ref[j] = dup
        dxn, dgain = _rms_bwd(xv, r, gain_v, dh)
        dx_ref[...] = dout_v + dxn

        @pl.when(pl.program_id(0) == 0)
        def _():
            dgain_ref[...] = jnp.zeros_like(dgain_ref)

        dgain_ref[...] += dgain

    tok = pl.BlockSpec((tm, d), lambda i: (i, 0))
    blk = pl.BlockSpec((nb, tm, f), lambda i: (0, i, 0))
    return _pallas(
        body, name, (t // tm,),
        [tok, _const((1, d)), tok, blk, blk, _resident(wg.shape), _resident(wu.shape), _resident(wd.shape)],
        [tok, _const((1, d)), tok, blk, blk, tok],
        [jax.ShapeDtypeStruct((t, d), F32), jax.ShapeDtypeStruct((1, d), F32), jax.ShapeDtypeStruct((t, d), BF16),
         jax.ShapeDtypeStruct((nb, t, f), BF16), jax.ShapeDtypeStruct((nb, t, f), BF16), jax.ShapeDtypeStruct((t, d), BF16)],
        (x, gain, dout, gate, up, wg, wu, wd), ("arbitrary",), carried)


def _tn_matmul(a, b, name, split_k=1, carried=None):
    t = a.shape[-2]
    k = a.shape[-1]
    n = b.shape[-1]
    tt = min(TN_TILE, t)
    nt = t // tt

    def body(a_ref, b_ref, o_ref):
        @pl.when(pl.program_id(1) == 0)
        def _():
            o_ref[...] = jnp.zeros_like(o_ref)

        o_ref[...] += _mm_tn(a_ref[...], b_ref[...])

    if split_k > 1:
        assert a.ndim == 2 and b.ndim == 2 and k % (split_k * 128) == 0
        tk = k // split_k
        g = split_k
        a_spec = pl.BlockSpec((tt, tk), lambda gi, ti: (ti, gi))
        b_spec = pl.BlockSpec((tt, n), lambda gi, ti: (ti, 0))
        o_spec = pl.BlockSpec((tk, n), lambda gi, ti: (gi, 0))
        out_shape = jax.ShapeDtypeStruct((k, n), F32)
    else:
        g = a.shape[0] if a.ndim == 3 else b.shape[0]
        a_spec = (pl.BlockSpec((None, tt, k), lambda gi, ti: (gi, ti, 0)) if a.ndim == 3
                  else pl.BlockSpec((tt, k), lambda gi, ti: (ti, 0)))
        b_spec = (pl.BlockSpec((None, tt, n), lambda gi, ti: (gi, ti, 0)) if b.ndim == 3
                  else pl.BlockSpec((tt, n), lambda gi, ti: (ti, 0)))
        o_spec = pl.BlockSpec((None, k, n), lambda gi, ti: (gi, 0, 0))
        out_shape = jax.ShapeDtypeStruct((g, k, n), F32)
    return _pallas(body, name, (g, nt), [a_spec, b_spec], [o_spec], [out_shape], (a, b), ("arbitrary", "arbitrary"), carried)[0]


def _inproj_fwd(x1, gain, w_m, w_p, qa_gain, kva_gain, qh_gain, kh_gain, w_uq, w_uk, w_uv, w_uvt, cos, sin, carried=None):
    t, d = x1.shape
    tm = TOKEN_TILE

    def body(x_ref, g_ref, wm_ref, wp_ref, qa_ref, kva_ref, qh_ref, kh_ref, wuq_ref, wuk_ref, wuv_ref, wuvt_ref, cos_ref, sin_ref,
             hb_ref, big_ref, lat_ref, q_ref, k_ref, v_ref, vt_ref):
        h, _ = _rms(x_ref[...], g_ref[...])
        hb = _bf(h)
        hb_ref[...] = hb
        big_ref[:, :M_COLS] = _mm_nt(hb, wm_ref[...])
        pp = _mm_nt(hb, wp_ref[...])
        big_ref[:, M_COLS:] = pp[:, :2 * D_MODEL]
        lat = pp[:, 2 * D_MODEL:]
        lat_ref[...] = lat
        cq, _ = _rms(lat[:, :Q_LORA], qa_ref[...])
        ckv, _ = _rms(lat[:, Q_LORA:Q_LORA + KV_LORA], kva_ref[...])
        k_rope = lat[:, Q_LORA + KV_LORA:]
        cqb = _bf(cq)
        ckvb = _bf(ckv)
        cos_v = cos_ref[...]
        sin_v = sin_ref[...]
        for hd in range(N_HEADS):
            qn, _ = _rms(_mm(cqb, wuq_ref[hd]), qh_ref[...], QK_DIM)
            q_ref[hd] = _bf(_rope(qn, cos_v, sin_v))
            kn, _ = _rms(_mm(ckvb, wuk_ref[hd]) + k_rope, kh_ref[...], QK_DIM)
            k_ref[hd] = _bf(_rope(kn, cos_v, sin_v))
            v_ref[hd] = _bf(_mm(ckvb, wuv_ref[hd]))
            vt_ref[hd] = _bf(_mm_nt(wuvt_ref[hd], ckvb))

    tok = lambda c: pl.BlockSpec((tm, c), lambda i: (i, 0))
    head = lambda c: pl.BlockSpec((N_HEADS, tm, c), lambda i: (0, i, 0))
    return _pallas(
        body, "inproj_fwd", (t // tm,),
        [tok(d), _const((1, d)), _resident(w_m.shape), _resident(w_p.shape), _const((1, Q_LORA)), _const((1, KV_LORA)),
         _const((1, HEAD_PAD)), _const((1, HEAD_PAD)), _resident(w_uq.shape), _resident(w_uk.shape),
         _resident(w_uv.shape), _resident(w_uvt.shape), tok(HEAD_PAD), tok(HEAD_PAD)],
        [tok(d), tok(BIG_COLS), tok(LAT_COLS), head(HEAD_PAD), head(HEAD_PAD), head(V_DIM),
         pl.BlockSpec((N_HEADS, V_DIM, tm), lambda i: (0, 0, i))],
        [jax.ShapeDtypeStruct((t, d), BF16), jax.ShapeDtypeStruct((t, BIG_COLS), F32),
         jax.ShapeDtypeStruct((t, LAT_COLS), F32), jax.ShapeDtypeStruct((N_HEADS, t, HEAD_PAD), BF16),
         jax.ShapeDtypeStruct((N_HEADS, t, HEAD_PAD), BF16), jax.ShapeDtypeStruct((N_HEADS, t, V_DIM), BF16),
         jax.ShapeDtypeStruct((N_HEADS, V_DIM, t), BF16)],
        (x1, gain, w_m, w_p, qa_gain, kva_gain, qh_gain, kh_gain, w_uq, w_uk, w_uv, w_uvt, cos, sin), ("arbitrary",), carried)


EXP2_SCALE = ATTN_SCALE * 1.4426950408889634


def _diagonal_keep(tk, tq):
    return lax.broadcasted_iota(jnp.int32, (tk, tq), 0) <= lax.broadcasted_iota(jnp.int32, (tk, tq), 1)


def _attn_fwd(q, k, vt, seq):
    _, t, _ = q.shape
    nseq = t // seq
    tq = tk = ATTN_TILE
    nq = seq // tq

    def body(q_ref, k_ref, vt_ref, o_ref, lse_ref):
        i = pl.program_id(1)
        qs = [q_ref[h] for h in range(N_HEADS)]
        keep = _diagonal_keep(tk, tq)

        def tile(h, state, k0, diagonal):
            m, l, acc = state
            st = _mm_nt(k_ref[h, pl.ds(k0, tk), :], qs[h])
            if diagonal:
                st = jnp.where(keep, st, MASK_VALUE)
            m_new = jnp.maximum(m, jnp.max(st, axis=0, keepdims=True))
            pt = jnp.exp2((st - m_new) * EXP2_SCALE)
            alpha = jnp.exp2((m - m_new) * EXP2_SCALE)
            l_new = alpha * l + jnp.sum(pt, axis=0, keepdims=True)
            return m_new, l_new, alpha * acc + _mm(vt_ref[h, :, pl.ds(k0, tk)], _bf(pt))

        def step(j, states):
            k0 = pl.multiple_of(j * tk, tk)
            return tuple(tile(h, states[h], k0, False) for h in range(N_HEADS))

        init = tuple((jnp.full((1, tq), MASK_VALUE, F32), jnp.zeros((1, tq), F32), jnp.zeros((V_DIM, tq), F32))
                     for _ in range(N_HEADS))
        states = lax.fori_loop(0, i, step, init)
        k0 = pl.multiple_of(i * tk, tk)
        outs = []
        for h in range(N_HEADS):
            m, l, acc = tile(h, states[h], k0, True)
            outs.append((acc / l).T)
            lse_ref[h] = m * EXP2_SCALE + jnp.log2(l)
        o_ref[...] = _bf(jnp.concatenate(outs, axis=-1))

    return pl.pallas_call(
        body, name="attn_fwd", grid=(nseq, nq),
        in_specs=[pl.BlockSpec((N_HEADS, tq, HEAD_PAD), lambda b, i: (0, b * nq + i, 0)),
                  pl.BlockSpec((N_HEADS, seq, HEAD_PAD), lambda b, i: (0, b, 0)),
                  pl.BlockSpec((N_HEADS, V_DIM, seq), lambda b, i: (0, 0, b))],
        out_specs=[pl.BlockSpec((tq, N_HEADS * V_DIM), lambda b, i: (b * nq + i, 0)),
                   pl.BlockSpec((N_HEADS, 1, tq), lambda b, i: (0, 0, b * nq + i))],
        out_shape=[jax.ShapeDtypeStruct((t, N_HEADS * V_DIM), BF16), jax.ShapeDtypeStruct((N_HEADS, 1, t), F32)],
        compiler_params=_params(("arbitrary", "arbitrary")))(q, k, vt)


ATTN_BWD_HEADS = 4


def _attn_bwd(q, k, v, do, lse, delta, seq, carried=None):
    _, t, _ = q.shape
    nseq = t // seq
    tq = tk = ATTN_TILE
    n = seq // tq
    hb = ATTN_BWD_HEADS

    def body(q_ref, k_ref, v_ref, do_ref, lse_ref, delta_ref, dq_ref, dk_ref, dv_ref):
        dq_ref[...] = jnp.zeros_like(dq_ref)
        dk_ref[...] = jnp.zeros_like(dk_ref)
        dv_ref[...] = jnp.zeros_like(dv_ref)
        keep = _diagonal_keep(tk, tq)

        def tile(h, k0, q0, diagonal):
            kj = k_ref[h, pl.ds(k0, tk), :]
            qi = q_ref[h, pl.ds(q0, tq), :]
            doi = _bf(do_ref[pl.ds(q0, tq), h * V_DIM:(h + 1) * V_DIM])
            st = _mm_nt(kj, qi)
            if diagonal:
                st = jnp.where(keep, st, MASK_VALUE)
            pt = jnp.exp2(st * EXP2_SCALE - lse_ref[h, :, pl.ds(q0, tq)])
            dv_ref[h, pl.ds(k0, tk), :] += _mm(_bf(pt), doi)
            dpt = _mm_nt(v_ref[h, pl.ds(k0, tk), :], doi)
            dst = _bf((pt * (dpt - delta_ref[pl.ds(h, 1), pl.ds(q0, tq)])) * ATTN_SCALE)
            dk_ref[h, pl.ds(k0, tk), :] += _mm(dst, qi)
            dq_ref[h, pl.ds(q0, tq), :] += _mm_tn(dst, kj)

        def kv_step(j, _):
            k0 = pl.multiple_of(j * tk, tk)
            for h in range(hb):
                tile(h, k0, k0, True)

            def q_step(i, _):
                q0 = pl.multiple_of(i * tq, tq)
                for h in range(hb):
                    tile(h, k0, q0, False)
                return 0

            lax.fori_loop(j + 1, n, q_step, 0)
            return 0

        lax.fori_loop(0, n, kv_step, 0)

    hspec = lambda c: pl.BlockSpec((hb, seq, c), lambda b, g: (g, b, 0))
    return _pallas(
        body, "attn_bwd", (nseq, N_HEADS // hb),
        [hspec(HEAD_PAD), hspec(HEAD_PAD), hspec(V_DIM), pl.BlockSpec((seq, hb * V_DIM), lambda b, g: (b, g)),
         pl.BlockSpec((hb, 1, seq), lambda b, g: (g, 0, b)), pl.BlockSpec((None, hb, seq), lambda b, g: (g, 0, b))],
        [hspec(HEAD_PAD), hspec(HEAD_PAD), hspec(V_DIM)],
        [jax.ShapeDtypeStruct((N_HEADS, t, HEAD_PAD), F32), jax.ShapeDtypeStruct((N_HEADS, t, HEAD_PAD), F32),
         jax.ShapeDtypeStruct((N_HEADS, t, V_DIM), F32)],
        (q, k, v, do, lse, delta), ("arbitrary", "arbitrary"), carried)


def _mixer_values(o_ref, gb_ref, gla_ref, glb_ref, xc_ref, gc_ref, xcp_ref, gcp_ref, bias_ref, cw_ref, wpa_ref, wpc_ref,
                  first_of_seq):
    gb = gb_ref[...]
    u = gc_ref[...] * xc_ref[...]
    u_prev = jnp.where(first_of_seq, 0.0, gcp_ref[...] * xcp_ref[...])
    cw = cw_ref[...]
    z = cw[2:3] * u + cw[1:2] * _shift_down(u, u_prev, 1) + cw[0:1] * _shift_down(u, u_prev, 2)
    gbz = _bf(gb * z)
    y_b = _mm(gbz, wpc_ref[...])
    y_a = _mm(o_ref[...], wpa_ref[...])
    bias = bias_ref[...]
    gate_a = _sigmoid(gla_ref[...] + bias[:, :D_MODEL])
    gate_b = _sigmoid(glb_ref[...] + bias[:, D_MODEL:])
    merged = _bf(gate_a * y_a + gate_b * y_b)
    return gb, u, z, gbz, y_a, y_b, gate_a, gate_b, merged


def _mixer_specs(tm, seq):
    d = D_MODEL
    tok = pl.BlockSpec((tm, d), lambda i: (i, 0))
    col = lambda c: pl.BlockSpec((tm, d), lambda i: (i, c))
    prev = lambda c: pl.BlockSpec((8, d), lambda i: (jnp.maximum(i * (tm // 8) - 1, 0), c))
    o_spec = pl.BlockSpec((tm, N_HEADS * V_DIM), lambda i: (i, 0))
    fwd_specs = [o_spec, col(0), col(1), col(2), col(3), col(4), prev(3), prev(4), _const((1, 2 * d)), _const((3, d)),
                 _resident((N_HEADS * V_DIM, d)), _resident((d, d)), _resident((d, d))]
    return tok, fwd_specs


def _mix_fwd(x1, o, big, gate_bias, conv_w, w_pa, w_pc, w_out, seq):
    t, d = x1.shape
    tm = TOKEN_TILE
    tiles_per_seq = seq // tm

    def body(x_ref, o_ref, gb_ref, gla_ref, glb_ref, xc_ref, gc_ref, xcp_ref, gcp_ref, bias_ref, cw_ref, wpa_ref, wpc_ref,
             wout_ref, x2_ref):
        first = pl.program_id(0) % tiles_per_seq == 0
        merged = _mixer_values(o_ref, gb_ref, gla_ref, glb_ref, xc_ref, gc_ref, xcp_ref, gcp_ref, bias_ref, cw_ref, wpa_ref,
                               wpc_ref, first)[-1]
        x2_ref[...] = x_ref[...] + _mm(merged, wout_ref[...])

    tok, fwd_specs = _mixer_specs(tm, seq)
    return pl.pallas_call(
        body, name="mix_fwd", grid=(t // tm,), in_specs=[tok] + fwd_specs, out_specs=tok,
        out_shape=jax.ShapeDtypeStruct((t, d), F32),
        compiler_params=_params(("arbitrary",)))(x1, o, big, big, big, big, big, big, big, gate_bias, conv_w, w_pa, w_pc, w_out)


def _mix_bwd(dx2, o, big, gate_bias, conv_w, w_pa, w_pc, w_out, seq, carried=None):
    t, d = dx2.shape
    tm = TOKEN_TILE
    tiles_per_seq = seq // tm
    hv = N_HEADS * V_DIM

    def body(dx_ref, o_ref, gb_ref, gla_ref, glb_ref, xc_ref, gc_ref, xcp_ref, gcp_ref, bias_ref, cw_ref, wpa_ref, wpc_ref,
             wout_ref, do_ref, delta_ref, dz_ref, dm_ref, dbias_ref, dwpa_ref, dwpc_ref, dwout_ref):
        first = pl.program_id(0) % tiles_per_seq == 0
        gb, _, z, gbz, y_a, y_b, gate_a, gate_b, merged = _mixer_values(
            o_ref, gb_ref, gla_ref, glb_ref, xc_ref, gc_ref, xcp_ref, gcp_ref, bias_ref, cw_ref, wpa_ref, wpc_ref, first)

        @pl.when(pl.program_id(0) == 0)
        def _():
            dbias_ref[...] = jnp.zeros_like(dbias_ref)
            dwpa_ref[...] = jnp.zeros_like(dwpa_ref)
            dwpc_ref[...] = jnp.zeros_like(dwpc_ref)
            dwout_ref[...] = jnp.zeros_like(dwout_ref)

        dxb = _bf(dx_ref[...])
        dmerged = _mm_nt(dxb, wout_ref[...])
        dwout_ref[...] += _mm_tn(merged, dxb)
        dla = (dmerged * y_a) * (gate_a * (1.0 - gate_a))
        dlb = (dmerged * y_b) * (gate_b * (1.0 - gate_b))
        dbias_ref[:, :d] += jnp.sum(dla, axis=0, keepdims=True)
        dbias_ref[:, d:] += jnp.sum(dlb, axis=0, keepdims=True)
        dya = _bf(dmerged * gate_a)
        dyb = _bf(dmerged * gate_b)
        do_v = _mm_nt(dya, wpa_ref[...])
        do_ref[...] = do_v
        head = lax.broadcasted_iota(jnp.int32, (N_HEADS, hv), 0) * V_DIM
        col = lax.broadcasted_iota(jnp.int32, (N_HEADS, hv), 1)
        in_head = ((col >= head) & (col < head + V_DIM)).astype(F32)
        delta_ref[...] = lax.dot_general(in_head, do_v * o_ref[...].astype(F32), (((1,), (1,)), ((), ())),
                                         precision=lax.Precision.HIGHEST, preferred_element_type=F32)
        dwpa_ref[...] += _mm_tn(o_ref[...], dya)
        dgz = _mm_nt(dyb, wpc_ref[...])
        dwpc_ref[...] += _mm_tn(gbz, dyb)
        dz_ref[...] = dgz * gb
        dm_ref[:, :d] = _bf(dgz * z)
        dm_ref[:, d:2 * d] = _bf(dla)
        dm_ref[:, 2 * d:] = _bf(dlb)

    tok, fwd_specs = _mixer_specs(tm, seq)
    return _pallas(
        body, "mix_bwd", (t // tm,), [tok] + fwd_specs,
        [pl.BlockSpec((tm, hv), lambda i: (i, 0)), pl.BlockSpec((N_HEADS, tm), lambda i: (0, i)), tok,
         pl.BlockSpec((tm, M_COLS), lambda i: (i, 0)), _const((1, 2 * d)), _const((hv, d)), _const((d, d)), _const((d, d))],
        [jax.ShapeDtypeStruct((t, hv), F32), jax.ShapeDtypeStruct((N_HEADS, t), F32), jax.ShapeDtypeStruct((t, d), F32),
         jax.ShapeDtypeStruct((t, M_COLS), BF16), jax.ShapeDtypeStruct((1, 2 * d), F32), jax.ShapeDtypeStruct((hv, d), F32),
         jax.ShapeDtypeStruct((d, d), F32), jax.ShapeDtypeStruct((d, d), F32)],
        (dx2, o, big, big, big, big, big, big, big, gate_bias, conv_w, w_pa, w_pc, w_out), ("arbitrary",), carried)


def _prep_bwd(lat, big, dz, dq, dk, dv, qa_gain, kva_gain, qh_gain, kh_gain, w_uq, w_uk, w_uv, cos, sin, conv_w, seq, carried=None):
    t = lat.shape[0]
    d = D_MODEL
    tm = TOKEN_TILE
    tiles_per_seq = seq // tm
    last_blk = t // 8 - 1

    def body(lat_ref, xc_ref, gc_ref, dz_ref, dzn_ref, dq_ref, dk_ref, dv_ref, qa_ref, kva_ref, qh_ref, kh_ref, wuq_ref, wuk_ref,
             wuv_ref, cos_ref, sin_ref, cw_ref,
             dp_ref, dwuq_ref, dwuk_ref, dwuv_ref, dqa_ref, dkva_ref, dqh_ref, dkh_ref, dcw_ref):
        pid = pl.program_id(0)

        @pl.when(pid == 0)
        def _():
            for r in (dwuq_ref, dwuk_ref, dwuv_ref, dqa_ref, dkva_ref, dqh_ref, dkh_ref, dcw_ref):
                r[...] = jnp.zeros_like(r)

        last = pid % tiles_per_seq == tiles_per_seq - 1
        dzv = dz_ref[...]
        dz_next = jnp.where(last, 0.0, dzn_ref[...])
        dz1 = _shift_up(dzv, dz_next, 1)
        dz2 = _shift_up(dzv, dz_next, 2)
        cw = cw_ref[...]
        xc = xc_ref[...]
        gc = gc_ref[...]
        u = gc * xc
        du = cw[2:3] * dzv + cw[1:2] * dz1 + cw[0:1] * dz2
        dp_ref[:, :d] = _bf(du * gc)
        dp_ref[:, d:2 * d] = _bf(du * xc)
        dcw_ref[0:1, :] += jnp.sum(dz2 * u, axis=0, keepdims=True)
        dcw_ref[1:2, :] += jnp.sum(dz1 * u, axis=0, keepdims=True)
        dcw_ref[2:3, :] += jnp.sum(dzv * u, axis=0, keepdims=True)

        lat_v = lat_ref[...]
        q_lat = lat_v[:, :Q_LORA]
        kv_lat = lat_v[:, Q_LORA:Q_LORA + KV_LORA]
        k_rope = lat_v[:, Q_LORA + KV_LORA:]
        qa_gain_v = qa_ref[...]
        kva_gain_v = kva_ref[...]
        qh_gain_v = qh_ref[...]
        kh_gain_v = kh_ref[...]
        cq, rq = _rms(q_lat, qa_gain_v)
        ckv, rkv = _rms(kv_lat, kva_gain_v)
        cqb = _bf(cq)
        ckvb = _bf(ckv)
        cos_v = cos_ref[...]
        sin_v = sin_ref[...]
        lane = lax.broadcasted_iota(jnp.int32, (tm, HEAD_PAD), 1)
        rope_lanes = (lane >= QK_NOPE) & (lane < QK_DIM)
        dcq = jnp.zeros((tm, Q_LORA), F32)
        dckv = jnp.zeros((tm, KV_LORA), F32)
        dk_rope = jnp.zeros((tm, HEAD_PAD), F32)
        dqh_gain = jnp.zeros((1, HEAD_PAD), F32)
        dkh_gain = jnp.zeros((1, HEAD_PAD), F32)
        for hd in range(N_HEADS):
            q_pre = _mm(cqb, wuq_ref[hd])
            _, rr = _rms(q_pre, qh_gain_v, QK_DIM)
            dq_pre, dg = _rms_bwd(q_pre, rr, qh_gain_v, _rope_bwd(dq_ref[hd], cos_v, sin_v), QK_DIM)
            dqh_gain = dqh_gain + dg
            dq_pre_b = _bf(dq_pre)
            dcq = dcq + _mm_nt(dq_pre_b, wuq_ref[hd])
            dwuq_ref[hd] += _mm_tn(cqb, dq_pre_b)

            k_pre = _mm(ckvb, wuk_ref[hd]) + k_rope
            _, rr = _rms(k_pre, kh_gain_v, QK_DIM)
            dk_pre, dg = _rms_bwd(k_pre, rr, kh_gain_v, _rope_bwd(dk_ref[hd], cos_v, sin_v), QK_DIM)
            dkh_gain = dkh_gain + dg
            dk_rope = dk_rope + jnp.where(rope_lanes, dk_pre, 0.0)
            dk_pre_b = _bf(dk_pre)
            dvb = _bf(dv_ref[hd])
            dckv = dckv + _mm_nt(dk_pre_b, wuk_ref[hd]) + _mm_nt(dvb, wuv_ref[hd])
            dwuk_ref[hd] += _mm_tn(ckvb, dk_pre_b)
            dwuv_ref[hd] += _mm_tn(ckvb, dvb)
        dqh_ref[...] += dqh_gain
        dkh_ref[...] += dkh_gain
        dq_lat, dg = _rms_bwd(q_lat, rq, qa_gain_v, dcq)
        dqa_ref[...] += dg
        dkv_lat, dg = _rms_bwd(kv_lat, rkv, kva_gain_v, dckv)
        dkva_ref[...] += dg
        dp_ref[:, 2 * d:2 * d + Q_LORA] = _bf(dq_lat)
        dp_ref[:, 2 * d + Q_LORA:2 * d + Q_LORA + KV_LORA] = _bf(dkv_lat)
        dp_ref[:, 2 * d + Q_LORA + KV_LORA:] = _bf(dk_rope)

    tok = lambda c: pl.BlockSpec((tm, c), lambda i: (i, 0))
    col = lambda c: pl.BlockSpec((tm, d), lambda i: (i, c))
    head = lambda c: pl.BlockSpec((N_HEADS, tm, c), lambda i: (0, i, 0))
    nxt = pl.BlockSpec((8, d), lambda i: (jnp.minimum((i + 1) * (tm // 8), last_blk), 0))
    return _pallas(
        body, "prep_bwd", (t // tm,),
        [tok(LAT_COLS), col(3), col(4), tok(d), nxt, head(HEAD_PAD), head(HEAD_PAD), head(V_DIM),
         _const((1, Q_LORA)), _const((1, KV_LORA)), _const((1, HEAD_PAD)), _const((1, HEAD_PAD)),
         _resident(w_uq.shape), _resident(w_uk.shape), _resident(w_uv.shape), tok(HEAD_PAD), tok(HEAD_PAD), _const((3, d))],
        [tok(P_COLS), _const(w_uq.shape), _const(w_uk.shape), _const(w_uv.shape), _const((1, Q_LORA)),
         _const((1, KV_LORA)), _const((1, HEAD_PAD)), _const((1, HEAD_PAD)), _const((3, d))],
        [jax.ShapeDtypeStruct((t, P_COLS), BF16), jax.ShapeDtypeStruct(w_uq.shape, F32),
         jax.ShapeDtypeStruct(w_uk.shape, F32), jax.ShapeDtypeStruct(w_uv.shape, F32),
         jax.ShapeDtypeStruct((1, Q_LORA), F32), jax.ShapeDtypeStruct((1, KV_LORA), F32),
         jax.ShapeDtypeStruct((1, HEAD_PAD), F32), jax.ShapeDtypeStruct((1, HEAD_PAD), F32), jax.ShapeDtypeStruct((3, d), F32)],
        (lat, big, big, dz, dz, dq, dk, dv, qa_gain, kva_gain, qh_gain, kh_gain, w_uq, w_uk, w_uv, cos, sin, conv_w),
        ("arbitrary",), carried)


def _inproj_bwd(x1, gain, dx2, dm, dp, w_m, w_p, carried=None):
    t, d = x1.shape
    tm = TOKEN_TILE

    def body(x_ref, g_ref, dx2_ref, dm_ref, dp_ref, wm_ref, wp_ref, dx1_ref, dgain_ref):
        xv = x_ref[...]
        gain_v = g_ref[...]
        _, r = _rms(xv, gain_v)
        dh = _mm(dm_ref[...], wm_ref[...]) + _mm(dp_ref[...], wp_ref[...])
        dxn, dgain = _rms_bwd(xv, r, gain_v, dh)
        dx1_ref[...] = dx2_ref[...] + dxn

        @pl.when(pl.program_id(0) == 0)
        def _():
            dgain_ref[...] = jnp.zeros_like(dgain_ref)

        dgain_ref[...] += dgain

    tok = lambda c: pl.BlockSpec((tm, c), lambda i: (i, 0))
    return _pallas(
        body, "inproj_bwd", (t // tm,),
        [tok(d), _const((1, d)), tok(d), tok(M_COLS), tok(P_COLS), _resident(w_m.shape), _resident(w_p.shape)],
        [tok(d), _const((1, d))], [jax.ShapeDtypeStruct((t, d), F32), jax.ShapeDtypeStruct((1, d), F32)],
        (x1, gain, dx2, dm, dp, w_m, w_p), ("arbitrary",), carried)


def _adamw(w, g, m, v, name):
    rows, cols = w.shape
    tr, tc = rows, cols
    for cand in (512, 256, 128, 64):
        if rows % cand == 0 and rows > cand:
            tr = cand
            break
    if tr == rows and rows * cols > 512 * 1024 and cols % 256 == 0:
        tc = 256

    def body(w_ref, g_ref, m_ref, v_ref, delta_ref, nm_ref, nv_ref):
        gv = g_ref[...]
        nm = ADAM_B1 * m_ref[...] + (1.0 - ADAM_B1) * gv
        nv = ADAM_B2 * v_ref[...] + (1.0 - ADAM_B2) * (gv * gv)
        m_hat = nm * (1.0 / (1.0 - ADAM_B1 ** ADAM_STEP))
        v_hat = nv * (1.0 / (1.0 - ADAM_B2 ** ADAM_STEP))
        delta_ref[...] = -ADAM_LR * (m_hat / (jnp.sqrt(v_hat) + ADAM_EPS) + ADAM_WD * w_ref[...])
        nm_ref[...] = nm
        nv_ref[...] = nv

    spec = pl.BlockSpec((tr, tc), lambda i, j: (i, j))
    shape = jax.ShapeDtypeStruct((rows, cols), F32)
    return pl.pallas_call(body, name=name, grid=(rows // tr, cols // tc), in_specs=[spec] * 4, out_specs=[spec] * 3,
                          out_shape=[shape] * 3, compiler_params=_params(("arbitrary", "arbitrary")))(w, g, m, v)


def _place():
    x, y, c = lax.axis_index("x"), lax.axis_index("y"), lax.axis_index("c")
    other_chips = [(1 - x, y), (x, 1 - y), (1 - x, 1 - y)]
    return x, y, c, other_chips


def _remote(src, dst, sems, send, recv, device):
    return pltpu.make_async_remote_copy(src_ref=src, dst_ref=dst, send_sem=sems.at[send], recv_sem=sems.at[recv],
                                        device_id=device, device_id_type=MESH_ID)


def _cast_shards(shards, out_dtypes):
    n = len(shards)

    def body(*refs):
        ins, outs, stage, sems = refs[:n], refs[n:2 * n], refs[2 * n:3 * n], refs[3 * n]
        x, y, _, _ = _place()
        me = 2 * x + y
        copies = []
        for w in range(n):
            stage[w][...] = ins[w][...].astype(out_dtypes[w])
            copies.append(pltpu.make_async_copy(stage[w], outs[w].at[me], sems.at[w]))
            copies[-1].start()
        for cp in copies:
            cp.wait()

    vm = pl.BlockSpec(memory_space=pltpu.VMEM)
    return pl.pallas_call(
        body, name="cast_shards", in_specs=[vm] * n, out_specs=[ANY] * n,
        out_shape=[jax.ShapeDtypeStruct((N_CHIPS,) + s.shape, dt) for s, dt in zip(shards, out_dtypes)],
        scratch_shapes=[pltpu.VMEM(s.shape, dt) for s, dt in zip(shards, out_dtypes)] + [pltpu.SemaphoreType.DMA((n,))],
        compiler_params=_params())(*shards)


BF16_ROWS = 16


def _split_rows(rows):
    return (rows // 2) % BF16_ROWS == 0


def _half_shape(rows, cols):
    return (rows // 2, cols) if _split_rows(rows) else (rows, cols // 2)


def _half(rows, cols, which):
    if _split_rows(rows):
        return (pl.ds(pl.multiple_of(which * (rows // 2), BF16_ROWS), rows // 2), slice(None))
    return (slice(None), pl.ds(pl.multiple_of(which * (cols // 2), 128), cols // 2))


def _gather_carried(bufs):
    n = len(bufs)

    def half(w, slot, which):
        _, rows, cols = bufs[w].shape
        return (slot,) + _half(rows, cols, which)

    def start(ins, outs, sems, base):
        x, y, c, other_chips = _place()
        me = 2 * x + y
        for w in range(n):
            mine = outs[w].at[half(w, me, c)]
            for p, (px, py) in enumerate(other_chips):
                _remote(mine, mine, sems, base + 12 * w + p, base + 12 * w + 3 + p, (px, py, c)).start()

    def finish(ins, outs, sems, base):
        x, y, c, other_chips = _place()
        me = 2 * x + y
        for w in range(n):
            for p, (px, py) in enumerate(other_chips):
                got = outs[w].at[half(w, 2 * px + py, c)]
                _remote(got, got, sems, base + 12 * w + p, base + 12 * w + 3 + p, (px, py, c)).wait_recv()
                _remote(got, got, sems, base + 12 * w + 6 + p, base + 12 * w + 9 + p, (x, y, 1 - c)).start()
        for w in range(n):
            mine = outs[w].at[half(w, me, c)]
            for p, (px, py) in enumerate(other_chips):
                got = outs[w].at[half(w, 2 * px + py, c)]
                theirs = outs[w].at[half(w, 2 * px + py, 1 - c)]
                _remote(got, theirs, sems, base + 12 * w + 6 + p, base + 12 * w + 9 + p, (x, y, 1 - c)).wait()
                _remote(mine, mine, sems, base + 12 * w + p, base + 12 * w + 3 + p, (px, py, c)).wait_send()

    shapes = [jax.ShapeDtypeStruct(b.shape, b.dtype) for b in bufs]
    return _Carried(bufs, shapes, {w: w for w in range(n)}, 12 * n, start, finish)


def _swap_carried(grads):
    n = len(grads)

    def copy(w, ins, outs, sems, base):
        x, y, c, _ = _place()
        _, rows, cols = grads[w].shape
        theirs = ins[w].at[(slice(None),) + _half(rows, cols, 1 - c)]
        return _remote(theirs, outs[w], sems, base + 2 * w, base + 2 * w + 1, (x, y, 1 - c))

    def start(ins, outs, sems, base):
        for w in range(n):
            copy(w, ins, outs, sems, base).start()

    def finish(ins, outs, sems, base):
        for w in range(n):
            copy(w, ins, outs, sems, base).wait()

    shapes = [jax.ShapeDtypeStruct((g.shape[0],) + _half_shape(*g.shape[1:]), F32) for g in grads]
    return _Carried(grads, shapes, {}, 2 * n, start, finish)


def _row_tile(rows):
    for cand in (256, 176, 128, 96, 64, 32, 16):
        if rows % cand == 0:
            return cand
    return rows


def _half_block_index(split_rows, tiles, i, core):
    return (core * tiles + i, 0) if split_rows else (i, core)


def _chip_partial(grad, other, place, name):
    nblk, hr, hc = other.shape
    by_rows = _split_rows(grad.shape[1])
    tr = _row_tile(hr)
    tiles = hr // tr

    def body(place_ref, g_ref, o_ref, sum_ref, sum_bf_ref):
        s = g_ref[...] + o_ref[...]
        sum_ref[...] = s
        sum_bf_ref[...] = _bf(s)

    grid_spec = pltpu.PrefetchScalarGridSpec(
        num_scalar_prefetch=1, grid=(nblk, tiles),
        in_specs=[pl.BlockSpec((None, tr, hc), lambda b, i, place_ref: (b,) + _half_block_index(by_rows, tiles, i, place_ref[1])),
                  pl.BlockSpec((None, tr, hc), lambda b, i, place_ref: (b, i, 0))],
        out_specs=[pl.BlockSpec((None, tr, hc), lambda b, i, place_ref: (b, i, 0))] * 2)
    return pl.pallas_call(body, name=name, grid_spec=grid_spec,
                          out_shape=[jax.ShapeDtypeStruct(other.shape, F32), jax.ShapeDtypeStruct(other.shape, BF16)],
                          compiler_params=_params(("arbitrary", "arbitrary")))(place, grad, other)


def _send_carried(partials):
    n = len(partials)

    def start(ins, outs, sems, base):
        x, y, c, other_chips = _place()
        me = 2 * x + y
        for w in range(n):
            for p, (px, py) in enumerate(other_chips):
                _remote(ins[w].at[2 * px + py], outs[w].at[me], sems, base + 6 * w + p, base + 6 * w + 3 + p, (px, py, c)).start()

    def finish(ins, outs, sems, base):
        x, y, c, other_chips = _place()
        for w in range(n):
            for p, (px, py) in enumerate(other_chips):
                _remote(ins[w].at[2 * px + py], outs[w].at[2 * px + py], sems, base + 6 * w + p, base + 6 * w + 3 + p,
                        (px, py, c)).wait()

    return _Carried(partials, [jax.ShapeDtypeStruct(p.shape, BF16) for p in partials], {}, 6 * n, start, finish)


def _chip_total(own, received, place, shape, name):
    nblk, hr, hc = own.shape
    by_rows = _split_rows(shape[0])
    tr = _row_tile(hr)
    tiles = hr // tr

    def body(place_ref, own_ref, r1_ref, r2_ref, r3_ref, out_ref):
        out_ref[...] = own_ref[...] + ((r1_ref[...].astype(F32) + r2_ref[...].astype(F32)) + r3_ref[...].astype(F32))

    def slot(k):
        return pl.BlockSpec((None, tr, hc), lambda i, place_ref: ((place_ref[0] + k) % N_CHIPS, i, 0))

    grid_spec = pltpu.PrefetchScalarGridSpec(
        num_scalar_prefetch=1, grid=(tiles,), in_specs=[slot(0), slot(1), slot(2), slot(3)],
        out_specs=pl.BlockSpec((tr, hc), lambda i, place_ref: _half_block_index(by_rows, tiles, i, place_ref[1])))
    return pl.pallas_call(body, name=name, grid_spec=grid_spec, out_shape=jax.ShapeDtypeStruct(tuple(shape), F32),
                          compiler_params=_params(("arbitrary",)))(place, own, received, received, received)


def _join_carried(totals):
    n = len(totals)

    def copy(w, outs, sems, base):
        x, y, c, _ = _place()
        mine = outs[w].at[_half(*totals[w].shape, c)]
        return _remote(mine, mine, sems, base + 2 * w, base + 2 * w + 1, (x, y, 1 - c))

    def start(ins, outs, sems, base):
        for w in range(n):
            copy(w, outs, sems, base).start()

    def finish(ins, outs, sems, base):
        for w in range(n):
            copy(w, outs, sems, base).wait()

    shapes = [jax.ShapeDtypeStruct(a.shape, F32) for a in totals]
    return _Carried(totals, shapes, {w: w for w in range(n)}, 2 * n, start, finish)


def _sum_devices(vec):
    rows, n = vec.shape

    def body(v_ref, out_ref, buf, send_sems, recv_sems):
        x, y, c, _ = _place()
        me = 4 * x + 2 * y + c
        buf[me] = v_ref[...]
        sends = []
        for k in range(1, N_DEV):
            peer = (1 - x if k & 4 else x, 1 - y if k & 2 else y, 1 - c if k & 1 else c)
            cp = pltpu.make_async_remote_copy(src_ref=v_ref, dst_ref=buf.at[me], send_sem=send_sems.at[k], recv_sem=recv_sems.at[k],
                                              device_id=peer, device_id_type=MESH_ID)
            cp.start()
            sends.append(cp)
        for cp in sends:
            cp.wait()
        total = buf[0]
        for dev in range(1, N_DEV):
            total = total + buf[dev]
        out_ref[...] = total

    vm = pl.BlockSpec(memory_space=pltpu.VMEM)
    return pl.pallas_call(
        body, name="sum_devices", in_specs=[vm], out_specs=vm, out_shape=jax.ShapeDtypeStruct((rows, n), F32),
        scratch_shapes=[pltpu.VMEM((N_DEV, rows, n), F32), pltpu.SemaphoreType.DMA((N_DEV,)), pltpu.SemaphoreType.DMA((N_DEV,))],
    )(vec)


def _rope_tables(positions):
    half = QK_ROPE // 2
    inv_freq = 1.0 / (ROPE_THETA ** (jnp.arange(half, dtype=F32) / half))
    ang = positions.astype(F32).reshape(-1, 1) * inv_freq
    cos, sin = jnp.cos(ang), jnp.sin(ang)
    t = ang.shape[0]
    ones, zeros = jnp.ones((t, QK_NOPE), F32), jnp.zeros((t, QK_NOPE), F32)
    pad = HEAD_PAD - QK_DIM
    cos_full = jnp.concatenate([ones, cos, cos, ones[:, :pad]], axis=1)
    sin_signed = jnp.concatenate([zeros, -sin, sin, zeros[:, :pad]], axis=1)
    return cos_full, sin_signed


def _partials(names, grads, from_sibling, place):
    return [_chip_partial(g, o, place, "chip_partial_" + n) for n, g, o in zip(names, grads, from_sibling)]


def _totals(names, grads, partials, received, place):
    return [_chip_total(pf, r, place, g.shape[1:], "chip_total_" + n) for n, g, (pf, _), r in zip(names, grads, partials, received)]


def _kernel_layouts(full):
    d = D_MODEL
    w_in = full["w_in"]
    o_kr = Q_LORA + KV_LORA
    o_xc = o_kr + QK_ROPE
    o_gb = o_xc + d
    o_gc = o_gb + d
    o_gl = o_gc + d
    k_rope_pad = jnp.pad(w_in[o_kr:o_xc], ((QK_NOPE, HEAD_PAD - QK_DIM), (0, 0)))
    w_m = jnp.concatenate([w_in[o_gb:o_gc], w_in[o_gl:]], axis=0)
    w_p = jnp.concatenate([w_in[o_xc:o_gb], w_in[o_gc:o_gl], w_in[:o_kr], k_rope_pad], axis=0)
    w_uq = jnp.pad(full["w_uq"].reshape(Q_LORA, N_HEADS, QK_DIM), ((0, 0), (0, 0), (0, HEAD_PAD - QK_DIM))).transpose(1, 0, 2)
    w_uk = jnp.pad(full["w_uk"].reshape(KV_LORA, N_HEADS, QK_NOPE), ((0, 0), (0, 0), (0, HEAD_PAD - QK_NOPE))).transpose(1, 0, 2)
    w_uv = full["w_uv"].reshape(KV_LORA, N_HEADS, V_DIM).transpose(1, 0, 2)
    return {"w_m": w_m, "w_p": w_p, "w_uq": w_uq, "w_uk": w_uk, "w_uv": w_uv, "w_uvt": w_uv.transpose(0, 2, 1)}


def _global_layouts(g):
    d = D_MODEL
    dm, dp = g["w_m"], g["w_p"]
    o_lat = 2 * d
    o_kr = o_lat + Q_LORA + KV_LORA + QK_NOPE
    w_in = jnp.concatenate([dp[o_lat:o_lat + Q_LORA + KV_LORA], dp[o_kr:o_kr + QK_ROPE], dp[:d], dm[:d], dp[d:o_lat], dm[d:]], axis=0)
    w_uq = g["w_uq"][:, :, :QK_DIM].transpose(1, 0, 2).reshape(Q_LORA, N_HEADS * QK_DIM)
    w_uk = g["w_uk"][:, :, :QK_NOPE].transpose(1, 0, 2).reshape(KV_LORA, N_HEADS * QK_NOPE)
    w_uv = g["w_uv"].transpose(1, 0, 2).reshape(KV_LORA, N_HEADS * V_DIM)
    return {"w_in": w_in, "w_uq": w_uq, "w_uk": w_uk, "w_uv": w_uv, "w_proj_attn": g["w_pa"], "w_proj_conv": g["w_pc"],
            "w_out": g["w_out"]}


def _col_blocks(a):
    r, c = a.shape
    return a.reshape(r, N_CHIPS, c // N_CHIPS).transpose(1, 0, 2)


def _from_col_blocks(a):
    n, r, c = a.shape
    return a.transpose(1, 0, 2).reshape(r, n * c)


COL_SHARDED = ("w_uq", "w_uk", "w_uv", "w_proj_attn")
TRANSPOSED = ("ffn1_w_gate", "ffn1_w_up", "ffn2_w_gate", "ffn2_w_up", "w_in")
SMALL = (("ffn1_norm", 1024), ("mix_norm", 1024), ("gate_bias", 2048), ("q_a_norm", 384), ("kv_a_norm", 256),
         ("q_head_norm", 128), ("k_head_norm", 128), ("ffn2_norm", 1024))
WEIGHT_ORDER = ("ffn1_norm", "ffn1_w_gate", "ffn1_w_up", "ffn1_w_down", "mix_norm", "w_in", "gate_bias", "q_a_norm", "w_uq",
                "kv_a_norm", "w_uk", "w_uv", "q_head_norm", "k_head_norm", "w_proj_attn", "conv_w", "w_proj_conv", "w_out",
                "ffn2_norm", "ffn2_w_gate", "ffn2_w_up", "ffn2_w_down")
MATRICES = ("ffn1_w_gate", "ffn1_w_up", "ffn1_w_down", "w_in", "w_uq", "w_uk", "w_uv", "w_proj_attn", "w_proj_conv", "w_out",
            "ffn2_w_gate", "ffn2_w_up", "ffn2_w_down")
GROUP_FFN1 = ("ffn1_w_gate", "ffn1_w_up", "ffn1_w_down")
GROUP_IN = ("w_in", "w_uq", "w_uk", "w_uv", "conv_w")
GROUP_MIX = ("w_proj_attn", "w_proj_conv", "w_out")
GROUP_FFN2 = ("ffn2_w_gate", "ffn2_w_up", "ffn2_w_down")
GROUP_MID = ("w_in", "w_uq", "w_uk", "w_uv", "w_proj_attn", "w_proj_conv", "w_out")


def _pad_lanes(a, n):
    return jnp.pad(a.reshape(1, -1), ((0, 0), (0, n - a.size)))


def kernel(x, positions, ffn1_norm, ffn1_w_gate, ffn1_w_up, ffn1_w_down, mix_norm, w_in, gate_bias, q_a_norm, w_uq, kv_a_norm, w_uk, w_uv, q_head_norm, k_head_norm, w_proj_attn, conv_w, w_proj_conv, w_out, ffn2_norm, ffn2_w_gate, ffn2_w_up, ffn2_w_down, loss_target, m_ffn1_norm, m_ffn1_w_gate, m_ffn1_w_up, m_ffn1_w_down, m_mix_norm, m_w_in, m_gate_bias, m_q_a_norm, m_w_uq, m_kv_a_norm, m_w_uk, m_w_uv, m_q_head_norm, m_k_head_norm, m_w_proj_attn, m_conv_w, m_w_proj_conv, m_w_out, m_ffn2_norm, m_ffn2_w_gate, m_ffn2_w_up, m_ffn2_w_down, v_ffn1_norm, v_ffn1_w_gate, v_ffn1_w_up, v_ffn1_w_down, v_mix_norm, v_w_in, v_gate_bias, v_q_a_norm, v_w_uq, v_kv_a_norm, v_w_uk, v_w_uv, v_q_head_norm, v_k_head_norm, v_w_proj_attn, v_conv_w, v_w_proj_conv, v_w_out, v_ffn2_norm, v_ffn2_w_gate, v_ffn2_w_up, v_ffn2_w_down):
    args = dict(locals())
    view = lambda n, a: a.T if n in TRANSPOSED else a
    weights = {n: view(n, args[n]) for n in WEIGHT_ORDER}
    moments_m = {n: view(n, args["m_" + n]) for n in WEIGHT_ORDER}
    moments_v = {n: view(n, args["v_" + n]) for n in WEIGHT_ORDER}
    nb, seq, d = x.shape
    t = nb * seq
    chip = (2 * lax.axis_index("x") + lax.axis_index("y")).astype(jnp.int32)
    place = jnp.stack([chip, lax.axis_index("c").astype(jnp.int32)])
    grads, delta, new_m, new_v = {}, {}, {}, {}

    def adamw(names):
        for n in names:
            delta[n], new_m[n], new_v[n] = _adamw(weights[n], grads[n], moments_m[n], moments_v[n], "adamw_" + n)

    conv_rows = conv_w.shape[0]
    conv_shard = jnp.pad(conv_w, ((0, 16 - conv_rows), (0, 0)))
    bufs = dict(zip(MATRICES + ("conv_w",), _cast_shards([weights[n] for n in MATRICES] + [conv_shard],
                                                         [BF16] * len(MATRICES) + [F32])))
    blocks = dict(zip(GROUP_FFN1, _run(_gather_carried([bufs[n] for n in GROUP_FFN1]), "gather_ffn1")))
    p = {n: _pad_lanes(weights[n], size) for n, size in SMALL}
    cos, sin = _rope_tables(positions)
    x_tok = x.reshape(t, d)

    gather_in = _gather_carried([bufs[n] for n in GROUP_IN])
    x1, gate1, up1, act1 = _ffn_fwd(x_tok, p["ffn1_norm"], blocks["ffn1_w_gate"], blocks["ffn1_w_up"], blocks["ffn1_w_down"], None,
                                    "ffn1_fwd", gather_in)
    blocks.update(zip(GROUP_IN, gather_in.results))
    w = _kernel_layouts({"w_in": blocks["w_in"].reshape(-1, d), **{n: _from_col_blocks(blocks[n]) for n in ("w_uq", "w_uk", "w_uv")}})
    p["conv_w"] = _from_col_blocks(blocks["conv_w"])[:conv_rows]

    gather_rest = _gather_carried([bufs[n] for n in GROUP_MIX + GROUP_FFN2])
    h2b, big, lat, q, k, v, vt = _inproj_fwd(x1, p["mix_norm"], w["w_m"], w["w_p"], p["q_a_norm"], p["kv_a_norm"], p["q_head_norm"],
                                             p["k_head_norm"], w["w_uq"], w["w_uk"], w["w_uv"], w["w_uvt"], cos, sin, gather_rest)
    blocks.update(zip(GROUP_MIX + GROUP_FFN2, gather_rest.results))
    w_pa = _from_col_blocks(blocks["w_proj_attn"])
    w_pc, w_out_full = blocks["w_proj_conv"].reshape(-1, d), blocks["w_out"].reshape(-1, d)
    wg2, wu2, wd2 = blocks["ffn2_w_gate"], blocks["ffn2_w_up"], blocks["ffn2_w_down"]

    o, lse = _attn_fwd(q, k, vt, seq)
    x2 = _mix_fwd(x1, o, big, p["gate_bias"], p["conv_w"], w_pa, w_pc, w_out_full, seq)
    dx3, gate2, up2, act2, loss = _ffn_fwd(x2, p["ffn2_norm"], wg2, wu2, wd2, loss_target.reshape(t, d), "ffn2_fwd")

    dx2, dg_ffn2, hb2, dgate2, dup2, dyb2 = _ffn_bwd_x(x2, p["ffn2_norm"], dx3, gate2, up2, wg2, wu2, wd2, "ffn2_bwd")
    g_ffn2 = [_tn_matmul(dgate2, hb2, "ffn2_dw_gate"), _tn_matmul(dup2, hb2, "ffn2_dw_up"), _tn_matmul(act2, dyb2, "ffn2_dw_down")]
    swap = _swap_carried(g_ffn2)
    do, delta_o, dz, dm, dbias, dw_pa, dw_pc, dw_out = _mix_bwd(dx2, o, big, p["gate_bias"], p["conv_w"], w_pa, w_pc, w_out_full, seq,
                                                                swap)
    part = _partials(GROUP_FFN2, g_ffn2, swap.results, place)
    send = _send_carried([pb for _, pb in part])
    dq, dk, dv = _attn_bwd(q, k, v, do, lse, delta_o.reshape(N_HEADS // ATTN_BWD_HEADS, ATTN_BWD_HEADS, -1), seq, send)
    join = _join_carried(_totals(GROUP_FFN2, g_ffn2, part, send.results, place))
    dp, dw_uq, dw_uk, dw_uv, dqa, dkva, dqh, dkh, dcw = _prep_bwd(
        lat, big, dz, dq, dk, dv, p["q_a_norm"], p["kv_a_norm"], p["q_head_norm"], p["k_head_norm"], w["w_uq"], w["w_uk"],
        w["w_uv"], cos, sin, p["conv_w"], seq, join)
    grads.update(zip(GROUP_FFN2, join.results))
    adamw(GROUP_FFN2)

    gg = _global_layouts({"w_m": _tn_matmul(dm, h2b, "dw_in_m", split_k=2), "w_p": _tn_matmul(dp, h2b, "dw_in_p", split_k=2),
                          "w_uq": dw_uq, "w_uk": dw_uk, "w_uv": dw_uv, "w_pa": dw_pa, "w_pc": dw_pc, "w_out": dw_out})
    g_mid = [_col_blocks(gg[n]) if n in COL_SHARDED else gg[n].reshape(N_CHIPS, -1, gg[n].shape[-1]) for n in GROUP_MID]
    swap = _swap_carried(g_mid)
    dx1, dg_mix = _inproj_bwd(x1, p["mix_norm"], dx2, dm, dp, w["w_m"], w["w_p"], swap)
    grad_x, dg_ffn1, hb1, dgate1, dup1, dyb1 = _ffn_bwd_x(x_tok, p["ffn1_norm"], dx1, gate1, up1, blocks["ffn1_w_gate"],
                                                         blocks["ffn1_w_up"], blocks["ffn1_w_down"], "ffn1_bwd")

    small_grads = {"ffn1_norm": dg_ffn1, "mix_norm": dg_mix, "gate_bias": dbias, "q_a_norm": dqa, "kv_a_norm": dkva,
                   "q_head_norm": dqh, "k_head_norm": dkh, "ffn2_norm": dg_ffn2}
    packed = jnp.concatenate([small_grads[n] for n, _ in SMALL] + [dcw.reshape(1, -1), loss], axis=1)
    total = _sum_devices(jnp.pad(packed, ((0, 7), (0, 0))))[0:1]
    n_small = sum(size for _, size in SMALL)
    conv_cols = conv_w.shape[1]
    conv_total = total[:, n_small:n_small + conv_rows * d].reshape(conv_rows, d)
    grads["conv_w"] = lax.dynamic_slice_in_dim(conv_total, chip * conv_cols, conv_cols, axis=1)
    loss_total = total[0, n_small + conv_rows * d]

    part = _partials(GROUP_MID, g_mid, swap.results, place)
    send = _send_carried([pb for _, pb in part])
    g_gate = _tn_matmul(dgate1, hb1, "ffn1_dw_gate", carried=send)
    join = _join_carried(_totals(GROUP_MID, g_mid, part, send.results, place))
    swap_gate = _swap_carried([g_gate])
    g_up = _tn_matmul(dup1, hb1, "ffn1_dw_up", carried=_both(join, swap_gate))
    grads.update(zip(GROUP_MID, join.results))
    adamw(GROUP_MID + ("conv_w",))
    part_gate = _partials(GROUP_FFN1[:1], [g_gate], swap_gate.results, place)
    send_gate, swap_up = _send_carried([part_gate[0][1]]), _swap_carried([g_up])
    g_down = _tn_matmul(act1, dyb1, "ffn1_dw_down", carried=_both(send_gate, swap_up))
    join_gate = _join_carried(_totals(GROUP_FFN1[:1], [g_gate], part_gate, send_gate.results, place))
    part_up = _partials(GROUP_FFN1[1:2], [g_up], swap_up.results, place)
    send_up, swap_down = _send_carried([part_up[0][1]]), _swap_carried([g_down])
    _run(_both(_both(send_up, swap_down), join_gate), "reduce_tail_1")
    grads["ffn1_w_gate"] = join_gate.results[0]
    adamw(GROUP_FFN1[:1])
    join_up = _join_carried(_totals(GROUP_FFN1[1:2], [g_up], part_up, send_up.results, place))
    part_down = _partials(GROUP_FFN1[2:], [g_down], swap_down.results, place)
    send_down = _send_carried([part_down[0][1]])
    _run(_both(send_down, join_up), "reduce_tail_2")
    grads["ffn1_w_up"] = join_up.results[0]
    adamw(GROUP_FFN1[1:2])
    grads["ffn1_w_down"] = _run(_join_carried(_totals(GROUP_FFN1[2:], [g_down], part_down, send_down.results, place)),
                                "reduce_tail_3")[0]
    adamw(GROUP_FFN1[2:])

    pack = lambda src: jnp.concatenate([_pad_lanes(src[n], size) for n, size in SMALL], axis=1)
    sd, sm, sv = _adamw(pack(weights), total[:, :n_small], pack(moments_m), pack(moments_v), "adamw_small")
    off = 0
    for n, size in SMALL:
        real = weights[n].size
        grads[n] = total[0, off:off + real]
        delta[n], new_m[n], new_v[n] = sd[0, off:off + real], sm[0, off:off + real], sv[0, off:off + real]
        off += size

    return (loss_total, grad_x.reshape(nb, seq, d), *[view(n, src[n]) for src in (grads, delta, new_m, new_v) for n in WEIGHT_ORDER])
```

```python
import functools

import jax
import jax.numpy as jnp
from jax import lax
from jax.experimental import pallas as pl
from jax.experimental.pallas import tpu as pltpu

F32 = jnp.float32
BF16 = jnp.bfloat16

D_MODEL = 1024
N_HEADS = 8
QK_NOPE = 64
QK_ROPE = 32
QK_DIM = QK_NOPE + QK_ROPE
V_DIM = 64
HEAD_PAD = 128
Q_LORA = 384
KV_LORA = 256
ROPE_THETA = 10000.0
NORM_EPS = 1e-6
ATTN_SCALE = QK_DIM ** -0.5
MASK_VALUE = -1e30
N_CHIPS = 4
N_DEV = 8

ADAM_LR = 0.001
ADAM_B1 = 0.9
ADAM_B2 = 0.999
ADAM_EPS = 1e-08
ADAM_WD = 0.01
ADAM_STEP = 10

TOKEN_TILE = 256
ATTN_TILE = 512
TN_TILE = 512
VMEM_LIMIT = 56 * 1024 * 1024

M_COLS = 3 * D_MODEL
P_COLS = 2 * D_MODEL + Q_LORA + KV_LORA + HEAD_PAD
BIG_COLS = 5 * D_MODEL
LAT_COLS = Q_LORA + KV_LORA + HEAD_PAD

MESH_ID = pl.DeviceIdType.MESH
ANY = pl.BlockSpec(memory_space=pl.ANY)


def _params(semantics=None):
    return pltpu.CompilerParams(dimension_semantics=semantics, vmem_limit_bytes=VMEM_LIMIT)


class _Carried:
    def __init__(self, operands, out_shapes, aliases, n_sems, start, finish):
        self.operands, self.out_shapes, self.aliases, self.n_sems = list(operands), list(out_shapes), dict(aliases), n_sems
        self.start, self.finish = start, finish
        self.results = None


def _both(a, b):
    na, nao = len(a.operands), len(a.out_shapes)

    def start(ins, outs, sems, base):
        a.start(ins[:na], outs[:nao], sems, base)
        b.start(ins[na:], outs[nao:], sems, base + a.n_sems)

    def finish(ins, outs, sems, base):
        a.finish(ins[:na], outs[:nao], sems, base)
        b.finish(ins[na:], outs[nao:], sems, base + a.n_sems)

    aliases = dict(a.aliases)
    aliases.update({na + i: nao + o for i, o in b.aliases.items()})
    both = _Carried(a.operands + b.operands, a.out_shapes + b.out_shapes, aliases, a.n_sems + b.n_sems, start, finish)
    both.parts = (a, b)
    return both


def _set_results(carried, results):
    carried.results = list(results)
    if hasattr(carried, "parts"):
        a, b = carried.parts
        _set_results(a, results[:len(a.out_shapes)])
        _set_results(b, results[len(a.out_shapes):])


def _pallas(body, name, grid, in_specs, out_specs, out_shape, args, semantics, carried=None):
    if carried is None:
        return pl.pallas_call(body, name=name, grid=grid, in_specs=in_specs, out_specs=out_specs, out_shape=out_shape,
                              compiler_params=_params(semantics))(*args)
    n_in, n_out, n_ci, n_co = len(in_specs), len(out_specs), len(carried.operands), len(carried.out_shapes)

    def wrapped(*refs):
        ins, c_ins = refs[:n_in], refs[n_in:n_in + n_ci]
        outs, c_outs = refs[n_in + n_ci:n_in + n_ci + n_out], refs[n_in + n_ci + n_out:n_in + n_ci + n_out + n_co]
        sems = refs[-1]
        first = pl.program_id(0) == 0
        last = pl.program_id(0) == grid[0] - 1
        for axis in range(1, len(grid)):
            first = jnp.logical_and(first, pl.program_id(axis) == 0)
            last = jnp.logical_and(last, pl.program_id(axis) == grid[axis] - 1)

        @pl.when(first)
        def _():
            carried.start(c_ins, c_outs, sems, 0)

        body(*ins, *outs)

        @pl.when(last)
        def _():
            carried.finish(c_ins, c_outs, sems, 0)

    results = pl.pallas_call(
        wrapped, name=name, grid=grid, in_specs=list(in_specs) + [ANY] * n_ci, out_specs=list(out_specs) + [ANY] * n_co,
        out_shape=list(out_shape) + carried.out_shapes,
        input_output_aliases={n_in + i: n_out + o for i, o in carried.aliases.items()},
        scratch_shapes=[pltpu.SemaphoreType.DMA((carried.n_sems,))], compiler_params=_params(semantics))(*args, *carried.operands)
    _set_results(carried, results[n_out:])
    return results[:n_out]


def _run(carried, name):
    n_ci, n_co = len(carried.operands), len(carried.out_shapes)

    def body(*refs):
        carried.start(refs[:n_ci], refs[n_ci:n_ci + n_co], refs[-1], 0)
        carried.finish(refs[:n_ci], refs[n_ci:n_ci + n_co], refs[-1], 0)

    results = pl.pallas_call(body, name=name, in_specs=[ANY] * n_ci, out_specs=[ANY] * n_co, out_shape=carried.out_shapes,
                             input_output_aliases=carried.aliases,
                             scratch_shapes=[pltpu.SemaphoreType.DMA((carried.n_sems,))])(*carried.operands)
    _set_results(carried, results)
    return carried.results


def _resident(shape):
    nd = len(shape)
    return pl.BlockSpec(shape, lambda *_: (0,) * nd, pipeline_mode=pl.Buffered(1))


def _const(shape):
    nd = len(shape)
    return pl.BlockSpec(shape, lambda *_: (0,) * nd)


def _mm(a, b):
    return jnp.dot(a, b, preferred_element_type=F32)


def _mm_nt(a, b):
    return lax.dot_general(a, b, (((1,), (1,)), ((), ())), preferred_element_type=F32)


def _mm_tn(a, b):
    return lax.dot_general(a, b, (((0,), (0,)), ((), ())), preferred_element_type=F32)


def _bf(a):
    return a.astype(BF16)


def _sigmoid(a):
    return 1.0 / (1.0 + jnp.exp(-a))


def _rms(x, gain, n=None):
    n = x.shape[-1] if n is None else n
    r = lax.rsqrt(jnp.sum(x * x, axis=-1, keepdims=True) * (1.0 / n) + NORM_EPS)
    return (x * r) * gain, r


def _rms_bwd(x, r, gain, dh, n=None):
    n = x.shape[-1] if n is None else n
    u = dh * gain
    dx = r * u - x * ((r * r * r) * (jnp.sum(u * x, axis=-1, keepdims=True) * (1.0 / n)))
    dgain = jnp.sum(dh * (x * r), axis=0, keepdims=True)
    return dx, dgain


def _rope_swap(t):
    lane = lax.broadcasted_iota(jnp.int32, t.shape, 1)
    lo = (lane >= QK_NOPE) & (lane < QK_NOPE + QK_ROPE // 2)
    hi = (lane >= QK_NOPE + QK_ROPE // 2) & (lane < QK_DIM)
    up = pltpu.roll(t, HEAD_PAD - QK_ROPE // 2, 1)
    down = pltpu.roll(t, QK_ROPE // 2, 1)
    return jnp.where(lo, up, jnp.where(hi, down, 0.0))


def _rope(t, cos, sin):
    return t * cos + _rope_swap(t) * sin


def _rope_bwd(dt, cos, sin):
    return dt * cos + _rope_swap(dt * sin)


def _shift_down(u, prev8, k):
    s = pltpu.roll(u, k, 0)
    p = pltpu.roll(prev8, k, 0)
    row = lax.broadcasted_iota(jnp.int32, prev8.shape, 0)
    top = jnp.where(row < k, p, s[:8])
    return jnp.concatenate([top, s[8:]], axis=0)


def _shift_up(d, next8, k):
    tm = d.shape[0]
    s = pltpu.roll(d, tm - k, 0)
    n = pltpu.roll(next8, 8 - k, 0)
    row = lax.broadcasted_iota(jnp.int32, next8.shape, 0)
    bot = jnp.where(row >= 8 - k, n, s[tm - 8:])
    return jnp.concatenate([s[:tm - 8], bot], axis=0)


def _ffn_fwd(x, gain, wg, wu, wd, target, name, carried=None):
    t, d = x.shape
    nb, f, _ = wg.shape
    tm = TOKEN_TILE
    with_loss = target is not None

    def body(*refs):
        if with_loss:
            x_ref, g_ref, wg_ref, wu_ref, wd_ref, t_ref, out_ref, gate_ref, up_ref, act_ref, loss_ref = refs
        else:
            x_ref, g_ref, wg_ref, wu_ref, wd_ref, out_ref, gate_ref, up_ref, act_ref = refs
        xv = x_ref[...]
        h, _ = _rms(xv, g_ref[...])
        hb = _bf(h)
        y = jnp.zeros((tm, d), F32)
        for j in range(nb):
            gate = _mm_nt(hb, wg_ref[j])
            up = _mm_nt(hb, wu_ref[j])
            act = _bf((gate * _sigmoid(gate)) * up)
            y = y + _mm(act, wd_ref[j])
            gate_ref[j] = _bf(gate)
            up_ref[j] = _bf(up)
            act_ref[j] = act
        out = xv + 0.5 * y
        if with_loss:
            err = out - t_ref[...]
            out_ref[...] = err * (1.0 / d)

            @pl.when(pl.program_id(0) == 0)
            def _():
                loss_ref[...] = jnp.zeros_like(loss_ref)

            part = jnp.sum(jnp.sum(err * err, axis=1, keepdims=True), axis=0, keepdims=True)
            loss_ref[...] += jnp.broadcast_to(part * (0.5 / d), loss_ref.shape)
        else:
            out_ref[...] = out

    tok = pl.BlockSpec((tm, d), lambda i: (i, 0))
    blk = pl.BlockSpec((nb, tm, f), lambda i: (0, i, 0))
    in_specs = [tok, _const((1, d)), _resident(wg.shape), _resident(wu.shape), _resident(wd.shape)]
    args = [x, gain, wg, wu, wd]
    out_shape = [jax.ShapeDtypeStruct((t, d), F32)] + [jax.ShapeDtypeStruct((nb, t, f), BF16)] * 3
    out_specs = [tok, blk, blk, blk]
    if with_loss:
        in_specs.append(tok)
        args.append(target)
        out_shape.append(jax.ShapeDtypeStruct((1, 128), F32))
        out_specs.append(_const((1, 128)))
    return _pallas(body, name, (t // tm,), in_specs, out_specs, out_shape, args, ("arbitrary",), carried)


def _ffn_bwd_x(x, gain, dout, gate, up, wg, wu, wd, name, carried=None):
    t, d = x.shape
    nb, f, _ = wg.shape
    tm = TOKEN_TILE

    def body(x_ref, g_ref, dout_ref, gate_ref, up_ref, wg_ref, wu_ref, wd_ref,
             dx_ref, dgain_ref, hb_ref, dgate_ref, dup_ref, dyb_ref):
        xv = x_ref[...]
        gain_v = g_ref[...]
        h, r = _rms(xv, gain_v)
        hb_ref[...] = _bf(h)
        dout_v = dout_ref[...]
        dyb = _bf(0.5 * dout_v)
        dyb_ref[...] = dyb
        dh = jnp.zeros((tm, d), F32)
        for j in range(nb):
            gt = gate_ref[j].astype(F32)
            uv = up_ref[j].astype(F32)
            s = _sigmoid(gt)
            dact = _mm_nt(dyb, wd_ref[j])
            dup = _bf(dact * (gt * s))
            dgate = _bf((dact * uv) * (s * (1.0 + gt * (1.0 - s))))
            dh = dh + _mm(dgate, wg_ref[j]) + _mm(dup, wu_ref[j])
            dgate_ref[j] = dgate
            dup_ref[j] = dup
        dxn, dgain = _rms_bwd(xv, r, gain_v, dh)
        dx_ref[...] = dout_v + dxn

        @pl.when(pl.program_id(0) == 0)
        def _():
            dgain_ref[...] = jnp.zeros_like(dgain_ref)

        dgain_ref[...] += dgain

    tok = pl.BlockSpec((tm, d), lambda i: (i, 0))
    blk = pl.BlockSpec((nb, tm, f), lambda i: (0, i, 0))
    return _pallas(
        body, name, (t // tm,),
        [tok, _const((1, d)), tok, blk, blk, _resident(wg.shape), _resident(wu.shape), _resident(wd.shape)],
        [tok, _const((1, d)), tok, blk, blk, tok],
        [jax.ShapeDtypeStruct((t, d), F32), jax.ShapeDtypeStruct((1, d), F32), jax.ShapeDtypeStruct((t, d), BF16),
         jax.ShapeDtypeStruct((nb, t, f), BF16), jax.ShapeDtypeStruct((nb, t, f), BF16), jax.ShapeDtypeStruct((t, d), BF16)],
        (x, gain, dout, gate, up, wg, wu, wd), ("arbitrary",), carried)


def _tn_matmul(a, b, name, split_k=1, carried=None):
    t = a.shape[-2]
    k = a.shape[-1]
    n = b.shape[-1]
    tt = min(TN_TILE, t)
    nt = t // tt

    def body(a_ref, b_ref, o_ref):
        @pl.when(pl.program_id(1) == 0)
        def _():
            o_ref[...] = jnp.zeros_like(o_ref)

        o_ref[...] += _mm_tn(a_ref[...], b_ref[...])

    if split_k > 1:
        assert a.ndim == 2 and b.ndim == 2 and k % (split_k * 128) == 0
        tk = k // split_k
        g = split_k
        a_spec = pl.BlockSpec((tt, tk), lambda gi, ti: (ti, gi))
        b_spec = pl.BlockSpec((tt, n), lambda gi, ti: (ti, 0))
        o_spec = pl.BlockSpec((tk, n), lambda gi, ti: (gi, 0))
        out_shape = jax.ShapeDtypeStruct((k, n), F32)
    else:
        g = a.shape[0] if a.ndim == 3 else b.shape[0]
        a_spec = (pl.BlockSpec((None, tt, k), lambda gi, ti: (gi, ti, 0)) if a.ndim == 3
                  else pl.BlockSpec((tt, k), lambda gi, ti: (ti, 0)))
        b_spec = (pl.BlockSpec((None, tt, n), lambda gi, ti: (gi, ti, 0)) if b.ndim == 3
                  else pl.BlockSpec((tt, n), lambda gi, ti: (ti, 0)))
        o_spec = pl.BlockSpec((None, k, n), lambda gi, ti: (gi, 0, 0))
        out_shape = jax.ShapeDtypeStruct((g, k, n), F32)
    return _pallas(body, name, (g, nt), [a_spec, b_spec], [o_spec], [out_shape], (a, b), ("arbitrary", "arbitrary"), carried)[0]


def _inproj_fwd(x1, gain, w_m, w_p, qa_gain, kva_gain, qh_gain, kh_gain, w_uq, w_uk, w_uv, w_uvt, cos, sin, carried=None):
    t, d = x1.shape
    tm = TOKEN_TILE

    def body(x_ref, g_ref, wm_ref, wp_ref, qa_ref, kva_ref, qh_ref, kh_ref, wuq_ref, wuk_ref, wuv_ref, wuvt_ref, cos_ref, sin_ref,
             hb_ref, big_ref, lat_ref, q_ref, k_ref, v_ref, vt_ref):
        h, _ = _rms(x_ref[...], g_ref[...])
        hb = _bf(h)
        hb_ref[...] = hb
        big_ref[:, :M_COLS] = _mm_nt(hb, wm_ref[...])
        pp = _mm_nt(hb, wp_ref[...])
        big_ref[:, M_COLS:] = pp[:, :2 * D_MODEL]
        lat = pp[:, 2 * D_MODEL:]
        lat_ref[...] = lat
        cq, _ = _rms(lat[:, :Q_LORA], qa_ref[...])
        ckv, _ = _rms(lat[:, Q_LORA:Q_LORA + KV_LORA], kva_ref[...])
        k_rope = lat[:, Q_LORA + KV_LORA:]
        cqb = _bf(cq)
        ckvb = _bf(ckv)
        cos_v = cos_ref[...]
        sin_v = sin_ref[...]
        for hd in range(N_HEADS):
            qn, _ = _rms(_mm(cqb, wuq_ref[hd]), qh_ref[...], QK_DIM)
            q_ref[hd] = _bf(_rope(qn, cos_v, sin_v))
            kn, _ = _rms(_mm(ckvb, wuk_ref[hd]) + k_rope, kh_ref[...], QK_DIM)
            k_ref[hd] = _bf(_rope(kn, cos_v, sin_v))
            v_ref[hd] = _bf(_mm(ckvb, wuv_ref[hd]))
            vt_ref[hd] = _bf(_mm_nt(wuvt_ref[hd], ckvb))

    tok = lambda c: pl.BlockSpec((tm, c), lambda i: (i, 0))
    head = lambda c: pl.BlockSpec((N_HEADS, tm, c), lambda i: (0, i, 0))
    return _pallas(
        body, "inproj_fwd", (t // tm,),
        [tok(d), _const((1, d)), _resident(w_m.shape), _resident(w_p.shape), _const((1, Q_LORA)), _const((1, KV_LORA)),
         _const((1, HEAD_PAD)), _const((1, HEAD_PAD)), _resident(w_uq.shape), _resident(w_uk.shape),
         _resident(w_uv.shape), _resident(w_uvt.shape), tok(HEAD_PAD), tok(HEAD_PAD)],
        [tok(d), tok(BIG_COLS), tok(LAT_COLS), head(HEAD_PAD), head(HEAD_PAD), head(V_DIM),
         pl.BlockSpec((N_HEADS, V_DIM, tm), lambda i: (0, 0, i))],
        [jax.ShapeDtypeStruct((t, d), BF16), jax.ShapeDtypeStruct((t, BIG_COLS), F32),
         jax.ShapeDtypeStruct((t, LAT_COLS), F32), jax.ShapeDtypeStruct((N_HEADS, t, HEAD_PAD), BF16),
         jax.ShapeDtypeStruct((N_HEADS, t, HEAD_PAD), BF16), jax.ShapeDtypeStruct((N_HEADS, t, V_DIM), BF16),
         jax.ShapeDtypeStruct((N_HEADS, V_DIM, t), BF16)],
        (x1, gain, w_m, w_p, qa_gain, kva_gain, qh_gain, kh_gain, w_uq, w_uk, w_uv, w_uvt, cos, sin), ("arbitrary",), carried)


EXP2_SCALE = ATTN_SCALE * 1.4426950408889634


def _diagonal_keep(tk, tq):
    return lax.broadcasted_iota(jnp.int32, (tk, tq), 0) <= lax.broadcasted_iota(jnp.int32, (tk, tq), 1)


def _attn_fwd(q, k, vt, seq):
    _, t, _ = q.shape
    nseq = t // seq
    tq = tk = ATTN_TILE
    nq = seq // tq

    def body(q_ref, k_ref, vt_ref, o_ref, lse_ref):
        i = pl.program_id(1)
        qs = [q_ref[h] for h in range(N_HEADS)]
        keep = _diagonal_keep(tk, tq)

        def tile(h, state, k0, diagonal):
            m, l, acc = state
            st = _mm_nt(k_ref[h, pl.ds(k0, tk), :], qs[h])
            if diagonal:
                st = jnp.where(keep, st, MASK_VALUE)
            m_new = jnp.maximum(m, jnp.max(st, axis=0, keepdims=True))
            pt = jnp.exp2((st - m_new) * EXP2_SCALE)
            alpha = jnp.exp2((m - m_new) * EXP2_SCALE)
            l_new = alpha * l + jnp.sum(pt, axis=0, keepdims=True)
            return m_new, l_new, alpha * acc + _mm(vt_ref[h, :, pl.ds(k0, tk)], _bf(pt))

        def step(j, states):
            k0 = pl.multiple_of(j * tk, tk)
            return tuple(tile(h, states[h], k0, False) for h in range(N_HEADS))

        init = tuple((jnp.full((1, tq), MASK_VALUE, F32), jnp.zeros((1, tq), F32), jnp.zeros((V_DIM, tq), F32))
                     for _ in range(N_HEADS))
        states = lax.fori_loop(0, i, step, init)
        k0 = pl.multiple_of(i * tk, tk)
        outs = []
        for h in range(N_HEADS):
            m, l, acc = tile(h, states[h], k0, True)
            outs.append((acc / l).T)
            lse_ref[h] = m * EXP2_SCALE + jnp.log2(l)
        o_ref[...] = _bf(jnp.concatenate(outs, axis=-1))

    return pl.pallas_call(
        body, name="attn_fwd", grid=(nseq, nq),
        in_specs=[pl.BlockSpec((N_HEADS, tq, HEAD_PAD), lambda b, i: (0, b * nq + i, 0)),
                  pl.BlockSpec((N_HEADS, seq, HEAD_PAD), lambda b, i: (0, b, 0)),
                  pl.BlockSpec((N_HEADS, V_DIM, seq), lambda b, i: (0, 0, b))],
        out_specs=[pl.BlockSpec((tq, N_HEADS * V_DIM), lambda b, i: (b * nq + i, 0)),
                   pl.BlockSpec((N_HEADS, 1, tq), lambda b, i: (0, 0, b * nq + i))],
        out_shape=[jax.ShapeDtypeStruct((t, N_HEADS * V_DIM), BF16), jax.ShapeDtypeStruct((N_HEADS, 1, t), F32)],
        compiler_params=_params(("arbitrary", "arbitrary")))(q, k, vt)


ATTN_BWD_HEADS = 4


def _attn_bwd(q, k, v, do, lse, delta, seq, carried=None):
    _, t, _ = q.shape
    nseq = t // seq
    tq = tk = ATTN_TILE
    n = seq // tq
    hb = ATTN_BWD_HEADS

    def body(q_ref, k_ref, v_ref, do_ref, lse_ref, delta_ref, dq_ref, dk_ref, dv_ref):
        dq_ref[...] = jnp.zeros_like(dq_ref)
        dk_ref[...] = jnp.zeros_like(dk_ref)
        dv_ref[...] = jnp.zeros_like(dv_ref)
        keep = _diagonal_keep(tk, tq)

        def tile(h, k0, q0, diagonal):
            kj = k_ref[h, pl.ds(k0, tk), :]
            qi = q_ref[h, pl.ds(q0, tq), :]
            doi = _bf(do_ref[pl.ds(q0, tq), h * V_DIM:(h + 1) * V_DIM])
            st = _mm_nt(kj, qi)
            if diagonal:
                st = jnp.where(keep, st, MASK_VALUE)
            pt = jnp.exp2(st * EXP2_SCALE - lse_ref[h, :, pl.ds(q0, tq)])
            dv_ref[h, pl.ds(k0, tk), :] += _mm(_bf(pt), doi)
            dpt = _mm_nt(v_ref[h, pl.ds(k0, tk), :], doi)
            dst = _bf((pt * (dpt - delta_ref[pl.ds(h, 1), pl.ds(q0, tq)])) * ATTN_SCALE)
            dk_ref[h, pl.ds(k0, tk), :] += _mm(dst, qi)
            dq_ref[h, pl.ds(q0, tq), :] += _mm_tn(dst, kj)

        def kv_step(j, _):
            k0 = pl.multiple_of(j * tk, tk)
            for h in range(hb):
                tile(h, k0, k0, True)

            def q_step(i, _):
                q0 = pl.multiple_of(i * tq, tq)
                for h in range(hb):
                    tile(h, k0, q0, False)
                return 0

            lax.fori_loop(j + 1, n, q_step, 0)
            return 0

        lax.fori_loop(0, n, kv_step, 0)

    hspec = lambda c: pl.BlockSpec((hb, seq, c), lambda b, g: (g, b, 0))
    return _pallas(
        body, "attn_bwd", (nseq, N_HEADS // hb),
        [hspec(HEAD_PAD), hspec(HEAD_PAD), hspec(V_DIM), pl.BlockSpec((seq, hb * V_DIM), lambda b, g: (b, g)),
         pl.BlockSpec((hb, 1, seq), lambda b, g: (g, 0, b)), pl.BlockSpec((None, hb, seq), lambda b, g: (g, 0, b))],
        [hspec(HEAD_PAD), hspec(HEAD_PAD), hspec(V_DIM)],
        [jax.ShapeDtypeStruct((N_HEADS, t, HEAD_PAD), F32), jax.ShapeDtypeStruct((N_HEADS, t, HEAD_PAD), F32),
         jax.ShapeDtypeStruct((N_HEADS, t, V_DIM), F32)],
        (q, k, v, do, lse, delta), ("arbitrary", "arbitrary"), carried)


def _mixer_values(o_ref, gb_ref, gla_ref, glb_ref, xc_ref, gc_ref, xcp_ref, gcp_ref, bias_ref, cw_ref, wpa_ref, wpc_ref,
                  first_of_seq):
    gb = gb_ref[...]
    u = gc_ref[...] * xc_ref[...]
    u_prev = jnp.where(first_of_seq, 0.0, gcp_ref[...] * xcp_ref[...])
    cw = cw_ref[...]
    z = cw[2:3] * u + cw[1:2] * _shift_down(u, u_prev, 1) + cw[0:1] * _shift_down(u, u_prev, 2)
    gbz = _bf(gb * z)
    y_b = _mm(gbz, wpc_ref[...])
    y_a = _mm(o_ref[...], wpa_ref[...])
    bias = bias_ref[...]
    gate_a = _sigmoid(gla_ref[...] + bias[:, :D_MODEL])
    gate_b = _sigmoid(glb_ref[...] + bias[:, D_MODEL:])
    merged = _bf(gate_a * y_a + gate_b * y_b)
    return gb, u, z, gbz, y_a, y_b, gate_a, gate_b, merged


def _mixer_specs(tm, seq):
    d = D_MODEL
    tok = pl.BlockSpec((tm, d), lambda i: (i, 0))
    col = lambda c: pl.BlockSpec((tm, d), lambda i: (i, c))
    prev = lambda c: pl.BlockSpec((8, d), lambda i: (jnp.maximum(i * (tm // 8) - 1, 0), c))
    o_spec = pl.BlockSpec((tm, N_HEADS * V_DIM), lambda i: (i, 0))
    fwd_specs = [o_spec, col(0), col(1), col(2), col(3), col(4), prev(3), prev(4), _const((1, 2 * d)), _const((3, d)),
                 _resident((N_HEADS * V_DIM, d)), _resident((d, d)), _resident((d, d))]
    return tok, fwd_specs


def _mix_fwd(x1, o, big, gate_bias, conv_w, w_pa, w_pc, w_out, seq):
    t, d = x1.shape
    tm = TOKEN_TILE
    tiles_per_seq = seq // tm

    def body(x_ref, o_ref, gb_ref, gla_ref, glb_ref, xc_ref, gc_ref, xcp_ref, gcp_ref, bias_ref, cw_ref, wpa_ref, wpc_ref,
             wout_ref, x2_ref):
        first = pl.program_id(0) % tiles_per_seq == 0
        merged = _mixer_values(o_ref, gb_ref, gla_ref, glb_ref, xc_ref, gc_ref, xcp_ref, gcp_ref, bias_ref, cw_ref, wpa_ref,
                               wpc_ref, first)[-1]
        x2_ref[...] = x_ref[...] + _mm(merged, wout_ref[...])

    tok, fwd_specs = _mixer_specs(tm, seq)
    return pl.pallas_call(
        body, name="mix_fwd", grid=(t // tm,), in_specs=[tok] + fwd_specs, out_specs=tok,
        out_shape=jax.ShapeDtypeStruct((t, d), F32),
        compiler_params=_params(("arbitrary",)))(x1, o, big, big, big, big, big, big, big, gate_bias, conv_w, w_pa, w_pc, w_out)


def _mix_bwd(dx2, o, big, gate_bias, conv_w, w_pa, w_pc, w_out, seq, carried=None):
    t, d = dx2.shape
    tm = TOKEN_TILE
    tiles_per_seq = seq // tm
    hv = N_HEADS * V_DIM

    def body(dx_ref, o_ref, gb_ref, gla_ref, glb_ref, xc_ref, gc_ref, xcp_ref, gcp_ref, bias_ref, cw_ref, wpa_ref, wpc_ref,
             wout_ref, do_ref, delta_ref, dz_ref, dm_ref, dbias_ref, dwpa_ref, dwpc_ref, dwout_ref):
        first = pl.program_id(0) % tiles_per_seq == 0
        gb, _, z, gbz, y_a, y_b, gate_a, gate_b, merged = _mixer_values(
            o_ref, gb_ref, gla_ref, glb_ref, xc_ref, gc_ref, xcp_ref, gcp_ref, bias_ref, cw_ref, wpa_ref, wpc_ref, first)

        @pl.when(pl.program_id(0) == 0)
        def _():
            dbias_ref[...] = jnp.zeros_like(dbias_ref)
            dwpa_ref[...] = jnp.zeros_like(dwpa_ref)
            dwpc_ref[...] = jnp.zeros_like(dwpc_ref)
            dwout_ref[...] = jnp.zeros_like(dwout_ref)

        dxb = _bf(dx_ref[...])
        dmerged = _mm_nt(dxb, wout_ref[...])
        dwout_ref[...] += _mm_tn(merged, dxb)
        dla = (dmerged * y_a) * (gate_a * (1.0 - gate_a))
        dlb = (dmerged * y_b) * (gate_b * (1.0 - gate_b))
        dbias_ref[:, :d] += jnp.sum(dla, axis=0, keepdims=True)
        dbias_ref[:, d:] += jnp.sum(dlb, axis=0, keepdims=True)
        dya = _bf(dmerged * gate_a)
        dyb = _bf(dmerged * gate_b)
        do_v = _mm_nt(dya, wpa_ref[...])
        do_ref[...] = do_v
        head = lax.broadcasted_iota(jnp.int32, (N_HEADS, hv), 0) * V_DIM
        col = lax.broadcasted_iota(jnp.int32, (N_HEADS, hv), 1)
        in_head = ((col >= head) & (col < head + V_DIM)).astype(F32)
        delta_ref[...] = lax.dot_general(in_head, do_v * o_ref[...].astype(F32), (((1,), (1,)), ((), ())),
                                         precision=lax.Precision.HIGHEST, preferred_element_type=F32)
        dwpa_ref[...] += _mm_tn(o_ref[...], dya)
        dgz = _mm_nt(dyb, wpc_ref[...])
        dwpc_ref[...] += _mm_tn(gbz, dyb)
        dz_ref[...] = dgz * gb
        dm_ref[:, :d] = _bf(dgz * z)
        dm_ref[:, d:2 * d] = _bf(dla)
        dm_ref[:, 2 * d:] = _bf(dlb)

    tok, fwd_specs = _mixer_specs(tm, seq)
    return _pallas(
        body, "mix_bwd", (t // tm,), [tok] + fwd_specs,
        [pl.BlockSpec((tm, hv), lambda i: (i, 0)), pl.BlockSpec((N_HEADS, tm), lambda i: (0, i)), tok,
         pl.BlockSpec((tm, M_COLS), lambda i: (i, 0)), _const((1, 2 * d)), _const((hv, d)), _const((d, d)), _const((d, d))],
        [jax.ShapeDtypeStruct((t, hv), F32), jax.ShapeDtypeStruct((N_HEADS, t), F32), jax.ShapeDtypeStruct((t, d), F32),
         jax.ShapeDtypeStruct((t, M_COLS), BF16), jax.ShapeDtypeStruct((1, 2 * d), F32), jax.ShapeDtypeStruct((hv, d), F32),
         jax.ShapeDtypeStruct((d, d), F32), jax.ShapeDtypeStruct((d, d), F32)],
        (dx2, o, big, big, big, big, big, big, big, gate_bias, conv_w, w_pa, w_pc, w_out), ("arbitrary",), carried)


def _prep_bwd(lat, big, dz, dq, dk, dv, qa_gain, kva_gain, qh_gain, kh_gain, w_uq, w_uk, w_uv, cos, sin, conv_w, seq, carried=None):
    t = lat.shape[0]
    d = D_MODEL
    tm = TOKEN_TILE
    tiles_per_seq = seq // tm
    last_blk = t // 8 - 1

    def body(lat_ref, xc_ref, gc_ref, dz_ref, dzn_ref, dq_ref, dk_ref, dv_ref, qa_ref, kva_ref, qh_ref, kh_ref, wuq_ref, wuk_ref,
             wuv_ref, cos_ref, sin_ref, cw_ref,
             dp_ref, dwuq_ref, dwuk_ref, dwuv_ref, dqa_ref, dkva_ref, dqh_ref, dkh_ref, dcw_ref):
        pid = pl.program_id(0)

        @pl.when(pid == 0)
        def _():
            for r in (dwuq_ref, dwuk_ref, dwuv_ref, dqa_ref, dkva_ref, dqh_ref, dkh_ref, dcw_ref):
                r[...] = jnp.zeros_like(r)

        last = pid % tiles_per_seq == tiles_per_seq - 1
        dzv = dz_ref[...]
        dz_next = jnp.where(last, 0.0, dzn_ref[...])
        dz1 = _shift_up(dzv, dz_next, 1)
        dz2 = _shift_up(dzv, dz_next, 2)
        cw = cw_ref[...]
        xc = xc_ref[...]
        gc = gc_ref[...]
        u = gc * xc
        du = cw[2:3] * dzv + cw[1:2] * dz1 + cw[0:1] * dz2
        dp_ref[:, :d] = _bf(du * gc)
        dp_ref[:, d:2 * d] = _bf(du * xc)
        dcw_ref[0:1, :] += jnp.sum(dz2 * u, axis=0, keepdims=True)
        dcw_ref[1:2, :] += jnp.sum(dz1 * u, axis=0, keepdims=True)
        dcw_ref[2:3, :] += jnp.sum(dzv * u, axis=0, keepdims=True)

        lat_v = lat_ref[...]
        q_lat = lat_v[:, :Q_LORA]
        kv_lat = lat_v[:, Q_LORA:Q_LORA + KV_LORA]
        k_rope = lat_v[:, Q_LORA + KV_LORA:]
        qa_gain_v = qa_ref[...]
        kva_gain_v = kva_ref[...]
        qh_gain_v = qh_ref[...]
        kh_gain_v = kh_ref[...]
        cq, rq = _rms(q_lat, qa_gain_v)
        ckv, rkv = _rms(kv_lat, kva_gain_v)
        cqb = _bf(cq)
        ckvb = _bf(ckv)
        cos_v = cos_ref[...]
        sin_v = sin_ref[...]
        lane = lax.broadcasted_iota(jnp.int32, (tm, HEAD_PAD), 1)
        rope_lanes = (lane >= QK_NOPE) & (lane < QK_DIM)
        dcq = jnp.zeros((tm, Q_LORA), F32)
        dckv = jnp.zeros((tm, KV_LORA), F32)
        dk_rope = jnp.zeros((tm, HEAD_PAD), F32)
        dqh_gain = jnp.zeros((1, HEAD_PAD), F32)
        dkh_gain = jnp.zeros((1, HEAD_PAD), F32)
        for hd in range(N_HEADS):
            q_pre = _mm(cqb, wuq_ref[hd])
            _, rr = _rms(q_pre, qh_gain_v, QK_DIM)
            dq_pre, dg = _rms_bwd(q_pre, rr, qh_gain_v, _rope_bwd(dq_ref[hd], cos_v, sin_v), QK_DIM)
            dqh_gain = dqh_gain + dg
            dq_pre_b = _bf(dq_pre)
            dcq = dcq + _mm_nt(dq_pre_b, wuq_ref[hd])
            dwuq_ref[hd] += _mm_tn(cqb, dq_pre_b)

            k_pre = _mm(ckvb, wuk_ref[hd]) + k_rope
            _, rr = _rms(k_pre, kh_gain_v, QK_DIM)
            dk_pre, dg = _rms_bwd(k_pre, rr, kh_gain_v, _rope_bwd(dk_ref[hd], cos_v, sin_v), QK_DIM)
            dkh_gain = dkh_gain + dg
            dk_rope = dk_rope + jnp.where(rope_lanes, dk_pre, 0.0)
            dk_pre_b = _bf(dk_pre)
            dvb = _bf(dv_ref[hd])
            dckv = dckv + _mm_nt(dk_pre_b, wuk_ref[hd]) + _mm_nt(dvb, wuv_ref[hd])
            dwuk_ref[hd] += _mm_tn(ckvb, dk_pre_b)
            dwuv_ref[hd] += _mm_tn(ckvb, dvb)
        dqh_ref[...] += dqh_gain
        dkh_ref[...] += dkh_gain
        dq_lat, dg = _rms_bwd(q_lat, rq, qa_gain_v, dcq)
        dqa_ref[...] += dg
        dkv_lat, dg = _rms_bwd(kv_lat, rkv, kva_gain_v, dckv)
        dkva_ref[...] += dg
        dp_ref[:, 2 * d:2 * d + Q_LORA] = _bf(dq_lat)
        dp_ref[:, 2 * d + Q_LORA:2 * d + Q_LORA + KV_LORA] = _bf(dkv_lat)
        dp_ref[:, 2 * d + Q_LORA + KV_LORA:] = _bf(dk_rope)

    tok = lambda c: pl.BlockSpec((tm, c), lambda i: (i, 0))
    col = lambda c: pl.BlockSpec((tm, d), lambda i: (i, c))
    head = lambda c: pl.BlockSpec((N_HEADS, tm, c), lambda i: (0, i, 0))
    nxt = pl.BlockSpec((8, d), lambda i: (jnp.minimum((i + 1) * (tm // 8), last_blk), 0))
    return _pallas(
        body, "prep_bwd", (t // tm,),
        [tok(LAT_COLS), col(3), col(4), tok(d), nxt, head(HEAD_PAD), head(HEAD_PAD), head(V_DIM),
         _const((1, Q_LORA)), _const((1, KV_LORA)), _const((1, HEAD_PAD)), _const((1, HEAD_PAD)),
         _resident(w_uq.shape), _resident(w_uk.shape), _resident(w_uv.shape), tok(HEAD_PAD), tok(HEAD_PAD), _const((3, d))],
        [tok(P_COLS), _const(w_uq.shape), _const(w_uk.shape), _const(w_uv.shape), _const((1, Q_LORA)),
         _const((1, KV_LORA)), _const((1, HEAD_PAD)), _const((1, HEAD_PAD)), _const((3, d))],
        [jax.ShapeDtypeStruct((t, P_COLS), BF16), jax.ShapeDtypeStruct(w_uq.shape, F32),
         jax.ShapeDtypeStruct(w_uk.shape, F32), jax.ShapeDtypeStruct(w_uv.shape, F32),
         jax.ShapeDtypeStruct((1, Q_LORA), F32), jax.ShapeDtypeStruct((1, KV_LORA), F32),
         jax.ShapeDtypeStruct((1, HEAD_PAD), F32), jax.ShapeDtypeStruct((1, HEAD_PAD), F32), jax.ShapeDtypeStruct((3, d), F32)],
        (lat, big, big, dz, dz, dq, dk, dv, qa_gain, kva_gain, qh_gain, kh_gain, w_uq, w_uk, w_uv, cos, sin, conv_w),
        ("arbitrary",), carried)


def _inproj_bwd(x1, gain, dx2, dm, dp, w_m, w_p, carried=None):
    t, d = x1.shape
    tm = TOKEN_TILE

    def body(x_ref, g_ref, dx2_ref, dm_ref, dp_ref, wm_ref, wp_ref, dx1_ref, dgain_ref):
        xv = x_ref[...]
        gain_v = g_ref[...]
        _, r = _rms(xv, gain_v)
        dh = _mm(dm_ref[...], wm_ref[...]) + _mm(dp_ref[...], wp_ref[...])
        dxn, dgain = _rms_bwd(xv, r, gain_v, dh)
        dx1_ref[...] = dx2_ref[...] + dxn

        @pl.when(pl.program_id(0) == 0)
        def _():
            dgain_ref[...] = jnp.zeros_like(dgain_ref)

        dgain_ref[...] += dgain

    tok = lambda c: pl.BlockSpec((tm, c), lambda i: (i, 0))
    return _pallas(
        body, "inproj_bwd", (t // tm,),
        [tok(d), _const((1, d)), tok(d), tok(M_COLS), tok(P_COLS), _resident(w_m.shape), _resident(w_p.shape)],
        [tok(d), _const((1, d))], [jax.ShapeDtypeStruct((t, d), F32), jax.ShapeDtypeStruct((1, d), F32)],
        (x1, gain, dx2, dm, dp, w_m, w_p), ("arbitrary",), carried)


def _adamw(quads, name, carried=None):
    k = len(quads)
    rows, cols = quads[0][0].shape
    tr, tc = rows, cols
    for cand in (512, 352, 256, 192, 128, 64):
        if rows % cand == 0 and rows > cand:
            tr = cand
            break
    if tr == rows and rows * cols > 512 * 1024 and cols % 256 == 0:
        tc = 256
    while k * 14 * tr * tc * 4 > VMEM_LIMIT // 2 and tr % 16 == 0:
        tr //= 2

    def body(*refs):
        for i in range(k):
            w_ref, g_ref, m_ref, v_ref = refs[4 * i:4 * i + 4]
            delta_ref, nm_ref, nv_ref = refs[4 * k + 3 * i:4 * k + 3 * i + 3]
            gv = g_ref[...]
            nm = ADAM_B1 * m_ref[...] + (1.0 - ADAM_B1) * gv
            nv = ADAM_B2 * v_ref[...] + (1.0 - ADAM_B2) * (gv * gv)
            m_hat = nm * (1.0 / (1.0 - ADAM_B1 ** ADAM_STEP))
            v_hat = nv * (1.0 / (1.0 - ADAM_B2 ** ADAM_STEP))
            delta_ref[...] = -ADAM_LR * (m_hat / (jnp.sqrt(v_hat) + ADAM_EPS) + ADAM_WD * w_ref[...])
            nm_ref[...] = nm
            nv_ref[...] = nv

    spec = pl.BlockSpec((tr, tc), lambda i, j: (i, j))
    shape = jax.ShapeDtypeStruct((rows, cols), F32)
    outs = _pallas(body, name, (rows // tr, cols // tc), [spec] * (4 * k), [spec] * (3 * k), [shape] * (3 * k),
                   [a for quad in quads for a in quad], ("arbitrary", "arbitrary"), carried)
    return [tuple(outs[3 * i:3 * i + 3]) for i in range(k)]


def _place():
    x, y, c = lax.axis_index("x"), lax.axis_index("y"), lax.axis_index("c")
    other_chips = [(1 - x, y), (x, 1 - y), (1 - x, 1 - y)]
    return x, y, c, other_chips


def _remote(src, dst, sems, send, recv, device):
    return pltpu.make_async_remote_copy(src_ref=src, dst_ref=dst, send_sem=sems.at[send], recv_sem=sems.at[recv],
                                        device_id=device, device_id_type=MESH_ID)


def _cast_shards(shards, out_dtypes):
    n = len(shards)

    def body(*refs):
        ins, outs, stage, sems = refs[:n], refs[n:2 * n], refs[2 * n:3 * n], refs[3 * n]
        x, y, _, _ = _place()
        me = 2 * x + y
        copies = []
        for w in range(n):
            stage[w][...] = ins[w][...].astype(out_dtypes[w])
            copies.append(pltpu.make_async_copy(stage[w], outs[w].at[me], sems.at[w]))
            copies[-1].start()
        for cp in copies:
            cp.wait()

    vm = pl.BlockSpec(memory_space=pltpu.VMEM)
    return pl.pallas_call(
        body, name="cast_shards", in_specs=[vm] * n, out_specs=[ANY] * n,
        out_shape=[jax.ShapeDtypeStruct((N_CHIPS,) + s.shape, dt) for s, dt in zip(shards, out_dtypes)],
        scratch_shapes=[pltpu.VMEM(s.shape, dt) for s, dt in zip(shards, out_dtypes)] + [pltpu.SemaphoreType.DMA((n,))],
        compiler_params=_params())(*shards)


BF16_ROWS = 16


def _split_rows(rows):
    return (rows // 2) % BF16_ROWS == 0


def _half_shape(rows, cols):
    return (rows // 2, cols) if _split_rows(rows) else (rows, cols // 2)


def _half(rows, cols, which):
    if _split_rows(rows):
        return (pl.ds(pl.multiple_of(which * (rows // 2), BF16_ROWS), rows // 2), slice(None))
    return (slice(None), pl.ds(pl.multiple_of(which * (cols // 2), 128), cols // 2))


def _gather_carried(bufs):
    n = len(bufs)

    def half(w, slot, which):
        _, rows, cols = bufs[w].shape
        return (slot,) + _half(rows, cols, which)

    def start(ins, outs, sems, base):
        x, y, c, other_chips = _place()
        me = 2 * x + y
        for w in range(n):
            mine = outs[w].at[half(w, me, c)]
            for p, (px, py) in enumerate(other_chips):
                _remote(mine, mine, sems, base + 12 * w + p, base + 12 * w + 3 + p, (px, py, c)).start()

    def finish(ins, outs, sems, base):
        x, y, c, other_chips = _place()
        me = 2 * x + y
        for w in range(n):
            for p, (px, py) in enumerate(other_chips):
                got = outs[w].at[half(w, 2 * px + py, c)]
                _remote(got, got, sems, base + 12 * w + p, base + 12 * w + 3 + p, (px, py, c)).wait_recv()
                _remote(got, got, sems, base + 12 * w + 6 + p, base + 12 * w + 9 + p, (x, y, 1 - c)).start()
        for w in range(n):
            mine = outs[w].at[half(w, me, c)]
            for p, (px, py) in enumerate(other_chips):
                got = outs[w].at[half(w, 2 * px + py, c)]
                theirs = outs[w].at[half(w, 2 * px + py, 1 - c)]
                _remote(got, theirs, sems, base + 12 * w + 6 + p, base + 12 * w + 9 + p, (x, y, 1 - c)).wait()
                _remote(mine, mine, sems, base + 12 * w + p, base + 12 * w + 3 + p, (px, py, c)).wait_send()

    shapes = [jax.ShapeDtypeStruct(b.shape, b.dtype) for b in bufs]
    return _Carried(bufs, shapes, {w: w for w in range(n)}, 12 * n, start, finish)


def _swap_carried(grads):
    n = len(grads)

    def copy(w, ins, outs, sems, base):
        x, y, c, _ = _place()
        _, rows, cols = grads[w].shape
        theirs = ins[w].at[(slice(None),) + _half(rows, cols, 1 - c)]
        return _remote(theirs, outs[w], sems, base + 2 * w, base + 2 * w + 1, (x, y, 1 - c))

    def start(ins, outs, sems, base):
        for w in range(n):
            copy(w, ins, outs, sems, base).start()

    def finish(ins, outs, sems, base):
        for w in range(n):
            copy(w, ins, outs, sems, base).wait()

    shapes = [jax.ShapeDtypeStruct((g.shape[0],) + _half_shape(*g.shape[1:]), F32) for g in grads]
    return _Carried(grads, shapes, {}, 2 * n, start, finish)


def _row_tile(rows):
    for cand in (512, 352, 256, 192, 128, 96, 64, 32, 16):
        if rows % cand == 0:
            return cand
    return rows


def _half_block_index(split_rows, tiles, i, core):
    return (core * tiles + i, 0) if split_rows else (i, core)


def _chip_partial(grad, other, place, name):
    nblk, hr, hc = other.shape
    by_rows = _split_rows(grad.shape[1])
    tr = _row_tile(hr)
    tiles = hr // tr

    def body(place_ref, g_ref, o_ref, sum_ref, sum_bf_ref):
        s = g_ref[...] + o_ref[...]
        sum_ref[...] = s
        sum_bf_ref[...] = _bf(s)

    grid_spec = pltpu.PrefetchScalarGridSpec(
        num_scalar_prefetch=1, grid=(nblk, tiles),
        in_specs=[pl.BlockSpec((None, tr, hc), lambda b, i, place_ref: (b,) + _half_block_index(by_rows, tiles, i, place_ref[1])),
                  pl.BlockSpec((None, tr, hc), lambda b, i, place_ref: (b, i, 0))],
        out_specs=[pl.BlockSpec((None, tr, hc), lambda b, i, place_ref: (b, i, 0))] * 2)
    return pl.pallas_call(body, name=name, grid_spec=grid_spec,
                          out_shape=[jax.ShapeDtypeStruct(other.shape, F32), jax.ShapeDtypeStruct(other.shape, BF16)],
                          compiler_params=_params(("arbitrary", "arbitrary")))(place, grad, other)


def _send_carried(partials):
    n = len(partials)

    def start(ins, outs, sems, base):
        x, y, c, other_chips = _place()
        me = 2 * x + y
        for w in range(n):
            for p, (px, py) in enumerate(other_chips):
                _remote(ins[w].at[2 * px + py], outs[w].at[me], sems, base + 6 * w + p, base + 6 * w + 3 + p, (px, py, c)).start()

    def finish(ins, outs, sems, base):
        x, y, c, other_chips = _place()
        for w in range(n):
            for p, (px, py) in enumerate(other_chips):
                _remote(ins[w].at[2 * px + py], outs[w].at[2 * px + py], sems, base + 6 * w + p, base + 6 * w + 3 + p,
                        (px, py, c)).wait()

    return _Carried(partials, [jax.ShapeDtypeStruct(p.shape, BF16) for p in partials], {}, 6 * n, start, finish)


def _chip_total(own, received, place, shape, name):
    nblk, hr, hc = own.shape
    by_rows = _split_rows(shape[0])
    tr = _row_tile(hr)
    tiles = hr // tr

    def body(place_ref, own_ref, r1_ref, r2_ref, r3_ref, out_ref):
        out_ref[...] = own_ref[...] + ((r1_ref[...].astype(F32) + r2_ref[...].astype(F32)) + r3_ref[...].astype(F32))

    def slot(k):
        return pl.BlockSpec((None, tr, hc), lambda i, place_ref: ((place_ref[0] + k) % N_CHIPS, i, 0))

    grid_spec = pltpu.PrefetchScalarGridSpec(
        num_scalar_prefetch=1, grid=(tiles,), in_specs=[slot(0), slot(1), slot(2), slot(3)],
        out_specs=pl.BlockSpec((tr, hc), lambda i, place_ref: _half_block_index(by_rows, tiles, i, place_ref[1])))
    return pl.pallas_call(body, name=name, grid_spec=grid_spec, out_shape=jax.ShapeDtypeStruct(tuple(shape), F32),
                          compiler_params=_params(("arbitrary",)))(place, own, received, received, received)


def _join_carried(totals):
    n = len(totals)

    def copy(w, outs, sems, base):
        x, y, c, _ = _place()
        mine = outs[w].at[_half(*totals[w].shape, c)]
        return _remote(mine, mine, sems, base + 2 * w, base + 2 * w + 1, (x, y, 1 - c))

    def start(ins, outs, sems, base):
        for w in range(n):
            copy(w, outs, sems, base).start()

    def finish(ins, outs, sems, base):
        for w in range(n):
            copy(w, outs, sems, base).wait()

    shapes = [jax.ShapeDtypeStruct(a.shape, F32) for a in totals]
    return _Carried(totals, shapes, {w: w for w in range(n)}, 2 * n, start, finish)


def _sum_devices(vec):
    rows, n = vec.shape

    def body(v_ref, out_ref, buf, send_sems, recv_sems):
        x, y, c, _ = _place()
        me = 4 * x + 2 * y + c
        buf[me] = v_ref[...]
        sends = []
        for k in range(1, N_DEV):
            peer = (1 - x if k & 4 else x, 1 - y if k & 2 else y, 1 - c if k & 1 else c)
            cp = pltpu.make_async_remote_copy(src_ref=v_ref, dst_ref=buf.at[me], send_sem=send_sems.at[k], recv_sem=recv_sems.at[k],
                                              device_id=peer, device_id_type=MESH_ID)
            cp.start()
            sends.append(cp)
        for cp in sends:
            cp.wait()
        total = buf[0]
        for dev in range(1, N_DEV):
            total = total + buf[dev]
        out_ref[...] = total

    vm = pl.BlockSpec(memory_space=pltpu.VMEM)
    return pl.pallas_call(
        body, name="sum_devices", in_specs=[vm], out_specs=vm, out_shape=jax.ShapeDtypeStruct((rows, n), F32),
        scratch_shapes=[pltpu.VMEM((N_DEV, rows, n), F32), pltpu.SemaphoreType.DMA((N_DEV,)), pltpu.SemaphoreType.DMA((N_DEV,))],
    )(vec)


def _rope_tables(positions):
    half = QK_ROPE // 2
    inv_freq = 1.0 / (ROPE_THETA ** (jnp.arange(half, dtype=F32) / half))
    ang = positions.astype(F32).reshape(-1, 1) * inv_freq
    cos, sin = jnp.cos(ang), jnp.sin(ang)
    t = ang.shape[0]
    ones, zeros = jnp.ones((t, QK_NOPE), F32), jnp.zeros((t, QK_NOPE), F32)
    pad = HEAD_PAD - QK_DIM
    cos_full = jnp.concatenate([ones, cos, cos, ones[:, :pad]], axis=1)
    sin_signed = jnp.concatenate([zeros, -sin, sin, zeros[:, :pad]], axis=1)
    return cos_full, sin_signed


def _partials(names, grads, from_sibling, place):
    return [_chip_partial(g, o, place, "chip_partial_" + n) for n, g, o in zip(names, grads, from_sibling)]


def _totals(names, grads, partials, received, place):
    return [_chip_total(pf, r, place, g.shape[1:], "chip_total_" + n) for n, g, (pf, _), r in zip(names, grads, partials, received)]


def _kernel_layouts(full):
    d = D_MODEL
    w_in = full["w_in"]
    o_kr = Q_LORA + KV_LORA
    o_xc = o_kr + QK_ROPE
    o_gb = o_xc + d
    o_gc = o_gb + d
    o_gl = o_gc + d
    k_rope_pad = jnp.pad(w_in[o_kr:o_xc], ((QK_NOPE, HEAD_PAD - QK_DIM), (0, 0)))
    w_m = jnp.concatenate([w_in[o_gb:o_gc], w_in[o_gl:]], axis=0)
    w_p = jnp.concatenate([w_in[o_xc:o_gb], w_in[o_gc:o_gl], w_in[:o_kr], k_rope_pad], axis=0)
    w_uq = jnp.pad(full["w_uq"].reshape(Q_LORA, N_HEADS, QK_DIM), ((0, 0), (0, 0), (0, HEAD_PAD - QK_DIM))).transpose(1, 0, 2)
    w_uk = jnp.pad(full["w_uk"].reshape(KV_LORA, N_HEADS, QK_NOPE), ((0, 0), (0, 0), (0, HEAD_PAD - QK_NOPE))).transpose(1, 0, 2)
    w_uv = full["w_uv"].reshape(KV_LORA, N_HEADS, V_DIM).transpose(1, 0, 2)
    return {"w_m": w_m, "w_p": w_p, "w_uq": w_uq, "w_uk": w_uk, "w_uv": w_uv, "w_uvt": w_uv.transpose(0, 2, 1)}


def _global_layouts(g):
    d = D_MODEL
    dm, dp = g["w_m"], g["w_p"]
    o_lat = 2 * d
    o_kr = o_lat + Q_LORA + KV_LORA + QK_NOPE
    w_in = jnp.concatenate([dp[o_lat:o_lat + Q_LORA + KV_LORA], dp[o_kr:o_kr + QK_ROPE], dp[:d], dm[:d], dp[d:o_lat], dm[d:]], axis=0)
    w_uq = g["w_uq"][:, :, :QK_DIM].transpose(1, 0, 2).reshape(Q_LORA, N_HEADS * QK_DIM)
    w_uk = g["w_uk"][:, :, :QK_NOPE].transpose(1, 0, 2).reshape(KV_LORA, N_HEADS * QK_NOPE)
    w_uv = g["w_uv"].transpose(1, 0, 2).reshape(KV_LORA, N_HEADS * V_DIM)
    return {"w_in": w_in, "w_uq": w_uq, "w_uk": w_uk, "w_uv": w_uv, "w_proj_attn": g["w_pa"], "w_proj_conv": g["w_pc"],
            "w_out": g["w_out"]}


def _col_blocks(a):
    r, c = a.shape
    return a.reshape(r, N_CHIPS, c // N_CHIPS).transpose(1, 0, 2)


def _from_col_blocks(a):
    n, r, c = a.shape
    return a.transpose(1, 0, 2).reshape(r, n * c)


COL_SHARDED = ("w_uq", "w_uk", "w_uv", "w_proj_attn")
TRANSPOSED = ("ffn1_w_gate", "ffn1_w_up", "ffn2_w_gate", "ffn2_w_up", "w_in")
SMALL = (("ffn1_norm", 1024), ("mix_norm", 1024), ("gate_bias", 2048), ("q_a_norm", 384), ("kv_a_norm", 256),
         ("q_head_norm", 128), ("k_head_norm", 128), ("ffn2_norm", 1024))
WEIGHT_ORDER = ("ffn1_norm", "ffn1_w_gate", "ffn1_w_up", "ffn1_w_down", "mix_norm", "w_in", "gate_bias", "q_a_norm", "w_uq",
                "kv_a_norm", "w_uk", "w_uv", "q_head_norm", "k_head_norm", "w_proj_attn", "conv_w", "w_proj_conv", "w_out",
                "ffn2_norm", "ffn2_w_gate", "ffn2_w_up", "ffn2_w_down")
MATRICES = ("ffn1_w_gate", "ffn1_w_up", "ffn1_w_down", "w_in", "w_uq", "w_uk", "w_uv", "w_proj_attn", "w_proj_conv", "w_out",
            "ffn2_w_gate", "ffn2_w_up", "ffn2_w_down")
GROUP_FFN1 = ("ffn1_w_gate", "ffn1_w_up", "ffn1_w_down")
GROUP_IN = ("w_in", "w_uq", "w_uk", "w_uv", "conv_w")
GROUP_MIX = ("w_proj_attn", "w_proj_conv", "w_out")
GROUP_FFN2 = ("ffn2_w_gate", "ffn2_w_up", "ffn2_w_down")
GROUP_MID = ("w_in", "w_uq", "w_uk", "w_uv", "w_proj_attn", "w_proj_conv", "w_out")


def _pad_lanes(a, n):
    return jnp.pad(a.reshape(1, -1), ((0, 0), (0, n - a.size)))


def kernel(x, positions, ffn1_norm, ffn1_w_gate, ffn1_w_up, ffn1_w_down, mix_norm, w_in, gate_bias, q_a_norm, w_uq, kv_a_norm, w_uk, w_uv, q_head_norm, k_head_norm, w_proj_attn, conv_w, w_proj_conv, w_out, ffn2_norm, ffn2_w_gate, ffn2_w_up, ffn2_w_down, loss_target, m_ffn1_norm, m_ffn1_w_gate, m_ffn1_w_up, m_ffn1_w_down, m_mix_norm, m_w_in, m_gate_bias, m_q_a_norm, m_w_uq, m_kv_a_norm, m_w_uk, m_w_uv, m_q_head_norm, m_k_head_norm, m_w_proj_attn, m_conv_w, m_w_proj_conv, m_w_out, m_ffn2_norm, m_ffn2_w_gate, m_ffn2_w_up, m_ffn2_w_down, v_ffn1_norm, v_ffn1_w_gate, v_ffn1_w_up, v_ffn1_w_down, v_mix_norm, v_w_in, v_gate_bias, v_q_a_norm, v_w_uq, v_kv_a_norm, v_w_uk, v_w_uv, v_q_head_norm, v_k_head_norm, v_w_proj_attn, v_conv_w, v_w_proj_conv, v_w_out, v_ffn2_norm, v_ffn2_w_gate, v_ffn2_w_up, v_ffn2_w_down):
    args = dict(locals())
    view = lambda n, a: a.T if n in TRANSPOSED else a
    weights = {n: view(n, args[n]) for n in WEIGHT_ORDER}
    moments_m = {n: view(n, args["m_" + n]) for n in WEIGHT_ORDER}
    moments_v = {n: view(n, args["v_" + n]) for n in WEIGHT_ORDER}
    nb, seq, d = x.shape
    t = nb * seq
    chip = (2 * lax.axis_index("x") + lax.axis_index("y")).astype(jnp.int32)
    place = jnp.stack([chip, lax.axis_index("c").astype(jnp.int32)])
    grads, delta, new_m, new_v = {}, {}, {}, {}

    def adamw(names, carried=None):
        results = _adamw([(weights[n], grads[n], moments_m[n], moments_v[n]) for n in names], "adamw_" + names[0], carried)
        for n, (dn, mn, vn) in zip(names, results):
            delta[n], new_m[n], new_v[n] = dn, mn, vn

    conv_rows = conv_w.shape[0]
    conv_shard = jnp.pad(conv_w, ((0, 16 - conv_rows), (0, 0)))
    bufs = dict(zip(MATRICES + ("conv_w",), _cast_shards([weights[n] for n in MATRICES] + [conv_shard],
                                                         [BF16] * len(MATRICES) + [F32])))
    blocks = dict(zip(GROUP_FFN1, _run(_gather_carried([bufs[n] for n in GROUP_FFN1]), "gather_ffn1")))
    p = {n: _pad_lanes(weights[n], size) for n, size in SMALL}
    cos, sin = _rope_tables(positions)
    x_tok = x.reshape(t, d)

    gather_in = _gather_carried([bufs[n] for n in GROUP_IN])
    x1, gate1, up1, act1 = _ffn_fwd(x_tok, p["ffn1_norm"], blocks["ffn1_w_gate"], blocks["ffn1_w_up"], blocks["ffn1_w_down"], None,
                                    "ffn1_fwd", gather_in)
    blocks.update(zip(GROUP_IN, gather_in.results))
    w = _kernel_layouts({"w_in": blocks["w_in"].reshape(-1, d), **{n: _from_col_blocks(blocks[n]) for n in ("w_uq", "w_uk", "w_uv")}})
    p["conv_w"] = _from_col_blocks(blocks["conv_w"])[:conv_rows]

    gather_rest = _gather_carried([bufs[n] for n in GROUP_MIX + GROUP_FFN2])
    h2b, big, lat, q, k, v, vt = _inproj_fwd(x1, p["mix_norm"], w["w_m"], w["w_p"], p["q_a_norm"], p["kv_a_norm"], p["q_head_norm"],
                                             p["k_head_norm"], w["w_uq"], w["w_uk"], w["w_uv"], w["w_uvt"], cos, sin, gather_rest)
    blocks.update(zip(GROUP_MIX + GROUP_FFN2, gather_rest.results))
    w_pa = _from_col_blocks(blocks["w_proj_attn"])
    w_pc, w_out_full = blocks["w_proj_conv"].reshape(-1, d), blocks["w_out"].reshape(-1, d)
    wg2, wu2, wd2 = blocks["ffn2_w_gate"], blocks["ffn2_w_up"], blocks["ffn2_w_down"]

    o, lse = _attn_fwd(q, k, vt, seq)
    x2 = _mix_fwd(x1, o, big, p["gate_bias"], p["conv_w"], w_pa, w_pc, w_out_full, seq)
    dx3, gate2, up2, act2, loss = _ffn_fwd(x2, p["ffn2_norm"], wg2, wu2, wd2, loss_target.reshape(t, d), "ffn2_fwd")

    dx2, dg_ffn2, hb2, dgate2, dup2, dyb2 = _ffn_bwd_x(x2, p["ffn2_norm"], dx3, gate2, up2, wg2, wu2, wd2, "ffn2_bwd")
    g_ffn2 = [_tn_matmul(dgate2, hb2, "ffn2_dw_gate"), _tn_matmul(dup2, hb2, "ffn2_dw_up"), _tn_matmul(act2, dyb2, "ffn2_dw_down")]
    swap = _swap_carried(g_ffn2)
    do, delta_o, dz, dm, dbias, dw_pa, dw_pc, dw_out = _mix_bwd(dx2, o, big, p["gate_bias"], p["conv_w"], w_pa, w_pc, w_out_full, seq,
                                                                swap)
    part = _partials(GROUP_FFN2, g_ffn2, swap.results, place)
    send = _send_carried([pb for _, pb in part])
    dq, dk, dv = _attn_bwd(q, k, v, do, lse, delta_o.reshape(N_HEADS // ATTN_BWD_HEADS, ATTN_BWD_HEADS, -1), seq, send)
    join = _join_carried(_totals(GROUP_FFN2, g_ffn2, part, send.results, place))
    dp, dw_uq, dw_uk, dw_uv, dqa, dkva, dqh, dkh, dcw = _prep_bwd(
        lat, big, dz, dq, dk, dv, p["q_a_norm"], p["kv_a_norm"], p["q_head_norm"], p["k_head_norm"], w["w_uq"], w["w_uk"],
        w["w_uv"], cos, sin, p["conv_w"], seq, join)
    grads.update(zip(GROUP_FFN2, join.results))

    gg = _global_layouts({"w_m": _tn_matmul(dm, h2b, "dw_in_m", split_k=2), "w_p": _tn_matmul(dp, h2b, "dw_in_p", split_k=2),
                          "w_uq": dw_uq, "w_uk": dw_uk, "w_uv": dw_uv, "w_pa": dw_pa, "w_pc": dw_pc, "w_out": dw_out})
    g_mid = [_col_blocks(gg[n]) if n in COL_SHARDED else gg[n].reshape(N_CHIPS, -1, gg[n].shape[-1]) for n in GROUP_MID]
    swap = _swap_carried(g_mid)
    dx1, dg_mix = _inproj_bwd(x1, p["mix_norm"], dx2, dm, dp, w["w_m"], w["w_p"], swap)
    part = _partials(GROUP_MID, g_mid, swap.results, place)
    send = _send_carried([pb for _, pb in part])
    grad_x, dg_ffn1, hb1, dgate1, dup1, dyb1 = _ffn_bwd_x(x_tok, p["ffn1_norm"], dx1, gate1, up1, blocks["ffn1_w_gate"],
                                                         blocks["ffn1_w_up"], blocks["ffn1_w_down"], "ffn1_bwd", send)

    small_grads = {"ffn1_norm": dg_ffn1, "mix_norm": dg_mix, "gate_bias": dbias, "q_a_norm": dqa, "kv_a_norm": dkva,
                   "q_head_norm": dqh, "k_head_norm": dkh, "ffn2_norm": dg_ffn2}
    packed = jnp.concatenate([small_grads[n] for n, _ in SMALL] + [dcw.reshape(1, -1), loss], axis=1)
    total = _sum_devices(packed.reshape(8, -1)).reshape(1, -1)
    n_small = sum(size for _, size in SMALL)
    conv_cols = conv_w.shape[1]
    conv_total = total[:, n_small:n_small + conv_rows * d].reshape(conv_rows, d)
    grads["conv_w"] = lax.dynamic_slice_in_dim(conv_total, chip * conv_cols, conv_cols, axis=1)
    loss_total = total[0, n_small + conv_rows * d]

    join = _join_carried(_totals(GROUP_MID, g_mid, part, send.results, place))
    g_gate = _tn_matmul(dgate1, hb1, "ffn1_dw_gate", carried=join)
    grads.update(zip(GROUP_MID, join.results))
    swap_gate = _swap_carried([g_gate])
    g_up = _tn_matmul(dup1, hb1, "ffn1_dw_up", carried=swap_gate)
    part_gate = _partials(GROUP_FFN1[:1], [g_gate], swap_gate.results, place)
    send_gate, swap_up = _send_carried([part_gate[0][1]]), _swap_carried([g_up])
    g_down = _tn_matmul(act1, dyb1, "ffn1_dw_down", carried=_both(send_gate, swap_up))
    join_gate = _join_carried(_totals(GROUP_FFN1[:1], [g_gate], part_gate, send_gate.results, place))
    part_up = _partials(GROUP_FFN1[1:2], [g_up], swap_up.results, place)
    send_up, swap_down = _send_carried([part_up[0][1]]), _swap_carried([g_down])
    adamw(GROUP_FFN2, _both(_both(send_up, swap_down), join_gate))
    grads["ffn1_w_gate"] = join_gate.results[0]
    join_up = _join_carried(_totals(GROUP_FFN1[1:2], [g_up], part_up, send_up.results, place))
    part_down = _partials(GROUP_FFN1[2:], [g_down], swap_down.results, place)
    send_down = _send_carried([part_down[0][1]])
    adamw(("w_in",), _both(send_down, join_up))
    grads["ffn1_w_up"] = join_up.results[0]
    join_down = _join_carried(_totals(GROUP_FFN1[2:], [g_down], part_down, send_down.results, place))
    adamw(GROUP_FFN1[:2], join_down)
    grads["ffn1_w_down"] = join_down.results[0]
    adamw(GROUP_FFN1[2:])
    for n in GROUP_MID[1:] + ("conv_w",):
        adamw((n,))

    pack = lambda src: jnp.concatenate([_pad_lanes(src[n], size) for n, size in SMALL], axis=1)
    (sd, sm, sv), = _adamw([(pack(weights), total[:, :n_small], pack(moments_m), pack(moments_v))], "adamw_small")
    off = 0
    for n, size in SMALL:
        real = weights[n].size
        grads[n] = total[0, off:off + real]
        delta[n], new_m[n], new_v[n] = sd[0, off:off + real], sm[0, off:off + real], sv[0, off:off + real]
        off += size

    return (loss_total, grad_x.reshape(nb, seq, d), *[view(n, src[n]) for src in (grads, delta, new_m, new_v) for n in WEIGHT_ORDER])
```

```python
import functools

import jax
import jax.numpy as jnp
from jax import lax
from jax.experimental import pallas as pl
from jax.experimental.pallas import tpu as pltpu

F32 = jnp.float32
BF16 = jnp.bfloat16

D_MODEL = 1024
N_HEADS = 8
QK_NOPE = 64
QK_ROPE = 32
QK_DIM = QK_NOPE + QK_ROPE
V_DIM = 64
HEAD_PAD = 128
Q_LORA = 384
KV_LORA = 256
ROPE_THETA = 10000.0
NORM_EPS = 1e-6
ATTN_SCALE = QK_DIM ** -0.5
MASK_VALUE = -1e30
N_CHIPS = 4
N_DEV = 8

ADAM_LR = 0.001
ADAM_B1 = 0.9
ADAM_B2 = 0.999
ADAM_EPS = 1e-08
ADAM_WD = 0.01
ADAM_STEP = 10

TOKEN_TILE = 256
ATTN_TILE = 512
TN_TILE = 512
VMEM_LIMIT = 56 * 1024 * 1024

M_COLS = 3 * D_MODEL
P_COLS = 2 * D_MODEL + Q_LORA + KV_LORA + HEAD_PAD
BIG_COLS = 5 * D_MODEL
LAT_COLS = Q_LORA + KV_LORA + HEAD_PAD

MESH_ID = pl.DeviceIdType.MESH
ANY = pl.BlockSpec(memory_space=pl.ANY)


def _params(semantics=None):
    return pltpu.CompilerParams(dimension_semantics=semantics, vmem_limit_bytes=VMEM_LIMIT)


class _Carried:
    def __init__(self, operands, out_shapes, aliases, n_sems, start, finish):
        self.operands, self.out_shapes, self.aliases, self.n_sems = list(operands), list(out_shapes), dict(aliases), n_sems
        self.start, self.finish = start, finish
        self.results = None


def _both(a, b):
    na, nao = len(a.operands), len(a.out_shapes)

    def start(ins, outs, sems, base):
        a.start(ins[:na], outs[:nao], sems, base)
        b.start(ins[na:], outs[nao:], sems, base + a.n_sems)

    def finish(ins, outs, sems, base):
        a.finish(ins[:na], outs[:nao], sems, base)
        b.finish(ins[na:], outs[nao:], sems, base + a.n_sems)

    aliases = dict(a.aliases)
    aliases.update({na + i: nao + o for i, o in b.aliases.items()})
    both = _Carried(a.operands + b.operands, a.out_shapes + b.out_shapes, aliases, a.n_sems + b.n_sems, start, finish)
    both.parts = (a, b)
    return both


def _set_results(carried, results):
    carried.results = list(results)
    if hasattr(carried, "parts"):
        a, b = carried.parts
        _set_results(a, results[:len(a.out_shapes)])
        _set_results(b, results[len(a.out_shapes):])


def _pallas(body, name, grid, in_specs, out_specs, out_shape, args, semantics, carried=None):
    if carried is None:
        return pl.pallas_call(body, name=name, grid=grid, in_specs=in_specs, out_specs=out_specs, out_shape=out_shape,
                              compiler_params=_params(semantics))(*args)
    n_in, n_out, n_ci, n_co = len(in_specs), len(out_specs), len(carried.operands), len(carried.out_shapes)

    def wrapped(*refs):
        ins, c_ins = refs[:n_in], refs[n_in:n_in + n_ci]
        outs, c_outs = refs[n_in + n_ci:n_in + n_ci + n_out], refs[n_in + n_ci + n_out:n_in + n_ci + n_out + n_co]
        sems = refs[-1]
        first = pl.program_id(0) == 0
        last = pl.program_id(0) == grid[0] - 1
        for axis in range(1, len(grid)):
            first = jnp.logical_and(first, pl.program_id(axis) == 0)
            last = jnp.logical_and(last, pl.program_id(axis) == grid[axis] - 1)

        @pl.when(first)
        def _():
            carried.start(c_ins, c_outs, sems, 0)

        body(*ins, *outs)

        @pl.when(last)
        def _():
            carried.finish(c_ins, c_outs, sems, 0)

    results = pl.pallas_call(
        wrapped, name=name, grid=grid, in_specs=list(in_specs) + [ANY] * n_ci, out_specs=list(out_specs) + [ANY] * n_co,
        out_shape=list(out_shape) + carried.out_shapes,
        input_output_aliases={n_in + i: n_out + o for i, o in carried.aliases.items()},
        scratch_shapes=[pltpu.SemaphoreType.DMA((carried.n_sems,))], compiler_params=_params(semantics))(*args, *carried.operands)
    _set_results(carried, results[n_out:])
    return results[:n_out]


def _run(carried, name):
    n_ci, n_co = len(carried.operands), len(carried.out_shapes)

    def body(*refs):
        carried.start(refs[:n_ci], refs[n_ci:n_ci + n_co], refs[-1], 0)
        carried.finish(refs[:n_ci], refs[n_ci:n_ci + n_co], refs[-1], 0)

    results = pl.pallas_call(body, name=name, in_specs=[ANY] * n_ci, out_specs=[ANY] * n_co, out_shape=carried.out_shapes,
                             input_output_aliases=carried.aliases,
                             scratch_shapes=[pltpu.SemaphoreType.DMA((carried.n_sems,))])(*carried.operands)
    _set_results(carried, results)
    return carried.results


def _resident(shape):
    nd = len(shape)
    return pl.BlockSpec(shape, lambda *_: (0,) * nd, pipeline_mode=pl.Buffered(1))


def _const(shape):
    nd = len(shape)
    return pl.BlockSpec(shape, lambda *_: (0,) * nd)


def _mm(a, b):
    return jnp.dot(a, b, preferred_element_type=F32)


def _mm_nt(a, b):
    return lax.dot_general(a, b, (((1,), (1,)), ((), ())), preferred_element_type=F32)


def _mm_tn(a, b):
    return lax.dot_general(a, b, (((0,), (0,)), ((), ())), preferred_element_type=F32)


def _bf(a):
    return a.astype(BF16)


def _sigmoid(a):
    return 1.0 / (1.0 + jnp.exp(-a))


def _rms(x, gain, n=None):
    n = x.shape[-1] if n is None else n
    r = lax.rsqrt(jnp.sum(x * x, axis=-1, keepdims=True) * (1.0 / n) + NORM_EPS)
    return (x * r) * gain, r


def _rms_bwd(x, r, gain, dh, n=None):
    n = x.shape[-1] if n is None else n
    u = dh * gain
    dx = r * u - x * ((r * r * r) * (jnp.sum(u * x, axis=-1, keepdims=True) * (1.0 / n)))
    dgain = jnp.sum(dh * (x * r), axis=0, keepdims=True)
    return dx, dgain


def _rope_swap(t):
    lane = lax.broadcasted_iota(jnp.int32, t.shape, 1)
    lo = (lane >= QK_NOPE) & (lane < QK_NOPE + QK_ROPE // 2)
    hi = (lane >= QK_NOPE + QK_ROPE // 2) & (lane < QK_DIM)
    up = pltpu.roll(t, HEAD_PAD - QK_ROPE // 2, 1)
    down = pltpu.roll(t, QK_ROPE // 2, 1)
    return jnp.where(lo, up, jnp.where(hi, down, 0.0))


def _rope(t, cos, sin):
    return t * cos + _rope_swap(t) * sin


def _rope_bwd(dt, cos, sin):
    return dt * cos + _rope_swap(dt * sin)


def _shift_down(u, prev8, k):
    s = pltpu.roll(u, k, 0)
    p = pltpu.roll(prev8, k, 0)
    row = lax.broadcasted_iota(jnp.int32, prev8.shape, 0)
    top = jnp.where(row < k, p, s[:8])
    return jnp.concatenate([top, s[8:]], axis=0)


def _shift_up(d, next8, k):
    tm = d.shape[0]
    s = pltpu.roll(d, tm - k, 0)
    n = pltpu.roll(next8, 8 - k, 0)
    row = lax.broadcasted_iota(jnp.int32, next8.shape, 0)
    bot = jnp.where(row >= 8 - k, n, s[tm - 8:])
    return jnp.concatenate([s[:tm - 8], bot], axis=0)


def _ffn_fwd(x, gain, wg, wu, wd, target, name, carried=None):
    t, d = x.shape
    nb, f, _ = wg.shape
    tm = TOKEN_TILE
    with_loss = target is not None

    def body(*refs):
        if with_loss:
            x_ref, g_ref, wg_ref, wu_ref, wd_ref, t_ref, out_ref, gate_ref, up_ref, act_ref, loss_ref = refs
        else:
            x_ref, g_ref, wg_ref, wu_ref, wd_ref, out_ref, gate_ref, up_ref, act_ref = refs
        xv = x_ref[...]
        h, _ = _rms(xv, g_ref[...])
        hb = _bf(h)
        y = jnp.zeros((tm, d), F32)
        for j in range(nb):
            gate = _mm_nt(hb, wg_ref[j])
            up = _mm_nt(hb, wu_ref[j])
            act = _bf((gate * _sigmoid(gate)) * up)
            y = y + _mm(act, wd_ref[j])
            gate_ref[j] = _bf(gate)
            up_ref[j] = _bf(up)
            act_ref[j] = act
        out = xv + 0.5 * y
        if with_loss:
            err = out - t_ref[...]
            out_ref[...] = err * (1.0 / d)

            @pl.when(pl.program_id(0) == 0)
            def _():
                loss_ref[...] = jnp.zeros_like(loss_ref)

            part = jnp.sum(jnp.sum(err * err, axis=1, keepdims=True), axis=0, keepdims=True)
            loss_ref[...] += jnp.broadcast_to(part * (0.5 / d), loss_ref.shape)
        else:
            out_ref[...] = out

    tok = pl.BlockSpec((tm, d), lambda i: (i, 0))
    blk = pl.BlockSpec((nb, tm, f), lambda i: (0, i, 0))
    in_specs = [tok, _const((1, d)), _resident(wg.shape), _resident(wu.shape), _resident(wd.shape)]
    args = [x, gain, wg, wu, wd]
    out_shape = [jax.ShapeDtypeStruct((t, d), F32)] + [jax.ShapeDtypeStruct((nb, t, f), BF16)] * 3
    out_specs = [tok, blk, blk, blk]
    if with_loss:
        in_specs.append(tok)
        args.append(target)
        out_shape.append(jax.ShapeDtypeStruct((1, 128), F32))
        out_specs.append(_const((1, 128)))
    return _pallas(body, name, (t // tm,), in_specs, out_specs, out_shape, args, ("arbitrary",), carried)


def _ffn_bwd_x(x, gain, dout, gate, up, wg, wu, wd, name, carried=None):
    t, d = x.shape
    nb, f, _ = wg.shape
    tm = TOKEN_TILE

    def body(x_ref, g_ref, dout_ref, gate_ref, up_ref, wg_ref, wu_ref, wd_ref,
             dx_ref, dgain_ref, hb_ref, dgate_ref, dup_ref, dyb_ref):
        xv = x_ref[...]
        gain_v = g_ref[...]
        h, r = _rms(xv, gain_v)
        hb_ref[...] = _bf(h)
        dout_v = dout_ref[...]
        dyb = _bf(0.5 * dout_v)
        dyb_ref[...] = dyb
        dh = jnp.zeros((tm, d), F32)
        for j in range(nb):
            gt = gate_ref[j].astype(F32)
            uv = up_ref[j].astype(F32)
            s = _sigmoid(gt)
            dact = _mm_nt(dyb, wd_ref[j])
            dup = _bf(dact * (gt * s))
            dgate = _bf((dact * uv) * (s * (1.0 + gt * (1.0 - s))))
            dh = dh + _mm(dgate, wg_ref[j]) + _mm(dup, wu_ref[j])
            dgate_ref[j] = dgate
            dup_ref[j] = dup
        dxn, dgain = _rms_bwd(xv, r, gain_v, dh)
        dx_ref[...] = dout_v + dxn

        @pl.when(pl.program_id(0) == 0)
        def _():
            dgain_ref[...] = jnp.zeros_like(dgain_ref)

        dgain_ref[...] += dgain

    tok = pl.BlockSpec((tm, d), lambda i: (i, 0))
    blk = pl.BlockSpec((nb, tm, f), lambda i: (0, i, 0))
    return _pallas(
        body, name, (t // tm,),
        [tok, _const((1, d)), tok, blk, blk, _resident(wg.shape), _resident(wu.shape), _resident(wd.shape)],
        [tok, _const((1, d)), tok, blk, blk, tok],
        [jax.ShapeDtypeStruct((t, d), F32), jax.ShapeDtypeStruct((1, d), F32), jax.ShapeDtypeStruct((t, d), BF16),
         jax.ShapeDtypeStruct((nb, t, f), BF16), jax.ShapeDtypeStruct((nb, t, f), BF16), jax.ShapeDtypeStruct((t, d), BF16)],
        (x, gain, dout, gate, up, wg, wu, wd), ("arbitrary",), carried)


def _tn_matmul(a, b, name, split_k=1, carried=None):
    t = a.shape[-2]
    k = a.shape[-1]
    n = b.shape[-1]
    tt = min(TN_TILE, t)
    nt = t // tt

    def body(a_ref, b_ref, o_ref):
        @pl.when(pl.program_id(1) == 0)
        def _():
            o_ref[...] = jnp.zeros_like(o_ref)

        o_ref[...] += _mm_tn(a_ref[...], b_ref[...])

    if split_k > 1:
        assert a.ndim == 2 and b.ndim == 2 and k % (split_k * 128) == 0
        tk = k // split_k
        g = split_k
        a_spec = pl.BlockSpec((tt, tk), lambda gi, ti: (ti, gi))
        b_spec = pl.BlockSpec((tt, n), lambda gi, ti: (ti, 0))
        o_spec = pl.BlockSpec((tk, n), lambda gi, ti: (gi, 0))
        out_shape = jax.ShapeDtypeStruct((k, n), F32)
    else:
        g = a.shape[0] if a.ndim == 3 else b.shape[0]
        a_spec = (pl.BlockSpec((None, tt, k), lambda gi, ti: (gi, ti, 0)) if a.ndim == 3
                  else pl.BlockSpec((tt, k), lambda gi, ti: (ti, 0)))
        b_spec = (pl.BlockSpec((None, tt, n), lambda gi, ti: (gi, ti, 0)) if b.ndim == 3
                  else pl.BlockSpec((tt, n), lambda gi, ti: (ti, 0)))
        o_spec = pl.BlockSpec((None, k, n), lambda gi, ti: (gi, 0, 0))
        out_shape = jax.ShapeDtypeStruct((g, k, n), F32)
    return _pallas(body, name, (g, nt), [a_spec, b_spec], [o_spec], [out_shape], (a, b), ("arbitrary", "arbitrary"), carried)[0]


def _inproj_fwd(x1, gain, w_m, w_p, qa_gain, kva_gain, qh_gain, kh_gain, w_uq, w_uk, w_uv, w_uvt, cos, sin, carried=None):
    t, d = x1.shape
    tm = TOKEN_TILE

    def body(x_ref, g_ref, wm_ref, wp_ref, qa_ref, kva_ref, qh_ref, kh_ref, wuq_ref, wuk_ref, wuv_ref, wuvt_ref, cos_ref, sin_ref,
             hb_ref, big_ref, lat_ref, q_ref, k_ref, v_ref, vt_ref):
        h, _ = _rms(x_ref[...], g_ref[...])
        hb = _bf(h)
        hb_ref[...] = hb
        big_ref[:, :M_COLS] = _mm_nt(hb, wm_ref[...])
        pp = _mm_nt(hb, wp_ref[...])
        big_ref[:, M_COLS:] = pp[:, :2 * D_MODEL]
        lat = pp[:, 2 * D_MODEL:]
        lat_ref[...] = lat
        cq, _ = _rms(lat[:, :Q_LORA], qa_ref[...])
        ckv, _ = _rms(lat[:, Q_LORA:Q_LORA + KV_LORA], kva_ref[...])
        k_rope = lat[:, Q_LORA + KV_LORA:]
        cqb = _bf(cq)
        ckvb = _bf(ckv)
        cos_v = cos_ref[...]
        sin_v = sin_ref[...]
        q_all = _mm(cqb, wuq_ref[...])
        k_all = _mm(ckvb, wuk_ref[...])
        v_ref[...] = _bf(_mm(ckvb, wuv_ref[...]))
        vt_all = _mm_nt(wuvt_ref[...], ckvb)
        for hd in range(N_HEADS):
            lanes = slice(hd * HEAD_PAD, (hd + 1) * HEAD_PAD)
            qn, _ = _rms(q_all[:, lanes], qh_ref[...], QK_DIM)
            q_ref[hd] = _bf(_rope(qn, cos_v, sin_v))
            kn, _ = _rms(k_all[:, lanes] + k_rope, kh_ref[...], QK_DIM)
            k_ref[hd] = _bf(_rope(kn, cos_v, sin_v))
            vt_ref[hd] = _bf(vt_all[hd * V_DIM:(hd + 1) * V_DIM])

    tok = lambda c: pl.BlockSpec((tm, c), lambda i: (i, 0))
    head = lambda c: pl.BlockSpec((N_HEADS, tm, c), lambda i: (0, i, 0))
    return _pallas(
        body, "inproj_fwd", (t // tm,),
        [tok(d), _const((1, d)), _resident(w_m.shape), _resident(w_p.shape), _const((1, Q_LORA)), _const((1, KV_LORA)),
         _const((1, HEAD_PAD)), _const((1, HEAD_PAD)), _resident(w_uq.shape), _resident(w_uk.shape),
         _resident(w_uv.shape), _resident(w_uvt.shape), tok(HEAD_PAD), tok(HEAD_PAD)],
        [tok(d), tok(BIG_COLS), tok(LAT_COLS), head(HEAD_PAD), head(HEAD_PAD), tok(N_HEADS * V_DIM),
         pl.BlockSpec((N_HEADS, V_DIM, tm), lambda i: (0, 0, i))],
        [jax.ShapeDtypeStruct((t, d), BF16), jax.ShapeDtypeStruct((t, BIG_COLS), F32),
         jax.ShapeDtypeStruct((t, LAT_COLS), F32), jax.ShapeDtypeStruct((N_HEADS, t, HEAD_PAD), BF16),
         jax.ShapeDtypeStruct((N_HEADS, t, HEAD_PAD), BF16), jax.ShapeDtypeStruct((t, N_HEADS * V_DIM), BF16),
         jax.ShapeDtypeStruct((N_HEADS, V_DIM, t), BF16)],
        (x1, gain, w_m, w_p, qa_gain, kva_gain, qh_gain, kh_gain, w_uq, w_uk, w_uv, w_uvt, cos, sin), ("arbitrary",), carried)


EXP2_SCALE = ATTN_SCALE * 1.4426950408889634


def _diagonal_keep(tk, tq):
    return lax.broadcasted_iota(jnp.int32, (tk, tq), 0) <= lax.broadcasted_iota(jnp.int32, (tk, tq), 1)


def _attn_fwd(q, k, vt, seq):
    _, t, _ = q.shape
    nseq = t // seq
    tq = tk = ATTN_TILE
    nq = seq // tq

    def body(q_ref, k_ref, vt_ref, o_ref, lse_ref):
        i = pl.program_id(1)
        qs = [q_ref[h] for h in range(N_HEADS)]
        keep = _diagonal_keep(tk, tq)

        def tile(h, state, k0, diagonal):
            m, l, acc = state
            st = _mm_nt(k_ref[h, pl.ds(k0, tk), :], qs[h])
            if diagonal:
                st = jnp.where(keep, st, MASK_VALUE)
            m_new = jnp.maximum(m, jnp.max(st, axis=0, keepdims=True))
            pt = jnp.exp2((st - m_new) * EXP2_SCALE)
            alpha = jnp.exp2((m - m_new) * EXP2_SCALE)
            l_new = alpha * l + jnp.sum(pt, axis=0, keepdims=True)
            return m_new, l_new, alpha * acc + _mm(vt_ref[h, :, pl.ds(k0, tk)], _bf(pt))

        def step(j, states):
            k0 = pl.multiple_of(j * tk, tk)
            return tuple(tile(h, states[h], k0, False) for h in range(N_HEADS))

        init = tuple((jnp.full((1, tq), MASK_VALUE, F32), jnp.zeros((1, tq), F32), jnp.zeros((V_DIM, tq), F32))
                     for _ in range(N_HEADS))
        states = lax.fori_loop(0, i, step, init)
        k0 = pl.multiple_of(i * tk, tk)
        outs = []
        for h in range(N_HEADS):
            m, l, acc = tile(h, states[h], k0, True)
            outs.append((acc / l).T)
            lse_ref[h] = m * EXP2_SCALE + jnp.log2(l)
        o_ref[...] = _bf(jnp.concatenate(outs, axis=-1))

    return pl.pallas_call(
        body, name="attn_fwd", grid=(nseq, nq),
        in_specs=[pl.BlockSpec((N_HEADS, tq, HEAD_PAD), lambda b, i: (0, b * nq + i, 0)),
                  pl.BlockSpec((N_HEADS, seq, HEAD_PAD), lambda b, i: (0, b, 0)),
                  pl.BlockSpec((N_HEADS, V_DIM, seq), lambda b, i: (0, 0, b))],
        out_specs=[pl.BlockSpec((tq, N_HEADS * V_DIM), lambda b, i: (b * nq + i, 0)),
                   pl.BlockSpec((N_HEADS, 1, tq), lambda b, i: (0, 0, b * nq + i))],
        out_shape=[jax.ShapeDtypeStruct((t, N_HEADS * V_DIM), BF16), jax.ShapeDtypeStruct((N_HEADS, 1, t), F32)],
        compiler_params=_params(("arbitrary", "arbitrary")))(q, k, vt)


ATTN_BWD_HEADS = 4


def _attn_bwd(q, k, v, do, lse, delta, seq, carried=None):
    _, t, _ = q.shape
    nseq = t // seq
    tq = tk = ATTN_TILE
    n = seq // tq
    hb = ATTN_BWD_HEADS

    def body(q_ref, k_ref, v_ref, do_ref, lse_ref, delta_ref, dq_ref, dk_ref, dv_ref):
        dq_ref[...] = jnp.zeros_like(dq_ref)
        dk_ref[...] = jnp.zeros_like(dk_ref)
        dv_ref[...] = jnp.zeros_like(dv_ref)
        keep = _diagonal_keep(tk, tq)

        def tile(h, k0, q0, diagonal):
            kj = k_ref[h, pl.ds(k0, tk), :]
            qi = q_ref[h, pl.ds(q0, tq), :]
            doi = _bf(do_ref[pl.ds(q0, tq), h * V_DIM:(h + 1) * V_DIM])
            st = _mm_nt(kj, qi)
            if diagonal:
                st = jnp.where(keep, st, MASK_VALUE)
            pt = jnp.exp2(st * EXP2_SCALE - lse_ref[h, :, pl.ds(q0, tq)])
            dv_ref[pl.ds(k0, tk), h * V_DIM:(h + 1) * V_DIM] += _mm(_bf(pt), doi)
            dpt = _mm_nt(v_ref[pl.ds(k0, tk), h * V_DIM:(h + 1) * V_DIM], doi)
            dst = _bf((pt * (dpt - delta_ref[pl.ds(h, 1), pl.ds(q0, tq)])) * ATTN_SCALE)
            dk_ref[h, pl.ds(k0, tk), :] += _mm(dst, qi)
            dq_ref[h, pl.ds(q0, tq), :] += _mm_tn(dst, kj)

        def kv_step(j, _):
            k0 = pl.multiple_of(j * tk, tk)
            for h in range(hb):
                tile(h, k0, k0, True)

            def q_step(i, _):
                q0 = pl.multiple_of(i * tq, tq)
                for h in range(hb):
                    tile(h, k0, q0, False)
                return 0

            lax.fori_loop(j + 1, n, q_step, 0)
            return 0

        lax.fori_loop(0, n, kv_step, 0)

    hspec = lambda c: pl.BlockSpec((hb, seq, c), lambda b, g: (g, b, 0))
    cols = pl.BlockSpec((seq, hb * V_DIM), lambda b, g: (b, g))
    return _pallas(
        body, "attn_bwd", (nseq, N_HEADS // hb),
        [hspec(HEAD_PAD), hspec(HEAD_PAD), cols, cols,
         pl.BlockSpec((hb, 1, seq), lambda b, g: (g, 0, b)), pl.BlockSpec((None, hb, seq), lambda b, g: (g, 0, b))],
        [hspec(HEAD_PAD), hspec(HEAD_PAD), cols],
        [jax.ShapeDtypeStruct((N_HEADS, t, HEAD_PAD), F32), jax.ShapeDtypeStruct((N_HEADS, t, HEAD_PAD), F32),
         jax.ShapeDtypeStruct((t, N_HEADS * V_DIM), F32)],
        (q, k, v, do, lse, delta), ("arbitrary", "arbitrary"), carried)


def _mixer_values(o_ref, gb_ref, gla_ref, glb_ref, xc_ref, gc_ref, xcp_ref, gcp_ref, bias_ref, cw_ref, wpa_ref, wpc_ref,
                  first_of_seq):
    gb = gb_ref[...]
    u = gc_ref[...] * xc_ref[...]
    u_prev = jnp.where(first_of_seq, 0.0, gcp_ref[...] * xcp_ref[...])
    cw = cw_ref[...]
    z = cw[2:3] * u + cw[1:2] * _shift_down(u, u_prev, 1) + cw[0:1] * _shift_down(u, u_prev, 2)
    gbz = _bf(gb * z)
    y_b = _mm(gbz, wpc_ref[...])
    y_a = _mm(o_ref[...], wpa_ref[...])
    bias = bias_ref[...]
    gate_a = _sigmoid(gla_ref[...] + bias[:, :D_MODEL])
    gate_b = _sigmoid(glb_ref[...] + bias[:, D_MODEL:])
    merged = _bf(gate_a * y_a + gate_b * y_b)
    return gb, u, z, gbz, y_a, y_b, gate_a, gate_b, merged


def _mixer_specs(tm, seq):
    d = D_MODEL
    tok = pl.BlockSpec((tm, d), lambda i: (i, 0))
    col = lambda c: pl.BlockSpec((tm, d), lambda i: (i, c))
    prev = lambda c: pl.BlockSpec((8, d), lambda i: (jnp.maximum(i * (tm // 8) - 1, 0), c))
    o_spec = pl.BlockSpec((tm, N_HEADS * V_DIM), lambda i: (i, 0))
    fwd_specs = [o_spec, col(0), col(1), col(2), col(3), col(4), prev(3), prev(4), _const((1, 2 * d)), _const((3, d)),
                 _resident((N_HEADS * V_DIM, d)), _resident((d, d)), _resident((d, d))]
    return tok, fwd_specs


def _mix_fwd(x1, o, big, gate_bias, conv_w, w_pa, w_pc, w_out, seq):
    t, d = x1.shape
    tm = TOKEN_TILE
    tiles_per_seq = seq // tm

    def body(x_ref, o_ref, gb_ref, gla_ref, glb_ref, xc_ref, gc_ref, xcp_ref, gcp_ref, bias_ref, cw_ref, wpa_ref, wpc_ref,
             wout_ref, x2_ref):
        first = pl.program_id(0) % tiles_per_seq == 0
        merged = _mixer_values(o_ref, gb_ref, gla_ref, glb_ref, xc_ref, gc_ref, xcp_ref, gcp_ref, bias_ref, cw_ref, wpa_ref,
                               wpc_ref, first)[-1]
        x2_ref[...] = x_ref[...] + _mm(merged, wout_ref[...])

    tok, fwd_specs = _mixer_specs(tm, seq)
    return pl.pallas_call(
        body, name="mix_fwd", grid=(t // tm,), in_specs=[tok] + fwd_specs, out_specs=tok,
        out_shape=jax.ShapeDtypeStruct((t, d), F32),
        compiler_params=_params(("arbitrary",)))(x1, o, big, big, big, big, big, big, big, gate_bias, conv_w, w_pa, w_pc, w_out)


def _mix_bwd(dx2, o, big, gate_bias, conv_w, w_pa, w_pc, w_out, seq, carried=None):
    t, d = dx2.shape
    tm = TOKEN_TILE
    tiles_per_seq = seq // tm
    hv = N_HEADS * V_DIM

    def body(dx_ref, o_ref, gb_ref, gla_ref, glb_ref, xc_ref, gc_ref, xcp_ref, gcp_ref, bias_ref, cw_ref, wpa_ref, wpc_ref,
             wout_ref, do_ref, delta_ref, dz_ref, dm_ref, dbias_ref, dwpa_ref, dwpc_ref, dwout_ref):
        first = pl.program_id(0) % tiles_per_seq == 0
        gb, _, z, gbz, y_a, y_b, gate_a, gate_b, merged = _mixer_values(
            o_ref, gb_ref, gla_ref, glb_ref, xc_ref, gc_ref, xcp_ref, gcp_ref, bias_ref, cw_ref, wpa_ref, wpc_ref, first)

        @pl.when(pl.program_id(0) == 0)
        def _():
            dbias_ref[...] = jnp.zeros_like(dbias_ref)
            dwpa_ref[...] = jnp.zeros_like(dwpa_ref)
            dwpc_ref[...] = jnp.zeros_like(dwpc_ref)
            dwout_ref[...] = jnp.zeros_like(dwout_ref)

        dxb = _bf(dx_ref[...])
        dmerged = _mm_nt(dxb, wout_ref[...])
        dwout_ref[...] += _mm_tn(merged, dxb)
        dla = (dmerged * y_a) * (gate_a * (1.0 - gate_a))
        dlb = (dmerged * y_b) * (gate_b * (1.0 - gate_b))
        dbias_ref[:, :d] += jnp.sum(dla, axis=0, keepdims=True)
        dbias_ref[:, d:] += jnp.sum(dlb, axis=0, keepdims=True)
        dya = _bf(dmerged * gate_a)
        dyb = _bf(dmerged * gate_b)
        do_v = _mm_nt(dya, wpa_ref[...])
        do_ref[...] = do_v
        head = lax.broadcasted_iota(jnp.int32, (N_HEADS, hv), 0) * V_DIM
        col = lax.broadcasted_iota(jnp.int32, (N_HEADS, hv), 1)
        in_head = ((col >= head) & (col < head + V_DIM)).astype(F32)
        delta_ref[...] = lax.dot_general(in_head, do_v * o_ref[...].astype(F32), (((1,), (1,)), ((), ())),
                                         precision=lax.Precision.HIGHEST, preferred_element_type=F32)
        dwpa_ref[...] += _mm_tn(o_ref[...], dya)
        dgz = _mm_nt(dyb, wpc_ref[...])
        dwpc_ref[...] += _mm_tn(gbz, dyb)
        dz_ref[...] = dgz * gb
        dm_ref[:, :d] = _bf(dgz * z)
        dm_ref[:, d:2 * d] = _bf(dla)
        dm_ref[:, 2 * d:] = _bf(dlb)

    tok, fwd_specs = _mixer_specs(tm, seq)
    return _pallas(
        body, "mix_bwd", (t // tm,), [tok] + fwd_specs,
        [pl.BlockSpec((tm, hv), lambda i: (i, 0)), pl.BlockSpec((N_HEADS, tm), lambda i: (0, i)), tok,
         pl.BlockSpec((tm, M_COLS), lambda i: (i, 0)), _const((1, 2 * d)), _const((hv, d)), _const((d, d)), _const((d, d))],
        [jax.ShapeDtypeStruct((t, hv), F32), jax.ShapeDtypeStruct((N_HEADS, t), F32), jax.ShapeDtypeStruct((t, d), F32),
         jax.ShapeDtypeStruct((t, M_COLS), BF16), jax.ShapeDtypeStruct((1, 2 * d), F32), jax.ShapeDtypeStruct((hv, d), F32),
         jax.ShapeDtypeStruct((d, d), F32), jax.ShapeDtypeStruct((d, d), F32)],
        (dx2, o, big, big, big, big, big, big, big, gate_bias, conv_w, w_pa, w_pc, w_out), ("arbitrary",), carried)


def _prep_bwd(lat, big, dz, dq, dk, dv, qa_gain, kva_gain, qh_gain, kh_gain, w_uq, w_uk, w_uv, cos, sin, conv_w, seq, carried=None):
    t = lat.shape[0]
    d = D_MODEL
    tm = TOKEN_TILE
    tiles_per_seq = seq // tm
    last_blk = t // 8 - 1

    def body(lat_ref, xc_ref, gc_ref, dz_ref, dzn_ref, dq_ref, dk_ref, dv_ref, qa_ref, kva_ref, qh_ref, kh_ref, wuq_ref, wuk_ref,
             wuv_ref, cos_ref, sin_ref, cw_ref,
             dp_ref, dwuq_ref, dwuk_ref, dwuv_ref, dqa_ref, dkva_ref, dqh_ref, dkh_ref, dcw_ref):
        pid = pl.program_id(0)

        @pl.when(pid == 0)
        def _():
            for r in (dwuq_ref, dwuk_ref, dwuv_ref, dqa_ref, dkva_ref, dqh_ref, dkh_ref, dcw_ref):
                r[...] = jnp.zeros_like(r)

        last = pid % tiles_per_seq == tiles_per_seq - 1
        dzv = dz_ref[...]
        dz_next = jnp.where(last, 0.0, dzn_ref[...])
        dz1 = _shift_up(dzv, dz_next, 1)
        dz2 = _shift_up(dzv, dz_next, 2)
        cw = cw_ref[...]
        xc = xc_ref[...]
        gc = gc_ref[...]
        u = gc * xc
        du = cw[2:3] * dzv + cw[1:2] * dz1 + cw[0:1] * dz2
        dp_ref[:, :d] = _bf(du * gc)
        dp_ref[:, d:2 * d] = _bf(du * xc)
        dcw_ref[0:1, :] += jnp.sum(dz2 * u, axis=0, keepdims=True)
        dcw_ref[1:2, :] += jnp.sum(dz1 * u, axis=0, keepdims=True)
        dcw_ref[2:3, :] += jnp.sum(dzv * u, axis=0, keepdims=True)

        lat_v = lat_ref[...]
        q_lat = lat_v[:, :Q_LORA]
        kv_lat = lat_v[:, Q_LORA:Q_LORA + KV_LORA]
        k_rope = lat_v[:, Q_LORA + KV_LORA:]
        qa_gain_v = qa_ref[...]
        kva_gain_v = kva_ref[...]
        qh_gain_v = qh_ref[...]
        kh_gain_v = kh_ref[...]
        cq, rq = _rms(q_lat, qa_gain_v)
        ckv, rkv = _rms(kv_lat, kva_gain_v)
        cqb = _bf(cq)
        ckvb = _bf(ckv)
        cos_v = cos_ref[...]
        sin_v = sin_ref[...]
        lane = lax.broadcasted_iota(jnp.int32, (tm, HEAD_PAD), 1)
        rope_lanes = (lane >= QK_NOPE) & (lane < QK_DIM)
        dk_rope = jnp.zeros((tm, HEAD_PAD), F32)
        dqh_gain = jnp.zeros((1, HEAD_PAD), F32)
        dkh_gain = jnp.zeros((1, HEAD_PAD), F32)
        q_all = _mm(cqb, wuq_ref[...])
        k_all = _mm(ckvb, wuk_ref[...])
        dq_heads, dk_heads = [], []
        for hd in range(N_HEADS):
            lanes = slice(hd * HEAD_PAD, (hd + 1) * HEAD_PAD)
            q_pre = q_all[:, lanes]
            _, rr = _rms(q_pre, qh_gain_v, QK_DIM)
            dq_pre, dg = _rms_bwd(q_pre, rr, qh_gain_v, _rope_bwd(dq_ref[hd], cos_v, sin_v), QK_DIM)
            dqh_gain = dqh_gain + dg
            dq_heads.append(_bf(dq_pre))

            k_pre = k_all[:, lanes] + k_rope
            _, rr = _rms(k_pre, kh_gain_v, QK_DIM)
            dk_pre, dg = _rms_bwd(k_pre, rr, kh_gain_v, _rope_bwd(dk_ref[hd], cos_v, sin_v), QK_DIM)
            dkh_gain = dkh_gain + dg
            dk_rope = dk_rope + jnp.where(rope_lanes, dk_pre, 0.0)
            dk_heads.append(_bf(dk_pre))
        dq_all = jnp.concatenate(dq_heads, axis=1)
        dk_all = jnp.concatenate(dk_heads, axis=1)
        dvb = _bf(dv_ref[...])
        dcq = _mm_nt(dq_all, wuq_ref[...])
        dckv = _mm_nt(dk_all, wuk_ref[...]) + _mm_nt(dvb, wuv_ref[...])
        dwuq_ref[...] += _mm_tn(cqb, dq_all)
        dwuk_ref[...] += _mm_tn(ckvb, dk_all)
        dwuv_ref[...] += _mm_tn(ckvb, dvb)
        dqh_ref[...] += dqh_gain
        dkh_ref[...] += dkh_gain
        dq_lat, dg = _rms_bwd(q_lat, rq, qa_gain_v, dcq)
        dqa_ref[...] += dg
        dkv_lat, dg = _rms_bwd(kv_lat, rkv, kva_gain_v, dckv)
        dkva_ref[...] += dg
        dp_ref[:, 2 * d:2 * d + Q_LORA] = _bf(dq_lat)
        dp_ref[:, 2 * d + Q_LORA:2 * d + Q_LORA + KV_LORA] = _bf(dkv_lat)
        dp_ref[:, 2 * d + Q_LORA + KV_LORA:] = _bf(dk_rope)

    tok = lambda c: pl.BlockSpec((tm, c), lambda i: (i, 0))
    col = lambda c: pl.BlockSpec((tm, d), lambda i: (i, c))
    head = lambda c: pl.BlockSpec((N_HEADS, tm, c), lambda i: (0, i, 0))
    nxt = pl.BlockSpec((8, d), lambda i: (jnp.minimum((i + 1) * (tm // 8), last_blk), 0))
    return _pallas(
        body, "prep_bwd", (t // tm,),
        [tok(LAT_COLS), col(3), col(4), tok(d), nxt, head(HEAD_PAD), head(HEAD_PAD), tok(N_HEADS * V_DIM),
         _const((1, Q_LORA)), _const((1, KV_LORA)), _const((1, HEAD_PAD)), _const((1, HEAD_PAD)),
         _resident(w_uq.shape), _resident(w_uk.shape), _resident(w_uv.shape), tok(HEAD_PAD), tok(HEAD_PAD), _const((3, d))],
        [tok(P_COLS), _const(w_uq.shape), _const(w_uk.shape), _const(w_uv.shape), _const((1, Q_LORA)),
         _const((1, KV_LORA)), _const((1, HEAD_PAD)), _const((1, HEAD_PAD)), _const((3, d))],
        [jax.ShapeDtypeStruct((t, P_COLS), BF16), jax.ShapeDtypeStruct(w_uq.shape, F32),
         jax.ShapeDtypeStruct(w_uk.shape, F32), jax.ShapeDtypeStruct(w_uv.shape, F32),
         jax.ShapeDtypeStruct((1, Q_LORA), F32), jax.ShapeDtypeStruct((1, KV_LORA), F32),
         jax.ShapeDtypeStruct((1, HEAD_PAD), F32), jax.ShapeDtypeStruct((1, HEAD_PAD), F32), jax.ShapeDtypeStruct((3, d), F32)],
        (lat, big, big, dz, dz, dq, dk, dv, qa_gain, kva_gain, qh_gain, kh_gain, w_uq, w_uk, w_uv, cos, sin, conv_w),
        ("arbitrary",), carried)


def _inproj_bwd(x1, gain, dx2, dm, dp, w_m, w_p, carried=None):
    t, d = x1.shape
    tm = TOKEN_TILE

    def body(x_ref, g_ref, dx2_ref, dm_ref, dp_ref, wm_ref, wp_ref, dx1_ref, dgain_ref):
        xv = x_ref[...]
        gain_v = g_ref[...]
        _, r = _rms(xv, gain_v)
        dh = _mm(dm_ref[...], wm_ref[...]) + _mm(dp_ref[...], wp_ref[...])
        dxn, dgain = _rms_bwd(xv, r, gain_v, dh)
        dx1_ref[...] = dx2_ref[...] + dxn

        @pl.when(pl.program_id(0) == 0)
        def _():
            dgain_ref[...] = jnp.zeros_like(dgain_ref)

        dgain_ref[...] += dgain

    tok = lambda c: pl.BlockSpec((tm, c), lambda i: (i, 0))
    return _pallas(
        body, "inproj_bwd", (t // tm,),
        [tok(d), _const((1, d)), tok(d), tok(M_COLS), tok(P_COLS), _resident(w_m.shape), _resident(w_p.shape)],
        [tok(d), _const((1, d))], [jax.ShapeDtypeStruct((t, d), F32), jax.ShapeDtypeStruct((1, d), F32)],
        (x1, gain, dx2, dm, dp, w_m, w_p), ("arbitrary",), carried)


def _adamw(quads, name, carried=None):
    k = len(quads)
    rows, cols = quads[0][0].shape
    tr, tc = rows, cols
    for cand in (512, 352, 256, 192, 128, 64):
        if rows % cand == 0 and rows > cand:
            tr = cand
            break
    if tr == rows and rows * cols > 512 * 1024 and cols % 256 == 0:
        tc = 256
    while k * 14 * tr * tc * 4 > VMEM_LIMIT // 2 and tr % 16 == 0:
        tr //= 2

    def body(*refs):
        for i in range(k):
            w_ref, g_ref, m_ref, v_ref = refs[4 * i:4 * i + 4]
            delta_ref, nm_ref, nv_ref = refs[4 * k + 3 * i:4 * k + 3 * i + 3]
            gv = g_ref[...]
            nm = ADAM_B1 * m_ref[...] + (1.0 - ADAM_B1) * gv
            nv = ADAM_B2 * v_ref[...] + (1.0 - ADAM_B2) * (gv * gv)
            m_hat = nm * (1.0 / (1.0 - ADAM_B1 ** ADAM_STEP))
            v_hat = nv * (1.0 / (1.0 - ADAM_B2 ** ADAM_STEP))
            delta_ref[...] = -ADAM_LR * (m_hat / (jnp.sqrt(v_hat) + ADAM_EPS) + ADAM_WD * w_ref[...])
            nm_ref[...] = nm
            nv_ref[...] = nv

    spec = pl.BlockSpec((tr, tc), lambda i, j: (i, j))
    shape = jax.ShapeDtypeStruct((rows, cols), F32)
    outs = _pallas(body, name, (rows // tr, cols // tc), [spec] * (4 * k), [spec] * (3 * k), [shape] * (3 * k),
                   [a for quad in quads for a in quad], ("arbitrary", "arbitrary"), carried)
    return [tuple(outs[3 * i:3 * i + 3]) for i in range(k)]


def _place():
    x, y, c = lax.axis_index("x"), lax.axis_index("y"), lax.axis_index("c")
    other_chips = [(1 - x, y), (x, 1 - y), (1 - x, 1 - y)]
    return x, y, c, other_chips


def _remote(src, dst, sems, send, recv, device):
    return pltpu.make_async_remote_copy(src_ref=src, dst_ref=dst, send_sem=sems.at[send], recv_sem=sems.at[recv],
                                        device_id=device, device_id_type=MESH_ID)


def _cast_shards(shards, out_dtypes):
    n = len(shards)

    def body(*refs):
        ins, outs, stage, sems = refs[:n], refs[n:2 * n], refs[2 * n:3 * n], refs[3 * n]
        x, y, _, _ = _place()
        me = 2 * x + y
        copies = []
        for w in range(n):
            stage[w][...] = ins[w][...].astype(out_dtypes[w])
            copies.append(pltpu.make_async_copy(stage[w], outs[w].at[me], sems.at[w]))
            copies[-1].start()
        for cp in copies:
            cp.wait()

    vm = pl.BlockSpec(memory_space=pltpu.VMEM)
    return pl.pallas_call(
        body, name="cast_shards", in_specs=[vm] * n, out_specs=[ANY] * n,
        out_shape=[jax.ShapeDtypeStruct((N_CHIPS,) + s.shape, dt) for s, dt in zip(shards, out_dtypes)],
        scratch_shapes=[pltpu.VMEM(s.shape, dt) for s, dt in zip(shards, out_dtypes)] + [pltpu.SemaphoreType.DMA((n,))],
        compiler_params=_params())(*shards)


BF16_ROWS = 16


def _split_rows(rows):
    return (rows // 2) % BF16_ROWS == 0


def _half_shape(rows, cols):
    return (rows // 2, cols) if _split_rows(rows) else (rows, cols // 2)


def _half(rows, cols, which):
    if _split_rows(rows):
        return (pl.ds(pl.multiple_of(which * (rows // 2), BF16_ROWS), rows // 2), slice(None))
    return (slice(None), pl.ds(pl.multiple_of(which * (cols // 2), 128), cols // 2))


def _gather_carried(bufs):
    n = len(bufs)

    def half(w, slot, which):
        _, rows, cols = bufs[w].shape
        return (slot,) + _half(rows, cols, which)

    def start(ins, outs, sems, base):
        x, y, c, other_chips = _place()
        me = 2 * x + y
        for w in range(n):
            mine = outs[w].at[half(w, me, c)]
            for p, (px, py) in enumerate(other_chips):
                _remote(mine, mine, sems, base + 12 * w + p, base + 12 * w + 3 + p, (px, py, c)).start()

    def finish(ins, outs, sems, base):
        x, y, c, other_chips = _place()
        me = 2 * x + y
        for w in range(n):
            for p, (px, py) in enumerate(other_chips):
                got = outs[w].at[half(w, 2 * px + py, c)]
                _remote(got, got, sems, base + 12 * w + p, base + 12 * w + 3 + p, (px, py, c)).wait_recv()
                _remote(got, got, sems, base + 12 * w + 6 + p, base + 12 * w + 9 + p, (x, y, 1 - c)).start()
        for w in range(n):
            mine = outs[w].at[half(w, me, c)]
            for p, (px, py) in enumerate(other_chips):
                got = outs[w].at[half(w, 2 * px + py, c)]
                theirs = outs[w].at[half(w, 2 * px + py, 1 - c)]
                _remote(got, theirs, sems, base + 12 * w + 6 + p, base + 12 * w + 9 + p, (x, y, 1 - c)).wait()
                _remote(mine, mine, sems, base + 12 * w + p, base + 12 * w + 3 + p, (px, py, c)).wait_send()

    shapes = [jax.ShapeDtypeStruct(b.shape, b.dtype) for b in bufs]
    return _Carried(bufs, shapes, {w: w for w in range(n)}, 12 * n, start, finish)


def _swap_carried(grads):
    n = len(grads)

    def copy(w, ins, outs, sems, base):
        x, y, c, _ = _place()
        _, rows, cols = grads[w].shape
        theirs = ins[w].at[(slice(None),) + _half(rows, cols, 1 - c)]
        return _remote(theirs, outs[w], sems, base + 2 * w, base + 2 * w + 1, (x, y, 1 - c))

    def start(ins, outs, sems, base):
        for w in range(n):
            copy(w, ins, outs, sems, base).start()

    def finish(ins, outs, sems, base):
        for w in range(n):
            copy(w, ins, outs, sems, base).wait()

    shapes = [jax.ShapeDtypeStruct((g.shape[0],) + _half_shape(*g.shape[1:]), F32) for g in grads]
    return _Carried(grads, shapes, {}, 2 * n, start, finish)


def _row_tile(rows):
    for cand in (512, 352, 256, 192, 128, 96, 64, 32, 16):
        if rows % cand == 0:
            return cand
    return rows


def _half_block_index(split_rows, tiles, i, core):
    return (core * tiles + i, 0) if split_rows else (i, core)


def _chip_partial(grad, other, place, name):
    nblk, hr, hc = other.shape
    by_rows = _split_rows(grad.shape[1])
    tr = _row_tile(hr)
    tiles = hr // tr

    def body(place_ref, g_ref, o_ref, sum_ref, sum_bf_ref):
        s = g_ref[...] + o_ref[...]
        sum_ref[...] = s
        sum_bf_ref[...] = _bf(s)

    grid_spec = pltpu.PrefetchScalarGridSpec(
        num_scalar_prefetch=1, grid=(nblk, tiles),
        in_specs=[pl.BlockSpec((None, tr, hc), lambda b, i, place_ref: (b,) + _half_block_index(by_rows, tiles, i, place_ref[1])),
                  pl.BlockSpec((None, tr, hc), lambda b, i, place_ref: (b, i, 0))],
        out_specs=[pl.BlockSpec((None, tr, hc), lambda b, i, place_ref: (b, i, 0))] * 2)
    return pl.pallas_call(body, name=name, grid_spec=grid_spec,
                          out_shape=[jax.ShapeDtypeStruct(other.shape, F32), jax.ShapeDtypeStruct(other.shape, BF16)],
                          compiler_params=_params(("arbitrary", "arbitrary")))(place, grad, other)


def _send_carried(partials):
    n = len(partials)

    def start(ins, outs, sems, base):
        x, y, c, other_chips = _place()
        me = 2 * x + y
        for w in range(n):
            for p, (px, py) in enumerate(other_chips):
                _remote(ins[w].at[2 * px + py], outs[w].at[me], sems, base + 6 * w + p, base + 6 * w + 3 + p, (px, py, c)).start()

    def finish(ins, outs, sems, base):
        x, y, c, other_chips = _place()
        for w in range(n):
            for p, (px, py) in enumerate(other_chips):
                _remote(ins[w].at[2 * px + py], outs[w].at[2 * px + py], sems, base + 6 * w + p, base + 6 * w + 3 + p,
                        (px, py, c)).wait()

    return _Carried(partials, [jax.ShapeDtypeStruct(p.shape, BF16) for p in partials], {}, 6 * n, start, finish)


def _chip_total(own, received, place, shape, name):
    nblk, hr, hc = own.shape
    by_rows = _split_rows(shape[0])
    tr = _row_tile(hr)
    tiles = hr // tr

    def body(place_ref, own_ref, r1_ref, r2_ref, r3_ref, out_ref):
        out_ref[...] = own_ref[...] + ((r1_ref[...].astype(F32) + r2_ref[...].astype(F32)) + r3_ref[...].astype(F32))

    def slot(k):
        return pl.BlockSpec((None, tr, hc), lambda i, place_ref: ((place_ref[0] + k) % N_CHIPS, i, 0))

    grid_spec = pltpu.PrefetchScalarGridSpec(
        num_scalar_prefetch=1, grid=(tiles,), in_specs=[slot(0), slot(1), slot(2), slot(3)],
        out_specs=pl.BlockSpec((tr, hc), lambda i, place_ref: _half_block_index(by_rows, tiles, i, place_ref[1])))
    return pl.pallas_call(body, name=name, grid_spec=grid_spec, out_shape=jax.ShapeDtypeStruct(tuple(shape), F32),
                          compiler_params=_params(("arbitrary",)))(place, own, received, received, received)


def _join_carried(totals):
    n = len(totals)

    def copy(w, outs, sems, base):
        x, y, c, _ = _place()
        mine = outs[w].at[_half(*totals[w].shape, c)]
        return _remote(mine, mine, sems, base + 2 * w, base + 2 * w + 1, (x, y, 1 - c))

    def start(ins, outs, sems, base):
        for w in range(n):
            copy(w, outs, sems, base).start()

    def finish(ins, outs, sems, base):
        for w in range(n):
            copy(w, outs, sems, base).wait()

    shapes = [jax.ShapeDtypeStruct(a.shape, F32) for a in totals]
    return _Carried(totals, shapes, {w: w for w in range(n)}, 2 * n, start, finish)


def _sum_devices(vec):
    rows, n = vec.shape

    def body(v_ref, out_ref, buf, send_sems, recv_sems):
        x, y, c, _ = _place()
        me = 4 * x + 2 * y + c
        buf[me] = v_ref[...]
        sends = []
        for k in range(1, N_DEV):
            peer = (1 - x if k & 4 else x, 1 - y if k & 2 else y, 1 - c if k & 1 else c)
            cp = pltpu.make_async_remote_copy(src_ref=v_ref, dst_ref=buf.at[me], send_sem=send_sems.at[k], recv_sem=recv_sems.at[k],
                                              device_id=peer, device_id_type=MESH_ID)
            cp.start()
            sends.append(cp)
        for cp in sends:
            cp.wait()
        total = buf[0]
        for dev in range(1, N_DEV):
            total = total + buf[dev]
        out_ref[...] = total

    vm = pl.BlockSpec(memory_space=pltpu.VMEM)
    return pl.pallas_call(
        body, name="sum_devices", in_specs=[vm], out_specs=vm, out_shape=jax.ShapeDtypeStruct((rows, n), F32),
        scratch_shapes=[pltpu.VMEM((N_DEV, rows, n), F32), pltpu.SemaphoreType.DMA((N_DEV,)), pltpu.SemaphoreType.DMA((N_DEV,))],
    )(vec)


def _rope_tables(positions):
    half = QK_ROPE // 2
    inv_freq = 1.0 / (ROPE_THETA ** (jnp.arange(half, dtype=F32) / half))
    ang = positions.astype(F32).reshape(-1, 1) * inv_freq
    cos, sin = jnp.cos(ang), jnp.sin(ang)
    t = ang.shape[0]
    ones, zeros = jnp.ones((t, QK_NOPE), F32), jnp.zeros((t, QK_NOPE), F32)
    pad = HEAD_PAD - QK_DIM
    cos_full = jnp.concatenate([ones, cos, cos, ones[:, :pad]], axis=1)
    sin_signed = jnp.concatenate([zeros, -sin, sin, zeros[:, :pad]], axis=1)
    return cos_full, sin_signed


def _partials(names, grads, from_sibling, place):
    return [_chip_partial(g, o, place, "chip_partial_" + n) for n, g, o in zip(names, grads, from_sibling)]


def _totals(names, grads, partials, received, place):
    return [_chip_total(pf, r, place, g.shape[1:], "chip_total_" + n) for n, g, (pf, _), r in zip(names, grads, partials, received)]


def _kernel_layouts(full):
    d = D_MODEL
    w_in = full["w_in"]
    o_kr = Q_LORA + KV_LORA
    o_xc = o_kr + QK_ROPE
    o_gb = o_xc + d
    o_gc = o_gb + d
    o_gl = o_gc + d
    k_rope_pad = jnp.pad(w_in[o_kr:o_xc], ((QK_NOPE, HEAD_PAD - QK_DIM), (0, 0)))
    w_m = jnp.concatenate([w_in[o_gb:o_gc], w_in[o_gl:]], axis=0)
    w_p = jnp.concatenate([w_in[o_xc:o_gb], w_in[o_gc:o_gl], w_in[:o_kr], k_rope_pad], axis=0)
    w_uq = jnp.pad(full["w_uq"].reshape(Q_LORA, N_HEADS, QK_DIM), ((0, 0), (0, 0), (0, HEAD_PAD - QK_DIM)))
    w_uk = jnp.pad(full["w_uk"].reshape(KV_LORA, N_HEADS, QK_NOPE), ((0, 0), (0, 0), (0, HEAD_PAD - QK_NOPE)))
    return {"w_m": w_m, "w_p": w_p, "w_uq": w_uq.reshape(Q_LORA, N_HEADS * HEAD_PAD), "w_uk": w_uk.reshape(KV_LORA, N_HEADS * HEAD_PAD),
            "w_uv": full["w_uv"], "w_uvt": full["w_uv"].T}


def _global_layouts(g):
    d = D_MODEL
    dm, dp = g["w_m"], g["w_p"]
    o_lat = 2 * d
    o_kr = o_lat + Q_LORA + KV_LORA + QK_NOPE
    w_in = jnp.concatenate([dp[o_lat:o_lat + Q_LORA + KV_LORA], dp[o_kr:o_kr + QK_ROPE], dp[:d], dm[:d], dp[d:o_lat], dm[d:]], axis=0)
    w_uq = g["w_uq"].reshape(Q_LORA, N_HEADS, HEAD_PAD)[:, :, :QK_DIM].reshape(Q_LORA, N_HEADS * QK_DIM)
    w_uk = g["w_uk"].reshape(KV_LORA, N_HEADS, HEAD_PAD)[:, :, :QK_NOPE].reshape(KV_LORA, N_HEADS * QK_NOPE)
    return {"w_in": w_in, "w_uq": w_uq, "w_uk": w_uk, "w_uv": g["w_uv"], "w_proj_attn": g["w_pa"], "w_proj_conv": g["w_pc"],
            "w_out": g["w_out"]}


def _col_blocks(a):
    r, c = a.shape
    return a.reshape(r, N_CHIPS, c // N_CHIPS).transpose(1, 0, 2)


def _from_col_blocks(a):
    n, r, c = a.shape
    return a.transpose(1, 0, 2).reshape(r, n * c)


COL_SHARDED = ("w_uq", "w_uk", "w_uv", "w_proj_attn")
TRANSPOSED = ("ffn1_w_gate", "ffn1_w_up", "ffn2_w_gate", "ffn2_w_up", "w_in")
SMALL = (("ffn1_norm", 1024), ("mix_norm", 1024), ("gate_bias", 2048), ("q_a_norm", 384), ("kv_a_norm", 256),
         ("q_head_norm", 128), ("k_head_norm", 128), ("ffn2_norm", 1024))
WEIGHT_ORDER = ("ffn1_norm", "ffn1_w_gate", "ffn1_w_up", "ffn1_w_down", "mix_norm", "w_in", "gate_bias", "q_a_norm", "w_uq",
                "kv_a_norm", "w_uk", "w_uv", "q_head_norm", "k_head_norm", "w_proj_attn", "conv_w", "w_proj_conv", "w_out",
                "ffn2_norm", "ffn2_w_gate", "ffn2_w_up", "ffn2_w_down")
MATRICES = ("ffn1_w_gate", "ffn1_w_up", "ffn1_w_down", "w_in", "w_uq", "w_uk", "w_uv", "w_proj_attn", "w_proj_conv", "w_out",
            "ffn2_w_gate", "ffn2_w_up", "ffn2_w_down")
GROUP_FFN1 = ("ffn1_w_gate", "ffn1_w_up", "ffn1_w_down")
GROUP_IN = ("w_in", "w_uq", "w_uk", "w_uv", "conv_w")
GROUP_MIX = ("w_proj_attn", "w_proj_conv", "w_out")
GROUP_FFN2 = ("ffn2_w_gate", "ffn2_w_up", "ffn2_w_down")
GROUP_MID = ("w_in", "w_uq", "w_uk", "w_uv", "w_proj_attn", "w_proj_conv", "w_out")


def _pad_lanes(a, n):
    return jnp.pad(a.reshape(1, -1), ((0, 0), (0, n - a.size)))


def kernel(x, positions, ffn1_norm, ffn1_w_gate, ffn1_w_up, ffn1_w_down, mix_norm, w_in, gate_bias, q_a_norm, w_uq, kv_a_norm, w_uk, w_uv, q_head_norm, k_head_norm, w_proj_attn, conv_w, w_proj_conv, w_out, ffn2_norm, ffn2_w_gate, ffn2_w_up, ffn2_w_down, loss_target, m_ffn1_norm, m_ffn1_w_gate, m_ffn1_w_up, m_ffn1_w_down, m_mix_norm, m_w_in, m_gate_bias, m_q_a_norm, m_w_uq, m_kv_a_norm, m_w_uk, m_w_uv, m_q_head_norm, m_k_head_norm, m_w_proj_attn, m_conv_w, m_w_proj_conv, m_w_out, m_ffn2_norm, m_ffn2_w_gate, m_ffn2_w_up, m_ffn2_w_down, v_ffn1_norm, v_ffn1_w_gate, v_ffn1_w_up, v_ffn1_w_down, v_mix_norm, v_w_in, v_gate_bias, v_q_a_norm, v_w_uq, v_kv_a_norm, v_w_uk, v_w_uv, v_q_head_norm, v_k_head_norm, v_w_proj_attn, v_conv_w, v_w_proj_conv, v_w_out, v_ffn2_norm, v_ffn2_w_gate, v_ffn2_w_up, v_ffn2_w_down):
    args = dict(locals())
    view = lambda n, a: a.T if n in TRANSPOSED else a
    weights = {n: view(n, args[n]) for n in WEIGHT_ORDER}
    moments_m = {n: view(n, args["m_" + n]) for n in WEIGHT_ORDER}
    moments_v = {n: view(n, args["v_" + n]) for n in WEIGHT_ORDER}
    nb, seq, d = x.shape
    t = nb * seq
    chip = (2 * lax.axis_index("x") + lax.axis_index("y")).astype(jnp.int32)
    place = jnp.stack([chip, lax.axis_index("c").astype(jnp.int32)])
    grads, delta, new_m, new_v = {}, {}, {}, {}

    def adamw(names, carried=None):
        results = _adamw([(weights[n], grads[n], moments_m[n], moments_v[n]) for n in names], "adamw_" + names[0], carried)
        for n, (dn, mn, vn) in zip(names, results):
            delta[n], new_m[n], new_v[n] = dn, mn, vn

    conv_rows = conv_w.shape[0]
    conv_shard = jnp.pad(conv_w, ((0, 16 - conv_rows), (0, 0)))
    bufs = dict(zip(MATRICES + ("conv_w",), _cast_shards([weights[n] for n in MATRICES] + [conv_shard],
                                                         [BF16] * len(MATRICES) + [F32])))
    blocks = dict(zip(GROUP_FFN1, _run(_gather_carried([bufs[n] for n in GROUP_FFN1]), "gather_ffn1")))
    p = {n: _pad_lanes(weights[n], size) for n, size in SMALL}
    cos, sin = _rope_tables(positions)
    x_tok = x.reshape(t, d)

    gather_in = _gather_carried([bufs[n] for n in GROUP_IN])
    x1, gate1, up1, act1 = _ffn_fwd(x_tok, p["ffn1_norm"], blocks["ffn1_w_gate"], blocks["ffn1_w_up"], blocks["ffn1_w_down"], None,
                                    "ffn1_fwd", gather_in)
    blocks.update(zip(GROUP_IN, gather_in.results))
    w = _kernel_layouts({"w_in": blocks["w_in"].reshape(-1, d), **{n: _from_col_blocks(blocks[n]) for n in ("w_uq", "w_uk", "w_uv")}})
    p["conv_w"] = _from_col_blocks(blocks["conv_w"])[:conv_rows]

    gather_rest = _gather_carried([bufs[n] for n in GROUP_MIX + GROUP_FFN2])
    h2b, big, lat, q, k, v, vt = _inproj_fwd(x1, p["mix_norm"], w["w_m"], w["w_p"], p["q_a_norm"], p["kv_a_norm"], p["q_head_norm"],
                                             p["k_head_norm"], w["w_uq"], w["w_uk"], w["w_uv"], w["w_uvt"], cos, sin, gather_rest)
    blocks.update(zip(GROUP_MIX + GROUP_FFN2, gather_rest.results))
    w_pa = _from_col_blocks(blocks["w_proj_attn"])
    w_pc, w_out_full = blocks["w_proj_conv"].reshape(-1, d), blocks["w_out"].reshape(-1, d)
    wg2, wu2, wd2 = blocks["ffn2_w_gate"], blocks["ffn2_w_up"], blocks["ffn2_w_down"]

    o, lse = _attn_fwd(q, k, vt, seq)
    x2 = _mix_fwd(x1, o, big, p["gate_bias"], p["conv_w"], w_pa, w_pc, w_out_full, seq)
    dx3, gate2, up2, act2, loss = _ffn_fwd(x2, p["ffn2_norm"], wg2, wu2, wd2, loss_target.reshape(t, d), "ffn2_fwd")

    dx2, dg_ffn2, hb2, dgate2, dup2, dyb2 = _ffn_bwd_x(x2, p["ffn2_norm"], dx3, gate2, up2, wg2, wu2, wd2, "ffn2_bwd")
    g_ffn2 = [_tn_matmul(dgate2, hb2, "ffn2_dw_gate"), _tn_matmul(dup2, hb2, "ffn2_dw_up"), _tn_matmul(act2, dyb2, "ffn2_dw_down")]
    swap = _swap_carried(g_ffn2)
    do, delta_o, dz, dm, dbias, dw_pa, dw_pc, dw_out = _mix_bwd(dx2, o, big, p["gate_bias"], p["conv_w"], w_pa, w_pc, w_out_full, seq,
                                                                swap)
    part = _partials(GROUP_FFN2, g_ffn2, swap.results, place)
    send = _send_carried([pb for _, pb in part])
    dq, dk, dv = _attn_bwd(q, k, v, do, lse, delta_o.reshape(N_HEADS // ATTN_BWD_HEADS, ATTN_BWD_HEADS, -1), seq, send)
    join = _join_carried(_totals(GROUP_FFN2, g_ffn2, part, send.results, place))
    dp, dw_uq, dw_uk, dw_uv, dqa, dkva, dqh, dkh, dcw = _prep_bwd(
        lat, big, dz, dq, dk, dv, p["q_a_norm"], p["kv_a_norm"], p["q_head_norm"], p["k_head_norm"], w["w_uq"], w["w_uk"],
        w["w_uv"], cos, sin, p["conv_w"], seq, join)
    grads.update(zip(GROUP_FFN2, join.results))

    gg = _global_layouts({"w_m": _tn_matmul(dm, h2b, "dw_in_m", split_k=2), "w_p": _tn_matmul(dp, h2b, "dw_in_p", split_k=2),
                          "w_uq": dw_uq, "w_uk": dw_uk, "w_uv": dw_uv, "w_pa": dw_pa, "w_pc": dw_pc, "w_out": dw_out})
    g_mid = [_col_blocks(gg[n]) if n in COL_SHARDED else gg[n].reshape(N_CHIPS, -1, gg[n].shape[-1]) for n in GROUP_MID]
    swap = _swap_carried(g_mid)
    dx1, dg_mix = _inproj_bwd(x1, p["mix_norm"], dx2, dm, dp, w["w_m"], w["w_p"], swap)
    part = _partials(GROUP_MID, g_mid, swap.results, place)
    send = _send_carried([pb for _, pb in part])
    grad_x, dg_ffn1, hb1, dgate1, dup1, dyb1 = _ffn_bwd_x(x_tok, p["ffn1_norm"], dx1, gate1, up1, blocks["ffn1_w_gate"],
                                                         blocks["ffn1_w_up"], blocks["ffn1_w_down"], "ffn1_bwd", send)

    small_grads = {"ffn1_norm": dg_ffn1, "mix_norm": dg_mix, "gate_bias": dbias, "q_a_norm": dqa, "kv_a_norm": dkva,
                   "q_head_norm": dqh, "k_head_norm": dkh, "ffn2_norm": dg_ffn2}
    packed = jnp.concatenate([small_grads[n] for n, _ in SMALL] + [dcw.reshape(1, -1), loss], axis=1)
    total = _sum_devices(packed.reshape(8, -1)).reshape(1, -1)
    n_small = sum(size for _, size in SMALL)
    conv_cols = conv_w.shape[1]
    conv_total = total[:, n_small:n_small + conv_rows * d].reshape(conv_rows, d)
    grads["conv_w"] = lax.dynamic_slice_in_dim(conv_total, chip * conv_cols, conv_cols, axis=1)
    loss_total = total[0, n_small + conv_rows * d]

    join = _join_carried(_totals(GROUP_MID, g_mid, part, send.results, place))
    g_gate = _tn_matmul(dgate1, hb1, "ffn1_dw_gate", carried=join)
    grads.update(zip(GROUP_MID, join.results))
    swap_gate = _swap_carried([g_gate])
    g_up = _tn_matmul(dup1, hb1, "ffn1_dw_up", carried=swap_gate)
    part_gate = _partials(GROUP_FFN1[:1], [g_gate], swap_gate.results, place)
    send_gate, swap_up = _send_carried([part_gate[0][1]]), _swap_carried([g_up])
    g_down = _tn_matmul(act1, dyb1, "ffn1_dw_down", carried=_both(send_gate, swap_up))
    join_gate = _join_carried(_totals(GROUP_FFN1[:1], [g_gate], part_gate, send_gate.results, place))
    part_up = _partials(GROUP_FFN1[1:2], [g_up], swap_up.results, place)
    send_up, swap_down = _send_carried([part_up[0][1]]), _swap_carried([g_down])
    adamw(GROUP_FFN2, _both(_both(send_up, swap_down), join_gate))
    grads["ffn1_w_gate"] = join_gate.results[0]
    join_up = _join_carried(_totals(GROUP_FFN1[1:2], [g_up], part_up, send_up.results, place))
    part_down = _partials(GROUP_FFN1[2:], [g_down], swap_down.results, place)
    send_down = _send_carried([part_down[0][1]])
    adamw(("w_in",), _both(send_down, join_up))
    grads["ffn1_w_up"] = join_up.results[0]
    join_down = _join_carried(_totals(GROUP_FFN1[2:], [g_down], part_down, send_down.results, place))
    adamw(GROUP_FFN1[:2], join_down)
    grads["ffn1_w_down"] = join_down.results[0]
    adamw(GROUP_FFN1[2:])
    for n in GROUP_MID[1:] + ("conv_w",):
        adamw((n,))

    pack = lambda src: jnp.concatenate([_pad_lanes(src[n], size) for n, size in SMALL], axis=1)
    (sd, sm, sv), = _adamw([(pack(weights), total[:, :n_small], pack(moments_m), pack(moments_v))], "adamw_small")
    off = 0
    for n, size in SMALL:
        real = weights[n].size
        grads[n] = total[0, off:off + real]
        delta[n], new_m[n], new_v[n] = sd[0, off:off + real], sm[0, off:off + real], sv[0, off:off + real]
        off += size

    return (loss_total, grad_x.reshape(nb, seq, d), *[view(n, src[n]) for src in (grads, delta, new_m, new_v) for n in WEIGHT_ORDER])
```

```python
import functools

import jax
import jax.numpy as jnp
from jax import lax
from jax.experimental import pallas as pl
from jax.experimental.pallas import tpu as pltpu

F32 = jnp.float32
BF16 = jnp.bfloat16

D_MODEL = 1024
N_HEADS = 8
QK_NOPE = 64
QK_ROPE = 32
QK_DIM = QK_NOPE + QK_ROPE
V_DIM = 64
HEAD_PAD = 128
Q_LORA = 384
KV_LORA = 256
ROPE_THETA = 10000.0
NORM_EPS = 1e-6
ATTN_SCALE = QK_DIM ** -0.5
MASK_VALUE = -1e30
N_CHIPS = 4
N_DEV = 8

ADAM_LR = 0.001
ADAM_B1 = 0.9
ADAM_B2 = 0.999
ADAM_EPS = 1e-08
ADAM_WD = 0.01
ADAM_STEP = 10

TOKEN_TILE = 256
ATTN_TILE = 512
TN_TILE = 512
VMEM_LIMIT = 56 * 1024 * 1024

M_COLS = 3 * D_MODEL
P_COLS = 2 * D_MODEL + Q_LORA + KV_LORA + HEAD_PAD
BIG_COLS = 5 * D_MODEL
LAT_COLS = Q_LORA + KV_LORA + HEAD_PAD

MESH_ID = pl.DeviceIdType.MESH
ANY = pl.BlockSpec(memory_space=pl.ANY)


def _params(semantics=None):
    return pltpu.CompilerParams(dimension_semantics=semantics, vmem_limit_bytes=VMEM_LIMIT)


class _Carried:
    def __init__(self, operands, out_shapes, aliases, n_sems, start, finish):
        self.operands, self.out_shapes, self.aliases, self.n_sems = list(operands), list(out_shapes), dict(aliases), n_sems
        self.start, self.finish = start, finish
        self.results = None


def _both(a, b):
    na, nao = len(a.operands), len(a.out_shapes)

    def start(ins, outs, sems, base):
        a.start(ins[:na], outs[:nao], sems, base)
        b.start(ins[na:], outs[nao:], sems, base + a.n_sems)

    def finish(ins, outs, sems, base):
        a.finish(ins[:na], outs[:nao], sems, base)
        b.finish(ins[na:], outs[nao:], sems, base + a.n_sems)

    aliases = dict(a.aliases)
    aliases.update({na + i: nao + o for i, o in b.aliases.items()})
    both = _Carried(a.operands + b.operands, a.out_shapes + b.out_shapes, aliases, a.n_sems + b.n_sems, start, finish)
    both.parts = (a, b)
    return both


def _set_results(carried, results):
    carried.results = list(results)
    if hasattr(carried, "parts"):
        a, b = carried.parts
        _set_results(a, results[:len(a.out_shapes)])
        _set_results(b, results[len(a.out_shapes):])


def _pallas(body, name, grid, in_specs, out_specs, out_shape, args, semantics, carried=None):
    if carried is None:
        return pl.pallas_call(body, name=name, grid=grid, in_specs=in_specs, out_specs=out_specs, out_shape=out_shape,
                              compiler_params=_params(semantics))(*args)
    n_in, n_out, n_ci, n_co = len(in_specs), len(out_specs), len(carried.operands), len(carried.out_shapes)

    def wrapped(*refs):
        ins, c_ins = refs[:n_in], refs[n_in:n_in + n_ci]
        outs, c_outs = refs[n_in + n_ci:n_in + n_ci + n_out], refs[n_in + n_ci + n_out:n_in + n_ci + n_out + n_co]
        sems = refs[-1]
        first = pl.program_id(0) == 0
        last = pl.program_id(0) == grid[0] - 1
        for axis in range(1, len(grid)):
            first = jnp.logical_and(first, pl.program_id(axis) == 0)
            last = jnp.logical_and(last, pl.program_id(axis) == grid[axis] - 1)

        @pl.when(first)
        def _():
            carried.start(c_ins, c_outs, sems, 0)

        body(*ins, *outs)

        @pl.when(last)
        def _():
            carried.finish(c_ins, c_outs, sems, 0)

    results = pl.pallas_call(
        wrapped, name=name, grid=grid, in_specs=list(in_specs) + [ANY] * n_ci, out_specs=list(out_specs) + [ANY] * n_co,
        out_shape=list(out_shape) + carried.out_shapes,
        input_output_aliases={n_in + i: n_out + o for i, o in carried.aliases.items()},
        scratch_shapes=[pltpu.SemaphoreType.DMA((carried.n_sems,))], compiler_params=_params(semantics))(*args, *carried.operands)
    _set_results(carried, results[n_out:])
    return results[:n_out]


def _run(carried, name):
    n_ci, n_co = len(carried.operands), len(carried.out_shapes)

    def body(*refs):
        carried.start(refs[:n_ci], refs[n_ci:n_ci + n_co], refs[-1], 0)
        carried.finish(refs[:n_ci], refs[n_ci:n_ci + n_co], refs[-1], 0)

    results = pl.pallas_call(body, name=name, in_specs=[ANY] * n_ci, out_specs=[ANY] * n_co, out_shape=carried.out_shapes,
                             input_output_aliases=carried.aliases,
                             scratch_shapes=[pltpu.SemaphoreType.DMA((carried.n_sems,))])(*carried.operands)
    _set_results(carried, results)
    return carried.results


def _resident(shape):
    nd = len(shape)
    return pl.BlockSpec(shape, lambda *_: (0,) * nd, pipeline_mode=pl.Buffered(1))


def _const(shape):
    nd = len(shape)
    return pl.BlockSpec(shape, lambda *_: (0,) * nd)


def _mm(a, b):
    return jnp.dot(a, b, preferred_element_type=F32)


def _mm_nt(a, b):
    return lax.dot_general(a, b, (((1,), (1,)), ((), ())), preferred_element_type=F32)


def _mm_tn(a, b):
    return lax.dot_general(a, b, (((0,), (0,)), ((), ())), preferred_element_type=F32)


def _bf(a):
    return a.astype(BF16)


def _sigmoid(a):
    return 1.0 / (1.0 + jnp.exp(-a))


def _rms(x, gain, n=None):
    n = x.shape[-1] if n is None else n
    r = lax.rsqrt(jnp.sum(x * x, axis=-1, keepdims=True) * (1.0 / n) + NORM_EPS)
    return (x * r) * gain, r


def _rms_bwd(x, r, gain, dh, n=None):
    n = x.shape[-1] if n is None else n
    u = dh * gain
    dx = r * u - x * ((r * r * r) * (jnp.sum(u * x, axis=-1, keepdims=True) * (1.0 / n)))
    dgain = jnp.sum(dh * (x * r), axis=0, keepdims=True)
    return dx, dgain


def _rope_swap(t):
    lane = lax.broadcasted_iota(jnp.int32, t.shape, 1)
    lo = (lane >= QK_NOPE) & (lane < QK_NOPE + QK_ROPE // 2)
    hi = (lane >= QK_NOPE + QK_ROPE // 2) & (lane < QK_DIM)
    up = pltpu.roll(t, HEAD_PAD - QK_ROPE // 2, 1)
    down = pltpu.roll(t, QK_ROPE // 2, 1)
    return jnp.where(lo, up, jnp.where(hi, down, 0.0))


def _rope(t, cos, sin):
    return t * cos + _rope_swap(t) * sin


def _rope_bwd(dt, cos, sin):
    return dt * cos + _rope_swap(dt * sin)


def _shift_down(u, prev8, k):
    s = pltpu.roll(u, k, 0)
    p = pltpu.roll(prev8, k, 0)
    row = lax.broadcasted_iota(jnp.int32, prev8.shape, 0)
    top = jnp.where(row < k, p, s[:8])
    return jnp.concatenate([top, s[8:]], axis=0)


def _shift_up(d, next8, k):
    tm = d.shape[0]
    s = pltpu.roll(d, tm - k, 0)
    n = pltpu.roll(next8, 8 - k, 0)
    row = lax.broadcasted_iota(jnp.int32, next8.shape, 0)
    bot = jnp.where(row >= 8 - k, n, s[tm - 8:])
    return jnp.concatenate([s[:tm - 8], bot], axis=0)


def _ffn_fwd(x, gain, wg, wu, wd, target, name, carried=None):
    t, d = x.shape
    nb, f, _ = wg.shape
    tm = TOKEN_TILE
    with_loss = target is not None

    def body(*refs):
        if with_loss:
            x_ref, g_ref, wg_ref, wu_ref, wd_ref, t_ref, out_ref, gate_ref, up_ref, act_ref, loss_ref = refs
        else:
            x_ref, g_ref, wg_ref, wu_ref, wd_ref, out_ref, gate_ref, up_ref, act_ref = refs
        xv = x_ref[...]
        h, _ = _rms(xv, g_ref[...])
        hb = _bf(h)
        y = jnp.zeros((tm, d), F32)
        for j in range(nb):
            gate = _mm_nt(hb, wg_ref[j])
            up = _mm_nt(hb, wu_ref[j])
            act = _bf((gate * _sigmoid(gate)) * up)
            y = y + _mm(act, wd_ref[j])
            gate_ref[j] = _bf(gate)
            up_ref[j] = _bf(up)
            act_ref[j] = act
        out = xv + 0.5 * y
        if with_loss:
            err = out - t_ref[...]
            out_ref[...] = err * (1.0 / d)

            @pl.when(pl.program_id(0) == 0)
            def _():
                loss_ref[...] = jnp.zeros_like(loss_ref)

            part = jnp.sum(jnp.sum(err * err, axis=1, keepdims=True), axis=0, keepdims=True)
            loss_ref[...] += jnp.broadcast_to(part * (0.5 / d), loss_ref.shape)
        else:
            out_ref[...] = out

    tok = pl.BlockSpec((tm, d), lambda i: (i, 0))
    blk = pl.BlockSpec((nb, tm, f), lambda i: (0, i, 0))
    in_specs = [tok, _const((1, d)), _resident(wg.shape), _resident(wu.shape), _resident(wd.shape)]
    args = [x, gain, wg, wu, wd]
    out_shape = [jax.ShapeDtypeStruct((t, d), F32)] + [jax.ShapeDtypeStruct((nb, t, f), BF16)] * 3
    out_specs = [tok, blk, blk, blk]
    if with_loss:
        in_specs.append(tok)
        args.append(target)
        out_shape.append(jax.ShapeDtypeStruct((1, 128), F32))
        out_specs.append(_const((1, 128)))
    return _pallas(body, name, (t // tm,), in_specs, out_specs, out_shape, args, ("arbitrary",), carried)


def _ffn_bwd_x(x, gain, dout, gate, up, wg, wu, wd, name, carried=None):
    t, d = x.shape
    nb, f, _ = wg.shape
    tm = TOKEN_TILE

    def body(x_ref, g_ref, dout_ref, gate_ref, up_ref, wg_ref, wu_ref, wd_ref,
             dx_ref, dgain_ref, hb_ref, dgate_ref, dup_ref, dyb_ref):
        xv = x_ref[...]
        gain_v = g_ref[...]
        h, r = _rms(xv, gain_v)
        hb_ref[...] = _bf(h)
        dout_v = dout_ref[...]
        dyb = _bf(0.5 * dout_v)
        dyb_ref[...] = dyb
        dh = jnp.zeros((tm, d), F32)
        for j in range(nb):
            gt = gate_ref[j].astype(F32)
            uv = up_ref[j].astype(F32)
            s = _sigmoid(gt)
            dact = _mm_nt(dyb, wd_ref[j])
            dup = _bf(dact * (gt * s))
            dgate = _bf((dact * uv) * (s * (1.0 + gt * (1.0 - s))))
            dh = dh + _mm(dgate, wg_ref[j]) + _mm(dup, wu_ref[j])
            dgate_ref[j] = dgate
            dup_ref[j] = dup
        dxn, dgain = _rms_bwd(xv, r, gain_v, dh)
        dx_ref[...] = dout_v + dxn

        @pl.when(pl.program_id(0) == 0)
        def _():
            dgain_ref[...] = jnp.zeros_like(dgain_ref)

        dgain_ref[...] += dgain

    tok = pl.BlockSpec((tm, d), lambda i: (i, 0))
    blk = pl.BlockSpec((nb, tm, f), lambda i: (0, i, 0))
    return _pallas(
        body, name, (t // tm,),
        [tok, _const((1, d)), tok, blk, blk, _resident(wg.shape), _resident(wu.shape), _resident(wd.shape)],
        [tok, _const((1, d)), tok, blk, blk, tok],
        [jax.ShapeDtypeStruct((t, d), F32), jax.ShapeDtypeStruct((1, d), F32), jax.ShapeDtypeStruct((t, d), BF16),
         jax.ShapeDtypeStruct((nb, t, f), BF16), jax.ShapeDtypeStruct((nb, t, f), BF16), jax.ShapeDtypeStruct((t, d), BF16)],
        (x, gain, dout, gate, up, wg, wu, wd), ("arbitrary",), carried)


def _tn_matmul(a, b, name, split_k=1, carried=None):
    t = a.shape[-2]
    k = a.shape[-1]
    n = b.shape[-1]
    tt = min(TN_TILE, t)
    nt = t // tt

    def body(a_ref, b_ref, o_ref):
        @pl.when(pl.program_id(1) == 0)
        def _():
            o_ref[...] = jnp.zeros_like(o_ref)

        o_ref[...] += _mm_tn(a_ref[...], b_ref[...])

    if split_k > 1:
        assert a.ndim == 2 and b.ndim == 2 and k % (split_k * 128) == 0
        tk = k // split_k
        g = split_k
        a_spec = pl.BlockSpec((tt, tk), lambda gi, ti: (ti, gi))
        b_spec = pl.BlockSpec((tt, n), lambda gi, ti: (ti, 0))
        o_spec = pl.BlockSpec((tk, n), lambda gi, ti: (gi, 0))
        out_shape = jax.ShapeDtypeStruct((k, n), F32)
    else:
        g = a.shape[0] if a.ndim == 3 else b.shape[0]
        a_spec = (pl.BlockSpec((None, tt, k), lambda gi, ti: (gi, ti, 0)) if a.ndim == 3
                  else pl.BlockSpec((tt, k), lambda gi, ti: (ti, 0)))
        b_spec = (pl.BlockSpec((None, tt, n), lambda gi, ti: (gi, ti, 0)) if b.ndim == 3
                  else pl.BlockSpec((tt, n), lambda gi, ti: (ti, 0)))
        o_spec = pl.BlockSpec((None, k, n), lambda gi, ti: (gi, 0, 0))
        out_shape = jax.ShapeDtypeStruct((g, k, n), F32)
    return _pallas(body, name, (g, nt), [a_spec, b_spec], [o_spec], [out_shape], (a, b), ("arbitrary", "arbitrary"), carried)[0]


def _inproj_fwd(x1, gain, w_m, w_p, qa_gain, kva_gain, qh_gain, kh_gain, w_uq, w_uk, w_uv, w_uvt, cos, sin, carried=None):
    t, d = x1.shape
    tm = TOKEN_TILE

    def body(x_ref, g_ref, wm_ref, wp_ref, qa_ref, kva_ref, qh_ref, kh_ref, wuq_ref, wuk_ref, wuv_ref, wuvt_ref, cos_ref, sin_ref,
             hb_ref, big_ref, lat_ref, q_ref, k_ref, v_ref, vt_ref):
        h, _ = _rms(x_ref[...], g_ref[...])
        hb = _bf(h)
        hb_ref[...] = hb
        big_ref[:, :M_COLS] = _mm_nt(hb, wm_ref[...])
        pp = _mm_nt(hb, wp_ref[...])
        big_ref[:, M_COLS:] = pp[:, :2 * D_MODEL]
        lat = pp[:, 2 * D_MODEL:]
        lat_ref[...] = lat
        cq, _ = _rms(lat[:, :Q_LORA], qa_ref[...])
        ckv, _ = _rms(lat[:, Q_LORA:Q_LORA + KV_LORA], kva_ref[...])
        k_rope = lat[:, Q_LORA + KV_LORA:]
        cqb = _bf(cq)
        ckvb = _bf(ckv)
        cos_v = cos_ref[...]
        sin_v = sin_ref[...]
        q_all = _mm(cqb, wuq_ref[...])
        k_all = _mm(ckvb, wuk_ref[...])
        v_ref[...] = _bf(_mm(ckvb, wuv_ref[...]))
        vt_all = _mm_nt(wuvt_ref[...], ckvb)
        for hd in range(N_HEADS):
            lanes = slice(hd * HEAD_PAD, (hd + 1) * HEAD_PAD)
            qn, _ = _rms(q_all[:, lanes], qh_ref[...], QK_DIM)
            q_ref[hd] = _bf(_rope(qn, cos_v, sin_v))
            kn, _ = _rms(k_all[:, lanes] + k_rope, kh_ref[...], QK_DIM)
            k_ref[hd] = _bf(_rope(kn, cos_v, sin_v))
            vt_ref[hd] = _bf(vt_all[hd * V_DIM:(hd + 1) * V_DIM])

    tok = lambda c: pl.BlockSpec((tm, c), lambda i: (i, 0))
    head = lambda c: pl.BlockSpec((N_HEADS, tm, c), lambda i: (0, i, 0))
    return _pallas(
        body, "inproj_fwd", (t // tm,),
        [tok(d), _const((1, d)), _resident(w_m.shape), _resident(w_p.shape), _const((1, Q_LORA)), _const((1, KV_LORA)),
         _const((1, HEAD_PAD)), _const((1, HEAD_PAD)), _resident(w_uq.shape), _resident(w_uk.shape),
         _resident(w_uv.shape), _resident(w_uvt.shape), tok(HEAD_PAD), tok(HEAD_PAD)],
        [tok(d), tok(BIG_COLS), tok(LAT_COLS), head(HEAD_PAD), head(HEAD_PAD), tok(N_HEADS * V_DIM),
         pl.BlockSpec((N_HEADS, V_DIM, tm), lambda i: (0, 0, i))],
        [jax.ShapeDtypeStruct((t, d), BF16), jax.ShapeDtypeStruct((t, BIG_COLS), F32),
         jax.ShapeDtypeStruct((t, LAT_COLS), F32), jax.ShapeDtypeStruct((N_HEADS, t, HEAD_PAD), BF16),
         jax.ShapeDtypeStruct((N_HEADS, t, HEAD_PAD), BF16), jax.ShapeDtypeStruct((t, N_HEADS * V_DIM), BF16),
         jax.ShapeDtypeStruct((N_HEADS, V_DIM, t), BF16)],
        (x1, gain, w_m, w_p, qa_gain, kva_gain, qh_gain, kh_gain, w_uq, w_uk, w_uv, w_uvt, cos, sin), ("arbitrary",), carried)


EXP2_SCALE = ATTN_SCALE * 1.4426950408889634


def _diagonal_keep(tk, tq):
    return lax.broadcasted_iota(jnp.int32, (tk, tq), 0) <= lax.broadcasted_iota(jnp.int32, (tk, tq), 1)


def _attn_fwd(q, k, vt, seq, carried=None):
    _, t, _ = q.shape
    nseq = t // seq
    tq = tk = ATTN_TILE
    nq = seq // tq

    def body(q_ref, k_ref, vt_ref, o_ref, lse_ref):
        i = pl.program_id(1)
        qs = [q_ref[h] for h in range(N_HEADS)]
        keep = _diagonal_keep(tk, tq)

        def tile(h, state, k0, diagonal):
            m, l, acc = state
            st = _mm_nt(k_ref[h, pl.ds(k0, tk), :], qs[h])
            if diagonal:
                st = jnp.where(keep, st, MASK_VALUE)
            m_new = jnp.maximum(m, jnp.max(st, axis=0, keepdims=True))
            pt = jnp.exp2((st - m_new) * EXP2_SCALE)
            alpha = jnp.exp2((m - m_new) * EXP2_SCALE)
            l_new = alpha * l + jnp.sum(pt, axis=0, keepdims=True)
            return m_new, l_new, alpha * acc + _mm(vt_ref[h, :, pl.ds(k0, tk)], _bf(pt))

        def step(j, states):
            k0 = pl.multiple_of(j * tk, tk)
            return tuple(tile(h, states[h], k0, False) for h in range(N_HEADS))

        init = tuple((jnp.full((1, tq), MASK_VALUE, F32), jnp.zeros((1, tq), F32), jnp.zeros((V_DIM, tq), F32))
                     for _ in range(N_HEADS))
        states = lax.fori_loop(0, i, step, init)
        k0 = pl.multiple_of(i * tk, tk)
        outs = []
        for h in range(N_HEADS):
            m, l, acc = tile(h, states[h], k0, True)
            outs.append((acc / l).T)
            lse_ref[h] = m * EXP2_SCALE + jnp.log2(l)
        o_ref[...] = _bf(jnp.concatenate(outs, axis=-1))

    return _pallas(
        body, "attn_fwd", (nseq, nq),
        [pl.BlockSpec((N_HEADS, tq, HEAD_PAD), lambda b, i: (0, b * nq + i, 0)),
         pl.BlockSpec((N_HEADS, seq, HEAD_PAD), lambda b, i: (0, b, 0)),
         pl.BlockSpec((N_HEADS, V_DIM, seq), lambda b, i: (0, 0, b))],
        [pl.BlockSpec((tq, N_HEADS * V_DIM), lambda b, i: (b * nq + i, 0)),
         pl.BlockSpec((N_HEADS, 1, tq), lambda b, i: (0, 0, b * nq + i))],
        [jax.ShapeDtypeStruct((t, N_HEADS * V_DIM), BF16), jax.ShapeDtypeStruct((N_HEADS, 1, t), F32)],
        (q, k, vt), ("arbitrary", "arbitrary"), carried)


ATTN_BWD_HEADS = 4


def _attn_bwd(q, k, v, do, lse, delta, seq, carried=None):
    _, t, _ = q.shape
    nseq = t // seq
    tq = tk = ATTN_TILE
    n = seq // tq
    hb = ATTN_BWD_HEADS

    def body(q_ref, k_ref, v_ref, do_ref, lse_ref, delta_ref, dq_ref, dk_ref, dv_ref):
        dq_ref[...] = jnp.zeros_like(dq_ref)
        dk_ref[...] = jnp.zeros_like(dk_ref)
        dv_ref[...] = jnp.zeros_like(dv_ref)
        keep = _diagonal_keep(tk, tq)

        def tile(h, k0, q0, diagonal):
            kj = k_ref[h, pl.ds(k0, tk), :]
            qi = q_ref[h, pl.ds(q0, tq), :]
            doi = _bf(do_ref[pl.ds(q0, tq), h * V_DIM:(h + 1) * V_DIM])
            st = _mm_nt(kj, qi)
            if diagonal:
                st = jnp.where(keep, st, MASK_VALUE)
            pt = jnp.exp2(st * EXP2_SCALE - lse_ref[h, :, pl.ds(q0, tq)])
            dv_ref[pl.ds(k0, tk), h * V_DIM:(h + 1) * V_DIM] += _mm(_bf(pt), doi)
            dpt = _mm_nt(v_ref[pl.ds(k0, tk), h * V_DIM:(h + 1) * V_DIM], doi)
            dst = _bf((pt * (dpt - delta_ref[pl.ds(h, 1), pl.ds(q0, tq)])) * ATTN_SCALE)
            dk_ref[h, pl.ds(k0, tk), :] += _mm(dst, qi)
            dq_ref[h, pl.ds(q0, tq), :] += _mm_tn(dst, kj)

        def kv_step(j, _):
            k0 = pl.multiple_of(j * tk, tk)
            for h in range(hb):
                tile(h, k0, k0, True)

            def q_step(i, _):
                q0 = pl.multiple_of(i * tq, tq)
                for h in range(hb):
                    tile(h, k0, q0, False)
                return 0

            lax.fori_loop(j + 1, n, q_step, 0)
            return 0

        lax.fori_loop(0, n, kv_step, 0)

    hspec = lambda c: pl.BlockSpec((hb, seq, c), lambda b, g: (g, b, 0))
    cols = pl.BlockSpec((seq, hb * V_DIM), lambda b, g: (b, g))
    return _pallas(
        body, "attn_bwd", (nseq, N_HEADS // hb),
        [hspec(HEAD_PAD), hspec(HEAD_PAD), cols, cols,
         pl.BlockSpec((hb, 1, seq), lambda b, g: (g, 0, b)), pl.BlockSpec((None, hb, seq), lambda b, g: (g, 0, b))],
        [hspec(HEAD_PAD), hspec(HEAD_PAD), cols],
        [jax.ShapeDtypeStruct((N_HEADS, t, HEAD_PAD), F32), jax.ShapeDtypeStruct((N_HEADS, t, HEAD_PAD), F32),
         jax.ShapeDtypeStruct((t, N_HEADS * V_DIM), F32)],
        (q, k, v, do, lse, delta), ("arbitrary", "arbitrary"), carried)


def _mixer_values(o_ref, gb_ref, gla_ref, glb_ref, xc_ref, gc_ref, xcp_ref, gcp_ref, bias_ref, cw_ref, wpa_ref, wpc_ref,
                  first_of_seq):
    gb = gb_ref[...]
    u = gc_ref[...] * xc_ref[...]
    u_prev = jnp.where(first_of_seq, 0.0, gcp_ref[...] * xcp_ref[...])
    cw = cw_ref[...]
    z = cw[2:3] * u + cw[1:2] * _shift_down(u, u_prev, 1) + cw[0:1] * _shift_down(u, u_prev, 2)
    gbz = _bf(gb * z)
    y_b = _mm(gbz, wpc_ref[...])
    y_a = _mm(o_ref[...], wpa_ref[...])
    bias = bias_ref[...]
    gate_a = _sigmoid(gla_ref[...] + bias[:, :D_MODEL])
    gate_b = _sigmoid(glb_ref[...] + bias[:, D_MODEL:])
    merged = _bf(gate_a * y_a + gate_b * y_b)
    return gb, u, z, gbz, y_a, y_b, gate_a, gate_b, merged


def _mixer_specs(tm, seq):
    d = D_MODEL
    tok = pl.BlockSpec((tm, d), lambda i: (i, 0))
    col = lambda c: pl.BlockSpec((tm, d), lambda i: (i, c))
    prev = lambda c: pl.BlockSpec((8, d), lambda i: (jnp.maximum(i * (tm // 8) - 1, 0), c))
    o_spec = pl.BlockSpec((tm, N_HEADS * V_DIM), lambda i: (i, 0))
    fwd_specs = [o_spec, col(0), col(1), col(2), col(3), col(4), prev(3), prev(4), _const((1, 2 * d)), _const((3, d)),
                 _resident((N_HEADS * V_DIM, d)), _resident((d, d)), _resident((d, d))]
    return tok, fwd_specs


def _mix_fwd(x1, o, big, gate_bias, conv_w, w_pa, w_pc, w_out, seq, carried=None):
    t, d = x1.shape
    tm = TOKEN_TILE
    tiles_per_seq = seq // tm

    def body(x_ref, o_ref, gb_ref, gla_ref, glb_ref, xc_ref, gc_ref, xcp_ref, gcp_ref, bias_ref, cw_ref, wpa_ref, wpc_ref,
             wout_ref, x2_ref):
        first = pl.program_id(0) % tiles_per_seq == 0
        merged = _mixer_values(o_ref, gb_ref, gla_ref, glb_ref, xc_ref, gc_ref, xcp_ref, gcp_ref, bias_ref, cw_ref, wpa_ref,
                               wpc_ref, first)[-1]
        x2_ref[...] = x_ref[...] + _mm(merged, wout_ref[...])

    tok, fwd_specs = _mixer_specs(tm, seq)
    return _pallas(body, "mix_fwd", (t // tm,), [tok] + fwd_specs, [tok], [jax.ShapeDtypeStruct((t, d), F32)],
                   (x1, o, big, big, big, big, big, big, big, gate_bias, conv_w, w_pa, w_pc, w_out), ("arbitrary",), carried)[0]


def _mix_bwd(dx2, o, big, gate_bias, conv_w, w_pa, w_pc, w_out, seq, carried=None):
    t, d = dx2.shape
    tm = TOKEN_TILE
    tiles_per_seq = seq // tm
    hv = N_HEADS * V_DIM

    def body(dx_ref, o_ref, gb_ref, gla_ref, glb_ref, xc_ref, gc_ref, xcp_ref, gcp_ref, bias_ref, cw_ref, wpa_ref, wpc_ref,
             wout_ref, do_ref, delta_ref, dz_ref, dm_ref, dbias_ref, dwpa_ref, dwpc_ref, dwout_ref):
        first = pl.program_id(0) % tiles_per_seq == 0
        gb, _, z, gbz, y_a, y_b, gate_a, gate_b, merged = _mixer_values(
            o_ref, gb_ref, gla_ref, glb_ref, xc_ref, gc_ref, xcp_ref, gcp_ref, bias_ref, cw_ref, wpa_ref, wpc_ref, first)

        @pl.when(pl.program_id(0) == 0)
        def _():
            dbias_ref[...] = jnp.zeros_like(dbias_ref)
            dwpa_ref[...] = jnp.zeros_like(dwpa_ref)
            dwpc_ref[...] = jnp.zeros_like(dwpc_ref)
            dwout_ref[...] = jnp.zeros_like(dwout_ref)

        dxb = _bf(dx_ref[...])
        dmerged = _mm_nt(dxb, wout_ref[...])
        dwout_ref[...] += _mm_tn(merged, dxb)
        dla = (dmerged * y_a) * (gate_a * (1.0 - gate_a))
        dlb = (dmerged * y_b) * (gate_b * (1.0 - gate_b))
        dbias_ref[:, :d] += jnp.sum(dla, axis=0, keepdims=True)
        dbias_ref[:, d:] += jnp.sum(dlb, axis=0, keepdims=True)
        dya = _bf(dmerged * gate_a)
        dyb = _bf(dmerged * gate_b)
        do_v = _mm_nt(dya, wpa_ref[...])
        do_ref[...] = do_v
        head = lax.broadcasted_iota(jnp.int32, (N_HEADS, hv), 0) * V_DIM
        col = lax.broadcasted_iota(jnp.int32, (N_HEADS, hv), 1)
        in_head = ((col >= head) & (col < head + V_DIM)).astype(F32)
        delta_ref[...] = lax.dot_general(in_head, do_v * o_ref[...].astype(F32), (((1,), (1,)), ((), ())),
                                         precision=lax.Precision.HIGHEST, preferred_element_type=F32)
        dwpa_ref[...] += _mm_tn(o_ref[...], dya)
        dgz = _mm_nt(dyb, wpc_ref[...])
        dwpc_ref[...] += _mm_tn(gbz, dyb)
        dz_ref[...] = dgz * gb
        dm_ref[:, :d] = _bf(dgz * z)
        dm_ref[:, d:2 * d] = _bf(dla)
        dm_ref[:, 2 * d:] = _bf(dlb)

    tok, fwd_specs = _mixer_specs(tm, seq)
    return _pallas(
        body, "mix_bwd", (t // tm,), [tok] + fwd_specs,
        [pl.BlockSpec((tm, hv), lambda i: (i, 0)), pl.BlockSpec((N_HEADS, tm), lambda i: (0, i)), tok,
         pl.BlockSpec((tm, M_COLS), lambda i: (i, 0)), _const((1, 2 * d)), _const((hv, d)), _const((d, d)), _const((d, d))],
        [jax.ShapeDtypeStruct((t, hv), F32), jax.ShapeDtypeStruct((N_HEADS, t), F32), jax.ShapeDtypeStruct((t, d), F32),
         jax.ShapeDtypeStruct((t, M_COLS), BF16), jax.ShapeDtypeStruct((1, 2 * d), F32), jax.ShapeDtypeStruct((hv, d), F32),
         jax.ShapeDtypeStruct((d, d), F32), jax.ShapeDtypeStruct((d, d), F32)],
        (dx2, o, big, big, big, big, big, big, big, gate_bias, conv_w, w_pa, w_pc, w_out), ("arbitrary",), carried)


def _prep_bwd(lat, big, dz, dq, dk, dv, qa_gain, kva_gain, qh_gain, kh_gain, w_uq, w_uk, w_uv, cos, sin, conv_w, seq, carried=None):
    t = lat.shape[0]
    d = D_MODEL
    tm = TOKEN_TILE
    tiles_per_seq = seq // tm
    last_blk = t // 8 - 1

    def body(lat_ref, xc_ref, gc_ref, dz_ref, dzn_ref, dq_ref, dk_ref, dv_ref, qa_ref, kva_ref, qh_ref, kh_ref, wuq_ref, wuk_ref,
             wuv_ref, cos_ref, sin_ref, cw_ref,
             dp_ref, dwuq_ref, dwuk_ref, dwuv_ref, dqa_ref, dkva_ref, dqh_ref, dkh_ref, dcw_ref):
        pid = pl.program_id(0)

        @pl.when(pid == 0)
        def _():
            for r in (dwuq_ref, dwuk_ref, dwuv_ref, dqa_ref, dkva_ref, dqh_ref, dkh_ref, dcw_ref):
                r[...] = jnp.zeros_like(r)

        last = pid % tiles_per_seq == tiles_per_seq - 1
        dzv = dz_ref[...]
        dz_next = jnp.where(last, 0.0, dzn_ref[...])
        dz1 = _shift_up(dzv, dz_next, 1)
        dz2 = _shift_up(dzv, dz_next, 2)
        cw = cw_ref[...]
        xc = xc_ref[...]
        gc = gc_ref[...]
        u = gc * xc
        du = cw[2:3] * dzv + cw[1:2] * dz1 + cw[0:1] * dz2
        dp_ref[:, :d] = _bf(du * gc)
        dp_ref[:, d:2 * d] = _bf(du * xc)
        dcw_ref[0:1, :] += jnp.sum(dz2 * u, axis=0, keepdims=True)
        dcw_ref[1:2, :] += jnp.sum(dz1 * u, axis=0, keepdims=True)
        dcw_ref[2:3, :] += jnp.sum(dzv * u, axis=0, keepdims=True)

        lat_v = lat_ref[...]
        q_lat = lat_v[:, :Q_LORA]
        kv_lat = lat_v[:, Q_LORA:Q_LORA + KV_LORA]
        k_rope = lat_v[:, Q_LORA + KV_LORA:]
        qa_gain_v = qa_ref[...]
        kva_gain_v = kva_ref[...]
        qh_gain_v = qh_ref[...]
        kh_gain_v = kh_ref[...]
        cq, rq = _rms(q_lat, qa_gain_v)
        ckv, rkv = _rms(kv_lat, kva_gain_v)
        cqb = _bf(cq)
        ckvb = _bf(ckv)
        cos_v = cos_ref[...]
        sin_v = sin_ref[...]
        lane = lax.broadcasted_iota(jnp.int32, (tm, HEAD_PAD), 1)
        rope_lanes = (lane >= QK_NOPE) & (lane < QK_DIM)
        dk_rope = jnp.zeros((tm, HEAD_PAD), F32)
        dqh_gain = jnp.zeros((1, HEAD_PAD), F32)
        dkh_gain = jnp.zeros((1, HEAD_PAD), F32)
        q_all = _mm(cqb, wuq_ref[...])
        k_all = _mm(ckvb, wuk_ref[...])
        dq_heads, dk_heads = [], []
        for hd in range(N_HEADS):
            lanes = slice(hd * HEAD_PAD, (hd + 1) * HEAD_PAD)
            q_pre = q_all[:, lanes]
            _, rr = _rms(q_pre, qh_gain_v, QK_DIM)
            dq_pre, dg = _rms_bwd(q_pre, rr, qh_gain_v, _rope_bwd(dq_ref[hd], cos_v, sin_v), QK_DIM)
            dqh_gain = dqh_gain + dg
            dq_heads.append(_bf(dq_pre))

            k_pre = k_all[:, lanes] + k_rope
            _, rr = _rms(k_pre, kh_gain_v, QK_DIM)
            dk_pre, dg = _rms_bwd(k_pre, rr, kh_gain_v, _rope_bwd(dk_ref[hd], cos_v, sin_v), QK_DIM)
            dkh_gain = dkh_gain + dg
            dk_rope = dk_rope + jnp.where(rope_lanes, dk_pre, 0.0)
            dk_heads.append(_bf(dk_pre))
        dq_all = jnp.concatenate(dq_heads, axis=1)
        dk_all = jnp.concatenate(dk_heads, axis=1)
        dvb = _bf(dv_ref[...])
        dcq = _mm_nt(dq_all, wuq_ref[...])
        dckv = _mm_nt(dk_all, wuk_ref[...]) + _mm_nt(dvb, wuv_ref[...])
        dwuq_ref[...] += _mm_tn(cqb, dq_all)
        dwuk_ref[...] += _mm_tn(ckvb, dk_all)
        dwuv_ref[...] += _mm_tn(ckvb, dvb)
        dqh_ref[...] += dqh_gain
        dkh_ref[...] += dkh_gain
        dq_lat, dg = _rms_bwd(q_lat, rq, qa_gain_v, dcq)
        dqa_ref[...] += dg
        dkv_lat, dg = _rms_bwd(kv_lat, rkv, kva_gain_v, dckv)
        dkva_ref[...] += dg
        dp_ref[:, 2 * d:2 * d + Q_LORA] = _bf(dq_lat)
        dp_ref[:, 2 * d + Q_LORA:2 * d + Q_LORA + KV_LORA] = _bf(dkv_lat)
        dp_ref[:, 2 * d + Q_LORA + KV_LORA:] = _bf(dk_rope)

    tok = lambda c: pl.BlockSpec((tm, c), lambda i: (i, 0))
    col = lambda c: pl.BlockSpec((tm, d), lambda i: (i, c))
    head = lambda c: pl.BlockSpec((N_HEADS, tm, c), lambda i: (0, i, 0))
    nxt = pl.BlockSpec((8, d), lambda i: (jnp.minimum((i + 1) * (tm // 8), last_blk), 0))
    return _pallas(
        body, "prep_bwd", (t // tm,),
        [tok(LAT_COLS), col(3), col(4), tok(d), nxt, head(HEAD_PAD), head(HEAD_PAD), tok(N_HEADS * V_DIM),
         _const((1, Q_LORA)), _const((1, KV_LORA)), _const((1, HEAD_PAD)), _const((1, HEAD_PAD)),
         _resident(w_uq.shape), _resident(w_uk.shape), _resident(w_uv.shape), tok(HEAD_PAD), tok(HEAD_PAD), _const((3, d))],
        [tok(P_COLS), _const(w_uq.shape), _const(w_uk.shape), _const(w_uv.shape), _const((1, Q_LORA)),
         _const((1, KV_LORA)), _const((1, HEAD_PAD)), _const((1, HEAD_PAD)), _const((3, d))],
        [jax.ShapeDtypeStruct((t, P_COLS), BF16), jax.ShapeDtypeStruct(w_uq.shape, F32),
         jax.ShapeDtypeStruct(w_uk.shape, F32), jax.ShapeDtypeStruct(w_uv.shape, F32),
         jax.ShapeDtypeStruct((1, Q_LORA), F32), jax.ShapeDtypeStruct((1, KV_LORA), F32),
         jax.ShapeDtypeStruct((1, HEAD_PAD), F32), jax.ShapeDtypeStruct((1, HEAD_PAD), F32), jax.ShapeDtypeStruct((3, d), F32)],
        (lat, big, big, dz, dz, dq, dk, dv, qa_gain, kva_gain, qh_gain, kh_gain, w_uq, w_uk, w_uv, cos, sin, conv_w),
        ("arbitrary",), carried)


def _inproj_bwd(x1, gain, dx2, dm, dp, w_m, w_p, carried=None):
    t, d = x1.shape
    tm = TOKEN_TILE

    def body(x_ref, g_ref, dx2_ref, dm_ref, dp_ref, wm_ref, wp_ref, dx1_ref, dgain_ref):
        xv = x_ref[...]
        gain_v = g_ref[...]
        _, r = _rms(xv, gain_v)
        dh = _mm(dm_ref[...], wm_ref[...]) + _mm(dp_ref[...], wp_ref[...])
        dxn, dgain = _rms_bwd(xv, r, gain_v, dh)
        dx1_ref[...] = dx2_ref[...] + dxn

        @pl.when(pl.program_id(0) == 0)
        def _():
            dgain_ref[...] = jnp.zeros_like(dgain_ref)

        dgain_ref[...] += dgain

    tok = lambda c: pl.BlockSpec((tm, c), lambda i: (i, 0))
    return _pallas(
        body, "inproj_bwd", (t // tm,),
        [tok(d), _const((1, d)), tok(d), tok(M_COLS), tok(P_COLS), _resident(w_m.shape), _resident(w_p.shape)],
        [tok(d), _const((1, d))], [jax.ShapeDtypeStruct((t, d), F32), jax.ShapeDtypeStruct((1, d), F32)],
        (x1, gain, dx2, dm, dp, w_m, w_p), ("arbitrary",), carried)


def _adamw(quads, name, carried=None):
    k = len(quads)
    rows, cols = quads[0][0].shape
    tr, tc = rows, cols
    for cand in (512, 352, 256, 192, 128, 64):
        if rows % cand == 0 and rows > cand:
            tr = cand
            break
    if tr == rows and rows * cols > 512 * 1024 and cols % 256 == 0:
        tc = 256
    while k * 14 * tr * tc * 4 > VMEM_LIMIT // 2 and tr % 16 == 0:
        tr //= 2

    def body(*refs):
        for i in range(k):
            w_ref, g_ref, m_ref, v_ref = refs[4 * i:4 * i + 4]
            delta_ref, nm_ref, nv_ref = refs[4 * k + 3 * i:4 * k + 3 * i + 3]
            gv = g_ref[...]
            nm = ADAM_B1 * m_ref[...] + (1.0 - ADAM_B1) * gv
            nv = ADAM_B2 * v_ref[...] + (1.0 - ADAM_B2) * (gv * gv)
            m_hat = nm * (1.0 / (1.0 - ADAM_B1 ** ADAM_STEP))
            v_hat = nv * (1.0 / (1.0 - ADAM_B2 ** ADAM_STEP))
            delta_ref[...] = -ADAM_LR * (m_hat / (jnp.sqrt(v_hat) + ADAM_EPS) + ADAM_WD * w_ref[...])
            nm_ref[...] = nm
            nv_ref[...] = nv

    spec = pl.BlockSpec((tr, tc), lambda i, j: (i, j))
    shape = jax.ShapeDtypeStruct((rows, cols), F32)
    outs = _pallas(body, name, (rows // tr, cols // tc), [spec] * (4 * k), [spec] * (3 * k), [shape] * (3 * k),
                   [a for quad in quads for a in quad], ("arbitrary", "arbitrary"), carried)
    return [tuple(outs[3 * i:3 * i + 3]) for i in range(k)]


def _adamw_small(packed_grads, triples, segments):
    k = len(triples)

    def body(*refs):
        g_ref = refs[0]
        off = 0
        for i in range(k):
            w_ref, m_ref, v_ref = refs[1 + 3 * i:4 + 3 * i]
            g_out, delta_ref, nm_ref, nv_ref = refs[1 + 3 * k + 4 * i:5 + 3 * k + 4 * i]
            gv = g_ref[:, off:off + w_ref.shape[1]]
            nm = ADAM_B1 * m_ref[...] + (1.0 - ADAM_B1) * gv
            nv = ADAM_B2 * v_ref[...] + (1.0 - ADAM_B2) * (gv * gv)
            m_hat = nm * (1.0 / (1.0 - ADAM_B1 ** ADAM_STEP))
            v_hat = nv * (1.0 / (1.0 - ADAM_B2 ** ADAM_STEP))
            g_out[...] = gv
            delta_ref[...] = -ADAM_LR * (m_hat / (jnp.sqrt(v_hat) + ADAM_EPS) + ADAM_WD * w_ref[...])
            nm_ref[...] = nm
            nv_ref[...] = nv
            off += segments[i]

    vm = pl.BlockSpec(memory_space=pltpu.VMEM)
    outs = pl.pallas_call(
        body, name="adamw_small", in_specs=[vm] * (1 + 3 * k), out_specs=[vm] * (4 * k),
        out_shape=[jax.ShapeDtypeStruct(w.shape, F32) for w, _, _ in triples for _ in range(4)],
    )(packed_grads, *[a for triple in triples for a in triple])
    return [tuple(outs[4 * i:4 * i + 4]) for i in range(k)]


def _place():
    x, y, c = lax.axis_index("x"), lax.axis_index("y"), lax.axis_index("c")
    other_chips = [(1 - x, y), (x, 1 - y), (1 - x, 1 - y)]
    return x, y, c, other_chips


def _remote(src, dst, sems, send, recv, device):
    return pltpu.make_async_remote_copy(src_ref=src, dst_ref=dst, send_sem=sems.at[send], recv_sem=sems.at[recv],
                                        device_id=device, device_id_type=MESH_ID)


def _cast_shards(shards, out_dtypes):
    n = len(shards)

    def body(*refs):
        ins, outs, stage, sems = refs[:n], refs[n:2 * n], refs[2 * n:3 * n], refs[3 * n]
        x, y, _, _ = _place()
        me = 2 * x + y
        copies = []
        for w in range(n):
            stage[w][...] = ins[w][...].astype(out_dtypes[w])
            copies.append(pltpu.make_async_copy(stage[w], outs[w].at[me], sems.at[w]))
            copies[-1].start()
        for cp in copies:
            cp.wait()

    vm = pl.BlockSpec(memory_space=pltpu.VMEM)
    return pl.pallas_call(
        body, name="cast_shards", in_specs=[vm] * n, out_specs=[ANY] * n,
        out_shape=[jax.ShapeDtypeStruct((N_CHIPS,) + s.shape, dt) for s, dt in zip(shards, out_dtypes)],
        scratch_shapes=[pltpu.VMEM(s.shape, dt) for s, dt in zip(shards, out_dtypes)] + [pltpu.SemaphoreType.DMA((n,))],
        compiler_params=_params())(*shards)


BF16_ROWS = 16


def _split_rows(rows):
    return (rows // 2) % BF16_ROWS == 0


def _half_shape(rows, cols):
    return (rows // 2, cols) if _split_rows(rows) else (rows, cols // 2)


def _half(rows, cols, which):
    if _split_rows(rows):
        return (pl.ds(pl.multiple_of(which * (rows // 2), BF16_ROWS), rows // 2), slice(None))
    return (slice(None), pl.ds(pl.multiple_of(which * (cols // 2), 128), cols // 2))


def _gather_carried(bufs):
    n = len(bufs)

    def half(w, slot, which):
        _, rows, cols = bufs[w].shape
        return (slot,) + _half(rows, cols, which)

    def start(ins, outs, sems, base):
        x, y, c, other_chips = _place()
        me = 2 * x + y
        for w in range(n):
            mine = outs[w].at[half(w, me, c)]
            for p, (px, py) in enumerate(other_chips):
                _remote(mine, mine, sems, base + 12 * w + p, base + 12 * w + 3 + p, (px, py, c)).start()

    def finish(ins, outs, sems, base):
        x, y, c, other_chips = _place()
        me = 2 * x + y
        for w in range(n):
            for p, (px, py) in enumerate(other_chips):
                got = outs[w].at[half(w, 2 * px + py, c)]
                _remote(got, got, sems, base + 12 * w + p, base + 12 * w + 3 + p, (px, py, c)).wait_recv()
                _remote(got, got, sems, base + 12 * w + 6 + p, base + 12 * w + 9 + p, (x, y, 1 - c)).start()
        for w in range(n):
            mine = outs[w].at[half(w, me, c)]
            for p, (px, py) in enumerate(other_chips):
                got = outs[w].at[half(w, 2 * px + py, c)]
                theirs = outs[w].at[half(w, 2 * px + py, 1 - c)]
                _remote(got, theirs, sems, base + 12 * w + 6 + p, base + 12 * w + 9 + p, (x, y, 1 - c)).wait()
                _remote(mine, mine, sems, base + 12 * w + p, base + 12 * w + 3 + p, (px, py, c)).wait_send()

    shapes = [jax.ShapeDtypeStruct(b.shape, b.dtype) for b in bufs]
    return _Carried(bufs, shapes, {w: w for w in range(n)}, 12 * n, start, finish)


def _swap_carried(grads):
    n = len(grads)

    def copy(w, ins, outs, sems, base):
        x, y, c, _ = _place()
        _, rows, cols = grads[w].shape
        theirs = ins[w].at[(slice(None),) + _half(rows, cols, 1 - c)]
        return _remote(theirs, outs[w], sems, base + 2 * w, base + 2 * w + 1, (x, y, 1 - c))

    def start(ins, outs, sems, base):
        for w in range(n):
            copy(w, ins, outs, sems, base).start()

    def finish(ins, outs, sems, base):
        for w in range(n):
            copy(w, ins, outs, sems, base).wait()

    shapes = [jax.ShapeDtypeStruct((g.shape[0],) + _half_shape(*g.shape[1:]), F32) for g in grads]
    return _Carried(grads, shapes, {}, 2 * n, start, finish)


def _row_tile(rows):
    for cand in (512, 352, 256, 192, 128, 96, 64, 32, 16):
        if rows % cand == 0:
            return cand
    return rows


def _half_block_index(split_rows, tiles, i, core):
    return (core * tiles + i, 0) if split_rows else (i, core)


def _chip_partial(grad, other, place, name):
    nblk, hr, hc = other.shape
    by_rows = _split_rows(grad.shape[1])
    tr = _row_tile(hr)
    tiles = hr // tr

    def body(place_ref, g_ref, o_ref, sum_ref, sum_bf_ref):
        s = g_ref[...] + o_ref[...]
        sum_ref[...] = s
        sum_bf_ref[...] = _bf(s)

    grid_spec = pltpu.PrefetchScalarGridSpec(
        num_scalar_prefetch=1, grid=(nblk, tiles),
        in_specs=[pl.BlockSpec((None, tr, hc), lambda b, i, place_ref: (b,) + _half_block_index(by_rows, tiles, i, place_ref[1])),
                  pl.BlockSpec((None, tr, hc), lambda b, i, place_ref: (b, i, 0))],
        out_specs=[pl.BlockSpec((None, tr, hc), lambda b, i, place_ref: (b, i, 0))] * 2)
    return pl.pallas_call(body, name=name, grid_spec=grid_spec,
                          out_shape=[jax.ShapeDtypeStruct(other.shape, F32), jax.ShapeDtypeStruct(other.shape, BF16)],
                          compiler_params=_params(("arbitrary", "arbitrary")))(place, grad, other)


def _send_carried(partials):
    n = len(partials)

    def start(ins, outs, sems, base):
        x, y, c, other_chips = _place()
        me = 2 * x + y
        for w in range(n):
            for p, (px, py) in enumerate(other_chips):
                _remote(ins[w].at[2 * px + py], outs[w].at[me], sems, base + 6 * w + p, base + 6 * w + 3 + p, (px, py, c)).start()

    def finish(ins, outs, sems, base):
        x, y, c, other_chips = _place()
        for w in range(n):
            for p, (px, py) in enumerate(other_chips):
                _remote(ins[w].at[2 * px + py], outs[w].at[2 * px + py], sems, base + 6 * w + p, base + 6 * w + 3 + p,
                        (px, py, c)).wait()

    return _Carried(partials, [jax.ShapeDtypeStruct(p.shape, BF16) for p in partials], {}, 6 * n, start, finish)


def _chip_total(own, received, place, shape, name):
    nblk, hr, hc = own.shape
    by_rows = _split_rows(shape[0])
    tr = _row_tile(hr)
    tiles = hr // tr

    def body(place_ref, own_ref, r1_ref, r2_ref, r3_ref, out_ref):
        out_ref[...] = own_ref[...] + ((r1_ref[...].astype(F32) + r2_ref[...].astype(F32)) + r3_ref[...].astype(F32))

    def slot(k):
        return pl.BlockSpec((None, tr, hc), lambda i, place_ref: ((place_ref[0] + k) % N_CHIPS, i, 0))

    grid_spec = pltpu.PrefetchScalarGridSpec(
        num_scalar_prefetch=1, grid=(tiles,), in_specs=[slot(0), slot(1), slot(2), slot(3)],
        out_specs=pl.BlockSpec((tr, hc), lambda i, place_ref: _half_block_index(by_rows, tiles, i, place_ref[1])))
    return pl.pallas_call(body, name=name, grid_spec=grid_spec, out_shape=jax.ShapeDtypeStruct(tuple(shape), F32),
                          compiler_params=_params(("arbitrary",)))(place, own, received, received, received)


def _join_carried(totals):
    n = len(totals)

    def copy(w, outs, sems, base):
        x, y, c, _ = _place()
        mine = outs[w].at[_half(*totals[w].shape, c)]
        return _remote(mine, mine, sems, base + 2 * w, base + 2 * w + 1, (x, y, 1 - c))

    def start(ins, outs, sems, base):
        for w in range(n):
            copy(w, outs, sems, base).start()

    def finish(ins, outs, sems, base):
        for w in range(n):
            copy(w, outs, sems, base).wait()

    shapes = [jax.ShapeDtypeStruct(a.shape, F32) for a in totals]
    return _Carried(totals, shapes, {w: w for w in range(n)}, 2 * n, start, finish)


def _sum_devices(vec):
    rows, n = vec.shape

    def body(v_ref, out_ref, buf, send_sems, recv_sems):
        x, y, c, _ = _place()
        me = 4 * x + 2 * y + c
        buf[me] = v_ref[...]
        sends = []
        for k in range(1, N_DEV):
            peer = (1 - x if k & 4 else x, 1 - y if k & 2 else y, 1 - c if k & 1 else c)
            cp = pltpu.make_async_remote_copy(src_ref=v_ref, dst_ref=buf.at[me], send_sem=send_sems.at[k], recv_sem=recv_sems.at[k],
                                              device_id=peer, device_id_type=MESH_ID)
            cp.start()
            sends.append(cp)
        for cp in sends:
            cp.wait()
        total = buf[0]
        for dev in range(1, N_DEV):
            total = total + buf[dev]
        out_ref[...] = total

    vm = pl.BlockSpec(memory_space=pltpu.VMEM)
    return pl.pallas_call(
        body, name="sum_devices", in_specs=[vm], out_specs=vm, out_shape=jax.ShapeDtypeStruct((rows, n), F32),
        scratch_shapes=[pltpu.VMEM((N_DEV, rows, n), F32), pltpu.SemaphoreType.DMA((N_DEV,)), pltpu.SemaphoreType.DMA((N_DEV,))],
    )(vec)


def _rope_tables(positions):
    half = QK_ROPE // 2
    inv_freq = 1.0 / (ROPE_THETA ** (jnp.arange(half, dtype=F32) / half))
    ang = positions.astype(F32).reshape(-1, 1) * inv_freq
    cos, sin = jnp.cos(ang), jnp.sin(ang)
    t = ang.shape[0]
    ones, zeros = jnp.ones((t, QK_NOPE), F32), jnp.zeros((t, QK_NOPE), F32)
    pad = HEAD_PAD - QK_DIM
    cos_full = jnp.concatenate([ones, cos, cos, ones[:, :pad]], axis=1)
    sin_signed = jnp.concatenate([zeros, -sin, sin, zeros[:, :pad]], axis=1)
    return cos_full, sin_signed


def _partials(names, grads, from_sibling, place):
    return [_chip_partial(g, o, place, "chip_partial_" + n) for n, g, o in zip(names, grads, from_sibling)]


def _totals(names, grads, partials, received, place):
    return [_chip_total(pf, r, place, g.shape[1:], "chip_total_" + n) for n, g, (pf, _), r in zip(names, grads, partials, received)]


def _kernel_layouts(full):
    d = D_MODEL
    w_in = full["w_in"]
    o_kr = Q_LORA + KV_LORA
    o_xc = o_kr + QK_ROPE
    o_gb = o_xc + d
    o_gc = o_gb + d
    o_gl = o_gc + d
    k_rope_pad = jnp.pad(w_in[o_kr:o_xc], ((QK_NOPE, HEAD_PAD - QK_DIM), (0, 0)))
    w_m = jnp.concatenate([w_in[o_gb:o_gc], w_in[o_gl:]], axis=0)
    w_p = jnp.concatenate([w_in[o_xc:o_gb], w_in[o_gc:o_gl], w_in[:o_kr], k_rope_pad], axis=0)
    w_uq = jnp.pad(full["w_uq"].reshape(Q_LORA, N_HEADS, QK_DIM), ((0, 0), (0, 0), (0, HEAD_PAD - QK_DIM)))
    w_uk = jnp.pad(full["w_uk"].reshape(KV_LORA, N_HEADS, QK_NOPE), ((0, 0), (0, 0), (0, HEAD_PAD - QK_NOPE)))
    return {"w_m": w_m, "w_p": w_p, "w_uq": w_uq.reshape(Q_LORA, N_HEADS * HEAD_PAD), "w_uk": w_uk.reshape(KV_LORA, N_HEADS * HEAD_PAD),
            "w_uv": full["w_uv"], "w_uvt": full["w_uv"].T}


def _global_layouts(g):
    d = D_MODEL
    dm, dp = g["w_m"], g["w_p"]
    o_lat = 2 * d
    o_kr = o_lat + Q_LORA + KV_LORA + QK_NOPE
    w_in = jnp.concatenate([dp[o_lat:o_lat + Q_LORA + KV_LORA], dp[o_kr:o_kr + QK_ROPE], dp[:d], dm[:d], dp[d:o_lat], dm[d:]], axis=0)
    w_uq = g["w_uq"].reshape(Q_LORA, N_HEADS, HEAD_PAD)[:, :, :QK_DIM].reshape(Q_LORA, N_HEADS * QK_DIM)
    w_uk = g["w_uk"].reshape(KV_LORA, N_HEADS, HEAD_PAD)[:, :, :QK_NOPE].reshape(KV_LORA, N_HEADS * QK_NOPE)
    return {"w_in": w_in, "w_uq": w_uq, "w_uk": w_uk, "w_uv": g["w_uv"], "w_proj_attn": g["w_pa"], "w_proj_conv": g["w_pc"],
            "w_out": g["w_out"]}


def _col_blocks(a):
    r, c = a.shape
    return a.reshape(r, N_CHIPS, c // N_CHIPS).transpose(1, 0, 2)


def _from_col_blocks(a):
    n, r, c = a.shape
    return a.transpose(1, 0, 2).reshape(r, n * c)


COL_SHARDED = ("w_uq", "w_uk", "w_uv", "w_proj_attn")
TRANSPOSED = ("ffn1_w_gate", "ffn1_w_up", "ffn2_w_gate", "ffn2_w_up", "w_in")
SMALL = (("ffn1_norm", 1024), ("mix_norm", 1024), ("gate_bias", 2048), ("q_a_norm", 384), ("kv_a_norm", 256),
         ("q_head_norm", 128), ("k_head_norm", 128), ("ffn2_norm", 1024))
WEIGHT_ORDER = ("ffn1_norm", "ffn1_w_gate", "ffn1_w_up", "ffn1_w_down", "mix_norm", "w_in", "gate_bias", "q_a_norm", "w_uq",
                "kv_a_norm", "w_uk", "w_uv", "q_head_norm", "k_head_norm", "w_proj_attn", "conv_w", "w_proj_conv", "w_out",
                "ffn2_norm", "ffn2_w_gate", "ffn2_w_up", "ffn2_w_down")
MATRICES = ("ffn1_w_gate", "ffn1_w_up", "ffn1_w_down", "w_in", "w_uq", "w_uk", "w_uv", "w_proj_attn", "w_proj_conv", "w_out",
            "ffn2_w_gate", "ffn2_w_up", "ffn2_w_down")
GROUP_FFN1 = ("ffn1_w_gate", "ffn1_w_up", "ffn1_w_down")
GROUP_IN = ("w_in", "w_uq", "w_uk", "w_uv", "conv_w")
GROUP_MIX = ("w_proj_attn", "w_proj_conv", "w_out")
GROUP_FFN2 = ("ffn2_w_gate", "ffn2_w_up", "ffn2_w_down")
GROUP_MID = ("w_in", "w_uq", "w_uk", "w_uv", "w_proj_attn", "w_proj_conv", "w_out")


def _pad_lanes(a, n):
    return jnp.pad(a.reshape(1, -1), ((0, 0), (0, n - a.size)))


def kernel(x, positions, ffn1_norm, ffn1_w_gate, ffn1_w_up, ffn1_w_down, mix_norm, w_in, gate_bias, q_a_norm, w_uq, kv_a_norm, w_uk, w_uv, q_head_norm, k_head_norm, w_proj_attn, conv_w, w_proj_conv, w_out, ffn2_norm, ffn2_w_gate, ffn2_w_up, ffn2_w_down, loss_target, m_ffn1_norm, m_ffn1_w_gate, m_ffn1_w_up, m_ffn1_w_down, m_mix_norm, m_w_in, m_gate_bias, m_q_a_norm, m_w_uq, m_kv_a_norm, m_w_uk, m_w_uv, m_q_head_norm, m_k_head_norm, m_w_proj_attn, m_conv_w, m_w_proj_conv, m_w_out, m_ffn2_norm, m_ffn2_w_gate, m_ffn2_w_up, m_ffn2_w_down, v_ffn1_norm, v_ffn1_w_gate, v_ffn1_w_up, v_ffn1_w_down, v_mix_norm, v_w_in, v_gate_bias, v_q_a_norm, v_w_uq, v_kv_a_norm, v_w_uk, v_w_uv, v_q_head_norm, v_k_head_norm, v_w_proj_attn, v_conv_w, v_w_proj_conv, v_w_out, v_ffn2_norm, v_ffn2_w_gate, v_ffn2_w_up, v_ffn2_w_down):
    args = dict(locals())
    view = lambda n, a: a.T if n in TRANSPOSED else a
    weights = {n: view(n, args[n]) for n in WEIGHT_ORDER}
    moments_m = {n: view(n, args["m_" + n]) for n in WEIGHT_ORDER}
    moments_v = {n: view(n, args["v_" + n]) for n in WEIGHT_ORDER}
    nb, seq, d = x.shape
    t = nb * seq
    chip = (2 * lax.axis_index("x") + lax.axis_index("y")).astype(jnp.int32)
    place = jnp.stack([chip, lax.axis_index("c").astype(jnp.int32)])
    grads, delta, new_m, new_v = {}, {}, {}, {}

    def adamw(names, carried=None):
        results = _adamw([(weights[n], grads[n], moments_m[n], moments_v[n]) for n in names], "adamw_" + names[0], carried)
        for n, (dn, mn, vn) in zip(names, results):
            delta[n], new_m[n], new_v[n] = dn, mn, vn

    conv_rows = conv_w.shape[0]
    conv_shard = jnp.pad(conv_w, ((0, 16 - conv_rows), (0, 0)))
    bufs = dict(zip(MATRICES + ("conv_w",), _cast_shards([weights[n] for n in MATRICES] + [conv_shard],
                                                         [BF16] * len(MATRICES) + [F32])))
    blocks = dict(zip(GROUP_FFN1, _run(_gather_carried([bufs[n] for n in GROUP_FFN1]), "gather_ffn1")))
    p = {n: _pad_lanes(weights[n], size) for n, size in SMALL}
    cos, sin = _rope_tables(positions)
    x_tok = x.reshape(t, d)

    gather_in = _gather_carried([bufs[n] for n in GROUP_IN])
    x1, gate1, up1, act1 = _ffn_fwd(x_tok, p["ffn1_norm"], blocks["ffn1_w_gate"], blocks["ffn1_w_up"], blocks["ffn1_w_down"], None,
                                    "ffn1_fwd", gather_in)
    blocks.update(zip(GROUP_IN, gather_in.results))
    w = _kernel_layouts({"w_in": blocks["w_in"].reshape(-1, d), **{n: _from_col_blocks(blocks[n]) for n in ("w_uq", "w_uk", "w_uv")}})
    p["conv_w"] = _from_col_blocks(blocks["conv_w"])[:conv_rows]

    gather_mix = _gather_carried([bufs[n] for n in GROUP_MIX])
    h2b, big, lat, q, k, v, vt = _inproj_fwd(x1, p["mix_norm"], w["w_m"], w["w_p"], p["q_a_norm"], p["kv_a_norm"], p["q_head_norm"],
                                             p["k_head_norm"], w["w_uq"], w["w_uk"], w["w_uv"], w["w_uvt"], cos, sin, gather_mix)
    blocks.update(zip(GROUP_MIX, gather_mix.results))
    w_pa = _from_col_blocks(blocks["w_proj_attn"])
    w_pc, w_out_full = blocks["w_proj_conv"].reshape(-1, d), blocks["w_out"].reshape(-1, d)

    gather_ffn2 = _gather_carried([bufs[n] for n in GROUP_FFN2[:2]])
    o, lse = _attn_fwd(q, k, vt, seq, gather_ffn2)
    gather_down = _gather_carried([bufs[n] for n in GROUP_FFN2[2:]])
    x2 = _mix_fwd(x1, o, big, p["gate_bias"], p["conv_w"], w_pa, w_pc, w_out_full, seq, gather_down)
    wg2, wu2 = gather_ffn2.results
    wd2, = gather_down.results
    dx3, gate2, up2, act2, loss = _ffn_fwd(x2, p["ffn2_norm"], wg2, wu2, wd2, loss_target.reshape(t, d), "ffn2_fwd")

    dx2, dg_ffn2, hb2, dgate2, dup2, dyb2 = _ffn_bwd_x(x2, p["ffn2_norm"], dx3, gate2, up2, wg2, wu2, wd2, "ffn2_bwd")
    g_ffn2 = [_tn_matmul(dgate2, hb2, "ffn2_dw_gate"), _tn_matmul(dup2, hb2, "ffn2_dw_up"), _tn_matmul(act2, dyb2, "ffn2_dw_down")]
    swap = _swap_carried(g_ffn2)
    do, delta_o, dz, dm, dbias, dw_pa, dw_pc, dw_out = _mix_bwd(dx2, o, big, p["gate_bias"], p["conv_w"], w_pa, w_pc, w_out_full, seq,
                                                                swap)
    part = _partials(GROUP_FFN2, g_ffn2, swap.results, place)
    send = _send_carried([pb for _, pb in part])
    dq, dk, dv = _attn_bwd(q, k, v, do, lse, delta_o.reshape(N_HEADS // ATTN_BWD_HEADS, ATTN_BWD_HEADS, -1), seq, send)
    join = _join_carried(_totals(GROUP_FFN2, g_ffn2, part, send.results, place))
    dp, dw_uq, dw_uk, dw_uv, dqa, dkva, dqh, dkh, dcw = _prep_bwd(
        lat, big, dz, dq, dk, dv, p["q_a_norm"], p["kv_a_norm"], p["q_head_norm"], p["k_head_norm"], w["w_uq"], w["w_uk"],
        w["w_uv"], cos, sin, p["conv_w"], seq, join)
    grads.update(zip(GROUP_FFN2, join.results))

    gg = _global_layouts({"w_m": _tn_matmul(dm, h2b, "dw_in_m", split_k=2), "w_p": _tn_matmul(dp, h2b, "dw_in_p", split_k=2),
                          "w_uq": dw_uq, "w_uk": dw_uk, "w_uv": dw_uv, "w_pa": dw_pa, "w_pc": dw_pc, "w_out": dw_out})
    g_mid = [_col_blocks(gg[n]) if n in COL_SHARDED else gg[n].reshape(N_CHIPS, -1, gg[n].shape[-1]) for n in GROUP_MID]
    swap = _swap_carried(g_mid)
    dx1, dg_mix = _inproj_bwd(x1, p["mix_norm"], dx2, dm, dp, w["w_m"], w["w_p"], swap)
    part = _partials(GROUP_MID, g_mid, swap.results, place)
    send = _send_carried([pb for _, pb in part])
    grad_x, dg_ffn1, hb1, dgate1, dup1, dyb1 = _ffn_bwd_x(x_tok, p["ffn1_norm"], dx1, gate1, up1, blocks["ffn1_w_gate"],
                                                         blocks["ffn1_w_up"], blocks["ffn1_w_down"], "ffn1_bwd", send)

    small_grads = {"ffn1_norm": dg_ffn1, "mix_norm": dg_mix, "gate_bias": dbias, "q_a_norm": dqa, "kv_a_norm": dkva,
                   "q_head_norm": dqh, "k_head_norm": dkh, "ffn2_norm": dg_ffn2}
    packed = jnp.concatenate([small_grads[n] for n, _ in SMALL] + [dcw.reshape(1, -1), loss], axis=1)
    total = _sum_devices(packed.reshape(8, -1)).reshape(1, -1)
    n_small = sum(size for _, size in SMALL)
    conv_cols = conv_w.shape[1]
    conv_total = total[:, n_small:n_small + conv_rows * d].reshape(conv_rows, d)
    grads["conv_w"] = lax.dynamic_slice_in_dim(conv_total, chip * conv_cols, conv_cols, axis=1)
    loss_total = total[0, n_small + conv_rows * d]

    join = _join_carried(_totals(GROUP_MID, g_mid, part, send.results, place))
    g_gate = _tn_matmul(dgate1, hb1, "ffn1_dw_gate", carried=join)
    grads.update(zip(GROUP_MID, join.results))
    swap_gate = _swap_carried([g_gate])
    g_up = _tn_matmul(dup1, hb1, "ffn1_dw_up", carried=swap_gate)
    part_gate = _partials(GROUP_FFN1[:1], [g_gate], swap_gate.results, place)
    send_gate, swap_up = _send_carried([part_gate[0][1]]), _swap_carried([g_up])
    g_down = _tn_matmul(act1, dyb1, "ffn1_dw_down", carried=_both(send_gate, swap_up))
    join_gate = _join_carried(_totals(GROUP_FFN1[:1], [g_gate], part_gate, send_gate.results, place))
    part_up = _partials(GROUP_FFN1[1:2], [g_up], swap_up.results, place)
    send_up, swap_down = _send_carried([part_up[0][1]]), _swap_carried([g_down])
    adamw(GROUP_FFN2, _both(_both(send_up, swap_down), join_gate))
    grads["ffn1_w_gate"] = join_gate.results[0]
    join_up = _join_carried(_totals(GROUP_FFN1[1:2], [g_up], part_up, send_up.results, place))
    part_down = _partials(GROUP_FFN1[2:], [g_down], swap_down.results, place)
    send_down = _send_carried([part_down[0][1]])
    adamw(("w_in",), _both(send_down, join_up))
    grads["ffn1_w_up"] = join_up.results[0]
    join_down = _join_carried(_totals(GROUP_FFN1[2:], [g_down], part_down, send_down.results, place))
    adamw(GROUP_FFN1[:2], join_down)
    grads["ffn1_w_down"] = join_down.results[0]
    adamw(GROUP_FFN1[2:])
    for n in GROUP_MID[1:] + ("conv_w",):
        adamw((n,))

    row = lambda a: a.reshape(1, -1)
    small = _adamw_small(total, [(row(weights[n]), row(moments_m[n]), row(moments_v[n])) for n, _ in SMALL], [size for _, size in SMALL])
    for (n, _), (gn, dn, mn, vn) in zip(SMALL, small):
        grads[n], delta[n], new_m[n], new_v[n] = gn.reshape(-1), dn.reshape(-1), mn.reshape(-1), vn.reshape(-1)

    return (loss_total, grad_x.reshape(nb, seq, d), *[view(n, src[n]) for src in (grads, delta, new_m, new_v) for n in WEIGHT_ORDER])
```

```python
import functools

import jax
import jax.numpy as jnp
from jax import lax
from jax.experimental import pallas as pl
from jax.experimental.pallas import tpu as pltpu

F32 = jnp.float32
BF16 = jnp.bfloat16

D_MODEL = 1024
N_HEADS = 8
QK_NOPE = 64
QK_ROPE = 32
QK_DIM = QK_NOPE + QK_ROPE
V_DIM = 64
HEAD_PAD = 128
Q_LORA = 384
KV_LORA = 256
ROPE_THETA = 10000.0
NORM_EPS = 1e-6
ATTN_SCALE = QK_DIM ** -0.5
MASK_VALUE = -1e30
N_CHIPS = 4
N_DEV = 8

ADAM_LR = 0.001
ADAM_B1 = 0.9
ADAM_B2 = 0.999
ADAM_EPS = 1e-08
ADAM_WD = 0.01
ADAM_STEP = 10

TOKEN_TILE = 256
ATTN_TILE = 512
TN_TILE = 512
VMEM_LIMIT = 56 * 1024 * 1024

M_COLS = 3 * D_MODEL
P_COLS = 2 * D_MODEL + Q_LORA + KV_LORA + HEAD_PAD
BIG_COLS = 5 * D_MODEL
LAT_COLS = Q_LORA + KV_LORA + HEAD_PAD

MESH_ID = pl.DeviceIdType.MESH
ANY = pl.BlockSpec(memory_space=pl.ANY)


def _params(semantics=None):
    return pltpu.CompilerParams(dimension_semantics=semantics, vmem_limit_bytes=VMEM_LIMIT)


class _Carried:
    def __init__(self, operands, out_shapes, aliases, n_sems, start, finish):
        self.operands, self.out_shapes, self.aliases, self.n_sems = list(operands), list(out_shapes), dict(aliases), n_sems
        self.start, self.finish = start, finish
        self.results = None


def _both(a, b):
    na, nao = len(a.operands), len(a.out_shapes)

    def start(ins, outs, sems, base):
        a.start(ins[:na], outs[:nao], sems, base)
        b.start(ins[na:], outs[nao:], sems, base + a.n_sems)

    def finish(ins, outs, sems, base):
        a.finish(ins[:na], outs[:nao], sems, base)
        b.finish(ins[na:], outs[nao:], sems, base + a.n_sems)

    aliases = dict(a.aliases)
    aliases.update({na + i: nao + o for i, o in b.aliases.items()})
    both = _Carried(a.operands + b.operands, a.out_shapes + b.out_shapes, aliases, a.n_sems + b.n_sems, start, finish)
    both.parts = (a, b)
    return both


def _set_results(carried, results):
    carried.results = list(results)
    if hasattr(carried, "parts"):
        a, b = carried.parts
        _set_results(a, results[:len(a.out_shapes)])
        _set_results(b, results[len(a.out_shapes):])


def _pallas(body, name, grid, in_specs, out_specs, out_shape, args, semantics, carried=None):
    if carried is None:
        return pl.pallas_call(body, name=name, grid=grid, in_specs=in_specs, out_specs=out_specs, out_shape=out_shape,
                              compiler_params=_params(semantics))(*args)
    n_in, n_out, n_ci, n_co = len(in_specs), len(out_specs), len(carried.operands), len(carried.out_shapes)

    def wrapped(*refs):
        ins, c_ins = refs[:n_in], refs[n_in:n_in + n_ci]
        outs, c_outs = refs[n_in + n_ci:n_in + n_ci + n_out], refs[n_in + n_ci + n_out:n_in + n_ci + n_out + n_co]
        sems = refs[-1]
        first = pl.program_id(0) == 0
        last = pl.program_id(0) == grid[0] - 1
        for axis in range(1, len(grid)):
            first = jnp.logical_and(first, pl.program_id(axis) == 0)
            last = jnp.logical_and(last, pl.program_id(axis) == grid[axis] - 1)

        @pl.when(first)
        def _():
            carried.start(c_ins, c_outs, sems, 0)

        body(*ins, *outs)

        @pl.when(last)
        def _():
            carried.finish(c_ins, c_outs, sems, 0)

    results = pl.pallas_call(
        wrapped, name=name, grid=grid, in_specs=list(in_specs) + [ANY] * n_ci, out_specs=list(out_specs) + [ANY] * n_co,
        out_shape=list(out_shape) + carried.out_shapes,
        input_output_aliases={n_in + i: n_out + o for i, o in carried.aliases.items()},
        scratch_shapes=[pltpu.SemaphoreType.DMA((carried.n_sems,))], compiler_params=_params(semantics))(*args, *carried.operands)
    _set_results(carried, results[n_out:])
    return results[:n_out]


def _run(carried, name):
    n_ci, n_co = len(carried.operands), len(carried.out_shapes)

    def body(*refs):
        carried.start(refs[:n_ci], refs[n_ci:n_ci + n_co], refs[-1], 0)
        carried.finish(refs[:n_ci], refs[n_ci:n_ci + n_co], refs[-1], 0)

    results = pl.pallas_call(body, name=name, in_specs=[ANY] * n_ci, out_specs=[ANY] * n_co, out_shape=carried.out_shapes,
                             input_output_aliases=carried.aliases,
                             scratch_shapes=[pltpu.SemaphoreType.DMA((carried.n_sems,))])(*carried.operands)
    _set_results(carried, results)
    return carried.results


def _resident(shape):
    nd = len(shape)
    return pl.BlockSpec(shape, lambda *_: (0,) * nd, pipeline_mode=pl.Buffered(1))


def _const(shape):
    nd = len(shape)
    return pl.BlockSpec(shape, lambda *_: (0,) * nd)


def _mm(a, b):
    return jnp.dot(a, b, preferred_element_type=F32)


def _mm_nt(a, b):
    return lax.dot_general(a, b, (((1,), (1,)), ((), ())), preferred_element_type=F32)


def _mm_tn(a, b):
    return lax.dot_general(a, b, (((0,), (0,)), ((), ())), preferred_element_type=F32)


def _bf(a):
    return a.astype(BF16)


def _sigmoid(a):
    return 1.0 / (1.0 + jnp.exp(-a))


def _rms(x, gain, n=None):
    n = x.shape[-1] if n is None else n
    r = lax.rsqrt(jnp.sum(x * x, axis=-1, keepdims=True) * (1.0 / n) + NORM_EPS)
    return (x * r) * gain, r


def _rms_bwd(x, r, gain, dh, n=None):
    n = x.shape[-1] if n is None else n
    u = dh * gain
    dx = r * u - x * ((r * r * r) * (jnp.sum(u * x, axis=-1, keepdims=True) * (1.0 / n)))
    dgain = jnp.sum(dh * (x * r), axis=0, keepdims=True)
    return dx, dgain


def _rope_swap(t):
    lane = lax.broadcasted_iota(jnp.int32, t.shape, 1)
    lo = (lane >= QK_NOPE) & (lane < QK_NOPE + QK_ROPE // 2)
    hi = (lane >= QK_NOPE + QK_ROPE // 2) & (lane < QK_DIM)
    up = pltpu.roll(t, HEAD_PAD - QK_ROPE // 2, 1)
    down = pltpu.roll(t, QK_ROPE // 2, 1)
    return jnp.where(lo, up, jnp.where(hi, down, 0.0))


def _rope(t, cos, sin):
    return t * cos + _rope_swap(t) * sin


def _rope_bwd(dt, cos, sin):
    return dt * cos + _rope_swap(dt * sin)


def _shift_down(u, prev8, k):
    s = pltpu.roll(u, k, 0)
    p = pltpu.roll(prev8, k, 0)
    row = lax.broadcasted_iota(jnp.int32, prev8.shape, 0)
    top = jnp.where(row < k, p, s[:8])
    return jnp.concatenate([top, s[8:]], axis=0)


def _shift_up(d, next8, k):
    tm = d.shape[0]
    s = pltpu.roll(d, tm - k, 0)
    n = pltpu.roll(next8, 8 - k, 0)
    row = lax.broadcasted_iota(jnp.int32, next8.shape, 0)
    bot = jnp.where(row >= 8 - k, n, s[tm - 8:])
    return jnp.concatenate([s[:tm - 8], bot], axis=0)


def _ffn_fwd(x, gain, wg, wu, wd, target, name, carried=None):
    t, d = x.shape
    nb, f, _ = wg.shape
    tm = TOKEN_TILE
    with_loss = target is not None

    def body(*refs):
        if with_loss:
            x_ref, g_ref, wg_ref, wu_ref, wd_ref, t_ref, out_ref, gate_ref, up_ref, act_ref, loss_ref = refs
        else:
            x_ref, g_ref, wg_ref, wu_ref, wd_ref, out_ref, gate_ref, up_ref, act_ref = refs
        xv = x_ref[...]
        h, _ = _rms(xv, g_ref[...])
        hb = _bf(h)
        y = jnp.zeros((tm, d), F32)
        nxt = (_mm_nt(hb, wg_ref[0]), _mm_nt(hb, wu_ref[0]))
        for j in range(nb):
            gate, up = nxt
            if j + 1 < nb:
                nxt = (_mm_nt(hb, wg_ref[j + 1]), _mm_nt(hb, wu_ref[j + 1]))
            act = _bf((gate * _sigmoid(gate)) * up)
            y = y + _mm(act, wd_ref[j])
            gate_ref[j] = _bf(gate)
            up_ref[j] = _bf(up)
            act_ref[j] = act
        out = xv + 0.5 * y
        if with_loss:
            err = out - t_ref[...]
            out_ref[...] = err * (1.0 / d)

            @pl.when(pl.program_id(0) == 0)
            def _():
                loss_ref[...] = jnp.zeros_like(loss_ref)

            part = jnp.sum(jnp.sum(err * err, axis=1, keepdims=True), axis=0, keepdims=True)
            loss_ref[...] += jnp.broadcast_to(part * (0.5 / d), loss_ref.shape)
        else:
            out_ref[...] = out

    tok = pl.BlockSpec((tm, d), lambda i: (i, 0))
    blk = pl.BlockSpec((nb, tm, f), lambda i: (0, i, 0))
    in_specs = [tok, _const((1, d)), _resident(wg.shape), _resident(wu.shape), _resident(wd.shape)]
    args = [x, gain, wg, wu, wd]
    out_shape = [jax.ShapeDtypeStruct((t, d), F32)] + [jax.ShapeDtypeStruct((nb, t, f), BF16)] * 3
    out_specs = [tok, blk, blk, blk]
    if with_loss:
        in_specs.append(tok)
        args.append(target)
        out_shape.append(jax.ShapeDtypeStruct((1, 128), F32))
        out_specs.append(_const((1, 128)))
    return _pallas(body, name, (t // tm,), in_specs, out_specs, out_shape, args, ("arbitrary",), carried)


def _ffn_bwd_x(x, gain, dout, gate, up, wg, wu, wd, name, carried=None):
    t, d = x.shape
    nb, f, _ = wg.shape
    tm = TOKEN_TILE

    def body(x_ref, g_ref, dout_ref, gate_ref, up_ref, wg_ref, wu_ref, wd_ref,
             dx_ref, dgain_ref, hb_ref, dgate_ref, dup_ref, dyb_ref):
        xv = x_ref[...]
        gain_v = g_ref[...]
        h, r = _rms(xv, gain_v)
        hb_ref[...] = _bf(h)
        dout_v = dout_ref[...]
        dyb = _bf(0.5 * dout_v)
        dyb_ref[...] = dyb
        dh = jnp.zeros((tm, d), F32)
        nxt = _mm_nt(dyb, wd_ref[0])
        for j in range(nb):
            dact = nxt
            if j + 1 < nb:
                nxt = _mm_nt(dyb, wd_ref[j + 1])
            gt = gate_ref[j].astype(F32)
            uv = up_ref[j].astype(F32)
            s = _sigmoid(gt)
            dup = _bf(dact * (gt * s))
            dgate = _bf((dact * uv) * (s * (1.0 + gt * (1.0 - s))))
            dh = dh + _mm(dgate, wg_ref[j]) + _mm(dup, wu_ref[j])
            dgate_ref[j] = dgate
            dup_ref[j] = dup
        dxn, dgain = _rms_bwd(xv, r, gain_v, dh)
        dx_ref[...] = dout_v + dxn

        @pl.when(pl.program_id(0) == 0)
        def _():
            dgain_ref[...] = jnp.zeros_like(dgain_ref)

        dgain_ref[...] += dgain

    tok = pl.BlockSpec((tm, d), lambda i: (i, 0))
    blk = pl.BlockSpec((nb, tm, f), lambda i: (0, i, 0))
    return _pallas(
        body, name, (t // tm,),
        [tok, _const((1, d)), tok, blk, blk, _resident(wg.shape), _resident(wu.shape), _resident(wd.shape)],
        [tok, _const((1, d)), tok, blk, blk, tok],
        [jax.ShapeDtypeStruct((t, d), F32), jax.ShapeDtypeStruct((1, d), F32), jax.ShapeDtypeStruct((t, d), BF16),
         jax.ShapeDtypeStruct((nb, t, f), BF16), jax.ShapeDtypeStruct((nb, t, f), BF16), jax.ShapeDtypeStruct((t, d), BF16)],
        (x, gain, dout, gate, up, wg, wu, wd), ("arbitrary",), carried)


def _tn_matmul(a, b, name, split_k=1, carried=None):
    t = a.shape[-2]
    k = a.shape[-1]
    n = b.shape[-1]
    tt = min(TN_TILE, t)
    nt = t // tt

    def body(a_ref, b_ref, o_ref):
        @pl.when(pl.program_id(1) == 0)
        def _():
            o_ref[...] = jnp.zeros_like(o_ref)

        o_ref[...] += _mm_tn(a_ref[...], b_ref[...])

    if split_k > 1:
        assert a.ndim == 2 and b.ndim == 2 and k % (split_k * 128) == 0
        tk = k // split_k
        g = split_k
        a_spec = pl.BlockSpec((tt, tk), lambda gi, ti: (ti, gi))
        b_spec = pl.BlockSpec((tt, n), lambda gi, ti: (ti, 0))
        o_spec = pl.BlockSpec((tk, n), lambda gi, ti: (gi, 0))
        out_shape = jax.ShapeDtypeStruct((k, n), F32)
    else:
        g = a.shape[0] if a.ndim == 3 else b.shape[0]
        a_spec = (pl.BlockSpec((None, tt, k), lambda gi, ti: (gi, ti, 0)) if a.ndim == 3
                  else pl.BlockSpec((tt, k), lambda gi, ti: (ti, 0)))
        b_spec = (pl.BlockSpec((None, tt, n), lambda gi, ti: (gi, ti, 0)) if b.ndim == 3
                  else pl.BlockSpec((tt, n), lambda gi, ti: (ti, 0)))
        o_spec = pl.BlockSpec((None, k, n), lambda gi, ti: (gi, 0, 0))
        out_shape = jax.ShapeDtypeStruct((g, k, n), F32)
    return _pallas(body, name, (g, nt), [a_spec, b_spec], [o_spec], [out_shape], (a, b), ("arbitrary", "arbitrary"), carried)[0]


def _inproj_fwd(x1, gain, w_m, w_p, qa_gain, kva_gain, qh_gain, kh_gain, w_uq, w_uk, w_uv, w_uvt, cos, sin, carried=None):
    t, d = x1.shape
    tm = TOKEN_TILE

    def body(x_ref, g_ref, wm_ref, wp_ref, qa_ref, kva_ref, qh_ref, kh_ref, wuq_ref, wuk_ref, wuv_ref, wuvt_ref, cos_ref, sin_ref,
             hb_ref, big_ref, lat_ref, q_ref, k_ref, v_ref, vt_ref):
        h, _ = _rms(x_ref[...], g_ref[...])
        hb = _bf(h)
        hb_ref[...] = hb
        lat = _mm_nt(hb, wp_ref[2 * D_MODEL:, :])
        lat_ref[...] = lat
        cq, _ = _rms(lat[:, :Q_LORA], qa_ref[...])
        ckv, _ = _rms(lat[:, Q_LORA:Q_LORA + KV_LORA], kva_ref[...])
        k_rope = lat[:, Q_LORA + KV_LORA:]
        cqb = _bf(cq)
        ckvb = _bf(ckv)
        cos_v = cos_ref[...]
        sin_v = sin_ref[...]
        q_all = _mm(cqb, wuq_ref[...])
        k_all = _mm(ckvb, wuk_ref[...])
        v_ref[...] = _bf(_mm(ckvb, wuv_ref[...]))
        vt_all = _mm_nt(wuvt_ref[...], ckvb)
        m_chunk, n_m = M_COLS // 6, 6
        for hd in range(N_HEADS):
            if hd < n_m:
                start, w_rows = hd * m_chunk, wm_ref[hd * m_chunk:(hd + 1) * m_chunk, :]
            else:
                start, w_rows = M_COLS + (hd - n_m) * D_MODEL, wp_ref[(hd - n_m) * D_MODEL:(hd - n_m + 1) * D_MODEL, :]
            big_ref[:, start:start + w_rows.shape[0]] = _mm_nt(hb, w_rows)
            lanes = slice(hd * HEAD_PAD, (hd + 1) * HEAD_PAD)
            qn, _ = _rms(q_all[:, lanes], qh_ref[...], QK_DIM)
            q_ref[hd] = _bf(_rope(qn, cos_v, sin_v))
            kn, _ = _rms(k_all[:, lanes] + k_rope, kh_ref[...], QK_DIM)
            k_ref[hd] = _bf(_rope(kn, cos_v, sin_v))
            vt_ref[hd] = _bf(vt_all[hd * V_DIM:(hd + 1) * V_DIM])

    tok = lambda c: pl.BlockSpec((tm, c), lambda i: (i, 0))
    head = lambda c: pl.BlockSpec((N_HEADS, tm, c), lambda i: (0, i, 0))
    return _pallas(
        body, "inproj_fwd", (t // tm,),
        [tok(d), _const((1, d)), _resident(w_m.shape), _resident(w_p.shape), _const((1, Q_LORA)), _const((1, KV_LORA)),
         _const((1, HEAD_PAD)), _const((1, HEAD_PAD)), _resident(w_uq.shape), _resident(w_uk.shape),
         _resident(w_uv.shape), _resident(w_uvt.shape), tok(HEAD_PAD), tok(HEAD_PAD)],
        [tok(d), tok(BIG_COLS), tok(LAT_COLS), head(HEAD_PAD), head(HEAD_PAD), tok(N_HEADS * V_DIM),
         pl.BlockSpec((N_HEADS, V_DIM, tm), lambda i: (0, 0, i))],
        [jax.ShapeDtypeStruct((t, d), BF16), jax.ShapeDtypeStruct((t, BIG_COLS), F32),
         jax.ShapeDtypeStruct((t, LAT_COLS), F32), jax.ShapeDtypeStruct((N_HEADS, t, HEAD_PAD), BF16),
         jax.ShapeDtypeStruct((N_HEADS, t, HEAD_PAD), BF16), jax.ShapeDtypeStruct((t, N_HEADS * V_DIM), BF16),
         jax.ShapeDtypeStruct((N_HEADS, V_DIM, t), BF16)],
        (x1, gain, w_m, w_p, qa_gain, kva_gain, qh_gain, kh_gain, w_uq, w_uk, w_uv, w_uvt, cos, sin), ("arbitrary",), carried)


EXP2_SCALE = ATTN_SCALE * 1.4426950408889634


def _diagonal_keep(tk, tq):
    return lax.broadcasted_iota(jnp.int32, (tk, tq), 0) <= lax.broadcasted_iota(jnp.int32, (tk, tq), 1)


def _attn_fwd(q, k, vt, seq, carried=None):
    _, t, _ = q.shape
    nseq = t // seq
    tq = tk = ATTN_TILE
    nq = seq // tq

    def body(q_ref, k_ref, vt_ref, o_ref, lse_ref):
        i = pl.program_id(1)
        qs = [q_ref[h] for h in range(N_HEADS)]
        keep = _diagonal_keep(tk, tq)

        def scores(h, k0):
            return _mm_nt(k_ref[h, pl.ds(k0, tk), :], qs[h])

        def update(h, st, state, k0, diagonal):
            m, l, acc = state
            if diagonal:
                st = jnp.where(keep, st, MASK_VALUE)
            m_new = jnp.maximum(m, jnp.max(st, axis=0, keepdims=True))
            pt = jnp.exp2((st - m_new) * EXP2_SCALE)
            alpha = jnp.exp2((m - m_new) * EXP2_SCALE)
            l_new = alpha * l + jnp.sum(pt, axis=0, keepdims=True)
            return m_new, l_new, alpha * acc + _mm(vt_ref[h, :, pl.ds(k0, tk)], _bf(pt))

        def tiles(states, k0, diagonal):
            st, new = scores(0, k0), []
            for h in range(N_HEADS):
                st_next = scores(h + 1, k0) if h + 1 < N_HEADS else None
                new.append(update(h, st, states[h], k0, diagonal))
                st = st_next
            return tuple(new)

        init = tuple((jnp.full((1, tq), MASK_VALUE, F32), jnp.zeros((1, tq), F32), jnp.zeros((V_DIM, tq), F32))
                     for _ in range(N_HEADS))
        states = lax.fori_loop(0, i, lambda j, s: tiles(s, pl.multiple_of(j * tk, tk), False), init)
        states = tiles(states, pl.multiple_of(i * tk, tk), True)
        outs = []
        for h in range(N_HEADS):
            m, l, acc = states[h]
            outs.append((acc / l).T)
            lse_ref[h] = m * EXP2_SCALE + jnp.log2(l)
        o_ref[...] = _bf(jnp.concatenate(outs, axis=-1))

    return _pallas(
        body, "attn_fwd", (nseq, nq),
        [pl.BlockSpec((N_HEADS, tq, HEAD_PAD), lambda b, i: (0, b * nq + i, 0)),
         pl.BlockSpec((N_HEADS, seq, HEAD_PAD), lambda b, i: (0, b, 0)),
         pl.BlockSpec((N_HEADS, V_DIM, seq), lambda b, i: (0, 0, b))],
        [pl.BlockSpec((tq, N_HEADS * V_DIM), lambda b, i: (b * nq + i, 0)),
         pl.BlockSpec((N_HEADS, 1, tq), lambda b, i: (0, 0, b * nq + i))],
        [jax.ShapeDtypeStruct((t, N_HEADS * V_DIM), BF16), jax.ShapeDtypeStruct((N_HEADS, 1, t), F32)],
        (q, k, vt), ("arbitrary", "arbitrary"), carried)


ATTN_BWD_HEADS = 4


def _attn_bwd(q, k, v, do, lse, delta, seq, carried=None):
    _, t, _ = q.shape
    nseq = t // seq
    tq = tk = ATTN_TILE
    n = seq // tq
    hb = ATTN_BWD_HEADS

    def body(q_ref, k_ref, v_ref, do_ref, lse_ref, delta_ref, dq_ref, dk_ref, dv_ref):
        dq_ref[...] = jnp.zeros_like(dq_ref)
        dk_ref[...] = jnp.zeros_like(dk_ref)
        dv_ref[...] = jnp.zeros_like(dv_ref)
        keep = _diagonal_keep(tk, tq)

        def tile(h, k0, q0, diagonal):
            kj = k_ref[h, pl.ds(k0, tk), :]
            qi = q_ref[h, pl.ds(q0, tq), :]
            doi = _bf(do_ref[pl.ds(q0, tq), h * V_DIM:(h + 1) * V_DIM])
            st = _mm_nt(kj, qi)
            if diagonal:
                st = jnp.where(keep, st, MASK_VALUE)
            pt = jnp.exp2(st * EXP2_SCALE - lse_ref[h, :, pl.ds(q0, tq)])
            dv_ref[pl.ds(k0, tk), h * V_DIM:(h + 1) * V_DIM] += _mm(_bf(pt), doi)
            dpt = _mm_nt(v_ref[pl.ds(k0, tk), h * V_DIM:(h + 1) * V_DIM], doi)
            dst = _bf((pt * (dpt - delta_ref[pl.ds(h, 1), pl.ds(q0, tq)])) * ATTN_SCALE)
            dk_ref[h, pl.ds(k0, tk), :] += _mm(dst, qi)
            dq_ref[h, pl.ds(q0, tq), :] += _mm_tn(dst, kj)

        def kv_step(j, _):
            k0 = pl.multiple_of(j * tk, tk)
            for h in range(hb):
                tile(h, k0, k0, True)

            def q_step(i, _):
                q0 = pl.multiple_of(i * tq, tq)
                for h in range(hb):
                    tile(h, k0, q0, False)
                return 0

            lax.fori_loop(j + 1, n, q_step, 0)
            return 0

        lax.fori_loop(0, n, kv_step, 0)

    hspec = lambda c: pl.BlockSpec((hb, seq, c), lambda b, g: (g, b, 0))
    cols = pl.BlockSpec((seq, hb * V_DIM), lambda b, g: (b, g))
    return _pallas(
        body, "attn_bwd", (nseq, N_HEADS // hb),
        [hspec(HEAD_PAD), hspec(HEAD_PAD), cols, cols,
         pl.BlockSpec((hb, 1, seq), lambda b, g: (g, 0, b)), pl.BlockSpec((None, hb, seq), lambda b, g: (g, 0, b))],
        [hspec(HEAD_PAD), hspec(HEAD_PAD), cols],
        [jax.ShapeDtypeStruct((N_HEADS, t, HEAD_PAD), F32), jax.ShapeDtypeStruct((N_HEADS, t, HEAD_PAD), F32),
         jax.ShapeDtypeStruct((t, N_HEADS * V_DIM), F32)],
        (q, k, v, do, lse, delta), ("arbitrary", "arbitrary"), carried)


def _mixer_values(o_ref, gb_ref, gla_ref, glb_ref, xc_ref, gc_ref, xcp_ref, gcp_ref, bias_ref, cw_ref, wpa_ref, wpc_ref,
                  first_of_seq, early=None):
    y_a = _mm(o_ref[...], wpa_ref[...])
    extra = early() if early is not None else None
    gb = gb_ref[...]
    u = gc_ref[...] * xc_ref[...]
    u_prev = jnp.where(first_of_seq, 0.0, gcp_ref[...] * xcp_ref[...])
    cw = cw_ref[...]
    z = cw[2:3] * u + cw[1:2] * _shift_down(u, u_prev, 1) + cw[0:1] * _shift_down(u, u_prev, 2)
    gbz = _bf(gb * z)
    y_b = _mm(gbz, wpc_ref[...])
    bias = bias_ref[...]
    gate_a = _sigmoid(gla_ref[...] + bias[:, :D_MODEL])
    gate_b = _sigmoid(glb_ref[...] + bias[:, D_MODEL:])
    merged = _bf(gate_a * y_a + gate_b * y_b)
    return gb, u, z, gbz, y_a, y_b, gate_a, gate_b, merged, extra


def _mixer_specs(tm, seq):
    d = D_MODEL
    tok = pl.BlockSpec((tm, d), lambda i: (i, 0))
    col = lambda c: pl.BlockSpec((tm, d), lambda i: (i, c))
    prev = lambda c: pl.BlockSpec((8, d), lambda i: (jnp.maximum(i * (tm // 8) - 1, 0), c))
    o_spec = pl.BlockSpec((tm, N_HEADS * V_DIM), lambda i: (i, 0))
    fwd_specs = [o_spec, col(0), col(1), col(2), col(3), col(4), prev(3), prev(4), _const((1, 2 * d)), _const((3, d)),
                 _resident((N_HEADS * V_DIM, d)), _resident((d, d)), _resident((d, d))]
    return tok, fwd_specs


def _mix_fwd(x1, o, big, gate_bias, conv_w, w_pa, w_pc, w_out, seq, carried=None):
    t, d = x1.shape
    tm = TOKEN_TILE
    tiles_per_seq = seq // tm

    def body(x_ref, o_ref, gb_ref, gla_ref, glb_ref, xc_ref, gc_ref, xcp_ref, gcp_ref, bias_ref, cw_ref, wpa_ref, wpc_ref,
             wout_ref, x2_ref):
        first = pl.program_id(0) % tiles_per_seq == 0
        merged = _mixer_values(o_ref, gb_ref, gla_ref, glb_ref, xc_ref, gc_ref, xcp_ref, gcp_ref, bias_ref, cw_ref, wpa_ref,
                               wpc_ref, first)[-2]
        x2_ref[...] = x_ref[...] + _mm(merged, wout_ref[...])

    tok, fwd_specs = _mixer_specs(tm, seq)
    return _pallas(body, "mix_fwd", (t // tm,), [tok] + fwd_specs, [tok], [jax.ShapeDtypeStruct((t, d), F32)],
                   (x1, o, big, big, big, big, big, big, big, gate_bias, conv_w, w_pa, w_pc, w_out), ("arbitrary",), carried)[0]


def _mix_bwd(dx2, o, big, gate_bias, conv_w, w_pa, w_pc, w_out, seq, carried=None):
    t, d = dx2.shape
    tm = TOKEN_TILE
    tiles_per_seq = seq // tm
    hv = N_HEADS * V_DIM

    def body(dx_ref, o_ref, gb_ref, gla_ref, glb_ref, xc_ref, gc_ref, xcp_ref, gcp_ref, bias_ref, cw_ref, wpa_ref, wpc_ref,
             wout_ref, do_ref, delta_ref, dz_ref, dm_ref, dbias_ref, dwpa_ref, dwpc_ref, dwout_ref):
        @pl.when(pl.program_id(0) == 0)
        def _():
            dbias_ref[...] = jnp.zeros_like(dbias_ref)
            dwpa_ref[...] = jnp.zeros_like(dwpa_ref)
            dwpc_ref[...] = jnp.zeros_like(dwpc_ref)
            dwout_ref[...] = jnp.zeros_like(dwout_ref)

        first = pl.program_id(0) % tiles_per_seq == 0
        dxb = _bf(dx_ref[...])
        gb, _, z, gbz, y_a, y_b, gate_a, gate_b, merged, dmerged = _mixer_values(
            o_ref, gb_ref, gla_ref, glb_ref, xc_ref, gc_ref, xcp_ref, gcp_ref, bias_ref, cw_ref, wpa_ref, wpc_ref, first,
            early=lambda: _mm_nt(dxb, wout_ref[...]))
        dwout_ref[...] += _mm_tn(merged, dxb)
        dya = _bf(dmerged * gate_a)
        dyb = _bf(dmerged * gate_b)
        do_v = _mm_nt(dya, wpa_ref[...])
        dgz = _mm_nt(dyb, wpc_ref[...])
        dwpa_ref[...] += _mm_tn(o_ref[...], dya)
        dwpc_ref[...] += _mm_tn(gbz, dyb)
        dla = (dmerged * y_a) * (gate_a * (1.0 - gate_a))
        dlb = (dmerged * y_b) * (gate_b * (1.0 - gate_b))
        dbias_ref[:, :d] += jnp.sum(dla, axis=0, keepdims=True)
        dbias_ref[:, d:] += jnp.sum(dlb, axis=0, keepdims=True)
        dm_ref[:, d:2 * d] = _bf(dla)
        dm_ref[:, 2 * d:] = _bf(dlb)
        do_ref[...] = do_v
        head = lax.broadcasted_iota(jnp.int32, (N_HEADS, hv), 0) * V_DIM
        col = lax.broadcasted_iota(jnp.int32, (N_HEADS, hv), 1)
        in_head = ((col >= head) & (col < head + V_DIM)).astype(F32)
        delta_ref[...] = lax.dot_general(in_head, do_v * o_ref[...].astype(F32), (((1,), (1,)), ((), ())),
                                         precision=lax.Precision.HIGHEST, preferred_element_type=F32)
        dz_ref[...] = dgz * gb
        dm_ref[:, :d] = _bf(dgz * z)

    tok, fwd_specs = _mixer_specs(tm, seq)
    return _pallas(
        body, "mix_bwd", (t // tm,), [tok] + fwd_specs,
        [pl.BlockSpec((tm, hv), lambda i: (i, 0)), pl.BlockSpec((N_HEADS, tm), lambda i: (0, i)), tok,
         pl.BlockSpec((tm, M_COLS), lambda i: (i, 0)), _const((1, 2 * d)), _const((hv, d)), _const((d, d)), _const((d, d))],
        [jax.ShapeDtypeStruct((t, hv), F32), jax.ShapeDtypeStruct((N_HEADS, t), F32), jax.ShapeDtypeStruct((t, d), F32),
         jax.ShapeDtypeStruct((t, M_COLS), BF16), jax.ShapeDtypeStruct((1, 2 * d), F32), jax.ShapeDtypeStruct((hv, d), F32),
         jax.ShapeDtypeStruct((d, d), F32), jax.ShapeDtypeStruct((d, d), F32)],
        (dx2, o, big, big, big, big, big, big, big, gate_bias, conv_w, w_pa, w_pc, w_out), ("arbitrary",), carried)


def _prep_bwd(lat, big, dz, dq, dk, dv, qa_gain, kva_gain, qh_gain, kh_gain, w_uq, w_uk, w_uv, cos, sin, conv_w, seq, carried=None):
    t = lat.shape[0]
    d = D_MODEL
    tm = TOKEN_TILE
    tiles_per_seq = seq // tm
    last_blk = t // 8 - 1

    def body(lat_ref, xc_ref, gc_ref, dz_ref, dzn_ref, dq_ref, dk_ref, dv_ref, qa_ref, kva_ref, qh_ref, kh_ref, wuq_ref, wuk_ref,
             wuv_ref, cos_ref, sin_ref, cw_ref,
             dp_ref, dwuq_ref, dwuk_ref, dwuv_ref, dqa_ref, dkva_ref, dqh_ref, dkh_ref, dcw_ref):
        pid = pl.program_id(0)

        @pl.when(pid == 0)
        def _():
            for r in (dwuq_ref, dwuk_ref, dwuv_ref, dqa_ref, dkva_ref, dqh_ref, dkh_ref, dcw_ref):
                r[...] = jnp.zeros_like(r)

        lat_v = lat_ref[...]
        q_lat = lat_v[:, :Q_LORA]
        kv_lat = lat_v[:, Q_LORA:Q_LORA + KV_LORA]
        k_rope = lat_v[:, Q_LORA + KV_LORA:]
        qa_gain_v = qa_ref[...]
        kva_gain_v = kva_ref[...]
        qh_gain_v = qh_ref[...]
        kh_gain_v = kh_ref[...]
        cq, rq = _rms(q_lat, qa_gain_v)
        ckv, rkv = _rms(kv_lat, kva_gain_v)
        cqb = _bf(cq)
        ckvb = _bf(ckv)
        cos_v = cos_ref[...]
        sin_v = sin_ref[...]
        lane = lax.broadcasted_iota(jnp.int32, (tm, HEAD_PAD), 1)
        rope_lanes = (lane >= QK_NOPE) & (lane < QK_DIM)
        dk_rope = jnp.zeros((tm, HEAD_PAD), F32)
        dqh_gain = jnp.zeros((1, HEAD_PAD), F32)
        dkh_gain = jnp.zeros((1, HEAD_PAD), F32)
        q_all = _mm(cqb, wuq_ref[...])
        k_all = _mm(ckvb, wuk_ref[...])
        dvb = _bf(dv_ref[...])
        dckv = _mm_nt(dvb, wuv_ref[...])
        dwuv_ref[...] += _mm_tn(ckvb, dvb)

        last = pid % tiles_per_seq == tiles_per_seq - 1
        dzv = dz_ref[...]
        dz_next = jnp.where(last, 0.0, dzn_ref[...])
        dz1 = _shift_up(dzv, dz_next, 1)
        dz2 = _shift_up(dzv, dz_next, 2)
        cw = cw_ref[...]
        xc = xc_ref[...]
        gc = gc_ref[...]
        u = gc * xc
        du = cw[2:3] * dzv + cw[1:2] * dz1 + cw[0:1] * dz2
        dp_ref[:, :d] = _bf(du * gc)
        dp_ref[:, d:2 * d] = _bf(du * xc)
        dcw_ref[0:1, :] += jnp.sum(dz2 * u, axis=0, keepdims=True)
        dcw_ref[1:2, :] += jnp.sum(dz1 * u, axis=0, keepdims=True)
        dcw_ref[2:3, :] += jnp.sum(dzv * u, axis=0, keepdims=True)

        dcq = jnp.zeros((tm, Q_LORA), F32)
        half = N_HEADS // 2
        for part in range(2):
            dq_heads, dk_heads = [], []
            for hd in range(part * half, (part + 1) * half):
                lanes = slice(hd * HEAD_PAD, (hd + 1) * HEAD_PAD)
                q_pre = q_all[:, lanes]
                _, rr = _rms(q_pre, qh_gain_v, QK_DIM)
                dq_pre, dg = _rms_bwd(q_pre, rr, qh_gain_v, _rope_bwd(dq_ref[hd], cos_v, sin_v), QK_DIM)
                dqh_gain = dqh_gain + dg
                dq_heads.append(_bf(dq_pre))

                k_pre = k_all[:, lanes] + k_rope
                _, rr = _rms(k_pre, kh_gain_v, QK_DIM)
                dk_pre, dg = _rms_bwd(k_pre, rr, kh_gain_v, _rope_bwd(dk_ref[hd], cos_v, sin_v), QK_DIM)
                dkh_gain = dkh_gain + dg
                dk_rope = dk_rope + jnp.where(rope_lanes, dk_pre, 0.0)
                dk_heads.append(_bf(dk_pre))
            dq_part = jnp.concatenate(dq_heads, axis=1)
            dk_part = jnp.concatenate(dk_heads, axis=1)
            cols = slice(part * half * HEAD_PAD, (part + 1) * half * HEAD_PAD)
            dcq = dcq + _mm_nt(dq_part, wuq_ref[:, cols])
            dckv = dckv + _mm_nt(dk_part, wuk_ref[:, cols])
            dwuq_ref[:, cols] += _mm_tn(cqb, dq_part)
            dwuk_ref[:, cols] += _mm_tn(ckvb, dk_part)
        dqh_ref[...] += dqh_gain
        dkh_ref[...] += dkh_gain
        dq_lat, dg = _rms_bwd(q_lat, rq, qa_gain_v, dcq)
        dqa_ref[...] += dg
        dkv_lat, dg = _rms_bwd(kv_lat, rkv, kva_gain_v, dckv)
        dkva_ref[...] += dg
        dp_ref[:, 2 * d:2 * d + Q_LORA] = _bf(dq_lat)
        dp_ref[:, 2 * d + Q_LORA:2 * d + Q_LORA + KV_LORA] = _bf(dkv_lat)
        dp_ref[:, 2 * d + Q_LORA + KV_LORA:] = _bf(dk_rope)

    tok = lambda c: pl.BlockSpec((tm, c), lambda i: (i, 0))
    col = lambda c: pl.BlockSpec((tm, d), lambda i: (i, c))
    head = lambda c: pl.BlockSpec((N_HEADS, tm, c), lambda i: (0, i, 0))
    nxt = pl.BlockSpec((8, d), lambda i: (jnp.minimum((i + 1) * (tm // 8), last_blk), 0))
    return _pallas(
        body, "prep_bwd", (t // tm,),
        [tok(LAT_COLS), col(3), col(4), tok(d), nxt, head(HEAD_PAD), head(HEAD_PAD), tok(N_HEADS * V_DIM),
         _const((1, Q_LORA)), _const((1, KV_LORA)), _const((1, HEAD_PAD)), _const((1, HEAD_PAD)),
         _resident(w_uq.shape), _resident(w_uk.shape), _resident(w_uv.shape), tok(HEAD_PAD), tok(HEAD_PAD), _const((3, d))],
        [tok(P_COLS), _const(w_uq.shape), _const(w_uk.shape), _const(w_uv.shape), _const((1, Q_LORA)),
         _const((1, KV_LORA)), _const((1, HEAD_PAD)), _const((1, HEAD_PAD)), _const((3, d))],
        [jax.ShapeDtypeStruct((t, P_COLS), BF16), jax.ShapeDtypeStruct(w_uq.shape, F32),
         jax.ShapeDtypeStruct(w_uk.shape, F32), jax.ShapeDtypeStruct(w_uv.shape, F32),
         jax.ShapeDtypeStruct((1, Q_LORA), F32), jax.ShapeDtypeStruct((1, KV_LORA), F32),
         jax.ShapeDtypeStruct((1, HEAD_PAD), F32), jax.ShapeDtypeStruct((1, HEAD_PAD), F32), jax.ShapeDtypeStruct((3, d), F32)],
        (lat, big, big, dz, dz, dq, dk, dv, qa_gain, kva_gain, qh_gain, kh_gain, w_uq, w_uk, w_uv, cos, sin, conv_w),
        ("arbitrary",), carried)


def _inproj_bwd(x1, gain, dx2, dm, dp, w_m, w_p, carried=None):
    t, d = x1.shape
    tm = TOKEN_TILE

    def body(x_ref, g_ref, dx2_ref, dm_ref, dp_ref, wm_ref, wp_ref, dx1_ref, dgain_ref):
        xv = x_ref[...]
        gain_v = g_ref[...]
        _, r = _rms(xv, gain_v)
        dh = _mm(dm_ref[...], wm_ref[...]) + _mm(dp_ref[...], wp_ref[...])
        dxn, dgain = _rms_bwd(xv, r, gain_v, dh)
        dx1_ref[...] = dx2_ref[...] + dxn

        @pl.when(pl.program_id(0) == 0)
        def _():
            dgain_ref[...] = jnp.zeros_like(dgain_ref)

        dgain_ref[...] += dgain

    tok = lambda c: pl.BlockSpec((tm, c), lambda i: (i, 0))
    return _pallas(
        body, "inproj_bwd", (t // tm,),
        [tok(d), _const((1, d)), tok(d), tok(M_COLS), tok(P_COLS), _resident(w_m.shape), _resident(w_p.shape)],
        [tok(d), _const((1, d))], [jax.ShapeDtypeStruct((t, d), F32), jax.ShapeDtypeStruct((1, d), F32)],
        (x1, gain, dx2, dm, dp, w_m, w_p), ("arbitrary",), carried)


def _adamw(quads, name, carried=None):
    k = len(quads)
    rows, cols = quads[0][0].shape
    tr, tc = rows, cols
    for cand in (512, 352, 256, 192, 128, 64):
        if rows % cand == 0 and rows > cand:
            tr = cand
            break
    if tr == rows and rows * cols > 512 * 1024 and cols % 256 == 0:
        tc = 256
    while k * 14 * tr * tc * 4 > VMEM_LIMIT // 2 and tr % 16 == 0:
        tr //= 2

    def body(*refs):
        for i in range(k):
            w_ref, g_ref, m_ref, v_ref = refs[4 * i:4 * i + 4]
            delta_ref, nm_ref, nv_ref = refs[4 * k + 3 * i:4 * k + 3 * i + 3]
            gv = g_ref[...]
            nm = ADAM_B1 * m_ref[...] + (1.0 - ADAM_B1) * gv
            nv = ADAM_B2 * v_ref[...] + (1.0 - ADAM_B2) * (gv * gv)
            m_hat = nm * (1.0 / (1.0 - ADAM_B1 ** ADAM_STEP))
            v_hat = nv * (1.0 / (1.0 - ADAM_B2 ** ADAM_STEP))
            delta_ref[...] = -ADAM_LR * (m_hat / (jnp.sqrt(v_hat) + ADAM_EPS) + ADAM_WD * w_ref[...])
            nm_ref[...] = nm
            nv_ref[...] = nv

    spec = pl.BlockSpec((tr, tc), lambda i, j: (i, j))
    shape = jax.ShapeDtypeStruct((rows, cols), F32)
    outs = _pallas(body, name, (rows // tr, cols // tc), [spec] * (4 * k), [spec] * (3 * k), [shape] * (3 * k),
                   [a for quad in quads for a in quad], ("arbitrary", "arbitrary"), carried)
    return [tuple(outs[3 * i:3 * i + 3]) for i in range(k)]


def _adamw_small(packed_grads, triples, segments):
    k = len(triples)

    def body(*refs):
        g_ref = refs[0]
        off = 0
        for i in range(k):
            w_ref, m_ref, v_ref = refs[1 + 3 * i:4 + 3 * i]
            g_out, delta_ref, nm_ref, nv_ref = refs[1 + 3 * k + 4 * i:5 + 3 * k + 4 * i]
            gv = g_ref[:, off:off + w_ref.shape[1]]
            nm = ADAM_B1 * m_ref[...] + (1.0 - ADAM_B1) * gv
            nv = ADAM_B2 * v_ref[...] + (1.0 - ADAM_B2) * (gv * gv)
            m_hat = nm * (1.0 / (1.0 - ADAM_B1 ** ADAM_STEP))
            v_hat = nv * (1.0 / (1.0 - ADAM_B2 ** ADAM_STEP))
            g_out[...] = gv
            delta_ref[...] = -ADAM_LR * (m_hat / (jnp.sqrt(v_hat) + ADAM_EPS) + ADAM_WD * w_ref[...])
            nm_ref[...] = nm
            nv_ref[...] = nv
            off += segments[i]

    vm = pl.BlockSpec(memory_space=pltpu.VMEM)
    outs = pl.pallas_call(
        body, name="adamw_small", in_specs=[vm] * (1 + 3 * k), out_specs=[vm] * (4 * k),
        out_shape=[jax.ShapeDtypeStruct(w.shape, F32) for w, _, _ in triples for _ in range(4)],
    )(packed_grads, *[a for triple in triples for a in triple])
    return [tuple(outs[4 * i:4 * i + 4]) for i in range(k)]


def _place():
    x, y, c = lax.axis_index("x"), lax.axis_index("y"), lax.axis_index("c")
    other_chips = [(1 - x, y), (x, 1 - y), (1 - x, 1 - y)]
    return x, y, c, other_chips


def _remote(src, dst, sems, send, recv, device):
    return pltpu.make_async_remote_copy(src_ref=src, dst_ref=dst, send_sem=sems.at[send], recv_sem=sems.at[recv],
                                        device_id=device, device_id_type=MESH_ID)


def _cast_shards(shards, out_dtypes):
    n = len(shards)

    def body(*refs):
        ins, outs, stage, sems = refs[:n], refs[n:2 * n], refs[2 * n:3 * n], refs[3 * n]
        x, y, _, _ = _place()
        me = 2 * x + y
        copies = []
        for w in range(n):
            stage[w][...] = ins[w][...].astype(out_dtypes[w])
            copies.append(pltpu.make_async_copy(stage[w], outs[w].at[me], sems.at[w]))
            copies[-1].start()
        for cp in copies:
            cp.wait()

    vm = pl.BlockSpec(memory_space=pltpu.VMEM)
    return pl.pallas_call(
        body, name="cast_shards", in_specs=[vm] * n, out_specs=[ANY] * n,
        out_shape=[jax.ShapeDtypeStruct((N_CHIPS,) + s.shape, dt) for s, dt in zip(shards, out_dtypes)],
        scratch_shapes=[pltpu.VMEM(s.shape, dt) for s, dt in zip(shards, out_dtypes)] + [pltpu.SemaphoreType.DMA((n,))],
        compiler_params=_params())(*shards)


BF16_ROWS = 16


def _split_rows(rows):
    return (rows // 2) % BF16_ROWS == 0


def _half_shape(rows, cols):
    return (rows // 2, cols) if _split_rows(rows) else (rows, cols // 2)


def _half(rows, cols, which):
    if _split_rows(rows):
        return (pl.ds(pl.multiple_of(which * (rows // 2), BF16_ROWS), rows // 2), slice(None))
    return (slice(None), pl.ds(pl.multiple_of(which * (cols // 2), 128), cols // 2))


def _gather_carried(bufs):
    n = len(bufs)

    def half(w, slot, which):
        _, rows, cols = bufs[w].shape
        return (slot,) + _half(rows, cols, which)

    def start(ins, outs, sems, base):
        x, y, c, other_chips = _place()
        me = 2 * x + y
        for w in range(n):
            mine = outs[w].at[half(w, me, c)]
            for p, (px, py) in enumerate(other_chips):
                _remote(mine, mine, sems, base + 12 * w + p, base + 12 * w + 3 + p, (px, py, c)).start()

    def finish(ins, outs, sems, base):
        x, y, c, other_chips = _place()
        me = 2 * x + y
        for w in range(n):
            for p, (px, py) in enumerate(other_chips):
                got = outs[w].at[half(w, 2 * px + py, c)]
                _remote(got, got, sems, base + 12 * w + p, base + 12 * w + 3 + p, (px, py, c)).wait_recv()
                _remote(got, got, sems, base + 12 * w + 6 + p, base + 12 * w + 9 + p, (x, y, 1 - c)).start()
        for w in range(n):
            mine = outs[w].at[half(w, me, c)]
            for p, (px, py) in enumerate(other_chips):
                got = outs[w].at[half(w, 2 * px + py, c)]
                theirs = outs[w].at[half(w, 2 * px + py, 1 - c)]
                _remote(got, theirs, sems, base + 12 * w + 6 + p, base + 12 * w + 9 + p, (x, y, 1 - c)).wait()
                _remote(mine, mine, sems, base + 12 * w + p, base + 12 * w + 3 + p, (px, py, c)).wait_send()

    shapes = [jax.ShapeDtypeStruct(b.shape, b.dtype) for b in bufs]
    return _Carried(bufs, shapes, {w: w for w in range(n)}, 12 * n, start, finish)


def _swap_carried(grads):
    n = len(grads)

    def copy(w, ins, outs, sems, base):
        x, y, c, _ = _place()
        _, rows, cols = grads[w].shape
        theirs = ins[w].at[(slice(None),) + _half(rows, cols, 1 - c)]
        return _remote(theirs, outs[w], sems, base + 2 * w, base + 2 * w + 1, (x, y, 1 - c))

    def start(ins, outs, sems, base):
        for w in range(n):
            copy(w, ins, outs, sems, base).start()

    def finish(ins, outs, sems, base):
        for w in range(n):
            copy(w, ins, outs, sems, base).wait()

    shapes = [jax.ShapeDtypeStruct((g.shape[0],) + _half_shape(*g.shape[1:]), F32) for g in grads]
    return _Carried(grads, shapes, {}, 2 * n, start, finish)


def _row_tile(rows):
    for cand in (512, 352, 256, 192, 128, 96, 64, 32, 16):
        if rows % cand == 0:
            return cand
    return rows


def _half_block_index(split_rows, tiles, i, core):
    return (core * tiles + i, 0) if split_rows else (i, core)


def _chip_partial(grad, other, place, name):
    nblk, hr, hc = other.shape
    by_rows = _split_rows(grad.shape[1])
    tr = _row_tile(hr)
    tiles = hr // tr

    def body(place_ref, g_ref, o_ref, sum_ref, sum_bf_ref):
        s = g_ref[...] + o_ref[...]
        sum_ref[...] = s
        sum_bf_ref[...] = _bf(s)

    grid_spec = pltpu.PrefetchScalarGridSpec(
        num_scalar_prefetch=1, grid=(nblk, tiles),
        in_specs=[pl.BlockSpec((None, tr, hc), lambda b, i, place_ref: (b,) + _half_block_index(by_rows, tiles, i, place_ref[1])),
                  pl.BlockSpec((None, tr, hc), lambda b, i, place_ref: (b, i, 0))],
        out_specs=[pl.BlockSpec((None, tr, hc), lambda b, i, place_ref: (b, i, 0))] * 2)
    return pl.pallas_call(body, name=name, grid_spec=grid_spec,
                          out_shape=[jax.ShapeDtypeStruct(other.shape, F32), jax.ShapeDtypeStruct(other.shape, BF16)],
                          compiler_params=_params(("arbitrary", "arbitrary")))(place, grad, other)


def _send_carried(partials):
    n = len(partials)

    def start(ins, outs, sems, base):
        x, y, c, other_chips = _place()
        me = 2 * x + y
        for w in range(n):
            for p, (px, py) in enumerate(other_chips):
                _remote(ins[w].at[2 * px + py], outs[w].at[me], sems, base + 6 * w + p, base + 6 * w + 3 + p, (px, py, c)).start()

    def finish(ins, outs, sems, base):
        x, y, c, other_chips = _place()
        for w in range(n):
            for p, (px, py) in enumerate(other_chips):
                _remote(ins[w].at[2 * px + py], outs[w].at[2 * px + py], sems, base + 6 * w + p, base + 6 * w + 3 + p,
                        (px, py, c)).wait()

    return _Carried(partials, [jax.ShapeDtypeStruct(p.shape, BF16) for p in partials], {}, 6 * n, start, finish)


def _chip_total(own, received, place, shape, name):
    nblk, hr, hc = own.shape
    by_rows = _split_rows(shape[0])
    tr = _row_tile(hr)
    tiles = hr // tr

    def body(place_ref, own_ref, r1_ref, r2_ref, r3_ref, out_ref):
        out_ref[...] = own_ref[...] + ((r1_ref[...].astype(F32) + r2_ref[...].astype(F32)) + r3_ref[...].astype(F32))

    def slot(k):
        return pl.BlockSpec((None, tr, hc), lambda i, place_ref: ((place_ref[0] + k) % N_CHIPS, i, 0))

    grid_spec = pltpu.PrefetchScalarGridSpec(
        num_scalar_prefetch=1, grid=(tiles,), in_specs=[slot(0), slot(1), slot(2), slot(3)],
        out_specs=pl.BlockSpec((tr, hc), lambda i, place_ref: _half_block_index(by_rows, tiles, i, place_ref[1])))
    return pl.pallas_call(body, name=name, grid_spec=grid_spec, out_shape=jax.ShapeDtypeStruct(tuple(shape), F32),
                          compiler_params=_params(("arbitrary",)))(place, own, received, received, received)


def _join_carried(totals):
    n = len(totals)

    def copy(w, outs, sems, base):
        x, y, c, _ = _place()
        mine = outs[w].at[_half(*totals[w].shape, c)]
        return _remote(mine, mine, sems, base + 2 * w, base + 2 * w + 1, (x, y, 1 - c))

    def start(ins, outs, sems, base):
        for w in range(n):
            copy(w, outs, sems, base).start()

    def finish(ins, outs, sems, base):
        for w in range(n):
            copy(w, outs, sems, base).wait()

    shapes = [jax.ShapeDtypeStruct(a.shape, F32) for a in totals]
    return _Carried(totals, shapes, {w: w for w in range(n)}, 2 * n, start, finish)


def _sum_devices(vec):
    rows, n = vec.shape

    def body(v_ref, out_ref, buf, send_sems, recv_sems):
        x, y, c, _ = _place()
        me = 4 * x + 2 * y + c
        buf[me] = v_ref[...]
        sends = []
        for k in range(1, N_DEV):
            peer = (1 - x if k & 4 else x, 1 - y if k & 2 else y, 1 - c if k & 1 else c)
            cp = pltpu.make_async_remote_copy(src_ref=v_ref, dst_ref=buf.at[me], send_sem=send_sems.at[k], recv_sem=recv_sems.at[k],
                                              device_id=peer, device_id_type=MESH_ID)
            cp.start()
            sends.append(cp)
        for cp in sends:
            cp.wait()
        total = buf[0]
        for dev in range(1, N_DEV):
            total = total + buf[dev]
        out_ref[...] = total

    vm = pl.BlockSpec(memory_space=pltpu.VMEM)
    return pl.pallas_call(
        body, name="sum_devices", in_specs=[vm], out_specs=vm, out_shape=jax.ShapeDtypeStruct((rows, n), F32),
        scratch_shapes=[pltpu.VMEM((N_DEV, rows, n), F32), pltpu.SemaphoreType.DMA((N_DEV,)), pltpu.SemaphoreType.DMA((N_DEV,))],
    )(vec)


def _rope_tables(positions):
    half = QK_ROPE // 2
    inv_freq = 1.0 / (ROPE_THETA ** (jnp.arange(half, dtype=F32) / half))
    ang = positions.astype(F32).reshape(-1, 1) * inv_freq
    cos, sin = jnp.cos(ang), jnp.sin(ang)
    t = ang.shape[0]
    ones, zeros = jnp.ones((t, QK_NOPE), F32), jnp.zeros((t, QK_NOPE), F32)
    pad = HEAD_PAD - QK_DIM
    cos_full = jnp.concatenate([ones, cos, cos, ones[:, :pad]], axis=1)
    sin_signed = jnp.concatenate([zeros, -sin, sin, zeros[:, :pad]], axis=1)
    return cos_full, sin_signed


def _partials(names, grads, from_sibling, place):
    return [_chip_partial(g, o, place, "chip_partial_" + n) for n, g, o in zip(names, grads, from_sibling)]


def _totals(names, grads, partials, received, place):
    return [_chip_total(pf, r, place, g.shape[1:], "chip_total_" + n) for n, g, (pf, _), r in zip(names, grads, partials, received)]


def _kernel_layouts(full):
    d = D_MODEL
    w_in = full["w_in"]
    o_kr = Q_LORA + KV_LORA
    o_xc = o_kr + QK_ROPE
    o_gb = o_xc + d
    o_gc = o_gb + d
    o_gl = o_gc + d
    k_rope_pad = jnp.pad(w_in[o_kr:o_xc], ((QK_NOPE, HEAD_PAD - QK_DIM), (0, 0)))
    w_m = jnp.concatenate([w_in[o_gb:o_gc], w_in[o_gl:]], axis=0)
    w_p = jnp.concatenate([w_in[o_xc:o_gb], w_in[o_gc:o_gl], w_in[:o_kr], k_rope_pad], axis=0)
    w_uq = jnp.pad(full["w_uq"].reshape(Q_LORA, N_HEADS, QK_DIM), ((0, 0), (0, 0), (0, HEAD_PAD - QK_DIM)))
    w_uk = jnp.pad(full["w_uk"].reshape(KV_LORA, N_HEADS, QK_NOPE), ((0, 0), (0, 0), (0, HEAD_PAD - QK_NOPE)))
    return {"w_m": w_m, "w_p": w_p, "w_uq": w_uq.reshape(Q_LORA, N_HEADS * HEAD_PAD), "w_uk": w_uk.reshape(KV_LORA, N_HEADS * HEAD_PAD),
            "w_uv": full["w_uv"], "w_uvt": full["w_uv"].T}


def _global_layouts(g):
    d = D_MODEL
    dm, dp = g["w_m"], g["w_p"]
    o_lat = 2 * d
    o_kr = o_lat + Q_LORA + KV_LORA + QK_NOPE
    w_in = jnp.concatenate([dp[o_lat:o_lat + Q_LORA + KV_LORA], dp[o_kr:o_kr + QK_ROPE], dp[:d], dm[:d], dp[d:o_lat], dm[d:]], axis=0)
    w_uq = g["w_uq"].reshape(Q_LORA, N_HEADS, HEAD_PAD)[:, :, :QK_DIM].reshape(Q_LORA, N_HEADS * QK_DIM)
    w_uk = g["w_uk"].reshape(KV_LORA, N_HEADS, HEAD_PAD)[:, :, :QK_NOPE].reshape(KV_LORA, N_HEADS * QK_NOPE)
    return {"w_in": w_in, "w_uq": w_uq, "w_uk": w_uk, "w_uv": g["w_uv"], "w_proj_attn": g["w_pa"], "w_proj_conv": g["w_pc"],
            "w_out": g["w_out"]}


def _col_blocks(a):
    r, c = a.shape
    return a.reshape(r, N_CHIPS, c // N_CHIPS).transpose(1, 0, 2)


def _from_col_blocks(a):
    n, r, c = a.shape
    return a.transpose(1, 0, 2).reshape(r, n * c)


COL_SHARDED = ("w_uq", "w_uk", "w_uv", "w_proj_attn")
TRANSPOSED = ("ffn1_w_gate", "ffn1_w_up", "ffn2_w_gate", "ffn2_w_up", "w_in")
SMALL = (("ffn1_norm", 1024), ("mix_norm", 1024), ("gate_bias", 2048), ("q_a_norm", 384), ("kv_a_norm", 256),
         ("q_head_norm", 128), ("k_head_norm", 128), ("ffn2_norm", 1024))
WEIGHT_ORDER = ("ffn1_norm", "ffn1_w_gate", "ffn1_w_up", "ffn1_w_down", "mix_norm", "w_in", "gate_bias", "q_a_norm", "w_uq",
                "kv_a_norm", "w_uk", "w_uv", "q_head_norm", "k_head_norm", "w_proj_attn", "conv_w", "w_proj_conv", "w_out",
                "ffn2_norm", "ffn2_w_gate", "ffn2_w_up", "ffn2_w_down")
MATRICES = ("ffn1_w_gate", "ffn1_w_up", "ffn1_w_down", "w_in", "w_uq", "w_uk", "w_uv", "w_proj_attn", "w_proj_conv", "w_out",
            "ffn2_w_gate", "ffn2_w_up", "ffn2_w_down")
GROUP_FFN1 = ("ffn1_w_gate", "ffn1_w_up", "ffn1_w_down")
GROUP_IN = ("w_in", "w_uq", "w_uk", "w_uv", "conv_w")
GROUP_MIX = ("w_proj_attn", "w_proj_conv", "w_out")
GROUP_FFN2 = ("ffn2_w_gate", "ffn2_w_up", "ffn2_w_down")
GROUP_MID = ("w_in", "w_uq", "w_uk", "w_uv", "w_proj_attn", "w_proj_conv", "w_out")


def _pad_lanes(a, n):
    return jnp.pad(a.reshape(1, -1), ((0, 0), (0, n - a.size)))


def kernel(x, positions, ffn1_norm, ffn1_w_gate, ffn1_w_up, ffn1_w_down, mix_norm, w_in, gate_bias, q_a_norm, w_uq, kv_a_norm, w_uk, w_uv, q_head_norm, k_head_norm, w_proj_attn, conv_w, w_proj_conv, w_out, ffn2_norm, ffn2_w_gate, ffn2_w_up, ffn2_w_down, loss_target, m_ffn1_norm, m_ffn1_w_gate, m_ffn1_w_up, m_ffn1_w_down, m_mix_norm, m_w_in, m_gate_bias, m_q_a_norm, m_w_uq, m_kv_a_norm, m_w_uk, m_w_uv, m_q_head_norm, m_k_head_norm, m_w_proj_attn, m_conv_w, m_w_proj_conv, m_w_out, m_ffn2_norm, m_ffn2_w_gate, m_ffn2_w_up, m_ffn2_w_down, v_ffn1_norm, v_ffn1_w_gate, v_ffn1_w_up, v_ffn1_w_down, v_mix_norm, v_w_in, v_gate_bias, v_q_a_norm, v_w_uq, v_kv_a_norm, v_w_uk, v_w_uv, v_q_head_norm, v_k_head_norm, v_w_proj_attn, v_conv_w, v_w_proj_conv, v_w_out, v_ffn2_norm, v_ffn2_w_gate, v_ffn2_w_up, v_ffn2_w_down):
    args = dict(locals())
    view = lambda n, a: a.T if n in TRANSPOSED else a
    weights = {n: view(n, args[n]) for n in WEIGHT_ORDER}
    moments_m = {n: view(n, args["m_" + n]) for n in WEIGHT_ORDER}
    moments_v = {n: view(n, args["v_" + n]) for n in WEIGHT_ORDER}
    nb, seq, d = x.shape
    t = nb * seq
    chip = (2 * lax.axis_index("x") + lax.axis_index("y")).astype(jnp.int32)
    place = jnp.stack([chip, lax.axis_index("c").astype(jnp.int32)])
    grads, delta, new_m, new_v = {}, {}, {}, {}

    def adamw(names, carried=None):
        results = _adamw([(weights[n], grads[n], moments_m[n], moments_v[n]) for n in names], "adamw_" + names[0], carried)
        for n, (dn, mn, vn) in zip(names, results):
            delta[n], new_m[n], new_v[n] = dn, mn, vn

    conv_rows = conv_w.shape[0]
    conv_shard = jnp.pad(conv_w, ((0, 16 - conv_rows), (0, 0)))
    bufs = dict(zip(MATRICES + ("conv_w",), _cast_shards([weights[n] for n in MATRICES] + [conv_shard],
                                                         [BF16] * len(MATRICES) + [F32])))
    blocks = dict(zip(GROUP_FFN1, _run(_gather_carried([bufs[n] for n in GROUP_FFN1]), "gather_ffn1")))
    p = {n: _pad_lanes(weights[n], size) for n, size in SMALL}
    cos, sin = _rope_tables(positions)
    x_tok = x.reshape(t, d)

    gather_in = _gather_carried([bufs[n] for n in GROUP_IN])
    x1, gate1, up1, act1 = _ffn_fwd(x_tok, p["ffn1_norm"], blocks["ffn1_w_gate"], blocks["ffn1_w_up"], blocks["ffn1_w_down"], None,
                                    "ffn1_fwd", gather_in)
    blocks.update(zip(GROUP_IN, gather_in.results))
    w = _kernel_layouts({"w_in": blocks["w_in"].reshape(-1, d), **{n: _from_col_blocks(blocks[n]) for n in ("w_uq", "w_uk", "w_uv")}})
    p["conv_w"] = _from_col_blocks(blocks["conv_w"])[:conv_rows]

    gather_mix = _gather_carried([bufs[n] for n in GROUP_MIX])
    h2b, big, lat, q, k, v, vt = _inproj_fwd(x1, p["mix_norm"], w["w_m"], w["w_p"], p["q_a_norm"], p["kv_a_norm"], p["q_head_norm"],
                                             p["k_head_norm"], w["w_uq"], w["w_uk"], w["w_uv"], w["w_uvt"], cos, sin, gather_mix)
    blocks.update(zip(GROUP_MIX, gather_mix.results))
    w_pa = _from_col_blocks(blocks["w_proj_attn"])
    w_pc, w_out_full = blocks["w_proj_conv"].reshape(-1, d), blocks["w_out"].reshape(-1, d)

    gather_ffn2 = _gather_carried([bufs[n] for n in GROUP_FFN2[:2]])
    o, lse = _attn_fwd(q, k, vt, seq, gather_ffn2)
    gather_down = _gather_carried([bufs[n] for n in GROUP_FFN2[2:]])
    x2 = _mix_fwd(x1, o, big, p["gate_bias"], p["conv_w"], w_pa, w_pc, w_out_full, seq, gather_down)
    wg2, wu2 = gather_ffn2.results
    wd2, = gather_down.results
    dx3, gate2, up2, act2, loss = _ffn_fwd(x2, p["ffn2_norm"], wg2, wu2, wd2, loss_target.reshape(t, d), "ffn2_fwd")

    dx2, dg_ffn2, hb2, dgate2, dup2, dyb2 = _ffn_bwd_x(x2, p["ffn2_norm"], dx3, gate2, up2, wg2, wu2, wd2, "ffn2_bwd")
    g_ffn2 = [_tn_matmul(dgate2, hb2, "ffn2_dw_gate"), _tn_matmul(dup2, hb2, "ffn2_dw_up"), _tn_matmul(act2, dyb2, "ffn2_dw_down")]
    swap = _swap_carried(g_ffn2)
    do, delta_o, dz, dm, dbias, dw_pa, dw_pc, dw_out = _mix_bwd(dx2, o, big, p["gate_bias"], p["conv_w"], w_pa, w_pc, w_out_full, seq,
                                                                swap)
    part = _partials(GROUP_FFN2, g_ffn2, swap.results, place)
    send = _send_carried([pb for _, pb in part])
    dq, dk, dv = _attn_bwd(q, k, v, do, lse, delta_o.reshape(N_HEADS // ATTN_BWD_HEADS, ATTN_BWD_HEADS, -1), seq, send)
    join = _join_carried(_totals(GROUP_FFN2, g_ffn2, part, send.results, place))
    dp, dw_uq, dw_uk, dw_uv, dqa, dkva, dqh, dkh, dcw = _prep_bwd(
        lat, big, dz, dq, dk, dv, p["q_a_norm"], p["kv_a_norm"], p["q_head_norm"], p["k_head_norm"], w["w_uq"], w["w_uk"],
        w["w_uv"], cos, sin, p["conv_w"], seq, join)
    grads.update(zip(GROUP_FFN2, join.results))

    gg = _global_layouts({"w_m": _tn_matmul(dm, h2b, "dw_in_m", split_k=2), "w_p": _tn_matmul(dp, h2b, "dw_in_p", split_k=2),
                          "w_uq": dw_uq, "w_uk": dw_uk, "w_uv": dw_uv, "w_pa": dw_pa, "w_pc": dw_pc, "w_out": dw_out})
    g_mid = [_col_blocks(gg[n]) if n in COL_SHARDED else gg[n].reshape(N_CHIPS, -1, gg[n].shape[-1]) for n in GROUP_MID]
    swap = _swap_carried(g_mid)
    dx1, dg_mix = _inproj_bwd(x1, p["mix_norm"], dx2, dm, dp, w["w_m"], w["w_p"], swap)
    part = _partials(GROUP_MID, g_mid, swap.results, place)
    send = _send_carried([pb for _, pb in part])
    grad_x, dg_ffn1, hb1, dgate1, dup1, dyb1 = _ffn_bwd_x(x_tok, p["ffn1_norm"], dx1, gate1, up1, blocks["ffn1_w_gate"],
                                                         blocks["ffn1_w_up"], blocks["ffn1_w_down"], "ffn1_bwd", send)

    small_grads = {"ffn1_norm": dg_ffn1, "mix_norm": dg_mix, "gate_bias": dbias, "q_a_norm": dqa, "kv_a_norm": dkva,
                   "q_head_norm": dqh, "k_head_norm": dkh, "ffn2_norm": dg_ffn2}
    packed = jnp.concatenate([small_grads[n] for n, _ in SMALL] + [dcw.reshape(1, -1), loss], axis=1)
    total = _sum_devices(packed.reshape(8, -1)).reshape(1, -1)
    n_small = sum(size for _, size in SMALL)
    conv_cols = conv_w.shape[1]
    conv_total = total[:, n_small:n_small + conv_rows * d].reshape(conv_rows, d)
    grads["conv_w"] = lax.dynamic_slice_in_dim(conv_total, chip * conv_cols, conv_cols, axis=1)
    loss_total = total[0, n_small + conv_rows * d]

    join = _join_carried(_totals(GROUP_MID, g_mid, part, send.results, place))
    g_gate = _tn_matmul(dgate1, hb1, "ffn1_dw_gate", carried=join)
    grads.update(zip(GROUP_MID, join.results))
    swap_gate = _swap_carried([g_gate])
    g_up = _tn_matmul(dup1, hb1, "ffn1_dw_up", carried=swap_gate)
    part_gate = _partials(GROUP_FFN1[:1], [g_gate], swap_gate.results, place)
    send_gate, swap_up = _send_carried([part_gate[0][1]]), _swap_carried([g_up])
    g_down = _tn_matmul(act1, dyb1, "ffn1_dw_down", carried=_both(send_gate, swap_up))
    join_gate = _join_carried(_totals(GROUP_FFN1[:1], [g_gate], part_gate, send_gate.results, place))
    part_up = _partials(GROUP_FFN1[1:2], [g_up], swap_up.results, place)
    send_up, swap_down = _send_carried([part_up[0][1]]), _swap_carried([g_down])
    adamw(GROUP_FFN2, _both(_both(send_up, swap_down), join_gate))
    grads["ffn1_w_gate"] = join_gate.results[0]
    join_up = _join_carried(_totals(GROUP_FFN1[1:2], [g_up], part_up, send_up.results, place))
    part_down = _partials(GROUP_FFN1[2:], [g_down], swap_down.results, place)
    send_down = _send_carried([part_down[0][1]])
    adamw(("w_in",), _both(send_down, join_up))
    grads["ffn1_w_up"] = join_up.results[0]
    join_down = _join_carried(_totals(GROUP_FFN1[2:], [g_down], part_down, send_down.results, place))
    adamw(GROUP_FFN1[:2], join_down)
    grads["ffn1_w_down"] = join_down.results[0]
    adamw(GROUP_FFN1[2:])
    for n in GROUP_MID[1:] + ("conv_w",):
        adamw((n,))

    row = lambda a: a.reshape(1, -1)
    small = _adamw_small(total, [(row(weights[n]), row(moments_m[n]), row(moments_v[n])) for n, _ in SMALL], [size for _, size in SMALL])
    for (n, _), (gn, dn, mn, vn) in zip(SMALL, small):
        grads[n], delta[n], new_m[n], new_v[n] = gn.reshape(-1), dn.reshape(-1), mn.reshape(-1), vn.reshape(-1)

    return (loss_total, grad_x.reshape(nb, seq, d), *[view(n, src[n]) for src in (grads, delta, new_m, new_v) for n in WEIGHT_ORDER])
```

```python
import functools

import jax
import jax.numpy as jnp
from jax import lax
from jax.experimental import pallas as pl
from jax.experimental.pallas import tpu as pltpu

F32 = jnp.float32
BF16 = jnp.bfloat16

D_MODEL = 1024
N_HEADS = 8
QK_NOPE = 64
QK_ROPE = 32
QK_DIM = QK_NOPE + QK_ROPE
V_DIM = 64
HEAD_PAD = 128
Q_LORA = 384
KV_LORA = 256
ROPE_THETA = 10000.0
NORM_EPS = 1e-6
ATTN_SCALE = QK_DIM ** -0.5
MASK_VALUE = -1e30
N_CHIPS = 4
N_DEV = 8

ADAM_LR = 0.001
ADAM_B1 = 0.9
ADAM_B2 = 0.999
ADAM_EPS = 1e-08
ADAM_WD = 0.01
ADAM_STEP = 10

TOKEN_TILE = 256
ATTN_TILE = 512
TN_TILE = 512
VMEM_LIMIT = 56 * 1024 * 1024

M_COLS = 3 * D_MODEL
P_COLS = 2 * D_MODEL + Q_LORA + KV_LORA + HEAD_PAD
BIG_COLS = 5 * D_MODEL
LAT_COLS = Q_LORA + KV_LORA + HEAD_PAD

MESH_ID = pl.DeviceIdType.MESH
ANY = pl.BlockSpec(memory_space=pl.ANY)


def _params(semantics=None):
    return pltpu.CompilerParams(dimension_semantics=semantics, vmem_limit_bytes=VMEM_LIMIT)


class _Carried:
    def __init__(self, operands, out_shapes, aliases, n_sems, start, finish):
        self.operands, self.out_shapes, self.aliases, self.n_sems = list(operands), list(out_shapes), dict(aliases), n_sems
        self.start, self.finish = start, finish
        self.results = None


def _both(a, b):
    na, nao = len(a.operands), len(a.out_shapes)

    def start(ins, outs, sems, base):
        a.start(ins[:na], outs[:nao], sems, base)
        b.start(ins[na:], outs[nao:], sems, base + a.n_sems)

    def finish(ins, outs, sems, base):
        a.finish(ins[:na], outs[:nao], sems, base)
        b.finish(ins[na:], outs[nao:], sems, base + a.n_sems)

    aliases = dict(a.aliases)
    aliases.update({na + i: nao + o for i, o in b.aliases.items()})
    both = _Carried(a.operands + b.operands, a.out_shapes + b.out_shapes, aliases, a.n_sems + b.n_sems, start, finish)
    both.parts = (a, b)
    return both


def _set_results(carried, results):
    carried.results = list(results)
    if hasattr(carried, "parts"):
        a, b = carried.parts
        _set_results(a, results[:len(a.out_shapes)])
        _set_results(b, results[len(a.out_shapes):])


def _pallas(body, name, grid, in_specs, out_specs, out_shape, args, semantics, carried=None):
    if carried is None:
        return pl.pallas_call(body, name=name, grid=grid, in_specs=in_specs, out_specs=out_specs, out_shape=out_shape,
                              compiler_params=_params(semantics))(*args)
    n_in, n_out, n_ci, n_co = len(in_specs), len(out_specs), len(carried.operands), len(carried.out_shapes)

    def wrapped(*refs):
        ins, c_ins = refs[:n_in], refs[n_in:n_in + n_ci]
        outs, c_outs = refs[n_in + n_ci:n_in + n_ci + n_out], refs[n_in + n_ci + n_out:n_in + n_ci + n_out + n_co]
        sems = refs[-1]
        first = pl.program_id(0) == 0
        last = pl.program_id(0) == grid[0] - 1
        for axis in range(1, len(grid)):
            first = jnp.logical_and(first, pl.program_id(axis) == 0)
            last = jnp.logical_and(last, pl.program_id(axis) == grid[axis] - 1)

        @pl.when(first)
        def _():
            carried.start(c_ins, c_outs, sems, 0)

        body(*ins, *outs)

        @pl.when(last)
        def _():
            carried.finish(c_ins, c_outs, sems, 0)

    results = pl.pallas_call(
        wrapped, name=name, grid=grid, in_specs=list(in_specs) + [ANY] * n_ci, out_specs=list(out_specs) + [ANY] * n_co,
        out_shape=list(out_shape) + carried.out_shapes,
        input_output_aliases={n_in + i: n_out + o for i, o in carried.aliases.items()},
        scratch_shapes=[pltpu.SemaphoreType.DMA((carried.n_sems,))], compiler_params=_params(semantics))(*args, *carried.operands)
    _set_results(carried, results[n_out:])
    return results[:n_out]


def _run(carried, name):
    n_ci, n_co = len(carried.operands), len(carried.out_shapes)

    def body(*refs):
        carried.start(refs[:n_ci], refs[n_ci:n_ci + n_co], refs[-1], 0)
        carried.finish(refs[:n_ci], refs[n_ci:n_ci + n_co], refs[-1], 0)

    results = pl.pallas_call(body, name=name, in_specs=[ANY] * n_ci, out_specs=[ANY] * n_co, out_shape=carried.out_shapes,
                             input_output_aliases=carried.aliases,
                             scratch_shapes=[pltpu.SemaphoreType.DMA((carried.n_sems,))])(*carried.operands)
    _set_results(carried, results)
    return carried.results


def _resident(shape):
    nd = len(shape)
    return pl.BlockSpec(shape, lambda *_: (0,) * nd, pipeline_mode=pl.Buffered(1))


def _const(shape):
    nd = len(shape)
    return pl.BlockSpec(shape, lambda *_: (0,) * nd)


def _mm(a, b):
    return jnp.dot(a, b, preferred_element_type=F32)


def _mm_nt(a, b):
    return lax.dot_general(a, b, (((1,), (1,)), ((), ())), preferred_element_type=F32)


def _mm_tn(a, b):
    return lax.dot_general(a, b, (((0,), (0,)), ((), ())), preferred_element_type=F32)


def _bf(a):
    return a.astype(BF16)


def _sigmoid(a):
    return 1.0 / (1.0 + jnp.exp(-a))


def _rms(x, gain, n=None):
    n = x.shape[-1] if n is None else n
    r = lax.rsqrt(jnp.sum(x * x, axis=-1, keepdims=True) * (1.0 / n) + NORM_EPS)
    return (x * r) * gain, r


def _rms_bwd(x, r, gain, dh, n=None):
    n = x.shape[-1] if n is None else n
    u = dh * gain
    dx = r * u - x * ((r * r * r) * (jnp.sum(u * x, axis=-1, keepdims=True) * (1.0 / n)))
    dgain = jnp.sum(dh * (x * r), axis=0, keepdims=True)
    return dx, dgain


ROPE_HALF = QK_ROPE // 2


def _rope(t, rope):
    cos, s_lo, s_hi = rope
    return t * cos + pltpu.roll(t, HEAD_PAD - ROPE_HALF, 1) * s_lo + pltpu.roll(t, ROPE_HALF, 1) * s_hi


def _rope_bwd(dt, rope):
    cos, s_lo, s_hi = rope
    return dt * cos + pltpu.roll(dt * s_lo, ROPE_HALF, 1) + pltpu.roll(dt * s_hi, HEAD_PAD - ROPE_HALF, 1)


def _shift_down(u, prev8, k):
    s = pltpu.roll(u, k, 0)
    p = pltpu.roll(prev8, k, 0)
    row = lax.broadcasted_iota(jnp.int32, prev8.shape, 0)
    top = jnp.where(row < k, p, s[:8])
    return jnp.concatenate([top, s[8:]], axis=0)


def _shift_up(d, next8, k):
    tm = d.shape[0]
    s = pltpu.roll(d, tm - k, 0)
    n = pltpu.roll(next8, 8 - k, 0)
    row = lax.broadcasted_iota(jnp.int32, next8.shape, 0)
    bot = jnp.where(row >= 8 - k, n, s[tm - 8:])
    return jnp.concatenate([s[:tm - 8], bot], axis=0)


def _ffn_fwd(x, gain, wg, wu, wd, target, name, carried=None):
    t, d = x.shape
    nb, f, _ = wg.shape
    tm = TOKEN_TILE
    with_loss = target is not None

    def body(*refs):
        if with_loss:
            x_ref, g_ref, wg_ref, wu_ref, wd_ref, t_ref, out_ref, gate_ref, up_ref, act_ref, loss_ref = refs
        else:
            x_ref, g_ref, wg_ref, wu_ref, wd_ref, out_ref, gate_ref, up_ref, act_ref = refs
        xv = x_ref[...]
        h, _ = _rms(xv, g_ref[...])
        hb = _bf(h)
        y = jnp.zeros((tm, d), F32)
        nxt = (_mm_nt(hb, wg_ref[0]), _mm_nt(hb, wu_ref[0]))
        for j in range(nb):
            gate, up = nxt
            if j + 1 < nb:
                nxt = (_mm_nt(hb, wg_ref[j + 1]), _mm_nt(hb, wu_ref[j + 1]))
            act = _bf((gate * _sigmoid(gate)) * up)
            y = y + _mm(act, wd_ref[j])
            gate_ref[j] = _bf(gate)
            up_ref[j] = _bf(up)
            act_ref[j] = act
        out = xv + 0.5 * y
        if with_loss:
            err = out - t_ref[...]
            out_ref[...] = err * (1.0 / d)

            @pl.when(pl.program_id(0) == 0)
            def _():
                loss_ref[...] = jnp.zeros_like(loss_ref)

            part = jnp.sum(jnp.sum(err * err, axis=1, keepdims=True), axis=0, keepdims=True)
            loss_ref[...] += jnp.broadcast_to(part * (0.5 / d), loss_ref.shape)
        else:
            out_ref[...] = out

    tok = pl.BlockSpec((tm, d), lambda i: (i, 0))
    blk = pl.BlockSpec((nb, tm, f), lambda i: (0, i, 0))
    in_specs = [tok, _const((1, d)), _resident(wg.shape), _resident(wu.shape), _resident(wd.shape)]
    args = [x, gain, wg, wu, wd]
    out_shape = [jax.ShapeDtypeStruct((t, d), F32)] + [jax.ShapeDtypeStruct((nb, t, f), BF16)] * 3
    out_specs = [tok, blk, blk, blk]
    if with_loss:
        in_specs.append(tok)
        args.append(target)
        out_shape.append(jax.ShapeDtypeStruct((1, 128), F32))
        out_specs.append(_const((1, 128)))
    return _pallas(body, name, (t // tm,), in_specs, out_specs, out_shape, args, ("arbitrary",), carried)


def _ffn_bwd_x(x, gain, dout, gate, up, wg, wu, wd, name, carried=None):
    t, d = x.shape
    nb, f, _ = wg.shape
    tm = TOKEN_TILE

    def body(x_ref, g_ref, dout_ref, gate_ref, up_ref, wg_ref, wu_ref, wd_ref,
             dx_ref, dgain_ref, hb_ref, dgate_ref, dup_ref, dyb_ref):
        xv = x_ref[...]
        gain_v = g_ref[...]
        h, r = _rms(xv, gain_v)
        hb_ref[...] = _bf(h)
        dout_v = dout_ref[...]
        dyb = _bf(0.5 * dout_v)
        dyb_ref[...] = dyb
        dh = jnp.zeros((tm, d), F32)
        nxt = _mm_nt(dyb, wd_ref[0])
        for j in range(nb):
            dact = nxt
            if j + 1 < nb:
                nxt = _mm_nt(dyb, wd_ref[j + 1])
            gt = gate_ref[j].astype(F32)
            uv = up_ref[j].astype(F32)
            s = _sigmoid(gt)
            dup = _bf(dact * (gt * s))
            dgate = _bf((dact * uv) * (s * (1.0 + gt * (1.0 - s))))
            dh = dh + _mm(dgate, wg_ref[j]) + _mm(dup, wu_ref[j])
            dgate_ref[j] = dgate
            dup_ref[j] = dup
        dxn, dgain = _rms_bwd(xv, r, gain_v, dh)
        dx_ref[...] = dout_v + dxn

        @pl.when(pl.program_id(0) == 0)
        def _():
            dgain_ref[...] = jnp.zeros_like(dgain_ref)

        dgain_ref[...] += dgain

    tok = pl.BlockSpec((tm, d), lambda i: (i, 0))
    blk = pl.BlockSpec((nb, tm, f), lambda i: (0, i, 0))
    return _pallas(
        body, name, (t // tm,),
        [tok, _const((1, d)), tok, blk, blk, _resident(wg.shape), _resident(wu.shape), _resident(wd.shape)],
        [tok, _const((1, d)), tok, blk, blk, tok],
        [jax.ShapeDtypeStruct((t, d), F32), jax.ShapeDtypeStruct((1, d), F32), jax.ShapeDtypeStruct((t, d), BF16),
         jax.ShapeDtypeStruct((nb, t, f), BF16), jax.ShapeDtypeStruct((nb, t, f), BF16), jax.ShapeDtypeStruct((t, d), BF16)],
        (x, gain, dout, gate, up, wg, wu, wd), ("arbitrary",), carried)


def _tn_matmul(a, b, name, split_k=1, carried=None):
    t = a.shape[-2]
    k = a.shape[-1]
    n = b.shape[-1]
    tt = min(TN_TILE, t)
    nt = t // tt

    def body(a_ref, b_ref, o_ref):
        @pl.when(pl.program_id(1) == 0)
        def _():
            o_ref[...] = jnp.zeros_like(o_ref)

        o_ref[...] += _mm_tn(a_ref[...], b_ref[...])

    if split_k > 1:
        assert a.ndim == 2 and b.ndim == 2 and k % (split_k * 128) == 0
        tk = k // split_k
        g = split_k
        a_spec = pl.BlockSpec((tt, tk), lambda gi, ti: (ti, gi))
        b_spec = pl.BlockSpec((tt, n), lambda gi, ti: (ti, 0))
        o_spec = pl.BlockSpec((tk, n), lambda gi, ti: (gi, 0))
        out_shape = jax.ShapeDtypeStruct((k, n), F32)
    else:
        g = a.shape[0] if a.ndim == 3 else b.shape[0]
        a_spec = (pl.BlockSpec((None, tt, k), lambda gi, ti: (gi, ti, 0)) if a.ndim == 3
                  else pl.BlockSpec((tt, k), lambda gi, ti: (ti, 0)))
        b_spec = (pl.BlockSpec((None, tt, n), lambda gi, ti: (gi, ti, 0)) if b.ndim == 3
                  else pl.BlockSpec((tt, n), lambda gi, ti: (ti, 0)))
        o_spec = pl.BlockSpec((None, k, n), lambda gi, ti: (gi, 0, 0))
        out_shape = jax.ShapeDtypeStruct((g, k, n), F32)
    return _pallas(body, name, (g, nt), [a_spec, b_spec], [o_spec], [out_shape], (a, b), ("arbitrary", "arbitrary"), carried)[0]


ROW_QKV, ROW_KR, ROW_XC = 0, Q_LORA + KV_LORA, Q_LORA + KV_LORA + QK_ROPE
ROW_GB, ROW_GC, ROW_GL = ROW_XC + D_MODEL, ROW_XC + 2 * D_MODEL, ROW_XC + 3 * D_MODEL
BIG_FROM_ROWS = ((0, ROW_GB, D_MODEL), (D_MODEL, ROW_GL, 2 * D_MODEL), (3 * D_MODEL, ROW_XC, D_MODEL), (4 * D_MODEL, ROW_GC, D_MODEL))


def _inproj_fwd(x1, gain, w_in, w_kr, qa_gain, kva_gain, qh_gain, kh_gain, w_uq, w_uk, w_uv, w_uvt, rope, carried=None):
    t, d = x1.shape
    tm = TOKEN_TILE
    chunk = 512
    chunks = []
    for col, row, size in BIG_FROM_ROWS:
        chunks += [(col + o, row + o, chunk) for o in range(0, size, chunk)]
    of_head = [[c for k, c in enumerate(chunks) if k * N_HEADS // len(chunks) == hd] for hd in range(N_HEADS)]

    def body(x_ref, g_ref, win_ref, wkr_ref, qa_ref, kva_ref, qh_ref, kh_ref, wuq_ref, wuk_ref, wuv_ref, wuvt_ref, cos_ref, slo_ref,
             shi_ref, hb_ref, big_ref, lat_ref, q_ref, k_ref, v_ref, vt_ref):
        h, _ = _rms(x_ref[...], g_ref[...])
        hb = _bf(h)
        hb_ref[...] = hb
        k_rope = _mm_nt(hb, wkr_ref[...])
        lat = jnp.concatenate([_mm_nt(hb, win_ref[ROW_QKV:ROW_KR, :]), k_rope], axis=1)
        lat_ref[...] = lat
        cq, _ = _rms(lat[:, :Q_LORA], qa_ref[...])
        ckv, _ = _rms(lat[:, Q_LORA:Q_LORA + KV_LORA], kva_ref[...])
        cqb = _bf(cq)
        ckvb = _bf(ckv)
        rope_v = (cos_ref[...], slo_ref[...], shi_ref[...])
        q_all = _mm(cqb, wuq_ref[...])
        k_all = _mm(ckvb, wuk_ref[...])
        v_ref[...] = _bf(_mm(ckvb, wuv_ref[...]))
        vt_all = _mm_nt(wuvt_ref[...], ckvb)
        for hd in range(N_HEADS):
            for col, row, size in of_head[hd]:
                big_ref[:, col:col + size] = _mm_nt(hb, win_ref[row:row + size, :])
            lanes = slice(hd * HEAD_PAD, (hd + 1) * HEAD_PAD)
            qn, _ = _rms(q_all[:, lanes], qh_ref[...], QK_DIM)
            q_ref[hd] = _bf(_rope(qn, rope_v))
            kn, _ = _rms(k_all[:, lanes] + k_rope, kh_ref[...], QK_DIM)
            k_ref[hd] = _bf(_rope(kn, rope_v))
            vt_ref[hd] = _bf(vt_all[hd * V_DIM:(hd + 1) * V_DIM])

    tok = lambda c: pl.BlockSpec((tm, c), lambda i: (i, 0))
    head = lambda c: pl.BlockSpec((N_HEADS, tm, c), lambda i: (0, i, 0))
    return _pallas(
        body, "inproj_fwd", (t // tm,),
        [tok(d), _const((1, d)), _resident(w_in.shape), _resident(w_kr.shape), _const((1, Q_LORA)), _const((1, KV_LORA)),
         _const((1, HEAD_PAD)), _const((1, HEAD_PAD)), _resident(w_uq.shape), _resident(w_uk.shape),
         _resident(w_uv.shape), _resident(w_uvt.shape), tok(HEAD_PAD), tok(HEAD_PAD), tok(HEAD_PAD)],
        [tok(d), tok(BIG_COLS), tok(LAT_COLS), head(HEAD_PAD), head(HEAD_PAD), tok(N_HEADS * V_DIM),
         pl.BlockSpec((N_HEADS, V_DIM, tm), lambda i: (0, 0, i))],
        [jax.ShapeDtypeStruct((t, d), BF16), jax.ShapeDtypeStruct((t, BIG_COLS), F32),
         jax.ShapeDtypeStruct((t, LAT_COLS), F32), jax.ShapeDtypeStruct((N_HEADS, t, HEAD_PAD), BF16),
         jax.ShapeDtypeStruct((N_HEADS, t, HEAD_PAD), BF16), jax.ShapeDtypeStruct((t, N_HEADS * V_DIM), BF16),
         jax.ShapeDtypeStruct((N_HEADS, V_DIM, t), BF16)],
        (x1, gain, w_in, w_kr, qa_gain, kva_gain, qh_gain, kh_gain, w_uq, w_uk, w_uv, w_uvt, *rope), ("arbitrary",), carried)


EXP2_SCALE = ATTN_SCALE * 1.4426950408889634


def _diagonal_keep(tk, tq):
    return lax.broadcasted_iota(jnp.int32, (tk, tq), 0) <= lax.broadcasted_iota(jnp.int32, (tk, tq), 1)


def _attn_fwd(q, k, vt, seq, carried=None):
    _, t, _ = q.shape
    nseq = t // seq
    tq = tk = ATTN_TILE
    nq = seq // tq

    def body(q_ref, k_ref, vt_ref, o_ref, lse_ref):
        i = pl.program_id(1)
        qs = [q_ref[h] for h in range(N_HEADS)]
        keep = _diagonal_keep(tk, tq)

        def scores(h, k0):
            return _mm_nt(k_ref[h, pl.ds(k0, tk), :], qs[h])

        def update(h, st, state, k0, diagonal):
            m, l, acc = state
            if diagonal:
                st = jnp.where(keep, st, MASK_VALUE)
            m_new = jnp.maximum(m, jnp.max(st, axis=0, keepdims=True))
            pt = jnp.exp2((st - m_new) * EXP2_SCALE)
            alpha = jnp.exp2((m - m_new) * EXP2_SCALE)
            l_new = alpha * l + jnp.sum(pt, axis=0, keepdims=True)
            return m_new, l_new, alpha * acc + _mm(vt_ref[h, :, pl.ds(k0, tk)], _bf(pt))

        def tiles(states, k0, diagonal):
            st, new = scores(0, k0), []
            for h in range(N_HEADS):
                st_next = scores(h + 1, k0) if h + 1 < N_HEADS else None
                new.append(update(h, st, states[h], k0, diagonal))
                st = st_next
            return tuple(new)

        init = tuple((jnp.full((1, tq), MASK_VALUE, F32), jnp.zeros((1, tq), F32), jnp.zeros((V_DIM, tq), F32))
                     for _ in range(N_HEADS))
        states = lax.fori_loop(0, i, lambda j, s: tiles(s, pl.multiple_of(j * tk, tk), False), init)
        states = tiles(states, pl.multiple_of(i * tk, tk), True)
        outs = []
        for h in range(N_HEADS):
            m, l, acc = states[h]
            outs.append((acc / l).T)
            lse_ref[h] = m * EXP2_SCALE + jnp.log2(l)
        o_ref[...] = _bf(jnp.concatenate(outs, axis=-1))

    return _pallas(
        body, "attn_fwd", (nseq, nq),
        [pl.BlockSpec((N_HEADS, tq, HEAD_PAD), lambda b, i: (0, b * nq + i, 0)),
         pl.BlockSpec((N_HEADS, seq, HEAD_PAD), lambda b, i: (0, b, 0)),
         pl.BlockSpec((N_HEADS, V_DIM, seq), lambda b, i: (0, 0, b))],
        [pl.BlockSpec((tq, N_HEADS * V_DIM), lambda b, i: (b * nq + i, 0)),
         pl.BlockSpec((N_HEADS, 1, tq), lambda b, i: (0, 0, b * nq + i))],
        [jax.ShapeDtypeStruct((t, N_HEADS * V_DIM), BF16), jax.ShapeDtypeStruct((N_HEADS, 1, t), F32)],
        (q, k, vt), ("arbitrary", "arbitrary"), carried)


ATTN_BWD_HEADS = 4


def _attn_bwd(q, k, v, do, lse, delta, seq, carried=None):
    _, t, _ = q.shape
    nseq = t // seq
    tq = tk = ATTN_TILE
    n = seq // tq
    hb = ATTN_BWD_HEADS

    def body(q_ref, k_ref, v_ref, do_ref, lse_ref, delta_ref, dq_ref, dk_ref, dv_ref):
        dq_ref[...] = jnp.zeros_like(dq_ref)
        dk_ref[...] = jnp.zeros_like(dk_ref)
        dv_ref[...] = jnp.zeros_like(dv_ref)
        keep = _diagonal_keep(tk, tq)

        def tile(h, k0, q0, diagonal):
            kj = k_ref[h, pl.ds(k0, tk), :]
            qi = q_ref[h, pl.ds(q0, tq), :]
            doi = _bf(do_ref[pl.ds(q0, tq), h * V_DIM:(h + 1) * V_DIM])
            st = _mm_nt(kj, qi)
            if diagonal:
                st = jnp.where(keep, st, MASK_VALUE)
            pt = jnp.exp2(st * EXP2_SCALE - lse_ref[h, :, pl.ds(q0, tq)])
            dv_ref[pl.ds(k0, tk), h * V_DIM:(h + 1) * V_DIM] += _mm(_bf(pt), doi)
            dpt = _mm_nt(v_ref[pl.ds(k0, tk), h * V_DIM:(h + 1) * V_DIM], doi)
            dst = _bf((pt * (dpt - delta_ref[pl.ds(h, 1), pl.ds(q0, tq)])) * ATTN_SCALE)
            dk_ref[h, pl.ds(k0, tk), :] += _mm(dst, qi)
            dq_ref[h, pl.ds(q0, tq), :] += _mm_tn(dst, kj)

        def kv_step(j, _):
            k0 = pl.multiple_of(j * tk, tk)
            for h in range(hb):
                tile(h, k0, k0, True)

            def q_step(i, _):
                q0 = pl.multiple_of(i * tq, tq)
                for h in range(hb):
                    tile(h, k0, q0, False)
                return 0

            lax.fori_loop(j + 1, n, q_step, 0)
            return 0

        lax.fori_loop(0, n, kv_step, 0)

    hspec = lambda c: pl.BlockSpec((hb, seq, c), lambda b, g: (g, b, 0))
    cols = pl.BlockSpec((seq, hb * V_DIM), lambda b, g: (b, g))
    return _pallas(
        body, "attn_bwd", (nseq, N_HEADS // hb),
        [hspec(HEAD_PAD), hspec(HEAD_PAD), cols, cols,
         pl.BlockSpec((hb, 1, seq), lambda b, g: (g, 0, b)), pl.BlockSpec((None, hb, seq), lambda b, g: (g, 0, b))],
        [hspec(HEAD_PAD), hspec(HEAD_PAD), cols],
        [jax.ShapeDtypeStruct((N_HEADS, t, HEAD_PAD), F32), jax.ShapeDtypeStruct((N_HEADS, t, HEAD_PAD), F32),
         jax.ShapeDtypeStruct((t, N_HEADS * V_DIM), F32)],
        (q, k, v, do, lse, delta), ("arbitrary", "arbitrary"), carried)


def _mixer_values(o_ref, gb_ref, gla_ref, glb_ref, xc_ref, gc_ref, xcp_ref, gcp_ref, bias_ref, cw_ref, wpa_ref, wpc_ref,
                  first_of_seq, early=None):
    y_a = _mm(o_ref[...], wpa_ref[...])
    extra = early() if early is not None else None
    gb = gb_ref[...]
    u = gc_ref[...] * xc_ref[...]
    u_prev = jnp.where(first_of_seq, 0.0, gcp_ref[...] * xcp_ref[...])
    cw = cw_ref[...]
    z = cw[2:3] * u + cw[1:2] * _shift_down(u, u_prev, 1) + cw[0:1] * _shift_down(u, u_prev, 2)
    gbz = _bf(gb * z)
    y_b = _mm(gbz, wpc_ref[...])
    bias = bias_ref[...]
    gate_a = _sigmoid(gla_ref[...] + bias[:, :D_MODEL])
    gate_b = _sigmoid(glb_ref[...] + bias[:, D_MODEL:])
    merged = _bf(gate_a * y_a + gate_b * y_b)
    return gb, u, z, gbz, y_a, y_b, gate_a, gate_b, merged, extra


def _mixer_specs(tm, seq):
    d = D_MODEL
    tok = pl.BlockSpec((tm, d), lambda i: (i, 0))
    col = lambda c: pl.BlockSpec((tm, d), lambda i: (i, c))
    prev = lambda c: pl.BlockSpec((8, d), lambda i: (jnp.maximum(i * (tm // 8) - 1, 0), c))
    o_spec = pl.BlockSpec((tm, N_HEADS * V_DIM), lambda i: (i, 0))
    fwd_specs = [o_spec, col(0), col(1), col(2), col(3), col(4), prev(3), prev(4), _const((1, 2 * d)), _const((3, d)),
                 _resident((N_HEADS * V_DIM, d)), _resident((d, d)), _resident((d, d))]
    return tok, fwd_specs


def _mix_fwd(x1, o, big, gate_bias, conv_w, w_pa, w_pc, w_out, seq, carried=None):
    t, d = x1.shape
    tm = TOKEN_TILE
    tiles_per_seq = seq // tm

    def body(x_ref, o_ref, gb_ref, gla_ref, glb_ref, xc_ref, gc_ref, xcp_ref, gcp_ref, bias_ref, cw_ref, wpa_ref, wpc_ref,
             wout_ref, x2_ref):
        first = pl.program_id(0) % tiles_per_seq == 0
        merged = _mixer_values(o_ref, gb_ref, gla_ref, glb_ref, xc_ref, gc_ref, xcp_ref, gcp_ref, bias_ref, cw_ref, wpa_ref,
                               wpc_ref, first)[-2]
        x2_ref[...] = x_ref[...] + _mm(merged, wout_ref[...])

    tok, fwd_specs = _mixer_specs(tm, seq)
    return _pallas(body, "mix_fwd", (t // tm,), [tok] + fwd_specs, [tok], [jax.ShapeDtypeStruct((t, d), F32)],
                   (x1, o, big, big, big, big, big, big, big, gate_bias, conv_w, w_pa, w_pc, w_out), ("arbitrary",), carried)[0]


def _mix_bwd(dx2, o, big, gate_bias, conv_w, w_pa, w_pc, w_out, seq, carried=None):
    t, d = dx2.shape
    tm = TOKEN_TILE
    tiles_per_seq = seq // tm
    hv = N_HEADS * V_DIM

    def body(dx_ref, o_ref, gb_ref, gla_ref, glb_ref, xc_ref, gc_ref, xcp_ref, gcp_ref, bias_ref, cw_ref, wpa_ref, wpc_ref,
             wout_ref, do_ref, delta_ref, dz_ref, dm_ref, dbias_ref, dwpa_ref, dwpc_ref, dwout_ref):
        @pl.when(pl.program_id(0) == 0)
        def _():
            dbias_ref[...] = jnp.zeros_like(dbias_ref)
            dwpa_ref[...] = jnp.zeros_like(dwpa_ref)
            dwpc_ref[...] = jnp.zeros_like(dwpc_ref)
            dwout_ref[...] = jnp.zeros_like(dwout_ref)

        first = pl.program_id(0) % tiles_per_seq == 0
        dxb = _bf(dx_ref[...])
        gb, _, z, gbz, y_a, y_b, gate_a, gate_b, merged, dmerged = _mixer_values(
            o_ref, gb_ref, gla_ref, glb_ref, xc_ref, gc_ref, xcp_ref, gcp_ref, bias_ref, cw_ref, wpa_ref, wpc_ref, first,
            early=lambda: _mm_nt(dxb, wout_ref[...]))
        dwout_ref[...] += _mm_tn(merged, dxb)
        dya = _bf(dmerged * gate_a)
        dyb = _bf(dmerged * gate_b)
        do_v = _mm_nt(dya, wpa_ref[...])
        dgz = _mm_nt(dyb, wpc_ref[...])
        dwpa_ref[...] += _mm_tn(o_ref[...], dya)
        dwpc_ref[...] += _mm_tn(gbz, dyb)
        dla = (dmerged * y_a) * (gate_a * (1.0 - gate_a))
        dlb = (dmerged * y_b) * (gate_b * (1.0 - gate_b))
        dbias_ref[:, :d] += jnp.sum(dla, axis=0, keepdims=True)
        dbias_ref[:, d:] += jnp.sum(dlb, axis=0, keepdims=True)
        dm_ref[:, d:2 * d] = _bf(dla)
        dm_ref[:, 2 * d:] = _bf(dlb)
        do_ref[...] = do_v
        head = lax.broadcasted_iota(jnp.int32, (N_HEADS, hv), 0) * V_DIM
        col = lax.broadcasted_iota(jnp.int32, (N_HEADS, hv), 1)
        in_head = ((col >= head) & (col < head + V_DIM)).astype(F32)
        delta_ref[...] = lax.dot_general(in_head, do_v * o_ref[...].astype(F32), (((1,), (1,)), ((), ())),
                                         precision=lax.Precision.HIGHEST, preferred_element_type=F32)
        dz_ref[...] = dgz * gb
        dm_ref[:, :d] = _bf(dgz * z)

    tok, fwd_specs = _mixer_specs(tm, seq)
    return _pallas(
        body, "mix_bwd", (t // tm,), [tok] + fwd_specs,
        [pl.BlockSpec((tm, hv), lambda i: (i, 0)), pl.BlockSpec((N_HEADS, tm), lambda i: (0, i)), tok,
         pl.BlockSpec((tm, M_COLS), lambda i: (i, 0)), _const((1, 2 * d)), _const((hv, d)), _const((d, d)), _const((d, d))],
        [jax.ShapeDtypeStruct((t, hv), F32), jax.ShapeDtypeStruct((N_HEADS, t), F32), jax.ShapeDtypeStruct((t, d), F32),
         jax.ShapeDtypeStruct((t, M_COLS), BF16), jax.ShapeDtypeStruct((1, 2 * d), F32), jax.ShapeDtypeStruct((hv, d), F32),
         jax.ShapeDtypeStruct((d, d), F32), jax.ShapeDtypeStruct((d, d), F32)],
        (dx2, o, big, big, big, big, big, big, big, gate_bias, conv_w, w_pa, w_pc, w_out), ("arbitrary",), carried)


def _prep_bwd(lat, big, dz, dq, dk, dv, qa_gain, kva_gain, qh_gain, kh_gain, w_uq, w_uk, w_uv, rope, conv_w, seq, carried=None):
    t = lat.shape[0]
    d = D_MODEL
    tm = TOKEN_TILE
    tiles_per_seq = seq // tm
    last_blk = t // 8 - 1

    def body(lat_ref, xc_ref, gc_ref, dz_ref, dzn_ref, dq_ref, dk_ref, dv_ref, qa_ref, kva_ref, qh_ref, kh_ref, wuq_ref, wuk_ref,
             wuv_ref, cos_ref, slo_ref, shi_ref, cw_ref,
             dp_ref, dwuq_ref, dwuk_ref, dwuv_ref, dqa_ref, dkva_ref, dqh_ref, dkh_ref, dcw_ref):
        pid = pl.program_id(0)

        @pl.when(pid == 0)
        def _():
            for r in (dwuq_ref, dwuk_ref, dwuv_ref, dqa_ref, dkva_ref, dqh_ref, dkh_ref, dcw_ref):
                r[...] = jnp.zeros_like(r)

        lat_v = lat_ref[...]
        q_lat = lat_v[:, :Q_LORA]
        kv_lat = lat_v[:, Q_LORA:Q_LORA + KV_LORA]
        k_rope = lat_v[:, Q_LORA + KV_LORA:]
        qa_gain_v = qa_ref[...]
        kva_gain_v = kva_ref[...]
        qh_gain_v = qh_ref[...]
        kh_gain_v = kh_ref[...]
        cq, rq = _rms(q_lat, qa_gain_v)
        ckv, rkv = _rms(kv_lat, kva_gain_v)
        cqb = _bf(cq)
        ckvb = _bf(ckv)
        rope_v = (cos_ref[...], slo_ref[...], shi_ref[...])
        lane = lax.broadcasted_iota(jnp.int32, (tm, HEAD_PAD), 1)
        rope_lanes = (lane >= QK_NOPE) & (lane < QK_DIM)
        dk_rope = jnp.zeros((tm, HEAD_PAD), F32)
        dqh_gain = jnp.zeros((1, HEAD_PAD), F32)
        dkh_gain = jnp.zeros((1, HEAD_PAD), F32)
        q_all = _mm(cqb, wuq_ref[...])
        k_all = _mm(ckvb, wuk_ref[...])
        dvb = _bf(dv_ref[...])
        dckv = _mm_nt(dvb, wuv_ref[...])
        dwuv_ref[...] += _mm_tn(ckvb, dvb)

        last = pid % tiles_per_seq == tiles_per_seq - 1
        dzv = dz_ref[...]
        dz_next = jnp.where(last, 0.0, dzn_ref[...])
        dz1 = _shift_up(dzv, dz_next, 1)
        dz2 = _shift_up(dzv, dz_next, 2)
        cw = cw_ref[...]
        xc = xc_ref[...]
        gc = gc_ref[...]
        u = gc * xc
        du = cw[2:3] * dzv + cw[1:2] * dz1 + cw[0:1] * dz2
        dp_ref[:, :d] = _bf(du * gc)
        dp_ref[:, d:2 * d] = _bf(du * xc)
        dcw_ref[0:1, :] += jnp.sum(dz2 * u, axis=0, keepdims=True)
        dcw_ref[1:2, :] += jnp.sum(dz1 * u, axis=0, keepdims=True)
        dcw_ref[2:3, :] += jnp.sum(dzv * u, axis=0, keepdims=True)

        dcq = jnp.zeros((tm, Q_LORA), F32)
        half = N_HEADS // 2
        for part in range(2):
            dq_heads, dk_heads = [], []
            for hd in range(part * half, (part + 1) * half):
                lanes = slice(hd * HEAD_PAD, (hd + 1) * HEAD_PAD)
                q_pre = q_all[:, lanes]
                _, rr = _rms(q_pre, qh_gain_v, QK_DIM)
                dq_pre, dg = _rms_bwd(q_pre, rr, qh_gain_v, _rope_bwd(dq_ref[hd], rope_v), QK_DIM)
                dqh_gain = dqh_gain + dg
                dq_heads.append(_bf(dq_pre))

                k_pre = k_all[:, lanes] + k_rope
                _, rr = _rms(k_pre, kh_gain_v, QK_DIM)
                dk_pre, dg = _rms_bwd(k_pre, rr, kh_gain_v, _rope_bwd(dk_ref[hd], rope_v), QK_DIM)
                dkh_gain = dkh_gain + dg
                dk_rope = dk_rope + jnp.where(rope_lanes, dk_pre, 0.0)
                dk_heads.append(_bf(dk_pre))
            dq_part = jnp.concatenate(dq_heads, axis=1)
            dk_part = jnp.concatenate(dk_heads, axis=1)
            cols = slice(part * half * HEAD_PAD, (part + 1) * half * HEAD_PAD)
            dcq = dcq + _mm_nt(dq_part, wuq_ref[:, cols])
            dckv = dckv + _mm_nt(dk_part, wuk_ref[:, cols])
            dwuq_ref[:, cols] += _mm_tn(cqb, dq_part)
            dwuk_ref[:, cols] += _mm_tn(ckvb, dk_part)
        dqh_ref[...] += dqh_gain
        dkh_ref[...] += dkh_gain
        dq_lat, dg = _rms_bwd(q_lat, rq, qa_gain_v, dcq)
        dqa_ref[...] += dg
        dkv_lat, dg = _rms_bwd(kv_lat, rkv, kva_gain_v, dckv)
        dkva_ref[...] += dg
        dp_ref[:, 2 * d:2 * d + Q_LORA] = _bf(dq_lat)
        dp_ref[:, 2 * d + Q_LORA:2 * d + Q_LORA + KV_LORA] = _bf(dkv_lat)
        dp_ref[:, 2 * d + Q_LORA + KV_LORA:] = _bf(dk_rope)

    tok = lambda c: pl.BlockSpec((tm, c), lambda i: (i, 0))
    col = lambda c: pl.BlockSpec((tm, d), lambda i: (i, c))
    head = lambda c: pl.BlockSpec((N_HEADS, tm, c), lambda i: (0, i, 0))
    nxt = pl.BlockSpec((8, d), lambda i: (jnp.minimum((i + 1) * (tm // 8), last_blk), 0))
    return _pallas(
        body, "prep_bwd", (t // tm,),
        [tok(LAT_COLS), col(3), col(4), tok(d), nxt, head(HEAD_PAD), head(HEAD_PAD), tok(N_HEADS * V_DIM),
         _const((1, Q_LORA)), _const((1, KV_LORA)), _const((1, HEAD_PAD)), _const((1, HEAD_PAD)),
         _resident(w_uq.shape), _resident(w_uk.shape), _resident(w_uv.shape), tok(HEAD_PAD), tok(HEAD_PAD), tok(HEAD_PAD),
         _const((3, d))],
        [tok(P_COLS), _const(w_uq.shape), _const(w_uk.shape), _const(w_uv.shape), _const((1, Q_LORA)),
         _const((1, KV_LORA)), _const((1, HEAD_PAD)), _const((1, HEAD_PAD)), _const((3, d))],
        [jax.ShapeDtypeStruct((t, P_COLS), BF16), jax.ShapeDtypeStruct(w_uq.shape, F32),
         jax.ShapeDtypeStruct(w_uk.shape, F32), jax.ShapeDtypeStruct(w_uv.shape, F32),
         jax.ShapeDtypeStruct((1, Q_LORA), F32), jax.ShapeDtypeStruct((1, KV_LORA), F32),
         jax.ShapeDtypeStruct((1, HEAD_PAD), F32), jax.ShapeDtypeStruct((1, HEAD_PAD), F32), jax.ShapeDtypeStruct((3, d), F32)],
        (lat, big, big, dz, dz, dq, dk, dv, qa_gain, kva_gain, qh_gain, kh_gain, w_uq, w_uk, w_uv, *rope, conv_w),
        ("arbitrary",), carried)


def _inproj_bwd(x1, gain, dx2, dm, dp, w_in, w_kr, carried=None):
    t, d = x1.shape
    tm = TOKEN_TILE

    def body(x_ref, g_ref, dx2_ref, dm_ref, dp_ref, win_ref, wkr_ref, dx1_ref, dgain_ref):
        xv = x_ref[...]
        gain_v = g_ref[...]
        _, r = _rms(xv, gain_v)
        dh = (_mm(dm_ref[:, :d], win_ref[ROW_GB:ROW_GC, :]) + _mm(dm_ref[:, d:], win_ref[ROW_GL:, :])
              + _mm(dp_ref[:, :d], win_ref[ROW_XC:ROW_GB, :]) + _mm(dp_ref[:, d:2 * d], win_ref[ROW_GC:ROW_GL, :])
              + _mm(dp_ref[:, 2 * d:2 * d + ROW_KR], win_ref[ROW_QKV:ROW_KR, :]) + _mm(dp_ref[:, 2 * d + ROW_KR:], wkr_ref[...]))
        dxn, dgain = _rms_bwd(xv, r, gain_v, dh)
        dx1_ref[...] = dx2_ref[...] + dxn

        @pl.when(pl.program_id(0) == 0)
        def _():
            dgain_ref[...] = jnp.zeros_like(dgain_ref)

        dgain_ref[...] += dgain

    tok = lambda c: pl.BlockSpec((tm, c), lambda i: (i, 0))
    return _pallas(
        body, "inproj_bwd", (t // tm,),
        [tok(d), _const((1, d)), tok(d), tok(M_COLS), tok(P_COLS), _resident(w_in.shape), _resident(w_kr.shape)],
        [tok(d), _const((1, d))], [jax.ShapeDtypeStruct((t, d), F32), jax.ShapeDtypeStruct((1, d), F32)],
        (x1, gain, dx2, dm, dp, w_in, w_kr), ("arbitrary",), carried)


def _adamw(quads, name, carried=None):
    k = len(quads)
    rows, cols = quads[0][0].shape
    tr, tc = rows, cols
    for cand in (512, 352, 256, 192, 128, 64):
        if rows % cand == 0 and rows > cand:
            tr = cand
            break
    if tr == rows and rows * cols > 512 * 1024 and cols % 256 == 0:
        tc = 256
    while k * 14 * tr * tc * 4 > VMEM_LIMIT // 2 and tr % 16 == 0:
        tr //= 2

    def body(*refs):
        for i in range(k):
            w_ref, g_ref, m_ref, v_ref = refs[4 * i:4 * i + 4]
            delta_ref, nm_ref, nv_ref = refs[4 * k + 3 * i:4 * k + 3 * i + 3]
            gv = g_ref[...]
            nm = ADAM_B1 * m_ref[...] + (1.0 - ADAM_B1) * gv
            nv = ADAM_B2 * v_ref[...] + (1.0 - ADAM_B2) * (gv * gv)
            m_hat = nm * (1.0 / (1.0 - ADAM_B1 ** ADAM_STEP))
            v_hat = nv * (1.0 / (1.0 - ADAM_B2 ** ADAM_STEP))
            delta_ref[...] = -ADAM_LR * (m_hat / (jnp.sqrt(v_hat) + ADAM_EPS) + ADAM_WD * w_ref[...])
            nm_ref[...] = nm
            nv_ref[...] = nv

    spec = pl.BlockSpec((tr, tc), lambda i, j: (i, j))
    shape = jax.ShapeDtypeStruct((rows, cols), F32)
    outs = _pallas(body, name, (rows // tr, cols // tc), [spec] * (4 * k), [spec] * (3 * k), [shape] * (3 * k),
                   [a for quad in quads for a in quad], ("arbitrary", "arbitrary"), carried)
    return [tuple(outs[3 * i:3 * i + 3]) for i in range(k)]


def _adamw_small(packed_grads, triples, segments):
    k = len(triples)

    def body(*refs):
        g_ref = refs[0]
        off = 0
        for i in range(k):
            w_ref, m_ref, v_ref = refs[1 + 3 * i:4 + 3 * i]
            g_out, delta_ref, nm_ref, nv_ref = refs[1 + 3 * k + 4 * i:5 + 3 * k + 4 * i]
            gv = g_ref[:, off:off + w_ref.shape[1]]
            nm = ADAM_B1 * m_ref[...] + (1.0 - ADAM_B1) * gv
            nv = ADAM_B2 * v_ref[...] + (1.0 - ADAM_B2) * (gv * gv)
            m_hat = nm * (1.0 / (1.0 - ADAM_B1 ** ADAM_STEP))
            v_hat = nv * (1.0 / (1.0 - ADAM_B2 ** ADAM_STEP))
            g_out[...] = gv
            delta_ref[...] = -ADAM_LR * (m_hat / (jnp.sqrt(v_hat) + ADAM_EPS) + ADAM_WD * w_ref[...])
            nm_ref[...] = nm
            nv_ref[...] = nv
            off += segments[i]

    vm = pl.BlockSpec(memory_space=pltpu.VMEM)
    outs = pl.pallas_call(
        body, name="adamw_small", in_specs=[vm] * (1 + 3 * k), out_specs=[vm] * (4 * k),
        out_shape=[jax.ShapeDtypeStruct(w.shape, F32) for w, _, _ in triples for _ in range(4)],
    )(packed_grads, *[a for triple in triples for a in triple])
    return [tuple(outs[4 * i:4 * i + 4]) for i in range(k)]


def _place():
    x, y, c = lax.axis_index("x"), lax.axis_index("y"), lax.axis_index("c")
    other_chips = [(1 - x, y), (x, 1 - y), (1 - x, 1 - y)]
    return x, y, c, other_chips


def _remote(src, dst, sems, send, recv, device):
    return pltpu.make_async_remote_copy(src_ref=src, dst_ref=dst, send_sem=sems.at[send], recv_sem=sems.at[recv],
                                        device_id=device, device_id_type=MESH_ID)


def _cast_shards(shards, out_dtypes):
    n = len(shards)

    def body(*refs):
        ins, outs, stage, sems = refs[:n], refs[n:2 * n], refs[2 * n:3 * n], refs[3 * n]
        x, y, _, _ = _place()
        me = 2 * x + y
        copies = []
        for w in range(n):
            stage[w][...] = ins[w][...].astype(out_dtypes[w])
            copies.append(pltpu.make_async_copy(stage[w], outs[w].at[me], sems.at[w]))
            copies[-1].start()
        for cp in copies:
            cp.wait()

    vm = pl.BlockSpec(memory_space=pltpu.VMEM)
    return pl.pallas_call(
        body, name="cast_shards", in_specs=[vm] * n, out_specs=[ANY] * n,
        out_shape=[jax.ShapeDtypeStruct((N_CHIPS,) + s.shape, dt) for s, dt in zip(shards, out_dtypes)],
        scratch_shapes=[pltpu.VMEM(s.shape, dt) for s, dt in zip(shards, out_dtypes)] + [pltpu.SemaphoreType.DMA((n,))],
        compiler_params=_params())(*shards)


BF16_ROWS = 16


def _split_rows(rows):
    return (rows // 2) % BF16_ROWS == 0


def _half_shape(rows, cols):
    return (rows // 2, cols) if _split_rows(rows) else (rows, cols // 2)


def _half(rows, cols, which):
    if _split_rows(rows):
        return (pl.ds(pl.multiple_of(which * (rows // 2), BF16_ROWS), rows // 2), slice(None))
    return (slice(None), pl.ds(pl.multiple_of(which * (cols // 2), 128), cols // 2))


def _gather_carried(bufs):
    n = len(bufs)

    def half(w, slot, which):
        _, rows, cols = bufs[w].shape
        return (slot,) + _half(rows, cols, which)

    def start(ins, outs, sems, base):
        x, y, c, other_chips = _place()
        me = 2 * x + y
        for w in range(n):
            mine = outs[w].at[half(w, me, c)]
            for p, (px, py) in enumerate(other_chips):
                _remote(mine, mine, sems, base + 12 * w + p, base + 12 * w + 3 + p, (px, py, c)).start()

    def finish(ins, outs, sems, base):
        x, y, c, other_chips = _place()
        me = 2 * x + y
        for w in range(n):
            for p, (px, py) in enumerate(other_chips):
                got = outs[w].at[half(w, 2 * px + py, c)]
                _remote(got, got, sems, base + 12 * w + p, base + 12 * w + 3 + p, (px, py, c)).wait_recv()
                _remote(got, got, sems, base + 12 * w + 6 + p, base + 12 * w + 9 + p, (x, y, 1 - c)).start()
        for w in range(n):
            mine = outs[w].at[half(w, me, c)]
            for p, (px, py) in enumerate(other_chips):
                got = outs[w].at[half(w, 2 * px + py, c)]
                theirs = outs[w].at[half(w, 2 * px + py, 1 - c)]
                _remote(got, theirs, sems, base + 12 * w + 6 + p, base + 12 * w + 9 + p, (x, y, 1 - c)).wait()
                _remote(mine, mine, sems, base + 12 * w + p, base + 12 * w + 3 + p, (px, py, c)).wait_send()

    shapes = [jax.ShapeDtypeStruct(b.shape, b.dtype) for b in bufs]
    return _Carried(bufs, shapes, {w: w for w in range(n)}, 12 * n, start, finish)


def _swap_carried(grads):
    n = len(grads)

    def copy(w, ins, outs, sems, base):
        x, y, c, _ = _place()
        _, rows, cols = grads[w].shape
        theirs = ins[w].at[(slice(None),) + _half(rows, cols, 1 - c)]
        return _remote(theirs, outs[w], sems, base + 2 * w, base + 2 * w + 1, (x, y, 1 - c))

    def start(ins, outs, sems, base):
        for w in range(n):
            copy(w, ins, outs, sems, base).start()

    def finish(ins, outs, sems, base):
        for w in range(n):
            copy(w, ins, outs, sems, base).wait()

    shapes = [jax.ShapeDtypeStruct((g.shape[0],) + _half_shape(*g.shape[1:]), F32) for g in grads]
    return _Carried(grads, shapes, {}, 2 * n, start, finish)


def _row_tile(rows):
    for cand in (512, 352, 256, 192, 128, 96, 64, 32, 16):
        if rows % cand == 0:
            return cand
    return rows


def _half_block_index(split_rows, tiles, i, core):
    return (core * tiles + i, 0) if split_rows else (i, core)


def _chip_partial(grad, other, place, name):
    nblk, hr, hc = other.shape
    by_rows = _split_rows(grad.shape[1])
    tr = _row_tile(hr)
    tiles = hr // tr

    def body(place_ref, g_ref, o_ref, sum_ref, sum_bf_ref):
        s = g_ref[...] + o_ref[...]
        sum_ref[...] = s
        sum_bf_ref[...] = _bf(s)

    grid_spec = pltpu.PrefetchScalarGridSpec(
        num_scalar_prefetch=1, grid=(nblk, tiles),
        in_specs=[pl.BlockSpec((None, tr, hc), lambda b, i, place_ref: (b,) + _half_block_index(by_rows, tiles, i, place_ref[1])),
                  pl.BlockSpec((None, tr, hc), lambda b, i, place_ref: (b, i, 0))],
        out_specs=[pl.BlockSpec((None, tr, hc), lambda b, i, place_ref: (b, i, 0))] * 2)
    return pl.pallas_call(body, name=name, grid_spec=grid_spec,
                          out_shape=[jax.ShapeDtypeStruct(other.shape, F32), jax.ShapeDtypeStruct(other.shape, BF16)],
                          compiler_params=_params(("arbitrary", "arbitrary")))(place, grad, other)


def _send_carried(partials):
    n = len(partials)

    def start(ins, outs, sems, base):
        x, y, c, other_chips = _place()
        me = 2 * x + y
        for w in range(n):
            for p, (px, py) in enumerate(other_chips):
                _remote(ins[w].at[2 * px + py], outs[w].at[me], sems, base + 6 * w + p, base + 6 * w + 3 + p, (px, py, c)).start()

    def finish(ins, outs, sems, base):
        x, y, c, other_chips = _place()
        for w in range(n):
            for p, (px, py) in enumerate(other_chips):
                _remote(ins[w].at[2 * px + py], outs[w].at[2 * px + py], sems, base + 6 * w + p, base + 6 * w + 3 + p,
                        (px, py, c)).wait()

    return _Carried(partials, [jax.ShapeDtypeStruct(p.shape, BF16) for p in partials], {}, 6 * n, start, finish)


def _chip_total(own, received, place, shape, name):
    nblk, hr, hc = own.shape
    by_rows = _split_rows(shape[0])
    tr = _row_tile(hr)
    tiles = hr // tr

    def body(place_ref, own_ref, r1_ref, r2_ref, r3_ref, out_ref):
        out_ref[...] = own_ref[...] + ((r1_ref[...].astype(F32) + r2_ref[...].astype(F32)) + r3_ref[...].astype(F32))

    def slot(k):
        return pl.BlockSpec((None, tr, hc), lambda i, place_ref: ((place_ref[0] + k) % N_CHIPS, i, 0))

    grid_spec = pltpu.PrefetchScalarGridSpec(
        num_scalar_prefetch=1, grid=(tiles,), in_specs=[slot(0), slot(1), slot(2), slot(3)],
        out_specs=pl.BlockSpec((tr, hc), lambda i, place_ref: _half_block_index(by_rows, tiles, i, place_ref[1])))
    return pl.pallas_call(body, name=name, grid_spec=grid_spec, out_shape=jax.ShapeDtypeStruct(tuple(shape), F32),
                          compiler_params=_params(("arbitrary",)))(place, own, received, received, received)


def _join_carried(totals):
    n = len(totals)

    def copy(w, outs, sems, base):
        x, y, c, _ = _place()
        mine = outs[w].at[_half(*totals[w].shape, c)]
        return _remote(mine, mine, sems, base + 2 * w, base + 2 * w + 1, (x, y, 1 - c))

    def start(ins, outs, sems, base):
        for w in range(n):
            copy(w, outs, sems, base).start()

    def finish(ins, outs, sems, base):
        for w in range(n):
            copy(w, outs, sems, base).wait()

    shapes = [jax.ShapeDtypeStruct(a.shape, F32) for a in totals]
    return _Carried(totals, shapes, {w: w for w in range(n)}, 2 * n, start, finish)


def _sum_devices(vec):
    rows, n = vec.shape

    def body(v_ref, out_ref, buf, send_sems, recv_sems):
        x, y, c, _ = _place()
        me = 4 * x + 2 * y + c
        buf[me] = v_ref[...]
        sends = []
        for k in range(1, N_DEV):
            peer = (1 - x if k & 4 else x, 1 - y if k & 2 else y, 1 - c if k & 1 else c)
            cp = pltpu.make_async_remote_copy(src_ref=v_ref, dst_ref=buf.at[me], send_sem=send_sems.at[k], recv_sem=recv_sems.at[k],
                                              device_id=peer, device_id_type=MESH_ID)
            cp.start()
            sends.append(cp)
        for cp in sends:
            cp.wait()
        total = buf[0]
        for dev in range(1, N_DEV):
            total = total + buf[dev]
        out_ref[...] = total

    vm = pl.BlockSpec(memory_space=pltpu.VMEM)
    return pl.pallas_call(
        body, name="sum_devices", in_specs=[vm], out_specs=vm, out_shape=jax.ShapeDtypeStruct((rows, n), F32),
        scratch_shapes=[pltpu.VMEM((N_DEV, rows, n), F32), pltpu.SemaphoreType.DMA((N_DEV,)), pltpu.SemaphoreType.DMA((N_DEV,))],
    )(vec)


def _rope_tables(positions):
    half = ROPE_HALF
    inv_freq = 1.0 / (ROPE_THETA ** (jnp.arange(half, dtype=F32) / half))
    ang = positions.astype(F32).reshape(-1, 1) * inv_freq
    cos, sin = jnp.cos(ang), jnp.sin(ang)
    t = ang.shape[0]
    ones, zeros = jnp.ones((t, QK_NOPE), F32), jnp.zeros((t, QK_NOPE), F32)
    pad, none = HEAD_PAD - QK_DIM, zeros[:, :half]
    cos_full = jnp.concatenate([ones, cos, cos, ones[:, :pad]], axis=1)
    s_lo = jnp.concatenate([zeros, -sin, none, zeros[:, :pad]], axis=1)
    s_hi = jnp.concatenate([zeros, none, sin, zeros[:, :pad]], axis=1)
    return cos_full, s_lo, s_hi


def _partials(names, grads, from_sibling, place):
    return [_chip_partial(g, o, place, "chip_partial_" + n) for n, g, o in zip(names, grads, from_sibling)]


def _totals(names, grads, partials, received, place):
    return [_chip_total(pf, r, place, g.shape[1:], "chip_total_" + n) for n, g, (pf, _), r in zip(names, grads, partials, received)]


def _kernel_layouts(full):
    w_in = full["w_in"]
    w_kr = jnp.pad(w_in[ROW_KR:ROW_XC], ((QK_NOPE, HEAD_PAD - QK_DIM), (0, 0)))
    w_uq = jnp.pad(full["w_uq"].reshape(Q_LORA, N_HEADS, QK_DIM), ((0, 0), (0, 0), (0, HEAD_PAD - QK_DIM)))
    w_uk = jnp.pad(full["w_uk"].reshape(KV_LORA, N_HEADS, QK_NOPE), ((0, 0), (0, 0), (0, HEAD_PAD - QK_NOPE)))
    return {"w_in": w_in, "w_kr": w_kr, "w_uq": w_uq.reshape(Q_LORA, N_HEADS * HEAD_PAD),
            "w_uk": w_uk.reshape(KV_LORA, N_HEADS * HEAD_PAD), "w_uv": full["w_uv"], "w_uvt": full["w_uv"].T}


def _global_layouts(g):
    d = D_MODEL
    dm, dp = g["w_m"], g["w_p"]
    o_lat = 2 * d
    o_kr = o_lat + Q_LORA + KV_LORA + QK_NOPE
    w_in = jnp.concatenate([dp[o_lat:o_lat + Q_LORA + KV_LORA], dp[o_kr:o_kr + QK_ROPE], dp[:d], dm[:d], dp[d:o_lat], dm[d:]], axis=0)
    w_uq = g["w_uq"].reshape(Q_LORA, N_HEADS, HEAD_PAD)[:, :, :QK_DIM].reshape(Q_LORA, N_HEADS * QK_DIM)
    w_uk = g["w_uk"].reshape(KV_LORA, N_HEADS, HEAD_PAD)[:, :, :QK_NOPE].reshape(KV_LORA, N_HEADS * QK_NOPE)
    return {"w_in": w_in, "w_uq": w_uq, "w_uk": w_uk, "w_uv": g["w_uv"], "w_proj_attn": g["w_pa"], "w_proj_conv": g["w_pc"],
            "w_out": g["w_out"]}


def _col_blocks(a):
    r, c = a.shape
    return a.reshape(r, N_CHIPS, c // N_CHIPS).transpose(1, 0, 2)


def _from_col_blocks(a):
    n, r, c = a.shape
    return a.transpose(1, 0, 2).reshape(r, n * c)


COL_SHARDED = ("w_uq", "w_uk", "w_uv", "w_proj_attn")
TRANSPOSED = ("ffn1_w_gate", "ffn1_w_up", "ffn2_w_gate", "ffn2_w_up", "w_in")
SMALL = (("ffn1_norm", 1024), ("mix_norm", 1024), ("gate_bias", 2048), ("q_a_norm", 384), ("kv_a_norm", 256),
         ("q_head_norm", 128), ("k_head_norm", 128), ("ffn2_norm", 1024))
WEIGHT_ORDER = ("ffn1_norm", "ffn1_w_gate", "ffn1_w_up", "ffn1_w_down", "mix_norm", "w_in", "gate_bias", "q_a_norm", "w_uq",
                "kv_a_norm", "w_uk", "w_uv", "q_head_norm", "k_head_norm", "w_proj_attn", "conv_w", "w_proj_conv", "w_out",
                "ffn2_norm", "ffn2_w_gate", "ffn2_w_up", "ffn2_w_down")
MATRICES = ("ffn1_w_gate", "ffn1_w_up", "ffn1_w_down", "w_in", "w_uq", "w_uk", "w_uv", "w_proj_attn", "w_proj_conv", "w_out",
            "ffn2_w_gate", "ffn2_w_up", "ffn2_w_down")
GROUP_FFN1 = ("ffn1_w_gate", "ffn1_w_up", "ffn1_w_down")
GROUP_IN = ("w_in", "w_uq", "w_uk", "w_uv", "conv_w")
GROUP_MIX = ("w_proj_attn", "w_proj_conv", "w_out")
GROUP_FFN2 = ("ffn2_w_gate", "ffn2_w_up", "ffn2_w_down")
GROUP_MID = ("w_in", "w_uq", "w_uk", "w_uv", "w_proj_attn", "w_proj_conv", "w_out")


def _pad_lanes(a, n):
    return jnp.pad(a.reshape(1, -1), ((0, 0), (0, n - a.size)))


def kernel(x, positions, ffn1_norm, ffn1_w_gate, ffn1_w_up, ffn1_w_down, mix_norm, w_in, gate_bias, q_a_norm, w_uq, kv_a_norm, w_uk, w_uv, q_head_norm, k_head_norm, w_proj_attn, conv_w, w_proj_conv, w_out, ffn2_norm, ffn2_w_gate, ffn2_w_up, ffn2_w_down, loss_target, m_ffn1_norm, m_ffn1_w_gate, m_ffn1_w_up, m_ffn1_w_down, m_mix_norm, m_w_in, m_gate_bias, m_q_a_norm, m_w_uq, m_kv_a_norm, m_w_uk, m_w_uv, m_q_head_norm, m_k_head_norm, m_w_proj_attn, m_conv_w, m_w_proj_conv, m_w_out, m_ffn2_norm, m_ffn2_w_gate, m_ffn2_w_up, m_ffn2_w_down, v_ffn1_norm, v_ffn1_w_gate, v_ffn1_w_up, v_ffn1_w_down, v_mix_norm, v_w_in, v_gate_bias, v_q_a_norm, v_w_uq, v_kv_a_norm, v_w_uk, v_w_uv, v_q_head_norm, v_k_head_norm, v_w_proj_attn, v_conv_w, v_w_proj_conv, v_w_out, v_ffn2_norm, v_ffn2_w_gate, v_ffn2_w_up, v_ffn2_w_down):
    args = dict(locals())
    view = lambda n, a: a.T if n in TRANSPOSED else a
    weights = {n: view(n, args[n]) for n in WEIGHT_ORDER}
    moments_m = {n: view(n, args["m_" + n]) for n in WEIGHT_ORDER}
    moments_v = {n: view(n, args["v_" + n]) for n in WEIGHT_ORDER}
    nb, seq, d = x.shape
    t = nb * seq
    chip = (2 * lax.axis_index("x") + lax.axis_index("y")).astype(jnp.int32)
    place = jnp.stack([chip, lax.axis_index("c").astype(jnp.int32)])
    grads, delta, new_m, new_v = {}, {}, {}, {}

    def adamw(names, carried=None):
        results = _adamw([(weights[n], grads[n], moments_m[n], moments_v[n]) for n in names], "adamw_" + names[0], carried)
        for n, (dn, mn, vn) in zip(names, results):
            delta[n], new_m[n], new_v[n] = dn, mn, vn

    conv_rows = conv_w.shape[0]
    conv_shard = jnp.pad(conv_w, ((0, 16 - conv_rows), (0, 0)))
    bufs = dict(zip(MATRICES + ("conv_w",), _cast_shards([weights[n] for n in MATRICES] + [conv_shard],
                                                         [BF16] * len(MATRICES) + [F32])))
    blocks = dict(zip(GROUP_FFN1, _run(_gather_carried([bufs[n] for n in GROUP_FFN1]), "gather_ffn1")))
    p = {n: _pad_lanes(weights[n], size) for n, size in SMALL}
    rope = _rope_tables(positions)
    x_tok = x.reshape(t, d)

    gather_in = _gather_carried([bufs[n] for n in GROUP_IN])
    x1, gate1, up1, act1 = _ffn_fwd(x_tok, p["ffn1_norm"], blocks["ffn1_w_gate"], blocks["ffn1_w_up"], blocks["ffn1_w_down"], None,
                                    "ffn1_fwd", gather_in)
    blocks.update(zip(GROUP_IN, gather_in.results))
    w = _kernel_layouts({"w_in": blocks["w_in"].reshape(-1, d), **{n: _from_col_blocks(blocks[n]) for n in ("w_uq", "w_uk", "w_uv")}})
    p["conv_w"] = _from_col_blocks(blocks["conv_w"])[:conv_rows]

    gather_mix = _gather_carried([bufs[n] for n in GROUP_MIX])
    h2b, big, lat, q, k, v, vt = _inproj_fwd(x1, p["mix_norm"], w["w_in"], w["w_kr"], p["q_a_norm"], p["kv_a_norm"], p["q_head_norm"],
                                             p["k_head_norm"], w["w_uq"], w["w_uk"], w["w_uv"], w["w_uvt"], rope, gather_mix)
    blocks.update(zip(GROUP_MIX, gather_mix.results))
    w_pa = _from_col_blocks(blocks["w_proj_attn"])
    w_pc, w_out_full = blocks["w_proj_conv"].reshape(-1, d), blocks["w_out"].reshape(-1, d)

    gather_ffn2 = _gather_carried([bufs[n] for n in GROUP_FFN2[:2]])
    o, lse = _attn_fwd(q, k, vt, seq, gather_ffn2)
    gather_down = _gather_carried([bufs[n] for n in GROUP_FFN2[2:]])
    x2 = _mix_fwd(x1, o, big, p["gate_bias"], p["conv_w"], w_pa, w_pc, w_out_full, seq, gather_down)
    wg2, wu2 = gather_ffn2.results
    wd2, = gather_down.results
    dx3, gate2, up2, act2, loss = _ffn_fwd(x2, p["ffn2_norm"], wg2, wu2, wd2, loss_target.reshape(t, d), "ffn2_fwd")

    dx2, dg_ffn2, hb2, dgate2, dup2, dyb2 = _ffn_bwd_x(x2, p["ffn2_norm"], dx3, gate2, up2, wg2, wu2, wd2, "ffn2_bwd")
    g_ffn2 = [_tn_matmul(dgate2, hb2, "ffn2_dw_gate"), _tn_matmul(dup2, hb2, "ffn2_dw_up"), _tn_matmul(act2, dyb2, "ffn2_dw_down")]
    swap = _swap_carried(g_ffn2)
    do, delta_o, dz, dm, dbias, dw_pa, dw_pc, dw_out = _mix_bwd(dx2, o, big, p["gate_bias"], p["conv_w"], w_pa, w_pc, w_out_full, seq,
                                                                swap)
    part = _partials(GROUP_FFN2, g_ffn2, swap.results, place)
    send = _send_carried([pb for _, pb in part])
    dq, dk, dv = _attn_bwd(q, k, v, do, lse, delta_o.reshape(N_HEADS // ATTN_BWD_HEADS, ATTN_BWD_HEADS, -1), seq, send)
    join = _join_carried(_totals(GROUP_FFN2, g_ffn2, part, send.results, place))
    dp, dw_uq, dw_uk, dw_uv, dqa, dkva, dqh, dkh, dcw = _prep_bwd(
        lat, big, dz, dq, dk, dv, p["q_a_norm"], p["kv_a_norm"], p["q_head_norm"], p["k_head_norm"], w["w_uq"], w["w_uk"],
        w["w_uv"], rope, p["conv_w"], seq, join)
    grads.update(zip(GROUP_FFN2, join.results))

    gg = _global_layouts({"w_m": _tn_matmul(dm, h2b, "dw_in_m", split_k=2), "w_p": _tn_matmul(dp, h2b, "dw_in_p", split_k=2),
                          "w_uq": dw_uq, "w_uk": dw_uk, "w_uv": dw_uv, "w_pa": dw_pa, "w_pc": dw_pc, "w_out": dw_out})
    g_mid = [_col_blocks(gg[n]) if n in COL_SHARDED else gg[n].reshape(N_CHIPS, -1, gg[n].shape[-1]) for n in GROUP_MID]
    swap = _swap_carried(g_mid)
    dx1, dg_mix = _inproj_bwd(x1, p["mix_norm"], dx2, dm, dp, w["w_in"], w["w_kr"], swap)
    part = _partials(GROUP_MID, g_mid, swap.results, place)
    send = _send_carried([pb for _, pb in part])
    grad_x, dg_ffn1, hb1, dgate1, dup1, dyb1 = _ffn_bwd_x(x_tok, p["ffn1_norm"], dx1, gate1, up1, blocks["ffn1_w_gate"],
                                                         blocks["ffn1_w_up"], blocks["ffn1_w_down"], "ffn1_bwd", send)

    small_grads = {"ffn1_norm": dg_ffn1, "mix_norm": dg_mix, "gate_bias": dbias, "q_a_norm": dqa, "kv_a_norm": dkva,
                   "q_head_norm": dqh, "k_head_norm": dkh, "ffn2_norm": dg_ffn2}
    packed = jnp.concatenate([small_grads[n] for n, _ in SMALL] + [dcw.reshape(1, -1), loss], axis=1)
    total = _sum_devices(packed.reshape(8, -1)).reshape(1, -1)
    n_small = sum(size for _, size in SMALL)
    conv_cols = conv_w.shape[1]
    conv_total = total[:, n_small:n_small + conv_rows * d].reshape(conv_rows, d)
    grads["conv_w"] = lax.dynamic_slice_in_dim(conv_total, chip * conv_cols, conv_cols, axis=1)
    loss_total = total[0, n_small + conv_rows * d]

    join = _join_carried(_totals(GROUP_MID, g_mid, part, send.results, place))
    g_gate = _tn_matmul(dgate1, hb1, "ffn1_dw_gate", carried=join)
    grads.update(zip(GROUP_MID, join.results))
    swap_gate = _swap_carried([g_gate])
    g_up = _tn_matmul(dup1, hb1, "ffn1_dw_up", carried=swap_gate)
    part_gate = _partials(GROUP_FFN1[:1], [g_gate], swap_gate.results, place)
    send_gate, swap_up = _send_carried([part_gate[0][1]]), _swap_carried([g_up])
    g_down = _tn_matmul(act1, dyb1, "ffn1_dw_down", carried=_both(send_gate, swap_up))
    join_gate = _join_carried(_totals(GROUP_FFN1[:1], [g_gate], part_gate, send_gate.results, place))
    part_up = _partials(GROUP_FFN1[1:2], [g_up], swap_up.results, place)
    send_up, swap_down = _send_carried([part_up[0][1]]), _swap_carried([g_down])
    adamw(GROUP_FFN2, _both(_both(send_up, swap_down), join_gate))
    grads["ffn1_w_gate"] = join_gate.results[0]
    join_up = _join_carried(_totals(GROUP_FFN1[1:2], [g_up], part_up, send_up.results, place))
    part_down = _partials(GROUP_FFN1[2:], [g_down], swap_down.results, place)
    send_down = _send_carried([part_down[0][1]])
    adamw(("w_in",), _both(send_down, join_up))
    grads["ffn1_w_up"] = join_up.results[0]
    join_down = _join_carried(_totals(GROUP_FFN1[2:], [g_down], part_down, send_down.results, place))
    adamw(GROUP_FFN1[:2], join_down)
    grads["ffn1_w_down"] = join_down.results[0]
    adamw(GROUP_FFN1[2:])
    for n in GROUP_MID[1:] + ("conv_w",):
        adamw((n,))

    row = lambda a: a.reshape(1, -1)
    small = _adamw_small(total, [(row(weights[n]), row(moments_m[n]), row(moments_v[n])) for n, _ in SMALL], [size for _, size in SMALL])
    for (n, _), (gn, dn, mn, vn) in zip(SMALL, small):
        grads[n], delta[n], new_m[n], new_v[n] = gn.reshape(-1), dn.reshape(-1), mn.reshape(-1), vn.reshape(-1)

    return (loss_total, grad_x.reshape(nb, seq, d), *[view(n, src[n]) for src in (grads, delta, new_m, new_v) for n in WEIGHT_ORDER])
```

```python
import functools

import jax
import jax.numpy as jnp
from jax import lax
from jax.experimental import pallas as pl
from jax.experimental.pallas import tpu as pltpu

F32 = jnp.float32
BF16 = jnp.bfloat16

D_MODEL = 1024
N_HEADS = 8
QK_NOPE = 64
QK_ROPE = 32
QK_DIM = QK_NOPE + QK_ROPE
V_DIM = 64
HEAD_PAD = 128
Q_LORA = 384
KV_LORA = 256
ROPE_THETA = 10000.0
NORM_EPS = 1e-6
ATTN_SCALE = QK_DIM ** -0.5
MASK_VALUE = -1e30
N_CHIPS = 4
N_DEV = 8

ADAM_LR = 0.001
ADAM_B1 = 0.9
ADAM_B2 = 0.999
ADAM_EPS = 1e-08
ADAM_WD = 0.01
ADAM_STEP = 10

TOKEN_TILE = 256
PREP_TILE = 256
ATTN_TILE = 512
TN_TILE = 2048
VMEM_LIMIT = 56 * 1024 * 1024

M_COLS = 3 * D_MODEL
P_COLS = 2 * D_MODEL + Q_LORA + KV_LORA + HEAD_PAD
BIG_COLS = 5 * D_MODEL
LAT_COLS = Q_LORA + KV_LORA + HEAD_PAD

MESH_ID = pl.DeviceIdType.MESH
ANY = pl.BlockSpec(memory_space=pl.ANY)


def _params(semantics=None):
    return pltpu.CompilerParams(dimension_semantics=semantics, vmem_limit_bytes=VMEM_LIMIT)


class _Carried:
    def __init__(self, operands, out_shapes, aliases, n_sems, start, finish):
        self.operands, self.out_shapes, self.aliases, self.n_sems = list(operands), list(out_shapes), dict(aliases), n_sems
        self.start, self.finish = start, finish
        self.results = None


def _both(a, b):
    na, nao = len(a.operands), len(a.out_shapes)

    def start(ins, outs, sems, base):
        a.start(ins[:na], outs[:nao], sems, base)
        b.start(ins[na:], outs[nao:], sems, base + a.n_sems)

    def finish(ins, outs, sems, base):
        a.finish(ins[:na], outs[:nao], sems, base)
        b.finish(ins[na:], outs[nao:], sems, base + a.n_sems)

    aliases = dict(a.aliases)
    aliases.update({na + i: nao + o for i, o in b.aliases.items()})
    both = _Carried(a.operands + b.operands, a.out_shapes + b.out_shapes, aliases, a.n_sems + b.n_sems, start, finish)
    both.parts = (a, b)
    return both


def _set_results(carried, results):
    carried.results = list(results)
    if hasattr(carried, "parts"):
        a, b = carried.parts
        _set_results(a, results[:len(a.out_shapes)])
        _set_results(b, results[len(a.out_shapes):])


def _pallas(body, name, grid, in_specs, out_specs, out_shape, args, semantics, carried=None):
    if carried is None:
        return pl.pallas_call(body, name=name, grid=grid, in_specs=in_specs, out_specs=out_specs, out_shape=out_shape,
                              compiler_params=_params(semantics))(*args)
    n_in, n_out, n_ci, n_co = len(in_specs), len(out_specs), len(carried.operands), len(carried.out_shapes)

    def wrapped(*refs):
        ins, c_ins = refs[:n_in], refs[n_in:n_in + n_ci]
        outs, c_outs = refs[n_in + n_ci:n_in + n_ci + n_out], refs[n_in + n_ci + n_out:n_in + n_ci + n_out + n_co]
        sems = refs[-1]
        first = pl.program_id(0) == 0
        last = pl.program_id(0) == grid[0] - 1
        for axis in range(1, len(grid)):
            first = jnp.logical_and(first, pl.program_id(axis) == 0)
            last = jnp.logical_and(last, pl.program_id(axis) == grid[axis] - 1)

        @pl.when(first)
        def _():
            carried.start(c_ins, c_outs, sems, 0)

        body(*ins, *outs)

        @pl.when(last)
        def _():
            carried.finish(c_ins, c_outs, sems, 0)

    results = pl.pallas_call(
        wrapped, name=name, grid=grid, in_specs=list(in_specs) + [ANY] * n_ci, out_specs=list(out_specs) + [ANY] * n_co,
        out_shape=list(out_shape) + carried.out_shapes,
        input_output_aliases={n_in + i: n_out + o for i, o in carried.aliases.items()},
        scratch_shapes=[pltpu.SemaphoreType.DMA((carried.n_sems,))], compiler_params=_params(semantics))(*args, *carried.operands)
    _set_results(carried, results[n_out:])
    return results[:n_out]


def _run(carried, name):
    n_ci, n_co = len(carried.operands), len(carried.out_shapes)

    def body(*refs):
        carried.start(refs[:n_ci], refs[n_ci:n_ci + n_co], refs[-1], 0)
        carried.finish(refs[:n_ci], refs[n_ci:n_ci + n_co], refs[-1], 0)

    results = pl.pallas_call(body, name=name, in_specs=[ANY] * n_ci, out_specs=[ANY] * n_co, out_shape=carried.out_shapes,
                             input_output_aliases=carried.aliases,
                             scratch_shapes=[pltpu.SemaphoreType.DMA((carried.n_sems,))])(*carried.operands)
    _set_results(carried, results)
    return carried.results


def _resident(shape):
    nd = len(shape)
    return pl.BlockSpec(shape, lambda *_: (0,) * nd, pipeline_mode=pl.Buffered(1))


def _const(shape):
    nd = len(shape)
    return pl.BlockSpec(shape, lambda *_: (0,) * nd)


def _mm(a, b):
    return jnp.dot(a, b, preferred_element_type=F32)


def _mm_nt(a, b):
    return lax.dot_general(a, b, (((1,), (1,)), ((), ())), preferred_element_type=F32)


def _mm_tn(a, b):
    return lax.dot_general(a, b, (((0,), (0,)), ((), ())), preferred_element_type=F32)


def _bf(a):
    return a.astype(BF16)


def _sigmoid(a):
    return 1.0 / (1.0 + jnp.exp(-a))


def _rms(x, gain, n=None):
    n = x.shape[-1] if n is None else n
    r = lax.rsqrt(jnp.sum(x * x, axis=-1, keepdims=True) * (1.0 / n) + NORM_EPS)
    return (x * r) * gain, r


def _rms_bwd(x, r, gain, dh, n=None):
    n = x.shape[-1] if n is None else n
    u = dh * gain
    dx = r * u - x * ((r * r * r) * (jnp.sum(u * x, axis=-1, keepdims=True) * (1.0 / n)))
    dgain = jnp.sum(dh * (x * r), axis=0, keepdims=True)
    return dx, dgain


ROPE_HALF = QK_ROPE // 2


def _rope(t, rope):
    cos, s_lo, s_hi = rope
    return t * cos + pltpu.roll(t, HEAD_PAD - ROPE_HALF, 1) * s_lo + pltpu.roll(t, ROPE_HALF, 1) * s_hi


def _rope_bwd(dt, rope):
    cos, s_lo, s_hi = rope
    return dt * cos + pltpu.roll(dt * s_lo, ROPE_HALF, 1) + pltpu.roll(dt * s_hi, HEAD_PAD - ROPE_HALF, 1)


def _shift_down(u, prev8, k):
    s = pltpu.roll(u, k, 0)
    p = pltpu.roll(prev8, k, 0)
    row = lax.broadcasted_iota(jnp.int32, prev8.shape, 0)
    top = jnp.where(row < k, p, s[:8])
    return jnp.concatenate([top, s[8:]], axis=0)


def _shift_up(d, next8, k):
    tm = d.shape[0]
    s = pltpu.roll(d, tm - k, 0)
    n = pltpu.roll(next8, 8 - k, 0)
    row = lax.broadcasted_iota(jnp.int32, next8.shape, 0)
    bot = jnp.where(row >= 8 - k, n, s[tm - 8:])
    return jnp.concatenate([s[:tm - 8], bot], axis=0)


def _ffn_fwd(x, gain, wg, wu, wd, target, name, carried=None):
    t, d = x.shape
    nb, f, _ = wg.shape
    tm = TOKEN_TILE
    with_loss = target is not None

    def body(*refs):
        if with_loss:
            x_ref, g_ref, wg_ref, wu_ref, wd_ref, t_ref, out_ref, gate_ref, up_ref, act_ref, loss_ref = refs
        else:
            x_ref, g_ref, wg_ref, wu_ref, wd_ref, out_ref, gate_ref, up_ref, act_ref = refs
        xv = x_ref[...]
        h, _ = _rms(xv, g_ref[...])
        hb = _bf(h)
        y = jnp.zeros((tm, d), F32)
        nxt = (_mm_nt(hb, wg_ref[0]), _mm_nt(hb, wu_ref[0]))
        for j in range(nb):
            gate, up = nxt
            if j + 1 < nb:
                nxt = (_mm_nt(hb, wg_ref[j + 1]), _mm_nt(hb, wu_ref[j + 1]))
            act = _bf((gate * _sigmoid(gate)) * up)
            y = y + _mm(act, wd_ref[j])
            gate_ref[j] = _bf(gate)
            up_ref[j] = _bf(up)
            act_ref[j] = act
        out = xv + 0.5 * y
        if with_loss:
            err = out - t_ref[...]
            out_ref[...] = err * (1.0 / d)

            @pl.when(pl.program_id(0) == 0)
            def _():
                loss_ref[...] = jnp.zeros_like(loss_ref)

            part = jnp.sum(jnp.sum(err * err, axis=1, keepdims=True), axis=0, keepdims=True)
            loss_ref[...] += jnp.broadcast_to(part * (0.5 / d), loss_ref.shape)
        else:
            out_ref[...] = out

    tok = pl.BlockSpec((tm, d), lambda i: (i, 0))
    blk = pl.BlockSpec((nb, tm, f), lambda i: (0, i, 0))
    in_specs = [tok, _const((1, d)), _resident(wg.shape), _resident(wu.shape), _resident(wd.shape)]
    args = [x, gain, wg, wu, wd]
    out_shape = [jax.ShapeDtypeStruct((t, d), F32)] + [jax.ShapeDtypeStruct((nb, t, f), BF16)] * 3
    out_specs = [tok, blk, blk, blk]
    if with_loss:
        in_specs.append(tok)
        args.append(target)
        out_shape.append(jax.ShapeDtypeStruct((1, 128), F32))
        out_specs.append(_const((1, 128)))
    return _pallas(body, name, (t // tm,), in_specs, out_specs, out_shape, args, ("arbitrary",), carried)


def _ffn_bwd_x(x, gain, dout, gate, up, wg, wu, wd, name, carried=None):
    t, d = x.shape
    nb, f, _ = wg.shape
    tm = TOKEN_TILE

    def body(x_ref, g_ref, dout_ref, gate_ref, up_ref, wg_ref, wu_ref, wd_ref,
             dx_ref, dgain_ref, hb_ref, dgate_ref, dup_ref, dyb_ref):
        xv = x_ref[...]
        gain_v = g_ref[...]
        h, r = _rms(xv, gain_v)
        hb_ref[...] = _bf(h)
        dout_v = dout_ref[...]
        dyb = _bf(0.5 * dout_v)
        dyb_ref[...] = dyb
        dh = jnp.zeros((tm, d), F32)
        nxt = _mm_nt(dyb, wd_ref[0])
        for j in range(nb):
            dact = nxt
            if j + 1 < nb:
                nxt = _mm_nt(dyb, wd_ref[j + 1])
            gt = gate_ref[j].astype(F32)
            uv = up_ref[j].astype(F32)
            s = _sigmoid(gt)
            dup = _bf(dact * (gt * s))
            dgate = _bf((dact * uv) * (s * (1.0 + gt * (1.0 - s))))
            dh = dh + _mm(dgate, wg_ref[j]) + _mm(dup, wu_ref[j])
            dgate_ref[j] = dgate
            dup_ref[j] = dup
        dxn, dgain = _rms_bwd(xv, r, gain_v, dh)
        dx_ref[...] = dout_v + dxn

        @pl.when(pl.program_id(0) == 0)
        def _():
            dgain_ref[...] = jnp.zeros_like(dgain_ref)

        dgain_ref[...] += dgain

    tok = pl.BlockSpec((tm, d), lambda i: (i, 0))
    blk = pl.BlockSpec((nb, tm, f), lambda i: (0, i, 0))
    return _pallas(
        body, name, (t // tm,),
        [tok, _const((1, d)), tok, blk, blk, _resident(wg.shape), _resident(wu.shape), _resident(wd.shape)],
        [tok, _const((1, d)), tok, blk, blk, tok],
        [jax.ShapeDtypeStruct((t, d), F32), jax.ShapeDtypeStruct((1, d), F32), jax.ShapeDtypeStruct((t, d), BF16),
         jax.ShapeDtypeStruct((nb, t, f), BF16), jax.ShapeDtypeStruct((nb, t, f), BF16), jax.ShapeDtypeStruct((t, d), BF16)],
        (x, gain, dout, gate, up, wg, wu, wd), ("arbitrary",), carried)


def _tn_matmul(a, b, name, split_k=1, carried=None):
    t = a.shape[-2]
    k = a.shape[-1]
    n = b.shape[-1]
    tt = min(TN_TILE, t)
    nt = t // tt

    def body(a_ref, b_ref, o_ref):
        @pl.when(pl.program_id(1) == 0)
        def _():
            o_ref[...] = jnp.zeros_like(o_ref)

        o_ref[...] += _mm_tn(a_ref[...], b_ref[...])

    if split_k > 1:
        assert a.ndim == 2 and b.ndim == 2 and k % (split_k * 128) == 0
        tk = k // split_k
        g = split_k
        a_spec = pl.BlockSpec((tt, tk), lambda gi, ti: (ti, gi))
        b_spec = pl.BlockSpec((tt, n), lambda gi, ti: (ti, 0))
        o_spec = pl.BlockSpec((tk, n), lambda gi, ti: (gi, 0))
        out_shape = jax.ShapeDtypeStruct((k, n), F32)
    else:
        g = a.shape[0] if a.ndim == 3 else b.shape[0]
        a_spec = (pl.BlockSpec((None, tt, k), lambda gi, ti: (gi, ti, 0)) if a.ndim == 3
                  else pl.BlockSpec((tt, k), lambda gi, ti: (ti, 0)))
        b_spec = (pl.BlockSpec((None, tt, n), lambda gi, ti: (gi, ti, 0)) if b.ndim == 3
                  else pl.BlockSpec((tt, n), lambda gi, ti: (ti, 0)))
        o_spec = pl.BlockSpec((None, k, n), lambda gi, ti: (gi, 0, 0))
        out_shape = jax.ShapeDtypeStruct((g, k, n), F32)
    return _pallas(body, name, (g, nt), [a_spec, b_spec], [o_spec], [out_shape], (a, b), ("arbitrary", "arbitrary"), carried)[0]


ROW_QKV, ROW_KR, ROW_XC = 0, Q_LORA + KV_LORA, Q_LORA + KV_LORA + QK_ROPE
ROW_GB, ROW_GC, ROW_GL = ROW_XC + D_MODEL, ROW_XC + 2 * D_MODEL, ROW_XC + 3 * D_MODEL
BIG_FROM_ROWS = ((0, ROW_GB, D_MODEL), (D_MODEL, ROW_GL, 2 * D_MODEL), (3 * D_MODEL, ROW_XC, D_MODEL), (4 * D_MODEL, ROW_GC, D_MODEL))


def _inproj_fwd(x1, gain, w_in, w_kr, qa_gain, kva_gain, qh_gain, kh_gain, w_uq, w_uk, w_uv, w_uvt, rope, carried=None):
    t, d = x1.shape
    tm = TOKEN_TILE
    chunk = 512
    chunks = []
    for col, row, size in BIG_FROM_ROWS:
        chunks += [(col + o, row + o, chunk) for o in range(0, size, chunk)]
    of_head = [[c for k, c in enumerate(chunks) if k * N_HEADS // len(chunks) == hd] for hd in range(N_HEADS)]

    def body(x_ref, g_ref, win_ref, wkr_ref, qa_ref, kva_ref, qh_ref, kh_ref, wuq_ref, wuk_ref, wuv_ref, wuvt_ref, cos_ref, slo_ref,
             shi_ref, hb_ref, big_ref, lat_ref, q_ref, k_ref, v_ref, vt_ref):
        h, _ = _rms(x_ref[...], g_ref[...])
        hb = _bf(h)
        hb_ref[...] = hb
        k_rope = _mm_nt(hb, wkr_ref[...])
        lat = jnp.concatenate([_mm_nt(hb, win_ref[ROW_QKV:ROW_KR, :]), k_rope], axis=1)
        lat_ref[...] = lat
        cq, _ = _rms(lat[:, :Q_LORA], qa_ref[...])
        ckv, _ = _rms(lat[:, Q_LORA:Q_LORA + KV_LORA], kva_ref[...])
        cqb = _bf(cq)
        ckvb = _bf(ckv)
        rope_v = (cos_ref[...], slo_ref[...], shi_ref[...])
        q_all = _mm(cqb, wuq_ref[...])
        k_all = _mm(ckvb, wuk_ref[...])
        v_ref[...] = _bf(_mm(ckvb, wuv_ref[...]))
        vt_all = _mm_nt(wuvt_ref[...], ckvb)
        for hd in range(N_HEADS):
            for col, row, size in of_head[hd]:
                big_ref[:, col:col + size] = _mm_nt(hb, win_ref[row:row + size, :])
            lanes = slice(hd * HEAD_PAD, (hd + 1) * HEAD_PAD)
            qn, _ = _rms(q_all[:, lanes], qh_ref[...], QK_DIM)
            q_ref[hd] = _bf(_rope(qn, rope_v))
            kn, _ = _rms(k_all[:, lanes] + k_rope, kh_ref[...], QK_DIM)
            k_ref[hd] = _bf(_rope(kn, rope_v))
            vt_ref[hd] = _bf(vt_all[hd * V_DIM:(hd + 1) * V_DIM])

    tok = lambda c: pl.BlockSpec((tm, c), lambda i: (i, 0))
    head = lambda c: pl.BlockSpec((N_HEADS, tm, c), lambda i: (0, i, 0))
    return _pallas(
        body, "inproj_fwd", (t // tm,),
        [tok(d), _const((1, d)), _resident(w_in.shape), _resident(w_kr.shape), _const((1, Q_LORA)), _const((1, KV_LORA)),
         _const((1, HEAD_PAD)), _const((1, HEAD_PAD)), _resident(w_uq.shape), _resident(w_uk.shape),
         _resident(w_uv.shape), _resident(w_uvt.shape), tok(HEAD_PAD), tok(HEAD_PAD), tok(HEAD_PAD)],
        [tok(d), tok(BIG_COLS), tok(LAT_COLS), head(HEAD_PAD), head(HEAD_PAD), tok(N_HEADS * V_DIM),
         pl.BlockSpec((N_HEADS, V_DIM, tm), lambda i: (0, 0, i))],
        [jax.ShapeDtypeStruct((t, d), BF16), jax.ShapeDtypeStruct((t, BIG_COLS), F32),
         jax.ShapeDtypeStruct((t, LAT_COLS), F32), jax.ShapeDtypeStruct((N_HEADS, t, HEAD_PAD), BF16),
         jax.ShapeDtypeStruct((N_HEADS, t, HEAD_PAD), BF16), jax.ShapeDtypeStruct((t, N_HEADS * V_DIM), BF16),
         jax.ShapeDtypeStruct((N_HEADS, V_DIM, t), BF16)],
        (x1, gain, w_in, w_kr, qa_gain, kva_gain, qh_gain, kh_gain, w_uq, w_uk, w_uv, w_uvt, *rope), ("arbitrary",), carried)


EXP2_SCALE = ATTN_SCALE * 1.4426950408889634


def _diagonal_keep(tk, tq):
    return lax.broadcasted_iota(jnp.int32, (tk, tq), 0) <= lax.broadcasted_iota(jnp.int32, (tk, tq), 1)


def _attn_fwd(q, k, vt, seq, carried=None):
    _, t, _ = q.shape
    nseq = t // seq
    tq = tk = ATTN_TILE
    nq = seq // tq

    def body(q_ref, k_ref, vt_ref, o_ref, lse_ref):
        i = pl.program_id(1)
        qs = [q_ref[h] for h in range(N_HEADS)]
        keep = _diagonal_keep(tk, tq)

        def scores(h, k0):
            return _mm_nt(k_ref[h, pl.ds(k0, tk), :], qs[h])

        def update(h, st, state, k0, diagonal):
            m, l, acc = state
            if diagonal:
                st = jnp.where(keep, st, MASK_VALUE)
            m_new = jnp.maximum(m, jnp.max(st, axis=0, keepdims=True))
            pt = jnp.exp2((st - m_new) * EXP2_SCALE)
            alpha = jnp.exp2((m - m_new) * EXP2_SCALE)
            l_new = alpha * l + jnp.sum(pt, axis=0, keepdims=True)
            return m_new, l_new, alpha * acc + _mm(vt_ref[h, :, pl.ds(k0, tk)], _bf(pt))

        def tiles(states, k0, diagonal):
            st, new = scores(0, k0), []
            for h in range(N_HEADS):
                st_next = scores(h + 1, k0) if h + 1 < N_HEADS else None
                new.append(update(h, st, states[h], k0, diagonal))
                st = st_next
            return tuple(new)

        init = tuple((jnp.full((1, tq), MASK_VALUE, F32), jnp.zeros((1, tq), F32), jnp.zeros((V_DIM, tq), F32))
                     for _ in range(N_HEADS))
        states = lax.fori_loop(0, i, lambda j, s: tiles(s, pl.multiple_of(j * tk, tk), False), init)
        states = tiles(states, pl.multiple_of(i * tk, tk), True)
        outs = []
        for h in range(N_HEADS):
            m, l, acc = states[h]
            outs.append((acc / l).T)
            lse_ref[h] = m * EXP2_SCALE + jnp.log2(l)
        o_ref[...] = _bf(jnp.concatenate(outs, axis=-1))

    return _pallas(
        body, "attn_fwd", (nseq, nq),
        [pl.BlockSpec((N_HEADS, tq, HEAD_PAD), lambda b, i: (0, b * nq + i, 0)),
         pl.BlockSpec((N_HEADS, seq, HEAD_PAD), lambda b, i: (0, b, 0)),
         pl.BlockSpec((N_HEADS, V_DIM, seq), lambda b, i: (0, 0, b))],
        [pl.BlockSpec((tq, N_HEADS * V_DIM), lambda b, i: (b * nq + i, 0)),
         pl.BlockSpec((N_HEADS, 1, tq), lambda b, i: (0, 0, b * nq + i))],
        [jax.ShapeDtypeStruct((t, N_HEADS * V_DIM), BF16), jax.ShapeDtypeStruct((N_HEADS, 1, t), F32)],
        (q, k, vt), ("arbitrary", "arbitrary"), carried)


ATTN_BWD_HEADS = 4


def _attn_bwd(q, k, v, do, lse, delta, seq, carried=None):
    _, t, _ = q.shape
    nseq = t // seq
    tq = tk = ATTN_TILE
    n = seq // tq
    hb = ATTN_BWD_HEADS

    def body(q_ref, k_ref, v_ref, do_ref, lse_ref, delta_ref, dq_ref, dk_ref, dv_ref):
        dq_ref[...] = jnp.zeros_like(dq_ref)
        dk_ref[...] = jnp.zeros_like(dk_ref)
        dv_ref[...] = jnp.zeros_like(dv_ref)
        keep = _diagonal_keep(tk, tq)

        def tile(h, k0, q0, diagonal):
            kj = k_ref[h, pl.ds(k0, tk), :]
            qi = q_ref[h, pl.ds(q0, tq), :]
            doi = _bf(do_ref[pl.ds(q0, tq), h * V_DIM:(h + 1) * V_DIM])
            st = _mm_nt(kj, qi)
            if diagonal:
                st = jnp.where(keep, st, MASK_VALUE)
            pt = jnp.exp2(st * EXP2_SCALE - lse_ref[h, :, pl.ds(q0, tq)])
            dv_ref[pl.ds(k0, tk), h * V_DIM:(h + 1) * V_DIM] += _mm(_bf(pt), doi)
            dpt = _mm_nt(v_ref[pl.ds(k0, tk), h * V_DIM:(h + 1) * V_DIM], doi)
            dst = _bf((pt * (dpt - delta_ref[pl.ds(h, 1), pl.ds(q0, tq)])) * ATTN_SCALE)
            dk_ref[h, pl.ds(k0, tk), :] += _mm(dst, qi)
            dq_ref[h, pl.ds(q0, tq), :] += _mm_tn(dst, kj)

        def kv_step(j, _):
            k0 = pl.multiple_of(j * tk, tk)
            for h in range(hb):
                tile(h, k0, k0, True)

            def q_step(i, _):
                q0 = pl.multiple_of(i * tq, tq)
                for h in range(hb):
                    tile(h, k0, q0, False)
                return 0

            lax.fori_loop(j + 1, n, q_step, 0)
            return 0

        lax.fori_loop(0, n, kv_step, 0)

    hspec = lambda c: pl.BlockSpec((hb, seq, c), lambda b, g: (g, b, 0))
    cols = pl.BlockSpec((seq, hb * V_DIM), lambda b, g: (b, g))
    return _pallas(
        body, "attn_bwd", (nseq, N_HEADS // hb),
        [hspec(HEAD_PAD), hspec(HEAD_PAD), cols, cols,
         pl.BlockSpec((hb, 1, seq), lambda b, g: (g, 0, b)), pl.BlockSpec((None, hb, seq), lambda b, g: (g, 0, b))],
        [hspec(HEAD_PAD), hspec(HEAD_PAD), cols],
        [jax.ShapeDtypeStruct((N_HEADS, t, HEAD_PAD), F32), jax.ShapeDtypeStruct((N_HEADS, t, HEAD_PAD), F32),
         jax.ShapeDtypeStruct((t, N_HEADS * V_DIM), F32)],
        (q, k, v, do, lse, delta), ("arbitrary", "arbitrary"), carried)


def _mixer_values(o_ref, gb_ref, gla_ref, glb_ref, xc_ref, gc_ref, xcp_ref, gcp_ref, bias_ref, cw_ref, wpa_ref, wpc_ref,
                  first_of_seq, early=None):
    y_a = _mm(o_ref[...], wpa_ref[...])
    extra = early() if early is not None else None
    gb = gb_ref[...]
    u = gc_ref[...] * xc_ref[...]
    u_prev = jnp.where(first_of_seq, 0.0, gcp_ref[...] * xcp_ref[...])
    cw = cw_ref[...]
    z = cw[2:3] * u + cw[1:2] * _shift_down(u, u_prev, 1) + cw[0:1] * _shift_down(u, u_prev, 2)
    gbz = _bf(gb * z)
    y_b = _mm(gbz, wpc_ref[...])
    bias = bias_ref[...]
    gate_a = _sigmoid(gla_ref[...] + bias[:, :D_MODEL])
    gate_b = _sigmoid(glb_ref[...] + bias[:, D_MODEL:])
    merged = _bf(gate_a * y_a + gate_b * y_b)
    return gb, u, z, gbz, y_a, y_b, gate_a, gate_b, merged, extra


def _mixer_specs(tm, seq):
    d = D_MODEL
    tok = pl.BlockSpec((tm, d), lambda i: (i, 0))
    col = lambda c: pl.BlockSpec((tm, d), lambda i: (i, c))
    prev = lambda c: pl.BlockSpec((8, d), lambda i: (jnp.maximum(i * (tm // 8) - 1, 0), c))
    o_spec = pl.BlockSpec((tm, N_HEADS * V_DIM), lambda i: (i, 0))
    fwd_specs = [o_spec, col(0), col(1), col(2), col(3), col(4), prev(3), prev(4), _const((1, 2 * d)), _const((3, d)),
                 _resident((N_HEADS * V_DIM, d)), _resident((d, d)), _resident((d, d))]
    return tok, fwd_specs


def _mix_fwd(x1, o, big, gate_bias, conv_w, w_pa, w_pc, w_out, seq, carried=None):
    t, d = x1.shape
    tm = TOKEN_TILE
    tiles_per_seq = seq // tm

    def body(x_ref, o_ref, gb_ref, gla_ref, glb_ref, xc_ref, gc_ref, xcp_ref, gcp_ref, bias_ref, cw_ref, wpa_ref, wpc_ref,
             wout_ref, x2_ref):
        first = pl.program_id(0) % tiles_per_seq == 0
        merged = _mixer_values(o_ref, gb_ref, gla_ref, glb_ref, xc_ref, gc_ref, xcp_ref, gcp_ref, bias_ref, cw_ref, wpa_ref,
                               wpc_ref, first)[-2]
        x2_ref[...] = x_ref[...] + _mm(merged, wout_ref[...])

    tok, fwd_specs = _mixer_specs(tm, seq)
    return _pallas(body, "mix_fwd", (t // tm,), [tok] + fwd_specs, [tok], [jax.ShapeDtypeStruct((t, d), F32)],
                   (x1, o, big, big, big, big, big, big, big, gate_bias, conv_w, w_pa, w_pc, w_out), ("arbitrary",), carried)[0]


def _mix_bwd(dx2, o, big, gate_bias, conv_w, w_pa, w_pc, w_out, seq, carried=None):
    t, d = dx2.shape
    tm = TOKEN_TILE
    tiles_per_seq = seq // tm
    hv = N_HEADS * V_DIM

    def body(dx_ref, o_ref, gb_ref, gla_ref, glb_ref, xc_ref, gc_ref, xcp_ref, gcp_ref, bias_ref, cw_ref, wpa_ref, wpc_ref,
             wout_ref, do_ref, delta_ref, dz_ref, dm_ref, dbias_ref, dwpa_ref, dwpc_ref, dwout_ref):
        @pl.when(pl.program_id(0) == 0)
        def _():
            dbias_ref[...] = jnp.zeros_like(dbias_ref)
            dwpa_ref[...] = jnp.zeros_like(dwpa_ref)
            dwpc_ref[...] = jnp.zeros_like(dwpc_ref)
            dwout_ref[...] = jnp.zeros_like(dwout_ref)

        first = pl.program_id(0) % tiles_per_seq == 0
        dxb = _bf(dx_ref[...])
        gb, _, z, gbz, y_a, y_b, gate_a, gate_b, merged, dmerged = _mixer_values(
            o_ref, gb_ref, gla_ref, glb_ref, xc_ref, gc_ref, xcp_ref, gcp_ref, bias_ref, cw_ref, wpa_ref, wpc_ref, first,
            early=lambda: _mm_nt(dxb, wout_ref[...]))
        dwout_ref[...] += _mm_tn(merged, dxb)
        dya = _bf(dmerged * gate_a)
        dyb = _bf(dmerged * gate_b)
        do_v = _mm_nt(dya, wpa_ref[...])
        dgz = _mm_nt(dyb, wpc_ref[...])
        dwpa_ref[...] += _mm_tn(o_ref[...], dya)
        dwpc_ref[...] += _mm_tn(gbz, dyb)
        dla = (dmerged * y_a) * (gate_a * (1.0 - gate_a))
        dlb = (dmerged * y_b) * (gate_b * (1.0 - gate_b))
        dbias_ref[:, :d] += jnp.sum(dla, axis=0, keepdims=True)
        dbias_ref[:, d:] += jnp.sum(dlb, axis=0, keepdims=True)
        dm_ref[:, d:2 * d] = _bf(dla)
        dm_ref[:, 2 * d:] = _bf(dlb)
        do_ref[...] = do_v
        head = lax.broadcasted_iota(jnp.int32, (N_HEADS, hv), 0) * V_DIM
        col = lax.broadcasted_iota(jnp.int32, (N_HEADS, hv), 1)
        in_head = ((col >= head) & (col < head + V_DIM)).astype(F32)
        delta_ref[...] = lax.dot_general(in_head, do_v * o_ref[...].astype(F32), (((1,), (1,)), ((), ())),
                                         precision=lax.Precision.HIGHEST, preferred_element_type=F32)
        dz_ref[...] = dgz * gb
        dm_ref[:, :d] = _bf(dgz * z)

    tok, fwd_specs = _mixer_specs(tm, seq)
    return _pallas(
        body, "mix_bwd", (t // tm,), [tok] + fwd_specs,
        [pl.BlockSpec((tm, hv), lambda i: (i, 0)), pl.BlockSpec((N_HEADS, tm), lambda i: (0, i)), tok,
         pl.BlockSpec((tm, M_COLS), lambda i: (i, 0)), _const((1, 2 * d)), _const((hv, d)), _const((d, d)), _const((d, d))],
        [jax.ShapeDtypeStruct((t, hv), F32), jax.ShapeDtypeStruct((N_HEADS, t), F32), jax.ShapeDtypeStruct((t, d), F32),
         jax.ShapeDtypeStruct((t, M_COLS), BF16), jax.ShapeDtypeStruct((1, 2 * d), F32), jax.ShapeDtypeStruct((hv, d), F32),
         jax.ShapeDtypeStruct((d, d), F32), jax.ShapeDtypeStruct((d, d), F32)],
        (dx2, o, big, big, big, big, big, big, big, gate_bias, conv_w, w_pa, w_pc, w_out), ("arbitrary",), carried)


def _prep_bwd(lat, big, dz, dq, dk, dv, qa_gain, kva_gain, qh_gain, kh_gain, w_uq, w_uk, w_uv, rope, conv_w, seq, carried=None):
    t = lat.shape[0]
    d = D_MODEL
    tm = PREP_TILE
    tiles_per_seq = seq // tm
    last_blk = t // 8 - 1

    def body(lat_ref, xc_ref, gc_ref, dz_ref, dzn_ref, dq_ref, dk_ref, dv_ref, qa_ref, kva_ref, qh_ref, kh_ref, wuq_ref, wuk_ref,
             wuv_ref, cos_ref, slo_ref, shi_ref, cw_ref,
             dp_ref, dwuq_ref, dwuk_ref, dwuv_ref, dqa_ref, dkva_ref, dqh_ref, dkh_ref, dcw_ref):
        pid = pl.program_id(0)

        @pl.when(pid == 0)
        def _():
            for r in (dwuq_ref, dwuk_ref, dwuv_ref, dqa_ref, dkva_ref, dqh_ref, dkh_ref, dcw_ref):
                r[...] = jnp.zeros_like(r)

        lat_v = lat_ref[...]
        q_lat = lat_v[:, :Q_LORA]
        kv_lat = lat_v[:, Q_LORA:Q_LORA + KV_LORA]
        k_rope = lat_v[:, Q_LORA + KV_LORA:]
        qa_gain_v = qa_ref[...]
        kva_gain_v = kva_ref[...]
        qh_gain_v = qh_ref[...]
        kh_gain_v = kh_ref[...]
        cq, rq = _rms(q_lat, qa_gain_v)
        ckv, rkv = _rms(kv_lat, kva_gain_v)
        cqb = _bf(cq)
        ckvb = _bf(ckv)
        rope_v = (cos_ref[...], slo_ref[...], shi_ref[...])
        lane = lax.broadcasted_iota(jnp.int32, (tm, HEAD_PAD), 1)
        rope_lanes = (lane >= QK_NOPE) & (lane < QK_DIM)
        dk_rope = jnp.zeros((tm, HEAD_PAD), F32)
        dqh_gain = jnp.zeros((1, HEAD_PAD), F32)
        dkh_gain = jnp.zeros((1, HEAD_PAD), F32)
        q_all = _mm(cqb, wuq_ref[...])
        k_all = _mm(ckvb, wuk_ref[...])
        dvb = _bf(dv_ref[...])
        dckv = _mm_nt(dvb, wuv_ref[...])
        dwuv_ref[...] += _mm_tn(ckvb, dvb)

        last = pid % tiles_per_seq == tiles_per_seq - 1
        dzv = dz_ref[...]
        dz_next = jnp.where(last, 0.0, dzn_ref[...])
        dz1 = _shift_up(dzv, dz_next, 1)
        dz2 = _shift_up(dzv, dz_next, 2)
        cw = cw_ref[...]
        xc = xc_ref[...]
        gc = gc_ref[...]
        u = gc * xc
        du = cw[2:3] * dzv + cw[1:2] * dz1 + cw[0:1] * dz2
        dp_ref[:, :d] = _bf(du * gc)
        dp_ref[:, d:2 * d] = _bf(du * xc)
        dcw_ref[0:1, :] += jnp.sum(dz2 * u, axis=0, keepdims=True)
        dcw_ref[1:2, :] += jnp.sum(dz1 * u, axis=0, keepdims=True)
        dcw_ref[2:3, :] += jnp.sum(dzv * u, axis=0, keepdims=True)

        dcq = jnp.zeros((tm, Q_LORA), F32)
        half = N_HEADS // 2
        for part in range(2):
            dq_heads, dk_heads = [], []
            for hd in range(part * half, (part + 1) * half):
                lanes = slice(hd * HEAD_PAD, (hd + 1) * HEAD_PAD)
                q_pre = q_all[:, lanes]
                _, rr = _rms(q_pre, qh_gain_v, QK_DIM)
                dq_pre, dg = _rms_bwd(q_pre, rr, qh_gain_v, _rope_bwd(dq_ref[hd], rope_v), QK_DIM)
                dqh_gain = dqh_gain + dg
                dq_heads.append(_bf(dq_pre))

                k_pre = k_all[:, lanes] + k_rope
                _, rr = _rms(k_pre, kh_gain_v, QK_DIM)
                dk_pre, dg = _rms_bwd(k_pre, rr, kh_gain_v, _rope_bwd(dk_ref[hd], rope_v), QK_DIM)
                dkh_gain = dkh_gain + dg
                dk_rope = dk_rope + jnp.where(rope_lanes, dk_pre, 0.0)
                dk_heads.append(_bf(dk_pre))
            dq_part = jnp.concatenate(dq_heads, axis=1)
            dk_part = jnp.concatenate(dk_heads, axis=1)
            cols = slice(part * half * HEAD_PAD, (part + 1) * half * HEAD_PAD)
            dcq = dcq + _mm_nt(dq_part, wuq_ref[:, cols])
            dckv = dckv + _mm_nt(dk_part, wuk_ref[:, cols])
            dwuq_ref[:, cols] += _mm_tn(cqb, dq_part)
            dwuk_ref[:, cols] += _mm_tn(ckvb, dk_part)
        dqh_ref[...] += dqh_gain
        dkh_ref[...] += dkh_gain
        dq_lat, dg = _rms_bwd(q_lat, rq, qa_gain_v, dcq)
        dqa_ref[...] += dg
        dkv_lat, dg = _rms_bwd(kv_lat, rkv, kva_gain_v, dckv)
        dkva_ref[...] += dg
        dp_ref[:, 2 * d:2 * d + Q_LORA] = _bf(dq_lat)
        dp_ref[:, 2 * d + Q_LORA:2 * d + Q_LORA + KV_LORA] = _bf(dkv_lat)
        dp_ref[:, 2 * d + Q_LORA + KV_LORA:] = _bf(dk_rope)

    tok = lambda c: pl.BlockSpec((tm, c), lambda i: (i, 0))
    col = lambda c: pl.BlockSpec((tm, d), lambda i: (i, c))
    head = lambda c: pl.BlockSpec((N_HEADS, tm, c), lambda i: (0, i, 0))
    nxt = pl.BlockSpec((8, d), lambda i: (jnp.minimum((i + 1) * (tm // 8), last_blk), 0))
    return _pallas(
        body, "prep_bwd", (t // tm,),
        [tok(LAT_COLS), col(3), col(4), tok(d), nxt, head(HEAD_PAD), head(HEAD_PAD), tok(N_HEADS * V_DIM),
         _const((1, Q_LORA)), _const((1, KV_LORA)), _const((1, HEAD_PAD)), _const((1, HEAD_PAD)),
         _resident(w_uq.shape), _resident(w_uk.shape), _resident(w_uv.shape), tok(HEAD_PAD), tok(HEAD_PAD), tok(HEAD_PAD),
         _const((3, d))],
        [tok(P_COLS), _const(w_uq.shape), _const(w_uk.shape), _const(w_uv.shape), _const((1, Q_LORA)),
         _const((1, KV_LORA)), _const((1, HEAD_PAD)), _const((1, HEAD_PAD)), _const((3, d))],
        [jax.ShapeDtypeStruct((t, P_COLS), BF16), jax.ShapeDtypeStruct(w_uq.shape, F32),
         jax.ShapeDtypeStruct(w_uk.shape, F32), jax.ShapeDtypeStruct(w_uv.shape, F32),
         jax.ShapeDtypeStruct((1, Q_LORA), F32), jax.ShapeDtypeStruct((1, KV_LORA), F32),
         jax.ShapeDtypeStruct((1, HEAD_PAD), F32), jax.ShapeDtypeStruct((1, HEAD_PAD), F32), jax.ShapeDtypeStruct((3, d), F32)],
        (lat, big, big, dz, dz, dq, dk, dv, qa_gain, kva_gain, qh_gain, kh_gain, w_uq, w_uk, w_uv, *rope, conv_w),
        ("arbitrary",), carried)


def _inproj_bwd(x1, gain, dx2, dm, dp, w_in, w_kr, carried=None):
    t, d = x1.shape
    tm = TOKEN_TILE

    def body(x_ref, g_ref, dx2_ref, dm_ref, dp_ref, win_ref, wkr_ref, dx1_ref, dgain_ref):
        xv = x_ref[...]
        gain_v = g_ref[...]
        _, r = _rms(xv, gain_v)
        dh = (_mm(dm_ref[:, :d], win_ref[ROW_GB:ROW_GC, :]) + _mm(dm_ref[:, d:], win_ref[ROW_GL:, :])
              + _mm(dp_ref[:, :d], win_ref[ROW_XC:ROW_GB, :]) + _mm(dp_ref[:, d:2 * d], win_ref[ROW_GC:ROW_GL, :])
              + _mm(dp_ref[:, 2 * d:2 * d + ROW_KR], win_ref[ROW_QKV:ROW_KR, :]) + _mm(dp_ref[:, 2 * d + ROW_KR:], wkr_ref[...]))
        dxn, dgain = _rms_bwd(xv, r, gain_v, dh)
        dx1_ref[...] = dx2_ref[...] + dxn

        @pl.when(pl.program_id(0) == 0)
        def _():
            dgain_ref[...] = jnp.zeros_like(dgain_ref)

        dgain_ref[...] += dgain

    tok = lambda c: pl.BlockSpec((tm, c), lambda i: (i, 0))
    return _pallas(
        body, "inproj_bwd", (t // tm,),
        [tok(d), _const((1, d)), tok(d), tok(M_COLS), tok(P_COLS), _resident(w_in.shape), _resident(w_kr.shape)],
        [tok(d), _const((1, d))], [jax.ShapeDtypeStruct((t, d), F32), jax.ShapeDtypeStruct((1, d), F32)],
        (x1, gain, dx2, dm, dp, w_in, w_kr), ("arbitrary",), carried)


def _adamw(quads, name, carried=None):
    k = len(quads)
    rows, cols = quads[0][0].shape
    tr, tc = rows, cols
    for cand in (512, 352, 256, 192, 128, 64):
        if rows % cand == 0 and rows > cand:
            tr = cand
            break
    if tr == rows and rows * cols > 512 * 1024 and cols % 256 == 0:
        tc = 256
    while k * 14 * tr * tc * 4 > VMEM_LIMIT // 2 and tr % 16 == 0:
        tr //= 2

    def body(*refs):
        for i in range(k):
            w_ref, g_ref, m_ref, v_ref = refs[4 * i:4 * i + 4]
            delta_ref, nm_ref, nv_ref = refs[4 * k + 3 * i:4 * k + 3 * i + 3]
            gv = g_ref[...]
            nm = ADAM_B1 * m_ref[...] + (1.0 - ADAM_B1) * gv
            nv = ADAM_B2 * v_ref[...] + (1.0 - ADAM_B2) * (gv * gv)
            m_hat = nm * (1.0 / (1.0 - ADAM_B1 ** ADAM_STEP))
            v_hat = nv * (1.0 / (1.0 - ADAM_B2 ** ADAM_STEP))
            delta_ref[...] = -ADAM_LR * (m_hat / (jnp.sqrt(v_hat) + ADAM_EPS) + ADAM_WD * w_ref[...])
            nm_ref[...] = nm
            nv_ref[...] = nv

    spec = pl.BlockSpec((tr, tc), lambda i, j: (i, j))
    shape = jax.ShapeDtypeStruct((rows, cols), F32)
    outs = _pallas(body, name, (rows // tr, cols // tc), [spec] * (4 * k), [spec] * (3 * k), [shape] * (3 * k),
                   [a for quad in quads for a in quad], ("arbitrary", "arbitrary"), carried)
    return [tuple(outs[3 * i:3 * i + 3]) for i in range(k)]


def _adamw_small(packed_grads, triples, segments):
    k = len(triples)

    def body(*refs):
        g_ref = refs[0]
        off = 0
        for i in range(k):
            w_ref, m_ref, v_ref = refs[1 + 3 * i:4 + 3 * i]
            g_out, delta_ref, nm_ref, nv_ref = refs[1 + 3 * k + 4 * i:5 + 3 * k + 4 * i]
            gv = g_ref[:, off:off + w_ref.shape[1]]
            nm = ADAM_B1 * m_ref[...] + (1.0 - ADAM_B1) * gv
            nv = ADAM_B2 * v_ref[...] + (1.0 - ADAM_B2) * (gv * gv)
            m_hat = nm * (1.0 / (1.0 - ADAM_B1 ** ADAM_STEP))
            v_hat = nv * (1.0 / (1.0 - ADAM_B2 ** ADAM_STEP))
            g_out[...] = gv
            delta_ref[...] = -ADAM_LR * (m_hat / (jnp.sqrt(v_hat) + ADAM_EPS) + ADAM_WD * w_ref[...])
            nm_ref[...] = nm
            nv_ref[...] = nv
            off += segments[i]

    vm = pl.BlockSpec(memory_space=pltpu.VMEM)
    outs = pl.pallas_call(
        body, name="adamw_small", in_specs=[vm] * (1 + 3 * k), out_specs=[vm] * (4 * k),
        out_shape=[jax.ShapeDtypeStruct(w.shape, F32) for w, _, _ in triples for _ in range(4)],
    )(packed_grads, *[a for triple in triples for a in triple])
    return [tuple(outs[4 * i:4 * i + 4]) for i in range(k)]


def _place():
    x, y, c = lax.axis_index("x"), lax.axis_index("y"), lax.axis_index("c")
    other_chips = [(1 - x, y), (x, 1 - y), (1 - x, 1 - y)]
    return x, y, c, other_chips


def _remote(src, dst, sems, send, recv, device):
    return pltpu.make_async_remote_copy(src_ref=src, dst_ref=dst, send_sem=sems.at[send], recv_sem=sems.at[recv],
                                        device_id=device, device_id_type=MESH_ID)


def _cast_shards(shards, out_dtypes):
    n = len(shards)

    def body(*refs):
        ins, outs, stage, sems = refs[:n], refs[n:2 * n], refs[2 * n:3 * n], refs[3 * n]
        x, y, _, _ = _place()
        me = 2 * x + y
        copies = []
        for w in range(n):
            stage[w][...] = ins[w][...].astype(out_dtypes[w])
            copies.append(pltpu.make_async_copy(stage[w], outs[w].at[me], sems.at[w]))
            copies[-1].start()
        for cp in copies:
            cp.wait()

    vm = pl.BlockSpec(memory_space=pltpu.VMEM)
    return pl.pallas_call(
        body, name="cast_shards", in_specs=[vm] * n, out_specs=[ANY] * n,
        out_shape=[jax.ShapeDtypeStruct((N_CHIPS,) + s.shape, dt) for s, dt in zip(shards, out_dtypes)],
        scratch_shapes=[pltpu.VMEM(s.shape, dt) for s, dt in zip(shards, out_dtypes)] + [pltpu.SemaphoreType.DMA((n,))],
        compiler_params=_params())(*shards)


BF16_ROWS = 16


def _split_rows(rows):
    return (rows // 2) % BF16_ROWS == 0


def _half_shape(rows, cols):
    return (rows // 2, cols) if _split_rows(rows) else (rows, cols // 2)


def _half(rows, cols, which):
    if _split_rows(rows):
        return (pl.ds(pl.multiple_of(which * (rows // 2), BF16_ROWS), rows // 2), slice(None))
    return (slice(None), pl.ds(pl.multiple_of(which * (cols // 2), 128), cols // 2))


def _gather_carried(bufs):
    n = len(bufs)

    def half(w, slot, which):
        _, rows, cols = bufs[w].shape
        return (slot,) + _half(rows, cols, which)

    def start(ins, outs, sems, base):
        x, y, c, other_chips = _place()
        me = 2 * x + y
        for w in range(n):
            mine = outs[w].at[half(w, me, c)]
            for p, (px, py) in enumerate(other_chips):
                _remote(mine, mine, sems, base + 12 * w + p, base + 12 * w + 3 + p, (px, py, c)).start()

    def finish(ins, outs, sems, base):
        x, y, c, other_chips = _place()
        me = 2 * x + y
        for w in range(n):
            for p, (px, py) in enumerate(other_chips):
                got = outs[w].at[half(w, 2 * px + py, c)]
                _remote(got, got, sems, base + 12 * w + p, base + 12 * w + 3 + p, (px, py, c)).wait_recv()
                _remote(got, got, sems, base + 12 * w + 6 + p, base + 12 * w + 9 + p, (x, y, 1 - c)).start()
        for w in range(n):
            mine = outs[w].at[half(w, me, c)]
            for p, (px, py) in enumerate(other_chips):
                got = outs[w].at[half(w, 2 * px + py, c)]
                theirs = outs[w].at[half(w, 2 * px + py, 1 - c)]
                _remote(got, theirs, sems, base + 12 * w + 6 + p, base + 12 * w + 9 + p, (x, y, 1 - c)).wait()
                _remote(mine, mine, sems, base + 12 * w + p, base + 12 * w + 3 + p, (px, py, c)).wait_send()

    shapes = [jax.ShapeDtypeStruct(b.shape, b.dtype) for b in bufs]
    return _Carried(bufs, shapes, {w: w for w in range(n)}, 12 * n, start, finish)


def _swap_carried(grads):
    n = len(grads)

    def copy(w, ins, outs, sems, base):
        x, y, c, _ = _place()
        _, rows, cols = grads[w].shape
        theirs = ins[w].at[(slice(None),) + _half(rows, cols, 1 - c)]
        return _remote(theirs, outs[w], sems, base + 2 * w, base + 2 * w + 1, (x, y, 1 - c))

    def start(ins, outs, sems, base):
        for w in range(n):
            copy(w, ins, outs, sems, base).start()

    def finish(ins, outs, sems, base):
        for w in range(n):
            copy(w, ins, outs, sems, base).wait()

    shapes = [jax.ShapeDtypeStruct((g.shape[0],) + _half_shape(*g.shape[1:]), F32) for g in grads]
    return _Carried(grads, shapes, {}, 2 * n, start, finish)


def _row_tile(rows):
    for cand in (512, 352, 256, 192, 128, 96, 64, 32, 16):
        if rows % cand == 0:
            return cand
    return rows


def _half_block_index(split_rows, tiles, i, core):
    return (core * tiles + i, 0) if split_rows else (i, core)


def _chip_partial(grad, other, place, name):
    nblk, hr, hc = other.shape
    by_rows = _split_rows(grad.shape[1])
    tr = _row_tile(hr)
    tiles = hr // tr

    def body(place_ref, g_ref, o_ref, sum_ref, sum_bf_ref):
        s = g_ref[...] + o_ref[...]
        sum_ref[...] = s
        sum_bf_ref[...] = _bf(s)

    grid_spec = pltpu.PrefetchScalarGridSpec(
        num_scalar_prefetch=1, grid=(nblk, tiles),
        in_specs=[pl.BlockSpec((None, tr, hc), lambda b, i, place_ref: (b,) + _half_block_index(by_rows, tiles, i, place_ref[1])),
                  pl.BlockSpec((None, tr, hc), lambda b, i, place_ref: (b, i, 0))],
        out_specs=[pl.BlockSpec((None, tr, hc), lambda b, i, place_ref: (b, i, 0))] * 2)
    return pl.pallas_call(body, name=name, grid_spec=grid_spec,
                          out_shape=[jax.ShapeDtypeStruct(other.shape, F32), jax.ShapeDtypeStruct(other.shape, BF16)],
                          compiler_params=_params(("arbitrary", "arbitrary")))(place, grad, other)


def _send_carried(partials):
    n = len(partials)

    def start(ins, outs, sems, base):
        x, y, c, other_chips = _place()
        me = 2 * x + y
        for w in range(n):
            for p, (px, py) in enumerate(other_chips):
                _remote(ins[w].at[2 * px + py], outs[w].at[me], sems, base + 6 * w + p, base + 6 * w + 3 + p, (px, py, c)).start()

    def finish(ins, outs, sems, base):
        x, y, c, other_chips = _place()
        for w in range(n):
            for p, (px, py) in enumerate(other_chips):
                _remote(ins[w].at[2 * px + py], outs[w].at[2 * px + py], sems, base + 6 * w + p, base + 6 * w + 3 + p,
                        (px, py, c)).wait()

    return _Carried(partials, [jax.ShapeDtypeStruct(p.shape, BF16) for p in partials], {}, 6 * n, start, finish)


def _chip_total(own, received, place, shape, name):
    nblk, hr, hc = own.shape
    by_rows = _split_rows(shape[0])
    tr = _row_tile(hr)
    tiles = hr // tr

    def body(place_ref, own_ref, r1_ref, r2_ref, r3_ref, out_ref):
        out_ref[...] = own_ref[...] + ((r1_ref[...].astype(F32) + r2_ref[...].astype(F32)) + r3_ref[...].astype(F32))

    def slot(k):
        return pl.BlockSpec((None, tr, hc), lambda i, place_ref: ((place_ref[0] + k) % N_CHIPS, i, 0))

    grid_spec = pltpu.PrefetchScalarGridSpec(
        num_scalar_prefetch=1, grid=(tiles,), in_specs=[slot(0), slot(1), slot(2), slot(3)],
        out_specs=pl.BlockSpec((tr, hc), lambda i, place_ref: _half_block_index(by_rows, tiles, i, place_ref[1])))
    return pl.pallas_call(body, name=name, grid_spec=grid_spec, out_shape=jax.ShapeDtypeStruct(tuple(shape), F32),
                          compiler_params=_params(("arbitrary",)))(place, own, received, received, received)


def _join_carried(totals):
    n = len(totals)

    def copy(w, outs, sems, base):
        x, y, c, _ = _place()
        mine = outs[w].at[_half(*totals[w].shape, c)]
        return _remote(mine, mine, sems, base + 2 * w, base + 2 * w + 1, (x, y, 1 - c))

    def start(ins, outs, sems, base):
        for w in range(n):
            copy(w, outs, sems, base).start()

    def finish(ins, outs, sems, base):
        for w in range(n):
            copy(w, outs, sems, base).wait()

    shapes = [jax.ShapeDtypeStruct(a.shape, F32) for a in totals]
    return _Carried(totals, shapes, {w: w for w in range(n)}, 2 * n, start, finish)


def _sum_devices(vec):
    rows, n = vec.shape

    def body(v_ref, out_ref, buf, send_sems, recv_sems):
        x, y, c, _ = _place()
        me = 4 * x + 2 * y + c
        buf[me] = v_ref[...]
        sends = []
        for k in range(1, N_DEV):
            peer = (1 - x if k & 4 else x, 1 - y if k & 2 else y, 1 - c if k & 1 else c)
            cp = pltpu.make_async_remote_copy(src_ref=v_ref, dst_ref=buf.at[me], send_sem=send_sems.at[k], recv_sem=recv_sems.at[k],
                                              device_id=peer, device_id_type=MESH_ID)
            cp.start()
            sends.append(cp)
        for cp in sends:
            cp.wait()
        total = buf[0]
        for dev in range(1, N_DEV):
            total = total + buf[dev]
        out_ref[...] = total

    vm = pl.BlockSpec(memory_space=pltpu.VMEM)
    return pl.pallas_call(
        body, name="sum_devices", in_specs=[vm], out_specs=vm, out_shape=jax.ShapeDtypeStruct((rows, n), F32),
        scratch_shapes=[pltpu.VMEM((N_DEV, rows, n), F32), pltpu.SemaphoreType.DMA((N_DEV,)), pltpu.SemaphoreType.DMA((N_DEV,))],
    )(vec)


def _rope_tables(positions):
    half = ROPE_HALF
    inv_freq = 1.0 / (ROPE_THETA ** (jnp.arange(half, dtype=F32) / half))
    ang = positions.astype(F32).reshape(-1, 1) * inv_freq
    cos, sin = jnp.cos(ang), jnp.sin(ang)
    t = ang.shape[0]
    ones, zeros = jnp.ones((t, QK_NOPE), F32), jnp.zeros((t, QK_NOPE), F32)
    pad, none = HEAD_PAD - QK_DIM, zeros[:, :half]
    cos_full = jnp.concatenate([ones, cos, cos, ones[:, :pad]], axis=1)
    s_lo = jnp.concatenate([zeros, -sin, none, zeros[:, :pad]], axis=1)
    s_hi = jnp.concatenate([zeros, none, sin, zeros[:, :pad]], axis=1)
    return cos_full, s_lo, s_hi


def _partials(names, grads, from_sibling, place):
    return [_chip_partial(g, o, place, "chip_partial_" + n) for n, g, o in zip(names, grads, from_sibling)]


def _totals(names, grads, partials, received, place):
    return [_chip_total(pf, r, place, g.shape[1:], "chip_total_" + n) for n, g, (pf, _), r in zip(names, grads, partials, received)]


def _kernel_layouts(full):
    w_in = full["w_in"]
    w_kr = jnp.pad(w_in[ROW_KR:ROW_XC], ((QK_NOPE, HEAD_PAD - QK_DIM), (0, 0)))
    w_uq = jnp.pad(full["w_uq"].reshape(Q_LORA, N_HEADS, QK_DIM), ((0, 0), (0, 0), (0, HEAD_PAD - QK_DIM)))
    w_uk = jnp.pad(full["w_uk"].reshape(KV_LORA, N_HEADS, QK_NOPE), ((0, 0), (0, 0), (0, HEAD_PAD - QK_NOPE)))
    return {"w_in": w_in, "w_kr": w_kr, "w_uq": w_uq.reshape(Q_LORA, N_HEADS * HEAD_PAD),
            "w_uk": w_uk.reshape(KV_LORA, N_HEADS * HEAD_PAD), "w_uv": full["w_uv"], "w_uvt": full["w_uv"].T}


def _global_layouts(g):
    d = D_MODEL
    dm, dp = g["w_m"], g["w_p"]
    o_lat = 2 * d
    o_kr = o_lat + Q_LORA + KV_LORA + QK_NOPE
    w_in = jnp.concatenate([dp[o_lat:o_lat + Q_LORA + KV_LORA], dp[o_kr:o_kr + QK_ROPE], dp[:d], dm[:d], dp[d:o_lat], dm[d:]], axis=0)
    w_uq = g["w_uq"].reshape(Q_LORA, N_HEADS, HEAD_PAD)[:, :, :QK_DIM].reshape(Q_LORA, N_HEADS * QK_DIM)
    w_uk = g["w_uk"].reshape(KV_LORA, N_HEADS, HEAD_PAD)[:, :, :QK_NOPE].reshape(KV_LORA, N_HEADS * QK_NOPE)
    return {"w_in": w_in, "w_uq": w_uq, "w_uk": w_uk, "w_uv": g["w_uv"], "w_proj_attn": g["w_pa"], "w_proj_conv": g["w_pc"],
            "w_out": g["w_out"]}


def _col_blocks(a):
    r, c = a.shape
    return a.reshape(r, N_CHIPS, c // N_CHIPS).transpose(1, 0, 2)


def _from_col_blocks(a):
    n, r, c = a.shape
    return a.transpose(1, 0, 2).reshape(r, n * c)


COL_SHARDED = ("w_uq", "w_uk", "w_uv", "w_proj_attn")
TRANSPOSED = ("ffn1_w_gate", "ffn1_w_up", "ffn2_w_gate", "ffn2_w_up", "w_in")
SMALL = (("ffn1_norm", 1024), ("mix_norm", 1024), ("gate_bias", 2048), ("q_a_norm", 384), ("kv_a_norm", 256),
         ("q_head_norm", 128), ("k_head_norm", 128), ("ffn2_norm", 1024))
WEIGHT_ORDER = ("ffn1_norm", "ffn1_w_gate", "ffn1_w_up", "ffn1_w_down", "mix_norm", "w_in", "gate_bias", "q_a_norm", "w_uq",
                "kv_a_norm", "w_uk", "w_uv", "q_head_norm", "k_head_norm", "w_proj_attn", "conv_w", "w_proj_conv", "w_out",
                "ffn2_norm", "ffn2_w_gate", "ffn2_w_up", "ffn2_w_down")
MATRICES = ("ffn1_w_gate", "ffn1_w_up", "ffn1_w_down", "w_in", "w_uq", "w_uk", "w_uv", "w_proj_attn", "w_proj_conv", "w_out",
            "ffn2_w_gate", "ffn2_w_up", "ffn2_w_down")
GROUP_FFN1 = ("ffn1_w_gate", "ffn1_w_up", "ffn1_w_down")
GROUP_IN = ("w_in", "w_uq", "w_uk", "w_uv", "conv_w")
GROUP_MIX = ("w_proj_attn", "w_proj_conv", "w_out")
GROUP_FFN2 = ("ffn2_w_gate", "ffn2_w_up", "ffn2_w_down")
GROUP_MID = ("w_in", "w_uq", "w_uk", "w_uv", "w_proj_attn", "w_proj_conv", "w_out")


def _pad_lanes(a, n):
    return jnp.pad(a.reshape(1, -1), ((0, 0), (0, n - a.size)))


def kernel(x, positions, ffn1_norm, ffn1_w_gate, ffn1_w_up, ffn1_w_down, mix_norm, w_in, gate_bias, q_a_norm, w_uq, kv_a_norm, w_uk, w_uv, q_head_norm, k_head_norm, w_proj_attn, conv_w, w_proj_conv, w_out, ffn2_norm, ffn2_w_gate, ffn2_w_up, ffn2_w_down, loss_target, m_ffn1_norm, m_ffn1_w_gate, m_ffn1_w_up, m_ffn1_w_down, m_mix_norm, m_w_in, m_gate_bias, m_q_a_norm, m_w_uq, m_kv_a_norm, m_w_uk, m_w_uv, m_q_head_norm, m_k_head_norm, m_w_proj_attn, m_conv_w, m_w_proj_conv, m_w_out, m_ffn2_norm, m_ffn2_w_gate, m_ffn2_w_up, m_ffn2_w_down, v_ffn1_norm, v_ffn1_w_gate, v_ffn1_w_up, v_ffn1_w_down, v_mix_norm, v_w_in, v_gate_bias, v_q_a_norm, v_w_uq, v_kv_a_norm, v_w_uk, v_w_uv, v_q_head_norm, v_k_head_norm, v_w_proj_attn, v_conv_w, v_w_proj_conv, v_w_out, v_ffn2_norm, v_ffn2_w_gate, v_ffn2_w_up, v_ffn2_w_down):
    args = dict(locals())
    view = lambda n, a: a.T if n in TRANSPOSED else a
    weights = {n: view(n, args[n]) for n in WEIGHT_ORDER}
    moments_m = {n: view(n, args["m_" + n]) for n in WEIGHT_ORDER}
    moments_v = {n: view(n, args["v_" + n]) for n in WEIGHT_ORDER}
    nb, seq, d = x.shape
    t = nb * seq
    chip = (2 * lax.axis_index("x") + lax.axis_index("y")).astype(jnp.int32)
    place = jnp.stack([chip, lax.axis_index("c").astype(jnp.int32)])
    grads, delta, new_m, new_v = {}, {}, {}, {}

    def adamw(names, carried=None):
        results = _adamw([(weights[n], grads[n], moments_m[n], moments_v[n]) for n in names], "adamw_" + names[0], carried)
        for n, (dn, mn, vn) in zip(names, results):
            delta[n], new_m[n], new_v[n] = dn, mn, vn

    conv_rows = conv_w.shape[0]
    conv_shard = jnp.pad(conv_w, ((0, 16 - conv_rows), (0, 0)))
    bufs = dict(zip(MATRICES + ("conv_w",), _cast_shards([weights[n] for n in MATRICES] + [conv_shard],
                                                         [BF16] * len(MATRICES) + [F32])))
    blocks = dict(zip(GROUP_FFN1, _run(_gather_carried([bufs[n] for n in GROUP_FFN1]), "gather_ffn1")))
    p = {n: _pad_lanes(weights[n], size) for n, size in SMALL}
    rope = _rope_tables(positions)
    x_tok = x.reshape(t, d)

    gather_in = _gather_carried([bufs[n] for n in GROUP_IN])
    x1, gate1, up1, act1 = _ffn_fwd(x_tok, p["ffn1_norm"], blocks["ffn1_w_gate"], blocks["ffn1_w_up"], blocks["ffn1_w_down"], None,
                                    "ffn1_fwd", gather_in)
    blocks.update(zip(GROUP_IN, gather_in.results))
    w = _kernel_layouts({"w_in": blocks["w_in"].reshape(-1, d), **{n: _from_col_blocks(blocks[n]) for n in ("w_uq", "w_uk", "w_uv")}})
    p["conv_w"] = _from_col_blocks(blocks["conv_w"])[:conv_rows]

    gather_mix = _gather_carried([bufs[n] for n in GROUP_MIX])
    h2b, big, lat, q, k, v, vt = _inproj_fwd(x1, p["mix_norm"], w["w_in"], w["w_kr"], p["q_a_norm"], p["kv_a_norm"], p["q_head_norm"],
                                             p["k_head_norm"], w["w_uq"], w["w_uk"], w["w_uv"], w["w_uvt"], rope, gather_mix)
    blocks.update(zip(GROUP_MIX, gather_mix.results))
    w_pa = _from_col_blocks(blocks["w_proj_attn"])
    w_pc, w_out_full = blocks["w_proj_conv"].reshape(-1, d), blocks["w_out"].reshape(-1, d)

    gather_ffn2 = _gather_carried([bufs[n] for n in GROUP_FFN2[:2]])
    o, lse = _attn_fwd(q, k, vt, seq, gather_ffn2)
    gather_down = _gather_carried([bufs[n] for n in GROUP_FFN2[2:]])
    x2 = _mix_fwd(x1, o, big, p["gate_bias"], p["conv_w"], w_pa, w_pc, w_out_full, seq, gather_down)
    wg2, wu2 = gather_ffn2.results
    wd2, = gather_down.results
    dx3, gate2, up2, act2, loss = _ffn_fwd(x2, p["ffn2_norm"], wg2, wu2, wd2, loss_target.reshape(t, d), "ffn2_fwd")

    dx2, dg_ffn2, hb2, dgate2, dup2, dyb2 = _ffn_bwd_x(x2, p["ffn2_norm"], dx3, gate2, up2, wg2, wu2, wd2, "ffn2_bwd")
    g_ffn2 = [_tn_matmul(dgate2, hb2, "ffn2_dw_gate"), _tn_matmul(dup2, hb2, "ffn2_dw_up"), _tn_matmul(act2, dyb2, "ffn2_dw_down")]
    swap = _swap_carried(g_ffn2)
    do, delta_o, dz, dm, dbias, dw_pa, dw_pc, dw_out = _mix_bwd(dx2, o, big, p["gate_bias"], p["conv_w"], w_pa, w_pc, w_out_full, seq,
                                                                swap)
    part = _partials(GROUP_FFN2, g_ffn2, swap.results, place)
    send = _send_carried([pb for _, pb in part])
    dq, dk, dv = _attn_bwd(q, k, v, do, lse, delta_o.reshape(N_HEADS // ATTN_BWD_HEADS, ATTN_BWD_HEADS, -1), seq, send)
    join = _join_carried(_totals(GROUP_FFN2, g_ffn2, part, send.results, place))
    dp, dw_uq, dw_uk, dw_uv, dqa, dkva, dqh, dkh, dcw = _prep_bwd(
        lat, big, dz, dq, dk, dv, p["q_a_norm"], p["kv_a_norm"], p["q_head_norm"], p["k_head_norm"], w["w_uq"], w["w_uk"],
        w["w_uv"], rope, p["conv_w"], seq, join)
    grads.update(zip(GROUP_FFN2, join.results))

    gg = _global_layouts({"w_m": _tn_matmul(dm, h2b, "dw_in_m", split_k=2), "w_p": _tn_matmul(dp, h2b, "dw_in_p", split_k=2),
                          "w_uq": dw_uq, "w_uk": dw_uk, "w_uv": dw_uv, "w_pa": dw_pa, "w_pc": dw_pc, "w_out": dw_out})
    g_mid = [_col_blocks(gg[n]) if n in COL_SHARDED else gg[n].reshape(N_CHIPS, -1, gg[n].shape[-1]) for n in GROUP_MID]
    swap = _swap_carried(g_mid)
    dx1, dg_mix = _inproj_bwd(x1, p["mix_norm"], dx2, dm, dp, w["w_in"], w["w_kr"], swap)
    part = _partials(GROUP_MID, g_mid, swap.results, place)
    send = _send_carried([pb for _, pb in part])
    grad_x, dg_ffn1, hb1, dgate1, dup1, dyb1 = _ffn_bwd_x(x_tok, p["ffn1_norm"], dx1, gate1, up1, blocks["ffn1_w_gate"],
                                                         blocks["ffn1_w_up"], blocks["ffn1_w_down"], "ffn1_bwd", send)

    small_grads = {"ffn1_norm": dg_ffn1, "mix_norm": dg_mix, "gate_bias": dbias, "q_a_norm": dqa, "kv_a_norm": dkva,
                   "q_head_norm": dqh, "k_head_norm": dkh, "ffn2_norm": dg_ffn2}
    packed = jnp.concatenate([small_grads[n] for n, _ in SMALL] + [dcw.reshape(1, -1), loss], axis=1)
    total = _sum_devices(packed.reshape(8, -1)).reshape(1, -1)
    n_small = sum(size for _, size in SMALL)
    conv_cols = conv_w.shape[1]
    conv_total = total[:, n_small:n_small + conv_rows * d].reshape(conv_rows, d)
    grads["conv_w"] = lax.dynamic_slice_in_dim(conv_total, chip * conv_cols, conv_cols, axis=1)
    loss_total = total[0, n_small + conv_rows * d]

    join = _join_carried(_totals(GROUP_MID, g_mid, part, send.results, place))
    g_gate = _tn_matmul(dgate1, hb1, "ffn1_dw_gate", carried=join)
    grads.update(zip(GROUP_MID, join.results))
    swap_gate = _swap_carried([g_gate])
    g_up = _tn_matmul(dup1, hb1, "ffn1_dw_up", carried=swap_gate)
    part_gate = _partials(GROUP_FFN1[:1], [g_gate], swap_gate.results, place)
    send_gate, swap_up = _send_carried([part_gate[0][1]]), _swap_carried([g_up])
    g_down = _tn_matmul(act1, dyb1, "ffn1_dw_down", carried=_both(send_gate, swap_up))
    join_gate = _join_carried(_totals(GROUP_FFN1[:1], [g_gate], part_gate, send_gate.results, place))
    part_up = _partials(GROUP_FFN1[1:2], [g_up], swap_up.results, place)
    send_up, swap_down = _send_carried([part_up[0][1]]), _swap_carried([g_down])
    adamw(GROUP_FFN2, _both(_both(send_up, swap_down), join_gate))
    grads["ffn1_w_gate"] = join_gate.results[0]
    join_up = _join_carried(_totals(GROUP_FFN1[1:2], [g_up], part_up, send_up.results, place))
    part_down = _partials(GROUP_FFN1[2:], [g_down], swap_down.results, place)
    send_down = _send_carried([part_down[0][1]])
    adamw(("w_in",), _both(send_down, join_up))
    grads["ffn1_w_up"] = join_up.results[0]
    join_down = _join_carried(_totals(GROUP_FFN1[2:], [g_down], part_down, send_down.results, place))
    adamw(GROUP_FFN1[:2], join_down)
    grads["ffn1_w_down"] = join_down.results[0]
    adamw(GROUP_FFN1[2:])
    for n in GROUP_MID[1:] + ("conv_w",):
        adamw((n,))

    row = lambda a: a.reshape(1, -1)
    small = _adamw_small(total, [(row(weights[n]), row(moments_m[n]), row(moments_v[n])) for n, _ in SMALL], [size for _, size in SMALL])
    for (n, _), (gn, dn, mn, vn) in zip(SMALL, small):
        grads[n], delta[n], new_m[n], new_v[n] = gn.reshape(-1), dn.reshape(-1), mn.reshape(-1), vn.reshape(-1)

    return (loss_total, grad_x.reshape(nb, seq, d), *[view(n, src[n]) for src in (grads, delta, new_m, new_v) for n in WEIGHT_ORDER])
```

```python
import functools

import jax
import jax.numpy as jnp
from jax import lax
from jax.experimental import pallas as pl
from jax.experimental.pallas import tpu as pltpu

F32 = jnp.float32
BF16 = jnp.bfloat16

D_MODEL = 1024
N_HEADS = 8
QK_NOPE = 64
QK_ROPE = 32
QK_DIM = QK_NOPE + QK_ROPE
V_DIM = 64
HEAD_PAD = 128
Q_LORA = 384
KV_LORA = 256
ROPE_THETA = 10000.0
NORM_EPS = 1e-6
ATTN_SCALE = QK_DIM ** -0.5
MASK_VALUE = -1e30
N_CHIPS = 4
N_DEV = 8

ADAM_LR = 0.001
ADAM_B1 = 0.9
ADAM_B2 = 0.999
ADAM_EPS = 1e-08
ADAM_WD = 0.01
ADAM_STEP = 10

TOKEN_TILE = 256
PREP_TILE = 256
ATTN_TILE = 512
TN_TILE = 2048
VMEM_LIMIT = 56 * 1024 * 1024

M_COLS = 3 * D_MODEL
P_COLS = 2 * D_MODEL + Q_LORA + KV_LORA + HEAD_PAD
BIG_COLS = 5 * D_MODEL
LAT_COLS = Q_LORA + KV_LORA + HEAD_PAD

MESH_ID = pl.DeviceIdType.MESH
ANY = pl.BlockSpec(memory_space=pl.ANY)


def _params(semantics=None):
    return pltpu.CompilerParams(dimension_semantics=semantics, vmem_limit_bytes=VMEM_LIMIT)


class _Carried:
    def __init__(self, operands, out_shapes, aliases, n_sems, start, finish):
        self.operands, self.out_shapes, self.aliases, self.n_sems = list(operands), list(out_shapes), dict(aliases), n_sems
        self.start, self.finish = start, finish
        self.results = None


def _both(a, b):
    na, nao = len(a.operands), len(a.out_shapes)

    def start(ins, outs, sems, base):
        a.start(ins[:na], outs[:nao], sems, base)
        b.start(ins[na:], outs[nao:], sems, base + a.n_sems)

    def finish(ins, outs, sems, base):
        a.finish(ins[:na], outs[:nao], sems, base)
        b.finish(ins[na:], outs[nao:], sems, base + a.n_sems)

    aliases = dict(a.aliases)
    aliases.update({na + i: nao + o for i, o in b.aliases.items()})
    both = _Carried(a.operands + b.operands, a.out_shapes + b.out_shapes, aliases, a.n_sems + b.n_sems, start, finish)
    both.parts = (a, b)
    return both


def _set_results(carried, results):
    carried.results = list(results)
    if hasattr(carried, "parts"):
        a, b = carried.parts
        _set_results(a, results[:len(a.out_shapes)])
        _set_results(b, results[len(a.out_shapes):])


def _pallas(body, name, grid, in_specs, out_specs, out_shape, args, semantics, carried=None):
    if carried is None:
        return pl.pallas_call(body, name=name, grid=grid, in_specs=in_specs, out_specs=out_specs, out_shape=out_shape,
                              compiler_params=_params(semantics))(*args)
    n_in, n_out, n_ci, n_co = len(in_specs), len(out_specs), len(carried.operands), len(carried.out_shapes)

    def wrapped(*refs):
        ins, c_ins = refs[:n_in], refs[n_in:n_in + n_ci]
        outs, c_outs = refs[n_in + n_ci:n_in + n_ci + n_out], refs[n_in + n_ci + n_out:n_in + n_ci + n_out + n_co]
        sems = refs[-1]
        first = pl.program_id(0) == 0
        last = pl.program_id(0) == grid[0] - 1
        for axis in range(1, len(grid)):
            first = jnp.logical_and(first, pl.program_id(axis) == 0)
            last = jnp.logical_and(last, pl.program_id(axis) == grid[axis] - 1)

        @pl.when(first)
        def _():
            carried.start(c_ins, c_outs, sems, 0)

        body(*ins, *outs)

        @pl.when(last)
        def _():
            carried.finish(c_ins, c_outs, sems, 0)

    results = pl.pallas_call(
        wrapped, name=name, grid=grid, in_specs=list(in_specs) + [ANY] * n_ci, out_specs=list(out_specs) + [ANY] * n_co,
        out_shape=list(out_shape) + carried.out_shapes,
        input_output_aliases={n_in + i: n_out + o for i, o in carried.aliases.items()},
        scratch_shapes=[pltpu.SemaphoreType.DMA((carried.n_sems,))], compiler_params=_params(semantics))(*args, *carried.operands)
    _set_results(carried, results[n_out:])
    return results[:n_out]


def _run(carried, name):
    n_ci, n_co = len(carried.operands), len(carried.out_shapes)

    def body(*refs):
        carried.start(refs[:n_ci], refs[n_ci:n_ci + n_co], refs[-1], 0)
        carried.finish(refs[:n_ci], refs[n_ci:n_ci + n_co], refs[-1], 0)

    results = pl.pallas_call(body, name=name, in_specs=[ANY] * n_ci, out_specs=[ANY] * n_co, out_shape=carried.out_shapes,
                             input_output_aliases=carried.aliases,
                             scratch_shapes=[pltpu.SemaphoreType.DMA((carried.n_sems,))])(*carried.operands)
    _set_results(carried, results)
    return carried.results


def _resident(shape):
    nd = len(shape)
    return pl.BlockSpec(shape, lambda *_: (0,) * nd, pipeline_mode=pl.Buffered(1))


def _const(shape):
    nd = len(shape)
    return pl.BlockSpec(shape, lambda *_: (0,) * nd)


def _mm(a, b):
    return jnp.dot(a, b, preferred_element_type=F32)


def _mm_nt(a, b):
    return lax.dot_general(a, b, (((1,), (1,)), ((), ())), preferred_element_type=F32)


def _mm_tn(a, b):
    return lax.dot_general(a, b, (((0,), (0,)), ((), ())), preferred_element_type=F32)


def _bf(a):
    return a.astype(BF16)


def _sigmoid(a):
    return 1.0 / (1.0 + jnp.exp(-a))


def _rms(x, gain, n=None):
    n = x.shape[-1] if n is None else n
    r = lax.rsqrt(jnp.sum(x * x, axis=-1, keepdims=True) * (1.0 / n) + NORM_EPS)
    return (x * r) * gain, r


def _rms_bwd(x, r, gain, dh, n=None):
    n = x.shape[-1] if n is None else n
    u = dh * gain
    dx = r * u - x * ((r * r * r) * (jnp.sum(u * x, axis=-1, keepdims=True) * (1.0 / n)))
    dgain = jnp.sum(dh * (x * r), axis=0, keepdims=True)
    return dx, dgain


ROPE_HALF = QK_ROPE // 2


def _rope(t, rope):
    cos, s_lo, s_hi = rope
    return t * cos + pltpu.roll(t, HEAD_PAD - ROPE_HALF, 1) * s_lo + pltpu.roll(t, ROPE_HALF, 1) * s_hi


def _rope_bwd(dt, rope):
    cos, s_lo, s_hi = rope
    return dt * cos + pltpu.roll(dt * s_lo, ROPE_HALF, 1) + pltpu.roll(dt * s_hi, HEAD_PAD - ROPE_HALF, 1)


def _shift_down(u, prev8, k):
    s = pltpu.roll(u, k, 0)
    p = pltpu.roll(prev8, k, 0)
    row = lax.broadcasted_iota(jnp.int32, prev8.shape, 0)
    top = jnp.where(row < k, p, s[:8])
    return jnp.concatenate([top, s[8:]], axis=0)


def _shift_up(d, next8, k):
    tm = d.shape[0]
    s = pltpu.roll(d, tm - k, 0)
    n = pltpu.roll(next8, 8 - k, 0)
    row = lax.broadcasted_iota(jnp.int32, next8.shape, 0)
    bot = jnp.where(row >= 8 - k, n, s[tm - 8:])
    return jnp.concatenate([s[:tm - 8], bot], axis=0)


def _ffn_fwd(x, gain, wg, wu, wd, target, name, carried=None):
    t, d = x.shape
    nb, f, _ = wg.shape
    tm = TOKEN_TILE
    with_loss = target is not None

    def body(*refs):
        if with_loss:
            x_ref, g_ref, wg_ref, wu_ref, wd_ref, t_ref, out_ref, gate_ref, up_ref, act_ref, loss_ref = refs
        else:
            x_ref, g_ref, wg_ref, wu_ref, wd_ref, out_ref, gate_ref, up_ref, act_ref = refs
        xv = x_ref[...]
        h, _ = _rms(xv, g_ref[...])
        hb = _bf(h)
        y = jnp.zeros((tm, d), F32)
        nxt = (_mm_nt(hb, wg_ref[0]), _mm_nt(hb, wu_ref[0]))
        for j in range(nb):
            gate, up = nxt
            if j + 1 < nb:
                nxt = (_mm_nt(hb, wg_ref[j + 1]), _mm_nt(hb, wu_ref[j + 1]))
            act = _bf((gate * _sigmoid(gate)) * up)
            y = y + _mm(act, wd_ref[j])
            gate_ref[j] = _bf(gate)
            up_ref[j] = _bf(up)
            act_ref[j] = act
        out = xv + 0.5 * y
        if with_loss:
            err = out - t_ref[...]
            out_ref[...] = err * (1.0 / d)

            @pl.when(pl.program_id(0) == 0)
            def _():
                loss_ref[...] = jnp.zeros_like(loss_ref)

            part = jnp.sum(jnp.sum(err * err, axis=1, keepdims=True), axis=0, keepdims=True)
            loss_ref[...] += jnp.broadcast_to(part * (0.5 / d), loss_ref.shape)
        else:
            out_ref[...] = out

    tok = pl.BlockSpec((tm, d), lambda i: (i, 0))
    blk = pl.BlockSpec((nb, tm, f), lambda i: (0, i, 0))
    in_specs = [tok, _const((1, d)), _resident(wg.shape), _resident(wu.shape), _resident(wd.shape)]
    args = [x, gain, wg, wu, wd]
    out_shape = [jax.ShapeDtypeStruct((t, d), F32)] + [jax.ShapeDtypeStruct((nb, t, f), BF16)] * 3
    out_specs = [tok, blk, blk, blk]
    if with_loss:
        in_specs.append(tok)
        args.append(target)
        out_shape.append(jax.ShapeDtypeStruct((1, 128), F32))
        out_specs.append(_const((1, 128)))
    return _pallas(body, name, (t // tm,), in_specs, out_specs, out_shape, args, ("arbitrary",), carried)


def _ffn_bwd_x(x, gain, dout, gate, up, wg, wu, wd, name, carried=None):
    t, d = x.shape
    nb, f, _ = wg.shape
    tm = TOKEN_TILE

    def body(x_ref, g_ref, dout_ref, gate_ref, up_ref, wg_ref, wu_ref, wd_ref,
             dx_ref, dgain_ref, hb_ref, dgate_ref, dup_ref, dyb_ref):
        xv = x_ref[...]
        gain_v = g_ref[...]
        h, r = _rms(xv, gain_v)
        hb_ref[...] = _bf(h)
        dout_v = dout_ref[...]
        dyb = _bf(0.5 * dout_v)
        dyb_ref[...] = dyb
        dh = jnp.zeros((tm, d), F32)
        nxt = _mm_nt(dyb, wd_ref[0])
        for j in range(nb):
            dact = nxt
            if j + 1 < nb:
                nxt = _mm_nt(dyb, wd_ref[j + 1])
            gt = gate_ref[j].astype(F32)
            uv = up_ref[j].astype(F32)
            s = _sigmoid(gt)
            dup = _bf(dact * (gt * s))
            dgate = _bf((dact * uv) * (s * (1.0 + gt * (1.0 - s))))
            dh = dh + _mm(dgate, wg_ref[j]) + _mm(dup, wu_ref[j])
            dgate_ref[j] = dgate
            dup_ref[j] = dup
        dxn, dgain = _rms_bwd(xv, r, gain_v, dh)
        dx_ref[...] = dout_v + dxn

        @pl.when(pl.program_id(0) == 0)
        def _():
            dgain_ref[...] = jnp.zeros_like(dgain_ref)

        dgain_ref[...] += dgain

    tok = pl.BlockSpec((tm, d), lambda i: (i, 0))
    blk = pl.BlockSpec((nb, tm, f), lambda i: (0, i, 0))
    return _pallas(
        body, name, (t // tm,),
        [tok, _const((1, d)), tok, blk, blk, _resident(wg.shape), _resident(wu.shape), _resident(wd.shape)],
        [tok, _const((1, d)), tok, blk, blk, tok],
        [jax.ShapeDtypeStruct((t, d), F32), jax.ShapeDtypeStruct((1, d), F32), jax.ShapeDtypeStruct((t, d), BF16),
         jax.ShapeDtypeStruct((nb, t, f), BF16), jax.ShapeDtypeStruct((nb, t, f), BF16), jax.ShapeDtypeStruct((t, d), BF16)],
        (x, gain, dout, gate, up, wg, wu, wd), ("arbitrary",), carried)


def _tn_matmul(a, b, name, split_k=1, carried=None):
    t = a.shape[-2]
    k = a.shape[-1]
    n = b.shape[-1]
    tt = min(TN_TILE, t)
    nt = t // tt

    def body(a_ref, b_ref, o_ref):
        @pl.when(pl.program_id(1) == 0)
        def _():
            o_ref[...] = jnp.zeros_like(o_ref)

        o_ref[...] += _mm_tn(a_ref[...], b_ref[...])

    if split_k > 1:
        assert a.ndim == 2 and b.ndim == 2 and k % (split_k * 128) == 0
        tk = k // split_k
        g = split_k
        a_spec = pl.BlockSpec((tt, tk), lambda gi, ti: (ti, gi))
        b_spec = pl.BlockSpec((tt, n), lambda gi, ti: (ti, 0))
        o_spec = pl.BlockSpec((tk, n), lambda gi, ti: (gi, 0))
        out_shape = jax.ShapeDtypeStruct((k, n), F32)
    else:
        g = a.shape[0] if a.ndim == 3 else b.shape[0]
        a_spec = (pl.BlockSpec((None, tt, k), lambda gi, ti: (gi, ti, 0)) if a.ndim == 3
                  else pl.BlockSpec((tt, k), lambda gi, ti: (ti, 0)))
        b_spec = (pl.BlockSpec((None, tt, n), lambda gi, ti: (gi, ti, 0)) if b.ndim == 3
                  else pl.BlockSpec((tt, n), lambda gi, ti: (ti, 0)))
        o_spec = pl.BlockSpec((None, k, n), lambda gi, ti: (gi, 0, 0))
        out_shape = jax.ShapeDtypeStruct((g, k, n), F32)
    return _pallas(body, name, (g, nt), [a_spec, b_spec], [o_spec], [out_shape], (a, b), ("arbitrary", "arbitrary"), carried)[0]


ROW_QKV, ROW_KR, ROW_XC = 0, Q_LORA + KV_LORA, Q_LORA + KV_LORA + QK_ROPE
ROW_GB, ROW_GC, ROW_GL = ROW_XC + D_MODEL, ROW_XC + 2 * D_MODEL, ROW_XC + 3 * D_MODEL
BIG_FROM_ROWS = ((0, ROW_GB, D_MODEL), (D_MODEL, ROW_GL, 2 * D_MODEL), (3 * D_MODEL, ROW_XC, D_MODEL), (4 * D_MODEL, ROW_GC, D_MODEL))


def _inproj_fwd(x1, gain, w_in, w_kr, qa_gain, kva_gain, qh_gain, kh_gain, w_uq, w_uk, w_uv, w_uvt, rope, carried=None):
    t, d = x1.shape
    tm = TOKEN_TILE
    chunk = 512
    chunks = []
    for col, row, size in BIG_FROM_ROWS:
        chunks += [(col + o, row + o, chunk) for o in range(0, size, chunk)]
    of_head = [[c for k, c in enumerate(chunks) if k * N_HEADS // len(chunks) == hd] for hd in range(N_HEADS)]

    def body(x_ref, g_ref, win_ref, wkr_ref, qa_ref, kva_ref, qh_ref, kh_ref, wuq_ref, wuk_ref, wuv_ref, wuvt_ref, cos_ref, slo_ref,
             shi_ref, hb_ref, big_ref, lat_ref, q_ref, k_ref, v_ref, vt_ref):
        h, _ = _rms(x_ref[...], g_ref[...])
        hb = _bf(h)
        hb_ref[...] = hb
        k_rope = _mm_nt(hb, wkr_ref[...])
        lat = jnp.concatenate([_mm_nt(hb, win_ref[ROW_QKV:ROW_KR, :]), k_rope], axis=1)
        lat_ref[...] = lat
        cq, _ = _rms(lat[:, :Q_LORA], qa_ref[...])
        ckv, _ = _rms(lat[:, Q_LORA:Q_LORA + KV_LORA], kva_ref[...])
        cqb = _bf(cq)
        ckvb = _bf(ckv)
        rope_v = (cos_ref[...], slo_ref[...], shi_ref[...])
        q_all = _mm(cqb, wuq_ref[...])
        k_all = _mm(ckvb, wuk_ref[...])
        v_ref[...] = _bf(_mm(ckvb, wuv_ref[...]))
        vt_all = _mm_nt(wuvt_ref[...], ckvb)
        for hd in range(N_HEADS):
            for col, row, size in of_head[hd]:
                big_ref[:, col:col + size] = _mm_nt(hb, win_ref[row:row + size, :])
            lanes = slice(hd * HEAD_PAD, (hd + 1) * HEAD_PAD)
            qn, _ = _rms(q_all[:, lanes], qh_ref[...], QK_DIM)
            q_ref[hd] = _bf(_rope(qn, rope_v))
            kn, _ = _rms(k_all[:, lanes] + k_rope, kh_ref[...], QK_DIM)
            k_ref[hd] = _bf(_rope(kn, rope_v))
            vt_ref[hd] = _bf(vt_all[hd * V_DIM:(hd + 1) * V_DIM])

    tok = lambda c: pl.BlockSpec((tm, c), lambda i: (i, 0))
    head = lambda c: pl.BlockSpec((N_HEADS, tm, c), lambda i: (0, i, 0))
    return _pallas(
        body, "inproj_fwd", (t // tm,),
        [tok(d), _const((1, d)), _resident(w_in.shape), _resident(w_kr.shape), _const((1, Q_LORA)), _const((1, KV_LORA)),
         _const((1, HEAD_PAD)), _const((1, HEAD_PAD)), _resident(w_uq.shape), _resident(w_uk.shape),
         _resident(w_uv.shape), _resident(w_uvt.shape), tok(HEAD_PAD), tok(HEAD_PAD), tok(HEAD_PAD)],
        [tok(d), tok(BIG_COLS), tok(LAT_COLS), head(HEAD_PAD), head(HEAD_PAD), tok(N_HEADS * V_DIM),
         pl.BlockSpec((N_HEADS, V_DIM, tm), lambda i: (0, 0, i))],
        [jax.ShapeDtypeStruct((t, d), BF16), jax.ShapeDtypeStruct((t, BIG_COLS), F32),
         jax.ShapeDtypeStruct((t, LAT_COLS), F32), jax.ShapeDtypeStruct((N_HEADS, t, HEAD_PAD), BF16),
         jax.ShapeDtypeStruct((N_HEADS, t, HEAD_PAD), BF16), jax.ShapeDtypeStruct((t, N_HEADS * V_DIM), BF16),
         jax.ShapeDtypeStruct((N_HEADS, V_DIM, t), BF16)],
        (x1, gain, w_in, w_kr, qa_gain, kva_gain, qh_gain, kh_gain, w_uq, w_uk, w_uv, w_uvt, *rope), ("arbitrary",), carried)


EXP2_SCALE = ATTN_SCALE * 1.4426950408889634


def _diagonal_keep(tk, tq):
    return lax.broadcasted_iota(jnp.int32, (tk, tq), 0) <= lax.broadcasted_iota(jnp.int32, (tk, tq), 1)


def _attn_fwd(q, k, vt, seq, carried=None):
    _, t, _ = q.shape
    nseq = t // seq
    tq = tk = ATTN_TILE
    nq = seq // tq

    def body(q_ref, k_ref, vt_ref, o_ref, lse_ref):
        i = pl.program_id(1)
        qs = [q_ref[h] for h in range(N_HEADS)]
        keep = _diagonal_keep(tk, tq)

        def scores(h, k0):
            return _mm_nt(k_ref[h, pl.ds(k0, tk), :], qs[h])

        def update(h, st, state, k0, diagonal):
            m, l, acc = state
            if diagonal:
                st = jnp.where(keep, st, MASK_VALUE)
            m_new = jnp.maximum(m, jnp.max(st, axis=0, keepdims=True))
            pt = jnp.exp2((st - m_new) * EXP2_SCALE)
            alpha = jnp.exp2((m - m_new) * EXP2_SCALE)
            l_new = alpha * l + jnp.sum(pt, axis=0, keepdims=True)
            return m_new, l_new, alpha * acc + _mm(vt_ref[h, :, pl.ds(k0, tk)], _bf(pt))

        def tiles(states, k0, diagonal):
            st, new = scores(0, k0), []
            for h in range(N_HEADS):
                st_next = scores(h + 1, k0) if h + 1 < N_HEADS else None
                new.append(update(h, st, states[h], k0, diagonal))
                st = st_next
            return tuple(new)

        init = tuple((jnp.full((1, tq), MASK_VALUE, F32), jnp.zeros((1, tq), F32), jnp.zeros((V_DIM, tq), F32))
                     for _ in range(N_HEADS))
        states = lax.fori_loop(0, i, lambda j, s: tiles(s, pl.multiple_of(j * tk, tk), False), init)
        states = tiles(states, pl.multiple_of(i * tk, tk), True)
        outs = []
        for h in range(N_HEADS):
            m, l, acc = states[h]
            outs.append((acc / l).T)
            lse_ref[h] = m * EXP2_SCALE + jnp.log2(l)
        o_ref[...] = _bf(jnp.concatenate(outs, axis=-1))

    return _pallas(
        body, "attn_fwd", (nseq, nq),
        [pl.BlockSpec((N_HEADS, tq, HEAD_PAD), lambda b, i: (0, b * nq + i, 0)),
         pl.BlockSpec((N_HEADS, seq, HEAD_PAD), lambda b, i: (0, b, 0)),
         pl.BlockSpec((N_HEADS, V_DIM, seq), lambda b, i: (0, 0, b))],
        [pl.BlockSpec((tq, N_HEADS * V_DIM), lambda b, i: (b * nq + i, 0)),
         pl.BlockSpec((N_HEADS, 1, tq), lambda b, i: (0, 0, b * nq + i))],
        [jax.ShapeDtypeStruct((t, N_HEADS * V_DIM), BF16), jax.ShapeDtypeStruct((N_HEADS, 1, t), F32)],
        (q, k, vt), ("arbitrary", "arbitrary"), carried)


ATTN_BWD_HEADS = 4


def _attn_bwd(q, k, v, do, lse, delta, seq, carried=None):
    _, t, _ = q.shape
    nseq = t // seq
    tq = tk = ATTN_TILE
    n = seq // tq
    hb = ATTN_BWD_HEADS

    def body(q_ref, k_ref, v_ref, do_ref, lse_ref, delta_ref, dq_ref, dk_ref, dv_ref):
        dq_ref[...] = jnp.zeros_like(dq_ref)
        dk_ref[...] = jnp.zeros_like(dk_ref)
        dv_ref[...] = jnp.zeros_like(dv_ref)
        keep = _diagonal_keep(tk, tq)

        def tile(h, k0, q0, diagonal):
            kj = k_ref[h, pl.ds(k0, tk), :]
            qi = q_ref[h, pl.ds(q0, tq), :]
            doi = _bf(do_ref[pl.ds(q0, tq), h * V_DIM:(h + 1) * V_DIM])
            st = _mm_nt(kj, qi)
            if diagonal:
                st = jnp.where(keep, st, MASK_VALUE)
            pt = jnp.exp2(st * EXP2_SCALE - lse_ref[h, :, pl.ds(q0, tq)])
            dv_ref[pl.ds(k0, tk), h * V_DIM:(h + 1) * V_DIM] += _mm(_bf(pt), doi)
            dpt = _mm_nt(v_ref[pl.ds(k0, tk), h * V_DIM:(h + 1) * V_DIM], doi)
            dst = _bf((pt * (dpt - delta_ref[pl.ds(h, 1), pl.ds(q0, tq)])) * ATTN_SCALE)
            dk_ref[h, pl.ds(k0, tk), :] += _mm(dst, qi)
            dq_ref[h, pl.ds(q0, tq), :] += _mm_tn(dst, kj)

        def kv_step(j, _):
            k0 = pl.multiple_of(j * tk, tk)
            for h in range(hb):
                tile(h, k0, k0, True)

            def q_step(i, _):
                q0 = pl.multiple_of(i * tq, tq)
                for h in range(hb):
                    tile(h, k0, q0, False)
                return 0

            lax.fori_loop(j + 1, n, q_step, 0)
            return 0

        lax.fori_loop(0, n, kv_step, 0)

    hspec = lambda c: pl.BlockSpec((hb, seq, c), lambda b, g: (g, b, 0))
    cols = pl.BlockSpec((seq, hb * V_DIM), lambda b, g: (b, g))
    return _pallas(
        body, "attn_bwd", (nseq, N_HEADS // hb),
        [hspec(HEAD_PAD), hspec(HEAD_PAD), cols, cols,
         pl.BlockSpec((hb, 1, seq), lambda b, g: (g, 0, b)), pl.BlockSpec((None, hb, seq), lambda b, g: (g, 0, b))],
        [hspec(HEAD_PAD), hspec(HEAD_PAD), cols],
        [jax.ShapeDtypeStruct((N_HEADS, t, HEAD_PAD), F32), jax.ShapeDtypeStruct((N_HEADS, t, HEAD_PAD), F32),
         jax.ShapeDtypeStruct((t, N_HEADS * V_DIM), F32)],
        (q, k, v, do, lse, delta), ("arbitrary", "arbitrary"), carried)


def _mixer_values(o_ref, gb_ref, gla_ref, glb_ref, xc_ref, gc_ref, xcp_ref, gcp_ref, bias_ref, cw_ref, wpa_ref, wpc_ref,
                  first_of_seq, early=None):
    y_a = _mm(o_ref[...], wpa_ref[...])
    extra = early() if early is not None else None
    gb = gb_ref[...]
    u = gc_ref[...] * xc_ref[...]
    u_prev = jnp.where(first_of_seq, 0.0, gcp_ref[...] * xcp_ref[...])
    cw = cw_ref[...]
    z = cw[2:3] * u + cw[1:2] * _shift_down(u, u_prev, 1) + cw[0:1] * _shift_down(u, u_prev, 2)
    gbz = _bf(gb * z)
    y_b = _mm(gbz, wpc_ref[...])
    bias = bias_ref[...]
    gate_a = _sigmoid(gla_ref[...] + bias[:, :D_MODEL])
    gate_b = _sigmoid(glb_ref[...] + bias[:, D_MODEL:])
    merged = _bf(gate_a * y_a + gate_b * y_b)
    return gb, u, z, gbz, y_a, y_b, gate_a, gate_b, merged, extra


def _mixer_specs(tm, seq):
    d = D_MODEL
    tok = pl.BlockSpec((tm, d), lambda i: (i, 0))
    col = lambda c: pl.BlockSpec((tm, d), lambda i: (i, c))
    prev = lambda c: pl.BlockSpec((8, d), lambda i: (jnp.maximum(i * (tm // 8) - 1, 0), c))
    o_spec = pl.BlockSpec((tm, N_HEADS * V_DIM), lambda i: (i, 0))
    fwd_specs = [o_spec, col(0), col(1), col(2), col(3), col(4), prev(3), prev(4), _const((1, 2 * d)), _const((3, d)),
                 _resident((N_HEADS * V_DIM, d)), _resident((d, d)), _resident((d, d))]
    return tok, fwd_specs


def _mix_fwd(x1, o, big, gate_bias, conv_w, w_pa, w_pc, w_out, seq, carried=None):
    t, d = x1.shape
    tm = TOKEN_TILE
    tiles_per_seq = seq // tm

    def body(x_ref, o_ref, gb_ref, gla_ref, glb_ref, xc_ref, gc_ref, xcp_ref, gcp_ref, bias_ref, cw_ref, wpa_ref, wpc_ref,
             wout_ref, x2_ref):
        first = pl.program_id(0) % tiles_per_seq == 0
        merged = _mixer_values(o_ref, gb_ref, gla_ref, glb_ref, xc_ref, gc_ref, xcp_ref, gcp_ref, bias_ref, cw_ref, wpa_ref,
                               wpc_ref, first)[-2]
        x2_ref[...] = x_ref[...] + _mm(merged, wout_ref[...])

    tok, fwd_specs = _mixer_specs(tm, seq)
    return _pallas(body, "mix_fwd", (t // tm,), [tok] + fwd_specs, [tok], [jax.ShapeDtypeStruct((t, d), F32)],
                   (x1, o, big, big, big, big, big, big, big, gate_bias, conv_w, w_pa, w_pc, w_out), ("arbitrary",), carried)[0]


def _mix_bwd(dx2, o, big, gate_bias, conv_w, w_pa, w_pc, w_out, seq, carried=None):
    t, d = dx2.shape
    tm = TOKEN_TILE
    tiles_per_seq = seq // tm
    hv = N_HEADS * V_DIM

    def body(dx_ref, o_ref, gb_ref, gla_ref, glb_ref, xc_ref, gc_ref, xcp_ref, gcp_ref, bias_ref, cw_ref, wpa_ref, wpc_ref,
             wout_ref, do_ref, delta_ref, dz_ref, dm_ref, dbias_ref, dwpa_ref, dwpc_ref, dwout_ref):
        @pl.when(pl.program_id(0) == 0)
        def _():
            dbias_ref[...] = jnp.zeros_like(dbias_ref)
            dwpa_ref[...] = jnp.zeros_like(dwpa_ref)
            dwpc_ref[...] = jnp.zeros_like(dwpc_ref)
            dwout_ref[...] = jnp.zeros_like(dwout_ref)

        first = pl.program_id(0) % tiles_per_seq == 0
        dxb = _bf(dx_ref[...])
        gb, _, z, gbz, y_a, y_b, gate_a, gate_b, merged, dmerged = _mixer_values(
            o_ref, gb_ref, gla_ref, glb_ref, xc_ref, gc_ref, xcp_ref, gcp_ref, bias_ref, cw_ref, wpa_ref, wpc_ref, first,
            early=lambda: _mm_nt(dxb, wout_ref[...]))
        dwout_ref[...] += _mm_tn(merged, dxb)
        dya = _bf(dmerged * gate_a)
        dyb = _bf(dmerged * gate_b)
        do_v = _mm_nt(dya, wpa_ref[...])
        dgz = _mm_nt(dyb, wpc_ref[...])
        dwpa_ref[...] += _mm_tn(o_ref[...], dya)
        dwpc_ref[...] += _mm_tn(gbz, dyb)
        dla = (dmerged * y_a) * (gate_a * (1.0 - gate_a))
        dlb = (dmerged * y_b) * (gate_b * (1.0 - gate_b))
        dbias_ref[:, :d] += jnp.sum(dla, axis=0, keepdims=True)
        dbias_ref[:, d:] += jnp.sum(dlb, axis=0, keepdims=True)
        dm_ref[:, d:2 * d] = _bf(dla)
        dm_ref[:, 2 * d:] = _bf(dlb)
        do_ref[...] = do_v
        head = lax.broadcasted_iota(jnp.int32, (N_HEADS, hv), 0) * V_DIM
        col = lax.broadcasted_iota(jnp.int32, (N_HEADS, hv), 1)
        in_head = ((col >= head) & (col < head + V_DIM)).astype(F32)
        delta_ref[...] = lax.dot_general(in_head, do_v * o_ref[...].astype(F32), (((1,), (1,)), ((), ())),
                                         precision=lax.Precision.HIGHEST, preferred_element_type=F32)
        dz_ref[...] = dgz * gb
        dm_ref[:, :d] = _bf(dgz * z)

    tok, fwd_specs = _mixer_specs(tm, seq)
    return _pallas(
        body, "mix_bwd", (t // tm,), [tok] + fwd_specs,
        [pl.BlockSpec((tm, hv), lambda i: (i, 0)), pl.BlockSpec((N_HEADS, tm), lambda i: (0, i)), tok,
         pl.BlockSpec((tm, M_COLS), lambda i: (i, 0)), _const((1, 2 * d)), _const((hv, d)), _const((d, d)), _const((d, d))],
        [jax.ShapeDtypeStruct((t, hv), F32), jax.ShapeDtypeStruct((N_HEADS, t), F32), jax.ShapeDtypeStruct((t, d), F32),
         jax.ShapeDtypeStruct((t, M_COLS), BF16), jax.ShapeDtypeStruct((1, 2 * d), F32), jax.ShapeDtypeStruct((hv, d), F32),
         jax.ShapeDtypeStruct((d, d), F32), jax.ShapeDtypeStruct((d, d), F32)],
        (dx2, o, big, big, big, big, big, big, big, gate_bias, conv_w, w_pa, w_pc, w_out), ("arbitrary",), carried)


def _prep_bwd(lat, big, dz, dq, dk, dv, qa_gain, kva_gain, qh_gain, kh_gain, w_uq, w_uk, w_uv, rope, conv_w, seq, carried=None):
    t = lat.shape[0]
    d = D_MODEL
    tm = PREP_TILE
    tiles_per_seq = seq // tm
    last_blk = t // 8 - 1

    def body(lat_ref, xc_ref, gc_ref, dz_ref, dzn_ref, dq_ref, dk_ref, dv_ref, qa_ref, kva_ref, qh_ref, kh_ref, wuq_ref, wuk_ref,
             wuv_ref, cos_ref, slo_ref, shi_ref, cw_ref,
             dp_ref, dwuq_ref, dwuk_ref, dwuv_ref, dqa_ref, dkva_ref, dqh_ref, dkh_ref, dcw_ref):
        pid = pl.program_id(0)

        @pl.when(pid == 0)
        def _():
            for r in (dwuq_ref, dwuk_ref, dwuv_ref, dqa_ref, dkva_ref, dqh_ref, dkh_ref, dcw_ref):
                r[...] = jnp.zeros_like(r)

        lat_v = lat_ref[...]
        q_lat = lat_v[:, :Q_LORA]
        kv_lat = lat_v[:, Q_LORA:Q_LORA + KV_LORA]
        k_rope = lat_v[:, Q_LORA + KV_LORA:]
        qa_gain_v = qa_ref[...]
        kva_gain_v = kva_ref[...]
        qh_gain_v = qh_ref[...]
        kh_gain_v = kh_ref[...]
        cq, rq = _rms(q_lat, qa_gain_v)
        ckv, rkv = _rms(kv_lat, kva_gain_v)
        cqb = _bf(cq)
        ckvb = _bf(ckv)
        rope_v = (cos_ref[...], slo_ref[...], shi_ref[...])
        lane = lax.broadcasted_iota(jnp.int32, (tm, HEAD_PAD), 1)
        rope_lanes = (lane >= QK_NOPE) & (lane < QK_DIM)
        dk_rope = jnp.zeros((tm, HEAD_PAD), F32)
        dqh_gain = jnp.zeros((1, HEAD_PAD), F32)
        dkh_gain = jnp.zeros((1, HEAD_PAD), F32)
        q_all = _mm(cqb, wuq_ref[...])
        k_all = _mm(ckvb, wuk_ref[...])
        dvb = _bf(dv_ref[...])
        dckv = _mm_nt(dvb, wuv_ref[...])
        dwuv_ref[...] += _mm_tn(ckvb, dvb)

        last = pid % tiles_per_seq == tiles_per_seq - 1
        dzv = dz_ref[...]
        dz_next = jnp.where(last, 0.0, dzn_ref[...])
        dz1 = _shift_up(dzv, dz_next, 1)
        dz2 = _shift_up(dzv, dz_next, 2)
        cw = cw_ref[...]
        xc = xc_ref[...]
        gc = gc_ref[...]
        u = gc * xc
        du = cw[2:3] * dzv + cw[1:2] * dz1 + cw[0:1] * dz2
        dp_ref[:, :d] = _bf(du * gc)
        dp_ref[:, d:2 * d] = _bf(du * xc)
        dcw_ref[0:1, :] += jnp.sum(dz2 * u, axis=0, keepdims=True)
        dcw_ref[1:2, :] += jnp.sum(dz1 * u, axis=0, keepdims=True)
        dcw_ref[2:3, :] += jnp.sum(dzv * u, axis=0, keepdims=True)

        dcq = jnp.zeros((tm, Q_LORA), F32)
        half = N_HEADS // 2
        for part in range(2):
            dq_heads, dk_heads = [], []
            for hd in range(part * half, (part + 1) * half):
                lanes = slice(hd * HEAD_PAD, (hd + 1) * HEAD_PAD)
                q_pre = q_all[:, lanes]
                _, rr = _rms(q_pre, qh_gain_v, QK_DIM)
                dq_pre, dg = _rms_bwd(q_pre, rr, qh_gain_v, _rope_bwd(dq_ref[hd], rope_v), QK_DIM)
                dqh_gain = dqh_gain + dg
                dq_heads.append(_bf(dq_pre))

                k_pre = k_all[:, lanes] + k_rope
                _, rr = _rms(k_pre, kh_gain_v, QK_DIM)
                dk_pre, dg = _rms_bwd(k_pre, rr, kh_gain_v, _rope_bwd(dk_ref[hd], rope_v), QK_DIM)
                dkh_gain = dkh_gain + dg
                dk_rope = dk_rope + jnp.where(rope_lanes, dk_pre, 0.0)
                dk_heads.append(_bf(dk_pre))
            dq_part = jnp.concatenate(dq_heads, axis=1)
            dk_part = jnp.concatenate(dk_heads, axis=1)
            cols = slice(part * half * HEAD_PAD, (part + 1) * half * HEAD_PAD)
            dcq = dcq + _mm_nt(dq_part, wuq_ref[:, cols])
            dckv = dckv + _mm_nt(dk_part, wuk_ref[:, cols])
            dwuq_ref[:, cols] += _mm_tn(cqb, dq_part)
            dwuk_ref[:, cols] += _mm_tn(ckvb, dk_part)
        dqh_ref[...] += dqh_gain
        dkh_ref[...] += dkh_gain
        dq_lat, dg = _rms_bwd(q_lat, rq, qa_gain_v, dcq)
        dqa_ref[...] += dg
        dkv_lat, dg = _rms_bwd(kv_lat, rkv, kva_gain_v, dckv)
        dkva_ref[...] += dg
        dp_ref[:, 2 * d:2 * d + Q_LORA] = _bf(dq_lat)
        dp_ref[:, 2 * d + Q_LORA:2 * d + Q_LORA + KV_LORA] = _bf(dkv_lat)
        dp_ref[:, 2 * d + Q_LORA + KV_LORA:] = _bf(dk_rope)

    tok = lambda c: pl.BlockSpec((tm, c), lambda i: (i, 0))
    col = lambda c: pl.BlockSpec((tm, d), lambda i: (i, c))
    head = lambda c: pl.BlockSpec((N_HEADS, tm, c), lambda i: (0, i, 0))
    nxt = pl.BlockSpec((8, d), lambda i: (jnp.minimum((i + 1) * (tm // 8), last_blk), 0))
    return _pallas(
        body, "prep_bwd", (t // tm,),
        [tok(LAT_COLS), col(3), col(4), tok(d), nxt, head(HEAD_PAD), head(HEAD_PAD), tok(N_HEADS * V_DIM),
         _const((1, Q_LORA)), _const((1, KV_LORA)), _const((1, HEAD_PAD)), _const((1, HEAD_PAD)),
         _resident(w_uq.shape), _resident(w_uk.shape), _resident(w_uv.shape), tok(HEAD_PAD), tok(HEAD_PAD), tok(HEAD_PAD),
         _const((3, d))],
        [tok(P_COLS), _const(w_uq.shape), _const(w_uk.shape), _const(w_uv.shape), _const((1, Q_LORA)),
         _const((1, KV_LORA)), _const((1, HEAD_PAD)), _const((1, HEAD_PAD)), _const((3, d))],
        [jax.ShapeDtypeStruct((t, P_COLS), BF16), jax.ShapeDtypeStruct(w_uq.shape, F32),
         jax.ShapeDtypeStruct(w_uk.shape, F32), jax.ShapeDtypeStruct(w_uv.shape, F32),
         jax.ShapeDtypeStruct((1, Q_LORA), F32), jax.ShapeDtypeStruct((1, KV_LORA), F32),
         jax.ShapeDtypeStruct((1, HEAD_PAD), F32), jax.ShapeDtypeStruct((1, HEAD_PAD), F32), jax.ShapeDtypeStruct((3, d), F32)],
        (lat, big, big, dz, dz, dq, dk, dv, qa_gain, kva_gain, qh_gain, kh_gain, w_uq, w_uk, w_uv, *rope, conv_w),
        ("arbitrary",), carried)


def _inproj_bwd(x1, gain, dx2, dm, dp, w_in, w_kr, carried=None):
    t, d = x1.shape
    tm = TOKEN_TILE

    def body(x_ref, g_ref, dx2_ref, dm_ref, dp_ref, win_ref, wkr_ref, dx1_ref, dgain_ref):
        xv = x_ref[...]
        gain_v = g_ref[...]
        _, r = _rms(xv, gain_v)
        dh = (_mm(dm_ref[:, :d], win_ref[ROW_GB:ROW_GC, :]) + _mm(dm_ref[:, d:], win_ref[ROW_GL:, :])
              + _mm(dp_ref[:, :d], win_ref[ROW_XC:ROW_GB, :]) + _mm(dp_ref[:, d:2 * d], win_ref[ROW_GC:ROW_GL, :])
              + _mm(dp_ref[:, 2 * d:2 * d + ROW_KR], win_ref[ROW_QKV:ROW_KR, :]) + _mm(dp_ref[:, 2 * d + ROW_KR:], wkr_ref[...]))
        dxn, dgain = _rms_bwd(xv, r, gain_v, dh)
        dx1_ref[...] = dx2_ref[...] + dxn

        @pl.when(pl.program_id(0) == 0)
        def _():
            dgain_ref[...] = jnp.zeros_like(dgain_ref)

        dgain_ref[...] += dgain

    tok = lambda c: pl.BlockSpec((tm, c), lambda i: (i, 0))
    return _pallas(
        body, "inproj_bwd", (t // tm,),
        [tok(d), _const((1, d)), tok(d), tok(M_COLS), tok(P_COLS), _resident(w_in.shape), _resident(w_kr.shape)],
        [tok(d), _const((1, d))], [jax.ShapeDtypeStruct((t, d), F32), jax.ShapeDtypeStruct((1, d), F32)],
        (x1, gain, dx2, dm, dp, w_in, w_kr), ("arbitrary",), carried)


def _adamw(quads, name, carried=None):
    k = len(quads)
    rows, cols = quads[0][0].shape
    tr, tc = rows, cols
    for cand in (512, 352, 256, 192, 128, 64):
        if rows % cand == 0 and rows > cand:
            tr = cand
            break
    if tr == rows and rows * cols > 512 * 1024 and cols % 256 == 0:
        tc = 256
    while k * 14 * tr * tc * 4 > VMEM_LIMIT // 2 and tr % 16 == 0:
        tr //= 2

    def body(*refs):
        for i in range(k):
            w_ref, g_ref, m_ref, v_ref = refs[4 * i:4 * i + 4]
            delta_ref, nm_ref, nv_ref = refs[4 * k + 3 * i:4 * k + 3 * i + 3]
            delta_ref[...], nm_ref[...], nv_ref[...] = _adamw_update(w_ref[...], g_ref[...], m_ref[...], v_ref[...])

    spec = pl.BlockSpec((tr, tc), lambda i, j: (i, j))
    shape = jax.ShapeDtypeStruct((rows, cols), F32)
    outs = _pallas(body, name, (rows // tr, cols // tc), [spec] * (4 * k), [spec] * (3 * k), [shape] * (3 * k),
                   [a for quad in quads for a in quad], ("arbitrary", "arbitrary"), carried)
    return [tuple(outs[3 * i:3 * i + 3]) for i in range(k)]


def _adamw_update(w, g, m, v):
    nm = ADAM_B1 * m + (1.0 - ADAM_B1) * g
    nv = ADAM_B2 * v + (1.0 - ADAM_B2) * (g * g)
    m_hat = nm * (1.0 / (1.0 - ADAM_B1 ** ADAM_STEP))
    v_hat = nv * (1.0 / (1.0 - ADAM_B2 ** ADAM_STEP))
    return -ADAM_LR * (m_hat / (jnp.sqrt(v_hat) + ADAM_EPS) + ADAM_WD * w), nm, nv


def _adamw_whole(quads, name):
    k = len(quads)

    def body(*refs):
        for i in range(k):
            w_ref, g_ref, m_ref, v_ref = refs[4 * i:4 * i + 4]
            g_out, delta_ref, nm_ref, nv_ref = refs[4 * k + 4 * i:4 * k + 4 * i + 4]
            gv = g_ref[...]
            g_out[...] = gv
            delta_ref[...], nm_ref[...], nv_ref[...] = _adamw_update(w_ref[...], gv, m_ref[...], v_ref[...])

    vm = pl.BlockSpec(memory_space=pltpu.VMEM)
    outs = pl.pallas_call(body, name=name, in_specs=[vm] * (4 * k), out_specs=[vm] * (4 * k),
                          out_shape=[jax.ShapeDtypeStruct(q[0].shape, F32) for q in quads for _ in range(4)],
                          compiler_params=_params())(*[a for quad in quads for a in quad])
    return [tuple(outs[4 * i:4 * i + 4]) for i in range(k)]


def _adamw_small(packed_grads, triples, segments):
    k = len(triples)

    def body(*refs):
        g_ref = refs[0]
        off = 0
        for i in range(k):
            w_ref, m_ref, v_ref = refs[1 + 3 * i:4 + 3 * i]
            g_out, delta_ref, nm_ref, nv_ref = refs[1 + 3 * k + 4 * i:5 + 3 * k + 4 * i]
            gv = g_ref[:, off:off + w_ref.shape[1]]
            g_out[...] = gv
            delta_ref[...], nm_ref[...], nv_ref[...] = _adamw_update(w_ref[...], gv, m_ref[...], v_ref[...])
            off += segments[i]

    vm = pl.BlockSpec(memory_space=pltpu.VMEM)
    outs = pl.pallas_call(
        body, name="adamw_small", in_specs=[vm] * (1 + 3 * k), out_specs=[vm] * (4 * k),
        out_shape=[jax.ShapeDtypeStruct(w.shape, F32) for w, _, _ in triples for _ in range(4)],
    )(packed_grads, *[a for triple in triples for a in triple])
    return [tuple(outs[4 * i:4 * i + 4]) for i in range(k)]


def _place():
    x, y, c = lax.axis_index("x"), lax.axis_index("y"), lax.axis_index("c")
    other_chips = [(1 - x, y), (x, 1 - y), (1 - x, 1 - y)]
    return x, y, c, other_chips


def _remote(src, dst, sems, send, recv, device):
    return pltpu.make_async_remote_copy(src_ref=src, dst_ref=dst, send_sem=sems.at[send], recv_sem=sems.at[recv],
                                        device_id=device, device_id_type=MESH_ID)


def _cast_shards(shards, out_dtypes):
    n = len(shards)

    def body(*refs):
        ins, outs, stage, sems = refs[:n], refs[n:2 * n], refs[2 * n:3 * n], refs[3 * n]
        x, y, _, _ = _place()
        me = 2 * x + y
        copies = []
        for w in range(n):
            stage[w][...] = ins[w][...].astype(out_dtypes[w])
            copies.append(pltpu.make_async_copy(stage[w], outs[w].at[me], sems.at[w]))
            copies[-1].start()
        for cp in copies:
            cp.wait()

    vm = pl.BlockSpec(memory_space=pltpu.VMEM)
    return pl.pallas_call(
        body, name="cast_shards", in_specs=[vm] * n, out_specs=[ANY] * n,
        out_shape=[jax.ShapeDtypeStruct((N_CHIPS,) + s.shape, dt) for s, dt in zip(shards, out_dtypes)],
        scratch_shapes=[pltpu.VMEM(s.shape, dt) for s, dt in zip(shards, out_dtypes)] + [pltpu.SemaphoreType.DMA((n,))],
        compiler_params=_params())(*shards)


BF16_ROWS = 16


def _split_rows(rows):
    return (rows // 2) % BF16_ROWS == 0


def _half_shape(rows, cols):
    return (rows // 2, cols) if _split_rows(rows) else (rows, cols // 2)


def _half(rows, cols, which):
    if _split_rows(rows):
        return (pl.ds(pl.multiple_of(which * (rows // 2), BF16_ROWS), rows // 2), slice(None))
    return (slice(None), pl.ds(pl.multiple_of(which * (cols // 2), 128), cols // 2))


def _gather_carried(bufs):
    n = len(bufs)

    def half(w, slot, which):
        _, rows, cols = bufs[w].shape
        return (slot,) + _half(rows, cols, which)

    def start(ins, outs, sems, base):
        x, y, c, other_chips = _place()
        me = 2 * x + y
        for w in range(n):
            mine = outs[w].at[half(w, me, c)]
            for p, (px, py) in enumerate(other_chips):
                _remote(mine, mine, sems, base + 12 * w + p, base + 12 * w + 3 + p, (px, py, c)).start()

    def finish(ins, outs, sems, base):
        x, y, c, other_chips = _place()
        me = 2 * x + y
        for w in range(n):
            for p, (px, py) in enumerate(other_chips):
                got = outs[w].at[half(w, 2 * px + py, c)]
                _remote(got, got, sems, base + 12 * w + p, base + 12 * w + 3 + p, (px, py, c)).wait_recv()
                _remote(got, got, sems, base + 12 * w + 6 + p, base + 12 * w + 9 + p, (x, y, 1 - c)).start()
        for w in range(n):
            mine = outs[w].at[half(w, me, c)]
            for p, (px, py) in enumerate(other_chips):
                got = outs[w].at[half(w, 2 * px + py, c)]
                theirs = outs[w].at[half(w, 2 * px + py, 1 - c)]
                _remote(got, theirs, sems, base + 12 * w + 6 + p, base + 12 * w + 9 + p, (x, y, 1 - c)).wait()
                _remote(mine, mine, sems, base + 12 * w + p, base + 12 * w + 3 + p, (px, py, c)).wait_send()

    shapes = [jax.ShapeDtypeStruct(b.shape, b.dtype) for b in bufs]
    return _Carried(bufs, shapes, {w: w for w in range(n)}, 12 * n, start, finish)


def _swap_carried(grads):
    n = len(grads)

    def copy(w, ins, outs, sems, base):
        x, y, c, _ = _place()
        _, rows, cols = grads[w].shape
        theirs = ins[w].at[(slice(None),) + _half(rows, cols, 1 - c)]
        return _remote(theirs, outs[w], sems, base + 2 * w, base + 2 * w + 1, (x, y, 1 - c))

    def start(ins, outs, sems, base):
        for w in range(n):
            copy(w, ins, outs, sems, base).start()

    def finish(ins, outs, sems, base):
        for w in range(n):
            copy(w, ins, outs, sems, base).wait()

    shapes = [jax.ShapeDtypeStruct((g.shape[0],) + _half_shape(*g.shape[1:]), F32) for g in grads]
    return _Carried(grads, shapes, {}, 2 * n, start, finish)


def _row_tile(rows):
    for cand in (512, 352, 256, 192, 128, 96, 64, 32, 16):
        if rows % cand == 0:
            return cand
    return rows


def _half_block_index(split_rows, tiles, i, core):
    return (core * tiles + i, 0) if split_rows else (i, core)


def _chip_partial(grad, other, place, name):
    nblk, hr, hc = other.shape
    by_rows = _split_rows(grad.shape[1])
    tr = _row_tile(hr)
    tiles = hr // tr

    def body(place_ref, g_ref, o_ref, sum_ref, sum_bf_ref):
        s = g_ref[...] + o_ref[...]
        sum_ref[...] = s
        sum_bf_ref[...] = _bf(s)

    grid_spec = pltpu.PrefetchScalarGridSpec(
        num_scalar_prefetch=1, grid=(nblk, tiles),
        in_specs=[pl.BlockSpec((None, tr, hc), lambda b, i, place_ref: (b,) + _half_block_index(by_rows, tiles, i, place_ref[1])),
                  pl.BlockSpec((None, tr, hc), lambda b, i, place_ref: (b, i, 0))],
        out_specs=[pl.BlockSpec((None, tr, hc), lambda b, i, place_ref: (b, i, 0))] * 2)
    return pl.pallas_call(body, name=name, grid_spec=grid_spec,
                          out_shape=[jax.ShapeDtypeStruct(other.shape, F32), jax.ShapeDtypeStruct(other.shape, BF16)],
                          compiler_params=_params(("arbitrary", "arbitrary")))(place, grad, other)


def _chip_partial_small(grads, others, place):
    n = len(grads)

    def body(*refs):
        place_ref, g_refs, o_refs = refs[0], refs[1:1 + n], refs[1 + n:1 + 2 * n]
        sum_refs, bf_refs = refs[1 + 2 * n:1 + 3 * n], refs[1 + 3 * n:]
        core = place_ref[1]
        for w in range(n):
            _, rows, cols = grads[w].shape
            s = g_refs[w][(slice(None),) + _half(rows, cols, core)] + o_refs[w][...]
            sum_refs[w][...] = s
            bf_refs[w][...] = _bf(s)

    vm = pl.BlockSpec(memory_space=pltpu.VMEM)
    outs = pl.pallas_call(
        body, name="chip_partial_small", in_specs=[pl.BlockSpec(memory_space=pltpu.SMEM)] + [vm] * (2 * n), out_specs=[vm] * (2 * n),
        out_shape=[jax.ShapeDtypeStruct(o.shape, F32) for o in others] + [jax.ShapeDtypeStruct(o.shape, BF16) for o in others],
        compiler_params=_params())(place, *grads, *others)
    return list(zip(outs[:n], outs[n:]))


def _chip_total_small(owns, receiveds, place, shapes):
    n = len(owns)

    def body(*refs):
        place_ref, own_refs, r_refs, out_refs = refs[0], refs[1:1 + n], refs[1 + n:1 + 2 * n], refs[1 + 2 * n:]
        chip, core = place_ref[0], place_ref[1]
        for w in range(n):
            r = [r_refs[w][(chip + k) % N_CHIPS].astype(F32) for k in (1, 2, 3)]
            out_refs[w][_half(*shapes[w], core)] = own_refs[w][chip] + ((r[0] + r[1]) + r[2])

    vm = pl.BlockSpec(memory_space=pltpu.VMEM)
    return list(pl.pallas_call(
        body, name="chip_total_small", in_specs=[pl.BlockSpec(memory_space=pltpu.SMEM)] + [vm] * (2 * n), out_specs=[vm] * n,
        out_shape=[jax.ShapeDtypeStruct(tuple(s), F32) for s in shapes], compiler_params=_params())(place, *owns, *receiveds))


def _send_carried(partials):
    n = len(partials)

    def start(ins, outs, sems, base):
        x, y, c, other_chips = _place()
        me = 2 * x + y
        for w in range(n):
            for p, (px, py) in enumerate(other_chips):
                _remote(ins[w].at[2 * px + py], outs[w].at[me], sems, base + 6 * w + p, base + 6 * w + 3 + p, (px, py, c)).start()

    def finish(ins, outs, sems, base):
        x, y, c, other_chips = _place()
        for w in range(n):
            for p, (px, py) in enumerate(other_chips):
                _remote(ins[w].at[2 * px + py], outs[w].at[2 * px + py], sems, base + 6 * w + p, base + 6 * w + 3 + p,
                        (px, py, c)).wait()

    return _Carried(partials, [jax.ShapeDtypeStruct(p.shape, BF16) for p in partials], {}, 6 * n, start, finish)


def _chip_total(own, received, place, shape, name):
    nblk, hr, hc = own.shape
    by_rows = _split_rows(shape[0])
    tr = _row_tile(hr)
    tiles = hr // tr

    def body(place_ref, own_ref, r1_ref, r2_ref, r3_ref, out_ref):
        out_ref[...] = own_ref[...] + ((r1_ref[...].astype(F32) + r2_ref[...].astype(F32)) + r3_ref[...].astype(F32))

    def slot(k):
        return pl.BlockSpec((None, tr, hc), lambda i, place_ref: ((place_ref[0] + k) % N_CHIPS, i, 0))

    grid_spec = pltpu.PrefetchScalarGridSpec(
        num_scalar_prefetch=1, grid=(tiles,), in_specs=[slot(0), slot(1), slot(2), slot(3)],
        out_specs=pl.BlockSpec((tr, hc), lambda i, place_ref: _half_block_index(by_rows, tiles, i, place_ref[1])))
    return pl.pallas_call(body, name=name, grid_spec=grid_spec, out_shape=jax.ShapeDtypeStruct(tuple(shape), F32),
                          compiler_params=_params(("arbitrary",)))(place, own, received, received, received)


def _join_carried(totals):
    n = len(totals)

    def copy(w, outs, sems, base):
        x, y, c, _ = _place()
        mine = outs[w].at[_half(*totals[w].shape, c)]
        return _remote(mine, mine, sems, base + 2 * w, base + 2 * w + 1, (x, y, 1 - c))

    def start(ins, outs, sems, base):
        for w in range(n):
            copy(w, outs, sems, base).start()

    def finish(ins, outs, sems, base):
        for w in range(n):
            copy(w, outs, sems, base).wait()

    shapes = [jax.ShapeDtypeStruct(a.shape, F32) for a in totals]
    return _Carried(totals, shapes, {w: w for w in range(n)}, 2 * n, start, finish)


def _sum_devices(vec):
    rows, n = vec.shape

    def body(v_ref, out_ref, buf, send_sems, recv_sems):
        x, y, c, _ = _place()
        me = 4 * x + 2 * y + c
        buf[me] = v_ref[...]
        sends = []
        for k in range(1, N_DEV):
            peer = (1 - x if k & 4 else x, 1 - y if k & 2 else y, 1 - c if k & 1 else c)
            cp = pltpu.make_async_remote_copy(src_ref=v_ref, dst_ref=buf.at[me], send_sem=send_sems.at[k], recv_sem=recv_sems.at[k],
                                              device_id=peer, device_id_type=MESH_ID)
            cp.start()
            sends.append(cp)
        for cp in sends:
            cp.wait()
        total = buf[0]
        for dev in range(1, N_DEV):
            total = total + buf[dev]
        out_ref[...] = total

    vm = pl.BlockSpec(memory_space=pltpu.VMEM)
    return pl.pallas_call(
        body, name="sum_devices", in_specs=[vm], out_specs=vm, out_shape=jax.ShapeDtypeStruct((rows, n), F32),
        scratch_shapes=[pltpu.VMEM((N_DEV, rows, n), F32), pltpu.SemaphoreType.DMA((N_DEV,)), pltpu.SemaphoreType.DMA((N_DEV,))],
    )(vec)


def _rope_tables(positions):
    half = ROPE_HALF
    inv_freq = 1.0 / (ROPE_THETA ** (jnp.arange(half, dtype=F32) / half))
    ang = positions.astype(F32).reshape(-1, 1) * inv_freq
    cos, sin = jnp.cos(ang), jnp.sin(ang)
    t = ang.shape[0]
    ones, zeros = jnp.ones((t, QK_NOPE), F32), jnp.zeros((t, QK_NOPE), F32)
    pad, none = HEAD_PAD - QK_DIM, zeros[:, :half]
    cos_full = jnp.concatenate([ones, cos, cos, ones[:, :pad]], axis=1)
    s_lo = jnp.concatenate([zeros, -sin, none, zeros[:, :pad]], axis=1)
    s_hi = jnp.concatenate([zeros, none, sin, zeros[:, :pad]], axis=1)
    return cos_full, s_lo, s_hi


def _partials(names, grads, from_sibling, place):
    return [_chip_partial(g, o, place, "chip_partial_" + n) for n, g, o in zip(names, grads, from_sibling)]


def _totals(names, grads, partials, received, place):
    return [_chip_total(pf, r, place, g.shape[1:], "chip_total_" + n) for n, g, (pf, _), r in zip(names, grads, partials, received)]


def _kernel_layouts(full):
    w_in = full["w_in"]
    w_kr = jnp.pad(w_in[ROW_KR:ROW_XC], ((QK_NOPE, HEAD_PAD - QK_DIM), (0, 0)))
    w_uq = jnp.pad(full["w_uq"].reshape(Q_LORA, N_HEADS, QK_DIM), ((0, 0), (0, 0), (0, HEAD_PAD - QK_DIM)))
    w_uk = jnp.pad(full["w_uk"].reshape(KV_LORA, N_HEADS, QK_NOPE), ((0, 0), (0, 0), (0, HEAD_PAD - QK_NOPE)))
    return {"w_in": w_in, "w_kr": w_kr, "w_uq": w_uq.reshape(Q_LORA, N_HEADS * HEAD_PAD),
            "w_uk": w_uk.reshape(KV_LORA, N_HEADS * HEAD_PAD), "w_uv": full["w_uv"], "w_uvt": full["w_uv"].T}


def _global_layouts(g):
    d = D_MODEL
    dm, dp = g["w_m"], g["w_p"]
    o_lat = 2 * d
    o_kr = o_lat + Q_LORA + KV_LORA + QK_NOPE
    w_in = jnp.concatenate([dp[o_lat:o_lat + Q_LORA + KV_LORA], dp[o_kr:o_kr + QK_ROPE], dp[:d], dm[:d], dp[d:o_lat], dm[d:]], axis=0)
    w_uq = g["w_uq"].reshape(Q_LORA, N_HEADS, HEAD_PAD)[:, :, :QK_DIM].reshape(Q_LORA, N_HEADS * QK_DIM)
    w_uk = g["w_uk"].reshape(KV_LORA, N_HEADS, HEAD_PAD)[:, :, :QK_NOPE].reshape(KV_LORA, N_HEADS * QK_NOPE)
    return {"w_in": w_in, "w_uq": w_uq, "w_uk": w_uk, "w_uv": g["w_uv"], "w_proj_attn": g["w_pa"], "w_proj_conv": g["w_pc"],
            "w_out": g["w_out"]}


def _col_blocks(a):
    r, c = a.shape
    return a.reshape(r, N_CHIPS, c // N_CHIPS).transpose(1, 0, 2)


def _from_col_blocks(a):
    n, r, c = a.shape
    return a.transpose(1, 0, 2).reshape(r, n * c)


COL_SHARDED = ("w_uq", "w_uk", "w_uv", "w_proj_attn")
TRANSPOSED = ("ffn1_w_gate", "ffn1_w_up", "ffn2_w_gate", "ffn2_w_up", "w_in")
SMALL = (("ffn1_norm", 1024), ("mix_norm", 1024), ("gate_bias", 2048), ("q_a_norm", 384), ("kv_a_norm", 256),
         ("q_head_norm", 128), ("k_head_norm", 128), ("ffn2_norm", 1024))
WEIGHT_ORDER = ("ffn1_norm", "ffn1_w_gate", "ffn1_w_up", "ffn1_w_down", "mix_norm", "w_in", "gate_bias", "q_a_norm", "w_uq",
                "kv_a_norm", "w_uk", "w_uv", "q_head_norm", "k_head_norm", "w_proj_attn", "conv_w", "w_proj_conv", "w_out",
                "ffn2_norm", "ffn2_w_gate", "ffn2_w_up", "ffn2_w_down")
MATRICES = ("ffn1_w_gate", "ffn1_w_up", "ffn1_w_down", "w_in", "w_uq", "w_uk", "w_uv", "w_proj_attn", "w_proj_conv", "w_out",
            "ffn2_w_gate", "ffn2_w_up", "ffn2_w_down")
GROUP_FFN1 = ("ffn1_w_gate", "ffn1_w_up", "ffn1_w_down")
GROUP_IN = ("w_in", "w_uq", "w_uk", "w_uv", "conv_w")
GROUP_MIX = ("w_proj_attn", "w_proj_conv", "w_out")
GROUP_FFN2 = ("ffn2_w_gate", "ffn2_w_up", "ffn2_w_down")
GROUP_MID = ("w_in", "w_uq", "w_uk", "w_uv", "w_proj_attn", "w_proj_conv", "w_out")


def _pad_lanes(a, n):
    return jnp.pad(a.reshape(1, -1), ((0, 0), (0, n - a.size)))


def kernel(x, positions, ffn1_norm, ffn1_w_gate, ffn1_w_up, ffn1_w_down, mix_norm, w_in, gate_bias, q_a_norm, w_uq, kv_a_norm, w_uk, w_uv, q_head_norm, k_head_norm, w_proj_attn, conv_w, w_proj_conv, w_out, ffn2_norm, ffn2_w_gate, ffn2_w_up, ffn2_w_down, loss_target, m_ffn1_norm, m_ffn1_w_gate, m_ffn1_w_up, m_ffn1_w_down, m_mix_norm, m_w_in, m_gate_bias, m_q_a_norm, m_w_uq, m_kv_a_norm, m_w_uk, m_w_uv, m_q_head_norm, m_k_head_norm, m_w_proj_attn, m_conv_w, m_w_proj_conv, m_w_out, m_ffn2_norm, m_ffn2_w_gate, m_ffn2_w_up, m_ffn2_w_down, v_ffn1_norm, v_ffn1_w_gate, v_ffn1_w_up, v_ffn1_w_down, v_mix_norm, v_w_in, v_gate_bias, v_q_a_norm, v_w_uq, v_kv_a_norm, v_w_uk, v_w_uv, v_q_head_norm, v_k_head_norm, v_w_proj_attn, v_conv_w, v_w_proj_conv, v_w_out, v_ffn2_norm, v_ffn2_w_gate, v_ffn2_w_up, v_ffn2_w_down):
    args = dict(locals())
    view = lambda n, a: a.T if n in TRANSPOSED else a
    weights = {n: view(n, args[n]) for n in WEIGHT_ORDER}
    moments_m = {n: view(n, args["m_" + n]) for n in WEIGHT_ORDER}
    moments_v = {n: view(n, args["v_" + n]) for n in WEIGHT_ORDER}
    nb, seq, d = x.shape
    t = nb * seq
    chip = (2 * lax.axis_index("x") + lax.axis_index("y")).astype(jnp.int32)
    place = jnp.stack([chip, lax.axis_index("c").astype(jnp.int32)])
    grads, delta, new_m, new_v = {}, {}, {}, {}

    def adamw(names, carried=None):
        results = _adamw([(weights[n], grads[n], moments_m[n], moments_v[n]) for n in names], "adamw_" + names[0], carried)
        for n, (dn, mn, vn) in zip(names, results):
            delta[n], new_m[n], new_v[n] = dn, mn, vn

    conv_rows = conv_w.shape[0]
    conv_shard = jnp.pad(conv_w, ((0, 16 - conv_rows), (0, 0)))
    bufs = dict(zip(MATRICES + ("conv_w",), _cast_shards([weights[n] for n in MATRICES] + [conv_shard],
                                                         [BF16] * len(MATRICES) + [F32])))
    blocks = dict(zip(GROUP_FFN1, _run(_gather_carried([bufs[n] for n in GROUP_FFN1]), "gather_ffn1")))
    p = {n: _pad_lanes(weights[n], size) for n, size in SMALL}
    rope = _rope_tables(positions)
    x_tok = x.reshape(t, d)

    gather_in = _gather_carried([bufs[n] for n in GROUP_IN])
    x1, gate1, up1, act1 = _ffn_fwd(x_tok, p["ffn1_norm"], blocks["ffn1_w_gate"], blocks["ffn1_w_up"], blocks["ffn1_w_down"], None,
                                    "ffn1_fwd", gather_in)
    blocks.update(zip(GROUP_IN, gather_in.results))
    w = _kernel_layouts({"w_in": blocks["w_in"].reshape(-1, d), **{n: _from_col_blocks(blocks[n]) for n in ("w_uq", "w_uk", "w_uv")}})
    p["conv_w"] = _from_col_blocks(blocks["conv_w"])[:conv_rows]

    gather_mix = _gather_carried([bufs[n] for n in GROUP_MIX])
    h2b, big, lat, q, k, v, vt = _inproj_fwd(x1, p["mix_norm"], w["w_in"], w["w_kr"], p["q_a_norm"], p["kv_a_norm"], p["q_head_norm"],
                                             p["k_head_norm"], w["w_uq"], w["w_uk"], w["w_uv"], w["w_uvt"], rope, gather_mix)
    blocks.update(zip(GROUP_MIX, gather_mix.results))
    w_pa = _from_col_blocks(blocks["w_proj_attn"])
    w_pc, w_out_full = blocks["w_proj_conv"].reshape(-1, d), blocks["w_out"].reshape(-1, d)

    gather_ffn2 = _gather_carried([bufs[n] for n in GROUP_FFN2[:2]])
    o, lse = _attn_fwd(q, k, vt, seq, gather_ffn2)
    gather_down = _gather_carried([bufs[n] for n in GROUP_FFN2[2:]])
    x2 = _mix_fwd(x1, o, big, p["gate_bias"], p["conv_w"], w_pa, w_pc, w_out_full, seq, gather_down)
    wg2, wu2 = gather_ffn2.results
    wd2, = gather_down.results
    dx3, gate2, up2, act2, loss = _ffn_fwd(x2, p["ffn2_norm"], wg2, wu2, wd2, loss_target.reshape(t, d), "ffn2_fwd")

    dx2, dg_ffn2, hb2, dgate2, dup2, dyb2 = _ffn_bwd_x(x2, p["ffn2_norm"], dx3, gate2, up2, wg2, wu2, wd2, "ffn2_bwd")
    g_ffn2 = [_tn_matmul(dgate2, hb2, "ffn2_dw_gate"), _tn_matmul(dup2, hb2, "ffn2_dw_up"), _tn_matmul(act2, dyb2, "ffn2_dw_down")]
    swap = _swap_carried(g_ffn2)
    do, delta_o, dz, dm, dbias, dw_pa, dw_pc, dw_out = _mix_bwd(dx2, o, big, p["gate_bias"], p["conv_w"], w_pa, w_pc, w_out_full, seq,
                                                                swap)
    part = _partials(GROUP_FFN2, g_ffn2, swap.results, place)
    send = _send_carried([pb for _, pb in part])
    dq, dk, dv = _attn_bwd(q, k, v, do, lse, delta_o.reshape(N_HEADS // ATTN_BWD_HEADS, ATTN_BWD_HEADS, -1), seq, send)
    join = _join_carried(_totals(GROUP_FFN2, g_ffn2, part, send.results, place))
    dp, dw_uq, dw_uk, dw_uv, dqa, dkva, dqh, dkh, dcw = _prep_bwd(
        lat, big, dz, dq, dk, dv, p["q_a_norm"], p["kv_a_norm"], p["q_head_norm"], p["k_head_norm"], w["w_uq"], w["w_uk"],
        w["w_uv"], rope, p["conv_w"], seq, join)
    grads.update(zip(GROUP_FFN2, join.results))

    gg = _global_layouts({"w_m": _tn_matmul(dm, h2b, "dw_in_m", split_k=2), "w_p": _tn_matmul(dp, h2b, "dw_in_p", split_k=2),
                          "w_uq": dw_uq, "w_uk": dw_uk, "w_uv": dw_uv, "w_pa": dw_pa, "w_pc": dw_pc, "w_out": dw_out})
    g_mid = [_col_blocks(gg[n]) if n in COL_SHARDED else gg[n].reshape(N_CHIPS, -1, gg[n].shape[-1]) for n in GROUP_MID]
    swap = _swap_carried(g_mid)
    dx1, dg_mix = _inproj_bwd(x1, p["mix_norm"], dx2, dm, dp, w["w_in"], w["w_kr"], swap)
    part = (_partials(GROUP_MID[:1], g_mid[:1], swap.results[:1], place)
            + _chip_partial_small(g_mid[1:], swap.results[1:], place))
    send = _send_carried([pb for _, pb in part])
    grad_x, dg_ffn1, hb1, dgate1, dup1, dyb1 = _ffn_bwd_x(x_tok, p["ffn1_norm"], dx1, gate1, up1, blocks["ffn1_w_gate"],
                                                         blocks["ffn1_w_up"], blocks["ffn1_w_down"], "ffn1_bwd", send)

    small_grads = {"ffn1_norm": dg_ffn1, "mix_norm": dg_mix, "gate_bias": dbias, "q_a_norm": dqa, "kv_a_norm": dkva,
                   "q_head_norm": dqh, "k_head_norm": dkh, "ffn2_norm": dg_ffn2}
    packed = jnp.concatenate([small_grads[n] for n, _ in SMALL] + [dcw.reshape(1, -1), loss], axis=1)
    total = _sum_devices(packed.reshape(8, -1)).reshape(1, -1)
    n_small = sum(size for _, size in SMALL)
    conv_cols = conv_w.shape[1]
    conv_total = total[:, n_small:n_small + conv_rows * d].reshape(conv_rows, d)
    grads["conv_w"] = lax.dynamic_slice_in_dim(conv_total, chip * conv_cols, conv_cols, axis=1)
    loss_total = total[0, n_small + conv_rows * d]

    join = _join_carried(_totals(GROUP_MID[:1], g_mid[:1], part[:1], send.results[:1], place)
                         + _chip_total_small([pf for pf, _ in part[1:]], send.results[1:], place, [g.shape[1:] for g in g_mid[1:]]))
    g_gate = _tn_matmul(dgate1, hb1, "ffn1_dw_gate", carried=join)
    grads.update(zip(GROUP_MID, join.results))
    swap_gate = _swap_carried([g_gate])
    g_up = _tn_matmul(dup1, hb1, "ffn1_dw_up", carried=swap_gate)
    part_gate = _partials(GROUP_FFN1[:1], [g_gate], swap_gate.results, place)
    send_gate, swap_up = _send_carried([part_gate[0][1]]), _swap_carried([g_up])
    g_down = _tn_matmul(act1, dyb1, "ffn1_dw_down", carried=_both(send_gate, swap_up))
    join_gate = _join_carried(_totals(GROUP_FFN1[:1], [g_gate], part_gate, send_gate.results, place))
    part_up = _partials(GROUP_FFN1[1:2], [g_up], swap_up.results, place)
    send_up, swap_down = _send_carried([part_up[0][1]]), _swap_carried([g_down])
    adamw(GROUP_FFN2, _both(_both(send_up, swap_down), join_gate))
    grads["ffn1_w_gate"] = join_gate.results[0]
    join_up = _join_carried(_totals(GROUP_FFN1[1:2], [g_up], part_up, send_up.results, place))
    part_down = _partials(GROUP_FFN1[2:], [g_down], swap_down.results, place)
    send_down = _send_carried([part_down[0][1]])
    adamw(("w_in",), _both(send_down, join_up))
    grads["ffn1_w_up"] = join_up.results[0]
    join_down = _join_carried(_totals(GROUP_FFN1[2:], [g_down], part_down, send_down.results, place))
    adamw(GROUP_FFN1[:2], join_down)
    grads["ffn1_w_down"] = join_down.results[0]
    adamw(GROUP_FFN1[2:])
    others = GROUP_MID[1:] + ("conv_w",)
    for n, (gn, dn, mn, vn) in zip(others, _adamw_whole([(weights[n], grads[n], moments_m[n], moments_v[n]) for n in others],
                                                        "adamw_others")):
        grads[n], delta[n], new_m[n], new_v[n] = gn, dn, mn, vn

    row = lambda a: a.reshape(1, -1)
    small = _adamw_small(total, [(row(weights[n]), row(moments_m[n]), row(moments_v[n])) for n, _ in SMALL], [size for _, size in SMALL])
    for (n, _), (gn, dn, mn, vn) in zip(SMALL, small):
        grads[n], delta[n], new_m[n], new_v[n] = gn.reshape(-1), dn.reshape(-1), mn.reshape(-1), vn.reshape(-1)

    return (loss_total, grad_x.reshape(nb, seq, d), *[view(n, src[n]) for src in (grads, delta, new_m, new_v) for n in WEIGHT_ORDER])
```

```python
import functools

import jax
import jax.numpy as jnp
from jax import lax
from jax.experimental import pallas as pl
from jax.experimental.pallas import tpu as pltpu

F32 = jnp.float32
BF16 = jnp.bfloat16

D_MODEL = 1024
N_HEADS = 8
QK_NOPE = 64
QK_ROPE = 32
QK_DIM = QK_NOPE + QK_ROPE
V_DIM = 64
HEAD_PAD = 128
Q_LORA = 384
KV_LORA = 256
ROPE_THETA = 10000.0
NORM_EPS = 1e-6
ATTN_SCALE = QK_DIM ** -0.5
MASK_VALUE = -1e30
N_CHIPS = 4
N_DEV = 8

ADAM_LR = 0.001
ADAM_B1 = 0.9
ADAM_B2 = 0.999
ADAM_EPS = 1e-08
ADAM_WD = 0.01
ADAM_STEP = 10

TOKEN_TILE = 256
PREP_TILE = 256
ATTN_TILE = 512
TN_TILE = 2048
VMEM_LIMIT = 56 * 1024 * 1024

M_COLS = 3 * D_MODEL
P_COLS = 2 * D_MODEL + Q_LORA + KV_LORA + HEAD_PAD
BIG_COLS = 5 * D_MODEL
LAT_COLS = Q_LORA + KV_LORA + HEAD_PAD

MESH_ID = pl.DeviceIdType.MESH
ANY = pl.BlockSpec(memory_space=pl.ANY)


def _params(semantics=None):
    return pltpu.CompilerParams(dimension_semantics=semantics, vmem_limit_bytes=VMEM_LIMIT)


class _Carried:
    def __init__(self, operands, out_shapes, aliases, n_sems, start, finish):
        self.operands, self.out_shapes, self.aliases, self.n_sems = list(operands), list(out_shapes), dict(aliases), n_sems
        self.start, self.finish = start, finish
        self.results = None


def _both(a, b):
    na, nao = len(a.operands), len(a.out_shapes)

    def start(ins, outs, sems, base):
        a.start(ins[:na], outs[:nao], sems, base)
        b.start(ins[na:], outs[nao:], sems, base + a.n_sems)

    def finish(ins, outs, sems, base):
        a.finish(ins[:na], outs[:nao], sems, base)
        b.finish(ins[na:], outs[nao:], sems, base + a.n_sems)

    aliases = dict(a.aliases)
    aliases.update({na + i: nao + o for i, o in b.aliases.items()})
    both = _Carried(a.operands + b.operands, a.out_shapes + b.out_shapes, aliases, a.n_sems + b.n_sems, start, finish)
    both.parts = (a, b)
    return both


def _set_results(carried, results):
    carried.results = list(results)
    if hasattr(carried, "parts"):
        a, b = carried.parts
        _set_results(a, results[:len(a.out_shapes)])
        _set_results(b, results[len(a.out_shapes):])


def _pallas(body, name, grid, in_specs, out_specs, out_shape, args, semantics, carried=None):
    if carried is None:
        return pl.pallas_call(body, name=name, grid=grid, in_specs=in_specs, out_specs=out_specs, out_shape=out_shape,
                              compiler_params=_params(semantics))(*args)
    n_in, n_out, n_ci, n_co = len(in_specs), len(out_specs), len(carried.operands), len(carried.out_shapes)

    def wrapped(*refs):
        ins, c_ins = refs[:n_in], refs[n_in:n_in + n_ci]
        outs, c_outs = refs[n_in + n_ci:n_in + n_ci + n_out], refs[n_in + n_ci + n_out:n_in + n_ci + n_out + n_co]
        sems = refs[-1]
        first = pl.program_id(0) == 0
        last = pl.program_id(0) == grid[0] - 1
        for axis in range(1, len(grid)):
            first = jnp.logical_and(first, pl.program_id(axis) == 0)
            last = jnp.logical_and(last, pl.program_id(axis) == grid[axis] - 1)

        @pl.when(first)
        def _():
            carried.start(c_ins, c_outs, sems, 0)

        body(*ins, *outs)

        @pl.when(last)
        def _():
            carried.finish(c_ins, c_outs, sems, 0)

    results = pl.pallas_call(
        wrapped, name=name, grid=grid, in_specs=list(in_specs) + [ANY] * n_ci, out_specs=list(out_specs) + [ANY] * n_co,
        out_shape=list(out_shape) + carried.out_shapes,
        input_output_aliases={n_in + i: n_out + o for i, o in carried.aliases.items()},
        scratch_shapes=[pltpu.SemaphoreType.DMA((carried.n_sems,))], compiler_params=_params(semantics))(*args, *carried.operands)
    _set_results(carried, results[n_out:])
    return results[:n_out]


def _run(carried, name):
    n_ci, n_co = len(carried.operands), len(carried.out_shapes)

    def body(*refs):
        carried.start(refs[:n_ci], refs[n_ci:n_ci + n_co], refs[-1], 0)
        carried.finish(refs[:n_ci], refs[n_ci:n_ci + n_co], refs[-1], 0)

    results = pl.pallas_call(body, name=name, in_specs=[ANY] * n_ci, out_specs=[ANY] * n_co, out_shape=carried.out_shapes,
                             input_output_aliases=carried.aliases,
                             scratch_shapes=[pltpu.SemaphoreType.DMA((carried.n_sems,))])(*carried.operands)
    _set_results(carried, results)
    return carried.results


def _resident(shape):
    nd = len(shape)
    return pl.BlockSpec(shape, lambda *_: (0,) * nd, pipeline_mode=pl.Buffered(1))


def _const(shape):
    nd = len(shape)
    return pl.BlockSpec(shape, lambda *_: (0,) * nd)


def _mm(a, b):
    return jnp.dot(a, b, preferred_element_type=F32)


def _mm_nt(a, b):
    return lax.dot_general(a, b, (((1,), (1,)), ((), ())), preferred_element_type=F32)


def _mm_tn(a, b):
    return lax.dot_general(a, b, (((0,), (0,)), ((), ())), preferred_element_type=F32)


def _bf(a):
    return a.astype(BF16)


def _sigmoid(a):
    return 1.0 / (1.0 + jnp.exp(-a))


def _rms(x, gain, n=None):
    n = x.shape[-1] if n is None else n
    r = lax.rsqrt(jnp.sum(x * x, axis=-1, keepdims=True) * (1.0 / n) + NORM_EPS)
    return (x * r) * gain, r


def _rms_bwd(x, r, gain, dh, n=None):
    n = x.shape[-1] if n is None else n
    u = dh * gain
    dx = r * u - x * ((r * r * r) * (jnp.sum(u * x, axis=-1, keepdims=True) * (1.0 / n)))
    dgain = jnp.sum(dh * (x * r), axis=0, keepdims=True)
    return dx, dgain


ROPE_HALF = QK_ROPE // 2


def _rope(t, rope):
    cos, s_lo, s_hi = rope
    return t * cos + pltpu.roll(t, HEAD_PAD - ROPE_HALF, 1) * s_lo + pltpu.roll(t, ROPE_HALF, 1) * s_hi


def _rope_bwd(dt, rope):
    cos, s_lo, s_hi = rope
    return dt * cos + pltpu.roll(dt * s_lo, ROPE_HALF, 1) + pltpu.roll(dt * s_hi, HEAD_PAD - ROPE_HALF, 1)


def _shift_down(u, prev8, k):
    s = pltpu.roll(u, k, 0)
    p = pltpu.roll(prev8, k, 0)
    row = lax.broadcasted_iota(jnp.int32, prev8.shape, 0)
    top = jnp.where(row < k, p, s[:8])
    return jnp.concatenate([top, s[8:]], axis=0)


def _shift_up(d, next8, k):
    tm = d.shape[0]
    s = pltpu.roll(d, tm - k, 0)
    n = pltpu.roll(next8, 8 - k, 0)
    row = lax.broadcasted_iota(jnp.int32, next8.shape, 0)
    bot = jnp.where(row >= 8 - k, n, s[tm - 8:])
    return jnp.concatenate([s[:tm - 8], bot], axis=0)


def _ffn_fwd(x, gain, wg, wu, wd, target, name, carried=None):
    t, d = x.shape
    nb, f, _ = wg.shape
    tm = TOKEN_TILE
    with_loss = target is not None

    def body(*refs):
        if with_loss:
            x_ref, g_ref, wg_ref, wu_ref, wd_ref, t_ref, out_ref, gate_ref, up_ref, act_ref, loss_ref = refs
        else:
            x_ref, g_ref, wg_ref, wu_ref, wd_ref, out_ref, gate_ref, up_ref, act_ref = refs
        xv = x_ref[...]
        h, _ = _rms(xv, g_ref[...])
        hb = _bf(h)
        y = jnp.zeros((tm, d), F32)
        nxt = (_mm_nt(hb, wg_ref[0]), _mm_nt(hb, wu_ref[0]))
        for j in range(nb):
            gate, up = nxt
            if j + 1 < nb:
                nxt = (_mm_nt(hb, wg_ref[j + 1]), _mm_nt(hb, wu_ref[j + 1]))
            act = _bf((gate * _sigmoid(gate)) * up)
            y = y + _mm(act, wd_ref[j])
            gate_ref[j] = _bf(gate)
            up_ref[j] = _bf(up)
            act_ref[j] = act
        out = xv + 0.5 * y
        if with_loss:
            err = out - t_ref[...]
            out_ref[...] = err * (1.0 / d)

            @pl.when(pl.program_id(0) == 0)
            def _():
                loss_ref[...] = jnp.zeros_like(loss_ref)

            part = jnp.sum(jnp.sum(err * err, axis=1, keepdims=True), axis=0, keepdims=True)
            loss_ref[...] += jnp.broadcast_to(part * (0.5 / d), loss_ref.shape)
        else:
            out_ref[...] = out

    tok = pl.BlockSpec((tm, d), lambda i: (i, 0))
    blk = pl.BlockSpec((nb, tm, f), lambda i: (0, i, 0))
    in_specs = [tok, _const((1, d)), _resident(wg.shape), _resident(wu.shape), _resident(wd.shape)]
    args = [x, gain, wg, wu, wd]
    out_shape = [jax.ShapeDtypeStruct((t, d), F32)] + [jax.ShapeDtypeStruct((nb, t, f), BF16)] * 3
    out_specs = [tok, blk, blk, blk]
    if with_loss:
        in_specs.append(tok)
        args.append(target)
        out_shape.append(jax.ShapeDtypeStruct((1, 128), F32))
        out_specs.append(_const((1, 128)))
    return _pallas(body, name, (t // tm,), in_specs, out_specs, out_shape, args, ("arbitrary",), carried)


def _ffn_bwd_x(x, gain, dout, gate, up, wg, wu, wd, name, carried=None):
    t, d = x.shape
    nb, f, _ = wg.shape
    tm = TOKEN_TILE

    def body(x_ref, g_ref, dout_ref, gate_ref, up_ref, wg_ref, wu_ref, wd_ref,
             dx_ref, dgain_ref, hb_ref, dgate_ref, dup_ref, dyb_ref):
        xv = x_ref[...]
        gain_v = g_ref[...]
        h, r = _rms(xv, gain_v)
        hb_ref[...] = _bf(h)
        dout_v = dout_ref[...]
        dyb = _bf(0.5 * dout_v)
        dyb_ref[...] = dyb
        dh = jnp.zeros((tm, d), F32)
        nxt = _mm_nt(dyb, wd_ref[0])
        for j in range(nb):
            dact = nxt
            if j + 1 < nb:
                nxt = _mm_nt(dyb, wd_ref[j + 1])
            gt = gate_ref[j].astype(F32)
            uv = up_ref[j].astype(F32)
            s = _sigmoid(gt)
            dup = _bf(dact * (gt * s))
            dgate = _bf((dact * uv) * (s * (1.0 + gt * (1.0 - s))))
            dh = dh + _mm(dgate, wg_ref[j]) + _mm(dup, wu_ref[j])
            dgate_ref[j] = dgate
            dup_ref[j] = dup
        dxn, dgain = _rms_bwd(xv, r, gain_v, dh)
        dx_ref[...] = dout_v + dxn

        @pl.when(pl.program_id(0) == 0)
        def _():
            dgain_ref[...] = jnp.zeros_like(dgain_ref)

        dgain_ref[...] += dgain

    tok = pl.BlockSpec((tm, d), lambda i: (i, 0))
    blk = pl.BlockSpec((nb, tm, f), lambda i: (0, i, 0))
    return _pallas(
        body, name, (t // tm,),
        [tok, _const((1, d)), tok, blk, blk, _resident(wg.shape), _resident(wu.shape), _resident(wd.shape)],
        [tok, _const((1, d)), tok, blk, blk, tok],
        [jax.ShapeDtypeStruct((t, d), F32), jax.ShapeDtypeStruct((1, d), F32), jax.ShapeDtypeStruct((t, d), BF16),
         jax.ShapeDtypeStruct((nb, t, f), BF16), jax.ShapeDtypeStruct((nb, t, f), BF16), jax.ShapeDtypeStruct((t, d), BF16)],
        (x, gain, dout, gate, up, wg, wu, wd), ("arbitrary",), carried)


def _tn_matmul(a, b, name, split_k=1, carried=None):
    t = a.shape[-2]
    k = a.shape[-1]
    n = b.shape[-1]
    tt = min(TN_TILE, t)
    nt = t // tt

    def body(a_ref, b_ref, o_ref):
        @pl.when(pl.program_id(1) == 0)
        def _():
            o_ref[...] = jnp.zeros_like(o_ref)

        o_ref[...] += _mm_tn(a_ref[...], b_ref[...])

    if split_k > 1:
        assert a.ndim == 2 and b.ndim == 2 and k % (split_k * 128) == 0
        tk = k // split_k
        g = split_k
        a_spec = pl.BlockSpec((tt, tk), lambda gi, ti: (ti, gi))
        b_spec = pl.BlockSpec((tt, n), lambda gi, ti: (ti, 0))
        o_spec = pl.BlockSpec((tk, n), lambda gi, ti: (gi, 0))
        out_shape = jax.ShapeDtypeStruct((k, n), F32)
    else:
        g = a.shape[0] if a.ndim == 3 else b.shape[0]
        a_spec = (pl.BlockSpec((None, tt, k), lambda gi, ti: (gi, ti, 0)) if a.ndim == 3
                  else pl.BlockSpec((tt, k), lambda gi, ti: (ti, 0)))
        b_spec = (pl.BlockSpec((None, tt, n), lambda gi, ti: (gi, ti, 0)) if b.ndim == 3
                  else pl.BlockSpec((tt, n), lambda gi, ti: (ti, 0)))
        o_spec = pl.BlockSpec((None, k, n), lambda gi, ti: (gi, 0, 0))
        out_shape = jax.ShapeDtypeStruct((g, k, n), F32)
    return _pallas(body, name, (g, nt), [a_spec, b_spec], [o_spec], [out_shape], (a, b), ("arbitrary", "arbitrary"), carried)[0]


ROW_QKV, ROW_KR, ROW_XC = 0, Q_LORA + KV_LORA, Q_LORA + KV_LORA + QK_ROPE
ROW_GB, ROW_GC, ROW_GL = ROW_XC + D_MODEL, ROW_XC + 2 * D_MODEL, ROW_XC + 3 * D_MODEL
BIG_FROM_ROWS = ((0, ROW_GB, D_MODEL), (D_MODEL, ROW_GL, 2 * D_MODEL), (3 * D_MODEL, ROW_XC, D_MODEL), (4 * D_MODEL, ROW_GC, D_MODEL))


def _inproj_fwd(x1, gain, w_in, w_kr, qa_gain, kva_gain, qh_gain, kh_gain, w_uq, w_uk, w_uv, w_uvt, rope, carried=None):
    t, d = x1.shape
    tm = TOKEN_TILE
    chunk = 512
    chunks = []
    for col, row, size in BIG_FROM_ROWS:
        chunks += [(col + o, row + o, chunk) for o in range(0, size, chunk)]
    of_head = [[c for k, c in enumerate(chunks) if k * N_HEADS // len(chunks) == hd] for hd in range(N_HEADS)]

    def body(x_ref, g_ref, win_ref, wkr_ref, qa_ref, kva_ref, qh_ref, kh_ref, wuq_ref, wuk_ref, wuv_ref, wuvt_ref, cos_ref, slo_ref,
             shi_ref, hb_ref, big_ref, lat_ref, q_ref, k_ref, v_ref, vt_ref):
        h, _ = _rms(x_ref[...], g_ref[...])
        hb = _bf(h)
        hb_ref[...] = hb
        k_rope = _mm_nt(hb, wkr_ref[...])
        lat = jnp.concatenate([_mm_nt(hb, win_ref[ROW_QKV:ROW_KR, :]), k_rope], axis=1)
        lat_ref[...] = lat
        cq, _ = _rms(lat[:, :Q_LORA], qa_ref[...])
        ckv, _ = _rms(lat[:, Q_LORA:Q_LORA + KV_LORA], kva_ref[...])
        cqb = _bf(cq)
        ckvb = _bf(ckv)
        rope_v = (cos_ref[...], slo_ref[...], shi_ref[...])
        q_all = _mm(cqb, wuq_ref[...])
        k_all = _mm(ckvb, wuk_ref[...])
        v_ref[...] = _bf(_mm(ckvb, wuv_ref[...]))
        vt_all = _mm_nt(wuvt_ref[...], ckvb)
        for hd in range(N_HEADS):
            for col, row, size in of_head[hd]:
                big_ref[:, col:col + size] = _mm_nt(hb, win_ref[row:row + size, :])
            lanes = slice(hd * HEAD_PAD, (hd + 1) * HEAD_PAD)
            qn, _ = _rms(q_all[:, lanes], qh_ref[...], QK_DIM)
            q_ref[hd] = _bf(_rope(qn, rope_v))
            kn, _ = _rms(k_all[:, lanes] + k_rope, kh_ref[...], QK_DIM)
            k_ref[hd] = _bf(_rope(kn, rope_v))
            vt_ref[hd] = _bf(vt_all[hd * V_DIM:(hd + 1) * V_DIM])

    tok = lambda c: pl.BlockSpec((tm, c), lambda i: (i, 0))
    head = lambda c: pl.BlockSpec((N_HEADS, tm, c), lambda i: (0, i, 0))
    return _pallas(
        body, "inproj_fwd", (t // tm,),
        [tok(d), _const((1, d)), _resident(w_in.shape), _resident(w_kr.shape), _const((1, Q_LORA)), _const((1, KV_LORA)),
         _const((1, HEAD_PAD)), _const((1, HEAD_PAD)), _resident(w_uq.shape), _resident(w_uk.shape),
         _resident(w_uv.shape), _resident(w_uvt.shape), tok(HEAD_PAD), tok(HEAD_PAD), tok(HEAD_PAD)],
        [tok(d), tok(BIG_COLS), tok(LAT_COLS), head(HEAD_PAD), head(HEAD_PAD), tok(N_HEADS * V_DIM),
         pl.BlockSpec((N_HEADS, V_DIM, tm), lambda i: (0, 0, i))],
        [jax.ShapeDtypeStruct((t, d), BF16), jax.ShapeDtypeStruct((t, BIG_COLS), F32),
         jax.ShapeDtypeStruct((t, LAT_COLS), F32), jax.ShapeDtypeStruct((N_HEADS, t, HEAD_PAD), BF16),
         jax.ShapeDtypeStruct((N_HEADS, t, HEAD_PAD), BF16), jax.ShapeDtypeStruct((t, N_HEADS * V_DIM), BF16),
         jax.ShapeDtypeStruct((N_HEADS, V_DIM, t), BF16)],
        (x1, gain, w_in, w_kr, qa_gain, kva_gain, qh_gain, kh_gain, w_uq, w_uk, w_uv, w_uvt, *rope), ("arbitrary",), carried)


EXP2_SCALE = ATTN_SCALE * 1.4426950408889634


def _diagonal_keep(tk, tq):
    return lax.broadcasted_iota(jnp.int32, (tk, tq), 0) <= lax.broadcasted_iota(jnp.int32, (tk, tq), 1)


def _attn_fwd(q, k, vt, seq, carried=None):
    _, t, _ = q.shape
    nseq = t // seq
    tq = tk = ATTN_TILE
    nq = seq // tq

    def body(q_ref, k_ref, vt_ref, o_ref, lse_ref):
        i = pl.program_id(1)
        qs = [q_ref[h] for h in range(N_HEADS)]
        keep = _diagonal_keep(tk, tq)

        def scores(h, k0):
            return _mm_nt(k_ref[h, pl.ds(k0, tk), :], qs[h])

        def update(h, st, state, k0, diagonal):
            m, l, acc = state
            if diagonal:
                st = jnp.where(keep, st, MASK_VALUE)
            m_new = jnp.maximum(m, jnp.max(st, axis=0, keepdims=True))
            pt = jnp.exp2((st - m_new) * EXP2_SCALE)
            alpha = jnp.exp2((m - m_new) * EXP2_SCALE)
            l_new = alpha * l + jnp.sum(pt, axis=0, keepdims=True)
            return m_new, l_new, alpha * acc + _mm(vt_ref[h, :, pl.ds(k0, tk)], _bf(pt))

        def tiles(states, k0, diagonal):
            st, new = scores(0, k0), []
            for h in range(N_HEADS):
                st_next = scores(h + 1, k0) if h + 1 < N_HEADS else None
                new.append(update(h, st, states[h], k0, diagonal))
                st = st_next
            return tuple(new)

        init = tuple((jnp.full((1, tq), MASK_VALUE, F32), jnp.zeros((1, tq), F32), jnp.zeros((V_DIM, tq), F32))
                     for _ in range(N_HEADS))
        states = lax.fori_loop(0, i, lambda j, s: tiles(s, pl.multiple_of(j * tk, tk), False), init)
        states = tiles(states, pl.multiple_of(i * tk, tk), True)
        outs = []
        for h in range(N_HEADS):
            m, l, acc = states[h]
            outs.append((acc / l).T)
            lse_ref[h] = m * EXP2_SCALE + jnp.log2(l)
        o_ref[...] = _bf(jnp.concatenate(outs, axis=-1))

    return _pallas(
        body, "attn_fwd", (nseq, nq),
        [pl.BlockSpec((N_HEADS, tq, HEAD_PAD), lambda b, i: (0, b * nq + i, 0)),
         pl.BlockSpec((N_HEADS, seq, HEAD_PAD), lambda b, i: (0, b, 0)),
         pl.BlockSpec((N_HEADS, V_DIM, seq), lambda b, i: (0, 0, b))],
        [pl.BlockSpec((tq, N_HEADS * V_DIM), lambda b, i: (b * nq + i, 0)),
         pl.BlockSpec((N_HEADS, 1, tq), lambda b, i: (0, 0, b * nq + i))],
        [jax.ShapeDtypeStruct((t, N_HEADS * V_DIM), BF16), jax.ShapeDtypeStruct((N_HEADS, 1, t), F32)],
        (q, k, vt), ("arbitrary", "arbitrary"), carried)


ATTN_BWD_HEADS = 4


def _attn_bwd(q, k, v, do, lse, delta, seq, carried=None):
    _, t, _ = q.shape
    nseq = t // seq
    tq = tk = ATTN_TILE
    n = seq // tq
    hb = ATTN_BWD_HEADS

    def body(q_ref, k_ref, v_ref, do_ref, lse_ref, delta_ref, dq_ref, dk_ref, dv_ref):
        dq_ref[...] = jnp.zeros_like(dq_ref)
        dk_ref[...] = jnp.zeros_like(dk_ref)
        dv_ref[...] = jnp.zeros_like(dv_ref)
        keep = _diagonal_keep(tk, tq)

        def tile(h, k0, q0, diagonal):
            kj = k_ref[h, pl.ds(k0, tk), :]
            qi = q_ref[h, pl.ds(q0, tq), :]
            doi = _bf(do_ref[pl.ds(q0, tq), h * V_DIM:(h + 1) * V_DIM])
            st = _mm_nt(kj, qi)
            if diagonal:
                st = jnp.where(keep, st, MASK_VALUE)
            pt = jnp.exp2(st * EXP2_SCALE - lse_ref[h, :, pl.ds(q0, tq)])
            dv_ref[pl.ds(k0, tk), h * V_DIM:(h + 1) * V_DIM] += _mm(_bf(pt), doi)
            dpt = _mm_nt(v_ref[pl.ds(k0, tk), h * V_DIM:(h + 1) * V_DIM], doi)
            dst = _bf((pt * (dpt - delta_ref[pl.ds(h, 1), pl.ds(q0, tq)])) * ATTN_SCALE)
            dk_ref[h, pl.ds(k0, tk), :] += _mm(dst, qi)
            dq_ref[h, pl.ds(q0, tq), :] += _mm_tn(dst, kj)

        def kv_step(j, _):
            k0 = pl.multiple_of(j * tk, tk)
            for h in range(hb):
                tile(h, k0, k0, True)

            def q_step(i, _):
                q0 = pl.multiple_of(i * tq, tq)
                for h in range(hb):
                    tile(h, k0, q0, False)
                return 0

            lax.fori_loop(j + 1, n, q_step, 0)
            return 0

        lax.fori_loop(0, n, kv_step, 0)

    hspec = lambda c: pl.BlockSpec((hb, seq, c), lambda b, g: (g, b, 0))
    cols = pl.BlockSpec((seq, hb * V_DIM), lambda b, g: (b, g))
    return _pallas(
        body, "attn_bwd", (nseq, N_HEADS // hb),
        [hspec(HEAD_PAD), hspec(HEAD_PAD), cols, cols,
         pl.BlockSpec((hb, 1, seq), lambda b, g: (g, 0, b)), pl.BlockSpec((None, hb, seq), lambda b, g: (g, 0, b))],
        [hspec(HEAD_PAD), hspec(HEAD_PAD), cols],
        [jax.ShapeDtypeStruct((N_HEADS, t, HEAD_PAD), F32), jax.ShapeDtypeStruct((N_HEADS, t, HEAD_PAD), F32),
         jax.ShapeDtypeStruct((t, N_HEADS * V_DIM), F32)],
        (q, k, v, do, lse, delta), ("arbitrary", "arbitrary"), carried)


def _mixer_values(o_ref, gb_ref, gla_ref, glb_ref, xc_ref, gc_ref, xcp_ref, gcp_ref, bias_ref, cw_ref, wpa_ref, wpc_ref,
                  first_of_seq, early=None):
    y_a = _mm(o_ref[...], wpa_ref[...])
    extra = early() if early is not None else None
    gb = gb_ref[...]
    u = gc_ref[...] * xc_ref[...]
    u_prev = jnp.where(first_of_seq, 0.0, gcp_ref[...] * xcp_ref[...])
    cw = cw_ref[...]
    z = cw[2:3] * u + cw[1:2] * _shift_down(u, u_prev, 1) + cw[0:1] * _shift_down(u, u_prev, 2)
    gbz = _bf(gb * z)
    y_b = _mm(gbz, wpc_ref[...])
    bias = bias_ref[...]
    gate_a = _sigmoid(gla_ref[...] + bias[:, :D_MODEL])
    gate_b = _sigmoid(glb_ref[...] + bias[:, D_MODEL:])
    merged = _bf(gate_a * y_a + gate_b * y_b)
    return gb, u, z, gbz, y_a, y_b, gate_a, gate_b, merged, extra


def _mixer_specs(tm, seq):
    d = D_MODEL
    tok = pl.BlockSpec((tm, d), lambda i: (i, 0))
    col = lambda c: pl.BlockSpec((tm, d), lambda i: (i, c))
    prev = lambda c: pl.BlockSpec((8, d), lambda i: (jnp.maximum(i * (tm // 8) - 1, 0), c))
    o_spec = pl.BlockSpec((tm, N_HEADS * V_DIM), lambda i: (i, 0))
    fwd_specs = [o_spec, col(0), col(1), col(2), col(3), col(4), prev(3), prev(4), _const((1, 2 * d)), _const((3, d)),
                 _resident((N_HEADS * V_DIM, d)), _resident((d, d)), _resident((d, d))]
    return tok, fwd_specs


def _mix_fwd(x1, o, big, gate_bias, conv_w, w_pa, w_pc, w_out, seq, carried=None):
    t, d = x1.shape
    tm = TOKEN_TILE
    tiles_per_seq = seq // tm

    def body(x_ref, o_ref, gb_ref, gla_ref, glb_ref, xc_ref, gc_ref, xcp_ref, gcp_ref, bias_ref, cw_ref, wpa_ref, wpc_ref,
             wout_ref, x2_ref):
        first = pl.program_id(0) % tiles_per_seq == 0
        merged = _mixer_values(o_ref, gb_ref, gla_ref, glb_ref, xc_ref, gc_ref, xcp_ref, gcp_ref, bias_ref, cw_ref, wpa_ref,
                               wpc_ref, first)[-2]
        x2_ref[...] = x_ref[...] + _mm(merged, wout_ref[...])

    tok, fwd_specs = _mixer_specs(tm, seq)
    return _pallas(body, "mix_fwd", (t // tm,), [tok] + fwd_specs, [tok], [jax.ShapeDtypeStruct((t, d), F32)],
                   (x1, o, big, big, big, big, big, big, big, gate_bias, conv_w, w_pa, w_pc, w_out), ("arbitrary",), carried)[0]


def _mix_bwd(dx2, o, big, gate_bias, conv_w, w_pa, w_pc, w_out, seq, carried=None):
    t, d = dx2.shape
    tm = TOKEN_TILE
    tiles_per_seq = seq // tm
    hv = N_HEADS * V_DIM

    def body(dx_ref, o_ref, gb_ref, gla_ref, glb_ref, xc_ref, gc_ref, xcp_ref, gcp_ref, bias_ref, cw_ref, wpa_ref, wpc_ref,
             wout_ref, do_ref, delta_ref, dz_ref, dm_ref, dbias_ref, dwpa_ref, dwpc_ref, dwout_ref):
        @pl.when(pl.program_id(0) == 0)
        def _():
            dbias_ref[...] = jnp.zeros_like(dbias_ref)
            dwpa_ref[...] = jnp.zeros_like(dwpa_ref)
            dwpc_ref[...] = jnp.zeros_like(dwpc_ref)
            dwout_ref[...] = jnp.zeros_like(dwout_ref)

        first = pl.program_id(0) % tiles_per_seq == 0
        dxb = _bf(dx_ref[...])
        gb, _, z, gbz, y_a, y_b, gate_a, gate_b, merged, dmerged = _mixer_values(
            o_ref, gb_ref, gla_ref, glb_ref, xc_ref, gc_ref, xcp_ref, gcp_ref, bias_ref, cw_ref, wpa_ref, wpc_ref, first,
            early=lambda: _mm_nt(dxb, wout_ref[...]))
        dwout_ref[...] += _mm_tn(merged, dxb)
        dya = _bf(dmerged * gate_a)
        dyb = _bf(dmerged * gate_b)
        do_v = _mm_nt(dya, wpa_ref[...])
        dgz = _mm_nt(dyb, wpc_ref[...])
        dwpa_ref[...] += _mm_tn(o_ref[...], dya)
        dwpc_ref[...] += _mm_tn(gbz, dyb)
        dla = (dmerged * y_a) * (gate_a * (1.0 - gate_a))
        dlb = (dmerged * y_b) * (gate_b * (1.0 - gate_b))
        dbias_ref[:, :d] += jnp.sum(dla, axis=0, keepdims=True)
        dbias_ref[:, d:] += jnp.sum(dlb, axis=0, keepdims=True)
        dm_ref[:, d:2 * d] = _bf(dla)
        dm_ref[:, 2 * d:] = _bf(dlb)
        do_ref[...] = do_v
        head = lax.broadcasted_iota(jnp.int32, (N_HEADS, hv), 0) * V_DIM
        col = lax.broadcasted_iota(jnp.int32, (N_HEADS, hv), 1)
        in_head = ((col >= head) & (col < head + V_DIM)).astype(F32)
        delta_ref[...] = lax.dot_general(in_head, do_v * o_ref[...].astype(F32), (((1,), (1,)), ((), ())),
                                         precision=lax.Precision.HIGHEST, preferred_element_type=F32)
        dz_ref[...] = dgz * gb
        dm_ref[:, :d] = _bf(dgz * z)

    tok, fwd_specs = _mixer_specs(tm, seq)
    return _pallas(
        body, "mix_bwd", (t // tm,), [tok] + fwd_specs,
        [pl.BlockSpec((tm, hv), lambda i: (i, 0)), pl.BlockSpec((N_HEADS, tm), lambda i: (0, i)), tok,
         pl.BlockSpec((tm, M_COLS), lambda i: (i, 0)), _const((1, 2 * d)), _const((hv, d)), _const((d, d)), _const((d, d))],
        [jax.ShapeDtypeStruct((t, hv), F32), jax.ShapeDtypeStruct((N_HEADS, t), F32), jax.ShapeDtypeStruct((t, d), F32),
         jax.ShapeDtypeStruct((t, M_COLS), BF16), jax.ShapeDtypeStruct((1, 2 * d), F32), jax.ShapeDtypeStruct((hv, d), F32),
         jax.ShapeDtypeStruct((d, d), F32), jax.ShapeDtypeStruct((d, d), F32)],
        (dx2, o, big, big, big, big, big, big, big, gate_bias, conv_w, w_pa, w_pc, w_out), ("arbitrary",), carried)


def _prep_bwd(lat, big, dz, dq, dk, dv, qa_gain, kva_gain, qh_gain, kh_gain, w_uq, w_uk, w_uv, rope, conv_w, seq, carried=None):
    t = lat.shape[0]
    d = D_MODEL
    tm = PREP_TILE
    tiles_per_seq = seq // tm
    last_blk = t // 8 - 1

    def body(lat_ref, xc_ref, gc_ref, dz_ref, dzn_ref, dq_ref, dk_ref, dv_ref, qa_ref, kva_ref, qh_ref, kh_ref, wuq_ref, wuk_ref,
             wuv_ref, cos_ref, slo_ref, shi_ref, cw_ref,
             dp_ref, dwuq_ref, dwuk_ref, dwuv_ref, dqa_ref, dkva_ref, dqh_ref, dkh_ref, dcw_ref):
        pid = pl.program_id(0)

        @pl.when(pid == 0)
        def _():
            for r in (dwuq_ref, dwuk_ref, dwuv_ref, dqa_ref, dkva_ref, dqh_ref, dkh_ref, dcw_ref):
                r[...] = jnp.zeros_like(r)

        lat_v = lat_ref[...]
        q_lat = lat_v[:, :Q_LORA]
        kv_lat = lat_v[:, Q_LORA:Q_LORA + KV_LORA]
        k_rope = lat_v[:, Q_LORA + KV_LORA:]
        qa_gain_v = qa_ref[...]
        kva_gain_v = kva_ref[...]
        qh_gain_v = qh_ref[...]
        kh_gain_v = kh_ref[...]
        cq, rq = _rms(q_lat, qa_gain_v)
        ckv, rkv = _rms(kv_lat, kva_gain_v)
        cqb = _bf(cq)
        ckvb = _bf(ckv)
        rope_v = (cos_ref[...], slo_ref[...], shi_ref[...])
        lane = lax.broadcasted_iota(jnp.int32, (tm, HEAD_PAD), 1)
        rope_lanes = (lane >= QK_NOPE) & (lane < QK_DIM)
        dk_rope = jnp.zeros((tm, HEAD_PAD), F32)
        dqh_gain = jnp.zeros((1, HEAD_PAD), F32)
        dkh_gain = jnp.zeros((1, HEAD_PAD), F32)
        q_all = _mm(cqb, wuq_ref[...])
        k_all = _mm(ckvb, wuk_ref[...])
        dvb = _bf(dv_ref[...])
        dckv = _mm_nt(dvb, wuv_ref[...])
        dwuv_ref[...] += _mm_tn(ckvb, dvb)

        last = pid % tiles_per_seq == tiles_per_seq - 1
        dzv = dz_ref[...]
        dz_next = jnp.where(last, 0.0, dzn_ref[...])
        dz1 = _shift_up(dzv, dz_next, 1)
        dz2 = _shift_up(dzv, dz_next, 2)
        cw = cw_ref[...]
        xc = xc_ref[...]
        gc = gc_ref[...]
        u = gc * xc
        du = cw[2:3] * dzv + cw[1:2] * dz1 + cw[0:1] * dz2
        dp_ref[:, :d] = _bf(du * gc)
        dp_ref[:, d:2 * d] = _bf(du * xc)
        dcw_ref[0:1, :] += jnp.sum(dz2 * u, axis=0, keepdims=True)
        dcw_ref[1:2, :] += jnp.sum(dz1 * u, axis=0, keepdims=True)
        dcw_ref[2:3, :] += jnp.sum(dzv * u, axis=0, keepdims=True)

        dcq = jnp.zeros((tm, Q_LORA), F32)
        half = N_HEADS // 2
        for part in range(2):
            dq_heads, dk_heads = [], []
            for hd in range(part * half, (part + 1) * half):
                lanes = slice(hd * HEAD_PAD, (hd + 1) * HEAD_PAD)
                q_pre = q_all[:, lanes]
                _, rr = _rms(q_pre, qh_gain_v, QK_DIM)
                dq_pre, dg = _rms_bwd(q_pre, rr, qh_gain_v, _rope_bwd(dq_ref[hd], rope_v), QK_DIM)
                dqh_gain = dqh_gain + dg
                dq_heads.append(_bf(dq_pre))

                k_pre = k_all[:, lanes] + k_rope
                _, rr = _rms(k_pre, kh_gain_v, QK_DIM)
                dk_pre, dg = _rms_bwd(k_pre, rr, kh_gain_v, _rope_bwd(dk_ref[hd], rope_v), QK_DIM)
                dkh_gain = dkh_gain + dg
                dk_rope = dk_rope + jnp.where(rope_lanes, dk_pre, 0.0)
                dk_heads.append(_bf(dk_pre))
            dq_part = jnp.concatenate(dq_heads, axis=1)
            dk_part = jnp.concatenate(dk_heads, axis=1)
            cols = slice(part * half * HEAD_PAD, (part + 1) * half * HEAD_PAD)
            dcq = dcq + _mm_nt(dq_part, wuq_ref[:, cols])
            dckv = dckv + _mm_nt(dk_part, wuk_ref[:, cols])
            dwuq_ref[:, cols] += _mm_tn(cqb, dq_part)
            dwuk_ref[:, cols] += _mm_tn(ckvb, dk_part)
        dqh_ref[...] += dqh_gain
        dkh_ref[...] += dkh_gain
        dq_lat, dg = _rms_bwd(q_lat, rq, qa_gain_v, dcq)
        dqa_ref[...] += dg
        dkv_lat, dg = _rms_bwd(kv_lat, rkv, kva_gain_v, dckv)
        dkva_ref[...] += dg
        dp_ref[:, 2 * d:2 * d + Q_LORA] = _bf(dq_lat)
        dp_ref[:, 2 * d + Q_LORA:2 * d + Q_LORA + KV_LORA] = _bf(dkv_lat)
        dp_ref[:, 2 * d + Q_LORA + KV_LORA:] = _bf(dk_rope)

    tok = lambda c: pl.BlockSpec((tm, c), lambda i: (i, 0))
    col = lambda c: pl.BlockSpec((tm, d), lambda i: (i, c))
    head = lambda c: pl.BlockSpec((N_HEADS, tm, c), lambda i: (0, i, 0))
    nxt = pl.BlockSpec((8, d), lambda i: (jnp.minimum((i + 1) * (tm // 8), last_blk), 0))
    return _pallas(
        body, "prep_bwd", (t // tm,),
        [tok(LAT_COLS), col(3), col(4), tok(d), nxt, head(HEAD_PAD), head(HEAD_PAD), tok(N_HEADS * V_DIM),
         _const((1, Q_LORA)), _const((1, KV_LORA)), _const((1, HEAD_PAD)), _const((1, HEAD_PAD)),
         _resident(w_uq.shape), _resident(w_uk.shape), _resident(w_uv.shape), tok(HEAD_PAD), tok(HEAD_PAD), tok(HEAD_PAD),
         _const((3, d))],
        [tok(P_COLS), _const(w_uq.shape), _const(w_uk.shape), _const(w_uv.shape), _const((1, Q_LORA)),
         _const((1, KV_LORA)), _const((1, HEAD_PAD)), _const((1, HEAD_PAD)), _const((3, d))],
        [jax.ShapeDtypeStruct((t, P_COLS), BF16), jax.ShapeDtypeStruct(w_uq.shape, F32),
         jax.ShapeDtypeStruct(w_uk.shape, F32), jax.ShapeDtypeStruct(w_uv.shape, F32),
         jax.ShapeDtypeStruct((1, Q_LORA), F32), jax.ShapeDtypeStruct((1, KV_LORA), F32),
         jax.ShapeDtypeStruct((1, HEAD_PAD), F32), jax.ShapeDtypeStruct((1, HEAD_PAD), F32), jax.ShapeDtypeStruct((3, d), F32)],
        (lat, big, big, dz, dz, dq, dk, dv, qa_gain, kva_gain, qh_gain, kh_gain, w_uq, w_uk, w_uv, *rope, conv_w),
        ("arbitrary",), carried)


def _inproj_bwd(x1, gain, dx2, dm, dp, w_in, w_kr, carried=None):
    t, d = x1.shape
    tm = TOKEN_TILE

    def body(x_ref, g_ref, dx2_ref, dm_ref, dp_ref, win_ref, wkr_ref, dx1_ref, dgain_ref):
        xv = x_ref[...]
        gain_v = g_ref[...]
        _, r = _rms(xv, gain_v)
        dh = (_mm(dm_ref[:, :d], win_ref[ROW_GB:ROW_GC, :]) + _mm(dm_ref[:, d:], win_ref[ROW_GL:, :])
              + _mm(dp_ref[:, :d], win_ref[ROW_XC:ROW_GB, :]) + _mm(dp_ref[:, d:2 * d], win_ref[ROW_GC:ROW_GL, :])
              + _mm(dp_ref[:, 2 * d:2 * d + ROW_KR], win_ref[ROW_QKV:ROW_KR, :]) + _mm(dp_ref[:, 2 * d + ROW_KR:], wkr_ref[...]))
        dxn, dgain = _rms_bwd(xv, r, gain_v, dh)
        dx1_ref[...] = dx2_ref[...] + dxn

        @pl.when(pl.program_id(0) == 0)
        def _():
            dgain_ref[...] = jnp.zeros_like(dgain_ref)

        dgain_ref[...] += dgain

    tok = lambda c: pl.BlockSpec((tm, c), lambda i: (i, 0))
    return _pallas(
        body, "inproj_bwd", (t // tm,),
        [tok(d), _const((1, d)), tok(d), tok(M_COLS), tok(P_COLS), _resident(w_in.shape), _resident(w_kr.shape)],
        [tok(d), _const((1, d))], [jax.ShapeDtypeStruct((t, d), F32), jax.ShapeDtypeStruct((1, d), F32)],
        (x1, gain, dx2, dm, dp, w_in, w_kr), ("arbitrary",), carried)


def _adamw(quads, name, carried=None):
    k = len(quads)
    rows, cols = quads[0][0].shape
    tr, tc = rows, cols
    for cand in (512, 352, 256, 192, 128, 64):
        if rows % cand == 0 and rows > cand:
            tr = cand
            break
    if tr == rows and rows * cols > 512 * 1024 and cols % 256 == 0:
        tc = 256
    while k * 14 * tr * tc * 4 > (VMEM_LIMIT * 3) // 4 and tr % 16 == 0:
        tr //= 2

    def body(*refs):
        for i in range(k):
            w_ref, g_ref, m_ref, v_ref = refs[4 * i:4 * i + 4]
            delta_ref, nm_ref, nv_ref = refs[4 * k + 3 * i:4 * k + 3 * i + 3]
            delta_ref[...], nm_ref[...], nv_ref[...] = _adamw_update(w_ref[...], g_ref[...], m_ref[...], v_ref[...])

    spec = pl.BlockSpec((tr, tc), lambda i, j: (i, j))
    shape = jax.ShapeDtypeStruct((rows, cols), F32)
    outs = _pallas(body, name, (rows // tr, cols // tc), [spec] * (4 * k), [spec] * (3 * k), [shape] * (3 * k),
                   [a for quad in quads for a in quad], ("arbitrary", "arbitrary"), carried)
    return [tuple(outs[3 * i:3 * i + 3]) for i in range(k)]


def _adamw_update(w, g, m, v):
    nm = ADAM_B1 * m + (1.0 - ADAM_B1) * g
    nv = ADAM_B2 * v + (1.0 - ADAM_B2) * (g * g)
    m_hat = nm * (1.0 / (1.0 - ADAM_B1 ** ADAM_STEP))
    v_hat = nv * (1.0 / (1.0 - ADAM_B2 ** ADAM_STEP))
    return -ADAM_LR * (m_hat / (jnp.sqrt(v_hat) + ADAM_EPS) + ADAM_WD * w), nm, nv


def _adamw_whole(quads, name):
    k = len(quads)

    def body(*refs):
        for i in range(k):
            w_ref, g_ref, m_ref, v_ref = refs[4 * i:4 * i + 4]
            g_out, delta_ref, nm_ref, nv_ref = refs[4 * k + 4 * i:4 * k + 4 * i + 4]
            gv = g_ref[...]
            g_out[...] = gv
            delta_ref[...], nm_ref[...], nv_ref[...] = _adamw_update(w_ref[...], gv, m_ref[...], v_ref[...])

    vm = pl.BlockSpec(memory_space=pltpu.VMEM)
    outs = pl.pallas_call(body, name=name, in_specs=[vm] * (4 * k), out_specs=[vm] * (4 * k),
                          out_shape=[jax.ShapeDtypeStruct(q[0].shape, F32) for q in quads for _ in range(4)],
                          compiler_params=_params())(*[a for quad in quads for a in quad])
    return [tuple(outs[4 * i:4 * i + 4]) for i in range(k)]


def _adamw_small(packed_grads, triples, segments):
    k = len(triples)

    def body(*refs):
        g_ref = refs[0]
        off = 0
        for i in range(k):
            w_ref, m_ref, v_ref = refs[1 + 3 * i:4 + 3 * i]
            g_out, delta_ref, nm_ref, nv_ref = refs[1 + 3 * k + 4 * i:5 + 3 * k + 4 * i]
            gv = g_ref[:, off:off + w_ref.shape[1]]
            g_out[...] = gv
            delta_ref[...], nm_ref[...], nv_ref[...] = _adamw_update(w_ref[...], gv, m_ref[...], v_ref[...])
            off += segments[i]

    vm = pl.BlockSpec(memory_space=pltpu.VMEM)
    outs = pl.pallas_call(
        body, name="adamw_small", in_specs=[vm] * (1 + 3 * k), out_specs=[vm] * (4 * k),
        out_shape=[jax.ShapeDtypeStruct(w.shape, F32) for w, _, _ in triples for _ in range(4)],
    )(packed_grads, *[a for triple in triples for a in triple])
    return [tuple(outs[4 * i:4 * i + 4]) for i in range(k)]


def _place():
    x, y, c = lax.axis_index("x"), lax.axis_index("y"), lax.axis_index("c")
    other_chips = [(1 - x, y), (x, 1 - y), (1 - x, 1 - y)]
    return x, y, c, other_chips


def _remote(src, dst, sems, send, recv, device):
    return pltpu.make_async_remote_copy(src_ref=src, dst_ref=dst, send_sem=sems.at[send], recv_sem=sems.at[recv],
                                        device_id=device, device_id_type=MESH_ID)


def _cast_shards(shards, out_dtypes):
    n = len(shards)

    def body(*refs):
        ins, outs, stage, sems = refs[:n], refs[n:2 * n], refs[2 * n:3 * n], refs[3 * n]
        x, y, _, _ = _place()
        me = 2 * x + y
        copies = []
        for w in range(n):
            stage[w][...] = ins[w][...].astype(out_dtypes[w])
            copies.append(pltpu.make_async_copy(stage[w], outs[w].at[me], sems.at[w]))
            copies[-1].start()
        for cp in copies:
            cp.wait()

    vm = pl.BlockSpec(memory_space=pltpu.VMEM)
    return pl.pallas_call(
        body, name="cast_shards", in_specs=[vm] * n, out_specs=[ANY] * n,
        out_shape=[jax.ShapeDtypeStruct((N_CHIPS,) + s.shape, dt) for s, dt in zip(shards, out_dtypes)],
        scratch_shapes=[pltpu.VMEM(s.shape, dt) for s, dt in zip(shards, out_dtypes)] + [pltpu.SemaphoreType.DMA((n,))],
        compiler_params=_params())(*shards)


BF16_ROWS = 16


def _split_rows(rows):
    return (rows // 2) % BF16_ROWS == 0


def _half_shape(rows, cols):
    return (rows // 2, cols) if _split_rows(rows) else (rows, cols // 2)


def _half(rows, cols, which):
    if _split_rows(rows):
        return (pl.ds(pl.multiple_of(which * (rows // 2), BF16_ROWS), rows // 2), slice(None))
    return (slice(None), pl.ds(pl.multiple_of(which * (cols // 2), 128), cols // 2))


def _gather_carried(bufs):
    n = len(bufs)

    def half(w, slot, which):
        _, rows, cols = bufs[w].shape
        return (slot,) + _half(rows, cols, which)

    def start(ins, outs, sems, base):
        x, y, c, other_chips = _place()
        me = 2 * x + y
        for w in range(n):
            mine = outs[w].at[half(w, me, c)]
            for p, (px, py) in enumerate(other_chips):
                _remote(mine, mine, sems, base + 12 * w + p, base + 12 * w + 3 + p, (px, py, c)).start()

    def finish(ins, outs, sems, base):
        x, y, c, other_chips = _place()
        me = 2 * x + y
        for w in range(n):
            for p, (px, py) in enumerate(other_chips):
                got = outs[w].at[half(w, 2 * px + py, c)]
                _remote(got, got, sems, base + 12 * w + p, base + 12 * w + 3 + p, (px, py, c)).wait_recv()
                _remote(got, got, sems, base + 12 * w + 6 + p, base + 12 * w + 9 + p, (x, y, 1 - c)).start()
        for w in range(n):
            mine = outs[w].at[half(w, me, c)]
            for p, (px, py) in enumerate(other_chips):
                got = outs[w].at[half(w, 2 * px + py, c)]
                theirs = outs[w].at[half(w, 2 * px + py, 1 - c)]
                _remote(got, theirs, sems, base + 12 * w + 6 + p, base + 12 * w + 9 + p, (x, y, 1 - c)).wait()
                _remote(mine, mine, sems, base + 12 * w + p, base + 12 * w + 3 + p, (px, py, c)).wait_send()

    shapes = [jax.ShapeDtypeStruct(b.shape, b.dtype) for b in bufs]
    return _Carried(bufs, shapes, {w: w for w in range(n)}, 12 * n, start, finish)


def _swap_carried(grads):
    n = len(grads)

    def copy(w, ins, outs, sems, base):
        x, y, c, _ = _place()
        _, rows, cols = grads[w].shape
        theirs = ins[w].at[(slice(None),) + _half(rows, cols, 1 - c)]
        return _remote(theirs, outs[w], sems, base + 2 * w, base + 2 * w + 1, (x, y, 1 - c))

    def start(ins, outs, sems, base):
        for w in range(n):
            copy(w, ins, outs, sems, base).start()

    def finish(ins, outs, sems, base):
        for w in range(n):
            copy(w, ins, outs, sems, base).wait()

    shapes = [jax.ShapeDtypeStruct((g.shape[0],) + _half_shape(*g.shape[1:]), F32) for g in grads]
    return _Carried(grads, shapes, {}, 2 * n, start, finish)


def _row_tile(rows):
    for cand in (512, 352, 256, 192, 128, 96, 64, 32, 16):
        if rows % cand == 0:
            return cand
    return rows


def _half_block_index(split_rows, tiles, i, core):
    return (core * tiles + i, 0) if split_rows else (i, core)


def _chip_partial(grad, other, place, name):
    nblk, hr, hc = other.shape
    by_rows = _split_rows(grad.shape[1])
    tr = _row_tile(hr)
    tiles = hr // tr

    def body(place_ref, g_ref, o_ref, sum_ref, sum_bf_ref):
        s = g_ref[...] + o_ref[...]
        sum_ref[...] = s
        sum_bf_ref[...] = _bf(s)

    grid_spec = pltpu.PrefetchScalarGridSpec(
        num_scalar_prefetch=1, grid=(nblk, tiles),
        in_specs=[pl.BlockSpec((None, tr, hc), lambda b, i, place_ref: (b,) + _half_block_index(by_rows, tiles, i, place_ref[1])),
                  pl.BlockSpec((None, tr, hc), lambda b, i, place_ref: (b, i, 0))],
        out_specs=[pl.BlockSpec((None, tr, hc), lambda b, i, place_ref: (b, i, 0))] * 2)
    return pl.pallas_call(body, name=name, grid_spec=grid_spec,
                          out_shape=[jax.ShapeDtypeStruct(other.shape, F32), jax.ShapeDtypeStruct(other.shape, BF16)],
                          compiler_params=_params(("arbitrary", "arbitrary")))(place, grad, other)


def _chip_partial_small(grads, others, place):
    n = len(grads)

    def body(*refs):
        place_ref, g_refs, o_refs = refs[0], refs[1:1 + n], refs[1 + n:1 + 2 * n]
        sum_refs, bf_refs = refs[1 + 2 * n:1 + 3 * n], refs[1 + 3 * n:]
        core = place_ref[1]
        for w in range(n):
            _, rows, cols = grads[w].shape
            s = g_refs[w][(slice(None),) + _half(rows, cols, core)] + o_refs[w][...]
            sum_refs[w][...] = s
            bf_refs[w][...] = _bf(s)

    vm = pl.BlockSpec(memory_space=pltpu.VMEM)
    outs = pl.pallas_call(
        body, name="chip_partial_small", in_specs=[pl.BlockSpec(memory_space=pltpu.SMEM)] + [vm] * (2 * n), out_specs=[vm] * (2 * n),
        out_shape=[jax.ShapeDtypeStruct(o.shape, F32) for o in others] + [jax.ShapeDtypeStruct(o.shape, BF16) for o in others],
        compiler_params=_params())(place, *grads, *others)
    return list(zip(outs[:n], outs[n:]))


def _chip_total_small(owns, receiveds, place, shapes):
    n = len(owns)

    def body(*refs):
        place_ref, own_refs, r_refs, out_refs = refs[0], refs[1:1 + n], refs[1 + n:1 + 2 * n], refs[1 + 2 * n:]
        chip, core = place_ref[0], place_ref[1]
        for w in range(n):
            r = [r_refs[w][(chip + k) % N_CHIPS].astype(F32) for k in (1, 2, 3)]
            out_refs[w][_half(*shapes[w], core)] = own_refs[w][chip] + ((r[0] + r[1]) + r[2])

    vm = pl.BlockSpec(memory_space=pltpu.VMEM)
    return list(pl.pallas_call(
        body, name="chip_total_small", in_specs=[pl.BlockSpec(memory_space=pltpu.SMEM)] + [vm] * (2 * n), out_specs=[vm] * n,
        out_shape=[jax.ShapeDtypeStruct(tuple(s), F32) for s in shapes], compiler_params=_params())(place, *owns, *receiveds))


def _send_carried(partials):
    n = len(partials)

    def start(ins, outs, sems, base):
        x, y, c, other_chips = _place()
        me = 2 * x + y
        for w in range(n):
            for p, (px, py) in enumerate(other_chips):
                _remote(ins[w].at[2 * px + py], outs[w].at[me], sems, base + 6 * w + p, base + 6 * w + 3 + p, (px, py, c)).start()

    def finish(ins, outs, sems, base):
        x, y, c, other_chips = _place()
        for w in range(n):
            for p, (px, py) in enumerate(other_chips):
                _remote(ins[w].at[2 * px + py], outs[w].at[2 * px + py], sems, base + 6 * w + p, base + 6 * w + 3 + p,
                        (px, py, c)).wait()

    return _Carried(partials, [jax.ShapeDtypeStruct(p.shape, BF16) for p in partials], {}, 6 * n, start, finish)


def _chip_total(own, received, place, shape, name):
    nblk, hr, hc = own.shape
    by_rows = _split_rows(shape[0])
    tr = _row_tile(hr)
    tiles = hr // tr

    def body(place_ref, own_ref, r1_ref, r2_ref, r3_ref, out_ref):
        out_ref[...] = own_ref[...] + ((r1_ref[...].astype(F32) + r2_ref[...].astype(F32)) + r3_ref[...].astype(F32))

    def slot(k):
        return pl.BlockSpec((None, tr, hc), lambda i, place_ref: ((place_ref[0] + k) % N_CHIPS, i, 0))

    grid_spec = pltpu.PrefetchScalarGridSpec(
        num_scalar_prefetch=1, grid=(tiles,), in_specs=[slot(0), slot(1), slot(2), slot(3)],
        out_specs=pl.BlockSpec((tr, hc), lambda i, place_ref: _half_block_index(by_rows, tiles, i, place_ref[1])))
    return pl.pallas_call(body, name=name, grid_spec=grid_spec, out_shape=jax.ShapeDtypeStruct(tuple(shape), F32),
                          compiler_params=_params(("arbitrary",)))(place, own, received, received, received)


def _join_carried(totals):
    n = len(totals)

    def copy(w, outs, sems, base):
        x, y, c, _ = _place()
        mine = outs[w].at[_half(*totals[w].shape, c)]
        return _remote(mine, mine, sems, base + 2 * w, base + 2 * w + 1, (x, y, 1 - c))

    def start(ins, outs, sems, base):
        for w in range(n):
            copy(w, outs, sems, base).start()

    def finish(ins, outs, sems, base):
        for w in range(n):
            copy(w, outs, sems, base).wait()

    shapes = [jax.ShapeDtypeStruct(a.shape, F32) for a in totals]
    return _Carried(totals, shapes, {w: w for w in range(n)}, 2 * n, start, finish)


def _sum_devices(vec):
    rows, n = vec.shape

    def body(v_ref, out_ref, buf, send_sems, recv_sems):
        x, y, c, _ = _place()
        me = 4 * x + 2 * y + c
        buf[me] = v_ref[...]
        sends = []
        for k in range(1, N_DEV):
            peer = (1 - x if k & 4 else x, 1 - y if k & 2 else y, 1 - c if k & 1 else c)
            cp = pltpu.make_async_remote_copy(src_ref=v_ref, dst_ref=buf.at[me], send_sem=send_sems.at[k], recv_sem=recv_sems.at[k],
                                              device_id=peer, device_id_type=MESH_ID)
            cp.start()
            sends.append(cp)
        for cp in sends:
            cp.wait()
        total = buf[0]
        for dev in range(1, N_DEV):
            total = total + buf[dev]
        out_ref[...] = total

    vm = pl.BlockSpec(memory_space=pltpu.VMEM)
    return pl.pallas_call(
        body, name="sum_devices", in_specs=[vm], out_specs=vm, out_shape=jax.ShapeDtypeStruct((rows, n), F32),
        scratch_shapes=[pltpu.VMEM((N_DEV, rows, n), F32), pltpu.SemaphoreType.DMA((N_DEV,)), pltpu.SemaphoreType.DMA((N_DEV,))],
    )(vec)


def _rope_tables(positions):
    half = ROPE_HALF
    inv_freq = 1.0 / (ROPE_THETA ** (jnp.arange(half, dtype=F32) / half))
    ang = positions.astype(F32).reshape(-1, 1) * inv_freq
    cos, sin = jnp.cos(ang), jnp.sin(ang)
    t = ang.shape[0]
    ones, zeros = jnp.ones((t, QK_NOPE), F32), jnp.zeros((t, QK_NOPE), F32)
    pad, none = HEAD_PAD - QK_DIM, zeros[:, :half]
    cos_full = jnp.concatenate([ones, cos, cos, ones[:, :pad]], axis=1)
    s_lo = jnp.concatenate([zeros, -sin, none, zeros[:, :pad]], axis=1)
    s_hi = jnp.concatenate([zeros, none, sin, zeros[:, :pad]], axis=1)
    return cos_full, s_lo, s_hi


def _partials(names, grads, from_sibling, place):
    return [_chip_partial(g, o, place, "chip_partial_" + n) for n, g, o in zip(names, grads, from_sibling)]


def _totals(names, grads, partials, received, place):
    return [_chip_total(pf, r, place, g.shape[1:], "chip_total_" + n) for n, g, (pf, _), r in zip(names, grads, partials, received)]


def _kernel_layouts(full):
    w_in = full["w_in"]
    w_kr = jnp.pad(w_in[ROW_KR:ROW_XC], ((QK_NOPE, HEAD_PAD - QK_DIM), (0, 0)))
    w_uq = jnp.pad(full["w_uq"].reshape(Q_LORA, N_HEADS, QK_DIM), ((0, 0), (0, 0), (0, HEAD_PAD - QK_DIM)))
    w_uk = jnp.pad(full["w_uk"].reshape(KV_LORA, N_HEADS, QK_NOPE), ((0, 0), (0, 0), (0, HEAD_PAD - QK_NOPE)))
    return {"w_in": w_in, "w_kr": w_kr, "w_uq": w_uq.reshape(Q_LORA, N_HEADS * HEAD_PAD),
            "w_uk": w_uk.reshape(KV_LORA, N_HEADS * HEAD_PAD), "w_uv": full["w_uv"], "w_uvt": full["w_uv"].T}


def _global_layouts(g):
    d = D_MODEL
    dm, dp = g["w_m"], g["w_p"]
    o_lat = 2 * d
    o_kr = o_lat + Q_LORA + KV_LORA + QK_NOPE
    w_in = jnp.concatenate([dp[o_lat:o_lat + Q_LORA + KV_LORA], dp[o_kr:o_kr + QK_ROPE], dp[:d], dm[:d], dp[d:o_lat], dm[d:]], axis=0)
    w_uq = g["w_uq"].reshape(Q_LORA, N_HEADS, HEAD_PAD)[:, :, :QK_DIM].reshape(Q_LORA, N_HEADS * QK_DIM)
    w_uk = g["w_uk"].reshape(KV_LORA, N_HEADS, HEAD_PAD)[:, :, :QK_NOPE].reshape(KV_LORA, N_HEADS * QK_NOPE)
    return {"w_in": w_in, "w_uq": w_uq, "w_uk": w_uk, "w_uv": g["w_uv"], "w_proj_attn": g["w_pa"], "w_proj_conv": g["w_pc"],
            "w_out": g["w_out"]}


def _col_blocks(a):
    r, c = a.shape
    return a.reshape(r, N_CHIPS, c // N_CHIPS).transpose(1, 0, 2)


def _from_col_blocks(a):
    n, r, c = a.shape
    return a.transpose(1, 0, 2).reshape(r, n * c)


COL_SHARDED = ("w_uq", "w_uk", "w_uv", "w_proj_attn")
TRANSPOSED = ("ffn1_w_gate", "ffn1_w_up", "ffn2_w_gate", "ffn2_w_up", "w_in")
SMALL = (("ffn1_norm", 1024), ("mix_norm", 1024), ("gate_bias", 2048), ("q_a_norm", 384), ("kv_a_norm", 256),
         ("q_head_norm", 128), ("k_head_norm", 128), ("ffn2_norm", 1024))
WEIGHT_ORDER = ("ffn1_norm", "ffn1_w_gate", "ffn1_w_up", "ffn1_w_down", "mix_norm", "w_in", "gate_bias", "q_a_norm", "w_uq",
                "kv_a_norm", "w_uk", "w_uv", "q_head_norm", "k_head_norm", "w_proj_attn", "conv_w", "w_proj_conv", "w_out",
                "ffn2_norm", "ffn2_w_gate", "ffn2_w_up", "ffn2_w_down")
MATRICES = ("ffn1_w_gate", "ffn1_w_up", "ffn1_w_down", "w_in", "w_uq", "w_uk", "w_uv", "w_proj_attn", "w_proj_conv", "w_out",
            "ffn2_w_gate", "ffn2_w_up", "ffn2_w_down")
GROUP_FFN1 = ("ffn1_w_gate", "ffn1_w_up", "ffn1_w_down")
GROUP_IN = ("w_in", "w_uq", "w_uk", "w_uv", "conv_w")
GROUP_MIX = ("w_proj_attn", "w_proj_conv", "w_out")
GROUP_FFN2 = ("ffn2_w_gate", "ffn2_w_up", "ffn2_w_down")
GROUP_MID = ("w_in", "w_uq", "w_uk", "w_uv", "w_proj_attn", "w_proj_conv", "w_out")


def _pad_lanes(a, n):
    return jnp.pad(a.reshape(1, -1), ((0, 0), (0, n - a.size)))


def kernel(x, positions, ffn1_norm, ffn1_w_gate, ffn1_w_up, ffn1_w_down, mix_norm, w_in, gate_bias, q_a_norm, w_uq, kv_a_norm, w_uk, w_uv, q_head_norm, k_head_norm, w_proj_attn, conv_w, w_proj_conv, w_out, ffn2_norm, ffn2_w_gate, ffn2_w_up, ffn2_w_down, loss_target, m_ffn1_norm, m_ffn1_w_gate, m_ffn1_w_up, m_ffn1_w_down, m_mix_norm, m_w_in, m_gate_bias, m_q_a_norm, m_w_uq, m_kv_a_norm, m_w_uk, m_w_uv, m_q_head_norm, m_k_head_norm, m_w_proj_attn, m_conv_w, m_w_proj_conv, m_w_out, m_ffn2_norm, m_ffn2_w_gate, m_ffn2_w_up, m_ffn2_w_down, v_ffn1_norm, v_ffn1_w_gate, v_ffn1_w_up, v_ffn1_w_down, v_mix_norm, v_w_in, v_gate_bias, v_q_a_norm, v_w_uq, v_kv_a_norm, v_w_uk, v_w_uv, v_q_head_norm, v_k_head_norm, v_w_proj_attn, v_conv_w, v_w_proj_conv, v_w_out, v_ffn2_norm, v_ffn2_w_gate, v_ffn2_w_up, v_ffn2_w_down):
    args = dict(locals())
    view = lambda n, a: a.T if n in TRANSPOSED else a
    weights = {n: view(n, args[n]) for n in WEIGHT_ORDER}
    moments_m = {n: view(n, args["m_" + n]) for n in WEIGHT_ORDER}
    moments_v = {n: view(n, args["v_" + n]) for n in WEIGHT_ORDER}
    nb, seq, d = x.shape
    t = nb * seq
    chip = (2 * lax.axis_index("x") + lax.axis_index("y")).astype(jnp.int32)
    place = jnp.stack([chip, lax.axis_index("c").astype(jnp.int32)])
    grads, delta, new_m, new_v = {}, {}, {}, {}

    def adamw(names, carried=None):
        results = _adamw([(weights[n], grads[n], moments_m[n], moments_v[n]) for n in names], "adamw_" + names[0], carried)
        for n, (dn, mn, vn) in zip(names, results):
            delta[n], new_m[n], new_v[n] = dn, mn, vn

    conv_rows = conv_w.shape[0]
    conv_shard = jnp.pad(conv_w, ((0, 16 - conv_rows), (0, 0)))
    bufs = dict(zip(MATRICES + ("conv_w",), _cast_shards([weights[n] for n in MATRICES] + [conv_shard],
                                                         [BF16] * len(MATRICES) + [F32])))
    blocks = dict(zip(GROUP_FFN1, _run(_gather_carried([bufs[n] for n in GROUP_FFN1]), "gather_ffn1")))
    p = {n: _pad_lanes(weights[n], size) for n, size in SMALL}
    rope = _rope_tables(positions)
    x_tok = x.reshape(t, d)

    gather_in = _gather_carried([bufs[n] for n in GROUP_IN])
    x1, gate1, up1, act1 = _ffn_fwd(x_tok, p["ffn1_norm"], blocks["ffn1_w_gate"], blocks["ffn1_w_up"], blocks["ffn1_w_down"], None,
                                    "ffn1_fwd", gather_in)
    blocks.update(zip(GROUP_IN, gather_in.results))
    w = _kernel_layouts({"w_in": blocks["w_in"].reshape(-1, d), **{n: _from_col_blocks(blocks[n]) for n in ("w_uq", "w_uk", "w_uv")}})
    p["conv_w"] = _from_col_blocks(blocks["conv_w"])[:conv_rows]

    gather_mix = _gather_carried([bufs[n] for n in GROUP_MIX + GROUP_FFN2[2:]])
    h2b, big, lat, q, k, v, vt = _inproj_fwd(x1, p["mix_norm"], w["w_in"], w["w_kr"], p["q_a_norm"], p["kv_a_norm"], p["q_head_norm"],
                                             p["k_head_norm"], w["w_uq"], w["w_uk"], w["w_uv"], w["w_uvt"], rope, gather_mix)
    blocks.update(zip(GROUP_MIX + GROUP_FFN2[2:], gather_mix.results))
    w_pa = _from_col_blocks(blocks["w_proj_attn"])
    w_pc, w_out_full = blocks["w_proj_conv"].reshape(-1, d), blocks["w_out"].reshape(-1, d)

    gather_ffn2 = _gather_carried([bufs[n] for n in GROUP_FFN2[:2]])
    o, lse = _attn_fwd(q, k, vt, seq, gather_ffn2)
    x2 = _mix_fwd(x1, o, big, p["gate_bias"], p["conv_w"], w_pa, w_pc, w_out_full, seq)
    wg2, wu2 = gather_ffn2.results
    wd2 = blocks["ffn2_w_down"]
    dx3, gate2, up2, act2, loss = _ffn_fwd(x2, p["ffn2_norm"], wg2, wu2, wd2, loss_target.reshape(t, d), "ffn2_fwd")

    dx2, dg_ffn2, hb2, dgate2, dup2, dyb2 = _ffn_bwd_x(x2, p["ffn2_norm"], dx3, gate2, up2, wg2, wu2, wd2, "ffn2_bwd")
    g_ffn2 = [_tn_matmul(dgate2, hb2, "ffn2_dw_gate"), _tn_matmul(dup2, hb2, "ffn2_dw_up"), _tn_matmul(act2, dyb2, "ffn2_dw_down")]
    swap = _swap_carried(g_ffn2)
    do, delta_o, dz, dm, dbias, dw_pa, dw_pc, dw_out = _mix_bwd(dx2, o, big, p["gate_bias"], p["conv_w"], w_pa, w_pc, w_out_full, seq,
                                                                swap)
    part = _partials(GROUP_FFN2, g_ffn2, swap.results, place)
    send = _send_carried([pb for _, pb in part])
    dq, dk, dv = _attn_bwd(q, k, v, do, lse, delta_o.reshape(N_HEADS // ATTN_BWD_HEADS, ATTN_BWD_HEADS, -1), seq, send)
    join = _join_carried(_totals(GROUP_FFN2, g_ffn2, part, send.results, place))
    dp, dw_uq, dw_uk, dw_uv, dqa, dkva, dqh, dkh, dcw = _prep_bwd(
        lat, big, dz, dq, dk, dv, p["q_a_norm"], p["kv_a_norm"], p["q_head_norm"], p["k_head_norm"], w["w_uq"], w["w_uk"],
        w["w_uv"], rope, p["conv_w"], seq, join)
    grads.update(zip(GROUP_FFN2, join.results))

    gg = _global_layouts({"w_m": _tn_matmul(dm, h2b, "dw_in_m", split_k=2), "w_p": _tn_matmul(dp, h2b, "dw_in_p", split_k=2),
                          "w_uq": dw_uq, "w_uk": dw_uk, "w_uv": dw_uv, "w_pa": dw_pa, "w_pc": dw_pc, "w_out": dw_out})
    g_mid = [_col_blocks(gg[n]) if n in COL_SHARDED else gg[n].reshape(N_CHIPS, -1, gg[n].shape[-1]) for n in GROUP_MID]
    swap = _swap_carried(g_mid)
    dx1, dg_mix = _inproj_bwd(x1, p["mix_norm"], dx2, dm, dp, w["w_in"], w["w_kr"], swap)
    part = (_partials(GROUP_MID[:1], g_mid[:1], swap.results[:1], place)
            + _chip_partial_small(g_mid[1:], swap.results[1:], place))
    send = _send_carried([pb for _, pb in part])
    grad_x, dg_ffn1, hb1, dgate1, dup1, dyb1 = _ffn_bwd_x(x_tok, p["ffn1_norm"], dx1, gate1, up1, blocks["ffn1_w_gate"],
                                                         blocks["ffn1_w_up"], blocks["ffn1_w_down"], "ffn1_bwd", send)

    small_grads = {"ffn1_norm": dg_ffn1, "mix_norm": dg_mix, "gate_bias": dbias, "q_a_norm": dqa, "kv_a_norm": dkva,
                   "q_head_norm": dqh, "k_head_norm": dkh, "ffn2_norm": dg_ffn2}
    packed = jnp.concatenate([small_grads[n] for n, _ in SMALL] + [dcw.reshape(1, -1), loss], axis=1)
    total = _sum_devices(packed.reshape(8, -1)).reshape(1, -1)
    n_small = sum(size for _, size in SMALL)
    conv_cols = conv_w.shape[1]
    conv_total = total[:, n_small:n_small + conv_rows * d].reshape(conv_rows, d)
    grads["conv_w"] = lax.dynamic_slice_in_dim(conv_total, chip * conv_cols, conv_cols, axis=1)
    loss_total = total[0, n_small + conv_rows * d]

    join = _join_carried(_totals(GROUP_MID[:1], g_mid[:1], part[:1], send.results[:1], place)
                         + _chip_total_small([pf for pf, _ in part[1:]], send.results[1:], place, [g.shape[1:] for g in g_mid[1:]]))
    g_gate = _tn_matmul(dgate1, hb1, "ffn1_dw_gate", carried=join)
    grads.update(zip(GROUP_MID, join.results))
    swap_gate = _swap_carried([g_gate])
    g_up = _tn_matmul(dup1, hb1, "ffn1_dw_up", carried=swap_gate)
    part_gate = _partials(GROUP_FFN1[:1], [g_gate], swap_gate.results, place)
    send_gate, swap_up = _send_carried([part_gate[0][1]]), _swap_carried([g_up])
    g_down = _tn_matmul(act1, dyb1, "ffn1_dw_down", carried=_both(send_gate, swap_up))
    join_gate = _join_carried(_totals(GROUP_FFN1[:1], [g_gate], part_gate, send_gate.results, place))
    part_up = _partials(GROUP_FFN1[1:2], [g_up], swap_up.results, place)
    send_up, swap_down = _send_carried([part_up[0][1]]), _swap_carried([g_down])
    adamw(GROUP_FFN2, _both(_both(send_up, swap_down), join_gate))
    grads["ffn1_w_gate"] = join_gate.results[0]
    join_up = _join_carried(_totals(GROUP_FFN1[1:2], [g_up], part_up, send_up.results, place))
    part_down = _partials(GROUP_FFN1[2:], [g_down], swap_down.results, place)
    send_down = _send_carried([part_down[0][1]])
    adamw(("w_in",), _both(send_down, join_up))
    grads["ffn1_w_up"] = join_up.results[0]
    join_down = _join_carried(_totals(GROUP_FFN1[2:], [g_down], part_down, send_down.results, place))
    adamw(GROUP_FFN1[:2], join_down)
    grads["ffn1_w_down"] = join_down.results[0]
    adamw(GROUP_FFN1[2:])
    others = GROUP_MID[1:] + ("conv_w",)
    for n, (gn, dn, mn, vn) in zip(others, _adamw_whole([(weights[n], grads[n], moments_m[n], moments_v[n]) for n in others],
                                                        "adamw_others")):
        grads[n], delta[n], new_m[n], new_v[n] = gn, dn, mn, vn

    row = lambda a: a.reshape(1, -1)
    small = _adamw_small(total, [(row(weights[n]), row(moments_m[n]), row(moments_v[n])) for n, _ in SMALL], [size for _, size in SMALL])
    for (n, _), (gn, dn, mn, vn) in zip(SMALL, small):
        grads[n], delta[n], new_m[n], new_v[n] = gn.reshape(-1), dn.reshape(-1), mn.reshape(-1), vn.reshape(-1)

    return (loss_total, grad_x.reshape(nb, seq, d), *[view(n, src[n]) for src in (grads, delta, new_m, new_v) for n in WEIGHT_ORDER])
```

```python
import functools

import jax
import jax.numpy as jnp
from jax import lax
from jax.experimental import pallas as pl
from jax.experimental.pallas import tpu as pltpu

F32 = jnp.float32
BF16 = jnp.bfloat16

D_MODEL = 1024
N_HEADS = 8
QK_NOPE = 64
QK_ROPE = 32
QK_DIM = QK_NOPE + QK_ROPE
V_DIM = 64
HEAD_PAD = 128
Q_LORA = 384
KV_LORA = 256
ROPE_THETA = 10000.0
NORM_EPS = 1e-6
ATTN_SCALE = QK_DIM ** -0.5
MASK_VALUE = -1e30
N_CHIPS = 4
N_DEV = 8

ADAM_LR = 0.001
ADAM_B1 = 0.9
ADAM_B2 = 0.999
ADAM_EPS = 1e-08
ADAM_WD = 0.01
ADAM_STEP = 10

TOKEN_TILE = 256
PREP_TILE = 256
ATTN_TILE = 512
TN_TILE = 2048
VMEM_LIMIT = 56 * 1024 * 1024

M_COLS = 3 * D_MODEL
P_COLS = 2 * D_MODEL + Q_LORA + KV_LORA + HEAD_PAD
BIG_COLS = 5 * D_MODEL
LAT_COLS = Q_LORA + KV_LORA + HEAD_PAD

MESH_ID = pl.DeviceIdType.MESH
ANY = pl.BlockSpec(memory_space=pl.ANY)


def _params(semantics=None):
    return pltpu.CompilerParams(dimension_semantics=semantics, vmem_limit_bytes=VMEM_LIMIT)


class _Carried:
    def __init__(self, operands, out_shapes, aliases, n_sems, start, finish):
        self.operands, self.out_shapes, self.aliases, self.n_sems = list(operands), list(out_shapes), dict(aliases), n_sems
        self.start, self.finish = start, finish
        self.results = None


def _both(a, b):
    na, nao = len(a.operands), len(a.out_shapes)

    def start(ins, outs, sems, base):
        a.start(ins[:na], outs[:nao], sems, base)
        b.start(ins[na:], outs[nao:], sems, base + a.n_sems)

    def finish(ins, outs, sems, base):
        a.finish(ins[:na], outs[:nao], sems, base)
        b.finish(ins[na:], outs[nao:], sems, base + a.n_sems)

    aliases = dict(a.aliases)
    aliases.update({na + i: nao + o for i, o in b.aliases.items()})
    both = _Carried(a.operands + b.operands, a.out_shapes + b.out_shapes, aliases, a.n_sems + b.n_sems, start, finish)
    both.parts = (a, b)
    return both


def _set_results(carried, results):
    carried.results = list(results)
    if hasattr(carried, "parts"):
        a, b = carried.parts
        _set_results(a, results[:len(a.out_shapes)])
        _set_results(b, results[len(a.out_shapes):])


def _pallas(body, name, grid, in_specs, out_specs, out_shape, args, semantics, carried=None):
    if carried is None:
        return pl.pallas_call(body, name=name, grid=grid, in_specs=in_specs, out_specs=out_specs, out_shape=out_shape,
                              compiler_params=_params(semantics))(*args)
    n_in, n_out, n_ci, n_co = len(in_specs), len(out_specs), len(carried.operands), len(carried.out_shapes)

    def wrapped(*refs):
        ins, c_ins = refs[:n_in], refs[n_in:n_in + n_ci]
        outs, c_outs = refs[n_in + n_ci:n_in + n_ci + n_out], refs[n_in + n_ci + n_out:n_in + n_ci + n_out + n_co]
        sems = refs[-1]
        first = pl.program_id(0) == 0
        last = pl.program_id(0) == grid[0] - 1
        for axis in range(1, len(grid)):
            first = jnp.logical_and(first, pl.program_id(axis) == 0)
            last = jnp.logical_and(last, pl.program_id(axis) == grid[axis] - 1)

        @pl.when(first)
        def _():
            carried.start(c_ins, c_outs, sems, 0)

        body(*ins, *outs)

        @pl.when(last)
        def _():
            carried.finish(c_ins, c_outs, sems, 0)

    results = pl.pallas_call(
        wrapped, name=name, grid=grid, in_specs=list(in_specs) + [ANY] * n_ci, out_specs=list(out_specs) + [ANY] * n_co,
        out_shape=list(out_shape) + carried.out_shapes,
        input_output_aliases={n_in + i: n_out + o for i, o in carried.aliases.items()},
        scratch_shapes=[pltpu.SemaphoreType.DMA((carried.n_sems,))], compiler_params=_params(semantics))(*args, *carried.operands)
    _set_results(carried, results[n_out:])
    return results[:n_out]


def _resident(shape):
    nd = len(shape)
    return pl.BlockSpec(shape, lambda *_: (0,) * nd, pipeline_mode=pl.Buffered(1))


def _const(shape):
    nd = len(shape)
    return pl.BlockSpec(shape, lambda *_: (0,) * nd)


def _mm(a, b):
    return jnp.dot(a, b, preferred_element_type=F32)


def _mm_nt(a, b):
    return lax.dot_general(a, b, (((1,), (1,)), ((), ())), preferred_element_type=F32)


def _mm_tn(a, b):
    return lax.dot_general(a, b, (((0,), (0,)), ((), ())), preferred_element_type=F32)


def _bf(a):
    return a.astype(BF16)


def _sigmoid(a):
    return 1.0 / (1.0 + jnp.exp(-a))


def _rms(x, gain, n=None):
    n = x.shape[-1] if n is None else n
    r = lax.rsqrt(jnp.sum(x * x, axis=-1, keepdims=True) * (1.0 / n) + NORM_EPS)
    return (x * r) * gain, r


def _rms_bwd(x, r, gain, dh, n=None):
    n = x.shape[-1] if n is None else n
    u = dh * gain
    dx = r * u - x * ((r * r * r) * (jnp.sum(u * x, axis=-1, keepdims=True) * (1.0 / n)))
    dgain = jnp.sum(dh * (x * r), axis=0, keepdims=True)
    return dx, dgain


ROPE_HALF = QK_ROPE // 2


def _rope(t, rope):
    cos, s_lo, s_hi = rope
    return t * cos + pltpu.roll(t, HEAD_PAD - ROPE_HALF, 1) * s_lo + pltpu.roll(t, ROPE_HALF, 1) * s_hi


def _rope_bwd(dt, rope):
    cos, s_lo, s_hi = rope
    return dt * cos + pltpu.roll(dt * s_lo, ROPE_HALF, 1) + pltpu.roll(dt * s_hi, HEAD_PAD - ROPE_HALF, 1)


def _shift_down(u, prev8, k):
    s = pltpu.roll(u, k, 0)
    p = pltpu.roll(prev8, k, 0)
    row = lax.broadcasted_iota(jnp.int32, prev8.shape, 0)
    top = jnp.where(row < k, p, s[:8])
    return jnp.concatenate([top, s[8:]], axis=0)


def _shift_up(d, next8, k):
    tm = d.shape[0]
    s = pltpu.roll(d, tm - k, 0)
    n = pltpu.roll(next8, 8 - k, 0)
    row = lax.broadcasted_iota(jnp.int32, next8.shape, 0)
    bot = jnp.where(row >= 8 - k, n, s[tm - 8:])
    return jnp.concatenate([s[:tm - 8], bot], axis=0)


def _ffn_fwd(x, gain, wg, wu, wd, target, name, carried=None):
    t, d = x.shape
    nb, f, _ = wg.shape
    tm = TOKEN_TILE
    with_loss = target is not None

    def body(*refs):
        if with_loss:
            x_ref, g_ref, wg_ref, wu_ref, wd_ref, t_ref, out_ref, gate_ref, up_ref, act_ref, loss_ref = refs
        else:
            x_ref, g_ref, wg_ref, wu_ref, wd_ref, out_ref, gate_ref, up_ref, act_ref = refs
        xv = x_ref[...]
        h, _ = _rms(xv, g_ref[...])
        hb = _bf(h)
        y = jnp.zeros((tm, d), F32)
        nxt = (_mm_nt(hb, wg_ref[0]), _mm_nt(hb, wu_ref[0]))
        for j in range(nb):
            gate, up = nxt
            if j + 1 < nb:
                nxt = (_mm_nt(hb, wg_ref[j + 1]), _mm_nt(hb, wu_ref[j + 1]))
            act = _bf((gate * _sigmoid(gate)) * up)
            y = y + _mm(act, wd_ref[j])
            gate_ref[j] = _bf(gate)
            up_ref[j] = _bf(up)
            act_ref[j] = act
        out = xv + 0.5 * y
        if with_loss:
            err = out - t_ref[...]
            out_ref[...] = err * (1.0 / d)

            @pl.when(pl.program_id(0) == 0)
            def _():
                loss_ref[...] = jnp.zeros_like(loss_ref)

            part = jnp.sum(jnp.sum(err * err, axis=1, keepdims=True), axis=0, keepdims=True)
            loss_ref[...] += jnp.broadcast_to(part * (0.5 / d), loss_ref.shape)
        else:
            out_ref[...] = out

    tok = pl.BlockSpec((tm, d), lambda i: (i, 0))
    blk = pl.BlockSpec((nb, tm, f), lambda i: (0, i, 0))
    in_specs = [tok, _const((1, d)), _resident(wg.shape), _resident(wu.shape), _resident(wd.shape)]
    args = [x, gain, wg, wu, wd]
    out_shape = [jax.ShapeDtypeStruct((t, d), F32)] + [jax.ShapeDtypeStruct((nb, t, f), BF16)] * 3
    out_specs = [tok, blk, blk, blk]
    if with_loss:
        in_specs.append(tok)
        args.append(target)
        out_shape.append(jax.ShapeDtypeStruct((1, 128), F32))
        out_specs.append(_const((1, 128)))
    return _pallas(body, name, (t // tm,), in_specs, out_specs, out_shape, args, ("arbitrary",), carried)


def _ffn_bwd_x(x, gain, dout, gate, up, wg, wu, wd, name, carried=None):
    t, d = x.shape
    nb, f, _ = wg.shape
    tm = TOKEN_TILE

    def body(x_ref, g_ref, dout_ref, gate_ref, up_ref, wg_ref, wu_ref, wd_ref,
             dx_ref, dgain_ref, hb_ref, dgate_ref, dup_ref, dyb_ref):
        xv = x_ref[...]
        gain_v = g_ref[...]
        h, r = _rms(xv, gain_v)
        hb_ref[...] = _bf(h)
        dout_v = dout_ref[...]
        dyb = _bf(0.5 * dout_v)
        dyb_ref[...] = dyb
        dh = jnp.zeros((tm, d), F32)
        nxt = _mm_nt(dyb, wd_ref[0])
        for j in range(nb):
            dact = nxt
            if j + 1 < nb:
                nxt = _mm_nt(dyb, wd_ref[j + 1])
            gt = gate_ref[j].astype(F32)
            uv = up_ref[j].astype(F32)
            s = _sigmoid(gt)
            dup = _bf(dact * (gt * s))
            dgate = _bf((dact * uv) * (s * (1.0 + gt * (1.0 - s))))
            dh = dh + _mm(dgate, wg_ref[j]) + _mm(dup, wu_ref[j])
            dgate_ref[j] = dgate
            dup_ref[j] = dup
        dxn, dgain = _rms_bwd(xv, r, gain_v, dh)
        dx_ref[...] = dout_v + dxn

        @pl.when(pl.program_id(0) == 0)
        def _():
            dgain_ref[...] = jnp.zeros_like(dgain_ref)

        dgain_ref[...] += dgain

    tok = pl.BlockSpec((tm, d), lambda i: (i, 0))
    blk = pl.BlockSpec((nb, tm, f), lambda i: (0, i, 0))
    return _pallas(
        body, name, (t // tm,),
        [tok, _const((1, d)), tok, blk, blk, _resident(wg.shape), _resident(wu.shape), _resident(wd.shape)],
        [tok, _const((1, d)), tok, blk, blk, tok],
        [jax.ShapeDtypeStruct((t, d), F32), jax.ShapeDtypeStruct((1, d), F32), jax.ShapeDtypeStruct((t, d), BF16),
         jax.ShapeDtypeStruct((nb, t, f), BF16), jax.ShapeDtypeStruct((nb, t, f), BF16), jax.ShapeDtypeStruct((t, d), BF16)],
        (x, gain, dout, gate, up, wg, wu, wd), ("arbitrary",), carried)


def _tn_matmul(a, b, name, split_k=1, carried=None):
    t = a.shape[-2]
    k = a.shape[-1]
    n = b.shape[-1]
    tt = min(TN_TILE, t)
    nt = t // tt

    def body(a_ref, b_ref, o_ref):
        @pl.when(pl.program_id(1) == 0)
        def _():
            o_ref[...] = jnp.zeros_like(o_ref)

        o_ref[...] += _mm_tn(a_ref[...], b_ref[...])

    if split_k > 1:
        assert a.ndim == 2 and b.ndim == 2 and k % (split_k * 128) == 0
        tk = k // split_k
        g = split_k
        a_spec = pl.BlockSpec((tt, tk), lambda gi, ti: (ti, gi))
        b_spec = pl.BlockSpec((tt, n), lambda gi, ti: (ti, 0))
        o_spec = pl.BlockSpec((tk, n), lambda gi, ti: (gi, 0))
        out_shape = jax.ShapeDtypeStruct((k, n), F32)
    else:
        g = a.shape[0] if a.ndim == 3 else b.shape[0]
        a_spec = (pl.BlockSpec((None, tt, k), lambda gi, ti: (gi, ti, 0)) if a.ndim == 3
                  else pl.BlockSpec((tt, k), lambda gi, ti: (ti, 0)))
        b_spec = (pl.BlockSpec((None, tt, n), lambda gi, ti: (gi, ti, 0)) if b.ndim == 3
                  else pl.BlockSpec((tt, n), lambda gi, ti: (ti, 0)))
        o_spec = pl.BlockSpec((None, k, n), lambda gi, ti: (gi, 0, 0))
        out_shape = jax.ShapeDtypeStruct((g, k, n), F32)
    return _pallas(body, name, (g, nt), [a_spec, b_spec], [o_spec], [out_shape], (a, b), ("arbitrary", "arbitrary"), carried)[0]


ROW_QKV, ROW_KR, ROW_XC = 0, Q_LORA + KV_LORA, Q_LORA + KV_LORA + QK_ROPE
ROW_GB, ROW_GC, ROW_GL = ROW_XC + D_MODEL, ROW_XC + 2 * D_MODEL, ROW_XC + 3 * D_MODEL
BIG_FROM_ROWS = ((0, ROW_GB, D_MODEL), (D_MODEL, ROW_GL, 2 * D_MODEL), (3 * D_MODEL, ROW_XC, D_MODEL), (4 * D_MODEL, ROW_GC, D_MODEL))


def _inproj_fwd(x1, gain, w_in, w_kr, qa_gain, kva_gain, qh_gain, kh_gain, w_uq, w_uk, w_uv, w_uvt, rope, carried=None):
    t, d = x1.shape
    tm = TOKEN_TILE
    chunk = 512
    chunks = []
    for col, row, size in BIG_FROM_ROWS:
        chunks += [(col + o, row + o, chunk) for o in range(0, size, chunk)]
    of_head = [[c for k, c in enumerate(chunks) if k * N_HEADS // len(chunks) == hd] for hd in range(N_HEADS)]

    def body(x_ref, g_ref, win_ref, wkr_ref, qa_ref, kva_ref, qh_ref, kh_ref, wuq_ref, wuk_ref, wuv_ref, wuvt_ref, cos_ref, slo_ref,
             shi_ref, hb_ref, big_ref, lat_ref, q_ref, k_ref, v_ref, vt_ref):
        h, _ = _rms(x_ref[...], g_ref[...])
        hb = _bf(h)
        hb_ref[...] = hb
        k_rope = _mm_nt(hb, wkr_ref[...])
        lat = jnp.concatenate([_mm_nt(hb, win_ref[ROW_QKV:ROW_KR, :]), k_rope], axis=1)
        lat_ref[...] = lat
        cq, _ = _rms(lat[:, :Q_LORA], qa_ref[...])
        ckv, _ = _rms(lat[:, Q_LORA:Q_LORA + KV_LORA], kva_ref[...])
        cqb = _bf(cq)
        ckvb = _bf(ckv)
        rope_v = (cos_ref[...], slo_ref[...], shi_ref[...])
        q_all = _mm(cqb, wuq_ref[...])
        k_all = _mm(ckvb, wuk_ref[...])
        v_ref[...] = _bf(_mm(ckvb, wuv_ref[...]))
        vt_all = _mm_nt(wuvt_ref[...], ckvb)
        for hd in range(N_HEADS):
            for col, row, size in of_head[hd]:
                big_ref[:, col:col + size] = _mm_nt(hb, win_ref[row:row + size, :])
            lanes = slice(hd * HEAD_PAD, (hd + 1) * HEAD_PAD)
            qn, _ = _rms(q_all[:, lanes], qh_ref[...], QK_DIM)
            q_ref[hd] = _bf(_rope(qn, rope_v))
            kn, _ = _rms(k_all[:, lanes] + k_rope, kh_ref[...], QK_DIM)
            k_ref[hd] = _bf(_rope(kn, rope_v))
            vt_ref[hd] = _bf(vt_all[hd * V_DIM:(hd + 1) * V_DIM])

    tok = lambda c: pl.BlockSpec((tm, c), lambda i: (i, 0))
    head = lambda c: pl.BlockSpec((N_HEADS, tm, c), lambda i: (0, i, 0))
    return _pallas(
        body, "inproj_fwd", (t // tm,),
        [tok(d), _const((1, d)), _resident(w_in.shape), _resident(w_kr.shape), _const((1, Q_LORA)), _const((1, KV_LORA)),
         _const((1, HEAD_PAD)), _const((1, HEAD_PAD)), _resident(w_uq.shape), _resident(w_uk.shape),
         _resident(w_uv.shape), _resident(w_uvt.shape), tok(HEAD_PAD), tok(HEAD_PAD), tok(HEAD_PAD)],
        [tok(d), tok(BIG_COLS), tok(LAT_COLS), head(HEAD_PAD), head(HEAD_PAD), tok(N_HEADS * V_DIM),
         pl.BlockSpec((N_HEADS, V_DIM, tm), lambda i: (0, 0, i))],
        [jax.ShapeDtypeStruct((t, d), BF16), jax.ShapeDtypeStruct((t, BIG_COLS), F32),
         jax.ShapeDtypeStruct((t, LAT_COLS), F32), jax.ShapeDtypeStruct((N_HEADS, t, HEAD_PAD), BF16),
         jax.ShapeDtypeStruct((N_HEADS, t, HEAD_PAD), BF16), jax.ShapeDtypeStruct((t, N_HEADS * V_DIM), BF16),
         jax.ShapeDtypeStruct((N_HEADS, V_DIM, t), BF16)],
        (x1, gain, w_in, w_kr, qa_gain, kva_gain, qh_gain, kh_gain, w_uq, w_uk, w_uv, w_uvt, *rope), ("arbitrary",), carried)


EXP2_SCALE = ATTN_SCALE * 1.4426950408889634


def _diagonal_keep(tk, tq):
    return lax.broadcasted_iota(jnp.int32, (tk, tq), 0) <= lax.broadcasted_iota(jnp.int32, (tk, tq), 1)


def _attn_fwd(q, k, vt, seq, carried=None):
    _, t, _ = q.shape
    nseq = t // seq
    tq = tk = ATTN_TILE
    nq = seq // tq

    def body(q_ref, k_ref, vt_ref, o_ref, lse_ref):
        i = pl.program_id(1)
        qs = [q_ref[h] for h in range(N_HEADS)]
        keep = _diagonal_keep(tk, tq)

        def scores(h, k0):
            return _mm_nt(k_ref[h, pl.ds(k0, tk), :], qs[h])

        def update(h, st, state, k0, diagonal):
            m, l, acc = state
            if diagonal:
                st = jnp.where(keep, st, MASK_VALUE)
            m_new = jnp.maximum(m, jnp.max(st, axis=0, keepdims=True))
            pt = jnp.exp2((st - m_new) * EXP2_SCALE)
            alpha = jnp.exp2((m - m_new) * EXP2_SCALE)
            l_new = alpha * l + jnp.sum(pt, axis=0, keepdims=True)
            return m_new, l_new, alpha * acc + _mm(vt_ref[h, :, pl.ds(k0, tk)], _bf(pt))

        def tiles(states, k0, diagonal):
            st, new = scores(0, k0), []
            for h in range(N_HEADS):
                st_next = scores(h + 1, k0) if h + 1 < N_HEADS else None
                new.append(update(h, st, states[h], k0, diagonal))
                st = st_next
            return tuple(new)

        init = tuple((jnp.full((1, tq), MASK_VALUE, F32), jnp.zeros((1, tq), F32), jnp.zeros((V_DIM, tq), F32))
                     for _ in range(N_HEADS))
        states = lax.fori_loop(0, i, lambda j, s: tiles(s, pl.multiple_of(j * tk, tk), False), init)
        states = tiles(states, pl.multiple_of(i * tk, tk), True)
        outs = []
        for h in range(N_HEADS):
            m, l, acc = states[h]
            outs.append((acc / l).T)
            lse_ref[h] = m * EXP2_SCALE + jnp.log2(l)
        o_ref[...] = _bf(jnp.concatenate(outs, axis=-1))

    return _pallas(
        body, "attn_fwd", (nseq, nq),
        [pl.BlockSpec((N_HEADS, tq, HEAD_PAD), lambda b, i: (0, b * nq + i, 0)),
         pl.BlockSpec((N_HEADS, seq, HEAD_PAD), lambda b, i: (0, b, 0)),
         pl.BlockSpec((N_HEADS, V_DIM, seq), lambda b, i: (0, 0, b))],
        [pl.BlockSpec((tq, N_HEADS * V_DIM), lambda b, i: (b * nq + i, 0)),
         pl.BlockSpec((N_HEADS, 1, tq), lambda b, i: (0, 0, b * nq + i))],
        [jax.ShapeDtypeStruct((t, N_HEADS * V_DIM), BF16), jax.ShapeDtypeStruct((N_HEADS, 1, t), F32)],
        (q, k, vt), ("arbitrary", "arbitrary"), carried)


ATTN_BWD_HEADS = 4


def _attn_bwd(q, k, v, do, lse, delta, seq, carried=None):
    _, t, _ = q.shape
    nseq = t // seq
    tq = tk = ATTN_TILE
    n = seq // tq
    hb = ATTN_BWD_HEADS

    def body(q_ref, k_ref, v_ref, do_ref, lse_ref, delta_ref, dq_ref, dk_ref, dv_ref):
        dq_ref[...] = jnp.zeros_like(dq_ref)
        dk_ref[...] = jnp.zeros_like(dk_ref)
        dv_ref[...] = jnp.zeros_like(dv_ref)
        keep = _diagonal_keep(tk, tq)

        def tile(h, k0, q0, diagonal):
            kj = k_ref[h, pl.ds(k0, tk), :]
            qi = q_ref[h, pl.ds(q0, tq), :]
            doi = _bf(do_ref[pl.ds(q0, tq), h * V_DIM:(h + 1) * V_DIM])
            st = _mm_nt(kj, qi)
            if diagonal:
                st = jnp.where(keep, st, MASK_VALUE)
            pt = jnp.exp2(st * EXP2_SCALE - lse_ref[h, :, pl.ds(q0, tq)])
            dv_ref[pl.ds(k0, tk), h * V_DIM:(h + 1) * V_DIM] += _mm(_bf(pt), doi)
            dpt = _mm_nt(v_ref[pl.ds(k0, tk), h * V_DIM:(h + 1) * V_DIM], doi)
            dst = _bf((pt * (dpt - delta_ref[pl.ds(h, 1), pl.ds(q0, tq)])) * ATTN_SCALE)
            dk_ref[h, pl.ds(k0, tk), :] += _mm(dst, qi)
            dq_ref[h, pl.ds(q0, tq), :] += _mm_tn(dst, kj)

        def kv_step(j, _):
            k0 = pl.multiple_of(j * tk, tk)
            for h in range(hb):
                tile(h, k0, k0, True)

            def q_step(i, _):
                q0 = pl.multiple_of(i * tq, tq)
                for h in range(hb):
                    tile(h, k0, q0, False)
                return 0

            lax.fori_loop(j + 1, n, q_step, 0)
            return 0

        lax.fori_loop(0, n, kv_step, 0)

    hspec = lambda c: pl.BlockSpec((hb, seq, c), lambda b, g: (g, b, 0))
    cols = pl.BlockSpec((seq, hb * V_DIM), lambda b, g: (b, g))
    return _pallas(
        body, "attn_bwd", (nseq, N_HEADS // hb),
        [hspec(HEAD_PAD), hspec(HEAD_PAD), cols, cols,
         pl.BlockSpec((hb, 1, seq), lambda b, g: (g, 0, b)), pl.BlockSpec((None, hb, seq), lambda b, g: (g, 0, b))],
        [hspec(HEAD_PAD), hspec(HEAD_PAD), cols],
        [jax.ShapeDtypeStruct((N_HEADS, t, HEAD_PAD), F32), jax.ShapeDtypeStruct((N_HEADS, t, HEAD_PAD), F32),
         jax.ShapeDtypeStruct((t, N_HEADS * V_DIM), F32)],
        (q, k, v, do, lse, delta), ("arbitrary", "arbitrary"), carried)


def _merged_mixers(o_ref, gb_ref, gla_ref, glb_ref, xc_ref, gc_ref, xcp_ref, gcp_ref, bias_ref, cw_ref, wpa_ref, wpc_ref,
                   first_of_seq):
    y_a = _mm(o_ref[...], wpa_ref[...])
    gb = gb_ref[...]
    u = gc_ref[...] * xc_ref[...]
    u_prev = jnp.where(first_of_seq, 0.0, gcp_ref[...] * xcp_ref[...])
    cw = cw_ref[...]
    z = cw[2:3] * u + cw[1:2] * _shift_down(u, u_prev, 1) + cw[0:1] * _shift_down(u, u_prev, 2)
    gbz = _bf(gb * z)
    y_b = _mm(gbz, wpc_ref[...])
    bias = bias_ref[...]
    gate_a = _sigmoid(gla_ref[...] + bias[:, :D_MODEL])
    gate_b = _sigmoid(glb_ref[...] + bias[:, D_MODEL:])
    return _bf(gate_a * y_a + gate_b * y_b)


def _mixer_specs(tm, seq):
    d = D_MODEL
    tok = pl.BlockSpec((tm, d), lambda i: (i, 0))
    col = lambda c: pl.BlockSpec((tm, d), lambda i: (i, c))
    prev = lambda c: pl.BlockSpec((8, d), lambda i: (jnp.maximum(i * (tm // 8) - 1, 0), c))
    o_spec = pl.BlockSpec((tm, N_HEADS * V_DIM), lambda i: (i, 0))
    fwd_specs = [o_spec, col(0), col(1), col(2), col(3), col(4), prev(3), prev(4), _const((1, 2 * d)), _const((3, d)),
                 _resident((N_HEADS * V_DIM, d)), _resident((d, d)), _resident((d, d))]
    return tok, fwd_specs


def _mix_fwd(x1, o, big, gate_bias, conv_w, w_pa, w_pc, w_out, seq, carried=None):
    t, d = x1.shape
    tm = TOKEN_TILE
    tiles_per_seq = seq // tm

    def body(x_ref, o_ref, gb_ref, gla_ref, glb_ref, xc_ref, gc_ref, xcp_ref, gcp_ref, bias_ref, cw_ref, wpa_ref, wpc_ref,
             wout_ref, x2_ref):
        first = pl.program_id(0) % tiles_per_seq == 0
        merged = _merged_mixers(o_ref, gb_ref, gla_ref, glb_ref, xc_ref, gc_ref, xcp_ref, gcp_ref, bias_ref, cw_ref, wpa_ref,
                                wpc_ref, first)
        x2_ref[...] = x_ref[...] + _mm(merged, wout_ref[...])

    tok, fwd_specs = _mixer_specs(tm, seq)
    return _pallas(body, "mix_fwd", (t // tm,), [tok] + fwd_specs, [tok], [jax.ShapeDtypeStruct((t, d), F32)],
                   (x1, o, big, big, big, big, big, big, big, gate_bias, conv_w, w_pa, w_pc, w_out), ("arbitrary",), carried)[0]


def _mix_bwd(dx2, o, big, gate_bias, conv_w, w_pa, w_pc, w_out, seq, carried=None):
    t, d = dx2.shape
    tm = TOKEN_TILE
    tiles_per_seq = seq // tm
    hv = N_HEADS * V_DIM

    def body(dx_ref, o_ref, gb_ref, gla_ref, glb_ref, xc_ref, gc_ref, xcp_ref, gcp_ref, bias_ref, cw_ref, wpa_ref, wpc_ref,
             wout_ref, do_ref, delta_ref, dz_ref, dm_ref, dbias_ref, dwpa_ref, dwpc_ref, dwout_ref):
        @pl.when(pl.program_id(0) == 0)
        def _():
            dbias_ref[...] = jnp.zeros_like(dbias_ref)
            dwpa_ref[...] = jnp.zeros_like(dwpa_ref)
            dwpc_ref[...] = jnp.zeros_like(dwpc_ref)
            dwout_ref[...] = jnp.zeros_like(dwout_ref)

        first = pl.program_id(0) % tiles_per_seq == 0
        dxb = _bf(dx_ref[...])
        dmerged = _mm_nt(dxb, wout_ref[...])
        y_a = _mm(o_ref[...], wpa_ref[...])
        bias = bias_ref[...]
        gate_a = _sigmoid(gla_ref[...] + bias[:, :d])
        gate_b = _sigmoid(glb_ref[...] + bias[:, d:])
        dya = _bf(dmerged * gate_a)
        dyb = _bf(dmerged * gate_b)
        do_v = _mm_nt(dya, wpa_ref[...])
        dgz = _mm_nt(dyb, wpc_ref[...])
        dwpa_ref[...] += _mm_tn(o_ref[...], dya)
        gb = gb_ref[...]
        u = gc_ref[...] * xc_ref[...]
        u_prev = jnp.where(first, 0.0, gcp_ref[...] * xcp_ref[...])
        cw = cw_ref[...]
        z = cw[2:3] * u + cw[1:2] * _shift_down(u, u_prev, 1) + cw[0:1] * _shift_down(u, u_prev, 2)
        gbz = _bf(gb * z)
        y_b = _mm(gbz, wpc_ref[...])
        dwpc_ref[...] += _mm_tn(gbz, dyb)
        do_ref[...] = do_v
        head = lax.broadcasted_iota(jnp.int32, (N_HEADS, hv), 0) * V_DIM
        col = lax.broadcasted_iota(jnp.int32, (N_HEADS, hv), 1)
        in_head = ((col >= head) & (col < head + V_DIM)).astype(F32)
        delta_ref[...] = lax.dot_general(in_head, do_v * o_ref[...].astype(F32), (((1,), (1,)), ((), ())),
                                         precision=lax.Precision.HIGHEST, preferred_element_type=F32)
        dz_ref[...] = dgz * gb
        dm_ref[:, :d] = _bf(dgz * z)
        merged = _bf(gate_a * y_a + gate_b * y_b)
        dwout_ref[...] += _mm_tn(merged, dxb)
        dla = (dmerged * y_a) * (gate_a * (1.0 - gate_a))
        dlb = (dmerged * y_b) * (gate_b * (1.0 - gate_b))
        dbias_ref[:, :d] += jnp.sum(dla, axis=0, keepdims=True)
        dbias_ref[:, d:] += jnp.sum(dlb, axis=0, keepdims=True)
        dm_ref[:, d:2 * d] = _bf(dla)
        dm_ref[:, 2 * d:] = _bf(dlb)

    tok, fwd_specs = _mixer_specs(tm, seq)
    return _pallas(
        body, "mix_bwd", (t // tm,), [tok] + fwd_specs,
        [pl.BlockSpec((tm, hv), lambda i: (i, 0)), pl.BlockSpec((N_HEADS, tm), lambda i: (0, i)), tok,
         pl.BlockSpec((tm, M_COLS), lambda i: (i, 0)), _const((1, 2 * d)), _const((hv, d)), _const((d, d)), _const((d, d))],
        [jax.ShapeDtypeStruct((t, hv), F32), jax.ShapeDtypeStruct((N_HEADS, t), F32), jax.ShapeDtypeStruct((t, d), F32),
         jax.ShapeDtypeStruct((t, M_COLS), BF16), jax.ShapeDtypeStruct((1, 2 * d), F32), jax.ShapeDtypeStruct((hv, d), F32),
         jax.ShapeDtypeStruct((d, d), F32), jax.ShapeDtypeStruct((d, d), F32)],
        (dx2, o, big, big, big, big, big, big, big, gate_bias, conv_w, w_pa, w_pc, w_out), ("arbitrary",), carried)


def _prep_bwd(lat, big, dz, dq, dk, dv, qa_gain, kva_gain, qh_gain, kh_gain, w_uq, w_uk, w_uv, rope, conv_w, seq, carried=None):
    t = lat.shape[0]
    d = D_MODEL
    tm = PREP_TILE
    tiles_per_seq = seq // tm
    last_blk = t // 8 - 1

    def body(lat_ref, xc_ref, gc_ref, dz_ref, dzn_ref, dq_ref, dk_ref, dv_ref, qa_ref, kva_ref, qh_ref, kh_ref, wuq_ref, wuk_ref,
             wuv_ref, cos_ref, slo_ref, shi_ref, cw_ref,
             dp_ref, dwuq_ref, dwuk_ref, dwuv_ref, dqa_ref, dkva_ref, dqh_ref, dkh_ref, dcw_ref):
        pid = pl.program_id(0)

        @pl.when(pid == 0)
        def _():
            for r in (dwuq_ref, dwuk_ref, dwuv_ref, dqa_ref, dkva_ref, dqh_ref, dkh_ref, dcw_ref):
                r[...] = jnp.zeros_like(r)

        lat_v = lat_ref[...]
        q_lat = lat_v[:, :Q_LORA]
        kv_lat = lat_v[:, Q_LORA:Q_LORA + KV_LORA]
        k_rope = lat_v[:, Q_LORA + KV_LORA:]
        qa_gain_v = qa_ref[...]
        kva_gain_v = kva_ref[...]
        qh_gain_v = qh_ref[...]
        kh_gain_v = kh_ref[...]
        cq, rq = _rms(q_lat, qa_gain_v)
        ckv, rkv = _rms(kv_lat, kva_gain_v)
        cqb = _bf(cq)
        ckvb = _bf(ckv)
        rope_v = (cos_ref[...], slo_ref[...], shi_ref[...])
        lane = lax.broadcasted_iota(jnp.int32, (tm, HEAD_PAD), 1)
        rope_lanes = (lane >= QK_NOPE) & (lane < QK_DIM)
        dk_rope = jnp.zeros((tm, HEAD_PAD), F32)
        dqh_gain = jnp.zeros((1, HEAD_PAD), F32)
        dkh_gain = jnp.zeros((1, HEAD_PAD), F32)
        q_all = _mm(cqb, wuq_ref[...])
        k_all = _mm(ckvb, wuk_ref[...])
        dvb = _bf(dv_ref[...])
        dckv = _mm_nt(dvb, wuv_ref[...])
        dwuv_ref[...] += _mm_tn(ckvb, dvb)

        last = pid % tiles_per_seq == tiles_per_seq - 1
        dzv = dz_ref[...]
        dz_next = jnp.where(last, 0.0, dzn_ref[...])
        dz1 = _shift_up(dzv, dz_next, 1)
        dz2 = _shift_up(dzv, dz_next, 2)
        cw = cw_ref[...]
        xc = xc_ref[...]
        gc = gc_ref[...]
        u = gc * xc
        du = cw[2:3] * dzv + cw[1:2] * dz1 + cw[0:1] * dz2
        dp_ref[:, :d] = _bf(du * gc)
        dp_ref[:, d:2 * d] = _bf(du * xc)
        dcw_ref[0:1, :] += jnp.sum(dz2 * u, axis=0, keepdims=True)
        dcw_ref[1:2, :] += jnp.sum(dz1 * u, axis=0, keepdims=True)
        dcw_ref[2:3, :] += jnp.sum(dzv * u, axis=0, keepdims=True)

        dcq = jnp.zeros((tm, Q_LORA), F32)
        half = N_HEADS // 2
        for part in range(2):
            dq_heads, dk_heads = [], []
            for hd in range(part * half, (part + 1) * half):
                lanes = slice(hd * HEAD_PAD, (hd + 1) * HEAD_PAD)
                q_pre = q_all[:, lanes]
                _, rr = _rms(q_pre, qh_gain_v, QK_DIM)
                dq_pre, dg = _rms_bwd(q_pre, rr, qh_gain_v, _rope_bwd(dq_ref[hd], rope_v), QK_DIM)
                dqh_gain = dqh_gain + dg
                dq_heads.append(_bf(dq_pre))

                k_pre = k_all[:, lanes] + k_rope
                _, rr = _rms(k_pre, kh_gain_v, QK_DIM)
                dk_pre, dg = _rms_bwd(k_pre, rr, kh_gain_v, _rope_bwd(dk_ref[hd], rope_v), QK_DIM)
                dkh_gain = dkh_gain + dg
                dk_rope = dk_rope + jnp.where(rope_lanes, dk_pre, 0.0)
                dk_heads.append(_bf(dk_pre))
            dq_part = jnp.concatenate(dq_heads, axis=1)
            dk_part = jnp.concatenate(dk_heads, axis=1)
            cols = slice(part * half * HEAD_PAD, (part + 1) * half * HEAD_PAD)
            dcq = dcq + _mm_nt(dq_part, wuq_ref[:, cols])
            dckv = dckv + _mm_nt(dk_part, wuk_ref[:, cols])
            dwuq_ref[:, cols] += _mm_tn(cqb, dq_part)
            dwuk_ref[:, cols] += _mm_tn(ckvb, dk_part)
        dqh_ref[...] += dqh_gain
        dkh_ref[...] += dkh_gain
        dq_lat, dg = _rms_bwd(q_lat, rq, qa_gain_v, dcq)
        dqa_ref[...] += dg
        dkv_lat, dg = _rms_bwd(kv_lat, rkv, kva_gain_v, dckv)
        dkva_ref[...] += dg
        dp_ref[:, 2 * d:2 * d + Q_LORA] = _bf(dq_lat)
        dp_ref[:, 2 * d + Q_LORA:2 * d + Q_LORA + KV_LORA] = _bf(dkv_lat)
        dp_ref[:, 2 * d + Q_LORA + KV_LORA:] = _bf(dk_rope)

    tok = lambda c: pl.BlockSpec((tm, c), lambda i: (i, 0))
    col = lambda c: pl.BlockSpec((tm, d), lambda i: (i, c))
    head = lambda c: pl.BlockSpec((N_HEADS, tm, c), lambda i: (0, i, 0))
    nxt = pl.BlockSpec((8, d), lambda i: (jnp.minimum((i + 1) * (tm // 8), last_blk), 0))
    return _pallas(
        body, "prep_bwd", (t // tm,),
        [tok(LAT_COLS), col(3), col(4), tok(d), nxt, head(HEAD_PAD), head(HEAD_PAD), tok(N_HEADS * V_DIM),
         _const((1, Q_LORA)), _const((1, KV_LORA)), _const((1, HEAD_PAD)), _const((1, HEAD_PAD)),
         _resident(w_uq.shape), _resident(w_uk.shape), _resident(w_uv.shape), tok(HEAD_PAD), tok(HEAD_PAD), tok(HEAD_PAD),
         _const((3, d))],
        [tok(P_COLS), _const(w_uq.shape), _const(w_uk.shape), _const(w_uv.shape), _const((1, Q_LORA)),
         _const((1, KV_LORA)), _const((1, HEAD_PAD)), _const((1, HEAD_PAD)), _const((3, d))],
        [jax.ShapeDtypeStruct((t, P_COLS), BF16), jax.ShapeDtypeStruct(w_uq.shape, F32),
         jax.ShapeDtypeStruct(w_uk.shape, F32), jax.ShapeDtypeStruct(w_uv.shape, F32),
         jax.ShapeDtypeStruct((1, Q_LORA), F32), jax.ShapeDtypeStruct((1, KV_LORA), F32),
         jax.ShapeDtypeStruct((1, HEAD_PAD), F32), jax.ShapeDtypeStruct((1, HEAD_PAD), F32), jax.ShapeDtypeStruct((3, d), F32)],
        (lat, big, big, dz, dz, dq, dk, dv, qa_gain, kva_gain, qh_gain, kh_gain, w_uq, w_uk, w_uv, *rope, conv_w),
        ("arbitrary",), carried)


def _inproj_bwd(x1, gain, dx2, dm, dp, w_in, w_kr, carried=None):
    t, d = x1.shape
    tm = TOKEN_TILE

    def body(x_ref, g_ref, dx2_ref, dm_ref, dp_ref, win_ref, wkr_ref, dx1_ref, dgain_ref):
        xv = x_ref[...]
        gain_v = g_ref[...]
        _, r = _rms(xv, gain_v)
        dh = (_mm(dm_ref[:, :d], win_ref[ROW_GB:ROW_GC, :]) + _mm(dm_ref[:, d:], win_ref[ROW_GL:, :])
              + _mm(dp_ref[:, :d], win_ref[ROW_XC:ROW_GB, :]) + _mm(dp_ref[:, d:2 * d], win_ref[ROW_GC:ROW_GL, :])
              + _mm(dp_ref[:, 2 * d:2 * d + ROW_KR], win_ref[ROW_QKV:ROW_KR, :]) + _mm(dp_ref[:, 2 * d + ROW_KR:], wkr_ref[...]))
        dxn, dgain = _rms_bwd(xv, r, gain_v, dh)
        dx1_ref[...] = dx2_ref[...] + dxn

        @pl.when(pl.program_id(0) == 0)
        def _():
            dgain_ref[...] = jnp.zeros_like(dgain_ref)

        dgain_ref[...] += dgain

    tok = lambda c: pl.BlockSpec((tm, c), lambda i: (i, 0))
    return _pallas(
        body, "inproj_bwd", (t // tm,),
        [tok(d), _const((1, d)), tok(d), tok(M_COLS), tok(P_COLS), _resident(w_in.shape), _resident(w_kr.shape)],
        [tok(d), _const((1, d))], [jax.ShapeDtypeStruct((t, d), F32), jax.ShapeDtypeStruct((1, d), F32)],
        (x1, gain, dx2, dm, dp, w_in, w_kr), ("arbitrary",), carried)


def _adamw(quads, name, carried=None):
    k = len(quads)
    rows, cols = quads[0][0].shape
    tr, tc = rows, cols
    for cand in (512, 352, 256, 192, 128, 64):
        if rows % cand == 0 and rows > cand:
            tr = cand
            break
    if tr == rows and rows * cols > 512 * 1024 and cols % 256 == 0:
        tc = 256
    while k * 14 * tr * tc * 4 > (VMEM_LIMIT * 3) // 4 and tr % 16 == 0:
        tr //= 2

    def body(*refs):
        for i in range(k):
            w_ref, g_ref, m_ref, v_ref = refs[4 * i:4 * i + 4]
            delta_ref, nm_ref, nv_ref = refs[4 * k + 3 * i:4 * k + 3 * i + 3]
            delta_ref[...], nm_ref[...], nv_ref[...] = _adamw_update(w_ref[...], g_ref[...], m_ref[...], v_ref[...])

    spec = pl.BlockSpec((tr, tc), lambda i, j: (i, j))
    shape = jax.ShapeDtypeStruct((rows, cols), F32)
    outs = _pallas(body, name, (rows // tr, cols // tc), [spec] * (4 * k), [spec] * (3 * k), [shape] * (3 * k),
                   [a for quad in quads for a in quad], ("arbitrary", "arbitrary"), carried)
    return [tuple(outs[3 * i:3 * i + 3]) for i in range(k)]


def _adamw_update(w, g, m, v):
    nm = ADAM_B1 * m + (1.0 - ADAM_B1) * g
    nv = ADAM_B2 * v + (1.0 - ADAM_B2) * (g * g)
    m_hat = nm * (1.0 / (1.0 - ADAM_B1 ** ADAM_STEP))
    v_hat = nv * (1.0 / (1.0 - ADAM_B2 ** ADAM_STEP))
    return -ADAM_LR * (m_hat / (jnp.sqrt(v_hat) + ADAM_EPS) + ADAM_WD * w), nm, nv


def _adamw_whole(quads, name):
    k = len(quads)

    def body(*refs):
        for i in range(k):
            w_ref, g_ref, m_ref, v_ref = refs[4 * i:4 * i + 4]
            g_out, delta_ref, nm_ref, nv_ref = refs[4 * k + 4 * i:4 * k + 4 * i + 4]
            gv = g_ref[...]
            g_out[...] = gv
            delta_ref[...], nm_ref[...], nv_ref[...] = _adamw_update(w_ref[...], gv, m_ref[...], v_ref[...])

    vm = pl.BlockSpec(memory_space=pltpu.VMEM)
    outs = pl.pallas_call(body, name=name, in_specs=[vm] * (4 * k), out_specs=[vm] * (4 * k),
                          out_shape=[jax.ShapeDtypeStruct(q[0].shape, F32) for q in quads for _ in range(4)],
                          compiler_params=_params())(*[a for quad in quads for a in quad])
    return [tuple(outs[4 * i:4 * i + 4]) for i in range(k)]


def _adamw_small(packed_grads, triples, segments):
    k = len(triples)

    def body(*refs):
        g_ref = refs[0]
        off = 0
        for i in range(k):
            w_ref, m_ref, v_ref = refs[1 + 3 * i:4 + 3 * i]
            g_out, delta_ref, nm_ref, nv_ref = refs[1 + 3 * k + 4 * i:5 + 3 * k + 4 * i]
            gv = g_ref[:, off:off + w_ref.shape[1]]
            g_out[...] = gv
            delta_ref[...], nm_ref[...], nv_ref[...] = _adamw_update(w_ref[...], gv, m_ref[...], v_ref[...])
            off += segments[i]

    vm = pl.BlockSpec(memory_space=pltpu.VMEM)
    outs = pl.pallas_call(
        body, name="adamw_small", in_specs=[vm] * (1 + 3 * k), out_specs=[vm] * (4 * k),
        out_shape=[jax.ShapeDtypeStruct(w.shape, F32) for w, _, _ in triples for _ in range(4)],
    )(packed_grads, *[a for triple in triples for a in triple])
    return [tuple(outs[4 * i:4 * i + 4]) for i in range(k)]


def _place():
    x, y, c = lax.axis_index("x"), lax.axis_index("y"), lax.axis_index("c")
    other_chips = [(1 - x, y), (x, 1 - y), (1 - x, 1 - y)]
    return x, y, c, other_chips


def _remote(src, dst, sems, send, recv, device):
    return pltpu.make_async_remote_copy(src_ref=src, dst_ref=dst, send_sem=sems.at[send], recv_sem=sems.at[recv],
                                        device_id=device, device_id_type=MESH_ID)


def _cast_shards(shards, out_dtypes, n_first):
    n = len(shards)
    out_shape = [jax.ShapeDtypeStruct((N_CHIPS,) + s.shape, dt) for s, dt in zip(shards, out_dtypes)]
    gather = _gather_carried(out_shape[:n_first])

    def body(*refs):
        ins, outs, stage, sems = refs[:n], refs[n:2 * n], refs[2 * n:3 * n], refs[3 * n]
        x, y, _, _ = _place()
        me = 2 * x + y

        def cast(first, last):
            copies = []
            for w in range(first, last):
                stage[w][...] = ins[w][...].astype(out_dtypes[w])
                copies.append(pltpu.make_async_copy(stage[w], outs[w].at[me], sems.at[w]))
                copies[-1].start()
            for cp in copies:
                cp.wait()

        cast(0, n_first)
        gather.start(None, outs[:n_first], sems, n)
        cast(n_first, n)
        gather.finish(None, outs[:n_first], sems, n)

    vm = pl.BlockSpec(memory_space=pltpu.VMEM)
    return pl.pallas_call(
        body, name="cast_shards", in_specs=[vm] * n, out_specs=[ANY] * n, out_shape=out_shape,
        scratch_shapes=[pltpu.VMEM(s.shape, dt) for s, dt in zip(shards, out_dtypes)] + [pltpu.SemaphoreType.DMA((n + gather.n_sems,))],
        compiler_params=_params())(*shards)


BF16_ROWS = 16


def _split_rows(rows):
    return (rows // 2) % BF16_ROWS == 0


def _half_shape(rows, cols):
    return (rows // 2, cols) if _split_rows(rows) else (rows, cols // 2)


def _half(rows, cols, which):
    if _split_rows(rows):
        return (pl.ds(pl.multiple_of(which * (rows // 2), BF16_ROWS), rows // 2), slice(None))
    return (slice(None), pl.ds(pl.multiple_of(which * (cols // 2), 128), cols // 2))


def _gather_carried(bufs):
    n = len(bufs)

    def half(w, slot, which):
        _, rows, cols = bufs[w].shape
        return (slot,) + _half(rows, cols, which)

    def start(ins, outs, sems, base):
        x, y, c, other_chips = _place()
        me = 2 * x + y
        for w in range(n):
            mine = outs[w].at[half(w, me, c)]
            for p, (px, py) in enumerate(other_chips):
                _remote(mine, mine, sems, base + 12 * w + p, base + 12 * w + 3 + p, (px, py, c)).start()

    def finish(ins, outs, sems, base):
        x, y, c, other_chips = _place()
        me = 2 * x + y
        for w in range(n):
            for p, (px, py) in enumerate(other_chips):
                got = outs[w].at[half(w, 2 * px + py, c)]
                _remote(got, got, sems, base + 12 * w + p, base + 12 * w + 3 + p, (px, py, c)).wait_recv()
                _remote(got, got, sems, base + 12 * w + 6 + p, base + 12 * w + 9 + p, (x, y, 1 - c)).start()
        for w in range(n):
            mine = outs[w].at[half(w, me, c)]
            for p, (px, py) in enumerate(other_chips):
                got = outs[w].at[half(w, 2 * px + py, c)]
                theirs = outs[w].at[half(w, 2 * px + py, 1 - c)]
                _remote(got, theirs, sems, base + 12 * w + 6 + p, base + 12 * w + 9 + p, (x, y, 1 - c)).wait()
                _remote(mine, mine, sems, base + 12 * w + p, base + 12 * w + 3 + p, (px, py, c)).wait_send()

    shapes = [jax.ShapeDtypeStruct(b.shape, b.dtype) for b in bufs]
    return _Carried(bufs, shapes, {w: w for w in range(n)}, 12 * n, start, finish)


def _swap_carried(grads):
    n = len(grads)

    def copy(w, ins, outs, sems, base):
        x, y, c, _ = _place()
        _, rows, cols = grads[w].shape
        theirs = ins[w].at[(slice(None),) + _half(rows, cols, 1 - c)]
        return _remote(theirs, outs[w], sems, base + 2 * w, base + 2 * w + 1, (x, y, 1 - c))

    def start(ins, outs, sems, base):
        for w in range(n):
            copy(w, ins, outs, sems, base).start()

    def finish(ins, outs, sems, base):
        for w in range(n):
            copy(w, ins, outs, sems, base).wait()

    shapes = [jax.ShapeDtypeStruct((g.shape[0],) + _half_shape(*g.shape[1:]), F32) for g in grads]
    return _Carried(grads, shapes, {}, 2 * n, start, finish)


def _row_tile(rows):
    for cand in (512, 352, 256, 192, 128, 96, 64, 32, 16):
        if rows % cand == 0:
            return cand
    return rows


def _half_block_index(split_rows, tiles, i, core):
    return (core * tiles + i, 0) if split_rows else (i, core)


def _chip_partial(grad, other, place, name):
    nblk, hr, hc = other.shape
    by_rows = _split_rows(grad.shape[1])
    tr = _row_tile(hr)
    tiles = hr // tr

    def body(place_ref, g_ref, o_ref, sum_ref, sum_bf_ref):
        s = g_ref[...] + o_ref[...]
        sum_ref[...] = s
        sum_bf_ref[...] = _bf(s)

    grid_spec = pltpu.PrefetchScalarGridSpec(
        num_scalar_prefetch=1, grid=(nblk, tiles),
        in_specs=[pl.BlockSpec((None, tr, hc), lambda b, i, place_ref: (b,) + _half_block_index(by_rows, tiles, i, place_ref[1])),
                  pl.BlockSpec((None, tr, hc), lambda b, i, place_ref: (b, i, 0))],
        out_specs=[pl.BlockSpec((None, tr, hc), lambda b, i, place_ref: (b, i, 0))] * 2)
    return pl.pallas_call(body, name=name, grid_spec=grid_spec,
                          out_shape=[jax.ShapeDtypeStruct(other.shape, F32), jax.ShapeDtypeStruct(other.shape, BF16)],
                          compiler_params=_params(("arbitrary", "arbitrary")))(place, grad, other)


def _chip_partial_small(grads, others, place):
    n = len(grads)

    def body(*refs):
        place_ref, g_refs, o_refs = refs[0], refs[1:1 + n], refs[1 + n:1 + 2 * n]
        sum_refs, bf_refs = refs[1 + 2 * n:1 + 3 * n], refs[1 + 3 * n:]
        core = place_ref[1]
        for w in range(n):
            _, rows, cols = grads[w].shape
            s = g_refs[w][(slice(None),) + _half(rows, cols, core)] + o_refs[w][...]
            sum_refs[w][...] = s
            bf_refs[w][...] = _bf(s)

    vm = pl.BlockSpec(memory_space=pltpu.VMEM)
    outs = pl.pallas_call(
        body, name="chip_partial_small", in_specs=[pl.BlockSpec(memory_space=pltpu.SMEM)] + [vm] * (2 * n), out_specs=[vm] * (2 * n),
        out_shape=[jax.ShapeDtypeStruct(o.shape, F32) for o in others] + [jax.ShapeDtypeStruct(o.shape, BF16) for o in others],
        compiler_params=_params())(place, *grads, *others)
    return list(zip(outs[:n], outs[n:]))


def _chip_total_small(owns, receiveds, place, shapes):
    n = len(owns)

    def body(*refs):
        place_ref, own_refs, r_refs, out_refs = refs[0], refs[1:1 + n], refs[1 + n:1 + 2 * n], refs[1 + 2 * n:]
        chip, core = place_ref[0], place_ref[1]
        for w in range(n):
            r = [r_refs[w][(chip + k) % N_CHIPS].astype(F32) for k in (1, 2, 3)]
            out_refs[w][_half(*shapes[w], core)] = own_refs[w][chip] + ((r[0] + r[1]) + r[2])

    vm = pl.BlockSpec(memory_space=pltpu.VMEM)
    return list(pl.pallas_call(
        body, name="chip_total_small", in_specs=[pl.BlockSpec(memory_space=pltpu.SMEM)] + [vm] * (2 * n), out_specs=[vm] * n,
        out_shape=[jax.ShapeDtypeStruct(tuple(s), F32) for s in shapes], compiler_params=_params())(place, *owns, *receiveds))


def _send_carried(partials):
    n = len(partials)

    def start(ins, outs, sems, base):
        x, y, c, other_chips = _place()
        me = 2 * x + y
        for w in range(n):
            for p, (px, py) in enumerate(other_chips):
                _remote(ins[w].at[2 * px + py], outs[w].at[me], sems, base + 6 * w + p, base + 6 * w + 3 + p, (px, py, c)).start()

    def finish(ins, outs, sems, base):
        x, y, c, other_chips = _place()
        for w in range(n):
            for p, (px, py) in enumerate(other_chips):
                _remote(ins[w].at[2 * px + py], outs[w].at[2 * px + py], sems, base + 6 * w + p, base + 6 * w + 3 + p,
                        (px, py, c)).wait()

    return _Carried(partials, [jax.ShapeDtypeStruct(p.shape, BF16) for p in partials], {}, 6 * n, start, finish)


def _chip_total(own, received, place, shape, name):
    nblk, hr, hc = own.shape
    by_rows = _split_rows(shape[0])
    tr = _row_tile(hr)
    tiles = hr // tr

    def body(place_ref, own_ref, r1_ref, r2_ref, r3_ref, out_ref):
        out_ref[...] = own_ref[...] + ((r1_ref[...].astype(F32) + r2_ref[...].astype(F32)) + r3_ref[...].astype(F32))

    def slot(k):
        return pl.BlockSpec((None, tr, hc), lambda i, place_ref: ((place_ref[0] + k) % N_CHIPS, i, 0))

    grid_spec = pltpu.PrefetchScalarGridSpec(
        num_scalar_prefetch=1, grid=(tiles,), in_specs=[slot(0), slot(1), slot(2), slot(3)],
        out_specs=pl.BlockSpec((tr, hc), lambda i, place_ref: _half_block_index(by_rows, tiles, i, place_ref[1])))
    return pl.pallas_call(body, name=name, grid_spec=grid_spec, out_shape=jax.ShapeDtypeStruct(tuple(shape), F32),
                          compiler_params=_params(("arbitrary",)))(place, own, received, received, received)


def _join_carried(totals):
    n = len(totals)

    def copy(w, outs, sems, base):
        x, y, c, _ = _place()
        mine = outs[w].at[_half(*totals[w].shape, c)]
        return _remote(mine, mine, sems, base + 2 * w, base + 2 * w + 1, (x, y, 1 - c))

    def start(ins, outs, sems, base):
        for w in range(n):
            copy(w, outs, sems, base).start()

    def finish(ins, outs, sems, base):
        for w in range(n):
            copy(w, outs, sems, base).wait()

    shapes = [jax.ShapeDtypeStruct(a.shape, F32) for a in totals]
    return _Carried(totals, shapes, {w: w for w in range(n)}, 2 * n, start, finish)


def _sum_devices(vec):
    rows, n = vec.shape

    def body(v_ref, out_ref, buf, send_sems, recv_sems):
        x, y, c, _ = _place()
        me = 4 * x + 2 * y + c
        buf[me] = v_ref[...]
        sends = []
        for k in range(1, N_DEV):
            peer = (1 - x if k & 4 else x, 1 - y if k & 2 else y, 1 - c if k & 1 else c)
            cp = pltpu.make_async_remote_copy(src_ref=v_ref, dst_ref=buf.at[me], send_sem=send_sems.at[k], recv_sem=recv_sems.at[k],
                                              device_id=peer, device_id_type=MESH_ID)
            cp.start()
            sends.append(cp)
        for cp in sends:
            cp.wait()
        total = buf[0]
        for dev in range(1, N_DEV):
            total = total + buf[dev]
        out_ref[...] = total

    vm = pl.BlockSpec(memory_space=pltpu.VMEM)
    return pl.pallas_call(
        body, name="sum_devices", in_specs=[vm], out_specs=vm, out_shape=jax.ShapeDtypeStruct((rows, n), F32),
        scratch_shapes=[pltpu.VMEM((N_DEV, rows, n), F32), pltpu.SemaphoreType.DMA((N_DEV,)), pltpu.SemaphoreType.DMA((N_DEV,))],
    )(vec)


def _rope_tables(positions):
    half = ROPE_HALF
    inv_freq = 1.0 / (ROPE_THETA ** (jnp.arange(half, dtype=F32) / half))
    ang = positions.astype(F32).reshape(-1, 1) * inv_freq
    cos, sin = jnp.cos(ang), jnp.sin(ang)
    t = ang.shape[0]
    ones, zeros = jnp.ones((t, QK_NOPE), F32), jnp.zeros((t, QK_NOPE), F32)
    pad, none = HEAD_PAD - QK_DIM, zeros[:, :half]
    cos_full = jnp.concatenate([ones, cos, cos, ones[:, :pad]], axis=1)
    s_lo = jnp.concatenate([zeros, -sin, none, zeros[:, :pad]], axis=1)
    s_hi = jnp.concatenate([zeros, none, sin, zeros[:, :pad]], axis=1)
    return cos_full, s_lo, s_hi


def _partials(names, grads, from_sibling, place):
    return [_chip_partial(g, o, place, "chip_partial_" + n) for n, g, o in zip(names, grads, from_sibling)]


def _totals(names, grads, partials, received, place):
    return [_chip_total(pf, r, place, g.shape[1:], "chip_total_" + n) for n, g, (pf, _), r in zip(names, grads, partials, received)]


def _kernel_layouts(full):
    w_in = full["w_in"]
    w_kr = jnp.pad(w_in[ROW_KR:ROW_XC], ((QK_NOPE, HEAD_PAD - QK_DIM), (0, 0)))
    w_uq = jnp.pad(full["w_uq"].reshape(Q_LORA, N_HEADS, QK_DIM), ((0, 0), (0, 0), (0, HEAD_PAD - QK_DIM)))
    w_uk = jnp.pad(full["w_uk"].reshape(KV_LORA, N_HEADS, QK_NOPE), ((0, 0), (0, 0), (0, HEAD_PAD - QK_NOPE)))
    return {"w_in": w_in, "w_kr": w_kr, "w_uq": w_uq.reshape(Q_LORA, N_HEADS * HEAD_PAD),
            "w_uk": w_uk.reshape(KV_LORA, N_HEADS * HEAD_PAD), "w_uv": full["w_uv"], "w_uvt": full["w_uv"].T}


def _global_layouts(g):
    d = D_MODEL
    dm, dp = g["w_m"], g["w_p"]
    o_lat = 2 * d
    o_kr = o_lat + Q_LORA + KV_LORA + QK_NOPE
    w_in = jnp.concatenate([dp[o_lat:o_lat + Q_LORA + KV_LORA], dp[o_kr:o_kr + QK_ROPE], dp[:d], dm[:d], dp[d:o_lat], dm[d:]], axis=0)
    w_uq = g["w_uq"].reshape(Q_LORA, N_HEADS, HEAD_PAD)[:, :, :QK_DIM].reshape(Q_LORA, N_HEADS * QK_DIM)
    w_uk = g["w_uk"].reshape(KV_LORA, N_HEADS, HEAD_PAD)[:, :, :QK_NOPE].reshape(KV_LORA, N_HEADS * QK_NOPE)
    return {"w_in": w_in, "w_uq": w_uq, "w_uk": w_uk, "w_uv": g["w_uv"], "w_proj_attn": g["w_pa"], "w_proj_conv": g["w_pc"],
            "w_out": g["w_out"]}


def _col_blocks(a):
    r, c = a.shape
    return a.reshape(r, N_CHIPS, c // N_CHIPS).transpose(1, 0, 2)


def _from_col_blocks(a):
    n, r, c = a.shape
    return a.transpose(1, 0, 2).reshape(r, n * c)


COL_SHARDED = ("w_uq", "w_uk", "w_uv", "w_proj_attn")
TRANSPOSED = ("ffn1_w_gate", "ffn1_w_up", "ffn2_w_gate", "ffn2_w_up", "w_in")
SMALL = (("ffn1_norm", 1024), ("mix_norm", 1024), ("gate_bias", 2048), ("q_a_norm", 384), ("kv_a_norm", 256),
         ("q_head_norm", 128), ("k_head_norm", 128), ("ffn2_norm", 1024))
WEIGHT_ORDER = ("ffn1_norm", "ffn1_w_gate", "ffn1_w_up", "ffn1_w_down", "mix_norm", "w_in", "gate_bias", "q_a_norm", "w_uq",
                "kv_a_norm", "w_uk", "w_uv", "q_head_norm", "k_head_norm", "w_proj_attn", "conv_w", "w_proj_conv", "w_out",
                "ffn2_norm", "ffn2_w_gate", "ffn2_w_up", "ffn2_w_down")
MATRICES = ("ffn1_w_gate", "ffn1_w_up", "ffn1_w_down", "w_in", "w_uq", "w_uk", "w_uv", "w_proj_attn", "w_proj_conv", "w_out",
            "ffn2_w_gate", "ffn2_w_up", "ffn2_w_down")
GROUP_FFN1 = ("ffn1_w_gate", "ffn1_w_up", "ffn1_w_down")
GROUP_IN = ("w_in", "w_uq", "w_uk", "w_uv", "conv_w")
GROUP_MIX = ("w_proj_attn", "w_proj_conv", "w_out")
GROUP_FFN2 = ("ffn2_w_gate", "ffn2_w_up", "ffn2_w_down")
GROUP_MID = ("w_in", "w_uq", "w_uk", "w_uv", "w_proj_attn", "w_proj_conv", "w_out")


def _pad_lanes(a, n):
    return jnp.pad(a.reshape(1, -1), ((0, 0), (0, n - a.size)))


def kernel(x, positions, ffn1_norm, ffn1_w_gate, ffn1_w_up, ffn1_w_down, mix_norm, w_in, gate_bias, q_a_norm, w_uq, kv_a_norm, w_uk, w_uv, q_head_norm, k_head_norm, w_proj_attn, conv_w, w_proj_conv, w_out, ffn2_norm, ffn2_w_gate, ffn2_w_up, ffn2_w_down, loss_target, m_ffn1_norm, m_ffn1_w_gate, m_ffn1_w_up, m_ffn1_w_down, m_mix_norm, m_w_in, m_gate_bias, m_q_a_norm, m_w_uq, m_kv_a_norm, m_w_uk, m_w_uv, m_q_head_norm, m_k_head_norm, m_w_proj_attn, m_conv_w, m_w_proj_conv, m_w_out, m_ffn2_norm, m_ffn2_w_gate, m_ffn2_w_up, m_ffn2_w_down, v_ffn1_norm, v_ffn1_w_gate, v_ffn1_w_up, v_ffn1_w_down, v_mix_norm, v_w_in, v_gate_bias, v_q_a_norm, v_w_uq, v_kv_a_norm, v_w_uk, v_w_uv, v_q_head_norm, v_k_head_norm, v_w_proj_attn, v_conv_w, v_w_proj_conv, v_w_out, v_ffn2_norm, v_ffn2_w_gate, v_ffn2_w_up, v_ffn2_w_down):
    args = dict(locals())
    view = lambda n, a: a.T if n in TRANSPOSED else a
    weights = {n: view(n, args[n]) for n in WEIGHT_ORDER}
    moments_m = {n: view(n, args["m_" + n]) for n in WEIGHT_ORDER}
    moments_v = {n: view(n, args["v_" + n]) for n in WEIGHT_ORDER}
    nb, seq, d = x.shape
    t = nb * seq
    chip = (2 * lax.axis_index("x") + lax.axis_index("y")).astype(jnp.int32)
    place = jnp.stack([chip, lax.axis_index("c").astype(jnp.int32)])
    grads, delta, new_m, new_v = {}, {}, {}, {}

    def adamw(names, carried=None):
        results = _adamw([(weights[n], grads[n], moments_m[n], moments_v[n]) for n in names], "adamw_" + names[0], carried)
        for n, (dn, mn, vn) in zip(names, results):
            delta[n], new_m[n], new_v[n] = dn, mn, vn

    conv_rows = conv_w.shape[0]
    conv_shard = jnp.pad(conv_w, ((0, 16 - conv_rows), (0, 0)))
    assert MATRICES[:len(GROUP_FFN1)] == GROUP_FFN1
    bufs = dict(zip(MATRICES + ("conv_w",), _cast_shards([weights[n] for n in MATRICES] + [conv_shard],
                                                         [BF16] * len(MATRICES) + [F32], len(GROUP_FFN1))))
    blocks = {n: bufs[n] for n in GROUP_FFN1}
    p = {n: _pad_lanes(weights[n], size) for n, size in SMALL}
    rope = _rope_tables(positions)
    x_tok = x.reshape(t, d)

    gather_in = _gather_carried([bufs[n] for n in GROUP_IN])
    x1, gate1, up1, act1 = _ffn_fwd(x_tok, p["ffn1_norm"], blocks["ffn1_w_gate"], blocks["ffn1_w_up"], blocks["ffn1_w_down"], None,
                                    "ffn1_fwd", gather_in)
    blocks.update(zip(GROUP_IN, gather_in.results))
    w = _kernel_layouts({"w_in": blocks["w_in"].reshape(-1, d), **{n: _from_col_blocks(blocks[n]) for n in ("w_uq", "w_uk", "w_uv")}})
    p["conv_w"] = _from_col_blocks(blocks["conv_w"])[:conv_rows]

    gather_mix = _gather_carried([bufs[n] for n in GROUP_MIX + GROUP_FFN2[2:]])
    h2b, big, lat, q, k, v, vt = _inproj_fwd(x1, p["mix_norm"], w["w_in"], w["w_kr"], p["q_a_norm"], p["kv_a_norm"], p["q_head_norm"],
                                             p["k_head_norm"], w["w_uq"], w["w_uk"], w["w_uv"], w["w_uvt"], rope, gather_mix)
    blocks.update(zip(GROUP_MIX + GROUP_FFN2[2:], gather_mix.results))
    w_pa = _from_col_blocks(blocks["w_proj_attn"])
    w_pc, w_out_full = blocks["w_proj_conv"].reshape(-1, d), blocks["w_out"].reshape(-1, d)

    gather_ffn2 = _gather_carried([bufs[n] for n in GROUP_FFN2[:2]])
    o, lse = _attn_fwd(q, k, vt, seq, gather_ffn2)
    x2 = _mix_fwd(x1, o, big, p["gate_bias"], p["conv_w"], w_pa, w_pc, w_out_full, seq)
    wg2, wu2 = gather_ffn2.results
    wd2 = blocks["ffn2_w_down"]
    dx3, gate2, up2, act2, loss = _ffn_fwd(x2, p["ffn2_norm"], wg2, wu2, wd2, loss_target.reshape(t, d), "ffn2_fwd")

    dx2, dg_ffn2, hb2, dgate2, dup2, dyb2 = _ffn_bwd_x(x2, p["ffn2_norm"], dx3, gate2, up2, wg2, wu2, wd2, "ffn2_bwd")
    g_ffn2 = [_tn_matmul(dgate2, hb2, "ffn2_dw_gate"), _tn_matmul(dup2, hb2, "ffn2_dw_up"), _tn_matmul(act2, dyb2, "ffn2_dw_down")]
    swap = _swap_carried(g_ffn2)
    do, delta_o, dz, dm, dbias, dw_pa, dw_pc, dw_out = _mix_bwd(dx2, o, big, p["gate_bias"], p["conv_w"], w_pa, w_pc, w_out_full, seq,
                                                                swap)
    part = _partials(GROUP_FFN2, g_ffn2, swap.results, place)
    send = _send_carried([pb for _, pb in part])
    dq, dk, dv = _attn_bwd(q, k, v, do, lse, delta_o.reshape(N_HEADS // ATTN_BWD_HEADS, ATTN_BWD_HEADS, -1), seq, send)
    join = _join_carried(_totals(GROUP_FFN2, g_ffn2, part, send.results, place))
    dp, dw_uq, dw_uk, dw_uv, dqa, dkva, dqh, dkh, dcw = _prep_bwd(
        lat, big, dz, dq, dk, dv, p["q_a_norm"], p["kv_a_norm"], p["q_head_norm"], p["k_head_norm"], w["w_uq"], w["w_uk"],
        w["w_uv"], rope, p["conv_w"], seq, join)
    grads.update(zip(GROUP_FFN2, join.results))

    gg = _global_layouts({"w_m": _tn_matmul(dm, h2b, "dw_in_m", split_k=2), "w_p": _tn_matmul(dp, h2b, "dw_in_p", split_k=2),
                          "w_uq": dw_uq, "w_uk": dw_uk, "w_uv": dw_uv, "w_pa": dw_pa, "w_pc": dw_pc, "w_out": dw_out})
    g_mid = [_col_blocks(gg[n]) if n in COL_SHARDED else gg[n].reshape(N_CHIPS, -1, gg[n].shape[-1]) for n in GROUP_MID]
    swap = _swap_carried(g_mid)
    dx1, dg_mix = _inproj_bwd(x1, p["mix_norm"], dx2, dm, dp, w["w_in"], w["w_kr"], swap)
    part = (_partials(GROUP_MID[:1], g_mid[:1], swap.results[:1], place)
            + _chip_partial_small(g_mid[1:], swap.results[1:], place))
    send = _send_carried([pb for _, pb in part])
    grad_x, dg_ffn1, hb1, dgate1, dup1, dyb1 = _ffn_bwd_x(x_tok, p["ffn1_norm"], dx1, gate1, up1, blocks["ffn1_w_gate"],
                                                         blocks["ffn1_w_up"], blocks["ffn1_w_down"], "ffn1_bwd", send)

    small_grads = {"ffn1_norm": dg_ffn1, "mix_norm": dg_mix, "gate_bias": dbias, "q_a_norm": dqa, "kv_a_norm": dkva,
                   "q_head_norm": dqh, "k_head_norm": dkh, "ffn2_norm": dg_ffn2}
    packed = jnp.concatenate([small_grads[n] for n, _ in SMALL] + [dcw.reshape(1, -1), loss], axis=1)
    total = _sum_devices(packed.reshape(8, -1)).reshape(1, -1)
    n_small = sum(size for _, size in SMALL)
    conv_cols = conv_w.shape[1]
    conv_total = total[:, n_small:n_small + conv_rows * d].reshape(conv_rows, d)
    grads["conv_w"] = lax.dynamic_slice_in_dim(conv_total, chip * conv_cols, conv_cols, axis=1)
    loss_total = total[0, n_small + conv_rows * d]

    join = _join_carried(_totals(GROUP_MID[:1], g_mid[:1], part[:1], send.results[:1], place)
                         + _chip_total_small([pf for pf, _ in part[1:]], send.results[1:], place, [g.shape[1:] for g in g_mid[1:]]))
    g_gate = _tn_matmul(dgate1, hb1, "ffn1_dw_gate", carried=join)
    grads.update(zip(GROUP_MID, join.results))
    swap_gate = _swap_carried([g_gate])
    g_up = _tn_matmul(dup1, hb1, "ffn1_dw_up", carried=swap_gate)
    part_gate = _partials(GROUP_FFN1[:1], [g_gate], swap_gate.results, place)
    send_gate, swap_up = _send_carried([part_gate[0][1]]), _swap_carried([g_up])
    g_down = _tn_matmul(act1, dyb1, "ffn1_dw_down", carried=_both(send_gate, swap_up))
    join_gate = _join_carried(_totals(GROUP_FFN1[:1], [g_gate], part_gate, send_gate.results, place))
    part_up = _partials(GROUP_FFN1[1:2], [g_up], swap_up.results, place)
    send_up, swap_down = _send_carried([part_up[0][1]]), _swap_carried([g_down])
    adamw(GROUP_FFN2, _both(_both(send_up, swap_down), join_gate))
    grads["ffn1_w_gate"] = join_gate.results[0]
    join_up = _join_carried(_totals(GROUP_FFN1[1:2], [g_up], part_up, send_up.results, place))
    part_down = _partials(GROUP_FFN1[2:], [g_down], swap_down.results, place)
    send_down = _send_carried([part_down[0][1]])
    adamw(("w_in",), _both(send_down, join_up))
    grads["ffn1_w_up"] = join_up.results[0]
    join_down = _join_carried(_totals(GROUP_FFN1[2:], [g_down], part_down, send_down.results, place))
    adamw(GROUP_FFN1[:2], join_down)
    grads["ffn1_w_down"] = join_down.results[0]
    adamw(GROUP_FFN1[2:])
    others = GROUP_MID[1:] + ("conv_w",)
    for n, (gn, dn, mn, vn) in zip(others, _adamw_whole([(weights[n], grads[n], moments_m[n], moments_v[n]) for n in others],
                                                        "adamw_others")):
        grads[n], delta[n], new_m[n], new_v[n] = gn, dn, mn, vn

    row = lambda a: a.reshape(1, -1)
    small = _adamw_small(total, [(row(weights[n]), row(moments_m[n]), row(moments_v[n])) for n, _ in SMALL], [size for _, size in SMALL])
    for (n, _), (gn, dn, mn, vn) in zip(SMALL, small):
        grads[n], delta[n], new_m[n], new_v[n] = gn.reshape(-1), dn.reshape(-1), mn.reshape(-1), vn.reshape(-1)

    return (loss_total, grad_x.reshape(nb, seq, d), *[view(n, src[n]) for src in (grads, delta, new_m, new_v) for n in WEIGHT_ORDER])
```

```python
import functools

import jax
import jax.numpy as jnp
from jax import lax
from jax.experimental import pallas as pl
from jax.experimental.pallas import tpu as pltpu

F32 = jnp.float32
BF16 = jnp.bfloat16

D_MODEL = 1024
N_HEADS = 8
QK_NOPE = 64
QK_ROPE = 32
QK_DIM = QK_NOPE + QK_ROPE
V_DIM = 64
HEAD_PAD = 128
Q_LORA = 384
KV_LORA = 256
ROPE_THETA = 10000.0
NORM_EPS = 1e-6
ATTN_SCALE = QK_DIM ** -0.5
MASK_VALUE = -1e30
N_CHIPS = 4
N_DEV = 8

ADAM_LR = 0.001
ADAM_B1 = 0.9
ADAM_B2 = 0.999
ADAM_EPS = 1e-08
ADAM_WD = 0.01
ADAM_STEP = 10

TOKEN_TILE = 256
PREP_TILE = 256
ATTN_TILE = 512
TN_TILE = 2048
VMEM_LIMIT = 56 * 1024 * 1024

M_COLS = 3 * D_MODEL
P_COLS = 2 * D_MODEL + Q_LORA + KV_LORA + HEAD_PAD
BIG_COLS = 5 * D_MODEL
LAT_COLS = Q_LORA + KV_LORA + HEAD_PAD

MESH_ID = pl.DeviceIdType.MESH
ANY = pl.BlockSpec(memory_space=pl.ANY)


def _params(semantics=None):
    return pltpu.CompilerParams(dimension_semantics=semantics, vmem_limit_bytes=VMEM_LIMIT)


class _Carried:
    def __init__(self, operands, out_shapes, aliases, n_sems, start, finish):
        self.operands, self.out_shapes, self.aliases, self.n_sems = list(operands), list(out_shapes), dict(aliases), n_sems
        self.start, self.finish = start, finish
        self.results = None


def _both(a, b):
    na, nao = len(a.operands), len(a.out_shapes)

    def start(ins, outs, sems, base):
        a.start(ins[:na], outs[:nao], sems, base)
        b.start(ins[na:], outs[nao:], sems, base + a.n_sems)

    def finish(ins, outs, sems, base):
        a.finish(ins[:na], outs[:nao], sems, base)
        b.finish(ins[na:], outs[nao:], sems, base + a.n_sems)

    aliases = dict(a.aliases)
    aliases.update({na + i: nao + o for i, o in b.aliases.items()})
    both = _Carried(a.operands + b.operands, a.out_shapes + b.out_shapes, aliases, a.n_sems + b.n_sems, start, finish)
    both.parts = (a, b)
    return both


def _set_results(carried, results):
    carried.results = list(results)
    if hasattr(carried, "parts"):
        a, b = carried.parts
        _set_results(a, results[:len(a.out_shapes)])
        _set_results(b, results[len(a.out_shapes):])


def _pallas(body, name, grid, in_specs, out_specs, out_shape, args, semantics, carried=None):
    if carried is None:
        return pl.pallas_call(body, name=name, grid=grid, in_specs=in_specs, out_specs=out_specs, out_shape=out_shape,
                              compiler_params=_params(semantics))(*args)
    n_in, n_out, n_ci, n_co = len(in_specs), len(out_specs), len(carried.operands), len(carried.out_shapes)

    def wrapped(*refs):
        ins, c_ins = refs[:n_in], refs[n_in:n_in + n_ci]
        outs, c_outs = refs[n_in + n_ci:n_in + n_ci + n_out], refs[n_in + n_ci + n_out:n_in + n_ci + n_out + n_co]
        sems = refs[-1]
        first = pl.program_id(0) == 0
        last = pl.program_id(0) == grid[0] - 1
        for axis in range(1, len(grid)):
            first = jnp.logical_and(first, pl.program_id(axis) == 0)
            last = jnp.logical_and(last, pl.program_id(axis) == grid[axis] - 1)

        @pl.when(first)
        def _():
            carried.start(c_ins, c_outs, sems, 0)

        body(*ins, *outs)

        @pl.when(last)
        def _():
            carried.finish(c_ins, c_outs, sems, 0)

    results = pl.pallas_call(
        wrapped, name=name, grid=grid, in_specs=list(in_specs) + [ANY] * n_ci, out_specs=list(out_specs) + [ANY] * n_co,
        out_shape=list(out_shape) + carried.out_shapes,
        input_output_aliases={n_in + i: n_out + o for i, o in carried.aliases.items()},
        scratch_shapes=[pltpu.SemaphoreType.DMA((carried.n_sems,))], compiler_params=_params(semantics))(*args, *carried.operands)
    _set_results(carried, results[n_out:])
    return results[:n_out]


def _resident(shape):
    nd = len(shape)
    return pl.BlockSpec(shape, lambda *_: (0,) * nd, pipeline_mode=pl.Buffered(1))


def _const(shape):
    nd = len(shape)
    return pl.BlockSpec(shape, lambda *_: (0,) * nd)


def _mm(a, b):
    return jnp.dot(a, b, preferred_element_type=F32)


def _mm_nt(a, b):
    return lax.dot_general(a, b, (((1,), (1,)), ((), ())), preferred_element_type=F32)


def _mm_tn(a, b):
    return lax.dot_general(a, b, (((0,), (0,)), ((), ())), preferred_element_type=F32)


def _bf(a):
    return a.astype(BF16)


def _sigmoid(a):
    return 1.0 / (1.0 + jnp.exp(-a))


def _rms(x, gain, n=None):
    n = x.shape[-1] if n is None else n
    r = lax.rsqrt(jnp.sum(x * x, axis=-1, keepdims=True) * (1.0 / n) + NORM_EPS)
    return (x * r) * gain, r


def _rms_bwd(x, r, gain, dh, n=None):
    n = x.shape[-1] if n is None else n
    u = dh * gain
    dx = r * u - x * ((r * r * r) * (jnp.sum(u * x, axis=-1, keepdims=True) * (1.0 / n)))
    dgain = jnp.sum(dh * (x * r), axis=0, keepdims=True)
    return dx, dgain


ROPE_HALF = QK_ROPE // 2


def _rope(t, rope):
    cos, s_lo, s_hi = rope
    return t * cos + pltpu.roll(t, HEAD_PAD - ROPE_HALF, 1) * s_lo + pltpu.roll(t, ROPE_HALF, 1) * s_hi


def _rope_bwd(dt, rope):
    cos, s_lo, s_hi = rope
    return dt * cos + pltpu.roll(dt * s_lo, ROPE_HALF, 1) + pltpu.roll(dt * s_hi, HEAD_PAD - ROPE_HALF, 1)


def _shift_down(u, prev8, k):
    s = pltpu.roll(u, k, 0)
    p = pltpu.roll(prev8, k, 0)
    row = lax.broadcasted_iota(jnp.int32, prev8.shape, 0)
    top = jnp.where(row < k, p, s[:8])
    return jnp.concatenate([top, s[8:]], axis=0)


def _shift_up(d, next8, k):
    tm = d.shape[0]
    s = pltpu.roll(d, tm - k, 0)
    n = pltpu.roll(next8, 8 - k, 0)
    row = lax.broadcasted_iota(jnp.int32, next8.shape, 0)
    bot = jnp.where(row >= 8 - k, n, s[tm - 8:])
    return jnp.concatenate([s[:tm - 8], bot], axis=0)


def _ffn_fwd(x, gain, wg, wu, wd, target, name, carried=None):
    t, d = x.shape
    nb, f, _ = wg.shape
    tm = TOKEN_TILE
    with_loss = target is not None

    def body(*refs):
        if with_loss:
            x_ref, g_ref, wg_ref, wu_ref, wd_ref, t_ref, out_ref, gate_ref, up_ref, act_ref, loss_ref = refs
        else:
            x_ref, g_ref, wg_ref, wu_ref, wd_ref, out_ref, gate_ref, up_ref, act_ref = refs
        xv = x_ref[...]
        h, _ = _rms(xv, g_ref[...])
        hb = _bf(h)
        y = jnp.zeros((tm, d), F32)
        nxt = (_mm_nt(hb, wg_ref[0]), _mm_nt(hb, wu_ref[0]))
        for j in range(nb):
            gate, up = nxt
            if j + 1 < nb:
                nxt = (_mm_nt(hb, wg_ref[j + 1]), _mm_nt(hb, wu_ref[j + 1]))
            act = _bf((gate * _sigmoid(gate)) * up)
            y = y + _mm(act, wd_ref[j])
            gate_ref[j] = _bf(gate)
            up_ref[j] = _bf(up)
            act_ref[j] = act
        out = xv + 0.5 * y
        if with_loss:
            err = out - t_ref[...]
            out_ref[...] = err * (1.0 / d)

            @pl.when(pl.program_id(0) == 0)
            def _():
                loss_ref[...] = jnp.zeros_like(loss_ref)

            part = jnp.sum(jnp.sum(err * err, axis=1, keepdims=True), axis=0, keepdims=True)
            loss_ref[...] += jnp.broadcast_to(part * (0.5 / d), loss_ref.shape)
        else:
            out_ref[...] = out

    tok = pl.BlockSpec((tm, d), lambda i: (i, 0))
    blk = pl.BlockSpec((nb, tm, f), lambda i: (0, i, 0))
    in_specs = [tok, _const((1, d)), _resident(wg.shape), _resident(wu.shape), _resident(wd.shape)]
    args = [x, gain, wg, wu, wd]
    out_shape = [jax.ShapeDtypeStruct((t, d), F32)] + [jax.ShapeDtypeStruct((nb, t, f), BF16)] * 3
    out_specs = [tok, blk, blk, blk]
    if with_loss:
        in_specs.append(tok)
        args.append(target)
        out_shape.append(jax.ShapeDtypeStruct((1, 128), F32))
        out_specs.append(_const((1, 128)))
    return _pallas(body, name, (t // tm,), in_specs, out_specs, out_shape, args, ("arbitrary",), carried)


def _ffn_bwd_x(x, gain, dout, gate, up, wg, wu, wd, name, carried=None):
    t, d = x.shape
    nb, f, _ = wg.shape
    tm = TOKEN_TILE

    def body(x_ref, g_ref, dout_ref, gate_ref, up_ref, wg_ref, wu_ref, wd_ref,
             dx_ref, dgain_ref, hb_ref, dgate_ref, dup_ref, dyb_ref):
        xv = x_ref[...]
        gain_v = g_ref[...]
        h, r = _rms(xv, gain_v)
        hb_ref[...] = _bf(h)
        dout_v = dout_ref[...]
        dyb = _bf(0.5 * dout_v)
        dyb_ref[...] = dyb
        dh = jnp.zeros((tm, d), F32)
        nxt = _mm_nt(dyb, wd_ref[0])
        for j in range(nb):
            dact = nxt
            if j + 1 < nb:
                nxt = _mm_nt(dyb, wd_ref[j + 1])
            gt = gate_ref[j].astype(F32)
            uv = up_ref[j].astype(F32)
            s = _sigmoid(gt)
            dup = _bf(dact * (gt * s))
            dgate = _bf((dact * uv) * (s * (1.0 + gt * (1.0 - s))))
            dh = dh + _mm(dgate, wg_ref[j]) + _mm(dup, wu_ref[j])
            dgate_ref[j] = dgate
            dup_ref[j] = dup
        dxn, dgain = _rms_bwd(xv, r, gain_v, dh)
        dx_ref[...] = dout_v + dxn

        @pl.when(pl.program_id(0) == 0)
        def _():
            dgain_ref[...] = jnp.zeros_like(dgain_ref)

        dgain_ref[...] += dgain

    tok = pl.BlockSpec((tm, d), lambda i: (i, 0))
    blk = pl.BlockSpec((nb, tm, f), lambda i: (0, i, 0))
    return _pallas(
        body, name, (t // tm,),
        [tok, _const((1, d)), tok, blk, blk, _resident(wg.shape), _resident(wu.shape), _resident(wd.shape)],
        [tok, _const((1, d)), tok, blk, blk, tok],
        [jax.ShapeDtypeStruct((t, d), F32), jax.ShapeDtypeStruct((1, d), F32), jax.ShapeDtypeStruct((t, d), BF16),
         jax.ShapeDtypeStruct((nb, t, f), BF16), jax.ShapeDtypeStruct((nb, t, f), BF16), jax.ShapeDtypeStruct((t, d), BF16)],
        (x, gain, dout, gate, up, wg, wu, wd), ("arbitrary",), carried)


def _tn_matmul(a, b, name, split_k=1, carried=None):
    t = a.shape[-2]
    k = a.shape[-1]
    n = b.shape[-1]
    tt = min(TN_TILE, t)
    nt = t // tt

    def body(a_ref, b_ref, o_ref):
        @pl.when(pl.program_id(1) == 0)
        def _():
            o_ref[...] = jnp.zeros_like(o_ref)

        o_ref[...] += _mm_tn(a_ref[...], b_ref[...])

    if split_k > 1:
        assert a.ndim == 2 and b.ndim == 2 and k % (split_k * 128) == 0
        tk = k // split_k
        g = split_k
        a_spec = pl.BlockSpec((tt, tk), lambda gi, ti: (ti, gi))
        b_spec = pl.BlockSpec((tt, n), lambda gi, ti: (ti, 0))
        o_spec = pl.BlockSpec((tk, n), lambda gi, ti: (gi, 0))
        out_shape = jax.ShapeDtypeStruct((k, n), F32)
    else:
        g = a.shape[0] if a.ndim == 3 else b.shape[0]
        a_spec = (pl.BlockSpec((None, tt, k), lambda gi, ti: (gi, ti, 0)) if a.ndim == 3
                  else pl.BlockSpec((tt, k), lambda gi, ti: (ti, 0)))
        b_spec = (pl.BlockSpec((None, tt, n), lambda gi, ti: (gi, ti, 0)) if b.ndim == 3
                  else pl.BlockSpec((tt, n), lambda gi, ti: (ti, 0)))
        o_spec = pl.BlockSpec((None, k, n), lambda gi, ti: (gi, 0, 0))
        out_shape = jax.ShapeDtypeStruct((g, k, n), F32)
    return _pallas(body, name, (g, nt), [a_spec, b_spec], [o_spec], [out_shape], (a, b), ("arbitrary", "arbitrary"), carried)[0]


ROW_QKV, ROW_KR, ROW_XC = 0, Q_LORA + KV_LORA, Q_LORA + KV_LORA + QK_ROPE
ROW_GB, ROW_GC, ROW_GL = ROW_XC + D_MODEL, ROW_XC + 2 * D_MODEL, ROW_XC + 3 * D_MODEL
BIG_FROM_ROWS = ((0, ROW_GB, D_MODEL), (D_MODEL, ROW_GL, 2 * D_MODEL), (3 * D_MODEL, ROW_XC, D_MODEL), (4 * D_MODEL, ROW_GC, D_MODEL))


def _inproj_fwd(x1, gain, w_in, w_kr, qa_gain, kva_gain, qh_gain, kh_gain, w_uq, w_uk, w_uv, w_uvt, rope, carried=None):
    t, d = x1.shape
    tm = TOKEN_TILE
    chunk = 512
    chunks = []
    for col, row, size in BIG_FROM_ROWS:
        chunks += [(col + o, row + o, chunk) for o in range(0, size, chunk)]
    of_head = [[c for k, c in enumerate(chunks) if k * N_HEADS // len(chunks) == hd] for hd in range(N_HEADS)]

    def body(x_ref, g_ref, win_ref, wkr_ref, qa_ref, kva_ref, qh_ref, kh_ref, wuq_ref, wuk_ref, wuv_ref, wuvt_ref, cos_ref, slo_ref,
             shi_ref, hb_ref, big_ref, lat_ref, q_ref, k_ref, v_ref, vt_ref):
        h, _ = _rms(x_ref[...], g_ref[...])
        hb = _bf(h)
        hb_ref[...] = hb
        k_rope = _mm_nt(hb, wkr_ref[...])
        lat = jnp.concatenate([_mm_nt(hb, win_ref[ROW_QKV:ROW_KR, :]), k_rope], axis=1)
        lat_ref[...] = lat
        cq, _ = _rms(lat[:, :Q_LORA], qa_ref[...])
        ckv, _ = _rms(lat[:, Q_LORA:Q_LORA + KV_LORA], kva_ref[...])
        cqb = _bf(cq)
        ckvb = _bf(ckv)
        rope_v = (cos_ref[...], slo_ref[...], shi_ref[...])
        q_all = _mm(cqb, wuq_ref[...])
        k_all = _mm(ckvb, wuk_ref[...])
        v_ref[...] = _bf(_mm(ckvb, wuv_ref[...]))
        vt_all = _mm_nt(wuvt_ref[...], ckvb)
        for hd in range(N_HEADS):
            for col, row, size in of_head[hd]:
                big_ref[:, col:col + size] = _mm_nt(hb, win_ref[row:row + size, :])
            lanes = slice(hd * HEAD_PAD, (hd + 1) * HEAD_PAD)
            qn, _ = _rms(q_all[:, lanes], qh_ref[...], QK_DIM)
            q_ref[hd] = _bf(_rope(qn, rope_v))
            kn, _ = _rms(k_all[:, lanes] + k_rope, kh_ref[...], QK_DIM)
            k_ref[hd] = _bf(_rope(kn, rope_v))
            vt_ref[hd] = _bf(vt_all[hd * V_DIM:(hd + 1) * V_DIM])

    tok = lambda c: pl.BlockSpec((tm, c), lambda i: (i, 0))
    head = lambda c: pl.BlockSpec((N_HEADS, tm, c), lambda i: (0, i, 0))
    return _pallas(
        body, "inproj_fwd", (t // tm,),
        [tok(d), _const((1, d)), _resident(w_in.shape), _resident(w_kr.shape), _const((1, Q_LORA)), _const((1, KV_LORA)),
         _const((1, HEAD_PAD)), _const((1, HEAD_PAD)), _resident(w_uq.shape), _resident(w_uk.shape),
         _resident(w_uv.shape), _resident(w_uvt.shape), tok(HEAD_PAD), tok(HEAD_PAD), tok(HEAD_PAD)],
        [tok(d), tok(BIG_COLS), tok(LAT_COLS), head(HEAD_PAD), head(HEAD_PAD), tok(N_HEADS * V_DIM),
         pl.BlockSpec((N_HEADS, V_DIM, tm), lambda i: (0, 0, i))],
        [jax.ShapeDtypeStruct((t, d), BF16), jax.ShapeDtypeStruct((t, BIG_COLS), F32),
         jax.ShapeDtypeStruct((t, LAT_COLS), F32), jax.ShapeDtypeStruct((N_HEADS, t, HEAD_PAD), BF16),
         jax.ShapeDtypeStruct((N_HEADS, t, HEAD_PAD), BF16), jax.ShapeDtypeStruct((t, N_HEADS * V_DIM), BF16),
         jax.ShapeDtypeStruct((N_HEADS, V_DIM, t), BF16)],
        (x1, gain, w_in, w_kr, qa_gain, kva_gain, qh_gain, kh_gain, w_uq, w_uk, w_uv, w_uvt, *rope), ("arbitrary",), carried)


EXP2_SCALE = ATTN_SCALE * 1.4426950408889634


def _diagonal_keep(tk, tq):
    return lax.broadcasted_iota(jnp.int32, (tk, tq), 0) <= lax.broadcasted_iota(jnp.int32, (tk, tq), 1)


def _attn_fwd(q, k, vt, seq, carried=None):
    _, t, _ = q.shape
    nseq = t // seq
    tq = tk = ATTN_TILE
    nq = seq // tq

    def body(q_ref, k_ref, vt_ref, o_ref, lse_ref):
        i = pl.program_id(1)
        qs = [q_ref[h] for h in range(N_HEADS)]
        keep = _diagonal_keep(tk, tq)

        def scores(h, k0):
            return _mm_nt(k_ref[h, pl.ds(k0, tk), :], qs[h])

        def update(h, st, state, k0, diagonal):
            m, l, acc = state
            if diagonal:
                st = jnp.where(keep, st, MASK_VALUE)
            m_new = jnp.maximum(m, jnp.max(st, axis=0, keepdims=True))
            pt = jnp.exp2((st - m_new) * EXP2_SCALE)
            alpha = jnp.exp2((m - m_new) * EXP2_SCALE)
            l_new = alpha * l + jnp.sum(pt, axis=0, keepdims=True)
            return m_new, l_new, alpha * acc + _mm(vt_ref[h, :, pl.ds(k0, tk)], _bf(pt))

        def tiles(states, k0, diagonal):
            st, new = scores(0, k0), []
            for h in range(N_HEADS):
                st_next = scores(h + 1, k0) if h + 1 < N_HEADS else None
                new.append(update(h, st, states[h], k0, diagonal))
                st = st_next
            return tuple(new)

        init = tuple((jnp.full((1, tq), MASK_VALUE, F32), jnp.zeros((1, tq), F32), jnp.zeros((V_DIM, tq), F32))
                     for _ in range(N_HEADS))
        states = lax.fori_loop(0, i, lambda j, s: tiles(s, pl.multiple_of(j * tk, tk), False), init)
        states = tiles(states, pl.multiple_of(i * tk, tk), True)
        outs = []
        for h in range(N_HEADS):
            m, l, acc = states[h]
            outs.append((acc / l).T)
            lse_ref[h] = m * EXP2_SCALE + jnp.log2(l)
        o_ref[...] = _bf(jnp.concatenate(outs, axis=-1))

    return _pallas(
        body, "attn_fwd", (nseq, nq),
        [pl.BlockSpec((N_HEADS, tq, HEAD_PAD), lambda b, i: (0, b * nq + i, 0)),
         pl.BlockSpec((N_HEADS, seq, HEAD_PAD), lambda b, i: (0, b, 0)),
         pl.BlockSpec((N_HEADS, V_DIM, seq), lambda b, i: (0, 0, b))],
        [pl.BlockSpec((tq, N_HEADS * V_DIM), lambda b, i: (b * nq + i, 0)),
         pl.BlockSpec((N_HEADS, 1, tq), lambda b, i: (0, 0, b * nq + i))],
        [jax.ShapeDtypeStruct((t, N_HEADS * V_DIM), BF16), jax.ShapeDtypeStruct((N_HEADS, 1, t), F32)],
        (q, k, vt), ("arbitrary", "arbitrary"), carried)


ATTN_BWD_HEADS = 4


def _attn_bwd(q, k, v, do, lse, delta, seq, carried=None):
    _, t, _ = q.shape
    nseq = t // seq
    tq = tk = ATTN_TILE
    n = seq // tq
    hb = ATTN_BWD_HEADS

    def body(q_ref, k_ref, v_ref, do_ref, lse_ref, delta_ref, dq_ref, dk_ref, dv_ref):
        dq_ref[...] = jnp.zeros_like(dq_ref)
        dk_ref[...] = jnp.zeros_like(dk_ref)
        dv_ref[...] = jnp.zeros_like(dv_ref)
        keep = _diagonal_keep(tk, tq)

        def tile(h, k0, q0, diagonal):
            kj = k_ref[h, pl.ds(k0, tk), :]
            qi = q_ref[h, pl.ds(q0, tq), :]
            doi = _bf(do_ref[pl.ds(q0, tq), h * V_DIM:(h + 1) * V_DIM])
            st = _mm_nt(kj, qi)
            if diagonal:
                st = jnp.where(keep, st, MASK_VALUE)
            pt = jnp.exp2(st * EXP2_SCALE - lse_ref[h, :, pl.ds(q0, tq)])
            dv_ref[pl.ds(k0, tk), h * V_DIM:(h + 1) * V_DIM] += _mm(_bf(pt), doi)
            dpt = _mm_nt(v_ref[pl.ds(k0, tk), h * V_DIM:(h + 1) * V_DIM], doi)
            dst = _bf((pt * (dpt - delta_ref[pl.ds(h, 1), pl.ds(q0, tq)])) * ATTN_SCALE)
            dk_ref[h, pl.ds(k0, tk), :] += _mm(dst, qi)
            dq_ref[h, pl.ds(q0, tq), :] += _mm_tn(dst, kj)

        def kv_step(j, _):
            k0 = pl.multiple_of(j * tk, tk)
            for h in range(hb):
                tile(h, k0, k0, True)

            def q_step(i, _):
                q0 = pl.multiple_of(i * tq, tq)
                for h in range(hb):
                    tile(h, k0, q0, False)
                return 0

            lax.fori_loop(j + 1, n, q_step, 0)
            return 0

        lax.fori_loop(0, n, kv_step, 0)

    hspec = lambda c: pl.BlockSpec((hb, seq, c), lambda b, g: (g, b, 0))
    cols = pl.BlockSpec((seq, hb * V_DIM), lambda b, g: (b, g))
    return _pallas(
        body, "attn_bwd", (nseq, N_HEADS // hb),
        [hspec(HEAD_PAD), hspec(HEAD_PAD), cols, cols,
         pl.BlockSpec((hb, 1, seq), lambda b, g: (g, 0, b)), pl.BlockSpec((None, hb, seq), lambda b, g: (g, 0, b))],
        [hspec(HEAD_PAD), hspec(HEAD_PAD), cols],
        [jax.ShapeDtypeStruct((N_HEADS, t, HEAD_PAD), F32), jax.ShapeDtypeStruct((N_HEADS, t, HEAD_PAD), F32),
         jax.ShapeDtypeStruct((t, N_HEADS * V_DIM), F32)],
        (q, k, v, do, lse, delta), ("arbitrary", "arbitrary"), carried)


def _merged_mixers(o_ref, gb_ref, gla_ref, glb_ref, xc_ref, gc_ref, xcp_ref, gcp_ref, bias_ref, cw_ref, wpa_ref, wpc_ref,
                   first_of_seq):
    y_a = _mm(o_ref[...], wpa_ref[...])
    gb = gb_ref[...]
    u = gc_ref[...] * xc_ref[...]
    u_prev = jnp.where(first_of_seq, 0.0, gcp_ref[...] * xcp_ref[...])
    cw = cw_ref[...]
    z = cw[2:3] * u + cw[1:2] * _shift_down(u, u_prev, 1) + cw[0:1] * _shift_down(u, u_prev, 2)
    gbz = _bf(gb * z)
    y_b = _mm(gbz, wpc_ref[...])
    bias = bias_ref[...]
    gate_a = _sigmoid(gla_ref[...] + bias[:, :D_MODEL])
    gate_b = _sigmoid(glb_ref[...] + bias[:, D_MODEL:])
    return _bf(gate_a * y_a + gate_b * y_b)


def _mixer_specs(tm, seq):
    d = D_MODEL
    tok = pl.BlockSpec((tm, d), lambda i: (i, 0))
    col = lambda c: pl.BlockSpec((tm, d), lambda i: (i, c))
    prev = lambda c: pl.BlockSpec((8, d), lambda i: (jnp.maximum(i * (tm // 8) - 1, 0), c))
    o_spec = pl.BlockSpec((tm, N_HEADS * V_DIM), lambda i: (i, 0))
    fwd_specs = [o_spec, col(0), col(1), col(2), col(3), col(4), prev(3), prev(4), _const((1, 2 * d)), _const((3, d)),
                 _resident((N_HEADS * V_DIM, d)), _resident((d, d)), _resident((d, d))]
    return tok, fwd_specs


def _mix_fwd(x1, o, big, gate_bias, conv_w, w_pa, w_pc, w_out, seq, carried=None):
    t, d = x1.shape
    tm = TOKEN_TILE
    tiles_per_seq = seq // tm

    def body(x_ref, o_ref, gb_ref, gla_ref, glb_ref, xc_ref, gc_ref, xcp_ref, gcp_ref, bias_ref, cw_ref, wpa_ref, wpc_ref,
             wout_ref, x2_ref):
        first = pl.program_id(0) % tiles_per_seq == 0
        merged = _merged_mixers(o_ref, gb_ref, gla_ref, glb_ref, xc_ref, gc_ref, xcp_ref, gcp_ref, bias_ref, cw_ref, wpa_ref,
                                wpc_ref, first)
        x2_ref[...] = x_ref[...] + _mm(merged, wout_ref[...])

    tok, fwd_specs = _mixer_specs(tm, seq)
    return _pallas(body, "mix_fwd", (t // tm,), [tok] + fwd_specs, [tok], [jax.ShapeDtypeStruct((t, d), F32)],
                   (x1, o, big, big, big, big, big, big, big, gate_bias, conv_w, w_pa, w_pc, w_out), ("arbitrary",), carried)[0]


def _mix_bwd(dx2, o, big, gate_bias, conv_w, w_pa, w_pc, w_out, seq, carried=None):
    t, d = dx2.shape
    tm = TOKEN_TILE
    tiles_per_seq = seq // tm
    hv = N_HEADS * V_DIM

    def body(dx_ref, o_ref, gb_ref, gla_ref, glb_ref, xc_ref, gc_ref, xcp_ref, gcp_ref, bias_ref, cw_ref, wpa_ref, wpc_ref,
             wout_ref, do_ref, delta_ref, dz_ref, dm_ref, dbias_ref, dwpa_ref, dwpc_ref, dwout_ref):
        @pl.when(pl.program_id(0) == 0)
        def _():
            dbias_ref[...] = jnp.zeros_like(dbias_ref)
            dwpa_ref[...] = jnp.zeros_like(dwpa_ref)
            dwpc_ref[...] = jnp.zeros_like(dwpc_ref)
            dwout_ref[...] = jnp.zeros_like(dwout_ref)

        first = pl.program_id(0) % tiles_per_seq == 0
        dxb = _bf(dx_ref[...])
        dmerged = _mm_nt(dxb, wout_ref[...])
        y_a = _mm(o_ref[...], wpa_ref[...])
        bias = bias_ref[...]
        gate_a = _sigmoid(gla_ref[...] + bias[:, :d])
        gate_b = _sigmoid(glb_ref[...] + bias[:, d:])
        dya = _bf(dmerged * gate_a)
        dyb = _bf(dmerged * gate_b)
        do_v = _mm_nt(dya, wpa_ref[...])
        dgz = _mm_nt(dyb, wpc_ref[...])
        dwpa_ref[...] += _mm_tn(o_ref[...], dya)
        gb = gb_ref[...]
        u = gc_ref[...] * xc_ref[...]
        u_prev = jnp.where(first, 0.0, gcp_ref[...] * xcp_ref[...])
        cw = cw_ref[...]
        z = cw[2:3] * u + cw[1:2] * _shift_down(u, u_prev, 1) + cw[0:1] * _shift_down(u, u_prev, 2)
        gbz = _bf(gb * z)
        y_b = _mm(gbz, wpc_ref[...])
        dwpc_ref[...] += _mm_tn(gbz, dyb)
        do_ref[...] = do_v
        head = lax.broadcasted_iota(jnp.int32, (N_HEADS, hv), 0) * V_DIM
        col = lax.broadcasted_iota(jnp.int32, (N_HEADS, hv), 1)
        in_head = ((col >= head) & (col < head + V_DIM)).astype(F32)
        delta_ref[...] = lax.dot_general(in_head, do_v * o_ref[...].astype(F32), (((1,), (1,)), ((), ())),
                                         precision=lax.Precision.HIGHEST, preferred_element_type=F32)
        dz_ref[...] = dgz * gb
        dm_ref[:, :d] = _bf(dgz * z)
        merged = _bf(gate_a * y_a + gate_b * y_b)
        dwout_ref[...] += _mm_tn(merged, dxb)
        dla = (dmerged * y_a) * (gate_a * (1.0 - gate_a))
        dlb = (dmerged * y_b) * (gate_b * (1.0 - gate_b))
        dbias_ref[:, :d] += jnp.sum(dla, axis=0, keepdims=True)
        dbias_ref[:, d:] += jnp.sum(dlb, axis=0, keepdims=True)
        dm_ref[:, d:2 * d] = _bf(dla)
        dm_ref[:, 2 * d:] = _bf(dlb)

    tok, fwd_specs = _mixer_specs(tm, seq)
    return _pallas(
        body, "mix_bwd", (t // tm,), [tok] + fwd_specs,
        [pl.BlockSpec((tm, hv), lambda i: (i, 0)), pl.BlockSpec((N_HEADS, tm), lambda i: (0, i)), tok,
         pl.BlockSpec((tm, M_COLS), lambda i: (i, 0)), _const((1, 2 * d)), _const((hv, d)), _const((d, d)), _const((d, d))],
        [jax.ShapeDtypeStruct((t, hv), F32), jax.ShapeDtypeStruct((N_HEADS, t), F32), jax.ShapeDtypeStruct((t, d), F32),
         jax.ShapeDtypeStruct((t, M_COLS), BF16), jax.ShapeDtypeStruct((1, 2 * d), F32), jax.ShapeDtypeStruct((hv, d), F32),
         jax.ShapeDtypeStruct((d, d), F32), jax.ShapeDtypeStruct((d, d), F32)],
        (dx2, o, big, big, big, big, big, big, big, gate_bias, conv_w, w_pa, w_pc, w_out), ("arbitrary",), carried)


def _prep_bwd(lat, big, dz, dq, dk, dv, qa_gain, kva_gain, qh_gain, kh_gain, w_uq, w_uk, w_uv, rope, conv_w, seq, carried=None):
    t = lat.shape[0]
    d = D_MODEL
    tm = PREP_TILE
    tiles_per_seq = seq // tm
    last_blk = t // 8 - 1

    def body(lat_ref, xc_ref, gc_ref, dz_ref, dzn_ref, dq_ref, dk_ref, dv_ref, qa_ref, kva_ref, qh_ref, kh_ref, wuq_ref, wuk_ref,
             wuv_ref, cos_ref, slo_ref, shi_ref, cw_ref,
             dp_ref, dwuq_ref, dwuk_ref, dwuv_ref, dqa_ref, dkva_ref, dqh_ref, dkh_ref, dcw_ref):
        pid = pl.program_id(0)

        @pl.when(pid == 0)
        def _():
            for r in (dwuq_ref, dwuk_ref, dwuv_ref, dqa_ref, dkva_ref, dqh_ref, dkh_ref, dcw_ref):
                r[...] = jnp.zeros_like(r)

        lat_v = lat_ref[...]
        q_lat = lat_v[:, :Q_LORA]
        kv_lat = lat_v[:, Q_LORA:Q_LORA + KV_LORA]
        k_rope = lat_v[:, Q_LORA + KV_LORA:]
        qa_gain_v = qa_ref[...]
        kva_gain_v = kva_ref[...]
        qh_gain_v = qh_ref[...]
        kh_gain_v = kh_ref[...]
        cq, rq = _rms(q_lat, qa_gain_v)
        ckv, rkv = _rms(kv_lat, kva_gain_v)
        cqb = _bf(cq)
        ckvb = _bf(ckv)
        rope_v = (cos_ref[...], slo_ref[...], shi_ref[...])
        lane = lax.broadcasted_iota(jnp.int32, (tm, HEAD_PAD), 1)
        rope_lanes = (lane >= QK_NOPE) & (lane < QK_DIM)
        dk_rope = jnp.zeros((tm, HEAD_PAD), F32)
        dqh_gain = jnp.zeros((1, HEAD_PAD), F32)
        dkh_gain = jnp.zeros((1, HEAD_PAD), F32)
        q_all = _mm(cqb, wuq_ref[...])
        k_all = _mm(ckvb, wuk_ref[...])
        dvb = _bf(dv_ref[...])
        dckv = _mm_nt(dvb, wuv_ref[...])
        dwuv_ref[...] += _mm_tn(ckvb, dvb)

        last = pid % tiles_per_seq == tiles_per_seq - 1
        dzv = dz_ref[...]
        dz_next = jnp.where(last, 0.0, dzn_ref[...])
        dz1 = _shift_up(dzv, dz_next, 1)
        dz2 = _shift_up(dzv, dz_next, 2)
        cw = cw_ref[...]
        xc = xc_ref[...]
        gc = gc_ref[...]
        u = gc * xc
        du = cw[2:3] * dzv + cw[1:2] * dz1 + cw[0:1] * dz2
        dp_ref[:, :d] = _bf(du * gc)
        dp_ref[:, d:2 * d] = _bf(du * xc)
        dcw_ref[0:1, :] += jnp.sum(dz2 * u, axis=0, keepdims=True)
        dcw_ref[1:2, :] += jnp.sum(dz1 * u, axis=0, keepdims=True)
        dcw_ref[2:3, :] += jnp.sum(dzv * u, axis=0, keepdims=True)

        dcq = jnp.zeros((tm, Q_LORA), F32)
        half = N_HEADS // 2
        for part in range(2):
            dq_heads, dk_heads = [], []
            for hd in range(part * half, (part + 1) * half):
                lanes = slice(hd * HEAD_PAD, (hd + 1) * HEAD_PAD)
                q_pre = q_all[:, lanes]
                _, rr = _rms(q_pre, qh_gain_v, QK_DIM)
                dq_pre, dg = _rms_bwd(q_pre, rr, qh_gain_v, _rope_bwd(dq_ref[hd], rope_v), QK_DIM)
                dqh_gain = dqh_gain + dg
                dq_heads.append(_bf(dq_pre))

                k_pre = k_all[:, lanes] + k_rope
                _, rr = _rms(k_pre, kh_gain_v, QK_DIM)
                dk_pre, dg = _rms_bwd(k_pre, rr, kh_gain_v, _rope_bwd(dk_ref[hd], rope_v), QK_DIM)
                dkh_gain = dkh_gain + dg
                dk_rope = dk_rope + jnp.where(rope_lanes, dk_pre, 0.0)
                dk_heads.append(_bf(dk_pre))
            dq_part = jnp.concatenate(dq_heads, axis=1)
            dk_part = jnp.concatenate(dk_heads, axis=1)
            cols = slice(part * half * HEAD_PAD, (part + 1) * half * HEAD_PAD)
            dcq = dcq + _mm_nt(dq_part, wuq_ref[:, cols])
            dckv = dckv + _mm_nt(dk_part, wuk_ref[:, cols])
            dwuq_ref[:, cols] += _mm_tn(cqb, dq_part)
            dwuk_ref[:, cols] += _mm_tn(ckvb, dk_part)
        dqh_ref[...] += dqh_gain
        dkh_ref[...] += dkh_gain
        dq_lat, dg = _rms_bwd(q_lat, rq, qa_gain_v, dcq)
        dqa_ref[...] += dg
        dkv_lat, dg = _rms_bwd(kv_lat, rkv, kva_gain_v, dckv)
        dkva_ref[...] += dg
        dp_ref[:, 2 * d:2 * d + Q_LORA] = _bf(dq_lat)
        dp_ref[:, 2 * d + Q_LORA:2 * d + Q_LORA + KV_LORA] = _bf(dkv_lat)
        dp_ref[:, 2 * d + Q_LORA + KV_LORA:] = _bf(dk_rope)

    tok = lambda c: pl.BlockSpec((tm, c), lambda i: (i, 0))
    col = lambda c: pl.BlockSpec((tm, d), lambda i: (i, c))
    head = lambda c: pl.BlockSpec((N_HEADS, tm, c), lambda i: (0, i, 0))
    nxt = pl.BlockSpec((8, d), lambda i: (jnp.minimum((i + 1) * (tm // 8), last_blk), 0))
    return _pallas(
        body, "prep_bwd", (t // tm,),
        [tok(LAT_COLS), col(3), col(4), tok(d), nxt, head(HEAD_PAD), head(HEAD_PAD), tok(N_HEADS * V_DIM),
         _const((1, Q_LORA)), _const((1, KV_LORA)), _const((1, HEAD_PAD)), _const((1, HEAD_PAD)),
         _resident(w_uq.shape), _resident(w_uk.shape), _resident(w_uv.shape), tok(HEAD_PAD), tok(HEAD_PAD), tok(HEAD_PAD),
         _const((3, d))],
        [tok(P_COLS), _const(w_uq.shape), _const(w_uk.shape), _const(w_uv.shape), _const((1, Q_LORA)),
         _const((1, KV_LORA)), _const((1, HEAD_PAD)), _const((1, HEAD_PAD)), _const((3, d))],
        [jax.ShapeDtypeStruct((t, P_COLS), BF16), jax.ShapeDtypeStruct(w_uq.shape, F32),
         jax.ShapeDtypeStruct(w_uk.shape, F32), jax.ShapeDtypeStruct(w_uv.shape, F32),
         jax.ShapeDtypeStruct((1, Q_LORA), F32), jax.ShapeDtypeStruct((1, KV_LORA), F32),
         jax.ShapeDtypeStruct((1, HEAD_PAD), F32), jax.ShapeDtypeStruct((1, HEAD_PAD), F32), jax.ShapeDtypeStruct((3, d), F32)],
        (lat, big, big, dz, dz, dq, dk, dv, qa_gain, kva_gain, qh_gain, kh_gain, w_uq, w_uk, w_uv, *rope, conv_w),
        ("arbitrary",), carried)


def _inproj_bwd(x1, gain, dx2, dm, dp, w_in, w_kr, carried=None):
    t, d = x1.shape
    tm = TOKEN_TILE

    def body(x_ref, g_ref, dx2_ref, dm_ref, dp_ref, win_ref, wkr_ref, dx1_ref, dgain_ref):
        xv = x_ref[...]
        gain_v = g_ref[...]
        _, r = _rms(xv, gain_v)
        dh = (_mm(dm_ref[:, :d], win_ref[ROW_GB:ROW_GC, :]) + _mm(dm_ref[:, d:], win_ref[ROW_GL:, :])
              + _mm(dp_ref[:, :d], win_ref[ROW_XC:ROW_GB, :]) + _mm(dp_ref[:, d:2 * d], win_ref[ROW_GC:ROW_GL, :])
              + _mm(dp_ref[:, 2 * d:2 * d + ROW_KR], win_ref[ROW_QKV:ROW_KR, :]) + _mm(dp_ref[:, 2 * d + ROW_KR:], wkr_ref[...]))
        dxn, dgain = _rms_bwd(xv, r, gain_v, dh)
        dx1_ref[...] = dx2_ref[...] + dxn

        @pl.when(pl.program_id(0) == 0)
        def _():
            dgain_ref[...] = jnp.zeros_like(dgain_ref)

        dgain_ref[...] += dgain

    tok = lambda c: pl.BlockSpec((tm, c), lambda i: (i, 0))
    return _pallas(
        body, "inproj_bwd", (t // tm,),
        [tok(d), _const((1, d)), tok(d), tok(M_COLS), tok(P_COLS), _resident(w_in.shape), _resident(w_kr.shape)],
        [tok(d), _const((1, d))], [jax.ShapeDtypeStruct((t, d), F32), jax.ShapeDtypeStruct((1, d), F32)],
        (x1, gain, dx2, dm, dp, w_in, w_kr), ("arbitrary",), carried)


def _adamw(quads, name, carried=None):
    k = len(quads)
    rows, cols = quads[0][0].shape
    tr, tc = rows, cols
    for cand in (512, 352, 256, 192, 128, 64):
        if rows % cand == 0 and rows > cand:
            tr = cand
            break
    if tr == rows and rows * cols > 512 * 1024 and cols % 256 == 0:
        tc = 256
    while k * 14 * tr * tc * 4 > (VMEM_LIMIT * 3) // 4 and tr % 16 == 0:
        tr //= 2

    def body(*refs):
        for i in range(k):
            w_ref, g_ref, m_ref, v_ref = refs[4 * i:4 * i + 4]
            delta_ref, nm_ref, nv_ref = refs[4 * k + 3 * i:4 * k + 3 * i + 3]
            delta_ref[...], nm_ref[...], nv_ref[...] = _adamw_update(w_ref[...], g_ref[...], m_ref[...], v_ref[...])

    spec = pl.BlockSpec((tr, tc), lambda i, j: (i, j))
    shape = jax.ShapeDtypeStruct((rows, cols), F32)
    outs = _pallas(body, name, (rows // tr, cols // tc), [spec] * (4 * k), [spec] * (3 * k), [shape] * (3 * k),
                   [a for quad in quads for a in quad], ("arbitrary", "arbitrary"), carried)
    return [tuple(outs[3 * i:3 * i + 3]) for i in range(k)]


def _adamw_update(w, g, m, v):
    nm = ADAM_B1 * m + (1.0 - ADAM_B1) * g
    nv = ADAM_B2 * v + (1.0 - ADAM_B2) * (g * g)
    m_hat = nm * (1.0 / (1.0 - ADAM_B1 ** ADAM_STEP))
    v_hat = nv * (1.0 / (1.0 - ADAM_B2 ** ADAM_STEP))
    return -ADAM_LR * (m_hat / (jnp.sqrt(v_hat) + ADAM_EPS) + ADAM_WD * w), nm, nv


def _adamw_whole(quads, name):
    k = len(quads)

    def body(*refs):
        for i in range(k):
            w_ref, g_ref, m_ref, v_ref = refs[4 * i:4 * i + 4]
            g_out, delta_ref, nm_ref, nv_ref = refs[4 * k + 4 * i:4 * k + 4 * i + 4]
            gv = g_ref[...]
            g_out[...] = gv
            delta_ref[...], nm_ref[...], nv_ref[...] = _adamw_update(w_ref[...], gv, m_ref[...], v_ref[...])

    vm = pl.BlockSpec(memory_space=pltpu.VMEM)
    outs = pl.pallas_call(body, name=name, in_specs=[vm] * (4 * k), out_specs=[vm] * (4 * k),
                          out_shape=[jax.ShapeDtypeStruct(q[0].shape, F32) for q in quads for _ in range(4)],
                          compiler_params=_params())(*[a for quad in quads for a in quad])
    return [tuple(outs[4 * i:4 * i + 4]) for i in range(k)]


def _adamw_small(packed_grads, triples, segments):
    k = len(triples)

    def body(*refs):
        g_ref = refs[0]
        off = 0
        for i in range(k):
            w_ref, m_ref, v_ref = refs[1 + 3 * i:4 + 3 * i]
            g_out, delta_ref, nm_ref, nv_ref = refs[1 + 3 * k + 4 * i:5 + 3 * k + 4 * i]
            gv = g_ref[:, off:off + w_ref.shape[1]]
            g_out[...] = gv
            delta_ref[...], nm_ref[...], nv_ref[...] = _adamw_update(w_ref[...], gv, m_ref[...], v_ref[...])
            off += segments[i]

    vm = pl.BlockSpec(memory_space=pltpu.VMEM)
    outs = pl.pallas_call(
        body, name="adamw_small", in_specs=[vm] * (1 + 3 * k), out_specs=[vm] * (4 * k),
        out_shape=[jax.ShapeDtypeStruct(w.shape, F32) for w, _, _ in triples for _ in range(4)],
    )(packed_grads, *[a for triple in triples for a in triple])
    return [tuple(outs[4 * i:4 * i + 4]) for i in range(k)]


def _place():
    x, y, c = lax.axis_index("x"), lax.axis_index("y"), lax.axis_index("c")
    other_chips = [(1 - x, y), (x, 1 - y), (1 - x, 1 - y)]
    return x, y, c, other_chips


def _remote(src, dst, sems, send, recv, device):
    return pltpu.make_async_remote_copy(src_ref=src, dst_ref=dst, send_sem=sems.at[send], recv_sem=sems.at[recv],
                                        device_id=device, device_id_type=MESH_ID)


def _cast_shards(shards, out_dtypes, n_first):
    n = len(shards)
    out_shape = [jax.ShapeDtypeStruct((N_CHIPS,) + s.shape, dt) for s, dt in zip(shards, out_dtypes)]
    gather = _gather_carried(out_shape[:n_first])

    def body(*refs):
        ins, outs, stage, sems = refs[:n], refs[n:2 * n], refs[2 * n:3 * n], refs[3 * n]
        x, y, _, _ = _place()
        me = 2 * x + y

        def cast(first, last):
            copies = []
            for w in range(first, last):
                stage[w][...] = ins[w][...].astype(out_dtypes[w])
                copies.append(pltpu.make_async_copy(stage[w], outs[w].at[me], sems.at[w]))
                copies[-1].start()
            for cp in copies:
                cp.wait()

        cast(0, n_first)
        gather.start(None, outs[:n_first], sems, n)
        cast(n_first, n)
        gather.finish(None, outs[:n_first], sems, n)

    vm = pl.BlockSpec(memory_space=pltpu.VMEM)
    return pl.pallas_call(
        body, name="cast_shards", in_specs=[vm] * n, out_specs=[ANY] * n, out_shape=out_shape,
        scratch_shapes=[pltpu.VMEM(s.shape, dt) for s, dt in zip(shards, out_dtypes)] + [pltpu.SemaphoreType.DMA((n + gather.n_sems,))],
        compiler_params=_params())(*shards)


BF16_ROWS = 16


def _split_rows(rows):
    return (rows // 2) % BF16_ROWS == 0


def _half_shape(rows, cols):
    return (rows // 2, cols) if _split_rows(rows) else (rows, cols // 2)


def _half(rows, cols, which):
    if _split_rows(rows):
        return (pl.ds(pl.multiple_of(which * (rows // 2), BF16_ROWS), rows // 2), slice(None))
    return (slice(None), pl.ds(pl.multiple_of(which * (cols // 2), 128), cols // 2))


def _gather_carried(bufs):
    n = len(bufs)

    def half(w, slot, which):
        _, rows, cols = bufs[w].shape
        return (slot,) + _half(rows, cols, which)

    def start(ins, outs, sems, base):
        x, y, c, other_chips = _place()
        me = 2 * x + y
        for w in range(n):
            mine = outs[w].at[half(w, me, c)]
            for p, (px, py) in enumerate(other_chips):
                _remote(mine, mine, sems, base + 12 * w + p, base + 12 * w + 3 + p, (px, py, c)).start()

    def finish(ins, outs, sems, base):
        x, y, c, other_chips = _place()
        me = 2 * x + y
        for w in range(n):
            for p, (px, py) in enumerate(other_chips):
                got = outs[w].at[half(w, 2 * px + py, c)]
                _remote(got, got, sems, base + 12 * w + p, base + 12 * w + 3 + p, (px, py, c)).wait_recv()
                _remote(got, got, sems, base + 12 * w + 6 + p, base + 12 * w + 9 + p, (x, y, 1 - c)).start()
        for w in range(n):
            mine = outs[w].at[half(w, me, c)]
            for p, (px, py) in enumerate(other_chips):
                got = outs[w].at[half(w, 2 * px + py, c)]
                theirs = outs[w].at[half(w, 2 * px + py, 1 - c)]
                _remote(got, theirs, sems, base + 12 * w + 6 + p, base + 12 * w + 9 + p, (x, y, 1 - c)).wait()
                _remote(mine, mine, sems, base + 12 * w + p, base + 12 * w + 3 + p, (px, py, c)).wait_send()

    shapes = [jax.ShapeDtypeStruct(b.shape, b.dtype) for b in bufs]
    return _Carried(bufs, shapes, {w: w for w in range(n)}, 12 * n, start, finish)


def _swap_carried(grads):
    n = len(grads)

    def copy(w, ins, outs, sems, base):
        x, y, c, _ = _place()
        _, rows, cols = grads[w].shape
        theirs = ins[w].at[(slice(None),) + _half(rows, cols, 1 - c)]
        return _remote(theirs, outs[w], sems, base + 2 * w, base + 2 * w + 1, (x, y, 1 - c))

    def start(ins, outs, sems, base):
        for w in range(n):
            copy(w, ins, outs, sems, base).start()

    def finish(ins, outs, sems, base):
        for w in range(n):
            copy(w, ins, outs, sems, base).wait()

    shapes = [jax.ShapeDtypeStruct((g.shape[0],) + _half_shape(*g.shape[1:]), F32) for g in grads]
    return _Carried(grads, shapes, {}, 2 * n, start, finish)


def _row_tile(rows):
    for cand in (512, 352, 256, 192, 128, 96, 64, 32, 16):
        if rows % cand == 0:
            return cand
    return rows


def _half_block_index(split_rows, tiles, i, core):
    return (core * tiles + i, 0) if split_rows else (i, core)


def _chip_partial(grad, other, place, name):
    nblk, hr, hc = other.shape
    by_rows = _split_rows(grad.shape[1])
    tr = _row_tile(hr)
    tiles = hr // tr

    def body(place_ref, g_ref, o_ref, own_ref, sum_bf_ref):
        s = g_ref[...] + o_ref[...]
        sum_bf_ref[...] = _bf(s)

        @pl.when(pl.program_id(1) == place_ref[0])
        def _():
            own_ref[...] = s

    grid_spec = pltpu.PrefetchScalarGridSpec(
        num_scalar_prefetch=1, grid=(tiles, nblk),
        in_specs=[pl.BlockSpec((None, tr, hc), lambda i, b, place_ref: (b,) + _half_block_index(by_rows, tiles, i, place_ref[1])),
                  pl.BlockSpec((None, tr, hc), lambda i, b, place_ref: (b, i, 0))],
        out_specs=[pl.BlockSpec((tr, hc), lambda i, b, place_ref: (i, 0)),
                   pl.BlockSpec((None, tr, hc), lambda i, b, place_ref: (b, i, 0))])
    return pl.pallas_call(body, name=name, grid_spec=grid_spec,
                          out_shape=[jax.ShapeDtypeStruct((hr, hc), F32), jax.ShapeDtypeStruct(other.shape, BF16)],
                          compiler_params=_params(("arbitrary", "arbitrary")))(place, grad, other)


def _chip_partial_small(grads, others, place):
    n = len(grads)

    def body(*refs):
        place_ref, g_refs, o_refs = refs[0], refs[1:1 + n], refs[1 + n:1 + 2 * n]
        own_refs, bf_refs = refs[1 + 2 * n:1 + 3 * n], refs[1 + 3 * n:]
        chip, core = place_ref[0], place_ref[1]
        for w in range(n):
            _, rows, cols = grads[w].shape
            half = _half(rows, cols, core)
            bf_refs[w][...] = _bf(g_refs[w][(slice(None),) + half] + o_refs[w][...])
            own_refs[w][...] = g_refs[w][(chip,) + half] + o_refs[w][chip]

    vm = pl.BlockSpec(memory_space=pltpu.VMEM)
    outs = pl.pallas_call(
        body, name="chip_partial_small", in_specs=[pl.BlockSpec(memory_space=pltpu.SMEM)] + [vm] * (2 * n), out_specs=[vm] * (2 * n),
        out_shape=[jax.ShapeDtypeStruct(o.shape[1:], F32) for o in others] + [jax.ShapeDtypeStruct(o.shape, BF16) for o in others],
        compiler_params=_params())(place, *grads, *others)
    return list(zip(outs[:n], outs[n:]))


def _chip_total_small(owns, receiveds, place, shapes):
    n = len(owns)

    def body(*refs):
        place_ref, own_refs, r_refs, out_refs = refs[0], refs[1:1 + n], refs[1 + n:1 + 2 * n], refs[1 + 2 * n:]
        chip, core = place_ref[0], place_ref[1]
        for w in range(n):
            r = [r_refs[w][(chip + k) % N_CHIPS].astype(F32) for k in (1, 2, 3)]
            out_refs[w][_half(*shapes[w], core)] = own_refs[w][...] + ((r[0] + r[1]) + r[2])

    vm = pl.BlockSpec(memory_space=pltpu.VMEM)
    return list(pl.pallas_call(
        body, name="chip_total_small", in_specs=[pl.BlockSpec(memory_space=pltpu.SMEM)] + [vm] * (2 * n), out_specs=[vm] * n,
        out_shape=[jax.ShapeDtypeStruct(tuple(s), F32) for s in shapes], compiler_params=_params())(place, *owns, *receiveds))


def _send_carried(partials):
    n = len(partials)

    def start(ins, outs, sems, base):
        x, y, c, other_chips = _place()
        me = 2 * x + y
        for w in range(n):
            for p, (px, py) in enumerate(other_chips):
                _remote(ins[w].at[2 * px + py], outs[w].at[me], sems, base + 6 * w + p, base + 6 * w + 3 + p, (px, py, c)).start()

    def finish(ins, outs, sems, base):
        x, y, c, other_chips = _place()
        for w in range(n):
            for p, (px, py) in enumerate(other_chips):
                _remote(ins[w].at[2 * px + py], outs[w].at[2 * px + py], sems, base + 6 * w + p, base + 6 * w + 3 + p,
                        (px, py, c)).wait()

    return _Carried(partials, [jax.ShapeDtypeStruct(p.shape, BF16) for p in partials], {}, 6 * n, start, finish)


def _chip_total(own, received, place, shape, name):
    hr, hc = own.shape
    by_rows = _split_rows(shape[0])
    tr = _row_tile(hr)
    tiles = hr // tr

    def body(place_ref, own_ref, r1_ref, r2_ref, r3_ref, out_ref):
        out_ref[...] = own_ref[...] + ((r1_ref[...].astype(F32) + r2_ref[...].astype(F32)) + r3_ref[...].astype(F32))

    def slot(k):
        return pl.BlockSpec((None, tr, hc), lambda i, place_ref: ((place_ref[0] + k) % N_CHIPS, i, 0))

    grid_spec = pltpu.PrefetchScalarGridSpec(
        num_scalar_prefetch=1, grid=(tiles,), in_specs=[pl.BlockSpec((tr, hc), lambda i, place_ref: (i, 0)), slot(1), slot(2), slot(3)],
        out_specs=pl.BlockSpec((tr, hc), lambda i, place_ref: _half_block_index(by_rows, tiles, i, place_ref[1])))
    return pl.pallas_call(body, name=name, grid_spec=grid_spec, out_shape=jax.ShapeDtypeStruct(tuple(shape), F32),
                          compiler_params=_params(("arbitrary",)))(place, own, received, received, received)


def _join_carried(totals):
    n = len(totals)

    def copy(w, outs, sems, base):
        x, y, c, _ = _place()
        mine = outs[w].at[_half(*totals[w].shape, c)]
        return _remote(mine, mine, sems, base + 2 * w, base + 2 * w + 1, (x, y, 1 - c))

    def start(ins, outs, sems, base):
        for w in range(n):
            copy(w, outs, sems, base).start()

    def finish(ins, outs, sems, base):
        for w in range(n):
            copy(w, outs, sems, base).wait()

    shapes = [jax.ShapeDtypeStruct(a.shape, F32) for a in totals]
    return _Carried(totals, shapes, {w: w for w in range(n)}, 2 * n, start, finish)


def _sum_devices(vec):
    rows, n = vec.shape

    def body(v_ref, out_ref, buf, send_sems, recv_sems):
        x, y, c, _ = _place()
        me = 4 * x + 2 * y + c
        buf[me] = v_ref[...]
        sends = []
        for k in range(1, N_DEV):
            peer = (1 - x if k & 4 else x, 1 - y if k & 2 else y, 1 - c if k & 1 else c)
            cp = pltpu.make_async_remote_copy(src_ref=v_ref, dst_ref=buf.at[me], send_sem=send_sems.at[k], recv_sem=recv_sems.at[k],
                                              device_id=peer, device_id_type=MESH_ID)
            cp.start()
            sends.append(cp)
        for cp in sends:
            cp.wait()
        total = buf[0]
        for dev in range(1, N_DEV):
            total = total + buf[dev]
        out_ref[...] = total

    vm = pl.BlockSpec(memory_space=pltpu.VMEM)
    return pl.pallas_call(
        body, name="sum_devices", in_specs=[vm], out_specs=vm, out_shape=jax.ShapeDtypeStruct((rows, n), F32),
        scratch_shapes=[pltpu.VMEM((N_DEV, rows, n), F32), pltpu.SemaphoreType.DMA((N_DEV,)), pltpu.SemaphoreType.DMA((N_DEV,))],
    )(vec)


def _rope_tables(positions):
    half = ROPE_HALF
    inv_freq = 1.0 / (ROPE_THETA ** (jnp.arange(half, dtype=F32) / half))
    ang = positions.astype(F32).reshape(-1, 1) * inv_freq
    cos, sin = jnp.cos(ang), jnp.sin(ang)
    t = ang.shape[0]
    ones, zeros = jnp.ones((t, QK_NOPE), F32), jnp.zeros((t, QK_NOPE), F32)
    pad, none = HEAD_PAD - QK_DIM, zeros[:, :half]
    cos_full = jnp.concatenate([ones, cos, cos, ones[:, :pad]], axis=1)
    s_lo = jnp.concatenate([zeros, -sin, none, zeros[:, :pad]], axis=1)
    s_hi = jnp.concatenate([zeros, none, sin, zeros[:, :pad]], axis=1)
    return cos_full, s_lo, s_hi


def _partials(names, grads, from_sibling, place):
    return [_chip_partial(g, o, place, "chip_partial_" + n) for n, g, o in zip(names, grads, from_sibling)]


def _totals(names, grads, partials, received, place):
    return [_chip_total(pf, r, place, g.shape[1:], "chip_total_" + n) for n, g, (pf, _), r in zip(names, grads, partials, received)]


def _kernel_layouts(full):
    w_in = full["w_in"]
    w_kr = jnp.pad(w_in[ROW_KR:ROW_XC], ((QK_NOPE, HEAD_PAD - QK_DIM), (0, 0)))
    w_uq = jnp.pad(full["w_uq"].reshape(Q_LORA, N_HEADS, QK_DIM), ((0, 0), (0, 0), (0, HEAD_PAD - QK_DIM)))
    w_uk = jnp.pad(full["w_uk"].reshape(KV_LORA, N_HEADS, QK_NOPE), ((0, 0), (0, 0), (0, HEAD_PAD - QK_NOPE)))
    return {"w_in": w_in, "w_kr": w_kr, "w_uq": w_uq.reshape(Q_LORA, N_HEADS * HEAD_PAD),
            "w_uk": w_uk.reshape(KV_LORA, N_HEADS * HEAD_PAD), "w_uv": full["w_uv"], "w_uvt": full["w_uv"].T}


def _global_layouts(g):
    d = D_MODEL
    dm, dp = g["w_m"], g["w_p"]
    o_lat = 2 * d
    o_kr = o_lat + Q_LORA + KV_LORA + QK_NOPE
    w_in = jnp.concatenate([dp[o_lat:o_lat + Q_LORA + KV_LORA], dp[o_kr:o_kr + QK_ROPE], dp[:d], dm[:d], dp[d:o_lat], dm[d:]], axis=0)
    w_uq = g["w_uq"].reshape(Q_LORA, N_HEADS, HEAD_PAD)[:, :, :QK_DIM].reshape(Q_LORA, N_HEADS * QK_DIM)
    w_uk = g["w_uk"].reshape(KV_LORA, N_HEADS, HEAD_PAD)[:, :, :QK_NOPE].reshape(KV_LORA, N_HEADS * QK_NOPE)
    return {"w_in": w_in, "w_uq": w_uq, "w_uk": w_uk, "w_uv": g["w_uv"], "w_proj_attn": g["w_pa"], "w_proj_conv": g["w_pc"],
            "w_out": g["w_out"]}


def _col_blocks(a):
    r, c = a.shape
    return a.reshape(r, N_CHIPS, c // N_CHIPS).transpose(1, 0, 2)


def _from_col_blocks(a):
    n, r, c = a.shape
    return a.transpose(1, 0, 2).reshape(r, n * c)


COL_SHARDED = ("w_uq", "w_uk", "w_uv", "w_proj_attn")
TRANSPOSED = ("ffn1_w_gate", "ffn1_w_up", "ffn2_w_gate", "ffn2_w_up", "w_in")
SMALL = (("ffn1_norm", 1024), ("mix_norm", 1024), ("gate_bias", 2048), ("q_a_norm", 384), ("kv_a_norm", 256),
         ("q_head_norm", 128), ("k_head_norm", 128), ("ffn2_norm", 1024))
WEIGHT_ORDER = ("ffn1_norm", "ffn1_w_gate", "ffn1_w_up", "ffn1_w_down", "mix_norm", "w_in", "gate_bias", "q_a_norm", "w_uq",
                "kv_a_norm", "w_uk", "w_uv", "q_head_norm", "k_head_norm", "w_proj_attn", "conv_w", "w_proj_conv", "w_out",
                "ffn2_norm", "ffn2_w_gate", "ffn2_w_up", "ffn2_w_down")
MATRICES = ("ffn1_w_gate", "ffn1_w_up", "ffn1_w_down", "w_in", "w_uq", "w_uk", "w_uv", "w_proj_attn", "w_proj_conv", "w_out",
            "ffn2_w_gate", "ffn2_w_up", "ffn2_w_down")
GROUP_FFN1 = ("ffn1_w_gate", "ffn1_w_up", "ffn1_w_down")
GROUP_IN = ("w_in", "w_uq", "w_uk", "w_uv", "conv_w")
GROUP_MIX = ("w_proj_attn", "w_proj_conv", "w_out")
GROUP_FFN2 = ("ffn2_w_gate", "ffn2_w_up", "ffn2_w_down")
GROUP_MID = ("w_in", "w_uq", "w_uk", "w_uv", "w_proj_attn", "w_proj_conv", "w_out")


def _pad_lanes(a, n):
    return jnp.pad(a.reshape(1, -1), ((0, 0), (0, n - a.size)))


def kernel(x, positions, ffn1_norm, ffn1_w_gate, ffn1_w_up, ffn1_w_down, mix_norm, w_in, gate_bias, q_a_norm, w_uq, kv_a_norm, w_uk, w_uv, q_head_norm, k_head_norm, w_proj_attn, conv_w, w_proj_conv, w_out, ffn2_norm, ffn2_w_gate, ffn2_w_up, ffn2_w_down, loss_target, m_ffn1_norm, m_ffn1_w_gate, m_ffn1_w_up, m_ffn1_w_down, m_mix_norm, m_w_in, m_gate_bias, m_q_a_norm, m_w_uq, m_kv_a_norm, m_w_uk, m_w_uv, m_q_head_norm, m_k_head_norm, m_w_proj_attn, m_conv_w, m_w_proj_conv, m_w_out, m_ffn2_norm, m_ffn2_w_gate, m_ffn2_w_up, m_ffn2_w_down, v_ffn1_norm, v_ffn1_w_gate, v_ffn1_w_up, v_ffn1_w_down, v_mix_norm, v_w_in, v_gate_bias, v_q_a_norm, v_w_uq, v_kv_a_norm, v_w_uk, v_w_uv, v_q_head_norm, v_k_head_norm, v_w_proj_attn, v_conv_w, v_w_proj_conv, v_w_out, v_ffn2_norm, v_ffn2_w_gate, v_ffn2_w_up, v_ffn2_w_down):
    args = dict(locals())
    view = lambda n, a: a.T if n in TRANSPOSED else a
    weights = {n: view(n, args[n]) for n in WEIGHT_ORDER}
    moments_m = {n: view(n, args["m_" + n]) for n in WEIGHT_ORDER}
    moments_v = {n: view(n, args["v_" + n]) for n in WEIGHT_ORDER}
    nb, seq, d = x.shape
    t = nb * seq
    chip = (2 * lax.axis_index("x") + lax.axis_index("y")).astype(jnp.int32)
    place = jnp.stack([chip, lax.axis_index("c").astype(jnp.int32)])
    grads, delta, new_m, new_v = {}, {}, {}, {}

    def adamw(names, carried=None):
        results = _adamw([(weights[n], grads[n], moments_m[n], moments_v[n]) for n in names], "adamw_" + names[0], carried)
        for n, (dn, mn, vn) in zip(names, results):
            delta[n], new_m[n], new_v[n] = dn, mn, vn

    conv_rows = conv_w.shape[0]
    conv_shard = jnp.pad(conv_w, ((0, 16 - conv_rows), (0, 0)))
    assert MATRICES[:len(GROUP_FFN1)] == GROUP_FFN1
    bufs = dict(zip(MATRICES + ("conv_w",), _cast_shards([weights[n] for n in MATRICES] + [conv_shard],
                                                         [BF16] * len(MATRICES) + [F32], len(GROUP_FFN1))))
    blocks = {n: bufs[n] for n in GROUP_FFN1}
    p = {n: _pad_lanes(weights[n], size) for n, size in SMALL}
    rope = _rope_tables(positions)
    x_tok = x.reshape(t, d)

    gather_in = _gather_carried([bufs[n] for n in GROUP_IN])
    x1, gate1, up1, act1 = _ffn_fwd(x_tok, p["ffn1_norm"], blocks["ffn1_w_gate"], blocks["ffn1_w_up"], blocks["ffn1_w_down"], None,
                                    "ffn1_fwd", gather_in)
    blocks.update(zip(GROUP_IN, gather_in.results))
    w = _kernel_layouts({"w_in": blocks["w_in"].reshape(-1, d), **{n: _from_col_blocks(blocks[n]) for n in ("w_uq", "w_uk", "w_uv")}})
    p["conv_w"] = _from_col_blocks(blocks["conv_w"])[:conv_rows]

    gather_mix = _gather_carried([bufs[n] for n in GROUP_MIX + GROUP_FFN2[2:]])
    h2b, big, lat, q, k, v, vt = _inproj_fwd(x1, p["mix_norm"], w["w_in"], w["w_kr"], p["q_a_norm"], p["kv_a_norm"], p["q_head_norm"],
                                             p["k_head_norm"], w["w_uq"], w["w_uk"], w["w_uv"], w["w_uvt"], rope, gather_mix)
    blocks.update(zip(GROUP_MIX + GROUP_FFN2[2:], gather_mix.results))
    w_pa = _from_col_blocks(blocks["w_proj_attn"])
    w_pc, w_out_full = blocks["w_proj_conv"].reshape(-1, d), blocks["w_out"].reshape(-1, d)

    gather_ffn2 = _gather_carried([bufs[n] for n in GROUP_FFN2[:2]])
    o, lse = _attn_fwd(q, k, vt, seq, gather_ffn2)
    x2 = _mix_fwd(x1, o, big, p["gate_bias"], p["conv_w"], w_pa, w_pc, w_out_full, seq)
    wg2, wu2 = gather_ffn2.results
    wd2 = blocks["ffn2_w_down"]
    dx3, gate2, up2, act2, loss = _ffn_fwd(x2, p["ffn2_norm"], wg2, wu2, wd2, loss_target.reshape(t, d), "ffn2_fwd")

    dx2, dg_ffn2, hb2, dgate2, dup2, dyb2 = _ffn_bwd_x(x2, p["ffn2_norm"], dx3, gate2, up2, wg2, wu2, wd2, "ffn2_bwd")
    g_ffn2 = [_tn_matmul(dgate2, hb2, "ffn2_dw_gate"), _tn_matmul(dup2, hb2, "ffn2_dw_up"), _tn_matmul(act2, dyb2, "ffn2_dw_down")]
    swap = _swap_carried(g_ffn2)
    do, delta_o, dz, dm, dbias, dw_pa, dw_pc, dw_out = _mix_bwd(dx2, o, big, p["gate_bias"], p["conv_w"], w_pa, w_pc, w_out_full, seq,
                                                                swap)
    part = _partials(GROUP_FFN2, g_ffn2, swap.results, place)
    send = _send_carried([pb for _, pb in part])
    dq, dk, dv = _attn_bwd(q, k, v, do, lse, delta_o.reshape(N_HEADS // ATTN_BWD_HEADS, ATTN_BWD_HEADS, -1), seq, send)
    join = _join_carried(_totals(GROUP_FFN2, g_ffn2, part, send.results, place))
    dp, dw_uq, dw_uk, dw_uv, dqa, dkva, dqh, dkh, dcw = _prep_bwd(
        lat, big, dz, dq, dk, dv, p["q_a_norm"], p["kv_a_norm"], p["q_head_norm"], p["k_head_norm"], w["w_uq"], w["w_uk"],
        w["w_uv"], rope, p["conv_w"], seq, join)
    grads.update(zip(GROUP_FFN2, join.results))

    gg = _global_layouts({"w_m": _tn_matmul(dm, h2b, "dw_in_m", split_k=2), "w_p": _tn_matmul(dp, h2b, "dw_in_p", split_k=2),
                          "w_uq": dw_uq, "w_uk": dw_uk, "w_uv": dw_uv, "w_pa": dw_pa, "w_pc": dw_pc, "w_out": dw_out})
    g_mid = [_col_blocks(gg[n]) if n in COL_SHARDED else gg[n].reshape(N_CHIPS, -1, gg[n].shape[-1]) for n in GROUP_MID]
    swap = _swap_carried(g_mid)
    dx1, dg_mix = _inproj_bwd(x1, p["mix_norm"], dx2, dm, dp, w["w_in"], w["w_kr"], swap)
    part = (_partials(GROUP_MID[:1], g_mid[:1], swap.results[:1], place)
            + _chip_partial_small(g_mid[1:], swap.results[1:], place))
    send = _send_carried([pb for _, pb in part])
    grad_x, dg_ffn1, hb1, dgate1, dup1, dyb1 = _ffn_bwd_x(x_tok, p["ffn1_norm"], dx1, gate1, up1, blocks["ffn1_w_gate"],
                                                         blocks["ffn1_w_up"], blocks["ffn1_w_down"], "ffn1_bwd", send)

    small_grads = {"ffn1_norm": dg_ffn1, "mix_norm": dg_mix, "gate_bias": dbias, "q_a_norm": dqa, "kv_a_norm": dkva,
                   "q_head_norm": dqh, "k_head_norm": dkh, "ffn2_norm": dg_ffn2}
    packed = jnp.concatenate([small_grads[n] for n, _ in SMALL] + [dcw.reshape(1, -1), loss], axis=1)
    total = _sum_devices(packed.reshape(8, -1)).reshape(1, -1)
    n_small = sum(size for _, size in SMALL)
    conv_cols = conv_w.shape[1]
    conv_total = total[:, n_small:n_small + conv_rows * d].reshape(conv_rows, d)
    grads["conv_w"] = lax.dynamic_slice_in_dim(conv_total, chip * conv_cols, conv_cols, axis=1)
    loss_total = total[0, n_small + conv_rows * d]

    join = _join_carried(_totals(GROUP_MID[:1], g_mid[:1], part[:1], send.results[:1], place)
                         + _chip_total_small([pf for pf, _ in part[1:]], send.results[1:], place, [g.shape[1:] for g in g_mid[1:]]))
    g_gate = _tn_matmul(dgate1, hb1, "ffn1_dw_gate", carried=join)
    grads.update(zip(GROUP_MID, join.results))
    swap_gate = _swap_carried([g_gate])
    g_up = _tn_matmul(dup1, hb1, "ffn1_dw_up", carried=swap_gate)
    part_gate = _partials(GROUP_FFN1[:1], [g_gate], swap_gate.results, place)
    send_gate, swap_up = _send_carried([part_gate[0][1]]), _swap_carried([g_up])
    g_down = _tn_matmul(act1, dyb1, "ffn1_dw_down", carried=_both(send_gate, swap_up))
    join_gate = _join_carried(_totals(GROUP_FFN1[:1], [g_gate], part_gate, send_gate.results, place))
    part_up = _partials(GROUP_FFN1[1:2], [g_up], swap_up.results, place)
    send_up, swap_down = _send_carried([part_up[0][1]]), _swap_carried([g_down])
    adamw(GROUP_FFN2, _both(_both(send_up, swap_down), join_gate))
    grads["ffn1_w_gate"] = join_gate.results[0]
    join_up = _join_carried(_totals(GROUP_FFN1[1:2], [g_up], part_up, send_up.results, place))
    part_down = _partials(GROUP_FFN1[2:], [g_down], swap_down.results, place)
    send_down = _send_carried([part_down[0][1]])
    adamw(("w_in",), _both(send_down, join_up))
    grads["ffn1_w_up"] = join_up.results[0]
    join_down = _join_carried(_totals(GROUP_FFN1[2:], [g_down], part_down, send_down.results, place))
    adamw(GROUP_FFN1[:2], join_down)
    grads["ffn1_w_down"] = join_down.results[0]
    adamw(GROUP_FFN1[2:])
    others = GROUP_MID[1:] + ("conv_w",)
    for n, (gn, dn, mn, vn) in zip(others, _adamw_whole([(weights[n], grads[n], moments_m[n], moments_v[n]) for n in others],
                                                        "adamw_others")):
        grads[n], delta[n], new_m[n], new_v[n] = gn, dn, mn, vn

    row = lambda a: a.reshape(1, -1)
    small = _adamw_small(total, [(row(weights[n]), row(moments_m[n]), row(moments_v[n])) for n, _ in SMALL], [size for _, size in SMALL])
    for (n, _), (gn, dn, mn, vn) in zip(SMALL, small):
        grads[n], delta[n], new_m[n], new_v[n] = gn.reshape(-1), dn.reshape(-1), mn.reshape(-1), vn.reshape(-1)

    return (loss_total, grad_x.reshape(nb, seq, d), *[view(n, src[n]) for src in (grads, delta, new_m, new_v) for n in WEIGHT_ORDER])
```

```python
import functools

import jax
import jax.numpy as jnp
from jax import lax
from jax.experimental import pallas as pl
from jax.experimental.pallas import tpu as pltpu

F32 = jnp.float32
BF16 = jnp.bfloat16

D_MODEL = 1024
N_HEADS = 8
QK_NOPE = 64
QK_ROPE = 32
QK_DIM = QK_NOPE + QK_ROPE
V_DIM = 64
HEAD_PAD = 128
Q_LORA = 384
KV_LORA = 256
ROPE_THETA = 10000.0
NORM_EPS = 1e-6
ATTN_SCALE = QK_DIM ** -0.5
MASK_VALUE = -1e30
N_CHIPS = 4
N_DEV = 8

ADAM_LR = 0.001
ADAM_B1 = 0.9
ADAM_B2 = 0.999
ADAM_EPS = 1e-08
ADAM_WD = 0.01
ADAM_STEP = 10

TOKEN_TILE = 256
PREP_TILE = 256
ATTN_TILE = 512
TN_TILE = 2048
VMEM_LIMIT = 56 * 1024 * 1024

M_COLS = 3 * D_MODEL
P_COLS = 2 * D_MODEL + Q_LORA + KV_LORA + HEAD_PAD
BIG_COLS = 5 * D_MODEL
LAT_COLS = Q_LORA + KV_LORA + HEAD_PAD

MESH_ID = pl.DeviceIdType.MESH
ANY = pl.BlockSpec(memory_space=pl.ANY)


def _params(semantics=None):
    return pltpu.CompilerParams(dimension_semantics=semantics, vmem_limit_bytes=VMEM_LIMIT)


class _Carried:
    def __init__(self, operands, out_shapes, aliases, n_sems, start, finish):
        self.operands, self.out_shapes, self.aliases, self.n_sems = list(operands), list(out_shapes), dict(aliases), n_sems
        self.start, self.finish = start, finish
        self.results = None


def _both(a, b):
    na, nao = len(a.operands), len(a.out_shapes)

    def start(ins, outs, sems, base):
        a.start(ins[:na], outs[:nao], sems, base)
        b.start(ins[na:], outs[nao:], sems, base + a.n_sems)

    def finish(ins, outs, sems, base):
        a.finish(ins[:na], outs[:nao], sems, base)
        b.finish(ins[na:], outs[nao:], sems, base + a.n_sems)

    aliases = dict(a.aliases)
    aliases.update({na + i: nao + o for i, o in b.aliases.items()})
    both = _Carried(a.operands + b.operands, a.out_shapes + b.out_shapes, aliases, a.n_sems + b.n_sems, start, finish)
    both.parts = (a, b)
    return both


def _set_results(carried, results):
    carried.results = list(results)
    if hasattr(carried, "parts"):
        a, b = carried.parts
        _set_results(a, results[:len(a.out_shapes)])
        _set_results(b, results[len(a.out_shapes):])


def _pallas(body, name, grid, in_specs, out_specs, out_shape, args, semantics, carried=None):
    if carried is None:
        return pl.pallas_call(body, name=name, grid=grid, in_specs=in_specs, out_specs=out_specs, out_shape=out_shape,
                              compiler_params=_params(semantics))(*args)
    n_in, n_out, n_ci, n_co = len(in_specs), len(out_specs), len(carried.operands), len(carried.out_shapes)

    def wrapped(*refs):
        ins, c_ins = refs[:n_in], refs[n_in:n_in + n_ci]
        outs, c_outs = refs[n_in + n_ci:n_in + n_ci + n_out], refs[n_in + n_ci + n_out:n_in + n_ci + n_out + n_co]
        sems = refs[-1]
        first = pl.program_id(0) == 0
        last = pl.program_id(0) == grid[0] - 1
        for axis in range(1, len(grid)):
            first = jnp.logical_and(first, pl.program_id(axis) == 0)
            last = jnp.logical_and(last, pl.program_id(axis) == grid[axis] - 1)

        @pl.when(first)
        def _():
            carried.start(c_ins, c_outs, sems, 0)

        body(*ins, *outs)

        @pl.when(last)
        def _():
            carried.finish(c_ins, c_outs, sems, 0)

    results = pl.pallas_call(
        wrapped, name=name, grid=grid, in_specs=list(in_specs) + [ANY] * n_ci, out_specs=list(out_specs) + [ANY] * n_co,
        out_shape=list(out_shape) + carried.out_shapes,
        input_output_aliases={n_in + i: n_out + o for i, o in carried.aliases.items()},
        scratch_shapes=[pltpu.SemaphoreType.DMA((carried.n_sems,))], compiler_params=_params(semantics))(*args, *carried.operands)
    _set_results(carried, results[n_out:])
    return results[:n_out]


def _resident(shape):
    nd = len(shape)
    return pl.BlockSpec(shape, lambda *_: (0,) * nd, pipeline_mode=pl.Buffered(1))


def _const(shape):
    nd = len(shape)
    return pl.BlockSpec(shape, lambda *_: (0,) * nd)


def _mm(a, b):
    return jnp.dot(a, b, preferred_element_type=F32)


def _mm_nt(a, b):
    return lax.dot_general(a, b, (((1,), (1,)), ((), ())), preferred_element_type=F32)


def _mm_tn(a, b):
    return lax.dot_general(a, b, (((0,), (0,)), ((), ())), preferred_element_type=F32)


def _bf(a):
    return a.astype(BF16)


def _sigmoid(a):
    return 1.0 / (1.0 + jnp.exp(-a))


def _rms(x, gain, n=None):
    n = x.shape[-1] if n is None else n
    r = lax.rsqrt(jnp.sum(x * x, axis=-1, keepdims=True) * (1.0 / n) + NORM_EPS)
    return (x * r) * gain, r


def _rms_bwd(x, r, gain, dh, n=None):
    n = x.shape[-1] if n is None else n
    u = dh * gain
    dx = r * u - x * ((r * r * r) * (jnp.sum(u * x, axis=-1, keepdims=True) * (1.0 / n)))
    dgain = jnp.sum(dh * (x * r), axis=0, keepdims=True)
    return dx, dgain


ROPE_HALF = QK_ROPE // 2


def _rope(t, rope):
    cos, s_lo, s_hi = rope
    return t * cos + pltpu.roll(t, HEAD_PAD - ROPE_HALF, 1) * s_lo + pltpu.roll(t, ROPE_HALF, 1) * s_hi


def _rope_bwd(dt, rope):
    cos, s_lo, s_hi = rope
    return dt * cos + pltpu.roll(dt * s_lo, ROPE_HALF, 1) + pltpu.roll(dt * s_hi, HEAD_PAD - ROPE_HALF, 1)


def _shift_down(u, prev8, k):
    s = pltpu.roll(u, k, 0)
    p = pltpu.roll(prev8, k, 0)
    row = lax.broadcasted_iota(jnp.int32, prev8.shape, 0)
    top = jnp.where(row < k, p, s[:8])
    return jnp.concatenate([top, s[8:]], axis=0)


def _shift_up(d, next8, k):
    tm = d.shape[0]
    s = pltpu.roll(d, tm - k, 0)
    n = pltpu.roll(next8, 8 - k, 0)
    row = lax.broadcasted_iota(jnp.int32, next8.shape, 0)
    bot = jnp.where(row >= 8 - k, n, s[tm - 8:])
    return jnp.concatenate([s[:tm - 8], bot], axis=0)


def _ffn_fwd(x, gain, wg, wu, wd, target, name, carried=None):
    t, d = x.shape
    nb, f, _ = wg.shape
    tm = TOKEN_TILE
    with_loss = target is not None

    def body(*refs):
        if with_loss:
            x_ref, g_ref, wg_ref, wu_ref, wd_ref, t_ref, out_ref, gate_ref, up_ref, act_ref, loss_ref = refs
        else:
            x_ref, g_ref, wg_ref, wu_ref, wd_ref, out_ref, gate_ref, up_ref, act_ref = refs
        xv = x_ref[...]
        h, _ = _rms(xv, g_ref[...])
        hb = _bf(h)
        y = jnp.zeros((tm, d), F32)
        nxt = (_mm_nt(hb, wg_ref[0]), _mm_nt(hb, wu_ref[0]))
        for j in range(nb):
            gate, up = nxt
            if j + 1 < nb:
                nxt = (_mm_nt(hb, wg_ref[j + 1]), _mm_nt(hb, wu_ref[j + 1]))
            act = _bf((gate * _sigmoid(gate)) * up)
            y = y + _mm(act, wd_ref[j])
            gate_ref[j] = _bf(gate)
            up_ref[j] = _bf(up)
            act_ref[j] = act
        out = xv + 0.5 * y
        if with_loss:
            err = out - t_ref[...]
            out_ref[...] = err * (1.0 / d)

            @pl.when(pl.program_id(0) == 0)
            def _():
                loss_ref[...] = jnp.zeros_like(loss_ref)

            part = jnp.sum(jnp.sum(err * err, axis=1, keepdims=True), axis=0, keepdims=True)
            loss_ref[...] += jnp.broadcast_to(part * (0.5 / d), loss_ref.shape)
        else:
            out_ref[...] = out

    tok = pl.BlockSpec((tm, d), lambda i: (i, 0))
    blk = pl.BlockSpec((nb, tm, f), lambda i: (0, i, 0))
    in_specs = [tok, _const((1, d)), _resident(wg.shape), _resident(wu.shape), _resident(wd.shape)]
    args = [x, gain, wg, wu, wd]
    out_shape = [jax.ShapeDtypeStruct((t, d), F32)] + [jax.ShapeDtypeStruct((nb, t, f), BF16)] * 3
    out_specs = [tok, blk, blk, blk]
    if with_loss:
        in_specs.append(tok)
        args.append(target)
        out_shape.append(jax.ShapeDtypeStruct((1, 128), F32))
        out_specs.append(_const((1, 128)))
    return _pallas(body, name, (t // tm,), in_specs, out_specs, out_shape, args, ("arbitrary",), carried)


def _ffn_bwd_x(x, gain, dout, gate, up, wg, wu, wd, name, carried=None):
    t, d = x.shape
    nb, f, _ = wg.shape
    tm = TOKEN_TILE

    def body(x_ref, g_ref, dout_ref, gate_ref, up_ref, wg_ref, wu_ref, wd_ref,
             dx_ref, dgain_ref, hb_ref, dgate_ref, dup_ref, dyb_ref):
        xv = x_ref[...]
        gain_v = g_ref[...]
        h, r = _rms(xv, gain_v)
        hb_ref[...] = _bf(h)
        dout_v = dout_ref[...]
        dyb = _bf(0.5 * dout_v)
        dyb_ref[...] = dyb
        dh = jnp.zeros((tm, d), F32)
        nxt = _mm_nt(dyb, wd_ref[0])
        for j in range(nb):
            dact = nxt
            if j + 1 < nb:
                nxt = _mm_nt(dyb, wd_ref[j + 1])
            gt = gate_ref[j].astype(F32)
            uv = up_ref[j].astype(F32)
            s = _sigmoid(gt)
            dup = _bf(dact * (gt * s))
            dgate = _bf((dact * uv) * (s * (1.0 + gt * (1.0 - s))))
            dh = dh + _mm(dgate, wg_ref[j]) + _mm(dup, wu_ref[j])
            dgate_ref[j] = dgate
            dup_ref[j] = dup
        dxn, dgain = _rms_bwd(xv, r, gain_v, dh)
        dx_ref[...] = dout_v + dxn

        @pl.when(pl.program_id(0) == 0)
        def _():
            dgain_ref[...] = jnp.zeros_like(dgain_ref)

        dgain_ref[...] += dgain

    tok = pl.BlockSpec((tm, d), lambda i: (i, 0))
    blk = pl.BlockSpec((nb, tm, f), lambda i: (0, i, 0))
    return _pallas(
        body, name, (t // tm,),
        [tok, _const((1, d)), tok, blk, blk, _resident(wg.shape), _resident(wu.shape), _resident(wd.shape)],
        [tok, _const((1, d)), tok, blk, blk, tok],
        [jax.ShapeDtypeStruct((t, d), F32), jax.ShapeDtypeStruct((1, d), F32), jax.ShapeDtypeStruct((t, d), BF16),
         jax.ShapeDtypeStruct((nb, t, f), BF16), jax.ShapeDtypeStruct((nb, t, f), BF16), jax.ShapeDtypeStruct((t, d), BF16)],
        (x, gain, dout, gate, up, wg, wu, wd), ("arbitrary",), carried)


def _tn_matmul(a, b, name, carried=None):
    t = a.shape[-2]
    k = a.shape[-1]
    n = b.shape[-1]
    tt = min(TN_TILE, t)
    nt = t // tt

    def body(a_ref, b_ref, o_ref):
        @pl.when(pl.program_id(1) == 0)
        def _():
            o_ref[...] = jnp.zeros_like(o_ref)

        o_ref[...] += _mm_tn(a_ref[...], b_ref[...])

    g = a.shape[0] if a.ndim == 3 else b.shape[0]
    a_spec = (pl.BlockSpec((None, tt, k), lambda gi, ti: (gi, ti, 0)) if a.ndim == 3
              else pl.BlockSpec((tt, k), lambda gi, ti: (ti, 0)))
    b_spec = (pl.BlockSpec((None, tt, n), lambda gi, ti: (gi, ti, 0)) if b.ndim == 3
              else pl.BlockSpec((tt, n), lambda gi, ti: (ti, 0)))
    o_spec = pl.BlockSpec((None, k, n), lambda gi, ti: (gi, 0, 0))
    out_shape = jax.ShapeDtypeStruct((g, k, n), F32)
    return _pallas(body, name, (g, nt), [a_spec, b_spec], [o_spec], [out_shape], (a, b), ("arbitrary", "arbitrary"), carried)[0]


def _tn_rows(a, b, out, chunks, width, rows_out, name):
    t, n = b.shape
    tt = min(TN_TILE, t)
    nt = t // tt

    def body(blocks_ref, a_ref, b_ref, *rest):
        out_ref, acc, sem = rest[-3:]
        g, ti = pl.program_id(0), pl.program_id(1)

        @pl.when(ti == 0)
        def _():
            acc[...] = jnp.zeros_like(acc)

        acc[...] += _mm_tn(a_ref[...], b_ref[...])
        for gi, (_, row, first, count) in enumerate(chunks):
            @pl.when(jnp.logical_and(g == gi, ti == nt - 1))
            def _(row=row, first=first, count=count):
                cp = pltpu.make_async_copy(acc.at[first:first + count], out_ref.at[row:row + count], sem)
                cp.start()
                cp.wait()

    blocks = jnp.asarray([c[0] for c in chunks], jnp.int32)
    grid_spec = pltpu.PrefetchScalarGridSpec(
        num_scalar_prefetch=1, grid=(len(chunks), nt),
        in_specs=[pl.BlockSpec((tt, width), lambda g, ti, blocks_ref: (ti, blocks_ref[g])),
                  pl.BlockSpec((tt, n), lambda g, ti, blocks_ref: (ti, 0))] + ([ANY] if out is not None else []),
        out_specs=ANY, scratch_shapes=[pltpu.VMEM((width, n), F32), pltpu.SemaphoreType.DMA])
    args = (blocks, a, b) + ((out,) if out is not None else ())
    return pl.pallas_call(body, name=name, grid_spec=grid_spec, out_shape=jax.ShapeDtypeStruct((rows_out, n), F32),
                          input_output_aliases={3: 0} if out is not None else {},
                          compiler_params=_params(("arbitrary", "arbitrary")))(*args)


ROW_QKV, ROW_KR, ROW_XC = 0, Q_LORA + KV_LORA, Q_LORA + KV_LORA + QK_ROPE
ROW_GB, ROW_GC, ROW_GL = ROW_XC + D_MODEL, ROW_XC + 2 * D_MODEL, ROW_XC + 3 * D_MODEL
BIG_FROM_ROWS = ((0, ROW_GB, D_MODEL), (D_MODEL, ROW_GL, 2 * D_MODEL), (3 * D_MODEL, ROW_XC, D_MODEL), (4 * D_MODEL, ROW_GC, D_MODEL))


def _inproj_fwd(x1, gain, w_in, w_kr, qa_gain, kva_gain, qh_gain, kh_gain, w_uq, w_uk, w_uv, w_uvt, rope, carried=None):
    t, d = x1.shape
    tm = TOKEN_TILE
    chunk = 512
    chunks = []
    for col, row, size in BIG_FROM_ROWS:
        chunks += [(col + o, row + o, chunk) for o in range(0, size, chunk)]
    of_head = [[c for k, c in enumerate(chunks) if k * N_HEADS // len(chunks) == hd] for hd in range(N_HEADS)]

    def body(x_ref, g_ref, win_ref, wkr_ref, qa_ref, kva_ref, qh_ref, kh_ref, wuq_ref, wuk_ref, wuv_ref, wuvt_ref, cos_ref, slo_ref,
             shi_ref, hb_ref, big_ref, lat_ref, q_ref, k_ref, v_ref, vt_ref):
        h, _ = _rms(x_ref[...], g_ref[...])
        hb = _bf(h)
        hb_ref[...] = hb
        k_rope = _mm_nt(hb, wkr_ref[...])
        lat = jnp.concatenate([_mm_nt(hb, win_ref[ROW_QKV:ROW_KR, :]), k_rope], axis=1)
        lat_ref[...] = lat
        cq, _ = _rms(lat[:, :Q_LORA], qa_ref[...])
        ckv, _ = _rms(lat[:, Q_LORA:Q_LORA + KV_LORA], kva_ref[...])
        cqb = _bf(cq)
        ckvb = _bf(ckv)
        rope_v = (cos_ref[...], slo_ref[...], shi_ref[...])
        q_all = _mm(cqb, wuq_ref[...])
        k_all = _mm(ckvb, wuk_ref[...])
        v_ref[...] = _bf(_mm(ckvb, wuv_ref[...]))
        vt_all = _mm_nt(wuvt_ref[...], ckvb)
        for hd in range(N_HEADS):
            for col, row, size in of_head[hd]:
                big_ref[:, col:col + size] = _mm_nt(hb, win_ref[row:row + size, :])
            lanes = slice(hd * HEAD_PAD, (hd + 1) * HEAD_PAD)
            qn, _ = _rms(q_all[:, lanes], qh_ref[...], QK_DIM)
            q_ref[hd] = _bf(_rope(qn, rope_v))
            kn, _ = _rms(k_all[:, lanes] + k_rope, kh_ref[...], QK_DIM)
            k_ref[hd] = _bf(_rope(kn, rope_v))
            vt_ref[hd] = _bf(vt_all[hd * V_DIM:(hd + 1) * V_DIM])

    tok = lambda c: pl.BlockSpec((tm, c), lambda i: (i, 0))
    head = lambda c: pl.BlockSpec((N_HEADS, tm, c), lambda i: (0, i, 0))
    return _pallas(
        body, "inproj_fwd", (t // tm,),
        [tok(d), _const((1, d)), _resident(w_in.shape), _resident(w_kr.shape), _const((1, Q_LORA)), _const((1, KV_LORA)),
         _const((1, HEAD_PAD)), _const((1, HEAD_PAD)), _resident(w_uq.shape), _resident(w_uk.shape),
         _resident(w_uv.shape), _resident(w_uvt.shape), tok(HEAD_PAD), tok(HEAD_PAD), tok(HEAD_PAD)],
        [tok(d), tok(BIG_COLS), tok(LAT_COLS), head(HEAD_PAD), head(HEAD_PAD), tok(N_HEADS * V_DIM),
         pl.BlockSpec((N_HEADS, V_DIM, tm), lambda i: (0, 0, i))],
        [jax.ShapeDtypeStruct((t, d), BF16), jax.ShapeDtypeStruct((t, BIG_COLS), F32),
         jax.ShapeDtypeStruct((t, LAT_COLS), F32), jax.ShapeDtypeStruct((N_HEADS, t, HEAD_PAD), BF16),
         jax.ShapeDtypeStruct((N_HEADS, t, HEAD_PAD), BF16), jax.ShapeDtypeStruct((t, N_HEADS * V_DIM), BF16),
         jax.ShapeDtypeStruct((N_HEADS, V_DIM, t), BF16)],
        (x1, gain, w_in, w_kr, qa_gain, kva_gain, qh_gain, kh_gain, w_uq, w_uk, w_uv, w_uvt, *rope), ("arbitrary",), carried)


EXP2_SCALE = ATTN_SCALE * 1.4426950408889634


def _diagonal_keep(tk, tq):
    return lax.broadcasted_iota(jnp.int32, (tk, tq), 0) <= lax.broadcasted_iota(jnp.int32, (tk, tq), 1)


def _attn_fwd(q, k, vt, seq, carried=None):
    _, t, _ = q.shape
    nseq = t // seq
    tq = tk = ATTN_TILE
    nq = seq // tq

    def body(q_ref, k_ref, vt_ref, o_ref, lse_ref):
        i = pl.program_id(1)
        qs = [q_ref[h] for h in range(N_HEADS)]
        keep = _diagonal_keep(tk, tq)

        def scores(h, k0):
            return _mm_nt(k_ref[h, pl.ds(k0, tk), :], qs[h])

        def update(h, st, state, k0, diagonal):
            m, l, acc = state
            if diagonal:
                st = jnp.where(keep, st, MASK_VALUE)
            m_new = jnp.maximum(m, jnp.max(st, axis=0, keepdims=True))
            pt = jnp.exp2((st - m_new) * EXP2_SCALE)
            alpha = jnp.exp2((m - m_new) * EXP2_SCALE)
            l_new = alpha * l + jnp.sum(pt, axis=0, keepdims=True)
            return m_new, l_new, alpha * acc + _mm(vt_ref[h, :, pl.ds(k0, tk)], _bf(pt))

        def tiles(states, k0, diagonal):
            st, new = scores(0, k0), []
            for h in range(N_HEADS):
                st_next = scores(h + 1, k0) if h + 1 < N_HEADS else None
                new.append(update(h, st, states[h], k0, diagonal))
                st = st_next
            return tuple(new)

        init = tuple((jnp.full((1, tq), MASK_VALUE, F32), jnp.zeros((1, tq), F32), jnp.zeros((V_DIM, tq), F32))
                     for _ in range(N_HEADS))
        states = lax.fori_loop(0, i, lambda j, s: tiles(s, pl.multiple_of(j * tk, tk), False), init)
        states = tiles(states, pl.multiple_of(i * tk, tk), True)
        outs = []
        for h in range(N_HEADS):
            m, l, acc = states[h]
            outs.append((acc / l).T)
            lse_ref[h] = m * EXP2_SCALE + jnp.log2(l)
        o_ref[...] = _bf(jnp.concatenate(outs, axis=-1))

    return _pallas(
        body, "attn_fwd", (nseq, nq),
        [pl.BlockSpec((N_HEADS, tq, HEAD_PAD), lambda b, i: (0, b * nq + i, 0)),
         pl.BlockSpec((N_HEADS, seq, HEAD_PAD), lambda b, i: (0, b, 0)),
         pl.BlockSpec((N_HEADS, V_DIM, seq), lambda b, i: (0, 0, b))],
        [pl.BlockSpec((tq, N_HEADS * V_DIM), lambda b, i: (b * nq + i, 0)),
         pl.BlockSpec((N_HEADS, 1, tq), lambda b, i: (0, 0, b * nq + i))],
        [jax.ShapeDtypeStruct((t, N_HEADS * V_DIM), BF16), jax.ShapeDtypeStruct((N_HEADS, 1, t), F32)],
        (q, k, vt), ("arbitrary", "arbitrary"), carried)


ATTN_BWD_HEADS = 4


def _attn_bwd(q, k, v, do, lse, delta, seq, carried=None):
    _, t, _ = q.shape
    nseq = t // seq
    tq = tk = ATTN_TILE
    n = seq // tq
    hb = ATTN_BWD_HEADS

    def body(q_ref, k_ref, v_ref, do_ref, lse_ref, delta_ref, dq_ref, dk_ref, dv_ref):
        dq_ref[...] = jnp.zeros_like(dq_ref)
        dk_ref[...] = jnp.zeros_like(dk_ref)
        dv_ref[...] = jnp.zeros_like(dv_ref)
        keep = _diagonal_keep(tk, tq)

        def tile(h, k0, q0, diagonal):
            kj = k_ref[h, pl.ds(k0, tk), :]
            qi = q_ref[h, pl.ds(q0, tq), :]
            doi = _bf(do_ref[pl.ds(q0, tq), h * V_DIM:(h + 1) * V_DIM])
            st = _mm_nt(kj, qi)
            if diagonal:
                st = jnp.where(keep, st, MASK_VALUE)
            pt = jnp.exp2(st * EXP2_SCALE - lse_ref[h, :, pl.ds(q0, tq)])
            dv_ref[pl.ds(k0, tk), h * V_DIM:(h + 1) * V_DIM] += _mm(_bf(pt), doi)
            dpt = _mm_nt(v_ref[pl.ds(k0, tk), h * V_DIM:(h + 1) * V_DIM], doi)
            dst = _bf((pt * (dpt - delta_ref[pl.ds(h, 1), pl.ds(q0, tq)])) * ATTN_SCALE)
            dk_ref[h, pl.ds(k0, tk), :] += _mm(dst, qi)
            dq_ref[h, pl.ds(q0, tq), :] += _mm_tn(dst, kj)

        def kv_step(j, _):
            k0 = pl.multiple_of(j * tk, tk)
            for h in range(hb):
                tile(h, k0, k0, True)

            def q_step(i, _):
                q0 = pl.multiple_of(i * tq, tq)
                for h in range(hb):
                    tile(h, k0, q0, False)
                return 0

            lax.fori_loop(j + 1, n, q_step, 0)
            return 0

        lax.fori_loop(0, n, kv_step, 0)

    hspec = lambda c: pl.BlockSpec((hb, seq, c), lambda b, g: (g, b, 0))
    cols = pl.BlockSpec((seq, hb * V_DIM), lambda b, g: (b, g))
    return _pallas(
        body, "attn_bwd", (nseq, N_HEADS // hb),
        [hspec(HEAD_PAD), hspec(HEAD_PAD), cols, cols,
         pl.BlockSpec((hb, 1, seq), lambda b, g: (g, 0, b)), pl.BlockSpec((None, hb, seq), lambda b, g: (g, 0, b))],
        [hspec(HEAD_PAD), hspec(HEAD_PAD), cols],
        [jax.ShapeDtypeStruct((N_HEADS, t, HEAD_PAD), F32), jax.ShapeDtypeStruct((N_HEADS, t, HEAD_PAD), F32),
         jax.ShapeDtypeStruct((t, N_HEADS * V_DIM), F32)],
        (q, k, v, do, lse, delta), ("arbitrary", "arbitrary"), carried)


def _merged_mixers(o_ref, gb_ref, gla_ref, glb_ref, xc_ref, gc_ref, xcp_ref, gcp_ref, bias_ref, cw_ref, wpa_ref, wpc_ref,
                   first_of_seq):
    y_a = _mm(o_ref[...], wpa_ref[...])
    gb = gb_ref[...]
    u = gc_ref[...] * xc_ref[...]
    u_prev = jnp.where(first_of_seq, 0.0, gcp_ref[...] * xcp_ref[...])
    cw = cw_ref[...]
    z = cw[2:3] * u + cw[1:2] * _shift_down(u, u_prev, 1) + cw[0:1] * _shift_down(u, u_prev, 2)
    gbz = _bf(gb * z)
    y_b = _mm(gbz, wpc_ref[...])
    bias = bias_ref[...]
    gate_a = _sigmoid(gla_ref[...] + bias[:, :D_MODEL])
    gate_b = _sigmoid(glb_ref[...] + bias[:, D_MODEL:])
    return _bf(gate_a * y_a + gate_b * y_b)


def _mixer_specs(tm, seq):
    d = D_MODEL
    tok = pl.BlockSpec((tm, d), lambda i: (i, 0))
    col = lambda c: pl.BlockSpec((tm, d), lambda i: (i, c))
    prev = lambda c: pl.BlockSpec((8, d), lambda i: (jnp.maximum(i * (tm // 8) - 1, 0), c))
    o_spec = pl.BlockSpec((tm, N_HEADS * V_DIM), lambda i: (i, 0))
    fwd_specs = [o_spec, col(0), col(1), col(2), col(3), col(4), prev(3), prev(4), _const((1, 2 * d)), _const((3, d)),
                 _resident((N_HEADS * V_DIM, d)), _resident((d, d)), _resident((d, d))]
    return tok, fwd_specs


def _mix_fwd(x1, o, big, gate_bias, conv_w, w_pa, w_pc, w_out, seq, carried=None):
    t, d = x1.shape
    tm = TOKEN_TILE
    tiles_per_seq = seq // tm

    def body(x_ref, o_ref, gb_ref, gla_ref, glb_ref, xc_ref, gc_ref, xcp_ref, gcp_ref, bias_ref, cw_ref, wpa_ref, wpc_ref,
             wout_ref, x2_ref):
        first = pl.program_id(0) % tiles_per_seq == 0
        merged = _merged_mixers(o_ref, gb_ref, gla_ref, glb_ref, xc_ref, gc_ref, xcp_ref, gcp_ref, bias_ref, cw_ref, wpa_ref,
                                wpc_ref, first)
        x2_ref[...] = x_ref[...] + _mm(merged, wout_ref[...])

    tok, fwd_specs = _mixer_specs(tm, seq)
    return _pallas(body, "mix_fwd", (t // tm,), [tok] + fwd_specs, [tok], [jax.ShapeDtypeStruct((t, d), F32)],
                   (x1, o, big, big, big, big, big, big, big, gate_bias, conv_w, w_pa, w_pc, w_out), ("arbitrary",), carried)[0]


def _mix_bwd(dx2, o, big, gate_bias, conv_w, w_pa, w_pc, w_out, seq, carried=None):
    t, d = dx2.shape
    tm = TOKEN_TILE
    tiles_per_seq = seq // tm
    hv = N_HEADS * V_DIM

    def body(dx_ref, o_ref, gb_ref, gla_ref, glb_ref, xc_ref, gc_ref, xcp_ref, gcp_ref, bias_ref, cw_ref, wpa_ref, wpc_ref,
             wout_ref, do_ref, delta_ref, dz_ref, dm_ref, dbias_ref, dwpa_ref, dwpc_ref, dwout_ref):
        @pl.when(pl.program_id(0) == 0)
        def _():
            dbias_ref[...] = jnp.zeros_like(dbias_ref)
            dwpa_ref[...] = jnp.zeros_like(dwpa_ref)
            dwpc_ref[...] = jnp.zeros_like(dwpc_ref)
            dwout_ref[...] = jnp.zeros_like(dwout_ref)

        first = pl.program_id(0) % tiles_per_seq == 0
        dxb = _bf(dx_ref[...])
        dmerged = _mm_nt(dxb, wout_ref[...])
        y_a = _mm(o_ref[...], wpa_ref[...])
        bias = bias_ref[...]
        gate_a = _sigmoid(gla_ref[...] + bias[:, :d])
        gate_b = _sigmoid(glb_ref[...] + bias[:, d:])
        dya = _bf(dmerged * gate_a)
        dyb = _bf(dmerged * gate_b)
        do_v = _mm_nt(dya, wpa_ref[...])
        dgz = _mm_nt(dyb, wpc_ref[...])
        dwpa_ref[...] += _mm_tn(o_ref[...], dya)
        gb = gb_ref[...]
        u = gc_ref[...] * xc_ref[...]
        u_prev = jnp.where(first, 0.0, gcp_ref[...] * xcp_ref[...])
        cw = cw_ref[...]
        z = cw[2:3] * u + cw[1:2] * _shift_down(u, u_prev, 1) + cw[0:1] * _shift_down(u, u_prev, 2)
        gbz = _bf(gb * z)
        y_b = _mm(gbz, wpc_ref[...])
        dwpc_ref[...] += _mm_tn(gbz, dyb)
        do_ref[...] = do_v
        head = lax.broadcasted_iota(jnp.int32, (N_HEADS, hv), 0) * V_DIM
        col = lax.broadcasted_iota(jnp.int32, (N_HEADS, hv), 1)
        in_head = ((col >= head) & (col < head + V_DIM)).astype(F32)
        delta_ref[...] = lax.dot_general(in_head, do_v * o_ref[...].astype(F32), (((1,), (1,)), ((), ())),
                                         precision=lax.Precision.HIGHEST, preferred_element_type=F32)
        dz_ref[...] = dgz * gb
        dm_ref[:, :d] = _bf(dgz * z)
        merged = _bf(gate_a * y_a + gate_b * y_b)
        dwout_ref[...] += _mm_tn(merged, dxb)
        dla = (dmerged * y_a) * (gate_a * (1.0 - gate_a))
        dlb = (dmerged * y_b) * (gate_b * (1.0 - gate_b))
        dbias_ref[:, :d] += jnp.sum(dla, axis=0, keepdims=True)
        dbias_ref[:, d:] += jnp.sum(dlb, axis=0, keepdims=True)
        dm_ref[:, d:2 * d] = _bf(dla)
        dm_ref[:, 2 * d:] = _bf(dlb)

    tok, fwd_specs = _mixer_specs(tm, seq)
    return _pallas(
        body, "mix_bwd", (t // tm,), [tok] + fwd_specs,
        [pl.BlockSpec((tm, hv), lambda i: (i, 0)), pl.BlockSpec((N_HEADS, tm), lambda i: (0, i)), tok,
         pl.BlockSpec((tm, M_COLS), lambda i: (i, 0)), _const((1, 2 * d)), _const((hv, d)), _const((d, d)), _const((d, d))],
        [jax.ShapeDtypeStruct((t, hv), F32), jax.ShapeDtypeStruct((N_HEADS, t), F32), jax.ShapeDtypeStruct((t, d), F32),
         jax.ShapeDtypeStruct((t, M_COLS), BF16), jax.ShapeDtypeStruct((1, 2 * d), F32), jax.ShapeDtypeStruct((hv, d), F32),
         jax.ShapeDtypeStruct((d, d), F32), jax.ShapeDtypeStruct((d, d), F32)],
        (dx2, o, big, big, big, big, big, big, big, gate_bias, conv_w, w_pa, w_pc, w_out), ("arbitrary",), carried)


def _prep_bwd(lat, big, dz, dq, dk, dv, qa_gain, kva_gain, qh_gain, kh_gain, w_uq, w_uk, w_uv, rope, conv_w, seq, carried=None):
    t = lat.shape[0]
    d = D_MODEL
    tm = PREP_TILE
    tiles_per_seq = seq // tm
    last_blk = t // 8 - 1

    def body(lat_ref, xc_ref, gc_ref, dz_ref, dzn_ref, dq_ref, dk_ref, dv_ref, qa_ref, kva_ref, qh_ref, kh_ref, wuq_ref, wuk_ref,
             wuv_ref, cos_ref, slo_ref, shi_ref, cw_ref,
             dp_ref, dwuq_ref, dwuk_ref, dwuv_ref, dqa_ref, dkva_ref, dqh_ref, dkh_ref, dcw_ref):
        pid = pl.program_id(0)

        @pl.when(pid == 0)
        def _():
            for r in (dwuq_ref, dwuk_ref, dwuv_ref, dqa_ref, dkva_ref, dqh_ref, dkh_ref, dcw_ref):
                r[...] = jnp.zeros_like(r)

        lat_v = lat_ref[...]
        q_lat = lat_v[:, :Q_LORA]
        kv_lat = lat_v[:, Q_LORA:Q_LORA + KV_LORA]
        k_rope = lat_v[:, Q_LORA + KV_LORA:]
        qa_gain_v = qa_ref[...]
        kva_gain_v = kva_ref[...]
        qh_gain_v = qh_ref[...]
        kh_gain_v = kh_ref[...]
        cq, rq = _rms(q_lat, qa_gain_v)
        ckv, rkv = _rms(kv_lat, kva_gain_v)
        cqb = _bf(cq)
        ckvb = _bf(ckv)
        rope_v = (cos_ref[...], slo_ref[...], shi_ref[...])
        lane = lax.broadcasted_iota(jnp.int32, (tm, HEAD_PAD), 1)
        rope_lanes = (lane >= QK_NOPE) & (lane < QK_DIM)
        dk_rope = jnp.zeros((tm, HEAD_PAD), F32)
        dqh_gain = jnp.zeros((1, HEAD_PAD), F32)
        dkh_gain = jnp.zeros((1, HEAD_PAD), F32)
        q_all = _mm(cqb, wuq_ref[...])
        k_all = _mm(ckvb, wuk_ref[...])
        dvb = _bf(dv_ref[...])
        dckv = _mm_nt(dvb, wuv_ref[...])
        dwuv_ref[...] += _mm_tn(ckvb, dvb)

        last = pid % tiles_per_seq == tiles_per_seq - 1
        dzv = dz_ref[...]
        dz_next = jnp.where(last, 0.0, dzn_ref[...])
        dz1 = _shift_up(dzv, dz_next, 1)
        dz2 = _shift_up(dzv, dz_next, 2)
        cw = cw_ref[...]
        xc = xc_ref[...]
        gc = gc_ref[...]
        u = gc * xc
        du = cw[2:3] * dzv + cw[1:2] * dz1 + cw[0:1] * dz2
        dp_ref[:, :d] = _bf(du * gc)
        dp_ref[:, d:2 * d] = _bf(du * xc)
        dcw_ref[0:1, :] += jnp.sum(dz2 * u, axis=0, keepdims=True)
        dcw_ref[1:2, :] += jnp.sum(dz1 * u, axis=0, keepdims=True)
        dcw_ref[2:3, :] += jnp.sum(dzv * u, axis=0, keepdims=True)

        dcq = jnp.zeros((tm, Q_LORA), F32)
        half = N_HEADS // 2
        for part in range(2):
            dq_heads, dk_heads = [], []
            for hd in range(part * half, (part + 1) * half):
                lanes = slice(hd * HEAD_PAD, (hd + 1) * HEAD_PAD)
                q_pre = q_all[:, lanes]
                _, rr = _rms(q_pre, qh_gain_v, QK_DIM)
                dq_pre, dg = _rms_bwd(q_pre, rr, qh_gain_v, _rope_bwd(dq_ref[hd], rope_v), QK_DIM)
                dqh_gain = dqh_gain + dg
                dq_heads.append(_bf(dq_pre))

                k_pre = k_all[:, lanes] + k_rope
                _, rr = _rms(k_pre, kh_gain_v, QK_DIM)
                dk_pre, dg = _rms_bwd(k_pre, rr, kh_gain_v, _rope_bwd(dk_ref[hd], rope_v), QK_DIM)
                dkh_gain = dkh_gain + dg
                dk_rope = dk_rope + jnp.where(rope_lanes, dk_pre, 0.0)
                dk_heads.append(_bf(dk_pre))
            dq_part = jnp.concatenate(dq_heads, axis=1)
            dk_part = jnp.concatenate(dk_heads, axis=1)
            cols = slice(part * half * HEAD_PAD, (part + 1) * half * HEAD_PAD)
            dcq = dcq + _mm_nt(dq_part, wuq_ref[:, cols])
            dckv = dckv + _mm_nt(dk_part, wuk_ref[:, cols])
            dwuq_ref[:, cols] += _mm_tn(cqb, dq_part)
            dwuk_ref[:, cols] += _mm_tn(ckvb, dk_part)
        dqh_ref[...] += dqh_gain
        dkh_ref[...] += dkh_gain
        dq_lat, dg = _rms_bwd(q_lat, rq, qa_gain_v, dcq)
        dqa_ref[...] += dg
        dkv_lat, dg = _rms_bwd(kv_lat, rkv, kva_gain_v, dckv)
        dkva_ref[...] += dg
        dp_ref[:, 2 * d:2 * d + Q_LORA] = _bf(dq_lat)
        dp_ref[:, 2 * d + Q_LORA:2 * d + Q_LORA + KV_LORA] = _bf(dkv_lat)
        dp_ref[:, 2 * d + Q_LORA + KV_LORA:] = _bf(dk_rope)

    tok = lambda c: pl.BlockSpec((tm, c), lambda i: (i, 0))
    col = lambda c: pl.BlockSpec((tm, d), lambda i: (i, c))
    head = lambda c: pl.BlockSpec((N_HEADS, tm, c), lambda i: (0, i, 0))
    nxt = pl.BlockSpec((8, d), lambda i: (jnp.minimum((i + 1) * (tm // 8), last_blk), 0))
    return _pallas(
        body, "prep_bwd", (t // tm,),
        [tok(LAT_COLS), col(3), col(4), tok(d), nxt, head(HEAD_PAD), head(HEAD_PAD), tok(N_HEADS * V_DIM),
         _const((1, Q_LORA)), _const((1, KV_LORA)), _const((1, HEAD_PAD)), _const((1, HEAD_PAD)),
         _resident(w_uq.shape), _resident(w_uk.shape), _resident(w_uv.shape), tok(HEAD_PAD), tok(HEAD_PAD), tok(HEAD_PAD),
         _const((3, d))],
        [tok(P_COLS), _const(w_uq.shape), _const(w_uk.shape), _const(w_uv.shape), _const((1, Q_LORA)),
         _const((1, KV_LORA)), _const((1, HEAD_PAD)), _const((1, HEAD_PAD)), _const((3, d))],
        [jax.ShapeDtypeStruct((t, P_COLS), BF16), jax.ShapeDtypeStruct(w_uq.shape, F32),
         jax.ShapeDtypeStruct(w_uk.shape, F32), jax.ShapeDtypeStruct(w_uv.shape, F32),
         jax.ShapeDtypeStruct((1, Q_LORA), F32), jax.ShapeDtypeStruct((1, KV_LORA), F32),
         jax.ShapeDtypeStruct((1, HEAD_PAD), F32), jax.ShapeDtypeStruct((1, HEAD_PAD), F32), jax.ShapeDtypeStruct((3, d), F32)],
        (lat, big, big, dz, dz, dq, dk, dv, qa_gain, kva_gain, qh_gain, kh_gain, w_uq, w_uk, w_uv, *rope, conv_w),
        ("arbitrary",), carried)


def _inproj_bwd(x1, gain, dx2, dm, dp, w_in, w_kr, carried=None):
    t, d = x1.shape
    tm = TOKEN_TILE

    def body(x_ref, g_ref, dx2_ref, dm_ref, dp_ref, win_ref, wkr_ref, dx1_ref, dgain_ref):
        xv = x_ref[...]
        gain_v = g_ref[...]
        _, r = _rms(xv, gain_v)
        dh = (_mm(dm_ref[:, :d], win_ref[ROW_GB:ROW_GC, :]) + _mm(dm_ref[:, d:], win_ref[ROW_GL:, :])
              + _mm(dp_ref[:, :d], win_ref[ROW_XC:ROW_GB, :]) + _mm(dp_ref[:, d:2 * d], win_ref[ROW_GC:ROW_GL, :])
              + _mm(dp_ref[:, 2 * d:2 * d + ROW_KR], win_ref[ROW_QKV:ROW_KR, :]) + _mm(dp_ref[:, 2 * d + ROW_KR:], wkr_ref[...]))
        dxn, dgain = _rms_bwd(xv, r, gain_v, dh)
        dx1_ref[...] = dx2_ref[...] + dxn

        @pl.when(pl.program_id(0) == 0)
        def _():
            dgain_ref[...] = jnp.zeros_like(dgain_ref)

        dgain_ref[...] += dgain

    tok = lambda c: pl.BlockSpec((tm, c), lambda i: (i, 0))
    return _pallas(
        body, "inproj_bwd", (t // tm,),
        [tok(d), _const((1, d)), tok(d), tok(M_COLS), tok(P_COLS), _resident(w_in.shape), _resident(w_kr.shape)],
        [tok(d), _const((1, d))], [jax.ShapeDtypeStruct((t, d), F32), jax.ShapeDtypeStruct((1, d), F32)],
        (x1, gain, dx2, dm, dp, w_in, w_kr), ("arbitrary",), carried)


def _adamw(quads, name, carried=None):
    k = len(quads)
    rows, cols = quads[0][0].shape
    tr, tc = rows, cols
    for cand in (512, 352, 256, 192, 128, 64):
        if rows % cand == 0 and rows > cand:
            tr = cand
            break
    if tr == rows and rows * cols > 512 * 1024 and cols % 256 == 0:
        tc = 256
    while k * 14 * tr * tc * 4 > (VMEM_LIMIT * 3) // 4 and tr % 16 == 0:
        tr //= 2

    def body(*refs):
        for i in range(k):
            w_ref, g_ref, m_ref, v_ref = refs[4 * i:4 * i + 4]
            delta_ref, nm_ref, nv_ref = refs[4 * k + 3 * i:4 * k + 3 * i + 3]
            delta_ref[...], nm_ref[...], nv_ref[...] = _adamw_update(w_ref[...], g_ref[...], m_ref[...], v_ref[...])

    spec = pl.BlockSpec((tr, tc), lambda i, j: (i, j))
    shape = jax.ShapeDtypeStruct((rows, cols), F32)
    outs = _pallas(body, name, (rows // tr, cols // tc), [spec] * (4 * k), [spec] * (3 * k), [shape] * (3 * k),
                   [a for quad in quads for a in quad], ("arbitrary", "arbitrary"), carried)
    return [tuple(outs[3 * i:3 * i + 3]) for i in range(k)]


def _adamw_update(w, g, m, v):
    nm = ADAM_B1 * m + (1.0 - ADAM_B1) * g
    nv = ADAM_B2 * v + (1.0 - ADAM_B2) * (g * g)
    m_hat = nm * (1.0 / (1.0 - ADAM_B1 ** ADAM_STEP))
    v_hat = nv * (1.0 / (1.0 - ADAM_B2 ** ADAM_STEP))
    return -ADAM_LR * (m_hat / (jnp.sqrt(v_hat) + ADAM_EPS) + ADAM_WD * w), nm, nv


def _adamw_whole(quads, name):
    k = len(quads)

    def body(*refs):
        for i in range(k):
            w_ref, g_ref, m_ref, v_ref = refs[4 * i:4 * i + 4]
            g_out, delta_ref, nm_ref, nv_ref = refs[4 * k + 4 * i:4 * k + 4 * i + 4]
            gv = g_ref[...]
            g_out[...] = gv
            delta_ref[...], nm_ref[...], nv_ref[...] = _adamw_update(w_ref[...], gv, m_ref[...], v_ref[...])

    vm = pl.BlockSpec(memory_space=pltpu.VMEM)
    outs = pl.pallas_call(body, name=name, in_specs=[vm] * (4 * k), out_specs=[vm] * (4 * k),
                          out_shape=[jax.ShapeDtypeStruct(q[0].shape, F32) for q in quads for _ in range(4)],
                          compiler_params=_params())(*[a for quad in quads for a in quad])
    return [tuple(outs[4 * i:4 * i + 4]) for i in range(k)]


def _adamw_small(packed_grads, triples, segments):
    k = len(triples)

    def body(*refs):
        g_ref = refs[0]
        off = 0
        for i in range(k):
            w_ref, m_ref, v_ref = refs[1 + 3 * i:4 + 3 * i]
            g_out, delta_ref, nm_ref, nv_ref = refs[1 + 3 * k + 4 * i:5 + 3 * k + 4 * i]
            gv = g_ref[:, off:off + w_ref.shape[1]]
            g_out[...] = gv
            delta_ref[...], nm_ref[...], nv_ref[...] = _adamw_update(w_ref[...], gv, m_ref[...], v_ref[...])
            off += segments[i]

    vm = pl.BlockSpec(memory_space=pltpu.VMEM)
    outs = pl.pallas_call(
        body, name="adamw_small", in_specs=[vm] * (1 + 3 * k), out_specs=[vm] * (4 * k),
        out_shape=[jax.ShapeDtypeStruct(w.shape, F32) for w, _, _ in triples for _ in range(4)],
    )(packed_grads, *[a for triple in triples for a in triple])
    return [tuple(outs[4 * i:4 * i + 4]) for i in range(k)]


def _place():
    x, y, c = lax.axis_index("x"), lax.axis_index("y"), lax.axis_index("c")
    other_chips = [(1 - x, y), (x, 1 - y), (1 - x, 1 - y)]
    return x, y, c, other_chips


def _remote(src, dst, sems, send, recv, device):
    return pltpu.make_async_remote_copy(src_ref=src, dst_ref=dst, send_sem=sems.at[send], recv_sem=sems.at[recv],
                                        device_id=device, device_id_type=MESH_ID)


def _cast_shards(shards, out_dtypes, n_first):
    n = len(shards)
    out_shape = [jax.ShapeDtypeStruct((N_CHIPS,) + s.shape, dt) for s, dt in zip(shards, out_dtypes)]
    gather = _gather_carried(out_shape[:n_first])

    def body(*refs):
        ins, outs, stage, sems = refs[:n], refs[n:2 * n], refs[2 * n:3 * n], refs[3 * n]
        x, y, _, _ = _place()
        me = 2 * x + y

        def cast(first, last):
            copies = []
            for w in range(first, last):
                stage[w][...] = ins[w][...].astype(out_dtypes[w])
                copies.append(pltpu.make_async_copy(stage[w], outs[w].at[me], sems.at[w]))
                copies[-1].start()
            for cp in copies:
                cp.wait()

        cast(0, n_first)
        gather.start(None, outs[:n_first], sems, n)
        cast(n_first, n)
        gather.finish(None, outs[:n_first], sems, n)

    vm = pl.BlockSpec(memory_space=pltpu.VMEM)
    return pl.pallas_call(
        body, name="cast_shards", in_specs=[vm] * n, out_specs=[ANY] * n, out_shape=out_shape,
        scratch_shapes=[pltpu.VMEM(s.shape, dt) for s, dt in zip(shards, out_dtypes)] + [pltpu.SemaphoreType.DMA((n + gather.n_sems,))],
        compiler_params=_params())(*shards)


BF16_ROWS = 16


def _split_rows(rows):
    return (rows // 2) % BF16_ROWS == 0


def _half_shape(rows, cols):
    return (rows // 2, cols) if _split_rows(rows) else (rows, cols // 2)


def _half(rows, cols, which):
    if _split_rows(rows):
        return (pl.ds(pl.multiple_of(which * (rows // 2), BF16_ROWS), rows // 2), slice(None))
    return (slice(None), pl.ds(pl.multiple_of(which * (cols // 2), 128), cols // 2))


def _gather_carried(bufs):
    n = len(bufs)

    def half(w, slot, which):
        _, rows, cols = bufs[w].shape
        return (slot,) + _half(rows, cols, which)

    def start(ins, outs, sems, base):
        x, y, c, other_chips = _place()
        me = 2 * x + y
        for w in range(n):
            mine = outs[w].at[half(w, me, c)]
            for p, (px, py) in enumerate(other_chips):
                _remote(mine, mine, sems, base + 12 * w + p, base + 12 * w + 3 + p, (px, py, c)).start()

    def finish(ins, outs, sems, base):
        x, y, c, other_chips = _place()
        me = 2 * x + y
        for w in range(n):
            for p, (px, py) in enumerate(other_chips):
                got = outs[w].at[half(w, 2 * px + py, c)]
                _remote(got, got, sems, base + 12 * w + p, base + 12 * w + 3 + p, (px, py, c)).wait_recv()
                _remote(got, got, sems, base + 12 * w + 6 + p, base + 12 * w + 9 + p, (x, y, 1 - c)).start()
        for w in range(n):
            mine = outs[w].at[half(w, me, c)]
            for p, (px, py) in enumerate(other_chips):
                got = outs[w].at[half(w, 2 * px + py, c)]
                theirs = outs[w].at[half(w, 2 * px + py, 1 - c)]
                _remote(got, theirs, sems, base + 12 * w + 6 + p, base + 12 * w + 9 + p, (x, y, 1 - c)).wait()
                _remote(mine, mine, sems, base + 12 * w + p, base + 12 * w + 3 + p, (px, py, c)).wait_send()

    shapes = [jax.ShapeDtypeStruct(b.shape, b.dtype) for b in bufs]
    return _Carried(bufs, shapes, {w: w for w in range(n)}, 12 * n, start, finish)


def _swap_carried(grads):
    n = len(grads)

    def copy(w, ins, outs, sems, base):
        x, y, c, _ = _place()
        _, rows, cols = grads[w].shape
        theirs = ins[w].at[(slice(None),) + _half(rows, cols, 1 - c)]
        return _remote(theirs, outs[w], sems, base + 2 * w, base + 2 * w + 1, (x, y, 1 - c))

    def start(ins, outs, sems, base):
        for w in range(n):
            copy(w, ins, outs, sems, base).start()

    def finish(ins, outs, sems, base):
        for w in range(n):
            copy(w, ins, outs, sems, base).wait()

    shapes = [jax.ShapeDtypeStruct((g.shape[0],) + _half_shape(*g.shape[1:]), F32) for g in grads]
    return _Carried(grads, shapes, {}, 2 * n, start, finish)


def _row_tile(rows):
    for cand in (512, 352, 256, 192, 128, 96, 64, 32, 16):
        if rows % cand == 0:
            return cand
    return rows


def _half_block_index(split_rows, tiles, i, core):
    return (core * tiles + i, 0) if split_rows else (i, core)


def _chip_partial(grad, other, place, name):
    nblk, hr, hc = other.shape
    by_rows = _split_rows(grad.shape[1])
    tr = _row_tile(hr)
    tiles = hr // tr

    def body(place_ref, g_ref, o_ref, own_ref, sum_bf_ref):
        s = g_ref[...] + o_ref[...]
        sum_bf_ref[...] = _bf(s)

        @pl.when(pl.program_id(1) == place_ref[0])
        def _():
            own_ref[...] = s

    grid_spec = pltpu.PrefetchScalarGridSpec(
        num_scalar_prefetch=1, grid=(tiles, nblk),
        in_specs=[pl.BlockSpec((None, tr, hc), lambda i, b, place_ref: (b,) + _half_block_index(by_rows, tiles, i, place_ref[1])),
                  pl.BlockSpec((None, tr, hc), lambda i, b, place_ref: (b, i, 0))],
        out_specs=[pl.BlockSpec((tr, hc), lambda i, b, place_ref: (i, 0)),
                   pl.BlockSpec((None, tr, hc), lambda i, b, place_ref: (b, i, 0))])
    return pl.pallas_call(body, name=name, grid_spec=grid_spec,
                          out_shape=[jax.ShapeDtypeStruct((hr, hc), F32), jax.ShapeDtypeStruct(other.shape, BF16)],
                          compiler_params=_params(("arbitrary", "arbitrary")))(place, grad, other)


def _chip_partial_small(grads, others, place):
    n = len(grads)

    def body(*refs):
        place_ref, g_refs, o_refs = refs[0], refs[1:1 + n], refs[1 + n:1 + 2 * n]
        own_refs, bf_refs = refs[1 + 2 * n:1 + 3 * n], refs[1 + 3 * n:]
        chip, core = place_ref[0], place_ref[1]
        for w in range(n):
            _, rows, cols = grads[w].shape
            half = _half(rows, cols, core)
            bf_refs[w][...] = _bf(g_refs[w][(slice(None),) + half] + o_refs[w][...])
            own_refs[w][...] = g_refs[w][(chip,) + half] + o_refs[w][chip]

    vm = pl.BlockSpec(memory_space=pltpu.VMEM)
    outs = pl.pallas_call(
        body, name="chip_partial_small", in_specs=[pl.BlockSpec(memory_space=pltpu.SMEM)] + [vm] * (2 * n), out_specs=[vm] * (2 * n),
        out_shape=[jax.ShapeDtypeStruct(o.shape[1:], F32) for o in others] + [jax.ShapeDtypeStruct(o.shape, BF16) for o in others],
        compiler_params=_params())(place, *grads, *others)
    return list(zip(outs[:n], outs[n:]))


def _chip_total_small(owns, receiveds, place, shapes):
    n = len(owns)

    def body(*refs):
        place_ref, own_refs, r_refs, out_refs = refs[0], refs[1:1 + n], refs[1 + n:1 + 2 * n], refs[1 + 2 * n:]
        chip, core = place_ref[0], place_ref[1]
        for w in range(n):
            r = [r_refs[w][(chip + k) % N_CHIPS].astype(F32) for k in (1, 2, 3)]
            out_refs[w][_half(*shapes[w], core)] = own_refs[w][...] + ((r[0] + r[1]) + r[2])

    vm = pl.BlockSpec(memory_space=pltpu.VMEM)
    return list(pl.pallas_call(
        body, name="chip_total_small", in_specs=[pl.BlockSpec(memory_space=pltpu.SMEM)] + [vm] * (2 * n), out_specs=[vm] * n,
        out_shape=[jax.ShapeDtypeStruct(tuple(s), F32) for s in shapes], compiler_params=_params())(place, *owns, *receiveds))


def _send_carried(partials):
    n = len(partials)

    def start(ins, outs, sems, base):
        x, y, c, other_chips = _place()
        me = 2 * x + y
        for w in range(n):
            for p, (px, py) in enumerate(other_chips):
                _remote(ins[w].at[2 * px + py], outs[w].at[me], sems, base + 6 * w + p, base + 6 * w + 3 + p, (px, py, c)).start()

    def finish(ins, outs, sems, base):
        x, y, c, other_chips = _place()
        for w in range(n):
            for p, (px, py) in enumerate(other_chips):
                _remote(ins[w].at[2 * px + py], outs[w].at[2 * px + py], sems, base + 6 * w + p, base + 6 * w + 3 + p,
                        (px, py, c)).wait()

    return _Carried(partials, [jax.ShapeDtypeStruct(p.shape, BF16) for p in partials], {}, 6 * n, start, finish)


def _chip_total(own, received, place, shape, name):
    hr, hc = own.shape
    by_rows = _split_rows(shape[0])
    tr = _row_tile(hr)
    tiles = hr // tr

    def body(place_ref, own_ref, r1_ref, r2_ref, r3_ref, out_ref):
        out_ref[...] = own_ref[...] + ((r1_ref[...].astype(F32) + r2_ref[...].astype(F32)) + r3_ref[...].astype(F32))

    def slot(k):
        return pl.BlockSpec((None, tr, hc), lambda i, place_ref: ((place_ref[0] + k) % N_CHIPS, i, 0))

    grid_spec = pltpu.PrefetchScalarGridSpec(
        num_scalar_prefetch=1, grid=(tiles,), in_specs=[pl.BlockSpec((tr, hc), lambda i, place_ref: (i, 0)), slot(1), slot(2), slot(3)],
        out_specs=pl.BlockSpec((tr, hc), lambda i, place_ref: _half_block_index(by_rows, tiles, i, place_ref[1])))
    return pl.pallas_call(body, name=name, grid_spec=grid_spec, out_shape=jax.ShapeDtypeStruct(tuple(shape), F32),
                          compiler_params=_params(("arbitrary",)))(place, own, received, received, received)


def _join_carried(totals):
    n = len(totals)

    def copy(w, outs, sems, base):
        x, y, c, _ = _place()
        mine = outs[w].at[_half(*totals[w].shape, c)]
        return _remote(mine, mine, sems, base + 2 * w, base + 2 * w + 1, (x, y, 1 - c))

    def start(ins, outs, sems, base):
        for w in range(n):
            copy(w, outs, sems, base).start()

    def finish(ins, outs, sems, base):
        for w in range(n):
            copy(w, outs, sems, base).wait()

    shapes = [jax.ShapeDtypeStruct(a.shape, F32) for a in totals]
    return _Carried(totals, shapes, {w: w for w in range(n)}, 2 * n, start, finish)


def _sum_devices(vec):
    rows, n = vec.shape

    def body(v_ref, out_ref, buf, send_sems, recv_sems):
        x, y, c, _ = _place()
        me = 4 * x + 2 * y + c
        buf[me] = v_ref[...]
        sends = []
        for k in range(1, N_DEV):
            peer = (1 - x if k & 4 else x, 1 - y if k & 2 else y, 1 - c if k & 1 else c)
            cp = pltpu.make_async_remote_copy(src_ref=v_ref, dst_ref=buf.at[me], send_sem=send_sems.at[k], recv_sem=recv_sems.at[k],
                                              device_id=peer, device_id_type=MESH_ID)
            cp.start()
            sends.append(cp)
        for cp in sends:
            cp.wait()
        total = buf[0]
        for dev in range(1, N_DEV):
            total = total + buf[dev]
        out_ref[...] = total

    vm = pl.BlockSpec(memory_space=pltpu.VMEM)
    return pl.pallas_call(
        body, name="sum_devices", in_specs=[vm], out_specs=vm, out_shape=jax.ShapeDtypeStruct((rows, n), F32),
        scratch_shapes=[pltpu.VMEM((N_DEV, rows, n), F32), pltpu.SemaphoreType.DMA((N_DEV,)), pltpu.SemaphoreType.DMA((N_DEV,))],
    )(vec)


def _rope_tables(positions):
    half = ROPE_HALF
    inv_freq = 1.0 / (ROPE_THETA ** (jnp.arange(half, dtype=F32) / half))
    ang = positions.astype(F32).reshape(-1, 1) * inv_freq
    cos, sin = jnp.cos(ang), jnp.sin(ang)
    t = ang.shape[0]
    ones, zeros = jnp.ones((t, QK_NOPE), F32), jnp.zeros((t, QK_NOPE), F32)
    pad, none = HEAD_PAD - QK_DIM, zeros[:, :half]
    cos_full = jnp.concatenate([ones, cos, cos, ones[:, :pad]], axis=1)
    s_lo = jnp.concatenate([zeros, -sin, none, zeros[:, :pad]], axis=1)
    s_hi = jnp.concatenate([zeros, none, sin, zeros[:, :pad]], axis=1)
    return cos_full, s_lo, s_hi


def _partials(names, grads, from_sibling, place):
    return [_chip_partial(g, o, place, "chip_partial_" + n) for n, g, o in zip(names, grads, from_sibling)]


def _totals(names, grads, partials, received, place):
    return [_chip_total(pf, r, place, g.shape[1:], "chip_total_" + n) for n, g, (pf, _), r in zip(names, grads, partials, received)]


def _kernel_layouts(full):
    w_in = full["w_in"]
    w_kr = jnp.pad(w_in[ROW_KR:ROW_XC], ((QK_NOPE, HEAD_PAD - QK_DIM), (0, 0)))
    w_uq = jnp.pad(full["w_uq"].reshape(Q_LORA, N_HEADS, QK_DIM), ((0, 0), (0, 0), (0, HEAD_PAD - QK_DIM)))
    w_uk = jnp.pad(full["w_uk"].reshape(KV_LORA, N_HEADS, QK_NOPE), ((0, 0), (0, 0), (0, HEAD_PAD - QK_NOPE)))
    return {"w_in": w_in, "w_kr": w_kr, "w_uq": w_uq.reshape(Q_LORA, N_HEADS * HEAD_PAD),
            "w_uk": w_uk.reshape(KV_LORA, N_HEADS * HEAD_PAD), "w_uv": full["w_uv"], "w_uvt": full["w_uv"].T}


def _global_layouts(g):
    w_uq = g["w_uq"].reshape(Q_LORA, N_HEADS, HEAD_PAD)[:, :, :QK_DIM].reshape(Q_LORA, N_HEADS * QK_DIM)
    w_uk = g["w_uk"].reshape(KV_LORA, N_HEADS, HEAD_PAD)[:, :, :QK_NOPE].reshape(KV_LORA, N_HEADS * QK_NOPE)
    return {"w_in": g["w_in"], "w_uq": w_uq, "w_uk": w_uk, "w_uv": g["w_uv"], "w_proj_attn": g["w_pa"], "w_proj_conv": g["w_pc"],
            "w_out": g["w_out"]}


def _dw_in(dm, dp, h2b):
    d, wide, narrow, rows = D_MODEL, 512, HEAD_PAD, ROW_GL + 2 * D_MODEL
    from_dm = [(k, ROW_GB + k * wide, 0, wide) for k in range(d // wide)]
    from_dm += [(d // wide + k, ROW_GL + k * wide, 0, wide) for k in range(2 * d // wide)]
    from_dp = [(k, ROW_XC + k * wide, 0, wide) for k in range(d // wide)]
    from_dp += [(d // wide + k, ROW_GC + k * wide, 0, wide) for k in range(d // wide)]
    from_dp += [(2 * d // wide, ROW_QKV, 0, wide)]
    lat0 = (2 * d + wide) // narrow
    tail = [(lat0, ROW_QKV + wide, 0, ROW_KR - wide), (lat0 + 1, ROW_KR, QK_NOPE, QK_ROPE)]
    out = _tn_rows(dm, h2b, None, from_dm, wide, rows, "dw_in_m")
    out = _tn_rows(dp, h2b, out, from_dp, wide, rows, "dw_in_p")
    return _tn_rows(dp, h2b, out, tail, narrow, rows, "dw_in_lat")


def _col_blocks(a):
    r, c = a.shape
    return a.reshape(r, N_CHIPS, c // N_CHIPS).transpose(1, 0, 2)


def _from_col_blocks(a):
    n, r, c = a.shape
    return a.transpose(1, 0, 2).reshape(r, n * c)


COL_SHARDED = ("w_uq", "w_uk", "w_uv", "w_proj_attn")
TRANSPOSED = ("ffn1_w_gate", "ffn1_w_up", "ffn2_w_gate", "ffn2_w_up", "w_in")
SMALL = (("ffn1_norm", 1024), ("mix_norm", 1024), ("gate_bias", 2048), ("q_a_norm", 384), ("kv_a_norm", 256),
         ("q_head_norm", 128), ("k_head_norm", 128), ("ffn2_norm", 1024))
WEIGHT_ORDER = ("ffn1_norm", "ffn1_w_gate", "ffn1_w_up", "ffn1_w_down", "mix_norm", "w_in", "gate_bias", "q_a_norm", "w_uq",
                "kv_a_norm", "w_uk", "w_uv", "q_head_norm", "k_head_norm", "w_proj_attn", "conv_w", "w_proj_conv", "w_out",
                "ffn2_norm", "ffn2_w_gate", "ffn2_w_up", "ffn2_w_down")
MATRICES = ("ffn1_w_gate", "ffn1_w_up", "ffn1_w_down", "w_in", "w_uq", "w_uk", "w_uv", "w_proj_attn", "w_proj_conv", "w_out",
            "ffn2_w_gate", "ffn2_w_up", "ffn2_w_down")
GROUP_FFN1 = ("ffn1_w_gate", "ffn1_w_up", "ffn1_w_down")
GROUP_IN = ("w_in", "w_uq", "w_uk", "w_uv", "conv_w")
GROUP_MIX = ("w_proj_attn", "w_proj_conv", "w_out")
GROUP_FFN2 = ("ffn2_w_gate", "ffn2_w_up", "ffn2_w_down")
GROUP_MID = ("w_in", "w_uq", "w_uk", "w_uv", "w_proj_attn", "w_proj_conv", "w_out")


def _pad_lanes(a, n):
    return jnp.pad(a.reshape(1, -1), ((0, 0), (0, n - a.size)))


def kernel(x, positions, ffn1_norm, ffn1_w_gate, ffn1_w_up, ffn1_w_down, mix_norm, w_in, gate_bias, q_a_norm, w_uq, kv_a_norm, w_uk, w_uv, q_head_norm, k_head_norm, w_proj_attn, conv_w, w_proj_conv, w_out, ffn2_norm, ffn2_w_gate, ffn2_w_up, ffn2_w_down, loss_target, m_ffn1_norm, m_ffn1_w_gate, m_ffn1_w_up, m_ffn1_w_down, m_mix_norm, m_w_in, m_gate_bias, m_q_a_norm, m_w_uq, m_kv_a_norm, m_w_uk, m_w_uv, m_q_head_norm, m_k_head_norm, m_w_proj_attn, m_conv_w, m_w_proj_conv, m_w_out, m_ffn2_norm, m_ffn2_w_gate, m_ffn2_w_up, m_ffn2_w_down, v_ffn1_norm, v_ffn1_w_gate, v_ffn1_w_up, v_ffn1_w_down, v_mix_norm, v_w_in, v_gate_bias, v_q_a_norm, v_w_uq, v_kv_a_norm, v_w_uk, v_w_uv, v_q_head_norm, v_k_head_norm, v_w_proj_attn, v_conv_w, v_w_proj_conv, v_w_out, v_ffn2_norm, v_ffn2_w_gate, v_ffn2_w_up, v_ffn2_w_down):
    args = dict(locals())
    view = lambda n, a: a.T if n in TRANSPOSED else a
    weights = {n: view(n, args[n]) for n in WEIGHT_ORDER}
    moments_m = {n: view(n, args["m_" + n]) for n in WEIGHT_ORDER}
    moments_v = {n: view(n, args["v_" + n]) for n in WEIGHT_ORDER}
    nb, seq, d = x.shape
    t = nb * seq
    chip = (2 * lax.axis_index("x") + lax.axis_index("y")).astype(jnp.int32)
    place = jnp.stack([chip, lax.axis_index("c").astype(jnp.int32)])
    grads, delta, new_m, new_v = {}, {}, {}, {}

    def adamw(names, carried=None):
        results = _adamw([(weights[n], grads[n], moments_m[n], moments_v[n]) for n in names], "adamw_" + names[0], carried)
        for n, (dn, mn, vn) in zip(names, results):
            delta[n], new_m[n], new_v[n] = dn, mn, vn

    conv_rows = conv_w.shape[0]
    conv_shard = jnp.pad(conv_w, ((0, 16 - conv_rows), (0, 0)))
    assert MATRICES[:len(GROUP_FFN1)] == GROUP_FFN1
    bufs = dict(zip(MATRICES + ("conv_w",), _cast_shards([weights[n] for n in MATRICES] + [conv_shard],
                                                         [BF16] * len(MATRICES) + [F32], len(GROUP_FFN1))))
    blocks = {n: bufs[n] for n in GROUP_FFN1}
    p = {n: _pad_lanes(weights[n], size) for n, size in SMALL}
    rope = _rope_tables(positions)
    x_tok = x.reshape(t, d)

    gather_in = _gather_carried([bufs[n] for n in GROUP_IN])
    x1, gate1, up1, act1 = _ffn_fwd(x_tok, p["ffn1_norm"], blocks["ffn1_w_gate"], blocks["ffn1_w_up"], blocks["ffn1_w_down"], None,
                                    "ffn1_fwd", gather_in)
    blocks.update(zip(GROUP_IN, gather_in.results))
    w = _kernel_layouts({"w_in": blocks["w_in"].reshape(-1, d), **{n: _from_col_blocks(blocks[n]) for n in ("w_uq", "w_uk", "w_uv")}})
    p["conv_w"] = _from_col_blocks(blocks["conv_w"])[:conv_rows]

    gather_mix = _gather_carried([bufs[n] for n in GROUP_MIX + GROUP_FFN2[2:]])
    h2b, big, lat, q, k, v, vt = _inproj_fwd(x1, p["mix_norm"], w["w_in"], w["w_kr"], p["q_a_norm"], p["kv_a_norm"], p["q_head_norm"],
                                             p["k_head_norm"], w["w_uq"], w["w_uk"], w["w_uv"], w["w_uvt"], rope, gather_mix)
    blocks.update(zip(GROUP_MIX + GROUP_FFN2[2:], gather_mix.results))
    w_pa = _from_col_blocks(blocks["w_proj_attn"])
    w_pc, w_out_full = blocks["w_proj_conv"].reshape(-1, d), blocks["w_out"].reshape(-1, d)

    gather_ffn2 = _gather_carried([bufs[n] for n in GROUP_FFN2[:2]])
    o, lse = _attn_fwd(q, k, vt, seq, gather_ffn2)
    x2 = _mix_fwd(x1, o, big, p["gate_bias"], p["conv_w"], w_pa, w_pc, w_out_full, seq)
    wg2, wu2 = gather_ffn2.results
    wd2 = blocks["ffn2_w_down"]
    dx3, gate2, up2, act2, loss = _ffn_fwd(x2, p["ffn2_norm"], wg2, wu2, wd2, loss_target.reshape(t, d), "ffn2_fwd")

    dx2, dg_ffn2, hb2, dgate2, dup2, dyb2 = _ffn_bwd_x(x2, p["ffn2_norm"], dx3, gate2, up2, wg2, wu2, wd2, "ffn2_bwd")
    g_ffn2 = [_tn_matmul(dgate2, hb2, "ffn2_dw_gate"), _tn_matmul(dup2, hb2, "ffn2_dw_up"), _tn_matmul(act2, dyb2, "ffn2_dw_down")]
    swap = _swap_carried(g_ffn2)
    do, delta_o, dz, dm, dbias, dw_pa, dw_pc, dw_out = _mix_bwd(dx2, o, big, p["gate_bias"], p["conv_w"], w_pa, w_pc, w_out_full, seq,
                                                                swap)
    part = _partials(GROUP_FFN2, g_ffn2, swap.results, place)
    send = _send_carried([pb for _, pb in part])
    dq, dk, dv = _attn_bwd(q, k, v, do, lse, delta_o.reshape(N_HEADS // ATTN_BWD_HEADS, ATTN_BWD_HEADS, -1), seq, send)
    join = _join_carried(_totals(GROUP_FFN2, g_ffn2, part, send.results, place))
    dp, dw_uq, dw_uk, dw_uv, dqa, dkva, dqh, dkh, dcw = _prep_bwd(
        lat, big, dz, dq, dk, dv, p["q_a_norm"], p["kv_a_norm"], p["q_head_norm"], p["k_head_norm"], w["w_uq"], w["w_uk"],
        w["w_uv"], rope, p["conv_w"], seq, join)
    grads.update(zip(GROUP_FFN2, join.results))

    gg = _global_layouts({"w_in": _dw_in(dm, dp, h2b), "w_uq": dw_uq, "w_uk": dw_uk, "w_uv": dw_uv, "w_pa": dw_pa, "w_pc": dw_pc,
                          "w_out": dw_out})
    g_mid = [_col_blocks(gg[n]) if n in COL_SHARDED else gg[n].reshape(N_CHIPS, -1, gg[n].shape[-1]) for n in GROUP_MID]
    swap = _swap_carried(g_mid)
    dx1, dg_mix = _inproj_bwd(x1, p["mix_norm"], dx2, dm, dp, w["w_in"], w["w_kr"], swap)
    part = (_partials(GROUP_MID[:1], g_mid[:1], swap.results[:1], place)
            + _chip_partial_small(g_mid[1:], swap.results[1:], place))
    send = _send_carried([pb for _, pb in part])
    grad_x, dg_ffn1, hb1, dgate1, dup1, dyb1 = _ffn_bwd_x(x_tok, p["ffn1_norm"], dx1, gate1, up1, blocks["ffn1_w_gate"],
                                                         blocks["ffn1_w_up"], blocks["ffn1_w_down"], "ffn1_bwd", send)

    small_grads = {"ffn1_norm": dg_ffn1, "mix_norm": dg_mix, "gate_bias": dbias, "q_a_norm": dqa, "kv_a_norm": dkva,
                   "q_head_norm": dqh, "k_head_norm": dkh, "ffn2_norm": dg_ffn2}
    packed = jnp.concatenate([small_grads[n] for n, _ in SMALL] + [dcw.reshape(1, -1), loss], axis=1)
    total = _sum_devices(packed.reshape(8, -1)).reshape(1, -1)
    n_small = sum(size for _, size in SMALL)
    conv_cols = conv_w.shape[1]
    conv_total = total[:, n_small:n_small + conv_rows * d].reshape(conv_rows, d)
    grads["conv_w"] = lax.dynamic_slice_in_dim(conv_total, chip * conv_cols, conv_cols, axis=1)
    loss_total = total[0, n_small + conv_rows * d]

    join = _join_carried(_totals(GROUP_MID[:1], g_mid[:1], part[:1], send.results[:1], place)
                         + _chip_total_small([pf for pf, _ in part[1:]], send.results[1:], place, [g.shape[1:] for g in g_mid[1:]]))
    g_gate = _tn_matmul(dgate1, hb1, "ffn1_dw_gate", carried=join)
    grads.update(zip(GROUP_MID, join.results))
    swap_gate = _swap_carried([g_gate])
    g_up = _tn_matmul(dup1, hb1, "ffn1_dw_up", carried=swap_gate)
    part_gate = _partials(GROUP_FFN1[:1], [g_gate], swap_gate.results, place)
    send_gate, swap_up = _send_carried([part_gate[0][1]]), _swap_carried([g_up])
    g_down = _tn_matmul(act1, dyb1, "ffn1_dw_down", carried=_both(send_gate, swap_up))
    join_gate = _join_carried(_totals(GROUP_FFN1[:1], [g_gate], part_gate, send_gate.results, place))
    part_up = _partials(GROUP_FFN1[1:2], [g_up], swap_up.results, place)
    send_up, swap_down = _send_carried([part_up[0][1]]), _swap_carried([g_down])
    adamw(GROUP_FFN2, _both(_both(send_up, swap_down), join_gate))
    grads["ffn1_w_gate"] = join_gate.results[0]
    join_up = _join_carried(_totals(GROUP_FFN1[1:2], [g_up], part_up, send_up.results, place))
    part_down = _partials(GROUP_FFN1[2:], [g_down], swap_down.results, place)
    send_down = _send_carried([part_down[0][1]])
    adamw(("w_in",), _both(send_down, join_up))
    grads["ffn1_w_up"] = join_up.results[0]
    join_down = _join_carried(_totals(GROUP_FFN1[2:], [g_down], part_down, send_down.results, place))
    adamw(GROUP_FFN1[:2], join_down)
    grads["ffn1_w_down"] = join_down.results[0]
    adamw(GROUP_FFN1[2:])
    others = GROUP_MID[1:] + ("conv_w",)
    for n, (gn, dn, mn, vn) in zip(others, _adamw_whole([(weights[n], grads[n], moments_m[n], moments_v[n]) for n in others],
                                                        "adamw_others")):
        grads[n], delta[n], new_m[n], new_v[n] = gn, dn, mn, vn

    row = lambda a: a.reshape(1, -1)
    small = _adamw_small(total, [(row(weights[n]), row(moments_m[n]), row(moments_v[n])) for n, _ in SMALL], [size for _, size in SMALL])
    for (n, _), (gn, dn, mn, vn) in zip(SMALL, small):
        grads[n], delta[n], new_m[n], new_v[n] = gn.reshape(-1), dn.reshape(-1), mn.reshape(-1), vn.reshape(-1)

    return (loss_total, grad_x.reshape(nb, seq, d), *[view(n, src[n]) for src in (grads, delta, new_m, new_v) for n in WEIGHT_ORDER])
```

```python
import functools

import jax
import jax.numpy as jnp
from jax import lax
from jax.experimental import pallas as pl
from jax.experimental.pallas import tpu as pltpu

F32 = jnp.float32
BF16 = jnp.bfloat16

D_MODEL = 1024
N_HEADS = 8
QK_NOPE = 64
QK_ROPE = 32
QK_DIM = QK_NOPE + QK_ROPE
V_DIM = 64
HEAD_PAD = 128
Q_LORA = 384
KV_LORA = 256
ROPE_THETA = 10000.0
NORM_EPS = 1e-6
ATTN_SCALE = QK_DIM ** -0.5
MASK_VALUE = -1e30
N_CHIPS = 4
N_DEV = 8

ADAM_LR = 0.001
ADAM_B1 = 0.9
ADAM_B2 = 0.999
ADAM_EPS = 1e-08
ADAM_WD = 0.01
ADAM_STEP = 10

TOKEN_TILE = 256
PREP_TILE = 256
ATTN_TILE = 512
TN_TILE = 2048
VMEM_LIMIT = 56 * 1024 * 1024

M_COLS = 3 * D_MODEL
P_COLS = 2 * D_MODEL + Q_LORA + KV_LORA + HEAD_PAD
BIG_COLS = 5 * D_MODEL
LAT_COLS = Q_LORA + KV_LORA + HEAD_PAD

MESH_ID = pl.DeviceIdType.MESH
ANY = pl.BlockSpec(memory_space=pl.ANY)


def _params(semantics=None):
    return pltpu.CompilerParams(dimension_semantics=semantics, vmem_limit_bytes=VMEM_LIMIT)


class _Carried:
    def __init__(self, operands, out_shapes, aliases, n_sems, start, finish):
        self.operands, self.out_shapes, self.aliases, self.n_sems = list(operands), list(out_shapes), dict(aliases), n_sems
        self.start, self.finish = start, finish
        self.results = None


def _both(a, b):
    na, nao = len(a.operands), len(a.out_shapes)

    def start(ins, outs, sems, base):
        a.start(ins[:na], outs[:nao], sems, base)
        b.start(ins[na:], outs[nao:], sems, base + a.n_sems)

    def finish(ins, outs, sems, base):
        a.finish(ins[:na], outs[:nao], sems, base)
        b.finish(ins[na:], outs[nao:], sems, base + a.n_sems)

    aliases = dict(a.aliases)
    aliases.update({na + i: nao + o for i, o in b.aliases.items()})
    both = _Carried(a.operands + b.operands, a.out_shapes + b.out_shapes, aliases, a.n_sems + b.n_sems, start, finish)
    both.parts = (a, b)
    return both


def _set_results(carried, results):
    carried.results = list(results)
    if hasattr(carried, "parts"):
        a, b = carried.parts
        _set_results(a, results[:len(a.out_shapes)])
        _set_results(b, results[len(a.out_shapes):])


def _pallas(body, name, grid, in_specs, out_specs, out_shape, args, semantics, carried=None):
    if carried is None:
        return pl.pallas_call(body, name=name, grid=grid, in_specs=in_specs, out_specs=out_specs, out_shape=out_shape,
                              compiler_params=_params(semantics))(*args)
    n_in, n_out, n_ci, n_co = len(in_specs), len(out_specs), len(carried.operands), len(carried.out_shapes)

    def wrapped(*refs):
        ins, c_ins = refs[:n_in], refs[n_in:n_in + n_ci]
        outs, c_outs = refs[n_in + n_ci:n_in + n_ci + n_out], refs[n_in + n_ci + n_out:n_in + n_ci + n_out + n_co]
        sems = refs[-1]
        first = pl.program_id(0) == 0
        last = pl.program_id(0) == grid[0] - 1
        for axis in range(1, len(grid)):
            first = jnp.logical_and(first, pl.program_id(axis) == 0)
            last = jnp.logical_and(last, pl.program_id(axis) == grid[axis] - 1)

        @pl.when(first)
        def _():
            carried.start(c_ins, c_outs, sems, 0)

        body(*ins, *outs)

        @pl.when(last)
        def _():
            carried.finish(c_ins, c_outs, sems, 0)

    results = pl.pallas_call(
        wrapped, name=name, grid=grid, in_specs=list(in_specs) + [ANY] * n_ci, out_specs=list(out_specs) + [ANY] * n_co,
        out_shape=list(out_shape) + carried.out_shapes,
        input_output_aliases={n_in + i: n_out + o for i, o in carried.aliases.items()},
        scratch_shapes=[pltpu.SemaphoreType.DMA((carried.n_sems,))], compiler_params=_params(semantics))(*args, *carried.operands)
    _set_results(carried, results[n_out:])
    return results[:n_out]


def _resident(shape):
    nd = len(shape)
    return pl.BlockSpec(shape, lambda *_: (0,) * nd, pipeline_mode=pl.Buffered(1))


def _const(shape):
    nd = len(shape)
    return pl.BlockSpec(shape, lambda *_: (0,) * nd)


def _mm(a, b):
    return jnp.dot(a, b, preferred_element_type=F32)


def _mm_nt(a, b):
    return lax.dot_general(a, b, (((1,), (1,)), ((), ())), preferred_element_type=F32)


def _mm_tn(a, b):
    return lax.dot_general(a, b, (((0,), (0,)), ((), ())), preferred_element_type=F32)


def _bf(a):
    return a.astype(BF16)


def _sigmoid(a):
    return 1.0 / (1.0 + jnp.exp(-a))


def _rms(x, gain, n=None):
    n = x.shape[-1] if n is None else n
    r = lax.rsqrt(jnp.sum(x * x, axis=-1, keepdims=True) * (1.0 / n) + NORM_EPS)
    return (x * r) * gain, r


def _rms_bwd(x, r, gain, dh, n=None):
    n = x.shape[-1] if n is None else n
    u = dh * gain
    dx = r * u - x * ((r * r * r) * (jnp.sum(u * x, axis=-1, keepdims=True) * (1.0 / n)))
    dgain = jnp.sum(dh * (x * r), axis=0, keepdims=True)
    return dx, dgain


ROPE_HALF = QK_ROPE // 2


def _rope(t, rope):
    cos, s_lo, s_hi = rope
    return t * cos + pltpu.roll(t, HEAD_PAD - ROPE_HALF, 1) * s_lo + pltpu.roll(t, ROPE_HALF, 1) * s_hi


def _rope_bwd(dt, rope):
    cos, s_lo, s_hi = rope
    return dt * cos + pltpu.roll(dt * s_lo, ROPE_HALF, 1) + pltpu.roll(dt * s_hi, HEAD_PAD - ROPE_HALF, 1)


def _shift_down(u, prev8, k):
    s = pltpu.roll(u, k, 0)
    p = pltpu.roll(prev8, k, 0)
    row = lax.broadcasted_iota(jnp.int32, prev8.shape, 0)
    top = jnp.where(row < k, p, s[:8])
    return jnp.concatenate([top, s[8:]], axis=0)


def _shift_up(d, next8, k):
    tm = d.shape[0]
    s = pltpu.roll(d, tm - k, 0)
    n = pltpu.roll(next8, 8 - k, 0)
    row = lax.broadcasted_iota(jnp.int32, next8.shape, 0)
    bot = jnp.where(row >= 8 - k, n, s[tm - 8:])
    return jnp.concatenate([s[:tm - 8], bot], axis=0)


def _ffn_fwd(x, gain, wg, wu, wd, target, name, carried=None):
    t, d = x.shape
    nb, f, _ = wg.shape
    tm = TOKEN_TILE
    with_loss = target is not None

    def body(*refs):
        if with_loss:
            x_ref, g_ref, wg_ref, wu_ref, wd_ref, t_ref, out_ref, gate_ref, up_ref, act_ref, loss_ref = refs
        else:
            x_ref, g_ref, wg_ref, wu_ref, wd_ref, out_ref, gate_ref, up_ref, act_ref = refs
        xv = x_ref[...]
        h, _ = _rms(xv, g_ref[...])
        hb = _bf(h)
        y = jnp.zeros((tm, d), F32)
        nxt = (_mm_nt(hb, wg_ref[0]), _mm_nt(hb, wu_ref[0]))
        for j in range(nb):
            gate, up = nxt
            if j + 1 < nb:
                nxt = (_mm_nt(hb, wg_ref[j + 1]), _mm_nt(hb, wu_ref[j + 1]))
            act = _bf((gate * _sigmoid(gate)) * up)
            y = y + _mm(act, wd_ref[j])
            gate_ref[j] = _bf(gate)
            up_ref[j] = _bf(up)
            act_ref[j] = act
        out = xv + 0.5 * y
        if with_loss:
            err = out - t_ref[...]
            out_ref[...] = err * (1.0 / d)

            @pl.when(pl.program_id(0) == 0)
            def _():
                loss_ref[...] = jnp.zeros_like(loss_ref)

            part = jnp.sum(jnp.sum(err * err, axis=1, keepdims=True), axis=0, keepdims=True)
            loss_ref[...] += jnp.broadcast_to(part * (0.5 / d), loss_ref.shape)
        else:
            out_ref[...] = out

    tok = pl.BlockSpec((tm, d), lambda i: (i, 0))
    blk = pl.BlockSpec((nb, tm, f), lambda i: (0, i, 0))
    in_specs = [tok, _const((1, d)), _resident(wg.shape), _resident(wu.shape), _resident(wd.shape)]
    args = [x, gain, wg, wu, wd]
    out_shape = [jax.ShapeDtypeStruct((t, d), F32)] + [jax.ShapeDtypeStruct((nb, t, f), BF16)] * 3
    out_specs = [tok, blk, blk, blk]
    if with_loss:
        in_specs.append(tok)
        args.append(target)
        out_shape.append(jax.ShapeDtypeStruct((1, 128), F32))
        out_specs.append(_const((1, 128)))
    return _pallas(body, name, (t // tm,), in_specs, out_specs, out_shape, args, ("arbitrary",), carried)


def _ffn_bwd_x(x, gain, dout, gate, up, wg, wu, wd, name, carried=None):
    t, d = x.shape
    nb, f, _ = wg.shape
    tm = TOKEN_TILE

    def body(x_ref, g_ref, dout_ref, gate_ref, up_ref, wg_ref, wu_ref, wd_ref,
             dx_ref, dgain_ref, hb_ref, dgate_ref, dup_ref, dyb_ref):
        xv = x_ref[...]
        gain_v = g_ref[...]
        h, r = _rms(xv, gain_v)
        hb_ref[...] = _bf(h)
        dout_v = dout_ref[...]
        dyb = _bf(0.5 * dout_v)
        dyb_ref[...] = dyb
        dh = jnp.zeros((tm, d), F32)
        nxt = _mm_nt(dyb, wd_ref[0])
        for j in range(nb):
            dact = nxt
            if j + 1 < nb:
                nxt = _mm_nt(dyb, wd_ref[j + 1])
            gt = gate_ref[j].astype(F32)
            uv = up_ref[j].astype(F32)
            s = _sigmoid(gt)
            dup = _bf(dact * (gt * s))
            dgate = _bf((dact * uv) * (s * (1.0 + gt * (1.0 - s))))
            dh = dh + _mm(dgate, wg_ref[j]) + _mm(dup, wu_ref[j])
            dgate_ref[j] = dgate
            dup_ref[j] = dup
        dxn, dgain = _rms_bwd(xv, r, gain_v, dh)
        dx_ref[...] = dout_v + dxn

        @pl.when(pl.program_id(0) == 0)
        def _():
            dgain_ref[...] = jnp.zeros_like(dgain_ref)

        dgain_ref[...] += dgain

    tok = pl.BlockSpec((tm, d), lambda i: (i, 0))
    blk = pl.BlockSpec((nb, tm, f), lambda i: (0, i, 0))
    return _pallas(
        body, name, (t // tm,),
        [tok, _const((1, d)), tok, blk, blk, _resident(wg.shape), _resident(wu.shape), _resident(wd.shape)],
        [tok, _const((1, d)), tok, blk, blk, tok],
        [jax.ShapeDtypeStruct((t, d), F32), jax.ShapeDtypeStruct((1, d), F32), jax.ShapeDtypeStruct((t, d), BF16),
         jax.ShapeDtypeStruct((nb, t, f), BF16), jax.ShapeDtypeStruct((nb, t, f), BF16), jax.ShapeDtypeStruct((t, d), BF16)],
        (x, gain, dout, gate, up, wg, wu, wd), ("arbitrary",), carried)


def _tn_matmul(a, b, name, carried=None):
    t = a.shape[-2]
    k = a.shape[-1]
    n = b.shape[-1]
    tt = min(TN_TILE, t)
    nt = t // tt

    def body(a_ref, b_ref, o_ref):
        @pl.when(pl.program_id(1) == 0)
        def _():
            o_ref[...] = jnp.zeros_like(o_ref)

        o_ref[...] += _mm_tn(a_ref[...], b_ref[...])

    g = a.shape[0] if a.ndim == 3 else b.shape[0]
    a_spec = (pl.BlockSpec((None, tt, k), lambda gi, ti: (gi, ti, 0)) if a.ndim == 3
              else pl.BlockSpec((tt, k), lambda gi, ti: (ti, 0)))
    b_spec = (pl.BlockSpec((None, tt, n), lambda gi, ti: (gi, ti, 0)) if b.ndim == 3
              else pl.BlockSpec((tt, n), lambda gi, ti: (ti, 0)))
    o_spec = pl.BlockSpec((None, k, n), lambda gi, ti: (gi, 0, 0))
    out_shape = jax.ShapeDtypeStruct((g, k, n), F32)
    return _pallas(body, name, (g, nt), [a_spec, b_spec], [o_spec], [out_shape], (a, b), ("arbitrary", "arbitrary"), carried)[0]


def _tn_rows(a, b, out, chunks, width, rows_out, name):
    t, n = b.shape
    tt = min(TN_TILE, t)
    nt = t // tt

    def body(blocks_ref, a_ref, b_ref, *rest):
        out_ref, acc, sem = rest[-3:]
        g, ti = pl.program_id(0), pl.program_id(1)

        @pl.when(ti == 0)
        def _():
            acc[...] = jnp.zeros_like(acc)

        acc[...] += _mm_tn(a_ref[...], b_ref[...])
        for gi, (_, ranges) in enumerate(chunks):
            @pl.when(jnp.logical_and(g == gi, ti == nt - 1))
            def _(ranges=ranges):
                for row, first, count in ranges:
                    cp = pltpu.make_async_copy(acc.at[first:first + count], out_ref.at[row:row + count], sem)
                    cp.start()
                    cp.wait()

    blocks = jnp.asarray([c[0] for c in chunks], jnp.int32)
    grid_spec = pltpu.PrefetchScalarGridSpec(
        num_scalar_prefetch=1, grid=(len(chunks), nt),
        in_specs=[pl.BlockSpec((tt, width), lambda g, ti, blocks_ref: (ti, blocks_ref[g])),
                  pl.BlockSpec((tt, n), lambda g, ti, blocks_ref: (ti, 0))] + ([ANY] if out is not None else []),
        out_specs=ANY, scratch_shapes=[pltpu.VMEM((width, n), F32), pltpu.SemaphoreType.DMA])
    args = (blocks, a, b) + ((out,) if out is not None else ())
    return pl.pallas_call(body, name=name, grid_spec=grid_spec, out_shape=jax.ShapeDtypeStruct((rows_out, n), F32),
                          input_output_aliases={3: 0} if out is not None else {},
                          compiler_params=_params(("arbitrary", "arbitrary")))(*args)


ROW_QKV, ROW_KR, ROW_XC = 0, Q_LORA + KV_LORA, Q_LORA + KV_LORA + QK_ROPE
ROW_GB, ROW_GC, ROW_GL = ROW_XC + D_MODEL, ROW_XC + 2 * D_MODEL, ROW_XC + 3 * D_MODEL
BIG_FROM_ROWS = ((0, ROW_GB, D_MODEL), (D_MODEL, ROW_GL, 2 * D_MODEL), (3 * D_MODEL, ROW_XC, D_MODEL), (4 * D_MODEL, ROW_GC, D_MODEL))


def _inproj_fwd(x1, gain, w_in, w_kr, qa_gain, kva_gain, qh_gain, kh_gain, w_uq, w_uk, w_uv, w_uvt, rope, carried=None):
    t, d = x1.shape
    tm = TOKEN_TILE
    chunk = 512
    chunks = []
    for col, row, size in BIG_FROM_ROWS:
        chunks += [(col + o, row + o, chunk) for o in range(0, size, chunk)]
    of_head = [[c for k, c in enumerate(chunks) if k * N_HEADS // len(chunks) == hd] for hd in range(N_HEADS)]

    def body(x_ref, g_ref, win_ref, wkr_ref, qa_ref, kva_ref, qh_ref, kh_ref, wuq_ref, wuk_ref, wuv_ref, wuvt_ref, cos_ref, slo_ref,
             shi_ref, hb_ref, big_ref, lat_ref, q_ref, k_ref, v_ref, vt_ref):
        h, _ = _rms(x_ref[...], g_ref[...])
        hb = _bf(h)
        hb_ref[...] = hb
        k_rope = _mm_nt(hb, wkr_ref[...])
        lat = jnp.concatenate([_mm_nt(hb, win_ref[ROW_QKV:ROW_KR, :]), k_rope], axis=1)
        lat_ref[...] = lat
        cq, _ = _rms(lat[:, :Q_LORA], qa_ref[...])
        ckv, _ = _rms(lat[:, Q_LORA:Q_LORA + KV_LORA], kva_ref[...])
        cqb = _bf(cq)
        ckvb = _bf(ckv)
        rope_v = (cos_ref[...], slo_ref[...], shi_ref[...])
        q_all = _mm(cqb, wuq_ref[...])
        k_all = _mm(ckvb, wuk_ref[...])
        v_ref[...] = _bf(_mm(ckvb, wuv_ref[...]))
        vt_all = _mm_nt(wuvt_ref[...], ckvb)
        for hd in range(N_HEADS):
            for col, row, size in of_head[hd]:
                big_ref[:, col:col + size] = _mm_nt(hb, win_ref[row:row + size, :])
            lanes = slice(hd * HEAD_PAD, (hd + 1) * HEAD_PAD)
            qn, _ = _rms(q_all[:, lanes], qh_ref[...], QK_DIM)
            q_ref[hd] = _bf(_rope(qn, rope_v))
            kn, _ = _rms(k_all[:, lanes] + k_rope, kh_ref[...], QK_DIM)
            k_ref[hd] = _bf(_rope(kn, rope_v))
            vt_ref[hd] = _bf(vt_all[hd * V_DIM:(hd + 1) * V_DIM])

    tok = lambda c: pl.BlockSpec((tm, c), lambda i: (i, 0))
    head = lambda c: pl.BlockSpec((N_HEADS, tm, c), lambda i: (0, i, 0))
    return _pallas(
        body, "inproj_fwd", (t // tm,),
        [tok(d), _const((1, d)), _resident(w_in.shape), _resident(w_kr.shape), _const((1, Q_LORA)), _const((1, KV_LORA)),
         _const((1, HEAD_PAD)), _const((1, HEAD_PAD)), _resident(w_uq.shape), _resident(w_uk.shape),
         _resident(w_uv.shape), _resident(w_uvt.shape), tok(HEAD_PAD), tok(HEAD_PAD), tok(HEAD_PAD)],
        [tok(d), tok(BIG_COLS), tok(LAT_COLS), head(HEAD_PAD), head(HEAD_PAD), tok(N_HEADS * V_DIM),
         pl.BlockSpec((N_HEADS, V_DIM, tm), lambda i: (0, 0, i))],
        [jax.ShapeDtypeStruct((t, d), BF16), jax.ShapeDtypeStruct((t, BIG_COLS), F32),
         jax.ShapeDtypeStruct((t, LAT_COLS), F32), jax.ShapeDtypeStruct((N_HEADS, t, HEAD_PAD), BF16),
         jax.ShapeDtypeStruct((N_HEADS, t, HEAD_PAD), BF16), jax.ShapeDtypeStruct((t, N_HEADS * V_DIM), BF16),
         jax.ShapeDtypeStruct((N_HEADS, V_DIM, t), BF16)],
        (x1, gain, w_in, w_kr, qa_gain, kva_gain, qh_gain, kh_gain, w_uq, w_uk, w_uv, w_uvt, *rope), ("arbitrary",), carried)


EXP2_SCALE = ATTN_SCALE * 1.4426950408889634


def _diagonal_keep(tk, tq):
    return lax.broadcasted_iota(jnp.int32, (tk, tq), 0) <= lax.broadcasted_iota(jnp.int32, (tk, tq), 1)


def _attn_fwd(q, k, vt, seq, carried=None):
    _, t, _ = q.shape
    nseq = t // seq
    tq = tk = ATTN_TILE
    nq = seq // tq

    def body(q_ref, k_ref, vt_ref, o_ref, lse_ref):
        i = pl.program_id(1)
        qs = [q_ref[h] for h in range(N_HEADS)]
        keep = _diagonal_keep(tk, tq)

        def scores(h, k0):
            return _mm_nt(k_ref[h, pl.ds(k0, tk), :], qs[h])

        def update(h, st, state, k0, diagonal):
            m, l, acc = state
            if diagonal:
                st = jnp.where(keep, st, MASK_VALUE)
            m_new = jnp.maximum(m, jnp.max(st, axis=0, keepdims=True))
            pt = jnp.exp2((st - m_new) * EXP2_SCALE)
            alpha = jnp.exp2((m - m_new) * EXP2_SCALE)
            l_new = alpha * l + jnp.sum(pt, axis=0, keepdims=True)
            return m_new, l_new, alpha * acc + _mm(vt_ref[h, :, pl.ds(k0, tk)], _bf(pt))

        def tiles(states, k0, diagonal):
            st, new = scores(0, k0), []
            for h in range(N_HEADS):
                st_next = scores(h + 1, k0) if h + 1 < N_HEADS else None
                new.append(update(h, st, states[h], k0, diagonal))
                st = st_next
            return tuple(new)

        init = tuple((jnp.full((1, tq), MASK_VALUE, F32), jnp.zeros((1, tq), F32), jnp.zeros((V_DIM, tq), F32))
                     for _ in range(N_HEADS))
        states = lax.fori_loop(0, i, lambda j, s: tiles(s, pl.multiple_of(j * tk, tk), False), init)
        states = tiles(states, pl.multiple_of(i * tk, tk), True)
        outs = []
        for h in range(N_HEADS):
            m, l, acc = states[h]
            outs.append((acc / l).T)
            lse_ref[h] = m * EXP2_SCALE + jnp.log2(l)
        o_ref[...] = _bf(jnp.concatenate(outs, axis=-1))

    return _pallas(
        body, "attn_fwd", (nseq, nq),
        [pl.BlockSpec((N_HEADS, tq, HEAD_PAD), lambda b, i: (0, b * nq + i, 0)),
         pl.BlockSpec((N_HEADS, seq, HEAD_PAD), lambda b, i: (0, b, 0)),
         pl.BlockSpec((N_HEADS, V_DIM, seq), lambda b, i: (0, 0, b))],
        [pl.BlockSpec((tq, N_HEADS * V_DIM), lambda b, i: (b * nq + i, 0)),
         pl.BlockSpec((N_HEADS, 1, tq), lambda b, i: (0, 0, b * nq + i))],
        [jax.ShapeDtypeStruct((t, N_HEADS * V_DIM), BF16), jax.ShapeDtypeStruct((N_HEADS, 1, t), F32)],
        (q, k, vt), ("arbitrary", "arbitrary"), carried)


ATTN_BWD_HEADS = 4


def _attn_bwd(q, k, v, do, lse, delta, seq, carried=None):
    _, t, _ = q.shape
    nseq = t // seq
    tq = tk = ATTN_TILE
    n = seq // tq
    hb = ATTN_BWD_HEADS

    def body(q_ref, k_ref, v_ref, do_ref, lse_ref, delta_ref, dq_ref, dk_ref, dv_ref):
        dq_ref[...] = jnp.zeros_like(dq_ref)
        dk_ref[...] = jnp.zeros_like(dk_ref)
        dv_ref[...] = jnp.zeros_like(dv_ref)
        keep = _diagonal_keep(tk, tq)

        def tile(h, k0, q0, diagonal):
            kj = k_ref[h, pl.ds(k0, tk), :]
            qi = q_ref[h, pl.ds(q0, tq), :]
            doi = _bf(do_ref[pl.ds(q0, tq), h * V_DIM:(h + 1) * V_DIM])
            st = _mm_nt(kj, qi)
            if diagonal:
                st = jnp.where(keep, st, MASK_VALUE)
            pt = jnp.exp2(st * EXP2_SCALE - lse_ref[h, :, pl.ds(q0, tq)])
            dv_ref[pl.ds(k0, tk), h * V_DIM:(h + 1) * V_DIM] += _mm(_bf(pt), doi)
            dpt = _mm_nt(v_ref[pl.ds(k0, tk), h * V_DIM:(h + 1) * V_DIM], doi)
            dst = _bf((pt * (dpt - delta_ref[pl.ds(h, 1), pl.ds(q0, tq)])) * ATTN_SCALE)
            dk_ref[h, pl.ds(k0, tk), :] += _mm(dst, qi)
            dq_ref[h, pl.ds(q0, tq), :] += _mm_tn(dst, kj)

        def kv_step(j, _):
            k0 = pl.multiple_of(j * tk, tk)
            for h in range(hb):
                tile(h, k0, k0, True)

            def q_step(i, _):
                q0 = pl.multiple_of(i * tq, tq)
                for h in range(hb):
                    tile(h, k0, q0, False)
                return 0

            lax.fori_loop(j + 1, n, q_step, 0)
            return 0

        lax.fori_loop(0, n, kv_step, 0)

    hspec = lambda c: pl.BlockSpec((hb, seq, c), lambda b, g: (g, b, 0))
    cols = pl.BlockSpec((seq, hb * V_DIM), lambda b, g: (b, g))
    return _pallas(
        body, "attn_bwd", (nseq, N_HEADS // hb),
        [hspec(HEAD_PAD), hspec(HEAD_PAD), cols, cols,
         pl.BlockSpec((hb, 1, seq), lambda b, g: (g, 0, b)), pl.BlockSpec((None, hb, seq), lambda b, g: (g, 0, b))],
        [hspec(HEAD_PAD), hspec(HEAD_PAD), cols],
        [jax.ShapeDtypeStruct((N_HEADS, t, HEAD_PAD), F32), jax.ShapeDtypeStruct((N_HEADS, t, HEAD_PAD), F32),
         jax.ShapeDtypeStruct((t, N_HEADS * V_DIM), F32)],
        (q, k, v, do, lse, delta), ("arbitrary", "arbitrary"), carried)


def _merged_mixers(o_ref, gb_ref, gla_ref, glb_ref, xc_ref, gc_ref, xcp_ref, gcp_ref, bias_ref, cw_ref, wpa_ref, wpc_ref,
                   first_of_seq):
    y_a = _mm(o_ref[...], wpa_ref[...])
    gb = gb_ref[...]
    u = gc_ref[...] * xc_ref[...]
    u_prev = jnp.where(first_of_seq, 0.0, gcp_ref[...] * xcp_ref[...])
    cw = cw_ref[...]
    z = cw[2:3] * u + cw[1:2] * _shift_down(u, u_prev, 1) + cw[0:1] * _shift_down(u, u_prev, 2)
    gbz = _bf(gb * z)
    y_b = _mm(gbz, wpc_ref[...])
    bias = bias_ref[...]
    gate_a = _sigmoid(gla_ref[...] + bias[:, :D_MODEL])
    gate_b = _sigmoid(glb_ref[...] + bias[:, D_MODEL:])
    return _bf(gate_a * y_a + gate_b * y_b)


def _mixer_specs(tm, seq):
    d = D_MODEL
    tok = pl.BlockSpec((tm, d), lambda i: (i, 0))
    col = lambda c: pl.BlockSpec((tm, d), lambda i: (i, c))
    prev = lambda c: pl.BlockSpec((8, d), lambda i: (jnp.maximum(i * (tm // 8) - 1, 0), c))
    o_spec = pl.BlockSpec((tm, N_HEADS * V_DIM), lambda i: (i, 0))
    fwd_specs = [o_spec, col(0), col(1), col(2), col(3), col(4), prev(3), prev(4), _const((1, 2 * d)), _const((3, d)),
                 _resident((N_HEADS * V_DIM, d)), _resident((d, d)), _resident((d, d))]
    return tok, fwd_specs


def _mix_fwd(x1, o, big, gate_bias, conv_w, w_pa, w_pc, w_out, seq, carried=None):
    t, d = x1.shape
    tm = TOKEN_TILE
    tiles_per_seq = seq // tm

    def body(x_ref, o_ref, gb_ref, gla_ref, glb_ref, xc_ref, gc_ref, xcp_ref, gcp_ref, bias_ref, cw_ref, wpa_ref, wpc_ref,
             wout_ref, x2_ref):
        first = pl.program_id(0) % tiles_per_seq == 0
        merged = _merged_mixers(o_ref, gb_ref, gla_ref, glb_ref, xc_ref, gc_ref, xcp_ref, gcp_ref, bias_ref, cw_ref, wpa_ref,
                                wpc_ref, first)
        x2_ref[...] = x_ref[...] + _mm(merged, wout_ref[...])

    tok, fwd_specs = _mixer_specs(tm, seq)
    return _pallas(body, "mix_fwd", (t // tm,), [tok] + fwd_specs, [tok], [jax.ShapeDtypeStruct((t, d), F32)],
                   (x1, o, big, big, big, big, big, big, big, gate_bias, conv_w, w_pa, w_pc, w_out), ("arbitrary",), carried)[0]


def _mix_bwd(dx2, o, big, gate_bias, conv_w, w_pa, w_pc, w_out, seq, carried=None):
    t, d = dx2.shape
    tm = TOKEN_TILE
    tiles_per_seq = seq // tm
    hv = N_HEADS * V_DIM

    def body(dx_ref, o_ref, gb_ref, gla_ref, glb_ref, xc_ref, gc_ref, xcp_ref, gcp_ref, bias_ref, cw_ref, wpa_ref, wpc_ref,
             wout_ref, do_ref, delta_ref, dz_ref, dm_ref, dbias_ref, dwpa_ref, dwpc_ref, dwout_ref):
        @pl.when(pl.program_id(0) == 0)
        def _():
            dbias_ref[...] = jnp.zeros_like(dbias_ref)
            dwpa_ref[...] = jnp.zeros_like(dwpa_ref)
            dwpc_ref[...] = jnp.zeros_like(dwpc_ref)
            dwout_ref[...] = jnp.zeros_like(dwout_ref)

        first = pl.program_id(0) % tiles_per_seq == 0
        dxb = _bf(dx_ref[...])
        dmerged = _mm_nt(dxb, wout_ref[...])
        y_a = _mm(o_ref[...], wpa_ref[...])
        bias = bias_ref[...]
        gate_a = _sigmoid(gla_ref[...] + bias[:, :d])
        gate_b = _sigmoid(glb_ref[...] + bias[:, d:])
        dya = _bf(dmerged * gate_a)
        dyb = _bf(dmerged * gate_b)
        do_v = _mm_nt(dya, wpa_ref[...])
        dgz = _mm_nt(dyb, wpc_ref[...])
        dwpa_ref[...] += _mm_tn(o_ref[...], dya)
        gb = gb_ref[...]
        u = gc_ref[...] * xc_ref[...]
        u_prev = jnp.where(first, 0.0, gcp_ref[...] * xcp_ref[...])
        cw = cw_ref[...]
        z = cw[2:3] * u + cw[1:2] * _shift_down(u, u_prev, 1) + cw[0:1] * _shift_down(u, u_prev, 2)
        gbz = _bf(gb * z)
        y_b = _mm(gbz, wpc_ref[...])
        dwpc_ref[...] += _mm_tn(gbz, dyb)
        do_ref[...] = do_v
        head = lax.broadcasted_iota(jnp.int32, (N_HEADS, hv), 0) * V_DIM
        col = lax.broadcasted_iota(jnp.int32, (N_HEADS, hv), 1)
        in_head = ((col >= head) & (col < head + V_DIM)).astype(F32)
        delta_ref[...] = lax.dot_general(in_head, do_v * o_ref[...].astype(F32), (((1,), (1,)), ((), ())),
                                         precision=lax.Precision.HIGHEST, preferred_element_type=F32)
        dz_ref[...] = dgz * gb
        dm_ref[:, :d] = _bf(dgz * z)
        merged = _bf(gate_a * y_a + gate_b * y_b)
        dwout_ref[...] += _mm_tn(merged, dxb)
        dla = (dmerged * y_a) * (gate_a * (1.0 - gate_a))
        dlb = (dmerged * y_b) * (gate_b * (1.0 - gate_b))
        dbias_ref[:, :d] += jnp.sum(dla, axis=0, keepdims=True)
        dbias_ref[:, d:] += jnp.sum(dlb, axis=0, keepdims=True)
        dm_ref[:, d:2 * d] = _bf(dla)
        dm_ref[:, 2 * d:] = _bf(dlb)

    tok, fwd_specs = _mixer_specs(tm, seq)
    return _pallas(
        body, "mix_bwd", (t // tm,), [tok] + fwd_specs,
        [pl.BlockSpec((tm, hv), lambda i: (i, 0)), pl.BlockSpec((N_HEADS, tm), lambda i: (0, i)), tok,
         pl.BlockSpec((tm, M_COLS), lambda i: (i, 0)), _const((1, 2 * d)), _const((hv, d)), _const((d, d)), _const((d, d))],
        [jax.ShapeDtypeStruct((t, hv), F32), jax.ShapeDtypeStruct((N_HEADS, t), F32), jax.ShapeDtypeStruct((t, d), F32),
         jax.ShapeDtypeStruct((t, M_COLS), BF16), jax.ShapeDtypeStruct((1, 2 * d), F32), jax.ShapeDtypeStruct((hv, d), F32),
         jax.ShapeDtypeStruct((d, d), F32), jax.ShapeDtypeStruct((d, d), F32)],
        (dx2, o, big, big, big, big, big, big, big, gate_bias, conv_w, w_pa, w_pc, w_out), ("arbitrary",), carried)


def _prep_bwd(lat, big, dz, dq, dk, dv, qa_gain, kva_gain, qh_gain, kh_gain, w_uq, w_uk, w_uv, rope, conv_w, seq, carried=None):
    t = lat.shape[0]
    d = D_MODEL
    tm = PREP_TILE
    tiles_per_seq = seq // tm
    last_blk = t // 8 - 1

    def body(lat_ref, xc_ref, gc_ref, dz_ref, dzn_ref, dq_ref, dk_ref, dv_ref, qa_ref, kva_ref, qh_ref, kh_ref, wuq_ref, wuk_ref,
             wuv_ref, cos_ref, slo_ref, shi_ref, cw_ref,
             dp_ref, dwuq_ref, dwuk_ref, dwuv_ref, dqa_ref, dkva_ref, dqh_ref, dkh_ref, dcw_ref):
        pid = pl.program_id(0)

        @pl.when(pid == 0)
        def _():
            for r in (dwuq_ref, dwuk_ref, dwuv_ref, dqa_ref, dkva_ref, dqh_ref, dkh_ref, dcw_ref):
                r[...] = jnp.zeros_like(r)

        lat_v = lat_ref[...]
        q_lat = lat_v[:, :Q_LORA]
        kv_lat = lat_v[:, Q_LORA:Q_LORA + KV_LORA]
        k_rope = lat_v[:, Q_LORA + KV_LORA:]
        qa_gain_v = qa_ref[...]
        kva_gain_v = kva_ref[...]
        qh_gain_v = qh_ref[...]
        kh_gain_v = kh_ref[...]
        cq, rq = _rms(q_lat, qa_gain_v)
        ckv, rkv = _rms(kv_lat, kva_gain_v)
        cqb = _bf(cq)
        ckvb = _bf(ckv)
        rope_v = (cos_ref[...], slo_ref[...], shi_ref[...])
        lane = lax.broadcasted_iota(jnp.int32, (tm, HEAD_PAD), 1)
        rope_lanes = (lane >= QK_NOPE) & (lane < QK_DIM)
        dk_rope = jnp.zeros((tm, HEAD_PAD), F32)
        dqh_gain = jnp.zeros((1, HEAD_PAD), F32)
        dkh_gain = jnp.zeros((1, HEAD_PAD), F32)
        q_all = _mm(cqb, wuq_ref[...])
        k_all = _mm(ckvb, wuk_ref[...])
        dvb = _bf(dv_ref[...])
        dckv = _mm_nt(dvb, wuv_ref[...])
        dwuv_ref[...] += _mm_tn(ckvb, dvb)

        last = pid % tiles_per_seq == tiles_per_seq - 1
        dzv = dz_ref[...]
        dz_next = jnp.where(last, 0.0, dzn_ref[...])
        dz1 = _shift_up(dzv, dz_next, 1)
        dz2 = _shift_up(dzv, dz_next, 2)
        cw = cw_ref[...]
        xc = xc_ref[...]
        gc = gc_ref[...]
        u = gc * xc
        du = cw[2:3] * dzv + cw[1:2] * dz1 + cw[0:1] * dz2
        dp_ref[:, :d] = _bf(du * gc)
        dp_ref[:, d:2 * d] = _bf(du * xc)
        dcw_ref[0:1, :] += jnp.sum(dz2 * u, axis=0, keepdims=True)
        dcw_ref[1:2, :] += jnp.sum(dz1 * u, axis=0, keepdims=True)
        dcw_ref[2:3, :] += jnp.sum(dzv * u, axis=0, keepdims=True)

        dcq = jnp.zeros((tm, Q_LORA), F32)
        half = N_HEADS // 2
        for part in range(2):
            dq_heads, dk_heads = [], []
            for hd in range(part * half, (part + 1) * half):
                lanes = slice(hd * HEAD_PAD, (hd + 1) * HEAD_PAD)
                q_pre = q_all[:, lanes]
                _, rr = _rms(q_pre, qh_gain_v, QK_DIM)
                dq_pre, dg = _rms_bwd(q_pre, rr, qh_gain_v, _rope_bwd(dq_ref[hd], rope_v), QK_DIM)
                dqh_gain = dqh_gain + dg
                dq_heads.append(_bf(dq_pre))

                k_pre = k_all[:, lanes] + k_rope
                _, rr = _rms(k_pre, kh_gain_v, QK_DIM)
                dk_pre, dg = _rms_bwd(k_pre, rr, kh_gain_v, _rope_bwd(dk_ref[hd], rope_v), QK_DIM)
                dkh_gain = dkh_gain + dg
                dk_rope = dk_rope + jnp.where(rope_lanes, dk_pre, 0.0)
                dk_heads.append(_bf(dk_pre))
            dq_part = jnp.concatenate(dq_heads, axis=1)
            dk_part = jnp.concatenate(dk_heads, axis=1)
            cols = slice(part * half * HEAD_PAD, (part + 1) * half * HEAD_PAD)
            dcq = dcq + _mm_nt(dq_part, wuq_ref[:, cols])
            dckv = dckv + _mm_nt(dk_part, wuk_ref[:, cols])
            dwuq_ref[:, cols] += _mm_tn(cqb, dq_part)
            dwuk_ref[:, cols] += _mm_tn(ckvb, dk_part)
        dqh_ref[...] += dqh_gain
        dkh_ref[...] += dkh_gain
        dq_lat, dg = _rms_bwd(q_lat, rq, qa_gain_v, dcq)
        dqa_ref[...] += dg
        dkv_lat, dg = _rms_bwd(kv_lat, rkv, kva_gain_v, dckv)
        dkva_ref[...] += dg
        dp_ref[:, 2 * d:2 * d + Q_LORA] = _bf(dq_lat)
        dp_ref[:, 2 * d + Q_LORA:2 * d + Q_LORA + KV_LORA] = _bf(dkv_lat)
        dp_ref[:, 2 * d + Q_LORA + KV_LORA:] = _bf(dk_rope)

    tok = lambda c: pl.BlockSpec((tm, c), lambda i: (i, 0))
    col = lambda c: pl.BlockSpec((tm, d), lambda i: (i, c))
    head = lambda c: pl.BlockSpec((N_HEADS, tm, c), lambda i: (0, i, 0))
    nxt = pl.BlockSpec((8, d), lambda i: (jnp.minimum((i + 1) * (tm // 8), last_blk), 0))
    return _pallas(
        body, "prep_bwd", (t // tm,),
        [tok(LAT_COLS), col(3), col(4), tok(d), nxt, head(HEAD_PAD), head(HEAD_PAD), tok(N_HEADS * V_DIM),
         _const((1, Q_LORA)), _const((1, KV_LORA)), _const((1, HEAD_PAD)), _const((1, HEAD_PAD)),
         _resident(w_uq.shape), _resident(w_uk.shape), _resident(w_uv.shape), tok(HEAD_PAD), tok(HEAD_PAD), tok(HEAD_PAD),
         _const((3, d))],
        [tok(P_COLS), _const(w_uq.shape), _const(w_uk.shape), _const(w_uv.shape), _const((1, Q_LORA)),
         _const((1, KV_LORA)), _const((1, HEAD_PAD)), _const((1, HEAD_PAD)), _const((3, d))],
        [jax.ShapeDtypeStruct((t, P_COLS), BF16), jax.ShapeDtypeStruct(w_uq.shape, F32),
         jax.ShapeDtypeStruct(w_uk.shape, F32), jax.ShapeDtypeStruct(w_uv.shape, F32),
         jax.ShapeDtypeStruct((1, Q_LORA), F32), jax.ShapeDtypeStruct((1, KV_LORA), F32),
         jax.ShapeDtypeStruct((1, HEAD_PAD), F32), jax.ShapeDtypeStruct((1, HEAD_PAD), F32), jax.ShapeDtypeStruct((3, d), F32)],
        (lat, big, big, dz, dz, dq, dk, dv, qa_gain, kva_gain, qh_gain, kh_gain, w_uq, w_uk, w_uv, *rope, conv_w),
        ("arbitrary",), carried)


def _inproj_bwd(x1, gain, dx2, dm, dp, w_in, w_kr, carried=None):
    t, d = x1.shape
    tm = TOKEN_TILE

    def body(x_ref, g_ref, dx2_ref, dm_ref, dp_ref, win_ref, wkr_ref, dx1_ref, dgain_ref):
        xv = x_ref[...]
        gain_v = g_ref[...]
        _, r = _rms(xv, gain_v)
        dh = (_mm(dm_ref[:, :d], win_ref[ROW_GB:ROW_GC, :]) + _mm(dm_ref[:, d:], win_ref[ROW_GL:, :])
              + _mm(dp_ref[:, :d], win_ref[ROW_XC:ROW_GB, :]) + _mm(dp_ref[:, d:2 * d], win_ref[ROW_GC:ROW_GL, :])
              + _mm(dp_ref[:, 2 * d:2 * d + ROW_KR], win_ref[ROW_QKV:ROW_KR, :]) + _mm(dp_ref[:, 2 * d + ROW_KR:], wkr_ref[...]))
        dxn, dgain = _rms_bwd(xv, r, gain_v, dh)
        dx1_ref[...] = dx2_ref[...] + dxn

        @pl.when(pl.program_id(0) == 0)
        def _():
            dgain_ref[...] = jnp.zeros_like(dgain_ref)

        dgain_ref[...] += dgain

    tok = lambda c: pl.BlockSpec((tm, c), lambda i: (i, 0))
    return _pallas(
        body, "inproj_bwd", (t // tm,),
        [tok(d), _const((1, d)), tok(d), tok(M_COLS), tok(P_COLS), _resident(w_in.shape), _resident(w_kr.shape)],
        [tok(d), _const((1, d))], [jax.ShapeDtypeStruct((t, d), F32), jax.ShapeDtypeStruct((1, d), F32)],
        (x1, gain, dx2, dm, dp, w_in, w_kr), ("arbitrary",), carried)


def _adamw(quads, name, carried=None):
    k = len(quads)
    rows, cols = quads[0][0].shape
    tr, tc = rows, cols
    for cand in (512, 352, 256, 192, 128, 64):
        if rows % cand == 0 and rows > cand:
            tr = cand
            break
    if tr == rows and rows * cols > 512 * 1024 and cols % 256 == 0:
        tc = 256
    while k * 14 * tr * tc * 4 > (VMEM_LIMIT * 3) // 4 and tr % 16 == 0:
        tr //= 2

    def body(*refs):
        for i in range(k):
            w_ref, g_ref, m_ref, v_ref = refs[4 * i:4 * i + 4]
            delta_ref, nm_ref, nv_ref = refs[4 * k + 3 * i:4 * k + 3 * i + 3]
            delta_ref[...], nm_ref[...], nv_ref[...] = _adamw_update(w_ref[...], g_ref[...], m_ref[...], v_ref[...])

    spec = pl.BlockSpec((tr, tc), lambda i, j: (i, j))
    shape = jax.ShapeDtypeStruct((rows, cols), F32)
    outs = _pallas(body, name, (rows // tr, cols // tc), [spec] * (4 * k), [spec] * (3 * k), [shape] * (3 * k),
                   [a for quad in quads for a in quad], ("arbitrary", "arbitrary"), carried)
    return [tuple(outs[3 * i:3 * i + 3]) for i in range(k)]


def _adamw_update(w, g, m, v):
    nm = ADAM_B1 * m + (1.0 - ADAM_B1) * g
    nv = ADAM_B2 * v + (1.0 - ADAM_B2) * (g * g)
    m_hat = nm * (1.0 / (1.0 - ADAM_B1 ** ADAM_STEP))
    v_hat = nv * (1.0 / (1.0 - ADAM_B2 ** ADAM_STEP))
    return -ADAM_LR * (m_hat / (jnp.sqrt(v_hat) + ADAM_EPS) + ADAM_WD * w), nm, nv


def _adamw_whole(quads, name):
    k = len(quads)

    def body(*refs):
        for i in range(k):
            w_ref, g_ref, m_ref, v_ref = refs[4 * i:4 * i + 4]
            g_out, delta_ref, nm_ref, nv_ref = refs[4 * k + 4 * i:4 * k + 4 * i + 4]
            gv = g_ref[...]
            g_out[...] = gv
            delta_ref[...], nm_ref[...], nv_ref[...] = _adamw_update(w_ref[...], gv, m_ref[...], v_ref[...])

    vm = pl.BlockSpec(memory_space=pltpu.VMEM)
    outs = pl.pallas_call(body, name=name, in_specs=[vm] * (4 * k), out_specs=[vm] * (4 * k),
                          out_shape=[jax.ShapeDtypeStruct(q[0].shape, F32) for q in quads for _ in range(4)],
                          compiler_params=_params())(*[a for quad in quads for a in quad])
    return [tuple(outs[4 * i:4 * i + 4]) for i in range(k)]


def _adamw_small(packed_grads, triples, segments):
    k = len(triples)

    def body(*refs):
        g_ref = refs[0]
        off = 0
        for i in range(k):
            w_ref, m_ref, v_ref = refs[1 + 3 * i:4 + 3 * i]
            g_out, delta_ref, nm_ref, nv_ref = refs[1 + 3 * k + 4 * i:5 + 3 * k + 4 * i]
            gv = g_ref[:, off:off + w_ref.shape[1]]
            g_out[...] = gv
            delta_ref[...], nm_ref[...], nv_ref[...] = _adamw_update(w_ref[...], gv, m_ref[...], v_ref[...])
            off += segments[i]

    vm = pl.BlockSpec(memory_space=pltpu.VMEM)
    outs = pl.pallas_call(
        body, name="adamw_small", in_specs=[vm] * (1 + 3 * k), out_specs=[vm] * (4 * k),
        out_shape=[jax.ShapeDtypeStruct(w.shape, F32) for w, _, _ in triples for _ in range(4)],
    )(packed_grads, *[a for triple in triples for a in triple])
    return [tuple(outs[4 * i:4 * i + 4]) for i in range(k)]


def _place():
    x, y, c = lax.axis_index("x"), lax.axis_index("y"), lax.axis_index("c")
    other_chips = [(1 - x, y), (x, 1 - y), (1 - x, 1 - y)]
    return x, y, c, other_chips


def _remote(src, dst, sems, send, recv, device):
    return pltpu.make_async_remote_copy(src_ref=src, dst_ref=dst, send_sem=sems.at[send], recv_sem=sems.at[recv],
                                        device_id=device, device_id_type=MESH_ID)


def _cast_shards(shards, out_dtypes, n_first):
    n = len(shards)
    out_shape = [jax.ShapeDtypeStruct((N_CHIPS,) + s.shape, dt) for s, dt in zip(shards, out_dtypes)]
    gather = _gather_carried(out_shape[:n_first])

    def body(*refs):
        ins, outs, stage, sems = refs[:n], refs[n:2 * n], refs[2 * n:3 * n], refs[3 * n]
        x, y, _, _ = _place()
        me = 2 * x + y

        def cast(first, last):
            copies = []
            for w in range(first, last):
                stage[w][...] = ins[w][...].astype(out_dtypes[w])
                copies.append(pltpu.make_async_copy(stage[w], outs[w].at[me], sems.at[w]))
                copies[-1].start()
            for cp in copies:
                cp.wait()

        cast(0, n_first)
        gather.start(None, outs[:n_first], sems, n)
        cast(n_first, n)
        gather.finish(None, outs[:n_first], sems, n)

    vm = pl.BlockSpec(memory_space=pltpu.VMEM)
    return pl.pallas_call(
        body, name="cast_shards", in_specs=[vm] * n, out_specs=[ANY] * n, out_shape=out_shape,
        scratch_shapes=[pltpu.VMEM(s.shape, dt) for s, dt in zip(shards, out_dtypes)] + [pltpu.SemaphoreType.DMA((n + gather.n_sems,))],
        compiler_params=_params())(*shards)


BF16_ROWS = 16


def _split_rows(rows):
    return (rows // 2) % BF16_ROWS == 0


def _half_shape(rows, cols):
    return (rows // 2, cols) if _split_rows(rows) else (rows, cols // 2)


def _half(rows, cols, which):
    if _split_rows(rows):
        return (pl.ds(pl.multiple_of(which * (rows // 2), BF16_ROWS), rows // 2), slice(None))
    return (slice(None), pl.ds(pl.multiple_of(which * (cols // 2), 128), cols // 2))


def _gather_carried(bufs):
    n = len(bufs)

    def half(w, slot, which):
        _, rows, cols = bufs[w].shape
        return (slot,) + _half(rows, cols, which)

    def start(ins, outs, sems, base):
        x, y, c, other_chips = _place()
        me = 2 * x + y
        for w in range(n):
            mine = outs[w].at[half(w, me, c)]
            for p, (px, py) in enumerate(other_chips):
                _remote(mine, mine, sems, base + 12 * w + p, base + 12 * w + 3 + p, (px, py, c)).start()

    def finish(ins, outs, sems, base):
        x, y, c, other_chips = _place()
        me = 2 * x + y
        for w in range(n):
            for p, (px, py) in enumerate(other_chips):
                got = outs[w].at[half(w, 2 * px + py, c)]
                _remote(got, got, sems, base + 12 * w + p, base + 12 * w + 3 + p, (px, py, c)).wait_recv()
                _remote(got, got, sems, base + 12 * w + 6 + p, base + 12 * w + 9 + p, (x, y, 1 - c)).start()
        for w in range(n):
            mine = outs[w].at[half(w, me, c)]
            for p, (px, py) in enumerate(other_chips):
                got = outs[w].at[half(w, 2 * px + py, c)]
                theirs = outs[w].at[half(w, 2 * px + py, 1 - c)]
                _remote(got, theirs, sems, base + 12 * w + 6 + p, base + 12 * w + 9 + p, (x, y, 1 - c)).wait()
                _remote(mine, mine, sems, base + 12 * w + p, base + 12 * w + 3 + p, (px, py, c)).wait_send()

    shapes = [jax.ShapeDtypeStruct(b.shape, b.dtype) for b in bufs]
    return _Carried(bufs, shapes, {w: w for w in range(n)}, 12 * n, start, finish)


def _swap_carried(grads):
    n = len(grads)

    def copy(w, ins, outs, sems, base):
        x, y, c, _ = _place()
        _, rows, cols = grads[w].shape
        theirs = ins[w].at[(slice(None),) + _half(rows, cols, 1 - c)]
        return _remote(theirs, outs[w], sems, base + 2 * w, base + 2 * w + 1, (x, y, 1 - c))

    def start(ins, outs, sems, base):
        for w in range(n):
            copy(w, ins, outs, sems, base).start()

    def finish(ins, outs, sems, base):
        for w in range(n):
            copy(w, ins, outs, sems, base).wait()

    shapes = [jax.ShapeDtypeStruct((g.shape[0],) + _half_shape(*g.shape[1:]), F32) for g in grads]
    return _Carried(grads, shapes, {}, 2 * n, start, finish)


def _row_tile(rows):
    for cand in (512, 352, 256, 192, 128, 96, 64, 32, 16):
        if rows % cand == 0:
            return cand
    return rows


def _half_block_index(split_rows, tiles, i, core):
    return (core * tiles + i, 0) if split_rows else (i, core)


def _chip_partial(grad, other, place, name):
    nblk, hr, hc = other.shape
    by_rows = _split_rows(grad.shape[1])
    tr = _row_tile(hr)
    tiles = hr // tr

    def body(place_ref, g_ref, o_ref, own_ref, sum_bf_ref):
        s = g_ref[...] + o_ref[...]
        sum_bf_ref[...] = _bf(s)

        @pl.when(pl.program_id(1) == place_ref[0])
        def _():
            own_ref[...] = s

    grid_spec = pltpu.PrefetchScalarGridSpec(
        num_scalar_prefetch=1, grid=(tiles, nblk),
        in_specs=[pl.BlockSpec((None, tr, hc), lambda i, b, place_ref: (b,) + _half_block_index(by_rows, tiles, i, place_ref[1])),
                  pl.BlockSpec((None, tr, hc), lambda i, b, place_ref: (b, i, 0))],
        out_specs=[pl.BlockSpec((tr, hc), lambda i, b, place_ref: (i, 0)),
                   pl.BlockSpec((None, tr, hc), lambda i, b, place_ref: (b, i, 0))])
    return pl.pallas_call(body, name=name, grid_spec=grid_spec,
                          out_shape=[jax.ShapeDtypeStruct((hr, hc), F32), jax.ShapeDtypeStruct(other.shape, BF16)],
                          compiler_params=_params(("arbitrary", "arbitrary")))(place, grad, other)


def _chip_partial_small(grads, others, place):
    n = len(grads)

    def body(*refs):
        place_ref, g_refs, o_refs = refs[0], refs[1:1 + n], refs[1 + n:1 + 2 * n]
        own_refs, bf_refs = refs[1 + 2 * n:1 + 3 * n], refs[1 + 3 * n:]
        chip, core = place_ref[0], place_ref[1]
        for w in range(n):
            _, rows, cols = grads[w].shape
            half = _half(rows, cols, core)
            bf_refs[w][...] = _bf(g_refs[w][(slice(None),) + half] + o_refs[w][...])
            own_refs[w][...] = g_refs[w][(chip,) + half] + o_refs[w][chip]

    vm = pl.BlockSpec(memory_space=pltpu.VMEM)
    outs = pl.pallas_call(
        body, name="chip_partial_small", in_specs=[pl.BlockSpec(memory_space=pltpu.SMEM)] + [vm] * (2 * n), out_specs=[vm] * (2 * n),
        out_shape=[jax.ShapeDtypeStruct(o.shape[1:], F32) for o in others] + [jax.ShapeDtypeStruct(o.shape, BF16) for o in others],
        compiler_params=_params())(place, *grads, *others)
    return list(zip(outs[:n], outs[n:]))


def _chip_total_small(owns, receiveds, place, shapes):
    n = len(owns)

    def body(*refs):
        place_ref, own_refs, r_refs, out_refs = refs[0], refs[1:1 + n], refs[1 + n:1 + 2 * n], refs[1 + 2 * n:]
        chip, core = place_ref[0], place_ref[1]
        for w in range(n):
            r = [r_refs[w][(chip + k) % N_CHIPS].astype(F32) for k in (1, 2, 3)]
            out_refs[w][_half(*shapes[w], core)] = own_refs[w][...] + ((r[0] + r[1]) + r[2])

    vm = pl.BlockSpec(memory_space=pltpu.VMEM)
    return list(pl.pallas_call(
        body, name="chip_total_small", in_specs=[pl.BlockSpec(memory_space=pltpu.SMEM)] + [vm] * (2 * n), out_specs=[vm] * n,
        out_shape=[jax.ShapeDtypeStruct(tuple(s), F32) for s in shapes], compiler_params=_params())(place, *owns, *receiveds))


def _send_carried(partials):
    n = len(partials)

    def start(ins, outs, sems, base):
        x, y, c, other_chips = _place()
        me = 2 * x + y
        for w in range(n):
            for p, (px, py) in enumerate(other_chips):
                _remote(ins[w].at[2 * px + py], outs[w].at[me], sems, base + 6 * w + p, base + 6 * w + 3 + p, (px, py, c)).start()

    def finish(ins, outs, sems, base):
        x, y, c, other_chips = _place()
        for w in range(n):
            for p, (px, py) in enumerate(other_chips):
                _remote(ins[w].at[2 * px + py], outs[w].at[2 * px + py], sems, base + 6 * w + p, base + 6 * w + 3 + p,
                        (px, py, c)).wait()

    return _Carried(partials, [jax.ShapeDtypeStruct(p.shape, BF16) for p in partials], {}, 6 * n, start, finish)


def _chip_total(own, received, place, shape, name):
    hr, hc = own.shape
    by_rows = _split_rows(shape[0])
    tr = _row_tile(hr)
    tiles = hr // tr

    def body(place_ref, own_ref, r1_ref, r2_ref, r3_ref, out_ref):
        out_ref[...] = own_ref[...] + ((r1_ref[...].astype(F32) + r2_ref[...].astype(F32)) + r3_ref[...].astype(F32))

    def slot(k):
        return pl.BlockSpec((None, tr, hc), lambda i, place_ref: ((place_ref[0] + k) % N_CHIPS, i, 0))

    grid_spec = pltpu.PrefetchScalarGridSpec(
        num_scalar_prefetch=1, grid=(tiles,), in_specs=[pl.BlockSpec((tr, hc), lambda i, place_ref: (i, 0)), slot(1), slot(2), slot(3)],
        out_specs=pl.BlockSpec((tr, hc), lambda i, place_ref: _half_block_index(by_rows, tiles, i, place_ref[1])))
    return pl.pallas_call(body, name=name, grid_spec=grid_spec, out_shape=jax.ShapeDtypeStruct(tuple(shape), F32),
                          compiler_params=_params(("arbitrary",)))(place, own, received, received, received)


def _join_carried(totals):
    n = len(totals)

    def copy(w, outs, sems, base):
        x, y, c, _ = _place()
        mine = outs[w].at[_half(*totals[w].shape, c)]
        return _remote(mine, mine, sems, base + 2 * w, base + 2 * w + 1, (x, y, 1 - c))

    def start(ins, outs, sems, base):
        for w in range(n):
            copy(w, outs, sems, base).start()

    def finish(ins, outs, sems, base):
        for w in range(n):
            copy(w, outs, sems, base).wait()

    shapes = [jax.ShapeDtypeStruct(a.shape, F32) for a in totals]
    return _Carried(totals, shapes, {w: w for w in range(n)}, 2 * n, start, finish)


def _sum_devices(vec):
    rows, n = vec.shape

    def body(v_ref, out_ref, buf, send_sems, recv_sems):
        x, y, c, _ = _place()
        me = 4 * x + 2 * y + c
        buf[me] = v_ref[...]
        sends = []
        for k in range(1, N_DEV):
            peer = (1 - x if k & 4 else x, 1 - y if k & 2 else y, 1 - c if k & 1 else c)
            cp = pltpu.make_async_remote_copy(src_ref=v_ref, dst_ref=buf.at[me], send_sem=send_sems.at[k], recv_sem=recv_sems.at[k],
                                              device_id=peer, device_id_type=MESH_ID)
            cp.start()
            sends.append(cp)
        for cp in sends:
            cp.wait()
        total = buf[0]
        for dev in range(1, N_DEV):
            total = total + buf[dev]
        out_ref[...] = total

    vm = pl.BlockSpec(memory_space=pltpu.VMEM)
    return pl.pallas_call(
        body, name="sum_devices", in_specs=[vm], out_specs=vm, out_shape=jax.ShapeDtypeStruct((rows, n), F32),
        scratch_shapes=[pltpu.VMEM((N_DEV, rows, n), F32), pltpu.SemaphoreType.DMA((N_DEV,)), pltpu.SemaphoreType.DMA((N_DEV,))],
    )(vec)


def _rope_tables(positions):
    half = ROPE_HALF
    inv_freq = 1.0 / (ROPE_THETA ** (jnp.arange(half, dtype=F32) / half))
    ang = positions.astype(F32).reshape(-1, 1) * inv_freq
    cos, sin = jnp.cos(ang), jnp.sin(ang)
    t = ang.shape[0]
    ones, zeros = jnp.ones((t, QK_NOPE), F32), jnp.zeros((t, QK_NOPE), F32)
    pad, none = HEAD_PAD - QK_DIM, zeros[:, :half]
    cos_full = jnp.concatenate([ones, cos, cos, ones[:, :pad]], axis=1)
    s_lo = jnp.concatenate([zeros, -sin, none, zeros[:, :pad]], axis=1)
    s_hi = jnp.concatenate([zeros, none, sin, zeros[:, :pad]], axis=1)
    return cos_full, s_lo, s_hi


def _partials(names, grads, from_sibling, place):
    return [_chip_partial(g, o, place, "chip_partial_" + n) for n, g, o in zip(names, grads, from_sibling)]


def _totals(names, grads, partials, received, place):
    return [_chip_total(pf, r, place, g.shape[1:], "chip_total_" + n) for n, g, (pf, _), r in zip(names, grads, partials, received)]


def _kernel_layouts(full):
    w_in = full["w_in"]
    w_kr = jnp.pad(w_in[ROW_KR:ROW_XC], ((QK_NOPE, HEAD_PAD - QK_DIM), (0, 0)))
    w_uq = jnp.pad(full["w_uq"].reshape(Q_LORA, N_HEADS, QK_DIM), ((0, 0), (0, 0), (0, HEAD_PAD - QK_DIM)))
    w_uk = jnp.pad(full["w_uk"].reshape(KV_LORA, N_HEADS, QK_NOPE), ((0, 0), (0, 0), (0, HEAD_PAD - QK_NOPE)))
    return {"w_in": w_in, "w_kr": w_kr, "w_uq": w_uq.reshape(Q_LORA, N_HEADS * HEAD_PAD),
            "w_uk": w_uk.reshape(KV_LORA, N_HEADS * HEAD_PAD), "w_uv": full["w_uv"], "w_uvt": full["w_uv"].T}


def _global_layouts(g):
    w_uq = g["w_uq"].reshape(Q_LORA, N_HEADS, HEAD_PAD)[:, :, :QK_DIM].reshape(Q_LORA, N_HEADS * QK_DIM)
    w_uk = g["w_uk"].reshape(KV_LORA, N_HEADS, HEAD_PAD)[:, :, :QK_NOPE].reshape(KV_LORA, N_HEADS * QK_NOPE)
    return {"w_in": g["w_in"], "w_uq": w_uq, "w_uk": w_uk, "w_uv": g["w_uv"], "w_proj_attn": g["w_pa"], "w_proj_conv": g["w_pc"],
            "w_out": g["w_out"]}


def _dw_in(dm, dp, h2b):
    d, rows = D_MODEL, ROW_GL + 2 * D_MODEL
    wm, wp = M_COLS // 2, P_COLS // 2
    from_dm = [(0, [(ROW_GB, 0, d), (ROW_GL, d, wm - d)]), (1, [(ROW_GL + wm - d, 0, wm)])]
    from_dp = [(0, [(ROW_XC, 0, d), (ROW_GC, d, wp - d)]),
               (1, [(ROW_GC + wp - d, 0, 2 * d - wp), (ROW_QKV, 2 * d - wp, ROW_KR), (ROW_KR, 2 * d - wp + ROW_KR + QK_NOPE, QK_ROPE)])]
    out = _tn_rows(dm, h2b, None, from_dm, wm, rows, "dw_in_m")
    return _tn_rows(dp, h2b, out, from_dp, wp, rows, "dw_in_p")


def _col_blocks(a):
    r, c = a.shape
    return a.reshape(r, N_CHIPS, c // N_CHIPS).transpose(1, 0, 2)


def _from_col_blocks(a):
    n, r, c = a.shape
    return a.transpose(1, 0, 2).reshape(r, n * c)


COL_SHARDED = ("w_uq", "w_uk", "w_uv", "w_proj_attn")
TRANSPOSED = ("ffn1_w_gate", "ffn1_w_up", "ffn2_w_gate", "ffn2_w_up", "w_in")
SMALL = (("ffn1_norm", 1024), ("mix_norm", 1024), ("gate_bias", 2048), ("q_a_norm", 384), ("kv_a_norm", 256),
         ("q_head_norm", 128), ("k_head_norm", 128), ("ffn2_norm", 1024))
WEIGHT_ORDER = ("ffn1_norm", "ffn1_w_gate", "ffn1_w_up", "ffn1_w_down", "mix_norm", "w_in", "gate_bias", "q_a_norm", "w_uq",
                "kv_a_norm", "w_uk", "w_uv", "q_head_norm", "k_head_norm", "w_proj_attn", "conv_w", "w_proj_conv", "w_out",
                "ffn2_norm", "ffn2_w_gate", "ffn2_w_up", "ffn2_w_down")
MATRICES = ("ffn1_w_gate", "ffn1_w_up", "ffn1_w_down", "w_in", "w_uq", "w_uk", "w_uv", "w_proj_attn", "w_proj_conv", "w_out",
            "ffn2_w_gate", "ffn2_w_up", "ffn2_w_down")
GROUP_FFN1 = ("ffn1_w_gate", "ffn1_w_up", "ffn1_w_down")
GROUP_IN = ("w_in", "w_uq", "w_uk", "w_uv", "conv_w")
GROUP_MIX = ("w_proj_attn", "w_proj_conv", "w_out")
GROUP_FFN2 = ("ffn2_w_gate", "ffn2_w_up", "ffn2_w_down")
GROUP_MID = ("w_in", "w_uq", "w_uk", "w_uv", "w_proj_attn", "w_proj_conv", "w_out")


def _pad_lanes(a, n):
    return jnp.pad(a.reshape(1, -1), ((0, 0), (0, n - a.size)))


def kernel(x, positions, ffn1_norm, ffn1_w_gate, ffn1_w_up, ffn1_w_down, mix_norm, w_in, gate_bias, q_a_norm, w_uq, kv_a_norm, w_uk, w_uv, q_head_norm, k_head_norm, w_proj_attn, conv_w, w_proj_conv, w_out, ffn2_norm, ffn2_w_gate, ffn2_w_up, ffn2_w_down, loss_target, m_ffn1_norm, m_ffn1_w_gate, m_ffn1_w_up, m_ffn1_w_down, m_mix_norm, m_w_in, m_gate_bias, m_q_a_norm, m_w_uq, m_kv_a_norm, m_w_uk, m_w_uv, m_q_head_norm, m_k_head_norm, m_w_proj_attn, m_conv_w, m_w_proj_conv, m_w_out, m_ffn2_norm, m_ffn2_w_gate, m_ffn2_w_up, m_ffn2_w_down, v_ffn1_norm, v_ffn1_w_gate, v_ffn1_w_up, v_ffn1_w_down, v_mix_norm, v_w_in, v_gate_bias, v_q_a_norm, v_w_uq, v_kv_a_norm, v_w_uk, v_w_uv, v_q_head_norm, v_k_head_norm, v_w_proj_attn, v_conv_w, v_w_proj_conv, v_w_out, v_ffn2_norm, v_ffn2_w_gate, v_ffn2_w_up, v_ffn2_w_down):
    args = dict(locals())
    view = lambda n, a: a.T if n in TRANSPOSED else a
    weights = {n: view(n, args[n]) for n in WEIGHT_ORDER}
    moments_m = {n: view(n, args["m_" + n]) for n in WEIGHT_ORDER}
    moments_v = {n: view(n, args["v_" + n]) for n in WEIGHT_ORDER}
    nb, seq, d = x.shape
    t = nb * seq
    chip = (2 * lax.axis_index("x") + lax.axis_index("y")).astype(jnp.int32)
    place = jnp.stack([chip, lax.axis_index("c").astype(jnp.int32)])
    grads, delta, new_m, new_v = {}, {}, {}, {}

    def adamw(names, carried=None):
        results = _adamw([(weights[n], grads[n], moments_m[n], moments_v[n]) for n in names], "adamw_" + names[0], carried)
        for n, (dn, mn, vn) in zip(names, results):
            delta[n], new_m[n], new_v[n] = dn, mn, vn

    conv_rows = conv_w.shape[0]
    conv_shard = jnp.pad(conv_w, ((0, 16 - conv_rows), (0, 0)))
    assert MATRICES[:len(GROUP_FFN1)] == GROUP_FFN1
    bufs = dict(zip(MATRICES + ("conv_w",), _cast_shards([weights[n] for n in MATRICES] + [conv_shard],
                                                         [BF16] * len(MATRICES) + [F32], len(GROUP_FFN1))))
    blocks = {n: bufs[n] for n in GROUP_FFN1}
    p = {n: _pad_lanes(weights[n], size) for n, size in SMALL}
    rope = _rope_tables(positions)
    x_tok = x.reshape(t, d)

    gather_in = _gather_carried([bufs[n] for n in GROUP_IN])
    x1, gate1, up1, act1 = _ffn_fwd(x_tok, p["ffn1_norm"], blocks["ffn1_w_gate"], blocks["ffn1_w_up"], blocks["ffn1_w_down"], None,
                                    "ffn1_fwd", gather_in)
    blocks.update(zip(GROUP_IN, gather_in.results))
    w = _kernel_layouts({"w_in": blocks["w_in"].reshape(-1, d), **{n: _from_col_blocks(blocks[n]) for n in ("w_uq", "w_uk", "w_uv")}})
    p["conv_w"] = _from_col_blocks(blocks["conv_w"])[:conv_rows]

    gather_mix = _gather_carried([bufs[n] for n in GROUP_MIX + GROUP_FFN2[2:]])
    h2b, big, lat, q, k, v, vt = _inproj_fwd(x1, p["mix_norm"], w["w_in"], w["w_kr"], p["q_a_norm"], p["kv_a_norm"], p["q_head_norm"],
                                             p["k_head_norm"], w["w_uq"], w["w_uk"], w["w_uv"], w["w_uvt"], rope, gather_mix)
    blocks.update(zip(GROUP_MIX + GROUP_FFN2[2:], gather_mix.results))
    w_pa = _from_col_blocks(blocks["w_proj_attn"])
    w_pc, w_out_full = blocks["w_proj_conv"].reshape(-1, d), blocks["w_out"].reshape(-1, d)

    gather_ffn2 = _gather_carried([bufs[n] for n in GROUP_FFN2[:2]])
    o, lse = _attn_fwd(q, k, vt, seq, gather_ffn2)
    x2 = _mix_fwd(x1, o, big, p["gate_bias"], p["conv_w"], w_pa, w_pc, w_out_full, seq)
    wg2, wu2 = gather_ffn2.results
    wd2 = blocks["ffn2_w_down"]
    dx3, gate2, up2, act2, loss = _ffn_fwd(x2, p["ffn2_norm"], wg2, wu2, wd2, loss_target.reshape(t, d), "ffn2_fwd")

    dx2, dg_ffn2, hb2, dgate2, dup2, dyb2 = _ffn_bwd_x(x2, p["ffn2_norm"], dx3, gate2, up2, wg2, wu2, wd2, "ffn2_bwd")
    g_ffn2 = [_tn_matmul(dgate2, hb2, "ffn2_dw_gate"), _tn_matmul(dup2, hb2, "ffn2_dw_up"), _tn_matmul(act2, dyb2, "ffn2_dw_down")]
    swap = _swap_carried(g_ffn2)
    do, delta_o, dz, dm, dbias, dw_pa, dw_pc, dw_out = _mix_bwd(dx2, o, big, p["gate_bias"], p["conv_w"], w_pa, w_pc, w_out_full, seq,
                                                                swap)
    part = _partials(GROUP_FFN2, g_ffn2, swap.results, place)
    send = _send_carried([pb for _, pb in part])
    dq, dk, dv = _attn_bwd(q, k, v, do, lse, delta_o.reshape(N_HEADS // ATTN_BWD_HEADS, ATTN_BWD_HEADS, -1), seq, send)
    join = _join_carried(_totals(GROUP_FFN2, g_ffn2, part, send.results, place))
    dp, dw_uq, dw_uk, dw_uv, dqa, dkva, dqh, dkh, dcw = _prep_bwd(
        lat, big, dz, dq, dk, dv, p["q_a_norm"], p["kv_a_norm"], p["q_head_norm"], p["k_head_norm"], w["w_uq"], w["w_uk"],
        w["w_uv"], rope, p["conv_w"], seq, join)
    grads.update(zip(GROUP_FFN2, join.results))

    gg = _global_layouts({"w_in": _dw_in(dm, dp, h2b), "w_uq": dw_uq, "w_uk": dw_uk, "w_uv": dw_uv, "w_pa": dw_pa, "w_pc": dw_pc,
                          "w_out": dw_out})
    g_mid = [_col_blocks(gg[n]) if n in COL_SHARDED else gg[n].reshape(N_CHIPS, -1, gg[n].shape[-1]) for n in GROUP_MID]
    swap = _swap_carried(g_mid)
    dx1, dg_mix = _inproj_bwd(x1, p["mix_norm"], dx2, dm, dp, w["w_in"], w["w_kr"], swap)
    part = (_partials(GROUP_MID[:1], g_mid[:1], swap.results[:1], place)
            + _chip_partial_small(g_mid[1:], swap.results[1:], place))
    send = _send_carried([pb for _, pb in part])
    grad_x, dg_ffn1, hb1, dgate1, dup1, dyb1 = _ffn_bwd_x(x_tok, p["ffn1_norm"], dx1, gate1, up1, blocks["ffn1_w_gate"],
                                                         blocks["ffn1_w_up"], blocks["ffn1_w_down"], "ffn1_bwd", send)

    small_grads = {"ffn1_norm": dg_ffn1, "mix_norm": dg_mix, "gate_bias": dbias, "q_a_norm": dqa, "kv_a_norm": dkva,
                   "q_head_norm": dqh, "k_head_norm": dkh, "ffn2_norm": dg_ffn2}
    packed = jnp.concatenate([small_grads[n] for n, _ in SMALL] + [dcw.reshape(1, -1), loss], axis=1)
    total = _sum_devices(packed.reshape(8, -1)).reshape(1, -1)
    n_small = sum(size for _, size in SMALL)
    conv_cols = conv_w.shape[1]
    conv_total = total[:, n_small:n_small + conv_rows * d].reshape(conv_rows, d)
    grads["conv_w"] = lax.dynamic_slice_in_dim(conv_total, chip * conv_cols, conv_cols, axis=1)
    loss_total = total[0, n_small + conv_rows * d]

    join = _join_carried(_totals(GROUP_MID[:1], g_mid[:1], part[:1], send.results[:1], place)
                         + _chip_total_small([pf for pf, _ in part[1:]], send.results[1:], place, [g.shape[1:] for g in g_mid[1:]]))
    g_gate = _tn_matmul(dgate1, hb1, "ffn1_dw_gate", carried=join)
    grads.update(zip(GROUP_MID, join.results))
    swap_gate = _swap_carried([g_gate])
    g_up = _tn_matmul(dup1, hb1, "ffn1_dw_up", carried=swap_gate)
    part_gate = _partials(GROUP_FFN1[:1], [g_gate], swap_gate.results, place)
    send_gate, swap_up = _send_carried([part_gate[0][1]]), _swap_carried([g_up])
    g_down = _tn_matmul(act1, dyb1, "ffn1_dw_down", carried=_both(send_gate, swap_up))
    join_gate = _join_carried(_totals(GROUP_FFN1[:1], [g_gate], part_gate, send_gate.results, place))
    part_up = _partials(GROUP_FFN1[1:2], [g_up], swap_up.results, place)
    send_up, swap_down = _send_carried([part_up[0][1]]), _swap_carried([g_down])
    adamw(GROUP_FFN2, _both(_both(send_up, swap_down), join_gate))
    grads["ffn1_w_gate"] = join_gate.results[0]
    join_up = _join_carried(_totals(GROUP_FFN1[1:2], [g_up], part_up, send_up.results, place))
    part_down = _partials(GROUP_FFN1[2:], [g_down], swap_down.results, place)
    send_down = _send_carried([part_down[0][1]])
    adamw(("w_in",), _both(send_down, join_up))
    grads["ffn1_w_up"] = join_up.results[0]
    join_down = _join_carried(_totals(GROUP_FFN1[2:], [g_down], part_down, send_down.results, place))
    adamw(GROUP_FFN1[:2], join_down)
    grads["ffn1_w_down"] = join_down.results[0]
    adamw(GROUP_FFN1[2:])
    others = GROUP_MID[1:] + ("conv_w",)
    for n, (gn, dn, mn, vn) in zip(others, _adamw_whole([(weights[n], grads[n], moments_m[n], moments_v[n]) for n in others],
                                                        "adamw_others")):
        grads[n], delta[n], new_m[n], new_v[n] = gn, dn, mn, vn

    row = lambda a: a.reshape(1, -1)
    small = _adamw_small(total, [(row(weights[n]), row(moments_m[n]), row(moments_v[n])) for n, _ in SMALL], [size for _, size in SMALL])
    for (n, _), (gn, dn, mn, vn) in zip(SMALL, small):
        grads[n], delta[n], new_m[n], new_v[n] = gn.reshape(-1), dn.reshape(-1), mn.reshape(-1), vn.reshape(-1)

    return (loss_total, grad_x.reshape(nb, seq, d), *[view(n, src[n]) for src in (grads, delta, new_m, new_v) for n in WEIGHT_ORDER])
```

```python
import functools

import jax
import jax.numpy as jnp
from jax import lax
from jax.experimental import pallas as pl
from jax.experimental.pallas import tpu as pltpu

F32 = jnp.float32
BF16 = jnp.bfloat16

D_MODEL = 1024
N_HEADS = 8
QK_NOPE = 64
QK_ROPE = 32
QK_DIM = QK_NOPE + QK_ROPE
V_DIM = 64
HEAD_PAD = 128
Q_LORA = 384
KV_LORA = 256
ROPE_THETA = 10000.0
NORM_EPS = 1e-6
ATTN_SCALE = QK_DIM ** -0.5
MASK_VALUE = -1e30
N_CHIPS = 4
N_DEV = 8

ADAM_LR = 0.001
ADAM_B1 = 0.9
ADAM_B2 = 0.999
ADAM_EPS = 1e-08
ADAM_WD = 0.01
ADAM_STEP = 10

TOKEN_TILE = 256
PREP_TILE = 256
ATTN_TILE = 512
TN_TILE = 2048
VMEM_LIMIT = 56 * 1024 * 1024

M_COLS = 3 * D_MODEL
P_COLS = 2 * D_MODEL + Q_LORA + KV_LORA + HEAD_PAD
BIG_COLS = 5 * D_MODEL
LAT_COLS = Q_LORA + KV_LORA + HEAD_PAD

MESH_ID = pl.DeviceIdType.MESH
ANY = pl.BlockSpec(memory_space=pl.ANY)


def _params(semantics=None):
    return pltpu.CompilerParams(dimension_semantics=semantics, vmem_limit_bytes=VMEM_LIMIT)


class _Carried:
    def __init__(self, operands, out_shapes, aliases, n_sems, start, finish):
        self.operands, self.out_shapes, self.aliases, self.n_sems = list(operands), list(out_shapes), dict(aliases), n_sems
        self.start, self.finish = start, finish
        self.results = None


def _both(a, b):
    na, nao = len(a.operands), len(a.out_shapes)

    def start(ins, outs, sems, base):
        a.start(ins[:na], outs[:nao], sems, base)
        b.start(ins[na:], outs[nao:], sems, base + a.n_sems)

    def finish(ins, outs, sems, base):
        a.finish(ins[:na], outs[:nao], sems, base)
        b.finish(ins[na:], outs[nao:], sems, base + a.n_sems)

    aliases = dict(a.aliases)
    aliases.update({na + i: nao + o for i, o in b.aliases.items()})
    both = _Carried(a.operands + b.operands, a.out_shapes + b.out_shapes, aliases, a.n_sems + b.n_sems, start, finish)
    both.parts = (a, b)
    return both


def _set_results(carried, results):
    carried.results = list(results)
    if hasattr(carried, "parts"):
        a, b = carried.parts
        _set_results(a, results[:len(a.out_shapes)])
        _set_results(b, results[len(a.out_shapes):])


def _pallas(body, name, grid, in_specs, out_specs, out_shape, args, semantics, carried=None):
    if carried is None:
        return pl.pallas_call(body, name=name, grid=grid, in_specs=in_specs, out_specs=out_specs, out_shape=out_shape,
                              compiler_params=_params(semantics))(*args)
    n_in, n_out, n_ci, n_co = len(in_specs), len(out_specs), len(carried.operands), len(carried.out_shapes)

    def wrapped(*refs):
        ins, c_ins = refs[:n_in], refs[n_in:n_in + n_ci]
        outs, c_outs = refs[n_in + n_ci:n_in + n_ci + n_out], refs[n_in + n_ci + n_out:n_in + n_ci + n_out + n_co]
        sems = refs[-1]
        first = pl.program_id(0) == 0
        last = pl.program_id(0) == grid[0] - 1
        for axis in range(1, len(grid)):
            first = jnp.logical_and(first, pl.program_id(axis) == 0)
            last = jnp.logical_and(last, pl.program_id(axis) == grid[axis] - 1)

        @pl.when(first)
        def _():
            carried.start(c_ins, c_outs, sems, 0)

        body(*ins, *outs)

        @pl.when(last)
        def _():
            carried.finish(c_ins, c_outs, sems, 0)

    results = pl.pallas_call(
        wrapped, name=name, grid=grid, in_specs=list(in_specs) + [ANY] * n_ci, out_specs=list(out_specs) + [ANY] * n_co,
        out_shape=list(out_shape) + carried.out_shapes,
        input_output_aliases={n_in + i: n_out + o for i, o in carried.aliases.items()},
        scratch_shapes=[pltpu.SemaphoreType.DMA((carried.n_sems,))], compiler_params=_params(semantics))(*args, *carried.operands)
    _set_results(carried, results[n_out:])
    return results[:n_out]


def _resident(shape):
    nd = len(shape)
    return pl.BlockSpec(shape, lambda *_: (0,) * nd, pipeline_mode=pl.Buffered(1))


def _const(shape):
    nd = len(shape)
    return pl.BlockSpec(shape, lambda *_: (0,) * nd)


def _mm(a, b):
    return jnp.dot(a, b, preferred_element_type=F32)


def _mm_nt(a, b):
    return lax.dot_general(a, b, (((1,), (1,)), ((), ())), preferred_element_type=F32)


def _mm_tn(a, b):
    return lax.dot_general(a, b, (((0,), (0,)), ((), ())), preferred_element_type=F32)


def _bf(a):
    return a.astype(BF16)


def _sigmoid(a):
    return 1.0 / (1.0 + jnp.exp(-a))


def _rms(x, gain, n=None):
    n = x.shape[-1] if n is None else n
    r = lax.rsqrt(jnp.sum(x * x, axis=-1, keepdims=True) * (1.0 / n) + NORM_EPS)
    return (x * r) * gain, r


def _rms_bwd(x, r, gain, dh, n=None):
    n = x.shape[-1] if n is None else n
    u = dh * gain
    dx = r * u - x * ((r * r * r) * (jnp.sum(u * x, axis=-1, keepdims=True) * (1.0 / n)))
    dgain = jnp.sum(dh * (x * r), axis=0, keepdims=True)
    return dx, dgain


ROPE_HALF = QK_ROPE // 2


def _rope(t, rope):
    cos, s_lo, s_hi = rope
    return t * cos + pltpu.roll(t, HEAD_PAD - ROPE_HALF, 1) * s_lo + pltpu.roll(t, ROPE_HALF, 1) * s_hi


def _rope_bwd(dt, rope):
    cos, s_lo, s_hi = rope
    return dt * cos + pltpu.roll(dt * s_lo, ROPE_HALF, 1) + pltpu.roll(dt * s_hi, HEAD_PAD - ROPE_HALF, 1)


def _shift_down(u, prev8, k):
    s = pltpu.roll(u, k, 0)
    p = pltpu.roll(prev8, k, 0)
    row = lax.broadcasted_iota(jnp.int32, prev8.shape, 0)
    top = jnp.where(row < k, p, s[:8])
    return jnp.concatenate([top, s[8:]], axis=0)


def _shift_up(d, next8, k):
    tm = d.shape[0]
    s = pltpu.roll(d, tm - k, 0)
    n = pltpu.roll(next8, 8 - k, 0)
    row = lax.broadcasted_iota(jnp.int32, next8.shape, 0)
    bot = jnp.where(row >= 8 - k, n, s[tm - 8:])
    return jnp.concatenate([s[:tm - 8], bot], axis=0)


def _ffn_fwd(x, gain, wg, wu, wd, target, name, carried=None):
    t, d = x.shape
    nb, f, _ = wg.shape
    tm = TOKEN_TILE
    with_loss = target is not None

    def body(*refs):
        if with_loss:
            x_ref, g_ref, wg_ref, wu_ref, wd_ref, t_ref, out_ref, gate_ref, up_ref, act_ref, loss_ref = refs
        else:
            x_ref, g_ref, wg_ref, wu_ref, wd_ref, out_ref, gate_ref, up_ref, act_ref = refs
        xv = x_ref[...]
        h, _ = _rms(xv, g_ref[...])
        hb = _bf(h)
        y = jnp.zeros((tm, d), F32)
        nxt = (_mm_nt(hb, wg_ref[0]), _mm_nt(hb, wu_ref[0]))
        for j in range(nb):
            gate, up = nxt
            if j + 1 < nb:
                nxt = (_mm_nt(hb, wg_ref[j + 1]), _mm_nt(hb, wu_ref[j + 1]))
            act = _bf((gate * _sigmoid(gate)) * up)
            y = y + _mm(act, wd_ref[j])
            gate_ref[j] = _bf(gate)
            up_ref[j] = _bf(up)
            act_ref[j] = act
        out = xv + 0.5 * y
        if with_loss:
            err = out - t_ref[...]
            out_ref[...] = err * (1.0 / d)

            @pl.when(pl.program_id(0) == 0)
            def _():
                loss_ref[...] = jnp.zeros_like(loss_ref)

            part = jnp.sum(jnp.sum(err * err, axis=1, keepdims=True), axis=0, keepdims=True)
            loss_ref[...] += jnp.broadcast_to(part * (0.5 / d), loss_ref.shape)
        else:
            out_ref[...] = out

    tok = pl.BlockSpec((tm, d), lambda i: (i, 0))
    blk = pl.BlockSpec((nb, tm, f), lambda i: (0, i, 0))
    in_specs = [tok, _const((1, d)), _resident(wg.shape), _resident(wu.shape), _resident(wd.shape)]
    args = [x, gain, wg, wu, wd]
    out_shape = [jax.ShapeDtypeStruct((t, d), F32)] + [jax.ShapeDtypeStruct((nb, t, f), BF16)] * 3
    out_specs = [tok, blk, blk, blk]
    if with_loss:
        in_specs.append(tok)
        args.append(target)
        out_shape.append(jax.ShapeDtypeStruct((1, 128), F32))
        out_specs.append(_const((1, 128)))
    return _pallas(body, name, (t // tm,), in_specs, out_specs, out_shape, args, ("arbitrary",), carried)


def _ffn_bwd_x(x, gain, dout, gate, up, wg, wu, wd, name, carried=None):
    t, d = x.shape
    nb, f, _ = wg.shape
    tm = TOKEN_TILE

    def body(x_ref, g_ref, dout_ref, gate_ref, up_ref, wg_ref, wu_ref, wd_ref,
             dx_ref, dgain_ref, hb_ref, dgate_ref, dup_ref, dyb_ref):
        xv = x_ref[...]
        gain_v = g_ref[...]
        h, r = _rms(xv, gain_v)
        hb_ref[...] = _bf(h)
        dout_v = dout_ref[...]
        dyb = _bf(0.5 * dout_v)
        dyb_ref[...] = dyb
        dh = jnp.zeros((tm, d), F32)
        nxt = _mm_nt(dyb, wd_ref[0])
        for j in range(nb):
            dact = nxt
            if j + 1 < nb:
                nxt = _mm_nt(dyb, wd_ref[j + 1])
            gt = gate_ref[j].astype(F32)
            uv = up_ref[j].astype(F32)
            s = _sigmoid(gt)
            dup = _bf(dact * (gt * s))
            dgate = _bf((dact * uv) * (s * (1.0 + gt * (1.0 - s))))
            dh = dh + _mm(dgate, wg_ref[j]) + _mm(dup, wu_ref[j])
            dgate_ref[j] = dgate
            dup_ref[j] = dup
        dxn, dgain = _rms_bwd(xv, r, gain_v, dh)
        dx_ref[...] = dout_v + dxn

        @pl.when(pl.program_id(0) == 0)
        def _():
            dgain_ref[...] = jnp.zeros_like(dgain_ref)

        dgain_ref[...] += dgain

    tok = pl.BlockSpec((tm, d), lambda i: (i, 0))
    blk = pl.BlockSpec((nb, tm, f), lambda i: (0, i, 0))
    return _pallas(
        body, name, (t // tm,),
        [tok, _const((1, d)), tok, blk, blk, _resident(wg.shape), _resident(wu.shape), _resident(wd.shape)],
        [tok, _const((1, d)), tok, blk, blk, tok],
        [jax.ShapeDtypeStruct((t, d), F32), jax.ShapeDtypeStruct((1, d), F32), jax.ShapeDtypeStruct((t, d), BF16),
         jax.ShapeDtypeStruct((nb, t, f), BF16), jax.ShapeDtypeStruct((nb, t, f), BF16), jax.ShapeDtypeStruct((t, d), BF16)],
        (x, gain, dout, gate, up, wg, wu, wd), ("arbitrary",), carried)


def _tn_matmul(a, b, name, carried=None):
    t = a.shape[-2]
    k = a.shape[-1]
    n = b.shape[-1]
    tt = min(TN_TILE, t)
    nt = t // tt

    def body(a_ref, b_ref, o_ref):
        @pl.when(pl.program_id(1) == 0)
        def _():
            o_ref[...] = jnp.zeros_like(o_ref)

        o_ref[...] += _mm_tn(a_ref[...], b_ref[...])

    g = a.shape[0] if a.ndim == 3 else b.shape[0]
    a_spec = (pl.BlockSpec((None, tt, k), lambda gi, ti: (gi, ti, 0)) if a.ndim == 3
              else pl.BlockSpec((tt, k), lambda gi, ti: (ti, 0)))
    b_spec = (pl.BlockSpec((None, tt, n), lambda gi, ti: (gi, ti, 0)) if b.ndim == 3
              else pl.BlockSpec((tt, n), lambda gi, ti: (ti, 0)))
    o_spec = pl.BlockSpec((None, k, n), lambda gi, ti: (gi, 0, 0))
    out_shape = jax.ShapeDtypeStruct((g, k, n), F32)
    return _pallas(body, name, (g, nt), [a_spec, b_spec], [o_spec], [out_shape], (a, b), ("arbitrary", "arbitrary"), carried)[0]


def _tn_rows(a, b, out, chunks, width, rows_out, name):
    t, n = b.shape
    tt = min(TN_TILE, t)
    nt = t // tt

    def body(blocks_ref, a_ref, b_ref, *rest):
        out_ref, acc, sem = rest[-3:]
        g, ti = pl.program_id(0), pl.program_id(1)

        @pl.when(ti == 0)
        def _():
            acc[...] = jnp.zeros_like(acc)

        acc[...] += _mm_tn(a_ref[...], b_ref[...])
        for gi, (_, ranges) in enumerate(chunks):
            @pl.when(jnp.logical_and(g == gi, ti == nt - 1))
            def _(ranges=ranges):
                for row, first, count in ranges:
                    cp = pltpu.make_async_copy(acc.at[first:first + count], out_ref.at[row:row + count], sem)
                    cp.start()
                    cp.wait()

    blocks = jnp.asarray([c[0] for c in chunks], jnp.int32)
    grid_spec = pltpu.PrefetchScalarGridSpec(
        num_scalar_prefetch=1, grid=(len(chunks), nt),
        in_specs=[pl.BlockSpec((tt, width), lambda g, ti, blocks_ref: (ti, blocks_ref[g])),
                  pl.BlockSpec((tt, n), lambda g, ti, blocks_ref: (ti, 0))] + ([ANY] if out is not None else []),
        out_specs=ANY, scratch_shapes=[pltpu.VMEM((width, n), F32), pltpu.SemaphoreType.DMA])
    args = (blocks, a, b) + ((out,) if out is not None else ())
    return pl.pallas_call(body, name=name, grid_spec=grid_spec, out_shape=jax.ShapeDtypeStruct((rows_out, n), F32),
                          input_output_aliases={3: 0} if out is not None else {},
                          compiler_params=_params(("arbitrary", "arbitrary")))(*args)


ROW_QKV, ROW_KR, ROW_XC = 0, Q_LORA + KV_LORA, Q_LORA + KV_LORA + QK_ROPE
ROW_GB, ROW_GC, ROW_GL = ROW_XC + D_MODEL, ROW_XC + 2 * D_MODEL, ROW_XC + 3 * D_MODEL
BIG_FROM_ROWS = ((0, ROW_GB, D_MODEL), (D_MODEL, ROW_GL, 2 * D_MODEL), (3 * D_MODEL, ROW_XC, D_MODEL), (4 * D_MODEL, ROW_GC, D_MODEL))


def _inproj_fwd(x1, gain, w_in, w_kr, qa_gain, kva_gain, qh_gain, kh_gain, w_uq, w_uk, w_uv, w_uvt, rope, carried=None):
    t, d = x1.shape
    tm = TOKEN_TILE
    chunk = 512
    chunks = []
    for col, row, size in BIG_FROM_ROWS:
        chunks += [(col + o, row + o, chunk) for o in range(0, size, chunk)]
    of_head = [[c for k, c in enumerate(chunks) if k * N_HEADS // len(chunks) == hd] for hd in range(N_HEADS)]

    def body(x_ref, g_ref, win_ref, wkr_ref, qa_ref, kva_ref, qh_ref, kh_ref, wuq_ref, wuk_ref, wuv_ref, wuvt_ref, cos_ref, slo_ref,
             shi_ref, hb_ref, big_ref, lat_ref, q_ref, k_ref, v_ref, vt_ref):
        h, _ = _rms(x_ref[...], g_ref[...])
        hb = _bf(h)
        hb_ref[...] = hb
        k_rope = _mm_nt(hb, wkr_ref[...])
        lat = jnp.concatenate([_mm_nt(hb, win_ref[ROW_QKV:ROW_KR, :]), k_rope], axis=1)
        lat_ref[...] = lat
        cq, _ = _rms(lat[:, :Q_LORA], qa_ref[...])
        ckv, _ = _rms(lat[:, Q_LORA:Q_LORA + KV_LORA], kva_ref[...])
        cqb = _bf(cq)
        ckvb = _bf(ckv)
        rope_v = (cos_ref[...], slo_ref[...], shi_ref[...])
        q_all = _mm(cqb, wuq_ref[...])
        k_all = _mm(ckvb, wuk_ref[...])
        v_ref[...] = _bf(_mm(ckvb, wuv_ref[...]))
        vt_all = _mm_nt(wuvt_ref[...], ckvb)
        for hd in range(N_HEADS):
            for col, row, size in of_head[hd]:
                big_ref[:, col:col + size] = _mm_nt(hb, win_ref[row:row + size, :])
            lanes = slice(hd * HEAD_PAD, (hd + 1) * HEAD_PAD)
            qn, _ = _rms(q_all[:, lanes], qh_ref[...], QK_DIM)
            q_ref[hd] = _bf(_rope(qn, rope_v))
            kn, _ = _rms(k_all[:, lanes] + k_rope, kh_ref[...], QK_DIM)
            k_ref[hd] = _bf(_rope(kn, rope_v))
            vt_ref[hd] = _bf(vt_all[hd * V_DIM:(hd + 1) * V_DIM])

    tok = lambda c: pl.BlockSpec((tm, c), lambda i: (i, 0))
    head = lambda c: pl.BlockSpec((N_HEADS, tm, c), lambda i: (0, i, 0))
    return _pallas(
        body, "inproj_fwd", (t // tm,),
        [tok(d), _const((1, d)), _resident(w_in.shape), _resident(w_kr.shape), _const((1, Q_LORA)), _const((1, KV_LORA)),
         _const((1, HEAD_PAD)), _const((1, HEAD_PAD)), _resident(w_uq.shape), _resident(w_uk.shape),
         _resident(w_uv.shape), _resident(w_uvt.shape), tok(HEAD_PAD), tok(HEAD_PAD), tok(HEAD_PAD)],
        [tok(d), tok(BIG_COLS), tok(LAT_COLS), head(HEAD_PAD), head(HEAD_PAD), tok(N_HEADS * V_DIM),
         pl.BlockSpec((N_HEADS, V_DIM, tm), lambda i: (0, 0, i))],
        [jax.ShapeDtypeStruct((t, d), BF16), jax.ShapeDtypeStruct((t, BIG_COLS), F32),
         jax.ShapeDtypeStruct((t, LAT_COLS), F32), jax.ShapeDtypeStruct((N_HEADS, t, HEAD_PAD), BF16),
         jax.ShapeDtypeStruct((N_HEADS, t, HEAD_PAD), BF16), jax.ShapeDtypeStruct((t, N_HEADS * V_DIM), BF16),
         jax.ShapeDtypeStruct((N_HEADS, V_DIM, t), BF16)],
        (x1, gain, w_in, w_kr, qa_gain, kva_gain, qh_gain, kh_gain, w_uq, w_uk, w_uv, w_uvt, *rope), ("arbitrary",), carried)


EXP2_SCALE = ATTN_SCALE * 1.4426950408889634


def _diagonal_keep(tk, tq):
    return lax.broadcasted_iota(jnp.int32, (tk, tq), 0) <= lax.broadcasted_iota(jnp.int32, (tk, tq), 1)


def _attn_fwd(q, k, vt, seq, carried=None):
    _, t, _ = q.shape
    nseq = t // seq
    tq = tk = ATTN_TILE
    nq = seq // tq

    def body(q_ref, k_ref, vt_ref, o_ref, lse_ref):
        i = pl.program_id(1)
        qs = [q_ref[h] for h in range(N_HEADS)]
        keep = _diagonal_keep(tk, tq)

        def scores(h, k0):
            return _mm_nt(k_ref[h, pl.ds(k0, tk), :], qs[h])

        def update(h, st, state, k0, diagonal):
            m, l, acc = state
            if diagonal:
                st = jnp.where(keep, st, MASK_VALUE)
            m_new = jnp.maximum(m, jnp.max(st, axis=0, keepdims=True))
            pt = jnp.exp2((st - m_new) * EXP2_SCALE)
            alpha = jnp.exp2((m - m_new) * EXP2_SCALE)
            l_new = alpha * l + jnp.sum(pt, axis=0, keepdims=True)
            return m_new, l_new, alpha * acc + _mm(vt_ref[h, :, pl.ds(k0, tk)], _bf(pt))

        def tiles(states, k0, diagonal):
            st, new = scores(0, k0), []
            for h in range(N_HEADS):
                st_next = scores(h + 1, k0) if h + 1 < N_HEADS else None
                new.append(update(h, st, states[h], k0, diagonal))
                st = st_next
            return tuple(new)

        init = tuple((jnp.full((1, tq), MASK_VALUE, F32), jnp.zeros((1, tq), F32), jnp.zeros((V_DIM, tq), F32))
                     for _ in range(N_HEADS))
        states = lax.fori_loop(0, i, lambda j, s: tiles(s, pl.multiple_of(j * tk, tk), False), init)
        states = tiles(states, pl.multiple_of(i * tk, tk), True)
        outs = []
        for h in range(N_HEADS):
            m, l, acc = states[h]
            outs.append((acc / l).T)
            lse_ref[h] = m * EXP2_SCALE + jnp.log2(l)
        o_ref[...] = _bf(jnp.concatenate(outs, axis=-1))

    return _pallas(
        body, "attn_fwd", (nseq, nq),
        [pl.BlockSpec((N_HEADS, tq, HEAD_PAD), lambda b, i: (0, b * nq + i, 0)),
         pl.BlockSpec((N_HEADS, seq, HEAD_PAD), lambda b, i: (0, b, 0)),
         pl.BlockSpec((N_HEADS, V_DIM, seq), lambda b, i: (0, 0, b))],
        [pl.BlockSpec((tq, N_HEADS * V_DIM), lambda b, i: (b * nq + i, 0)),
         pl.BlockSpec((N_HEADS, 1, tq), lambda b, i: (0, 0, b * nq + i))],
        [jax.ShapeDtypeStruct((t, N_HEADS * V_DIM), BF16), jax.ShapeDtypeStruct((N_HEADS, 1, t), F32)],
        (q, k, vt), ("arbitrary", "arbitrary"), carried)


ATTN_BWD_HEADS = 4


def _attn_bwd(q, k, v, do, lse, delta, seq, carried=None):
    _, t, _ = q.shape
    nseq = t // seq
    tq = tk = ATTN_TILE
    n = seq // tq
    hb = ATTN_BWD_HEADS

    def body(q_ref, k_ref, v_ref, do_ref, lse_ref, delta_ref, dq_ref, dk_ref, dv_ref):
        dq_ref[...] = jnp.zeros_like(dq_ref)
        dk_ref[...] = jnp.zeros_like(dk_ref)
        dv_ref[...] = jnp.zeros_like(dv_ref)
        keep = _diagonal_keep(tk, tq)

        def tile(h, k0, q0, diagonal):
            kj = k_ref[h, pl.ds(k0, tk), :]
            qi = q_ref[h, pl.ds(q0, tq), :]
            doi = _bf(do_ref[pl.ds(q0, tq), h * V_DIM:(h + 1) * V_DIM])
            st = _mm_nt(kj, qi)
            if diagonal:
                st = jnp.where(keep, st, MASK_VALUE)
            pt = jnp.exp2(st * EXP2_SCALE - lse_ref[h, :, pl.ds(q0, tq)])
            dv_ref[pl.ds(k0, tk), h * V_DIM:(h + 1) * V_DIM] += _mm(_bf(pt), doi)
            dpt = _mm_nt(v_ref[pl.ds(k0, tk), h * V_DIM:(h + 1) * V_DIM], doi)
            dst = _bf((pt * (dpt - delta_ref[pl.ds(h, 1), pl.ds(q0, tq)])) * ATTN_SCALE)
            dk_ref[h, pl.ds(k0, tk), :] += _mm(dst, qi)
            dq_ref[h, pl.ds(q0, tq), :] += _mm_tn(dst, kj)

        def kv_step(j, _):
            k0 = pl.multiple_of(j * tk, tk)
            for h in range(hb):
                tile(h, k0, k0, True)

            def q_step(i, _):
                q0 = pl.multiple_of(i * tq, tq)
                for h in range(hb):
                    tile(h, k0, q0, False)
                return 0

            lax.fori_loop(j + 1, n, q_step, 0)
            return 0

        lax.fori_loop(0, n, kv_step, 0)

    hspec = lambda c: pl.BlockSpec((hb, seq, c), lambda b, g: (g, b, 0))
    cols = pl.BlockSpec((seq, hb * V_DIM), lambda b, g: (b, g))
    return _pallas(
        body, "attn_bwd", (nseq, N_HEADS // hb),
        [hspec(HEAD_PAD), hspec(HEAD_PAD), cols, cols,
         pl.BlockSpec((hb, 1, seq), lambda b, g: (g, 0, b)), pl.BlockSpec((None, hb, seq), lambda b, g: (g, 0, b))],
        [hspec(HEAD_PAD), hspec(HEAD_PAD), cols],
        [jax.ShapeDtypeStruct((N_HEADS, t, HEAD_PAD), F32), jax.ShapeDtypeStruct((N_HEADS, t, HEAD_PAD), F32),
         jax.ShapeDtypeStruct((t, N_HEADS * V_DIM), F32)],
        (q, k, v, do, lse, delta), ("arbitrary", "arbitrary"), carried)


def _merged_mixers(o_ref, gb_ref, gla_ref, glb_ref, xc_ref, gc_ref, xcp_ref, gcp_ref, bias_ref, cw_ref, wpa_ref, wpc_ref,
                   first_of_seq):
    y_a = _mm(o_ref[...], wpa_ref[...])
    gb = gb_ref[...]
    u = gc_ref[...] * xc_ref[...]
    u_prev = jnp.where(first_of_seq, 0.0, gcp_ref[...] * xcp_ref[...])
    cw = cw_ref[...]
    z = cw[2:3] * u + cw[1:2] * _shift_down(u, u_prev, 1) + cw[0:1] * _shift_down(u, u_prev, 2)
    gbz = _bf(gb * z)
    y_b = _mm(gbz, wpc_ref[...])
    bias = bias_ref[...]
    gate_a = _sigmoid(gla_ref[...] + bias[:, :D_MODEL])
    gate_b = _sigmoid(glb_ref[...] + bias[:, D_MODEL:])
    return _bf(gate_a * y_a + gate_b * y_b)


def _mixer_specs(tm, seq):
    d = D_MODEL
    tok = pl.BlockSpec((tm, d), lambda i: (i, 0))
    col = lambda c: pl.BlockSpec((tm, d), lambda i: (i, c))
    prev = lambda c: pl.BlockSpec((8, d), lambda i: (jnp.maximum(i * (tm // 8) - 1, 0), c))
    o_spec = pl.BlockSpec((tm, N_HEADS * V_DIM), lambda i: (i, 0))
    fwd_specs = [o_spec, col(0), col(1), col(2), col(3), col(4), prev(3), prev(4), _const((1, 2 * d)), _const((3, d)),
                 _resident((N_HEADS * V_DIM, d)), _resident((d, d)), _resident((d, d))]
    return tok, fwd_specs


def _mix_fwd(x1, o, big, gate_bias, conv_w, w_pa, w_pc, w_out, seq, carried=None):
    t, d = x1.shape
    tm = TOKEN_TILE
    tiles_per_seq = seq // tm

    def body(x_ref, o_ref, gb_ref, gla_ref, glb_ref, xc_ref, gc_ref, xcp_ref, gcp_ref, bias_ref, cw_ref, wpa_ref, wpc_ref,
             wout_ref, x2_ref):
        first = pl.program_id(0) % tiles_per_seq == 0
        merged = _merged_mixers(o_ref, gb_ref, gla_ref, glb_ref, xc_ref, gc_ref, xcp_ref, gcp_ref, bias_ref, cw_ref, wpa_ref,
                                wpc_ref, first)
        x2_ref[...] = x_ref[...] + _mm(merged, wout_ref[...])

    tok, fwd_specs = _mixer_specs(tm, seq)
    return _pallas(body, "mix_fwd", (t // tm,), [tok] + fwd_specs, [tok], [jax.ShapeDtypeStruct((t, d), F32)],
                   (x1, o, big, big, big, big, big, big, big, gate_bias, conv_w, w_pa, w_pc, w_out), ("arbitrary",), carried)[0]


def _mix_bwd(dx2, o, big, gate_bias, conv_w, w_pa, w_pc, w_out, seq, carried=None):
    t, d = dx2.shape
    tm = TOKEN_TILE
    tiles_per_seq = seq // tm
    hv = N_HEADS * V_DIM

    def body(dx_ref, o_ref, gb_ref, gla_ref, glb_ref, xc_ref, gc_ref, xcp_ref, gcp_ref, bias_ref, cw_ref, wpa_ref, wpc_ref,
             wout_ref, do_ref, delta_ref, dz_ref, dm_ref, dbias_ref, dwpa_ref, dwpc_ref, dwout_ref):
        @pl.when(pl.program_id(0) == 0)
        def _():
            dbias_ref[...] = jnp.zeros_like(dbias_ref)
            dwpa_ref[...] = jnp.zeros_like(dwpa_ref)
            dwpc_ref[...] = jnp.zeros_like(dwpc_ref)
            dwout_ref[...] = jnp.zeros_like(dwout_ref)

        first = pl.program_id(0) % tiles_per_seq == 0
        dxb = _bf(dx_ref[...])
        dmerged = _mm_nt(dxb, wout_ref[...])
        y_a = _mm(o_ref[...], wpa_ref[...])
        bias = bias_ref[...]
        gate_a = _sigmoid(gla_ref[...] + bias[:, :d])
        gate_b = _sigmoid(glb_ref[...] + bias[:, d:])
        dya = _bf(dmerged * gate_a)
        dyb = _bf(dmerged * gate_b)
        do_v = _mm_nt(dya, wpa_ref[...])
        dgz = _mm_nt(dyb, wpc_ref[...])
        dwpa_ref[...] += _mm_tn(o_ref[...], dya)
        gb = gb_ref[...]
        u = gc_ref[...] * xc_ref[...]
        u_prev = jnp.where(first, 0.0, gcp_ref[...] * xcp_ref[...])
        cw = cw_ref[...]
        z = cw[2:3] * u + cw[1:2] * _shift_down(u, u_prev, 1) + cw[0:1] * _shift_down(u, u_prev, 2)
        gbz = _bf(gb * z)
        y_b = _mm(gbz, wpc_ref[...])
        dwpc_ref[...] += _mm_tn(gbz, dyb)
        do_ref[...] = do_v
        head = lax.broadcasted_iota(jnp.int32, (N_HEADS, hv), 0) * V_DIM
        col = lax.broadcasted_iota(jnp.int32, (N_HEADS, hv), 1)
        in_head = ((col >= head) & (col < head + V_DIM)).astype(F32)
        delta_ref[...] = lax.dot_general(in_head, do_v * o_ref[...].astype(F32), (((1,), (1,)), ((), ())),
                                         precision=lax.Precision.HIGHEST, preferred_element_type=F32)
        dz_ref[...] = dgz * gb
        dm_ref[:, :d] = _bf(dgz * z)
        merged = _bf(gate_a * y_a + gate_b * y_b)
        dwout_ref[...] += _mm_tn(merged, dxb)
        dla = (dmerged * y_a) * (gate_a * (1.0 - gate_a))
        dlb = (dmerged * y_b) * (gate_b * (1.0 - gate_b))
        dbias_ref[:, :d] += jnp.sum(dla, axis=0, keepdims=True)
        dbias_ref[:, d:] += jnp.sum(dlb, axis=0, keepdims=True)
        dm_ref[:, d:2 * d] = _bf(dla)
        dm_ref[:, 2 * d:] = _bf(dlb)

    tok, fwd_specs = _mixer_specs(tm, seq)
    return _pallas(
        body, "mix_bwd", (t // tm,), [tok] + fwd_specs,
        [pl.BlockSpec((tm, hv), lambda i: (i, 0)), pl.BlockSpec((N_HEADS, tm), lambda i: (0, i)), tok,
         pl.BlockSpec((tm, M_COLS), lambda i: (i, 0)), _const((1, 2 * d)), _const((hv, d)), _const((d, d)), _const((d, d))],
        [jax.ShapeDtypeStruct((t, hv), F32), jax.ShapeDtypeStruct((N_HEADS, t), F32), jax.ShapeDtypeStruct((t, d), F32),
         jax.ShapeDtypeStruct((t, M_COLS), BF16), jax.ShapeDtypeStruct((1, 2 * d), F32), jax.ShapeDtypeStruct((hv, d), F32),
         jax.ShapeDtypeStruct((d, d), F32), jax.ShapeDtypeStruct((d, d), F32)],
        (dx2, o, big, big, big, big, big, big, big, gate_bias, conv_w, w_pa, w_pc, w_out), ("arbitrary",), carried)


def _prep_bwd(lat, big, dz, dq, dk, dv, qa_gain, kva_gain, qh_gain, kh_gain, w_uq, w_uk, w_uv, rope, conv_w, seq, carried=None):
    t = lat.shape[0]
    d = D_MODEL
    tm = PREP_TILE
    tiles_per_seq = seq // tm
    last_blk = t // 8 - 1

    def body(lat_ref, xc_ref, gc_ref, dz_ref, dzn_ref, dq_ref, dk_ref, dv_ref, qa_ref, kva_ref, qh_ref, kh_ref, wuq_ref, wuk_ref,
             wuv_ref, cos_ref, slo_ref, shi_ref, cw_ref,
             dp_ref, dwuq_ref, dwuk_ref, dwuv_ref, dqa_ref, dkva_ref, dqh_ref, dkh_ref, dcw_ref):
        pid = pl.program_id(0)

        @pl.when(pid == 0)
        def _():
            for r in (dwuq_ref, dwuk_ref, dwuv_ref, dqa_ref, dkva_ref, dqh_ref, dkh_ref, dcw_ref):
                r[...] = jnp.zeros_like(r)

        lat_v = lat_ref[...]
        q_lat = lat_v[:, :Q_LORA]
        kv_lat = lat_v[:, Q_LORA:Q_LORA + KV_LORA]
        k_rope = lat_v[:, Q_LORA + KV_LORA:]
        qa_gain_v = qa_ref[...]
        kva_gain_v = kva_ref[...]
        qh_gain_v = qh_ref[...]
        kh_gain_v = kh_ref[...]
        cq, rq = _rms(q_lat, qa_gain_v)
        ckv, rkv = _rms(kv_lat, kva_gain_v)
        cqb = _bf(cq)
        ckvb = _bf(ckv)
        rope_v = (cos_ref[...], slo_ref[...], shi_ref[...])
        lane = lax.broadcasted_iota(jnp.int32, (tm, HEAD_PAD), 1)
        rope_lanes = (lane >= QK_NOPE) & (lane < QK_DIM)
        dk_rope = jnp.zeros((tm, HEAD_PAD), F32)
        dqh_gain = jnp.zeros((1, HEAD_PAD), F32)
        dkh_gain = jnp.zeros((1, HEAD_PAD), F32)
        q_all = _mm(cqb, wuq_ref[...])
        k_all = _mm(ckvb, wuk_ref[...])
        dvb = _bf(dv_ref[...])
        dckv = _mm_nt(dvb, wuv_ref[...])
        dwuv_ref[...] += _mm_tn(ckvb, dvb)

        last = pid % tiles_per_seq == tiles_per_seq - 1
        dzv = dz_ref[...]
        dz_next = jnp.where(last, 0.0, dzn_ref[...])
        dz1 = _shift_up(dzv, dz_next, 1)
        dz2 = _shift_up(dzv, dz_next, 2)
        cw = cw_ref[...]
        xc = xc_ref[...]
        gc = gc_ref[...]
        u = gc * xc
        du = cw[2:3] * dzv + cw[1:2] * dz1 + cw[0:1] * dz2
        dp_ref[:, :d] = _bf(du * gc)
        dp_ref[:, d:2 * d] = _bf(du * xc)
        dcw_ref[0:1, :] += jnp.sum(dz2 * u, axis=0, keepdims=True)
        dcw_ref[1:2, :] += jnp.sum(dz1 * u, axis=0, keepdims=True)
        dcw_ref[2:3, :] += jnp.sum(dzv * u, axis=0, keepdims=True)

        dcq = jnp.zeros((tm, Q_LORA), F32)
        half = N_HEADS // 2
        for part in range(2):
            dq_heads, dk_heads = [], []
            for hd in range(part * half, (part + 1) * half):
                lanes = slice(hd * HEAD_PAD, (hd + 1) * HEAD_PAD)
                q_pre = q_all[:, lanes]
                _, rr = _rms(q_pre, qh_gain_v, QK_DIM)
                dq_pre, dg = _rms_bwd(q_pre, rr, qh_gain_v, _rope_bwd(dq_ref[hd], rope_v), QK_DIM)
                dqh_gain = dqh_gain + dg
                dq_heads.append(_bf(dq_pre))

                k_pre = k_all[:, lanes] + k_rope
                _, rr = _rms(k_pre, kh_gain_v, QK_DIM)
                dk_pre, dg = _rms_bwd(k_pre, rr, kh_gain_v, _rope_bwd(dk_ref[hd], rope_v), QK_DIM)
                dkh_gain = dkh_gain + dg
                dk_rope = dk_rope + jnp.where(rope_lanes, dk_pre, 0.0)
                dk_heads.append(_bf(dk_pre))
            dq_part = jnp.concatenate(dq_heads, axis=1)
            dk_part = jnp.concatenate(dk_heads, axis=1)
            cols = slice(part * half * HEAD_PAD, (part + 1) * half * HEAD_PAD)
            dcq = dcq + _mm_nt(dq_part, wuq_ref[:, cols])
            dckv = dckv + _mm_nt(dk_part, wuk_ref[:, cols])
            dwuq_ref[:, cols] += _mm_tn(cqb, dq_part)
            dwuk_ref[:, cols] += _mm_tn(ckvb, dk_part)
        dqh_ref[...] += dqh_gain
        dkh_ref[...] += dkh_gain
        dq_lat, dg = _rms_bwd(q_lat, rq, qa_gain_v, dcq)
        dqa_ref[...] += dg
        dkv_lat, dg = _rms_bwd(kv_lat, rkv, kva_gain_v, dckv)
        dkva_ref[...] += dg
        dp_ref[:, 2 * d:2 * d + Q_LORA] = _bf(dq_lat)
        dp_ref[:, 2 * d + Q_LORA:2 * d + Q_LORA + KV_LORA] = _bf(dkv_lat)
        dp_ref[:, 2 * d + Q_LORA + KV_LORA:] = _bf(dk_rope)

    tok = lambda c: pl.BlockSpec((tm, c), lambda i: (i, 0))
    col = lambda c: pl.BlockSpec((tm, d), lambda i: (i, c))
    head = lambda c: pl.BlockSpec((N_HEADS, tm, c), lambda i: (0, i, 0))
    nxt = pl.BlockSpec((8, d), lambda i: (jnp.minimum((i + 1) * (tm // 8), last_blk), 0))
    return _pallas(
        body, "prep_bwd", (t // tm,),
        [tok(LAT_COLS), col(3), col(4), tok(d), nxt, head(HEAD_PAD), head(HEAD_PAD), tok(N_HEADS * V_DIM),
         _const((1, Q_LORA)), _const((1, KV_LORA)), _const((1, HEAD_PAD)), _const((1, HEAD_PAD)),
         _resident(w_uq.shape), _resident(w_uk.shape), _resident(w_uv.shape), tok(HEAD_PAD), tok(HEAD_PAD), tok(HEAD_PAD),
         _const((3, d))],
        [tok(P_COLS), _const(w_uq.shape), _const(w_uk.shape), _const(w_uv.shape), _const((1, Q_LORA)),
         _const((1, KV_LORA)), _const((1, HEAD_PAD)), _const((1, HEAD_PAD)), _const((3, d))],
        [jax.ShapeDtypeStruct((t, P_COLS), BF16), jax.ShapeDtypeStruct(w_uq.shape, F32),
         jax.ShapeDtypeStruct(w_uk.shape, F32), jax.ShapeDtypeStruct(w_uv.shape, F32),
         jax.ShapeDtypeStruct((1, Q_LORA), F32), jax.ShapeDtypeStruct((1, KV_LORA), F32),
         jax.ShapeDtypeStruct((1, HEAD_PAD), F32), jax.ShapeDtypeStruct((1, HEAD_PAD), F32), jax.ShapeDtypeStruct((3, d), F32)],
        (lat, big, big, dz, dz, dq, dk, dv, qa_gain, kva_gain, qh_gain, kh_gain, w_uq, w_uk, w_uv, *rope, conv_w),
        ("arbitrary",), carried)


def _inproj_bwd(x1, gain, dx2, dm, dp, w_in, w_kr, carried=None):
    t, d = x1.shape
    tm = TOKEN_TILE

    def body(x_ref, g_ref, dx2_ref, dm_ref, dp_ref, win_ref, wkr_ref, dx1_ref, dgain_ref):
        xv = x_ref[...]
        gain_v = g_ref[...]
        _, r = _rms(xv, gain_v)
        dh = (_mm(dm_ref[:, :d], win_ref[ROW_GB:ROW_GC, :]) + _mm(dm_ref[:, d:], win_ref[ROW_GL:, :])
              + _mm(dp_ref[:, :d], win_ref[ROW_XC:ROW_GB, :]) + _mm(dp_ref[:, d:2 * d], win_ref[ROW_GC:ROW_GL, :])
              + _mm(dp_ref[:, 2 * d:2 * d + ROW_KR], win_ref[ROW_QKV:ROW_KR, :]) + _mm(dp_ref[:, 2 * d + ROW_KR:], wkr_ref[...]))
        dxn, dgain = _rms_bwd(xv, r, gain_v, dh)
        dx1_ref[...] = dx2_ref[...] + dxn

        @pl.when(pl.program_id(0) == 0)
        def _():
            dgain_ref[...] = jnp.zeros_like(dgain_ref)

        dgain_ref[...] += dgain

    tok = lambda c: pl.BlockSpec((tm, c), lambda i: (i, 0))
    return _pallas(
        body, "inproj_bwd", (t // tm,),
        [tok(d), _const((1, d)), tok(d), tok(M_COLS), tok(P_COLS), _resident(w_in.shape), _resident(w_kr.shape)],
        [tok(d), _const((1, d))], [jax.ShapeDtypeStruct((t, d), F32), jax.ShapeDtypeStruct((1, d), F32)],
        (x1, gain, dx2, dm, dp, w_in, w_kr), ("arbitrary",), carried)


def _adamw(quads, name, carried=None):
    k = len(quads)
    rows, cols = quads[0][0].shape
    tr, tc = rows, cols
    for cand in (512, 352, 256, 192, 128, 64):
        if rows % cand == 0 and rows > cand:
            tr = cand
            break
    if tr == rows and rows * cols > 512 * 1024 and cols % 256 == 0:
        tc = 256
    while k * 14 * tr * tc * 4 > (VMEM_LIMIT * 3) // 4 and tr % 16 == 0:
        tr //= 2

    def body(*refs):
        for i in range(k):
            w_ref, g_ref, m_ref, v_ref = refs[4 * i:4 * i + 4]
            delta_ref, nm_ref, nv_ref = refs[4 * k + 3 * i:4 * k + 3 * i + 3]
            delta_ref[...], nm_ref[...], nv_ref[...] = _adamw_update(w_ref[...], g_ref[...], m_ref[...], v_ref[...])

    spec = pl.BlockSpec((tr, tc), lambda i, j: (i, j))
    shape = jax.ShapeDtypeStruct((rows, cols), F32)
    outs = _pallas(body, name, (rows // tr, cols // tc), [spec] * (4 * k), [spec] * (3 * k), [shape] * (3 * k),
                   [a for quad in quads for a in quad], ("arbitrary", "arbitrary"), carried)
    return [tuple(outs[3 * i:3 * i + 3]) for i in range(k)]


def _adamw_update(w, g, m, v):
    nm = ADAM_B1 * m + (1.0 - ADAM_B1) * g
    nv = ADAM_B2 * v + (1.0 - ADAM_B2) * (g * g)
    m_hat = nm * (1.0 / (1.0 - ADAM_B1 ** ADAM_STEP))
    v_hat = nv * (1.0 / (1.0 - ADAM_B2 ** ADAM_STEP))
    return -ADAM_LR * (m_hat / (jnp.sqrt(v_hat) + ADAM_EPS) + ADAM_WD * w), nm, nv


def _adamw_whole(quads, name):
    k = len(quads)

    def body(*refs):
        for i in range(k):
            w_ref, g_ref, m_ref, v_ref = refs[4 * i:4 * i + 4]
            g_out, delta_ref, nm_ref, nv_ref = refs[4 * k + 4 * i:4 * k + 4 * i + 4]
            gv = g_ref[...]
            g_out[...] = gv
            delta_ref[...], nm_ref[...], nv_ref[...] = _adamw_update(w_ref[...], gv, m_ref[...], v_ref[...])

    vm = pl.BlockSpec(memory_space=pltpu.VMEM)
    outs = pl.pallas_call(body, name=name, in_specs=[vm] * (4 * k), out_specs=[vm] * (4 * k),
                          out_shape=[jax.ShapeDtypeStruct(q[0].shape, F32) for q in quads for _ in range(4)],
                          compiler_params=_params())(*[a for quad in quads for a in quad])
    return [tuple(outs[4 * i:4 * i + 4]) for i in range(k)]


def _adamw_small(packed_grads, triples, segments):
    k = len(triples)

    def body(*refs):
        g_ref = refs[0]
        off = 0
        for i in range(k):
            w_ref, m_ref, v_ref = refs[1 + 3 * i:4 + 3 * i]
            g_out, delta_ref, nm_ref, nv_ref = refs[1 + 3 * k + 4 * i:5 + 3 * k + 4 * i]
            gv = g_ref[:, off:off + w_ref.shape[1]]
            g_out[...] = gv
            delta_ref[...], nm_ref[...], nv_ref[...] = _adamw_update(w_ref[...], gv, m_ref[...], v_ref[...])
            off += segments[i]

    vm = pl.BlockSpec(memory_space=pltpu.VMEM)
    outs = pl.pallas_call(
        body, name="adamw_small", in_specs=[vm] * (1 + 3 * k), out_specs=[vm] * (4 * k),
        out_shape=[jax.ShapeDtypeStruct(w.shape, F32) for w, _, _ in triples for _ in range(4)],
    )(packed_grads, *[a for triple in triples for a in triple])
    return [tuple(outs[4 * i:4 * i + 4]) for i in range(k)]


def _place():
    x, y, c = lax.axis_index("x"), lax.axis_index("y"), lax.axis_index("c")
    other_chips = [(1 - x, y), (x, 1 - y), (1 - x, 1 - y)]
    return x, y, c, other_chips


def _remote(src, dst, sems, send, recv, device):
    return pltpu.make_async_remote_copy(src_ref=src, dst_ref=dst, send_sem=sems.at[send], recv_sem=sems.at[recv],
                                        device_id=device, device_id_type=MESH_ID)


def _cast_shards(shards, out_dtypes, n_first):
    n = len(shards)
    out_shape = [jax.ShapeDtypeStruct((N_CHIPS,) + s.shape, dt) for s, dt in zip(shards, out_dtypes)]
    gather = _gather_carried(out_shape[:n_first])

    def body(*refs):
        ins, outs, stage, sems = refs[:n], refs[n:2 * n], refs[2 * n:3 * n], refs[3 * n]
        x, y, _, _ = _place()
        me = 2 * x + y

        def cast(first, last):
            copies = []
            for w in range(first, last):
                stage[w][...] = ins[w][...].astype(out_dtypes[w])
                copies.append(pltpu.make_async_copy(stage[w], outs[w].at[me], sems.at[w]))
                copies[-1].start()
            for cp in copies:
                cp.wait()

        cast(0, n_first)
        gather.start(None, outs[:n_first], sems, n)
        cast(n_first, n)
        gather.finish(None, outs[:n_first], sems, n)

    vm = pl.BlockSpec(memory_space=pltpu.VMEM)
    return pl.pallas_call(
        body, name="cast_shards", in_specs=[vm] * n, out_specs=[ANY] * n, out_shape=out_shape,
        scratch_shapes=[pltpu.VMEM(s.shape, dt) for s, dt in zip(shards, out_dtypes)] + [pltpu.SemaphoreType.DMA((n + gather.n_sems,))],
        compiler_params=_params())(*shards)


BF16_ROWS = 16


def _split_rows(rows):
    return (rows // 2) % BF16_ROWS == 0


def _half_shape(rows, cols):
    return (rows // 2, cols) if _split_rows(rows) else (rows, cols // 2)


def _half(rows, cols, which):
    if _split_rows(rows):
        return (pl.ds(pl.multiple_of(which * (rows // 2), BF16_ROWS), rows // 2), slice(None))
    return (slice(None), pl.ds(pl.multiple_of(which * (cols // 2), 128), cols // 2))


def _gather_carried(bufs):
    n = len(bufs)

    def half(w, slot, which):
        _, rows, cols = bufs[w].shape
        return (slot,) + _half(rows, cols, which)

    def start(ins, outs, sems, base):
        x, y, c, other_chips = _place()
        me = 2 * x + y
        for w in range(n):
            mine = outs[w].at[half(w, me, c)]
            for p, (px, py) in enumerate(other_chips):
                _remote(mine, mine, sems, base + 12 * w + p, base + 12 * w + 3 + p, (px, py, c)).start()

    def finish(ins, outs, sems, base):
        x, y, c, other_chips = _place()
        me = 2 * x + y
        for w in range(n):
            for p, (px, py) in enumerate(other_chips):
                got = outs[w].at[half(w, 2 * px + py, c)]
                _remote(got, got, sems, base + 12 * w + p, base + 12 * w + 3 + p, (px, py, c)).wait_recv()
                _remote(got, got, sems, base + 12 * w + 6 + p, base + 12 * w + 9 + p, (x, y, 1 - c)).start()
        for w in range(n):
            mine = outs[w].at[half(w, me, c)]
            for p, (px, py) in enumerate(other_chips):
                got = outs[w].at[half(w, 2 * px + py, c)]
                theirs = outs[w].at[half(w, 2 * px + py, 1 - c)]
                _remote(got, theirs, sems, base + 12 * w + 6 + p, base + 12 * w + 9 + p, (x, y, 1 - c)).wait()
                _remote(mine, mine, sems, base + 12 * w + p, base + 12 * w + 3 + p, (px, py, c)).wait_send()

    shapes = [jax.ShapeDtypeStruct(b.shape, b.dtype) for b in bufs]
    return _Carried(bufs, shapes, {w: w for w in range(n)}, 12 * n, start, finish)


def _swap_carried(grads):
    n = len(grads)

    def copy(w, ins, outs, sems, base):
        x, y, c, _ = _place()
        _, rows, cols = grads[w].shape
        theirs = ins[w].at[(slice(None),) + _half(rows, cols, 1 - c)]
        return _remote(theirs, outs[w], sems, base + 2 * w, base + 2 * w + 1, (x, y, 1 - c))

    def start(ins, outs, sems, base):
        for w in range(n):
            copy(w, ins, outs, sems, base).start()

    def finish(ins, outs, sems, base):
        for w in range(n):
            copy(w, ins, outs, sems, base).wait()

    shapes = [jax.ShapeDtypeStruct((g.shape[0],) + _half_shape(*g.shape[1:]), F32) for g in grads]
    return _Carried(grads, shapes, {}, 2 * n, start, finish)


def _row_tile(rows):
    for cand in (512, 352, 256, 192, 128, 96, 64, 32, 16):
        if rows % cand == 0:
            return cand
    return rows


def _half_block_index(split_rows, tiles, i, core):
    return (core * tiles + i, 0) if split_rows else (i, core)


def _chip_partial(grad, other, place, name):
    nblk, hr, hc = other.shape
    by_rows = _split_rows(grad.shape[1])
    tr = _row_tile(hr)
    tiles = hr // tr

    def body(place_ref, g_ref, o_ref, own_ref, sum_bf_ref):
        s = g_ref[...] + o_ref[...]
        sum_bf_ref[...] = _bf(s)

        @pl.when(pl.program_id(1) == place_ref[0])
        def _():
            own_ref[...] = s

    grid_spec = pltpu.PrefetchScalarGridSpec(
        num_scalar_prefetch=1, grid=(tiles, nblk),
        in_specs=[pl.BlockSpec((None, tr, hc), lambda i, b, place_ref: (b,) + _half_block_index(by_rows, tiles, i, place_ref[1])),
                  pl.BlockSpec((None, tr, hc), lambda i, b, place_ref: (b, i, 0))],
        out_specs=[pl.BlockSpec((tr, hc), lambda i, b, place_ref: (i, 0)),
                   pl.BlockSpec((None, tr, hc), lambda i, b, place_ref: (b, i, 0))])
    return pl.pallas_call(body, name=name, grid_spec=grid_spec,
                          out_shape=[jax.ShapeDtypeStruct((hr, hc), F32), jax.ShapeDtypeStruct(other.shape, BF16)],
                          compiler_params=_params(("arbitrary", "arbitrary")))(place, grad, other)


def _chip_partial_small(grads, others, place):
    n = len(grads)

    def body(*refs):
        place_ref, g_refs, o_refs = refs[0], refs[1:1 + n], refs[1 + n:1 + 2 * n]
        own_refs, bf_refs = refs[1 + 2 * n:1 + 3 * n], refs[1 + 3 * n:]
        chip, core = place_ref[0], place_ref[1]
        for w in range(n):
            _, rows, cols = grads[w].shape
            half = _half(rows, cols, core)
            bf_refs[w][...] = _bf(g_refs[w][(slice(None),) + half] + o_refs[w][...])
            own_refs[w][...] = g_refs[w][(chip,) + half] + o_refs[w][chip]

    vm = pl.BlockSpec(memory_space=pltpu.VMEM)
    outs = pl.pallas_call(
        body, name="chip_partial_small", in_specs=[pl.BlockSpec(memory_space=pltpu.SMEM)] + [vm] * (2 * n), out_specs=[vm] * (2 * n),
        out_shape=[jax.ShapeDtypeStruct(o.shape[1:], F32) for o in others] + [jax.ShapeDtypeStruct(o.shape, BF16) for o in others],
        compiler_params=_params())(place, *grads, *others)
    return list(zip(outs[:n], outs[n:]))


def _chip_total_small(owns, receiveds, place, shapes):
    n = len(owns)

    def body(*refs):
        place_ref, own_refs, r_refs, out_refs = refs[0], refs[1:1 + n], refs[1 + n:1 + 2 * n], refs[1 + 2 * n:]
        chip, core = place_ref[0], place_ref[1]
        for w in range(n):
            r = [r_refs[w][(chip + k) % N_CHIPS].astype(F32) for k in (1, 2, 3)]
            out_refs[w][_half(*shapes[w], core)] = own_refs[w][...] + ((r[0] + r[1]) + r[2])

    vm = pl.BlockSpec(memory_space=pltpu.VMEM)
    return list(pl.pallas_call(
        body, name="chip_total_small", in_specs=[pl.BlockSpec(memory_space=pltpu.SMEM)] + [vm] * (2 * n), out_specs=[vm] * n,
        out_shape=[jax.ShapeDtypeStruct(tuple(s), F32) for s in shapes], compiler_params=_params())(place, *owns, *receiveds))


def _send_carried(partials):
    n = len(partials)

    def start(ins, outs, sems, base):
        x, y, c, other_chips = _place()
        me = 2 * x + y
        for w in range(n):
            for p, (px, py) in enumerate(other_chips):
                _remote(ins[w].at[2 * px + py], outs[w].at[me], sems, base + 6 * w + p, base + 6 * w + 3 + p, (px, py, c)).start()

    def finish(ins, outs, sems, base):
        x, y, c, other_chips = _place()
        for w in range(n):
            for p, (px, py) in enumerate(other_chips):
                _remote(ins[w].at[2 * px + py], outs[w].at[2 * px + py], sems, base + 6 * w + p, base + 6 * w + 3 + p,
                        (px, py, c)).wait()

    return _Carried(partials, [jax.ShapeDtypeStruct(p.shape, BF16) for p in partials], {}, 6 * n, start, finish)


def _chip_total(own, received, place, shape, name):
    hr, hc = own.shape
    by_rows = _split_rows(shape[0])
    tr = _row_tile(hr)
    tiles = hr // tr

    def body(place_ref, own_ref, r1_ref, r2_ref, r3_ref, out_ref):
        out_ref[...] = own_ref[...] + ((r1_ref[...].astype(F32) + r2_ref[...].astype(F32)) + r3_ref[...].astype(F32))

    def slot(k):
        return pl.BlockSpec((None, tr, hc), lambda i, place_ref: ((place_ref[0] + k) % N_CHIPS, i, 0))

    grid_spec = pltpu.PrefetchScalarGridSpec(
        num_scalar_prefetch=1, grid=(tiles,), in_specs=[pl.BlockSpec((tr, hc), lambda i, place_ref: (i, 0)), slot(1), slot(2), slot(3)],
        out_specs=pl.BlockSpec((tr, hc), lambda i, place_ref: _half_block_index(by_rows, tiles, i, place_ref[1])))
    return pl.pallas_call(body, name=name, grid_spec=grid_spec, out_shape=jax.ShapeDtypeStruct(tuple(shape), F32),
                          compiler_params=_params(("arbitrary",)))(place, own, received, received, received)


def _join_carried(totals):
    n = len(totals)

    def copy(w, outs, sems, base):
        x, y, c, _ = _place()
        mine = outs[w].at[_half(*totals[w].shape, c)]
        return _remote(mine, mine, sems, base + 2 * w, base + 2 * w + 1, (x, y, 1 - c))

    def start(ins, outs, sems, base):
        for w in range(n):
            copy(w, outs, sems, base).start()

    def finish(ins, outs, sems, base):
        for w in range(n):
            copy(w, outs, sems, base).wait()

    shapes = [jax.ShapeDtypeStruct(a.shape, F32) for a in totals]
    return _Carried(totals, shapes, {w: w for w in range(n)}, 2 * n, start, finish)


def _sum_devices(vec):
    rows, n = vec.shape

    def body(v_ref, out_ref, buf, send_sems, recv_sems):
        x, y, c, _ = _place()
        me = 4 * x + 2 * y + c
        buf[me] = v_ref[...]
        sends = []
        for k in range(1, N_DEV):
            peer = (1 - x if k & 4 else x, 1 - y if k & 2 else y, 1 - c if k & 1 else c)
            cp = pltpu.make_async_remote_copy(src_ref=v_ref, dst_ref=buf.at[me], send_sem=send_sems.at[k], recv_sem=recv_sems.at[k],
                                              device_id=peer, device_id_type=MESH_ID)
            cp.start()
            sends.append(cp)
        for cp in sends:
            cp.wait()
        total = buf[0]
        for dev in range(1, N_DEV):
            total = total + buf[dev]
        out_ref[...] = total

    vm = pl.BlockSpec(memory_space=pltpu.VMEM)
    return pl.pallas_call(
        body, name="sum_devices", in_specs=[vm], out_specs=vm, out_shape=jax.ShapeDtypeStruct((rows, n), F32),
        scratch_shapes=[pltpu.VMEM((N_DEV, rows, n), F32), pltpu.SemaphoreType.DMA((N_DEV,)), pltpu.SemaphoreType.DMA((N_DEV,))],
    )(vec)


def _rope_tables(positions):
    half = ROPE_HALF
    inv_freq = 1.0 / (ROPE_THETA ** (jnp.arange(half, dtype=F32) / half))
    ang = positions.astype(F32).reshape(-1, 1) * inv_freq
    cos, sin = jnp.cos(ang), jnp.sin(ang)
    t = ang.shape[0]
    ones, zeros = jnp.ones((t, QK_NOPE), F32), jnp.zeros((t, QK_NOPE), F32)
    pad, none = HEAD_PAD - QK_DIM, zeros[:, :half]
    cos_full = jnp.concatenate([ones, cos, cos, ones[:, :pad]], axis=1)
    s_lo = jnp.concatenate([zeros, -sin, none, zeros[:, :pad]], axis=1)
    s_hi = jnp.concatenate([zeros, none, sin, zeros[:, :pad]], axis=1)
    return cos_full, s_lo, s_hi


def _partials(names, grads, from_sibling, place):
    return [_chip_partial(g, o, place, "chip_partial_" + n) for n, g, o in zip(names, grads, from_sibling)]


def _totals(names, grads, partials, received, place):
    return [_chip_total(pf, r, place, g.shape[1:], "chip_total_" + n) for n, g, (pf, _), r in zip(names, grads, partials, received)]


def _kernel_layouts(full):
    w_in = full["w_in"]
    w_kr = jnp.pad(w_in[ROW_KR:ROW_XC], ((QK_NOPE, HEAD_PAD - QK_DIM), (0, 0)))
    w_uq = jnp.pad(full["w_uq"].reshape(Q_LORA, N_HEADS, QK_DIM), ((0, 0), (0, 0), (0, HEAD_PAD - QK_DIM)))
    w_uk = jnp.pad(full["w_uk"].reshape(KV_LORA, N_HEADS, QK_NOPE), ((0, 0), (0, 0), (0, HEAD_PAD - QK_NOPE)))
    return {"w_in": w_in, "w_kr": w_kr, "w_uq": w_uq.reshape(Q_LORA, N_HEADS * HEAD_PAD),
            "w_uk": w_uk.reshape(KV_LORA, N_HEADS * HEAD_PAD), "w_uv": full["w_uv"], "w_uvt": full["w_uv"].T}


def _global_layouts(g):
    w_uq = g["w_uq"].reshape(Q_LORA, N_HEADS, HEAD_PAD)[:, :, :QK_DIM].reshape(Q_LORA, N_HEADS * QK_DIM)
    w_uk = g["w_uk"].reshape(KV_LORA, N_HEADS, HEAD_PAD)[:, :, :QK_NOPE].reshape(KV_LORA, N_HEADS * QK_NOPE)
    return {"w_in": g["w_in"], "w_uq": w_uq, "w_uk": w_uk, "w_uv": g["w_uv"], "w_proj_attn": g["w_pa"], "w_proj_conv": g["w_pc"],
            "w_out": g["w_out"]}


def _dw_in(dm, dp, h2b):
    d, rows = D_MODEL, ROW_GL + 2 * D_MODEL
    wm, wp = M_COLS // 2, P_COLS // 2
    from_dm = [(0, [(ROW_GB, 0, d), (ROW_GL, d, wm - d)]), (1, [(ROW_GL + wm - d, 0, wm)])]
    from_dp = [(0, [(ROW_XC, 0, d), (ROW_GC, d, wp - d)]),
               (1, [(ROW_GC + wp - d, 0, 2 * d - wp), (ROW_QKV, 2 * d - wp, ROW_KR), (ROW_KR, 2 * d - wp + ROW_KR + QK_NOPE, QK_ROPE)])]
    out = _tn_rows(dm, h2b, None, from_dm, wm, rows, "dw_in_m")
    return _tn_rows(dp, h2b, out, from_dp, wp, rows, "dw_in_p")


def _col_blocks(a):
    r, c = a.shape
    return a.reshape(r, N_CHIPS, c // N_CHIPS).transpose(1, 0, 2)


def _from_col_blocks(a):
    n, r, c = a.shape
    return a.transpose(1, 0, 2).reshape(r, n * c)


COL_SHARDED = ("w_uq", "w_uk", "w_uv", "w_proj_attn")
TRANSPOSED = ("ffn1_w_gate", "ffn1_w_up", "ffn2_w_gate", "ffn2_w_up", "w_in")
SMALL = (("ffn1_norm", 1024), ("mix_norm", 1024), ("gate_bias", 2048), ("q_a_norm", 384), ("kv_a_norm", 256),
         ("q_head_norm", 128), ("k_head_norm", 128), ("ffn2_norm", 1024))
WEIGHT_ORDER = ("ffn1_norm", "ffn1_w_gate", "ffn1_w_up", "ffn1_w_down", "mix_norm", "w_in", "gate_bias", "q_a_norm", "w_uq",
                "kv_a_norm", "w_uk", "w_uv", "q_head_norm", "k_head_norm", "w_proj_attn", "conv_w", "w_proj_conv", "w_out",
                "ffn2_norm", "ffn2_w_gate", "ffn2_w_up", "ffn2_w_down")
MATRICES = ("ffn1_w_gate", "ffn1_w_up", "ffn1_w_down", "w_in", "w_uq", "w_uk", "w_uv", "w_proj_attn", "w_proj_conv", "w_out",
            "ffn2_w_gate", "ffn2_w_up", "ffn2_w_down")
GROUP_FFN1 = ("ffn1_w_gate", "ffn1_w_up", "ffn1_w_down")
GROUP_IN = ("w_in", "w_uq", "w_uk", "w_uv", "conv_w")
GROUP_MIX = ("w_proj_attn", "w_proj_conv", "w_out")
GROUP_FFN2 = ("ffn2_w_gate", "ffn2_w_up", "ffn2_w_down")
GROUP_MID = ("w_in", "w_uq", "w_uk", "w_uv", "w_proj_attn", "w_proj_conv", "w_out")


def _pad_lanes(a, n):
    return jnp.pad(a.reshape(1, -1), ((0, 0), (0, n - a.size)))


def kernel(x, positions, ffn1_norm, ffn1_w_gate, ffn1_w_up, ffn1_w_down, mix_norm, w_in, gate_bias, q_a_norm, w_uq, kv_a_norm, w_uk, w_uv, q_head_norm, k_head_norm, w_proj_attn, conv_w, w_proj_conv, w_out, ffn2_norm, ffn2_w_gate, ffn2_w_up, ffn2_w_down, loss_target, m_ffn1_norm, m_ffn1_w_gate, m_ffn1_w_up, m_ffn1_w_down, m_mix_norm, m_w_in, m_gate_bias, m_q_a_norm, m_w_uq, m_kv_a_norm, m_w_uk, m_w_uv, m_q_head_norm, m_k_head_norm, m_w_proj_attn, m_conv_w, m_w_proj_conv, m_w_out, m_ffn2_norm, m_ffn2_w_gate, m_ffn2_w_up, m_ffn2_w_down, v_ffn1_norm, v_ffn1_w_gate, v_ffn1_w_up, v_ffn1_w_down, v_mix_norm, v_w_in, v_gate_bias, v_q_a_norm, v_w_uq, v_kv_a_norm, v_w_uk, v_w_uv, v_q_head_norm, v_k_head_norm, v_w_proj_attn, v_conv_w, v_w_proj_conv, v_w_out, v_ffn2_norm, v_ffn2_w_gate, v_ffn2_w_up, v_ffn2_w_down):
    args = dict(locals())
    view = lambda n, a: a.T if n in TRANSPOSED else a
    weights = {n: view(n, args[n]) for n in WEIGHT_ORDER}
    moments_m = {n: view(n, args["m_" + n]) for n in WEIGHT_ORDER}
    moments_v = {n: view(n, args["v_" + n]) for n in WEIGHT_ORDER}
    nb, seq, d = x.shape
    t = nb * seq
    chip = (2 * lax.axis_index("x") + lax.axis_index("y")).astype(jnp.int32)
    place = jnp.stack([chip, lax.axis_index("c").astype(jnp.int32)])
    grads, delta, new_m, new_v = {}, {}, {}, {}

    def adamw(names, carried=None):
        results = _adamw([(weights[n], grads[n], moments_m[n], moments_v[n]) for n in names], "adamw_" + names[0], carried)
        for n, (dn, mn, vn) in zip(names, results):
            delta[n], new_m[n], new_v[n] = dn, mn, vn

    conv_rows = conv_w.shape[0]
    conv_shard = jnp.pad(conv_w, ((0, 16 - conv_rows), (0, 0)))
    assert MATRICES[:len(GROUP_FFN1)] == GROUP_FFN1
    bufs = dict(zip(MATRICES + ("conv_w",), _cast_shards([weights[n] for n in MATRICES] + [conv_shard],
                                                         [BF16] * len(MATRICES) + [F32], len(GROUP_FFN1))))
    blocks = {n: bufs[n] for n in GROUP_FFN1}
    p = {n: _pad_lanes(weights[n], size) for n, size in SMALL}
    rope = _rope_tables(positions)
    x_tok = x.reshape(t, d)

    gather_in = _gather_carried([bufs[n] for n in GROUP_IN])
    x1, gate1, up1, act1 = _ffn_fwd(x_tok, p["ffn1_norm"], blocks["ffn1_w_gate"], blocks["ffn1_w_up"], blocks["ffn1_w_down"], None,
                                    "ffn1_fwd", gather_in)
    blocks.update(zip(GROUP_IN, gather_in.results))
    w = _kernel_layouts({"w_in": blocks["w_in"].reshape(-1, d), **{n: _from_col_blocks(blocks[n]) for n in ("w_uq", "w_uk", "w_uv")}})
    p["conv_w"] = _from_col_blocks(blocks["conv_w"])[:conv_rows]

    gather_mix = _gather_carried([bufs[n] for n in GROUP_MIX + GROUP_FFN2[2:]])
    h2b, big, lat, q, k, v, vt = _inproj_fwd(x1, p["mix_norm"], w["w_in"], w["w_kr"], p["q_a_norm"], p["kv_a_norm"], p["q_head_norm"],
                                             p["k_head_norm"], w["w_uq"], w["w_uk"], w["w_uv"], w["w_uvt"], rope, gather_mix)
    blocks.update(zip(GROUP_MIX + GROUP_FFN2[2:], gather_mix.results))
    w_pa = _from_col_blocks(blocks["w_proj_attn"])
    w_pc, w_out_full = blocks["w_proj_conv"].reshape(-1, d), blocks["w_out"].reshape(-1, d)

    gather_gate = _gather_carried([bufs["ffn2_w_gate"]])
    o, lse = _attn_fwd(q, k, vt, seq, gather_gate)
    gather_up = _gather_carried([bufs["ffn2_w_up"]])
    x2 = _mix_fwd(x1, o, big, p["gate_bias"], p["conv_w"], w_pa, w_pc, w_out_full, seq, gather_up)
    wg2, wu2, wd2 = gather_gate.results[0], gather_up.results[0], blocks["ffn2_w_down"]
    dx3, gate2, up2, act2, loss = _ffn_fwd(x2, p["ffn2_norm"], wg2, wu2, wd2, loss_target.reshape(t, d), "ffn2_fwd")

    dx2, dg_ffn2, hb2, dgate2, dup2, dyb2 = _ffn_bwd_x(x2, p["ffn2_norm"], dx3, gate2, up2, wg2, wu2, wd2, "ffn2_bwd")
    g_ffn2 = [_tn_matmul(dgate2, hb2, "ffn2_dw_gate"), _tn_matmul(dup2, hb2, "ffn2_dw_up"), _tn_matmul(act2, dyb2, "ffn2_dw_down")]
    swap = _swap_carried(g_ffn2)
    do, delta_o, dz, dm, dbias, dw_pa, dw_pc, dw_out = _mix_bwd(dx2, o, big, p["gate_bias"], p["conv_w"], w_pa, w_pc, w_out_full, seq,
                                                                swap)
    part = _partials(GROUP_FFN2, g_ffn2, swap.results, place)
    send = _send_carried([pb for _, pb in part])
    dq, dk, dv = _attn_bwd(q, k, v, do, lse, delta_o.reshape(N_HEADS // ATTN_BWD_HEADS, ATTN_BWD_HEADS, -1), seq, send)
    join = _join_carried(_totals(GROUP_FFN2, g_ffn2, part, send.results, place))
    dp, dw_uq, dw_uk, dw_uv, dqa, dkva, dqh, dkh, dcw = _prep_bwd(
        lat, big, dz, dq, dk, dv, p["q_a_norm"], p["kv_a_norm"], p["q_head_norm"], p["k_head_norm"], w["w_uq"], w["w_uk"],
        w["w_uv"], rope, p["conv_w"], seq, join)
    grads.update(zip(GROUP_FFN2, join.results))

    gg = _global_layouts({"w_in": _dw_in(dm, dp, h2b), "w_uq": dw_uq, "w_uk": dw_uk, "w_uv": dw_uv, "w_pa": dw_pa, "w_pc": dw_pc,
                          "w_out": dw_out})
    g_mid = [_col_blocks(gg[n]) if n in COL_SHARDED else gg[n].reshape(N_CHIPS, -1, gg[n].shape[-1]) for n in GROUP_MID]
    swap = _swap_carried(g_mid)
    dx1, dg_mix = _inproj_bwd(x1, p["mix_norm"], dx2, dm, dp, w["w_in"], w["w_kr"], swap)
    part = (_partials(GROUP_MID[:1], g_mid[:1], swap.results[:1], place)
            + _chip_partial_small(g_mid[1:], swap.results[1:], place))
    send = _send_carried([pb for _, pb in part])
    grad_x, dg_ffn1, hb1, dgate1, dup1, dyb1 = _ffn_bwd_x(x_tok, p["ffn1_norm"], dx1, gate1, up1, blocks["ffn1_w_gate"],
                                                         blocks["ffn1_w_up"], blocks["ffn1_w_down"], "ffn1_bwd", send)

    small_grads = {"ffn1_norm": dg_ffn1, "mix_norm": dg_mix, "gate_bias": dbias, "q_a_norm": dqa, "kv_a_norm": dkva,
                   "q_head_norm": dqh, "k_head_norm": dkh, "ffn2_norm": dg_ffn2}
    packed = jnp.concatenate([small_grads[n] for n, _ in SMALL] + [dcw.reshape(1, -1), loss], axis=1)
    total = _sum_devices(packed.reshape(8, -1)).reshape(1, -1)
    n_small = sum(size for _, size in SMALL)
    conv_cols = conv_w.shape[1]
    conv_total = total[:, n_small:n_small + conv_rows * d].reshape(conv_rows, d)
    grads["conv_w"] = lax.dynamic_slice_in_dim(conv_total, chip * conv_cols, conv_cols, axis=1)
    loss_total = total[0, n_small + conv_rows * d]

    join = _join_carried(_totals(GROUP_MID[:1], g_mid[:1], part[:1], send.results[:1], place)
                         + _chip_total_small([pf for pf, _ in part[1:]], send.results[1:], place, [g.shape[1:] for g in g_mid[1:]]))
    g_gate = _tn_matmul(dgate1, hb1, "ffn1_dw_gate", carried=join)
    grads.update(zip(GROUP_MID, join.results))
    swap_gate = _swap_carried([g_gate])
    g_up = _tn_matmul(dup1, hb1, "ffn1_dw_up", carried=swap_gate)
    part_gate = _partials(GROUP_FFN1[:1], [g_gate], swap_gate.results, place)
    send_gate, swap_up = _send_carried([part_gate[0][1]]), _swap_carried([g_up])
    g_down = _tn_matmul(act1, dyb1, "ffn1_dw_down", carried=_both(send_gate, swap_up))
    join_gate = _join_carried(_totals(GROUP_FFN1[:1], [g_gate], part_gate, send_gate.results, place))
    part_up = _partials(GROUP_FFN1[1:2], [g_up], swap_up.results, place)
    send_up, swap_down = _send_carried([part_up[0][1]]), _swap_carried([g_down])
    adamw(GROUP_FFN2, _both(_both(send_up, swap_down), join_gate))
    grads["ffn1_w_gate"] = join_gate.results[0]
    join_up = _join_carried(_totals(GROUP_FFN1[1:2], [g_up], part_up, send_up.results, place))
    part_down = _partials(GROUP_FFN1[2:], [g_down], swap_down.results, place)
    send_down = _send_carried([part_down[0][1]])
    adamw(("w_in",), _both(send_down, join_up))
    grads["ffn1_w_up"] = join_up.results[0]
    join_down = _join_carried(_totals(GROUP_FFN1[2:], [g_down], part_down, send_down.results, place))
    adamw(GROUP_FFN1[:2], join_down)
    grads["ffn1_w_down"] = join_down.results[0]
    adamw(GROUP_FFN1[2:])
    others = GROUP_MID[1:] + ("conv_w",)
    for n, (gn, dn, mn, vn) in zip(others, _adamw_whole([(weights[n], grads[n], moments_m[n], moments_v[n]) for n in others],
                                                        "adamw_others")):
        grads[n], delta[n], new_m[n], new_v[n] = gn, dn, mn, vn

    row = lambda a: a.reshape(1, -1)
    small = _adamw_small(total, [(row(weights[n]), row(moments_m[n]), row(moments_v[n])) for n, _ in SMALL], [size for _, size in SMALL])
    for (n, _), (gn, dn, mn, vn) in zip(SMALL, small):
        grads[n], delta[n], new_m[n], new_v[n] = gn.reshape(-1), dn.reshape(-1), mn.reshape(-1), vn.reshape(-1)

    return (loss_total, grad_x.reshape(nb, seq, d), *[view(n, src[n]) for src in (grads, delta, new_m, new_v) for n in WEIGHT_ORDER])
```

```python
import functools

import jax
import jax.numpy as jnp
from jax import lax
from jax.experimental import pallas as pl
from jax.experimental.pallas import tpu as pltpu

F32 = jnp.float32
BF16 = jnp.bfloat16

D_MODEL = 1024
N_HEADS = 8
QK_NOPE = 64
QK_ROPE = 32
QK_DIM = QK_NOPE + QK_ROPE
V_DIM = 64
HEAD_PAD = 128
Q_LORA = 384
KV_LORA = 256
ROPE_THETA = 10000.0
NORM_EPS = 1e-6
ATTN_SCALE = QK_DIM ** -0.5
MASK_VALUE = -1e30
N_CHIPS = 4
N_DEV = 8

ADAM_LR = 0.001
ADAM_B1 = 0.9
ADAM_B2 = 0.999
ADAM_EPS = 1e-08
ADAM_WD = 0.01
ADAM_STEP = 10

TOKEN_TILE = 256
WIDE_TILE = 512
ATTN_TILE = 512
TN_TILE = 2048
VMEM_LIMIT = 56 * 1024 * 1024

M_COLS = 3 * D_MODEL
P_COLS = 2 * D_MODEL + Q_LORA + KV_LORA + HEAD_PAD
BIG_COLS = 5 * D_MODEL
LAT_COLS = Q_LORA + KV_LORA + HEAD_PAD

MESH_ID = pl.DeviceIdType.MESH
ANY = pl.BlockSpec(memory_space=pl.ANY)


def _params(semantics=None):
    return pltpu.CompilerParams(dimension_semantics=semantics, vmem_limit_bytes=VMEM_LIMIT)


class _Carried:
    def __init__(self, operands, out_shapes, aliases, n_sems, start, finish):
        self.operands, self.out_shapes, self.aliases, self.n_sems = list(operands), list(out_shapes), dict(aliases), n_sems
        self.start, self.finish = start, finish
        self.results = None


def _both(a, b):
    na, nao = len(a.operands), len(a.out_shapes)

    def start(ins, outs, sems, base):
        a.start(ins[:na], outs[:nao], sems, base)
        b.start(ins[na:], outs[nao:], sems, base + a.n_sems)

    def finish(ins, outs, sems, base):
        a.finish(ins[:na], outs[:nao], sems, base)
        b.finish(ins[na:], outs[nao:], sems, base + a.n_sems)

    aliases = dict(a.aliases)
    aliases.update({na + i: nao + o for i, o in b.aliases.items()})
    both = _Carried(a.operands + b.operands, a.out_shapes + b.out_shapes, aliases, a.n_sems + b.n_sems, start, finish)
    both.parts = (a, b)
    return both


def _set_results(carried, results):
    carried.results = list(results)
    if hasattr(carried, "parts"):
        a, b = carried.parts
        _set_results(a, results[:len(a.out_shapes)])
        _set_results(b, results[len(a.out_shapes):])


def _pallas(body, name, grid, in_specs, out_specs, out_shape, args, semantics, carried=None):
    if carried is None:
        return pl.pallas_call(body, name=name, grid=grid, in_specs=in_specs, out_specs=out_specs, out_shape=out_shape,
                              compiler_params=_params(semantics))(*args)
    n_in, n_out, n_ci, n_co = len(in_specs), len(out_specs), len(carried.operands), len(carried.out_shapes)

    def wrapped(*refs):
        ins, c_ins = refs[:n_in], refs[n_in:n_in + n_ci]
        outs, c_outs = refs[n_in + n_ci:n_in + n_ci + n_out], refs[n_in + n_ci + n_out:n_in + n_ci + n_out + n_co]
        sems = refs[-1]
        first = pl.program_id(0) == 0
        last = pl.program_id(0) == grid[0] - 1
        for axis in range(1, len(grid)):
            first = jnp.logical_and(first, pl.program_id(axis) == 0)
            last = jnp.logical_and(last, pl.program_id(axis) == grid[axis] - 1)

        @pl.when(first)
        def _():
            carried.start(c_ins, c_outs, sems, 0)

        body(*ins, *outs)

        @pl.when(last)
        def _():
            carried.finish(c_ins, c_outs, sems, 0)

    results = pl.pallas_call(
        wrapped, name=name, grid=grid, in_specs=list(in_specs) + [ANY] * n_ci, out_specs=list(out_specs) + [ANY] * n_co,
        out_shape=list(out_shape) + carried.out_shapes,
        input_output_aliases={n_in + i: n_out + o for i, o in carried.aliases.items()},
        scratch_shapes=[pltpu.SemaphoreType.DMA((carried.n_sems,))], compiler_params=_params(semantics))(*args, *carried.operands)
    _set_results(carried, results[n_out:])
    return results[:n_out]


def _resident(shape):
    nd = len(shape)
    return pl.BlockSpec(shape, lambda *_: (0,) * nd, pipeline_mode=pl.Buffered(1))


def _const(shape):
    nd = len(shape)
    return pl.BlockSpec(shape, lambda *_: (0,) * nd)


def _mm(a, b):
    return jnp.dot(a, b, preferred_element_type=F32)


def _mm_nt(a, b):
    return lax.dot_general(a, b, (((1,), (1,)), ((), ())), preferred_element_type=F32)


def _mm_tn(a, b):
    return lax.dot_general(a, b, (((0,), (0,)), ((), ())), preferred_element_type=F32)


def _bf(a):
    return a.astype(BF16)


def _sigmoid(a):
    return 1.0 / (1.0 + jnp.exp(-a))


def _rms(x, gain, n=None):
    n = x.shape[-1] if n is None else n
    r = lax.rsqrt(jnp.sum(x * x, axis=-1, keepdims=True) * (1.0 / n) + NORM_EPS)
    return (x * r) * gain, r


def _rms_bwd(x, r, gain, dh, n=None):
    n = x.shape[-1] if n is None else n
    u = dh * gain
    dx = r * u - x * ((r * r * r) * (jnp.sum(u * x, axis=-1, keepdims=True) * (1.0 / n)))
    dgain = jnp.sum(dh * (x * r), axis=0, keepdims=True)
    return dx, dgain


ROPE_HALF = QK_ROPE // 2


def _rope(t, rope):
    cos, s_lo, s_hi = rope
    return t * cos + pltpu.roll(t, HEAD_PAD - ROPE_HALF, 1) * s_lo + pltpu.roll(t, ROPE_HALF, 1) * s_hi


def _rope_bwd(dt, rope):
    cos, s_lo, s_hi = rope
    return dt * cos + pltpu.roll(dt * s_lo, ROPE_HALF, 1) + pltpu.roll(dt * s_hi, HEAD_PAD - ROPE_HALF, 1)


def _shift_down(u, prev8, k):
    s = pltpu.roll(u, k, 0)
    p = pltpu.roll(prev8, k, 0)
    row = lax.broadcasted_iota(jnp.int32, prev8.shape, 0)
    top = jnp.where(row < k, p, s[:8])
    return jnp.concatenate([top, s[8:]], axis=0)


def _shift_up(d, next8, k):
    tm = d.shape[0]
    s = pltpu.roll(d, tm - k, 0)
    n = pltpu.roll(next8, 8 - k, 0)
    row = lax.broadcasted_iota(jnp.int32, next8.shape, 0)
    bot = jnp.where(row >= 8 - k, n, s[tm - 8:])
    return jnp.concatenate([s[:tm - 8], bot], axis=0)


def _ffn_fwd(x, gain, wg, wu, wd, target, name, carried=None):
    t, d = x.shape
    nb, f, _ = wg.shape
    tm = TOKEN_TILE
    with_loss = target is not None

    def body(*refs):
        if with_loss:
            x_ref, g_ref, wg_ref, wu_ref, wd_ref, t_ref, out_ref, gate_ref, up_ref, act_ref, loss_ref = refs
        else:
            x_ref, g_ref, wg_ref, wu_ref, wd_ref, out_ref, gate_ref, up_ref, act_ref = refs
        xv = x_ref[...]
        h, _ = _rms(xv, g_ref[...])
        hb = _bf(h)
        y = jnp.zeros((tm, d), F32)
        nxt = (_mm_nt(hb, wg_ref[0]), _mm_nt(hb, wu_ref[0]))
        for j in range(nb):
            gate, up = nxt
            if j + 1 < nb:
                nxt = (_mm_nt(hb, wg_ref[j + 1]), _mm_nt(hb, wu_ref[j + 1]))
            act = _bf((gate * _sigmoid(gate)) * up)
            y = y + _mm(act, wd_ref[j])
            gate_ref[j] = _bf(gate)
            up_ref[j] = _bf(up)
            act_ref[j] = act
        out = xv + 0.5 * y
        if with_loss:
            err = out - t_ref[...]
            out_ref[...] = err * (1.0 / d)

            @pl.when(pl.program_id(0) == 0)
            def _():
                loss_ref[...] = jnp.zeros_like(loss_ref)

            part = jnp.sum(jnp.sum(err * err, axis=1, keepdims=True), axis=0, keepdims=True)
            loss_ref[...] += jnp.broadcast_to(part * (0.5 / d), loss_ref.shape)
        else:
            out_ref[...] = out

    tok = pl.BlockSpec((tm, d), lambda i: (i, 0))
    blk = pl.BlockSpec((nb, tm, f), lambda i: (0, i, 0))
    in_specs = [tok, _const((1, d)), _resident(wg.shape), _resident(wu.shape), _resident(wd.shape)]
    args = [x, gain, wg, wu, wd]
    out_shape = [jax.ShapeDtypeStruct((t, d), F32)] + [jax.ShapeDtypeStruct((nb, t, f), BF16)] * 3
    out_specs = [tok, blk, blk, blk]
    if with_loss:
        in_specs.append(tok)
        args.append(target)
        out_shape.append(jax.ShapeDtypeStruct((1, 128), F32))
        out_specs.append(_const((1, 128)))
    return _pallas(body, name, (t // tm,), in_specs, out_specs, out_shape, args, ("arbitrary",), carried)


def _ffn_bwd_x(x, gain, dout, gate, up, wg, wu, wd, name, carried=None):
    t, d = x.shape
    nb, f, _ = wg.shape
    tm = TOKEN_TILE

    def body(x_ref, g_ref, dout_ref, gate_ref, up_ref, wg_ref, wu_ref, wd_ref,
             dx_ref, dgain_ref, hb_ref, dgate_ref, dup_ref, dyb_ref):
        xv = x_ref[...]
        gain_v = g_ref[...]
        h, r = _rms(xv, gain_v)
        hb_ref[...] = _bf(h)
        dout_v = dout_ref[...]
        dyb = _bf(0.5 * dout_v)
        dyb_ref[...] = dyb
        dh = jnp.zeros((tm, d), F32)
        nxt = _mm_nt(dyb, wd_ref[0])
        for j in range(nb):
            dact = nxt
            if j + 1 < nb:
                nxt = _mm_nt(dyb, wd_ref[j + 1])
            gt = gate_ref[j].astype(F32)
            uv = up_ref[j].astype(F32)
            s = _sigmoid(gt)
            dup = _bf(dact * (gt * s))
            dgate = _bf((dact * uv) * (s * (1.0 + gt * (1.0 - s))))
            dh = dh + _mm(dgate, wg_ref[j]) + _mm(dup, wu_ref[j])
            dgate_ref[j] = dgate
            dup_ref[j] = dup
        dxn, dgain = _rms_bwd(xv, r, gain_v, dh)
        dx_ref[...] = dout_v + dxn

        @pl.when(pl.program_id(0) == 0)
        def _():
            dgain_ref[...] = jnp.zeros_like(dgain_ref)

        dgain_ref[...] += dgain

    tok = pl.BlockSpec((tm, d), lambda i: (i, 0))
    blk = pl.BlockSpec((nb, tm, f), lambda i: (0, i, 0))
    return _pallas(
        body, name, (t // tm,),
        [tok, _const((1, d)), tok, blk, blk, _resident(wg.shape), _resident(wu.shape), _resident(wd.shape)],
        [tok, _const((1, d)), tok, blk, blk, tok],
        [jax.ShapeDtypeStruct((t, d), F32), jax.ShapeDtypeStruct((1, d), F32), jax.ShapeDtypeStruct((t, d), BF16),
         jax.ShapeDtypeStruct((nb, t, f), BF16), jax.ShapeDtypeStruct((nb, t, f), BF16), jax.ShapeDtypeStruct((t, d), BF16)],
        (x, gain, dout, gate, up, wg, wu, wd), ("arbitrary",), carried)


def _tn_matmul(a, b, name, carried=None):
    t = a.shape[-2]
    k = a.shape[-1]
    n = b.shape[-1]
    tt = min(TN_TILE, t)
    nt = t // tt

    def body(a_ref, b_ref, o_ref):
        @pl.when(pl.program_id(1) == 0)
        def _():
            o_ref[...] = jnp.zeros_like(o_ref)

        o_ref[...] += _mm_tn(a_ref[...], b_ref[...])

    g = a.shape[0] if a.ndim == 3 else b.shape[0]
    a_spec = (pl.BlockSpec((None, tt, k), lambda gi, ti: (gi, ti, 0)) if a.ndim == 3
              else pl.BlockSpec((tt, k), lambda gi, ti: (ti, 0)))
    b_spec = (pl.BlockSpec((None, tt, n), lambda gi, ti: (gi, ti, 0)) if b.ndim == 3
              else pl.BlockSpec((tt, n), lambda gi, ti: (ti, 0)))
    o_spec = pl.BlockSpec((None, k, n), lambda gi, ti: (gi, 0, 0))
    out_shape = jax.ShapeDtypeStruct((g, k, n), F32)
    return _pallas(body, name, (g, nt), [a_spec, b_spec], [o_spec], [out_shape], (a, b), ("arbitrary", "arbitrary"), carried)[0]


def _tn_rows(a, b, out, chunks, width, rows_out, name):
    t, n = b.shape
    tt = min(TN_TILE, t)
    nt = t // tt

    def body(blocks_ref, a_ref, b_ref, *rest):
        out_ref, acc, sem = rest[-3:]
        g, ti = pl.program_id(0), pl.program_id(1)

        @pl.when(ti == 0)
        def _():
            acc[...] = jnp.zeros_like(acc)

        acc[...] += _mm_tn(a_ref[...], b_ref[...])
        for gi, (_, ranges) in enumerate(chunks):
            @pl.when(jnp.logical_and(g == gi, ti == nt - 1))
            def _(ranges=ranges):
                for row, first, count in ranges:
                    cp = pltpu.make_async_copy(acc.at[first:first + count], out_ref.at[row:row + count], sem)
                    cp.start()
                    cp.wait()

    blocks = jnp.asarray([c[0] for c in chunks], jnp.int32)
    grid_spec = pltpu.PrefetchScalarGridSpec(
        num_scalar_prefetch=1, grid=(len(chunks), nt),
        in_specs=[pl.BlockSpec((tt, width), lambda g, ti, blocks_ref: (ti, blocks_ref[g])),
                  pl.BlockSpec((tt, n), lambda g, ti, blocks_ref: (ti, 0))] + ([ANY] if out is not None else []),
        out_specs=ANY, scratch_shapes=[pltpu.VMEM((width, n), F32), pltpu.SemaphoreType.DMA])
    args = (blocks, a, b) + ((out,) if out is not None else ())
    return pl.pallas_call(body, name=name, grid_spec=grid_spec, out_shape=jax.ShapeDtypeStruct((rows_out, n), F32),
                          input_output_aliases={3: 0} if out is not None else {},
                          compiler_params=_params(("arbitrary", "arbitrary")))(*args)


ROW_QKV, ROW_KR, ROW_XC = 0, Q_LORA + KV_LORA, Q_LORA + KV_LORA + QK_ROPE
ROW_GB, ROW_GC, ROW_GL = ROW_XC + D_MODEL, ROW_XC + 2 * D_MODEL, ROW_XC + 3 * D_MODEL
BIG_FROM_ROWS = ((0, ROW_GB, D_MODEL), (D_MODEL, ROW_GL, 2 * D_MODEL), (3 * D_MODEL, ROW_XC, D_MODEL), (4 * D_MODEL, ROW_GC, D_MODEL))


def _inproj_fwd(x1, gain, w_in, w_kr, qa_gain, kva_gain, qh_gain, kh_gain, w_uq, w_uk, w_uv, w_uvt, rope, carried=None):
    t, d = x1.shape
    tm = TOKEN_TILE
    chunk = 512
    chunks = []
    for col, row, size in BIG_FROM_ROWS:
        chunks += [(col + o, row + o, chunk) for o in range(0, size, chunk)]
    of_head = [[c for k, c in enumerate(chunks) if k * N_HEADS // len(chunks) == hd] for hd in range(N_HEADS)]

    def body(x_ref, g_ref, win_ref, wkr_ref, qa_ref, kva_ref, qh_ref, kh_ref, wuq_ref, wuk_ref, wuv_ref, wuvt_ref, cos_ref, slo_ref,
             shi_ref, hb_ref, big_ref, lat_ref, q_ref, k_ref, v_ref, vt_ref):
        h, _ = _rms(x_ref[...], g_ref[...])
        hb = _bf(h)
        hb_ref[...] = hb
        k_rope = _mm_nt(hb, wkr_ref[...])
        lat = jnp.concatenate([_mm_nt(hb, win_ref[ROW_QKV:ROW_KR, :]), k_rope], axis=1)
        lat_ref[...] = lat
        cq, _ = _rms(lat[:, :Q_LORA], qa_ref[...])
        ckv, _ = _rms(lat[:, Q_LORA:Q_LORA + KV_LORA], kva_ref[...])
        cqb = _bf(cq)
        ckvb = _bf(ckv)
        rope_v = (cos_ref[...], slo_ref[...], shi_ref[...])
        q_all = _mm(cqb, wuq_ref[...])
        k_all = _mm(ckvb, wuk_ref[...])
        v_ref[...] = _bf(_mm(ckvb, wuv_ref[...]))
        vt_all = _mm_nt(wuvt_ref[...], ckvb)
        for hd in range(N_HEADS):
            for col, row, size in of_head[hd]:
                big_ref[:, col:col + size] = _mm_nt(hb, win_ref[row:row + size, :])
            lanes = slice(hd * HEAD_PAD, (hd + 1) * HEAD_PAD)
            qn, _ = _rms(q_all[:, lanes], qh_ref[...], QK_DIM)
            q_ref[hd] = _bf(_rope(qn, rope_v))
            kn, _ = _rms(k_all[:, lanes] + k_rope, kh_ref[...], QK_DIM)
            k_ref[hd] = _bf(_rope(kn, rope_v))
            vt_ref[hd] = _bf(vt_all[hd * V_DIM:(hd + 1) * V_DIM])

    tok = lambda c: pl.BlockSpec((tm, c), lambda i: (i, 0))
    head = lambda c: pl.BlockSpec((N_HEADS, tm, c), lambda i: (0, i, 0))
    return _pallas(
        body, "inproj_fwd", (t // tm,),
        [tok(d), _const((1, d)), _resident(w_in.shape), _resident(w_kr.shape), _const((1, Q_LORA)), _const((1, KV_LORA)),
         _const((1, HEAD_PAD)), _const((1, HEAD_PAD)), _resident(w_uq.shape), _resident(w_uk.shape),
         _resident(w_uv.shape), _resident(w_uvt.shape), tok(HEAD_PAD), tok(HEAD_PAD), tok(HEAD_PAD)],
        [tok(d), tok(BIG_COLS), tok(LAT_COLS), head(HEAD_PAD), head(HEAD_PAD), tok(N_HEADS * V_DIM),
         pl.BlockSpec((N_HEADS, V_DIM, tm), lambda i: (0, 0, i))],
        [jax.ShapeDtypeStruct((t, d), BF16), jax.ShapeDtypeStruct((t, BIG_COLS), F32),
         jax.ShapeDtypeStruct((t, LAT_COLS), F32), jax.ShapeDtypeStruct((N_HEADS, t, HEAD_PAD), BF16),
         jax.ShapeDtypeStruct((N_HEADS, t, HEAD_PAD), BF16), jax.ShapeDtypeStruct((t, N_HEADS * V_DIM), BF16),
         jax.ShapeDtypeStruct((N_HEADS, V_DIM, t), BF16)],
        (x1, gain, w_in, w_kr, qa_gain, kva_gain, qh_gain, kh_gain, w_uq, w_uk, w_uv, w_uvt, *rope), ("arbitrary",), carried)


EXP2_SCALE = ATTN_SCALE * 1.4426950408889634


def _diagonal_keep(tk, tq):
    return lax.broadcasted_iota(jnp.int32, (tk, tq), 0) <= lax.broadcasted_iota(jnp.int32, (tk, tq), 1)


def _attn_fwd(q, k, vt, seq, carried=None):
    _, t, _ = q.shape
    nseq = t // seq
    tq = tk = ATTN_TILE
    nq = seq // tq

    def body(q_ref, k_ref, vt_ref, o_ref, lse_ref):
        i = pl.program_id(1)
        qs = [q_ref[h] for h in range(N_HEADS)]
        keep = _diagonal_keep(tk, tq)

        def scores(h, k0):
            return _mm_nt(k_ref[h, pl.ds(k0, tk), :], qs[h])

        def update(h, st, state, k0, diagonal):
            m, l, acc = state
            if diagonal:
                st = jnp.where(keep, st, MASK_VALUE)
            m_new = jnp.maximum(m, jnp.max(st, axis=0, keepdims=True))
            pt = jnp.exp2((st - m_new) * EXP2_SCALE)
            alpha = jnp.exp2((m - m_new) * EXP2_SCALE)
            l_new = alpha * l + jnp.sum(pt, axis=0, keepdims=True)
            return m_new, l_new, alpha * acc + _mm(vt_ref[h, :, pl.ds(k0, tk)], _bf(pt))

        def tiles(states, k0, diagonal):
            st, new = scores(0, k0), []
            for h in range(N_HEADS):
                st_next = scores(h + 1, k0) if h + 1 < N_HEADS else None
                new.append(update(h, st, states[h], k0, diagonal))
                st = st_next
            return tuple(new)

        init = tuple((jnp.full((1, tq), MASK_VALUE, F32), jnp.zeros((1, tq), F32), jnp.zeros((V_DIM, tq), F32))
                     for _ in range(N_HEADS))
        states = lax.fori_loop(0, i, lambda j, s: tiles(s, pl.multiple_of(j * tk, tk), False), init)
        states = tiles(states, pl.multiple_of(i * tk, tk), True)
        outs = []
        for h in range(N_HEADS):
            m, l, acc = states[h]
            outs.append((acc / l).T)
            lse_ref[h] = m * EXP2_SCALE + jnp.log2(l)
        o_ref[...] = _bf(jnp.concatenate(outs, axis=-1))

    return _pallas(
        body, "attn_fwd", (nseq, nq),
        [pl.BlockSpec((N_HEADS, tq, HEAD_PAD), lambda b, i: (0, b * nq + i, 0)),
         pl.BlockSpec((N_HEADS, seq, HEAD_PAD), lambda b, i: (0, b, 0)),
         pl.BlockSpec((N_HEADS, V_DIM, seq), lambda b, i: (0, 0, b))],
        [pl.BlockSpec((tq, N_HEADS * V_DIM), lambda b, i: (b * nq + i, 0)),
         pl.BlockSpec((N_HEADS, 1, tq), lambda b, i: (0, 0, b * nq + i))],
        [jax.ShapeDtypeStruct((t, N_HEADS * V_DIM), BF16), jax.ShapeDtypeStruct((N_HEADS, 1, t), F32)],
        (q, k, vt), ("arbitrary", "arbitrary"), carried)


ATTN_BWD_HEADS = 4


def _attn_bwd(q, k, v, do, lse, delta, seq, carried=None):
    _, t, _ = q.shape
    nseq = t // seq
    tq = tk = ATTN_TILE
    n = seq // tq
    hb = ATTN_BWD_HEADS

    def body(q_ref, k_ref, v_ref, do_ref, lse_ref, delta_ref, dq_ref, dk_ref, dv_ref):
        dq_ref[...] = jnp.zeros_like(dq_ref)
        dk_ref[...] = jnp.zeros_like(dk_ref)
        dv_ref[...] = jnp.zeros_like(dv_ref)
        keep = _diagonal_keep(tk, tq)

        def tile(h, k0, q0, diagonal):
            kj = k_ref[h, pl.ds(k0, tk), :]
            qi = q_ref[h, pl.ds(q0, tq), :]
            doi = _bf(do_ref[pl.ds(q0, tq), h * V_DIM:(h + 1) * V_DIM])
            st = _mm_nt(kj, qi)
            if diagonal:
                st = jnp.where(keep, st, MASK_VALUE)
            pt = jnp.exp2(st * EXP2_SCALE - lse_ref[h, :, pl.ds(q0, tq)])
            dv_ref[pl.ds(k0, tk), h * V_DIM:(h + 1) * V_DIM] += _mm(_bf(pt), doi)
            dpt = _mm_nt(v_ref[pl.ds(k0, tk), h * V_DIM:(h + 1) * V_DIM], doi)
            dst = _bf((pt * (dpt - delta_ref[pl.ds(h, 1), pl.ds(q0, tq)])) * ATTN_SCALE)
            dk_ref[h, pl.ds(k0, tk), :] += _mm(dst, qi)
            dq_ref[h, pl.ds(q0, tq), :] += _mm_tn(dst, kj)

        def kv_step(j, _):
            k0 = pl.multiple_of(j * tk, tk)
            for h in range(hb):
                tile(h, k0, k0, True)

            def q_step(i, _):
                q0 = pl.multiple_of(i * tq, tq)
                for h in range(hb):
                    tile(h, k0, q0, False)
                return 0

            lax.fori_loop(j + 1, n, q_step, 0)
            return 0

        lax.fori_loop(0, n, kv_step, 0)

    hspec = lambda c: pl.BlockSpec((hb, seq, c), lambda b, g: (g, b, 0))
    cols = pl.BlockSpec((seq, hb * V_DIM), lambda b, g: (b, g))
    return _pallas(
        body, "attn_bwd", (nseq, N_HEADS // hb),
        [hspec(HEAD_PAD), hspec(HEAD_PAD), cols, cols,
         pl.BlockSpec((hb, 1, seq), lambda b, g: (g, 0, b)), pl.BlockSpec((None, hb, seq), lambda b, g: (g, 0, b))],
        [hspec(HEAD_PAD), hspec(HEAD_PAD), cols],
        [jax.ShapeDtypeStruct((N_HEADS, t, HEAD_PAD), F32), jax.ShapeDtypeStruct((N_HEADS, t, HEAD_PAD), F32),
         jax.ShapeDtypeStruct((t, N_HEADS * V_DIM), F32)],
        (q, k, v, do, lse, delta), ("arbitrary", "arbitrary"), carried)


def _merged_mixers(o_ref, gb_ref, gla_ref, glb_ref, xc_ref, gc_ref, xcp_ref, gcp_ref, bias_ref, cw_ref, wpa_ref, wpc_ref,
                   first_of_seq):
    y_a = _mm(o_ref[...], wpa_ref[...])
    gb = gb_ref[...]
    u = gc_ref[...] * xc_ref[...]
    u_prev = jnp.where(first_of_seq, 0.0, gcp_ref[...] * xcp_ref[...])
    cw = cw_ref[...]
    z = cw[2:3] * u + cw[1:2] * _shift_down(u, u_prev, 1) + cw[0:1] * _shift_down(u, u_prev, 2)
    gbz = _bf(gb * z)
    y_b = _mm(gbz, wpc_ref[...])
    bias = bias_ref[...]
    gate_a = _sigmoid(gla_ref[...] + bias[:, :D_MODEL])
    gate_b = _sigmoid(glb_ref[...] + bias[:, D_MODEL:])
    return _bf(gate_a * y_a + gate_b * y_b)


def _mixer_specs(tm, seq):
    d = D_MODEL
    tok = pl.BlockSpec((tm, d), lambda i: (i, 0))
    col = lambda c: pl.BlockSpec((tm, d), lambda i: (i, c))
    prev = lambda c: pl.BlockSpec((8, d), lambda i: (jnp.maximum(i * (tm // 8) - 1, 0), c))
    o_spec = pl.BlockSpec((tm, N_HEADS * V_DIM), lambda i: (i, 0))
    fwd_specs = [o_spec, col(0), col(1), col(2), col(3), col(4), prev(3), prev(4), _const((1, 2 * d)), _const((3, d)),
                 _resident((N_HEADS * V_DIM, d)), _resident((d, d)), _resident((d, d))]
    return tok, fwd_specs


def _mix_fwd(x1, o, big, gate_bias, conv_w, w_pa, w_pc, w_out, seq, carried=None):
    t, d = x1.shape
    tm = WIDE_TILE
    tiles_per_seq = seq // tm

    def body(x_ref, o_ref, gb_ref, gla_ref, glb_ref, xc_ref, gc_ref, xcp_ref, gcp_ref, bias_ref, cw_ref, wpa_ref, wpc_ref,
             wout_ref, x2_ref):
        first = pl.program_id(0) % tiles_per_seq == 0
        merged = _merged_mixers(o_ref, gb_ref, gla_ref, glb_ref, xc_ref, gc_ref, xcp_ref, gcp_ref, bias_ref, cw_ref, wpa_ref,
                                wpc_ref, first)
        x2_ref[...] = x_ref[...] + _mm(merged, wout_ref[...])

    tok, fwd_specs = _mixer_specs(tm, seq)
    return _pallas(body, "mix_fwd", (t // tm,), [tok] + fwd_specs, [tok], [jax.ShapeDtypeStruct((t, d), F32)],
                   (x1, o, big, big, big, big, big, big, big, gate_bias, conv_w, w_pa, w_pc, w_out), ("arbitrary",), carried)[0]


def _mix_bwd(dx2, o, big, gate_bias, conv_w, w_pa, w_pc, w_out, seq, carried=None):
    t, d = dx2.shape
    tm = TOKEN_TILE
    tiles_per_seq = seq // tm
    hv = N_HEADS * V_DIM

    def body(dx_ref, o_ref, gb_ref, gla_ref, glb_ref, xc_ref, gc_ref, xcp_ref, gcp_ref, bias_ref, cw_ref, wpa_ref, wpc_ref,
             wout_ref, do_ref, delta_ref, dz_ref, dm_ref, dbias_ref, dwpa_ref, dwpc_ref, dwout_ref):
        @pl.when(pl.program_id(0) == 0)
        def _():
            dbias_ref[...] = jnp.zeros_like(dbias_ref)
            dwpa_ref[...] = jnp.zeros_like(dwpa_ref)
            dwpc_ref[...] = jnp.zeros_like(dwpc_ref)
            dwout_ref[...] = jnp.zeros_like(dwout_ref)

        first = pl.program_id(0) % tiles_per_seq == 0
        dxb = _bf(dx_ref[...])
        dmerged = _mm_nt(dxb, wout_ref[...])
        y_a = _mm(o_ref[...], wpa_ref[...])
        bias = bias_ref[...]
        gate_a = _sigmoid(gla_ref[...] + bias[:, :d])
        gate_b = _sigmoid(glb_ref[...] + bias[:, d:])
        dya = _bf(dmerged * gate_a)
        dyb = _bf(dmerged * gate_b)
        do_v = _mm_nt(dya, wpa_ref[...])
        dgz = _mm_nt(dyb, wpc_ref[...])
        dwpa_ref[...] += _mm_tn(o_ref[...], dya)
        gb = gb_ref[...]
        u = gc_ref[...] * xc_ref[...]
        u_prev = jnp.where(first, 0.0, gcp_ref[...] * xcp_ref[...])
        cw = cw_ref[...]
        z = cw[2:3] * u + cw[1:2] * _shift_down(u, u_prev, 1) + cw[0:1] * _shift_down(u, u_prev, 2)
        gbz = _bf(gb * z)
        y_b = _mm(gbz, wpc_ref[...])
        dwpc_ref[...] += _mm_tn(gbz, dyb)
        do_ref[...] = do_v
        head = lax.broadcasted_iota(jnp.int32, (N_HEADS, hv), 0) * V_DIM
        col = lax.broadcasted_iota(jnp.int32, (N_HEADS, hv), 1)
        in_head = ((col >= head) & (col < head + V_DIM)).astype(F32)
        delta_ref[...] = lax.dot_general(in_head, do_v * o_ref[...].astype(F32), (((1,), (1,)), ((), ())),
                                         precision=lax.Precision.HIGHEST, preferred_element_type=F32)
        dz_ref[...] = dgz * gb
        dm_ref[:, :d] = _bf(dgz * z)
        merged = _bf(gate_a * y_a + gate_b * y_b)
        dwout_ref[...] += _mm_tn(merged, dxb)
        dla = (dmerged * y_a) * (gate_a * (1.0 - gate_a))
        dlb = (dmerged * y_b) * (gate_b * (1.0 - gate_b))
        dbias_ref[:, :d] += jnp.sum(dla, axis=0, keepdims=True)
        dbias_ref[:, d:] += jnp.sum(dlb, axis=0, keepdims=True)
        dm_ref[:, d:2 * d] = _bf(dla)
        dm_ref[:, 2 * d:] = _bf(dlb)

    tok, fwd_specs = _mixer_specs(tm, seq)
    return _pallas(
        body, "mix_bwd", (t // tm,), [tok] + fwd_specs,
        [pl.BlockSpec((tm, hv), lambda i: (i, 0)), pl.BlockSpec((N_HEADS, tm), lambda i: (0, i)), tok,
         pl.BlockSpec((tm, M_COLS), lambda i: (i, 0)), _const((1, 2 * d)), _const((hv, d)), _const((d, d)), _const((d, d))],
        [jax.ShapeDtypeStruct((t, hv), F32), jax.ShapeDtypeStruct((N_HEADS, t), F32), jax.ShapeDtypeStruct((t, d), F32),
         jax.ShapeDtypeStruct((t, M_COLS), BF16), jax.ShapeDtypeStruct((1, 2 * d), F32), jax.ShapeDtypeStruct((hv, d), F32),
         jax.ShapeDtypeStruct((d, d), F32), jax.ShapeDtypeStruct((d, d), F32)],
        (dx2, o, big, big, big, big, big, big, big, gate_bias, conv_w, w_pa, w_pc, w_out), ("arbitrary",), carried)


def _prep_bwd(lat, big, dz, dq, dk, dv, qa_gain, kva_gain, qh_gain, kh_gain, w_uq, w_uk, w_uv, rope, conv_w, seq, carried=None):
    t = lat.shape[0]
    d = D_MODEL
    tm = WIDE_TILE
    tiles_per_seq = seq // tm
    last_blk = t // 8 - 1

    def body(lat_ref, xc_ref, gc_ref, dz_ref, dzn_ref, dq_ref, dk_ref, dv_ref, qa_ref, kva_ref, qh_ref, kh_ref, wuq_ref, wuk_ref,
             wuv_ref, cos_ref, slo_ref, shi_ref, cw_ref,
             dp_ref, dwuq_ref, dwuk_ref, dwuv_ref, dqa_ref, dkva_ref, dqh_ref, dkh_ref, dcw_ref):
        pid = pl.program_id(0)

        @pl.when(pid == 0)
        def _():
            for r in (dwuq_ref, dwuk_ref, dwuv_ref, dqa_ref, dkva_ref, dqh_ref, dkh_ref, dcw_ref):
                r[...] = jnp.zeros_like(r)

        lat_v = lat_ref[...]
        q_lat = lat_v[:, :Q_LORA]
        kv_lat = lat_v[:, Q_LORA:Q_LORA + KV_LORA]
        k_rope = lat_v[:, Q_LORA + KV_LORA:]
        qa_gain_v = qa_ref[...]
        kva_gain_v = kva_ref[...]
        qh_gain_v = qh_ref[...]
        kh_gain_v = kh_ref[...]
        cq, rq = _rms(q_lat, qa_gain_v)
        ckv, rkv = _rms(kv_lat, kva_gain_v)
        cqb = _bf(cq)
        ckvb = _bf(ckv)
        rope_v = (cos_ref[...], slo_ref[...], shi_ref[...])
        lane = lax.broadcasted_iota(jnp.int32, (tm, HEAD_PAD), 1)
        rope_lanes = (lane >= QK_NOPE) & (lane < QK_DIM)
        dk_rope = jnp.zeros((tm, HEAD_PAD), F32)
        dqh_gain = jnp.zeros((1, HEAD_PAD), F32)
        dkh_gain = jnp.zeros((1, HEAD_PAD), F32)
        q_all = _mm(cqb, wuq_ref[...])
        k_all = _mm(ckvb, wuk_ref[...])
        dvb = _bf(dv_ref[...])
        dckv = _mm_nt(dvb, wuv_ref[...])
        dwuv_ref[...] += _mm_tn(ckvb, dvb)

        last = pid % tiles_per_seq == tiles_per_seq - 1
        dzv = dz_ref[...]
        dz_next = jnp.where(last, 0.0, dzn_ref[...])
        dz1 = _shift_up(dzv, dz_next, 1)
        dz2 = _shift_up(dzv, dz_next, 2)
        cw = cw_ref[...]
        xc = xc_ref[...]
        gc = gc_ref[...]
        u = gc * xc
        du = cw[2:3] * dzv + cw[1:2] * dz1 + cw[0:1] * dz2
        dp_ref[:, :d] = _bf(du * gc)
        dp_ref[:, d:2 * d] = _bf(du * xc)
        dcw_ref[0:1, :] += jnp.sum(dz2 * u, axis=0, keepdims=True)
        dcw_ref[1:2, :] += jnp.sum(dz1 * u, axis=0, keepdims=True)
        dcw_ref[2:3, :] += jnp.sum(dzv * u, axis=0, keepdims=True)

        dcq = jnp.zeros((tm, Q_LORA), F32)
        half = N_HEADS // 2
        for part in range(2):
            dq_heads, dk_heads = [], []
            for hd in range(part * half, (part + 1) * half):
                lanes = slice(hd * HEAD_PAD, (hd + 1) * HEAD_PAD)
                q_pre = q_all[:, lanes]
                _, rr = _rms(q_pre, qh_gain_v, QK_DIM)
                dq_pre, dg = _rms_bwd(q_pre, rr, qh_gain_v, _rope_bwd(dq_ref[hd], rope_v), QK_DIM)
                dqh_gain = dqh_gain + dg
                dq_heads.append(_bf(dq_pre))

                k_pre = k_all[:, lanes] + k_rope
                _, rr = _rms(k_pre, kh_gain_v, QK_DIM)
                dk_pre, dg = _rms_bwd(k_pre, rr, kh_gain_v, _rope_bwd(dk_ref[hd], rope_v), QK_DIM)
                dkh_gain = dkh_gain + dg
                dk_rope = dk_rope + jnp.where(rope_lanes, dk_pre, 0.0)
                dk_heads.append(_bf(dk_pre))
            dq_part = jnp.concatenate(dq_heads, axis=1)
            dk_part = jnp.concatenate(dk_heads, axis=1)
            cols = slice(part * half * HEAD_PAD, (part + 1) * half * HEAD_PAD)
            dcq = dcq + _mm_nt(dq_part, wuq_ref[:, cols])
            dckv = dckv + _mm_nt(dk_part, wuk_ref[:, cols])
            dwuq_ref[:, cols] += _mm_tn(cqb, dq_part)
            dwuk_ref[:, cols] += _mm_tn(ckvb, dk_part)
        dqh_ref[...] += dqh_gain
        dkh_ref[...] += dkh_gain
        dq_lat, dg = _rms_bwd(q_lat, rq, qa_gain_v, dcq)
        dqa_ref[...] += dg
        dkv_lat, dg = _rms_bwd(kv_lat, rkv, kva_gain_v, dckv)
        dkva_ref[...] += dg
        dp_ref[:, 2 * d:2 * d + Q_LORA] = _bf(dq_lat)
        dp_ref[:, 2 * d + Q_LORA:2 * d + Q_LORA + KV_LORA] = _bf(dkv_lat)
        dp_ref[:, 2 * d + Q_LORA + KV_LORA:] = _bf(dk_rope)

    tok = lambda c: pl.BlockSpec((tm, c), lambda i: (i, 0))
    col = lambda c: pl.BlockSpec((tm, d), lambda i: (i, c))
    head = lambda c: pl.BlockSpec((N_HEADS, tm, c), lambda i: (0, i, 0))
    nxt = pl.BlockSpec((8, d), lambda i: (jnp.minimum((i + 1) * (tm // 8), last_blk), 0))
    return _pallas(
        body, "prep_bwd", (t // tm,),
        [tok(LAT_COLS), col(3), col(4), tok(d), nxt, head(HEAD_PAD), head(HEAD_PAD), tok(N_HEADS * V_DIM),
         _const((1, Q_LORA)), _const((1, KV_LORA)), _const((1, HEAD_PAD)), _const((1, HEAD_PAD)),
         _resident(w_uq.shape), _resident(w_uk.shape), _resident(w_uv.shape), tok(HEAD_PAD), tok(HEAD_PAD), tok(HEAD_PAD),
         _const((3, d))],
        [tok(P_COLS), _const(w_uq.shape), _const(w_uk.shape), _const(w_uv.shape), _const((1, Q_LORA)),
         _const((1, KV_LORA)), _const((1, HEAD_PAD)), _const((1, HEAD_PAD)), _const((3, d))],
        [jax.ShapeDtypeStruct((t, P_COLS), BF16), jax.ShapeDtypeStruct(w_uq.shape, F32),
         jax.ShapeDtypeStruct(w_uk.shape, F32), jax.ShapeDtypeStruct(w_uv.shape, F32),
         jax.ShapeDtypeStruct((1, Q_LORA), F32), jax.ShapeDtypeStruct((1, KV_LORA), F32),
         jax.ShapeDtypeStruct((1, HEAD_PAD), F32), jax.ShapeDtypeStruct((1, HEAD_PAD), F32), jax.ShapeDtypeStruct((3, d), F32)],
        (lat, big, big, dz, dz, dq, dk, dv, qa_gain, kva_gain, qh_gain, kh_gain, w_uq, w_uk, w_uv, *rope, conv_w),
        ("arbitrary",), carried)


def _inproj_bwd(x1, gain, dx2, dm, dp, w_in, w_kr, carried=None):
    t, d = x1.shape
    tm = TOKEN_TILE

    def body(x_ref, g_ref, dx2_ref, dm_ref, dp_ref, win_ref, wkr_ref, dx1_ref, dgain_ref):
        xv = x_ref[...]
        gain_v = g_ref[...]
        _, r = _rms(xv, gain_v)
        dh = (_mm(dm_ref[:, :d], win_ref[ROW_GB:ROW_GC, :]) + _mm(dm_ref[:, d:], win_ref[ROW_GL:, :])
              + _mm(dp_ref[:, :d], win_ref[ROW_XC:ROW_GB, :]) + _mm(dp_ref[:, d:2 * d], win_ref[ROW_GC:ROW_GL, :])
              + _mm(dp_ref[:, 2 * d:2 * d + ROW_KR], win_ref[ROW_QKV:ROW_KR, :]) + _mm(dp_ref[:, 2 * d + ROW_KR:], wkr_ref[...]))
        dxn, dgain = _rms_bwd(xv, r, gain_v, dh)
        dx1_ref[...] = dx2_ref[...] + dxn

        @pl.when(pl.program_id(0) == 0)
        def _():
            dgain_ref[...] = jnp.zeros_like(dgain_ref)

        dgain_ref[...] += dgain

    tok = lambda c: pl.BlockSpec((tm, c), lambda i: (i, 0))
    return _pallas(
        body, "inproj_bwd", (t // tm,),
        [tok(d), _const((1, d)), tok(d), tok(M_COLS), tok(P_COLS), _resident(w_in.shape), _resident(w_kr.shape)],
        [tok(d), _const((1, d))], [jax.ShapeDtypeStruct((t, d), F32), jax.ShapeDtypeStruct((1, d), F32)],
        (x1, gain, dx2, dm, dp, w_in, w_kr), ("arbitrary",), carried)


def _adamw(quads, name, carried=None):
    k = len(quads)
    rows, cols = quads[0][0].shape
    tr, tc = rows, cols
    for cand in (512, 352, 256, 192, 128, 64):
        if rows % cand == 0 and rows > cand:
            tr = cand
            break
    if tr == rows and rows * cols > 512 * 1024 and cols % 256 == 0:
        tc = 256
    while k * 14 * tr * tc * 4 > (VMEM_LIMIT * 3) // 4 and tr % 16 == 0:
        tr //= 2

    def body(*refs):
        for i in range(k):
            w_ref, g_ref, m_ref, v_ref = refs[4 * i:4 * i + 4]
            delta_ref, nm_ref, nv_ref = refs[4 * k + 3 * i:4 * k + 3 * i + 3]
            delta_ref[...], nm_ref[...], nv_ref[...] = _adamw_update(w_ref[...], g_ref[...], m_ref[...], v_ref[...])

    spec = pl.BlockSpec((tr, tc), lambda i, j: (i, j))
    shape = jax.ShapeDtypeStruct((rows, cols), F32)
    outs = _pallas(body, name, (rows // tr, cols // tc), [spec] * (4 * k), [spec] * (3 * k), [shape] * (3 * k),
                   [a for quad in quads for a in quad], ("arbitrary", "arbitrary"), carried)
    return [tuple(outs[3 * i:3 * i + 3]) for i in range(k)]


def _adamw_update(w, g, m, v):
    nm = ADAM_B1 * m + (1.0 - ADAM_B1) * g
    nv = ADAM_B2 * v + (1.0 - ADAM_B2) * (g * g)
    m_hat = nm * (1.0 / (1.0 - ADAM_B1 ** ADAM_STEP))
    v_hat = nv * (1.0 / (1.0 - ADAM_B2 ** ADAM_STEP))
    return -ADAM_LR * (m_hat / (jnp.sqrt(v_hat) + ADAM_EPS) + ADAM_WD * w), nm, nv


def _adamw_whole(quads, name):
    k = len(quads)

    def body(*refs):
        for i in range(k):
            w_ref, g_ref, m_ref, v_ref = refs[4 * i:4 * i + 4]
            g_out, delta_ref, nm_ref, nv_ref = refs[4 * k + 4 * i:4 * k + 4 * i + 4]
            gv = g_ref[...]
            g_out[...] = gv
            delta_ref[...], nm_ref[...], nv_ref[...] = _adamw_update(w_ref[...], gv, m_ref[...], v_ref[...])

    vm = pl.BlockSpec(memory_space=pltpu.VMEM)
    outs = pl.pallas_call(body, name=name, in_specs=[vm] * (4 * k), out_specs=[vm] * (4 * k),
                          out_shape=[jax.ShapeDtypeStruct(q[0].shape, F32) for q in quads for _ in range(4)],
                          compiler_params=_params())(*[a for quad in quads for a in quad])
    return [tuple(outs[4 * i:4 * i + 4]) for i in range(k)]


def _adamw_small(packed_grads, triples, segments):
    k = len(triples)

    def body(*refs):
        g_ref = refs[0]
        off = 0
        for i in range(k):
            w_ref, m_ref, v_ref = refs[1 + 3 * i:4 + 3 * i]
            g_out, delta_ref, nm_ref, nv_ref = refs[1 + 3 * k + 4 * i:5 + 3 * k + 4 * i]
            gv = g_ref[:, off:off + w_ref.shape[1]]
            g_out[...] = gv
            delta_ref[...], nm_ref[...], nv_ref[...] = _adamw_update(w_ref[...], gv, m_ref[...], v_ref[...])
            off += segments[i]

    vm = pl.BlockSpec(memory_space=pltpu.VMEM)
    outs = pl.pallas_call(
        body, name="adamw_small", in_specs=[vm] * (1 + 3 * k), out_specs=[vm] * (4 * k),
        out_shape=[jax.ShapeDtypeStruct(w.shape, F32) for w, _, _ in triples for _ in range(4)],
    )(packed_grads, *[a for triple in triples for a in triple])
    return [tuple(outs[4 * i:4 * i + 4]) for i in range(k)]


def _place():
    x, y, c = lax.axis_index("x"), lax.axis_index("y"), lax.axis_index("c")
    other_chips = [(1 - x, y), (x, 1 - y), (1 - x, 1 - y)]
    return x, y, c, other_chips


def _remote(src, dst, sems, send, recv, device):
    return pltpu.make_async_remote_copy(src_ref=src, dst_ref=dst, send_sem=sems.at[send], recv_sem=sems.at[recv],
                                        device_id=device, device_id_type=MESH_ID)


def _cast_shards(shards, out_dtypes, n_first):
    n = len(shards)
    out_shape = [jax.ShapeDtypeStruct((N_CHIPS,) + s.shape, dt) for s, dt in zip(shards, out_dtypes)]
    gather = _gather_carried(out_shape[:n_first])

    def body(*refs):
        ins, outs, stage, sems = refs[:n], refs[n:2 * n], refs[2 * n:3 * n], refs[3 * n]
        x, y, _, _ = _place()
        me = 2 * x + y

        def cast(first, last):
            copies = []
            for w in range(first, last):
                stage[w][...] = ins[w][...].astype(out_dtypes[w])
                copies.append(pltpu.make_async_copy(stage[w], outs[w].at[me], sems.at[w]))
                copies[-1].start()
            for cp in copies:
                cp.wait()

        cast(0, n_first)
        gather.start(None, outs[:n_first], sems, n)
        cast(n_first, n)
        gather.finish(None, outs[:n_first], sems, n)

    vm = pl.BlockSpec(memory_space=pltpu.VMEM)
    return pl.pallas_call(
        body, name="cast_shards", in_specs=[vm] * n, out_specs=[ANY] * n, out_shape=out_shape,
        scratch_shapes=[pltpu.VMEM(s.shape, dt) for s, dt in zip(shards, out_dtypes)] + [pltpu.SemaphoreType.DMA((n + gather.n_sems,))],
        compiler_params=_params())(*shards)


BF16_ROWS = 16


def _split_rows(rows):
    return (rows // 2) % BF16_ROWS == 0


def _half_shape(rows, cols):
    return (rows // 2, cols) if _split_rows(rows) else (rows, cols // 2)


def _half(rows, cols, which):
    if _split_rows(rows):
        return (pl.ds(pl.multiple_of(which * (rows // 2), BF16_ROWS), rows // 2), slice(None))
    return (slice(None), pl.ds(pl.multiple_of(which * (cols // 2), 128), cols // 2))


def _gather_carried(bufs):
    n = len(bufs)

    def half(w, slot, which):
        _, rows, cols = bufs[w].shape
        return (slot,) + _half(rows, cols, which)

    def start(ins, outs, sems, base):
        x, y, c, other_chips = _place()
        me = 2 * x + y
        for w in range(n):
            mine = outs[w].at[half(w, me, c)]
            for p, (px, py) in enumerate(other_chips):
                _remote(mine, mine, sems, base + 12 * w + p, base + 12 * w + 3 + p, (px, py, c)).start()

    def finish(ins, outs, sems, base):
        x, y, c, other_chips = _place()
        me = 2 * x + y
        for w in range(n):
            for p, (px, py) in enumerate(other_chips):
                got = outs[w].at[half(w, 2 * px + py, c)]
                _remote(got, got, sems, base + 12 * w + p, base + 12 * w + 3 + p, (px, py, c)).wait_recv()
                _remote(got, got, sems, base + 12 * w + 6 + p, base + 12 * w + 9 + p, (x, y, 1 - c)).start()
        for w in range(n):
            mine = outs[w].at[half(w, me, c)]
            for p, (px, py) in enumerate(other_chips):
                got = outs[w].at[half(w, 2 * px + py, c)]
                theirs = outs[w].at[half(w, 2 * px + py, 1 - c)]
                _remote(got, theirs, sems, base + 12 * w + 6 + p, base + 12 * w + 9 + p, (x, y, 1 - c)).wait()
                _remote(mine, mine, sems, base + 12 * w + p, base + 12 * w + 3 + p, (px, py, c)).wait_send()

    shapes = [jax.ShapeDtypeStruct(b.shape, b.dtype) for b in bufs]
    return _Carried(bufs, shapes, {w: w for w in range(n)}, 12 * n, start, finish)


def _swap_carried(grads):
    n = len(grads)

    def copy(w, ins, outs, sems, base):
        x, y, c, _ = _place()
        _, rows, cols = grads[w].shape
        theirs = ins[w].at[(slice(None),) + _half(rows, cols, 1 - c)]
        return _remote(theirs, outs[w], sems, base + 2 * w, base + 2 * w + 1, (x, y, 1 - c))

    def start(ins, outs, sems, base):
        for w in range(n):
            copy(w, ins, outs, sems, base).start()

    def finish(ins, outs, sems, base):
        for w in range(n):
            copy(w, ins, outs, sems, base).wait()

    shapes = [jax.ShapeDtypeStruct((g.shape[0],) + _half_shape(*g.shape[1:]), F32) for g in grads]
    return _Carried(grads, shapes, {}, 2 * n, start, finish)


def _row_tile(rows):
    for cand in (512, 352, 256, 192, 128, 96, 64, 32, 16):
        if rows % cand == 0:
            return cand
    return rows


def _half_block_index(split_rows, tiles, i, core):
    return (core * tiles + i, 0) if split_rows else (i, core)


def _chip_partial(grad, other, place, name):
    nblk, hr, hc = other.shape
    by_rows = _split_rows(grad.shape[1])
    tr = _row_tile(hr)
    tiles = hr // tr

    def body(place_ref, g_ref, o_ref, own_ref, sum_bf_ref):
        s = g_ref[...] + o_ref[...]
        sum_bf_ref[...] = _bf(s)

        @pl.when(pl.program_id(1) == place_ref[0])
        def _():
            own_ref[...] = s

    grid_spec = pltpu.PrefetchScalarGridSpec(
        num_scalar_prefetch=1, grid=(tiles, nblk),
        in_specs=[pl.BlockSpec((None, tr, hc), lambda i, b, place_ref: (b,) + _half_block_index(by_rows, tiles, i, place_ref[1])),
                  pl.BlockSpec((None, tr, hc), lambda i, b, place_ref: (b, i, 0))],
        out_specs=[pl.BlockSpec((tr, hc), lambda i, b, place_ref: (i, 0)),
                   pl.BlockSpec((None, tr, hc), lambda i, b, place_ref: (b, i, 0))])
    return pl.pallas_call(body, name=name, grid_spec=grid_spec,
                          out_shape=[jax.ShapeDtypeStruct((hr, hc), F32), jax.ShapeDtypeStruct(other.shape, BF16)],
                          compiler_params=_params(("arbitrary", "arbitrary")))(place, grad, other)


def _chip_partial_small(grads, others, place):
    n = len(grads)

    def body(*refs):
        place_ref, g_refs, o_refs = refs[0], refs[1:1 + n], refs[1 + n:1 + 2 * n]
        own_refs, bf_refs = refs[1 + 2 * n:1 + 3 * n], refs[1 + 3 * n:]
        chip, core = place_ref[0], place_ref[1]
        for w in range(n):
            _, rows, cols = grads[w].shape
            half = _half(rows, cols, core)
            bf_refs[w][...] = _bf(g_refs[w][(slice(None),) + half] + o_refs[w][...])
            own_refs[w][...] = g_refs[w][(chip,) + half] + o_refs[w][chip]

    vm = pl.BlockSpec(memory_space=pltpu.VMEM)
    outs = pl.pallas_call(
        body, name="chip_partial_small", in_specs=[pl.BlockSpec(memory_space=pltpu.SMEM)] + [vm] * (2 * n), out_specs=[vm] * (2 * n),
        out_shape=[jax.ShapeDtypeStruct(o.shape[1:], F32) for o in others] + [jax.ShapeDtypeStruct(o.shape, BF16) for o in others],
        compiler_params=_params())(place, *grads, *others)
    return list(zip(outs[:n], outs[n:]))


def _chip_total_small(owns, receiveds, place, shapes):
    n = len(owns)

    def body(*refs):
        place_ref, own_refs, r_refs, out_refs = refs[0], refs[1:1 + n], refs[1 + n:1 + 2 * n], refs[1 + 2 * n:]
        chip, core = place_ref[0], place_ref[1]
        for w in range(n):
            r = [r_refs[w][(chip + k) % N_CHIPS].astype(F32) for k in (1, 2, 3)]
            out_refs[w][_half(*shapes[w], core)] = own_refs[w][...] + ((r[0] + r[1]) + r[2])

    vm = pl.BlockSpec(memory_space=pltpu.VMEM)
    return list(pl.pallas_call(
        body, name="chip_total_small", in_specs=[pl.BlockSpec(memory_space=pltpu.SMEM)] + [vm] * (2 * n), out_specs=[vm] * n,
        out_shape=[jax.ShapeDtypeStruct(tuple(s), F32) for s in shapes], compiler_params=_params())(place, *owns, *receiveds))


def _send_carried(partials):
    n = len(partials)

    def start(ins, outs, sems, base):
        x, y, c, other_chips = _place()
        me = 2 * x + y
        for w in range(n):
            for p, (px, py) in enumerate(other_chips):
                _remote(ins[w].at[2 * px + py], outs[w].at[me], sems, base + 6 * w + p, base + 6 * w + 3 + p, (px, py, c)).start()

    def finish(ins, outs, sems, base):
        x, y, c, other_chips = _place()
        for w in range(n):
            for p, (px, py) in enumerate(other_chips):
                _remote(ins[w].at[2 * px + py], outs[w].at[2 * px + py], sems, base + 6 * w + p, base + 6 * w + 3 + p,
                        (px, py, c)).wait()

    return _Carried(partials, [jax.ShapeDtypeStruct(p.shape, BF16) for p in partials], {}, 6 * n, start, finish)


def _chip_total(own, received, place, shape, name):
    hr, hc = own.shape
    by_rows = _split_rows(shape[0])
    tr = _row_tile(hr)
    tiles = hr // tr

    def body(place_ref, own_ref, r1_ref, r2_ref, r3_ref, out_ref):
        out_ref[...] = own_ref[...] + ((r1_ref[...].astype(F32) + r2_ref[...].astype(F32)) + r3_ref[...].astype(F32))

    def slot(k):
        return pl.BlockSpec((None, tr, hc), lambda i, place_ref: ((place_ref[0] + k) % N_CHIPS, i, 0))

    grid_spec = pltpu.PrefetchScalarGridSpec(
        num_scalar_prefetch=1, grid=(tiles,), in_specs=[pl.BlockSpec((tr, hc), lambda i, place_ref: (i, 0)), slot(1), slot(2), slot(3)],
        out_specs=pl.BlockSpec((tr, hc), lambda i, place_ref: _half_block_index(by_rows, tiles, i, place_ref[1])))
    return pl.pallas_call(body, name=name, grid_spec=grid_spec, out_shape=jax.ShapeDtypeStruct(tuple(shape), F32),
                          compiler_params=_params(("arbitrary",)))(place, own, received, received, received)


def _join_carried(totals):
    n = len(totals)

    def copy(w, outs, sems, base):
        x, y, c, _ = _place()
        mine = outs[w].at[_half(*totals[w].shape, c)]
        return _remote(mine, mine, sems, base + 2 * w, base + 2 * w + 1, (x, y, 1 - c))

    def start(ins, outs, sems, base):
        for w in range(n):
            copy(w, outs, sems, base).start()

    def finish(ins, outs, sems, base):
        for w in range(n):
            copy(w, outs, sems, base).wait()

    shapes = [jax.ShapeDtypeStruct(a.shape, F32) for a in totals]
    return _Carried(totals, shapes, {w: w for w in range(n)}, 2 * n, start, finish)


def _sum_devices(vec):
    rows, n = vec.shape

    def body(v_ref, out_ref, buf, send_sems, recv_sems):
        x, y, c, _ = _place()
        me = 4 * x + 2 * y + c
        buf[me] = v_ref[...]
        sends = []
        for k in range(1, N_DEV):
            peer = (1 - x if k & 4 else x, 1 - y if k & 2 else y, 1 - c if k & 1 else c)
            cp = pltpu.make_async_remote_copy(src_ref=v_ref, dst_ref=buf.at[me], send_sem=send_sems.at[k], recv_sem=recv_sems.at[k],
                                              device_id=peer, device_id_type=MESH_ID)
            cp.start()
            sends.append(cp)
        for cp in sends:
            cp.wait()
        total = buf[0]
        for dev in range(1, N_DEV):
            total = total + buf[dev]
        out_ref[...] = total

    vm = pl.BlockSpec(memory_space=pltpu.VMEM)
    return pl.pallas_call(
        body, name="sum_devices", in_specs=[vm], out_specs=vm, out_shape=jax.ShapeDtypeStruct((rows, n), F32),
        scratch_shapes=[pltpu.VMEM((N_DEV, rows, n), F32), pltpu.SemaphoreType.DMA((N_DEV,)), pltpu.SemaphoreType.DMA((N_DEV,))],
    )(vec)


def _rope_tables(positions):
    half = ROPE_HALF
    inv_freq = 1.0 / (ROPE_THETA ** (jnp.arange(half, dtype=F32) / half))
    ang = positions.astype(F32).reshape(-1, 1) * inv_freq
    cos, sin = jnp.cos(ang), jnp.sin(ang)
    t = ang.shape[0]
    ones, zeros = jnp.ones((t, QK_NOPE), F32), jnp.zeros((t, QK_NOPE), F32)
    pad, none = HEAD_PAD - QK_DIM, zeros[:, :half]
    cos_full = jnp.concatenate([ones, cos, cos, ones[:, :pad]], axis=1)
    s_lo = jnp.concatenate([zeros, -sin, none, zeros[:, :pad]], axis=1)
    s_hi = jnp.concatenate([zeros, none, sin, zeros[:, :pad]], axis=1)
    return cos_full, s_lo, s_hi


def _partials(names, grads, from_sibling, place):
    return [_chip_partial(g, o, place, "chip_partial_" + n) for n, g, o in zip(names, grads, from_sibling)]


def _totals(names, grads, partials, received, place):
    return [_chip_total(pf, r, place, g.shape[1:], "chip_total_" + n) for n, g, (pf, _), r in zip(names, grads, partials, received)]


def _kernel_layouts(full):
    w_in = full["w_in"]
    w_kr = jnp.pad(w_in[ROW_KR:ROW_XC], ((QK_NOPE, HEAD_PAD - QK_DIM), (0, 0)))
    w_uq = jnp.pad(full["w_uq"].reshape(Q_LORA, N_HEADS, QK_DIM), ((0, 0), (0, 0), (0, HEAD_PAD - QK_DIM)))
    w_uk = jnp.pad(full["w_uk"].reshape(KV_LORA, N_HEADS, QK_NOPE), ((0, 0), (0, 0), (0, HEAD_PAD - QK_NOPE)))
    return {"w_in": w_in, "w_kr": w_kr, "w_uq": w_uq.reshape(Q_LORA, N_HEADS * HEAD_PAD),
            "w_uk": w_uk.reshape(KV_LORA, N_HEADS * HEAD_PAD), "w_uv": full["w_uv"], "w_uvt": full["w_uv"].T}


def _global_layouts(g):
    w_uq = g["w_uq"].reshape(Q_LORA, N_HEADS, HEAD_PAD)[:, :, :QK_DIM].reshape(Q_LORA, N_HEADS * QK_DIM)
    w_uk = g["w_uk"].reshape(KV_LORA, N_HEADS, HEAD_PAD)[:, :, :QK_NOPE].reshape(KV_LORA, N_HEADS * QK_NOPE)
    return {"w_in": g["w_in"], "w_uq": w_uq, "w_uk": w_uk, "w_uv": g["w_uv"], "w_proj_attn": g["w_pa"], "w_proj_conv": g["w_pc"],
            "w_out": g["w_out"]}


def _dw_in(dm, dp, h2b):
    d, rows = D_MODEL, ROW_GL + 2 * D_MODEL
    wm, wp = M_COLS // 2, P_COLS // 2
    from_dm = [(0, [(ROW_GB, 0, d), (ROW_GL, d, wm - d)]), (1, [(ROW_GL + wm - d, 0, wm)])]
    from_dp = [(0, [(ROW_XC, 0, d), (ROW_GC, d, wp - d)]),
               (1, [(ROW_GC + wp - d, 0, 2 * d - wp), (ROW_QKV, 2 * d - wp, ROW_KR), (ROW_KR, 2 * d - wp + ROW_KR + QK_NOPE, QK_ROPE)])]
    out = _tn_rows(dm, h2b, None, from_dm, wm, rows, "dw_in_m")
    return _tn_rows(dp, h2b, out, from_dp, wp, rows, "dw_in_p")


def _col_blocks(a):
    r, c = a.shape
    return a.reshape(r, N_CHIPS, c // N_CHIPS).transpose(1, 0, 2)


def _from_col_blocks(a):
    n, r, c = a.shape
    return a.transpose(1, 0, 2).reshape(r, n * c)


COL_SHARDED = ("w_uq", "w_uk", "w_uv", "w_proj_attn")
TRANSPOSED = ("ffn1_w_gate", "ffn1_w_up", "ffn2_w_gate", "ffn2_w_up", "w_in")
SMALL = (("ffn1_norm", 1024), ("mix_norm", 1024), ("gate_bias", 2048), ("q_a_norm", 384), ("kv_a_norm", 256),
         ("q_head_norm", 128), ("k_head_norm", 128), ("ffn2_norm", 1024))
WEIGHT_ORDER = ("ffn1_norm", "ffn1_w_gate", "ffn1_w_up", "ffn1_w_down", "mix_norm", "w_in", "gate_bias", "q_a_norm", "w_uq",
                "kv_a_norm", "w_uk", "w_uv", "q_head_norm", "k_head_norm", "w_proj_attn", "conv_w", "w_proj_conv", "w_out",
                "ffn2_norm", "ffn2_w_gate", "ffn2_w_up", "ffn2_w_down")
MATRICES = ("ffn1_w_gate", "ffn1_w_up", "ffn1_w_down", "w_in", "w_uq", "w_uk", "w_uv", "w_proj_attn", "w_proj_conv", "w_out",
            "ffn2_w_gate", "ffn2_w_up", "ffn2_w_down")
GROUP_FFN1 = ("ffn1_w_gate", "ffn1_w_up", "ffn1_w_down")
GROUP_IN = ("w_in", "w_uq", "w_uk", "w_uv", "conv_w")
GROUP_MIX = ("w_proj_attn", "w_proj_conv", "w_out")
GROUP_FFN2 = ("ffn2_w_gate", "ffn2_w_up", "ffn2_w_down")
GROUP_MID = ("w_in", "w_uq", "w_uk", "w_uv", "w_proj_attn", "w_proj_conv", "w_out")


def _pad_lanes(a, n):
    return jnp.pad(a.reshape(1, -1), ((0, 0), (0, n - a.size)))


def kernel(x, positions, ffn1_norm, ffn1_w_gate, ffn1_w_up, ffn1_w_down, mix_norm, w_in, gate_bias, q_a_norm, w_uq, kv_a_norm, w_uk, w_uv, q_head_norm, k_head_norm, w_proj_attn, conv_w, w_proj_conv, w_out, ffn2_norm, ffn2_w_gate, ffn2_w_up, ffn2_w_down, loss_target, m_ffn1_norm, m_ffn1_w_gate, m_ffn1_w_up, m_ffn1_w_down, m_mix_norm, m_w_in, m_gate_bias, m_q_a_norm, m_w_uq, m_kv_a_norm, m_w_uk, m_w_uv, m_q_head_norm, m_k_head_norm, m_w_proj_attn, m_conv_w, m_w_proj_conv, m_w_out, m_ffn2_norm, m_ffn2_w_gate, m_ffn2_w_up, m_ffn2_w_down, v_ffn1_norm, v_ffn1_w_gate, v_ffn1_w_up, v_ffn1_w_down, v_mix_norm, v_w_in, v_gate_bias, v_q_a_norm, v_w_uq, v_kv_a_norm, v_w_uk, v_w_uv, v_q_head_norm, v_k_head_norm, v_w_proj_attn, v_conv_w, v_w_proj_conv, v_w_out, v_ffn2_norm, v_ffn2_w_gate, v_ffn2_w_up, v_ffn2_w_down):
    args = dict(locals())
    view = lambda n, a: a.T if n in TRANSPOSED else a
    weights = {n: view(n, args[n]) for n in WEIGHT_ORDER}
    moments_m = {n: view(n, args["m_" + n]) for n in WEIGHT_ORDER}
    moments_v = {n: view(n, args["v_" + n]) for n in WEIGHT_ORDER}
    nb, seq, d = x.shape
    t = nb * seq
    chip = (2 * lax.axis_index("x") + lax.axis_index("y")).astype(jnp.int32)
    place = jnp.stack([chip, lax.axis_index("c").astype(jnp.int32)])
    grads, delta, new_m, new_v = {}, {}, {}, {}

    def adamw(names, carried=None):
        results = _adamw([(weights[n], grads[n], moments_m[n], moments_v[n]) for n in names], "adamw_" + names[0], carried)
        for n, (dn, mn, vn) in zip(names, results):
            delta[n], new_m[n], new_v[n] = dn, mn, vn

    conv_rows = conv_w.shape[0]
    conv_shard = jnp.pad(conv_w, ((0, 16 - conv_rows), (0, 0)))
    assert MATRICES[:len(GROUP_FFN1)] == GROUP_FFN1
    bufs = dict(zip(MATRICES + ("conv_w",), _cast_shards([weights[n] for n in MATRICES] + [conv_shard],
                                                         [BF16] * len(MATRICES) + [F32], len(GROUP_FFN1))))
    blocks = {n: bufs[n] for n in GROUP_FFN1}
    p = {n: _pad_lanes(weights[n], size) for n, size in SMALL}
    rope = _rope_tables(positions)
    x_tok = x.reshape(t, d)

    gather_in = _gather_carried([bufs[n] for n in GROUP_IN])
    x1, gate1, up1, act1 = _ffn_fwd(x_tok, p["ffn1_norm"], blocks["ffn1_w_gate"], blocks["ffn1_w_up"], blocks["ffn1_w_down"], None,
                                    "ffn1_fwd", gather_in)
    blocks.update(zip(GROUP_IN, gather_in.results))
    w = _kernel_layouts({"w_in": blocks["w_in"].reshape(-1, d), **{n: _from_col_blocks(blocks[n]) for n in ("w_uq", "w_uk", "w_uv")}})
    p["conv_w"] = _from_col_blocks(blocks["conv_w"])[:conv_rows]

    gather_mix = _gather_carried([bufs[n] for n in GROUP_MIX + GROUP_FFN2[2:]])
    h2b, big, lat, q, k, v, vt = _inproj_fwd(x1, p["mix_norm"], w["w_in"], w["w_kr"], p["q_a_norm"], p["kv_a_norm"], p["q_head_norm"],
                                             p["k_head_norm"], w["w_uq"], w["w_uk"], w["w_uv"], w["w_uvt"], rope, gather_mix)
    blocks.update(zip(GROUP_MIX + GROUP_FFN2[2:], gather_mix.results))
    w_pa = _from_col_blocks(blocks["w_proj_attn"])
    w_pc, w_out_full = blocks["w_proj_conv"].reshape(-1, d), blocks["w_out"].reshape(-1, d)

    gather_ffn2 = _gather_carried([bufs[n] for n in GROUP_FFN2[:2]])
    o, lse = _attn_fwd(q, k, vt, seq, gather_ffn2)
    x2 = _mix_fwd(x1, o, big, p["gate_bias"], p["conv_w"], w_pa, w_pc, w_out_full, seq)
    wg2, wu2 = gather_ffn2.results
    wd2 = blocks["ffn2_w_down"]
    dx3, gate2, up2, act2, loss = _ffn_fwd(x2, p["ffn2_norm"], wg2, wu2, wd2, loss_target.reshape(t, d), "ffn2_fwd")

    dx2, dg_ffn2, hb2, dgate2, dup2, dyb2 = _ffn_bwd_x(x2, p["ffn2_norm"], dx3, gate2, up2, wg2, wu2, wd2, "ffn2_bwd")
    g_ffn2 = [_tn_matmul(dgate2, hb2, "ffn2_dw_gate"), _tn_matmul(dup2, hb2, "ffn2_dw_up"), _tn_matmul(act2, dyb2, "ffn2_dw_down")]
    swap = _swap_carried(g_ffn2)
    do, delta_o, dz, dm, dbias, dw_pa, dw_pc, dw_out = _mix_bwd(dx2, o, big, p["gate_bias"], p["conv_w"], w_pa, w_pc, w_out_full, seq,
                                                                swap)
    part = _partials(GROUP_FFN2, g_ffn2, swap.results, place)
    send = _send_carried([pb for _, pb in part])
    dq, dk, dv = _attn_bwd(q, k, v, do, lse, delta_o.reshape(N_HEADS // ATTN_BWD_HEADS, ATTN_BWD_HEADS, -1), seq, send)
    join = _join_carried(_totals(GROUP_FFN2, g_ffn2, part, send.results, place))
    dp, dw_uq, dw_uk, dw_uv, dqa, dkva, dqh, dkh, dcw = _prep_bwd(
        lat, big, dz, dq, dk, dv, p["q_a_norm"], p["kv_a_norm"], p["q_head_norm"], p["k_head_norm"], w["w_uq"], w["w_uk"],
        w["w_uv"], rope, p["conv_w"], seq, join)
    grads.update(zip(GROUP_FFN2, join.results))

    gg = _global_layouts({"w_in": _dw_in(dm, dp, h2b), "w_uq": dw_uq, "w_uk": dw_uk, "w_uv": dw_uv, "w_pa": dw_pa, "w_pc": dw_pc,
                          "w_out": dw_out})
    g_mid = [_col_blocks(gg[n]) if n in COL_SHARDED else gg[n].reshape(N_CHIPS, -1, gg[n].shape[-1]) for n in GROUP_MID]
    swap = _swap_carried(g_mid)
    dx1, dg_mix = _inproj_bwd(x1, p["mix_norm"], dx2, dm, dp, w["w_in"], w["w_kr"], swap)
    part = (_partials(GROUP_MID[:1], g_mid[:1], swap.results[:1], place)
            + _chip_partial_small(g_mid[1:], swap.results[1:], place))
    send = _send_carried([pb for _, pb in part])
    grad_x, dg_ffn1, hb1, dgate1, dup1, dyb1 = _ffn_bwd_x(x_tok, p["ffn1_norm"], dx1, gate1, up1, blocks["ffn1_w_gate"],
                                                         blocks["ffn1_w_up"], blocks["ffn1_w_down"], "ffn1_bwd", send)

    small_grads = {"ffn1_norm": dg_ffn1, "mix_norm": dg_mix, "gate_bias": dbias, "q_a_norm": dqa, "kv_a_norm": dkva,
                   "q_head_norm": dqh, "k_head_norm": dkh, "ffn2_norm": dg_ffn2}
    packed = jnp.concatenate([small_grads[n] for n, _ in SMALL] + [dcw.reshape(1, -1), loss], axis=1)
    total = _sum_devices(packed.reshape(8, -1)).reshape(1, -1)
    n_small = sum(size for _, size in SMALL)
    conv_cols = conv_w.shape[1]
    conv_total = total[:, n_small:n_small + conv_rows * d].reshape(conv_rows, d)
    grads["conv_w"] = lax.dynamic_slice_in_dim(conv_total, chip * conv_cols, conv_cols, axis=1)
    loss_total = total[0, n_small + conv_rows * d]

    join = _join_carried(_totals(GROUP_MID[:1], g_mid[:1], part[:1], send.results[:1], place)
                         + _chip_total_small([pf for pf, _ in part[1:]], send.results[1:], place, [g.shape[1:] for g in g_mid[1:]]))
    g_gate = _tn_matmul(dgate1, hb1, "ffn1_dw_gate", carried=join)
    grads.update(zip(GROUP_MID, join.results))
    swap_gate = _swap_carried([g_gate])
    g_up = _tn_matmul(dup1, hb1, "ffn1_dw_up", carried=swap_gate)
    part_gate = _partials(GROUP_FFN1[:1], [g_gate], swap_gate.results, place)
    send_gate, swap_up = _send_carried([part_gate[0][1]]), _swap_carried([g_up])
    g_down = _tn_matmul(act1, dyb1, "ffn1_dw_down", carried=_both(send_gate, swap_up))
    join_gate = _join_carried(_totals(GROUP_FFN1[:1], [g_gate], part_gate, send_gate.results, place))
    part_up = _partials(GROUP_FFN1[1:2], [g_up], swap_up.results, place)
    send_up, swap_down = _send_carried([part_up[0][1]]), _swap_carried([g_down])
    adamw(GROUP_FFN2, _both(_both(send_up, swap_down), join_gate))
    grads["ffn1_w_gate"] = join_gate.results[0]
    join_up = _join_carried(_totals(GROUP_FFN1[1:2], [g_up], part_up, send_up.results, place))
    part_down = _partials(GROUP_FFN1[2:], [g_down], swap_down.results, place)
    send_down = _send_carried([part_down[0][1]])
    adamw(("w_in",), _both(send_down, join_up))
    grads["ffn1_w_up"] = join_up.results[0]
    join_down = _join_carried(_totals(GROUP_FFN1[2:], [g_down], part_down, send_down.results, place))
    adamw(GROUP_FFN1[:2], join_down)
    grads["ffn1_w_down"] = join_down.results[0]
    adamw(GROUP_FFN1[2:])
    others = GROUP_MID[1:] + ("conv_w",)
    for n, (gn, dn, mn, vn) in zip(others, _adamw_whole([(weights[n], grads[n], moments_m[n], moments_v[n]) for n in others],
                                                        "adamw_others")):
        grads[n], delta[n], new_m[n], new_v[n] = gn, dn, mn, vn

    row = lambda a: a.reshape(1, -1)
    small = _adamw_small(total, [(row(weights[n]), row(moments_m[n]), row(moments_v[n])) for n, _ in SMALL], [size for _, size in SMALL])
    for (n, _), (gn, dn, mn, vn) in zip(SMALL, small):
        grads[n], delta[n], new_m[n], new_v[n] = gn.reshape(-1), dn.reshape(-1), mn.reshape(-1), vn.reshape(-1)

    return (loss_total, grad_x.reshape(nb, seq, d), *[view(n, src[n]) for src in (grads, delta, new_m, new_v) for n in WEIGHT_ORDER])
```

```python
import functools

import jax
import jax.numpy as jnp
from jax import lax
from jax.experimental import pallas as pl
from jax.experimental.pallas import tpu as pltpu

F32 = jnp.float32
BF16 = jnp.bfloat16

D_MODEL = 1024
N_HEADS = 8
QK_NOPE = 64
QK_ROPE = 32
QK_DIM = QK_NOPE + QK_ROPE
V_DIM = 64
HEAD_PAD = 128
Q_LORA = 384
KV_LORA = 256
ROPE_THETA = 10000.0
NORM_EPS = 1e-6
ATTN_SCALE = QK_DIM ** -0.5
MASK_VALUE = -1e30
N_CHIPS = 4
N_DEV = 8

ADAM_LR = 0.001
ADAM_B1 = 0.9
ADAM_B2 = 0.999
ADAM_EPS = 1e-08
ADAM_WD = 0.01
ADAM_STEP = 10

TOKEN_TILE = 256
WIDE_TILE = 512
ATTN_TILE = 512
TN_TILE = 2048
VMEM_LIMIT = 56 * 1024 * 1024

M_COLS = 3 * D_MODEL
P_COLS = 2 * D_MODEL + Q_LORA + KV_LORA + HEAD_PAD
BIG_COLS = 5 * D_MODEL
LAT_COLS = Q_LORA + KV_LORA + HEAD_PAD

MESH_ID = pl.DeviceIdType.MESH
ANY = pl.BlockSpec(memory_space=pl.ANY)


def _params(semantics=None):
    return pltpu.CompilerParams(dimension_semantics=semantics, vmem_limit_bytes=VMEM_LIMIT)


class _Carried:
    def __init__(self, operands, out_shapes, aliases, n_sems, start, finish):
        self.operands, self.out_shapes, self.aliases, self.n_sems = list(operands), list(out_shapes), dict(aliases), n_sems
        self.start, self.finish = start, finish
        self.results = None


def _both(a, b):
    na, nao = len(a.operands), len(a.out_shapes)

    def start(ins, outs, sems, base):
        a.start(ins[:na], outs[:nao], sems, base)
        b.start(ins[na:], outs[nao:], sems, base + a.n_sems)

    def finish(ins, outs, sems, base):
        a.finish(ins[:na], outs[:nao], sems, base)
        b.finish(ins[na:], outs[nao:], sems, base + a.n_sems)

    aliases = dict(a.aliases)
    aliases.update({na + i: nao + o for i, o in b.aliases.items()})
    both = _Carried(a.operands + b.operands, a.out_shapes + b.out_shapes, aliases, a.n_sems + b.n_sems, start, finish)
    both.parts = (a, b)
    return both


def _set_results(carried, results):
    carried.results = list(results)
    if hasattr(carried, "parts"):
        a, b = carried.parts
        _set_results(a, results[:len(a.out_shapes)])
        _set_results(b, results[len(a.out_shapes):])


def _pallas(body, name, grid, in_specs, out_specs, out_shape, args, semantics, carried=None):
    if carried is None:
        return pl.pallas_call(body, name=name, grid=grid, in_specs=in_specs, out_specs=out_specs, out_shape=out_shape,
                              compiler_params=_params(semantics))(*args)
    n_in, n_out, n_ci, n_co = len(in_specs), len(out_specs), len(carried.operands), len(carried.out_shapes)

    def wrapped(*refs):
        ins, c_ins = refs[:n_in], refs[n_in:n_in + n_ci]
        outs, c_outs = refs[n_in + n_ci:n_in + n_ci + n_out], refs[n_in + n_ci + n_out:n_in + n_ci + n_out + n_co]
        sems = refs[-1]
        first = pl.program_id(0) == 0
        last = pl.program_id(0) == grid[0] - 1
        for axis in range(1, len(grid)):
            first = jnp.logical_and(first, pl.program_id(axis) == 0)
            last = jnp.logical_and(last, pl.program_id(axis) == grid[axis] - 1)

        @pl.when(first)
        def _():
            carried.start(c_ins, c_outs, sems, 0)

        body(*ins, *outs)

        @pl.when(last)
        def _():
            carried.finish(c_ins, c_outs, sems, 0)

    results = pl.pallas_call(
        wrapped, name=name, grid=grid, in_specs=list(in_specs) + [ANY] * n_ci, out_specs=list(out_specs) + [ANY] * n_co,
        out_shape=list(out_shape) + carried.out_shapes,
        input_output_aliases={n_in + i: n_out + o for i, o in carried.aliases.items()},
        scratch_shapes=[pltpu.SemaphoreType.DMA((carried.n_sems,))], compiler_params=_params(semantics))(*args, *carried.operands)
    _set_results(carried, results[n_out:])
    return results[:n_out]


def _resident(shape):
    nd = len(shape)
    return pl.BlockSpec(shape, lambda *_: (0,) * nd, pipeline_mode=pl.Buffered(1))


def _const(shape):
    nd = len(shape)
    return pl.BlockSpec(shape, lambda *_: (0,) * nd)


def _mm(a, b):
    return jnp.dot(a, b, preferred_element_type=F32)


def _mm_nt(a, b):
    return lax.dot_general(a, b, (((1,), (1,)), ((), ())), preferred_element_type=F32)


def _mm_tn(a, b):
    return lax.dot_general(a, b, (((0,), (0,)), ((), ())), preferred_element_type=F32)


def _bf(a):
    return a.astype(BF16)


def _sigmoid(a):
    return 1.0 / (1.0 + jnp.exp(-a))


def _rms(x, gain, n=None):
    n = x.shape[-1] if n is None else n
    r = lax.rsqrt(jnp.sum(x * x, axis=-1, keepdims=True) * (1.0 / n) + NORM_EPS)
    return (x * r) * gain, r


def _rms_bwd(x, r, gain, dh, n=None):
    n = x.shape[-1] if n is None else n
    u = dh * gain
    dx = r * u - x * ((r * r * r) * (jnp.sum(u * x, axis=-1, keepdims=True) * (1.0 / n)))
    dgain = jnp.sum(dh * (x * r), axis=0, keepdims=True)
    return dx, dgain


ROPE_HALF = QK_ROPE // 2


def _rope(t, rope):
    cos, s_lo, s_hi = rope
    return t * cos + pltpu.roll(t, HEAD_PAD - ROPE_HALF, 1) * s_lo + pltpu.roll(t, ROPE_HALF, 1) * s_hi


def _rope_bwd(dt, rope):
    cos, s_lo, s_hi = rope
    return dt * cos + pltpu.roll(dt * s_lo, ROPE_HALF, 1) + pltpu.roll(dt * s_hi, HEAD_PAD - ROPE_HALF, 1)


def _shift_down(u, prev8, k):
    s = pltpu.roll(u, k, 0)
    p = pltpu.roll(prev8, k, 0)
    row = lax.broadcasted_iota(jnp.int32, prev8.shape, 0)
    top = jnp.where(row < k, p, s[:8])
    return jnp.concatenate([top, s[8:]], axis=0)


def _shift_up(d, next8, k):
    tm = d.shape[0]
    s = pltpu.roll(d, tm - k, 0)
    n = pltpu.roll(next8, 8 - k, 0)
    row = lax.broadcasted_iota(jnp.int32, next8.shape, 0)
    bot = jnp.where(row >= 8 - k, n, s[tm - 8:])
    return jnp.concatenate([s[:tm - 8], bot], axis=0)


def _ffn_fwd(x, gain, wg, wu, wd, target, name, carried=None):
    t, d = x.shape
    nb, f, _ = wg.shape
    tm = TOKEN_TILE
    with_loss = target is not None

    def body(*refs):
        if with_loss:
            x_ref, g_ref, wg_ref, wu_ref, wd_ref, t_ref, out_ref, gate_ref, up_ref, act_ref, loss_ref = refs
        else:
            x_ref, g_ref, wg_ref, wu_ref, wd_ref, out_ref, gate_ref, up_ref, act_ref = refs
        xv = x_ref[...]
        h, _ = _rms(xv, g_ref[...])
        hb = _bf(h)
        y = jnp.zeros((tm, d), F32)
        nxt = (_mm_nt(hb, wg_ref[0]), _mm_nt(hb, wu_ref[0]))
        for j in range(nb):
            gate, up = nxt
            if j + 1 < nb:
                nxt = (_mm_nt(hb, wg_ref[j + 1]), _mm_nt(hb, wu_ref[j + 1]))
            act = _bf((gate * _sigmoid(gate)) * up)
            y = y + _mm(act, wd_ref[j])
            gate_ref[j] = _bf(gate)
            up_ref[j] = _bf(up)
            act_ref[j] = act
        out = xv + 0.5 * y
        if with_loss:
            err = out - t_ref[...]
            out_ref[...] = err * (1.0 / d)

            @pl.when(pl.program_id(0) == 0)
            def _():
                loss_ref[...] = jnp.zeros_like(loss_ref)

            part = jnp.sum(jnp.sum(err * err, axis=1, keepdims=True), axis=0, keepdims=True)
            loss_ref[...] += jnp.broadcast_to(part * (0.5 / d), loss_ref.shape)
        else:
            out_ref[...] = out

    tok = pl.BlockSpec((tm, d), lambda i: (i, 0))
    blk = pl.BlockSpec((nb, tm, f), lambda i: (0, i, 0))
    in_specs = [tok, _const((1, d)), _resident(wg.shape), _resident(wu.shape), _resident(wd.shape)]
    args = [x, gain, wg, wu, wd]
    out_shape = [jax.ShapeDtypeStruct((t, d), F32)] + [jax.ShapeDtypeStruct((nb, t, f), BF16)] * 3
    out_specs = [tok, blk, blk, blk]
    if with_loss:
        in_specs.append(tok)
        args.append(target)
        out_shape.append(jax.ShapeDtypeStruct((1, 128), F32))
        out_specs.append(_const((1, 128)))
    return _pallas(body, name, (t // tm,), in_specs, out_specs, out_shape, args, ("arbitrary",), carried)


def _ffn_bwd_x(x, gain, dout, gate, up, wg, wu, wd, name, carried=None):
    t, d = x.shape
    nb, f, _ = wg.shape
    tm = TOKEN_TILE

    def body(x_ref, g_ref, dout_ref, gate_ref, up_ref, wg_ref, wu_ref, wd_ref,
             dx_ref, dgain_ref, hb_ref, dgate_ref, dup_ref, dyb_ref):
        xv = x_ref[...]
        gain_v = g_ref[...]
        h, r = _rms(xv, gain_v)
        hb_ref[...] = _bf(h)
        dout_v = dout_ref[...]
        dyb = _bf(0.5 * dout_v)
        dyb_ref[...] = dyb
        dh = jnp.zeros((tm, d), F32)
        nxt = _mm_nt(dyb, wd_ref[0])
        for j in range(nb):
            dact = nxt
            if j + 1 < nb:
                nxt = _mm_nt(dyb, wd_ref[j + 1])
            gt = gate_ref[j].astype(F32)
            uv = up_ref[j].astype(F32)
            s = _sigmoid(gt)
            dup = _bf(dact * (gt * s))
            dgate = _bf((dact * uv) * (s * (1.0 + gt * (1.0 - s))))
            dh = dh + _mm(dgate, wg_ref[j]) + _mm(dup, wu_ref[j])
            dgate_ref[j] = dgate
            dup_ref[j] = dup
        dxn, dgain = _rms_bwd(xv, r, gain_v, dh)
        dx_ref[...] = dout_v + dxn

        @pl.when(pl.program_id(0) == 0)
        def _():
            dgain_ref[...] = jnp.zeros_like(dgain_ref)

        dgain_ref[...] += dgain

    tok = pl.BlockSpec((tm, d), lambda i: (i, 0))
    blk = pl.BlockSpec((nb, tm, f), lambda i: (0, i, 0))
    return _pallas(
        body, name, (t // tm,),
        [tok, _const((1, d)), tok, blk, blk, _resident(wg.shape), _resident(wu.shape), _resident(wd.shape)],
        [tok, _const((1, d)), tok, blk, blk, tok],
        [jax.ShapeDtypeStruct((t, d), F32), jax.ShapeDtypeStruct((1, d), F32), jax.ShapeDtypeStruct((t, d), BF16),
         jax.ShapeDtypeStruct((nb, t, f), BF16), jax.ShapeDtypeStruct((nb, t, f), BF16), jax.ShapeDtypeStruct((t, d), BF16)],
        (x, gain, dout, gate, up, wg, wu, wd), ("arbitrary",), carried)


def _tn_matmul(a, b, name, carried=None):
    t = a.shape[-2]
    k = a.shape[-1]
    n = b.shape[-1]
    tt = min(2 * TN_TILE, t)
    nt = t // tt

    def body(a_ref, b_ref, o_ref):
        @pl.when(pl.program_id(1) == 0)
        def _():
            o_ref[...] = jnp.zeros_like(o_ref)

        o_ref[...] += _mm_tn(a_ref[...], b_ref[...])

    g = a.shape[0] if a.ndim == 3 else b.shape[0]
    a_spec = (pl.BlockSpec((None, tt, k), lambda gi, ti: (gi, ti, 0)) if a.ndim == 3
              else pl.BlockSpec((tt, k), lambda gi, ti: (ti, 0)))
    b_spec = (pl.BlockSpec((None, tt, n), lambda gi, ti: (gi, ti, 0)) if b.ndim == 3
              else pl.BlockSpec((tt, n), lambda gi, ti: (ti, 0)))
    o_spec = pl.BlockSpec((None, k, n), lambda gi, ti: (gi, 0, 0))
    out_shape = jax.ShapeDtypeStruct((g, k, n), F32)
    return _pallas(body, name, (g, nt), [a_spec, b_spec], [o_spec], [out_shape], (a, b), ("arbitrary", "arbitrary"), carried)[0]


def _tn_rows(a, b, out, chunks, width, rows_out, name):
    t, n = b.shape
    tt = min(TN_TILE, t)
    nt = t // tt

    def body(blocks_ref, a_ref, b_ref, *rest):
        out_ref, acc, sem = rest[-3:]
        g, ti = pl.program_id(0), pl.program_id(1)

        @pl.when(ti == 0)
        def _():
            acc[...] = jnp.zeros_like(acc)

        acc[...] += _mm_tn(a_ref[...], b_ref[...])
        for gi, (_, ranges) in enumerate(chunks):
            @pl.when(jnp.logical_and(g == gi, ti == nt - 1))
            def _(ranges=ranges):
                for row, first, count in ranges:
                    cp = pltpu.make_async_copy(acc.at[first:first + count], out_ref.at[row:row + count], sem)
                    cp.start()
                    cp.wait()

    blocks = jnp.asarray([c[0] for c in chunks], jnp.int32)
    grid_spec = pltpu.PrefetchScalarGridSpec(
        num_scalar_prefetch=1, grid=(len(chunks), nt),
        in_specs=[pl.BlockSpec((tt, width), lambda g, ti, blocks_ref: (ti, blocks_ref[g])),
                  pl.BlockSpec((tt, n), lambda g, ti, blocks_ref: (ti, 0))] + ([ANY] if out is not None else []),
        out_specs=ANY, scratch_shapes=[pltpu.VMEM((width, n), F32), pltpu.SemaphoreType.DMA])
    args = (blocks, a, b) + ((out,) if out is not None else ())
    return pl.pallas_call(body, name=name, grid_spec=grid_spec, out_shape=jax.ShapeDtypeStruct((rows_out, n), F32),
                          input_output_aliases={3: 0} if out is not None else {},
                          compiler_params=_params(("arbitrary", "arbitrary")))(*args)


ROW_QKV, ROW_KR, ROW_XC = 0, Q_LORA + KV_LORA, Q_LORA + KV_LORA + QK_ROPE
ROW_GB, ROW_GC, ROW_GL = ROW_XC + D_MODEL, ROW_XC + 2 * D_MODEL, ROW_XC + 3 * D_MODEL
BIG_FROM_ROWS = ((0, ROW_GB, D_MODEL), (D_MODEL, ROW_GL, 2 * D_MODEL), (3 * D_MODEL, ROW_XC, D_MODEL), (4 * D_MODEL, ROW_GC, D_MODEL))


def _inproj_fwd(x1, gain, w_in, w_kr, qa_gain, kva_gain, qh_gain, kh_gain, w_uq, w_uk, w_uv, w_uvt, rope, carried=None):
    t, d = x1.shape
    tm = TOKEN_TILE
    chunk = 512
    chunks = []
    for col, row, size in BIG_FROM_ROWS:
        chunks += [(col + o, row + o, chunk) for o in range(0, size, chunk)]
    of_head = [[c for k, c in enumerate(chunks) if k * N_HEADS // len(chunks) == hd] for hd in range(N_HEADS)]

    def body(x_ref, g_ref, win_ref, wkr_ref, qa_ref, kva_ref, qh_ref, kh_ref, wuq_ref, wuk_ref, wuv_ref, wuvt_ref, cos_ref, slo_ref,
             shi_ref, hb_ref, big_ref, lat_ref, q_ref, k_ref, v_ref, vt_ref):
        h, _ = _rms(x_ref[...], g_ref[...])
        hb = _bf(h)
        hb_ref[...] = hb
        k_rope = _mm_nt(hb, wkr_ref[...])
        lat = jnp.concatenate([_mm_nt(hb, win_ref[ROW_QKV:ROW_KR, :]), k_rope], axis=1)
        lat_ref[...] = lat
        cq, _ = _rms(lat[:, :Q_LORA], qa_ref[...])
        ckv, _ = _rms(lat[:, Q_LORA:Q_LORA + KV_LORA], kva_ref[...])
        cqb = _bf(cq)
        ckvb = _bf(ckv)
        rope_v = (cos_ref[...], slo_ref[...], shi_ref[...])
        q_all = _mm(cqb, wuq_ref[...])
        k_all = _mm(ckvb, wuk_ref[...])
        v_ref[...] = _bf(_mm(ckvb, wuv_ref[...]))
        vt_all = _mm_nt(wuvt_ref[...], ckvb)
        for hd in range(N_HEADS):
            for col, row, size in of_head[hd]:
                big_ref[:, col:col + size] = _mm_nt(hb, win_ref[row:row + size, :])
            lanes = slice(hd * HEAD_PAD, (hd + 1) * HEAD_PAD)
            qn, _ = _rms(q_all[:, lanes], qh_ref[...], QK_DIM)
            q_ref[hd] = _bf(_rope(qn, rope_v))
            kn, _ = _rms(k_all[:, lanes] + k_rope, kh_ref[...], QK_DIM)
            k_ref[hd] = _bf(_rope(kn, rope_v))
            vt_ref[hd] = _bf(vt_all[hd * V_DIM:(hd + 1) * V_DIM])

    tok = lambda c: pl.BlockSpec((tm, c), lambda i: (i, 0))
    head = lambda c: pl.BlockSpec((N_HEADS, tm, c), lambda i: (0, i, 0))
    return _pallas(
        body, "inproj_fwd", (t // tm,),
        [tok(d), _const((1, d)), _resident(w_in.shape), _resident(w_kr.shape), _const((1, Q_LORA)), _const((1, KV_LORA)),
         _const((1, HEAD_PAD)), _const((1, HEAD_PAD)), _resident(w_uq.shape), _resident(w_uk.shape),
         _resident(w_uv.shape), _resident(w_uvt.shape), tok(HEAD_PAD), tok(HEAD_PAD), tok(HEAD_PAD)],
        [tok(d), tok(BIG_COLS), tok(LAT_COLS), head(HEAD_PAD), head(HEAD_PAD), tok(N_HEADS * V_DIM),
         pl.BlockSpec((N_HEADS, V_DIM, tm), lambda i: (0, 0, i))],
        [jax.ShapeDtypeStruct((t, d), BF16), jax.ShapeDtypeStruct((t, BIG_COLS), F32),
         jax.ShapeDtypeStruct((t, LAT_COLS), F32), jax.ShapeDtypeStruct((N_HEADS, t, HEAD_PAD), BF16),
         jax.ShapeDtypeStruct((N_HEADS, t, HEAD_PAD), BF16), jax.ShapeDtypeStruct((t, N_HEADS * V_DIM), BF16),
         jax.ShapeDtypeStruct((N_HEADS, V_DIM, t), BF16)],
        (x1, gain, w_in, w_kr, qa_gain, kva_gain, qh_gain, kh_gain, w_uq, w_uk, w_uv, w_uvt, *rope), ("arbitrary",), carried)


EXP2_SCALE = ATTN_SCALE * 1.4426950408889634


def _diagonal_keep(tk, tq):
    return lax.broadcasted_iota(jnp.int32, (tk, tq), 0) <= lax.broadcasted_iota(jnp.int32, (tk, tq), 1)


def _attn_fwd(q, k, vt, seq, carried=None):
    _, t, _ = q.shape
    nseq = t // seq
    tq = tk = ATTN_TILE
    nq = seq // tq

    def body(q_ref, k_ref, vt_ref, o_ref, lse_ref):
        i = pl.program_id(1)
        qs = [q_ref[h] for h in range(N_HEADS)]
        keep = _diagonal_keep(tk, tq)

        def scores(h, k0):
            return _mm_nt(k_ref[h, pl.ds(k0, tk), :], qs[h])

        def update(h, st, state, k0, diagonal):
            m, l, acc = state
            if diagonal:
                st = jnp.where(keep, st, MASK_VALUE)
            m_new = jnp.maximum(m, jnp.max(st, axis=0, keepdims=True))
            pt = jnp.exp2((st - m_new) * EXP2_SCALE)
            alpha = jnp.exp2((m - m_new) * EXP2_SCALE)
            l_new = alpha * l + jnp.sum(pt, axis=0, keepdims=True)
            return m_new, l_new, alpha * acc + _mm(vt_ref[h, :, pl.ds(k0, tk)], _bf(pt))

        def tiles(states, k0, diagonal):
            st, new = scores(0, k0), []
            for h in range(N_HEADS):
                st_next = scores(h + 1, k0) if h + 1 < N_HEADS else None
                new.append(update(h, st, states[h], k0, diagonal))
                st = st_next
            return tuple(new)

        init = tuple((jnp.full((1, tq), MASK_VALUE, F32), jnp.zeros((1, tq), F32), jnp.zeros((V_DIM, tq), F32))
                     for _ in range(N_HEADS))
        states = lax.fori_loop(0, i, lambda j, s: tiles(s, pl.multiple_of(j * tk, tk), False), init)
        states = tiles(states, pl.multiple_of(i * tk, tk), True)
        outs = []
        for h in range(N_HEADS):
            m, l, acc = states[h]
            outs.append((acc / l).T)
            lse_ref[h] = m * EXP2_SCALE + jnp.log2(l)
        o_ref[...] = _bf(jnp.concatenate(outs, axis=-1))

    return _pallas(
        body, "attn_fwd", (nseq, nq),
        [pl.BlockSpec((N_HEADS, tq, HEAD_PAD), lambda b, i: (0, b * nq + i, 0)),
         pl.BlockSpec((N_HEADS, seq, HEAD_PAD), lambda b, i: (0, b, 0)),
         pl.BlockSpec((N_HEADS, V_DIM, seq), lambda b, i: (0, 0, b))],
        [pl.BlockSpec((tq, N_HEADS * V_DIM), lambda b, i: (b * nq + i, 0)),
         pl.BlockSpec((N_HEADS, 1, tq), lambda b, i: (0, 0, b * nq + i))],
        [jax.ShapeDtypeStruct((t, N_HEADS * V_DIM), BF16), jax.ShapeDtypeStruct((N_HEADS, 1, t), F32)],
        (q, k, vt), ("arbitrary", "arbitrary"), carried)


ATTN_BWD_HEADS = 4


def _attn_bwd(q, k, v, do, lse, delta, seq, carried=None):
    _, t, _ = q.shape
    nseq = t // seq
    tq = tk = ATTN_TILE
    n = seq // tq
    hb = ATTN_BWD_HEADS

    def body(q_ref, k_ref, v_ref, do_ref, lse_ref, delta_ref, dq_ref, dk_ref, dv_ref):
        dq_ref[...] = jnp.zeros_like(dq_ref)
        dk_ref[...] = jnp.zeros_like(dk_ref)
        dv_ref[...] = jnp.zeros_like(dv_ref)
        keep = _diagonal_keep(tk, tq)

        def tile(h, k0, q0, diagonal):
            kj = k_ref[h, pl.ds(k0, tk), :]
            qi = q_ref[h, pl.ds(q0, tq), :]
            doi = _bf(do_ref[pl.ds(q0, tq), h * V_DIM:(h + 1) * V_DIM])
            st = _mm_nt(kj, qi)
            if diagonal:
                st = jnp.where(keep, st, MASK_VALUE)
            pt = jnp.exp2(st * EXP2_SCALE - lse_ref[h, :, pl.ds(q0, tq)])
            dv_ref[pl.ds(k0, tk), h * V_DIM:(h + 1) * V_DIM] += _mm(_bf(pt), doi)
            dpt = _mm_nt(v_ref[pl.ds(k0, tk), h * V_DIM:(h + 1) * V_DIM], doi)
            dst = _bf((pt * (dpt - delta_ref[pl.ds(h, 1), pl.ds(q0, tq)])) * ATTN_SCALE)
            dk_ref[h, pl.ds(k0, tk), :] += _mm(dst, qi)
            dq_ref[h, pl.ds(q0, tq), :] += _mm_tn(dst, kj)

        def kv_step(j, _):
            k0 = pl.multiple_of(j * tk, tk)
            for h in range(hb):
                tile(h, k0, k0, True)

            def q_step(i, _):
                q0 = pl.multiple_of(i * tq, tq)
                for h in range(hb):
                    tile(h, k0, q0, False)
                return 0

            lax.fori_loop(j + 1, n, q_step, 0)
            return 0

        lax.fori_loop(0, n, kv_step, 0)

    hspec = lambda c: pl.BlockSpec((hb, seq, c), lambda b, g: (g, b, 0))
    cols = pl.BlockSpec((seq, hb * V_DIM), lambda b, g: (b, g))
    return _pallas(
        body, "attn_bwd", (nseq, N_HEADS // hb),
        [hspec(HEAD_PAD), hspec(HEAD_PAD), cols, cols,
         pl.BlockSpec((hb, 1, seq), lambda b, g: (g, 0, b)), pl.BlockSpec((None, hb, seq), lambda b, g: (g, 0, b))],
        [hspec(HEAD_PAD), hspec(HEAD_PAD), cols],
        [jax.ShapeDtypeStruct((N_HEADS, t, HEAD_PAD), F32), jax.ShapeDtypeStruct((N_HEADS, t, HEAD_PAD), F32),
         jax.ShapeDtypeStruct((t, N_HEADS * V_DIM), F32)],
        (q, k, v, do, lse, delta), ("arbitrary", "arbitrary"), carried)


def _merged_mixers(o_ref, gb_ref, gla_ref, glb_ref, xc_ref, gc_ref, xcp_ref, gcp_ref, bias_ref, cw_ref, wpa_ref, wpc_ref,
                   first_of_seq):
    y_a = _mm(o_ref[...], wpa_ref[...])
    gb = gb_ref[...]
    u = gc_ref[...] * xc_ref[...]
    u_prev = jnp.where(first_of_seq, 0.0, gcp_ref[...] * xcp_ref[...])
    cw = cw_ref[...]
    z = cw[2:3] * u + cw[1:2] * _shift_down(u, u_prev, 1) + cw[0:1] * _shift_down(u, u_prev, 2)
    gbz = _bf(gb * z)
    y_b = _mm(gbz, wpc_ref[...])
    bias = bias_ref[...]
    gate_a = _sigmoid(gla_ref[...] + bias[:, :D_MODEL])
    gate_b = _sigmoid(glb_ref[...] + bias[:, D_MODEL:])
    return _bf(gate_a * y_a + gate_b * y_b)


def _mixer_specs(tm, seq):
    d = D_MODEL
    tok = pl.BlockSpec((tm, d), lambda i: (i, 0))
    col = lambda c: pl.BlockSpec((tm, d), lambda i: (i, c))
    prev = lambda c: pl.BlockSpec((8, d), lambda i: (jnp.maximum(i * (tm // 8) - 1, 0), c))
    o_spec = pl.BlockSpec((tm, N_HEADS * V_DIM), lambda i: (i, 0))
    fwd_specs = [o_spec, col(0), col(1), col(2), col(3), col(4), prev(3), prev(4), _const((1, 2 * d)), _const((3, d)),
                 _resident((N_HEADS * V_DIM, d)), _resident((d, d)), _resident((d, d))]
    return tok, fwd_specs


def _mix_fwd(x1, o, big, gate_bias, conv_w, w_pa, w_pc, w_out, seq, carried=None):
    t, d = x1.shape
    tm = WIDE_TILE
    tiles_per_seq = seq // tm

    def body(x_ref, o_ref, gb_ref, gla_ref, glb_ref, xc_ref, gc_ref, xcp_ref, gcp_ref, bias_ref, cw_ref, wpa_ref, wpc_ref,
             wout_ref, x2_ref):
        first = pl.program_id(0) % tiles_per_seq == 0
        merged = _merged_mixers(o_ref, gb_ref, gla_ref, glb_ref, xc_ref, gc_ref, xcp_ref, gcp_ref, bias_ref, cw_ref, wpa_ref,
                                wpc_ref, first)
        x2_ref[...] = x_ref[...] + _mm(merged, wout_ref[...])

    tok, fwd_specs = _mixer_specs(tm, seq)
    return _pallas(body, "mix_fwd", (t // tm,), [tok] + fwd_specs, [tok], [jax.ShapeDtypeStruct((t, d), F32)],
                   (x1, o, big, big, big, big, big, big, big, gate_bias, conv_w, w_pa, w_pc, w_out), ("arbitrary",), carried)[0]


def _mix_bwd(dx2, o, big, gate_bias, conv_w, w_pa, w_pc, w_out, seq, carried=None):
    t, d = dx2.shape
    tm = TOKEN_TILE
    tiles_per_seq = seq // tm
    hv = N_HEADS * V_DIM

    def body(dx_ref, o_ref, gb_ref, gla_ref, glb_ref, xc_ref, gc_ref, xcp_ref, gcp_ref, bias_ref, cw_ref, wpa_ref, wpc_ref,
             wout_ref, do_ref, delta_ref, dz_ref, dm_ref, dbias_ref, dwpa_ref, dwpc_ref, dwout_ref):
        @pl.when(pl.program_id(0) == 0)
        def _():
            dbias_ref[...] = jnp.zeros_like(dbias_ref)
            dwpa_ref[...] = jnp.zeros_like(dwpa_ref)
            dwpc_ref[...] = jnp.zeros_like(dwpc_ref)
            dwout_ref[...] = jnp.zeros_like(dwout_ref)

        first = pl.program_id(0) % tiles_per_seq == 0
        dxb = _bf(dx_ref[...])
        dmerged = _mm_nt(dxb, wout_ref[...])
        y_a = _mm(o_ref[...], wpa_ref[...])
        bias = bias_ref[...]
        gate_a = _sigmoid(gla_ref[...] + bias[:, :d])
        gate_b = _sigmoid(glb_ref[...] + bias[:, d:])
        dya = _bf(dmerged * gate_a)
        dyb = _bf(dmerged * gate_b)
        do_v = _mm_nt(dya, wpa_ref[...])
        dgz = _mm_nt(dyb, wpc_ref[...])
        dwpa_ref[...] += _mm_tn(o_ref[...], dya)
        gb = gb_ref[...]
        u = gc_ref[...] * xc_ref[...]
        u_prev = jnp.where(first, 0.0, gcp_ref[...] * xcp_ref[...])
        cw = cw_ref[...]
        z = cw[2:3] * u + cw[1:2] * _shift_down(u, u_prev, 1) + cw[0:1] * _shift_down(u, u_prev, 2)
        gbz = _bf(gb * z)
        y_b = _mm(gbz, wpc_ref[...])
        dwpc_ref[...] += _mm_tn(gbz, dyb)
        do_ref[...] = do_v
        head = lax.broadcasted_iota(jnp.int32, (N_HEADS, hv), 0) * V_DIM
        col = lax.broadcasted_iota(jnp.int32, (N_HEADS, hv), 1)
        in_head = ((col >= head) & (col < head + V_DIM)).astype(F32)
        delta_ref[...] = lax.dot_general(in_head, do_v * o_ref[...].astype(F32), (((1,), (1,)), ((), ())),
                                         precision=lax.Precision.HIGHEST, preferred_element_type=F32)
        dz_ref[...] = dgz * gb
        dm_ref[:, :d] = _bf(dgz * z)
        merged = _bf(gate_a * y_a + gate_b * y_b)
        dwout_ref[...] += _mm_tn(merged, dxb)
        dla = (dmerged * y_a) * (gate_a * (1.0 - gate_a))
        dlb = (dmerged * y_b) * (gate_b * (1.0 - gate_b))
        dbias_ref[:, :d] += jnp.sum(dla, axis=0, keepdims=True)
        dbias_ref[:, d:] += jnp.sum(dlb, axis=0, keepdims=True)
        dm_ref[:, d:2 * d] = _bf(dla)
        dm_ref[:, 2 * d:] = _bf(dlb)

    tok, fwd_specs = _mixer_specs(tm, seq)
    return _pallas(
        body, "mix_bwd", (t // tm,), [tok] + fwd_specs,
        [pl.BlockSpec((tm, hv), lambda i: (i, 0)), pl.BlockSpec((N_HEADS, tm), lambda i: (0, i)), tok,
         pl.BlockSpec((tm, M_COLS), lambda i: (i, 0)), _const((1, 2 * d)), _const((hv, d)), _const((d, d)), _const((d, d))],
        [jax.ShapeDtypeStruct((t, hv), F32), jax.ShapeDtypeStruct((N_HEADS, t), F32), jax.ShapeDtypeStruct((t, d), F32),
         jax.ShapeDtypeStruct((t, M_COLS), BF16), jax.ShapeDtypeStruct((1, 2 * d), F32), jax.ShapeDtypeStruct((hv, d), F32),
         jax.ShapeDtypeStruct((d, d), F32), jax.ShapeDtypeStruct((d, d), F32)],
        (dx2, o, big, big, big, big, big, big, big, gate_bias, conv_w, w_pa, w_pc, w_out), ("arbitrary",), carried)


def _prep_bwd(lat, big, dz, dq, dk, dv, qa_gain, kva_gain, qh_gain, kh_gain, w_uq, w_uk, w_uv, rope, conv_w, seq, carried=None):
    t = lat.shape[0]
    d = D_MODEL
    tm = WIDE_TILE
    tiles_per_seq = seq // tm
    last_blk = t // 8 - 1

    def body(lat_ref, xc_ref, gc_ref, dz_ref, dzn_ref, dq_ref, dk_ref, dv_ref, qa_ref, kva_ref, qh_ref, kh_ref, wuq_ref, wuk_ref,
             wuv_ref, cos_ref, slo_ref, shi_ref, cw_ref,
             dp_ref, dwuq_ref, dwuk_ref, dwuv_ref, dqa_ref, dkva_ref, dqh_ref, dkh_ref, dcw_ref):
        pid = pl.program_id(0)

        @pl.when(pid == 0)
        def _():
            for r in (dwuq_ref, dwuk_ref, dwuv_ref, dqa_ref, dkva_ref, dqh_ref, dkh_ref, dcw_ref):
                r[...] = jnp.zeros_like(r)

        lat_v = lat_ref[...]
        q_lat = lat_v[:, :Q_LORA]
        kv_lat = lat_v[:, Q_LORA:Q_LORA + KV_LORA]
        k_rope = lat_v[:, Q_LORA + KV_LORA:]
        qa_gain_v = qa_ref[...]
        kva_gain_v = kva_ref[...]
        qh_gain_v = qh_ref[...]
        kh_gain_v = kh_ref[...]
        cq, rq = _rms(q_lat, qa_gain_v)
        ckv, rkv = _rms(kv_lat, kva_gain_v)
        cqb = _bf(cq)
        ckvb = _bf(ckv)
        rope_v = (cos_ref[...], slo_ref[...], shi_ref[...])
        lane = lax.broadcasted_iota(jnp.int32, (tm, HEAD_PAD), 1)
        rope_lanes = (lane >= QK_NOPE) & (lane < QK_DIM)
        dk_rope = jnp.zeros((tm, HEAD_PAD), F32)
        dqh_gain = jnp.zeros((1, HEAD_PAD), F32)
        dkh_gain = jnp.zeros((1, HEAD_PAD), F32)
        q_all = _mm(cqb, wuq_ref[...])
        k_all = _mm(ckvb, wuk_ref[...])
        dvb = _bf(dv_ref[...])
        dckv = _mm_nt(dvb, wuv_ref[...])
        dwuv_ref[...] += _mm_tn(ckvb, dvb)

        last = pid % tiles_per_seq == tiles_per_seq - 1
        dzv = dz_ref[...]
        dz_next = jnp.where(last, 0.0, dzn_ref[...])
        dz1 = _shift_up(dzv, dz_next, 1)
        dz2 = _shift_up(dzv, dz_next, 2)
        cw = cw_ref[...]
        xc = xc_ref[...]
        gc = gc_ref[...]
        u = gc * xc
        du = cw[2:3] * dzv + cw[1:2] * dz1 + cw[0:1] * dz2
        dp_ref[:, :d] = _bf(du * gc)
        dp_ref[:, d:2 * d] = _bf(du * xc)
        dcw_ref[0:1, :] += jnp.sum(dz2 * u, axis=0, keepdims=True)
        dcw_ref[1:2, :] += jnp.sum(dz1 * u, axis=0, keepdims=True)
        dcw_ref[2:3, :] += jnp.sum(dzv * u, axis=0, keepdims=True)

        dcq = jnp.zeros((tm, Q_LORA), F32)
        half = N_HEADS // 2
        for part in range(2):
            dq_heads, dk_heads = [], []
            for hd in range(part * half, (part + 1) * half):
                lanes = slice(hd * HEAD_PAD, (hd + 1) * HEAD_PAD)
                q_pre = q_all[:, lanes]
                _, rr = _rms(q_pre, qh_gain_v, QK_DIM)
                dq_pre, dg = _rms_bwd(q_pre, rr, qh_gain_v, _rope_bwd(dq_ref[hd], rope_v), QK_DIM)
                dqh_gain = dqh_gain + dg
                dq_heads.append(_bf(dq_pre))

                k_pre = k_all[:, lanes] + k_rope
                _, rr = _rms(k_pre, kh_gain_v, QK_DIM)
                dk_pre, dg = _rms_bwd(k_pre, rr, kh_gain_v, _rope_bwd(dk_ref[hd], rope_v), QK_DIM)
                dkh_gain = dkh_gain + dg
                dk_rope = dk_rope + jnp.where(rope_lanes, dk_pre, 0.0)
                dk_heads.append(_bf(dk_pre))
            dq_part = jnp.concatenate(dq_heads, axis=1)
            dk_part = jnp.concatenate(dk_heads, axis=1)
            cols = slice(part * half * HEAD_PAD, (part + 1) * half * HEAD_PAD)
            dcq = dcq + _mm_nt(dq_part, wuq_ref[:, cols])
            dckv = dckv + _mm_nt(dk_part, wuk_ref[:, cols])
            dwuq_ref[:, cols] += _mm_tn(cqb, dq_part)
            dwuk_ref[:, cols] += _mm_tn(ckvb, dk_part)
        dqh_ref[...] += dqh_gain
        dkh_ref[...] += dkh_gain
        dq_lat, dg = _rms_bwd(q_lat, rq, qa_gain_v, dcq)
        dqa_ref[...] += dg
        dkv_lat, dg = _rms_bwd(kv_lat, rkv, kva_gain_v, dckv)
        dkva_ref[...] += dg
        dp_ref[:, 2 * d:2 * d + Q_LORA] = _bf(dq_lat)
        dp_ref[:, 2 * d + Q_LORA:2 * d + Q_LORA + KV_LORA] = _bf(dkv_lat)
        dp_ref[:, 2 * d + Q_LORA + KV_LORA:] = _bf(dk_rope)

    tok = lambda c: pl.BlockSpec((tm, c), lambda i: (i, 0))
    col = lambda c: pl.BlockSpec((tm, d), lambda i: (i, c))
    head = lambda c: pl.BlockSpec((N_HEADS, tm, c), lambda i: (0, i, 0))
    nxt = pl.BlockSpec((8, d), lambda i: (jnp.minimum((i + 1) * (tm // 8), last_blk), 0))
    return _pallas(
        body, "prep_bwd", (t // tm,),
        [tok(LAT_COLS), col(3), col(4), tok(d), nxt, head(HEAD_PAD), head(HEAD_PAD), tok(N_HEADS * V_DIM),
         _const((1, Q_LORA)), _const((1, KV_LORA)), _const((1, HEAD_PAD)), _const((1, HEAD_PAD)),
         _resident(w_uq.shape), _resident(w_uk.shape), _resident(w_uv.shape), tok(HEAD_PAD), tok(HEAD_PAD), tok(HEAD_PAD),
         _const((3, d))],
        [tok(P_COLS), _const(w_uq.shape), _const(w_uk.shape), _const(w_uv.shape), _const((1, Q_LORA)),
         _const((1, KV_LORA)), _const((1, HEAD_PAD)), _const((1, HEAD_PAD)), _const((3, d))],
        [jax.ShapeDtypeStruct((t, P_COLS), BF16), jax.ShapeDtypeStruct(w_uq.shape, F32),
         jax.ShapeDtypeStruct(w_uk.shape, F32), jax.ShapeDtypeStruct(w_uv.shape, F32),
         jax.ShapeDtypeStruct((1, Q_LORA), F32), jax.ShapeDtypeStruct((1, KV_LORA), F32),
         jax.ShapeDtypeStruct((1, HEAD_PAD), F32), jax.ShapeDtypeStruct((1, HEAD_PAD), F32), jax.ShapeDtypeStruct((3, d), F32)],
        (lat, big, big, dz, dz, dq, dk, dv, qa_gain, kva_gain, qh_gain, kh_gain, w_uq, w_uk, w_uv, *rope, conv_w),
        ("arbitrary",), carried)


def _inproj_bwd(x1, gain, dx2, dm, dp, w_in, w_kr, carried=None):
    t, d = x1.shape
    tm = TOKEN_TILE

    def body(x_ref, g_ref, dx2_ref, dm_ref, dp_ref, win_ref, wkr_ref, dx1_ref, dgain_ref):
        xv = x_ref[...]
        gain_v = g_ref[...]
        _, r = _rms(xv, gain_v)
        dh = (_mm(dm_ref[:, :d], win_ref[ROW_GB:ROW_GC, :]) + _mm(dm_ref[:, d:], win_ref[ROW_GL:, :])
              + _mm(dp_ref[:, :d], win_ref[ROW_XC:ROW_GB, :]) + _mm(dp_ref[:, d:2 * d], win_ref[ROW_GC:ROW_GL, :])
              + _mm(dp_ref[:, 2 * d:2 * d + ROW_KR], win_ref[ROW_QKV:ROW_KR, :]) + _mm(dp_ref[:, 2 * d + ROW_KR:], wkr_ref[...]))
        dxn, dgain = _rms_bwd(xv, r, gain_v, dh)
        dx1_ref[...] = dx2_ref[...] + dxn

        @pl.when(pl.program_id(0) == 0)
        def _():
            dgain_ref[...] = jnp.zeros_like(dgain_ref)

        dgain_ref[...] += dgain

    tok = lambda c: pl.BlockSpec((tm, c), lambda i: (i, 0))
    return _pallas(
        body, "inproj_bwd", (t // tm,),
        [tok(d), _const((1, d)), tok(d), tok(M_COLS), tok(P_COLS), _resident(w_in.shape), _resident(w_kr.shape)],
        [tok(d), _const((1, d))], [jax.ShapeDtypeStruct((t, d), F32), jax.ShapeDtypeStruct((1, d), F32)],
        (x1, gain, dx2, dm, dp, w_in, w_kr), ("arbitrary",), carried)


def _adamw(quads, name, carried=None):
    k = len(quads)
    rows, cols = quads[0][0].shape
    tr, tc = rows, cols
    for cand in (512, 352, 256, 192, 128, 64):
        if rows % cand == 0 and rows > cand:
            tr = cand
            break
    if tr == rows and rows * cols > 512 * 1024 and cols % 256 == 0:
        tc = 256
    while k * 14 * tr * tc * 4 > (VMEM_LIMIT * 3) // 4 and tr % 16 == 0:
        tr //= 2

    def body(*refs):
        for i in range(k):
            w_ref, g_ref, m_ref, v_ref = refs[4 * i:4 * i + 4]
            delta_ref, nm_ref, nv_ref = refs[4 * k + 3 * i:4 * k + 3 * i + 3]
            delta_ref[...], nm_ref[...], nv_ref[...] = _adamw_update(w_ref[...], g_ref[...], m_ref[...], v_ref[...])

    spec = pl.BlockSpec((tr, tc), lambda i, j: (i, j))
    shape = jax.ShapeDtypeStruct((rows, cols), F32)
    outs = _pallas(body, name, (rows // tr, cols // tc), [spec] * (4 * k), [spec] * (3 * k), [shape] * (3 * k),
                   [a for quad in quads for a in quad], ("arbitrary", "arbitrary"), carried)
    return [tuple(outs[3 * i:3 * i + 3]) for i in range(k)]


def _adamw_update(w, g, m, v):
    nm = ADAM_B1 * m + (1.0 - ADAM_B1) * g
    nv = ADAM_B2 * v + (1.0 - ADAM_B2) * (g * g)
    m_hat = nm * (1.0 / (1.0 - ADAM_B1 ** ADAM_STEP))
    v_hat = nv * (1.0 / (1.0 - ADAM_B2 ** ADAM_STEP))
    return -ADAM_LR * (m_hat / (jnp.sqrt(v_hat) + ADAM_EPS) + ADAM_WD * w), nm, nv


def _adamw_whole(quads, name):
    k = len(quads)

    def body(*refs):
        for i in range(k):
            w_ref, g_ref, m_ref, v_ref = refs[4 * i:4 * i + 4]
            g_out, delta_ref, nm_ref, nv_ref = refs[4 * k + 4 * i:4 * k + 4 * i + 4]
            gv = g_ref[...]
            g_out[...] = gv
            delta_ref[...], nm_ref[...], nv_ref[...] = _adamw_update(w_ref[...], gv, m_ref[...], v_ref[...])

    vm = pl.BlockSpec(memory_space=pltpu.VMEM)
    outs = pl.pallas_call(body, name=name, in_specs=[vm] * (4 * k), out_specs=[vm] * (4 * k),
                          out_shape=[jax.ShapeDtypeStruct(q[0].shape, F32) for q in quads for _ in range(4)],
                          compiler_params=_params())(*[a for quad in quads for a in quad])
    return [tuple(outs[4 * i:4 * i + 4]) for i in range(k)]


def _adamw_small(packed_grads, triples, segments):
    k = len(triples)

    def body(*refs):
        g_ref = refs[0]
        off = 0
        for i in range(k):
            w_ref, m_ref, v_ref = refs[1 + 3 * i:4 + 3 * i]
            g_out, delta_ref, nm_ref, nv_ref = refs[1 + 3 * k + 4 * i:5 + 3 * k + 4 * i]
            gv = g_ref[:, off:off + w_ref.shape[1]]
            g_out[...] = gv
            delta_ref[...], nm_ref[...], nv_ref[...] = _adamw_update(w_ref[...], gv, m_ref[...], v_ref[...])
            off += segments[i]

    vm = pl.BlockSpec(memory_space=pltpu.VMEM)
    outs = pl.pallas_call(
        body, name="adamw_small", in_specs=[vm] * (1 + 3 * k), out_specs=[vm] * (4 * k),
        out_shape=[jax.ShapeDtypeStruct(w.shape, F32) for w, _, _ in triples for _ in range(4)],
    )(packed_grads, *[a for triple in triples for a in triple])
    return [tuple(outs[4 * i:4 * i + 4]) for i in range(k)]


def _place():
    x, y, c = lax.axis_index("x"), lax.axis_index("y"), lax.axis_index("c")
    other_chips = [(1 - x, y), (x, 1 - y), (1 - x, 1 - y)]
    return x, y, c, other_chips


def _remote(src, dst, sems, send, recv, device):
    return pltpu.make_async_remote_copy(src_ref=src, dst_ref=dst, send_sem=sems.at[send], recv_sem=sems.at[recv],
                                        device_id=device, device_id_type=MESH_ID)


def _cast_shards(shards, out_dtypes, n_first):
    n = len(shards)
    out_shape = [jax.ShapeDtypeStruct((N_CHIPS,) + s.shape, dt) for s, dt in zip(shards, out_dtypes)]
    gather = _gather_carried(out_shape[:n_first])

    def body(*refs):
        ins, outs, stage, sems = refs[:n], refs[n:2 * n], refs[2 * n:3 * n], refs[3 * n]
        x, y, _, _ = _place()
        me = 2 * x + y

        def cast(first, last):
            copies = []
            for w in range(first, last):
                stage[w][...] = ins[w][...].astype(out_dtypes[w])
                copies.append(pltpu.make_async_copy(stage[w], outs[w].at[me], sems.at[w]))
                copies[-1].start()
            for cp in copies:
                cp.wait()

        cast(0, n_first)
        gather.start(None, outs[:n_first], sems, n)
        cast(n_first, n)
        gather.finish(None, outs[:n_first], sems, n)

    vm = pl.BlockSpec(memory_space=pltpu.VMEM)
    return pl.pallas_call(
        body, name="cast_shards", in_specs=[vm] * n, out_specs=[ANY] * n, out_shape=out_shape,
        scratch_shapes=[pltpu.VMEM(s.shape, dt) for s, dt in zip(shards, out_dtypes)] + [pltpu.SemaphoreType.DMA((n + gather.n_sems,))],
        compiler_params=_params())(*shards)


BF16_ROWS = 16


def _split_rows(rows):
    return (rows // 2) % BF16_ROWS == 0


def _half_shape(rows, cols):
    return (rows // 2, cols) if _split_rows(rows) else (rows, cols // 2)


def _half(rows, cols, which):
    if _split_rows(rows):
        return (pl.ds(pl.multiple_of(which * (rows // 2), BF16_ROWS), rows // 2), slice(None))
    return (slice(None), pl.ds(pl.multiple_of(which * (cols // 2), 128), cols // 2))


def _gather_carried(bufs):
    n = len(bufs)

    def half(w, slot, which):
        _, rows, cols = bufs[w].shape
        return (slot,) + _half(rows, cols, which)

    def start(ins, outs, sems, base):
        x, y, c, other_chips = _place()
        me = 2 * x + y
        for w in range(n):
            mine = outs[w].at[half(w, me, c)]
            for p, (px, py) in enumerate(other_chips):
                _remote(mine, mine, sems, base + 12 * w + p, base + 12 * w + 3 + p, (px, py, c)).start()

    def finish(ins, outs, sems, base):
        x, y, c, other_chips = _place()
        me = 2 * x + y
        for w in range(n):
            for p, (px, py) in enumerate(other_chips):
                got = outs[w].at[half(w, 2 * px + py, c)]
                _remote(got, got, sems, base + 12 * w + p, base + 12 * w + 3 + p, (px, py, c)).wait_recv()
                _remote(got, got, sems, base + 12 * w + 6 + p, base + 12 * w + 9 + p, (x, y, 1 - c)).start()
        for w in range(n):
            mine = outs[w].at[half(w, me, c)]
            for p, (px, py) in enumerate(other_chips):
                got = outs[w].at[half(w, 2 * px + py, c)]
                theirs = outs[w].at[half(w, 2 * px + py, 1 - c)]
                _remote(got, theirs, sems, base + 12 * w + 6 + p, base + 12 * w + 9 + p, (x, y, 1 - c)).wait()
                _remote(mine, mine, sems, base + 12 * w + p, base + 12 * w + 3 + p, (px, py, c)).wait_send()

    shapes = [jax.ShapeDtypeStruct(b.shape, b.dtype) for b in bufs]
    return _Carried(bufs, shapes, {w: w for w in range(n)}, 12 * n, start, finish)


def _swap_carried(grads):
    n = len(grads)

    def copy(w, ins, outs, sems, base):
        x, y, c, _ = _place()
        _, rows, cols = grads[w].shape
        theirs = ins[w].at[(slice(None),) + _half(rows, cols, 1 - c)]
        return _remote(theirs, outs[w], sems, base + 2 * w, base + 2 * w + 1, (x, y, 1 - c))

    def start(ins, outs, sems, base):
        for w in range(n):
            copy(w, ins, outs, sems, base).start()

    def finish(ins, outs, sems, base):
        for w in range(n):
            copy(w, ins, outs, sems, base).wait()

    shapes = [jax.ShapeDtypeStruct((g.shape[0],) + _half_shape(*g.shape[1:]), F32) for g in grads]
    return _Carried(grads, shapes, {}, 2 * n, start, finish)


def _row_tile(rows):
    for cand in (512, 352, 256, 192, 128, 96, 64, 32, 16):
        if rows % cand == 0:
            return cand
    return rows


def _half_block_index(split_rows, tiles, i, core):
    return (core * tiles + i, 0) if split_rows else (i, core)


def _chip_partial(grad, other, place, name):
    nblk, hr, hc = other.shape
    by_rows = _split_rows(grad.shape[1])
    tr = _row_tile(hr)
    tiles = hr // tr

    def body(place_ref, g_ref, o_ref, own_ref, sum_bf_ref):
        s = g_ref[...] + o_ref[...]
        sum_bf_ref[...] = _bf(s)

        @pl.when(pl.program_id(1) == place_ref[0])
        def _():
            own_ref[...] = s

    grid_spec = pltpu.PrefetchScalarGridSpec(
        num_scalar_prefetch=1, grid=(tiles, nblk),
        in_specs=[pl.BlockSpec((None, tr, hc), lambda i, b, place_ref: (b,) + _half_block_index(by_rows, tiles, i, place_ref[1])),
                  pl.BlockSpec((None, tr, hc), lambda i, b, place_ref: (b, i, 0))],
        out_specs=[pl.BlockSpec((tr, hc), lambda i, b, place_ref: (i, 0)),
                   pl.BlockSpec((None, tr, hc), lambda i, b, place_ref: (b, i, 0))])
    return pl.pallas_call(body, name=name, grid_spec=grid_spec,
                          out_shape=[jax.ShapeDtypeStruct((hr, hc), F32), jax.ShapeDtypeStruct(other.shape, BF16)],
                          compiler_params=_params(("arbitrary", "arbitrary")))(place, grad, other)


def _chip_partial_small(grads, others, place):
    n = len(grads)

    def body(*refs):
        place_ref, g_refs, o_refs = refs[0], refs[1:1 + n], refs[1 + n:1 + 2 * n]
        own_refs, bf_refs = refs[1 + 2 * n:1 + 3 * n], refs[1 + 3 * n:]
        chip, core = place_ref[0], place_ref[1]
        for w in range(n):
            _, rows, cols = grads[w].shape
            half = _half(rows, cols, core)
            bf_refs[w][...] = _bf(g_refs[w][(slice(None),) + half] + o_refs[w][...])
            own_refs[w][...] = g_refs[w][(chip,) + half] + o_refs[w][chip]

    vm = pl.BlockSpec(memory_space=pltpu.VMEM)
    outs = pl.pallas_call(
        body, name="chip_partial_small", in_specs=[pl.BlockSpec(memory_space=pltpu.SMEM)] + [vm] * (2 * n), out_specs=[vm] * (2 * n),
        out_shape=[jax.ShapeDtypeStruct(o.shape[1:], F32) for o in others] + [jax.ShapeDtypeStruct(o.shape, BF16) for o in others],
        compiler_params=_params())(place, *grads, *others)
    return list(zip(outs[:n], outs[n:]))


def _chip_total_small(owns, receiveds, place, shapes):
    n = len(owns)

    def body(*refs):
        place_ref, own_refs, r_refs, out_refs = refs[0], refs[1:1 + n], refs[1 + n:1 + 2 * n], refs[1 + 2 * n:]
        chip, core = place_ref[0], place_ref[1]
        for w in range(n):
            r = [r_refs[w][(chip + k) % N_CHIPS].astype(F32) for k in (1, 2, 3)]
            out_refs[w][_half(*shapes[w], core)] = own_refs[w][...] + ((r[0] + r[1]) + r[2])

    vm = pl.BlockSpec(memory_space=pltpu.VMEM)
    return list(pl.pallas_call(
        body, name="chip_total_small", in_specs=[pl.BlockSpec(memory_space=pltpu.SMEM)] + [vm] * (2 * n), out_specs=[vm] * n,
        out_shape=[jax.ShapeDtypeStruct(tuple(s), F32) for s in shapes], compiler_params=_params())(place, *owns, *receiveds))


def _send_carried(partials):
    n = len(partials)

    def start(ins, outs, sems, base):
        x, y, c, other_chips = _place()
        me = 2 * x + y
        for w in range(n):
            for p, (px, py) in enumerate(other_chips):
                _remote(ins[w].at[2 * px + py], outs[w].at[me], sems, base + 6 * w + p, base + 6 * w + 3 + p, (px, py, c)).start()

    def finish(ins, outs, sems, base):
        x, y, c, other_chips = _place()
        for w in range(n):
            for p, (px, py) in enumerate(other_chips):
                _remote(ins[w].at[2 * px + py], outs[w].at[2 * px + py], sems, base + 6 * w + p, base + 6 * w + 3 + p,
                        (px, py, c)).wait()

    return _Carried(partials, [jax.ShapeDtypeStruct(p.shape, BF16) for p in partials], {}, 6 * n, start, finish)


def _chip_total(own, received, place, shape, name):
    hr, hc = own.shape
    by_rows = _split_rows(shape[0])
    tr = _row_tile(hr)
    tiles = hr // tr

    def body(place_ref, own_ref, r1_ref, r2_ref, r3_ref, out_ref):
        out_ref[...] = own_ref[...] + ((r1_ref[...].astype(F32) + r2_ref[...].astype(F32)) + r3_ref[...].astype(F32))

    def slot(k):
        return pl.BlockSpec((None, tr, hc), lambda i, place_ref: ((place_ref[0] + k) % N_CHIPS, i, 0))

    grid_spec = pltpu.PrefetchScalarGridSpec(
        num_scalar_prefetch=1, grid=(tiles,), in_specs=[pl.BlockSpec((tr, hc), lambda i, place_ref: (i, 0)), slot(1), slot(2), slot(3)],
        out_specs=pl.BlockSpec((tr, hc), lambda i, place_ref: _half_block_index(by_rows, tiles, i, place_ref[1])))
    return pl.pallas_call(body, name=name, grid_spec=grid_spec, out_shape=jax.ShapeDtypeStruct(tuple(shape), F32),
                          compiler_params=_params(("arbitrary",)))(place, own, received, received, received)


def _join_carried(totals):
    n = len(totals)

    def copy(w, outs, sems, base):
        x, y, c, _ = _place()
        mine = outs[w].at[_half(*totals[w].shape, c)]
        return _remote(mine, mine, sems, base + 2 * w, base + 2 * w + 1, (x, y, 1 - c))

    def start(ins, outs, sems, base):
        for w in range(n):
            copy(w, outs, sems, base).start()

    def finish(ins, outs, sems, base):
        for w in range(n):
            copy(w, outs, sems, base).wait()

    shapes = [jax.ShapeDtypeStruct(a.shape, F32) for a in totals]
    return _Carried(totals, shapes, {w: w for w in range(n)}, 2 * n, start, finish)


def _sum_devices(vec):
    rows, n = vec.shape

    def body(v_ref, out_ref, buf, send_sems, recv_sems):
        x, y, c, _ = _place()
        me = 4 * x + 2 * y + c
        buf[me] = v_ref[...]
        sends = []
        for k in range(1, N_DEV):
            peer = (1 - x if k & 4 else x, 1 - y if k & 2 else y, 1 - c if k & 1 else c)
            cp = pltpu.make_async_remote_copy(src_ref=v_ref, dst_ref=buf.at[me], send_sem=send_sems.at[k], recv_sem=recv_sems.at[k],
                                              device_id=peer, device_id_type=MESH_ID)
            cp.start()
            sends.append(cp)
        for cp in sends:
            cp.wait()
        total = buf[0]
        for dev in range(1, N_DEV):
            total = total + buf[dev]
        out_ref[...] = total

    vm = pl.BlockSpec(memory_space=pltpu.VMEM)
    return pl.pallas_call(
        body, name="sum_devices", in_specs=[vm], out_specs=vm, out_shape=jax.ShapeDtypeStruct((rows, n), F32),
        scratch_shapes=[pltpu.VMEM((N_DEV, rows, n), F32), pltpu.SemaphoreType.DMA((N_DEV,)), pltpu.SemaphoreType.DMA((N_DEV,))],
    )(vec)


def _rope_tables(positions):
    half = ROPE_HALF
    inv_freq = 1.0 / (ROPE_THETA ** (jnp.arange(half, dtype=F32) / half))
    ang = positions.astype(F32).reshape(-1, 1) * inv_freq
    cos, sin = jnp.cos(ang), jnp.sin(ang)
    t = ang.shape[0]
    ones, zeros = jnp.ones((t, QK_NOPE), F32), jnp.zeros((t, QK_NOPE), F32)
    pad, none = HEAD_PAD - QK_DIM, zeros[:, :half]
    cos_full = jnp.concatenate([ones, cos, cos, ones[:, :pad]], axis=1)
    s_lo = jnp.concatenate([zeros, -sin, none, zeros[:, :pad]], axis=1)
    s_hi = jnp.concatenate([zeros, none, sin, zeros[:, :pad]], axis=1)
    return cos_full, s_lo, s_hi


def _partials(names, grads, from_sibling, place):
    return [_chip_partial(g, o, place, "chip_partial_" + n) for n, g, o in zip(names, grads, from_sibling)]


def _totals(names, grads, partials, received, place):
    return [_chip_total(pf, r, place, g.shape[1:], "chip_total_" + n) for n, g, (pf, _), r in zip(names, grads, partials, received)]


def _kernel_layouts(full):
    w_in = full["w_in"]
    w_kr = jnp.pad(w_in[ROW_KR:ROW_XC], ((QK_NOPE, HEAD_PAD - QK_DIM), (0, 0)))
    w_uq = jnp.pad(full["w_uq"].reshape(Q_LORA, N_HEADS, QK_DIM), ((0, 0), (0, 0), (0, HEAD_PAD - QK_DIM)))
    w_uk = jnp.pad(full["w_uk"].reshape(KV_LORA, N_HEADS, QK_NOPE), ((0, 0), (0, 0), (0, HEAD_PAD - QK_NOPE)))
    return {"w_in": w_in, "w_kr": w_kr, "w_uq": w_uq.reshape(Q_LORA, N_HEADS * HEAD_PAD),
            "w_uk": w_uk.reshape(KV_LORA, N_HEADS * HEAD_PAD), "w_uv": full["w_uv"], "w_uvt": full["w_uv"].T}


def _global_layouts(g):
    w_uq = g["w_uq"].reshape(Q_LORA, N_HEADS, HEAD_PAD)[:, :, :QK_DIM].reshape(Q_LORA, N_HEADS * QK_DIM)
    w_uk = g["w_uk"].reshape(KV_LORA, N_HEADS, HEAD_PAD)[:, :, :QK_NOPE].reshape(KV_LORA, N_HEADS * QK_NOPE)
    return {"w_in": g["w_in"], "w_uq": w_uq, "w_uk": w_uk, "w_uv": g["w_uv"], "w_proj_attn": g["w_pa"], "w_proj_conv": g["w_pc"],
            "w_out": g["w_out"]}


def _dw_in(dm, dp, h2b):
    d, rows = D_MODEL, ROW_GL + 2 * D_MODEL
    wm, wp = M_COLS // 2, P_COLS // 2
    from_dm = [(0, [(ROW_GB, 0, d), (ROW_GL, d, wm - d)]), (1, [(ROW_GL + wm - d, 0, wm)])]
    from_dp = [(0, [(ROW_XC, 0, d), (ROW_GC, d, wp - d)]),
               (1, [(ROW_GC + wp - d, 0, 2 * d - wp), (ROW_QKV, 2 * d - wp, ROW_KR), (ROW_KR, 2 * d - wp + ROW_KR + QK_NOPE, QK_ROPE)])]
    out = _tn_rows(dm, h2b, None, from_dm, wm, rows, "dw_in_m")
    return _tn_rows(dp, h2b, out, from_dp, wp, rows, "dw_in_p")


def _col_blocks(a):
    r, c = a.shape
    return a.reshape(r, N_CHIPS, c // N_CHIPS).transpose(1, 0, 2)


def _from_col_blocks(a):
    n, r, c = a.shape
    return a.transpose(1, 0, 2).reshape(r, n * c)


COL_SHARDED = ("w_uq", "w_uk", "w_uv", "w_proj_attn")
TRANSPOSED = ("ffn1_w_gate", "ffn1_w_up", "ffn2_w_gate", "ffn2_w_up", "w_in")
SMALL = (("ffn1_norm", 1024), ("mix_norm", 1024), ("gate_bias", 2048), ("q_a_norm", 384), ("kv_a_norm", 256),
         ("q_head_norm", 128), ("k_head_norm", 128), ("ffn2_norm", 1024))
WEIGHT_ORDER = ("ffn1_norm", "ffn1_w_gate", "ffn1_w_up", "ffn1_w_down", "mix_norm", "w_in", "gate_bias", "q_a_norm", "w_uq",
                "kv_a_norm", "w_uk", "w_uv", "q_head_norm", "k_head_norm", "w_proj_attn", "conv_w", "w_proj_conv", "w_out",
                "ffn2_norm", "ffn2_w_gate", "ffn2_w_up", "ffn2_w_down")
MATRICES = ("ffn1_w_gate", "ffn1_w_up", "ffn1_w_down", "w_in", "w_uq", "w_uk", "w_uv", "w_proj_attn", "w_proj_conv", "w_out",
            "ffn2_w_gate", "ffn2_w_up", "ffn2_w_down")
GROUP_FFN1 = ("ffn1_w_gate", "ffn1_w_up", "ffn1_w_down")
GROUP_IN = ("w_in", "w_uq", "w_uk", "w_uv", "conv_w")
GROUP_MIX = ("w_proj_attn", "w_proj_conv", "w_out")
GROUP_FFN2 = ("ffn2_w_gate", "ffn2_w_up", "ffn2_w_down")
GROUP_MID = ("w_in", "w_uq", "w_uk", "w_uv", "w_proj_attn", "w_proj_conv", "w_out")


def _pad_lanes(a, n):
    return jnp.pad(a.reshape(1, -1), ((0, 0), (0, n - a.size)))


def kernel(x, positions, ffn1_norm, ffn1_w_gate, ffn1_w_up, ffn1_w_down, mix_norm, w_in, gate_bias, q_a_norm, w_uq, kv_a_norm, w_uk, w_uv, q_head_norm, k_head_norm, w_proj_attn, conv_w, w_proj_conv, w_out, ffn2_norm, ffn2_w_gate, ffn2_w_up, ffn2_w_down, loss_target, m_ffn1_norm, m_ffn1_w_gate, m_ffn1_w_up, m_ffn1_w_down, m_mix_norm, m_w_in, m_gate_bias, m_q_a_norm, m_w_uq, m_kv_a_norm, m_w_uk, m_w_uv, m_q_head_norm, m_k_head_norm, m_w_proj_attn, m_conv_w, m_w_proj_conv, m_w_out, m_ffn2_norm, m_ffn2_w_gate, m_ffn2_w_up, m_ffn2_w_down, v_ffn1_norm, v_ffn1_w_gate, v_ffn1_w_up, v_ffn1_w_down, v_mix_norm, v_w_in, v_gate_bias, v_q_a_norm, v_w_uq, v_kv_a_norm, v_w_uk, v_w_uv, v_q_head_norm, v_k_head_norm, v_w_proj_attn, v_conv_w, v_w_proj_conv, v_w_out, v_ffn2_norm, v_ffn2_w_gate, v_ffn2_w_up, v_ffn2_w_down):
    args = dict(locals())
    view = lambda n, a: a.T if n in TRANSPOSED else a
    weights = {n: view(n, args[n]) for n in WEIGHT_ORDER}
    moments_m = {n: view(n, args["m_" + n]) for n in WEIGHT_ORDER}
    moments_v = {n: view(n, args["v_" + n]) for n in WEIGHT_ORDER}
    nb, seq, d = x.shape
    t = nb * seq
    chip = (2 * lax.axis_index("x") + lax.axis_index("y")).astype(jnp.int32)
    place = jnp.stack([chip, lax.axis_index("c").astype(jnp.int32)])
    grads, delta, new_m, new_v = {}, {}, {}, {}

    def adamw(names, carried=None):
        results = _adamw([(weights[n], grads[n], moments_m[n], moments_v[n]) for n in names], "adamw_" + names[0], carried)
        for n, (dn, mn, vn) in zip(names, results):
            delta[n], new_m[n], new_v[n] = dn, mn, vn

    conv_rows = conv_w.shape[0]
    conv_shard = jnp.pad(conv_w, ((0, 16 - conv_rows), (0, 0)))
    assert MATRICES[:len(GROUP_FFN1)] == GROUP_FFN1
    bufs = dict(zip(MATRICES + ("conv_w",), _cast_shards([weights[n] for n in MATRICES] + [conv_shard],
                                                         [BF16] * len(MATRICES) + [F32], len(GROUP_FFN1))))
    blocks = {n: bufs[n] for n in GROUP_FFN1}
    p = {n: _pad_lanes(weights[n], size) for n, size in SMALL}
    rope = _rope_tables(positions)
    x_tok = x.reshape(t, d)

    gather_in = _gather_carried([bufs[n] for n in GROUP_IN])
    x1, gate1, up1, act1 = _ffn_fwd(x_tok, p["ffn1_norm"], blocks["ffn1_w_gate"], blocks["ffn1_w_up"], blocks["ffn1_w_down"], None,
                                    "ffn1_fwd", gather_in)
    blocks.update(zip(GROUP_IN, gather_in.results))
    w = _kernel_layouts({"w_in": blocks["w_in"].reshape(-1, d), **{n: _from_col_blocks(blocks[n]) for n in ("w_uq", "w_uk", "w_uv")}})
    p["conv_w"] = _from_col_blocks(blocks["conv_w"])[:conv_rows]

    gather_mix = _gather_carried([bufs[n] for n in GROUP_MIX + GROUP_FFN2[2:]])
    h2b, big, lat, q, k, v, vt = _inproj_fwd(x1, p["mix_norm"], w["w_in"], w["w_kr"], p["q_a_norm"], p["kv_a_norm"], p["q_head_norm"],
                                             p["k_head_norm"], w["w_uq"], w["w_uk"], w["w_uv"], w["w_uvt"], rope, gather_mix)
    blocks.update(zip(GROUP_MIX + GROUP_FFN2[2:], gather_mix.results))
    w_pa = _from_col_blocks(blocks["w_proj_attn"])
    w_pc, w_out_full = blocks["w_proj_conv"].reshape(-1, d), blocks["w_out"].reshape(-1, d)

    gather_ffn2 = _gather_carried([bufs[n] for n in GROUP_FFN2[:2]])
    o, lse = _attn_fwd(q, k, vt, seq, gather_ffn2)
    x2 = _mix_fwd(x1, o, big, p["gate_bias"], p["conv_w"], w_pa, w_pc, w_out_full, seq)
    wg2, wu2 = gather_ffn2.results
    wd2 = blocks["ffn2_w_down"]
    dx3, gate2, up2, act2, loss = _ffn_fwd(x2, p["ffn2_norm"], wg2, wu2, wd2, loss_target.reshape(t, d), "ffn2_fwd")

    dx2, dg_ffn2, hb2, dgate2, dup2, dyb2 = _ffn_bwd_x(x2, p["ffn2_norm"], dx3, gate2, up2, wg2, wu2, wd2, "ffn2_bwd")
    g_ffn2 = [_tn_matmul(dgate2, hb2, "ffn2_dw_gate"), _tn_matmul(dup2, hb2, "ffn2_dw_up"), _tn_matmul(act2, dyb2, "ffn2_dw_down")]
    swap = _swap_carried(g_ffn2)
    do, delta_o, dz, dm, dbias, dw_pa, dw_pc, dw_out = _mix_bwd(dx2, o, big, p["gate_bias"], p["conv_w"], w_pa, w_pc, w_out_full, seq,
                                                                swap)
    part = _partials(GROUP_FFN2, g_ffn2, swap.results, place)
    send = _send_carried([pb for _, pb in part])
    dq, dk, dv = _attn_bwd(q, k, v, do, lse, delta_o.reshape(N_HEADS // ATTN_BWD_HEADS, ATTN_BWD_HEADS, -1), seq, send)
    join = _join_carried(_totals(GROUP_FFN2, g_ffn2, part, send.results, place))
    dp, dw_uq, dw_uk, dw_uv, dqa, dkva, dqh, dkh, dcw = _prep_bwd(
        lat, big, dz, dq, dk, dv, p["q_a_norm"], p["kv_a_norm"], p["q_head_norm"], p["k_head_norm"], w["w_uq"], w["w_uk"],
        w["w_uv"], rope, p["conv_w"], seq, join)
    grads.update(zip(GROUP_FFN2, join.results))

    gg = _global_layouts({"w_in": _dw_in(dm, dp, h2b), "w_uq": dw_uq, "w_uk": dw_uk, "w_uv": dw_uv, "w_pa": dw_pa, "w_pc": dw_pc,
                          "w_out": dw_out})
    g_mid = [_col_blocks(gg[n]) if n in COL_SHARDED else gg[n].reshape(N_CHIPS, -1, gg[n].shape[-1]) for n in GROUP_MID]
    swap = _swap_carried(g_mid)
    dx1, dg_mix = _inproj_bwd(x1, p["mix_norm"], dx2, dm, dp, w["w_in"], w["w_kr"], swap)
    part = (_partials(GROUP_MID[:1], g_mid[:1], swap.results[:1], place)
            + _chip_partial_small(g_mid[1:], swap.results[1:], place))
    send = _send_carried([pb for _, pb in part])
    grad_x, dg_ffn1, hb1, dgate1, dup1, dyb1 = _ffn_bwd_x(x_tok, p["ffn1_norm"], dx1, gate1, up1, blocks["ffn1_w_gate"],
                                                         blocks["ffn1_w_up"], blocks["ffn1_w_down"], "ffn1_bwd", send)

    small_grads = {"ffn1_norm": dg_ffn1, "mix_norm": dg_mix, "gate_bias": dbias, "q_a_norm": dqa, "kv_a_norm": dkva,
                   "q_head_norm": dqh, "k_head_norm": dkh, "ffn2_norm": dg_ffn2}
    packed = jnp.concatenate([small_grads[n] for n, _ in SMALL] + [dcw.reshape(1, -1), loss], axis=1)
    total = _sum_devices(packed.reshape(8, -1)).reshape(1, -1)
    n_small = sum(size for _, size in SMALL)
    conv_cols = conv_w.shape[1]
    conv_total = total[:, n_small:n_small + conv_rows * d].reshape(conv_rows, d)
    grads["conv_w"] = lax.dynamic_slice_in_dim(conv_total, chip * conv_cols, conv_cols, axis=1)
    loss_total = total[0, n_small + conv_rows * d]

    join = _join_carried(_totals(GROUP_MID[:1], g_mid[:1], part[:1], send.results[:1], place)
                         + _chip_total_small([pf for pf, _ in part[1:]], send.results[1:], place, [g.shape[1:] for g in g_mid[1:]]))
    g_gate = _tn_matmul(dgate1, hb1, "ffn1_dw_gate", carried=join)
    grads.update(zip(GROUP_MID, join.results))
    swap_gate = _swap_carried([g_gate])
    g_up = _tn_matmul(dup1, hb1, "ffn1_dw_up", carried=swap_gate)
    part_gate = _partials(GROUP_FFN1[:1], [g_gate], swap_gate.results, place)
    send_gate, swap_up = _send_carried([part_gate[0][1]]), _swap_carried([g_up])
    g_down = _tn_matmul(act1, dyb1, "ffn1_dw_down", carried=_both(send_gate, swap_up))
    join_gate = _join_carried(_totals(GROUP_FFN1[:1], [g_gate], part_gate, send_gate.results, place))
    part_up = _partials(GROUP_FFN1[1:2], [g_up], swap_up.results, place)
    send_up, swap_down = _send_carried([part_up[0][1]]), _swap_carried([g_down])
    adamw(GROUP_FFN2, _both(_both(send_up, swap_down), join_gate))
    grads["ffn1_w_gate"] = join_gate.results[0]
    join_up = _join_carried(_totals(GROUP_FFN1[1:2], [g_up], part_up, send_up.results, place))
    part_down = _partials(GROUP_FFN1[2:], [g_down], swap_down.results, place)
    send_down = _send_carried([part_down[0][1]])
    adamw(("w_in",), _both(send_down, join_up))
    grads["ffn1_w_up"] = join_up.results[0]
    join_down = _join_carried(_totals(GROUP_FFN1[2:], [g_down], part_down, send_down.results, place))
    adamw(GROUP_FFN1[:2], join_down)
    grads["ffn1_w_down"] = join_down.results[0]
    adamw(GROUP_FFN1[2:])
    others = GROUP_MID[1:] + ("conv_w",)
    for n, (gn, dn, mn, vn) in zip(others, _adamw_whole([(weights[n], grads[n], moments_m[n], moments_v[n]) for n in others],
                                                        "adamw_others")):
        grads[n], delta[n], new_m[n], new_v[n] = gn, dn, mn, vn

    row = lambda a: a.reshape(1, -1)
    small = _adamw_small(total, [(row(weights[n]), row(moments_m[n]), row(moments_v[n])) for n, _ in SMALL], [size for _, size in SMALL])
    for (n, _), (gn, dn, mn, vn) in zip(SMALL, small):
        grads[n], delta[n], new_m[n], new_v[n] = gn.reshape(-1), dn.reshape(-1), mn.reshape(-1), vn.reshape(-1)

    return (loss_total, grad_x.reshape(nb, seq, d), *[view(n, src[n]) for src in (grads, delta, new_m, new_v) for n in WEIGHT_ORDER])
```

```python
import functools

import jax
import jax.numpy as jnp
from jax import lax
from jax.experimental import pallas as pl
from jax.experimental.pallas import tpu as pltpu

F32 = jnp.float32
BF16 = jnp.bfloat16

D_MODEL = 1024
N_HEADS = 8
QK_NOPE = 64
QK_ROPE = 32
QK_DIM = QK_NOPE + QK_ROPE
V_DIM = 64
HEAD_PAD = 128
Q_LORA = 384
KV_LORA = 256
ROPE_THETA = 10000.0
NORM_EPS = 1e-6
ATTN_SCALE = QK_DIM ** -0.5
MASK_VALUE = -1e30
N_CHIPS = 4
N_DEV = 8

ADAM_LR = 0.001
ADAM_B1 = 0.9
ADAM_B2 = 0.999
ADAM_EPS = 1e-08
ADAM_WD = 0.01
ADAM_STEP = 10

TOKEN_TILE = 256
WIDE_TILE = 512
ATTN_TILE = 512
TN_TILE = 2048
VMEM_LIMIT = 56 * 1024 * 1024

M_COLS = 3 * D_MODEL
P_COLS = 2 * D_MODEL + Q_LORA + KV_LORA + HEAD_PAD
BIG_COLS = 5 * D_MODEL
LAT_COLS = Q_LORA + KV_LORA + HEAD_PAD

MESH_ID = pl.DeviceIdType.MESH
ANY = pl.BlockSpec(memory_space=pl.ANY)


def _params(semantics=None):
    return pltpu.CompilerParams(dimension_semantics=semantics, vmem_limit_bytes=VMEM_LIMIT)


class _Carried:
    def __init__(self, operands, out_shapes, aliases, n_sems, start, finish):
        self.operands, self.out_shapes, self.aliases, self.n_sems = list(operands), list(out_shapes), dict(aliases), n_sems
        self.start, self.finish = start, finish
        self.results = None


def _both(a, b):
    na, nao = len(a.operands), len(a.out_shapes)

    def start(ins, outs, sems, base):
        a.start(ins[:na], outs[:nao], sems, base)
        b.start(ins[na:], outs[nao:], sems, base + a.n_sems)

    def finish(ins, outs, sems, base):
        a.finish(ins[:na], outs[:nao], sems, base)
        b.finish(ins[na:], outs[nao:], sems, base + a.n_sems)

    aliases = dict(a.aliases)
    aliases.update({na + i: nao + o for i, o in b.aliases.items()})
    both = _Carried(a.operands + b.operands, a.out_shapes + b.out_shapes, aliases, a.n_sems + b.n_sems, start, finish)
    both.parts = (a, b)
    return both


def _set_results(carried, results):
    carried.results = list(results)
    if hasattr(carried, "parts"):
        a, b = carried.parts
        _set_results(a, results[:len(a.out_shapes)])
        _set_results(b, results[len(a.out_shapes):])


def _pallas(body, name, grid, in_specs, out_specs, out_shape, args, semantics, carried=None):
    if carried is None:
        return pl.pallas_call(body, name=name, grid=grid, in_specs=in_specs, out_specs=out_specs, out_shape=out_shape,
                              compiler_params=_params(semantics))(*args)
    n_in, n_out, n_ci, n_co = len(in_specs), len(out_specs), len(carried.operands), len(carried.out_shapes)

    def wrapped(*refs):
        ins, c_ins = refs[:n_in], refs[n_in:n_in + n_ci]
        outs, c_outs = refs[n_in + n_ci:n_in + n_ci + n_out], refs[n_in + n_ci + n_out:n_in + n_ci + n_out + n_co]
        sems = refs[-1]
        first = pl.program_id(0) == 0
        last = pl.program_id(0) == grid[0] - 1
        for axis in range(1, len(grid)):
            first = jnp.logical_and(first, pl.program_id(axis) == 0)
            last = jnp.logical_and(last, pl.program_id(axis) == grid[axis] - 1)

        @pl.when(first)
        def _():
            carried.start(c_ins, c_outs, sems, 0)

        body(*ins, *outs)

        @pl.when(last)
        def _():
            carried.finish(c_ins, c_outs, sems, 0)

    results = pl.pallas_call(
        wrapped, name=name, grid=grid, in_specs=list(in_specs) + [ANY] * n_ci, out_specs=list(out_specs) + [ANY] * n_co,
        out_shape=list(out_shape) + carried.out_shapes,
        input_output_aliases={n_in + i: n_out + o for i, o in carried.aliases.items()},
        scratch_shapes=[pltpu.SemaphoreType.DMA((carried.n_sems,))], compiler_params=_params(semantics))(*args, *carried.operands)
    _set_results(carried, results[n_out:])
    return results[:n_out]


def _resident(shape):
    nd = len(shape)
    return pl.BlockSpec(shape, lambda *_: (0,) * nd, pipeline_mode=pl.Buffered(1))


def _const(shape):
    nd = len(shape)
    return pl.BlockSpec(shape, lambda *_: (0,) * nd)


def _mm(a, b):
    return jnp.dot(a, b, preferred_element_type=F32)


def _mm_nt(a, b):
    return lax.dot_general(a, b, (((1,), (1,)), ((), ())), preferred_element_type=F32)


def _mm_tn(a, b):
    return lax.dot_general(a, b, (((0,), (0,)), ((), ())), preferred_element_type=F32)


def _bf(a):
    return a.astype(BF16)


def _sigmoid(a):
    return 1.0 / (1.0 + jnp.exp(-a))


def _rms(x, gain, n=None):
    n = x.shape[-1] if n is None else n
    r = lax.rsqrt(jnp.sum(x * x, axis=-1, keepdims=True) * (1.0 / n) + NORM_EPS)
    return (x * r) * gain, r


def _rms_bwd(x, r, gain, dh, n=None):
    n = x.shape[-1] if n is None else n
    u = dh * gain
    dx = r * u - x * ((r * r * r) * (jnp.sum(u * x, axis=-1, keepdims=True) * (1.0 / n)))
    dgain = jnp.sum(dh * (x * r), axis=0, keepdims=True)
    return dx, dgain


ROPE_HALF = QK_ROPE // 2


def _rope(t, rope):
    cos, s_lo, s_hi = rope
    return t * cos + pltpu.roll(t, HEAD_PAD - ROPE_HALF, 1) * s_lo + pltpu.roll(t, ROPE_HALF, 1) * s_hi


def _rope_bwd(dt, rope):
    cos, s_lo, s_hi = rope
    return dt * cos + pltpu.roll(dt * s_lo, ROPE_HALF, 1) + pltpu.roll(dt * s_hi, HEAD_PAD - ROPE_HALF, 1)


def _shift_down(u, prev8, k):
    s = pltpu.roll(u, k, 0)
    p = pltpu.roll(prev8, k, 0)
    row = lax.broadcasted_iota(jnp.int32, prev8.shape, 0)
    top = jnp.where(row < k, p, s[:8])
    return jnp.concatenate([top, s[8:]], axis=0)


def _shift_up(d, next8, k):
    tm = d.shape[0]
    s = pltpu.roll(d, tm - k, 0)
    n = pltpu.roll(next8, 8 - k, 0)
    row = lax.broadcasted_iota(jnp.int32, next8.shape, 0)
    bot = jnp.where(row >= 8 - k, n, s[tm - 8:])
    return jnp.concatenate([s[:tm - 8], bot], axis=0)


def _ffn_fwd(x, gain, wg, wu, wd, target, name, carried=None):
    t, d = x.shape
    nb, f, _ = wg.shape
    tm = TOKEN_TILE
    with_loss = target is not None

    def body(*refs):
        if with_loss:
            x_ref, g_ref, wg_ref, wu_ref, wd_ref, t_ref, out_ref, gate_ref, up_ref, act_ref, loss_ref = refs
        else:
            x_ref, g_ref, wg_ref, wu_ref, wd_ref, out_ref, gate_ref, up_ref, act_ref = refs
        xv = x_ref[...]
        h, _ = _rms(xv, g_ref[...])
        hb = _bf(h)
        y = jnp.zeros((tm, d), F32)
        nxt = (_mm_nt(hb, wg_ref[0]), _mm_nt(hb, wu_ref[0]))
        for j in range(nb):
            gate, up = nxt
            if j + 1 < nb:
                nxt = (_mm_nt(hb, wg_ref[j + 1]), _mm_nt(hb, wu_ref[j + 1]))
            act = _bf((gate * _sigmoid(gate)) * up)
            y = y + _mm(act, wd_ref[j])
            gate_ref[j] = _bf(gate)
            up_ref[j] = _bf(up)
            act_ref[j] = act
        out = xv + 0.5 * y
        if with_loss:
            err = out - t_ref[...]
            out_ref[...] = err * (1.0 / d)

            @pl.when(pl.program_id(0) == 0)
            def _():
                loss_ref[...] = jnp.zeros_like(loss_ref)

            part = jnp.sum(jnp.sum(err * err, axis=1, keepdims=True), axis=0, keepdims=True)
            loss_ref[...] += jnp.broadcast_to(part * (0.5 / d), loss_ref.shape)
        else:
            out_ref[...] = out

    tok = pl.BlockSpec((tm, d), lambda i: (i, 0))
    blk = pl.BlockSpec((nb, tm, f), lambda i: (0, i, 0))
    in_specs = [tok, _const((1, d)), _resident(wg.shape), _resident(wu.shape), _resident(wd.shape)]
    args = [x, gain, wg, wu, wd]
    out_shape = [jax.ShapeDtypeStruct((t, d), F32)] + [jax.ShapeDtypeStruct((nb, t, f), BF16)] * 3
    out_specs = [tok, blk, blk, blk]
    if with_loss:
        in_specs.append(tok)
        args.append(target)
        out_shape.append(jax.ShapeDtypeStruct((1, 128), F32))
        out_specs.append(_const((1, 128)))
    return _pallas(body, name, (t // tm,), in_specs, out_specs, out_shape, args, ("arbitrary",), carried)


def _ffn_bwd_x(x, gain, dout, gate, up, wg, wu, wd, name, carried=None):
    t, d = x.shape
    nb, f, _ = wg.shape
    tm = TOKEN_TILE

    def body(x_ref, g_ref, dout_ref, gate_ref, up_ref, wg_ref, wu_ref, wd_ref,
             dx_ref, dgain_ref, hb_ref, dgate_ref, dup_ref, dyb_ref):
        xv = x_ref[...]
        gain_v = g_ref[...]
        h, r = _rms(xv, gain_v)
        hb_ref[...] = _bf(h)
        dout_v = dout_ref[...]
        dyb = _bf(0.5 * dout_v)
        dyb_ref[...] = dyb
        dh = jnp.zeros((tm, d), F32)
        nxt = _mm_nt(dyb, wd_ref[0])
        for j in range(nb):
            dact = nxt
            if j + 1 < nb:
                nxt = _mm_nt(dyb, wd_ref[j + 1])
            gt = gate_ref[j].astype(F32)
            uv = up_ref[j].astype(F32)
            s = _sigmoid(gt)
            dup = _bf(dact * (gt * s))
            dgate = _bf((dact * uv) * (s * (1.0 + gt * (1.0 - s))))
            dh = dh + _mm(dgate, wg_ref[j]) + _mm(dup, wu_ref[j])
            dgate_ref[j] = dgate
            dup_ref[j] = dup
        dxn, dgain = _rms_bwd(xv, r, gain_v, dh)
        dx_ref[...] = dout_v + dxn

        @pl.when(pl.program_id(0) == 0)
        def _():
            dgain_ref[...] = jnp.zeros_like(dgain_ref)

        dgain_ref[...] += dgain

    tok = pl.BlockSpec((tm, d), lambda i: (i, 0))
    blk = pl.BlockSpec((nb, tm, f), lambda i: (0, i, 0))
    return _pallas(
        body, name, (t // tm,),
        [tok, _const((1, d)), tok, blk, blk, _resident(wg.shape), _resident(wu.shape), _resident(wd.shape)],
        [tok, _const((1, d)), tok, blk, blk, tok],
        [jax.ShapeDtypeStruct((t, d), F32), jax.ShapeDtypeStruct((1, d), F32), jax.ShapeDtypeStruct((t, d), BF16),
         jax.ShapeDtypeStruct((nb, t, f), BF16), jax.ShapeDtypeStruct((nb, t, f), BF16), jax.ShapeDtypeStruct((t, d), BF16)],
        (x, gain, dout, gate, up, wg, wu, wd), ("arbitrary",), carried)


def _tn_matmul(a, b, name, carried=None):
    t = a.shape[-2]
    k = a.shape[-1]
    n = b.shape[-1]
    tt = min(2 * TN_TILE, t)
    nt = t // tt

    def body(a_ref, b_ref, o_ref):
        if nt == 1:
            o_ref[...] = _mm_tn(a_ref[...], b_ref[...])
            return

        @pl.when(pl.program_id(1) == 0)
        def _():
            o_ref[...] = jnp.zeros_like(o_ref)

        o_ref[...] += _mm_tn(a_ref[...], b_ref[...])

    g = a.shape[0] if a.ndim == 3 else b.shape[0]
    a_spec = (pl.BlockSpec((None, tt, k), lambda gi, ti: (gi, ti, 0)) if a.ndim == 3
              else pl.BlockSpec((tt, k), lambda gi, ti: (ti, 0)))
    b_spec = (pl.BlockSpec((None, tt, n), lambda gi, ti: (gi, ti, 0)) if b.ndim == 3
              else pl.BlockSpec((tt, n), lambda gi, ti: (ti, 0)))
    o_spec = pl.BlockSpec((None, k, n), lambda gi, ti: (gi, 0, 0))
    out_shape = jax.ShapeDtypeStruct((g, k, n), F32)
    return _pallas(body, name, (g, nt), [a_spec, b_spec], [o_spec], [out_shape], (a, b), ("arbitrary", "arbitrary"), carried)[0]


def _tn_rows(a, b, out, chunks, width, rows_out, name):
    t, n = b.shape
    tt = min(TN_TILE, t)
    nt = t // tt

    def body(blocks_ref, a_ref, b_ref, *rest):
        out_ref, acc, sem = rest[-3:]
        g, ti = pl.program_id(0), pl.program_id(1)

        @pl.when(ti == 0)
        def _():
            acc[...] = jnp.zeros_like(acc)

        acc[...] += _mm_tn(a_ref[...], b_ref[...])
        for gi, (_, ranges) in enumerate(chunks):
            @pl.when(jnp.logical_and(g == gi, ti == nt - 1))
            def _(ranges=ranges):
                for row, first, count in ranges:
                    cp = pltpu.make_async_copy(acc.at[first:first + count], out_ref.at[row:row + count], sem)
                    cp.start()
                    cp.wait()

    blocks = jnp.asarray([c[0] for c in chunks], jnp.int32)
    grid_spec = pltpu.PrefetchScalarGridSpec(
        num_scalar_prefetch=1, grid=(len(chunks), nt),
        in_specs=[pl.BlockSpec((tt, width), lambda g, ti, blocks_ref: (ti, blocks_ref[g])),
                  pl.BlockSpec((tt, n), lambda g, ti, blocks_ref: (ti, 0))] + ([ANY] if out is not None else []),
        out_specs=ANY, scratch_shapes=[pltpu.VMEM((width, n), F32), pltpu.SemaphoreType.DMA])
    args = (blocks, a, b) + ((out,) if out is not None else ())
    return pl.pallas_call(body, name=name, grid_spec=grid_spec, out_shape=jax.ShapeDtypeStruct((rows_out, n), F32),
                          input_output_aliases={3: 0} if out is not None else {},
                          compiler_params=_params(("arbitrary", "arbitrary")))(*args)


ROW_QKV, ROW_KR, ROW_XC = 0, Q_LORA + KV_LORA, Q_LORA + KV_LORA + QK_ROPE
ROW_GB, ROW_GC, ROW_GL = ROW_XC + D_MODEL, ROW_XC + 2 * D_MODEL, ROW_XC + 3 * D_MODEL
BIG_FROM_ROWS = ((0, ROW_GB, D_MODEL), (D_MODEL, ROW_GL, 2 * D_MODEL), (3 * D_MODEL, ROW_XC, D_MODEL), (4 * D_MODEL, ROW_GC, D_MODEL))


def _inproj_fwd(x1, gain, w_in, w_kr, qa_gain, kva_gain, qh_gain, kh_gain, w_uq, w_uk, w_uv, w_uvt, rope, carried=None):
    t, d = x1.shape
    tm = TOKEN_TILE
    chunk = 512
    chunks = []
    for col, row, size in BIG_FROM_ROWS:
        chunks += [(col + o, row + o, chunk) for o in range(0, size, chunk)]
    of_head = [[c for k, c in enumerate(chunks) if k * N_HEADS // len(chunks) == hd] for hd in range(N_HEADS)]

    def body(x_ref, g_ref, win_ref, wkr_ref, qa_ref, kva_ref, qh_ref, kh_ref, wuq_ref, wuk_ref, wuv_ref, wuvt_ref, cos_ref, slo_ref,
             shi_ref, hb_ref, big_ref, lat_ref, q_ref, k_ref, v_ref, vt_ref):
        h, _ = _rms(x_ref[...], g_ref[...])
        hb = _bf(h)
        hb_ref[...] = hb
        k_rope = _mm_nt(hb, wkr_ref[...])
        lat = jnp.concatenate([_mm_nt(hb, win_ref[ROW_QKV:ROW_KR, :]), k_rope], axis=1)
        lat_ref[...] = lat
        cq, _ = _rms(lat[:, :Q_LORA], qa_ref[...])
        ckv, _ = _rms(lat[:, Q_LORA:Q_LORA + KV_LORA], kva_ref[...])
        cqb = _bf(cq)
        ckvb = _bf(ckv)
        rope_v = (cos_ref[...], slo_ref[...], shi_ref[...])
        q_all = _mm(cqb, wuq_ref[...])
        k_all = _mm(ckvb, wuk_ref[...])
        v_ref[...] = _bf(_mm(ckvb, wuv_ref[...]))
        vt_all = _mm_nt(wuvt_ref[...], ckvb)
        for hd in range(N_HEADS):
            for col, row, size in of_head[hd]:
                big_ref[:, col:col + size] = _mm_nt(hb, win_ref[row:row + size, :])
            lanes = slice(hd * HEAD_PAD, (hd + 1) * HEAD_PAD)
            qn, _ = _rms(q_all[:, lanes], qh_ref[...], QK_DIM)
            q_ref[hd] = _bf(_rope(qn, rope_v))
            kn, _ = _rms(k_all[:, lanes] + k_rope, kh_ref[...], QK_DIM)
            k_ref[hd] = _bf(_rope(kn, rope_v))
            vt_ref[hd] = _bf(vt_all[hd * V_DIM:(hd + 1) * V_DIM])

    tok = lambda c: pl.BlockSpec((tm, c), lambda i: (i, 0))
    head = lambda c: pl.BlockSpec((N_HEADS, tm, c), lambda i: (0, i, 0))
    return _pallas(
        body, "inproj_fwd", (t // tm,),
        [tok(d), _const((1, d)), _resident(w_in.shape), _resident(w_kr.shape), _const((1, Q_LORA)), _const((1, KV_LORA)),
         _const((1, HEAD_PAD)), _const((1, HEAD_PAD)), _resident(w_uq.shape), _resident(w_uk.shape),
         _resident(w_uv.shape), _resident(w_uvt.shape), tok(HEAD_PAD), tok(HEAD_PAD), tok(HEAD_PAD)],
        [tok(d), tok(BIG_COLS), tok(LAT_COLS), head(HEAD_PAD), head(HEAD_PAD), tok(N_HEADS * V_DIM),
         pl.BlockSpec((N_HEADS, V_DIM, tm), lambda i: (0, 0, i))],
        [jax.ShapeDtypeStruct((t, d), BF16), jax.ShapeDtypeStruct((t, BIG_COLS), F32),
         jax.ShapeDtypeStruct((t, LAT_COLS), F32), jax.ShapeDtypeStruct((N_HEADS, t, HEAD_PAD), BF16),
         jax.ShapeDtypeStruct((N_HEADS, t, HEAD_PAD), BF16), jax.ShapeDtypeStruct((t, N_HEADS * V_DIM), BF16),
         jax.ShapeDtypeStruct((N_HEADS, V_DIM, t), BF16)],
        (x1, gain, w_in, w_kr, qa_gain, kva_gain, qh_gain, kh_gain, w_uq, w_uk, w_uv, w_uvt, *rope), ("arbitrary",), carried)


EXP2_SCALE = ATTN_SCALE * 1.4426950408889634


def _diagonal_keep(tk, tq):
    return lax.broadcasted_iota(jnp.int32, (tk, tq), 0) <= lax.broadcasted_iota(jnp.int32, (tk, tq), 1)


def _attn_fwd(q, k, vt, seq, carried=None):
    _, t, _ = q.shape
    nseq = t // seq
    tq = tk = ATTN_TILE
    nq = seq // tq

    def body(q_ref, k_ref, vt_ref, o_ref, lse_ref):
        i = pl.program_id(1)
        qs = [q_ref[h] for h in range(N_HEADS)]
        keep = _diagonal_keep(tk, tq)

        def scores(h, k0):
            return _mm_nt(k_ref[h, pl.ds(k0, tk), :], qs[h])

        def update(h, st, state, k0, diagonal):
            m, l, acc = state
            if diagonal:
                st = jnp.where(keep, st, MASK_VALUE)
            m_new = jnp.maximum(m, jnp.max(st, axis=0, keepdims=True))
            pt = jnp.exp2((st - m_new) * EXP2_SCALE)
            alpha = jnp.exp2((m - m_new) * EXP2_SCALE)
            l_new = alpha * l + jnp.sum(pt, axis=0, keepdims=True)
            return m_new, l_new, alpha * acc + _mm(vt_ref[h, :, pl.ds(k0, tk)], _bf(pt))

        def tiles(states, k0, diagonal):
            st, new = scores(0, k0), []
            for h in range(N_HEADS):
                st_next = scores(h + 1, k0) if h + 1 < N_HEADS else None
                new.append(update(h, st, states[h], k0, diagonal))
                st = st_next
            return tuple(new)

        init = tuple((jnp.full((1, tq), MASK_VALUE, F32), jnp.zeros((1, tq), F32), jnp.zeros((V_DIM, tq), F32))
                     for _ in range(N_HEADS))
        states = lax.fori_loop(0, i, lambda j, s: tiles(s, pl.multiple_of(j * tk, tk), False), init)
        states = tiles(states, pl.multiple_of(i * tk, tk), True)
        outs = []
        for h in range(N_HEADS):
            m, l, acc = states[h]
            outs.append((acc / l).T)
            lse_ref[h] = m * EXP2_SCALE + jnp.log2(l)
        o_ref[...] = _bf(jnp.concatenate(outs, axis=-1))

    return _pallas(
        body, "attn_fwd", (nseq, nq),
        [pl.BlockSpec((N_HEADS, tq, HEAD_PAD), lambda b, i: (0, b * nq + i, 0)),
         pl.BlockSpec((N_HEADS, seq, HEAD_PAD), lambda b, i: (0, b, 0)),
         pl.BlockSpec((N_HEADS, V_DIM, seq), lambda b, i: (0, 0, b))],
        [pl.BlockSpec((tq, N_HEADS * V_DIM), lambda b, i: (b * nq + i, 0)),
         pl.BlockSpec((N_HEADS, 1, tq), lambda b, i: (0, 0, b * nq + i))],
        [jax.ShapeDtypeStruct((t, N_HEADS * V_DIM), BF16), jax.ShapeDtypeStruct((N_HEADS, 1, t), F32)],
        (q, k, vt), ("arbitrary", "arbitrary"), carried)


ATTN_BWD_HEADS = 4


def _attn_bwd(q, k, v, do, lse, delta, seq, carried=None):
    _, t, _ = q.shape
    nseq = t // seq
    tq = tk = ATTN_TILE
    n = seq // tq
    hb = ATTN_BWD_HEADS

    def body(q_ref, k_ref, v_ref, do_ref, lse_ref, delta_ref, dq_ref, dk_ref, dv_ref):
        dq_ref[...] = jnp.zeros_like(dq_ref)
        dk_ref[...] = jnp.zeros_like(dk_ref)
        dv_ref[...] = jnp.zeros_like(dv_ref)
        keep = _diagonal_keep(tk, tq)

        def tile(h, k0, q0, diagonal):
            kj = k_ref[h, pl.ds(k0, tk), :]
            qi = q_ref[h, pl.ds(q0, tq), :]
            doi = _bf(do_ref[pl.ds(q0, tq), h * V_DIM:(h + 1) * V_DIM])
            st = _mm_nt(kj, qi)
            if diagonal:
                st = jnp.where(keep, st, MASK_VALUE)
            pt = jnp.exp2(st * EXP2_SCALE - lse_ref[h, :, pl.ds(q0, tq)])
            dv_ref[pl.ds(k0, tk), h * V_DIM:(h + 1) * V_DIM] += _mm(_bf(pt), doi)
            dpt = _mm_nt(v_ref[pl.ds(k0, tk), h * V_DIM:(h + 1) * V_DIM], doi)
            dst = _bf((pt * (dpt - delta_ref[pl.ds(h, 1), pl.ds(q0, tq)])) * ATTN_SCALE)
            dk_ref[h, pl.ds(k0, tk), :] += _mm(dst, qi)
            dq_ref[h, pl.ds(q0, tq), :] += _mm_tn(dst, kj)

        def kv_step(j, _):
            k0 = pl.multiple_of(j * tk, tk)
            for h in range(hb):
                tile(h, k0, k0, True)

            def q_step(i, _):
                q0 = pl.multiple_of(i * tq, tq)
                for h in range(hb):
                    tile(h, k0, q0, False)
                return 0

            lax.fori_loop(j + 1, n, q_step, 0)
            return 0

        lax.fori_loop(0, n, kv_step, 0)

    hspec = lambda c: pl.BlockSpec((hb, seq, c), lambda b, g: (g, b, 0))
    cols = pl.BlockSpec((seq, hb * V_DIM), lambda b, g: (b, g))
    return _pallas(
        body, "attn_bwd", (nseq, N_HEADS // hb),
        [hspec(HEAD_PAD), hspec(HEAD_PAD), cols, cols,
         pl.BlockSpec((hb, 1, seq), lambda b, g: (g, 0, b)), pl.BlockSpec((None, hb, seq), lambda b, g: (g, 0, b))],
        [hspec(HEAD_PAD), hspec(HEAD_PAD), cols],
        [jax.ShapeDtypeStruct((N_HEADS, t, HEAD_PAD), F32), jax.ShapeDtypeStruct((N_HEADS, t, HEAD_PAD), F32),
         jax.ShapeDtypeStruct((t, N_HEADS * V_DIM), F32)],
        (q, k, v, do, lse, delta), ("arbitrary", "arbitrary"), carried)


def _merged_mixers(o_ref, gb_ref, gla_ref, glb_ref, xc_ref, gc_ref, xcp_ref, gcp_ref, bias_ref, cw_ref, wpa_ref, wpc_ref,
                   first_of_seq):
    y_a = _mm(o_ref[...], wpa_ref[...])
    gb = gb_ref[...]
    u = gc_ref[...] * xc_ref[...]
    u_prev = jnp.where(first_of_seq, 0.0, gcp_ref[...] * xcp_ref[...])
    cw = cw_ref[...]
    z = cw[2:3] * u + cw[1:2] * _shift_down(u, u_prev, 1) + cw[0:1] * _shift_down(u, u_prev, 2)
    gbz = _bf(gb * z)
    y_b = _mm(gbz, wpc_ref[...])
    bias = bias_ref[...]
    gate_a = _sigmoid(gla_ref[...] + bias[:, :D_MODEL])
    gate_b = _sigmoid(glb_ref[...] + bias[:, D_MODEL:])
    return _bf(gate_a * y_a + gate_b * y_b)


def _mixer_specs(tm, seq):
    d = D_MODEL
    tok = pl.BlockSpec((tm, d), lambda i: (i, 0))
    col = lambda c: pl.BlockSpec((tm, d), lambda i: (i, c))
    prev = lambda c: pl.BlockSpec((8, d), lambda i: (jnp.maximum(i * (tm // 8) - 1, 0), c))
    o_spec = pl.BlockSpec((tm, N_HEADS * V_DIM), lambda i: (i, 0))
    fwd_specs = [o_spec, col(0), col(1), col(2), col(3), col(4), prev(3), prev(4), _const((1, 2 * d)), _const((3, d)),
                 _resident((N_HEADS * V_DIM, d)), _resident((d, d)), _resident((d, d))]
    return tok, fwd_specs


def _mix_fwd(x1, o, big, gate_bias, conv_w, w_pa, w_pc, w_out, seq, carried=None):
    t, d = x1.shape
    tm = WIDE_TILE
    tiles_per_seq = seq // tm

    def body(x_ref, o_ref, gb_ref, gla_ref, glb_ref, xc_ref, gc_ref, xcp_ref, gcp_ref, bias_ref, cw_ref, wpa_ref, wpc_ref,
             wout_ref, x2_ref):
        first = pl.program_id(0) % tiles_per_seq == 0
        merged = _merged_mixers(o_ref, gb_ref, gla_ref, glb_ref, xc_ref, gc_ref, xcp_ref, gcp_ref, bias_ref, cw_ref, wpa_ref,
                                wpc_ref, first)
        x2_ref[...] = x_ref[...] + _mm(merged, wout_ref[...])

    tok, fwd_specs = _mixer_specs(tm, seq)
    return _pallas(body, "mix_fwd", (t // tm,), [tok] + fwd_specs, [tok], [jax.ShapeDtypeStruct((t, d), F32)],
                   (x1, o, big, big, big, big, big, big, big, gate_bias, conv_w, w_pa, w_pc, w_out), ("arbitrary",), carried)[0]


def _mix_bwd(dx2, o, big, gate_bias, conv_w, w_pa, w_pc, w_out, seq, carried=None):
    t, d = dx2.shape
    tm = TOKEN_TILE
    tiles_per_seq = seq // tm
    hv = N_HEADS * V_DIM

    def body(dx_ref, o_ref, gb_ref, gla_ref, glb_ref, xc_ref, gc_ref, xcp_ref, gcp_ref, bias_ref, cw_ref, wpa_ref, wpc_ref,
             wout_ref, do_ref, delta_ref, dz_ref, dm_ref, dbias_ref, dwpa_ref, dwpc_ref, dwout_ref):
        @pl.when(pl.program_id(0) == 0)
        def _():
            dbias_ref[...] = jnp.zeros_like(dbias_ref)
            dwpa_ref[...] = jnp.zeros_like(dwpa_ref)
            dwpc_ref[...] = jnp.zeros_like(dwpc_ref)
            dwout_ref[...] = jnp.zeros_like(dwout_ref)

        first = pl.program_id(0) % tiles_per_seq == 0
        dxb = _bf(dx_ref[...])
        dmerged = _mm_nt(dxb, wout_ref[...])
        y_a = _mm(o_ref[...], wpa_ref[...])
        bias = bias_ref[...]
        gate_a = _sigmoid(gla_ref[...] + bias[:, :d])
        gate_b = _sigmoid(glb_ref[...] + bias[:, d:])
        dya = _bf(dmerged * gate_a)
        dyb = _bf(dmerged * gate_b)
        do_v = _mm_nt(dya, wpa_ref[...])
        dgz = _mm_nt(dyb, wpc_ref[...])
        dwpa_ref[...] += _mm_tn(o_ref[...], dya)
        gb = gb_ref[...]
        u = gc_ref[...] * xc_ref[...]
        u_prev = jnp.where(first, 0.0, gcp_ref[...] * xcp_ref[...])
        cw = cw_ref[...]
        z = cw[2:3] * u + cw[1:2] * _shift_down(u, u_prev, 1) + cw[0:1] * _shift_down(u, u_prev, 2)
        gbz = _bf(gb * z)
        y_b = _mm(gbz, wpc_ref[...])
        dwpc_ref[...] += _mm_tn(gbz, dyb)
        do_ref[...] = do_v
        head = lax.broadcasted_iota(jnp.int32, (N_HEADS, hv), 0) * V_DIM
        col = lax.broadcasted_iota(jnp.int32, (N_HEADS, hv), 1)
        in_head = ((col >= head) & (col < head + V_DIM)).astype(F32)
        delta_ref[...] = lax.dot_general(in_head, do_v * o_ref[...].astype(F32), (((1,), (1,)), ((), ())),
                                         precision=lax.Precision.HIGHEST, preferred_element_type=F32)
        dz_ref[...] = dgz * gb
        dm_ref[:, :d] = _bf(dgz * z)
        merged = _bf(gate_a * y_a + gate_b * y_b)
        dwout_ref[...] += _mm_tn(merged, dxb)
        dla = (dmerged * y_a) * (gate_a * (1.0 - gate_a))
        dlb = (dmerged * y_b) * (gate_b * (1.0 - gate_b))
        dbias_ref[:, :d] += jnp.sum(dla, axis=0, keepdims=True)
        dbias_ref[:, d:] += jnp.sum(dlb, axis=0, keepdims=True)
        dm_ref[:, d:2 * d] = _bf(dla)
        dm_ref[:, 2 * d:] = _bf(dlb)

    tok, fwd_specs = _mixer_specs(tm, seq)
    return _pallas(
        body, "mix_bwd", (t // tm,), [tok] + fwd_specs,
        [pl.BlockSpec((tm, hv), lambda i: (i, 0)), pl.BlockSpec((N_HEADS, tm), lambda i: (0, i)), tok,
         pl.BlockSpec((tm, M_COLS), lambda i: (i, 0)), _const((1, 2 * d)), _const((hv, d)), _const((d, d)), _const((d, d))],
        [jax.ShapeDtypeStruct((t, hv), F32), jax.ShapeDtypeStruct((N_HEADS, t), F32), jax.ShapeDtypeStruct((t, d), F32),
         jax.ShapeDtypeStruct((t, M_COLS), BF16), jax.ShapeDtypeStruct((1, 2 * d), F32), jax.ShapeDtypeStruct((hv, d), F32),
         jax.ShapeDtypeStruct((d, d), F32), jax.ShapeDtypeStruct((d, d), F32)],
        (dx2, o, big, big, big, big, big, big, big, gate_bias, conv_w, w_pa, w_pc, w_out), ("arbitrary",), carried)


def _prep_bwd(lat, big, dz, dq, dk, dv, qa_gain, kva_gain, qh_gain, kh_gain, w_uq, w_uk, w_uv, rope, conv_w, seq, carried=None):
    t = lat.shape[0]
    d = D_MODEL
    tm = WIDE_TILE
    tiles_per_seq = seq // tm
    last_blk = t // 8 - 1

    def body(lat_ref, xc_ref, gc_ref, dz_ref, dzn_ref, dq_ref, dk_ref, dv_ref, qa_ref, kva_ref, qh_ref, kh_ref, wuq_ref, wuk_ref,
             wuv_ref, cos_ref, slo_ref, shi_ref, cw_ref,
             dp_ref, dwuq_ref, dwuk_ref, dwuv_ref, dqa_ref, dkva_ref, dqh_ref, dkh_ref, dcw_ref):
        pid = pl.program_id(0)

        @pl.when(pid == 0)
        def _():
            for r in (dwuq_ref, dwuk_ref, dwuv_ref, dqa_ref, dkva_ref, dqh_ref, dkh_ref, dcw_ref):
                r[...] = jnp.zeros_like(r)

        lat_v = lat_ref[...]
        q_lat = lat_v[:, :Q_LORA]
        kv_lat = lat_v[:, Q_LORA:Q_LORA + KV_LORA]
        k_rope = lat_v[:, Q_LORA + KV_LORA:]
        qa_gain_v = qa_ref[...]
        kva_gain_v = kva_ref[...]
        qh_gain_v = qh_ref[...]
        kh_gain_v = kh_ref[...]
        cq, rq = _rms(q_lat, qa_gain_v)
        ckv, rkv = _rms(kv_lat, kva_gain_v)
        cqb = _bf(cq)
        ckvb = _bf(ckv)
        rope_v = (cos_ref[...], slo_ref[...], shi_ref[...])
        lane = lax.broadcasted_iota(jnp.int32, (tm, HEAD_PAD), 1)
        rope_lanes = (lane >= QK_NOPE) & (lane < QK_DIM)
        dk_rope = jnp.zeros((tm, HEAD_PAD), F32)
        dqh_gain = jnp.zeros((1, HEAD_PAD), F32)
        dkh_gain = jnp.zeros((1, HEAD_PAD), F32)
        q_all = _mm(cqb, wuq_ref[...])
        k_all = _mm(ckvb, wuk_ref[...])
        dvb = _bf(dv_ref[...])
        dckv = _mm_nt(dvb, wuv_ref[...])
        dwuv_ref[...] += _mm_tn(ckvb, dvb)

        last = pid % tiles_per_seq == tiles_per_seq - 1
        dzv = dz_ref[...]
        dz_next = jnp.where(last, 0.0, dzn_ref[...])
        dz1 = _shift_up(dzv, dz_next, 1)
        dz2 = _shift_up(dzv, dz_next, 2)
        cw = cw_ref[...]
        xc = xc_ref[...]
        gc = gc_ref[...]
        u = gc * xc
        du = cw[2:3] * dzv + cw[1:2] * dz1 + cw[0:1] * dz2
        dp_ref[:, :d] = _bf(du * gc)
        dp_ref[:, d:2 * d] = _bf(du * xc)
        dcw_ref[0:1, :] += jnp.sum(dz2 * u, axis=0, keepdims=True)
        dcw_ref[1:2, :] += jnp.sum(dz1 * u, axis=0, keepdims=True)
        dcw_ref[2:3, :] += jnp.sum(dzv * u, axis=0, keepdims=True)

        dcq = jnp.zeros((tm, Q_LORA), F32)
        half = N_HEADS // 2
        for part in range(2):
            dq_heads, dk_heads = [], []
            for hd in range(part * half, (part + 1) * half):
                lanes = slice(hd * HEAD_PAD, (hd + 1) * HEAD_PAD)
                q_pre = q_all[:, lanes]
                _, rr = _rms(q_pre, qh_gain_v, QK_DIM)
                dq_pre, dg = _rms_bwd(q_pre, rr, qh_gain_v, _rope_bwd(dq_ref[hd], rope_v), QK_DIM)
                dqh_gain = dqh_gain + dg
                dq_heads.append(_bf(dq_pre))

                k_pre = k_all[:, lanes] + k_rope
                _, rr = _rms(k_pre, kh_gain_v, QK_DIM)
                dk_pre, dg = _rms_bwd(k_pre, rr, kh_gain_v, _rope_bwd(dk_ref[hd], rope_v), QK_DIM)
                dkh_gain = dkh_gain + dg
                dk_rope = dk_rope + jnp.where(rope_lanes, dk_pre, 0.0)
                dk_heads.append(_bf(dk_pre))
            dq_part = jnp.concatenate(dq_heads, axis=1)
            dk_part = jnp.concatenate(dk_heads, axis=1)
            cols = slice(part * half * HEAD_PAD, (part + 1) * half * HEAD_PAD)
            dcq = dcq + _mm_nt(dq_part, wuq_ref[:, cols])
            dckv = dckv + _mm_nt(dk_part, wuk_ref[:, cols])
            dwuq_ref[:, cols] += _mm_tn(cqb, dq_part)
            dwuk_ref[:, cols] += _mm_tn(ckvb, dk_part)
        dqh_ref[...] += dqh_gain
        dkh_ref[...] += dkh_gain
        dq_lat, dg = _rms_bwd(q_lat, rq, qa_gain_v, dcq)
        dqa_ref[...] += dg
        dkv_lat, dg = _rms_bwd(kv_lat, rkv, kva_gain_v, dckv)
        dkva_ref[...] += dg
        dp_ref[:, 2 * d:2 * d + Q_LORA] = _bf(dq_lat)
        dp_ref[:, 2 * d + Q_LORA:2 * d + Q_LORA + KV_LORA] = _bf(dkv_lat)
        dp_ref[:, 2 * d + Q_LORA + KV_LORA:] = _bf(dk_rope)

    tok = lambda c: pl.BlockSpec((tm, c), lambda i: (i, 0))
    col = lambda c: pl.BlockSpec((tm, d), lambda i: (i, c))
    head = lambda c: pl.BlockSpec((N_HEADS, tm, c), lambda i: (0, i, 0))
    nxt = pl.BlockSpec((8, d), lambda i: (jnp.minimum((i + 1) * (tm // 8), last_blk), 0))
    return _pallas(
        body, "prep_bwd", (t // tm,),
        [tok(LAT_COLS), col(3), col(4), tok(d), nxt, head(HEAD_PAD), head(HEAD_PAD), tok(N_HEADS * V_DIM),
         _const((1, Q_LORA)), _const((1, KV_LORA)), _const((1, HEAD_PAD)), _const((1, HEAD_PAD)),
         _resident(w_uq.shape), _resident(w_uk.shape), _resident(w_uv.shape), tok(HEAD_PAD), tok(HEAD_PAD), tok(HEAD_PAD),
         _const((3, d))],
        [tok(P_COLS), _const(w_uq.shape), _const(w_uk.shape), _const(w_uv.shape), _const((1, Q_LORA)),
         _const((1, KV_LORA)), _const((1, HEAD_PAD)), _const((1, HEAD_PAD)), _const((3, d))],
        [jax.ShapeDtypeStruct((t, P_COLS), BF16), jax.ShapeDtypeStruct(w_uq.shape, F32),
         jax.ShapeDtypeStruct(w_uk.shape, F32), jax.ShapeDtypeStruct(w_uv.shape, F32),
         jax.ShapeDtypeStruct((1, Q_LORA), F32), jax.ShapeDtypeStruct((1, KV_LORA), F32),
         jax.ShapeDtypeStruct((1, HEAD_PAD), F32), jax.ShapeDtypeStruct((1, HEAD_PAD), F32), jax.ShapeDtypeStruct((3, d), F32)],
        (lat, big, big, dz, dz, dq, dk, dv, qa_gain, kva_gain, qh_gain, kh_gain, w_uq, w_uk, w_uv, *rope, conv_w),
        ("arbitrary",), carried)


def _inproj_bwd(x1, gain, dx2, dm, dp, w_in, w_kr, carried=None):
    t, d = x1.shape
    tm = TOKEN_TILE

    def body(x_ref, g_ref, dx2_ref, dm_ref, dp_ref, win_ref, wkr_ref, dx1_ref, dgain_ref):
        xv = x_ref[...]
        gain_v = g_ref[...]
        _, r = _rms(xv, gain_v)
        dh = (_mm(dm_ref[:, :d], win_ref[ROW_GB:ROW_GC, :]) + _mm(dm_ref[:, d:], win_ref[ROW_GL:, :])
              + _mm(dp_ref[:, :d], win_ref[ROW_XC:ROW_GB, :]) + _mm(dp_ref[:, d:2 * d], win_ref[ROW_GC:ROW_GL, :])
              + _mm(dp_ref[:, 2 * d:2 * d + ROW_KR], win_ref[ROW_QKV:ROW_KR, :]) + _mm(dp_ref[:, 2 * d + ROW_KR:], wkr_ref[...]))
        dxn, dgain = _rms_bwd(xv, r, gain_v, dh)
        dx1_ref[...] = dx2_ref[...] + dxn

        @pl.when(pl.program_id(0) == 0)
        def _():
            dgain_ref[...] = jnp.zeros_like(dgain_ref)

        dgain_ref[...] += dgain

    tok = lambda c: pl.BlockSpec((tm, c), lambda i: (i, 0))
    return _pallas(
        body, "inproj_bwd", (t // tm,),
        [tok(d), _const((1, d)), tok(d), tok(M_COLS), tok(P_COLS), _resident(w_in.shape), _resident(w_kr.shape)],
        [tok(d), _const((1, d))], [jax.ShapeDtypeStruct((t, d), F32), jax.ShapeDtypeStruct((1, d), F32)],
        (x1, gain, dx2, dm, dp, w_in, w_kr), ("arbitrary",), carried)


def _adamw(quads, name, carried=None):
    k = len(quads)
    rows, cols = quads[0][0].shape
    tr, tc = rows, cols
    for cand in (512, 352, 256, 192, 128, 64):
        if rows % cand == 0 and rows > cand:
            tr = cand
            break
    if tr == rows and rows * cols > 512 * 1024 and cols % 256 == 0:
        tc = 256
    while k * 14 * tr * tc * 4 > (VMEM_LIMIT * 3) // 4 and tr % 16 == 0:
        tr //= 2

    def body(*refs):
        for i in range(k):
            w_ref, g_ref, m_ref, v_ref = refs[4 * i:4 * i + 4]
            delta_ref, nm_ref, nv_ref = refs[4 * k + 3 * i:4 * k + 3 * i + 3]
            delta_ref[...], nm_ref[...], nv_ref[...] = _adamw_update(w_ref[...], g_ref[...], m_ref[...], v_ref[...])

    spec = pl.BlockSpec((tr, tc), lambda i, j: (i, j))
    shape = jax.ShapeDtypeStruct((rows, cols), F32)
    outs = _pallas(body, name, (rows // tr, cols // tc), [spec] * (4 * k), [spec] * (3 * k), [shape] * (3 * k),
                   [a for quad in quads for a in quad], ("arbitrary", "arbitrary"), carried)
    return [tuple(outs[3 * i:3 * i + 3]) for i in range(k)]


def _adamw_update(w, g, m, v):
    nm = ADAM_B1 * m + (1.0 - ADAM_B1) * g
    nv = ADAM_B2 * v + (1.0 - ADAM_B2) * (g * g)
    m_hat = nm * (1.0 / (1.0 - ADAM_B1 ** ADAM_STEP))
    v_hat = nv * (1.0 / (1.0 - ADAM_B2 ** ADAM_STEP))
    return -ADAM_LR * (m_hat / (jnp.sqrt(v_hat) + ADAM_EPS) + ADAM_WD * w), nm, nv


def _adamw_whole(quads, name):
    k = len(quads)

    def body(*refs):
        for i in range(k):
            w_ref, g_ref, m_ref, v_ref = refs[4 * i:4 * i + 4]
            g_out, delta_ref, nm_ref, nv_ref = refs[4 * k + 4 * i:4 * k + 4 * i + 4]
            gv = g_ref[...]
            g_out[...] = gv
            delta_ref[...], nm_ref[...], nv_ref[...] = _adamw_update(w_ref[...], gv, m_ref[...], v_ref[...])

    vm = pl.BlockSpec(memory_space=pltpu.VMEM)
    outs = pl.pallas_call(body, name=name, in_specs=[vm] * (4 * k), out_specs=[vm] * (4 * k),
                          out_shape=[jax.ShapeDtypeStruct(q[0].shape, F32) for q in quads for _ in range(4)],
                          compiler_params=_params())(*[a for quad in quads for a in quad])
    return [tuple(outs[4 * i:4 * i + 4]) for i in range(k)]


def _adamw_small(packed_grads, triples, segments):
    k = len(triples)

    def body(*refs):
        g_ref = refs[0]
        off = 0
        for i in range(k):
            w_ref, m_ref, v_ref = refs[1 + 3 * i:4 + 3 * i]
            g_out, delta_ref, nm_ref, nv_ref = refs[1 + 3 * k + 4 * i:5 + 3 * k + 4 * i]
            gv = g_ref[:, off:off + w_ref.shape[1]]
            g_out[...] = gv
            delta_ref[...], nm_ref[...], nv_ref[...] = _adamw_update(w_ref[...], gv, m_ref[...], v_ref[...])
            off += segments[i]

    vm = pl.BlockSpec(memory_space=pltpu.VMEM)
    outs = pl.pallas_call(
        body, name="adamw_small", in_specs=[vm] * (1 + 3 * k), out_specs=[vm] * (4 * k),
        out_shape=[jax.ShapeDtypeStruct(w.shape, F32) for w, _, _ in triples for _ in range(4)],
    )(packed_grads, *[a for triple in triples for a in triple])
    return [tuple(outs[4 * i:4 * i + 4]) for i in range(k)]


def _place():
    x, y, c = lax.axis_index("x"), lax.axis_index("y"), lax.axis_index("c")
    other_chips = [(1 - x, y), (x, 1 - y), (1 - x, 1 - y)]
    return x, y, c, other_chips


def _remote(src, dst, sems, send, recv, device):
    return pltpu.make_async_remote_copy(src_ref=src, dst_ref=dst, send_sem=sems.at[send], recv_sem=sems.at[recv],
                                        device_id=device, device_id_type=MESH_ID)


def _cast_shards(shards, out_dtypes, n_first):
    n = len(shards)
    out_shape = [jax.ShapeDtypeStruct((N_CHIPS,) + s.shape, dt) for s, dt in zip(shards, out_dtypes)]
    gather = _gather_carried(out_shape[:n_first])

    def body(*refs):
        ins, outs, stage, sems = refs[:n], refs[n:2 * n], refs[2 * n:3 * n], refs[3 * n]
        x, y, _, _ = _place()
        me = 2 * x + y

        def cast(first, last):
            copies = []
            for w in range(first, last):
                stage[w][...] = ins[w][...].astype(out_dtypes[w])
                copies.append(pltpu.make_async_copy(stage[w], outs[w].at[me], sems.at[w]))
                copies[-1].start()
            for cp in copies:
                cp.wait()

        cast(0, n_first)
        gather.start(None, outs[:n_first], sems, n)
        cast(n_first, n)
        gather.finish(None, outs[:n_first], sems, n)

    vm = pl.BlockSpec(memory_space=pltpu.VMEM)
    return pl.pallas_call(
        body, name="cast_shards", in_specs=[vm] * n, out_specs=[ANY] * n, out_shape=out_shape,
        scratch_shapes=[pltpu.VMEM(s.shape, dt) for s, dt in zip(shards, out_dtypes)] + [pltpu.SemaphoreType.DMA((n + gather.n_sems,))],
        compiler_params=_params())(*shards)


BF16_ROWS = 16


def _split_rows(rows):
    return (rows // 2) % BF16_ROWS == 0


def _half_shape(rows, cols):
    return (rows // 2, cols) if _split_rows(rows) else (rows, cols // 2)


def _half(rows, cols, which):
    if _split_rows(rows):
        return (pl.ds(pl.multiple_of(which * (rows // 2), BF16_ROWS), rows // 2), slice(None))
    return (slice(None), pl.ds(pl.multiple_of(which * (cols // 2), 128), cols // 2))


def _gather_carried(bufs):
    n = len(bufs)

    def half(w, slot, which):
        _, rows, cols = bufs[w].shape
        return (slot,) + _half(rows, cols, which)

    def start(ins, outs, sems, base):
        x, y, c, other_chips = _place()
        me = 2 * x + y
        for w in range(n):
            mine = outs[w].at[half(w, me, c)]
            for p, (px, py) in enumerate(other_chips):
                _remote(mine, mine, sems, base + 12 * w + p, base + 12 * w + 3 + p, (px, py, c)).start()

    def finish(ins, outs, sems, base):
        x, y, c, other_chips = _place()
        me = 2 * x + y
        for w in range(n):
            for p, (px, py) in enumerate(other_chips):
                got = outs[w].at[half(w, 2 * px + py, c)]
                _remote(got, got, sems, base + 12 * w + p, base + 12 * w + 3 + p, (px, py, c)).wait_recv()
                _remote(got, got, sems, base + 12 * w + 6 + p, base + 12 * w + 9 + p, (x, y, 1 - c)).start()
        for w in range(n):
            mine = outs[w].at[half(w, me, c)]
            for p, (px, py) in enumerate(other_chips):
                got = outs[w].at[half(w, 2 * px + py, c)]
                theirs = outs[w].at[half(w, 2 * px + py, 1 - c)]
                _remote(got, theirs, sems, base + 12 * w + 6 + p, base + 12 * w + 9 + p, (x, y, 1 - c)).wait()
                _remote(mine, mine, sems, base + 12 * w + p, base + 12 * w + 3 + p, (px, py, c)).wait_send()

    shapes = [jax.ShapeDtypeStruct(b.shape, b.dtype) for b in bufs]
    return _Carried(bufs, shapes, {w: w for w in range(n)}, 12 * n, start, finish)


def _swap_carried(grads):
    n = len(grads)

    def copy(w, ins, outs, sems, base):
        x, y, c, _ = _place()
        _, rows, cols = grads[w].shape
        theirs = ins[w].at[(slice(None),) + _half(rows, cols, 1 - c)]
        return _remote(theirs, outs[w], sems, base + 2 * w, base + 2 * w + 1, (x, y, 1 - c))

    def start(ins, outs, sems, base):
        for w in range(n):
            copy(w, ins, outs, sems, base).start()

    def finish(ins, outs, sems, base):
        for w in range(n):
            copy(w, ins, outs, sems, base).wait()

    shapes = [jax.ShapeDtypeStruct((g.shape[0],) + _half_shape(*g.shape[1:]), F32) for g in grads]
    return _Carried(grads, shapes, {}, 2 * n, start, finish)


def _row_tile(rows):
    for cand in (512, 352, 256, 192, 128, 96, 64, 32, 16):
        if rows % cand == 0:
            return cand
    return rows


def _half_block_index(split_rows, tiles, i, core):
    return (core * tiles + i, 0) if split_rows else (i, core)


def _chip_partial(grad, other, place, name):
    nblk, hr, hc = other.shape
    by_rows = _split_rows(grad.shape[1])
    tr = _row_tile(hr)
    tiles = hr // tr

    def body(place_ref, g_ref, o_ref, own_ref, sum_bf_ref):
        s = g_ref[...] + o_ref[...]
        sum_bf_ref[...] = _bf(s)

        @pl.when(pl.program_id(1) == place_ref[0])
        def _():
            own_ref[...] = s

    grid_spec = pltpu.PrefetchScalarGridSpec(
        num_scalar_prefetch=1, grid=(tiles, nblk),
        in_specs=[pl.BlockSpec((None, tr, hc), lambda i, b, place_ref: (b,) + _half_block_index(by_rows, tiles, i, place_ref[1])),
                  pl.BlockSpec((None, tr, hc), lambda i, b, place_ref: (b, i, 0))],
        out_specs=[pl.BlockSpec((tr, hc), lambda i, b, place_ref: (i, 0)),
                   pl.BlockSpec((None, tr, hc), lambda i, b, place_ref: (b, i, 0))])
    return pl.pallas_call(body, name=name, grid_spec=grid_spec,
                          out_shape=[jax.ShapeDtypeStruct((hr, hc), F32), jax.ShapeDtypeStruct(other.shape, BF16)],
                          compiler_params=_params(("arbitrary", "arbitrary")))(place, grad, other)


def _chip_partial_small(grads, others, place):
    n = len(grads)

    def body(*refs):
        place_ref, g_refs, o_refs = refs[0], refs[1:1 + n], refs[1 + n:1 + 2 * n]
        own_refs, bf_refs = refs[1 + 2 * n:1 + 3 * n], refs[1 + 3 * n:]
        chip, core = place_ref[0], place_ref[1]
        for w in range(n):
            _, rows, cols = grads[w].shape
            half = _half(rows, cols, core)
            bf_refs[w][...] = _bf(g_refs[w][(slice(None),) + half] + o_refs[w][...])
            own_refs[w][...] = g_refs[w][(chip,) + half] + o_refs[w][chip]

    vm = pl.BlockSpec(memory_space=pltpu.VMEM)
    outs = pl.pallas_call(
        body, name="chip_partial_small", in_specs=[pl.BlockSpec(memory_space=pltpu.SMEM)] + [vm] * (2 * n), out_specs=[vm] * (2 * n),
        out_shape=[jax.ShapeDtypeStruct(o.shape[1:], F32) for o in others] + [jax.ShapeDtypeStruct(o.shape, BF16) for o in others],
        compiler_params=_params())(place, *grads, *others)
    return list(zip(outs[:n], outs[n:]))


def _chip_total_small(owns, receiveds, place, shapes):
    n = len(owns)

    def body(*refs):
        place_ref, own_refs, r_refs, out_refs = refs[0], refs[1:1 + n], refs[1 + n:1 + 2 * n], refs[1 + 2 * n:]
        chip, core = place_ref[0], place_ref[1]
        for w in range(n):
            r = [r_refs[w][(chip + k) % N_CHIPS].astype(F32) for k in (1, 2, 3)]
            out_refs[w][_half(*shapes[w], core)] = own_refs[w][...] + ((r[0] + r[1]) + r[2])

    vm = pl.BlockSpec(memory_space=pltpu.VMEM)
    return list(pl.pallas_call(
        body, name="chip_total_small", in_specs=[pl.BlockSpec(memory_space=pltpu.SMEM)] + [vm] * (2 * n), out_specs=[vm] * n,
        out_shape=[jax.ShapeDtypeStruct(tuple(s), F32) for s in shapes], compiler_params=_params())(place, *owns, *receiveds))


def _send_carried(partials):
    n = len(partials)

    def start(ins, outs, sems, base):
        x, y, c, other_chips = _place()
        me = 2 * x + y
        for w in range(n):
            for p, (px, py) in enumerate(other_chips):
                _remote(ins[w].at[2 * px + py], outs[w].at[me], sems, base + 6 * w + p, base + 6 * w + 3 + p, (px, py, c)).start()

    def finish(ins, outs, sems, base):
        x, y, c, other_chips = _place()
        for w in range(n):
            for p, (px, py) in enumerate(other_chips):
                _remote(ins[w].at[2 * px + py], outs[w].at[2 * px + py], sems, base + 6 * w + p, base + 6 * w + 3 + p,
                        (px, py, c)).wait()

    return _Carried(partials, [jax.ShapeDtypeStruct(p.shape, BF16) for p in partials], {}, 6 * n, start, finish)


def _chip_total(own, received, place, shape, name):
    hr, hc = own.shape
    by_rows = _split_rows(shape[0])
    tr = _row_tile(hr)
    tiles = hr // tr

    def body(place_ref, own_ref, r1_ref, r2_ref, r3_ref, out_ref):
        out_ref[...] = own_ref[...] + ((r1_ref[...].astype(F32) + r2_ref[...].astype(F32)) + r3_ref[...].astype(F32))

    def slot(k):
        return pl.BlockSpec((None, tr, hc), lambda i, place_ref: ((place_ref[0] + k) % N_CHIPS, i, 0))

    grid_spec = pltpu.PrefetchScalarGridSpec(
        num_scalar_prefetch=1, grid=(tiles,), in_specs=[pl.BlockSpec((tr, hc), lambda i, place_ref: (i, 0)), slot(1), slot(2), slot(3)],
        out_specs=pl.BlockSpec((tr, hc), lambda i, place_ref: _half_block_index(by_rows, tiles, i, place_ref[1])))
    return pl.pallas_call(body, name=name, grid_spec=grid_spec, out_shape=jax.ShapeDtypeStruct(tuple(shape), F32),
                          compiler_params=_params(("arbitrary",)))(place, own, received, received, received)


def _join_carried(totals):
    n = len(totals)

    def copy(w, outs, sems, base):
        x, y, c, _ = _place()
        mine = outs[w].at[_half(*totals[w].shape, c)]
        return _remote(mine, mine, sems, base + 2 * w, base + 2 * w + 1, (x, y, 1 - c))

    def start(ins, outs, sems, base):
        for w in range(n):
            copy(w, outs, sems, base).start()

    def finish(ins, outs, sems, base):
        for w in range(n):
            copy(w, outs, sems, base).wait()

    shapes = [jax.ShapeDtypeStruct(a.shape, F32) for a in totals]
    return _Carried(totals, shapes, {w: w for w in range(n)}, 2 * n, start, finish)


def _sum_devices(vec):
    rows, n = vec.shape

    def body(v_ref, out_ref, buf, send_sems, recv_sems):
        x, y, c, _ = _place()
        me = 4 * x + 2 * y + c
        buf[me] = v_ref[...]
        sends = []
        for k in range(1, N_DEV):
            peer = (1 - x if k & 4 else x, 1 - y if k & 2 else y, 1 - c if k & 1 else c)
            cp = pltpu.make_async_remote_copy(src_ref=v_ref, dst_ref=buf.at[me], send_sem=send_sems.at[k], recv_sem=recv_sems.at[k],
                                              device_id=peer, device_id_type=MESH_ID)
            cp.start()
            sends.append(cp)
        for cp in sends:
            cp.wait()
        total = buf[0]
        for dev in range(1, N_DEV):
            total = total + buf[dev]
        out_ref[...] = total

    vm = pl.BlockSpec(memory_space=pltpu.VMEM)
    return pl.pallas_call(
        body, name="sum_devices", in_specs=[vm], out_specs=vm, out_shape=jax.ShapeDtypeStruct((rows, n), F32),
        scratch_shapes=[pltpu.VMEM((N_DEV, rows, n), F32), pltpu.SemaphoreType.DMA((N_DEV,)), pltpu.SemaphoreType.DMA((N_DEV,))],
    )(vec)


def _rope_tables(positions):
    half = ROPE_HALF
    inv_freq = 1.0 / (ROPE_THETA ** (jnp.arange(half, dtype=F32) / half))
    ang = positions.astype(F32).reshape(-1, 1) * inv_freq
    cos, sin = jnp.cos(ang), jnp.sin(ang)
    t = ang.shape[0]
    ones, zeros = jnp.ones((t, QK_NOPE), F32), jnp.zeros((t, QK_NOPE), F32)
    pad, none = HEAD_PAD - QK_DIM, zeros[:, :half]
    cos_full = jnp.concatenate([ones, cos, cos, ones[:, :pad]], axis=1)
    s_lo = jnp.concatenate([zeros, -sin, none, zeros[:, :pad]], axis=1)
    s_hi = jnp.concatenate([zeros, none, sin, zeros[:, :pad]], axis=1)
    return cos_full, s_lo, s_hi


def _partials(names, grads, from_sibling, place):
    return [_chip_partial(g, o, place, "chip_partial_" + n) for n, g, o in zip(names, grads, from_sibling)]


def _totals(names, grads, partials, received, place):
    return [_chip_total(pf, r, place, g.shape[1:], "chip_total_" + n) for n, g, (pf, _), r in zip(names, grads, partials, received)]


def _kernel_layouts(full):
    w_in = full["w_in"]
    w_kr = jnp.pad(w_in[ROW_KR:ROW_XC], ((QK_NOPE, HEAD_PAD - QK_DIM), (0, 0)))
    w_uq = jnp.pad(full["w_uq"].reshape(Q_LORA, N_HEADS, QK_DIM), ((0, 0), (0, 0), (0, HEAD_PAD - QK_DIM)))
    w_uk = jnp.pad(full["w_uk"].reshape(KV_LORA, N_HEADS, QK_NOPE), ((0, 0), (0, 0), (0, HEAD_PAD - QK_NOPE)))
    return {"w_in": w_in, "w_kr": w_kr, "w_uq": w_uq.reshape(Q_LORA, N_HEADS * HEAD_PAD),
            "w_uk": w_uk.reshape(KV_LORA, N_HEADS * HEAD_PAD), "w_uv": full["w_uv"], "w_uvt": full["w_uv"].T}


def _global_layouts(g):
    w_uq = g["w_uq"].reshape(Q_LORA, N_HEADS, HEAD_PAD)[:, :, :QK_DIM].reshape(Q_LORA, N_HEADS * QK_DIM)
    w_uk = g["w_uk"].reshape(KV_LORA, N_HEADS, HEAD_PAD)[:, :, :QK_NOPE].reshape(KV_LORA, N_HEADS * QK_NOPE)
    return {"w_in": g["w_in"], "w_uq": w_uq, "w_uk": w_uk, "w_uv": g["w_uv"], "w_proj_attn": g["w_pa"], "w_proj_conv": g["w_pc"],
            "w_out": g["w_out"]}


def _dw_in(dm, dp, h2b):
    d, rows = D_MODEL, ROW_GL + 2 * D_MODEL
    wm, wp = M_COLS // 2, P_COLS // 2
    from_dm = [(0, [(ROW_GB, 0, d), (ROW_GL, d, wm - d)]), (1, [(ROW_GL + wm - d, 0, wm)])]
    from_dp = [(0, [(ROW_XC, 0, d), (ROW_GC, d, wp - d)]),
               (1, [(ROW_GC + wp - d, 0, 2 * d - wp), (ROW_QKV, 2 * d - wp, ROW_KR), (ROW_KR, 2 * d - wp + ROW_KR + QK_NOPE, QK_ROPE)])]
    out = _tn_rows(dm, h2b, None, from_dm, wm, rows, "dw_in_m")
    return _tn_rows(dp, h2b, out, from_dp, wp, rows, "dw_in_p")


def _col_blocks(a):
    r, c = a.shape
    return a.reshape(r, N_CHIPS, c // N_CHIPS).transpose(1, 0, 2)


def _from_col_blocks(a):
    n, r, c = a.shape
    return a.transpose(1, 0, 2).reshape(r, n * c)


COL_SHARDED = ("w_uq", "w_uk", "w_uv", "w_proj_attn")
TRANSPOSED = ("ffn1_w_gate", "ffn1_w_up", "ffn2_w_gate", "ffn2_w_up", "w_in")
SMALL = (("ffn1_norm", 1024), ("mix_norm", 1024), ("gate_bias", 2048), ("q_a_norm", 384), ("kv_a_norm", 256),
         ("q_head_norm", 128), ("k_head_norm", 128), ("ffn2_norm", 1024))
WEIGHT_ORDER = ("ffn1_norm", "ffn1_w_gate", "ffn1_w_up", "ffn1_w_down", "mix_norm", "w_in", "gate_bias", "q_a_norm", "w_uq",
                "kv_a_norm", "w_uk", "w_uv", "q_head_norm", "k_head_norm", "w_proj_attn", "conv_w", "w_proj_conv", "w_out",
                "ffn2_norm", "ffn2_w_gate", "ffn2_w_up", "ffn2_w_down")
MATRICES = ("ffn1_w_gate", "ffn1_w_up", "ffn1_w_down", "w_in", "w_uq", "w_uk", "w_uv", "w_proj_attn", "w_proj_conv", "w_out",
            "ffn2_w_gate", "ffn2_w_up", "ffn2_w_down")
GROUP_FFN1 = ("ffn1_w_gate", "ffn1_w_up", "ffn1_w_down")
GROUP_IN = ("w_in", "w_uq", "w_uk", "w_uv", "conv_w")
GROUP_MIX = ("w_proj_attn", "w_proj_conv", "w_out")
GROUP_FFN2 = ("ffn2_w_gate", "ffn2_w_up", "ffn2_w_down")
GROUP_MID = ("w_in", "w_uq", "w_uk", "w_uv", "w_proj_attn", "w_proj_conv", "w_out")


def _pad_lanes(a, n):
    return jnp.pad(a.reshape(1, -1), ((0, 0), (0, n - a.size)))


def kernel(x, positions, ffn1_norm, ffn1_w_gate, ffn1_w_up, ffn1_w_down, mix_norm, w_in, gate_bias, q_a_norm, w_uq, kv_a_norm, w_uk, w_uv, q_head_norm, k_head_norm, w_proj_attn, conv_w, w_proj_conv, w_out, ffn2_norm, ffn2_w_gate, ffn2_w_up, ffn2_w_down, loss_target, m_ffn1_norm, m_ffn1_w_gate, m_ffn1_w_up, m_ffn1_w_down, m_mix_norm, m_w_in, m_gate_bias, m_q_a_norm, m_w_uq, m_kv_a_norm, m_w_uk, m_w_uv, m_q_head_norm, m_k_head_norm, m_w_proj_attn, m_conv_w, m_w_proj_conv, m_w_out, m_ffn2_norm, m_ffn2_w_gate, m_ffn2_w_up, m_ffn2_w_down, v_ffn1_norm, v_ffn1_w_gate, v_ffn1_w_up, v_ffn1_w_down, v_mix_norm, v_w_in, v_gate_bias, v_q_a_norm, v_w_uq, v_kv_a_norm, v_w_uk, v_w_uv, v_q_head_norm, v_k_head_norm, v_w_proj_attn, v_conv_w, v_w_proj_conv, v_w_out, v_ffn2_norm, v_ffn2_w_gate, v_ffn2_w_up, v_ffn2_w_down):
    args = dict(locals())
    view = lambda n, a: a.T if n in TRANSPOSED else a
    weights = {n: view(n, args[n]) for n in WEIGHT_ORDER}
    moments_m = {n: view(n, args["m_" + n]) for n in WEIGHT_ORDER}
    moments_v = {n: view(n, args["v_" + n]) for n in WEIGHT_ORDER}
    nb, seq, d = x.shape
    t = nb * seq
    chip = (2 * lax.axis_index("x") + lax.axis_index("y")).astype(jnp.int32)
    place = jnp.stack([chip, lax.axis_index("c").astype(jnp.int32)])
    grads, delta, new_m, new_v = {}, {}, {}, {}

    def adamw(names, carried=None):
        results = _adamw([(weights[n], grads[n], moments_m[n], moments_v[n]) for n in names], "adamw_" + names[0], carried)
        for n, (dn, mn, vn) in zip(names, results):
            delta[n], new_m[n], new_v[n] = dn, mn, vn

    conv_rows = conv_w.shape[0]
    conv_shard = jnp.pad(conv_w, ((0, 16 - conv_rows), (0, 0)))
    assert MATRICES[:len(GROUP_FFN1)] == GROUP_FFN1
    bufs = dict(zip(MATRICES + ("conv_w",), _cast_shards([weights[n] for n in MATRICES] + [conv_shard],
                                                         [BF16] * len(MATRICES) + [F32], len(GROUP_FFN1))))
    blocks = {n: bufs[n] for n in GROUP_FFN1}
    p = {n: _pad_lanes(weights[n], size) for n, size in SMALL}
    rope = _rope_tables(positions)
    x_tok = x.reshape(t, d)

    gather_in = _gather_carried([bufs[n] for n in GROUP_IN])
    x1, gate1, up1, act1 = _ffn_fwd(x_tok, p["ffn1_norm"], blocks["ffn1_w_gate"], blocks["ffn1_w_up"], blocks["ffn1_w_down"], None,
                                    "ffn1_fwd", gather_in)
    blocks.update(zip(GROUP_IN, gather_in.results))
    w = _kernel_layouts({"w_in": blocks["w_in"].reshape(-1, d), **{n: _from_col_blocks(blocks[n]) for n in ("w_uq", "w_uk", "w_uv")}})
    p["conv_w"] = _from_col_blocks(blocks["conv_w"])[:conv_rows]

    gather_mix = _gather_carried([bufs[n] for n in GROUP_MIX + GROUP_FFN2[2:]])
    h2b, big, lat, q, k, v, vt = _inproj_fwd(x1, p["mix_norm"], w["w_in"], w["w_kr"], p["q_a_norm"], p["kv_a_norm"], p["q_head_norm"],
                                             p["k_head_norm"], w["w_uq"], w["w_uk"], w["w_uv"], w["w_uvt"], rope, gather_mix)
    blocks.update(zip(GROUP_MIX + GROUP_FFN2[2:], gather_mix.results))
    w_pa = _from_col_blocks(blocks["w_proj_attn"])
    w_pc, w_out_full = blocks["w_proj_conv"].reshape(-1, d), blocks["w_out"].reshape(-1, d)

    gather_ffn2 = _gather_carried([bufs[n] for n in GROUP_FFN2[:2]])
    o, lse = _attn_fwd(q, k, vt, seq, gather_ffn2)
    x2 = _mix_fwd(x1, o, big, p["gate_bias"], p["conv_w"], w_pa, w_pc, w_out_full, seq)
    wg2, wu2 = gather_ffn2.results
    wd2 = blocks["ffn2_w_down"]
    dx3, gate2, up2, act2, loss = _ffn_fwd(x2, p["ffn2_norm"], wg2, wu2, wd2, loss_target.reshape(t, d), "ffn2_fwd")

    dx2, dg_ffn2, hb2, dgate2, dup2, dyb2 = _ffn_bwd_x(x2, p["ffn2_norm"], dx3, gate2, up2, wg2, wu2, wd2, "ffn2_bwd")
    g_ffn2 = [_tn_matmul(dgate2, hb2, "ffn2_dw_gate"), _tn_matmul(dup2, hb2, "ffn2_dw_up"), _tn_matmul(act2, dyb2, "ffn2_dw_down")]
    swap = _swap_carried(g_ffn2)
    do, delta_o, dz, dm, dbias, dw_pa, dw_pc, dw_out = _mix_bwd(dx2, o, big, p["gate_bias"], p["conv_w"], w_pa, w_pc, w_out_full, seq,
                                                                swap)
    part = _partials(GROUP_FFN2, g_ffn2, swap.results, place)
    send = _send_carried([pb for _, pb in part])
    dq, dk, dv = _attn_bwd(q, k, v, do, lse, delta_o.reshape(N_HEADS // ATTN_BWD_HEADS, ATTN_BWD_HEADS, -1), seq, send)
    join = _join_carried(_totals(GROUP_FFN2, g_ffn2, part, send.results, place))
    dp, dw_uq, dw_uk, dw_uv, dqa, dkva, dqh, dkh, dcw = _prep_bwd(
        lat, big, dz, dq, dk, dv, p["q_a_norm"], p["kv_a_norm"], p["q_head_norm"], p["k_head_norm"], w["w_uq"], w["w_uk"],
        w["w_uv"], rope, p["conv_w"], seq, join)
    grads.update(zip(GROUP_FFN2, join.results))

    gg = _global_layouts({"w_in": _dw_in(dm, dp, h2b), "w_uq": dw_uq, "w_uk": dw_uk, "w_uv": dw_uv, "w_pa": dw_pa, "w_pc": dw_pc,
                          "w_out": dw_out})
    g_mid = [_col_blocks(gg[n]) if n in COL_SHARDED else gg[n].reshape(N_CHIPS, -1, gg[n].shape[-1]) for n in GROUP_MID]
    swap = _swap_carried(g_mid)
    dx1, dg_mix = _inproj_bwd(x1, p["mix_norm"], dx2, dm, dp, w["w_in"], w["w_kr"], swap)
    part = (_partials(GROUP_MID[:1], g_mid[:1], swap.results[:1], place)
            + _chip_partial_small(g_mid[1:], swap.results[1:], place))
    send = _send_carried([pb for _, pb in part])
    grad_x, dg_ffn1, hb1, dgate1, dup1, dyb1 = _ffn_bwd_x(x_tok, p["ffn1_norm"], dx1, gate1, up1, blocks["ffn1_w_gate"],
                                                         blocks["ffn1_w_up"], blocks["ffn1_w_down"], "ffn1_bwd", send)

    small_grads = {"ffn1_norm": dg_ffn1, "mix_norm": dg_mix, "gate_bias": dbias, "q_a_norm": dqa, "kv_a_norm": dkva,
                   "q_head_norm": dqh, "k_head_norm": dkh, "ffn2_norm": dg_ffn2}
    packed = jnp.concatenate([small_grads[n] for n, _ in SMALL] + [dcw.reshape(1, -1), loss], axis=1)
    total = _sum_devices(packed.reshape(8, -1)).reshape(1, -1)
    n_small = sum(size for _, size in SMALL)
    conv_cols = conv_w.shape[1]
    conv_total = total[:, n_small:n_small + conv_rows * d].reshape(conv_rows, d)
    grads["conv_w"] = lax.dynamic_slice_in_dim(conv_total, chip * conv_cols, conv_cols, axis=1)
    loss_total = total[0, n_small + conv_rows * d]

    join = _join_carried(_totals(GROUP_MID[:1], g_mid[:1], part[:1], send.results[:1], place)
                         + _chip_total_small([pf for pf, _ in part[1:]], send.results[1:], place, [g.shape[1:] for g in g_mid[1:]]))
    g_gate = _tn_matmul(dgate1, hb1, "ffn1_dw_gate", carried=join)
    grads.update(zip(GROUP_MID, join.results))
    swap_gate = _swap_carried([g_gate])
    g_up = _tn_matmul(dup1, hb1, "ffn1_dw_up", carried=swap_gate)
    part_gate = _partials(GROUP_FFN1[:1], [g_gate], swap_gate.results, place)
    send_gate, swap_up = _send_carried([part_gate[0][1]]), _swap_carried([g_up])
    g_down = _tn_matmul(act1, dyb1, "ffn1_dw_down", carried=_both(send_gate, swap_up))
    join_gate = _join_carried(_totals(GROUP_FFN1[:1], [g_gate], part_gate, send_gate.results, place))
    part_up = _partials(GROUP_FFN1[1:2], [g_up], swap_up.results, place)
    send_up, swap_down = _send_carried([part_up[0][1]]), _swap_carried([g_down])
    adamw(GROUP_FFN2, _both(_both(send_up, swap_down), join_gate))
    grads["ffn1_w_gate"] = join_gate.results[0]
    join_up = _join_carried(_totals(GROUP_FFN1[1:2], [g_up], part_up, send_up.results, place))
    part_down = _partials(GROUP_FFN1[2:], [g_down], swap_down.results, place)
    send_down = _send_carried([part_down[0][1]])
    adamw(("w_in",), _both(send_down, join_up))
    grads["ffn1_w_up"] = join_up.results[0]
    join_down = _join_carried(_totals(GROUP_FFN1[2:], [g_down], part_down, send_down.results, place))
    adamw(GROUP_FFN1[:2], join_down)
    grads["ffn1_w_down"] = join_down.results[0]
    adamw(GROUP_FFN1[2:])
    others = GROUP_MID[1:] + ("conv_w",)
    for n, (gn, dn, mn, vn) in zip(others, _adamw_whole([(weights[n], grads[n], moments_m[n], moments_v[n]) for n in others],
                                                        "adamw_others")):
        grads[n], delta[n], new_m[n], new_v[n] = gn, dn, mn, vn

    row = lambda a: a.reshape(1, -1)
    small = _adamw_small(total, [(row(weights[n]), row(moments_m[n]), row(moments_v[n])) for n, _ in SMALL], [size for _, size in SMALL])
    for (n, _), (gn, dn, mn, vn) in zip(SMALL, small):
        grads[n], delta[n], new_m[n], new_v[n] = gn.reshape(-1), dn.reshape(-1), mn.reshape(-1), vn.reshape(-1)

    return (loss_total, grad_x.reshape(nb, seq, d), *[view(n, src[n]) for src in (grads, delta, new_m, new_v) for n in WEIGHT_ORDER])
```

```python
import functools

import jax
import jax.numpy as jnp
from jax import lax
from jax.experimental import pallas as pl
from jax.experimental.pallas import tpu as pltpu

F32 = jnp.float32
BF16 = jnp.bfloat16

D_MODEL = 1024
N_HEADS = 8
QK_NOPE = 64
QK_ROPE = 32
QK_DIM = QK_NOPE + QK_ROPE
V_DIM = 64
HEAD_PAD = 128
Q_LORA = 384
KV_LORA = 256
ROPE_THETA = 10000.0
NORM_EPS = 1e-6
ATTN_SCALE = QK_DIM ** -0.5
MASK_VALUE = -1e30
N_CHIPS = 4
N_DEV = 8

ADAM_LR = 0.001
ADAM_B1 = 0.9
ADAM_B2 = 0.999
ADAM_EPS = 1e-08
ADAM_WD = 0.01
ADAM_STEP = 10

TOKEN_TILE = 256
WIDE_TILE = 512
ATTN_TILE = 512
TN_TILE = 2048
VMEM_LIMIT = 56 * 1024 * 1024

M_COLS = 3 * D_MODEL
P_COLS = 2 * D_MODEL + Q_LORA + KV_LORA + HEAD_PAD
BIG_COLS = 5 * D_MODEL
LAT_COLS = Q_LORA + KV_LORA + HEAD_PAD

MESH_ID = pl.DeviceIdType.MESH
ANY = pl.BlockSpec(memory_space=pl.ANY)


def _params(semantics=None):
    return pltpu.CompilerParams(dimension_semantics=semantics, vmem_limit_bytes=VMEM_LIMIT)


class _Carried:
    def __init__(self, operands, out_shapes, aliases, n_sems, start, finish):
        self.operands, self.out_shapes, self.aliases, self.n_sems = list(operands), list(out_shapes), dict(aliases), n_sems
        self.start, self.finish = start, finish
        self.results = None


def _both(a, b):
    na, nao = len(a.operands), len(a.out_shapes)

    def start(ins, outs, sems, base):
        a.start(ins[:na], outs[:nao], sems, base)
        b.start(ins[na:], outs[nao:], sems, base + a.n_sems)

    def finish(ins, outs, sems, base):
        a.finish(ins[:na], outs[:nao], sems, base)
        b.finish(ins[na:], outs[nao:], sems, base + a.n_sems)

    aliases = dict(a.aliases)
    aliases.update({na + i: nao + o for i, o in b.aliases.items()})
    both = _Carried(a.operands + b.operands, a.out_shapes + b.out_shapes, aliases, a.n_sems + b.n_sems, start, finish)
    both.parts = (a, b)
    return both


def _set_results(carried, results):
    carried.results = list(results)
    if hasattr(carried, "parts"):
        a, b = carried.parts
        _set_results(a, results[:len(a.out_shapes)])
        _set_results(b, results[len(a.out_shapes):])


def _pallas(body, name, grid, in_specs, out_specs, out_shape, args, semantics, carried=None):
    if carried is None:
        return pl.pallas_call(body, name=name, grid=grid, in_specs=in_specs, out_specs=out_specs, out_shape=out_shape,
                              compiler_params=_params(semantics))(*args)
    n_in, n_out, n_ci, n_co = len(in_specs), len(out_specs), len(carried.operands), len(carried.out_shapes)

    def wrapped(*refs):
        ins, c_ins = refs[:n_in], refs[n_in:n_in + n_ci]
        outs, c_outs = refs[n_in + n_ci:n_in + n_ci + n_out], refs[n_in + n_ci + n_out:n_in + n_ci + n_out + n_co]
        sems = refs[-1]
        first = pl.program_id(0) == 0
        last = pl.program_id(0) == grid[0] - 1
        for axis in range(1, len(grid)):
            first = jnp.logical_and(first, pl.program_id(axis) == 0)
            last = jnp.logical_and(last, pl.program_id(axis) == grid[axis] - 1)

        @pl.when(first)
        def _():
            carried.start(c_ins, c_outs, sems, 0)

        body(*ins, *outs)

        @pl.when(last)
        def _():
            carried.finish(c_ins, c_outs, sems, 0)

    results = pl.pallas_call(
        wrapped, name=name, grid=grid, in_specs=list(in_specs) + [ANY] * n_ci, out_specs=list(out_specs) + [ANY] * n_co,
        out_shape=list(out_shape) + carried.out_shapes,
        input_output_aliases={n_in + i: n_out + o for i, o in carried.aliases.items()},
        scratch_shapes=[pltpu.SemaphoreType.DMA((carried.n_sems,))], compiler_params=_params(semantics))(*args, *carried.operands)
    _set_results(carried, results[n_out:])
    return results[:n_out]


def _resident(shape):
    nd = len(shape)
    return pl.BlockSpec(shape, lambda *_: (0,) * nd, pipeline_mode=pl.Buffered(1))


def _const(shape):
    nd = len(shape)
    return pl.BlockSpec(shape, lambda *_: (0,) * nd)


def _mm(a, b):
    return jnp.dot(a, b, preferred_element_type=F32)


def _mm_nt(a, b):
    return lax.dot_general(a, b, (((1,), (1,)), ((), ())), preferred_element_type=F32)


def _mm_tn(a, b):
    return lax.dot_general(a, b, (((0,), (0,)), ((), ())), preferred_element_type=F32)


def _bf(a):
    return a.astype(BF16)


def _sigmoid(a):
    return 1.0 / (1.0 + jnp.exp(-a))


def _rms(x, gain, n=None):
    n = x.shape[-1] if n is None else n
    r = lax.rsqrt(jnp.sum(x * x, axis=-1, keepdims=True) * (1.0 / n) + NORM_EPS)
    return (x * r) * gain, r


def _rms_bwd(x, r, gain, dh, n=None):
    n = x.shape[-1] if n is None else n
    u = dh * gain
    dx = r * u - x * ((r * r * r) * (jnp.sum(u * x, axis=-1, keepdims=True) * (1.0 / n)))
    dgain = jnp.sum(dh * (x * r), axis=0, keepdims=True)
    return dx, dgain


ROPE_HALF = QK_ROPE // 2


def _rope(t, rope):
    cos, s_lo, s_hi = rope
    return t * cos + pltpu.roll(t, HEAD_PAD - ROPE_HALF, 1) * s_lo + pltpu.roll(t, ROPE_HALF, 1) * s_hi


def _rope_bwd(dt, rope):
    cos, s_lo, s_hi = rope
    return dt * cos + pltpu.roll(dt * s_lo, ROPE_HALF, 1) + pltpu.roll(dt * s_hi, HEAD_PAD - ROPE_HALF, 1)


def _shift_down(u, prev8, k):
    s = pltpu.roll(u, k, 0)
    p = pltpu.roll(prev8, k, 0)
    row = lax.broadcasted_iota(jnp.int32, prev8.shape, 0)
    top = jnp.where(row < k, p, s[:8])
    return jnp.concatenate([top, s[8:]], axis=0)


def _shift_up(d, next8, k):
    tm = d.shape[0]
    s = pltpu.roll(d, tm - k, 0)
    n = pltpu.roll(next8, 8 - k, 0)
    row = lax.broadcasted_iota(jnp.int32, next8.shape, 0)
    bot = jnp.where(row >= 8 - k, n, s[tm - 8:])
    return jnp.concatenate([s[:tm - 8], bot], axis=0)


def _ffn_fwd(x, gain, wg, wu, wd, target, name, carried=None):
    t, d = x.shape
    nb, f, _ = wg.shape
    tm = TOKEN_TILE
    with_loss = target is not None

    def body(*refs):
        if with_loss:
            x_ref, g_ref, wg_ref, wu_ref, wd_ref, t_ref, out_ref, gate_ref, up_ref, act_ref, loss_ref = refs
        else:
            x_ref, g_ref, wg_ref, wu_ref, wd_ref, out_ref, gate_ref, up_ref, act_ref = refs
        xv = x_ref[...]
        h, _ = _rms(xv, g_ref[...])
        hb = _bf(h)
        y = jnp.zeros((tm, d), F32)
        nxt = (_mm_nt(hb, wg_ref[0]), _mm_nt(hb, wu_ref[0]))
        for j in range(nb):
            gate, up = nxt
            if j + 1 < nb:
                nxt = (_mm_nt(hb, wg_ref[j + 1]), _mm_nt(hb, wu_ref[j + 1]))
            act = _bf((gate * _sigmoid(gate)) * up)
            y = y + _mm(act, wd_ref[j])
            gate_ref[j] = _bf(gate)
            up_ref[j] = _bf(up)
            act_ref[j] = act
        out = xv + 0.5 * y
        if with_loss:
            err = out - t_ref[...]
            out_ref[...] = err * (1.0 / d)

            @pl.when(pl.program_id(0) == 0)
            def _():
                loss_ref[...] = jnp.zeros_like(loss_ref)

            part = jnp.sum(jnp.sum(err * err, axis=1, keepdims=True), axis=0, keepdims=True)
            loss_ref[...] += jnp.broadcast_to(part * (0.5 / d), loss_ref.shape)
        else:
            out_ref[...] = out

    tok = pl.BlockSpec((tm, d), lambda i: (i, 0))
    blk = pl.BlockSpec((nb, tm, f), lambda i: (0, i, 0))
    in_specs = [tok, _const((1, d)), _resident(wg.shape), _resident(wu.shape), _resident(wd.shape)]
    args = [x, gain, wg, wu, wd]
    out_shape = [jax.ShapeDtypeStruct((t, d), F32)] + [jax.ShapeDtypeStruct((nb, t, f), BF16)] * 3
    out_specs = [tok, blk, blk, blk]
    if with_loss:
        in_specs.append(tok)
        args.append(target)
        out_shape.append(jax.ShapeDtypeStruct((1, 128), F32))
        out_specs.append(_const((1, 128)))
    return _pallas(body, name, (t // tm,), in_specs, out_specs, out_shape, args, ("arbitrary",), carried)


def _ffn_bwd_x(x, gain, dout, gate, up, wg, wu, wd, name, carried=None):
    t, d = x.shape
    nb, f, _ = wg.shape
    tm = TOKEN_TILE

    def body(x_ref, g_ref, dout_ref, gate_ref, up_ref, wg_ref, wu_ref, wd_ref,
             dx_ref, dgain_ref, hb_ref, dgate_ref, dup_ref, dyb_ref):
        xv = x_ref[...]
        gain_v = g_ref[...]
        h, r = _rms(xv, gain_v)
        hb_ref[...] = _bf(h)
        dout_v = dout_ref[...]
        dyb = _bf(0.5 * dout_v)
        dyb_ref[...] = dyb
        dh = jnp.zeros((tm, d), F32)
        nxt = _mm_nt(dyb, wd_ref[0])
        for j in range(nb):
            dact = nxt
            if j + 1 < nb:
                nxt = _mm_nt(dyb, wd_ref[j + 1])
            gt = gate_ref[j].astype(F32)
            uv = up_ref[j].astype(F32)
            s = _sigmoid(gt)
            dup = _bf(dact * (gt * s))
            dgate = _bf((dact * uv) * (s * (1.0 + gt * (1.0 - s))))
            dh = dh + _mm(dgate, wg_ref[j]) + _mm(dup, wu_ref[j])
            dgate_ref[j] = dgate
            dup_ref[j] = dup
        dxn, dgain = _rms_bwd(xv, r, gain_v, dh)
        dx_ref[...] = dout_v + dxn

        @pl.when(pl.program_id(0) == 0)
        def _():
            dgain_ref[...] = jnp.zeros_like(dgain_ref)

        dgain_ref[...] += dgain

    tok = pl.BlockSpec((tm, d), lambda i: (i, 0))
    blk = pl.BlockSpec((nb, tm, f), lambda i: (0, i, 0))
    return _pallas(
        body, name, (t // tm,),
        [tok, _const((1, d)), tok, blk, blk, _resident(wg.shape), _resident(wu.shape), _resident(wd.shape)],
        [tok, _const((1, d)), tok, blk, blk, tok],
        [jax.ShapeDtypeStruct((t, d), F32), jax.ShapeDtypeStruct((1, d), F32), jax.ShapeDtypeStruct((t, d), BF16),
         jax.ShapeDtypeStruct((nb, t, f), BF16), jax.ShapeDtypeStruct((nb, t, f), BF16), jax.ShapeDtypeStruct((t, d), BF16)],
        (x, gain, dout, gate, up, wg, wu, wd), ("arbitrary",), carried)


def _tn_matmul(a, b, name, carried=None):
    t = a.shape[-2]
    k = a.shape[-1]
    n = b.shape[-1]
    tt = min(2 * TN_TILE, t)
    nt = t // tt

    def body(a_ref, b_ref, o_ref):
        if nt == 1:
            o_ref[...] = _mm_tn(a_ref[...], b_ref[...])
            return

        @pl.when(pl.program_id(1) == 0)
        def _():
            o_ref[...] = jnp.zeros_like(o_ref)

        o_ref[...] += _mm_tn(a_ref[...], b_ref[...])

    g = a.shape[0] if a.ndim == 3 else b.shape[0]
    a_spec = (pl.BlockSpec((None, tt, k), lambda gi, ti: (gi, ti, 0)) if a.ndim == 3
              else pl.BlockSpec((tt, k), lambda gi, ti: (ti, 0)))
    b_spec = (pl.BlockSpec((None, tt, n), lambda gi, ti: (gi, ti, 0)) if b.ndim == 3
              else pl.BlockSpec((tt, n), lambda gi, ti: (ti, 0)))
    o_spec = pl.BlockSpec((None, k, n), lambda gi, ti: (gi, 0, 0))
    out_shape = jax.ShapeDtypeStruct((g, k, n), F32)
    return _pallas(body, name, (g, nt), [a_spec, b_spec], [o_spec], [out_shape], (a, b), ("arbitrary", "arbitrary"), carried)[0]


def _tn_rows(a, b, out, chunks, width, rows_out, name):
    t, n = b.shape
    tt = min(TN_TILE, t)
    nt = t // tt

    def body(blocks_ref, a_ref, b_ref, *rest):
        out_ref, acc, sem = rest[-3:]
        g, ti = pl.program_id(0), pl.program_id(1)

        @pl.when(ti == 0)
        def _():
            acc[...] = jnp.zeros_like(acc)

        acc[...] += _mm_tn(a_ref[...], b_ref[...])
        for gi, (_, ranges) in enumerate(chunks):
            @pl.when(jnp.logical_and(g == gi, ti == nt - 1))
            def _(ranges=ranges):
                for row, first, count in ranges:
                    cp = pltpu.make_async_copy(acc.at[first:first + count], out_ref.at[row:row + count], sem)
                    cp.start()
                    cp.wait()

    blocks = jnp.asarray([c[0] for c in chunks], jnp.int32)
    grid_spec = pltpu.PrefetchScalarGridSpec(
        num_scalar_prefetch=1, grid=(len(chunks), nt),
        in_specs=[pl.BlockSpec((tt, width), lambda g, ti, blocks_ref: (ti, blocks_ref[g])),
                  pl.BlockSpec((tt, n), lambda g, ti, blocks_ref: (ti, 0))] + ([ANY] if out is not None else []),
        out_specs=ANY, scratch_shapes=[pltpu.VMEM((width, n), F32), pltpu.SemaphoreType.DMA])
    args = (blocks, a, b) + ((out,) if out is not None else ())
    return pl.pallas_call(body, name=name, grid_spec=grid_spec, out_shape=jax.ShapeDtypeStruct((rows_out, n), F32),
                          input_output_aliases={3: 0} if out is not None else {},
                          compiler_params=_params(("arbitrary", "arbitrary")))(*args)


ROW_QKV, ROW_KR, ROW_XC = 0, Q_LORA + KV_LORA, Q_LORA + KV_LORA + QK_ROPE
ROW_GB, ROW_GC, ROW_GL = ROW_XC + D_MODEL, ROW_XC + 2 * D_MODEL, ROW_XC + 3 * D_MODEL
BIG_FROM_ROWS = ((0, ROW_GB, D_MODEL), (D_MODEL, ROW_GL, 2 * D_MODEL), (3 * D_MODEL, ROW_XC, D_MODEL), (4 * D_MODEL, ROW_GC, D_MODEL))


def _inproj_fwd(x1, gain, w_in, w_kr, qa_gain, kva_gain, qh_gain, kh_gain, w_uq, w_uk, w_uv, w_uvt, rope, carried=None):
    t, d = x1.shape
    tm = TOKEN_TILE
    chunk = 512
    chunks = []
    for col, row, size in BIG_FROM_ROWS:
        chunks += [(col + o, row + o, chunk) for o in range(0, size, chunk)]
    of_head = [[c for k, c in enumerate(chunks) if k * N_HEADS // len(chunks) == hd] for hd in range(N_HEADS)]

    def body(x_ref, g_ref, win_ref, wkr_ref, qa_ref, kva_ref, qh_ref, kh_ref, wuq_ref, wuk_ref, wuv_ref, wuvt_ref, cos_ref, slo_ref,
             shi_ref, hb_ref, big_ref, lat_ref, q_ref, k_ref, v_ref, vt_ref):
        h, _ = _rms(x_ref[...], g_ref[...])
        hb = _bf(h)
        hb_ref[...] = hb
        k_rope = _mm_nt(hb, wkr_ref[...])
        lat = jnp.concatenate([_mm_nt(hb, win_ref[ROW_QKV:ROW_KR, :]), k_rope], axis=1)
        lat_ref[...] = lat
        cq, _ = _rms(lat[:, :Q_LORA], qa_ref[...])
        ckv, _ = _rms(lat[:, Q_LORA:Q_LORA + KV_LORA], kva_ref[...])
        cqb = _bf(cq)
        ckvb = _bf(ckv)
        rope_v = (cos_ref[...], slo_ref[...], shi_ref[...])
        q_all = _mm(cqb, wuq_ref[...])
        k_all = _mm(ckvb, wuk_ref[...])
        v_ref[...] = _bf(_mm(ckvb, wuv_ref[...]))
        vt_all = _mm_nt(wuvt_ref[...], ckvb)
        for hd in range(N_HEADS):
            for col, row, size in of_head[hd]:
                big_ref[:, col:col + size] = _mm_nt(hb, win_ref[row:row + size, :])
            lanes = slice(hd * HEAD_PAD, (hd + 1) * HEAD_PAD)
            qn, _ = _rms(q_all[:, lanes], qh_ref[...], QK_DIM)
            q_ref[hd] = _bf(_rope(qn, rope_v))
            kn, _ = _rms(k_all[:, lanes] + k_rope, kh_ref[...], QK_DIM)
            k_ref[hd] = _bf(_rope(kn, rope_v))
            vt_ref[hd] = _bf(vt_all[hd * V_DIM:(hd + 1) * V_DIM])

    tok = lambda c: pl.BlockSpec((tm, c), lambda i: (i, 0))
    head = lambda c: pl.BlockSpec((N_HEADS, tm, c), lambda i: (0, i, 0))
    return _pallas(
        body, "inproj_fwd", (t // tm,),
        [tok(d), _const((1, d)), _resident(w_in.shape), _resident(w_kr.shape), _const((1, Q_LORA)), _const((1, KV_LORA)),
         _const((1, HEAD_PAD)), _const((1, HEAD_PAD)), _resident(w_uq.shape), _resident(w_uk.shape),
         _resident(w_uv.shape), _resident(w_uvt.shape), tok(HEAD_PAD), tok(HEAD_PAD), tok(HEAD_PAD)],
        [tok(d), tok(BIG_COLS), tok(LAT_COLS), head(HEAD_PAD), head(HEAD_PAD), tok(N_HEADS * V_DIM),
         pl.BlockSpec((N_HEADS, V_DIM, tm), lambda i: (0, 0, i))],
        [jax.ShapeDtypeStruct((t, d), BF16), jax.ShapeDtypeStruct((t, BIG_COLS), F32),
         jax.ShapeDtypeStruct((t, LAT_COLS), F32), jax.ShapeDtypeStruct((N_HEADS, t, HEAD_PAD), BF16),
         jax.ShapeDtypeStruct((N_HEADS, t, HEAD_PAD), BF16), jax.ShapeDtypeStruct((t, N_HEADS * V_DIM), BF16),
         jax.ShapeDtypeStruct((N_HEADS, V_DIM, t), BF16)],
        (x1, gain, w_in, w_kr, qa_gain, kva_gain, qh_gain, kh_gain, w_uq, w_uk, w_uv, w_uvt, *rope), ("arbitrary",), carried)


EXP2_SCALE = ATTN_SCALE * 1.4426950408889634


def _diagonal_keep(tk, tq):
    return lax.broadcasted_iota(jnp.int32, (tk, tq), 0) <= lax.broadcasted_iota(jnp.int32, (tk, tq), 1)


def _attn_fwd(q, k, vt, seq, carried=None):
    _, t, _ = q.shape
    nseq = t // seq
    tq = tk = ATTN_TILE
    nq = seq // tq

    def body(q_ref, k_ref, vt_ref, o_ref, lse_ref):
        i = pl.program_id(1)
        qs = [q_ref[h] for h in range(N_HEADS)]
        keep = _diagonal_keep(tk, tq)

        def scores(h, k0):
            return _mm_nt(k_ref[h, pl.ds(k0, tk), :], qs[h])

        def update(h, st, state, k0, diagonal):
            m, l, acc = state
            if diagonal:
                st = jnp.where(keep, st, MASK_VALUE)
            m_new = jnp.maximum(m, jnp.max(st, axis=0, keepdims=True))
            pt = jnp.exp2((st - m_new) * EXP2_SCALE)
            alpha = jnp.exp2((m - m_new) * EXP2_SCALE)
            l_new = alpha * l + jnp.sum(pt, axis=0, keepdims=True)
            return m_new, l_new, alpha * acc + _mm(vt_ref[h, :, pl.ds(k0, tk)], _bf(pt))

        def tiles(states, k0, diagonal):
            st, new = scores(0, k0), []
            for h in range(N_HEADS):
                st_next = scores(h + 1, k0) if h + 1 < N_HEADS else None
                new.append(update(h, st, states[h], k0, diagonal))
                st = st_next
            return tuple(new)

        init = tuple((jnp.full((1, tq), MASK_VALUE, F32), jnp.zeros((1, tq), F32), jnp.zeros((V_DIM, tq), F32))
                     for _ in range(N_HEADS))
        states = lax.fori_loop(0, i, lambda j, s: tiles(s, pl.multiple_of(j * tk, tk), False), init)
        states = tiles(states, pl.multiple_of(i * tk, tk), True)
        outs = []
        for h in range(N_HEADS):
            m, l, acc = states[h]
            outs.append((acc / l).T)
            lse_ref[h] = m * EXP2_SCALE + jnp.log2(l)
        o_ref[...] = _bf(jnp.concatenate(outs, axis=-1))

    return _pallas(
        body, "attn_fwd", (nseq, nq),
        [pl.BlockSpec((N_HEADS, tq, HEAD_PAD), lambda b, i: (0, b * nq + i, 0)),
         pl.BlockSpec((N_HEADS, seq, HEAD_PAD), lambda b, i: (0, b, 0)),
         pl.BlockSpec((N_HEADS, V_DIM, seq), lambda b, i: (0, 0, b))],
        [pl.BlockSpec((tq, N_HEADS * V_DIM), lambda b, i: (b * nq + i, 0)),
         pl.BlockSpec((N_HEADS, 1, tq), lambda b, i: (0, 0, b * nq + i))],
        [jax.ShapeDtypeStruct((t, N_HEADS * V_DIM), BF16), jax.ShapeDtypeStruct((N_HEADS, 1, t), F32)],
        (q, k, vt), ("arbitrary", "arbitrary"), carried)


ATTN_BWD_HEADS = 4


def _attn_bwd(q, k, v, do, lse, delta, seq, carried=None):
    _, t, _ = q.shape
    nseq = t // seq
    tq = tk = ATTN_TILE
    n = seq // tq
    hb = ATTN_BWD_HEADS

    def body(q_ref, k_ref, v_ref, do_ref, lse_ref, delta_ref, dq_ref, dk_ref, dv_ref):
        dq_ref[...] = jnp.zeros_like(dq_ref)
        dk_ref[...] = jnp.zeros_like(dk_ref)
        dv_ref[...] = jnp.zeros_like(dv_ref)
        keep = _diagonal_keep(tk, tq)

        def tile(h, k0, q0, diagonal):
            kj = k_ref[h, pl.ds(k0, tk), :]
            qi = q_ref[h, pl.ds(q0, tq), :]
            doi = _bf(do_ref[pl.ds(q0, tq), h * V_DIM:(h + 1) * V_DIM])
            st = _mm_nt(kj, qi)
            if diagonal:
                st = jnp.where(keep, st, MASK_VALUE)
            pt = jnp.exp2(st * EXP2_SCALE - lse_ref[h, :, pl.ds(q0, tq)])
            dv_ref[pl.ds(k0, tk), h * V_DIM:(h + 1) * V_DIM] += _mm(_bf(pt), doi)
            dpt = _mm_nt(v_ref[pl.ds(k0, tk), h * V_DIM:(h + 1) * V_DIM], doi)
            dst = _bf((pt * (dpt - delta_ref[pl.ds(h, 1), pl.ds(q0, tq)])) * ATTN_SCALE)
            dk_ref[h, pl.ds(k0, tk), :] += _mm(dst, qi)
            dq_ref[h, pl.ds(q0, tq), :] += _mm_tn(dst, kj)

        def kv_step(j, _):
            k0 = pl.multiple_of(j * tk, tk)
            for h in range(hb):
                tile(h, k0, k0, True)

            def q_step(i, _):
                q0 = pl.multiple_of(i * tq, tq)
                for h in range(hb):
                    tile(h, k0, q0, False)
                return 0

            lax.fori_loop(j + 1, n, q_step, 0)
            return 0

        lax.fori_loop(0, n, kv_step, 0)

    hspec = lambda c: pl.BlockSpec((hb, seq, c), lambda b, g: (g, b, 0))
    cols = pl.BlockSpec((seq, hb * V_DIM), lambda b, g: (b, g))
    return _pallas(
        body, "attn_bwd", (nseq, N_HEADS // hb),
        [hspec(HEAD_PAD), hspec(HEAD_PAD), cols, cols,
         pl.BlockSpec((hb, 1, seq), lambda b, g: (g, 0, b)), pl.BlockSpec((None, hb, seq), lambda b, g: (g, 0, b))],
        [hspec(HEAD_PAD), hspec(HEAD_PAD), cols],
        [jax.ShapeDtypeStruct((N_HEADS, t, HEAD_PAD), F32), jax.ShapeDtypeStruct((N_HEADS, t, HEAD_PAD), F32),
         jax.ShapeDtypeStruct((t, N_HEADS * V_DIM), F32)],
        (q, k, v, do, lse, delta), ("arbitrary", "arbitrary"), carried)


def _merged_mixers(o_ref, gb_ref, gla_ref, glb_ref, xc_ref, gc_ref, xcp_ref, gcp_ref, bias_ref, cw_ref, wpa_ref, wpc_ref,
                   first_of_seq):
    y_a = _mm(o_ref[...], wpa_ref[...])
    gb = gb_ref[...]
    u = gc_ref[...] * xc_ref[...]
    u_prev = jnp.where(first_of_seq, 0.0, gcp_ref[...] * xcp_ref[...])
    cw = cw_ref[...]
    z = cw[2:3] * u + cw[1:2] * _shift_down(u, u_prev, 1) + cw[0:1] * _shift_down(u, u_prev, 2)
    gbz = _bf(gb * z)
    y_b = _mm(gbz, wpc_ref[...])
    bias = bias_ref[...]
    gate_a = _sigmoid(gla_ref[...] + bias[:, :D_MODEL])
    gate_b = _sigmoid(glb_ref[...] + bias[:, D_MODEL:])
    return _bf(gate_a * y_a + gate_b * y_b)


def _mixer_specs(tm, seq):
    d = D_MODEL
    tok = pl.BlockSpec((tm, d), lambda i: (i, 0))
    col = lambda c: pl.BlockSpec((tm, d), lambda i: (i, c))
    prev = lambda c: pl.BlockSpec((8, d), lambda i: (jnp.maximum(i * (tm // 8) - 1, 0), c))
    o_spec = pl.BlockSpec((tm, N_HEADS * V_DIM), lambda i: (i, 0))
    fwd_specs = [o_spec, col(0), col(1), col(2), col(3), col(4), prev(3), prev(4), _const((1, 2 * d)), _const((3, d)),
                 _resident((N_HEADS * V_DIM, d)), _resident((d, d)), _resident((d, d))]
    return tok, fwd_specs


def _mix_fwd(x1, o, big, gate_bias, conv_w, w_pa, w_pc, w_out, seq, carried=None):
    t, d = x1.shape
    tm = WIDE_TILE
    tiles_per_seq = seq // tm

    def body(x_ref, o_ref, gb_ref, gla_ref, glb_ref, xc_ref, gc_ref, xcp_ref, gcp_ref, bias_ref, cw_ref, wpa_ref, wpc_ref,
             wout_ref, x2_ref):
        first = pl.program_id(0) % tiles_per_seq == 0
        merged = _merged_mixers(o_ref, gb_ref, gla_ref, glb_ref, xc_ref, gc_ref, xcp_ref, gcp_ref, bias_ref, cw_ref, wpa_ref,
                                wpc_ref, first)
        x2_ref[...] = x_ref[...] + _mm(merged, wout_ref[...])

    tok, fwd_specs = _mixer_specs(tm, seq)
    return _pallas(body, "mix_fwd", (t // tm,), [tok] + fwd_specs, [tok], [jax.ShapeDtypeStruct((t, d), F32)],
                   (x1, o, big, big, big, big, big, big, big, gate_bias, conv_w, w_pa, w_pc, w_out), ("arbitrary",), carried)[0]


def _mix_bwd(dx2, o, big, gate_bias, conv_w, w_pa, w_pc, w_out, seq, carried=None):
    t, d = dx2.shape
    tm = TOKEN_TILE
    tiles_per_seq = seq // tm
    hv = N_HEADS * V_DIM

    def body(dx_ref, o_ref, gb_ref, gla_ref, glb_ref, xc_ref, gc_ref, xcp_ref, gcp_ref, bias_ref, cw_ref, wpa_ref, wpc_ref,
             wout_ref, do_ref, delta_ref, dz_ref, dm_ref, dbias_ref, dwpa_ref, dwpc_ref, dwout_ref):
        @pl.when(pl.program_id(0) == 0)
        def _():
            dbias_ref[...] = jnp.zeros_like(dbias_ref)
            dwpa_ref[...] = jnp.zeros_like(dwpa_ref)
            dwpc_ref[...] = jnp.zeros_like(dwpc_ref)
            dwout_ref[...] = jnp.zeros_like(dwout_ref)

        first = pl.program_id(0) % tiles_per_seq == 0
        dxb = _bf(dx_ref[...])
        dmerged = _mm_nt(dxb, wout_ref[...])
        y_a = _mm(o_ref[...], wpa_ref[...])
        bias = bias_ref[...]
        gate_a = _sigmoid(gla_ref[...] + bias[:, :d])
        gate_b = _sigmoid(glb_ref[...] + bias[:, d:])
        dya = _bf(dmerged * gate_a)
        dyb = _bf(dmerged * gate_b)
        do_v = _mm_nt(dya, wpa_ref[...])
        dgz = _mm_nt(dyb, wpc_ref[...])
        dwpa_ref[...] += _mm_tn(o_ref[...], dya)
        gb = gb_ref[...]
        u = gc_ref[...] * xc_ref[...]
        u_prev = jnp.where(first, 0.0, gcp_ref[...] * xcp_ref[...])
        cw = cw_ref[...]
        z = cw[2:3] * u + cw[1:2] * _shift_down(u, u_prev, 1) + cw[0:1] * _shift_down(u, u_prev, 2)
        gbz = _bf(gb * z)
        y_b = _mm(gbz, wpc_ref[...])
        dwpc_ref[...] += _mm_tn(gbz, dyb)
        do_ref[...] = do_v
        head = lax.broadcasted_iota(jnp.int32, (N_HEADS, hv), 0) * V_DIM
        col = lax.broadcasted_iota(jnp.int32, (N_HEADS, hv), 1)
        in_head = ((col >= head) & (col < head + V_DIM)).astype(F32)
        delta_ref[...] = lax.dot_general(in_head, do_v * o_ref[...].astype(F32), (((1,), (1,)), ((), ())),
                                         precision=lax.Precision.HIGHEST, preferred_element_type=F32)
        dz_ref[...] = dgz * gb
        dm_ref[:, :d] = _bf(dgz * z)
        merged = _bf(gate_a * y_a + gate_b * y_b)
        dwout_ref[...] += _mm_tn(merged, dxb)
        dla = (dmerged * y_a) * (gate_a * (1.0 - gate_a))
        dlb = (dmerged * y_b) * (gate_b * (1.0 - gate_b))
        dbias_ref[:, :d] += jnp.sum(dla, axis=0, keepdims=True)
        dbias_ref[:, d:] += jnp.sum(dlb, axis=0, keepdims=True)
        dm_ref[:, d:2 * d] = _bf(dla)
        dm_ref[:, 2 * d:] = _bf(dlb)

    tok, fwd_specs = _mixer_specs(tm, seq)
    return _pallas(
        body, "mix_bwd", (t // tm,), [tok] + fwd_specs,
        [pl.BlockSpec((tm, hv), lambda i: (i, 0)), pl.BlockSpec((N_HEADS, tm), lambda i: (0, i)), tok,
         pl.BlockSpec((tm, M_COLS), lambda i: (i, 0)), _const((1, 2 * d)), _const((hv, d)), _const((d, d)), _const((d, d))],
        [jax.ShapeDtypeStruct((t, hv), F32), jax.ShapeDtypeStruct((N_HEADS, t), F32), jax.ShapeDtypeStruct((t, d), F32),
         jax.ShapeDtypeStruct((t, M_COLS), BF16), jax.ShapeDtypeStruct((1, 2 * d), F32), jax.ShapeDtypeStruct((hv, d), F32),
         jax.ShapeDtypeStruct((d, d), F32), jax.ShapeDtypeStruct((d, d), F32)],
        (dx2, o, big, big, big, big, big, big, big, gate_bias, conv_w, w_pa, w_pc, w_out), ("arbitrary",), carried)


def _prep_bwd(lat, big, dz, dq, dk, dv, qa_gain, kva_gain, qh_gain, kh_gain, w_uq, w_uk, w_uv, rope, conv_w, seq, carried=None):
    t = lat.shape[0]
    d = D_MODEL
    tm = WIDE_TILE
    tiles_per_seq = seq // tm
    last_blk = t // 8 - 1

    def body(lat_ref, xc_ref, gc_ref, dz_ref, dzn_ref, dq_ref, dk_ref, dv_ref, qa_ref, kva_ref, qh_ref, kh_ref, wuq_ref, wuk_ref,
             wuv_ref, cos_ref, slo_ref, shi_ref, cw_ref,
             dp_ref, dwuq_ref, dwuk_ref, dwuv_ref, dqa_ref, dkva_ref, dqh_ref, dkh_ref, dcw_ref):
        pid = pl.program_id(0)

        @pl.when(pid == 0)
        def _():
            for r in (dwuq_ref, dwuk_ref, dwuv_ref, dqa_ref, dkva_ref, dqh_ref, dkh_ref, dcw_ref):
                r[...] = jnp.zeros_like(r)

        lat_v = lat_ref[...]
        q_lat = lat_v[:, :Q_LORA]
        kv_lat = lat_v[:, Q_LORA:Q_LORA + KV_LORA]
        k_rope = lat_v[:, Q_LORA + KV_LORA:]
        qa_gain_v = qa_ref[...]
        kva_gain_v = kva_ref[...]
        qh_gain_v = qh_ref[...]
        kh_gain_v = kh_ref[...]
        cq, rq = _rms(q_lat, qa_gain_v)
        ckv, rkv = _rms(kv_lat, kva_gain_v)
        cqb = _bf(cq)
        ckvb = _bf(ckv)
        rope_v = (cos_ref[...], slo_ref[...], shi_ref[...])
        lane = lax.broadcasted_iota(jnp.int32, (tm, HEAD_PAD), 1)
        rope_lanes = (lane >= QK_NOPE) & (lane < QK_DIM)
        dk_rope = jnp.zeros((tm, HEAD_PAD), F32)
        dqh_gain = jnp.zeros((1, HEAD_PAD), F32)
        dkh_gain = jnp.zeros((1, HEAD_PAD), F32)
        q_all = _mm(cqb, wuq_ref[...])
        k_all = _mm(ckvb, wuk_ref[...])
        dvb = _bf(dv_ref[...])
        dckv = _mm_nt(dvb, wuv_ref[...])
        dwuv_ref[...] += _mm_tn(ckvb, dvb)

        last = pid % tiles_per_seq == tiles_per_seq - 1
        dzv = dz_ref[...]
        dz_next = jnp.where(last, 0.0, dzn_ref[...])
        dz1 = _shift_up(dzv, dz_next, 1)
        dz2 = _shift_up(dzv, dz_next, 2)
        cw = cw_ref[...]
        xc = xc_ref[...]
        gc = gc_ref[...]
        u = gc * xc
        du = cw[2:3] * dzv + cw[1:2] * dz1 + cw[0:1] * dz2
        dp_ref[:, :d] = _bf(du * gc)
        dp_ref[:, d:2 * d] = _bf(du * xc)
        dcw_ref[0:1, :] += jnp.sum(dz2 * u, axis=0, keepdims=True)
        dcw_ref[1:2, :] += jnp.sum(dz1 * u, axis=0, keepdims=True)
        dcw_ref[2:3, :] += jnp.sum(dzv * u, axis=0, keepdims=True)

        dcq = jnp.zeros((tm, Q_LORA), F32)
        half = N_HEADS // 2
        for part in range(2):
            dq_heads, dk_heads = [], []
            for hd in range(part * half, (part + 1) * half):
                lanes = slice(hd * HEAD_PAD, (hd + 1) * HEAD_PAD)
                q_pre = q_all[:, lanes]
                _, rr = _rms(q_pre, qh_gain_v, QK_DIM)
                dq_pre, dg = _rms_bwd(q_pre, rr, qh_gain_v, _rope_bwd(dq_ref[hd], rope_v), QK_DIM)
                dqh_gain = dqh_gain + dg
                dq_heads.append(_bf(dq_pre))

                k_pre = k_all[:, lanes] + k_rope
                _, rr = _rms(k_pre, kh_gain_v, QK_DIM)
                dk_pre, dg = _rms_bwd(k_pre, rr, kh_gain_v, _rope_bwd(dk_ref[hd], rope_v), QK_DIM)
                dkh_gain = dkh_gain + dg
                dk_rope = dk_rope + jnp.where(rope_lanes, dk_pre, 0.0)
                dk_heads.append(_bf(dk_pre))
            dq_part = jnp.concatenate(dq_heads, axis=1)
            dk_part = jnp.concatenate(dk_heads, axis=1)
            cols = slice(part * half * HEAD_PAD, (part + 1) * half * HEAD_PAD)
            dcq = dcq + _mm_nt(dq_part, wuq_ref[:, cols])
            dckv = dckv + _mm_nt(dk_part, wuk_ref[:, cols])
            dwuq_ref[:, cols] += _mm_tn(cqb, dq_part)
            dwuk_ref[:, cols] += _mm_tn(ckvb, dk_part)
        dqh_ref[...] += dqh_gain
        dkh_ref[...] += dkh_gain
        dq_lat, dg = _rms_bwd(q_lat, rq, qa_gain_v, dcq)
        dqa_ref[...] += dg
        dkv_lat, dg = _rms_bwd(kv_lat, rkv, kva_gain_v, dckv)
        dkva_ref[...] += dg
        dp_ref[:, 2 * d:2 * d + Q_LORA] = _bf(dq_lat)
        dp_ref[:, 2 * d + Q_LORA:2 * d + Q_LORA + KV_LORA] = _bf(dkv_lat)
        dp_ref[:, 2 * d + Q_LORA + KV_LORA:] = _bf(dk_rope)

    tok = lambda c: pl.BlockSpec((tm, c), lambda i: (i, 0))
    col = lambda c: pl.BlockSpec((tm, d), lambda i: (i, c))
    head = lambda c: pl.BlockSpec((N_HEADS, tm, c), lambda i: (0, i, 0))
    nxt = pl.BlockSpec((8, d), lambda i: (jnp.minimum((i + 1) * (tm // 8), last_blk), 0))
    return _pallas(
        body, "prep_bwd", (t // tm,),
        [tok(LAT_COLS), col(3), col(4), tok(d), nxt, head(HEAD_PAD), head(HEAD_PAD), tok(N_HEADS * V_DIM),
         _const((1, Q_LORA)), _const((1, KV_LORA)), _const((1, HEAD_PAD)), _const((1, HEAD_PAD)),
         _resident(w_uq.shape), _resident(w_uk.shape), _resident(w_uv.shape), tok(HEAD_PAD), tok(HEAD_PAD), tok(HEAD_PAD),
         _const((3, d))],
        [tok(P_COLS), _const(w_uq.shape), _const(w_uk.shape), _const(w_uv.shape), _const((1, Q_LORA)),
         _const((1, KV_LORA)), _const((1, HEAD_PAD)), _const((1, HEAD_PAD)), _const((3, d))],
        [jax.ShapeDtypeStruct((t, P_COLS), BF16), jax.ShapeDtypeStruct(w_uq.shape, F32),
         jax.ShapeDtypeStruct(w_uk.shape, F32), jax.ShapeDtypeStruct(w_uv.shape, F32),
         jax.ShapeDtypeStruct((1, Q_LORA), F32), jax.ShapeDtypeStruct((1, KV_LORA), F32),
         jax.ShapeDtypeStruct((1, HEAD_PAD), F32), jax.ShapeDtypeStruct((1, HEAD_PAD), F32), jax.ShapeDtypeStruct((3, d), F32)],
        (lat, big, big, dz, dz, dq, dk, dv, qa_gain, kva_gain, qh_gain, kh_gain, w_uq, w_uk, w_uv, *rope, conv_w),
        ("arbitrary",), carried)


def _inproj_bwd(x1, gain, dx2, dm, dp, w_in, w_kr, carried=None):
    t, d = x1.shape
    tm = TOKEN_TILE

    def body(x_ref, g_ref, dx2_ref, dm_ref, dp_ref, win_ref, wkr_ref, dx1_ref, dgain_ref):
        xv = x_ref[...]
        gain_v = g_ref[...]
        _, r = _rms(xv, gain_v)
        dh = (_mm(dm_ref[:, :d], win_ref[ROW_GB:ROW_GC, :]) + _mm(dm_ref[:, d:], win_ref[ROW_GL:, :])
              + _mm(dp_ref[:, :d], win_ref[ROW_XC:ROW_GB, :]) + _mm(dp_ref[:, d:2 * d], win_ref[ROW_GC:ROW_GL, :])
              + _mm(dp_ref[:, 2 * d:2 * d + ROW_KR], win_ref[ROW_QKV:ROW_KR, :]) + _mm(dp_ref[:, 2 * d + ROW_KR:], wkr_ref[...]))
        dxn, dgain = _rms_bwd(xv, r, gain_v, dh)
        dx1_ref[...] = dx2_ref[...] + dxn

        @pl.when(pl.program_id(0) == 0)
        def _():
            dgain_ref[...] = jnp.zeros_like(dgain_ref)

        dgain_ref[...] += dgain

    tok = lambda c: pl.BlockSpec((tm, c), lambda i: (i, 0))
    return _pallas(
        body, "inproj_bwd", (t // tm,),
        [tok(d), _const((1, d)), tok(d), tok(M_COLS), tok(P_COLS), _resident(w_in.shape), _resident(w_kr.shape)],
        [tok(d), _const((1, d))], [jax.ShapeDtypeStruct((t, d), F32), jax.ShapeDtypeStruct((1, d), F32)],
        (x1, gain, dx2, dm, dp, w_in, w_kr), ("arbitrary",), carried)


def _adamw(quads, name, carried=None):
    k = len(quads)
    rows, cols = quads[0][0].shape
    tr, tc = rows, cols
    for cand in (512, 352, 256, 192, 128, 64):
        if rows % cand == 0 and rows > cand:
            tr = cand
            break
    if tr == rows and rows * cols > 512 * 1024 and cols % 256 == 0:
        tc = 256
    while k * 14 * tr * tc * 4 > (VMEM_LIMIT * 3) // 4 and tr % 16 == 0:
        tr //= 2

    def body(*refs):
        for i in range(k):
            w_ref, g_ref, m_ref, v_ref = refs[4 * i:4 * i + 4]
            delta_ref, nm_ref, nv_ref = refs[4 * k + 3 * i:4 * k + 3 * i + 3]
            delta_ref[...], nm_ref[...], nv_ref[...] = _adamw_update(w_ref[...], g_ref[...], m_ref[...], v_ref[...])

    spec = pl.BlockSpec((tr, tc), lambda i, j: (i, j))
    shape = jax.ShapeDtypeStruct((rows, cols), F32)
    outs = _pallas(body, name, (rows // tr, cols // tc), [spec] * (4 * k), [spec] * (3 * k), [shape] * (3 * k),
                   [a for quad in quads for a in quad], ("arbitrary", "arbitrary"), carried)
    return [tuple(outs[3 * i:3 * i + 3]) for i in range(k)]


def _adamw_update(w, g, m, v):
    nm = ADAM_B1 * m + (1.0 - ADAM_B1) * g
    nv = ADAM_B2 * v + (1.0 - ADAM_B2) * (g * g)
    m_hat = nm * (1.0 / (1.0 - ADAM_B1 ** ADAM_STEP))
    v_hat = nv * (1.0 / (1.0 - ADAM_B2 ** ADAM_STEP))
    return -ADAM_LR * (m_hat / (jnp.sqrt(v_hat) + ADAM_EPS) + ADAM_WD * w), nm, nv


def _adamw_whole(quads, name):
    k = len(quads)

    def body(*refs):
        for i in range(k):
            w_ref, g_ref, m_ref, v_ref = refs[4 * i:4 * i + 4]
            g_out, delta_ref, nm_ref, nv_ref = refs[4 * k + 4 * i:4 * k + 4 * i + 4]
            gv = g_ref[...]
            g_out[...] = gv
            delta_ref[...], nm_ref[...], nv_ref[...] = _adamw_update(w_ref[...], gv, m_ref[...], v_ref[...])

    vm = pl.BlockSpec(memory_space=pltpu.VMEM)
    outs = pl.pallas_call(body, name=name, in_specs=[vm] * (4 * k), out_specs=[vm] * (4 * k),
                          out_shape=[jax.ShapeDtypeStruct(q[0].shape, F32) for q in quads for _ in range(4)],
                          compiler_params=_params())(*[a for quad in quads for a in quad])
    return [tuple(outs[4 * i:4 * i + 4]) for i in range(k)]


def _adamw_small(packed_grads, triples, segments):
    k = len(triples)

    def body(*refs):
        g_ref = refs[0]
        off = 0
        for i in range(k):
            w_ref, m_ref, v_ref = refs[1 + 3 * i:4 + 3 * i]
            g_out, delta_ref, nm_ref, nv_ref = refs[1 + 3 * k + 4 * i:5 + 3 * k + 4 * i]
            gv = g_ref[:, off:off + w_ref.shape[1]]
            g_out[...] = gv
            delta_ref[...], nm_ref[...], nv_ref[...] = _adamw_update(w_ref[...], gv, m_ref[...], v_ref[...])
            off += segments[i]

    vm = pl.BlockSpec(memory_space=pltpu.VMEM)
    outs = pl.pallas_call(
        body, name="adamw_small", in_specs=[vm] * (1 + 3 * k), out_specs=[vm] * (4 * k),
        out_shape=[jax.ShapeDtypeStruct(w.shape, F32) for w, _, _ in triples for _ in range(4)],
    )(packed_grads, *[a for triple in triples for a in triple])
    return [tuple(outs[4 * i:4 * i + 4]) for i in range(k)]


def _place():
    x, y, c = lax.axis_index("x"), lax.axis_index("y"), lax.axis_index("c")
    other_chips = [(1 - x, y), (x, 1 - y), (1 - x, 1 - y)]
    return x, y, c, other_chips


def _remote(src, dst, sems, send, recv, device):
    return pltpu.make_async_remote_copy(src_ref=src, dst_ref=dst, send_sem=sems.at[send], recv_sem=sems.at[recv],
                                        device_id=device, device_id_type=MESH_ID)


def _cast_shards(shards, out_dtypes, n_first):
    n = len(shards)
    out_shape = [jax.ShapeDtypeStruct((N_CHIPS,) + s.shape, dt) for s, dt in zip(shards, out_dtypes)]
    gather = _gather_carried(out_shape[:n_first])

    def body(*refs):
        ins, outs, stage, sems = refs[:n], refs[n:2 * n], refs[2 * n:3 * n], refs[3 * n]
        x, y, _, _ = _place()
        me = 2 * x + y

        def cast(first, last):
            copies = []
            for w in range(first, last):
                stage[w][...] = ins[w][...].astype(out_dtypes[w])
                copies.append(pltpu.make_async_copy(stage[w], outs[w].at[me], sems.at[w]))
                copies[-1].start()
            for cp in copies:
                cp.wait()

        cast(0, n_first)
        gather.start(None, outs[:n_first], sems, n)
        cast(n_first, n)
        gather.finish(None, outs[:n_first], sems, n)

    vm = pl.BlockSpec(memory_space=pltpu.VMEM)
    return pl.pallas_call(
        body, name="cast_shards", in_specs=[vm] * n, out_specs=[ANY] * n, out_shape=out_shape,
        scratch_shapes=[pltpu.VMEM(s.shape, dt) for s, dt in zip(shards, out_dtypes)] + [pltpu.SemaphoreType.DMA((n + gather.n_sems,))],
        compiler_params=_params())(*shards)


BF16_ROWS = 16


def _split_rows(rows):
    return (rows // 2) % BF16_ROWS == 0


def _half_shape(rows, cols):
    return (rows // 2, cols) if _split_rows(rows) else (rows, cols // 2)


def _half(rows, cols, which):
    if _split_rows(rows):
        return (pl.ds(pl.multiple_of(which * (rows // 2), BF16_ROWS), rows // 2), slice(None))
    return (slice(None), pl.ds(pl.multiple_of(which * (cols // 2), 128), cols // 2))


def _gather_carried(bufs):
    n = len(bufs)

    def half(w, slot, which):
        _, rows, cols = bufs[w].shape
        return (slot,) + _half(rows, cols, which)

    def start(ins, outs, sems, base):
        x, y, c, other_chips = _place()
        me = 2 * x + y
        for w in range(n):
            mine = outs[w].at[half(w, me, c)]
            for p, (px, py) in enumerate(other_chips):
                _remote(mine, mine, sems, base + 12 * w + p, base + 12 * w + 3 + p, (px, py, c)).start()

    def finish(ins, outs, sems, base):
        x, y, c, other_chips = _place()
        me = 2 * x + y
        for w in range(n):
            for p, (px, py) in enumerate(other_chips):
                got = outs[w].at[half(w, 2 * px + py, c)]
                _remote(got, got, sems, base + 12 * w + p, base + 12 * w + 3 + p, (px, py, c)).wait_recv()
                _remote(got, got, sems, base + 12 * w + 6 + p, base + 12 * w + 9 + p, (x, y, 1 - c)).start()
        for w in range(n):
            mine = outs[w].at[half(w, me, c)]
            for p, (px, py) in enumerate(other_chips):
                got = outs[w].at[half(w, 2 * px + py, c)]
                theirs = outs[w].at[half(w, 2 * px + py, 1 - c)]
                _remote(got, theirs, sems, base + 12 * w + 6 + p, base + 12 * w + 9 + p, (x, y, 1 - c)).wait()
                _remote(mine, mine, sems, base + 12 * w + p, base + 12 * w + 3 + p, (px, py, c)).wait_send()

    shapes = [jax.ShapeDtypeStruct(b.shape, b.dtype) for b in bufs]
    return _Carried(bufs, shapes, {w: w for w in range(n)}, 12 * n, start, finish)


def _swap_carried(grads):
    n = len(grads)

    def copy(w, ins, outs, sems, base):
        x, y, c, _ = _place()
        _, rows, cols = grads[w].shape
        theirs = ins[w].at[(slice(None),) + _half(rows, cols, 1 - c)]
        return _remote(theirs, outs[w], sems, base + 2 * w, base + 2 * w + 1, (x, y, 1 - c))

    def start(ins, outs, sems, base):
        for w in range(n):
            copy(w, ins, outs, sems, base).start()

    def finish(ins, outs, sems, base):
        for w in range(n):
            copy(w, ins, outs, sems, base).wait()

    shapes = [jax.ShapeDtypeStruct((g.shape[0],) + _half_shape(*g.shape[1:]), F32) for g in grads]
    return _Carried(grads, shapes, {}, 2 * n, start, finish)


def _row_tile(rows):
    for cand in (512, 352, 256, 192, 128, 96, 64, 32, 16):
        if rows % cand == 0:
            return cand
    return rows


def _half_block_index(split_rows, tiles, i, core):
    return (core * tiles + i, 0) if split_rows else (i, core)


def _chip_partial(grad, other, place, name):
    nblk, hr, hc = other.shape
    by_rows = _split_rows(grad.shape[1])
    tr = _row_tile(hr)
    tiles = hr // tr

    def body(place_ref, g_ref, o_ref, own_ref, sum_bf_ref):
        s = g_ref[...] + o_ref[...]
        sum_bf_ref[...] = _bf(s)

        @pl.when(pl.program_id(1) == place_ref[0])
        def _():
            own_ref[...] = s

    grid_spec = pltpu.PrefetchScalarGridSpec(
        num_scalar_prefetch=1, grid=(tiles, nblk),
        in_specs=[pl.BlockSpec((None, tr, hc), lambda i, b, place_ref: (b,) + _half_block_index(by_rows, tiles, i, place_ref[1])),
                  pl.BlockSpec((None, tr, hc), lambda i, b, place_ref: (b, i, 0))],
        out_specs=[pl.BlockSpec((tr, hc), lambda i, b, place_ref: (i, 0)),
                   pl.BlockSpec((None, tr, hc), lambda i, b, place_ref: (b, i, 0))])
    return pl.pallas_call(body, name=name, grid_spec=grid_spec,
                          out_shape=[jax.ShapeDtypeStruct((hr, hc), F32), jax.ShapeDtypeStruct(other.shape, BF16)],
                          compiler_params=_params(("arbitrary", "arbitrary")))(place, grad, other)


def _chip_partial_small(grads, others, place):
    n = len(grads)

    def body(*refs):
        place_ref, g_refs, o_refs = refs[0], refs[1:1 + n], refs[1 + n:1 + 2 * n]
        own_refs, bf_refs = refs[1 + 2 * n:1 + 3 * n], refs[1 + 3 * n:]
        chip, core = place_ref[0], place_ref[1]
        for w in range(n):
            _, rows, cols = grads[w].shape
            half = _half(rows, cols, core)
            bf_refs[w][...] = _bf(g_refs[w][(slice(None),) + half] + o_refs[w][...])
            own_refs[w][...] = g_refs[w][(chip,) + half] + o_refs[w][chip]

    vm = pl.BlockSpec(memory_space=pltpu.VMEM)
    outs = pl.pallas_call(
        body, name="chip_partial_small", in_specs=[pl.BlockSpec(memory_space=pltpu.SMEM)] + [vm] * (2 * n), out_specs=[vm] * (2 * n),
        out_shape=[jax.ShapeDtypeStruct(o.shape[1:], F32) for o in others] + [jax.ShapeDtypeStruct(o.shape, BF16) for o in others],
        compiler_params=_params())(place, *grads, *others)
    return list(zip(outs[:n], outs[n:]))


def _chip_total_small(owns, receiveds, place, shapes):
    n = len(owns)

    def body(*refs):
        place_ref, own_refs, r_refs, out_refs = refs[0], refs[1:1 + n], refs[1 + n:1 + 2 * n], refs[1 + 2 * n:]
        chip, core = place_ref[0], place_ref[1]
        for w in range(n):
            r = [r_refs[w][(chip + k) % N_CHIPS].astype(F32) for k in (1, 2, 3)]
            out_refs[w][_half(*shapes[w], core)] = own_refs[w][...] + ((r[0] + r[1]) + r[2])

    vm = pl.BlockSpec(memory_space=pltpu.VMEM)
    return list(pl.pallas_call(
        body, name="chip_total_small", in_specs=[pl.BlockSpec(memory_space=pltpu.SMEM)] + [vm] * (2 * n), out_specs=[vm] * n,
        out_shape=[jax.ShapeDtypeStruct(tuple(s), F32) for s in shapes], compiler_params=_params())(place, *owns, *receiveds))


def _send_carried(partials):
    n = len(partials)

    def start(ins, outs, sems, base):
        x, y, c, other_chips = _place()
        me = 2 * x + y
        for w in range(n):
            for p, (px, py) in enumerate(other_chips):
                _remote(ins[w].at[2 * px + py], outs[w].at[me], sems, base + 6 * w + p, base + 6 * w + 3 + p, (px, py, c)).start()

    def finish(ins, outs, sems, base):
        x, y, c, other_chips = _place()
        for w in range(n):
            for p, (px, py) in enumerate(other_chips):
                _remote(ins[w].at[2 * px + py], outs[w].at[2 * px + py], sems, base + 6 * w + p, base + 6 * w + 3 + p,
                        (px, py, c)).wait()

    return _Carried(partials, [jax.ShapeDtypeStruct(p.shape, BF16) for p in partials], {}, 6 * n, start, finish)


def _chip_total(own, received, place, shape, name):
    hr, hc = own.shape
    by_rows = _split_rows(shape[0])
    tr = _row_tile(hr)
    tiles = hr // tr

    def body(place_ref, own_ref, r1_ref, r2_ref, r3_ref, out_ref):
        out_ref[...] = own_ref[...] + ((r1_ref[...].astype(F32) + r2_ref[...].astype(F32)) + r3_ref[...].astype(F32))

    def slot(k):
        return pl.BlockSpec((None, tr, hc), lambda i, place_ref: ((place_ref[0] + k) % N_CHIPS, i, 0))

    grid_spec = pltpu.PrefetchScalarGridSpec(
        num_scalar_prefetch=1, grid=(tiles,), in_specs=[pl.BlockSpec((tr, hc), lambda i, place_ref: (i, 0)), slot(1), slot(2), slot(3)],
        out_specs=pl.BlockSpec((tr, hc), lambda i, place_ref: _half_block_index(by_rows, tiles, i, place_ref[1])))
    return pl.pallas_call(body, name=name, grid_spec=grid_spec, out_shape=jax.ShapeDtypeStruct(tuple(shape), F32),
                          compiler_params=_params(("arbitrary",)))(place, own, received, received, received)


def _join_carried(totals):
    n = len(totals)

    def copy(w, outs, sems, base):
        x, y, c, _ = _place()
        mine = outs[w].at[_half(*totals[w].shape, c)]
        return _remote(mine, mine, sems, base + 2 * w, base + 2 * w + 1, (x, y, 1 - c))

    def start(ins, outs, sems, base):
        for w in range(n):
            copy(w, outs, sems, base).start()

    def finish(ins, outs, sems, base):
        for w in range(n):
            copy(w, outs, sems, base).wait()

    shapes = [jax.ShapeDtypeStruct(a.shape, F32) for a in totals]
    return _Carried(totals, shapes, {w: w for w in range(n)}, 2 * n, start, finish)


def _sum_devices(vec):
    rows, n = vec.shape

    def body(v_ref, out_ref, buf, send_sems, recv_sems):
        x, y, c, _ = _place()
        me = 4 * x + 2 * y + c
        buf[me] = v_ref[...]
        sends = []
        for k in range(1, N_DEV):
            peer = (1 - x if k & 4 else x, 1 - y if k & 2 else y, 1 - c if k & 1 else c)
            cp = pltpu.make_async_remote_copy(src_ref=v_ref, dst_ref=buf.at[me], send_sem=send_sems.at[k], recv_sem=recv_sems.at[k],
                                              device_id=peer, device_id_type=MESH_ID)
            cp.start()
            sends.append(cp)
        for cp in sends:
            cp.wait()
        total = buf[0]
        for dev in range(1, N_DEV):
            total = total + buf[dev]
        out_ref[...] = total

    vm = pl.BlockSpec(memory_space=pltpu.VMEM)
    return pl.pallas_call(
        body, name="sum_devices", in_specs=[vm], out_specs=vm, out_shape=jax.ShapeDtypeStruct((rows, n), F32),
        scratch_shapes=[pltpu.VMEM((N_DEV, rows, n), F32), pltpu.SemaphoreType.DMA((N_DEV,)), pltpu.SemaphoreType.DMA((N_DEV,))],
    )(vec)


def _rope_tables(positions):
    half = ROPE_HALF
    inv_freq = 1.0 / (ROPE_THETA ** (jnp.arange(half, dtype=F32) / half))
    ang = positions.astype(F32).reshape(-1, 1) * inv_freq
    cos, sin = jnp.cos(ang), jnp.sin(ang)
    t = ang.shape[0]
    ones, zeros = jnp.ones((t, QK_NOPE), F32), jnp.zeros((t, QK_NOPE), F32)
    pad, none = HEAD_PAD - QK_DIM, zeros[:, :half]
    cos_full = jnp.concatenate([ones, cos, cos, ones[:, :pad]], axis=1)
    s_lo = jnp.concatenate([zeros, -sin, none, zeros[:, :pad]], axis=1)
    s_hi = jnp.concatenate([zeros, none, sin, zeros[:, :pad]], axis=1)
    return cos_full, s_lo, s_hi


def _partials(names, grads, from_sibling, place):
    return [_chip_partial(g, o, place, "chip_partial_" + n) for n, g, o in zip(names, grads, from_sibling)]


def _totals(names, grads, partials, received, place):
    return [_chip_total(pf, r, place, g.shape[1:], "chip_total_" + n) for n, g, (pf, _), r in zip(names, grads, partials, received)]


def _kernel_layouts(full):
    w_in = full["w_in"]
    w_kr = jnp.pad(w_in[ROW_KR:ROW_XC], ((QK_NOPE, HEAD_PAD - QK_DIM), (0, 0)))
    w_uq = jnp.pad(full["w_uq"].reshape(Q_LORA, N_HEADS, QK_DIM), ((0, 0), (0, 0), (0, HEAD_PAD - QK_DIM)))
    w_uk = jnp.pad(full["w_uk"].reshape(KV_LORA, N_HEADS, QK_NOPE), ((0, 0), (0, 0), (0, HEAD_PAD - QK_NOPE)))
    return {"w_in": w_in, "w_kr": w_kr, "w_uq": w_uq.reshape(Q_LORA, N_HEADS * HEAD_PAD),
            "w_uk": w_uk.reshape(KV_LORA, N_HEADS * HEAD_PAD), "w_uv": full["w_uv"], "w_uvt": full["w_uv"].T}


def _global_layouts(g):
    w_uq = g["w_uq"].reshape(Q_LORA, N_HEADS, HEAD_PAD)[:, :, :QK_DIM].reshape(Q_LORA, N_HEADS * QK_DIM)
    w_uk = g["w_uk"].reshape(KV_LORA, N_HEADS, HEAD_PAD)[:, :, :QK_NOPE].reshape(KV_LORA, N_HEADS * QK_NOPE)
    return {"w_in": g["w_in"], "w_uq": w_uq, "w_uk": w_uk, "w_uv": g["w_uv"], "w_proj_attn": g["w_pa"], "w_proj_conv": g["w_pc"],
            "w_out": g["w_out"]}


def _dw_in(dm, dp, h2b):
    d, rows = D_MODEL, ROW_GL + 2 * D_MODEL
    wm, wp = M_COLS // 2, P_COLS // 2
    from_dm = [(0, [(ROW_GB, 0, d), (ROW_GL, d, wm - d)]), (1, [(ROW_GL + wm - d, 0, wm)])]
    from_dp = [(0, [(ROW_XC, 0, d), (ROW_GC, d, wp - d)]),
               (1, [(ROW_GC + wp - d, 0, 2 * d - wp), (ROW_QKV, 2 * d - wp, ROW_KR), (ROW_KR, 2 * d - wp + ROW_KR + QK_NOPE, QK_ROPE)])]
    out = _tn_rows(dm, h2b, None, from_dm, wm, rows, "dw_in_m")
    return _tn_rows(dp, h2b, out, from_dp, wp, rows, "dw_in_p")


def _col_blocks(a):
    r, c = a.shape
    return a.reshape(r, N_CHIPS, c // N_CHIPS).transpose(1, 0, 2)


def _from_col_blocks(a):
    n, r, c = a.shape
    return a.transpose(1, 0, 2).reshape(r, n * c)


COL_SHARDED = ("w_uq", "w_uk", "w_uv", "w_proj_attn")
TRANSPOSED = ("ffn1_w_gate", "ffn1_w_up", "ffn2_w_gate", "ffn2_w_up", "w_in")
SMALL = (("ffn1_norm", 1024), ("mix_norm", 1024), ("gate_bias", 2048), ("q_a_norm", 384), ("kv_a_norm", 256),
         ("q_head_norm", 128), ("k_head_norm", 128), ("ffn2_norm", 1024))
WEIGHT_ORDER = ("ffn1_norm", "ffn1_w_gate", "ffn1_w_up", "ffn1_w_down", "mix_norm", "w_in", "gate_bias", "q_a_norm", "w_uq",
                "kv_a_norm", "w_uk", "w_uv", "q_head_norm", "k_head_norm", "w_proj_attn", "conv_w", "w_proj_conv", "w_out",
                "ffn2_norm", "ffn2_w_gate", "ffn2_w_up", "ffn2_w_down")
MATRICES = ("ffn1_w_gate", "ffn1_w_up", "ffn1_w_down", "w_in", "w_uq", "w_uk", "w_uv", "w_proj_attn", "w_proj_conv", "w_out",
            "ffn2_w_gate", "ffn2_w_up", "ffn2_w_down")
GROUP_FFN1 = ("ffn1_w_gate", "ffn1_w_up", "ffn1_w_down")
GROUP_IN = ("w_in", "w_uq", "w_uk", "w_uv", "conv_w")
GROUP_MIX = ("w_proj_attn", "w_proj_conv", "w_out")
GROUP_FFN2 = ("ffn2_w_gate", "ffn2_w_up", "ffn2_w_down")
GROUP_MID = ("w_in", "w_uq", "w_uk", "w_uv", "w_proj_attn", "w_proj_conv", "w_out")


def _pad_lanes(a, n):
    return jnp.pad(a.reshape(1, -1), ((0, 0), (0, n - a.size)))


def kernel(x, positions, ffn1_norm, ffn1_w_gate, ffn1_w_up, ffn1_w_down, mix_norm, w_in, gate_bias, q_a_norm, w_uq, kv_a_norm, w_uk, w_uv, q_head_norm, k_head_norm, w_proj_attn, conv_w, w_proj_conv, w_out, ffn2_norm, ffn2_w_gate, ffn2_w_up, ffn2_w_down, loss_target, m_ffn1_norm, m_ffn1_w_gate, m_ffn1_w_up, m_ffn1_w_down, m_mix_norm, m_w_in, m_gate_bias, m_q_a_norm, m_w_uq, m_kv_a_norm, m_w_uk, m_w_uv, m_q_head_norm, m_k_head_norm, m_w_proj_attn, m_conv_w, m_w_proj_conv, m_w_out, m_ffn2_norm, m_ffn2_w_gate, m_ffn2_w_up, m_ffn2_w_down, v_ffn1_norm, v_ffn1_w_gate, v_ffn1_w_up, v_ffn1_w_down, v_mix_norm, v_w_in, v_gate_bias, v_q_a_norm, v_w_uq, v_kv_a_norm, v_w_uk, v_w_uv, v_q_head_norm, v_k_head_norm, v_w_proj_attn, v_conv_w, v_w_proj_conv, v_w_out, v_ffn2_norm, v_ffn2_w_gate, v_ffn2_w_up, v_ffn2_w_down):
    args = dict(locals())
    view = lambda n, a: a.T if n in TRANSPOSED else a
    weights = {n: view(n, args[n]) for n in WEIGHT_ORDER}
    moments_m = {n: view(n, args["m_" + n]) for n in WEIGHT_ORDER}
    moments_v = {n: view(n, args["v_" + n]) for n in WEIGHT_ORDER}
    nb, seq, d = x.shape
    t = nb * seq
    chip = (2 * lax.axis_index("x") + lax.axis_index("y")).astype(jnp.int32)
    place = jnp.stack([chip, lax.axis_index("c").astype(jnp.int32)])
    grads, delta, new_m, new_v = {}, {}, {}, {}

    def adamw(names, carried=None):
        results = _adamw([(weights[n], grads[n], moments_m[n], moments_v[n]) for n in names], "adamw_" + names[0], carried)
        for n, (dn, mn, vn) in zip(names, results):
            delta[n], new_m[n], new_v[n] = dn, mn, vn

    conv_rows = conv_w.shape[0]
    conv_shard = jnp.pad(conv_w, ((0, 16 - conv_rows), (0, 0)))
    assert MATRICES[:len(GROUP_FFN1)] == GROUP_FFN1
    bufs = dict(zip(MATRICES + ("conv_w",), _cast_shards([weights[n] for n in MATRICES] + [conv_shard],
                                                         [BF16] * len(MATRICES) + [F32], len(GROUP_FFN1))))
    blocks = {n: bufs[n] for n in GROUP_FFN1}
    p = {n: _pad_lanes(weights[n], size) for n, size in SMALL}
    rope = _rope_tables(positions)
    x_tok = x.reshape(t, d)

    gather_in = _gather_carried([bufs[n] for n in GROUP_IN])
    x1, gate1, up1, act1 = _ffn_fwd(x_tok, p["ffn1_norm"], blocks["ffn1_w_gate"], blocks["ffn1_w_up"], blocks["ffn1_w_down"], None,
                                    "ffn1_fwd", gather_in)
    blocks.update(zip(GROUP_IN, gather_in.results))
    w = _kernel_layouts({"w_in": blocks["w_in"].reshape(-1, d), **{n: _from_col_blocks(blocks[n]) for n in ("w_uq", "w_uk", "w_uv")}})
    p["conv_w"] = _from_col_blocks(blocks["conv_w"])[:conv_rows]

    gather_mix = _gather_carried([bufs[n] for n in GROUP_MIX + GROUP_FFN2[2:]])
    h2b, big, lat, q, k, v, vt = _inproj_fwd(x1, p["mix_norm"], w["w_in"], w["w_kr"], p["q_a_norm"], p["kv_a_norm"], p["q_head_norm"],
                                             p["k_head_norm"], w["w_uq"], w["w_uk"], w["w_uv"], w["w_uvt"], rope, gather_mix)
    blocks.update(zip(GROUP_MIX + GROUP_FFN2[2:], gather_mix.results))
    w_pa = _from_col_blocks(blocks["w_proj_attn"])
    w_pc, w_out_full = blocks["w_proj_conv"].reshape(-1, d), blocks["w_out"].reshape(-1, d)

    gather_ffn2 = _gather_carried([bufs[n] for n in GROUP_FFN2[:2]])
    o, lse = _attn_fwd(q, k, vt, seq, gather_ffn2)
    x2 = _mix_fwd(x1, o, big, p["gate_bias"], p["conv_w"], w_pa, w_pc, w_out_full, seq)
    wg2, wu2 = gather_ffn2.results
    wd2 = blocks["ffn2_w_down"]
    dx3, gate2, up2, act2, loss = _ffn_fwd(x2, p["ffn2_norm"], wg2, wu2, wd2, loss_target.reshape(t, d), "ffn2_fwd")

    dx2, dg_ffn2, hb2, dgate2, dup2, dyb2 = _ffn_bwd_x(x2, p["ffn2_norm"], dx3, gate2, up2, wg2, wu2, wd2, "ffn2_bwd")
    g_ffn2 = [_tn_matmul(dgate2, hb2, "ffn2_dw_gate"), _tn_matmul(dup2, hb2, "ffn2_dw_up"), _tn_matmul(act2, dyb2, "ffn2_dw_down")]
    swap = _swap_carried(g_ffn2)
    do, delta_o, dz, dm, dbias, dw_pa, dw_pc, dw_out = _mix_bwd(dx2, o, big, p["gate_bias"], p["conv_w"], w_pa, w_pc, w_out_full, seq,
                                                                swap)
    part = _partials(GROUP_FFN2, g_ffn2, swap.results, place)
    send = _send_carried([pb for _, pb in part])
    dq, dk, dv = _attn_bwd(q, k, v, do, lse, delta_o.reshape(N_HEADS // ATTN_BWD_HEADS, ATTN_BWD_HEADS, -1), seq, send)
    join = _join_carried(_totals(GROUP_FFN2, g_ffn2, part, send.results, place))
    dp, dw_uq, dw_uk, dw_uv, dqa, dkva, dqh, dkh, dcw = _prep_bwd(
        lat, big, dz, dq, dk, dv, p["q_a_norm"], p["kv_a_norm"], p["q_head_norm"], p["k_head_norm"], w["w_uq"], w["w_uk"],
        w["w_uv"], rope, p["conv_w"], seq, join)
    grads.update(zip(GROUP_FFN2, join.results))

    gg = _global_layouts({"w_in": _dw_in(dm, dp, h2b), "w_uq": dw_uq, "w_uk": dw_uk, "w_uv": dw_uv, "w_pa": dw_pa, "w_pc": dw_pc,
                          "w_out": dw_out})
    g_mid = [_col_blocks(gg[n]) if n in COL_SHARDED else gg[n].reshape(N_CHIPS, -1, gg[n].shape[-1]) for n in GROUP_MID]
    swap = _swap_carried(g_mid)
    dx1, dg_mix = _inproj_bwd(x1, p["mix_norm"], dx2, dm, dp, w["w_in"], w["w_kr"], swap)
    part = (_partials(GROUP_MID[:1], g_mid[:1], swap.results[:1], place)
            + _chip_partial_small(g_mid[1:], swap.results[1:], place))
    send = _send_carried([pb for _, pb in part])
    grad_x, dg_ffn1, hb1, dgate1, dup1, dyb1 = _ffn_bwd_x(x_tok, p["ffn1_norm"], dx1, gate1, up1, blocks["ffn1_w_gate"],
                                                         blocks["ffn1_w_up"], blocks["ffn1_w_down"], "ffn1_bwd", send)

    small_grads = {"ffn1_norm": dg_ffn1, "mix_norm": dg_mix, "gate_bias": dbias, "q_a_norm": dqa, "kv_a_norm": dkva,
                   "q_head_norm": dqh, "k_head_norm": dkh, "ffn2_norm": dg_ffn2}
    packed = jnp.concatenate([small_grads[n] for n, _ in SMALL] + [dcw.reshape(1, -1), loss], axis=1)
    total = _sum_devices(packed.reshape(8, -1)).reshape(1, -1)
    n_small = sum(size for _, size in SMALL)
    conv_cols = conv_w.shape[1]
    conv_total = total[:, n_small:n_small + conv_rows * d].reshape(conv_rows, d)
    grads["conv_w"] = lax.dynamic_slice_in_dim(conv_total, chip * conv_cols, conv_cols, axis=1)
    loss_total = total[0, n_small + conv_rows * d]

    join = _join_carried(_totals(GROUP_MID[:1], g_mid[:1], part[:1], send.results[:1], place)
                         + _chip_total_small([pf for pf, _ in part[1:]], send.results[1:], place, [g.shape[1:] for g in g_mid[1:]]))
    g_gate = _tn_matmul(dgate1, hb1, "ffn1_dw_gate", carried=join)
    grads.update(zip(GROUP_MID, join.results))
    swap_gate = _swap_carried([g_gate])
    g_up = _tn_matmul(dup1, hb1, "ffn1_dw_up", carried=swap_gate)
    part_gate = _partials(GROUP_FFN1[:1], [g_gate], swap_gate.results, place)
    send_gate, swap_up = _send_carried([part_gate[0][1]]), _swap_carried([g_up])
    g_down = _tn_matmul(act1, dyb1, "ffn1_dw_down", carried=_both(send_gate, swap_up))
    join_gate = _join_carried(_totals(GROUP_FFN1[:1], [g_gate], part_gate, send_gate.results, place))
    part_up = _partials(GROUP_FFN1[1:2], [g_up], swap_up.results, place)
    send_up, swap_down = _send_carried([part_up[0][1]]), _swap_carried([g_down])
    adamw(("w_in",), _both(_both(send_up, swap_down), join_gate))
    grads["ffn1_w_gate"] = join_gate.results[0]
    join_up = _join_carried(_totals(GROUP_FFN1[1:2], [g_up], part_up, send_up.results, place))
    part_down = _partials(GROUP_FFN1[2:], [g_down], swap_down.results, place)
    send_down = _send_carried([part_down[0][1]])
    adamw(GROUP_FFN2, _both(send_down, join_up))
    grads["ffn1_w_up"] = join_up.results[0]
    join_down = _join_carried(_totals(GROUP_FFN1[2:], [g_down], part_down, send_down.results, place))
    adamw(GROUP_FFN1[:2], join_down)
    grads["ffn1_w_down"] = join_down.results[0]
    adamw(GROUP_FFN1[2:])
    others = GROUP_MID[1:] + ("conv_w",)
    for n, (gn, dn, mn, vn) in zip(others, _adamw_whole([(weights[n], grads[n], moments_m[n], moments_v[n]) for n in others],
                                                        "adamw_others")):
        grads[n], delta[n], new_m[n], new_v[n] = gn, dn, mn, vn

    row = lambda a: a.reshape(1, -1)
    small = _adamw_small(total, [(row(weights[n]), row(moments_m[n]), row(moments_v[n])) for n, _ in SMALL], [size for _, size in SMALL])
    for (n, _), (gn, dn, mn, vn) in zip(SMALL, small):
        grads[n], delta[n], new_m[n], new_v[n] = gn.reshape(-1), dn.reshape(-1), mn.reshape(-1), vn.reshape(-1)

    return (loss_total, grad_x.reshape(nb, seq, d), *[view(n, src[n]) for src in (grads, delta, new_m, new_v) for n in WEIGHT_ORDER])
```

```python
import functools

import jax
import jax.numpy as jnp
from jax import lax
from jax.experimental import pallas as pl
from jax.experimental.pallas import tpu as pltpu

F32 = jnp.float32
BF16 = jnp.bfloat16

D_MODEL = 1024
N_HEADS = 8
QK_NOPE = 64
QK_ROPE = 32
QK_DIM = QK_NOPE + QK_ROPE
V_DIM = 64
HEAD_PAD = 128
Q_LORA = 384
KV_LORA = 256
ROPE_THETA = 10000.0
NORM_EPS = 1e-6
ATTN_SCALE = QK_DIM ** -0.5
MASK_VALUE = -1e30
N_CHIPS = 4
N_DEV = 8

ADAM_LR = 0.001
ADAM_B1 = 0.9
ADAM_B2 = 0.999
ADAM_EPS = 1e-08
ADAM_WD = 0.01
ADAM_STEP = 10

TOKEN_TILE = 256
WIDE_TILE = 512
ATTN_TILE = 512
TN_TILE = 2048
VMEM_LIMIT = 56 * 1024 * 1024

M_COLS = 3 * D_MODEL
P_COLS = 2 * D_MODEL + Q_LORA + KV_LORA + HEAD_PAD
BIG_COLS = 5 * D_MODEL
LAT_COLS = Q_LORA + KV_LORA + HEAD_PAD

MESH_ID = pl.DeviceIdType.MESH
ANY = pl.BlockSpec(memory_space=pl.ANY)


def _params(semantics=None):
    return pltpu.CompilerParams(dimension_semantics=semantics, vmem_limit_bytes=VMEM_LIMIT)


class _Carried:
    def __init__(self, operands, out_shapes, aliases, n_sems, start, finish):
        self.operands, self.out_shapes, self.aliases, self.n_sems = list(operands), list(out_shapes), dict(aliases), n_sems
        self.start, self.finish = start, finish
        self.results = None


def _both(a, b):
    na, nao = len(a.operands), len(a.out_shapes)

    def start(ins, outs, sems, base):
        a.start(ins[:na], outs[:nao], sems, base)
        b.start(ins[na:], outs[nao:], sems, base + a.n_sems)

    def finish(ins, outs, sems, base):
        a.finish(ins[:na], outs[:nao], sems, base)
        b.finish(ins[na:], outs[nao:], sems, base + a.n_sems)

    aliases = dict(a.aliases)
    aliases.update({na + i: nao + o for i, o in b.aliases.items()})
    both = _Carried(a.operands + b.operands, a.out_shapes + b.out_shapes, aliases, a.n_sems + b.n_sems, start, finish)
    both.parts = (a, b)
    return both


def _set_results(carried, results):
    carried.results = list(results)
    if hasattr(carried, "parts"):
        a, b = carried.parts
        _set_results(a, results[:len(a.out_shapes)])
        _set_results(b, results[len(a.out_shapes):])


def _pallas(body, name, grid, in_specs, out_specs, out_shape, args, semantics, carried=None):
    if carried is None:
        return pl.pallas_call(body, name=name, grid=grid, in_specs=in_specs, out_specs=out_specs, out_shape=out_shape,
                              compiler_params=_params(semantics))(*args)
    n_in, n_out, n_ci, n_co = len(in_specs), len(out_specs), len(carried.operands), len(carried.out_shapes)

    def wrapped(*refs):
        ins, c_ins = refs[:n_in], refs[n_in:n_in + n_ci]
        outs, c_outs = refs[n_in + n_ci:n_in + n_ci + n_out], refs[n_in + n_ci + n_out:n_in + n_ci + n_out + n_co]
        sems = refs[-1]
        first = pl.program_id(0) == 0
        last = pl.program_id(0) == grid[0] - 1
        for axis in range(1, len(grid)):
            first = jnp.logical_and(first, pl.program_id(axis) == 0)
            last = jnp.logical_and(last, pl.program_id(axis) == grid[axis] - 1)

        @pl.when(first)
        def _():
            carried.start(c_ins, c_outs, sems, 0)

        body(*ins, *outs)

        @pl.when(last)
        def _():
            carried.finish(c_ins, c_outs, sems, 0)

    results = pl.pallas_call(
        wrapped, name=name, grid=grid, in_specs=list(in_specs) + [ANY] * n_ci, out_specs=list(out_specs) + [ANY] * n_co,
        out_shape=list(out_shape) + carried.out_shapes,
        input_output_aliases={n_in + i: n_out + o for i, o in carried.aliases.items()},
        scratch_shapes=[pltpu.SemaphoreType.DMA((carried.n_sems,))], compiler_params=_params(semantics))(*args, *carried.operands)
    _set_results(carried, results[n_out:])
    return results[:n_out]


def _resident(shape):
    nd = len(shape)
    return pl.BlockSpec(shape, lambda *_: (0,) * nd, pipeline_mode=pl.Buffered(1))


def _const(shape):
    nd = len(shape)
    return pl.BlockSpec(shape, lambda *_: (0,) * nd)


def _mm(a, b):
    return jnp.dot(a, b, preferred_element_type=F32)


def _mm_nt(a, b):
    return lax.dot_general(a, b, (((1,), (1,)), ((), ())), preferred_element_type=F32)


def _mm_tn(a, b):
    return lax.dot_general(a, b, (((0,), (0,)), ((), ())), preferred_element_type=F32)


def _bf(a):
    return a.astype(BF16)


def _sigmoid(a):
    return 1.0 / (1.0 + jnp.exp(-a))


def _rms(x, gain, n=None):
    n = x.shape[-1] if n is None else n
    r = lax.rsqrt(jnp.sum(x * x, axis=-1, keepdims=True) * (1.0 / n) + NORM_EPS)
    return (x * r) * gain, r


def _rms_bwd(x, r, gain, dh, n=None):
    n = x.shape[-1] if n is None else n
    u = dh * gain
    dx = r * u - x * ((r * r * r) * (jnp.sum(u * x, axis=-1, keepdims=True) * (1.0 / n)))
    dgain = jnp.sum(dh * (x * r), axis=0, keepdims=True)
    return dx, dgain


ROPE_HALF = QK_ROPE // 2


def _rope(t, rope):
    cos, s_lo, s_hi = rope
    return t * cos + pltpu.roll(t, HEAD_PAD - ROPE_HALF, 1) * s_lo + pltpu.roll(t, ROPE_HALF, 1) * s_hi


def _rope_bwd(dt, rope):
    cos, s_lo, s_hi = rope
    return dt * cos + pltpu.roll(dt * s_lo, ROPE_HALF, 1) + pltpu.roll(dt * s_hi, HEAD_PAD - ROPE_HALF, 1)


def _shift_down(u, prev8, k):
    s = pltpu.roll(u, k, 0)
    p = pltpu.roll(prev8, k, 0)
    row = lax.broadcasted_iota(jnp.int32, prev8.shape, 0)
    top = jnp.where(row < k, p, s[:8])
    return jnp.concatenate([top, s[8:]], axis=0)


def _shift_up(d, next8, k):
    tm = d.shape[0]
    s = pltpu.roll(d, tm - k, 0)
    n = pltpu.roll(next8, 8 - k, 0)
    row = lax.broadcasted_iota(jnp.int32, next8.shape, 0)
    bot = jnp.where(row >= 8 - k, n, s[tm - 8:])
    return jnp.concatenate([s[:tm - 8], bot], axis=0)


def _ffn_fwd(x, gain, wg, wu, wd, target, name, carried=None):
    t, d = x.shape
    nb, f, _ = wg.shape
    tm = TOKEN_TILE
    with_loss = target is not None

    def body(*refs):
        if with_loss:
            x_ref, g_ref, wg_ref, wu_ref, wd_ref, t_ref, out_ref, gate_ref, up_ref, act_ref, loss_ref = refs
        else:
            x_ref, g_ref, wg_ref, wu_ref, wd_ref, out_ref, gate_ref, up_ref, act_ref = refs
        xv = x_ref[...]
        h, _ = _rms(xv, g_ref[...])
        hb = _bf(h)
        y = jnp.zeros((tm, d), F32)
        nxt = (_mm_nt(hb, wg_ref[0]), _mm_nt(hb, wu_ref[0]))
        for j in range(nb):
            gate, up = nxt
            if j + 1 < nb:
                nxt = (_mm_nt(hb, wg_ref[j + 1]), _mm_nt(hb, wu_ref[j + 1]))
            act = _bf((gate * _sigmoid(gate)) * up)
            y = y + _mm(act, wd_ref[j])
            gate_ref[j] = _bf(gate)
            up_ref[j] = _bf(up)
            act_ref[j] = act
        out = xv + 0.5 * y
        if with_loss:
            err = out - t_ref[...]
            out_ref[...] = err * (1.0 / d)

            @pl.when(pl.program_id(0) == 0)
            def _():
                loss_ref[...] = jnp.zeros_like(loss_ref)

            part = jnp.sum(jnp.sum(err * err, axis=1, keepdims=True), axis=0, keepdims=True)
            loss_ref[...] += jnp.broadcast_to(part * (0.5 / d), loss_ref.shape)
        else:
            out_ref[...] = out

    tok = pl.BlockSpec((tm, d), lambda i: (i, 0))
    blk = pl.BlockSpec((nb, tm, f), lambda i: (0, i, 0))
    in_specs = [tok, _const((1, d)), _resident(wg.shape), _resident(wu.shape), _resident(wd.shape)]
    args = [x, gain, wg, wu, wd]
    out_shape = [jax.ShapeDtypeStruct((t, d), F32)] + [jax.ShapeDtypeStruct((nb, t, f), BF16)] * 3
    out_specs = [tok, blk, blk, blk]
    if with_loss:
        in_specs.append(tok)
        args.append(target)
        out_shape.append(jax.ShapeDtypeStruct((1, 128), F32))
        out_specs.append(_const((1, 128)))
    return _pallas(body, name, (t // tm,), in_specs, out_specs, out_shape, args, ("arbitrary",), carried)


def _ffn_bwd_x(x, gain, dout, gate, up, wg, wu, wd, name, carried=None):
    t, d = x.shape
    nb, f, _ = wg.shape
    tm = TOKEN_TILE

    def body(x_ref, g_ref, dout_ref, gate_ref, up_ref, wg_ref, wu_ref, wd_ref,
             dx_ref, dgain_ref, hb_ref, dgate_ref, dup_ref, dyb_ref):
        xv = x_ref[...]
        gain_v = g_ref[...]
        h, r = _rms(xv, gain_v)
        hb_ref[...] = _bf(h)
        dout_v = dout_ref[...]
        dyb = _bf(0.5 * dout_v)
        dyb_ref[...] = dyb
        dh = jnp.zeros((tm, d), F32)
        nxt = _mm_nt(dyb, wd_ref[0])
        for j in range(nb):
            dact = nxt
            if j + 1 < nb:
                nxt = _mm_nt(dyb, wd_ref[j + 1])
            gt = gate_ref[j].astype(F32)
            uv = up_ref[j].astype(F32)
            s = _sigmoid(gt)
            dup = _bf(dact * (gt * s))
            dgate = _bf((dact * uv) * (s * (1.0 + gt * (1.0 - s))))
            dh = dh + _mm(dgate, wg_ref[j]) + _mm(dup, wu_ref[j])
            dgate_ref[j] = dgate
            dup_ref[j] = dup
        dxn, dgain = _rms_bwd(xv, r, gain_v, dh)
        dx_ref[...] = dout_v + dxn

        @pl.when(pl.program_id(0) == 0)
        def _():
            dgain_ref[...] = jnp.zeros_like(dgain_ref)

        dgain_ref[...] += dgain

    tok = pl.BlockSpec((tm, d), lambda i: (i, 0))
    blk = pl.BlockSpec((nb, tm, f), lambda i: (0, i, 0))
    return _pallas(
        body, name, (t // tm,),
        [tok, _const((1, d)), tok, blk, blk, _resident(wg.shape), _resident(wu.shape), _resident(wd.shape)],
        [tok, _const((1, d)), tok, blk, blk, tok],
        [jax.ShapeDtypeStruct((t, d), F32), jax.ShapeDtypeStruct((1, d), F32), jax.ShapeDtypeStruct((t, d), BF16),
         jax.ShapeDtypeStruct((nb, t, f), BF16), jax.ShapeDtypeStruct((nb, t, f), BF16), jax.ShapeDtypeStruct((t, d), BF16)],
        (x, gain, dout, gate, up, wg, wu, wd), ("arbitrary",), carried)


def _tn_matmul(a, b, name, carried=None):
    t = a.shape[-2]
    k = a.shape[-1]
    n = b.shape[-1]
    tt = min(2 * TN_TILE, t)
    nt = t // tt

    def body(a_ref, b_ref, o_ref):
        if nt == 1:
            o_ref[...] = _mm_tn(a_ref[...], b_ref[...])
            return

        @pl.when(pl.program_id(1) == 0)
        def _():
            o_ref[...] = jnp.zeros_like(o_ref)

        o_ref[...] += _mm_tn(a_ref[...], b_ref[...])

    g = a.shape[0] if a.ndim == 3 else b.shape[0]
    a_spec = (pl.BlockSpec((None, tt, k), lambda gi, ti: (gi, ti, 0)) if a.ndim == 3
              else pl.BlockSpec((tt, k), lambda gi, ti: (ti, 0)))
    b_spec = (pl.BlockSpec((None, tt, n), lambda gi, ti: (gi, ti, 0)) if b.ndim == 3
              else pl.BlockSpec((tt, n), lambda gi, ti: (ti, 0)))
    o_spec = pl.BlockSpec((None, k, n), lambda gi, ti: (gi, 0, 0))
    out_shape = jax.ShapeDtypeStruct((g, k, n), F32)
    return _pallas(body, name, (g, nt), [a_spec, b_spec], [o_spec], [out_shape], (a, b), ("arbitrary", "arbitrary"), carried)[0]


def _tn_rows(a, b, out, chunks, width, rows_out, name):
    t, n = b.shape
    tt = min(2 * TN_TILE, t)
    nt = t // tt

    def body(blocks_ref, a_ref, b_ref, *rest):
        out_ref, acc, sem = rest[-3:]
        g, ti = pl.program_id(0), pl.program_id(1)

        @pl.when(ti == 0)
        def _():
            acc[...] = jnp.zeros_like(acc)

        acc[...] += _mm_tn(a_ref[...], b_ref[...])
        for gi, (_, ranges) in enumerate(chunks):
            @pl.when(jnp.logical_and(g == gi, ti == nt - 1))
            def _(ranges=ranges):
                for row, first, count in ranges:
                    cp = pltpu.make_async_copy(acc.at[first:first + count], out_ref.at[row:row + count], sem)
                    cp.start()
                    cp.wait()

    blocks = jnp.asarray([c[0] for c in chunks], jnp.int32)
    grid_spec = pltpu.PrefetchScalarGridSpec(
        num_scalar_prefetch=1, grid=(len(chunks), nt),
        in_specs=[pl.BlockSpec((tt, width), lambda g, ti, blocks_ref: (ti, blocks_ref[g])),
                  pl.BlockSpec((tt, n), lambda g, ti, blocks_ref: (ti, 0))] + ([ANY] if out is not None else []),
        out_specs=ANY, scratch_shapes=[pltpu.VMEM((width, n), F32), pltpu.SemaphoreType.DMA])
    args = (blocks, a, b) + ((out,) if out is not None else ())
    return pl.pallas_call(body, name=name, grid_spec=grid_spec, out_shape=jax.ShapeDtypeStruct((rows_out, n), F32),
                          input_output_aliases={3: 0} if out is not None else {},
                          compiler_params=_params(("arbitrary", "arbitrary")))(*args)


ROW_QKV, ROW_KR, ROW_XC = 0, Q_LORA + KV_LORA, Q_LORA + KV_LORA + QK_ROPE
ROW_GB, ROW_GC, ROW_GL = ROW_XC + D_MODEL, ROW_XC + 2 * D_MODEL, ROW_XC + 3 * D_MODEL
BIG_FROM_ROWS = ((0, ROW_GB, D_MODEL), (D_MODEL, ROW_GL, 2 * D_MODEL), (3 * D_MODEL, ROW_XC, D_MODEL), (4 * D_MODEL, ROW_GC, D_MODEL))


def _inproj_fwd(x1, gain, w_in, w_kr, qa_gain, kva_gain, qh_gain, kh_gain, w_uq, w_uk, w_uv, w_uvt, rope, carried=None):
    t, d = x1.shape
    tm = TOKEN_TILE
    chunk = 512
    chunks = []
    for col, row, size in BIG_FROM_ROWS:
        chunks += [(col + o, row + o, chunk) for o in range(0, size, chunk)]
    of_head = [[c for k, c in enumerate(chunks) if k * N_HEADS // len(chunks) == hd] for hd in range(N_HEADS)]

    def body(x_ref, g_ref, win_ref, wkr_ref, qa_ref, kva_ref, qh_ref, kh_ref, wuq_ref, wuk_ref, wuv_ref, wuvt_ref, cos_ref, slo_ref,
             shi_ref, hb_ref, big_ref, lat_ref, q_ref, k_ref, v_ref, vt_ref):
        h, _ = _rms(x_ref[...], g_ref[...])
        hb = _bf(h)
        hb_ref[...] = hb
        k_rope = _mm_nt(hb, wkr_ref[...])
        lat = jnp.concatenate([_mm_nt(hb, win_ref[ROW_QKV:ROW_KR, :]), k_rope], axis=1)
        lat_ref[...] = lat
        cq, _ = _rms(lat[:, :Q_LORA], qa_ref[...])
        ckv, _ = _rms(lat[:, Q_LORA:Q_LORA + KV_LORA], kva_ref[...])
        cqb = _bf(cq)
        ckvb = _bf(ckv)
        rope_v = (cos_ref[...], slo_ref[...], shi_ref[...])
        q_all = _mm(cqb, wuq_ref[...])
        k_all = _mm(ckvb, wuk_ref[...])
        v_ref[...] = _bf(_mm(ckvb, wuv_ref[...]))
        vt_all = _mm_nt(wuvt_ref[...], ckvb)
        for hd in range(N_HEADS):
            for col, row, size in of_head[hd]:
                big_ref[:, col:col + size] = _mm_nt(hb, win_ref[row:row + size, :])
            lanes = slice(hd * HEAD_PAD, (hd + 1) * HEAD_PAD)
            qn, _ = _rms(q_all[:, lanes], qh_ref[...], QK_DIM)
            q_ref[hd] = _bf(_rope(qn, rope_v))
            kn, _ = _rms(k_all[:, lanes] + k_rope, kh_ref[...], QK_DIM)
            k_ref[hd] = _bf(_rope(kn, rope_v))
            vt_ref[hd] = _bf(vt_all[hd * V_DIM:(hd + 1) * V_DIM])

    tok = lambda c: pl.BlockSpec((tm, c), lambda i: (i, 0))
    head = lambda c: pl.BlockSpec((N_HEADS, tm, c), lambda i: (0, i, 0))
    return _pallas(
        body, "inproj_fwd", (t // tm,),
        [tok(d), _const((1, d)), _resident(w_in.shape), _resident(w_kr.shape), _const((1, Q_LORA)), _const((1, KV_LORA)),
         _const((1, HEAD_PAD)), _const((1, HEAD_PAD)), _resident(w_uq.shape), _resident(w_uk.shape),
         _resident(w_uv.shape), _resident(w_uvt.shape), tok(HEAD_PAD), tok(HEAD_PAD), tok(HEAD_PAD)],
        [tok(d), tok(BIG_COLS), tok(LAT_COLS), head(HEAD_PAD), head(HEAD_PAD), tok(N_HEADS * V_DIM),
         pl.BlockSpec((N_HEADS, V_DIM, tm), lambda i: (0, 0, i))],
        [jax.ShapeDtypeStruct((t, d), BF16), jax.ShapeDtypeStruct((t, BIG_COLS), F32),
         jax.ShapeDtypeStruct((t, LAT_COLS), F32), jax.ShapeDtypeStruct((N_HEADS, t, HEAD_PAD), BF16),
         jax.ShapeDtypeStruct((N_HEADS, t, HEAD_PAD), BF16), jax.ShapeDtypeStruct((t, N_HEADS * V_DIM), BF16),
         jax.ShapeDtypeStruct((N_HEADS, V_DIM, t), BF16)],
        (x1, gain, w_in, w_kr, qa_gain, kva_gain, qh_gain, kh_gain, w_uq, w_uk, w_uv, w_uvt, *rope), ("arbitrary",), carried)


EXP2_SCALE = ATTN_SCALE * 1.4426950408889634


def _diagonal_keep(tk, tq):
    return lax.broadcasted_iota(jnp.int32, (tk, tq), 0) <= lax.broadcasted_iota(jnp.int32, (tk, tq), 1)


def _attn_fwd(q, k, vt, seq, carried=None):
    _, t, _ = q.shape
    nseq = t // seq
    tq = tk = ATTN_TILE
    nq = seq // tq

    def body(q_ref, k_ref, vt_ref, o_ref, lse_ref):
        i = pl.program_id(1)
        qs = [q_ref[h] for h in range(N_HEADS)]
        keep = _diagonal_keep(tk, tq)

        def scores(h, k0):
            return _mm_nt(k_ref[h, pl.ds(k0, tk), :], qs[h])

        def update(h, st, state, k0, diagonal):
            m, l, acc = state
            if diagonal:
                st = jnp.where(keep, st, MASK_VALUE)
            m_new = jnp.maximum(m, jnp.max(st, axis=0, keepdims=True))
            pt = jnp.exp2((st - m_new) * EXP2_SCALE)
            alpha = jnp.exp2((m - m_new) * EXP2_SCALE)
            l_new = alpha * l + jnp.sum(pt, axis=0, keepdims=True)
            return m_new, l_new, alpha * acc + _mm(vt_ref[h, :, pl.ds(k0, tk)], _bf(pt))

        def tiles(states, k0, diagonal):
            st, new = scores(0, k0), []
            for h in range(N_HEADS):
                st_next = scores(h + 1, k0) if h + 1 < N_HEADS else None
                new.append(update(h, st, states[h], k0, diagonal))
                st = st_next
            return tuple(new)

        init = tuple((jnp.full((1, tq), MASK_VALUE, F32), jnp.zeros((1, tq), F32), jnp.zeros((V_DIM, tq), F32))
                     for _ in range(N_HEADS))
        states = lax.fori_loop(0, i, lambda j, s: tiles(s, pl.multiple_of(j * tk, tk), False), init)
        states = tiles(states, pl.multiple_of(i * tk, tk), True)
        outs = []
        for h in range(N_HEADS):
            m, l, acc = states[h]
            outs.append((acc / l).T)
            lse_ref[h] = m * EXP2_SCALE + jnp.log2(l)
        o_ref[...] = _bf(jnp.concatenate(outs, axis=-1))

    return _pallas(
        body, "attn_fwd", (nseq, nq),
        [pl.BlockSpec((N_HEADS, tq, HEAD_PAD), lambda b, i: (0, b * nq + i, 0)),
         pl.BlockSpec((N_HEADS, seq, HEAD_PAD), lambda b, i: (0, b, 0)),
         pl.BlockSpec((N_HEADS, V_DIM, seq), lambda b, i: (0, 0, b))],
        [pl.BlockSpec((tq, N_HEADS * V_DIM), lambda b, i: (b * nq + i, 0)),
         pl.BlockSpec((N_HEADS, 1, tq), lambda b, i: (0, 0, b * nq + i))],
        [jax.ShapeDtypeStruct((t, N_HEADS * V_DIM), BF16), jax.ShapeDtypeStruct((N_HEADS, 1, t), F32)],
        (q, k, vt), ("arbitrary", "arbitrary"), carried)


ATTN_BWD_HEADS = 4


def _attn_bwd(q, k, v, do, lse, delta, seq, carried=None):
    _, t, _ = q.shape
    nseq = t // seq
    tq = tk = ATTN_TILE
    n = seq // tq
    hb = ATTN_BWD_HEADS

    def body(q_ref, k_ref, v_ref, do_ref, lse_ref, delta_ref, dq_ref, dk_ref, dv_ref):
        dq_ref[...] = jnp.zeros_like(dq_ref)
        dk_ref[...] = jnp.zeros_like(dk_ref)
        dv_ref[...] = jnp.zeros_like(dv_ref)
        keep = _diagonal_keep(tk, tq)

        def tile(h, k0, q0, diagonal):
            kj = k_ref[h, pl.ds(k0, tk), :]
            qi = q_ref[h, pl.ds(q0, tq), :]
            doi = _bf(do_ref[pl.ds(q0, tq), h * V_DIM:(h + 1) * V_DIM])
            st = _mm_nt(kj, qi)
            if diagonal:
                st = jnp.where(keep, st, MASK_VALUE)
            pt = jnp.exp2(st * EXP2_SCALE - lse_ref[h, :, pl.ds(q0, tq)])
            dv_ref[pl.ds(k0, tk), h * V_DIM:(h + 1) * V_DIM] += _mm(_bf(pt), doi)
            dpt = _mm_nt(v_ref[pl.ds(k0, tk), h * V_DIM:(h + 1) * V_DIM], doi)
            dst = _bf((pt * (dpt - delta_ref[pl.ds(h, 1), pl.ds(q0, tq)])) * ATTN_SCALE)
            dk_ref[h, pl.ds(k0, tk), :] += _mm(dst, qi)
            dq_ref[h, pl.ds(q0, tq), :] += _mm_tn(dst, kj)

        def kv_step(j, _):
            k0 = pl.multiple_of(j * tk, tk)
            for h in range(hb):
                tile(h, k0, k0, True)

            def q_step(i, _):
                q0 = pl.multiple_of(i * tq, tq)
                for h in range(hb):
                    tile(h, k0, q0, False)
                return 0

            lax.fori_loop(j + 1, n, q_step, 0)
            return 0

        lax.fori_loop(0, n, kv_step, 0)

    hspec = lambda c: pl.BlockSpec((hb, seq, c), lambda b, g: (g, b, 0))
    cols = pl.BlockSpec((seq, hb * V_DIM), lambda b, g: (b, g))
    return _pallas(
        body, "attn_bwd", (nseq, N_HEADS // hb),
        [hspec(HEAD_PAD), hspec(HEAD_PAD), cols, cols,
         pl.BlockSpec((hb, 1, seq), lambda b, g: (g, 0, b)), pl.BlockSpec((None, hb, seq), lambda b, g: (g, 0, b))],
        [hspec(HEAD_PAD), hspec(HEAD_PAD), cols],
        [jax.ShapeDtypeStruct((N_HEADS, t, HEAD_PAD), F32), jax.ShapeDtypeStruct((N_HEADS, t, HEAD_PAD), F32),
         jax.ShapeDtypeStruct((t, N_HEADS * V_DIM), F32)],
        (q, k, v, do, lse, delta), ("arbitrary", "arbitrary"), carried)


def _merged_mixers(o_ref, gb_ref, gla_ref, glb_ref, xc_ref, gc_ref, xcp_ref, gcp_ref, bias_ref, cw_ref, wpa_ref, wpc_ref,
                   first_of_seq):
    y_a = _mm(o_ref[...], wpa_ref[...])
    gb = gb_ref[...]
    u = gc_ref[...] * xc_ref[...]
    u_prev = jnp.where(first_of_seq, 0.0, gcp_ref[...] * xcp_ref[...])
    cw = cw_ref[...]
    z = cw[2:3] * u + cw[1:2] * _shift_down(u, u_prev, 1) + cw[0:1] * _shift_down(u, u_prev, 2)
    gbz = _bf(gb * z)
    y_b = _mm(gbz, wpc_ref[...])
    bias = bias_ref[...]
    gate_a = _sigmoid(gla_ref[...] + bias[:, :D_MODEL])
    gate_b = _sigmoid(glb_ref[...] + bias[:, D_MODEL:])
    return _bf(gate_a * y_a + gate_b * y_b)


def _mixer_specs(tm, seq):
    d = D_MODEL
    tok = pl.BlockSpec((tm, d), lambda i: (i, 0))
    col = lambda c: pl.BlockSpec((tm, d), lambda i: (i, c))
    prev = lambda c: pl.BlockSpec((8, d), lambda i: (jnp.maximum(i * (tm // 8) - 1, 0), c))
    o_spec = pl.BlockSpec((tm, N_HEADS * V_DIM), lambda i: (i, 0))
    fwd_specs = [o_spec, col(0), col(1), col(2), col(3), col(4), prev(3), prev(4), _const((1, 2 * d)), _const((3, d)),
                 _resident((N_HEADS * V_DIM, d)), _resident((d, d)), _resident((d, d))]
    return tok, fwd_specs


def _mix_fwd(x1, o, big, gate_bias, conv_w, w_pa, w_pc, w_out, seq, carried=None):
    t, d = x1.shape
    tm = WIDE_TILE
    tiles_per_seq = seq // tm

    def body(x_ref, o_ref, gb_ref, gla_ref, glb_ref, xc_ref, gc_ref, xcp_ref, gcp_ref, bias_ref, cw_ref, wpa_ref, wpc_ref,
             wout_ref, x2_ref):
        first = pl.program_id(0) % tiles_per_seq == 0
        merged = _merged_mixers(o_ref, gb_ref, gla_ref, glb_ref, xc_ref, gc_ref, xcp_ref, gcp_ref, bias_ref, cw_ref, wpa_ref,
                                wpc_ref, first)
        x2_ref[...] = x_ref[...] + _mm(merged, wout_ref[...])

    tok, fwd_specs = _mixer_specs(tm, seq)
    return _pallas(body, "mix_fwd", (t // tm,), [tok] + fwd_specs, [tok], [jax.ShapeDtypeStruct((t, d), F32)],
                   (x1, o, big, big, big, big, big, big, big, gate_bias, conv_w, w_pa, w_pc, w_out), ("arbitrary",), carried)[0]


def _mix_bwd(dx2, o, big, gate_bias, conv_w, w_pa, w_pc, w_out, seq, carried=None):
    t, d = dx2.shape
    tm = TOKEN_TILE
    tiles_per_seq = seq // tm
    hv = N_HEADS * V_DIM

    def body(dx_ref, o_ref, gb_ref, gla_ref, glb_ref, xc_ref, gc_ref, xcp_ref, gcp_ref, bias_ref, cw_ref, wpa_ref, wpc_ref,
             wout_ref, do_ref, delta_ref, dz_ref, dm_ref, dbias_ref, dwpa_ref, dwpc_ref, dwout_ref):
        @pl.when(pl.program_id(0) == 0)
        def _():
            dbias_ref[...] = jnp.zeros_like(dbias_ref)
            dwpa_ref[...] = jnp.zeros_like(dwpa_ref)
            dwpc_ref[...] = jnp.zeros_like(dwpc_ref)
            dwout_ref[...] = jnp.zeros_like(dwout_ref)

        first = pl.program_id(0) % tiles_per_seq == 0
        dxb = _bf(dx_ref[...])
        dmerged = _mm_nt(dxb, wout_ref[...])
        y_a = _mm(o_ref[...], wpa_ref[...])
        bias = bias_ref[...]
        gate_a = _sigmoid(gla_ref[...] + bias[:, :d])
        gate_b = _sigmoid(glb_ref[...] + bias[:, d:])
        dya = _bf(dmerged * gate_a)
        dyb = _bf(dmerged * gate_b)
        do_v = _mm_nt(dya, wpa_ref[...])
        dgz = _mm_nt(dyb, wpc_ref[...])
        dwpa_ref[...] += _mm_tn(o_ref[...], dya)
        gb = gb_ref[...]
        u = gc_ref[...] * xc_ref[...]
        u_prev = jnp.where(first, 0.0, gcp_ref[...] * xcp_ref[...])
        cw = cw_ref[...]
        z = cw[2:3] * u + cw[1:2] * _shift_down(u, u_prev, 1) + cw[0:1] * _shift_down(u, u_prev, 2)
        gbz = _bf(gb * z)
        y_b = _mm(gbz, wpc_ref[...])
        dwpc_ref[...] += _mm_tn(gbz, dyb)
        do_ref[...] = do_v
        head = lax.broadcasted_iota(jnp.int32, (N_HEADS, hv), 0) * V_DIM
        col = lax.broadcasted_iota(jnp.int32, (N_HEADS, hv), 1)
        in_head = ((col >= head) & (col < head + V_DIM)).astype(F32)
        delta_ref[...] = lax.dot_general(in_head, do_v * o_ref[...].astype(F32), (((1,), (1,)), ((), ())),
                                         precision=lax.Precision.HIGHEST, preferred_element_type=F32)
        dz_ref[...] = dgz * gb
        dm_ref[:, :d] = _bf(dgz * z)
        merged = _bf(gate_a * y_a + gate_b * y_b)
        dwout_ref[...] += _mm_tn(merged, dxb)
        dla = (dmerged * y_a) * (gate_a * (1.0 - gate_a))
        dlb = (dmerged * y_b) * (gate_b * (1.0 - gate_b))
        dbias_ref[:, :d] += jnp.sum(dla, axis=0, keepdims=True)
        dbias_ref[:, d:] += jnp.sum(dlb, axis=0, keepdims=True)
        dm_ref[:, d:2 * d] = _bf(dla)
        dm_ref[:, 2 * d:] = _bf(dlb)

    tok, fwd_specs = _mixer_specs(tm, seq)
    return _pallas(
        body, "mix_bwd", (t // tm,), [tok] + fwd_specs,
        [pl.BlockSpec((tm, hv), lambda i: (i, 0)), pl.BlockSpec((N_HEADS, tm), lambda i: (0, i)), tok,
         pl.BlockSpec((tm, M_COLS), lambda i: (i, 0)), _const((1, 2 * d)), _const((hv, d)), _const((d, d)), _const((d, d))],
        [jax.ShapeDtypeStruct((t, hv), F32), jax.ShapeDtypeStruct((N_HEADS, t), F32), jax.ShapeDtypeStruct((t, d), F32),
         jax.ShapeDtypeStruct((t, M_COLS), BF16), jax.ShapeDtypeStruct((1, 2 * d), F32), jax.ShapeDtypeStruct((hv, d), F32),
         jax.ShapeDtypeStruct((d, d), F32), jax.ShapeDtypeStruct((d, d), F32)],
        (dx2, o, big, big, big, big, big, big, big, gate_bias, conv_w, w_pa, w_pc, w_out), ("arbitrary",), carried)


def _prep_bwd(lat, big, dz, dq, dk, dv, qa_gain, kva_gain, qh_gain, kh_gain, w_uq, w_uk, w_uv, rope, conv_w, seq, carried=None):
    t = lat.shape[0]
    d = D_MODEL
    tm = WIDE_TILE
    tiles_per_seq = seq // tm
    last_blk = t // 8 - 1

    def body(lat_ref, xc_ref, gc_ref, dz_ref, dzn_ref, dq_ref, dk_ref, dv_ref, qa_ref, kva_ref, qh_ref, kh_ref, wuq_ref, wuk_ref,
             wuv_ref, cos_ref, slo_ref, shi_ref, cw_ref,
             dp_ref, dwuq_ref, dwuk_ref, dwuv_ref, dqa_ref, dkva_ref, dqh_ref, dkh_ref, dcw_ref):
        pid = pl.program_id(0)

        @pl.when(pid == 0)
        def _():
            for r in (dwuq_ref, dwuk_ref, dwuv_ref, dqa_ref, dkva_ref, dqh_ref, dkh_ref, dcw_ref):
                r[...] = jnp.zeros_like(r)

        lat_v = lat_ref[...]
        q_lat = lat_v[:, :Q_LORA]
        kv_lat = lat_v[:, Q_LORA:Q_LORA + KV_LORA]
        k_rope = lat_v[:, Q_LORA + KV_LORA:]
        qa_gain_v = qa_ref[...]
        kva_gain_v = kva_ref[...]
        qh_gain_v = qh_ref[...]
        kh_gain_v = kh_ref[...]
        cq, rq = _rms(q_lat, qa_gain_v)
        ckv, rkv = _rms(kv_lat, kva_gain_v)
        cqb = _bf(cq)
        ckvb = _bf(ckv)
        rope_v = (cos_ref[...], slo_ref[...], shi_ref[...])
        lane = lax.broadcasted_iota(jnp.int32, (tm, HEAD_PAD), 1)
        rope_lanes = (lane >= QK_NOPE) & (lane < QK_DIM)
        dk_rope = jnp.zeros((tm, HEAD_PAD), F32)
        dqh_gain = jnp.zeros((1, HEAD_PAD), F32)
        dkh_gain = jnp.zeros((1, HEAD_PAD), F32)
        q_all = _mm(cqb, wuq_ref[...])
        k_all = _mm(ckvb, wuk_ref[...])
        dvb = _bf(dv_ref[...])
        dckv = _mm_nt(dvb, wuv_ref[...])
        dwuv_ref[...] += _mm_tn(ckvb, dvb)

        last = pid % tiles_per_seq == tiles_per_seq - 1
        dzv = dz_ref[...]
        dz_next = jnp.where(last, 0.0, dzn_ref[...])
        dz1 = _shift_up(dzv, dz_next, 1)
        dz2 = _shift_up(dzv, dz_next, 2)
        cw = cw_ref[...]
        xc = xc_ref[...]
        gc = gc_ref[...]
        u = gc * xc
        du = cw[2:3] * dzv + cw[1:2] * dz1 + cw[0:1] * dz2
        dp_ref[:, :d] = _bf(du * gc)
        dp_ref[:, d:2 * d] = _bf(du * xc)
        dcw_ref[0:1, :] += jnp.sum(dz2 * u, axis=0, keepdims=True)
        dcw_ref[1:2, :] += jnp.sum(dz1 * u, axis=0, keepdims=True)
        dcw_ref[2:3, :] += jnp.sum(dzv * u, axis=0, keepdims=True)

        dcq = jnp.zeros((tm, Q_LORA), F32)
        half = N_HEADS // 2
        for part in range(2):
            dq_heads, dk_heads = [], []
            for hd in range(part * half, (part + 1) * half):
                lanes = slice(hd * HEAD_PAD, (hd + 1) * HEAD_PAD)
                q_pre = q_all[:, lanes]
                _, rr = _rms(q_pre, qh_gain_v, QK_DIM)
                dq_pre, dg = _rms_bwd(q_pre, rr, qh_gain_v, _rope_bwd(dq_ref[hd], rope_v), QK_DIM)
                dqh_gain = dqh_gain + dg
                dq_heads.append(_bf(dq_pre))

                k_pre = k_all[:, lanes] + k_rope
                _, rr = _rms(k_pre, kh_gain_v, QK_DIM)
                dk_pre, dg = _rms_bwd(k_pre, rr, kh_gain_v, _rope_bwd(dk_ref[hd], rope_v), QK_DIM)
                dkh_gain = dkh_gain + dg
                dk_rope = dk_rope + jnp.where(rope_lanes, dk_pre, 0.0)
                dk_heads.append(_bf(dk_pre))
            dq_part = jnp.concatenate(dq_heads, axis=1)
            dk_part = jnp.concatenate(dk_heads, axis=1)
            cols = slice(part * half * HEAD_PAD, (part + 1) * half * HEAD_PAD)
            dcq = dcq + _mm_nt(dq_part, wuq_ref[:, cols])
            dckv = dckv + _mm_nt(dk_part, wuk_ref[:, cols])
            dwuq_ref[:, cols] += _mm_tn(cqb, dq_part)
            dwuk_ref[:, cols] += _mm_tn(ckvb, dk_part)
        dqh_ref[...] += dqh_gain
        dkh_ref[...] += dkh_gain
        dq_lat, dg = _rms_bwd(q_lat, rq, qa_gain_v, dcq)
        dqa_ref[...] += dg
        dkv_lat, dg = _rms_bwd(kv_lat, rkv, kva_gain_v, dckv)
        dkva_ref[...] += dg
        dp_ref[:, 2 * d:2 * d + Q_LORA] = _bf(dq_lat)
        dp_ref[:, 2 * d + Q_LORA:2 * d + Q_LORA + KV_LORA] = _bf(dkv_lat)
        dp_ref[:, 2 * d + Q_LORA + KV_LORA:] = _bf(dk_rope)

    tok = lambda c: pl.BlockSpec((tm, c), lambda i: (i, 0))
    col = lambda c: pl.BlockSpec((tm, d), lambda i: (i, c))
    head = lambda c: pl.BlockSpec((N_HEADS, tm, c), lambda i: (0, i, 0))
    nxt = pl.BlockSpec((8, d), lambda i: (jnp.minimum((i + 1) * (tm // 8), last_blk), 0))
    return _pallas(
        body, "prep_bwd", (t // tm,),
        [tok(LAT_COLS), col(3), col(4), tok(d), nxt, head(HEAD_PAD), head(HEAD_PAD), tok(N_HEADS * V_DIM),
         _const((1, Q_LORA)), _const((1, KV_LORA)), _const((1, HEAD_PAD)), _const((1, HEAD_PAD)),
         _resident(w_uq.shape), _resident(w_uk.shape), _resident(w_uv.shape), tok(HEAD_PAD), tok(HEAD_PAD), tok(HEAD_PAD),
         _const((3, d))],
        [tok(P_COLS), _const(w_uq.shape), _const(w_uk.shape), _const(w_uv.shape), _const((1, Q_LORA)),
         _const((1, KV_LORA)), _const((1, HEAD_PAD)), _const((1, HEAD_PAD)), _const((3, d))],
        [jax.ShapeDtypeStruct((t, P_COLS), BF16), jax.ShapeDtypeStruct(w_uq.shape, F32),
         jax.ShapeDtypeStruct(w_uk.shape, F32), jax.ShapeDtypeStruct(w_uv.shape, F32),
         jax.ShapeDtypeStruct((1, Q_LORA), F32), jax.ShapeDtypeStruct((1, KV_LORA), F32),
         jax.ShapeDtypeStruct((1, HEAD_PAD), F32), jax.ShapeDtypeStruct((1, HEAD_PAD), F32), jax.ShapeDtypeStruct((3, d), F32)],
        (lat, big, big, dz, dz, dq, dk, dv, qa_gain, kva_gain, qh_gain, kh_gain, w_uq, w_uk, w_uv, *rope, conv_w),
        ("arbitrary",), carried)


def _inproj_bwd(x1, gain, dx2, dm, dp, w_in, w_kr, carried=None):
    t, d = x1.shape
    tm = TOKEN_TILE

    def body(x_ref, g_ref, dx2_ref, dm_ref, dp_ref, win_ref, wkr_ref, dx1_ref, dgain_ref):
        xv = x_ref[...]
        gain_v = g_ref[...]
        _, r = _rms(xv, gain_v)
        dh = (_mm(dm_ref[:, :d], win_ref[ROW_GB:ROW_GC, :]) + _mm(dm_ref[:, d:], win_ref[ROW_GL:, :])
              + _mm(dp_ref[:, :d], win_ref[ROW_XC:ROW_GB, :]) + _mm(dp_ref[:, d:2 * d], win_ref[ROW_GC:ROW_GL, :])
              + _mm(dp_ref[:, 2 * d:2 * d + ROW_KR], win_ref[ROW_QKV:ROW_KR, :]) + _mm(dp_ref[:, 2 * d + ROW_KR:], wkr_ref[...]))
        dxn, dgain = _rms_bwd(xv, r, gain_v, dh)
        dx1_ref[...] = dx2_ref[...] + dxn

        @pl.when(pl.program_id(0) == 0)
        def _():
            dgain_ref[...] = jnp.zeros_like(dgain_ref)

        dgain_ref[...] += dgain

    tok = lambda c: pl.BlockSpec((tm, c), lambda i: (i, 0))
    return _pallas(
        body, "inproj_bwd", (t // tm,),
        [tok(d), _const((1, d)), tok(d), tok(M_COLS), tok(P_COLS), _resident(w_in.shape), _resident(w_kr.shape)],
        [tok(d), _const((1, d))], [jax.ShapeDtypeStruct((t, d), F32), jax.ShapeDtypeStruct((1, d), F32)],
        (x1, gain, dx2, dm, dp, w_in, w_kr), ("arbitrary",), carried)


def _adamw(quads, name, carried=None):
    k = len(quads)
    rows, cols = quads[0][0].shape
    tr, tc = rows, cols
    for cand in (512, 352, 256, 192, 128, 64):
        if rows % cand == 0 and rows > cand:
            tr = cand
            break
    if tr == rows and rows * cols > 512 * 1024 and cols % 256 == 0:
        tc = 256
    while k * 14 * tr * tc * 4 > (VMEM_LIMIT * 3) // 4 and tr % 16 == 0:
        tr //= 2

    def body(*refs):
        for i in range(k):
            w_ref, g_ref, m_ref, v_ref = refs[4 * i:4 * i + 4]
            delta_ref, nm_ref, nv_ref = refs[4 * k + 3 * i:4 * k + 3 * i + 3]
            delta_ref[...], nm_ref[...], nv_ref[...] = _adamw_update(w_ref[...], g_ref[...], m_ref[...], v_ref[...])

    spec = pl.BlockSpec((tr, tc), lambda i, j: (i, j))
    shape = jax.ShapeDtypeStruct((rows, cols), F32)
    outs = _pallas(body, name, (rows // tr, cols // tc), [spec] * (4 * k), [spec] * (3 * k), [shape] * (3 * k),
                   [a for quad in quads for a in quad], ("arbitrary", "arbitrary"), carried)
    return [tuple(outs[3 * i:3 * i + 3]) for i in range(k)]


def _adamw_update(w, g, m, v):
    nm = ADAM_B1 * m + (1.0 - ADAM_B1) * g
    nv = ADAM_B2 * v + (1.0 - ADAM_B2) * (g * g)
    m_hat = nm * (1.0 / (1.0 - ADAM_B1 ** ADAM_STEP))
    v_hat = nv * (1.0 / (1.0 - ADAM_B2 ** ADAM_STEP))
    return -ADAM_LR * (m_hat / (jnp.sqrt(v_hat) + ADAM_EPS) + ADAM_WD * w), nm, nv


def _adamw_whole(quads, name):
    k = len(quads)

    def body(*refs):
        for i in range(k):
            w_ref, g_ref, m_ref, v_ref = refs[4 * i:4 * i + 4]
            g_out, delta_ref, nm_ref, nv_ref = refs[4 * k + 4 * i:4 * k + 4 * i + 4]
            gv = g_ref[...]
            g_out[...] = gv
            delta_ref[...], nm_ref[...], nv_ref[...] = _adamw_update(w_ref[...], gv, m_ref[...], v_ref[...])

    vm = pl.BlockSpec(memory_space=pltpu.VMEM)
    outs = pl.pallas_call(body, name=name, in_specs=[vm] * (4 * k), out_specs=[vm] * (4 * k),
                          out_shape=[jax.ShapeDtypeStruct(q[0].shape, F32) for q in quads for _ in range(4)],
                          compiler_params=_params())(*[a for quad in quads for a in quad])
    return [tuple(outs[4 * i:4 * i + 4]) for i in range(k)]


def _adamw_small(packed_grads, triples, segments):
    k = len(triples)

    def body(*refs):
        g_ref = refs[0]
        off = 0
        for i in range(k):
            w_ref, m_ref, v_ref = refs[1 + 3 * i:4 + 3 * i]
            g_out, delta_ref, nm_ref, nv_ref = refs[1 + 3 * k + 4 * i:5 + 3 * k + 4 * i]
            gv = g_ref[:, off:off + w_ref.shape[1]]
            g_out[...] = gv
            delta_ref[...], nm_ref[...], nv_ref[...] = _adamw_update(w_ref[...], gv, m_ref[...], v_ref[...])
            off += segments[i]

    vm = pl.BlockSpec(memory_space=pltpu.VMEM)
    outs = pl.pallas_call(
        body, name="adamw_small", in_specs=[vm] * (1 + 3 * k), out_specs=[vm] * (4 * k),
        out_shape=[jax.ShapeDtypeStruct(w.shape, F32) for w, _, _ in triples for _ in range(4)],
    )(packed_grads, *[a for triple in triples for a in triple])
    return [tuple(outs[4 * i:4 * i + 4]) for i in range(k)]


def _place():
    x, y, c = lax.axis_index("x"), lax.axis_index("y"), lax.axis_index("c")
    other_chips = [(1 - x, y), (x, 1 - y), (1 - x, 1 - y)]
    return x, y, c, other_chips


def _remote(src, dst, sems, send, recv, device):
    return pltpu.make_async_remote_copy(src_ref=src, dst_ref=dst, send_sem=sems.at[send], recv_sem=sems.at[recv],
                                        device_id=device, device_id_type=MESH_ID)


def _cast_shards(shards, out_dtypes, n_first):
    n = len(shards)
    out_shape = [jax.ShapeDtypeStruct((N_CHIPS,) + s.shape, dt) for s, dt in zip(shards, out_dtypes)]
    gather = _gather_carried(out_shape[:n_first])

    def body(*refs):
        ins, outs, stage, sems = refs[:n], refs[n:2 * n], refs[2 * n:3 * n], refs[3 * n]
        x, y, _, _ = _place()
        me = 2 * x + y

        def cast(first, last):
            copies = []
            for w in range(first, last):
                stage[w][...] = ins[w][...].astype(out_dtypes[w])
                copies.append(pltpu.make_async_copy(stage[w], outs[w].at[me], sems.at[w]))
                copies[-1].start()
            for cp in copies:
                cp.wait()

        cast(0, n_first)
        gather.start(None, outs[:n_first], sems, n)
        cast(n_first, n)
        gather.finish(None, outs[:n_first], sems, n)

    vm = pl.BlockSpec(memory_space=pltpu.VMEM)
    return pl.pallas_call(
        body, name="cast_shards", in_specs=[vm] * n, out_specs=[ANY] * n, out_shape=out_shape,
        scratch_shapes=[pltpu.VMEM(s.shape, dt) for s, dt in zip(shards, out_dtypes)] + [pltpu.SemaphoreType.DMA((n + gather.n_sems,))],
        compiler_params=_params())(*shards)


BF16_ROWS = 16


def _split_rows(rows):
    return (rows // 2) % BF16_ROWS == 0


def _half_shape(rows, cols):
    return (rows // 2, cols) if _split_rows(rows) else (rows, cols // 2)


def _half(rows, cols, which):
    if _split_rows(rows):
        return (pl.ds(pl.multiple_of(which * (rows // 2), BF16_ROWS), rows // 2), slice(None))
    return (slice(None), pl.ds(pl.multiple_of(which * (cols // 2), 128), cols // 2))


def _gather_carried(bufs):
    n = len(bufs)

    def half(w, slot, which):
        _, rows, cols = bufs[w].shape
        return (slot,) + _half(rows, cols, which)

    def start(ins, outs, sems, base):
        x, y, c, other_chips = _place()
        me = 2 * x + y
        for w in range(n):
            mine = outs[w].at[half(w, me, c)]
            for p, (px, py) in enumerate(other_chips):
                _remote(mine, mine, sems, base + 12 * w + p, base + 12 * w + 3 + p, (px, py, c)).start()

    def finish(ins, outs, sems, base):
        x, y, c, other_chips = _place()
        me = 2 * x + y
        for w in range(n):
            for p, (px, py) in enumerate(other_chips):
                got = outs[w].at[half(w, 2 * px + py, c)]
                _remote(got, got, sems, base + 12 * w + p, base + 12 * w + 3 + p, (px, py, c)).wait_recv()
                _remote(got, got, sems, base + 12 * w + 6 + p, base + 12 * w + 9 + p, (x, y, 1 - c)).start()
        for w in range(n):
            mine = outs[w].at[half(w, me, c)]
            for p, (px, py) in enumerate(other_chips):
                got = outs[w].at[half(w, 2 * px + py, c)]
                theirs = outs[w].at[half(w, 2 * px + py, 1 - c)]
                _remote(got, theirs, sems, base + 12 * w + 6 + p, base + 12 * w + 9 + p, (x, y, 1 - c)).wait()
                _remote(mine, mine, sems, base + 12 * w + p, base + 12 * w + 3 + p, (px, py, c)).wait_send()

    shapes = [jax.ShapeDtypeStruct(b.shape, b.dtype) for b in bufs]
    return _Carried(bufs, shapes, {w: w for w in range(n)}, 12 * n, start, finish)


def _swap_carried(grads):
    n = len(grads)

    def copy(w, ins, outs, sems, base):
        x, y, c, _ = _place()
        _, rows, cols = grads[w].shape
        theirs = ins[w].at[(slice(None),) + _half(rows, cols, 1 - c)]
        return _remote(theirs, outs[w], sems, base + 2 * w, base + 2 * w + 1, (x, y, 1 - c))

    def start(ins, outs, sems, base):
        for w in range(n):
            copy(w, ins, outs, sems, base).start()

    def finish(ins, outs, sems, base):
        for w in range(n):
            copy(w, ins, outs, sems, base).wait()

    shapes = [jax.ShapeDtypeStruct((g.shape[0],) + _half_shape(*g.shape[1:]), F32) for g in grads]
    return _Carried(grads, shapes, {}, 2 * n, start, finish)


def _row_tile(rows):
    for cand in (512, 352, 256, 192, 128, 96, 64, 32, 16):
        if rows % cand == 0:
            return cand
    return rows


def _half_block_index(split_rows, tiles, i, core):
    return (core * tiles + i, 0) if split_rows else (i, core)


def _chip_partial(grad, other, place, name):
    nblk, hr, hc = other.shape
    by_rows = _split_rows(grad.shape[1])
    tr = _row_tile(hr)
    tiles = hr // tr

    def body(place_ref, g_ref, o_ref, own_ref, sum_bf_ref):
        s = g_ref[...] + o_ref[...]
        sum_bf_ref[...] = _bf(s)

        @pl.when(pl.program_id(1) == place_ref[0])
        def _():
            own_ref[...] = s

    grid_spec = pltpu.PrefetchScalarGridSpec(
        num_scalar_prefetch=1, grid=(tiles, nblk),
        in_specs=[pl.BlockSpec((None, tr, hc), lambda i, b, place_ref: (b,) + _half_block_index(by_rows, tiles, i, place_ref[1])),
                  pl.BlockSpec((None, tr, hc), lambda i, b, place_ref: (b, i, 0))],
        out_specs=[pl.BlockSpec((tr, hc), lambda i, b, place_ref: (i, 0)),
                   pl.BlockSpec((None, tr, hc), lambda i, b, place_ref: (b, i, 0))])
    return pl.pallas_call(body, name=name, grid_spec=grid_spec,
                          out_shape=[jax.ShapeDtypeStruct((hr, hc), F32), jax.ShapeDtypeStruct(other.shape, BF16)],
                          compiler_params=_params(("arbitrary", "arbitrary")))(place, grad, other)


def _chip_partial_small(grads, others, place):
    n = len(grads)

    def body(*refs):
        place_ref, g_refs, o_refs = refs[0], refs[1:1 + n], refs[1 + n:1 + 2 * n]
        own_refs, bf_refs = refs[1 + 2 * n:1 + 3 * n], refs[1 + 3 * n:]
        chip, core = place_ref[0], place_ref[1]
        for w in range(n):
            _, rows, cols = grads[w].shape
            half = _half(rows, cols, core)
            bf_refs[w][...] = _bf(g_refs[w][(slice(None),) + half] + o_refs[w][...])
            own_refs[w][...] = g_refs[w][(chip,) + half] + o_refs[w][chip]

    vm = pl.BlockSpec(memory_space=pltpu.VMEM)
    outs = pl.pallas_call(
        body, name="chip_partial_small", in_specs=[pl.BlockSpec(memory_space=pltpu.SMEM)] + [vm] * (2 * n), out_specs=[vm] * (2 * n),
        out_shape=[jax.ShapeDtypeStruct(o.shape[1:], F32) for o in others] + [jax.ShapeDtypeStruct(o.shape, BF16) for o in others],
        compiler_params=_params())(place, *grads, *others)
    return list(zip(outs[:n], outs[n:]))


def _chip_total_small(owns, receiveds, place, shapes):
    n = len(owns)

    def body(*refs):
        place_ref, own_refs, r_refs, out_refs = refs[0], refs[1:1 + n], refs[1 + n:1 + 2 * n], refs[1 + 2 * n:]
        chip, core = place_ref[0], place_ref[1]
        for w in range(n):
            r = [r_refs[w][(chip + k) % N_CHIPS].astype(F32) for k in (1, 2, 3)]
            out_refs[w][_half(*shapes[w], core)] = own_refs[w][...] + ((r[0] + r[1]) + r[2])

    vm = pl.BlockSpec(memory_space=pltpu.VMEM)
    return list(pl.pallas_call(
        body, name="chip_total_small", in_specs=[pl.BlockSpec(memory_space=pltpu.SMEM)] + [vm] * (2 * n), out_specs=[vm] * n,
        out_shape=[jax.ShapeDtypeStruct(tuple(s), F32) for s in shapes], compiler_params=_params())(place, *owns, *receiveds))


def _send_carried(partials):
    n = len(partials)

    def start(ins, outs, sems, base):
        x, y, c, other_chips = _place()
        me = 2 * x + y
        for w in range(n):
            for p, (px, py) in enumerate(other_chips):
                _remote(ins[w].at[2 * px + py], outs[w].at[me], sems, base + 6 * w + p, base + 6 * w + 3 + p, (px, py, c)).start()

    def finish(ins, outs, sems, base):
        x, y, c, other_chips = _place()
        for w in range(n):
            for p, (px, py) in enumerate(other_chips):
                _remote(ins[w].at[2 * px + py], outs[w].at[2 * px + py], sems, base + 6 * w + p, base + 6 * w + 3 + p,
                        (px, py, c)).wait()

    return _Carried(partials, [jax.ShapeDtypeStruct(p.shape, BF16) for p in partials], {}, 6 * n, start, finish)


def _chip_total(own, received, place, shape, name):
    hr, hc = own.shape
    by_rows = _split_rows(shape[0])
    tr = _row_tile(hr)
    tiles = hr // tr

    def body(place_ref, own_ref, r1_ref, r2_ref, r3_ref, out_ref):
        out_ref[...] = own_ref[...] + ((r1_ref[...].astype(F32) + r2_ref[...].astype(F32)) + r3_ref[...].astype(F32))

    def slot(k):
        return pl.BlockSpec((None, tr, hc), lambda i, place_ref: ((place_ref[0] + k) % N_CHIPS, i, 0))

    grid_spec = pltpu.PrefetchScalarGridSpec(
        num_scalar_prefetch=1, grid=(tiles,), in_specs=[pl.BlockSpec((tr, hc), lambda i, place_ref: (i, 0)), slot(1), slot(2), slot(3)],
        out_specs=pl.BlockSpec((tr, hc), lambda i, place_ref: _half_block_index(by_rows, tiles, i, place_ref[1])))
    return pl.pallas_call(body, name=name, grid_spec=grid_spec, out_shape=jax.ShapeDtypeStruct(tuple(shape), F32),
                          compiler_params=_params(("arbitrary",)))(place, own, received, received, received)


def _join_carried(totals):
    n = len(totals)

    def copy(w, outs, sems, base):
        x, y, c, _ = _place()
        mine = outs[w].at[_half(*totals[w].shape, c)]
        return _remote(mine, mine, sems, base + 2 * w, base + 2 * w + 1, (x, y, 1 - c))

    def start(ins, outs, sems, base):
        for w in range(n):
            copy(w, outs, sems, base).start()

    def finish(ins, outs, sems, base):
        for w in range(n):
            copy(w, outs, sems, base).wait()

    shapes = [jax.ShapeDtypeStruct(a.shape, F32) for a in totals]
    return _Carried(totals, shapes, {w: w for w in range(n)}, 2 * n, start, finish)


def _sum_devices(vec):
    rows, n = vec.shape

    def body(v_ref, out_ref, buf, send_sems, recv_sems):
        x, y, c, _ = _place()
        me = 4 * x + 2 * y + c
        buf[me] = v_ref[...]
        sends = []
        for k in range(1, N_DEV):
            peer = (1 - x if k & 4 else x, 1 - y if k & 2 else y, 1 - c if k & 1 else c)
            cp = pltpu.make_async_remote_copy(src_ref=v_ref, dst_ref=buf.at[me], send_sem=send_sems.at[k], recv_sem=recv_sems.at[k],
                                              device_id=peer, device_id_type=MESH_ID)
            cp.start()
            sends.append(cp)
        for cp in sends:
            cp.wait()
        total = buf[0]
        for dev in range(1, N_DEV):
            total = total + buf[dev]
        out_ref[...] = total

    vm = pl.BlockSpec(memory_space=pltpu.VMEM)
    return pl.pallas_call(
        body, name="sum_devices", in_specs=[vm], out_specs=vm, out_shape=jax.ShapeDtypeStruct((rows, n), F32),
        scratch_shapes=[pltpu.VMEM((N_DEV, rows, n), F32), pltpu.SemaphoreType.DMA((N_DEV,)), pltpu.SemaphoreType.DMA((N_DEV,))],
    )(vec)


def _rope_tables(positions):
    half = ROPE_HALF
    inv_freq = 1.0 / (ROPE_THETA ** (jnp.arange(half, dtype=F32) / half))
    ang = positions.astype(F32).reshape(-1, 1) * inv_freq
    cos, sin = jnp.cos(ang), jnp.sin(ang)
    t = ang.shape[0]
    ones, zeros = jnp.ones((t, QK_NOPE), F32), jnp.zeros((t, QK_NOPE), F32)
    pad, none = HEAD_PAD - QK_DIM, zeros[:, :half]
    cos_full = jnp.concatenate([ones, cos, cos, ones[:, :pad]], axis=1)
    s_lo = jnp.concatenate([zeros, -sin, none, zeros[:, :pad]], axis=1)
    s_hi = jnp.concatenate([zeros, none, sin, zeros[:, :pad]], axis=1)
    return cos_full, s_lo, s_hi


def _partials(names, grads, from_sibling, place):
    return [_chip_partial(g, o, place, "chip_partial_" + n) for n, g, o in zip(names, grads, from_sibling)]


def _totals(names, grads, partials, received, place):
    return [_chip_total(pf, r, place, g.shape[1:], "chip_total_" + n) for n, g, (pf, _), r in zip(names, grads, partials, received)]


def _kernel_layouts(full):
    w_in = full["w_in"]
    w_kr = jnp.pad(w_in[ROW_KR:ROW_XC], ((QK_NOPE, HEAD_PAD - QK_DIM), (0, 0)))
    w_uq = jnp.pad(full["w_uq"].reshape(Q_LORA, N_HEADS, QK_DIM), ((0, 0), (0, 0), (0, HEAD_PAD - QK_DIM)))
    w_uk = jnp.pad(full["w_uk"].reshape(KV_LORA, N_HEADS, QK_NOPE), ((0, 0), (0, 0), (0, HEAD_PAD - QK_NOPE)))
    return {"w_in": w_in, "w_kr": w_kr, "w_uq": w_uq.reshape(Q_LORA, N_HEADS * HEAD_PAD),
            "w_uk": w_uk.reshape(KV_LORA, N_HEADS * HEAD_PAD), "w_uv": full["w_uv"], "w_uvt": full["w_uv"].T}


def _global_layouts(g):
    w_uq = g["w_uq"].reshape(Q_LORA, N_HEADS, HEAD_PAD)[:, :, :QK_DIM].reshape(Q_LORA, N_HEADS * QK_DIM)
    w_uk = g["w_uk"].reshape(KV_LORA, N_HEADS, HEAD_PAD)[:, :, :QK_NOPE].reshape(KV_LORA, N_HEADS * QK_NOPE)
    return {"w_in": g["w_in"], "w_uq": w_uq, "w_uk": w_uk, "w_uv": g["w_uv"], "w_proj_attn": g["w_pa"], "w_proj_conv": g["w_pc"],
            "w_out": g["w_out"]}


def _dw_in(dm, dp, h2b):
    d, rows = D_MODEL, ROW_GL + 2 * D_MODEL
    wm, wp = M_COLS // 2, P_COLS // 2
    from_dm = [(0, [(ROW_GB, 0, d), (ROW_GL, d, wm - d)]), (1, [(ROW_GL + wm - d, 0, wm)])]
    from_dp = [(0, [(ROW_XC, 0, d), (ROW_GC, d, wp - d)]),
               (1, [(ROW_GC + wp - d, 0, 2 * d - wp), (ROW_QKV, 2 * d - wp, ROW_KR), (ROW_KR, 2 * d - wp + ROW_KR + QK_NOPE, QK_ROPE)])]
    out = _tn_rows(dm, h2b, None, from_dm, wm, rows, "dw_in_m")
    return _tn_rows(dp, h2b, out, from_dp, wp, rows, "dw_in_p")


def _col_blocks(a):
    r, c = a.shape
    return a.reshape(r, N_CHIPS, c // N_CHIPS).transpose(1, 0, 2)


def _from_col_blocks(a):
    n, r, c = a.shape
    return a.transpose(1, 0, 2).reshape(r, n * c)


COL_SHARDED = ("w_uq", "w_uk", "w_uv", "w_proj_attn")
TRANSPOSED = ("ffn1_w_gate", "ffn1_w_up", "ffn2_w_gate", "ffn2_w_up", "w_in")
SMALL = (("ffn1_norm", 1024), ("mix_norm", 1024), ("gate_bias", 2048), ("q_a_norm", 384), ("kv_a_norm", 256),
         ("q_head_norm", 128), ("k_head_norm", 128), ("ffn2_norm", 1024))
WEIGHT_ORDER = ("ffn1_norm", "ffn1_w_gate", "ffn1_w_up", "ffn1_w_down", "mix_norm", "w_in", "gate_bias", "q_a_norm", "w_uq",
                "kv_a_norm", "w_uk", "w_uv", "q_head_norm", "k_head_norm", "w_proj_attn", "conv_w", "w_proj_conv", "w_out",
                "ffn2_norm", "ffn2_w_gate", "ffn2_w_up", "ffn2_w_down")
MATRICES = ("ffn1_w_gate", "ffn1_w_up", "ffn1_w_down", "w_in", "w_uq", "w_uk", "w_uv", "w_proj_attn", "w_proj_conv", "w_out",
            "ffn2_w_gate", "ffn2_w_up", "ffn2_w_down")
GROUP_FFN1 = ("ffn1_w_gate", "ffn1_w_up", "ffn1_w_down")
GROUP_IN = ("w_in", "w_uq", "w_uk", "w_uv", "conv_w")
GROUP_MIX = ("w_proj_attn", "w_proj_conv", "w_out")
GROUP_FFN2 = ("ffn2_w_gate", "ffn2_w_up", "ffn2_w_down")
GROUP_MID = ("w_in", "w_uq", "w_uk", "w_uv", "w_proj_attn", "w_proj_conv", "w_out")


def _pad_lanes(a, n):
    return jnp.pad(a.reshape(1, -1), ((0, 0), (0, n - a.size)))


def kernel(x, positions, ffn1_norm, ffn1_w_gate, ffn1_w_up, ffn1_w_down, mix_norm, w_in, gate_bias, q_a_norm, w_uq, kv_a_norm, w_uk, w_uv, q_head_norm, k_head_norm, w_proj_attn, conv_w, w_proj_conv, w_out, ffn2_norm, ffn2_w_gate, ffn2_w_up, ffn2_w_down, loss_target, m_ffn1_norm, m_ffn1_w_gate, m_ffn1_w_up, m_ffn1_w_down, m_mix_norm, m_w_in, m_gate_bias, m_q_a_norm, m_w_uq, m_kv_a_norm, m_w_uk, m_w_uv, m_q_head_norm, m_k_head_norm, m_w_proj_attn, m_conv_w, m_w_proj_conv, m_w_out, m_ffn2_norm, m_ffn2_w_gate, m_ffn2_w_up, m_ffn2_w_down, v_ffn1_norm, v_ffn1_w_gate, v_ffn1_w_up, v_ffn1_w_down, v_mix_norm, v_w_in, v_gate_bias, v_q_a_norm, v_w_uq, v_kv_a_norm, v_w_uk, v_w_uv, v_q_head_norm, v_k_head_norm, v_w_proj_attn, v_conv_w, v_w_proj_conv, v_w_out, v_ffn2_norm, v_ffn2_w_gate, v_ffn2_w_up, v_ffn2_w_down):
    args = dict(locals())
    view = lambda n, a: a.T if n in TRANSPOSED else a
    weights = {n: view(n, args[n]) for n in WEIGHT_ORDER}
    moments_m = {n: view(n, args["m_" + n]) for n in WEIGHT_ORDER}
    moments_v = {n: view(n, args["v_" + n]) for n in WEIGHT_ORDER}
    nb, seq, d = x.shape
    t = nb * seq
    chip = (2 * lax.axis_index("x") + lax.axis_index("y")).astype(jnp.int32)
    place = jnp.stack([chip, lax.axis_index("c").astype(jnp.int32)])
    grads, delta, new_m, new_v = {}, {}, {}, {}

    def adamw(names, carried=None):
        results = _adamw([(weights[n], grads[n], moments_m[n], moments_v[n]) for n in names], "adamw_" + names[0], carried)
        for n, (dn, mn, vn) in zip(names, results):
            delta[n], new_m[n], new_v[n] = dn, mn, vn

    conv_rows = conv_w.shape[0]
    conv_shard = jnp.pad(conv_w, ((0, 16 - conv_rows), (0, 0)))
    assert MATRICES[:len(GROUP_FFN1)] == GROUP_FFN1
    bufs = dict(zip(MATRICES + ("conv_w",), _cast_shards([weights[n] for n in MATRICES] + [conv_shard],
                                                         [BF16] * len(MATRICES) + [F32], len(GROUP_FFN1))))
    blocks = {n: bufs[n] for n in GROUP_FFN1}
    p = {n: _pad_lanes(weights[n], size) for n, size in SMALL}
    rope = _rope_tables(positions)
    x_tok = x.reshape(t, d)

    gather_in = _gather_carried([bufs[n] for n in GROUP_IN])
    x1, gate1, up1, act1 = _ffn_fwd(x_tok, p["ffn1_norm"], blocks["ffn1_w_gate"], blocks["ffn1_w_up"], blocks["ffn1_w_down"], None,
                                    "ffn1_fwd", gather_in)
    blocks.update(zip(GROUP_IN, gather_in.results))
    w = _kernel_layouts({"w_in": blocks["w_in"].reshape(-1, d), **{n: _from_col_blocks(blocks[n]) for n in ("w_uq", "w_uk", "w_uv")}})
    p["conv_w"] = _from_col_blocks(blocks["conv_w"])[:conv_rows]

    gather_mix = _gather_carried([bufs[n] for n in GROUP_MIX + GROUP_FFN2[2:]])
    h2b, big, lat, q, k, v, vt = _inproj_fwd(x1, p["mix_norm"], w["w_in"], w["w_kr"], p["q_a_norm"], p["kv_a_norm"], p["q_head_norm"],
                                             p["k_head_norm"], w["w_uq"], w["w_uk"], w["w_uv"], w["w_uvt"], rope, gather_mix)
    blocks.update(zip(GROUP_MIX + GROUP_FFN2[2:], gather_mix.results))
    w_pa = _from_col_blocks(blocks["w_proj_attn"])
    w_pc, w_out_full = blocks["w_proj_conv"].reshape(-1, d), blocks["w_out"].reshape(-1, d)

    gather_ffn2 = _gather_carried([bufs[n] for n in GROUP_FFN2[:2]])
    o, lse = _attn_fwd(q, k, vt, seq, gather_ffn2)
    x2 = _mix_fwd(x1, o, big, p["gate_bias"], p["conv_w"], w_pa, w_pc, w_out_full, seq)
    wg2, wu2 = gather_ffn2.results
    wd2 = blocks["ffn2_w_down"]
    dx3, gate2, up2, act2, loss = _ffn_fwd(x2, p["ffn2_norm"], wg2, wu2, wd2, loss_target.reshape(t, d), "ffn2_fwd")

    dx2, dg_ffn2, hb2, dgate2, dup2, dyb2 = _ffn_bwd_x(x2, p["ffn2_norm"], dx3, gate2, up2, wg2, wu2, wd2, "ffn2_bwd")
    g_ffn2 = [_tn_matmul(dgate2, hb2, "ffn2_dw_gate"), _tn_matmul(dup2, hb2, "ffn2_dw_up"), _tn_matmul(act2, dyb2, "ffn2_dw_down")]
    swap = _swap_carried(g_ffn2)
    do, delta_o, dz, dm, dbias, dw_pa, dw_pc, dw_out = _mix_bwd(dx2, o, big, p["gate_bias"], p["conv_w"], w_pa, w_pc, w_out_full, seq,
                                                                swap)
    part = _partials(GROUP_FFN2, g_ffn2, swap.results, place)
    send = _send_carried([pb for _, pb in part])
    dq, dk, dv = _attn_bwd(q, k, v, do, lse, delta_o.reshape(N_HEADS // ATTN_BWD_HEADS, ATTN_BWD_HEADS, -1), seq, send)
    join = _join_carried(_totals(GROUP_FFN2, g_ffn2, part, send.results, place))
    dp, dw_uq, dw_uk, dw_uv, dqa, dkva, dqh, dkh, dcw = _prep_bwd(
        lat, big, dz, dq, dk, dv, p["q_a_norm"], p["kv_a_norm"], p["q_head_norm"], p["k_head_norm"], w["w_uq"], w["w_uk"],
        w["w_uv"], rope, p["conv_w"], seq, join)
    grads.update(zip(GROUP_FFN2, join.results))

    gg = _global_layouts({"w_in": _dw_in(dm, dp, h2b), "w_uq": dw_uq, "w_uk": dw_uk, "w_uv": dw_uv, "w_pa": dw_pa, "w_pc": dw_pc,
                          "w_out": dw_out})
    g_mid = [_col_blocks(gg[n]) if n in COL_SHARDED else gg[n].reshape(N_CHIPS, -1, gg[n].shape[-1]) for n in GROUP_MID]
    swap = _swap_carried(g_mid)
    dx1, dg_mix = _inproj_bwd(x1, p["mix_norm"], dx2, dm, dp, w["w_in"], w["w_kr"], swap)
    part = (_partials(GROUP_MID[:1], g_mid[:1], swap.results[:1], place)
            + _chip_partial_small(g_mid[1:], swap.results[1:], place))
    send = _send_carried([pb for _, pb in part])
    grad_x, dg_ffn1, hb1, dgate1, dup1, dyb1 = _ffn_bwd_x(x_tok, p["ffn1_norm"], dx1, gate1, up1, blocks["ffn1_w_gate"],
                                                         blocks["ffn1_w_up"], blocks["ffn1_w_down"], "ffn1_bwd", send)

    small_grads = {"ffn1_norm": dg_ffn1, "mix_norm": dg_mix, "gate_bias": dbias, "q_a_norm": dqa, "kv_a_norm": dkva,
                   "q_head_norm": dqh, "k_head_norm": dkh, "ffn2_norm": dg_ffn2}
    packed = jnp.concatenate([small_grads[n] for n, _ in SMALL] + [dcw.reshape(1, -1), loss], axis=1)
    total = _sum_devices(packed.reshape(8, -1)).reshape(1, -1)
    n_small = sum(size for _, size in SMALL)
    conv_cols = conv_w.shape[1]
    conv_total = total[:, n_small:n_small + conv_rows * d].reshape(conv_rows, d)
    grads["conv_w"] = lax.dynamic_slice_in_dim(conv_total, chip * conv_cols, conv_cols, axis=1)
    loss_total = total[0, n_small + conv_rows * d]

    join = _join_carried(_totals(GROUP_MID[:1], g_mid[:1], part[:1], send.results[:1], place)
                         + _chip_total_small([pf for pf, _ in part[1:]], send.results[1:], place, [g.shape[1:] for g in g_mid[1:]]))
    g_gate = _tn_matmul(dgate1, hb1, "ffn1_dw_gate", carried=join)
    grads.update(zip(GROUP_MID, join.results))
    swap_gate = _swap_carried([g_gate])
    g_up = _tn_matmul(dup1, hb1, "ffn1_dw_up", carried=swap_gate)
    part_gate = _partials(GROUP_FFN1[:1], [g_gate], swap_gate.results, place)
    send_gate, swap_up = _send_carried([part_gate[0][1]]), _swap_carried([g_up])
    g_down = _tn_matmul(act1, dyb1, "ffn1_dw_down", carried=_both(send_gate, swap_up))
    join_gate = _join_carried(_totals(GROUP_FFN1[:1], [g_gate], part_gate, send_gate.results, place))
    part_up = _partials(GROUP_FFN1[1:2], [g_up], swap_up.results, place)
    send_up, swap_down = _send_carried([part_up[0][1]]), _swap_carried([g_down])
    adamw(("w_in",), _both(_both(send_up, swap_down), join_gate))
    grads["ffn1_w_gate"] = join_gate.results[0]
    join_up = _join_carried(_totals(GROUP_FFN1[1:2], [g_up], part_up, send_up.results, place))
    part_down = _partials(GROUP_FFN1[2:], [g_down], swap_down.results, place)
    send_down = _send_carried([part_down[0][1]])
    adamw(GROUP_FFN2, _both(send_down, join_up))
    grads["ffn1_w_up"] = join_up.results[0]
    join_down = _join_carried(_totals(GROUP_FFN1[2:], [g_down], part_down, send_down.results, place))
    adamw(GROUP_FFN1[:2], join_down)
    grads["ffn1_w_down"] = join_down.results[0]
    adamw(GROUP_FFN1[2:])
    others = GROUP_MID[1:] + ("conv_w",)
    for n, (gn, dn, mn, vn) in zip(others, _adamw_whole([(weights[n], grads[n], moments_m[n], moments_v[n]) for n in others],
                                                        "adamw_others")):
        grads[n], delta[n], new_m[n], new_v[n] = gn, dn, mn, vn

    row = lambda a: a.reshape(1, -1)
    small = _adamw_small(total, [(row(weights[n]), row(moments_m[n]), row(moments_v[n])) for n, _ in SMALL], [size for _, size in SMALL])
    for (n, _), (gn, dn, mn, vn) in zip(SMALL, small):
        grads[n], delta[n], new_m[n], new_v[n] = gn.reshape(-1), dn.reshape(-1), mn.reshape(-1), vn.reshape(-1)

    return (loss_total, grad_x.reshape(nb, seq, d), *[view(n, src[n]) for src in (grads, delta, new_m, new_v) for n in WEIGHT_ORDER])
```

```python
import functools

import jax
import jax.numpy as jnp
from jax import lax
from jax.experimental import pallas as pl
from jax.experimental.pallas import tpu as pltpu

F32 = jnp.float32
BF16 = jnp.bfloat16

D_MODEL = 1024
N_HEADS = 8
QK_NOPE = 64
QK_ROPE = 32
QK_DIM = QK_NOPE + QK_ROPE
V_DIM = 64
HEAD_PAD = 128
Q_LORA = 384
KV_LORA = 256
ROPE_THETA = 10000.0
NORM_EPS = 1e-6
ATTN_SCALE = QK_DIM ** -0.5
MASK_VALUE = -1e30
N_CHIPS = 4
N_DEV = 8

ADAM_LR = 0.001
ADAM_B1 = 0.9
ADAM_B2 = 0.999
ADAM_EPS = 1e-08
ADAM_WD = 0.01
ADAM_STEP = 10

TOKEN_TILE = 256
WIDE_TILE = 512
ATTN_TILE = 512
TN_TILE = 2048
VMEM_LIMIT = 56 * 1024 * 1024

M_COLS = 3 * D_MODEL
P_COLS = 2 * D_MODEL + Q_LORA + KV_LORA + HEAD_PAD
BIG_COLS = 5 * D_MODEL
LAT_COLS = Q_LORA + KV_LORA + HEAD_PAD

MESH_ID = pl.DeviceIdType.MESH
ANY = pl.BlockSpec(memory_space=pl.ANY)


def _params(semantics=None):
    return pltpu.CompilerParams(dimension_semantics=semantics, vmem_limit_bytes=VMEM_LIMIT)


class _Carried:
    def __init__(self, operands, out_shapes, aliases, n_sems, start, finish):
        self.operands, self.out_shapes, self.aliases, self.n_sems = list(operands), list(out_shapes), dict(aliases), n_sems
        self.start, self.finish = start, finish
        self.results = None


def _both(a, b):
    na, nao = len(a.operands), len(a.out_shapes)

    def start(ins, outs, sems, base):
        a.start(ins[:na], outs[:nao], sems, base)
        b.start(ins[na:], outs[nao:], sems, base + a.n_sems)

    def finish(ins, outs, sems, base):
        a.finish(ins[:na], outs[:nao], sems, base)
        b.finish(ins[na:], outs[nao:], sems, base + a.n_sems)

    aliases = dict(a.aliases)
    aliases.update({na + i: nao + o for i, o in b.aliases.items()})
    both = _Carried(a.operands + b.operands, a.out_shapes + b.out_shapes, aliases, a.n_sems + b.n_sems, start, finish)
    both.parts = (a, b)
    return both


def _set_results(carried, results):
    carried.results = list(results)
    if hasattr(carried, "parts"):
        a, b = carried.parts
        _set_results(a, results[:len(a.out_shapes)])
        _set_results(b, results[len(a.out_shapes):])


def _pallas(body, name, grid, in_specs, out_specs, out_shape, args, semantics, carried=None):
    if carried is None:
        return pl.pallas_call(body, name=name, grid=grid, in_specs=in_specs, out_specs=out_specs, out_shape=out_shape,
                              compiler_params=_params(semantics))(*args)
    n_in, n_out, n_ci, n_co = len(in_specs), len(out_specs), len(carried.operands), len(carried.out_shapes)

    def wrapped(*refs):
        ins, c_ins = refs[:n_in], refs[n_in:n_in + n_ci]
        outs, c_outs = refs[n_in + n_ci:n_in + n_ci + n_out], refs[n_in + n_ci + n_out:n_in + n_ci + n_out + n_co]
        sems = refs[-1]
        first = pl.program_id(0) == 0
        last = pl.program_id(0) == grid[0] - 1
        for axis in range(1, len(grid)):
            first = jnp.logical_and(first, pl.program_id(axis) == 0)
            last = jnp.logical_and(last, pl.program_id(axis) == grid[axis] - 1)

        @pl.when(first)
        def _():
            carried.start(c_ins, c_outs, sems, 0)

        body(*ins, *outs)

        @pl.when(last)
        def _():
            carried.finish(c_ins, c_outs, sems, 0)

    results = pl.pallas_call(
        wrapped, name=name, grid=grid, in_specs=list(in_specs) + [ANY] * n_ci, out_specs=list(out_specs) + [ANY] * n_co,
        out_shape=list(out_shape) + carried.out_shapes,
        input_output_aliases={n_in + i: n_out + o for i, o in carried.aliases.items()},
        scratch_shapes=[pltpu.SemaphoreType.DMA((carried.n_sems,))], compiler_params=_params(semantics))(*args, *carried.operands)
    _set_results(carried, results[n_out:])
    return results[:n_out]


def _resident(shape):
    nd = len(shape)
    return pl.BlockSpec(shape, lambda *_: (0,) * nd, pipeline_mode=pl.Buffered(1))


def _const(shape):
    nd = len(shape)
    return pl.BlockSpec(shape, lambda *_: (0,) * nd)


def _mm(a, b):
    return jnp.dot(a, b, preferred_element_type=F32)


def _mm_nt(a, b):
    return lax.dot_general(a, b, (((1,), (1,)), ((), ())), preferred_element_type=F32)


def _mm_tn(a, b):
    return lax.dot_general(a, b, (((0,), (0,)), ((), ())), preferred_element_type=F32)


def _bf(a):
    return a.astype(BF16)


def _sigmoid(a):
    return 1.0 / (1.0 + jnp.exp(-a))


def _rms(x, gain, n=None):
    n = x.shape[-1] if n is None else n
    r = lax.rsqrt(jnp.sum(x * x, axis=-1, keepdims=True) * (1.0 / n) + NORM_EPS)
    return (x * r) * gain, r


def _rms_bwd(x, r, gain, dh, n=None):
    n = x.shape[-1] if n is None else n
    u = dh * gain
    dx = r * u - x * ((r * r * r) * (jnp.sum(u * x, axis=-1, keepdims=True) * (1.0 / n)))
    dgain = jnp.sum(dh * (x * r), axis=0, keepdims=True)
    return dx, dgain


ROPE_HALF = QK_ROPE // 2


def _rope(t, rope):
    cos, s_lo, s_hi = rope
    return t * cos + pltpu.roll(t, HEAD_PAD - ROPE_HALF, 1) * s_lo + pltpu.roll(t, ROPE_HALF, 1) * s_hi


def _rope_bwd(dt, rope):
    cos, s_lo, s_hi = rope
    return dt * cos + pltpu.roll(dt * s_lo, ROPE_HALF, 1) + pltpu.roll(dt * s_hi, HEAD_PAD - ROPE_HALF, 1)


def _shift_down(u, prev8, k):
    s = pltpu.roll(u, k, 0)
    p = pltpu.roll(prev8, k, 0)
    row = lax.broadcasted_iota(jnp.int32, prev8.shape, 0)
    top = jnp.where(row < k, p, s[:8])
    return jnp.concatenate([top, s[8:]], axis=0)


def _shift_up(d, next8, k):
    tm = d.shape[0]
    s = pltpu.roll(d, tm - k, 0)
    n = pltpu.roll(next8, 8 - k, 0)
    row = lax.broadcasted_iota(jnp.int32, next8.shape, 0)
    bot = jnp.where(row >= 8 - k, n, s[tm - 8:])
    return jnp.concatenate([s[:tm - 8], bot], axis=0)


def _ffn_fwd(x, gain, wg, wu, wd, target, name, carried=None):
    t, d = x.shape
    nb, f, _ = wg.shape
    tm = TOKEN_TILE
    with_loss = target is not None

    def body(*refs):
        if with_loss:
            x_ref, g_ref, wg_ref, wu_ref, wd_ref, t_ref, out_ref, gate_ref, up_ref, act_ref, loss_ref = refs
        else:
            x_ref, g_ref, wg_ref, wu_ref, wd_ref, out_ref, gate_ref, up_ref, act_ref = refs
        xv = x_ref[...]
        h, _ = _rms(xv, g_ref[...])
        hb = _bf(h)
        y = jnp.zeros((tm, d), F32)
        nxt = (_mm_nt(hb, wg_ref[0]), _mm_nt(hb, wu_ref[0]))
        for j in range(nb):
            gate, up = nxt
            if j + 1 < nb:
                nxt = (_mm_nt(hb, wg_ref[j + 1]), _mm_nt(hb, wu_ref[j + 1]))
            act = _bf((gate * _sigmoid(gate)) * up)
            y = y + _mm(act, wd_ref[j])
            gate_ref[j] = _bf(gate)
            up_ref[j] = _bf(up)
            act_ref[j] = act
        out = xv + 0.5 * y
        if with_loss:
            err = out - t_ref[...]
            out_ref[...] = err * (1.0 / d)

            @pl.when(pl.program_id(0) == 0)
            def _():
                loss_ref[...] = jnp.zeros_like(loss_ref)

            part = jnp.sum(jnp.sum(err * err, axis=1, keepdims=True), axis=0, keepdims=True)
            loss_ref[...] += jnp.broadcast_to(part * (0.5 / d), loss_ref.shape)
        else:
            out_ref[...] = out

    tok = pl.BlockSpec((tm, d), lambda i: (i, 0))
    blk = pl.BlockSpec((nb, tm, f), lambda i: (0, i, 0))
    in_specs = [tok, _const((1, d)), _resident(wg.shape), _resident(wu.shape), _resident(wd.shape)]
    args = [x, gain, wg, wu, wd]
    out_shape = [jax.ShapeDtypeStruct((t, d), F32)] + [jax.ShapeDtypeStruct((nb, t, f), BF16)] * 3
    out_specs = [tok, blk, blk, blk]
    if with_loss:
        in_specs.append(tok)
        args.append(target)
        out_shape.append(jax.ShapeDtypeStruct((1, 128), F32))
        out_specs.append(_const((1, 128)))
    return _pallas(body, name, (t // tm,), in_specs, out_specs, out_shape, args, ("arbitrary",), carried)


def _ffn_bwd_x(x, gain, dout, gate, up, wg, wu, wd, name, carried=None):
    t, d = x.shape
    nb, f, _ = wg.shape
    tm = TOKEN_TILE

    def body(x_ref, g_ref, dout_ref, gate_ref, up_ref, wg_ref, wu_ref, wd_ref,
             dx_ref, dgain_ref, hb_ref, dgate_ref, dup_ref, dyb_ref):
        xv = x_ref[...]
        gain_v = g_ref[...]
        h, r = _rms(xv, gain_v)
        hb_ref[...] = _bf(h)
        dout_v = dout_ref[...]
        dyb = _bf(0.5 * dout_v)
        dyb_ref[...] = dyb
        dh = jnp.zeros((tm, d), F32)
        nxt = _mm_nt(dyb, wd_ref[0])
        for j in range(nb):
            dact = nxt
            if j + 1 < nb:
                nxt = _mm_nt(dyb, wd_ref[j + 1])
            gt = gate_ref[j].astype(F32)
            uv = up_ref[j].astype(F32)
            s = _sigmoid(gt)
            dup = _bf(dact * (gt * s))
            dgate = _bf((dact * uv) * (s * (1.0 + gt * (1.0 - s))))
            dh = dh + _mm(dgate, wg_ref[j]) + _mm(dup, wu_ref[j])
            dgate_ref[j] = dgate
            dup_ref[j] = dup
        dxn, dgain = _rms_bwd(xv, r, gain_v, dh)
        dx_ref[...] = dout_v + dxn

        @pl.when(pl.program_id(0) == 0)
        def _():
            dgain_ref[...] = jnp.zeros_like(dgain_ref)

        dgain_ref[...] += dgain

    tok = pl.BlockSpec((tm, d), lambda i: (i, 0))
    blk = pl.BlockSpec((nb, tm, f), lambda i: (0, i, 0))
    return _pallas(
        body, name, (t // tm,),
        [tok, _const((1, d)), tok, blk, blk, _resident(wg.shape), _resident(wu.shape), _resident(wd.shape)],
        [tok, _const((1, d)), tok, blk, blk, tok],
        [jax.ShapeDtypeStruct((t, d), F32), jax.ShapeDtypeStruct((1, d), F32), jax.ShapeDtypeStruct((t, d), BF16),
         jax.ShapeDtypeStruct((nb, t, f), BF16), jax.ShapeDtypeStruct((nb, t, f), BF16), jax.ShapeDtypeStruct((t, d), BF16)],
        (x, gain, dout, gate, up, wg, wu, wd), ("arbitrary",), carried)


def _tn_matmul(a, b, name, carried=None):
    t = a.shape[-2]
    k = a.shape[-1]
    n = b.shape[-1]
    tt = min(2 * TN_TILE, t)
    nt = t // tt

    def body(a_ref, b_ref, o_ref):
        if nt == 1:
            o_ref[...] = _mm_tn(a_ref[...], b_ref[...])
            return

        @pl.when(pl.program_id(1) == 0)
        def _():
            o_ref[...] = jnp.zeros_like(o_ref)

        o_ref[...] += _mm_tn(a_ref[...], b_ref[...])

    g = a.shape[0] if a.ndim == 3 else b.shape[0]
    a_spec = (pl.BlockSpec((None, tt, k), lambda gi, ti: (gi, ti, 0)) if a.ndim == 3
              else pl.BlockSpec((tt, k), lambda gi, ti: (ti, 0)))
    b_spec = (pl.BlockSpec((None, tt, n), lambda gi, ti: (gi, ti, 0)) if b.ndim == 3
              else pl.BlockSpec((tt, n), lambda gi, ti: (ti, 0)))
    o_spec = pl.BlockSpec((None, k, n), lambda gi, ti: (gi, 0, 0))
    out_shape = jax.ShapeDtypeStruct((g, k, n), F32)
    return _pallas(body, name, (g, nt), [a_spec, b_spec], [o_spec], [out_shape], (a, b), ("arbitrary", "arbitrary"), carried)[0]


def _tn_rows(a, b, out, chunks, width, rows_out, name):
    t, n = b.shape
    tt = min(TN_TILE, t)
    nt = t // tt

    def body(blocks_ref, a_ref, b_ref, *rest):
        out_ref, acc, sem = rest[-3:]
        g, ti = pl.program_id(0), pl.program_id(1)

        @pl.when(ti == 0)
        def _():
            acc[...] = jnp.zeros_like(acc)

        acc[...] += _mm_tn(a_ref[...], b_ref[...])
        for gi, (_, ranges) in enumerate(chunks):
            @pl.when(jnp.logical_and(g == gi, ti == nt - 1))
            def _(ranges=ranges):
                for row, first, count in ranges:
                    cp = pltpu.make_async_copy(acc.at[first:first + count], out_ref.at[row:row + count], sem)
                    cp.start()
                    cp.wait()

    blocks = jnp.asarray([c[0] for c in chunks], jnp.int32)
    grid_spec = pltpu.PrefetchScalarGridSpec(
        num_scalar_prefetch=1, grid=(len(chunks), nt),
        in_specs=[pl.BlockSpec((tt, width), lambda g, ti, blocks_ref: (ti, blocks_ref[g])),
                  pl.BlockSpec((tt, n), lambda g, ti, blocks_ref: (ti, 0))] + ([ANY] if out is not None else []),
        out_specs=ANY, scratch_shapes=[pltpu.VMEM((width, n), F32), pltpu.SemaphoreType.DMA])
    args = (blocks, a, b) + ((out,) if out is not None else ())
    return pl.pallas_call(body, name=name, grid_spec=grid_spec, out_shape=jax.ShapeDtypeStruct((rows_out, n), F32),
                          input_output_aliases={3: 0} if out is not None else {},
                          compiler_params=_params(("arbitrary", "arbitrary")))(*args)


ROW_QKV, ROW_KR, ROW_XC = 0, Q_LORA + KV_LORA, Q_LORA + KV_LORA + QK_ROPE
ROW_GB, ROW_GC, ROW_GL = ROW_XC + D_MODEL, ROW_XC + 2 * D_MODEL, ROW_XC + 3 * D_MODEL
BIG_FROM_ROWS = ((0, ROW_GB, D_MODEL), (D_MODEL, ROW_GL, 2 * D_MODEL), (3 * D_MODEL, ROW_XC, D_MODEL), (4 * D_MODEL, ROW_GC, D_MODEL))


def _inproj_fwd(x1, gain, w_in, w_kr, qa_gain, kva_gain, qh_gain, kh_gain, w_uq, w_uk, w_uv, w_uvt, rope, carried=None):
    t, d = x1.shape
    tm = TOKEN_TILE
    chunk = 512
    chunks = []
    for col, row, size in BIG_FROM_ROWS:
        chunks += [(col + o, row + o, chunk) for o in range(0, size, chunk)]
    of_head = [[c for k, c in enumerate(chunks) if k * N_HEADS // len(chunks) == hd] for hd in range(N_HEADS)]

    def body(x_ref, g_ref, win_ref, wkr_ref, qa_ref, kva_ref, qh_ref, kh_ref, wuq_ref, wuk_ref, wuv_ref, wuvt_ref, cos_ref, slo_ref,
             shi_ref, hb_ref, big_ref, lat_ref, q_ref, k_ref, v_ref, vt_ref):
        h, _ = _rms(x_ref[...], g_ref[...])
        hb = _bf(h)
        hb_ref[...] = hb
        k_rope = _mm_nt(hb, wkr_ref[...])
        lat = jnp.concatenate([_mm_nt(hb, win_ref[ROW_QKV:ROW_KR, :]), k_rope], axis=1)
        lat_ref[...] = lat
        cq, _ = _rms(lat[:, :Q_LORA], qa_ref[...])
        ckv, _ = _rms(lat[:, Q_LORA:Q_LORA + KV_LORA], kva_ref[...])
        cqb = _bf(cq)
        ckvb = _bf(ckv)
        rope_v = (cos_ref[...], slo_ref[...], shi_ref[...])
        q_all = _mm(cqb, wuq_ref[...])
        k_all = _mm(ckvb, wuk_ref[...])
        v_ref[...] = _bf(_mm(ckvb, wuv_ref[...]))
        vt_all = _mm_nt(wuvt_ref[...], ckvb)
        for hd in range(N_HEADS):
            for col, row, size in of_head[hd]:
                big_ref[:, col:col + size] = _mm_nt(hb, win_ref[row:row + size, :])
            lanes = slice(hd * HEAD_PAD, (hd + 1) * HEAD_PAD)
            qn, _ = _rms(q_all[:, lanes], qh_ref[...], QK_DIM)
            q_ref[hd] = _bf(_rope(qn, rope_v))
            kn, _ = _rms(k_all[:, lanes] + k_rope, kh_ref[...], QK_DIM)
            k_ref[hd] = _bf(_rope(kn, rope_v))
            vt_ref[hd] = _bf(vt_all[hd * V_DIM:(hd + 1) * V_DIM])

    tok = lambda c: pl.BlockSpec((tm, c), lambda i: (i, 0))
    head = lambda c: pl.BlockSpec((N_HEADS, tm, c), lambda i: (0, i, 0))
    return _pallas(
        body, "inproj_fwd", (t // tm,),
        [tok(d), _const((1, d)), _resident(w_in.shape), _resident(w_kr.shape), _const((1, Q_LORA)), _const((1, KV_LORA)),
         _const((1, HEAD_PAD)), _const((1, HEAD_PAD)), _resident(w_uq.shape), _resident(w_uk.shape),
         _resident(w_uv.shape), _resident(w_uvt.shape), tok(HEAD_PAD), tok(HEAD_PAD), tok(HEAD_PAD)],
        [tok(d), tok(BIG_COLS), tok(LAT_COLS), head(HEAD_PAD), head(HEAD_PAD), tok(N_HEADS * V_DIM),
         pl.BlockSpec((N_HEADS, V_DIM, tm), lambda i: (0, 0, i))],
        [jax.ShapeDtypeStruct((t, d), BF16), jax.ShapeDtypeStruct((t, BIG_COLS), F32),
         jax.ShapeDtypeStruct((t, LAT_COLS), F32), jax.ShapeDtypeStruct((N_HEADS, t, HEAD_PAD), BF16),
         jax.ShapeDtypeStruct((N_HEADS, t, HEAD_PAD), BF16), jax.ShapeDtypeStruct((t, N_HEADS * V_DIM), BF16),
         jax.ShapeDtypeStruct((N_HEADS, V_DIM, t), BF16)],
        (x1, gain, w_in, w_kr, qa_gain, kva_gain, qh_gain, kh_gain, w_uq, w_uk, w_uv, w_uvt, *rope), ("arbitrary",), carried)


EXP2_SCALE = ATTN_SCALE * 1.4426950408889634


def _diagonal_keep(tk, tq):
    return lax.broadcasted_iota(jnp.int32, (tk, tq), 0) <= lax.broadcasted_iota(jnp.int32, (tk, tq), 1)


ATTN_FWD_HEADS = 4


def _attn_fwd(q, k, vt, seq, carried=None):
    _, t, _ = q.shape
    nseq = t // seq
    tq = tk = ATTN_TILE
    nq = seq // tq
    hb = ATTN_FWD_HEADS

    def body(q_ref, k_ref, vt_ref, o_ref, lse_ref):
        i = pl.program_id(2)
        qs = [q_ref[h] for h in range(hb)]
        keep = _diagonal_keep(tk, tq)

        def scores(h, k0):
            return _mm_nt(k_ref[h, pl.ds(k0, tk), :], qs[h])

        def update(h, st, state, k0, diagonal):
            m, l, acc = state
            if diagonal:
                st = jnp.where(keep, st, MASK_VALUE)
            m_new = jnp.maximum(m, jnp.max(st, axis=0, keepdims=True))
            pt = jnp.exp2((st - m_new) * EXP2_SCALE)
            alpha = jnp.exp2((m - m_new) * EXP2_SCALE)
            l_new = alpha * l + jnp.sum(pt, axis=0, keepdims=True)
            return m_new, l_new, alpha * acc + _mm(vt_ref[h, :, pl.ds(k0, tk)], _bf(pt))

        def tiles(states, k0, diagonal):
            st, new = scores(0, k0), []
            for h in range(hb):
                st_next = scores(h + 1, k0) if h + 1 < hb else None
                new.append(update(h, st, states[h], k0, diagonal))
                st = st_next
            return tuple(new)

        init = tuple((jnp.full((1, tq), MASK_VALUE, F32), jnp.zeros((1, tq), F32), jnp.zeros((V_DIM, tq), F32))
                     for _ in range(hb))
        states = lax.fori_loop(0, i, lambda j, s: tiles(s, pl.multiple_of(j * tk, tk), False), init)
        states = tiles(states, pl.multiple_of(i * tk, tk), True)
        outs = []
        for h in range(hb):
            m, l, acc = states[h]
            outs.append((acc / l).T)
            lse_ref[h] = m * EXP2_SCALE + jnp.log2(l)
        o_ref[...] = _bf(jnp.concatenate(outs, axis=-1))

    return _pallas(
        body, "attn_fwd", (nseq, N_HEADS // hb, nq),
        [pl.BlockSpec((hb, tq, HEAD_PAD), lambda b, g, i: (g, b * nq + i, 0)),
         pl.BlockSpec((hb, seq, HEAD_PAD), lambda b, g, i: (g, b, 0)),
         pl.BlockSpec((hb, V_DIM, seq), lambda b, g, i: (g, 0, b))],
        [pl.BlockSpec((tq, hb * V_DIM), lambda b, g, i: (b * nq + i, g)),
         pl.BlockSpec((hb, 1, tq), lambda b, g, i: (g, 0, b * nq + i))],
        [jax.ShapeDtypeStruct((t, N_HEADS * V_DIM), BF16), jax.ShapeDtypeStruct((N_HEADS, 1, t), F32)],
        (q, k, vt), ("arbitrary", "arbitrary", "arbitrary"), carried)


ATTN_BWD_HEADS = 4


def _attn_bwd(q, k, v, do, lse, delta, seq, carried=None):
    _, t, _ = q.shape
    nseq = t // seq
    tq = tk = ATTN_TILE
    n = seq // tq
    hb = ATTN_BWD_HEADS

    def body(q_ref, k_ref, v_ref, do_ref, lse_ref, delta_ref, dq_ref, dk_ref, dv_ref):
        dq_ref[...] = jnp.zeros_like(dq_ref)
        dk_ref[...] = jnp.zeros_like(dk_ref)
        dv_ref[...] = jnp.zeros_like(dv_ref)
        keep = _diagonal_keep(tk, tq)

        def tile(h, k0, q0, diagonal):
            kj = k_ref[h, pl.ds(k0, tk), :]
            qi = q_ref[h, pl.ds(q0, tq), :]
            doi = _bf(do_ref[pl.ds(q0, tq), h * V_DIM:(h + 1) * V_DIM])
            st = _mm_nt(kj, qi)
            if diagonal:
                st = jnp.where(keep, st, MASK_VALUE)
            pt = jnp.exp2(st * EXP2_SCALE - lse_ref[h, :, pl.ds(q0, tq)])
            dv_ref[pl.ds(k0, tk), h * V_DIM:(h + 1) * V_DIM] += _mm(_bf(pt), doi)
            dpt = _mm_nt(v_ref[pl.ds(k0, tk), h * V_DIM:(h + 1) * V_DIM], doi)
            dst = _bf((pt * (dpt - delta_ref[pl.ds(h, 1), pl.ds(q0, tq)])) * ATTN_SCALE)
            dk_ref[h, pl.ds(k0, tk), :] += _mm(dst, qi)
            dq_ref[h, pl.ds(q0, tq), :] += _mm_tn(dst, kj)

        def kv_step(j, _):
            k0 = pl.multiple_of(j * tk, tk)
            for h in range(hb):
                tile(h, k0, k0, True)

            def q_step(i, _):
                q0 = pl.multiple_of(i * tq, tq)
                for h in range(hb):
                    tile(h, k0, q0, False)
                return 0

            lax.fori_loop(j + 1, n, q_step, 0)
            return 0

        lax.fori_loop(0, n, kv_step, 0)

    hspec = lambda c: pl.BlockSpec((hb, seq, c), lambda b, g: (g, b, 0))
    cols = pl.BlockSpec((seq, hb * V_DIM), lambda b, g: (b, g))
    return _pallas(
        body, "attn_bwd", (nseq, N_HEADS // hb),
        [hspec(HEAD_PAD), hspec(HEAD_PAD), cols, cols,
         pl.BlockSpec((hb, 1, seq), lambda b, g: (g, 0, b)), pl.BlockSpec((None, hb, seq), lambda b, g: (g, 0, b))],
        [hspec(HEAD_PAD), hspec(HEAD_PAD), cols],
        [jax.ShapeDtypeStruct((N_HEADS, t, HEAD_PAD), F32), jax.ShapeDtypeStruct((N_HEADS, t, HEAD_PAD), F32),
         jax.ShapeDtypeStruct((t, N_HEADS * V_DIM), F32)],
        (q, k, v, do, lse, delta), ("arbitrary", "arbitrary"), carried)


def _merged_mixers(o_ref, gb_ref, gla_ref, glb_ref, xc_ref, gc_ref, xcp_ref, gcp_ref, bias_ref, cw_ref, wpa_ref, wpc_ref,
                   first_of_seq):
    y_a = _mm(o_ref[...], wpa_ref[...])
    gb = gb_ref[...]
    u = gc_ref[...] * xc_ref[...]
    u_prev = jnp.where(first_of_seq, 0.0, gcp_ref[...] * xcp_ref[...])
    cw = cw_ref[...]
    z = cw[2:3] * u + cw[1:2] * _shift_down(u, u_prev, 1) + cw[0:1] * _shift_down(u, u_prev, 2)
    gbz = _bf(gb * z)
    y_b = _mm(gbz, wpc_ref[...])
    bias = bias_ref[...]
    gate_a = _sigmoid(gla_ref[...] + bias[:, :D_MODEL])
    gate_b = _sigmoid(glb_ref[...] + bias[:, D_MODEL:])
    return _bf(gate_a * y_a + gate_b * y_b)


def _mixer_specs(tm, seq):
    d = D_MODEL
    tok = pl.BlockSpec((tm, d), lambda i: (i, 0))
    col = lambda c: pl.BlockSpec((tm, d), lambda i: (i, c))
    prev = lambda c: pl.BlockSpec((8, d), lambda i: (jnp.maximum(i * (tm // 8) - 1, 0), c))
    o_spec = pl.BlockSpec((tm, N_HEADS * V_DIM), lambda i: (i, 0))
    fwd_specs = [o_spec, col(0), col(1), col(2), col(3), col(4), prev(3), prev(4), _const((1, 2 * d)), _const((3, d)),
                 _resident((N_HEADS * V_DIM, d)), _resident((d, d)), _resident((d, d))]
    return tok, fwd_specs


def _mix_fwd(x1, o, big, gate_bias, conv_w, w_pa, w_pc, w_out, seq, carried=None):
    t, d = x1.shape
    tm = WIDE_TILE
    tiles_per_seq = seq // tm

    def body(x_ref, o_ref, gb_ref, gla_ref, glb_ref, xc_ref, gc_ref, xcp_ref, gcp_ref, bias_ref, cw_ref, wpa_ref, wpc_ref,
             wout_ref, x2_ref):
        first = pl.program_id(0) % tiles_per_seq == 0
        merged = _merged_mixers(o_ref, gb_ref, gla_ref, glb_ref, xc_ref, gc_ref, xcp_ref, gcp_ref, bias_ref, cw_ref, wpa_ref,
                                wpc_ref, first)
        x2_ref[...] = x_ref[...] + _mm(merged, wout_ref[...])

    tok, fwd_specs = _mixer_specs(tm, seq)
    return _pallas(body, "mix_fwd", (t // tm,), [tok] + fwd_specs, [tok], [jax.ShapeDtypeStruct((t, d), F32)],
                   (x1, o, big, big, big, big, big, big, big, gate_bias, conv_w, w_pa, w_pc, w_out), ("arbitrary",), carried)[0]


def _mix_bwd(dx2, o, big, gate_bias, conv_w, w_pa, w_pc, w_out, seq, carried=None):
    t, d = dx2.shape
    tm = TOKEN_TILE
    tiles_per_seq = seq // tm
    hv = N_HEADS * V_DIM

    def body(dx_ref, o_ref, gb_ref, gla_ref, glb_ref, xc_ref, gc_ref, xcp_ref, gcp_ref, bias_ref, cw_ref, wpa_ref, wpc_ref,
             wout_ref, do_ref, delta_ref, dz_ref, dm_ref, dbias_ref, dwpa_ref, dwpc_ref, dwout_ref):
        @pl.when(pl.program_id(0) == 0)
        def _():
            dbias_ref[...] = jnp.zeros_like(dbias_ref)
            dwpa_ref[...] = jnp.zeros_like(dwpa_ref)
            dwpc_ref[...] = jnp.zeros_like(dwpc_ref)
            dwout_ref[...] = jnp.zeros_like(dwout_ref)

        first = pl.program_id(0) % tiles_per_seq == 0
        dxb = _bf(dx_ref[...])
        dmerged = _mm_nt(dxb, wout_ref[...])
        y_a = _mm(o_ref[...], wpa_ref[...])
        bias = bias_ref[...]
        gate_a = _sigmoid(gla_ref[...] + bias[:, :d])
        gate_b = _sigmoid(glb_ref[...] + bias[:, d:])
        dya = _bf(dmerged * gate_a)
        dyb = _bf(dmerged * gate_b)
        do_v = _mm_nt(dya, wpa_ref[...])
        dgz = _mm_nt(dyb, wpc_ref[...])
        dwpa_ref[...] += _mm_tn(o_ref[...], dya)
        gb = gb_ref[...]
        u = gc_ref[...] * xc_ref[...]
        u_prev = jnp.where(first, 0.0, gcp_ref[...] * xcp_ref[...])
        cw = cw_ref[...]
        z = cw[2:3] * u + cw[1:2] * _shift_down(u, u_prev, 1) + cw[0:1] * _shift_down(u, u_prev, 2)
        gbz = _bf(gb * z)
        y_b = _mm(gbz, wpc_ref[...])
        dwpc_ref[...] += _mm_tn(gbz, dyb)
        do_ref[...] = do_v
        head = lax.broadcasted_iota(jnp.int32, (N_HEADS, hv), 0) * V_DIM
        col = lax.broadcasted_iota(jnp.int32, (N_HEADS, hv), 1)
        in_head = ((col >= head) & (col < head + V_DIM)).astype(F32)
        delta_ref[...] = lax.dot_general(in_head, do_v * o_ref[...].astype(F32), (((1,), (1,)), ((), ())),
                                         precision=lax.Precision.HIGHEST, preferred_element_type=F32)
        dz_ref[...] = dgz * gb
        dm_ref[:, :d] = _bf(dgz * z)
        merged = _bf(gate_a * y_a + gate_b * y_b)
        dwout_ref[...] += _mm_tn(merged, dxb)
        dla = (dmerged * y_a) * (gate_a * (1.0 - gate_a))
        dlb = (dmerged * y_b) * (gate_b * (1.0 - gate_b))
        dbias_ref[:, :d] += jnp.sum(dla, axis=0, keepdims=True)
        dbias_ref[:, d:] += jnp.sum(dlb, axis=0, keepdims=True)
        dm_ref[:, d:2 * d] = _bf(dla)
        dm_ref[:, 2 * d:] = _bf(dlb)

    tok, fwd_specs = _mixer_specs(tm, seq)
    return _pallas(
        body, "mix_bwd", (t // tm,), [tok] + fwd_specs,
        [pl.BlockSpec((tm, hv), lambda i: (i, 0)), pl.BlockSpec((N_HEADS, tm), lambda i: (0, i)), tok,
         pl.BlockSpec((tm, M_COLS), lambda i: (i, 0)), _const((1, 2 * d)), _const((hv, d)), _const((d, d)), _const((d, d))],
        [jax.ShapeDtypeStruct((t, hv), F32), jax.ShapeDtypeStruct((N_HEADS, t), F32), jax.ShapeDtypeStruct((t, d), F32),
         jax.ShapeDtypeStruct((t, M_COLS), BF16), jax.ShapeDtypeStruct((1, 2 * d), F32), jax.ShapeDtypeStruct((hv, d), F32),
         jax.ShapeDtypeStruct((d, d), F32), jax.ShapeDtypeStruct((d, d), F32)],
        (dx2, o, big, big, big, big, big, big, big, gate_bias, conv_w, w_pa, w_pc, w_out), ("arbitrary",), carried)


def _prep_bwd(lat, big, dz, dq, dk, dv, qa_gain, kva_gain, qh_gain, kh_gain, w_uq, w_uk, w_uv, rope, conv_w, seq, carried=None):
    t = lat.shape[0]
    d = D_MODEL
    tm = WIDE_TILE
    tiles_per_seq = seq // tm
    last_blk = t // 8 - 1

    def body(lat_ref, xc_ref, gc_ref, dz_ref, dzn_ref, dq_ref, dk_ref, dv_ref, qa_ref, kva_ref, qh_ref, kh_ref, wuq_ref, wuk_ref,
             wuv_ref, cos_ref, slo_ref, shi_ref, cw_ref,
             dp_ref, dwuq_ref, dwuk_ref, dwuv_ref, dqa_ref, dkva_ref, dqh_ref, dkh_ref, dcw_ref):
        pid = pl.program_id(0)

        @pl.when(pid == 0)
        def _():
            for r in (dwuq_ref, dwuk_ref, dwuv_ref, dqa_ref, dkva_ref, dqh_ref, dkh_ref, dcw_ref):
                r[...] = jnp.zeros_like(r)

        lat_v = lat_ref[...]
        q_lat = lat_v[:, :Q_LORA]
        kv_lat = lat_v[:, Q_LORA:Q_LORA + KV_LORA]
        k_rope = lat_v[:, Q_LORA + KV_LORA:]
        qa_gain_v = qa_ref[...]
        kva_gain_v = kva_ref[...]
        qh_gain_v = qh_ref[...]
        kh_gain_v = kh_ref[...]
        cq, rq = _rms(q_lat, qa_gain_v)
        ckv, rkv = _rms(kv_lat, kva_gain_v)
        cqb = _bf(cq)
        ckvb = _bf(ckv)
        rope_v = (cos_ref[...], slo_ref[...], shi_ref[...])
        lane = lax.broadcasted_iota(jnp.int32, (tm, HEAD_PAD), 1)
        rope_lanes = (lane >= QK_NOPE) & (lane < QK_DIM)
        dk_rope = jnp.zeros((tm, HEAD_PAD), F32)
        dqh_gain = jnp.zeros((1, HEAD_PAD), F32)
        dkh_gain = jnp.zeros((1, HEAD_PAD), F32)
        q_all = _mm(cqb, wuq_ref[...])
        k_all = _mm(ckvb, wuk_ref[...])
        dvb = _bf(dv_ref[...])
        dckv = _mm_nt(dvb, wuv_ref[...])
        dwuv_ref[...] += _mm_tn(ckvb, dvb)

        last = pid % tiles_per_seq == tiles_per_seq - 1
        dzv = dz_ref[...]
        dz_next = jnp.where(last, 0.0, dzn_ref[...])
        dz1 = _shift_up(dzv, dz_next, 1)
        dz2 = _shift_up(dzv, dz_next, 2)
        cw = cw_ref[...]
        xc = xc_ref[...]
        gc = gc_ref[...]
        u = gc * xc
        du = cw[2:3] * dzv + cw[1:2] * dz1 + cw[0:1] * dz2
        dp_ref[:, :d] = _bf(du * gc)
        dp_ref[:, d:2 * d] = _bf(du * xc)
        dcw_ref[0:1, :] += jnp.sum(dz2 * u, axis=0, keepdims=True)
        dcw_ref[1:2, :] += jnp.sum(dz1 * u, axis=0, keepdims=True)
        dcw_ref[2:3, :] += jnp.sum(dzv * u, axis=0, keepdims=True)

        dcq = jnp.zeros((tm, Q_LORA), F32)
        half = N_HEADS // 2
        for part in range(2):
            dq_heads, dk_heads = [], []
            for hd in range(part * half, (part + 1) * half):
                lanes = slice(hd * HEAD_PAD, (hd + 1) * HEAD_PAD)
                q_pre = q_all[:, lanes]
                _, rr = _rms(q_pre, qh_gain_v, QK_DIM)
                dq_pre, dg = _rms_bwd(q_pre, rr, qh_gain_v, _rope_bwd(dq_ref[hd], rope_v), QK_DIM)
                dqh_gain = dqh_gain + dg
                dq_heads.append(_bf(dq_pre))

                k_pre = k_all[:, lanes] + k_rope
                _, rr = _rms(k_pre, kh_gain_v, QK_DIM)
                dk_pre, dg = _rms_bwd(k_pre, rr, kh_gain_v, _rope_bwd(dk_ref[hd], rope_v), QK_DIM)
                dkh_gain = dkh_gain + dg
                dk_rope = dk_rope + jnp.where(rope_lanes, dk_pre, 0.0)
                dk_heads.append(_bf(dk_pre))
            dq_part = jnp.concatenate(dq_heads, axis=1)
            dk_part = jnp.concatenate(dk_heads, axis=1)
            cols = slice(part * half * HEAD_PAD, (part + 1) * half * HEAD_PAD)
            dcq = dcq + _mm_nt(dq_part, wuq_ref[:, cols])
            dckv = dckv + _mm_nt(dk_part, wuk_ref[:, cols])
            dwuq_ref[:, cols] += _mm_tn(cqb, dq_part)
            dwuk_ref[:, cols] += _mm_tn(ckvb, dk_part)
        dqh_ref[...] += dqh_gain
        dkh_ref[...] += dkh_gain
        dq_lat, dg = _rms_bwd(q_lat, rq, qa_gain_v, dcq)
        dqa_ref[...] += dg
        dkv_lat, dg = _rms_bwd(kv_lat, rkv, kva_gain_v, dckv)
        dkva_ref[...] += dg
        dp_ref[:, 2 * d:2 * d + Q_LORA] = _bf(dq_lat)
        dp_ref[:, 2 * d + Q_LORA:2 * d + Q_LORA + KV_LORA] = _bf(dkv_lat)
        dp_ref[:, 2 * d + Q_LORA + KV_LORA:] = _bf(dk_rope)

    tok = lambda c: pl.BlockSpec((tm, c), lambda i: (i, 0))
    col = lambda c: pl.BlockSpec((tm, d), lambda i: (i, c))
    head = lambda c: pl.BlockSpec((N_HEADS, tm, c), lambda i: (0, i, 0))
    nxt = pl.BlockSpec((8, d), lambda i: (jnp.minimum((i + 1) * (tm // 8), last_blk), 0))
    return _pallas(
        body, "prep_bwd", (t // tm,),
        [tok(LAT_COLS), col(3), col(4), tok(d), nxt, head(HEAD_PAD), head(HEAD_PAD), tok(N_HEADS * V_DIM),
         _const((1, Q_LORA)), _const((1, KV_LORA)), _const((1, HEAD_PAD)), _const((1, HEAD_PAD)),
         _resident(w_uq.shape), _resident(w_uk.shape), _resident(w_uv.shape), tok(HEAD_PAD), tok(HEAD_PAD), tok(HEAD_PAD),
         _const((3, d))],
        [tok(P_COLS), _const(w_uq.shape), _const(w_uk.shape), _const(w_uv.shape), _const((1, Q_LORA)),
         _const((1, KV_LORA)), _const((1, HEAD_PAD)), _const((1, HEAD_PAD)), _const((3, d))],
        [jax.ShapeDtypeStruct((t, P_COLS), BF16), jax.ShapeDtypeStruct(w_uq.shape, F32),
         jax.ShapeDtypeStruct(w_uk.shape, F32), jax.ShapeDtypeStruct(w_uv.shape, F32),
         jax.ShapeDtypeStruct((1, Q_LORA), F32), jax.ShapeDtypeStruct((1, KV_LORA), F32),
         jax.ShapeDtypeStruct((1, HEAD_PAD), F32), jax.ShapeDtypeStruct((1, HEAD_PAD), F32), jax.ShapeDtypeStruct((3, d), F32)],
        (lat, big, big, dz, dz, dq, dk, dv, qa_gain, kva_gain, qh_gain, kh_gain, w_uq, w_uk, w_uv, *rope, conv_w),
        ("arbitrary",), carried)


def _inproj_bwd(x1, gain, dx2, dm, dp, w_in, w_kr, carried=None):
    t, d = x1.shape
    tm = TOKEN_TILE

    def body(x_ref, g_ref, dx2_ref, dm_ref, dp_ref, win_ref, wkr_ref, dx1_ref, dgain_ref):
        xv = x_ref[...]
        gain_v = g_ref[...]
        _, r = _rms(xv, gain_v)
        dh = (_mm(dm_ref[:, :d], win_ref[ROW_GB:ROW_GC, :]) + _mm(dm_ref[:, d:], win_ref[ROW_GL:, :])
              + _mm(dp_ref[:, :d], win_ref[ROW_XC:ROW_GB, :]) + _mm(dp_ref[:, d:2 * d], win_ref[ROW_GC:ROW_GL, :])
              + _mm(dp_ref[:, 2 * d:2 * d + ROW_KR], win_ref[ROW_QKV:ROW_KR, :]) + _mm(dp_ref[:, 2 * d + ROW_KR:], wkr_ref[...]))
        dxn, dgain = _rms_bwd(xv, r, gain_v, dh)
        dx1_ref[...] = dx2_ref[...] + dxn

        @pl.when(pl.program_id(0) == 0)
        def _():
            dgain_ref[...] = jnp.zeros_like(dgain_ref)

        dgain_ref[...] += dgain

    tok = lambda c: pl.BlockSpec((tm, c), lambda i: (i, 0))
    return _pallas(
        body, "inproj_bwd", (t // tm,),
        [tok(d), _const((1, d)), tok(d), tok(M_COLS), tok(P_COLS), _resident(w_in.shape), _resident(w_kr.shape)],
        [tok(d), _const((1, d))], [jax.ShapeDtypeStruct((t, d), F32), jax.ShapeDtypeStruct((1, d), F32)],
        (x1, gain, dx2, dm, dp, w_in, w_kr), ("arbitrary",), carried)


def _adamw(quads, name, carried=None):
    k = len(quads)
    rows, cols = quads[0][0].shape
    tr, tc = rows, cols
    for cand in (512, 352, 256, 192, 128, 64):
        if rows % cand == 0 and rows > cand:
            tr = cand
            break
    if tr == rows and rows * cols > 512 * 1024 and cols % 256 == 0:
        tc = 256
    while k * 14 * tr * tc * 4 > (VMEM_LIMIT * 3) // 4 and tr % 16 == 0:
        tr //= 2

    def body(*refs):
        for i in range(k):
            w_ref, g_ref, m_ref, v_ref = refs[4 * i:4 * i + 4]
            delta_ref, nm_ref, nv_ref = refs[4 * k + 3 * i:4 * k + 3 * i + 3]
            delta_ref[...], nm_ref[...], nv_ref[...] = _adamw_update(w_ref[...], g_ref[...], m_ref[...], v_ref[...])

    spec = pl.BlockSpec((tr, tc), lambda i, j: (i, j))
    shape = jax.ShapeDtypeStruct((rows, cols), F32)
    outs = _pallas(body, name, (rows // tr, cols // tc), [spec] * (4 * k), [spec] * (3 * k), [shape] * (3 * k),
                   [a for quad in quads for a in quad], ("arbitrary", "arbitrary"), carried)
    return [tuple(outs[3 * i:3 * i + 3]) for i in range(k)]


def _adamw_update(w, g, m, v):
    nm = ADAM_B1 * m + (1.0 - ADAM_B1) * g
    nv = ADAM_B2 * v + (1.0 - ADAM_B2) * (g * g)
    m_hat = nm * (1.0 / (1.0 - ADAM_B1 ** ADAM_STEP))
    v_hat = nv * (1.0 / (1.0 - ADAM_B2 ** ADAM_STEP))
    return -ADAM_LR * (m_hat / (jnp.sqrt(v_hat) + ADAM_EPS) + ADAM_WD * w), nm, nv


def _adamw_whole(quads, name):
    k = len(quads)

    def body(*refs):
        for i in range(k):
            w_ref, g_ref, m_ref, v_ref = refs[4 * i:4 * i + 4]
            g_out, delta_ref, nm_ref, nv_ref = refs[4 * k + 4 * i:4 * k + 4 * i + 4]
            gv = g_ref[...]
            g_out[...] = gv
            delta_ref[...], nm_ref[...], nv_ref[...] = _adamw_update(w_ref[...], gv, m_ref[...], v_ref[...])

    vm = pl.BlockSpec(memory_space=pltpu.VMEM)
    outs = pl.pallas_call(body, name=name, in_specs=[vm] * (4 * k), out_specs=[vm] * (4 * k),
                          out_shape=[jax.ShapeDtypeStruct(q[0].shape, F32) for q in quads for _ in range(4)],
                          compiler_params=_params())(*[a for quad in quads for a in quad])
    return [tuple(outs[4 * i:4 * i + 4]) for i in range(k)]


def _adamw_small(packed_grads, triples, segments):
    k = len(triples)

    def body(*refs):
        g_ref = refs[0]
        off = 0
        for i in range(k):
            w_ref, m_ref, v_ref = refs[1 + 3 * i:4 + 3 * i]
            g_out, delta_ref, nm_ref, nv_ref = refs[1 + 3 * k + 4 * i:5 + 3 * k + 4 * i]
            gv = g_ref[:, off:off + w_ref.shape[1]]
            g_out[...] = gv
            delta_ref[...], nm_ref[...], nv_ref[...] = _adamw_update(w_ref[...], gv, m_ref[...], v_ref[...])
            off += segments[i]

    vm = pl.BlockSpec(memory_space=pltpu.VMEM)
    outs = pl.pallas_call(
        body, name="adamw_small", in_specs=[vm] * (1 + 3 * k), out_specs=[vm] * (4 * k),
        out_shape=[jax.ShapeDtypeStruct(w.shape, F32) for w, _, _ in triples for _ in range(4)],
    )(packed_grads, *[a for triple in triples for a in triple])
    return [tuple(outs[4 * i:4 * i + 4]) for i in range(k)]


def _place():
    x, y, c = lax.axis_index("x"), lax.axis_index("y"), lax.axis_index("c")
    other_chips = [(1 - x, y), (x, 1 - y), (1 - x, 1 - y)]
    return x, y, c, other_chips


def _remote(src, dst, sems, send, recv, device):
    return pltpu.make_async_remote_copy(src_ref=src, dst_ref=dst, send_sem=sems.at[send], recv_sem=sems.at[recv],
                                        device_id=device, device_id_type=MESH_ID)


def _cast_shards(shards, out_dtypes, n_first):
    n = len(shards)
    out_shape = [jax.ShapeDtypeStruct((N_CHIPS,) + s.shape, dt) for s, dt in zip(shards, out_dtypes)]
    gather = _gather_carried(out_shape[:n_first])

    def body(*refs):
        ins, outs, stage, sems = refs[:n], refs[n:2 * n], refs[2 * n:3 * n], refs[3 * n]
        x, y, _, _ = _place()
        me = 2 * x + y

        def cast(first, last):
            copies = []
            for w in range(first, last):
                stage[w][...] = ins[w][...].astype(out_dtypes[w])
                copies.append(pltpu.make_async_copy(stage[w], outs[w].at[me], sems.at[w]))
                copies[-1].start()
            for cp in copies:
                cp.wait()

        cast(0, n_first)
        gather.start(None, outs[:n_first], sems, n)
        cast(n_first, n)
        gather.finish(None, outs[:n_first], sems, n)

    vm = pl.BlockSpec(memory_space=pltpu.VMEM)
    return pl.pallas_call(
        body, name="cast_shards", in_specs=[vm] * n, out_specs=[ANY] * n, out_shape=out_shape,
        scratch_shapes=[pltpu.VMEM(s.shape, dt) for s, dt in zip(shards, out_dtypes)] + [pltpu.SemaphoreType.DMA((n + gather.n_sems,))],
        compiler_params=_params())(*shards)


BF16_ROWS = 16


def _split_rows(rows):
    return (rows // 2) % BF16_ROWS == 0


def _half_shape(rows, cols):
    return (rows // 2, cols) if _split_rows(rows) else (rows, cols // 2)


def _half(rows, cols, which):
    if _split_rows(rows):
        return (pl.ds(pl.multiple_of(which * (rows // 2), BF16_ROWS), rows // 2), slice(None))
    return (slice(None), pl.ds(pl.multiple_of(which * (cols // 2), 128), cols // 2))


def _gather_carried(bufs):
    n = len(bufs)

    def half(w, slot, which):
        _, rows, cols = bufs[w].shape
        return (slot,) + _half(rows, cols, which)

    def start(ins, outs, sems, base):
        x, y, c, other_chips = _place()
        me = 2 * x + y
        for w in range(n):
            mine = outs[w].at[half(w, me, c)]
            for p, (px, py) in enumerate(other_chips):
                _remote(mine, mine, sems, base + 12 * w + p, base + 12 * w + 3 + p, (px, py, c)).start()

    def finish(ins, outs, sems, base):
        x, y, c, other_chips = _place()
        me = 2 * x + y
        for w in range(n):
            for p, (px, py) in enumerate(other_chips):
                got = outs[w].at[half(w, 2 * px + py, c)]
                _remote(got, got, sems, base + 12 * w + p, base + 12 * w + 3 + p, (px, py, c)).wait_recv()
                _remote(got, got, sems, base + 12 * w + 6 + p, base + 12 * w + 9 + p, (x, y, 1 - c)).start()
        for w in range(n):
            mine = outs[w].at[half(w, me, c)]
            for p, (px, py) in enumerate(other_chips):
                got = outs[w].at[half(w, 2 * px + py, c)]
                theirs = outs[w].at[half(w, 2 * px + py, 1 - c)]
                _remote(got, theirs, sems, base + 12 * w + 6 + p, base + 12 * w + 9 + p, (x, y, 1 - c)).wait()
                _remote(mine, mine, sems, base + 12 * w + p, base + 12 * w + 3 + p, (px, py, c)).wait_send()

    shapes = [jax.ShapeDtypeStruct(b.shape, b.dtype) for b in bufs]
    return _Carried(bufs, shapes, {w: w for w in range(n)}, 12 * n, start, finish)


def _swap_carried(grads):
    n = len(grads)

    def copy(w, ins, outs, sems, base):
        x, y, c, _ = _place()
        _, rows, cols = grads[w].shape
        theirs = ins[w].at[(slice(None),) + _half(rows, cols, 1 - c)]
        return _remote(theirs, outs[w], sems, base + 2 * w, base + 2 * w + 1, (x, y, 1 - c))

    def start(ins, outs, sems, base):
        for w in range(n):
            copy(w, ins, outs, sems, base).start()

    def finish(ins, outs, sems, base):
        for w in range(n):
            copy(w, ins, outs, sems, base).wait()

    shapes = [jax.ShapeDtypeStruct((g.shape[0],) + _half_shape(*g.shape[1:]), F32) for g in grads]
    return _Carried(grads, shapes, {}, 2 * n, start, finish)


def _row_tile(rows):
    for cand in (512, 352, 256, 192, 128, 96, 64, 32, 16):
        if rows % cand == 0:
            return cand
    return rows


def _half_block_index(split_rows, tiles, i, core):
    return (core * tiles + i, 0) if split_rows else (i, core)


def _chip_partial(grad, other, place, name):
    nblk, hr, hc = other.shape
    by_rows = _split_rows(grad.shape[1])
    tr = _row_tile(hr)
    tiles = hr // tr

    def body(place_ref, g_ref, o_ref, own_ref, sum_bf_ref):
        s = g_ref[...] + o_ref[...]
        sum_bf_ref[...] = _bf(s)

        @pl.when(pl.program_id(1) == place_ref[0])
        def _():
            own_ref[...] = s

    grid_spec = pltpu.PrefetchScalarGridSpec(
        num_scalar_prefetch=1, grid=(tiles, nblk),
        in_specs=[pl.BlockSpec((None, tr, hc), lambda i, b, place_ref: (b,) + _half_block_index(by_rows, tiles, i, place_ref[1])),
                  pl.BlockSpec((None, tr, hc), lambda i, b, place_ref: (b, i, 0))],
        out_specs=[pl.BlockSpec((tr, hc), lambda i, b, place_ref: (i, 0)),
                   pl.BlockSpec((None, tr, hc), lambda i, b, place_ref: (b, i, 0))])
    return pl.pallas_call(body, name=name, grid_spec=grid_spec,
                          out_shape=[jax.ShapeDtypeStruct((hr, hc), F32), jax.ShapeDtypeStruct(other.shape, BF16)],
                          compiler_params=_params(("arbitrary", "arbitrary")))(place, grad, other)


def _chip_partial_small(grads, others, place):
    n = len(grads)

    def body(*refs):
        place_ref, g_refs, o_refs = refs[0], refs[1:1 + n], refs[1 + n:1 + 2 * n]
        own_refs, bf_refs = refs[1 + 2 * n:1 + 3 * n], refs[1 + 3 * n:]
        chip, core = place_ref[0], place_ref[1]
        for w in range(n):
            _, rows, cols = grads[w].shape
            half = _half(rows, cols, core)
            bf_refs[w][...] = _bf(g_refs[w][(slice(None),) + half] + o_refs[w][...])
            own_refs[w][...] = g_refs[w][(chip,) + half] + o_refs[w][chip]

    vm = pl.BlockSpec(memory_space=pltpu.VMEM)
    outs = pl.pallas_call(
        body, name="chip_partial_small", in_specs=[pl.BlockSpec(memory_space=pltpu.SMEM)] + [vm] * (2 * n), out_specs=[vm] * (2 * n),
        out_shape=[jax.ShapeDtypeStruct(o.shape[1:], F32) for o in others] + [jax.ShapeDtypeStruct(o.shape, BF16) for o in others],
        compiler_params=_params())(place, *grads, *others)
    return list(zip(outs[:n], outs[n:]))


def _chip_total_small(owns, receiveds, place, shapes):
    n = len(owns)

    def body(*refs):
        place_ref, own_refs, r_refs, out_refs = refs[0], refs[1:1 + n], refs[1 + n:1 + 2 * n], refs[1 + 2 * n:]
        chip, core = place_ref[0], place_ref[1]
        for w in range(n):
            r = [r_refs[w][(chip + k) % N_CHIPS].astype(F32) for k in (1, 2, 3)]
            out_refs[w][_half(*shapes[w], core)] = own_refs[w][...] + ((r[0] + r[1]) + r[2])

    vm = pl.BlockSpec(memory_space=pltpu.VMEM)
    return list(pl.pallas_call(
        body, name="chip_total_small", in_specs=[pl.BlockSpec(memory_space=pltpu.SMEM)] + [vm] * (2 * n), out_specs=[vm] * n,
        out_shape=[jax.ShapeDtypeStruct(tuple(s), F32) for s in shapes], compiler_params=_params())(place, *owns, *receiveds))


def _send_carried(partials):
    n = len(partials)

    def start(ins, outs, sems, base):
        x, y, c, other_chips = _place()
        me = 2 * x + y
        for w in range(n):
            for p, (px, py) in enumerate(other_chips):
                _remote(ins[w].at[2 * px + py], outs[w].at[me], sems, base + 6 * w + p, base + 6 * w + 3 + p, (px, py, c)).start()

    def finish(ins, outs, sems, base):
        x, y, c, other_chips = _place()
        for w in range(n):
            for p, (px, py) in enumerate(other_chips):
                _remote(ins[w].at[2 * px + py], outs[w].at[2 * px + py], sems, base + 6 * w + p, base + 6 * w + 3 + p,
                        (px, py, c)).wait()

    return _Carried(partials, [jax.ShapeDtypeStruct(p.shape, BF16) for p in partials], {}, 6 * n, start, finish)


def _chip_total(own, received, place, shape, name):
    hr, hc = own.shape
    by_rows = _split_rows(shape[0])
    tr = _row_tile(hr)
    tiles = hr // tr

    def body(place_ref, own_ref, r1_ref, r2_ref, r3_ref, out_ref):
        out_ref[...] = own_ref[...] + ((r1_ref[...].astype(F32) + r2_ref[...].astype(F32)) + r3_ref[...].astype(F32))

    def slot(k):
        return pl.BlockSpec((None, tr, hc), lambda i, place_ref: ((place_ref[0] + k) % N_CHIPS, i, 0))

    grid_spec = pltpu.PrefetchScalarGridSpec(
        num_scalar_prefetch=1, grid=(tiles,), in_specs=[pl.BlockSpec((tr, hc), lambda i, place_ref: (i, 0)), slot(1), slot(2), slot(3)],
        out_specs=pl.BlockSpec((tr, hc), lambda i, place_ref: _half_block_index(by_rows, tiles, i, place_ref[1])))
    return pl.pallas_call(body, name=name, grid_spec=grid_spec, out_shape=jax.ShapeDtypeStruct(tuple(shape), F32),
                          compiler_params=_params(("arbitrary",)))(place, own, received, received, received)


def _join_carried(totals):
    n = len(totals)

    def copy(w, outs, sems, base):
        x, y, c, _ = _place()
        mine = outs[w].at[_half(*totals[w].shape, c)]
        return _remote(mine, mine, sems, base + 2 * w, base + 2 * w + 1, (x, y, 1 - c))

    def start(ins, outs, sems, base):
        for w in range(n):
            copy(w, outs, sems, base).start()

    def finish(ins, outs, sems, base):
        for w in range(n):
            copy(w, outs, sems, base).wait()

    shapes = [jax.ShapeDtypeStruct(a.shape, F32) for a in totals]
    return _Carried(totals, shapes, {w: w for w in range(n)}, 2 * n, start, finish)


def _sum_devices(vec):
    rows, n = vec.shape

    def body(v_ref, out_ref, buf, send_sems, recv_sems):
        x, y, c, _ = _place()
        me = 4 * x + 2 * y + c
        buf[me] = v_ref[...]
        sends = []
        for k in range(1, N_DEV):
            peer = (1 - x if k & 4 else x, 1 - y if k & 2 else y, 1 - c if k & 1 else c)
            cp = pltpu.make_async_remote_copy(src_ref=v_ref, dst_ref=buf.at[me], send_sem=send_sems.at[k], recv_sem=recv_sems.at[k],
                                              device_id=peer, device_id_type=MESH_ID)
            cp.start()
            sends.append(cp)
        for cp in sends:
            cp.wait()
        total = buf[0]
        for dev in range(1, N_DEV):
            total = total + buf[dev]
        out_ref[...] = total

    vm = pl.BlockSpec(memory_space=pltpu.VMEM)
    return pl.pallas_call(
        body, name="sum_devices", in_specs=[vm], out_specs=vm, out_shape=jax.ShapeDtypeStruct((rows, n), F32),
        scratch_shapes=[pltpu.VMEM((N_DEV, rows, n), F32), pltpu.SemaphoreType.DMA((N_DEV,)), pltpu.SemaphoreType.DMA((N_DEV,))],
    )(vec)


def _rope_tables(positions):
    half = ROPE_HALF
    inv_freq = 1.0 / (ROPE_THETA ** (jnp.arange(half, dtype=F32) / half))
    ang = positions.astype(F32).reshape(-1, 1) * inv_freq
    cos, sin = jnp.cos(ang), jnp.sin(ang)
    t = ang.shape[0]
    ones, zeros = jnp.ones((t, QK_NOPE), F32), jnp.zeros((t, QK_NOPE), F32)
    pad, none = HEAD_PAD - QK_DIM, zeros[:, :half]
    cos_full = jnp.concatenate([ones, cos, cos, ones[:, :pad]], axis=1)
    s_lo = jnp.concatenate([zeros, -sin, none, zeros[:, :pad]], axis=1)
    s_hi = jnp.concatenate([zeros, none, sin, zeros[:, :pad]], axis=1)
    return cos_full, s_lo, s_hi


def _partials(names, grads, from_sibling, place):
    return [_chip_partial(g, o, place, "chip_partial_" + n) for n, g, o in zip(names, grads, from_sibling)]


def _totals(names, grads, partials, received, place):
    return [_chip_total(pf, r, place, g.shape[1:], "chip_total_" + n) for n, g, (pf, _), r in zip(names, grads, partials, received)]


def _kernel_layouts(full):
    w_in = full["w_in"]
    w_kr = jnp.pad(w_in[ROW_KR:ROW_XC], ((QK_NOPE, HEAD_PAD - QK_DIM), (0, 0)))
    w_uq = jnp.pad(full["w_uq"].reshape(Q_LORA, N_HEADS, QK_DIM), ((0, 0), (0, 0), (0, HEAD_PAD - QK_DIM)))
    w_uk = jnp.pad(full["w_uk"].reshape(KV_LORA, N_HEADS, QK_NOPE), ((0, 0), (0, 0), (0, HEAD_PAD - QK_NOPE)))
    return {"w_in": w_in, "w_kr": w_kr, "w_uq": w_uq.reshape(Q_LORA, N_HEADS * HEAD_PAD),
            "w_uk": w_uk.reshape(KV_LORA, N_HEADS * HEAD_PAD), "w_uv": full["w_uv"], "w_uvt": full["w_uv"].T}


def _global_layouts(g):
    w_uq = g["w_uq"].reshape(Q_LORA, N_HEADS, HEAD_PAD)[:, :, :QK_DIM].reshape(Q_LORA, N_HEADS * QK_DIM)
    w_uk = g["w_uk"].reshape(KV_LORA, N_HEADS, HEAD_PAD)[:, :, :QK_NOPE].reshape(KV_LORA, N_HEADS * QK_NOPE)
    return {"w_in": g["w_in"], "w_uq": w_uq, "w_uk": w_uk, "w_uv": g["w_uv"], "w_proj_attn": g["w_pa"], "w_proj_conv": g["w_pc"],
            "w_out": g["w_out"]}


def _dw_in(dm, dp, h2b):
    d, rows = D_MODEL, ROW_GL + 2 * D_MODEL
    wm, wp = M_COLS // 2, P_COLS // 2
    from_dm = [(0, [(ROW_GB, 0, d), (ROW_GL, d, wm - d)]), (1, [(ROW_GL + wm - d, 0, wm)])]
    from_dp = [(0, [(ROW_XC, 0, d), (ROW_GC, d, wp - d)]),
               (1, [(ROW_GC + wp - d, 0, 2 * d - wp), (ROW_QKV, 2 * d - wp, ROW_KR), (ROW_KR, 2 * d - wp + ROW_KR + QK_NOPE, QK_ROPE)])]
    out = _tn_rows(dm, h2b, None, from_dm, wm, rows, "dw_in_m")
    return _tn_rows(dp, h2b, out, from_dp, wp, rows, "dw_in_p")


def _col_blocks(a):
    r, c = a.shape
    return a.reshape(r, N_CHIPS, c // N_CHIPS).transpose(1, 0, 2)


def _from_col_blocks(a):
    n, r, c = a.shape
    return a.transpose(1, 0, 2).reshape(r, n * c)


COL_SHARDED = ("w_uq", "w_uk", "w_uv", "w_proj_attn")
TRANSPOSED = ("ffn1_w_gate", "ffn1_w_up", "ffn2_w_gate", "ffn2_w_up", "w_in")
SMALL = (("ffn1_norm", 1024), ("mix_norm", 1024), ("gate_bias", 2048), ("q_a_norm", 384), ("kv_a_norm", 256),
         ("q_head_norm", 128), ("k_head_norm", 128), ("ffn2_norm", 1024))
WEIGHT_ORDER = ("ffn1_norm", "ffn1_w_gate", "ffn1_w_up", "ffn1_w_down", "mix_norm", "w_in", "gate_bias", "q_a_norm", "w_uq",
                "kv_a_norm", "w_uk", "w_uv", "q_head_norm", "k_head_norm", "w_proj_attn", "conv_w", "w_proj_conv", "w_out",
                "ffn2_norm", "ffn2_w_gate", "ffn2_w_up", "ffn2_w_down")
MATRICES = ("ffn1_w_gate", "ffn1_w_up", "ffn1_w_down", "w_in", "w_uq", "w_uk", "w_uv", "w_proj_attn", "w_proj_conv", "w_out",
            "ffn2_w_gate", "ffn2_w_up", "ffn2_w_down")
GROUP_FFN1 = ("ffn1_w_gate", "ffn1_w_up", "ffn1_w_down")
GROUP_IN = ("w_in", "w_uq", "w_uk", "w_uv", "conv_w")
GROUP_MIX = ("w_proj_attn", "w_proj_conv", "w_out")
GROUP_FFN2 = ("ffn2_w_gate", "ffn2_w_up", "ffn2_w_down")
GROUP_MID = ("w_in", "w_uq", "w_uk", "w_uv", "w_proj_attn", "w_proj_conv", "w_out")


def _pad_lanes(a, n):
    return jnp.pad(a.reshape(1, -1), ((0, 0), (0, n - a.size)))


def kernel(x, positions, ffn1_norm, ffn1_w_gate, ffn1_w_up, ffn1_w_down, mix_norm, w_in, gate_bias, q_a_norm, w_uq, kv_a_norm, w_uk, w_uv, q_head_norm, k_head_norm, w_proj_attn, conv_w, w_proj_conv, w_out, ffn2_norm, ffn2_w_gate, ffn2_w_up, ffn2_w_down, loss_target, m_ffn1_norm, m_ffn1_w_gate, m_ffn1_w_up, m_ffn1_w_down, m_mix_norm, m_w_in, m_gate_bias, m_q_a_norm, m_w_uq, m_kv_a_norm, m_w_uk, m_w_uv, m_q_head_norm, m_k_head_norm, m_w_proj_attn, m_conv_w, m_w_proj_conv, m_w_out, m_ffn2_norm, m_ffn2_w_gate, m_ffn2_w_up, m_ffn2_w_down, v_ffn1_norm, v_ffn1_w_gate, v_ffn1_w_up, v_ffn1_w_down, v_mix_norm, v_w_in, v_gate_bias, v_q_a_norm, v_w_uq, v_kv_a_norm, v_w_uk, v_w_uv, v_q_head_norm, v_k_head_norm, v_w_proj_attn, v_conv_w, v_w_proj_conv, v_w_out, v_ffn2_norm, v_ffn2_w_gate, v_ffn2_w_up, v_ffn2_w_down):
    args = dict(locals())
    view = lambda n, a: a.T if n in TRANSPOSED else a
    weights = {n: view(n, args[n]) for n in WEIGHT_ORDER}
    moments_m = {n: view(n, args["m_" + n]) for n in WEIGHT_ORDER}
    moments_v = {n: view(n, args["v_" + n]) for n in WEIGHT_ORDER}
    nb, seq, d = x.shape
    t = nb * seq
    chip = (2 * lax.axis_index("x") + lax.axis_index("y")).astype(jnp.int32)
    place = jnp.stack([chip, lax.axis_index("c").astype(jnp.int32)])
    grads, delta, new_m, new_v = {}, {}, {}, {}

    def adamw(names, carried=None):
        results = _adamw([(weights[n], grads[n], moments_m[n], moments_v[n]) for n in names], "adamw_" + names[0], carried)
        for n, (dn, mn, vn) in zip(names, results):
            delta[n], new_m[n], new_v[n] = dn, mn, vn

    conv_rows = conv_w.shape[0]
    conv_shard = jnp.pad(conv_w, ((0, 16 - conv_rows), (0, 0)))
    assert MATRICES[:len(GROUP_FFN1)] == GROUP_FFN1
    bufs = dict(zip(MATRICES + ("conv_w",), _cast_shards([weights[n] for n in MATRICES] + [conv_shard],
                                                         [BF16] * len(MATRICES) + [F32], len(GROUP_FFN1))))
    blocks = {n: bufs[n] for n in GROUP_FFN1}
    p = {n: _pad_lanes(weights[n], size) for n, size in SMALL}
    rope = _rope_tables(positions)
    x_tok = x.reshape(t, d)

    gather_in = _gather_carried([bufs[n] for n in GROUP_IN])
    x1, gate1, up1, act1 = _ffn_fwd(x_tok, p["ffn1_norm"], blocks["ffn1_w_gate"], blocks["ffn1_w_up"], blocks["ffn1_w_down"], None,
                                    "ffn1_fwd", gather_in)
    blocks.update(zip(GROUP_IN, gather_in.results))
    w = _kernel_layouts({"w_in": blocks["w_in"].reshape(-1, d), **{n: _from_col_blocks(blocks[n]) for n in ("w_uq", "w_uk", "w_uv")}})
    p["conv_w"] = _from_col_blocks(blocks["conv_w"])[:conv_rows]

    gather_mix = _gather_carried([bufs[n] for n in GROUP_MIX + GROUP_FFN2[2:]])
    h2b, big, lat, q, k, v, vt = _inproj_fwd(x1, p["mix_norm"], w["w_in"], w["w_kr"], p["q_a_norm"], p["kv_a_norm"], p["q_head_norm"],
                                             p["k_head_norm"], w["w_uq"], w["w_uk"], w["w_uv"], w["w_uvt"], rope, gather_mix)
    blocks.update(zip(GROUP_MIX + GROUP_FFN2[2:], gather_mix.results))
    w_pa = _from_col_blocks(blocks["w_proj_attn"])
    w_pc, w_out_full = blocks["w_proj_conv"].reshape(-1, d), blocks["w_out"].reshape(-1, d)

    gather_ffn2 = _gather_carried([bufs[n] for n in GROUP_FFN2[:2]])
    o, lse = _attn_fwd(q, k, vt, seq, gather_ffn2)
    x2 = _mix_fwd(x1, o, big, p["gate_bias"], p["conv_w"], w_pa, w_pc, w_out_full, seq)
    wg2, wu2 = gather_ffn2.results
    wd2 = blocks["ffn2_w_down"]
    dx3, gate2, up2, act2, loss = _ffn_fwd(x2, p["ffn2_norm"], wg2, wu2, wd2, loss_target.reshape(t, d), "ffn2_fwd")

    dx2, dg_ffn2, hb2, dgate2, dup2, dyb2 = _ffn_bwd_x(x2, p["ffn2_norm"], dx3, gate2, up2, wg2, wu2, wd2, "ffn2_bwd")
    g_ffn2 = [_tn_matmul(dgate2, hb2, "ffn2_dw_gate"), _tn_matmul(dup2, hb2, "ffn2_dw_up"), _tn_matmul(act2, dyb2, "ffn2_dw_down")]
    swap = _swap_carried(g_ffn2)
    do, delta_o, dz, dm, dbias, dw_pa, dw_pc, dw_out = _mix_bwd(dx2, o, big, p["gate_bias"], p["conv_w"], w_pa, w_pc, w_out_full, seq,
                                                                swap)
    part = _partials(GROUP_FFN2, g_ffn2, swap.results, place)
    send = _send_carried([pb for _, pb in part])
    dq, dk, dv = _attn_bwd(q, k, v, do, lse, delta_o.reshape(N_HEADS // ATTN_BWD_HEADS, ATTN_BWD_HEADS, -1), seq, send)
    join = _join_carried(_totals(GROUP_FFN2, g_ffn2, part, send.results, place))
    dp, dw_uq, dw_uk, dw_uv, dqa, dkva, dqh, dkh, dcw = _prep_bwd(
        lat, big, dz, dq, dk, dv, p["q_a_norm"], p["kv_a_norm"], p["q_head_norm"], p["k_head_norm"], w["w_uq"], w["w_uk"],
        w["w_uv"], rope, p["conv_w"], seq, join)
    grads.update(zip(GROUP_FFN2, join.results))

    gg = _global_layouts({"w_in": _dw_in(dm, dp, h2b), "w_uq": dw_uq, "w_uk": dw_uk, "w_uv": dw_uv, "w_pa": dw_pa, "w_pc": dw_pc,
                          "w_out": dw_out})
    g_mid = [_col_blocks(gg[n]) if n in COL_SHARDED else gg[n].reshape(N_CHIPS, -1, gg[n].shape[-1]) for n in GROUP_MID]
    swap = _swap_carried(g_mid)
    dx1, dg_mix = _inproj_bwd(x1, p["mix_norm"], dx2, dm, dp, w["w_in"], w["w_kr"], swap)
    part = (_partials(GROUP_MID[:1], g_mid[:1], swap.results[:1], place)
            + _chip_partial_small(g_mid[1:], swap.results[1:], place))
    send = _send_carried([pb for _, pb in part])
    grad_x, dg_ffn1, hb1, dgate1, dup1, dyb1 = _ffn_bwd_x(x_tok, p["ffn1_norm"], dx1, gate1, up1, blocks["ffn1_w_gate"],
                                                         blocks["ffn1_w_up"], blocks["ffn1_w_down"], "ffn1_bwd", send)

    small_grads = {"ffn1_norm": dg_ffn1, "mix_norm": dg_mix, "gate_bias": dbias, "q_a_norm": dqa, "kv_a_norm": dkva,
                   "q_head_norm": dqh, "k_head_norm": dkh, "ffn2_norm": dg_ffn2}
    packed = jnp.concatenate([small_grads[n] for n, _ in SMALL] + [dcw.reshape(1, -1), loss], axis=1)
    total = _sum_devices(packed.reshape(8, -1)).reshape(1, -1)
    n_small = sum(size for _, size in SMALL)
    conv_cols = conv_w.shape[1]
    conv_total = total[:, n_small:n_small + conv_rows * d].reshape(conv_rows, d)
    grads["conv_w"] = lax.dynamic_slice_in_dim(conv_total, chip * conv_cols, conv_cols, axis=1)
    loss_total = total[0, n_small + conv_rows * d]

    join = _join_carried(_totals(GROUP_MID[:1], g_mid[:1], part[:1], send.results[:1], place)
                         + _chip_total_small([pf for pf, _ in part[1:]], send.results[1:], place, [g.shape[1:] for g in g_mid[1:]]))
    g_gate = _tn_matmul(dgate1, hb1, "ffn1_dw_gate", carried=join)
    grads.update(zip(GROUP_MID, join.results))
    swap_gate = _swap_carried([g_gate])
    g_up = _tn_matmul(dup1, hb1, "ffn1_dw_up", carried=swap_gate)
    part_gate = _partials(GROUP_FFN1[:1], [g_gate], swap_gate.results, place)
    send_gate, swap_up = _send_carried([part_gate[0][1]]), _swap_carried([g_up])
    g_down = _tn_matmul(act1, dyb1, "ffn1_dw_down", carried=_both(send_gate, swap_up))
    join_gate = _join_carried(_totals(GROUP_FFN1[:1], [g_gate], part_gate, send_gate.results, place))
    part_up = _partials(GROUP_FFN1[1:2], [g_up], swap_up.results, place)
    send_up, swap_down = _send_carried([part_up[0][1]]), _swap_carried([g_down])
    adamw(("w_in",), _both(_both(send_up, swap_down), join_gate))
    grads["ffn1_w_gate"] = join_gate.results[0]
    join_up = _join_carried(_totals(GROUP_FFN1[1:2], [g_up], part_up, send_up.results, place))
    part_down = _partials(GROUP_FFN1[2:], [g_down], swap_down.results, place)
    send_down = _send_carried([part_down[0][1]])
    adamw(GROUP_FFN2, _both(send_down, join_up))
    grads["ffn1_w_up"] = join_up.results[0]
    join_down = _join_carried(_totals(GROUP_FFN1[2:], [g_down], part_down, send_down.results, place))
    adamw(GROUP_FFN1[:2], join_down)
    grads["ffn1_w_down"] = join_down.results[0]
    adamw(GROUP_FFN1[2:])
    others = GROUP_MID[1:] + ("conv_w",)
    for n, (gn, dn, mn, vn) in zip(others, _adamw_whole([(weights[n], grads[n], moments_m[n], moments_v[n]) for n in others],
                                                        "adamw_others")):
        grads[n], delta[n], new_m[n], new_v[n] = gn, dn, mn, vn

    row = lambda a: a.reshape(1, -1)
    small = _adamw_small(total, [(row(weights[n]), row(moments_m[n]), row(moments_v[n])) for n, _ in SMALL], [size for _, size in SMALL])
    for (n, _), (gn, dn, mn, vn) in zip(SMALL, small):
        grads[n], delta[n], new_m[n], new_v[n] = gn.reshape(-1), dn.reshape(-1), mn.reshape(-1), vn.reshape(-1)

    return (loss_total, grad_x.reshape(nb, seq, d), *[view(n, src[n]) for src in (grads, delta, new_m, new_v) for n in WEIGHT_ORDER])
```
